```python
import math
import jax, jax.numpy as jnp
from jax import lax
import numpy as np

D_MODEL = 1024
BATCH = 8
SEQ = 2048
DEPTH = 1

HEAD_DIM = 64
ATTN_HEADS = 8
DILATION_GROUPS = ((128, 1), (512, 4), (2048, 16))
N_GROUPS = len(DILATION_GROUPS)
ATTN_WIDTH = ATTN_HEADS * HEAD_DIM
QKV_WIDTH = N_GROUPS * ATTN_WIDTH
BLOCK = 128
ROPE_THETA = 10000.0
NEG_INF = -1e30
SSM_GROUP = 16
SSM_GROUPS = 32
SSM_WIDTH = SSM_GROUP * SSM_GROUPS
SSM_STATE = 64
DT_MIN = 1e-3
DT_MAX = 1e-1
N_BRANCH = 2
IN_WIDTH = 3 * QKV_WIDTH + SSM_WIDTH + N_BRANCH * D_MODEL
D_FF = -(-(8 * D_MODEL) // (3 * 256)) * 256
DN_ALPHA = (2.0 * DEPTH) ** 0.25
DN_BETA = (8.0 * DEPTH) ** -0.25
LN_EPS = 1e-5

kernel_name = "dilated_attn_s5_gated_hybrid_deepnorm"


def layer_norm(x, g, b):
    xf = x.astype(jnp.float32)
    mu = jnp.mean(xf, axis=-1, keepdims=True)
    var = jnp.mean(jnp.square(xf - mu), axis=-1, keepdims=True)
    y = (xf - mu) * lax.rsqrt(var + LN_EPS)
    return (y * g.astype(jnp.float32) + b.astype(jnp.float32)).astype(x.dtype)


def apply_rope(t, pos):
    half = HEAD_DIM // 2
    inv_freq = ROPE_THETA ** (-jnp.arange(half, dtype=jnp.float32) / half)
    ang = pos[:, None] * inv_freq[None, :]
    cos = jnp.cos(ang)[None, :, None, None, :]
    sin = jnp.sin(ang)[None, :, None, None, :]
    t1 = t[..., :half].astype(jnp.float32)
    t2 = t[..., half:].astype(jnp.float32)
    return jnp.concatenate([t1 * cos - t2 * sin, t1 * sin + t2 * cos], axis=-1)


def dilated_group_attention(q, k, v, window, dilation):
    b, s, h, dh = q.shape
    n = s // dilation
    nb = -(-n // BLOCK)
    n_pad = nb * BLOCK
    back = window // dilation

    def to_phase_blocks(t):
        t = t.reshape(b, n, dilation, h, dh)
        t = jnp.pad(t, ((0, 0), (0, n_pad - n), (0, 0), (0, 0), (0, 0)))
        return t.reshape(b, nb, BLOCK, dilation, h, dh)

    def with_prev_block(t):
        prev = jnp.pad(t, ((0, 0), (1, 0), (0, 0), (0, 0), (0, 0), (0, 0)))[:, :-1]
        return jnp.concatenate([prev, t], axis=2)

    qb = to_phase_blocks(q)
    kw = with_prev_block(to_phase_blocks(k))
    vw = with_prev_block(to_phase_blocks(v))

    scores = jnp.einsum("bnqrhd,bnkrhd->bnrhqk", qb, kw,
                        preferred_element_type=jnp.float32) / math.sqrt(dh)
    a_idx = jnp.arange(BLOCK)[None, :, None]
    c_idx = jnp.arange(2 * BLOCK)[None, None, :]
    blk = jnp.arange(nb)[:, None, None]
    dist = BLOCK + a_idx - c_idx
    key_sub = (blk - 1) * BLOCK + c_idx
    valid = (dist >= 0) & (dist <= back) & (key_sub >= 0)
    scores = jnp.where(valid[None, :, None, None], scores, NEG_INF)
    lse = jax.nn.logsumexp(scores, axis=-1)
    probs = jnp.exp(scores - lse[..., None])
    out = jnp.einsum("bnrhqk,bnkrhd->bnqrhd", probs, vw.astype(jnp.float32))
    out = out.reshape(b, n_pad, dilation, h, dh)[:, :n].reshape(b, s, h, dh)
    lse = jnp.transpose(lse, (0, 1, 4, 2, 3)).reshape(b, n_pad, dilation, h)[:, :n]
    return out, lse.reshape(b, s, h)


def s5_ssm(u, a_re, a_im, log_dt, b_re, b_im, c_re, c_im, d_skip):
    f32 = jnp.float32
    bsz, s, _ = u.shape
    ug = u.astype(f32).reshape(bsz, s, SSM_GROUPS, SSM_GROUP)
    lam = lax.complex(a_re.astype(f32), a_im.astype(f32))
    dt = jnp.exp(log_dt.astype(f32))[:, None]
    a_bar = jnp.exp(lam * dt)
    b_c = lax.complex(b_re.astype(f32), b_im.astype(f32))
    b_bar = ((a_bar - 1.0) / lam)[..., None] * b_c
    bu = jnp.einsum("gph,bsgh->bsgp", b_bar, ug.astype(jnp.complex64))
    a_seq = jnp.broadcast_to(a_bar, bu.shape)

    def combine(left, right):
        a_l, x_l = left
        a_r, x_r = right
        return a_r * a_l, a_r * x_l + x_r

    _, states = lax.associative_scan(combine, (a_seq, bu), axis=1)
    c_c = lax.complex(c_re.astype(f32), c_im.astype(f32))
    y = jnp.einsum("ghp,bsgp->bsgh", c_c, states).real
    y = y + d_skip.astype(f32).reshape(SSM_GROUPS, SSM_GROUP) * ug
    return y.reshape(bsz, s, SSM_WIDTH)


def _fwd_setup_inputs(seed: int = 0) -> dict:
    key = jax.random.key(seed)
    ks = jax.random.split(key, 24)
    f32 = jnp.float32
    L = DEPTH

    def nrm(k, shape, scale):
        return jax.random.normal(k, shape, f32) * scale

    x = jax.random.normal(ks[0], (BATCH, SEQ, D_MODEL), f32)
    w_in = nrm(ks[1], (L, D_MODEL, IN_WIDTH), D_MODEL ** -0.5)
    b_gate = nrm(ks[2], (L, N_BRANCH, D_MODEL), 0.02)
    w_attn_br = nrm(ks[3], (L, ATTN_WIDTH, D_MODEL), ATTN_WIDTH ** -0.5)
    w_ssm_br = nrm(ks[4], (L, SSM_WIDTH, D_MODEL), SSM_WIDTH ** -0.5)
    w_out = nrm(ks[5], (L, D_MODEL, D_MODEL), DN_BETA * D_MODEL ** -0.5)
    ssm_a_re = -0.5 + nrm(ks[6], (L, SSM_GROUPS, SSM_STATE), 0.01)
    ssm_a_im = (math.pi * jnp.arange(SSM_STATE, dtype=f32))[None, None, :] + nrm(ks[7], (L, SSM_GROUPS, SSM_STATE), 0.01)
    ssm_log_dt = jax.random.uniform(ks[8], (L, SSM_GROUPS), f32, math.log(DT_MIN), math.log(DT_MAX))
    ssm_b_re = nrm(ks[9], (L, SSM_GROUPS, SSM_STATE, SSM_GROUP), (2 * SSM_GROUP) ** -0.5)
    ssm_b_im = nrm(ks[10], (L, SSM_GROUPS, SSM_STATE, SSM_GROUP), (2 * SSM_GROUP) ** -0.5)
    ssm_c_re = nrm(ks[11], (L, SSM_GROUPS, SSM_GROUP, SSM_STATE), SSM_STATE ** -0.5)
    ssm_c_im = nrm(ks[12], (L, SSM_GROUPS, SSM_GROUP, SSM_STATE), SSM_STATE ** -0.5)
    ssm_d = nrm(ks[13], (L, SSM_WIDTH), 1.0)
    w_glu = nrm(ks[14], (L, SSM_WIDTH, 2 * SSM_WIDTH), SSM_WIDTH ** -0.5)
    ln1_g = 1.0 + nrm(ks[15], (L, D_MODEL), 0.02)
    ln1_b = nrm(ks[16], (L, D_MODEL), 0.02)
    w_ff_gate = nrm(ks[17], (L, D_MODEL, D_FF), D_MODEL ** -0.5)
    w_ff_up = nrm(ks[18], (L, D_MODEL, D_FF), D_MODEL ** -0.5)
    w_ff_down = nrm(ks[19], (L, D_FF, D_MODEL), DN_BETA * D_FF ** -0.5)
    ln2_g = 1.0 + nrm(ks[20], (L, D_MODEL), 0.02)
    ln2_b = nrm(ks[21], (L, D_MODEL), 0.02)
    return {"x": x, "w_in": w_in, "b_gate": b_gate, "w_attn_br": w_attn_br,
            "w_ssm_br": w_ssm_br, "w_out": w_out, "ssm_a_re": ssm_a_re,
            "ssm_a_im": ssm_a_im, "ssm_log_dt": ssm_log_dt, "ssm_b_re": ssm_b_re,
            "ssm_b_im": ssm_b_im, "ssm_c_re": ssm_c_re, "ssm_c_im": ssm_c_im,
            "ssm_d": ssm_d, "w_glu": w_glu, "ln1_g": ln1_g, "ln1_b": ln1_b,
            "w_ff_gate": w_ff_gate, "w_ff_up": w_ff_up, "w_ff_down": w_ff_down,
            "ln2_g": ln2_g, "ln2_b": ln2_b}


def _fwd_reference(x, w_in, b_gate, w_attn_br, w_ssm_br, w_out, ssm_a_re, ssm_a_im,
              ssm_log_dt, ssm_b_re, ssm_b_im, ssm_c_re, ssm_c_im, ssm_d, w_glu,
              ln1_g, ln1_b, w_ff_gate, w_ff_up, w_ff_down, ln2_g, ln2_b):
    bsz, s, _ = x.shape
    pos = jnp.arange(s, dtype=jnp.float32)
    for layer in range(DEPTH):
        proj = x @ w_in[layer]
        q = proj[..., :QKV_WIDTH].reshape(bsz, s, N_GROUPS, ATTN_HEADS, HEAD_DIM)
        k = proj[..., QKV_WIDTH:2 * QKV_WIDTH].reshape(bsz, s, N_GROUPS, ATTN_HEADS, HEAD_DIM)
        v = proj[..., 2 * QKV_WIDTH:3 * QKV_WIDTH].reshape(bsz, s, N_GROUPS, ATTN_HEADS, HEAD_DIM)
        u = proj[..., 3 * QKV_WIDTH:3 * QKV_WIDTH + SSM_WIDTH]
        gate_logits = proj[..., 3 * QKV_WIDTH + SSM_WIDTH:].reshape(bsz, s, N_BRANCH, D_MODEL)
        q = apply_rope(q, pos)
        k = apply_rope(k, pos)

        outs, lses = [], []
        for g, (window, dilation) in enumerate(DILATION_GROUPS):
            o_g, lse_g = dilated_group_attention(q[:, :, g], k[:, :, g], v[:, :, g], window, dilation)
            outs.append(o_g)
            lses.append(lse_g)
        wts = jax.nn.softmax(jnp.stack(lses, axis=0), axis=0)
        attn = jnp.sum(wts[..., None] * jnp.stack(outs, axis=0), axis=0)
        y_attn = attn.reshape(bsz, s, ATTN_WIDTH).astype(x.dtype) @ w_attn_br[layer]

        y_s = s5_ssm(u, ssm_a_re[layer], ssm_a_im[layer], ssm_log_dt[layer], ssm_b_re[layer],
                     ssm_b_im[layer], ssm_c_re[layer], ssm_c_im[layer], ssm_d[layer])
        glu = jax.nn.gelu(y_s).astype(x.dtype) @ w_glu[layer]
        y_s = glu[..., :SSM_WIDTH] * jax.nn.sigmoid(glu[..., SSM_WIDTH:])
        y_ssm = y_s @ w_ssm_br[layer]

        gates = jax.nn.sigmoid((gate_logits + b_gate[layer]).astype(jnp.float32))
        mixed = gates[..., 0, :] * y_attn.astype(jnp.float32) + gates[..., 1, :] * y_ssm.astype(jnp.float32)
        mix_out = mixed.astype(x.dtype) @ w_out[layer]
        h = layer_norm(DN_ALPHA * x + mix_out.astype(x.dtype), ln1_g[layer], ln1_b[layer])

        ff = (jax.nn.silu(h @ w_ff_gate[layer]) * (h @ w_ff_up[layer])) @ w_ff_down[layer]
        x = layer_norm(DN_ALPHA * h + ff.astype(h.dtype), ln2_g[layer], ln2_b[layer])
    return x


import jax as _jax
import jax.numpy as _jnp

TWIN_FORMAT = 'train_step'
FWD_PARAMS = ['x', 'w_in', 'b_gate', 'w_attn_br', 'w_ssm_br', 'w_out', 'ssm_a_re', 'ssm_a_im', 'ssm_log_dt', 'ssm_b_re', 'ssm_b_im', 'ssm_c_re', 'ssm_c_im', 'ssm_d', 'w_glu', 'ln1_g', 'ln1_b', 'w_ff_gate', 'w_ff_up', 'w_ff_down', 'ln2_g', 'ln2_b']
TWIN_WEIGHTS = ['w_in', 'b_gate', 'w_attn_br', 'w_ssm_br', 'w_out', 'ssm_a_re', 'ssm_a_im', 'ssm_log_dt', 'ssm_b_re', 'ssm_b_im', 'ssm_c_re', 'ssm_c_im', 'ssm_d', 'w_glu', 'ln1_g', 'ln1_b', 'w_ff_gate', 'w_ff_up', 'w_ff_down', 'ln2_g', 'ln2_b']
TWIN_DIFF_INPUT = 'x'
TWIN_INPUTS = ['x', 'w_in', 'b_gate', 'w_attn_br', 'w_ssm_br', 'w_out', 'ssm_a_re', 'ssm_a_im', 'ssm_log_dt', 'ssm_b_re', 'ssm_b_im', 'ssm_c_re', 'ssm_c_im', 'ssm_d', 'w_glu', 'ln1_g', 'ln1_b', 'w_ff_gate', 'w_ff_up', 'w_ff_down', 'ln2_g', 'ln2_b', 'loss_target', 'm_w_in', 'm_b_gate', 'm_w_attn_br', 'm_w_ssm_br', 'm_w_out', 'm_ssm_a_re', 'm_ssm_a_im', 'm_ssm_log_dt', 'm_ssm_b_re', 'm_ssm_b_im', 'm_ssm_c_re', 'm_ssm_c_im', 'm_ssm_d', 'm_w_glu', 'm_ln1_g', 'm_ln1_b', 'm_w_ff_gate', 'm_w_ff_up', 'm_w_ff_down', 'm_ln2_g', 'm_ln2_b', 'v_w_in', 'v_b_gate', 'v_w_attn_br', 'v_w_ssm_br', 'v_w_out', 'v_ssm_a_re', 'v_ssm_a_im', 'v_ssm_log_dt', 'v_ssm_b_re', 'v_ssm_b_im', 'v_ssm_c_re', 'v_ssm_c_im', 'v_ssm_d', 'v_w_glu', 'v_ln1_g', 'v_ln1_b', 'v_w_ff_gate', 'v_w_ff_up', 'v_w_ff_down', 'v_ln2_g', 'v_ln2_b']
TWIN_OUTPUTS = ['loss', 'grad_x', 'grad_w_in', 'grad_b_gate', 'grad_w_attn_br', 'grad_w_ssm_br', 'grad_w_out', 'grad_ssm_a_re', 'grad_ssm_a_im', 'grad_ssm_log_dt', 'grad_ssm_b_re', 'grad_ssm_b_im', 'grad_ssm_c_re', 'grad_ssm_c_im', 'grad_ssm_d', 'grad_w_glu', 'grad_ln1_g', 'grad_ln1_b', 'grad_w_ff_gate', 'grad_w_ff_up', 'grad_w_ff_down', 'grad_ln2_g', 'grad_ln2_b', 'delta_w_in', 'delta_b_gate', 'delta_w_attn_br', 'delta_w_ssm_br', 'delta_w_out', 'delta_ssm_a_re', 'delta_ssm_a_im', 'delta_ssm_log_dt', 'delta_ssm_b_re', 'delta_ssm_b_im', 'delta_ssm_c_re', 'delta_ssm_c_im', 'delta_ssm_d', 'delta_w_glu', 'delta_ln1_g', 'delta_ln1_b', 'delta_w_ff_gate', 'delta_w_ff_up', 'delta_w_ff_down', 'delta_ln2_g', 'delta_ln2_b', 'new_m_w_in', 'new_m_b_gate', 'new_m_w_attn_br', 'new_m_w_ssm_br', 'new_m_w_out', 'new_m_ssm_a_re', 'new_m_ssm_a_im', 'new_m_ssm_log_dt', 'new_m_ssm_b_re', 'new_m_ssm_b_im', 'new_m_ssm_c_re', 'new_m_ssm_c_im', 'new_m_ssm_d', 'new_m_w_glu', 'new_m_ln1_g', 'new_m_ln1_b', 'new_m_w_ff_gate', 'new_m_w_ff_up', 'new_m_w_ff_down', 'new_m_ln2_g', 'new_m_ln2_b', 'new_v_w_in', 'new_v_b_gate', 'new_v_w_attn_br', 'new_v_w_ssm_br', 'new_v_w_out', 'new_v_ssm_a_re', 'new_v_ssm_a_im', 'new_v_ssm_log_dt', 'new_v_ssm_b_re', 'new_v_ssm_b_im', 'new_v_ssm_c_re', 'new_v_ssm_c_im', 'new_v_ssm_d', 'new_v_w_glu', 'new_v_ln1_g', 'new_v_ln1_b', 'new_v_w_ff_gate', 'new_v_w_ff_up', 'new_v_w_ff_down', 'new_v_ln2_g', 'new_v_ln2_b']
TWIN_LEAF_KINDS = {'loss': 'loss', 'grad_x': 'grad_x', 'grad_w_in': 'grad_w', 'grad_b_gate': 'grad_w', 'grad_w_attn_br': 'grad_w', 'grad_w_ssm_br': 'grad_w', 'grad_w_out': 'grad_w', 'grad_ssm_a_re': 'grad_w', 'grad_ssm_a_im': 'grad_w', 'grad_ssm_log_dt': 'grad_w', 'grad_ssm_b_re': 'grad_w', 'grad_ssm_b_im': 'grad_w', 'grad_ssm_c_re': 'grad_w', 'grad_ssm_c_im': 'grad_w', 'grad_ssm_d': 'grad_w', 'grad_w_glu': 'grad_w', 'grad_ln1_g': 'grad_w', 'grad_ln1_b': 'grad_w', 'grad_w_ff_gate': 'grad_w', 'grad_w_ff_up': 'grad_w', 'grad_w_ff_down': 'grad_w', 'grad_ln2_g': 'grad_w', 'grad_ln2_b': 'grad_w', 'delta_w_in': 'delta_w', 'delta_b_gate': 'delta_w', 'delta_w_attn_br': 'delta_w', 'delta_w_ssm_br': 'delta_w', 'delta_w_out': 'delta_w', 'delta_ssm_a_re': 'delta_w', 'delta_ssm_a_im': 'delta_w', 'delta_ssm_log_dt': 'delta_w', 'delta_ssm_b_re': 'delta_w', 'delta_ssm_b_im': 'delta_w', 'delta_ssm_c_re': 'delta_w', 'delta_ssm_c_im': 'delta_w', 'delta_ssm_d': 'delta_w', 'delta_w_glu': 'delta_w', 'delta_ln1_g': 'delta_w', 'delta_ln1_b': 'delta_w', 'delta_w_ff_gate': 'delta_w', 'delta_w_ff_up': 'delta_w', 'delta_w_ff_down': 'delta_w', 'delta_ln2_g': 'delta_w', 'delta_ln2_b': 'delta_w', 'new_m_w_in': 'new_m', 'new_m_b_gate': 'new_m', 'new_m_w_attn_br': 'new_m', 'new_m_w_ssm_br': 'new_m', 'new_m_w_out': 'new_m', 'new_m_ssm_a_re': 'new_m', 'new_m_ssm_a_im': 'new_m', 'new_m_ssm_log_dt': 'new_m', 'new_m_ssm_b_re': 'new_m', 'new_m_ssm_b_im': 'new_m', 'new_m_ssm_c_re': 'new_m', 'new_m_ssm_c_im': 'new_m', 'new_m_ssm_d': 'new_m', 'new_m_w_glu': 'new_m', 'new_m_ln1_g': 'new_m', 'new_m_ln1_b': 'new_m', 'new_m_w_ff_gate': 'new_m', 'new_m_w_ff_up': 'new_m', 'new_m_w_ff_down': 'new_m', 'new_m_ln2_g': 'new_m', 'new_m_ln2_b': 'new_m', 'new_v_w_in': 'new_v', 'new_v_b_gate': 'new_v', 'new_v_w_attn_br': 'new_v', 'new_v_w_ssm_br': 'new_v', 'new_v_w_out': 'new_v', 'new_v_ssm_a_re': 'new_v', 'new_v_ssm_a_im': 'new_v', 'new_v_ssm_log_dt': 'new_v', 'new_v_ssm_b_re': 'new_v', 'new_v_ssm_b_im': 'new_v', 'new_v_ssm_c_re': 'new_v', 'new_v_ssm_c_im': 'new_v', 'new_v_ssm_d': 'new_v', 'new_v_w_glu': 'new_v', 'new_v_ln1_g': 'new_v', 'new_v_ln1_b': 'new_v', 'new_v_w_ff_gate': 'new_v', 'new_v_w_ff_up': 'new_v', 'new_v_w_ff_down': 'new_v', 'new_v_ln2_g': 'new_v', 'new_v_ln2_b': 'new_v'}


def _forward(args):
    return _fwd_reference(*[args[k] for k in FWD_PARAMS])


def _output_shape():
    out = _jax.eval_shape(lambda: _forward(_fwd_setup_inputs(0)))
    return out.shape, out.dtype

N_MICROBATCH = 1
ADAM_LR = 0.001
ADAM_B1 = 0.9
ADAM_B2 = 0.999
ADAM_EPS = 1e-08
ADAM_WD = 0.01
ADAM_STEP = 10
PER_EXAMPLE_BATCH_AXIS = {'x': 0, 'loss_target': 0}
SHARED_INPUTS = []
_WEIGHT_DTYPES = {'w_in': _jnp.float32, 'b_gate': _jnp.float32, 'w_attn_br': _jnp.float32, 'w_ssm_br': _jnp.float32, 'w_out': _jnp.float32, 'ssm_a_re': _jnp.float32, 'ssm_a_im': _jnp.float32, 'ssm_log_dt': _jnp.float32, 'ssm_b_re': _jnp.float32, 'ssm_b_im': _jnp.float32, 'ssm_c_re': _jnp.float32, 'ssm_c_im': _jnp.float32, 'ssm_d': _jnp.float32, 'w_glu': _jnp.float32, 'ln1_g': _jnp.float32, 'ln1_b': _jnp.float32, 'w_ff_gate': _jnp.float32, 'w_ff_up': _jnp.float32, 'w_ff_down': _jnp.float32, 'ln2_g': _jnp.float32, 'ln2_b': _jnp.float32}
MOMENT_SCALE = {'w_in': 6.849080e-03, 'b_gate': 4.310829e-03, 'w_attn_br': 6.638284e-03, 'w_ssm_br': 1.543384e-02, 'w_out': 2.601749e-02, 'ssm_a_re': 1.307853e-03, 'ssm_a_im': 1.365326e-03, 'ssm_log_dt': 1.332289e+00, 'ssm_b_re': 7.951751e-04, 'ssm_b_im': 8.276479e-04, 'ssm_c_re': 1.141920e-03, 'ssm_c_im': 1.134980e-03, 'ssm_d': 2.182769e-02, 'w_glu': 1.635842e-02, 'ln1_g': 5.060918e-01, 'ln1_b': 2.800184e-01, 'w_ff_gate': 2.232353e-02, 'w_ff_up': 2.160981e-02, 'w_ff_down': 6.018110e-02, 'ln2_g': 1.602578e+01, 'ln2_b': 7.027837e-01}


def _to_microbatches(a, axis):
    t = _jnp.moveaxis(a, axis, 0)
    t = t.reshape((N_MICROBATCH, t.shape[0] // N_MICROBATCH) + t.shape[1:])
    return _jnp.moveaxis(t, 1, axis + 1)


def setup_inputs(seed: int = 0) -> dict:
    inp = _fwd_setup_inputs(seed)
    key = _jax.random.fold_in(_jax.random.key(seed), 7919)
    shape, _ = _output_shape()
    out = dict(inp)
    out["loss_target"] = _jax.random.normal(_jax.random.fold_in(key, 0), shape, _jnp.float32)
    for i, name in enumerate(TWIN_WEIGHTS):
        w = inp[name].astype(_jnp.float32)
        if MOMENT_SCALE is None:
            s = _jnp.sqrt(_jnp.mean(_jnp.square(w)) + 1e-30)
        else:
            s = MOMENT_SCALE[name]
        km, kv = _jax.random.split(_jax.random.fold_in(key, i + 1))
        out[name] = w
        out["m_" + name] = s * _jax.random.normal(km, w.shape, _jnp.float32)
        out["v_" + name] = (s * s) * _jax.random.uniform(kv, w.shape, _jnp.float32, 0.5, 1.5)
    if N_MICROBATCH > 1:
        for name, axis in PER_EXAMPLE_BATCH_AXIS.items():
            out[name] = _to_microbatches(out[name], axis)
    return {'x': out['x'], 'w_in': out['w_in'], 'b_gate': out['b_gate'], 'w_attn_br': out['w_attn_br'], 'w_ssm_br': out['w_ssm_br'], 'w_out': out['w_out'], 'ssm_a_re': out['ssm_a_re'], 'ssm_a_im': out['ssm_a_im'], 'ssm_log_dt': out['ssm_log_dt'], 'ssm_b_re': out['ssm_b_re'], 'ssm_b_im': out['ssm_b_im'], 'ssm_c_re': out['ssm_c_re'], 'ssm_c_im': out['ssm_c_im'], 'ssm_d': out['ssm_d'], 'w_glu': out['w_glu'], 'ln1_g': out['ln1_g'], 'ln1_b': out['ln1_b'], 'w_ff_gate': out['w_ff_gate'], 'w_ff_up': out['w_ff_up'], 'w_ff_down': out['w_ff_down'], 'ln2_g': out['ln2_g'], 'ln2_b': out['ln2_b'], 'loss_target': out['loss_target'], 'm_w_in': out['m_w_in'], 'm_b_gate': out['m_b_gate'], 'm_w_attn_br': out['m_w_attn_br'], 'm_w_ssm_br': out['m_w_ssm_br'], 'm_w_out': out['m_w_out'], 'm_ssm_a_re': out['m_ssm_a_re'], 'm_ssm_a_im': out['m_ssm_a_im'], 'm_ssm_log_dt': out['m_ssm_log_dt'], 'm_ssm_b_re': out['m_ssm_b_re'], 'm_ssm_b_im': out['m_ssm_b_im'], 'm_ssm_c_re': out['m_ssm_c_re'], 'm_ssm_c_im': out['m_ssm_c_im'], 'm_ssm_d': out['m_ssm_d'], 'm_w_glu': out['m_w_glu'], 'm_ln1_g': out['m_ln1_g'], 'm_ln1_b': out['m_ln1_b'], 'm_w_ff_gate': out['m_w_ff_gate'], 'm_w_ff_up': out['m_w_ff_up'], 'm_w_ff_down': out['m_w_ff_down'], 'm_ln2_g': out['m_ln2_g'], 'm_ln2_b': out['m_ln2_b'], 'v_w_in': out['v_w_in'], 'v_b_gate': out['v_b_gate'], 'v_w_attn_br': out['v_w_attn_br'], 'v_w_ssm_br': out['v_w_ssm_br'], 'v_w_out': out['v_w_out'], 'v_ssm_a_re': out['v_ssm_a_re'], 'v_ssm_a_im': out['v_ssm_a_im'], 'v_ssm_log_dt': out['v_ssm_log_dt'], 'v_ssm_b_re': out['v_ssm_b_re'], 'v_ssm_b_im': out['v_ssm_b_im'], 'v_ssm_c_re': out['v_ssm_c_re'], 'v_ssm_c_im': out['v_ssm_c_im'], 'v_ssm_d': out['v_ssm_d'], 'v_w_glu': out['v_w_glu'], 'v_ln1_g': out['v_ln1_g'], 'v_ln1_b': out['v_ln1_b'], 'v_w_ff_gate': out['v_w_ff_gate'], 'v_w_ff_up': out['v_w_ff_up'], 'v_w_ff_down': out['v_w_ff_down'], 'v_ln2_g': out['v_ln2_g'], 'v_ln2_b': out['v_ln2_b']}


def _loss(weights, diff, rest, loss_target):
    with _jax.named_scope("forward"):
        args = {**rest, TWIN_DIFF_INPUT: diff, **{k: w.astype(_WEIGHT_DTYPES[k]) for k, w in weights.items()}}
        y = _forward(args)
    with _jax.named_scope("loss_head"):
        err = _jnp.square(y.astype(_jnp.float32) - loss_target)
        return 0.5 * _jnp.sum(_jnp.mean(err, axis=-1)) if err.ndim else 0.5 * err


def _adamw(w, g, m, v):
    m = ADAM_B1 * m + (1.0 - ADAM_B1) * g
    v = ADAM_B2 * v + (1.0 - ADAM_B2) * _jnp.square(g)
    m_hat = m / (1.0 - ADAM_B1 ** ADAM_STEP)
    v_hat = v / (1.0 - ADAM_B2 ** ADAM_STEP)
    delta = -ADAM_LR * (m_hat / (_jnp.sqrt(v_hat) + ADAM_EPS) + ADAM_WD * w)
    return delta, m, v


def reference(x, w_in, b_gate, w_attn_br, w_ssm_br, w_out, ssm_a_re, ssm_a_im, ssm_log_dt, ssm_b_re, ssm_b_im, ssm_c_re, ssm_c_im, ssm_d, w_glu, ln1_g, ln1_b, w_ff_gate, w_ff_up, w_ff_down, ln2_g, ln2_b, loss_target, m_w_in, m_b_gate, m_w_attn_br, m_w_ssm_br, m_w_out, m_ssm_a_re, m_ssm_a_im, m_ssm_log_dt, m_ssm_b_re, m_ssm_b_im, m_ssm_c_re, m_ssm_c_im, m_ssm_d, m_w_glu, m_ln1_g, m_ln1_b, m_w_ff_gate, m_w_ff_up, m_w_ff_down, m_ln2_g, m_ln2_b, v_w_in, v_b_gate, v_w_attn_br, v_w_ssm_br, v_w_out, v_ssm_a_re, v_ssm_a_im, v_ssm_log_dt, v_ssm_b_re, v_ssm_b_im, v_ssm_c_re, v_ssm_c_im, v_ssm_d, v_w_glu, v_ln1_g, v_ln1_b, v_w_ff_gate, v_w_ff_up, v_w_ff_down, v_ln2_g, v_ln2_b):
    given = dict(x=x, w_in=w_in, b_gate=b_gate, w_attn_br=w_attn_br, w_ssm_br=w_ssm_br, w_out=w_out, ssm_a_re=ssm_a_re, ssm_a_im=ssm_a_im, ssm_log_dt=ssm_log_dt, ssm_b_re=ssm_b_re, ssm_b_im=ssm_b_im, ssm_c_re=ssm_c_re, ssm_c_im=ssm_c_im, ssm_d=ssm_d, w_glu=w_glu, ln1_g=ln1_g, ln1_b=ln1_b, w_ff_gate=w_ff_gate, w_ff_up=w_ff_up, w_ff_down=w_ff_down, ln2_g=ln2_g, ln2_b=ln2_b, loss_target=loss_target, m_w_in=m_w_in, m_b_gate=m_b_gate, m_w_attn_br=m_w_attn_br, m_w_ssm_br=m_w_ssm_br, m_w_out=m_w_out, m_ssm_a_re=m_ssm_a_re, m_ssm_a_im=m_ssm_a_im, m_ssm_log_dt=m_ssm_log_dt, m_ssm_b_re=m_ssm_b_re, m_ssm_b_im=m_ssm_b_im, m_ssm_c_re=m_ssm_c_re, m_ssm_c_im=m_ssm_c_im, m_ssm_d=m_ssm_d, m_w_glu=m_w_glu, m_ln1_g=m_ln1_g, m_ln1_b=m_ln1_b, m_w_ff_gate=m_w_ff_gate, m_w_ff_up=m_w_ff_up, m_w_ff_down=m_w_ff_down, m_ln2_g=m_ln2_g, m_ln2_b=m_ln2_b, v_w_in=v_w_in, v_b_gate=v_b_gate, v_w_attn_br=v_w_attn_br, v_w_ssm_br=v_w_ssm_br, v_w_out=v_w_out, v_ssm_a_re=v_ssm_a_re, v_ssm_a_im=v_ssm_a_im, v_ssm_log_dt=v_ssm_log_dt, v_ssm_b_re=v_ssm_b_re, v_ssm_b_im=v_ssm_b_im, v_ssm_c_re=v_ssm_c_re, v_ssm_c_im=v_ssm_c_im, v_ssm_d=v_ssm_d, v_w_glu=v_w_glu, v_ln1_g=v_ln1_g, v_ln1_b=v_ln1_b, v_w_ff_gate=v_w_ff_gate, v_w_ff_up=v_w_ff_up, v_w_ff_down=v_w_ff_down, v_ln2_g=v_ln2_g, v_ln2_b=v_ln2_b)
    weights = {n: given[n] for n in TWIN_WEIGHTS}
    shared = {n: given[n] for n in SHARED_INPUTS}
    per_example = {n: given[n] for n in ['x']}
    grad_fn = _jax.value_and_grad(_loss, argnums=(0, 1))

    def one_microbatch(ex, loss_target):
        ex = dict(ex)
        diff = ex.pop(TWIN_DIFF_INPUT)
        return grad_fn(weights, diff, {**shared, **ex}, loss_target)

    if N_MICROBATCH == 1:
        loss, (grad_w, grad_x) = one_microbatch(per_example, given["loss_target"])
    else:
        def body(carry, xs):
            loss_sum, grad_sum = carry
            l_k, (gw_k, gx_k) = one_microbatch(xs[0], xs[1])
            with _jax.named_scope("update"):
                return (loss_sum + l_k, _jax.tree.map(_jnp.add, grad_sum, gw_k)), gx_k

        init = (_jnp.zeros((), _jnp.float32), _jax.tree.map(_jnp.zeros_like, weights))
        (loss, grad_w), grad_x = _jax.lax.scan(body, init, (per_example, given["loss_target"]))
    with _jax.named_scope("update"):
        delta_w, new_m, new_v = {}, {}, {}
        for n in TWIN_WEIGHTS:
            delta_w[n], new_m[n], new_v[n] = _adamw(weights[n], grad_w[n], given["m_" + n], given["v_" + n])
    return (loss, grad_x, *[grad_w[n] for n in TWIN_WEIGHTS], *[delta_w[n] for n in TWIN_WEIGHTS],
            *[new_m[n] for n in TWIN_WEIGHTS], *[new_v[n] for n in TWIN_WEIGHTS])
```

```python
import functools
import math

import jax
import jax.numpy as jnp
from jax import lax
from jax.experimental import pallas as pl
from jax.experimental.pallas import tpu as pltpu

F32 = jnp.float32
BF16 = jnp.bfloat16

N_DEV = 8
SEQ = 2048
D_MODEL = 1024
HEAD_DIM = 64
ATTN_WIDTH = 512
QKV_WIDTH = 1536
SSM_WIDTH = 512
SSM_GROUPS = 32
SSM_GROUP = 16
SSM_STATE = 64
IN_WIDTH = 7168
D_FF = 2816
FF_SHARD = D_FF // N_DEV
FF_PAD = 384
D_FF_PAD = FF_PAD * N_DEV
DN_ALPHA = 2.0 ** 0.25
LN_EPS = 1e-5
NEG_INF = -1e30
ROPE_THETA = 10000.0
BLOCK = 128
GROUPS = ((1, 16), (4, 4), (16, 1))

ADAM_LR = 0.001
ADAM_B1 = 0.9
ADAM_B2 = 0.999
ADAM_EPS = 1e-08
ADAM_WD = 0.01
ADAM_STEP = 10

VMEM_LIMIT = 56 * 1024 * 1024


_pallas_call = pl.pallas_call


def _cparams(**kw):
    return pltpu.CompilerParams(vmem_limit_bytes=VMEM_LIMIT, **kw)


def _dot(a, b):
    return jnp.dot(a, b, preferred_element_type=F32)


def _dot_nt(a, b):
    return lax.dot_general(a, b, (((1,), (1,)), ((), ())), preferred_element_type=F32)


def _dot_tn(a, b):
    return lax.dot_general(a, b, (((0,), (0,)), ((), ())), preferred_element_type=F32)


def _rope_tables():
    half = HEAD_DIM // 2
    inv_freq = ROPE_THETA ** (-jnp.arange(half, dtype=F32) / half)
    ang = jnp.arange(SEQ, dtype=F32)[:, None] * inv_freq[None, :]
    cos, sin = jnp.cos(ang), jnp.sin(ang)
    return jnp.tile(cos, (1, 4)), jnp.tile(jnp.concatenate([-sin, sin], axis=1), (1, 2))


def _swap_halves(x):
    lane = lax.broadcasted_iota(jnp.int32, x.shape, 1)
    return jnp.where((lane & 63) < 32, pltpu.roll(x, 96, axis=1), pltpu.roll(x, 32, axis=1))


def _group_rows(d, nb, r, i):
    src = pl.ds(i * BLOCK, BLOCK) if d == 1 else pl.ds(r + i * BLOCK * d, BLOCK, stride=d)
    return src, pl.ds((r * nb + i) * BLOCK, BLOCK)


def _attn_masks():
    a_idx = lax.broadcasted_iota(jnp.int32, (BLOCK, BLOCK), 0)
    c_idx = lax.broadcasted_iota(jnp.int32, (BLOCK, BLOCK), 1)
    lane = lax.broadcasted_iota(jnp.int32, (BLOCK, 128), 1)
    return c_idx <= a_idx, c_idx >= a_idx, lane < HEAD_DIM


def _attn_fwd(proj, cos_t, sin_t):
    def body(q0, q1, q2, k0, k1, k2, v0, v1, v2, cos_ref, sin_ref, attn_ref, lse_ref,
             qs, ks, vs, os_, ms, ls, acc, mnat, lnat):
        cur_ok, prev_ok, head0 = _attn_masks()
        for g, (d, nb) in enumerate(GROUPS):
            q_ref, k_ref, v_ref = (q0, q1, q2)[g], (k0, k1, k2)[g], (v0, v1, v2)[g]
            for r in range(d):
                for i in range(nb):
                    src, dst = _group_rows(d, nb, r, i)
                    c, s = cos_ref[src, :], sin_ref[src, :]
                    q = q_ref[src, :]
                    k = k_ref[src, :]
                    qs[dst, :] = ((q * c + _swap_halves(q) * s) * 0.125).astype(BF16)
                    ks[dst, :] = (k * c + _swap_halves(k) * s).astype(BF16)
                    vs[dst, :] = v_ref[src, :].astype(BF16)

            def block(b, carry, nb=nb):
                has_prev = (b & (nb - 1)) > 0
                cur = pl.ds(pl.multiple_of(b * BLOCK, BLOCK), BLOCK)
                prv = pl.ds(pl.multiple_of(jnp.maximum(b - 1, 0) * BLOCK, BLOCK), BLOCK)
                q = qs[cur, :]
                kc, kp, vc, vp = ks[cur, :], ks[prv, :], vs[cur, :], vs[prv, :]
                outs, maxs, sums = [], [], []
                for h in range(2):
                    mine = head0 if h == 0 else jnp.logical_not(head0)
                    qh = jnp.where(mine, q, jnp.zeros_like(q))
                    s_c = jnp.where(cur_ok, _dot_nt(qh, kc), NEG_INF)
                    s_p = jnp.where(jnp.logical_and(prev_ok, has_prev), _dot_nt(qh, kp), NEG_INF)
                    m = jnp.maximum(jnp.max(s_c, axis=1, keepdims=True), jnp.max(s_p, axis=1, keepdims=True))
                    p_c, p_p = jnp.exp(s_c - m), jnp.exp(s_p - m)
                    sums.append(jnp.sum(p_c, axis=1, keepdims=True) + jnp.sum(p_p, axis=1, keepdims=True))
                    maxs.append(m)
                    outs.append(_dot(p_c.astype(BF16), vc) + _dot(p_p.astype(BF16), vp))
                os_[cur, :] = jnp.where(head0, outs[0], outs[1])
                ms[cur, :] = jnp.where(head0, maxs[0], maxs[1])
                ls[cur, :] = jnp.where(head0, sums[0], sums[1])
                return carry

            lax.fori_loop(0, SEQ // BLOCK, block, 0)

            for r in range(d):
                for i in range(nb):
                    src, dst = _group_rows(d, nb, r, i)
                    if g == 0:
                        acc[src, :], mnat[src, :], lnat[src, :] = os_[dst, :], ms[dst, :], ls[dst, :]
                    else:
                        m_old, m_g = mnat[src, :], ms[dst, :]
                        m_new = jnp.maximum(m_old, m_g)
                        a_old, a_g = jnp.exp(m_old - m_new), jnp.exp(m_g - m_new)
                        acc[src, :] = a_old * acc[src, :] + a_g * os_[dst, :]
                        lnat[src, :] = a_old * lnat[src, :] + a_g * ls[dst, :]
                        mnat[src, :] = m_new
        for i in range(SEQ // BLOCK):
            rows = pl.ds(i * BLOCK, BLOCK)
            l = lnat[rows, :]
            attn_ref[rows, :] = acc[rows, :] / l
            lse_ref[rows, :] = mnat[rows, :] + jnp.log(l)

    def col(base):
        return pl.BlockSpec((SEQ, 128), lambda hp, base=base: (0, base + hp))

    in_specs = [col(g * 4) for g in range(3)] + [col(12 + g * 4) for g in range(3)] + [col(24 + g * 4) for g in range(3)]
    table = pl.BlockSpec((SEQ, 128), lambda hp: (0, 0))
    out = pl.BlockSpec((SEQ, 128), lambda hp: (0, hp))
    return _pallas_call(
        body, name="attn_fwd", grid=(4,),
        in_specs=in_specs + [table, table], out_specs=(out, out),
        out_shape=(jax.ShapeDtypeStruct((SEQ, ATTN_WIDTH), F32), jax.ShapeDtypeStruct((SEQ, ATTN_WIDTH), F32)),
        scratch_shapes=[pltpu.VMEM((SEQ, 128), BF16)] * 3 + [pltpu.VMEM((SEQ, 128), F32)] * 6,
        compiler_params=_cparams(dimension_semantics=("arbitrary",)),
    )(*([proj] * 9), cos_t, sin_t)


def _attn_bwd_group(g, proj, cos_t, sin_t, attn, lse, dattn):
    d, nb = GROUPS[g]

    def body(q_ref, k_ref, v_ref, cos_ref, sin_ref, attn_ref, lse_ref, dattn_ref, dq_ref, dk_ref, dv_ref,
             qs, ks, vs, dos, lss, dss, dqs, dks, dvs):
        cur_ok, prev_ok, head0 = _attn_masks()
        for r in range(d):
            for i in range(nb):
                src, dst = _group_rows(d, nb, r, i)
                c, s = cos_ref[src, :], sin_ref[src, :]
                q = q_ref[src, :]
                k = k_ref[src, :]
                qs[dst, :] = ((q * c + _swap_halves(q) * s) * 0.125).astype(BF16)
                ks[dst, :] = (k * c + _swap_halves(k) * s).astype(BF16)
                vs[dst, :] = v_ref[src, :].astype(BF16)
                do = dattn_ref[src, :]
                prod = do * attn_ref[src, :]
                d0 = jnp.sum(jnp.where(head0, prod, 0.0), axis=1, keepdims=True)
                d1 = jnp.sum(jnp.where(head0, 0.0, prod), axis=1, keepdims=True)
                dos[dst, :] = do.astype(BF16)
                dss[dst, :] = jnp.where(head0, d0, d1)
                lss[dst, :] = lse_ref[src, :]
                dks[dst, :] = jnp.zeros((BLOCK, 128), F32)
                dvs[dst, :] = jnp.zeros((BLOCK, 128), F32)

        def block(b, carry):
            has_prev = (b & (nb - 1)) > 0
            cur = pl.ds(pl.multiple_of(b * BLOCK, BLOCK), BLOCK)
            prv = pl.ds(pl.multiple_of(jnp.maximum(b - 1, 0) * BLOCK, BLOCK), BLOCK)
            q, do = qs[cur, :], dos[cur, :]
            kc, kp, vc, vp = ks[cur, :], ks[prv, :], vs[cur, :], vs[prv, :]
            lse_b, dsum_b = lss[cur, :], dss[cur, :]
            dq = jnp.zeros((BLOCK, 128), F32)
            dk_c, dk_p, dv_c, dv_p = dq, dq, dq, dq
            for h in range(2):
                mine = head0 if h == 0 else jnp.logical_not(head0)
                zero = jnp.zeros_like(q)
                qh, doh = jnp.where(mine, q, zero), jnp.where(mine, do, zero)
                kch, kph = jnp.where(mine, kc, zero), jnp.where(mine, kp, zero)
                lse_h = jnp.max(jnp.where(mine, lse_b, NEG_INF), axis=1, keepdims=True)
                dsum_h = jnp.max(jnp.where(mine, dsum_b, NEG_INF), axis=1, keepdims=True)
                s_c = jnp.where(cur_ok, _dot_nt(qh, kc), NEG_INF)
                s_p = jnp.where(jnp.logical_and(prev_ok, has_prev), _dot_nt(qh, kp), NEG_INF)
                p_c, p_p = jnp.exp(s_c - lse_h), jnp.exp(s_p - lse_h)
                ds_c = (p_c * (_dot_nt(doh, vc) - dsum_h)).astype(BF16)
                ds_p = (p_p * (_dot_nt(doh, vp) - dsum_h)).astype(BF16)
                dv_c += _dot_tn(p_c.astype(BF16), doh)
                dv_p += _dot_tn(p_p.astype(BF16), doh)
                dq += _dot(ds_c, kch) + _dot(ds_p, kph)
                dk_c += _dot_tn(ds_c, qh)
                dk_p += _dot_tn(ds_p, qh)
            dqs[cur, :] = dq
            dks[cur, :] += dk_c
            dvs[cur, :] += dv_c
            dks[prv, :] += dk_p
            dvs[prv, :] += dv_p
            return carry

        lax.fori_loop(0, SEQ // BLOCK, block, 0)

        for r in range(d):
            for i in range(nb):
                src, dst = _group_rows(d, nb, r, i)
                c, s = cos_ref[src, :], sin_ref[src, :]
                gq, gk = dqs[dst, :] * 0.125, dks[dst, :]
                dq_ref[src, :] = gq * c - _swap_halves(gq) * s
                dk_ref[src, :] = gk * c - _swap_halves(gk) * s
                dv_ref[src, :] = dvs[dst, :]

    def col(base):
        return pl.BlockSpec((SEQ, 128), lambda hp, base=base: (0, base + hp))

    table = pl.BlockSpec((SEQ, 128), lambda hp: (0, 0))
    out = jax.ShapeDtypeStruct((SEQ, ATTN_WIDTH), F32)
    return _pallas_call(
        body, name=f"attn_bwd_g{g}", grid=(4,),
        in_specs=[col(g * 4), col(12 + g * 4), col(24 + g * 4), table, table, col(0), col(0), col(0)],
        out_specs=(col(0), col(0), col(0)), out_shape=(out, out, out),
        scratch_shapes=[pltpu.VMEM((SEQ, 128), BF16)] * 4 + [pltpu.VMEM((SEQ, 128), F32)] * 5,
        compiler_params=_cparams(dimension_semantics=("arbitrary",)),
    )(proj, proj, proj, cos_t, sin_t, attn, lse, dattn)


SSM_CHUNKS = 4
CHUNK_STATES = 512
SCAN_ROWS = 8
U_COL = (3 * QKV_WIDTH) // 128


def _cmul(xr, xi, yr, yi):
    return xr * yr - xi * yi, xr * yi + xi * yr


def _ssm_prep(a_re, a_im, log_dt, b_re_t, b_im_t):
    def body(ar_ref, ai_ref, ldt_ref, br_ref, bi_ref, abr_ref, abi_ref, er_ref, ei_ref, bbr_ref, bbi_ref):
        ar, ai = ar_ref[...], ai_ref[...]
        dt = jnp.exp(ldt_ref[...])
        mag = jnp.exp(ar * dt)
        abr, abi = mag * jnp.cos(ai * dt), mag * jnp.sin(ai * dt)
        den = ar * ar + ai * ai
        nr, ni = abr - 1.0, abi
        er, ei = (nr * ar + ni * ai) / den, (ni * ar - nr * ai) / den
        abr_ref[...], abi_ref[...], er_ref[...], ei_ref[...] = abr, abi, er, ei
        er3, ei3 = er[:, None, :], ei[:, None, :]
        br, bi = br_ref[...], bi_ref[...]
        bbr_ref[...] = er3 * br - ei3 * bi
        bbi_ref[...] = er3 * bi + ei3 * br

    gp = jax.ShapeDtypeStruct(a_re.shape, F32)
    gb = jax.ShapeDtypeStruct(b_re_t.shape, F32)
    return _pallas_call(body, name="ssm_prep", out_shape=(gp, gp, gp, gp, gb, gb))(a_re, a_im, log_dt, b_re_t, b_im_t)


def _ssm_param_bwd(a_re, a_im, log_dt, b_re_t, b_im_t, abar_re, abar_im, e_re, e_im, ga_re, ga_im, gbb_re_t, gbb_im_t):
    def body(ar_ref, ai_ref, ldt_ref, br_ref, bi_ref, abr_ref, abi_ref, er_ref, ei_ref, gar_ref, gai_ref, gbr_ref, gbi_ref,
             o_ar, o_ai, o_ldt, o_br, o_bi):
        ar, ai = ar_ref[...], ai_ref[...]
        dt = jnp.exp(ldt_ref[...])
        er, ei = er_ref[...], ei_ref[...]
        br, bi, gbr, gbi = br_ref[...], bi_ref[...], gbr_ref[...], gbi_ref[...]
        er3, ei3 = er[:, None, :], ei[:, None, :]
        o_br[...] = er3 * gbr + ei3 * gbi
        o_bi[...] = er3 * gbi - ei3 * gbr
        ge_r = jnp.sum(br * gbr + bi * gbi, axis=1)
        ge_i = jnp.sum(br * gbi - bi * gbr, axis=1)
        den = ar * ar + ai * ai
        ilr, ili = ar / den, -ai / den
        t_r, t_i = _cmul(ilr, -ili, ge_r, ge_i)
        gab_r, gab_i = gar_ref[...] + t_r, gai_ref[...] + t_i
        gz_r, gz_i = _cmul(abr_ref[...], -abi_ref[...], gab_r, gab_i)
        el_r, el_i = _cmul(er, ei, ilr, ili)
        u_r, u_i = _cmul(el_r, -el_i, ge_r, ge_i)
        o_ar[...] = dt * gz_r - u_r
        o_ai[...] = dt * gz_i - u_i
        o_ldt[...] = jnp.sum(gz_r * ar + gz_i * ai, axis=1, keepdims=True) * dt

    gp = jax.ShapeDtypeStruct(a_re.shape, F32)
    gb = jax.ShapeDtypeStruct(b_re_t.shape, F32)
    return _pallas_call(body, name="ssm_param_bwd", out_shape=(gp, gp, jax.ShapeDtypeStruct(log_dt.shape, F32), gb, gb))(
        a_re, a_im, log_dt, b_re_t, b_im_t, abar_re, abar_im, e_re, e_im, ga_re, ga_im, gbb_re_t, gbb_im_t)


def _block_diag(blocks_re, blocks_im, sign_im, rows_are_channels):
    both = jnp.stack([blocks_re, sign_im * blocks_im]).reshape(2, SSM_CHUNKS, 8, SSM_GROUP, SSM_STATE)
    eye = jnp.eye(8, dtype=F32)
    if rows_are_channels:
        return jnp.einsum("rcghp,gk->cghrkp", both, eye).reshape(SSM_CHUNKS, 128, 2 * CHUNK_STATES)
    return jnp.einsum("rcghp,gk->crkpgh", both, eye).reshape(SSM_CHUNKS, 2 * CHUNK_STATES, 128)


def _block_diag_parts(mat, rows_are_channels):
    if rows_are_channels:
        six = mat.reshape(SSM_CHUNKS, 8, SSM_GROUP, 2, 8, SSM_STATE)
        parts = jnp.einsum("cghrgp->rcghp", six)
    else:
        six = mat.reshape(SSM_CHUNKS, 2, 8, SSM_STATE, 8, SSM_GROUP)
        parts = jnp.einsum("crgpgh->rcghp", six)
    parts = parts.reshape(2, SSM_GROUPS, SSM_GROUP, SSM_STATE)
    return parts[0], parts[1]


def _scan_consts(a_ref, conj, reverse):
    ar = jnp.broadcast_to(a_ref[:, :CHUNK_STATES], (SCAN_ROWS, CHUNK_STATES))
    ai = jnp.broadcast_to(a_ref[:, CHUNK_STATES:], (SCAN_ROWS, CHUNK_STATES))
    if conj:
        ai = -ai
    row = lax.broadcasted_iota(jnp.int32, (SCAN_ROWS, CHUNK_STATES), 0)
    if reverse:
        row = SCAN_ROWS - 1 - row
    zero = jnp.zeros_like(ar)
    steps = []
    pr, pi = ar, ai
    for shift in (1, 2, 4):
        keep = row >= shift
        steps.append((SCAN_ROWS - shift if reverse else shift, jnp.where(keep, pr, zero), jnp.where(keep, pi, zero)))
        pr, pi = _cmul(pr, pi, pr, pi)
    first = row == 0
    return steps, (jnp.where(first, ar, zero), jnp.where(first, ai, zero)), first


def _scan_tile(xr, xi, prev_r, prev_i, steps, carry_in, reverse):
    edge = SCAN_ROWS - 1 if reverse else 1
    cr, ci = pltpu.roll(prev_r, edge, axis=0), pltpu.roll(prev_i, edge, axis=0)
    xr, xi = xr + carry_in[0] * cr - carry_in[1] * ci, xi + carry_in[0] * ci + carry_in[1] * cr
    for shift, mr, mi in steps:
        sr, si = pltpu.roll(xr, shift, axis=0), pltpu.roll(xi, shift, axis=0)
        xr, xi = xr + mr * sr - mi * si, xi + mr * si + mi * sr
    return xr, xi


MM_ROWS = 256


def _ssm_fwd(proj, bmat, cmat, a_chunks, d_skip):
    def body(u_ref, b_ref, c_ref, a_ref, d_ref, y_ref, h_ref):
        for i in range(SEQ // MM_ROWS):
            rows = pl.ds(i * MM_ROWS, MM_ROWS)
            h_ref[rows, :] = _dot(u_ref[rows, :].astype(BF16), b_ref[...])
        steps, carry_in, _ = _scan_consts(a_ref, conj=False, reverse=False)

        def tile(k, carry):
            rows = pl.ds(pl.multiple_of(k * SCAN_ROWS, SCAN_ROWS), SCAN_ROWS)
            xr, xi = _scan_tile(h_ref[rows, :CHUNK_STATES], h_ref[rows, CHUNK_STATES:], carry[0], carry[1], steps, carry_in, False)
            h_ref[rows, :CHUNK_STATES] = xr
            h_ref[rows, CHUNK_STATES:] = xi
            return xr, xi

        zero = jnp.zeros((SCAN_ROWS, CHUNK_STATES), F32)
        lax.fori_loop(0, SEQ // SCAN_ROWS, tile, (zero, zero))
        for i in range(SEQ // MM_ROWS):
            rows = pl.ds(i * MM_ROWS, MM_ROWS)
            y_ref[rows, :] = _dot(h_ref[rows, :].astype(BF16), c_ref[...]) + d_ref[...] * u_ref[rows, :]

    return _pallas_call(
        body, name="ssm_fwd", grid=(SSM_CHUNKS,),
        in_specs=[pl.BlockSpec((SEQ, 128), lambda c: (0, U_COL + c)),
                  pl.BlockSpec((None, 128, 2 * CHUNK_STATES), lambda c: (c, 0, 0)),
                  pl.BlockSpec((None, 2 * CHUNK_STATES, 128), lambda c: (c, 0, 0)),
                  pl.BlockSpec((None, 1, 2 * CHUNK_STATES), lambda c: (c, 0, 0)),
                  pl.BlockSpec((1, 128), lambda c: (0, c))],
        out_specs=(pl.BlockSpec((SEQ, 128), lambda c: (0, c)), pl.BlockSpec((SEQ, 2 * CHUNK_STATES), lambda c: (0, c))),
        out_shape=(jax.ShapeDtypeStruct((SEQ, SSM_WIDTH), F32), jax.ShapeDtypeStruct((SEQ, SSM_CHUNKS * 2 * CHUNK_STATES), F32)),
        compiler_params=_cparams(dimension_semantics=("arbitrary",)),
    )(proj, bmat, cmat, a_chunks, d_skip)


def _ssm_bwd(dys, proj, h, bmat, cmat, a_chunks, d_skip):
    def body(dy_ref, u_ref, h_ref, b_ref, c_ref, a_ref, d_ref, du_ref, db_ref, dc_ref, da_ref, dd_ref, g_ref):
        dsum = jnp.zeros((1, 128), F32)
        dcm = jnp.zeros((2 * CHUNK_STATES, 128), F32)
        for i in range(SEQ // MM_ROWS):
            rows = pl.ds(i * MM_ROWS, MM_ROWS)
            dy = dy_ref[rows, :]
            g_ref[rows, :] = _dot_nt(dy.astype(BF16), c_ref[...])
            dsum += jnp.sum(dy * u_ref[rows, :], axis=0, keepdims=True)
            dcm += _dot_tn(h_ref[rows, :].astype(BF16), dy.astype(BF16))
        dd_ref[...] = dsum
        dc_ref[...] = dcm
        steps, carry_in, _ = _scan_consts(a_ref, conj=True, reverse=True)
        first_row = lax.broadcasted_iota(jnp.int32, (SCAN_ROWS, CHUNK_STATES), 0) == 0
        n_tiles = SEQ // SCAN_ROWS

        def tile(j, carry):
            k = n_tiles - 1 - j
            rows = pl.ds(pl.multiple_of(k * SCAN_ROWS, SCAN_ROWS), SCAN_ROWS)
            before = pl.ds(pl.multiple_of(jnp.maximum(k - 1, 0) * SCAN_ROWS, SCAN_ROWS), SCAN_ROWS)
            gr, gi = _scan_tile(g_ref[rows, :CHUNK_STATES], g_ref[rows, CHUNK_STATES:], carry[0], carry[1], steps, carry_in, True)
            g_ref[rows, :CHUNK_STATES] = gr
            g_ref[rows, CHUNK_STATES:] = gi
            has_before = jnp.where(k > 0, 1.0, 0.0)
            hr = jnp.where(first_row, pltpu.roll(h_ref[before, :CHUNK_STATES], 1, axis=0) * has_before,
                           pltpu.roll(h_ref[rows, :CHUNK_STATES], 1, axis=0))
            hi = jnp.where(first_row, pltpu.roll(h_ref[before, CHUNK_STATES:], 1, axis=0) * has_before,
                           pltpu.roll(h_ref[rows, CHUNK_STATES:], 1, axis=0))
            return gr, gi, carry[2] + hr * gr + hi * gi, carry[3] + hr * gi - hi * gr

        zero = jnp.zeros((SCAN_ROWS, CHUNK_STATES), F32)
        _, _, sar, sai = lax.fori_loop(0, n_tiles, tile, (zero, zero, zero, zero))
        da_ref[:, :CHUNK_STATES] = jnp.sum(sar, axis=0, keepdims=True)
        da_ref[:, CHUNK_STATES:] = jnp.sum(sai, axis=0, keepdims=True)
        dbm = jnp.zeros((128, 2 * CHUNK_STATES), F32)
        for i in range(SEQ // MM_ROWS):
            rows = pl.ds(i * MM_ROWS, MM_ROWS)
            g = g_ref[rows, :].astype(BF16)
            du_ref[rows, :] = _dot_nt(g, b_ref[...]) + d_ref[...] * dy_ref[rows, :]
            dbm += _dot_tn(u_ref[rows, :].astype(BF16), g)
        db_ref[...] = dbm

    chunk_col = pl.BlockSpec((SEQ, 128), lambda c: (0, c))
    return _pallas_call(
        body, name="ssm_bwd", grid=(SSM_CHUNKS,),
        in_specs=[chunk_col,
                  pl.BlockSpec((SEQ, 128), lambda c: (0, U_COL + c)),
                  pl.BlockSpec((SEQ, 2 * CHUNK_STATES), lambda c: (0, c)),
                  pl.BlockSpec((None, 128, 2 * CHUNK_STATES), lambda c: (c, 0, 0)),
                  pl.BlockSpec((None, 2 * CHUNK_STATES, 128), lambda c: (c, 0, 0)),
                  pl.BlockSpec((None, 1, 2 * CHUNK_STATES), lambda c: (c, 0, 0)),
                  pl.BlockSpec((1, 128), lambda c: (0, c))],
        out_specs=(chunk_col,
                   pl.BlockSpec((None, 128, 2 * CHUNK_STATES), lambda c: (c, 0, 0)),
                   pl.BlockSpec((None, 2 * CHUNK_STATES, 128), lambda c: (c, 0, 0)),
                   pl.BlockSpec((None, 1, 2 * CHUNK_STATES), lambda c: (c, 0, 0)),
                   pl.BlockSpec((1, 128), lambda c: (0, c))),
        out_shape=(jax.ShapeDtypeStruct((SEQ, SSM_WIDTH), F32),
                   jax.ShapeDtypeStruct((SSM_CHUNKS, 128, 2 * CHUNK_STATES), F32),
                   jax.ShapeDtypeStruct((SSM_CHUNKS, 2 * CHUNK_STATES, 128), F32),
                   jax.ShapeDtypeStruct((SSM_CHUNKS, 1, 2 * CHUNK_STATES), F32),
                   jax.ShapeDtypeStruct((1, SSM_WIDTH), F32)),
        scratch_shapes=[pltpu.VMEM((SEQ, 2 * CHUNK_STATES), F32)],
        compiler_params=_cparams(dimension_semantics=("arbitrary",)),
    )(dys, proj, h, bmat, cmat, a_chunks, d_skip)


def _ssm_tables(abar_re, abar_im, bbar_re_t, bbar_im_t, c_re, c_im):
    bmat = _block_diag(bbar_re_t, bbar_im_t, 1.0, True).astype(BF16)
    cmat = _block_diag(c_re, c_im, -1.0, False).astype(BF16)
    a_chunks = jnp.concatenate([abar_re.reshape(SSM_CHUNKS, 1, CHUNK_STATES), abar_im.reshape(SSM_CHUNKS, 1, CHUNK_STATES)], axis=2)
    return bmat, cmat, a_chunks


GL_COL = (3 * QKV_WIDTH + SSM_WIDTH) // D_MODEL
GELU_C = math.sqrt(2.0 / math.pi)
GELU_A = 0.044715


def _sds(shape, dtype):
    return jax.ShapeDtypeStruct(shape, dtype)


def _gelu(x):
    t = jnp.tanh(GELU_C * (x + GELU_A * x * x * x))
    return 0.5 * x * (1.0 + t), t


def _gelu_grad(x, t):
    return 0.5 * (1.0 + t) + 0.5 * x * (1.0 - t * t) * GELU_C * (1.0 + 3.0 * GELU_A * x * x)


def _layer_norm(r, g, b):
    mu = jnp.mean(r, axis=-1, keepdims=True)
    xc = r - mu
    rstd = lax.rsqrt(jnp.mean(xc * xc, axis=-1, keepdims=True) + LN_EPS)
    xhat = xc * rstd
    return xhat * g + b, xhat, rstd


def _layer_norm_bwd(dy, xhat, rstd, g):
    dxhat = dy * g
    m1 = jnp.mean(dxhat, axis=-1, keepdims=True)
    m2 = jnp.mean(dxhat * xhat, axis=-1, keepdims=True)
    return rstd * (dxhat - m1 - xhat * m2)


def _unshard(w_g, name):
    _, k, n = w_g.shape

    def body(i_ref, o_ref):
        o_ref[...] = i_ref[...]

    return _pallas_call(
        body, name=name, grid=(N_DEV,),
        in_specs=[pl.BlockSpec((None, k, n), lambda j: (j, 0, 0))], out_specs=pl.BlockSpec((k, n), lambda j: (0, j)),
        out_shape=_sds((k, N_DEV * n), w_g.dtype), compiler_params=_cparams(dimension_semantics=("arbitrary",)),
    )(w_g)


def _proj(x, w_in):
    tm, tn = 512, 1792

    def body(x_ref, w_ref, o_ref):
        o_ref[...] = _dot(x_ref[...].astype(BF16), w_ref[...])

    return _pallas_call(
        body, name="proj", grid=(SEQ // tm, IN_WIDTH // tn),
        in_specs=[pl.BlockSpec((tm, D_MODEL), lambda i, j: (i, 0)), pl.BlockSpec((D_MODEL, tn), lambda i, j: (0, j))],
        out_specs=pl.BlockSpec((tm, tn), lambda i, j: (i, j)), out_shape=_sds((SEQ, IN_WIDTH), F32),
        compiler_params=_cparams(dimension_semantics=("arbitrary", "arbitrary")),
    )(x, w_in)


def _row_spec(tm, width, col=0):
    return pl.BlockSpec((tm, width), lambda i, col=col: (i, col))


def _full_spec(shape):
    return pl.BlockSpec(shape, lambda i: (0,) * len(shape))


def _mixer_out(attn, ys, proj, x, w_ab, w_sb, w_glu, w_out, b_gate, ln_g, ln_b):
    tm = 256

    def body(attn_ref, ys_ref, gl0_ref, gl1_ref, x_ref, wab_ref, wsb_ref, wglu_ref, wout_ref, bg_ref, g_ref, b_ref,
             h_ref, xhat_ref, rstd_ref, glu_ref, ya_ref, yssm_ref):
        gy, _ = _gelu(ys_ref[...])
        glu = _dot(gy.astype(BF16), wglu_ref[...])
        glu_ref[...] = glu
        y_s = glu[:, :SSM_WIDTH] * jax.nn.sigmoid(glu[:, SSM_WIDTH:])
        y_ssm = _dot(y_s.astype(BF16), wsb_ref[...])
        y_attn = _dot(attn_ref[...].astype(BF16), wab_ref[...])
        ya_ref[...] = y_attn
        yssm_ref[...] = y_ssm
        g0 = jax.nn.sigmoid(gl0_ref[...] + bg_ref[0:1, :])
        g1 = jax.nn.sigmoid(gl1_ref[...] + bg_ref[1:2, :])
        mixed = g0 * y_attn + g1 * y_ssm
        r1 = DN_ALPHA * x_ref[...] + _dot(mixed.astype(BF16), wout_ref[...])
        h, xhat, rstd = _layer_norm(r1, g_ref[...], b_ref[...])
        h_ref[...] = h
        xhat_ref[...] = xhat
        rstd_ref[...] = jnp.broadcast_to(rstd, (tm, 128))

    wide = _sds((SEQ, D_MODEL), F32)
    return _pallas_call(
        body, name="mixer_out", grid=(SEQ // tm,),
        in_specs=[_row_spec(tm, ATTN_WIDTH), _row_spec(tm, SSM_WIDTH), _row_spec(tm, D_MODEL, GL_COL), _row_spec(tm, D_MODEL, GL_COL + 1),
                  _row_spec(tm, D_MODEL), _full_spec((ATTN_WIDTH, D_MODEL)), _full_spec((SSM_WIDTH, D_MODEL)),
                  _full_spec((SSM_WIDTH, 2 * SSM_WIDTH)), _full_spec((D_MODEL, D_MODEL)), _full_spec((2, D_MODEL)),
                  _full_spec((1, D_MODEL)), _full_spec((1, D_MODEL))],
        out_specs=(_row_spec(tm, D_MODEL), _row_spec(tm, D_MODEL), _row_spec(tm, 128), _row_spec(tm, D_MODEL),
                   _row_spec(tm, D_MODEL), _row_spec(tm, D_MODEL)),
        out_shape=(wide, wide, _sds((SEQ, 128), F32), wide, wide, wide),
        compiler_params=_cparams(dimension_semantics=("arbitrary",)),
    )(attn, ys, proj, proj, x, w_ab, w_sb, w_glu, w_out, b_gate, ln_g, ln_b)


def _ff_up(h, w_gate, w_up):
    tm, tn = 512, 768

    def body(h_ref, wg_ref, wu_ref, a_ref, b_ref, f_ref):
        hb = h_ref[...].astype(BF16)
        a, b = _dot(hb, wg_ref[...]), _dot(hb, wu_ref[...])
        a_ref[...] = a.astype(BF16)
        b_ref[...] = b.astype(BF16)
        f_ref[...] = (a * jax.nn.sigmoid(a) * b).astype(BF16)

    tile = pl.BlockSpec((tm, tn), lambda i, j: (i, j))
    wtile = pl.BlockSpec((D_MODEL, tn), lambda i, j: (0, j))
    out = _sds((SEQ, D_FF_PAD), BF16)
    return _pallas_call(
        body, name="ff_up", grid=(SEQ // tm, D_FF_PAD // tn),
        in_specs=[pl.BlockSpec((tm, D_MODEL), lambda i, j: (i, 0)), wtile, wtile],
        out_specs=(tile, tile, tile), out_shape=(out, out, out),
        compiler_params=_cparams(dimension_semantics=("arbitrary", "arbitrary")),
    )(h, w_gate, w_up)


def _ff_down_loss(f, w_down, h, target, ln_g, ln_b):
    tm = 256

    def body(f_ref, w_ref, h_ref, t_ref, g_ref, b_ref, dr_ref, dg_ref, db_ref, loss_ref):
        @pl.when(pl.program_id(0) == 0)
        def _():
            dg_ref[...] = jnp.zeros_like(dg_ref)
            db_ref[...] = jnp.zeros_like(db_ref)
            loss_ref[...] = jnp.zeros_like(loss_ref)

        r2 = DN_ALPHA * h_ref[...] + _dot(f_ref[...], w_ref[...])
        g = g_ref[...]
        out, xhat, rstd = _layer_norm(r2, g, b_ref[...])
        err = out - t_ref[...]
        loss_ref[...] += 0.5 * jnp.sum(jnp.mean(err * err, axis=-1, keepdims=True), axis=0, keepdims=True)
        dout = err * (1.0 / D_MODEL)
        dg_ref[...] += jnp.sum(dout * xhat, axis=0, keepdims=True)
        db_ref[...] += jnp.sum(dout, axis=0, keepdims=True)
        dr_ref[...] = _layer_norm_bwd(dout, xhat, rstd, g)

    vec = _sds((1, D_MODEL), F32)
    return _pallas_call(
        body, name="ff_down_loss", grid=(SEQ // tm,),
        in_specs=[_row_spec(tm, D_FF_PAD), _full_spec((D_FF_PAD, D_MODEL)), _row_spec(tm, D_MODEL), _row_spec(tm, D_MODEL),
                  _full_spec((1, D_MODEL)), _full_spec((1, D_MODEL))],
        out_specs=(_row_spec(tm, D_MODEL), _full_spec((1, D_MODEL)), _full_spec((1, D_MODEL)), _full_spec((1, 128))),
        out_shape=(_sds((SEQ, D_MODEL), F32), vec, vec, _sds((1, 128), F32)),
        compiler_params=_cparams(dimension_semantics=("arbitrary",)),
    )(f, w_down, h, target, ln_g, ln_b)


def _ff_down_bwd(dr2, w_down, a, b):
    tm, tn = 512, 768

    def body(dr_ref, w_ref, a_ref, b_ref, da_ref, db_ref):
        df = _dot_nt(dr_ref[...].astype(BF16), w_ref[...])
        av, bv = a_ref[...].astype(F32), b_ref[...].astype(F32)
        sg = jax.nn.sigmoid(av)
        da_ref[...] = (df * bv * sg * (1.0 + av * (1.0 - sg))).astype(BF16)
        db_ref[...] = (df * av * sg).astype(BF16)

    tile = pl.BlockSpec((tm, tn), lambda i, j: (i, j))
    out = _sds((SEQ, D_FF_PAD), BF16)
    return _pallas_call(
        body, name="ff_down_bwd", grid=(SEQ // tm, D_FF_PAD // tn),
        in_specs=[pl.BlockSpec((tm, D_MODEL), lambda i, j: (i, 0)), pl.BlockSpec((tn, D_MODEL), lambda i, j: (j, 0)), tile, tile],
        out_specs=(tile, tile), out_shape=(out, out),
        compiler_params=_cparams(dimension_semantics=("arbitrary", "arbitrary")),
    )(dr2, w_down, a, b)


def _ff_up_bwd(da, db, w_gate, w_up, dr2, xhat1, rstd1, ln_g):
    tm, tk = 512, 768
    nk = D_FF_PAD // tk

    def body(da_ref, db_ref, wg_ref, wu_ref, dr2_ref, xhat_ref, rstd_ref, g_ref, dr1_ref, dg_ref, dbias_ref, acc):
        i, k = pl.program_id(0), pl.program_id(1)

        @pl.when(jnp.logical_and(i == 0, k == 0))
        def _():
            dg_ref[...] = jnp.zeros_like(dg_ref)
            dbias_ref[...] = jnp.zeros_like(dbias_ref)

        part = _dot_nt(da_ref[...], wg_ref[...]) + _dot_nt(db_ref[...], wu_ref[...])

        @pl.when(k == 0)
        def _():
            acc[...] = part

        @pl.when(k > 0)
        def _():
            acc[...] += part

        @pl.when(k == nk - 1)
        def _():
            dh = DN_ALPHA * dr2_ref[...] + acc[...]
            xhat = xhat_ref[...]
            dg_ref[...] += jnp.sum(dh * xhat, axis=0, keepdims=True)
            dbias_ref[...] += jnp.sum(dh, axis=0, keepdims=True)
            rstd = jnp.max(rstd_ref[...], axis=1, keepdims=True)
            dr1_ref[...] = _layer_norm_bwd(dh, xhat, rstd, g_ref[...])

    hid = pl.BlockSpec((tm, tk), lambda i, k: (i, k))
    wtile = pl.BlockSpec((D_MODEL, tk), lambda i, k: (0, k))
    row = pl.BlockSpec((tm, D_MODEL), lambda i, k: (i, 0))
    vec = pl.BlockSpec((1, D_MODEL), lambda i, k: (0, 0))
    return _pallas_call(
        body, name="ff_up_bwd", grid=(SEQ // tm, nk),
        in_specs=[hid, hid, wtile, wtile, row, row, pl.BlockSpec((tm, 128), lambda i, k: (i, 0)), vec],
        out_specs=(row, vec, vec), out_shape=(_sds((SEQ, D_MODEL), F32), _sds((1, D_MODEL), F32), _sds((1, D_MODEL), F32)),
        scratch_shapes=[pltpu.VMEM((tm, D_MODEL), F32)],
        compiler_params=_cparams(dimension_semantics=("arbitrary", "arbitrary")),
    )(da, db, w_gate, w_up, dr2, xhat1, rstd1, ln_g)


def _mixer_bwd(dr1, proj, y_attn, y_ssm, glu, ys, w_ab, w_sb, w_glu, w_out, b_gate):
    tm = 256

    def body(dr1_ref, gl0_ref, gl1_ref, ya_ref, yssm_ref, glu_ref, ys_ref, wab_ref, wsb_ref, wglu_ref, wout_ref, bg_ref,
             dya_ref, dyssm_ref, dgl_ref, dattn_ref, dglu_ref, dys_ref, mixed_ref, ysb_ref, gy_ref, dbg_ref):
        @pl.when(pl.program_id(0) == 0)
        def _():
            dbg_ref[...] = jnp.zeros_like(dbg_ref)

        dmixed = _dot_nt(dr1_ref[...].astype(BF16), wout_ref[...])
        g0 = jax.nn.sigmoid(gl0_ref[...] + bg_ref[0:1, :])
        g1 = jax.nn.sigmoid(gl1_ref[...] + bg_ref[1:2, :])
        y_attn, y_ssm = ya_ref[...], yssm_ref[...]
        mixed_ref[...] = (g0 * y_attn + g1 * y_ssm).astype(BF16)
        dya = (dmixed * g0).astype(BF16)
        dyssm = (dmixed * g1).astype(BF16)
        dya_ref[...] = dya
        dyssm_ref[...] = dyssm
        dgl0 = dmixed * y_attn * g0 * (1.0 - g0)
        dgl1 = dmixed * y_ssm * g1 * (1.0 - g1)
        dgl_ref[:, :D_MODEL] = dgl0.astype(BF16)
        dgl_ref[:, D_MODEL:] = dgl1.astype(BF16)
        dbg_ref[:, :D_MODEL] += jnp.sum(dgl0, axis=0, keepdims=True)
        dbg_ref[:, D_MODEL:] += jnp.sum(dgl1, axis=0, keepdims=True)
        dattn_ref[...] = _dot_nt(dya, wab_ref[...])
        dy_s = _dot_nt(dyssm, wsb_ref[...])
        glu = glu_ref[...]
        glu1, sg = glu[:, :SSM_WIDTH], jax.nn.sigmoid(glu[:, SSM_WIDTH:])
        ysb_ref[...] = (glu1 * sg).astype(BF16)
        dglu1 = (dy_s * sg).astype(BF16)
        dglu2 = (dy_s * glu1 * sg * (1.0 - sg)).astype(BF16)
        dglu_ref[:, :SSM_WIDTH] = dglu1
        dglu_ref[:, SSM_WIDTH:] = dglu2
        dgy = _dot_nt(dglu1, wglu_ref[:, :SSM_WIDTH]) + _dot_nt(dglu2, wglu_ref[:, SSM_WIDTH:])
        ys = ys_ref[...]
        gy, t = _gelu(ys)
        gy_ref[...] = gy.astype(BF16)
        dys_ref[...] = dgy * _gelu_grad(ys, t)

    wide_b, half_b = _sds((SEQ, D_MODEL), BF16), _sds((SEQ, SSM_WIDTH), BF16)
    half_f = _sds((SEQ, SSM_WIDTH), F32)
    return _pallas_call(
        body, name="mixer_bwd", grid=(SEQ // tm,),
        in_specs=[_row_spec(tm, D_MODEL), _row_spec(tm, D_MODEL, GL_COL), _row_spec(tm, D_MODEL, GL_COL + 1), _row_spec(tm, D_MODEL),
                  _row_spec(tm, D_MODEL), _row_spec(tm, D_MODEL), _row_spec(tm, SSM_WIDTH), _full_spec((ATTN_WIDTH, D_MODEL)),
                  _full_spec((SSM_WIDTH, D_MODEL)), _full_spec((SSM_WIDTH, 2 * SSM_WIDTH)), _full_spec((D_MODEL, D_MODEL)),
                  _full_spec((2, D_MODEL))],
        out_specs=(_row_spec(tm, D_MODEL), _row_spec(tm, D_MODEL), _row_spec(tm, 2 * D_MODEL), _row_spec(tm, ATTN_WIDTH),
                   _row_spec(tm, D_MODEL), _row_spec(tm, SSM_WIDTH), _row_spec(tm, D_MODEL), _row_spec(tm, SSM_WIDTH),
                   _row_spec(tm, SSM_WIDTH), _full_spec((1, 2 * D_MODEL))),
        out_shape=(wide_b, wide_b, _sds((SEQ, 2 * D_MODEL), BF16), half_f, wide_b, half_f, wide_b, half_b, half_b,
                   _sds((1, 2 * D_MODEL), F32)),
        compiler_params=_cparams(dimension_semantics=("arbitrary",)),
    )(dr1, proj, proj, y_attn, y_ssm, glu, ys, w_ab, w_sb, w_glu, w_out, b_gate)


def _grad_x(dproj, w_in, dr1):
    tm, tk = 512, 1792
    nk = IN_WIDTH // tk

    def body(dp_ref, w_ref, dr1_ref, o_ref, acc):
        k = pl.program_id(1)
        part = _dot_nt(dp_ref[...], w_ref[...])

        @pl.when(k == 0)
        def _():
            acc[...] = part

        @pl.when(k > 0)
        def _():
            acc[...] += part

        @pl.when(k == nk - 1)
        def _():
            o_ref[...] = DN_ALPHA * dr1_ref[...] + acc[...]

    row = pl.BlockSpec((tm, D_MODEL), lambda i, k: (i, 0))
    return _pallas_call(
        body, name="grad_x", grid=(SEQ // tm, nk),
        in_specs=[pl.BlockSpec((tm, tk), lambda i, k: (i, k)), pl.BlockSpec((D_MODEL, tk), lambda i, k: (0, k)), row],
        out_specs=row, out_shape=_sds((SEQ, D_MODEL), F32), scratch_shapes=[pltpu.VMEM((tm, D_MODEL), F32)],
        compiler_params=_cparams(dimension_semantics=("arbitrary", "arbitrary")),
    )(dproj, w_in, dr1)


def _weight_grad(a, b, name, shard_cols=None):
    k, n = a.shape[1], b.shape[1]
    tm = 512
    nm = SEQ // tm
    tk = min(k, 512)
    tn = n // 4 if shard_cols else min(n, 1024)

    def body(a_ref, b_ref, o_ref, acc):
        m = pl.program_id(2)
        part = _dot_tn(a_ref[...].astype(BF16), b_ref[...].astype(BF16))

        @pl.when(m == 0)
        def _():
            acc[...] = part

        @pl.when(m > 0)
        def _():
            acc[...] += part

        @pl.when(m == nm - 1)
        def _():
            if shard_cols:
                o_ref[0] = acc[:, :shard_cols].astype(BF16)
                o_ref[1] = acc[:, shard_cols:].astype(BF16)
            else:
                o_ref[...] = acc[...].astype(BF16)

    if shard_cols:
        out_spec = pl.BlockSpec((2, None, tk, shard_cols), lambda kk, j, m: (0, j, kk, 0))
        out_shape = _sds((2, 4, k, shard_cols), BF16)
    else:
        out_spec = pl.BlockSpec((tk, tn), lambda kk, j, m: (kk, j))
        out_shape = _sds((k, n), BF16)
    return _pallas_call(
        body, name=name, grid=(k // tk, n // tn, nm),
        in_specs=[pl.BlockSpec((tm, tk), lambda kk, j, m: (m, kk)), pl.BlockSpec((tm, tn), lambda kk, j, m: (m, j))],
        out_specs=out_spec, out_shape=out_shape, scratch_shapes=[pltpu.VMEM((tk, tn), F32)],
        compiler_params=_cparams(dimension_semantics=("arbitrary", "arbitrary", "arbitrary")),
    )(a, b)


MESH = pl.DeviceIdType.MESH
ANY = pl.BlockSpec(memory_space=pl.ANY)


def _place():
    return lax.axis_index("x"), lax.axis_index("y"), lax.axis_index("c")


def _other_chips(x, y):
    return [(1 - x, y), (x, 1 - y), (1 - x, 1 - y)]


def _send_buffer(w, rows, cols, name):
    r, c = w.shape

    def body(w_ref, o_ref):
        if (r, c) != (rows, cols):
            o_ref[...] = jnp.zeros((rows, cols), BF16)
        o_ref[:r, :c] = w_ref[...].astype(BF16)

    return _pallas_call(body, name=name, out_shape=_sds((rows, cols), BF16))(w)


def _all_gather(shards, name):
    n = len(shards)

    def body(*refs):
        ins, outs = refs[:n], refs[n:2 * n]
        send_sems, recv_sems, local_sems = refs[2 * n:]
        x, y, c = _place()
        me, sibling = (x, y, c), (x, y, 1 - c)
        chips = _other_chips(x, y)

        def slot(a, px, py, pc):
            return outs[a].at[4 * px + 2 * py + pc]

        def copy(a, k, block, to, src=None):
            return pltpu.make_async_remote_copy(
                src_ref=slot(a, *block) if src is None else src, dst_ref=slot(a, *block),
                send_sem=send_sems.at[a, k], recv_sem=recv_sems.at[a, k], device_id=to, device_id_type=MESH)

        mine = [pltpu.make_async_copy(ins[a], slot(a, *me), local_sems.at[a]) for a in range(n)]
        for cp in mine:
            cp.start()
        first = []
        for a in range(n):
            first.append(copy(a, 0, me, sibling, src=ins[a]))
            first += [copy(a, 1 + j, me, (*chip, c), src=ins[a]) for j, chip in enumerate(chips)]
        for cp in first:
            cp.start()
        passed = []
        for j, chip in enumerate(chips):
            for a in range(n):
                copy(a, 1 + j, (*chip, c), me).wait_recv()
                onward = copy(a, 4 + j, (*chip, c), sibling)
                onward.start()
                passed.append(onward)
        for a in range(n):
            copy(a, 0, sibling, me).wait_recv()
            for j, chip in enumerate(chips):
                copy(a, 4 + j, (*chip, 1 - c), me).wait_recv()
        for cp in first + passed:
            cp.wait_send()
        for cp in mine:
            cp.wait()

    return _pallas_call(
        body, name=name, in_specs=[ANY] * n, out_specs=[ANY] * n,
        out_shape=[_sds((N_DEV,) + s.shape, s.dtype) for s in shards],
        scratch_shapes=[pltpu.SemaphoreType.DMA((n, 7)), pltpu.SemaphoreType.DMA((n, 7)), pltpu.SemaphoreType.DMA((n,))],
    )(*shards)


def _swap_with_sibling(grads, name):
    n = len(grads)

    def body(*refs):
        ins, outs = refs[:n], refs[n:2 * n]
        send_sems, recv_sems = refs[2 * n:]
        x, y, c = _place()
        copies = [pltpu.make_async_remote_copy(
            src_ref=ins[a].at[1 - c], dst_ref=outs[a], send_sem=send_sems.at[a], recv_sem=recv_sems.at[a],
            device_id=(x, y, 1 - c), device_id_type=MESH) for a in range(n)]
        for cp in copies:
            cp.start()
        for cp in copies:
            cp.wait()

    return _pallas_call(
        body, name=name, in_specs=[ANY] * n, out_specs=[ANY] * n,
        out_shape=[_sds(g.shape[1:], g.dtype) for g in grads],
        scratch_shapes=[pltpu.SemaphoreType.DMA((n,)), pltpu.SemaphoreType.DMA((n,))],
    )(*grads)


def _pair_sum(g, r, core, name):
    _, _, k, n = g.shape

    def body(core_ref, g_ref, r_ref, o_ref):
        o_ref[...] = (g_ref[...].astype(F32) + r_ref[...].astype(F32)).astype(o_ref.dtype)

    return _pallas_call(
        body, name=name,
        grid_spec=pltpu.PrefetchScalarGridSpec(
            num_scalar_prefetch=1, grid=(4,),
            in_specs=[pl.BlockSpec((None, None, k, n), lambda p, core_ref: (core_ref[0], p, 0, 0)),
                      pl.BlockSpec((None, k, n), lambda p, core_ref: (p, 0, 0))],
            out_specs=pl.BlockSpec((None, k, n), lambda p, core_ref: (p, 0, 0))),
        out_shape=_sds((4, k, n), g.dtype), compiler_params=_cparams(dimension_semantics=("arbitrary",)),
    )(core, g, r)


def _swap_between_chips(sums, name):
    n = len(sums)

    def body(*refs):
        ins, outs = refs[:n], refs[n:2 * n]
        send_sems, recv_sems, local_sems = refs[2 * n:]
        x, y, c = _place()
        mine = 2 * x + y
        copies = []
        for a in range(n):
            copies.append(pltpu.make_async_copy(ins[a].at[mine], outs[a].at[mine], local_sems.at[a]))
            for j, (px, py) in enumerate(_other_chips(x, y)):
                copies.append(pltpu.make_async_remote_copy(
                    src_ref=ins[a].at[2 * px + py], dst_ref=outs[a].at[mine], send_sem=send_sems.at[a, j],
                    recv_sem=recv_sems.at[a, j], device_id=(px, py, c), device_id_type=MESH))
        for cp in copies:
            cp.start()
        for cp in copies:
            cp.wait()

    return _pallas_call(
        body, name=name, in_specs=[ANY] * n, out_specs=[ANY] * n, out_shape=[_sds(s.shape, s.dtype) for s in sums],
        scratch_shapes=[pltpu.SemaphoreType.DMA((n, 3)), pltpu.SemaphoreType.DMA((n, 3)), pltpu.SemaphoreType.DMA((n,))],
    )(*sums)


def _adamw_math(w, g, m, v):
    m = ADAM_B1 * m + (1.0 - ADAM_B1) * g
    v = ADAM_B2 * v + (1.0 - ADAM_B2) * (g * g)
    m_hat = m / (1.0 - ADAM_B1 ** ADAM_STEP)
    v_hat = v / (1.0 - ADAM_B2 ** ADAM_STEP)
    return -ADAM_LR * (m_hat / (jnp.sqrt(v_hat) + ADAM_EPS) + ADAM_WD * w), m, v


def _adamw(w, m, v, parts, name):
    r, c = w.shape
    tr = r if r <= 512 else 256
    n_parts, pr, pc = parts.shape
    assert r % tr == 0 and (tr == r or pr == r)

    def body(w_ref, m_ref, v_ref, p_ref, g_out, d_out, m_out, v_out):
        g = p_ref[0, :tr, :c].astype(F32)
        for p in range(1, n_parts):
            g = g + p_ref[p, :tr, :c].astype(F32)
        g_out[...] = g
        d_out[...], m_out[...], v_out[...] = _adamw_math(w_ref[...], g, m_ref[...], v_ref[...])

    tile = pl.BlockSpec((tr, c), lambda i: (i, 0))
    part_tile = pl.BlockSpec((n_parts, pr if tr == r else tr, pc), lambda i: (0, i, 0))
    out = _sds((r, c), F32)
    return _pallas_call(
        body, name=name, grid=(r // tr,), in_specs=[tile, tile, tile, part_tile], out_specs=(tile,) * 4,
        out_shape=(out,) * 4, compiler_params=_cparams(dimension_semantics=("arbitrary",)),
    )(w, m, v, parts)


SMALL = (("ssm_a_re", (32, 64)), ("ssm_a_im", (32, 64)), ("ssm_log_dt", (32,)), ("ssm_b_re", (32, 64, 16)),
         ("ssm_b_im", (32, 64, 16)), ("ssm_c_re", (32, 16, 64)), ("ssm_c_im", (32, 16, 64)), ("ssm_d", (512,)),
         ("ln1_g", (1024,)), ("ln1_b", (1024,)), ("ln2_g", (1024,)), ("ln2_b", (1024,)))


def _pack_small(arrays):
    rows = []
    for a in arrays:
        flat = a.reshape(-1)
        rows.append(jnp.pad(flat, (0, -flat.shape[0] % 128)).reshape(-1, 128))
    packed = jnp.concatenate(rows, axis=0)
    return jnp.pad(packed, ((0, -packed.shape[0] % 256), (0, 0)))


def _unpack_small(packed):
    out, row = [], 0
    for _, shape in SMALL:
        size = math.prod(shape)
        n_rows = -(-size // 128)
        out.append(packed[row:row + n_rows].reshape(-1)[:size].reshape((1,) + shape))
        row += n_rows
    return out


def kernel(x, w_in, b_gate, w_attn_br, w_ssm_br, w_out, ssm_a_re, ssm_a_im, ssm_log_dt, ssm_b_re, ssm_b_im, ssm_c_re, ssm_c_im, ssm_d, w_glu, ln1_g, ln1_b, w_ff_gate, w_ff_up, w_ff_down, ln2_g, ln2_b, loss_target, m_w_in, m_b_gate, m_w_attn_br, m_w_ssm_br, m_w_out, m_ssm_a_re, m_ssm_a_im, m_ssm_log_dt, m_ssm_b_re, m_ssm_b_im, m_ssm_c_re, m_ssm_c_im, m_ssm_d, m_w_glu, m_ln1_g, m_ln1_b, m_w_ff_gate, m_w_ff_up, m_w_ff_down, m_ln2_g, m_ln2_b, v_w_in, v_b_gate, v_w_attn_br, v_w_ssm_br, v_w_out, v_ssm_a_re, v_ssm_a_im, v_ssm_log_dt, v_ssm_b_re, v_ssm_b_im, v_ssm_c_re, v_ssm_c_im, v_ssm_d, v_w_glu, v_ln1_g, v_ln1_b, v_w_ff_gate, v_w_ff_up, v_w_ff_down, v_ln2_g, v_ln2_b):
    given = dict(locals())
    x2, target = x[0], loss_target[0]
    core = lax.axis_index("c").astype(jnp.int32).reshape(1)

    sharded = ("w_in", "w_attn_br", "w_ssm_br", "w_glu", "w_out", "w_ff_gate", "w_ff_up", "w_ff_down", "b_gate")
    send_shape = dict(w_in=(D_MODEL, 896), w_attn_br=(ATTN_WIDTH, 128), w_ssm_br=(SSM_WIDTH, 128), w_glu=(SSM_WIDTH, 128),
                      w_out=(128, D_MODEL), w_ff_gate=(D_MODEL, FF_PAD), w_ff_up=(D_MODEL, FF_PAD), w_ff_down=(FF_PAD, D_MODEL))
    local = {k: given[k][0] for k in sharded}
    sends = [_send_buffer(local[k], *send_shape[k], name="send_" + k) for k in sharded[:-1]] + [local["b_gate"]]
    gathered = dict(zip(sharded, _all_gather(sends, "gather_weights")))
    wt = {k: _unshard(gathered[k], "unshard_" + k) for k in ("w_in", "w_attn_br", "w_ssm_br", "w_glu", "w_ff_gate", "w_ff_up")}
    wt["w_out"] = gathered["w_out"].reshape(D_MODEL, D_MODEL)
    wt["w_ff_down"] = gathered["w_ff_down"].reshape(D_FF_PAD, D_MODEL)
    b_gate_full = gathered["b_gate"].transpose(1, 0, 2).reshape(2, D_MODEL)

    a_re, a_im, log_dt = ssm_a_re[0], ssm_a_im[0], ssm_log_dt[0].reshape(SSM_GROUPS, 1)
    b_re_t, b_im_t = ssm_b_re[0].transpose(0, 2, 1), ssm_b_im[0].transpose(0, 2, 1)
    abar_re, abar_im, e_re, e_im, bbar_re_t, bbar_im_t = _ssm_prep(a_re, a_im, log_dt, b_re_t, b_im_t)
    bmat, cmat, a_chunks = _ssm_tables(abar_re, abar_im, bbar_re_t, bbar_im_t, ssm_c_re[0], ssm_c_im[0])
    cos_t, sin_t = _rope_tables()

    proj = _proj(x2, wt["w_in"])
    attn, lse = _attn_fwd(proj, cos_t, sin_t)
    ys, states = _ssm_fwd(proj, bmat, cmat, a_chunks, ssm_d)
    h, xhat1, rstd1, glu, y_attn, y_ssm = _mixer_out(attn, ys, proj, x2, wt["w_attn_br"], wt["w_ssm_br"], wt["w_glu"],
                                                      wt["w_out"], b_gate_full, ln1_g, ln1_b)
    ff_a, ff_b, ff_f = _ff_up(h, wt["w_ff_gate"], wt["w_ff_up"])
    dr2, d_ln2_g, d_ln2_b, loss_lanes = _ff_down_loss(ff_f, wt["w_ff_down"], h, target, ln2_g, ln2_b)
    loss = lax.psum(loss_lanes[0, 0], ("x", "y", "c"))

    d_a, d_b = _ff_down_bwd(dr2, wt["w_ff_down"], ff_a, ff_b)
    dr1, d_ln1_g, d_ln1_b = _ff_up_bwd(d_a, d_b, wt["w_ff_gate"], wt["w_ff_up"], dr2, xhat1, rstd1, ln1_g)
    d_ya, d_yssm, d_gl, d_attn, d_glu, d_ys, mixed, y_s, gy, d_bg = _mixer_bwd(
        dr1, proj, y_attn, y_ssm, glu, ys, wt["w_attn_br"], wt["w_ssm_br"], wt["w_glu"], wt["w_out"], b_gate_full)
    d_qkv = [_attn_bwd_group(g, proj, cos_t, sin_t, attn, lse, d_attn) for g in range(3)]
    d_u, d_bmat, d_cmat, d_abar, d_skip = _ssm_bwd(d_ys, proj, states, bmat, cmat, a_chunks, ssm_d)
    d_proj = jnp.concatenate([d_qkv[g][kind].astype(BF16) for kind in range(3) for g in range(3)] + [d_u.astype(BF16), d_gl], axis=1)
    grad_x = _grad_x(d_proj, wt["w_in"], dr1)

    contrib = dict(
        w_in=_weight_grad(x2, d_proj, "wgrad_w_in", 896),
        w_attn_br=_weight_grad(attn, d_ya, "wgrad_w_attn_br", 128),
        w_ssm_br=_weight_grad(y_s, d_yssm, "wgrad_w_ssm_br", 128),
        w_glu=_weight_grad(gy, d_glu, "wgrad_w_glu", 128),
        w_out=_weight_grad(mixed, dr1, "wgrad_w_out").reshape(4, 2, 128, D_MODEL).transpose(1, 0, 2, 3),
        w_ff_gate=_weight_grad(h, d_a, "wgrad_w_ff_gate", FF_PAD),
        w_ff_up=_weight_grad(h, d_b, "wgrad_w_ff_up", FF_PAD),
        w_ff_down=_weight_grad(ff_f, dr2, "wgrad_w_ff_down").reshape(4, 2, FF_PAD, D_MODEL).transpose(1, 0, 2, 3),
        b_gate=d_bg.reshape(2, 4, 2, 128).transpose(2, 1, 0, 3),
    )
    from_sibling = _swap_with_sibling([contrib[k] for k in sharded], "swap_with_sibling")
    chip_sums = [_pair_sum(contrib[k], r, core, "pair_sum_" + k) for k, r in zip(sharded, from_sibling)]
    parts = dict(zip(sharded, _swap_between_chips(chip_sums, "swap_between_chips")))

    grads, deltas, new_m, new_v = {}, {}, {}, {}
    for k in sharded:
        w2 = local[k]
        out = _adamw(w2, given["m_" + k][0], given["v_" + k][0], parts[k], "adamw_" + k)
        grads[k], deltas[k], new_m[k], new_v[k] = (o.reshape((1,) + w2.shape) for o in out)

    gbb_re_t, gbb_im_t = _block_diag_parts(d_bmat, True)
    gc_re, gc_im = _block_diag_parts(d_cmat, False)
    ga_re = d_abar[:, 0, :CHUNK_STATES].reshape(SSM_GROUPS, SSM_STATE)
    ga_im = d_abar[:, 0, CHUNK_STATES:].reshape(SSM_GROUPS, SSM_STATE)
    g_a_re, g_a_im, g_log_dt, g_b_re_t, g_b_im_t = _ssm_param_bwd(
        a_re, a_im, log_dt, b_re_t, b_im_t, abar_re, abar_im, e_re, e_im, ga_re, ga_im, gbb_re_t, gbb_im_t)
    small_contrib = _pack_small([g_a_re, g_a_im, g_log_dt, g_b_re_t.transpose(0, 2, 1), g_b_im_t.transpose(0, 2, 1), gc_re, -gc_im,
                                 d_skip, d_ln1_g, d_ln1_b, d_ln2_g, d_ln2_b])
    small_parts, = _all_gather([small_contrib], "gather_small_grads")
    names = [k for k, _ in SMALL]
    small = _adamw(_pack_small([given[k] for k in names]), _pack_small([given["m_" + k] for k in names]),
                   _pack_small([given["v_" + k] for k in names]), small_parts, "adamw_small")
    for res, packed in zip((grads, deltas, new_m, new_v), small):
        res.update(zip(names, _unpack_small(packed)))

    order = ("w_in", "b_gate", "w_attn_br", "w_ssm_br", "w_out", "ssm_a_re", "ssm_a_im", "ssm_log_dt", "ssm_b_re", "ssm_b_im",
             "ssm_c_re", "ssm_c_im", "ssm_d", "w_glu", "ln1_g", "ln1_b", "w_ff_gate", "w_ff_up", "w_ff_down", "ln2_g", "ln2_b")
    return (loss, grad_x[None], *[grads[k] for k in order], *[deltas[k] for k in order], *[new_m[k] for k in order],
            *[new_v[k] for k in order])
```

```python
import functools
import math

import jax
import jax.numpy as jnp
from jax import lax
from jax.experimental import pallas as pl
from jax.experimental.pallas import tpu as pltpu

F32 = jnp.float32
BF16 = jnp.bfloat16

N_DEV = 8
SEQ = 2048
D_MODEL = 1024
HEAD_DIM = 64
ATTN_WIDTH = 512
QKV_WIDTH = 1536
SSM_WIDTH = 512
SSM_GROUPS = 32
SSM_GROUP = 16
SSM_STATE = 64
IN_WIDTH = 7168
D_FF = 2816
FF_SHARD = D_FF // N_DEV
FF_PAD = 384
D_FF_PAD = FF_PAD * N_DEV
DN_ALPHA = 2.0 ** 0.25
LN_EPS = 1e-5
NEG_INF = -1e30
ROPE_THETA = 10000.0
BLOCK = 128
GROUPS = ((1, 16), (4, 4), (16, 1))

ADAM_LR = 0.001
ADAM_B1 = 0.9
ADAM_B2 = 0.999
ADAM_EPS = 1e-08
ADAM_WD = 0.01
ADAM_STEP = 10

VMEM_LIMIT = 56 * 1024 * 1024


_pallas_call = pl.pallas_call


def _cparams(**kw):
    return pltpu.CompilerParams(vmem_limit_bytes=VMEM_LIMIT, **kw)


def _dot(a, b):
    return jnp.dot(a, b, preferred_element_type=F32)


def _dot_nt(a, b):
    return lax.dot_general(a, b, (((1,), (1,)), ((), ())), preferred_element_type=F32)


def _dot_tn(a, b):
    return lax.dot_general(a, b, (((0,), (0,)), ((), ())), preferred_element_type=F32)


def _rope_tables():
    half = HEAD_DIM // 2
    inv_freq = ROPE_THETA ** (-jnp.arange(half, dtype=F32) / half)
    ang = jnp.arange(SEQ, dtype=F32)[:, None] * inv_freq[None, :]
    cos, sin = jnp.cos(ang), jnp.sin(ang)
    return jnp.tile(cos, (1, 4)), jnp.tile(jnp.concatenate([-sin, sin], axis=1), (1, 2))


def _swap_halves(x):
    lane = lax.broadcasted_iota(jnp.int32, x.shape, 1)
    return jnp.where((lane & 63) < 32, pltpu.roll(x, 96, axis=1), pltpu.roll(x, 32, axis=1))


def _group_rows(d, nb, r, i):
    src = pl.ds(i * BLOCK, BLOCK) if d == 1 else pl.ds(r + i * BLOCK * d, BLOCK, stride=d)
    return src, pl.ds((r * nb + i) * BLOCK, BLOCK)


def _attn_masks():
    a_idx = lax.broadcasted_iota(jnp.int32, (2 * BLOCK, 2 * BLOCK), 0) & (BLOCK - 1)
    c_idx = lax.broadcasted_iota(jnp.int32, (2 * BLOCK, 2 * BLOCK), 1)
    cur_ok = jnp.logical_and(c_idx >= BLOCK, c_idx - BLOCK <= a_idx)
    prev_ok = jnp.logical_and(c_idx < BLOCK, c_idx >= a_idx)
    lane = lax.broadcasted_iota(jnp.int32, (BLOCK, 128), 1)
    return cur_ok, prev_ok, lane < HEAD_DIM


def _stack_heads(t, head0):
    zero = jnp.zeros_like(t)
    return jnp.concatenate([jnp.where(head0, t, zero), jnp.where(head0, zero, t)], axis=0)


def _unstack_heads(t2, head0):
    return jnp.where(head0, t2[:BLOCK], t2[BLOCK:])


def _attn_fwd(proj, cos_t, sin_t):
    def body(q0, q1, q2, k0, k1, k2, v0, v1, v2, cos_ref, sin_ref, attn_ref, lse_ref,
             qs, ks, vs, os_, ms, ls, acc, mnat, lnat):
        cur_ok, prev_ok, head0 = _attn_masks()
        ks[:BLOCK, :] = jnp.zeros((BLOCK, 128), BF16)
        vs[:BLOCK, :] = jnp.zeros((BLOCK, 128), BF16)
        for g, (d, nb) in enumerate(GROUPS):
            q_ref, k_ref, v_ref = (q0, q1, q2)[g], (k0, k1, k2)[g], (v0, v1, v2)[g]
            for r in range(d):
                for i in range(nb):
                    src, dst = _group_rows(d, nb, r, i)
                    below = pl.ds(dst.start + BLOCK, BLOCK)
                    c, s = cos_ref[src, :], sin_ref[src, :]
                    q = q_ref[src, :]
                    k = k_ref[src, :]
                    qs[dst, :] = ((q * c + _swap_halves(q) * s) * 0.125).astype(BF16)
                    ks[below, :] = (k * c + _swap_halves(k) * s).astype(BF16)
                    vs[below, :] = v_ref[src, :].astype(BF16)

            def block(b, carry, nb=nb):
                has_prev = (b & (nb - 1)) > 0
                cur = pl.ds(pl.multiple_of(b * BLOCK, BLOCK), BLOCK)
                window = pl.ds(pl.multiple_of(b * BLOCK, BLOCK), 2 * BLOCK)
                valid = jnp.logical_or(cur_ok, jnp.logical_and(prev_ok, has_prev))
                s = jnp.where(valid, _dot_nt(_stack_heads(qs[cur, :], head0), ks[window, :]), NEG_INF)
                m = jnp.max(s, axis=1, keepdims=True)
                p = jnp.exp(s - m)
                os_[cur, :] = _unstack_heads(_dot(p.astype(BF16), vs[window, :]), head0)
                ms[cur, :] = _unstack_heads(m, head0)
                ls[cur, :] = _unstack_heads(jnp.sum(p, axis=1, keepdims=True), head0)
                return carry

            lax.fori_loop(0, SEQ // BLOCK, block, 0)

            for r in range(d):
                for i in range(nb):
                    src, dst = _group_rows(d, nb, r, i)
                    if g == 0:
                        acc[src, :], mnat[src, :], lnat[src, :] = os_[dst, :], ms[dst, :], ls[dst, :]
                    else:
                        m_old, m_g = mnat[src, :], ms[dst, :]
                        m_new = jnp.maximum(m_old, m_g)
                        a_old, a_g = jnp.exp(m_old - m_new), jnp.exp(m_g - m_new)
                        acc[src, :] = a_old * acc[src, :] + a_g * os_[dst, :]
                        lnat[src, :] = a_old * lnat[src, :] + a_g * ls[dst, :]
                        mnat[src, :] = m_new
        for i in range(SEQ // BLOCK):
            rows = pl.ds(i * BLOCK, BLOCK)
            l = lnat[rows, :]
            attn_ref[rows, :] = acc[rows, :] / l
            lse_ref[rows, :] = mnat[rows, :] + jnp.log(l)

    def col(base):
        return pl.BlockSpec((SEQ, 128), lambda hp, base=base: (0, base + hp))

    in_specs = [col(g * 4) for g in range(3)] + [col(12 + g * 4) for g in range(3)] + [col(24 + g * 4) for g in range(3)]
    table = pl.BlockSpec((SEQ, 128), lambda hp: (0, 0))
    out = pl.BlockSpec((SEQ, 128), lambda hp: (0, hp))
    return _pallas_call(
        body, name="attn_fwd", grid=(4,),
        in_specs=in_specs + [table, table], out_specs=(out, out),
        out_shape=(jax.ShapeDtypeStruct((SEQ, ATTN_WIDTH), F32), jax.ShapeDtypeStruct((SEQ, ATTN_WIDTH), F32)),
        scratch_shapes=[pltpu.VMEM((SEQ, 128), BF16)] + [pltpu.VMEM((SEQ + BLOCK, 128), BF16)] * 2 + [pltpu.VMEM((SEQ, 128), F32)] * 6,
        compiler_params=_cparams(dimension_semantics=("arbitrary",)),
    )(*([proj] * 9), cos_t, sin_t)


def _attn_bwd_group(g, proj, cos_t, sin_t, attn, lse, dattn, dproj):
    d, nb = GROUPS[g]

    def body(q_ref, k_ref, v_ref, cos_ref, sin_ref, attn_ref, lse_ref, dattn_ref, dproj_in, dproj_ref,
             qs, ks, vs, dos, lss, dss, dqs, dks, dvs, stage, outs, sems):
        del dproj_in
        cur_ok, prev_ok, head0 = _attn_masks()
        ks[:BLOCK, :] = jnp.zeros((BLOCK, 128), BF16)
        vs[:BLOCK, :] = jnp.zeros((BLOCK, 128), BF16)
        dks[:BLOCK, :] = jnp.zeros((BLOCK, 128), F32)
        dvs[:BLOCK, :] = jnp.zeros((BLOCK, 128), F32)
        for r in range(d):
            for i in range(nb):
                src, dst = _group_rows(d, nb, r, i)
                below = pl.ds(dst.start + BLOCK, BLOCK)
                c, s = cos_ref[src, :], sin_ref[src, :]
                q = q_ref[src, :]
                k = k_ref[src, :]
                qs[dst, :] = ((q * c + _swap_halves(q) * s) * 0.125).astype(BF16)
                ks[below, :] = (k * c + _swap_halves(k) * s).astype(BF16)
                vs[below, :] = v_ref[src, :].astype(BF16)
                do = dattn_ref[src, :]
                prod = do * attn_ref[src, :]
                d0 = jnp.sum(jnp.where(head0, prod, 0.0), axis=1, keepdims=True)
                d1 = jnp.sum(jnp.where(head0, 0.0, prod), axis=1, keepdims=True)
                dos[dst, :] = do.astype(BF16)
                dss[dst, :] = jnp.where(head0, d0, d1)
                lss[dst, :] = lse_ref[src, :]
                dks[below, :] = jnp.zeros((BLOCK, 128), F32)
                dvs[below, :] = jnp.zeros((BLOCK, 128), F32)

        def per_head_column(t):
            return jnp.concatenate([jnp.max(jnp.where(head0, t, NEG_INF), axis=1, keepdims=True),
                                    jnp.max(jnp.where(head0, NEG_INF, t), axis=1, keepdims=True)], axis=0)

        def block(b, carry):
            has_prev = (b & (nb - 1)) > 0
            cur = pl.ds(pl.multiple_of(b * BLOCK, BLOCK), BLOCK)
            window = pl.ds(pl.multiple_of(b * BLOCK, BLOCK), 2 * BLOCK)
            valid = jnp.logical_or(cur_ok, jnp.logical_and(prev_ok, has_prev))
            q2, do2 = _stack_heads(qs[cur, :], head0), _stack_heads(dos[cur, :], head0)
            kw, vw = ks[window, :], vs[window, :]
            s = jnp.where(valid, _dot_nt(q2, kw), NEG_INF)
            p = jnp.exp(s - per_head_column(lss[cur, :]))
            ds = (p * (_dot_nt(do2, vw) - per_head_column(dss[cur, :]))).astype(BF16)
            dvs[window, :] += _dot_tn(p.astype(BF16), do2)
            dks[window, :] += _dot_tn(ds, q2)
            dqs[cur, :] = _unstack_heads(_dot(ds, kw), head0)
            return carry

        lax.fori_loop(0, SEQ // BLOCK, block, 0)

        hp = pl.program_id(0)
        copies = []
        for kind in range(3):
            for r in range(d):
                for i in range(nb):
                    src, dst = _group_rows(d, nb, r, i)
                    below = pl.ds(dst.start + BLOCK, BLOCK)
                    if kind == 2:
                        stage[src, :] = dvs[below, :]
                    else:
                        c, s = cos_ref[src, :], sin_ref[src, :]
                        t = dqs[dst, :] * 0.125 if kind == 0 else dks[below, :]
                        stage[src, :] = t * c - _swap_halves(t) * s
            for i in range(SEQ // MM_ROWS):
                rows = pl.ds(i * MM_ROWS, MM_ROWS)
                outs[kind, rows, :] = stage[rows, :].astype(BF16)
            column = pl.multiple_of((kind * 12 + g * 4 + hp) * 128, 128)
            copies.append(pltpu.make_async_copy(outs.at[kind], dproj_ref.at[:, pl.ds(column, 128)], sems.at[kind]))
            copies[-1].start()
        for cp in copies:
            cp.wait()

    def col(base):
        return pl.BlockSpec((SEQ, 128), lambda hp, base=base: (0, base + hp))

    table = pl.BlockSpec((SEQ, 128), lambda hp: (0, 0))
    return _pallas_call(
        body, name=f"attn_bwd_g{g}", grid=(4,),
        in_specs=[col(g * 4), col(12 + g * 4), col(24 + g * 4), table, table, col(0), col(0), col(0), ANY],
        out_specs=ANY, out_shape=_sds((SEQ, IN_WIDTH), BF16), input_output_aliases={8: 0},
        scratch_shapes=[pltpu.VMEM((SEQ, 128), BF16)] + [pltpu.VMEM((SEQ + BLOCK, 128), BF16)] * 2 + [pltpu.VMEM((SEQ, 128), BF16)]
        + [pltpu.VMEM((SEQ, 128), F32)] * 3 + [pltpu.VMEM((SEQ + BLOCK, 128), F32)] * 2 + [pltpu.VMEM((SEQ, 128), F32)]
        + [pltpu.VMEM((3, SEQ, 128), BF16), pltpu.SemaphoreType.DMA((3,))],
        compiler_params=_cparams(dimension_semantics=("arbitrary",)),
    )(proj, proj, proj, cos_t, sin_t, attn, lse, dattn, dproj)


SSM_CHUNKS = 4
CHUNK_STATES = 512
SCAN_ROWS = 8
U_COL = (3 * QKV_WIDTH) // 128


def _cmul(xr, xi, yr, yi):
    return xr * yr - xi * yi, xr * yi + xi * yr


def _ssm_prep(a_re, a_im, log_dt, b_re_t, b_im_t):
    def body(ar_ref, ai_ref, ldt_ref, br_ref, bi_ref, abr_ref, abi_ref, er_ref, ei_ref, bbr_ref, bbi_ref):
        ar, ai = ar_ref[...], ai_ref[...]
        dt = jnp.exp(ldt_ref[...])
        mag = jnp.exp(ar * dt)
        abr, abi = mag * jnp.cos(ai * dt), mag * jnp.sin(ai * dt)
        den = ar * ar + ai * ai
        nr, ni = abr - 1.0, abi
        er, ei = (nr * ar + ni * ai) / den, (ni * ar - nr * ai) / den
        abr_ref[...], abi_ref[...], er_ref[...], ei_ref[...] = abr, abi, er, ei
        er3, ei3 = er[:, None, :], ei[:, None, :]
        br, bi = br_ref[...], bi_ref[...]
        bbr_ref[...] = er3 * br - ei3 * bi
        bbi_ref[...] = er3 * bi + ei3 * br

    gp = jax.ShapeDtypeStruct(a_re.shape, F32)
    gb = jax.ShapeDtypeStruct(b_re_t.shape, F32)
    return _pallas_call(body, name="ssm_prep", out_shape=(gp, gp, gp, gp, gb, gb))(a_re, a_im, log_dt, b_re_t, b_im_t)


def _ssm_param_bwd(a_re, a_im, log_dt, b_re_t, b_im_t, abar_re, abar_im, e_re, e_im, ga_re, ga_im, gbb_re_t, gbb_im_t):
    def body(ar_ref, ai_ref, ldt_ref, br_ref, bi_ref, abr_ref, abi_ref, er_ref, ei_ref, gar_ref, gai_ref, gbr_ref, gbi_ref,
             o_ar, o_ai, o_ldt, o_br, o_bi):
        ar, ai = ar_ref[...], ai_ref[...]
        dt = jnp.exp(ldt_ref[...])
        er, ei = er_ref[...], ei_ref[...]
        br, bi, gbr, gbi = br_ref[...], bi_ref[...], gbr_ref[...], gbi_ref[...]
        er3, ei3 = er[:, None, :], ei[:, None, :]
        o_br[...] = er3 * gbr + ei3 * gbi
        o_bi[...] = er3 * gbi - ei3 * gbr
        ge_r = jnp.sum(br * gbr + bi * gbi, axis=1)
        ge_i = jnp.sum(br * gbi - bi * gbr, axis=1)
        den = ar * ar + ai * ai
        ilr, ili = ar / den, -ai / den
        t_r, t_i = _cmul(ilr, -ili, ge_r, ge_i)
        gab_r, gab_i = gar_ref[...] + t_r, gai_ref[...] + t_i
        gz_r, gz_i = _cmul(abr_ref[...], -abi_ref[...], gab_r, gab_i)
        el_r, el_i = _cmul(er, ei, ilr, ili)
        u_r, u_i = _cmul(el_r, -el_i, ge_r, ge_i)
        o_ar[...] = dt * gz_r - u_r
        o_ai[...] = dt * gz_i - u_i
        o_ldt[...] = jnp.sum(gz_r * ar + gz_i * ai, axis=1, keepdims=True) * dt

    gp = jax.ShapeDtypeStruct(a_re.shape, F32)
    gb = jax.ShapeDtypeStruct(b_re_t.shape, F32)
    return _pallas_call(body, name="ssm_param_bwd", out_shape=(gp, gp, jax.ShapeDtypeStruct(log_dt.shape, F32), gb, gb))(
        a_re, a_im, log_dt, b_re_t, b_im_t, abar_re, abar_im, e_re, e_im, ga_re, ga_im, gbb_re_t, gbb_im_t)


def _block_diag(blocks_re, blocks_im, sign_im, rows_are_channels):
    both = jnp.stack([blocks_re, sign_im * blocks_im]).reshape(2, SSM_CHUNKS, 8, SSM_GROUP, SSM_STATE)
    eye = jnp.eye(8, dtype=F32)
    if rows_are_channels:
        return jnp.einsum("rcghp,gk->cghrkp", both, eye).reshape(SSM_CHUNKS, 128, 2 * CHUNK_STATES)
    return jnp.einsum("rcghp,gk->crkpgh", both, eye).reshape(SSM_CHUNKS, 2 * CHUNK_STATES, 128)


def _block_diag_parts(mat, rows_are_channels):
    if rows_are_channels:
        six = mat.reshape(SSM_CHUNKS, 8, SSM_GROUP, 2, 8, SSM_STATE)
        parts = jnp.einsum("cghrgp->rcghp", six)
    else:
        six = mat.reshape(SSM_CHUNKS, 2, 8, SSM_STATE, 8, SSM_GROUP)
        parts = jnp.einsum("crgpgh->rcghp", six)
    parts = parts.reshape(2, SSM_GROUPS, SSM_GROUP, SSM_STATE)
    return parts[0], parts[1]


def _scan_consts(a_ref, conj, reverse):
    ar = jnp.broadcast_to(a_ref[:, :CHUNK_STATES], (SCAN_ROWS, CHUNK_STATES))
    ai = jnp.broadcast_to(a_ref[:, CHUNK_STATES:], (SCAN_ROWS, CHUNK_STATES))
    if conj:
        ai = -ai
    row = lax.broadcasted_iota(jnp.int32, (SCAN_ROWS, CHUNK_STATES), 0)
    if reverse:
        row = SCAN_ROWS - 1 - row
    zero = jnp.zeros_like(ar)
    steps = []
    pr, pi = ar, ai
    for shift in (1, 2, 4):
        keep = row >= shift
        steps.append((SCAN_ROWS - shift if reverse else shift, jnp.where(keep, pr, zero), jnp.where(keep, pi, zero)))
        pr, pi = _cmul(pr, pi, pr, pi)
    first = row == 0
    return steps, (jnp.where(first, ar, zero), jnp.where(first, ai, zero)), first


def _scan_tile(xr, xi, prev_r, prev_i, steps, carry_in, reverse):
    edge = SCAN_ROWS - 1 if reverse else 1
    cr, ci = pltpu.roll(prev_r, edge, axis=0), pltpu.roll(prev_i, edge, axis=0)
    xr, xi = xr + carry_in[0] * cr - carry_in[1] * ci, xi + carry_in[0] * ci + carry_in[1] * cr
    for shift, mr, mi in steps:
        sr, si = pltpu.roll(xr, shift, axis=0), pltpu.roll(xi, shift, axis=0)
        xr, xi = xr + mr * sr - mi * si, xi + mr * si + mi * sr
    return xr, xi


MM_ROWS = 256


def _ssm_fwd(proj, bmat, cmat, a_chunks, d_skip):
    def body(u_ref, b_ref, c_ref, a_ref, d_ref, y_ref, h_ref):
        for i in range(SEQ // MM_ROWS):
            rows = pl.ds(i * MM_ROWS, MM_ROWS)
            h_ref[rows, :] = _dot(u_ref[rows, :].astype(BF16), b_ref[...])
        steps, carry_in, _ = _scan_consts(a_ref, conj=False, reverse=False)

        def tile(k, carry):
            rows = pl.ds(pl.multiple_of(k * SCAN_ROWS, SCAN_ROWS), SCAN_ROWS)
            xr, xi = _scan_tile(h_ref[rows, :CHUNK_STATES], h_ref[rows, CHUNK_STATES:], carry[0], carry[1], steps, carry_in, False)
            h_ref[rows, :CHUNK_STATES] = xr
            h_ref[rows, CHUNK_STATES:] = xi
            return xr, xi

        zero = jnp.zeros((SCAN_ROWS, CHUNK_STATES), F32)
        lax.fori_loop(0, SEQ // SCAN_ROWS, tile, (zero, zero))
        for i in range(SEQ // MM_ROWS):
            rows = pl.ds(i * MM_ROWS, MM_ROWS)
            y_ref[rows, :] = _dot(h_ref[rows, :].astype(BF16), c_ref[...]) + d_ref[...] * u_ref[rows, :]

    return _pallas_call(
        body, name="ssm_fwd", grid=(SSM_CHUNKS,),
        in_specs=[pl.BlockSpec((SEQ, 128), lambda c: (0, U_COL + c)),
                  pl.BlockSpec((None, 128, 2 * CHUNK_STATES), lambda c: (c, 0, 0)),
                  pl.BlockSpec((None, 2 * CHUNK_STATES, 128), lambda c: (c, 0, 0)),
                  pl.BlockSpec((None, 1, 2 * CHUNK_STATES), lambda c: (c, 0, 0)),
                  pl.BlockSpec((1, 128), lambda c: (0, c))],
        out_specs=(pl.BlockSpec((SEQ, 128), lambda c: (0, c)), pl.BlockSpec((SEQ, 2 * CHUNK_STATES), lambda c: (0, c))),
        out_shape=(jax.ShapeDtypeStruct((SEQ, SSM_WIDTH), F32), jax.ShapeDtypeStruct((SEQ, SSM_CHUNKS * 2 * CHUNK_STATES), F32)),
        compiler_params=_cparams(dimension_semantics=("arbitrary",)),
    )(proj, bmat, cmat, a_chunks, d_skip)


def _ssm_bwd(dys, proj, h, bmat, cmat, a_chunks, d_skip, dproj):
    def body(dy_ref, u_ref, h_ref, b_ref, c_ref, a_ref, d_ref, dproj_in, du_ref, db_ref, dc_ref, da_ref, dd_ref, g_ref):
        del dproj_in
        dsum = jnp.zeros((1, 128), F32)
        dcm = jnp.zeros((2 * CHUNK_STATES, 128), F32)
        for i in range(SEQ // MM_ROWS):
            rows = pl.ds(i * MM_ROWS, MM_ROWS)
            dy = dy_ref[rows, :]
            g_ref[rows, :] = _dot_nt(dy.astype(BF16), c_ref[...])
            dsum += jnp.sum(dy * u_ref[rows, :], axis=0, keepdims=True)
            dcm += _dot_tn(h_ref[rows, :].astype(BF16), dy.astype(BF16))
        dd_ref[...] = dsum
        dc_ref[...] = dcm
        steps, carry_in, _ = _scan_consts(a_ref, conj=True, reverse=True)
        first_row = lax.broadcasted_iota(jnp.int32, (SCAN_ROWS, CHUNK_STATES), 0) == 0
        n_tiles = SEQ // SCAN_ROWS

        def tile(j, carry):
            k = n_tiles - 1 - j
            rows = pl.ds(pl.multiple_of(k * SCAN_ROWS, SCAN_ROWS), SCAN_ROWS)
            before = pl.ds(pl.multiple_of(jnp.maximum(k - 1, 0) * SCAN_ROWS, SCAN_ROWS), SCAN_ROWS)
            gr, gi = _scan_tile(g_ref[rows, :CHUNK_STATES], g_ref[rows, CHUNK_STATES:], carry[0], carry[1], steps, carry_in, True)
            g_ref[rows, :CHUNK_STATES] = gr
            g_ref[rows, CHUNK_STATES:] = gi
            has_before = jnp.where(k > 0, 1.0, 0.0)
            hr = jnp.where(first_row, pltpu.roll(h_ref[before, :CHUNK_STATES], 1, axis=0) * has_before,
                           pltpu.roll(h_ref[rows, :CHUNK_STATES], 1, axis=0))
            hi = jnp.where(first_row, pltpu.roll(h_ref[before, CHUNK_STATES:], 1, axis=0) * has_before,
                           pltpu.roll(h_ref[rows, CHUNK_STATES:], 1, axis=0))
            return gr, gi, carry[2] + hr * gr + hi * gi, carry[3] + hr * gi - hi * gr

        zero = jnp.zeros((SCAN_ROWS, CHUNK_STATES), F32)
        _, _, sar, sai = lax.fori_loop(0, n_tiles, tile, (zero, zero, zero, zero))
        da_ref[:, :CHUNK_STATES] = jnp.sum(sar, axis=0, keepdims=True)
        da_ref[:, CHUNK_STATES:] = jnp.sum(sai, axis=0, keepdims=True)
        dbm = jnp.zeros((128, 2 * CHUNK_STATES), F32)
        for i in range(SEQ // MM_ROWS):
            rows = pl.ds(i * MM_ROWS, MM_ROWS)
            g = g_ref[rows, :].astype(BF16)
            du_ref[rows, :] = (_dot_nt(g, b_ref[...]) + d_ref[...] * dy_ref[rows, :]).astype(BF16)
            dbm += _dot_tn(u_ref[rows, :].astype(BF16), g)
        db_ref[...] = dbm

    chunk_col = pl.BlockSpec((SEQ, 128), lambda c: (0, c))
    return _pallas_call(
        body, name="ssm_bwd", grid=(SSM_CHUNKS,),
        in_specs=[chunk_col,
                  pl.BlockSpec((SEQ, 128), lambda c: (0, U_COL + c)),
                  pl.BlockSpec((SEQ, 2 * CHUNK_STATES), lambda c: (0, c)),
                  pl.BlockSpec((None, 128, 2 * CHUNK_STATES), lambda c: (c, 0, 0)),
                  pl.BlockSpec((None, 2 * CHUNK_STATES, 128), lambda c: (c, 0, 0)),
                  pl.BlockSpec((None, 1, 2 * CHUNK_STATES), lambda c: (c, 0, 0)),
                  pl.BlockSpec((1, 128), lambda c: (0, c)), ANY],
        out_specs=(pl.BlockSpec((SEQ, 128), lambda c: (0, U_COL + c)),
                   pl.BlockSpec((None, 128, 2 * CHUNK_STATES), lambda c: (c, 0, 0)),
                   pl.BlockSpec((None, 2 * CHUNK_STATES, 128), lambda c: (c, 0, 0)),
                   pl.BlockSpec((None, 1, 2 * CHUNK_STATES), lambda c: (c, 0, 0)),
                   pl.BlockSpec((1, 128), lambda c: (0, c))),
        input_output_aliases={7: 0},
        out_shape=(jax.ShapeDtypeStruct((SEQ, IN_WIDTH), BF16),
                   jax.ShapeDtypeStruct((SSM_CHUNKS, 128, 2 * CHUNK_STATES), F32),
                   jax.ShapeDtypeStruct((SSM_CHUNKS, 2 * CHUNK_STATES, 128), F32),
                   jax.ShapeDtypeStruct((SSM_CHUNKS, 1, 2 * CHUNK_STATES), F32),
                   jax.ShapeDtypeStruct((1, SSM_WIDTH), F32)),
        scratch_shapes=[pltpu.VMEM((SEQ, 2 * CHUNK_STATES), F32)],
        compiler_params=_cparams(dimension_semantics=("arbitrary",)),
    )(dys, proj, h, bmat, cmat, a_chunks, d_skip, dproj)


def _ssm_tables(abar_re, abar_im, bbar_re_t, bbar_im_t, c_re, c_im):
    bmat = _block_diag(bbar_re_t, bbar_im_t, 1.0, True).astype(BF16)
    cmat = _block_diag(c_re, c_im, -1.0, False).astype(BF16)
    a_chunks = jnp.concatenate([abar_re.reshape(SSM_CHUNKS, 1, CHUNK_STATES), abar_im.reshape(SSM_CHUNKS, 1, CHUNK_STATES)], axis=2)
    return bmat, cmat, a_chunks


GL_COL = (3 * QKV_WIDTH + SSM_WIDTH) // D_MODEL
GELU_C = math.sqrt(2.0 / math.pi)
GELU_A = 0.044715


def _sds(shape, dtype):
    return jax.ShapeDtypeStruct(shape, dtype)


def _gelu(x):
    t = jnp.tanh(GELU_C * (x + GELU_A * x * x * x))
    return 0.5 * x * (1.0 + t), t


def _gelu_grad(x, t):
    return 0.5 * (1.0 + t) + 0.5 * x * (1.0 - t * t) * GELU_C * (1.0 + 3.0 * GELU_A * x * x)


def _layer_norm(r, g, b):
    mu = jnp.mean(r, axis=-1, keepdims=True)
    xc = r - mu
    rstd = lax.rsqrt(jnp.mean(xc * xc, axis=-1, keepdims=True) + LN_EPS)
    xhat = xc * rstd
    return xhat * g + b, xhat, rstd


def _layer_norm_bwd(dy, xhat, rstd, g):
    dxhat = dy * g
    m1 = jnp.mean(dxhat, axis=-1, keepdims=True)
    m2 = jnp.mean(dxhat * xhat, axis=-1, keepdims=True)
    return rstd * (dxhat - m1 - xhat * m2)


def _proj(x, w_in):
    tm, tn = 512, 1792

    def body(x_ref, w_ref, o_ref):
        o_ref[...] = _dot(x_ref[...].astype(BF16), w_ref[...])

    return _pallas_call(
        body, name="proj", grid=(SEQ // tm, IN_WIDTH // tn),
        in_specs=[pl.BlockSpec((tm, D_MODEL), lambda i, j: (i, 0)), pl.BlockSpec((D_MODEL, tn), lambda i, j: (0, j))],
        out_specs=pl.BlockSpec((tm, tn), lambda i, j: (i, j)), out_shape=_sds((SEQ, IN_WIDTH), F32),
        compiler_params=_cparams(dimension_semantics=("arbitrary", "arbitrary")),
    )(x, w_in)


def _row_spec(tm, width, col=0):
    return pl.BlockSpec((tm, width), lambda i, col=col: (i, col))


def _full_spec(shape):
    return pl.BlockSpec(shape, lambda i: (0,) * len(shape))


def _mixer_out(attn, ys, proj, x, w_ab, w_sb, w_glu, w_out, b_gate, ln_g, ln_b):
    tm = 256

    def body(attn_ref, ys_ref, gl0_ref, gl1_ref, x_ref, wab_ref, wsb_ref, wglu_ref, wout_ref, bg_ref, g_ref, b_ref,
             h_ref, xhat_ref, rstd_ref, glu_ref, ya_ref, yssm_ref):
        gy, _ = _gelu(ys_ref[...])
        glu = _dot(gy.astype(BF16), wglu_ref[...])
        glu_ref[...] = glu
        y_s = glu[:, :SSM_WIDTH] * jax.nn.sigmoid(glu[:, SSM_WIDTH:])
        y_ssm = _dot(y_s.astype(BF16), wsb_ref[...])
        y_attn = _dot(attn_ref[...].astype(BF16), wab_ref[...])
        ya_ref[...] = y_attn
        yssm_ref[...] = y_ssm
        g0 = jax.nn.sigmoid(gl0_ref[...] + bg_ref[0:1, :])
        g1 = jax.nn.sigmoid(gl1_ref[...] + bg_ref[1:2, :])
        mixed = g0 * y_attn + g1 * y_ssm
        r1 = DN_ALPHA * x_ref[...] + _dot(mixed.astype(BF16), wout_ref[...])
        h, xhat, rstd = _layer_norm(r1, g_ref[...], b_ref[...])
        h_ref[...] = h
        xhat_ref[...] = xhat
        rstd_ref[...] = jnp.broadcast_to(rstd, (tm, 128))

    wide = _sds((SEQ, D_MODEL), F32)
    return _pallas_call(
        body, name="mixer_out", grid=(SEQ // tm,),
        in_specs=[_row_spec(tm, ATTN_WIDTH), _row_spec(tm, SSM_WIDTH), _row_spec(tm, D_MODEL, GL_COL), _row_spec(tm, D_MODEL, GL_COL + 1),
                  _row_spec(tm, D_MODEL), _full_spec((ATTN_WIDTH, D_MODEL)), _full_spec((SSM_WIDTH, D_MODEL)),
                  _full_spec((SSM_WIDTH, 2 * SSM_WIDTH)), _full_spec((D_MODEL, D_MODEL)), _full_spec((2, D_MODEL)),
                  _full_spec((1, D_MODEL)), _full_spec((1, D_MODEL))],
        out_specs=(_row_spec(tm, D_MODEL), _row_spec(tm, D_MODEL), _row_spec(tm, 128), _row_spec(tm, D_MODEL),
                   _row_spec(tm, D_MODEL), _row_spec(tm, D_MODEL)),
        out_shape=(wide, wide, _sds((SEQ, 128), F32), wide, wide, wide),
        compiler_params=_cparams(dimension_semantics=("arbitrary",)),
    )(attn, ys, proj, proj, x, w_ab, w_sb, w_glu, w_out, b_gate, ln_g, ln_b)


def _ff_up(h, w_gate, w_up):
    tm, tn = 512, 768

    def body(h_ref, wg_ref, wu_ref, a_ref, b_ref, f_ref):
        hb = h_ref[...].astype(BF16)
        a, b = _dot(hb, wg_ref[...]), _dot(hb, wu_ref[...])
        a_ref[...] = a.astype(BF16)
        b_ref[...] = b.astype(BF16)
        f_ref[...] = (a * jax.nn.sigmoid(a) * b).astype(BF16)

    tile = pl.BlockSpec((tm, tn), lambda i, j: (i, j))
    wtile = pl.BlockSpec((D_MODEL, tn), lambda i, j: (0, j))
    out = _sds((SEQ, D_FF_PAD), BF16)
    return _pallas_call(
        body, name="ff_up", grid=(SEQ // tm, D_FF_PAD // tn),
        in_specs=[pl.BlockSpec((tm, D_MODEL), lambda i, j: (i, 0)), wtile, wtile],
        out_specs=(tile, tile, tile), out_shape=(out, out, out),
        compiler_params=_cparams(dimension_semantics=("arbitrary", "arbitrary")),
    )(h, w_gate, w_up)


def _ff_down_loss(f, w_down, h, target, ln_g, ln_b):
    tm = 256

    def body(f_ref, w_ref, h_ref, t_ref, g_ref, b_ref, dr_ref, dg_ref, db_ref, loss_ref):
        @pl.when(pl.program_id(0) == 0)
        def _():
            dg_ref[...] = jnp.zeros_like(dg_ref)
            db_ref[...] = jnp.zeros_like(db_ref)
            loss_ref[...] = jnp.zeros_like(loss_ref)

        r2 = DN_ALPHA * h_ref[...] + _dot(f_ref[...], w_ref[...])
        g = g_ref[...]
        out, xhat, rstd = _layer_norm(r2, g, b_ref[...])
        err = out - t_ref[...]
        loss_ref[...] += 0.5 * jnp.sum(jnp.mean(err * err, axis=-1, keepdims=True), axis=0, keepdims=True)
        dout = err * (1.0 / D_MODEL)
        dg_ref[...] += jnp.sum(dout * xhat, axis=0, keepdims=True)
        db_ref[...] += jnp.sum(dout, axis=0, keepdims=True)
        dr_ref[...] = _layer_norm_bwd(dout, xhat, rstd, g)

    vec = _sds((1, D_MODEL), F32)
    return _pallas_call(
        body, name="ff_down_loss", grid=(SEQ // tm,),
        in_specs=[_row_spec(tm, D_FF_PAD), _full_spec((D_FF_PAD, D_MODEL)), _row_spec(tm, D_MODEL), _row_spec(tm, D_MODEL),
                  _full_spec((1, D_MODEL)), _full_spec((1, D_MODEL))],
        out_specs=(_row_spec(tm, D_MODEL), _full_spec((1, D_MODEL)), _full_spec((1, D_MODEL)), _full_spec((1, 128))),
        out_shape=(_sds((SEQ, D_MODEL), F32), vec, vec, _sds((1, 128), F32)),
        compiler_params=_cparams(dimension_semantics=("arbitrary",)),
    )(f, w_down, h, target, ln_g, ln_b)


def _ff_down_bwd(dr2, w_down, a, b):
    tm, tn = 512, 768

    def body(dr_ref, w_ref, a_ref, b_ref, da_ref, db_ref):
        df = _dot_nt(dr_ref[...].astype(BF16), w_ref[...])
        av, bv = a_ref[...].astype(F32), b_ref[...].astype(F32)
        sg = jax.nn.sigmoid(av)
        da_ref[...] = (df * bv * sg * (1.0 + av * (1.0 - sg))).astype(BF16)
        db_ref[...] = (df * av * sg).astype(BF16)

    tile = pl.BlockSpec((tm, tn), lambda i, j: (i, j))
    out = _sds((SEQ, D_FF_PAD), BF16)
    return _pallas_call(
        body, name="ff_down_bwd", grid=(SEQ // tm, D_FF_PAD // tn),
        in_specs=[pl.BlockSpec((tm, D_MODEL), lambda i, j: (i, 0)), pl.BlockSpec((tn, D_MODEL), lambda i, j: (j, 0)), tile, tile],
        out_specs=(tile, tile), out_shape=(out, out),
        compiler_params=_cparams(dimension_semantics=("arbitrary", "arbitrary")),
    )(dr2, w_down, a, b)


def _ff_up_bwd(da, db, w_gate, w_up, dr2, xhat1, rstd1, ln_g):
    tm, tk = 512, 768
    nk = D_FF_PAD // tk

    def body(da_ref, db_ref, wg_ref, wu_ref, dr2_ref, xhat_ref, rstd_ref, g_ref, dr1_ref, dg_ref, dbias_ref, acc):
        i, k = pl.program_id(0), pl.program_id(1)

        @pl.when(jnp.logical_and(i == 0, k == 0))
        def _():
            dg_ref[...] = jnp.zeros_like(dg_ref)
            dbias_ref[...] = jnp.zeros_like(dbias_ref)

        part = _dot_nt(da_ref[...], wg_ref[...]) + _dot_nt(db_ref[...], wu_ref[...])

        @pl.when(k == 0)
        def _():
            acc[...] = part

        @pl.when(k > 0)
        def _():
            acc[...] += part

        @pl.when(k == nk - 1)
        def _():
            dh = DN_ALPHA * dr2_ref[...] + acc[...]
            xhat = xhat_ref[...]
            dg_ref[...] += jnp.sum(dh * xhat, axis=0, keepdims=True)
            dbias_ref[...] += jnp.sum(dh, axis=0, keepdims=True)
            rstd = jnp.max(rstd_ref[...], axis=1, keepdims=True)
            dr1_ref[...] = _layer_norm_bwd(dh, xhat, rstd, g_ref[...])

    hid = pl.BlockSpec((tm, tk), lambda i, k: (i, k))
    wtile = pl.BlockSpec((D_MODEL, tk), lambda i, k: (0, k))
    row = pl.BlockSpec((tm, D_MODEL), lambda i, k: (i, 0))
    vec = pl.BlockSpec((1, D_MODEL), lambda i, k: (0, 0))
    return _pallas_call(
        body, name="ff_up_bwd", grid=(SEQ // tm, nk),
        in_specs=[hid, hid, wtile, wtile, row, row, pl.BlockSpec((tm, 128), lambda i, k: (i, 0)), vec],
        out_specs=(row, vec, vec), out_shape=(_sds((SEQ, D_MODEL), F32), _sds((1, D_MODEL), F32), _sds((1, D_MODEL), F32)),
        scratch_shapes=[pltpu.VMEM((tm, D_MODEL), F32)],
        compiler_params=_cparams(dimension_semantics=("arbitrary", "arbitrary")),
    )(da, db, w_gate, w_up, dr2, xhat1, rstd1, ln_g)


def _mixer_bwd(dr1, proj, y_attn, y_ssm, glu, ys, w_ab, w_sb, w_glu, w_out, b_gate):
    tm = 256

    def body(dr1_ref, gl0_ref, gl1_ref, ya_ref, yssm_ref, glu_ref, ys_ref, wab_ref, wsb_ref, wglu_ref, wout_ref, bg_ref,
             dya_ref, dyssm_ref, dgl_ref, dattn_ref, dglu_ref, dys_ref, mixed_ref, ysb_ref, gy_ref, dbg_ref):
        @pl.when(pl.program_id(0) == 0)
        def _():
            dbg_ref[...] = jnp.zeros_like(dbg_ref)

        dmixed = _dot_nt(dr1_ref[...].astype(BF16), wout_ref[...])
        g0 = jax.nn.sigmoid(gl0_ref[...] + bg_ref[0:1, :])
        g1 = jax.nn.sigmoid(gl1_ref[...] + bg_ref[1:2, :])
        y_attn, y_ssm = ya_ref[...], yssm_ref[...]
        mixed_ref[...] = (g0 * y_attn + g1 * y_ssm).astype(BF16)
        dya = (dmixed * g0).astype(BF16)
        dyssm = (dmixed * g1).astype(BF16)
        dya_ref[...] = dya
        dyssm_ref[...] = dyssm
        dgl0 = dmixed * y_attn * g0 * (1.0 - g0)
        dgl1 = dmixed * y_ssm * g1 * (1.0 - g1)
        dgl_ref[:, :GL_COL * D_MODEL] = jnp.zeros((tm, GL_COL * D_MODEL), BF16)
        dgl_ref[:, GL_COL * D_MODEL:(GL_COL + 1) * D_MODEL] = dgl0.astype(BF16)
        dgl_ref[:, (GL_COL + 1) * D_MODEL:] = dgl1.astype(BF16)
        dbg_ref[:, :D_MODEL] += jnp.sum(dgl0, axis=0, keepdims=True)
        dbg_ref[:, D_MODEL:] += jnp.sum(dgl1, axis=0, keepdims=True)
        dattn_ref[...] = _dot_nt(dya, wab_ref[...])
        dy_s = _dot_nt(dyssm, wsb_ref[...])
        glu = glu_ref[...]
        glu1, sg = glu[:, :SSM_WIDTH], jax.nn.sigmoid(glu[:, SSM_WIDTH:])
        ysb_ref[...] = (glu1 * sg).astype(BF16)
        dglu1 = (dy_s * sg).astype(BF16)
        dglu2 = (dy_s * glu1 * sg * (1.0 - sg)).astype(BF16)
        dglu_ref[:, :SSM_WIDTH] = dglu1
        dglu_ref[:, SSM_WIDTH:] = dglu2
        dgy = _dot_nt(dglu1, wglu_ref[:, :SSM_WIDTH]) + _dot_nt(dglu2, wglu_ref[:, SSM_WIDTH:])
        ys = ys_ref[...]
        gy, t = _gelu(ys)
        gy_ref[...] = gy.astype(BF16)
        dys_ref[...] = dgy * _gelu_grad(ys, t)

    wide_b, half_b = _sds((SEQ, D_MODEL), BF16), _sds((SEQ, SSM_WIDTH), BF16)
    half_f = _sds((SEQ, SSM_WIDTH), F32)
    return _pallas_call(
        body, name="mixer_bwd", grid=(SEQ // tm,),
        in_specs=[_row_spec(tm, D_MODEL), _row_spec(tm, D_MODEL, GL_COL), _row_spec(tm, D_MODEL, GL_COL + 1), _row_spec(tm, D_MODEL),
                  _row_spec(tm, D_MODEL), _row_spec(tm, D_MODEL), _row_spec(tm, SSM_WIDTH), _full_spec((ATTN_WIDTH, D_MODEL)),
                  _full_spec((SSM_WIDTH, D_MODEL)), _full_spec((SSM_WIDTH, 2 * SSM_WIDTH)), _full_spec((D_MODEL, D_MODEL)),
                  _full_spec((2, D_MODEL))],
        out_specs=(_row_spec(tm, D_MODEL), _row_spec(tm, D_MODEL), _row_spec(tm, IN_WIDTH), _row_spec(tm, ATTN_WIDTH),
                   _row_spec(tm, D_MODEL), _row_spec(tm, SSM_WIDTH), _row_spec(tm, D_MODEL), _row_spec(tm, SSM_WIDTH),
                   _row_spec(tm, SSM_WIDTH), _full_spec((1, 2 * D_MODEL))),
        out_shape=(wide_b, wide_b, _sds((SEQ, IN_WIDTH), BF16), half_f, wide_b, half_f, wide_b, half_b, half_b,
                   _sds((1, 2 * D_MODEL), F32)),
        compiler_params=_cparams(dimension_semantics=("arbitrary",)),
    )(dr1, proj, proj, y_attn, y_ssm, glu, ys, w_ab, w_sb, w_glu, w_out, b_gate)


def _grad_x(dproj, w_in, dr1):
    tm, tk = 512, 1792
    nk = IN_WIDTH // tk

    def body(dp_ref, w_ref, dr1_ref, o_ref, acc):
        k = pl.program_id(1)
        part = _dot_nt(dp_ref[...], w_ref[...])

        @pl.when(k == 0)
        def _():
            acc[...] = part

        @pl.when(k > 0)
        def _():
            acc[...] += part

        @pl.when(k == nk - 1)
        def _():
            o_ref[...] = DN_ALPHA * dr1_ref[...] + acc[...]

    row = pl.BlockSpec((tm, D_MODEL), lambda i, k: (i, 0))
    return _pallas_call(
        body, name="grad_x", grid=(SEQ // tm, nk),
        in_specs=[pl.BlockSpec((tm, tk), lambda i, k: (i, k)), pl.BlockSpec((D_MODEL, tk), lambda i, k: (0, k)), row],
        out_specs=row, out_shape=_sds((SEQ, D_MODEL), F32), scratch_shapes=[pltpu.VMEM((tm, D_MODEL), F32)],
        compiler_params=_cparams(dimension_semantics=("arbitrary", "arbitrary")),
    )(dproj, w_in, dr1)


def _weight_grad(a, b, name, shard_cols=None):
    k, n = a.shape[1], b.shape[1]
    tm = 512
    nm = SEQ // tm
    tk = min(k, 512) if shard_cols else k // N_DEV
    tn = n // 4 if shard_cols else min(n, 1024)

    def body(a_ref, b_ref, o_ref, acc):
        m = pl.program_id(2)
        part = _dot_tn(a_ref[...].astype(BF16), b_ref[...].astype(BF16))

        @pl.when(m == 0)
        def _():
            acc[...] = part

        @pl.when(m > 0)
        def _():
            acc[...] += part

        @pl.when(m == nm - 1)
        def _():
            if shard_cols:
                o_ref[0] = acc[:, :shard_cols].astype(BF16)
                o_ref[1] = acc[:, shard_cols:].astype(BF16)
            else:
                o_ref[...] = acc[...].astype(BF16)

    if shard_cols:
        out_spec = pl.BlockSpec((2, None, tk, shard_cols), lambda kk, j, m: (0, j, kk, 0))
        out_shape = _sds((2, 4, k, shard_cols), BF16)
    else:
        out_spec = pl.BlockSpec((None, None, tk, tn), lambda kk, j, m: (kk % 2, kk // 2, 0, j))
        out_shape = _sds((2, 4, tk, n), BF16)
    return _pallas_call(
        body, name=name, grid=(k // tk, n // tn, nm),
        in_specs=[pl.BlockSpec((tm, tk), lambda kk, j, m: (m, kk)), pl.BlockSpec((tm, tn), lambda kk, j, m: (m, j))],
        out_specs=out_spec, out_shape=out_shape, scratch_shapes=[pltpu.VMEM((tk, tn), F32)],
        compiler_params=_cparams(dimension_semantics=("arbitrary", "arbitrary", "arbitrary")),
    )(a, b)


MESH = pl.DeviceIdType.MESH
ANY = pl.BlockSpec(memory_space=pl.ANY)


def _place():
    return lax.axis_index("x"), lax.axis_index("y"), lax.axis_index("c")


def _other_chips(x, y):
    return [(1 - x, y), (x, 1 - y), (1 - x, 1 - y)]


def _send_buffer(w, rows, cols, name):
    r, c = w.shape

    def body(w_ref, o_ref):
        if (r, c) != (rows, cols):
            o_ref[...] = jnp.zeros((rows, cols), BF16)
        o_ref[:r, :c] = w_ref[...].astype(BF16)

    return _pallas_call(body, name=name, out_shape=_sds((rows, cols), BF16))(w)


def _all_gather(shards, name, n_wide=0):
    n = len(shards)

    def body(*refs):
        ins, outs, wide = refs[:n], refs[n:2 * n], refs[2 * n:2 * n + n_wide]
        send_sems, recv_sems, local_sems, wide_sems = refs[2 * n + n_wide:]
        x, y, c = _place()
        me, sibling = (x, y, c), (x, y, 1 - c)
        chips = _other_chips(x, y)
        widened = []

        def index(px, py, pc):
            return 4 * px + 2 * py + pc

        def slot(a, px, py, pc):
            return outs[a].at[index(px, py, pc)]

        def copy(a, k, block, to, src=None):
            return pltpu.make_async_remote_copy(
                src_ref=slot(a, *block) if src is None else src, dst_ref=slot(a, *block),
                send_sem=send_sems.at[a, k], recv_sem=recv_sems.at[a, k], device_id=to, device_id_type=MESH)

        def widen(a, block, src=None):
            if a < n_wide:
                cols = shards[a].shape[1]
                window = wide[a].at[:, pl.ds(pl.multiple_of(index(*block) * cols, 128), cols)]
                cp = pltpu.make_async_copy(slot(a, *block) if src is None else src, window, wide_sems.at[a, index(*block)])
                cp.start()
                widened.append(cp)

        mine = [pltpu.make_async_copy(ins[a], slot(a, *me), local_sems.at[a]) for a in range(n)]
        for cp in mine:
            cp.start()
        first = []
        for a in range(n):
            first.append(copy(a, 0, me, sibling, src=ins[a]))
            first += [copy(a, 1 + j, me, (*chip, c), src=ins[a]) for j, chip in enumerate(chips)]
        for cp in first:
            cp.start()
        for a in range(n_wide):
            widen(a, me, src=ins[a])
        passed = []
        for j, chip in enumerate(chips):
            for a in range(n):
                copy(a, 1 + j, (*chip, c), me).wait_recv()
                onward = copy(a, 4 + j, (*chip, c), sibling)
                onward.start()
                passed.append(onward)
                widen(a, (*chip, c))
        for a in range(n):
            copy(a, 0, sibling, me).wait_recv()
            widen(a, sibling)
            for j, chip in enumerate(chips):
                copy(a, 4 + j, (*chip, 1 - c), me).wait_recv()
                widen(a, (*chip, 1 - c))
        for cp in first + passed:
            cp.wait_send()
        for cp in mine + widened:
            cp.wait()

    return _pallas_call(
        body, name=name, in_specs=[ANY] * n, out_specs=[ANY] * (n + n_wide),
        out_shape=[_sds((N_DEV,) + s.shape, s.dtype) for s in shards]
        + [_sds((s.shape[0], N_DEV * s.shape[1]), s.dtype) for s in shards[:n_wide]],
        scratch_shapes=[pltpu.SemaphoreType.DMA((n, 7)), pltpu.SemaphoreType.DMA((n, 7)), pltpu.SemaphoreType.DMA((n,)),
                        pltpu.SemaphoreType.DMA((max(n_wide, 1), N_DEV))],
    )(*shards)


def _swap_with_sibling(grads, name):
    n = len(grads)

    def body(*refs):
        ins, outs = refs[:n], refs[n:2 * n]
        send_sems, recv_sems = refs[2 * n:]
        x, y, c = _place()
        copies = [pltpu.make_async_remote_copy(
            src_ref=ins[a].at[1 - c], dst_ref=outs[a], send_sem=send_sems.at[a], recv_sem=recv_sems.at[a],
            device_id=(x, y, 1 - c), device_id_type=MESH) for a in range(n)]
        for cp in copies:
            cp.start()
        for cp in copies:
            cp.wait()

    return _pallas_call(
        body, name=name, in_specs=[ANY] * n, out_specs=[ANY] * n,
        out_shape=[_sds(g.shape[1:], g.dtype) for g in grads],
        scratch_shapes=[pltpu.SemaphoreType.DMA((n,)), pltpu.SemaphoreType.DMA((n,))],
    )(*grads)


def _pair_sum(g, r, core, name):
    _, _, k, n = g.shape

    def body(core_ref, g_ref, r_ref, o_ref):
        o_ref[...] = (g_ref[...].astype(F32) + r_ref[...].astype(F32)).astype(o_ref.dtype)

    return _pallas_call(
        body, name=name,
        grid_spec=pltpu.PrefetchScalarGridSpec(
            num_scalar_prefetch=1, grid=(4,),
            in_specs=[pl.BlockSpec((None, None, k, n), lambda p, core_ref: (core_ref[0], p, 0, 0)),
                      pl.BlockSpec((None, k, n), lambda p, core_ref: (p, 0, 0))],
            out_specs=pl.BlockSpec((None, k, n), lambda p, core_ref: (p, 0, 0))),
        out_shape=_sds((4, k, n), g.dtype), compiler_params=_cparams(dimension_semantics=("arbitrary",)),
    )(core, g, r)


def _swap_between_chips(sums, name):
    n = len(sums)

    def body(*refs):
        ins, outs = refs[:n], refs[n:2 * n]
        send_sems, recv_sems, local_sems = refs[2 * n:]
        x, y, c = _place()
        mine = 2 * x + y
        copies = []
        for a in range(n):
            copies.append(pltpu.make_async_copy(ins[a].at[mine], outs[a].at[mine], local_sems.at[a]))
            for j, (px, py) in enumerate(_other_chips(x, y)):
                copies.append(pltpu.make_async_remote_copy(
                    src_ref=ins[a].at[2 * px + py], dst_ref=outs[a].at[mine], send_sem=send_sems.at[a, j],
                    recv_sem=recv_sems.at[a, j], device_id=(px, py, c), device_id_type=MESH))
        for cp in copies:
            cp.start()
        for cp in copies:
            cp.wait()

    return _pallas_call(
        body, name=name, in_specs=[ANY] * n, out_specs=[ANY] * n, out_shape=[_sds(s.shape, s.dtype) for s in sums],
        scratch_shapes=[pltpu.SemaphoreType.DMA((n, 3)), pltpu.SemaphoreType.DMA((n, 3)), pltpu.SemaphoreType.DMA((n,))],
    )(*sums)


def _adamw_math(w, g, m, v):
    m = ADAM_B1 * m + (1.0 - ADAM_B1) * g
    v = ADAM_B2 * v + (1.0 - ADAM_B2) * (g * g)
    m_hat = m / (1.0 - ADAM_B1 ** ADAM_STEP)
    v_hat = v / (1.0 - ADAM_B2 ** ADAM_STEP)
    return -ADAM_LR * (m_hat / (jnp.sqrt(v_hat) + ADAM_EPS) + ADAM_WD * w), m, v


def _adamw(w, m, v, parts, name):
    r, c = w.shape
    tr = r if r <= 512 else 256
    n_parts, pr, pc = parts.shape
    assert r % tr == 0 and (tr == r or pr == r)

    def body(w_ref, m_ref, v_ref, p_ref, g_out, d_out, m_out, v_out):
        g = p_ref[0, :tr, :c].astype(F32)
        for p in range(1, n_parts):
            g = g + p_ref[p, :tr, :c].astype(F32)
        g_out[...] = g
        d_out[...], m_out[...], v_out[...] = _adamw_math(w_ref[...], g, m_ref[...], v_ref[...])

    tile = pl.BlockSpec((tr, c), lambda i: (i, 0))
    part_tile = pl.BlockSpec((n_parts, pr if tr == r else tr, pc), lambda i: (0, i, 0))
    out = _sds((r, c), F32)
    return _pallas_call(
        body, name=name, grid=(r // tr,), in_specs=[tile, tile, tile, part_tile], out_specs=(tile,) * 4,
        out_shape=(out,) * 4, compiler_params=_cparams(dimension_semantics=("arbitrary",)),
    )(w, m, v, parts)


SMALL = ("ssm_a_re", "ssm_a_im", "ssm_log_dt", "ssm_b_re", "ssm_b_im", "ssm_c_re", "ssm_c_im", "ssm_d",
         "ln1_g", "ln1_b", "ln2_g", "ln2_b")


def _adamw_replicated(ws, ms, vs, parts, loss_parts):
    n = len(ws)

    def body(*refs):
        w_refs, m_refs, v_refs, p_refs = (refs[i * n:(i + 1) * n] for i in range(4))
        loss_ref = refs[4 * n]
        outs = refs[4 * n + 1:]
        g_out, d_out, m_out, v_out = (outs[i * n:(i + 1) * n] for i in range(4))
        for i in range(n):
            g = p_refs[i][0]
            for dev in range(1, N_DEV):
                g = g + p_refs[i][dev]
            g_out[i][...] = g
            d_out[i][...], m_out[i][...], v_out[i][...] = _adamw_math(w_refs[i][...], g, m_refs[i][...], v_refs[i][...])
        loss = loss_ref[0]
        for dev in range(1, N_DEV):
            loss = loss + loss_ref[dev]
        outs[4 * n][...] = loss

    shapes = [_sds(w.shape, F32) for w in ws]
    out = _pallas_call(body, name="adamw_replicated", out_shape=shapes * 4 + [_sds((1, 128), F32)],
                       compiler_params=_cparams())(*ws, *ms, *vs, *parts, loss_parts)
    return out[:n], out[n:2 * n], out[2 * n:3 * n], out[3 * n:4 * n], out[4 * n]


def kernel(x, w_in, b_gate, w_attn_br, w_ssm_br, w_out, ssm_a_re, ssm_a_im, ssm_log_dt, ssm_b_re, ssm_b_im, ssm_c_re, ssm_c_im, ssm_d, w_glu, ln1_g, ln1_b, w_ff_gate, w_ff_up, w_ff_down, ln2_g, ln2_b, loss_target, m_w_in, m_b_gate, m_w_attn_br, m_w_ssm_br, m_w_out, m_ssm_a_re, m_ssm_a_im, m_ssm_log_dt, m_ssm_b_re, m_ssm_b_im, m_ssm_c_re, m_ssm_c_im, m_ssm_d, m_w_glu, m_ln1_g, m_ln1_b, m_w_ff_gate, m_w_ff_up, m_w_ff_down, m_ln2_g, m_ln2_b, v_w_in, v_b_gate, v_w_attn_br, v_w_ssm_br, v_w_out, v_ssm_a_re, v_ssm_a_im, v_ssm_log_dt, v_ssm_b_re, v_ssm_b_im, v_ssm_c_re, v_ssm_c_im, v_ssm_d, v_w_glu, v_ln1_g, v_ln1_b, v_w_ff_gate, v_w_ff_up, v_w_ff_down, v_ln2_g, v_ln2_b):
    given = dict(locals())
    x2, target = x[0], loss_target[0]
    core = lax.axis_index("c").astype(jnp.int32).reshape(1)

    by_cols = ("w_in", "w_attn_br", "w_ssm_br", "w_glu", "w_ff_gate", "w_ff_up", "b_gate")
    sharded = by_cols + ("w_out", "w_ff_down")
    send_shape = dict(w_in=(D_MODEL, 896), w_attn_br=(ATTN_WIDTH, 128), w_ssm_br=(SSM_WIDTH, 128), w_glu=(SSM_WIDTH, 128),
                      w_out=(128, D_MODEL), w_ff_gate=(D_MODEL, FF_PAD), w_ff_up=(D_MODEL, FF_PAD), w_ff_down=(FF_PAD, D_MODEL))
    local = {k: given[k][0] for k in sharded}
    sends = [local[k] if k == "b_gate" else _send_buffer(local[k], *send_shape[k], name="send_" + k) for k in sharded]
    gathered = _all_gather(sends, "gather_weights", n_wide=len(by_cols))
    wt = dict(zip(by_cols, gathered[len(sharded):]))
    wt["w_out"] = gathered[sharded.index("w_out")].reshape(D_MODEL, D_MODEL)
    wt["w_ff_down"] = gathered[sharded.index("w_ff_down")].reshape(D_FF_PAD, D_MODEL)
    b_gate_full = wt["b_gate"]

    a_re, a_im, log_dt = ssm_a_re[0], ssm_a_im[0], ssm_log_dt[0].reshape(SSM_GROUPS, 1)
    b_re_t, b_im_t = ssm_b_re[0].transpose(0, 2, 1), ssm_b_im[0].transpose(0, 2, 1)
    abar_re, abar_im, e_re, e_im, bbar_re_t, bbar_im_t = _ssm_prep(a_re, a_im, log_dt, b_re_t, b_im_t)
    bmat, cmat, a_chunks = _ssm_tables(abar_re, abar_im, bbar_re_t, bbar_im_t, ssm_c_re[0], ssm_c_im[0])
    cos_t, sin_t = _rope_tables()

    proj = _proj(x2, wt["w_in"])
    attn, lse = _attn_fwd(proj, cos_t, sin_t)
    ys, states = _ssm_fwd(proj, bmat, cmat, a_chunks, ssm_d)
    h, xhat1, rstd1, glu, y_attn, y_ssm = _mixer_out(attn, ys, proj, x2, wt["w_attn_br"], wt["w_ssm_br"], wt["w_glu"],
                                                      wt["w_out"], b_gate_full, ln1_g, ln1_b)
    ff_a, ff_b, ff_f = _ff_up(h, wt["w_ff_gate"], wt["w_ff_up"])
    dr2, d_ln2_g, d_ln2_b, loss_lanes = _ff_down_loss(ff_f, wt["w_ff_down"], h, target, ln2_g, ln2_b)

    d_a, d_b = _ff_down_bwd(dr2, wt["w_ff_down"], ff_a, ff_b)
    dr1, d_ln1_g, d_ln1_b = _ff_up_bwd(d_a, d_b, wt["w_ff_gate"], wt["w_ff_up"], dr2, xhat1, rstd1, ln1_g)
    d_ya, d_yssm, d_proj, d_attn, d_glu, d_ys, mixed, y_s, gy, d_bg = _mixer_bwd(
        dr1, proj, y_attn, y_ssm, glu, ys, wt["w_attn_br"], wt["w_ssm_br"], wt["w_glu"], wt["w_out"], b_gate_full)
    for g in range(3):
        d_proj = _attn_bwd_group(g, proj, cos_t, sin_t, attn, lse, d_attn, d_proj)
    d_proj, d_bmat, d_cmat, d_abar, d_skip = _ssm_bwd(d_ys, proj, states, bmat, cmat, a_chunks, ssm_d, d_proj)
    grad_x = _grad_x(d_proj, wt["w_in"], dr1)

    contrib = dict(
        w_in=_weight_grad(x2, d_proj, "wgrad_w_in", 896),
        w_attn_br=_weight_grad(attn, d_ya, "wgrad_w_attn_br", 128),
        w_ssm_br=_weight_grad(y_s, d_yssm, "wgrad_w_ssm_br", 128),
        w_glu=_weight_grad(gy, d_glu, "wgrad_w_glu", 128),
        w_out=_weight_grad(mixed, dr1, "wgrad_w_out"),
        w_ff_gate=_weight_grad(h, d_a, "wgrad_w_ff_gate", FF_PAD),
        w_ff_up=_weight_grad(h, d_b, "wgrad_w_ff_up", FF_PAD),
        w_ff_down=_weight_grad(ff_f, dr2, "wgrad_w_ff_down"),
        b_gate=d_bg.reshape(2, 4, 2, 128).transpose(2, 1, 0, 3),
    )
    from_sibling = _swap_with_sibling([contrib[k] for k in sharded], "swap_with_sibling")
    chip_sums = [_pair_sum(contrib[k], r, core, "pair_sum_" + k) for k, r in zip(sharded, from_sibling)]
    parts = dict(zip(sharded, _swap_between_chips(chip_sums, "swap_between_chips")))

    grads, deltas, new_m, new_v = {}, {}, {}, {}
    for k in sharded:
        w2 = local[k]
        out = _adamw(w2, given["m_" + k][0], given["v_" + k][0], parts[k], "adamw_" + k)
        grads[k], deltas[k], new_m[k], new_v[k] = (o.reshape((1,) + w2.shape) for o in out)

    gbb_re_t, gbb_im_t = _block_diag_parts(d_bmat, True)
    gc_re, gc_im = _block_diag_parts(d_cmat, False)
    ga_re = d_abar[:, 0, :CHUNK_STATES].reshape(SSM_GROUPS, SSM_STATE)
    ga_im = d_abar[:, 0, CHUNK_STATES:].reshape(SSM_GROUPS, SSM_STATE)
    g_a_re, g_a_im, g_log_dt, g_b_re_t, g_b_im_t = _ssm_param_bwd(
        a_re, a_im, log_dt, b_re_t, b_im_t, abar_re, abar_im, e_re, e_im, ga_re, ga_im, gbb_re_t, gbb_im_t)
    mine = [g_a_re, g_a_im, g_log_dt, g_b_re_t.transpose(0, 2, 1), g_b_im_t.transpose(0, 2, 1), gc_re, -gc_im,
            d_skip, d_ln1_g, d_ln1_b, d_ln2_g, d_ln2_b]
    mine = [g.reshape(given[k].shape) for k, g in zip(SMALL, mine)]
    *small_parts, loss_parts = _all_gather(mine + [loss_lanes], "gather_small_grads")
    *small, loss_sum = _adamw_replicated([given[k] for k in SMALL], [given["m_" + k] for k in SMALL],
                                         [given["v_" + k] for k in SMALL], small_parts, loss_parts)
    for res, values in zip((grads, deltas, new_m, new_v), small):
        res.update(zip(SMALL, values))
    loss = loss_sum[0, 0]

    order = ("w_in", "b_gate", "w_attn_br", "w_ssm_br", "w_out", "ssm_a_re", "ssm_a_im", "ssm_log_dt", "ssm_b_re", "ssm_b_im",
             "ssm_c_re", "ssm_c_im", "ssm_d", "w_glu", "ln1_g", "ln1_b", "w_ff_gate", "w_ff_up", "w_ff_down", "ln2_g", "ln2_b")
    return (loss, grad_x[None], *[grads[k] for k in order], *[deltas[k] for k in order], *[new_m[k] for k in order],
            *[new_v[k] for k in order])
```

```python
import functools
import math

import jax
import jax.numpy as jnp
from jax import lax
from jax.experimental import pallas as pl
from jax.experimental.pallas import tpu as pltpu

F32 = jnp.float32
BF16 = jnp.bfloat16

N_DEV = 8
SEQ = 2048
D_MODEL = 1024
HEAD_DIM = 64
ATTN_WIDTH = 512
QKV_WIDTH = 1536
SSM_WIDTH = 512
SSM_GROUPS = 32
SSM_GROUP = 16
SSM_STATE = 64
IN_WIDTH = 7168
D_FF = 2816
FF_SHARD = D_FF // N_DEV
FF_PAD = 384
D_FF_PAD = FF_PAD * N_DEV
DN_ALPHA = 2.0 ** 0.25
LN_EPS = 1e-5
NEG_INF = -1e30
ROPE_THETA = 10000.0
BLOCK = 128
GROUPS = ((1, 16), (4, 4), (16, 1))

ADAM_LR = 0.001
ADAM_B1 = 0.9
ADAM_B2 = 0.999
ADAM_EPS = 1e-08
ADAM_WD = 0.01
ADAM_STEP = 10

VMEM_LIMIT = 56 * 1024 * 1024


_pallas_call = pl.pallas_call


def _cparams(**kw):
    return pltpu.CompilerParams(vmem_limit_bytes=VMEM_LIMIT, **kw)


def _dot(a, b):
    return jnp.dot(a, b, preferred_element_type=F32)


def _dot_nt(a, b):
    return lax.dot_general(a, b, (((1,), (1,)), ((), ())), preferred_element_type=F32)


def _side_by_side(w_ref, row=None):
    rows = slice(None) if row is None else pl.ds(row, 1)
    return jnp.concatenate([w_ref[i, rows, :] for i in range(w_ref.shape[0])], axis=1)


def _dot_tn(a, b):
    return lax.dot_general(a, b, (((0,), (0,)), ((), ())), preferred_element_type=F32)


def _rope_tables():
    half = HEAD_DIM // 2
    inv_freq = ROPE_THETA ** (-jnp.arange(half, dtype=F32) / half)
    ang = jnp.arange(SEQ, dtype=F32)[:, None] * inv_freq[None, :]
    cos, sin = jnp.cos(ang), jnp.sin(ang)
    return jnp.tile(cos, (1, 4)), jnp.tile(jnp.concatenate([-sin, sin], axis=1), (1, 2))


def _swap_halves(x):
    lane = lax.broadcasted_iota(jnp.int32, x.shape, 1)
    return jnp.where((lane & 63) < 32, pltpu.roll(x, 96, axis=1), pltpu.roll(x, 32, axis=1))


def _group_rows(d, nb, r, i):
    src = pl.ds(i * BLOCK, BLOCK) if d == 1 else pl.ds(r + i * BLOCK * d, BLOCK, stride=d)
    return src, pl.ds((r * nb + i) * BLOCK, BLOCK)


def _attn_masks():
    a_idx = lax.broadcasted_iota(jnp.int32, (2 * BLOCK, 2 * BLOCK), 0) & (BLOCK - 1)
    c_idx = lax.broadcasted_iota(jnp.int32, (2 * BLOCK, 2 * BLOCK), 1)
    cur_ok = jnp.logical_and(c_idx >= BLOCK, c_idx - BLOCK <= a_idx)
    prev_ok = jnp.logical_and(c_idx < BLOCK, c_idx >= a_idx)
    lane = lax.broadcasted_iota(jnp.int32, (BLOCK, 128), 1)
    return cur_ok, prev_ok, lane < HEAD_DIM


def _stack_heads(t, head0):
    zero = jnp.zeros_like(t)
    return jnp.concatenate([jnp.where(head0, t, zero), jnp.where(head0, zero, t)], axis=0)


def _unstack_heads(t2, head0):
    return jnp.where(head0, t2[:BLOCK], t2[BLOCK:])


def _attn_fwd(proj, cos_t, sin_t):
    def body(q0, q1, q2, k0, k1, k2, v0, v1, v2, cos_ref, sin_ref, attn_ref, lse_ref,
             qs, ks, vs, os_, ms, ls, acc, mnat, lnat):
        cur_ok, prev_ok, head0 = _attn_masks()
        ks[:BLOCK, :] = jnp.zeros((BLOCK, 128), BF16)
        vs[:BLOCK, :] = jnp.zeros((BLOCK, 128), BF16)
        for g, (d, nb) in enumerate(GROUPS):
            q_ref, k_ref, v_ref = (q0, q1, q2)[g], (k0, k1, k2)[g], (v0, v1, v2)[g]
            for r in range(d):
                for i in range(nb):
                    src, dst = _group_rows(d, nb, r, i)
                    below = pl.ds(dst.start + BLOCK, BLOCK)
                    c, s = cos_ref[src, :], sin_ref[src, :]
                    q = q_ref[src, :]
                    k = k_ref[src, :]
                    qs[dst, :] = ((q * c + _swap_halves(q) * s) * 0.125).astype(BF16)
                    ks[below, :] = (k * c + _swap_halves(k) * s).astype(BF16)
                    vs[below, :] = v_ref[src, :].astype(BF16)

            def block(b, carry, nb=nb):
                has_prev = (b & (nb - 1)) > 0
                cur = pl.ds(pl.multiple_of(b * BLOCK, BLOCK), BLOCK)
                window = pl.ds(pl.multiple_of(b * BLOCK, BLOCK), 2 * BLOCK)
                valid = jnp.logical_or(cur_ok, jnp.logical_and(prev_ok, has_prev))
                s = jnp.where(valid, _dot_nt(_stack_heads(qs[cur, :], head0), ks[window, :]), NEG_INF)
                m = jnp.max(s, axis=1, keepdims=True)
                p = jnp.exp(s - m)
                os_[cur, :] = _unstack_heads(_dot(p.astype(BF16), vs[window, :]), head0)
                ms[cur, :] = _unstack_heads(m, head0)
                ls[cur, :] = _unstack_heads(jnp.sum(p, axis=1, keepdims=True), head0)
                return carry

            lax.fori_loop(0, SEQ // BLOCK, block, 0, unroll=2)

            for r in range(d):
                for i in range(nb):
                    src, dst = _group_rows(d, nb, r, i)
                    if g == 0:
                        acc[src, :], mnat[src, :], lnat[src, :] = os_[dst, :], ms[dst, :], ls[dst, :]
                    else:
                        m_old, m_g = mnat[src, :], ms[dst, :]
                        m_new = jnp.maximum(m_old, m_g)
                        a_old, a_g = jnp.exp(m_old - m_new), jnp.exp(m_g - m_new)
                        acc[src, :] = a_old * acc[src, :] + a_g * os_[dst, :]
                        lnat[src, :] = a_old * lnat[src, :] + a_g * ls[dst, :]
                        mnat[src, :] = m_new
        for i in range(SEQ // BLOCK):
            rows = pl.ds(i * BLOCK, BLOCK)
            l = lnat[rows, :]
            attn_ref[rows, :] = acc[rows, :] / l
            lse_ref[rows, :] = mnat[rows, :] + jnp.log(l)

    def col(base):
        return pl.BlockSpec((SEQ, 128), lambda hp, base=base: (0, base + hp))

    in_specs = [col(g * 4) for g in range(3)] + [col(12 + g * 4) for g in range(3)] + [col(24 + g * 4) for g in range(3)]
    table = pl.BlockSpec((SEQ, 128), lambda hp: (0, 0))
    out = pl.BlockSpec((SEQ, 128), lambda hp: (0, hp))
    return _pallas_call(
        body, name="attn_fwd", grid=(4,),
        in_specs=in_specs + [table, table], out_specs=(out, out),
        out_shape=(jax.ShapeDtypeStruct((SEQ, ATTN_WIDTH), F32), jax.ShapeDtypeStruct((SEQ, ATTN_WIDTH), F32)),
        scratch_shapes=[pltpu.VMEM((SEQ, 128), BF16)] + [pltpu.VMEM((SEQ + BLOCK, 128), BF16)] * 2 + [pltpu.VMEM((SEQ, 128), F32)] * 6,
        compiler_params=_cparams(dimension_semantics=("arbitrary",)),
    )(*([proj] * 9), cos_t, sin_t)


def _attn_bwd_group(g, proj, cos_t, sin_t, attn, lse, dattn, dproj):
    d, nb = GROUPS[g]

    def body(q_ref, k_ref, v_ref, cos_ref, sin_ref, attn_ref, lse_ref, dattn_ref, dproj_in, dproj_ref,
             qs, ks, vs, dos, lss, dss, dqs, dks, dvs, stage, outs, sems):
        del dproj_in
        cur_ok, prev_ok, head0 = _attn_masks()
        ks[:BLOCK, :] = jnp.zeros((BLOCK, 128), BF16)
        vs[:BLOCK, :] = jnp.zeros((BLOCK, 128), BF16)
        dks[:BLOCK, :] = jnp.zeros((BLOCK, 128), F32)
        dvs[:BLOCK, :] = jnp.zeros((BLOCK, 128), F32)
        for r in range(d):
            for i in range(nb):
                src, dst = _group_rows(d, nb, r, i)
                below = pl.ds(dst.start + BLOCK, BLOCK)
                c, s = cos_ref[src, :], sin_ref[src, :]
                q = q_ref[src, :]
                k = k_ref[src, :]
                qs[dst, :] = ((q * c + _swap_halves(q) * s) * 0.125).astype(BF16)
                ks[below, :] = (k * c + _swap_halves(k) * s).astype(BF16)
                vs[below, :] = v_ref[src, :].astype(BF16)
                do = dattn_ref[src, :]
                prod = do * attn_ref[src, :]
                d0 = jnp.sum(jnp.where(head0, prod, 0.0), axis=1, keepdims=True)
                d1 = jnp.sum(jnp.where(head0, 0.0, prod), axis=1, keepdims=True)
                dos[dst, :] = do.astype(BF16)
                dss[dst, :] = jnp.where(head0, d0, d1)
                lss[dst, :] = lse_ref[src, :]
                dks[below, :] = jnp.zeros((BLOCK, 128), F32)
                dvs[below, :] = jnp.zeros((BLOCK, 128), F32)

        def per_head_column(t):
            return jnp.concatenate([jnp.max(jnp.where(head0, t, NEG_INF), axis=1, keepdims=True),
                                    jnp.max(jnp.where(head0, NEG_INF, t), axis=1, keepdims=True)], axis=0)

        def block(b, carry):
            has_prev = (b & (nb - 1)) > 0
            cur = pl.ds(pl.multiple_of(b * BLOCK, BLOCK), BLOCK)
            window = pl.ds(pl.multiple_of(b * BLOCK, BLOCK), 2 * BLOCK)
            valid = jnp.logical_or(cur_ok, jnp.logical_and(prev_ok, has_prev))
            q2, do2 = _stack_heads(qs[cur, :], head0), _stack_heads(dos[cur, :], head0)
            kw, vw = ks[window, :], vs[window, :]
            s = jnp.where(valid, _dot_nt(q2, kw), NEG_INF)
            p = jnp.exp(s - per_head_column(lss[cur, :]))
            ds = (p * (_dot_nt(do2, vw) - per_head_column(dss[cur, :]))).astype(BF16)
            dvs[window, :] += _dot_tn(p.astype(BF16), do2)
            dks[window, :] += _dot_tn(ds, q2)
            dqs[cur, :] = _unstack_heads(_dot(ds, kw), head0)
            return carry

        lax.fori_loop(0, SEQ // BLOCK, block, 0, unroll=2)

        hp = pl.program_id(0)
        copies = []
        for kind in range(3):
            for r in range(d):
                for i in range(nb):
                    src, dst = _group_rows(d, nb, r, i)
                    below = pl.ds(dst.start + BLOCK, BLOCK)
                    if kind == 2:
                        stage[src, :] = dvs[below, :]
                    else:
                        c, s = cos_ref[src, :], sin_ref[src, :]
                        t = dqs[dst, :] * 0.125 if kind == 0 else dks[below, :]
                        stage[src, :] = t * c - _swap_halves(t) * s
            for i in range(SEQ // MM_ROWS):
                rows = pl.ds(i * MM_ROWS, MM_ROWS)
                outs[kind, rows, :] = stage[rows, :].astype(BF16)
            column = pl.multiple_of((kind * 12 + g * 4 + hp) * 128, 128)
            copies.append(pltpu.make_async_copy(outs.at[kind], dproj_ref.at[:, pl.ds(column, 128)], sems.at[kind]))
            copies[-1].start()
        for cp in copies:
            cp.wait()

    def col(base):
        return pl.BlockSpec((SEQ, 128), lambda hp, base=base: (0, base + hp))

    table = pl.BlockSpec((SEQ, 128), lambda hp: (0, 0))
    return _pallas_call(
        body, name=f"attn_bwd_g{g}", grid=(4,),
        in_specs=[col(g * 4), col(12 + g * 4), col(24 + g * 4), table, table, col(0), col(0), col(0), ANY],
        out_specs=ANY, out_shape=_sds((SEQ, IN_WIDTH), BF16), input_output_aliases={8: 0},
        scratch_shapes=[pltpu.VMEM((SEQ, 128), BF16)] + [pltpu.VMEM((SEQ + BLOCK, 128), BF16)] * 2 + [pltpu.VMEM((SEQ, 128), BF16)]
        + [pltpu.VMEM((SEQ, 128), F32)] * 3 + [pltpu.VMEM((SEQ + BLOCK, 128), F32)] * 2 + [pltpu.VMEM((SEQ, 128), F32)]
        + [pltpu.VMEM((3, SEQ, 128), BF16), pltpu.SemaphoreType.DMA((3,))],
        compiler_params=_cparams(dimension_semantics=("arbitrary",)),
    )(proj, proj, proj, cos_t, sin_t, attn, lse, dattn, dproj)


SSM_CHUNKS = 4
CHUNK_STATES = 512
SCAN_ROWS = 8
U_COL = (3 * QKV_WIDTH) // 128


def _cmul(xr, xi, yr, yi):
    return xr * yr - xi * yi, xr * yi + xi * yr


def _ssm_prep(a_re, a_im, log_dt, b_re_t, b_im_t):
    def body(ar_ref, ai_ref, ldt_ref, br_ref, bi_ref, abr_ref, abi_ref, er_ref, ei_ref, bbr_ref, bbi_ref):
        ar, ai = ar_ref[...], ai_ref[...]
        dt = jnp.exp(ldt_ref[...])
        mag = jnp.exp(ar * dt)
        abr, abi = mag * jnp.cos(ai * dt), mag * jnp.sin(ai * dt)
        den = ar * ar + ai * ai
        nr, ni = abr - 1.0, abi
        er, ei = (nr * ar + ni * ai) / den, (ni * ar - nr * ai) / den
        abr_ref[...], abi_ref[...], er_ref[...], ei_ref[...] = abr, abi, er, ei
        er3, ei3 = er[:, None, :], ei[:, None, :]
        br, bi = br_ref[...], bi_ref[...]
        bbr_ref[...] = er3 * br - ei3 * bi
        bbi_ref[...] = er3 * bi + ei3 * br

    gp = jax.ShapeDtypeStruct(a_re.shape, F32)
    gb = jax.ShapeDtypeStruct(b_re_t.shape, F32)
    return _pallas_call(body, name="ssm_prep", out_shape=(gp, gp, gp, gp, gb, gb))(a_re, a_im, log_dt, b_re_t, b_im_t)


def _ssm_param_bwd(a_re, a_im, log_dt, b_re_t, b_im_t, abar_re, abar_im, e_re, e_im, ga_re, ga_im, gbb_re_t, gbb_im_t):
    def body(ar_ref, ai_ref, ldt_ref, br_ref, bi_ref, abr_ref, abi_ref, er_ref, ei_ref, gar_ref, gai_ref, gbr_ref, gbi_ref,
             o_ar, o_ai, o_ldt, o_br, o_bi):
        ar, ai = ar_ref[...], ai_ref[...]
        dt = jnp.exp(ldt_ref[...])
        er, ei = er_ref[...], ei_ref[...]
        br, bi, gbr, gbi = br_ref[...], bi_ref[...], gbr_ref[...], gbi_ref[...]
        er3, ei3 = er[:, None, :], ei[:, None, :]
        o_br[...] = er3 * gbr + ei3 * gbi
        o_bi[...] = er3 * gbi - ei3 * gbr
        ge_r = jnp.sum(br * gbr + bi * gbi, axis=1)
        ge_i = jnp.sum(br * gbi - bi * gbr, axis=1)
        den = ar * ar + ai * ai
        ilr, ili = ar / den, -ai / den
        t_r, t_i = _cmul(ilr, -ili, ge_r, ge_i)
        gab_r, gab_i = gar_ref[...] + t_r, gai_ref[...] + t_i
        gz_r, gz_i = _cmul(abr_ref[...], -abi_ref[...], gab_r, gab_i)
        el_r, el_i = _cmul(er, ei, ilr, ili)
        u_r, u_i = _cmul(el_r, -el_i, ge_r, ge_i)
        o_ar[...] = dt * gz_r - u_r
        o_ai[...] = dt * gz_i - u_i
        o_ldt[...] = jnp.sum(gz_r * ar + gz_i * ai, axis=1, keepdims=True) * dt

    gp = jax.ShapeDtypeStruct(a_re.shape, F32)
    gb = jax.ShapeDtypeStruct(b_re_t.shape, F32)
    return _pallas_call(body, name="ssm_param_bwd", out_shape=(gp, gp, jax.ShapeDtypeStruct(log_dt.shape, F32), gb, gb))(
        a_re, a_im, log_dt, b_re_t, b_im_t, abar_re, abar_im, e_re, e_im, ga_re, ga_im, gbb_re_t, gbb_im_t)


def _block_diag(blocks_re, blocks_im, sign_im, rows_are_channels):
    both = jnp.stack([blocks_re, sign_im * blocks_im]).reshape(2, SSM_CHUNKS, 8, SSM_GROUP, SSM_STATE)
    eye = jnp.eye(8, dtype=F32)
    if rows_are_channels:
        return jnp.einsum("rcghp,gk->cghrkp", both, eye).reshape(SSM_CHUNKS, 128, 2 * CHUNK_STATES)
    return jnp.einsum("rcghp,gk->crkpgh", both, eye).reshape(SSM_CHUNKS, 2 * CHUNK_STATES, 128)


def _block_diag_parts(mat, rows_are_channels):
    if rows_are_channels:
        six = mat.reshape(SSM_CHUNKS, 8, SSM_GROUP, 2, 8, SSM_STATE)
        parts = jnp.einsum("cghrgp->rcghp", six)
    else:
        six = mat.reshape(SSM_CHUNKS, 2, 8, SSM_STATE, 8, SSM_GROUP)
        parts = jnp.einsum("crgpgh->rcghp", six)
    parts = parts.reshape(2, SSM_GROUPS, SSM_GROUP, SSM_STATE)
    return parts[0], parts[1]


def _scan_consts(a_ref, conj, reverse):
    ar = jnp.broadcast_to(a_ref[:, :CHUNK_STATES], (SCAN_ROWS, CHUNK_STATES))
    ai = jnp.broadcast_to(a_ref[:, CHUNK_STATES:], (SCAN_ROWS, CHUNK_STATES))
    if conj:
        ai = -ai
    row = lax.broadcasted_iota(jnp.int32, (SCAN_ROWS, CHUNK_STATES), 0)
    if reverse:
        row = SCAN_ROWS - 1 - row
    zero = jnp.zeros_like(ar)
    steps = []
    pr, pi = ar, ai
    for shift in (1, 2, 4):
        keep = row >= shift
        steps.append((SCAN_ROWS - shift if reverse else shift, jnp.where(keep, pr, zero), jnp.where(keep, pi, zero)))
        pr, pi = _cmul(pr, pi, pr, pi)
    first = row == 0
    return steps, (jnp.where(first, ar, zero), jnp.where(first, ai, zero)), first


def _scan_tile(xr, xi, prev_r, prev_i, steps, carry_in, reverse):
    edge = SCAN_ROWS - 1 if reverse else 1
    cr, ci = pltpu.roll(prev_r, edge, axis=0), pltpu.roll(prev_i, edge, axis=0)
    xr, xi = xr + carry_in[0] * cr - carry_in[1] * ci, xi + carry_in[0] * ci + carry_in[1] * cr
    for shift, mr, mi in steps:
        sr, si = pltpu.roll(xr, shift, axis=0), pltpu.roll(xi, shift, axis=0)
        xr, xi = xr + mr * sr - mi * si, xi + mr * si + mi * sr
    return xr, xi


MM_ROWS = 256


def _ssm_fwd(proj, bmat, cmat, a_chunks, d_skip):
    def body(u_ref, b_ref, c_ref, a_ref, d_ref, y_ref, h_ref):
        for i in range(SEQ // MM_ROWS):
            rows = pl.ds(i * MM_ROWS, MM_ROWS)
            h_ref[rows, :] = _dot(u_ref[rows, :].astype(BF16), b_ref[...])
        steps, carry_in, _ = _scan_consts(a_ref, conj=False, reverse=False)

        def tile(k, carry):
            rows = pl.ds(pl.multiple_of(k * SCAN_ROWS, SCAN_ROWS), SCAN_ROWS)
            xr, xi = _scan_tile(h_ref[rows, :CHUNK_STATES], h_ref[rows, CHUNK_STATES:], carry[0], carry[1], steps, carry_in, False)
            h_ref[rows, :CHUNK_STATES] = xr
            h_ref[rows, CHUNK_STATES:] = xi
            return xr, xi

        zero = jnp.zeros((SCAN_ROWS, CHUNK_STATES), F32)
        lax.fori_loop(0, SEQ // SCAN_ROWS, tile, (zero, zero))
        for i in range(SEQ // MM_ROWS):
            rows = pl.ds(i * MM_ROWS, MM_ROWS)
            y_ref[rows, :] = _dot(h_ref[rows, :].astype(BF16), c_ref[...]) + d_ref[...] * u_ref[rows, :]

    return _pallas_call(
        body, name="ssm_fwd", grid=(SSM_CHUNKS,),
        in_specs=[pl.BlockSpec((SEQ, 128), lambda c: (0, U_COL + c)),
                  pl.BlockSpec((None, 128, 2 * CHUNK_STATES), lambda c: (c, 0, 0)),
                  pl.BlockSpec((None, 2 * CHUNK_STATES, 128), lambda c: (c, 0, 0)),
                  pl.BlockSpec((None, 1, 2 * CHUNK_STATES), lambda c: (c, 0, 0)),
                  pl.BlockSpec((1, 128), lambda c: (0, c))],
        out_specs=(pl.BlockSpec((SEQ, 128), lambda c: (0, c)), pl.BlockSpec((SEQ, 2 * CHUNK_STATES), lambda c: (0, c))),
        out_shape=(jax.ShapeDtypeStruct((SEQ, SSM_WIDTH), F32), jax.ShapeDtypeStruct((SEQ, SSM_CHUNKS * 2 * CHUNK_STATES), F32)),
        compiler_params=_cparams(dimension_semantics=("arbitrary",)),
    )(proj, bmat, cmat, a_chunks, d_skip)


def _ssm_bwd(dys, proj, h, bmat, cmat, a_chunks, d_skip, dproj):
    def body(dy_ref, u_ref, h_ref, b_ref, c_ref, a_ref, d_ref, dproj_in, du_ref, db_ref, dc_ref, da_ref, dd_ref, g_ref):
        del dproj_in
        dsum = jnp.zeros((1, 128), F32)
        dcm = jnp.zeros((2 * CHUNK_STATES, 128), F32)
        for i in range(SEQ // MM_ROWS):
            rows = pl.ds(i * MM_ROWS, MM_ROWS)
            dy = dy_ref[rows, :]
            g_ref[rows, :] = _dot_nt(dy.astype(BF16), c_ref[...])
            dsum += jnp.sum(dy * u_ref[rows, :], axis=0, keepdims=True)
            dcm += _dot_tn(h_ref[rows, :].astype(BF16), dy.astype(BF16))
        dd_ref[...] = dsum
        dc_ref[...] = dcm
        steps, carry_in, _ = _scan_consts(a_ref, conj=True, reverse=True)
        first_row = lax.broadcasted_iota(jnp.int32, (SCAN_ROWS, CHUNK_STATES), 0) == 0
        n_tiles = SEQ // SCAN_ROWS

        def tile(j, carry):
            k = n_tiles - 1 - j
            rows = pl.ds(pl.multiple_of(k * SCAN_ROWS, SCAN_ROWS), SCAN_ROWS)
            before = pl.ds(pl.multiple_of(jnp.maximum(k - 1, 0) * SCAN_ROWS, SCAN_ROWS), SCAN_ROWS)
            gr, gi = _scan_tile(g_ref[rows, :CHUNK_STATES], g_ref[rows, CHUNK_STATES:], carry[0], carry[1], steps, carry_in, True)
            g_ref[rows, :CHUNK_STATES] = gr
            g_ref[rows, CHUNK_STATES:] = gi
            has_before = jnp.where(k > 0, 1.0, 0.0)
            hr = jnp.where(first_row, pltpu.roll(h_ref[before, :CHUNK_STATES], 1, axis=0) * has_before,
                           pltpu.roll(h_ref[rows, :CHUNK_STATES], 1, axis=0))
            hi = jnp.where(first_row, pltpu.roll(h_ref[before, CHUNK_STATES:], 1, axis=0) * has_before,
                           pltpu.roll(h_ref[rows, CHUNK_STATES:], 1, axis=0))
            return gr, gi, carry[2] + hr * gr + hi * gi, carry[3] + hr * gi - hi * gr

        zero = jnp.zeros((SCAN_ROWS, CHUNK_STATES), F32)
        _, _, sar, sai = lax.fori_loop(0, n_tiles, tile, (zero, zero, zero, zero))
        da_ref[:, :CHUNK_STATES] = jnp.sum(sar, axis=0, keepdims=True)
        da_ref[:, CHUNK_STATES:] = jnp.sum(sai, axis=0, keepdims=True)
        dbm = jnp.zeros((128, 2 * CHUNK_STATES), F32)
        for i in range(SEQ // MM_ROWS):
            rows = pl.ds(i * MM_ROWS, MM_ROWS)
            g = g_ref[rows, :].astype(BF16)
            du_ref[rows, :] = (_dot_nt(g, b_ref[...]) + d_ref[...] * dy_ref[rows, :]).astype(BF16)
            dbm += _dot_tn(u_ref[rows, :].astype(BF16), g)
        db_ref[...] = dbm

    chunk_col = pl.BlockSpec((SEQ, 128), lambda c: (0, c))
    return _pallas_call(
        body, name="ssm_bwd", grid=(SSM_CHUNKS,),
        in_specs=[chunk_col,
                  pl.BlockSpec((SEQ, 128), lambda c: (0, U_COL + c)),
                  pl.BlockSpec((SEQ, 2 * CHUNK_STATES), lambda c: (0, c)),
                  pl.BlockSpec((None, 128, 2 * CHUNK_STATES), lambda c: (c, 0, 0)),
                  pl.BlockSpec((None, 2 * CHUNK_STATES, 128), lambda c: (c, 0, 0)),
                  pl.BlockSpec((None, 1, 2 * CHUNK_STATES), lambda c: (c, 0, 0)),
                  pl.BlockSpec((1, 128), lambda c: (0, c)), ANY],
        out_specs=(pl.BlockSpec((SEQ, 128), lambda c: (0, U_COL + c)),
                   pl.BlockSpec((None, 128, 2 * CHUNK_STATES), lambda c: (c, 0, 0)),
                   pl.BlockSpec((None, 2 * CHUNK_STATES, 128), lambda c: (c, 0, 0)),
                   pl.BlockSpec((None, 1, 2 * CHUNK_STATES), lambda c: (c, 0, 0)),
                   pl.BlockSpec((1, 128), lambda c: (0, c))),
        input_output_aliases={7: 0},
        out_shape=(jax.ShapeDtypeStruct((SEQ, IN_WIDTH), BF16),
                   jax.ShapeDtypeStruct((SSM_CHUNKS, 128, 2 * CHUNK_STATES), F32),
                   jax.ShapeDtypeStruct((SSM_CHUNKS, 2 * CHUNK_STATES, 128), F32),
                   jax.ShapeDtypeStruct((SSM_CHUNKS, 1, 2 * CHUNK_STATES), F32),
                   jax.ShapeDtypeStruct((1, SSM_WIDTH), F32)),
        scratch_shapes=[pltpu.VMEM((SEQ, 2 * CHUNK_STATES), F32)],
        compiler_params=_cparams(dimension_semantics=("arbitrary",)),
    )(dys, proj, h, bmat, cmat, a_chunks, d_skip, dproj)


def _ssm_tables(abar_re, abar_im, bbar_re_t, bbar_im_t, c_re, c_im):
    bmat = _block_diag(bbar_re_t, bbar_im_t, 1.0, True).astype(BF16)
    cmat = _block_diag(c_re, c_im, -1.0, False).astype(BF16)
    a_chunks = jnp.concatenate([abar_re.reshape(SSM_CHUNKS, 1, CHUNK_STATES), abar_im.reshape(SSM_CHUNKS, 1, CHUNK_STATES)], axis=2)
    return bmat, cmat, a_chunks


GL_COL = (3 * QKV_WIDTH + SSM_WIDTH) // D_MODEL
GELU_C = math.sqrt(2.0 / math.pi)
GELU_A = 0.044715


def _sds(shape, dtype):
    return jax.ShapeDtypeStruct(shape, dtype)


def _gelu(x):
    t = jnp.tanh(GELU_C * (x + GELU_A * x * x * x))
    return 0.5 * x * (1.0 + t), t


def _gelu_grad(x, t):
    return 0.5 * (1.0 + t) + 0.5 * x * (1.0 - t * t) * GELU_C * (1.0 + 3.0 * GELU_A * x * x)


def _layer_norm(r, g, b):
    mu = jnp.mean(r, axis=-1, keepdims=True)
    xc = r - mu
    rstd = lax.rsqrt(jnp.mean(xc * xc, axis=-1, keepdims=True) + LN_EPS)
    xhat = xc * rstd
    return xhat * g + b, xhat, rstd


def _layer_norm_bwd(dy, xhat, rstd, g):
    dxhat = dy * g
    m1 = jnp.mean(dxhat, axis=-1, keepdims=True)
    m2 = jnp.mean(dxhat * xhat, axis=-1, keepdims=True)
    return rstd * (dxhat - m1 - xhat * m2)


def _proj(x, w_in):
    tm, tn = 512, 1792

    def body(x_ref, w_ref, o_ref):
        o_ref[...] = _dot(x_ref[...].astype(BF16), _side_by_side(w_ref))

    return _pallas_call(
        body, name="proj", grid=(SEQ // tm, IN_WIDTH // tn),
        in_specs=[pl.BlockSpec((tm, D_MODEL), lambda i, j: (i, 0)), pl.BlockSpec((2, D_MODEL, tn // 2), lambda i, j: (j, 0, 0))],
        out_specs=pl.BlockSpec((tm, tn), lambda i, j: (i, j)), out_shape=_sds((SEQ, IN_WIDTH), F32),
        compiler_params=_cparams(dimension_semantics=("arbitrary", "arbitrary")),
    )(x, w_in)


def _row_spec(tm, width, col=0):
    return pl.BlockSpec((tm, width), lambda i, col=col: (i, col))


def _full_spec(shape):
    return pl.BlockSpec(shape, lambda i: (0,) * len(shape))


def _mixer_out(attn, ys, proj, x, w_ab, w_sb, w_glu, w_out, b_gate, ln_g, ln_b):
    tm = 256

    def body(attn_ref, ys_ref, gl0_ref, gl1_ref, x_ref, wab_ref, wsb_ref, wglu_ref, wout_ref, bg_ref, g_ref, b_ref,
             h_ref, xhat_ref, rstd_ref, glu_ref, ya_ref, yssm_ref):
        gy, _ = _gelu(ys_ref[...])
        glu = _dot(gy.astype(BF16), _side_by_side(wglu_ref))
        glu_ref[...] = glu
        y_s = glu[:, :SSM_WIDTH] * jax.nn.sigmoid(glu[:, SSM_WIDTH:])
        y_ssm = _dot(y_s.astype(BF16), _side_by_side(wsb_ref))
        y_attn = _dot(attn_ref[...].astype(BF16), _side_by_side(wab_ref))
        ya_ref[...] = y_attn
        yssm_ref[...] = y_ssm
        g0 = jax.nn.sigmoid(gl0_ref[...] + _side_by_side(bg_ref, 0))
        g1 = jax.nn.sigmoid(gl1_ref[...] + _side_by_side(bg_ref, 1))
        mixed = g0 * y_attn + g1 * y_ssm
        r1 = DN_ALPHA * x_ref[...] + _dot(mixed.astype(BF16), wout_ref[...])
        h, xhat, rstd = _layer_norm(r1, g_ref[...], b_ref[...])
        h_ref[...] = h
        xhat_ref[...] = xhat
        rstd_ref[...] = jnp.broadcast_to(rstd, (tm, 128))

    wide = _sds((SEQ, D_MODEL), F32)
    return _pallas_call(
        body, name="mixer_out", grid=(SEQ // tm,),
        in_specs=[_row_spec(tm, ATTN_WIDTH), _row_spec(tm, SSM_WIDTH), _row_spec(tm, D_MODEL, GL_COL), _row_spec(tm, D_MODEL, GL_COL + 1),
                  _row_spec(tm, D_MODEL), _full_spec((N_DEV, ATTN_WIDTH, 128)), _full_spec((N_DEV, SSM_WIDTH, 128)),
                  _full_spec((N_DEV, SSM_WIDTH, 128)), _full_spec((D_MODEL, D_MODEL)), _full_spec((N_DEV, 2, 128)),
                  _full_spec((1, D_MODEL)), _full_spec((1, D_MODEL))],
        out_specs=(_row_spec(tm, D_MODEL), _row_spec(tm, D_MODEL), _row_spec(tm, 128), _row_spec(tm, D_MODEL),
                   _row_spec(tm, D_MODEL), _row_spec(tm, D_MODEL)),
        out_shape=(wide, wide, _sds((SEQ, 128), F32), wide, wide, wide),
        compiler_params=_cparams(dimension_semantics=("arbitrary",)),
    )(attn, ys, proj, proj, x, w_ab, w_sb, w_glu, w_out, b_gate, ln_g, ln_b)


def _ff_up(h, w_gate, w_up):
    tm, tn = 512, 768

    def body(h_ref, wg_ref, wu_ref, a_ref, b_ref, f_ref):
        hb = h_ref[...].astype(BF16)
        a, b = _dot(hb, _side_by_side(wg_ref)), _dot(hb, _side_by_side(wu_ref))
        a_ref[...] = a.astype(BF16)
        b_ref[...] = b.astype(BF16)
        f_ref[...] = (a * jax.nn.sigmoid(a) * b).astype(BF16)

    tile = pl.BlockSpec((tm, tn), lambda i, j: (i, j))
    wtile = pl.BlockSpec((tn // FF_PAD, D_MODEL, FF_PAD), lambda i, j: (j, 0, 0))
    out = _sds((SEQ, D_FF_PAD), BF16)
    return _pallas_call(
        body, name="ff_up", grid=(SEQ // tm, D_FF_PAD // tn),
        in_specs=[pl.BlockSpec((tm, D_MODEL), lambda i, j: (i, 0)), wtile, wtile],
        out_specs=(tile, tile, tile), out_shape=(out, out, out),
        compiler_params=_cparams(dimension_semantics=("arbitrary", "arbitrary")),
    )(h, w_gate, w_up)


def _ff_down_loss(f, w_down, h, target, ln_g, ln_b):
    tm = 256

    def body(f_ref, w_ref, h_ref, t_ref, g_ref, b_ref, dr_ref, dg_ref, db_ref, loss_ref):
        @pl.when(pl.program_id(0) == 0)
        def _():
            dg_ref[...] = jnp.zeros_like(dg_ref)
            db_ref[...] = jnp.zeros_like(db_ref)
            loss_ref[...] = jnp.zeros_like(loss_ref)

        r2 = DN_ALPHA * h_ref[...] + _dot(f_ref[...], w_ref[...])
        g = g_ref[...]
        out, xhat, rstd = _layer_norm(r2, g, b_ref[...])
        err = out - t_ref[...]
        loss_ref[...] += 0.5 * jnp.sum(jnp.mean(err * err, axis=-1, keepdims=True), axis=0, keepdims=True)
        dout = err * (1.0 / D_MODEL)
        dg_ref[...] += jnp.sum(dout * xhat, axis=0, keepdims=True)
        db_ref[...] += jnp.sum(dout, axis=0, keepdims=True)
        dr_ref[...] = _layer_norm_bwd(dout, xhat, rstd, g)

    vec = _sds((1, D_MODEL), F32)
    return _pallas_call(
        body, name="ff_down_loss", grid=(SEQ // tm,),
        in_specs=[_row_spec(tm, D_FF_PAD), _full_spec((D_FF_PAD, D_MODEL)), _row_spec(tm, D_MODEL), _row_spec(tm, D_MODEL),
                  _full_spec((1, D_MODEL)), _full_spec((1, D_MODEL))],
        out_specs=(_row_spec(tm, D_MODEL), _full_spec((1, D_MODEL)), _full_spec((1, D_MODEL)), _full_spec((1, 128))),
        out_shape=(_sds((SEQ, D_MODEL), F32), vec, vec, _sds((1, 128), F32)),
        compiler_params=_cparams(dimension_semantics=("arbitrary",)),
    )(f, w_down, h, target, ln_g, ln_b)


def _ff_down_bwd(dr2, w_down, a, b):
    tm, tn = 512, 768

    def body(dr_ref, w_ref, a_ref, b_ref, da_ref, db_ref):
        df = _dot_nt(dr_ref[...].astype(BF16), w_ref[...])
        av, bv = a_ref[...].astype(F32), b_ref[...].astype(F32)
        sg = jax.nn.sigmoid(av)
        da_ref[...] = (df * bv * sg * (1.0 + av * (1.0 - sg))).astype(BF16)
        db_ref[...] = (df * av * sg).astype(BF16)

    tile = pl.BlockSpec((tm, tn), lambda i, j: (i, j))
    out = _sds((SEQ, D_FF_PAD), BF16)
    return _pallas_call(
        body, name="ff_down_bwd", grid=(SEQ // tm, D_FF_PAD // tn),
        in_specs=[pl.BlockSpec((tm, D_MODEL), lambda i, j: (i, 0)), pl.BlockSpec((tn, D_MODEL), lambda i, j: (j, 0)), tile, tile],
        out_specs=(tile, tile), out_shape=(out, out),
        compiler_params=_cparams(dimension_semantics=("arbitrary", "arbitrary")),
    )(dr2, w_down, a, b)


def _ff_up_bwd(da, db, w_gate, w_up, dr2, xhat1, rstd1, ln_g):
    tm, tk = 512, 768
    nk = D_FF_PAD // tk

    def body(da_ref, db_ref, wg_ref, wu_ref, dr2_ref, xhat_ref, rstd_ref, g_ref, dr1_ref, dg_ref, dbias_ref, acc):
        i, k = pl.program_id(0), pl.program_id(1)

        @pl.when(jnp.logical_and(i == 0, k == 0))
        def _():
            dg_ref[...] = jnp.zeros_like(dg_ref)
            dbias_ref[...] = jnp.zeros_like(dbias_ref)

        part = _dot_nt(da_ref[...], _side_by_side(wg_ref)) + _dot_nt(db_ref[...], _side_by_side(wu_ref))

        @pl.when(k == 0)
        def _():
            acc[...] = part

        @pl.when(k > 0)
        def _():
            acc[...] += part

        @pl.when(k == nk - 1)
        def _():
            dh = DN_ALPHA * dr2_ref[...] + acc[...]
            xhat = xhat_ref[...]
            dg_ref[...] += jnp.sum(dh * xhat, axis=0, keepdims=True)
            dbias_ref[...] += jnp.sum(dh, axis=0, keepdims=True)
            rstd = jnp.max(rstd_ref[...], axis=1, keepdims=True)
            dr1_ref[...] = _layer_norm_bwd(dh, xhat, rstd, g_ref[...])

    hid = pl.BlockSpec((tm, tk), lambda i, k: (i, k))
    wtile = pl.BlockSpec((tk // FF_PAD, D_MODEL, FF_PAD), lambda i, k: (k, 0, 0))
    row = pl.BlockSpec((tm, D_MODEL), lambda i, k: (i, 0))
    vec = pl.BlockSpec((1, D_MODEL), lambda i, k: (0, 0))
    return _pallas_call(
        body, name="ff_up_bwd", grid=(SEQ // tm, nk),
        in_specs=[hid, hid, wtile, wtile, row, row, pl.BlockSpec((tm, 128), lambda i, k: (i, 0)), vec],
        out_specs=(row, vec, vec), out_shape=(_sds((SEQ, D_MODEL), F32), _sds((1, D_MODEL), F32), _sds((1, D_MODEL), F32)),
        scratch_shapes=[pltpu.VMEM((tm, D_MODEL), F32)],
        compiler_params=_cparams(dimension_semantics=("arbitrary", "arbitrary")),
    )(da, db, w_gate, w_up, dr2, xhat1, rstd1, ln_g)


def _mixer_bwd(dr1, proj, y_attn, y_ssm, glu, ys, w_ab, w_sb, w_glu, w_out, b_gate):
    tm = 256

    def body(dr1_ref, gl0_ref, gl1_ref, ya_ref, yssm_ref, glu_ref, ys_ref, wab_ref, wsb_ref, wglu_ref, wout_ref, bg_ref,
             dya_ref, dyssm_ref, dgl_ref, dattn_ref, dglu_ref, dys_ref, mixed_ref, ysb_ref, gy_ref, dbg_ref):
        @pl.when(pl.program_id(0) == 0)
        def _():
            dbg_ref[...] = jnp.zeros_like(dbg_ref)

        dmixed = _dot_nt(dr1_ref[...].astype(BF16), wout_ref[...])
        g0 = jax.nn.sigmoid(gl0_ref[...] + _side_by_side(bg_ref, 0))
        g1 = jax.nn.sigmoid(gl1_ref[...] + _side_by_side(bg_ref, 1))
        y_attn, y_ssm = ya_ref[...], yssm_ref[...]
        mixed_ref[...] = (g0 * y_attn + g1 * y_ssm).astype(BF16)
        dya = (dmixed * g0).astype(BF16)
        dyssm = (dmixed * g1).astype(BF16)
        dya_ref[...] = dya
        dyssm_ref[...] = dyssm
        dgl0 = dmixed * y_attn * g0 * (1.0 - g0)
        dgl1 = dmixed * y_ssm * g1 * (1.0 - g1)
        dgl_ref[:, :GL_COL * D_MODEL] = jnp.zeros((tm, GL_COL * D_MODEL), BF16)
        dgl_ref[:, GL_COL * D_MODEL:(GL_COL + 1) * D_MODEL] = dgl0.astype(BF16)
        dgl_ref[:, (GL_COL + 1) * D_MODEL:] = dgl1.astype(BF16)
        dbg_ref[:, :D_MODEL] += jnp.sum(dgl0, axis=0, keepdims=True)
        dbg_ref[:, D_MODEL:] += jnp.sum(dgl1, axis=0, keepdims=True)
        dattn_ref[...] = _dot_nt(dya, _side_by_side(wab_ref))
        dy_s = _dot_nt(dyssm, _side_by_side(wsb_ref))
        glu = glu_ref[...]
        glu1, sg = glu[:, :SSM_WIDTH], jax.nn.sigmoid(glu[:, SSM_WIDTH:])
        ysb_ref[...] = (glu1 * sg).astype(BF16)
        dglu1 = (dy_s * sg).astype(BF16)
        dglu2 = (dy_s * glu1 * sg * (1.0 - sg)).astype(BF16)
        dglu_ref[:, :SSM_WIDTH] = dglu1
        dglu_ref[:, SSM_WIDTH:] = dglu2
        dgy = _dot_nt(jnp.concatenate([dglu1, dglu2], axis=1), _side_by_side(wglu_ref))
        ys = ys_ref[...]
        gy, t = _gelu(ys)
        gy_ref[...] = gy.astype(BF16)
        dys_ref[...] = dgy * _gelu_grad(ys, t)

    wide_b, half_b = _sds((SEQ, D_MODEL), BF16), _sds((SEQ, SSM_WIDTH), BF16)
    half_f = _sds((SEQ, SSM_WIDTH), F32)
    return _pallas_call(
        body, name="mixer_bwd", grid=(SEQ // tm,),
        in_specs=[_row_spec(tm, D_MODEL), _row_spec(tm, D_MODEL, GL_COL), _row_spec(tm, D_MODEL, GL_COL + 1), _row_spec(tm, D_MODEL),
                  _row_spec(tm, D_MODEL), _row_spec(tm, D_MODEL), _row_spec(tm, SSM_WIDTH), _full_spec((N_DEV, ATTN_WIDTH, 128)),
                  _full_spec((N_DEV, SSM_WIDTH, 128)), _full_spec((N_DEV, SSM_WIDTH, 128)), _full_spec((D_MODEL, D_MODEL)),
                  _full_spec((N_DEV, 2, 128))],
        out_specs=(_row_spec(tm, D_MODEL), _row_spec(tm, D_MODEL), _row_spec(tm, IN_WIDTH), _row_spec(tm, ATTN_WIDTH),
                   _row_spec(tm, D_MODEL), _row_spec(tm, SSM_WIDTH), _row_spec(tm, D_MODEL), _row_spec(tm, SSM_WIDTH),
                   _row_spec(tm, SSM_WIDTH), _full_spec((1, 2 * D_MODEL))),
        out_shape=(wide_b, wide_b, _sds((SEQ, IN_WIDTH), BF16), half_f, wide_b, half_f, wide_b, half_b, half_b,
                   _sds((1, 2 * D_MODEL), F32)),
        compiler_params=_cparams(dimension_semantics=("arbitrary",)),
    )(dr1, proj, proj, y_attn, y_ssm, glu, ys, w_ab, w_sb, w_glu, w_out, b_gate)


def _grad_x(dproj, w_in, dr1):
    tm, tk = 512, 1792
    nk = IN_WIDTH // tk

    def body(dp_ref, w_ref, dr1_ref, o_ref, acc):
        k = pl.program_id(1)
        part = _dot_nt(dp_ref[...], _side_by_side(w_ref))

        @pl.when(k == 0)
        def _():
            acc[...] = part

        @pl.when(k > 0)
        def _():
            acc[...] += part

        @pl.when(k == nk - 1)
        def _():
            o_ref[...] = DN_ALPHA * dr1_ref[...] + acc[...]

    row = pl.BlockSpec((tm, D_MODEL), lambda i, k: (i, 0))
    return _pallas_call(
        body, name="grad_x", grid=(SEQ // tm, nk),
        in_specs=[pl.BlockSpec((tm, tk), lambda i, k: (i, k)), pl.BlockSpec((2, D_MODEL, tk // 2), lambda i, k: (k, 0, 0)), row],
        out_specs=row, out_shape=_sds((SEQ, D_MODEL), F32), scratch_shapes=[pltpu.VMEM((tm, D_MODEL), F32)],
        compiler_params=_cparams(dimension_semantics=("arbitrary", "arbitrary")),
    )(dproj, w_in, dr1)


def _weight_grad(a, b, name, shard_cols=None):
    k, n = a.shape[1], b.shape[1]
    tm = 512
    nm = SEQ // tm
    tk = min(k, 512) if shard_cols else k // N_DEV
    tn = n // 4 if shard_cols else min(n, 1024)

    def body(a_ref, b_ref, o_ref, acc):
        m = pl.program_id(2)
        part = _dot_tn(a_ref[...].astype(BF16), b_ref[...].astype(BF16))

        @pl.when(m == 0)
        def _():
            acc[...] = part

        @pl.when(m > 0)
        def _():
            acc[...] += part

        @pl.when(m == nm - 1)
        def _():
            if shard_cols:
                o_ref[0] = acc[:, :shard_cols].astype(BF16)
                o_ref[1] = acc[:, shard_cols:].astype(BF16)
            else:
                o_ref[...] = acc[...].astype(BF16)

    if shard_cols:
        out_spec = pl.BlockSpec((2, None, tk, shard_cols), lambda kk, j, m: (0, j, kk, 0))
        out_shape = _sds((2, 4, k, shard_cols), BF16)
    else:
        out_spec = pl.BlockSpec((None, None, tk, tn), lambda kk, j, m: (kk % 2, kk // 2, 0, j))
        out_shape = _sds((2, 4, tk, n), BF16)
    return _pallas_call(
        body, name=name, grid=(k // tk, n // tn, nm),
        in_specs=[pl.BlockSpec((tm, tk), lambda kk, j, m: (m, kk)), pl.BlockSpec((tm, tn), lambda kk, j, m: (m, j))],
        out_specs=out_spec, out_shape=out_shape, scratch_shapes=[pltpu.VMEM((tk, tn), F32)],
        compiler_params=_cparams(dimension_semantics=("arbitrary", "arbitrary", "arbitrary")),
    )(a, b)


MESH = pl.DeviceIdType.MESH
ANY = pl.BlockSpec(memory_space=pl.ANY)


def _place():
    return lax.axis_index("x"), lax.axis_index("y"), lax.axis_index("c")


def _other_chips(x, y):
    return [(1 - x, y), (x, 1 - y), (1 - x, 1 - y)]


def _send_buffer(w, rows, cols, name):
    r, c = w.shape

    def body(w_ref, o_ref):
        if (r, c) != (rows, cols):
            o_ref[...] = jnp.zeros((rows, cols), BF16)
        o_ref[:r, :c] = w_ref[...].astype(BF16)

    return _pallas_call(body, name=name, out_shape=_sds((rows, cols), BF16))(w)


def _all_gather(shards, name):
    n = len(shards)

    def body(*refs):
        ins, outs = refs[:n], refs[n:2 * n]
        send_sems, recv_sems, local_sems = refs[2 * n:]
        x, y, c = _place()
        me, sibling = (x, y, c), (x, y, 1 - c)
        chips = _other_chips(x, y)

        def slot(a, px, py, pc):
            return outs[a].at[4 * px + 2 * py + pc]

        def copy(a, k, block, to, src=None):
            return pltpu.make_async_remote_copy(
                src_ref=slot(a, *block) if src is None else src, dst_ref=slot(a, *block),
                send_sem=send_sems.at[a, k], recv_sem=recv_sems.at[a, k], device_id=to, device_id_type=MESH)

        mine = [pltpu.make_async_copy(ins[a], slot(a, *me), local_sems.at[a]) for a in range(n)]
        for cp in mine:
            cp.start()
        first = []
        for a in range(n):
            first.append(copy(a, 0, me, sibling, src=ins[a]))
            first += [copy(a, 1 + j, me, (*chip, c), src=ins[a]) for j, chip in enumerate(chips)]
        for cp in first:
            cp.start()
        passed = []
        for j, chip in enumerate(chips):
            for a in range(n):
                copy(a, 1 + j, (*chip, c), me).wait_recv()
                onward = copy(a, 4 + j, (*chip, c), sibling)
                onward.start()
                passed.append(onward)
        for a in range(n):
            copy(a, 0, sibling, me).wait_recv()
            for j, chip in enumerate(chips):
                copy(a, 4 + j, (*chip, 1 - c), me).wait_recv()
        for cp in first + passed:
            cp.wait_send()
        for cp in mine:
            cp.wait()

    return _pallas_call(
        body, name=name, in_specs=[ANY] * n, out_specs=[ANY] * n,
        out_shape=[_sds((N_DEV,) + s.shape, s.dtype) for s in shards],
        scratch_shapes=[pltpu.SemaphoreType.DMA((n, 7)), pltpu.SemaphoreType.DMA((n, 7)), pltpu.SemaphoreType.DMA((n,))],
    )(*shards)


def _swap_with_sibling(grads, name):
    n = len(grads)

    def body(*refs):
        ins, outs = refs[:n], refs[n:2 * n]
        send_sems, recv_sems = refs[2 * n:]
        x, y, c = _place()
        copies = [pltpu.make_async_remote_copy(
            src_ref=ins[a].at[1 - c], dst_ref=outs[a], send_sem=send_sems.at[a], recv_sem=recv_sems.at[a],
            device_id=(x, y, 1 - c), device_id_type=MESH) for a in range(n)]
        for cp in copies:
            cp.start()
        for cp in copies:
            cp.wait()

    return _pallas_call(
        body, name=name, in_specs=[ANY] * n, out_specs=[ANY] * n,
        out_shape=[_sds(g.shape[1:], g.dtype) for g in grads],
        scratch_shapes=[pltpu.SemaphoreType.DMA((n,)), pltpu.SemaphoreType.DMA((n,))],
    )(*grads)


def _pair_sum(g, r, core, name):
    _, _, k, n = g.shape

    def body(core_ref, g_ref, r_ref, o_ref):
        o_ref[...] = (g_ref[...].astype(F32) + r_ref[...].astype(F32)).astype(o_ref.dtype)

    return _pallas_call(
        body, name=name,
        grid_spec=pltpu.PrefetchScalarGridSpec(
            num_scalar_prefetch=1, grid=(4,),
            in_specs=[pl.BlockSpec((None, None, k, n), lambda p, core_ref: (core_ref[0], p, 0, 0)),
                      pl.BlockSpec((None, k, n), lambda p, core_ref: (p, 0, 0))],
            out_specs=pl.BlockSpec((None, k, n), lambda p, core_ref: (p, 0, 0))),
        out_shape=_sds((4, k, n), g.dtype), compiler_params=_cparams(dimension_semantics=("arbitrary",)),
    )(core, g, r)


def _swap_between_chips(sums, name):
    n = len(sums)

    def body(*refs):
        ins, outs = refs[:n], refs[n:2 * n]
        send_sems, recv_sems, local_sems = refs[2 * n:]
        x, y, c = _place()
        mine = 2 * x + y
        copies = []
        for a in range(n):
            copies.append(pltpu.make_async_copy(ins[a].at[mine], outs[a].at[mine], local_sems.at[a]))
            for j, (px, py) in enumerate(_other_chips(x, y)):
                copies.append(pltpu.make_async_remote_copy(
                    src_ref=ins[a].at[2 * px + py], dst_ref=outs[a].at[mine], send_sem=send_sems.at[a, j],
                    recv_sem=recv_sems.at[a, j], device_id=(px, py, c), device_id_type=MESH))
        for cp in copies:
            cp.start()
        for cp in copies:
            cp.wait()

    return _pallas_call(
        body, name=name, in_specs=[ANY] * n, out_specs=[ANY] * n, out_shape=[_sds(s.shape, s.dtype) for s in sums],
        scratch_shapes=[pltpu.SemaphoreType.DMA((n, 3)), pltpu.SemaphoreType.DMA((n, 3)), pltpu.SemaphoreType.DMA((n,))],
    )(*sums)


def _adamw_math(w, g, m, v):
    m = ADAM_B1 * m + (1.0 - ADAM_B1) * g
    v = ADAM_B2 * v + (1.0 - ADAM_B2) * (g * g)
    m_hat = m / (1.0 - ADAM_B1 ** ADAM_STEP)
    v_hat = v / (1.0 - ADAM_B2 ** ADAM_STEP)
    return -ADAM_LR * (m_hat / (jnp.sqrt(v_hat) + ADAM_EPS) + ADAM_WD * w), m, v


def _adamw(w, m, v, parts, name):
    r, c = w.shape
    tr = r if r <= 512 else 256
    n_parts, pr, pc = parts.shape
    assert r % tr == 0 and (tr == r or pr == r)

    def body(w_ref, m_ref, v_ref, p_ref, g_out, d_out, m_out, v_out):
        g = p_ref[0, :tr, :c].astype(F32)
        for p in range(1, n_parts):
            g = g + p_ref[p, :tr, :c].astype(F32)
        g_out[...] = g
        d_out[...], m_out[...], v_out[...] = _adamw_math(w_ref[...], g, m_ref[...], v_ref[...])

    tile = pl.BlockSpec((tr, c), lambda i: (i, 0))
    part_tile = pl.BlockSpec((n_parts, pr if tr == r else tr, pc), lambda i: (0, i, 0))
    out = _sds((r, c), F32)
    return _pallas_call(
        body, name=name, grid=(r // tr,), in_specs=[tile, tile, tile, part_tile], out_specs=(tile,) * 4,
        out_shape=(out,) * 4, compiler_params=_cparams(dimension_semantics=("arbitrary",)),
    )(w, m, v, parts)


SMALL = ("ssm_a_re", "ssm_a_im", "ssm_log_dt", "ssm_b_re", "ssm_b_im", "ssm_c_re", "ssm_c_im", "ssm_d",
         "ln1_g", "ln1_b", "ln2_g", "ln2_b")


def _pack_rows(arrays):
    rows = []
    for a in arrays:
        flat = a.reshape(-1)
        rows.append(jnp.pad(flat, (0, -flat.shape[0] % 128)).reshape(-1, 128))
    packed = jnp.concatenate(rows, axis=0)
    return jnp.pad(packed, ((0, -packed.shape[0] % 8), (0, 0)))


def _unpack_rows(packed, shapes):
    out, row = [], 0
    for shape in shapes:
        size = math.prod(shape)
        n_rows = -(-size // 128)
        out.append(packed[row:row + n_rows].reshape(-1)[:size].reshape(shape))
        row += n_rows
    return out


def _sum_devices(parts):
    def body(p_ref, o_ref):
        total = p_ref[0]
        for dev in range(1, N_DEV):
            total = total + p_ref[dev]
        o_ref[...] = total

    return _pallas_call(body, name="sum_devices", out_shape=_sds(parts.shape[1:], F32))(parts)


def _adamw_replicated(ws, ms, vs, gs):
    n = len(ws)

    def body(*refs):
        w_refs, m_refs, v_refs, g_refs, d_out, m_out, v_out = (refs[i * n:(i + 1) * n] for i in range(7))
        for i in range(n):
            d_out[i][...], m_out[i][...], v_out[i][...] = _adamw_math(w_refs[i][...], g_refs[i][...], m_refs[i][...], v_refs[i][...])

    out = _pallas_call(body, name="adamw_replicated", out_shape=[_sds(w.shape, F32) for w in ws] * 3,
                       compiler_params=_cparams())(*ws, *ms, *vs, *gs)
    return out[:n], out[n:2 * n], out[2 * n:]


def kernel(x, w_in, b_gate, w_attn_br, w_ssm_br, w_out, ssm_a_re, ssm_a_im, ssm_log_dt, ssm_b_re, ssm_b_im, ssm_c_re, ssm_c_im, ssm_d, w_glu, ln1_g, ln1_b, w_ff_gate, w_ff_up, w_ff_down, ln2_g, ln2_b, loss_target, m_w_in, m_b_gate, m_w_attn_br, m_w_ssm_br, m_w_out, m_ssm_a_re, m_ssm_a_im, m_ssm_log_dt, m_ssm_b_re, m_ssm_b_im, m_ssm_c_re, m_ssm_c_im, m_ssm_d, m_w_glu, m_ln1_g, m_ln1_b, m_w_ff_gate, m_w_ff_up, m_w_ff_down, m_ln2_g, m_ln2_b, v_w_in, v_b_gate, v_w_attn_br, v_w_ssm_br, v_w_out, v_ssm_a_re, v_ssm_a_im, v_ssm_log_dt, v_ssm_b_re, v_ssm_b_im, v_ssm_c_re, v_ssm_c_im, v_ssm_d, v_w_glu, v_ln1_g, v_ln1_b, v_w_ff_gate, v_w_ff_up, v_w_ff_down, v_ln2_g, v_ln2_b):
    given = dict(locals())
    x2, target = x[0], loss_target[0]
    core = lax.axis_index("c").astype(jnp.int32).reshape(1)

    sharded = ("w_in", "w_attn_br", "w_ssm_br", "w_glu", "w_ff_gate", "w_ff_up", "b_gate", "w_out", "w_ff_down")
    send_shape = dict(w_in=(D_MODEL, 896), w_attn_br=(ATTN_WIDTH, 128), w_ssm_br=(SSM_WIDTH, 128), w_glu=(SSM_WIDTH, 128),
                      w_out=(128, D_MODEL), w_ff_gate=(D_MODEL, FF_PAD), w_ff_up=(D_MODEL, FF_PAD), w_ff_down=(FF_PAD, D_MODEL))
    local = {k: given[k][0] for k in sharded}
    sends = [local[k] if k == "b_gate" else _send_buffer(local[k], *send_shape[k], name="send_" + k) for k in sharded]
    wt = dict(zip(sharded, _all_gather(sends, "gather_weights")))
    wt["w_out"] = wt["w_out"].reshape(D_MODEL, D_MODEL)
    wt["w_ff_down"] = wt["w_ff_down"].reshape(D_FF_PAD, D_MODEL)
    b_gate_full = wt["b_gate"]

    a_re, a_im, log_dt = ssm_a_re[0], ssm_a_im[0], ssm_log_dt[0].reshape(SSM_GROUPS, 1)
    b_re_t, b_im_t = ssm_b_re[0].transpose(0, 2, 1), ssm_b_im[0].transpose(0, 2, 1)
    abar_re, abar_im, e_re, e_im, bbar_re_t, bbar_im_t = _ssm_prep(a_re, a_im, log_dt, b_re_t, b_im_t)
    bmat, cmat, a_chunks = _ssm_tables(abar_re, abar_im, bbar_re_t, bbar_im_t, ssm_c_re[0], ssm_c_im[0])
    cos_t, sin_t = _rope_tables()

    proj = _proj(x2, wt["w_in"])
    attn, lse = _attn_fwd(proj, cos_t, sin_t)
    ys, states = _ssm_fwd(proj, bmat, cmat, a_chunks, ssm_d)
    h, xhat1, rstd1, glu, y_attn, y_ssm = _mixer_out(attn, ys, proj, x2, wt["w_attn_br"], wt["w_ssm_br"], wt["w_glu"],
                                                      wt["w_out"], b_gate_full, ln1_g, ln1_b)
    ff_a, ff_b, ff_f = _ff_up(h, wt["w_ff_gate"], wt["w_ff_up"])
    dr2, d_ln2_g, d_ln2_b, loss_lanes = _ff_down_loss(ff_f, wt["w_ff_down"], h, target, ln2_g, ln2_b)

    d_a, d_b = _ff_down_bwd(dr2, wt["w_ff_down"], ff_a, ff_b)
    dr1, d_ln1_g, d_ln1_b = _ff_up_bwd(d_a, d_b, wt["w_ff_gate"], wt["w_ff_up"], dr2, xhat1, rstd1, ln1_g)
    d_ya, d_yssm, d_proj, d_attn, d_glu, d_ys, mixed, y_s, gy, d_bg = _mixer_bwd(
        dr1, proj, y_attn, y_ssm, glu, ys, wt["w_attn_br"], wt["w_ssm_br"], wt["w_glu"], wt["w_out"], b_gate_full)
    for g in range(3):
        d_proj = _attn_bwd_group(g, proj, cos_t, sin_t, attn, lse, d_attn, d_proj)
    d_proj, d_bmat, d_cmat, d_abar, d_skip = _ssm_bwd(d_ys, proj, states, bmat, cmat, a_chunks, ssm_d, d_proj)
    grad_x = _grad_x(d_proj, wt["w_in"], dr1)

    contrib = dict(
        w_in=_weight_grad(x2, d_proj, "wgrad_w_in", 896),
        w_attn_br=_weight_grad(attn, d_ya, "wgrad_w_attn_br", 128),
        w_ssm_br=_weight_grad(y_s, d_yssm, "wgrad_w_ssm_br", 128),
        w_glu=_weight_grad(gy, d_glu, "wgrad_w_glu", 128),
        w_out=_weight_grad(mixed, dr1, "wgrad_w_out"),
        w_ff_gate=_weight_grad(h, d_a, "wgrad_w_ff_gate", FF_PAD),
        w_ff_up=_weight_grad(h, d_b, "wgrad_w_ff_up", FF_PAD),
        w_ff_down=_weight_grad(ff_f, dr2, "wgrad_w_ff_down"),
        b_gate=d_bg.reshape(2, 4, 2, 128).transpose(2, 1, 0, 3),
    )
    from_sibling = _swap_with_sibling([contrib[k] for k in sharded], "swap_with_sibling")
    chip_sums = [_pair_sum(contrib[k], r, core, "pair_sum_" + k) for k, r in zip(sharded, from_sibling)]
    parts = dict(zip(sharded, _swap_between_chips(chip_sums, "swap_between_chips")))

    grads, deltas, new_m, new_v = {}, {}, {}, {}
    for k in sharded:
        w2 = local[k]
        out = _adamw(w2, given["m_" + k][0], given["v_" + k][0], parts[k], "adamw_" + k)
        grads[k], deltas[k], new_m[k], new_v[k] = (o.reshape((1,) + w2.shape) for o in out)

    gbb_re_t, gbb_im_t = _block_diag_parts(d_bmat, True)
    gc_re, gc_im = _block_diag_parts(d_cmat, False)
    ga_re = d_abar[:, 0, :CHUNK_STATES].reshape(SSM_GROUPS, SSM_STATE)
    ga_im = d_abar[:, 0, CHUNK_STATES:].reshape(SSM_GROUPS, SSM_STATE)
    g_a_re, g_a_im, g_log_dt, g_b_re_t, g_b_im_t = _ssm_param_bwd(
        a_re, a_im, log_dt, b_re_t, b_im_t, abar_re, abar_im, e_re, e_im, ga_re, ga_im, gbb_re_t, gbb_im_t)
    mine = [g_a_re, g_a_im, g_log_dt, g_b_re_t.transpose(0, 2, 1), g_b_im_t.transpose(0, 2, 1), gc_re, -gc_im,
            d_skip, d_ln1_g, d_ln1_b, d_ln2_g, d_ln2_b]
    every, = _all_gather([_pack_rows(mine + [loss_lanes])], "gather_small_grads")
    *small_grads, loss_sum = _unpack_rows(_sum_devices(every), [given[k].shape for k in SMALL] + [(1, 128)])
    small = _adamw_replicated([given[k] for k in SMALL], [given["m_" + k] for k in SMALL],
                              [given["v_" + k] for k in SMALL], small_grads)
    for res, values in zip((grads, deltas, new_m, new_v), (small_grads,) + small):
        res.update(zip(SMALL, values))
    loss = loss_sum[0, 0]

    order = ("w_in", "b_gate", "w_attn_br", "w_ssm_br", "w_out", "ssm_a_re", "ssm_a_im", "ssm_log_dt", "ssm_b_re", "ssm_b_im",
             "ssm_c_re", "ssm_c_im", "ssm_d", "w_glu", "ln1_g", "ln1_b", "w_ff_gate", "w_ff_up", "w_ff_down", "ln2_g", "ln2_b")
    return (loss, grad_x[None], *[grads[k] for k in order], *[deltas[k] for k in order], *[new_m[k] for k in order],
            *[new_v[k] for k in order])
```

```python
import functools
import math

import jax
import jax.numpy as jnp
from jax import lax
from jax.experimental import pallas as pl
from jax.experimental.pallas import tpu as pltpu

F32 = jnp.float32
BF16 = jnp.bfloat16

N_DEV = 8
SEQ = 2048
D_MODEL = 1024
HEAD_DIM = 64
ATTN_WIDTH = 512
QKV_WIDTH = 1536
SSM_WIDTH = 512
SSM_GROUPS = 32
SSM_GROUP = 16
SSM_STATE = 64
IN_WIDTH = 7168
D_FF = 2816
FF_SHARD = D_FF // N_DEV
FF_PAD = 384
D_FF_PAD = FF_PAD * N_DEV
DN_ALPHA = 2.0 ** 0.25
LN_EPS = 1e-5
NEG_INF = -1e30
ROPE_THETA = 10000.0
BLOCK = 128
GROUPS = ((1, 16), (4, 4), (16, 1))

ADAM_LR = 0.001
ADAM_B1 = 0.9
ADAM_B2 = 0.999
ADAM_EPS = 1e-08
ADAM_WD = 0.01
ADAM_STEP = 10

VMEM_LIMIT = 56 * 1024 * 1024


_pallas_call = pl.pallas_call


def _cparams(**kw):
    return pltpu.CompilerParams(vmem_limit_bytes=VMEM_LIMIT, **kw)


def _dot(a, b):
    return jnp.dot(a, b, preferred_element_type=F32)


def _dot_nt(a, b):
    return lax.dot_general(a, b, (((1,), (1,)), ((), ())), preferred_element_type=F32)


def _side_by_side(w_ref, row=None):
    rows = slice(None) if row is None else pl.ds(row, 1)
    return jnp.concatenate([w_ref[i, rows, :] for i in range(w_ref.shape[0])], axis=1)


def _dot_tn(a, b):
    return lax.dot_general(a, b, (((0,), (0,)), ((), ())), preferred_element_type=F32)


def _rope_tables():
    half = HEAD_DIM // 2
    inv_freq = ROPE_THETA ** (-jnp.arange(half, dtype=F32) / half)
    ang = jnp.arange(SEQ, dtype=F32)[:, None] * inv_freq[None, :]
    cos, sin = jnp.cos(ang), jnp.sin(ang)
    return jnp.tile(cos, (1, 4)), jnp.tile(jnp.concatenate([-sin, sin], axis=1), (1, 2))


def _swap_halves(x):
    lane = lax.broadcasted_iota(jnp.int32, x.shape, 1)
    return jnp.where((lane & 63) < 32, pltpu.roll(x, 96, axis=1), pltpu.roll(x, 32, axis=1))


def _group_rows(d, nb, r, i):
    src = pl.ds(i * BLOCK, BLOCK) if d == 1 else pl.ds(r + i * BLOCK * d, BLOCK, stride=d)
    return src, pl.ds((r * nb + i) * BLOCK, BLOCK)


def _attn_masks():
    a_idx = lax.broadcasted_iota(jnp.int32, (2 * BLOCK, 2 * BLOCK), 0) & (BLOCK - 1)
    c_idx = lax.broadcasted_iota(jnp.int32, (2 * BLOCK, 2 * BLOCK), 1)
    cur_ok = jnp.logical_and(c_idx >= BLOCK, c_idx - BLOCK <= a_idx)
    prev_ok = jnp.logical_and(c_idx < BLOCK, c_idx >= a_idx)
    lane = lax.broadcasted_iota(jnp.int32, (BLOCK, 128), 1)
    return cur_ok, prev_ok, lane < HEAD_DIM


def _stack_heads(t, head0):
    zero = jnp.zeros_like(t)
    return jnp.concatenate([jnp.where(head0, t, zero), jnp.where(head0, zero, t)], axis=0)


def _unstack_heads(t2, head0):
    return jnp.where(head0, t2[:BLOCK], t2[BLOCK:])


def _attn_fwd(proj, cos_t, sin_t, ride=None):
    def body(q0, q1, q2, k0, k1, k2, v0, v1, v2, cos_ref, sin_ref, attn_ref, lse_ref,
             qs, ks, vs, os_, ms, ls, acc, mnat, lnat):
        cur_ok, prev_ok, head0 = _attn_masks()
        ks[:BLOCK, :] = jnp.zeros((BLOCK, 128), BF16)
        vs[:BLOCK, :] = jnp.zeros((BLOCK, 128), BF16)
        for g, (d, nb) in enumerate(GROUPS):
            q_ref, k_ref, v_ref = (q0, q1, q2)[g], (k0, k1, k2)[g], (v0, v1, v2)[g]
            for r in range(d):
                for i in range(nb):
                    src, dst = _group_rows(d, nb, r, i)
                    below = pl.ds(dst.start + BLOCK, BLOCK)
                    c, s = cos_ref[src, :], sin_ref[src, :]
                    q = q_ref[src, :]
                    k = k_ref[src, :]
                    qs[dst, :] = ((q * c + _swap_halves(q) * s) * 0.125).astype(BF16)
                    ks[below, :] = (k * c + _swap_halves(k) * s).astype(BF16)
                    vs[below, :] = v_ref[src, :].astype(BF16)

            def block(b, carry, nb=nb):
                has_prev = (b & (nb - 1)) > 0
                cur = pl.ds(pl.multiple_of(b * BLOCK, BLOCK), BLOCK)
                window = pl.ds(pl.multiple_of(b * BLOCK, BLOCK), 2 * BLOCK)
                valid = jnp.logical_or(cur_ok, jnp.logical_and(prev_ok, has_prev))
                s = jnp.where(valid, _dot_nt(_stack_heads(qs[cur, :], head0), ks[window, :]), NEG_INF)
                m = jnp.max(s, axis=1, keepdims=True)
                p = jnp.exp(s - m)
                os_[cur, :] = _unstack_heads(_dot(p.astype(BF16), vs[window, :]), head0)
                ms[cur, :] = _unstack_heads(m, head0)
                ls[cur, :] = _unstack_heads(jnp.sum(p, axis=1, keepdims=True), head0)
                return carry

            lax.fori_loop(0, SEQ // BLOCK, block, 0, unroll=2)

            for r in range(d):
                for i in range(nb):
                    src, dst = _group_rows(d, nb, r, i)
                    if g == 0:
                        acc[src, :], mnat[src, :], lnat[src, :] = os_[dst, :], ms[dst, :], ls[dst, :]
                    else:
                        m_old, m_g = mnat[src, :], ms[dst, :]
                        m_new = jnp.maximum(m_old, m_g)
                        a_old, a_g = jnp.exp(m_old - m_new), jnp.exp(m_g - m_new)
                        acc[src, :] = a_old * acc[src, :] + a_g * os_[dst, :]
                        lnat[src, :] = a_old * lnat[src, :] + a_g * ls[dst, :]
                        mnat[src, :] = m_new
        for i in range(SEQ // BLOCK):
            rows = pl.ds(i * BLOCK, BLOCK)
            l = lnat[rows, :]
            attn_ref[rows, :] = acc[rows, :] / l
            lse_ref[rows, :] = mnat[rows, :] + jnp.log(l)

    def col(base):
        return pl.BlockSpec((SEQ, 128), lambda hp, base=base: (0, base + hp))

    in_specs = [col(g * 4) for g in range(3)] + [col(12 + g * 4) for g in range(3)] + [col(24 + g * 4) for g in range(3)]
    table = pl.BlockSpec((SEQ, 128), lambda hp: (0, 0))
    out = pl.BlockSpec((SEQ, 128), lambda hp: (0, hp))
    return _call(
        body, "attn_fwd", (4,), in_specs + [table, table], [out, out],
        [_sds((SEQ, ATTN_WIDTH), F32), _sds((SEQ, ATTN_WIDTH), F32)],
        [pltpu.VMEM((SEQ, 128), BF16)] + [pltpu.VMEM((SEQ + BLOCK, 128), BF16)] * 2 + [pltpu.VMEM((SEQ, 128), F32)] * 6,
        [proj] * 9 + [cos_t, sin_t], ride)


def _attn_bwd_group(g, proj, cos_t, sin_t, attn, lse, dattn, dproj):
    d, nb = GROUPS[g]

    def body(q_ref, k_ref, v_ref, cos_ref, sin_ref, attn_ref, lse_ref, dattn_ref, dproj_in, dproj_ref,
             qs, ks, vs, dos, lss, dss, dqs, dks, dvs, stage, outs, sems):
        del dproj_in
        cur_ok, prev_ok, head0 = _attn_masks()
        ks[:BLOCK, :] = jnp.zeros((BLOCK, 128), BF16)
        vs[:BLOCK, :] = jnp.zeros((BLOCK, 128), BF16)
        dks[:BLOCK, :] = jnp.zeros((BLOCK, 128), F32)
        dvs[:BLOCK, :] = jnp.zeros((BLOCK, 128), F32)
        for r in range(d):
            for i in range(nb):
                src, dst = _group_rows(d, nb, r, i)
                below = pl.ds(dst.start + BLOCK, BLOCK)
                c, s = cos_ref[src, :], sin_ref[src, :]
                q = q_ref[src, :]
                k = k_ref[src, :]
                qs[dst, :] = ((q * c + _swap_halves(q) * s) * 0.125).astype(BF16)
                ks[below, :] = (k * c + _swap_halves(k) * s).astype(BF16)
                vs[below, :] = v_ref[src, :].astype(BF16)
                do = dattn_ref[src, :]
                prod = do * attn_ref[src, :]
                d0 = jnp.sum(jnp.where(head0, prod, 0.0), axis=1, keepdims=True)
                d1 = jnp.sum(jnp.where(head0, 0.0, prod), axis=1, keepdims=True)
                dos[dst, :] = do.astype(BF16)
                dss[dst, :] = jnp.where(head0, d0, d1)
                lss[dst, :] = lse_ref[src, :]
                dks[below, :] = jnp.zeros((BLOCK, 128), F32)
                dvs[below, :] = jnp.zeros((BLOCK, 128), F32)

        def per_head_column(t):
            return jnp.concatenate([jnp.max(jnp.where(head0, t, NEG_INF), axis=1, keepdims=True),
                                    jnp.max(jnp.where(head0, NEG_INF, t), axis=1, keepdims=True)], axis=0)

        def block(b, carry):
            has_prev = (b & (nb - 1)) > 0
            cur = pl.ds(pl.multiple_of(b * BLOCK, BLOCK), BLOCK)
            window = pl.ds(pl.multiple_of(b * BLOCK, BLOCK), 2 * BLOCK)
            valid = jnp.logical_or(cur_ok, jnp.logical_and(prev_ok, has_prev))
            q2, do2 = _stack_heads(qs[cur, :], head0), _stack_heads(dos[cur, :], head0)
            kw, vw = ks[window, :], vs[window, :]
            s = jnp.where(valid, _dot_nt(q2, kw), NEG_INF)
            p = jnp.exp(s - per_head_column(lss[cur, :]))
            ds = (p * (_dot_nt(do2, vw) - per_head_column(dss[cur, :]))).astype(BF16)
            dvs[window, :] += _dot_tn(p.astype(BF16), do2)
            dks[window, :] += _dot_tn(ds, q2)
            dqs[cur, :] = _unstack_heads(_dot(ds, kw), head0)
            return carry

        lax.fori_loop(0, SEQ // BLOCK, block, 0, unroll=2)

        hp = pl.program_id(0)
        copies = []
        for kind in range(3):
            for r in range(d):
                for i in range(nb):
                    src, dst = _group_rows(d, nb, r, i)
                    below = pl.ds(dst.start + BLOCK, BLOCK)
                    if kind == 2:
                        stage[src, :] = dvs[below, :]
                    else:
                        c, s = cos_ref[src, :], sin_ref[src, :]
                        t = dqs[dst, :] * 0.125 if kind == 0 else dks[below, :]
                        stage[src, :] = t * c - _swap_halves(t) * s
            for i in range(SEQ // MM_ROWS):
                rows = pl.ds(i * MM_ROWS, MM_ROWS)
                outs[kind, rows, :] = stage[rows, :].astype(BF16)
            column = pl.multiple_of((kind * 12 + g * 4 + hp) * 128, 128)
            copies.append(pltpu.make_async_copy(outs.at[kind], dproj_ref.at[:, pl.ds(column, 128)], sems.at[kind]))
            copies[-1].start()
        for cp in copies:
            cp.wait()

    def col(base):
        return pl.BlockSpec((SEQ, 128), lambda hp, base=base: (0, base + hp))

    table = pl.BlockSpec((SEQ, 128), lambda hp: (0, 0))
    return _pallas_call(
        body, name=f"attn_bwd_g{g}", grid=(4,),
        in_specs=[col(g * 4), col(12 + g * 4), col(24 + g * 4), table, table, col(0), col(0), col(0), ANY],
        out_specs=ANY, out_shape=_sds((SEQ, IN_WIDTH), BF16), input_output_aliases={8: 0},
        scratch_shapes=[pltpu.VMEM((SEQ, 128), BF16)] + [pltpu.VMEM((SEQ + BLOCK, 128), BF16)] * 2 + [pltpu.VMEM((SEQ, 128), BF16)]
        + [pltpu.VMEM((SEQ, 128), F32)] * 3 + [pltpu.VMEM((SEQ + BLOCK, 128), F32)] * 2 + [pltpu.VMEM((SEQ, 128), F32)]
        + [pltpu.VMEM((3, SEQ, 128), BF16), pltpu.SemaphoreType.DMA((3,))],
        compiler_params=_cparams(dimension_semantics=("arbitrary",)),
    )(proj, proj, proj, cos_t, sin_t, attn, lse, dattn, dproj)


SSM_CHUNKS = 4
CHUNK_STATES = 512
SCAN_ROWS = 8
U_COL = (3 * QKV_WIDTH) // 128


def _cmul(xr, xi, yr, yi):
    return xr * yr - xi * yi, xr * yi + xi * yr


def _ssm_prep(a_re, a_im, log_dt, b_re_t, b_im_t):
    def body(ar_ref, ai_ref, ldt_ref, br_ref, bi_ref, abr_ref, abi_ref, er_ref, ei_ref, bbr_ref, bbi_ref):
        ar, ai = ar_ref[...], ai_ref[...]
        dt = jnp.exp(ldt_ref[...])
        mag = jnp.exp(ar * dt)
        abr, abi = mag * jnp.cos(ai * dt), mag * jnp.sin(ai * dt)
        den = ar * ar + ai * ai
        nr, ni = abr - 1.0, abi
        er, ei = (nr * ar + ni * ai) / den, (ni * ar - nr * ai) / den
        abr_ref[...], abi_ref[...], er_ref[...], ei_ref[...] = abr, abi, er, ei
        er3, ei3 = er[:, None, :], ei[:, None, :]
        br, bi = br_ref[...], bi_ref[...]
        bbr_ref[...] = er3 * br - ei3 * bi
        bbi_ref[...] = er3 * bi + ei3 * br

    gp = jax.ShapeDtypeStruct(a_re.shape, F32)
    gb = jax.ShapeDtypeStruct(b_re_t.shape, F32)
    return _pallas_call(body, name="ssm_prep", out_shape=(gp, gp, gp, gp, gb, gb))(a_re, a_im, log_dt, b_re_t, b_im_t)


def _ssm_param_bwd(a_re, a_im, log_dt, b_re_t, b_im_t, abar_re, abar_im, e_re, e_im, ga_re, ga_im, gbb_re_t, gbb_im_t):
    def body(ar_ref, ai_ref, ldt_ref, br_ref, bi_ref, abr_ref, abi_ref, er_ref, ei_ref, gar_ref, gai_ref, gbr_ref, gbi_ref,
             o_ar, o_ai, o_ldt, o_br, o_bi):
        ar, ai = ar_ref[...], ai_ref[...]
        dt = jnp.exp(ldt_ref[...])
        er, ei = er_ref[...], ei_ref[...]
        br, bi, gbr, gbi = br_ref[...], bi_ref[...], gbr_ref[...], gbi_ref[...]
        er3, ei3 = er[:, None, :], ei[:, None, :]
        o_br[...] = er3 * gbr + ei3 * gbi
        o_bi[...] = er3 * gbi - ei3 * gbr
        ge_r = jnp.sum(br * gbr + bi * gbi, axis=1)
        ge_i = jnp.sum(br * gbi - bi * gbr, axis=1)
        den = ar * ar + ai * ai
        ilr, ili = ar / den, -ai / den
        t_r, t_i = _cmul(ilr, -ili, ge_r, ge_i)
        gab_r, gab_i = gar_ref[...] + t_r, gai_ref[...] + t_i
        gz_r, gz_i = _cmul(abr_ref[...], -abi_ref[...], gab_r, gab_i)
        el_r, el_i = _cmul(er, ei, ilr, ili)
        u_r, u_i = _cmul(el_r, -el_i, ge_r, ge_i)
        o_ar[...] = dt * gz_r - u_r
        o_ai[...] = dt * gz_i - u_i
        o_ldt[...] = jnp.sum(gz_r * ar + gz_i * ai, axis=1, keepdims=True) * dt

    gp = jax.ShapeDtypeStruct(a_re.shape, F32)
    gb = jax.ShapeDtypeStruct(b_re_t.shape, F32)
    return _pallas_call(body, name="ssm_param_bwd", out_shape=(gp, gp, jax.ShapeDtypeStruct(log_dt.shape, F32), gb, gb))(
        a_re, a_im, log_dt, b_re_t, b_im_t, abar_re, abar_im, e_re, e_im, ga_re, ga_im, gbb_re_t, gbb_im_t)


def _block_diag(blocks_re, blocks_im, sign_im, rows_are_channels):
    both = jnp.stack([blocks_re, sign_im * blocks_im]).reshape(2, SSM_CHUNKS, 8, SSM_GROUP, SSM_STATE)
    eye = jnp.eye(8, dtype=F32)
    if rows_are_channels:
        return jnp.einsum("rcghp,gk->cghrkp", both, eye).reshape(SSM_CHUNKS, 128, 2 * CHUNK_STATES)
    return jnp.einsum("rcghp,gk->crkpgh", both, eye).reshape(SSM_CHUNKS, 2 * CHUNK_STATES, 128)


def _block_diag_parts(mat, rows_are_channels):
    if rows_are_channels:
        six = mat.reshape(SSM_CHUNKS, 8, SSM_GROUP, 2, 8, SSM_STATE)
        parts = jnp.einsum("cghrgp->rcghp", six)
    else:
        six = mat.reshape(SSM_CHUNKS, 2, 8, SSM_STATE, 8, SSM_GROUP)
        parts = jnp.einsum("crgpgh->rcghp", six)
    parts = parts.reshape(2, SSM_GROUPS, SSM_GROUP, SSM_STATE)
    return parts[0], parts[1]


def _scan_consts(a_ref, conj, reverse):
    ar = jnp.broadcast_to(a_ref[:, :CHUNK_STATES], (SCAN_ROWS, CHUNK_STATES))
    ai = jnp.broadcast_to(a_ref[:, CHUNK_STATES:], (SCAN_ROWS, CHUNK_STATES))
    if conj:
        ai = -ai
    row = lax.broadcasted_iota(jnp.int32, (SCAN_ROWS, CHUNK_STATES), 0)
    if reverse:
        row = SCAN_ROWS - 1 - row
    zero = jnp.zeros_like(ar)
    steps = []
    pr, pi = ar, ai
    for shift in (1, 2, 4):
        keep = row >= shift
        steps.append((SCAN_ROWS - shift if reverse else shift, jnp.where(keep, pr, zero), jnp.where(keep, pi, zero)))
        pr, pi = _cmul(pr, pi, pr, pi)
    first = row == 0
    return steps, (jnp.where(first, ar, zero), jnp.where(first, ai, zero)), first


def _scan_tile(xr, xi, prev_r, prev_i, steps, carry_in, reverse):
    edge = SCAN_ROWS - 1 if reverse else 1
    cr, ci = pltpu.roll(prev_r, edge, axis=0), pltpu.roll(prev_i, edge, axis=0)
    xr, xi = xr + carry_in[0] * cr - carry_in[1] * ci, xi + carry_in[0] * ci + carry_in[1] * cr
    for shift, mr, mi in steps:
        sr, si = pltpu.roll(xr, shift, axis=0), pltpu.roll(xi, shift, axis=0)
        xr, xi = xr + mr * sr - mi * si, xi + mr * si + mi * sr
    return xr, xi


MM_ROWS = 256


def _ssm_fwd(proj, bmat, cmat, a_chunks, d_skip, ride=None):
    def body(u_ref, b_ref, c_ref, a_ref, d_ref, y_ref, h_ref):
        for i in range(SEQ // MM_ROWS):
            rows = pl.ds(i * MM_ROWS, MM_ROWS)
            h_ref[rows, :] = _dot(u_ref[rows, :].astype(BF16), b_ref[...])
        steps, carry_in, _ = _scan_consts(a_ref, conj=False, reverse=False)

        def tile(k, carry):
            rows = pl.ds(pl.multiple_of(k * SCAN_ROWS, SCAN_ROWS), SCAN_ROWS)
            xr, xi = _scan_tile(h_ref[rows, :CHUNK_STATES], h_ref[rows, CHUNK_STATES:], carry[0], carry[1], steps, carry_in, False)
            h_ref[rows, :CHUNK_STATES] = xr
            h_ref[rows, CHUNK_STATES:] = xi
            return xr, xi

        zero = jnp.zeros((SCAN_ROWS, CHUNK_STATES), F32)
        lax.fori_loop(0, SEQ // SCAN_ROWS, tile, (zero, zero))
        for i in range(SEQ // MM_ROWS):
            rows = pl.ds(i * MM_ROWS, MM_ROWS)
            y_ref[rows, :] = _dot(h_ref[rows, :].astype(BF16), c_ref[...]) + d_ref[...] * u_ref[rows, :]

    return _call(
        body, "ssm_fwd", (SSM_CHUNKS,),
        [pl.BlockSpec((SEQ, 128), lambda c: (0, U_COL + c)),
         pl.BlockSpec((None, 128, 2 * CHUNK_STATES), lambda c: (c, 0, 0)),
         pl.BlockSpec((None, 2 * CHUNK_STATES, 128), lambda c: (c, 0, 0)),
         pl.BlockSpec((None, 1, 2 * CHUNK_STATES), lambda c: (c, 0, 0)),
         pl.BlockSpec((1, 128), lambda c: (0, c))],
        [pl.BlockSpec((SEQ, 128), lambda c: (0, c)), pl.BlockSpec((SEQ, 2 * CHUNK_STATES), lambda c: (0, c))],
        [_sds((SEQ, SSM_WIDTH), F32), _sds((SEQ, SSM_CHUNKS * 2 * CHUNK_STATES), F32)], [],
        [proj, bmat, cmat, a_chunks, d_skip], ride)


def _ssm_bwd(dys, proj, h, bmat, cmat, a_chunks, d_skip, dproj):
    def body(dy_ref, u_ref, h_ref, b_ref, c_ref, a_ref, d_ref, dproj_in, du_ref, db_ref, dc_ref, da_ref, dd_ref, g_ref):
        del dproj_in
        dsum = jnp.zeros((1, 128), F32)
        dcm = jnp.zeros((2 * CHUNK_STATES, 128), F32)
        for i in range(SEQ // MM_ROWS):
            rows = pl.ds(i * MM_ROWS, MM_ROWS)
            dy = dy_ref[rows, :]
            g_ref[rows, :] = _dot_nt(dy.astype(BF16), c_ref[...])
            dsum += jnp.sum(dy * u_ref[rows, :], axis=0, keepdims=True)
            dcm += _dot_tn(h_ref[rows, :].astype(BF16), dy.astype(BF16))
        dd_ref[...] = dsum
        dc_ref[...] = dcm
        steps, carry_in, _ = _scan_consts(a_ref, conj=True, reverse=True)
        first_row = lax.broadcasted_iota(jnp.int32, (SCAN_ROWS, CHUNK_STATES), 0) == 0
        n_tiles = SEQ // SCAN_ROWS

        def tile(j, carry):
            k = n_tiles - 1 - j
            rows = pl.ds(pl.multiple_of(k * SCAN_ROWS, SCAN_ROWS), SCAN_ROWS)
            before = pl.ds(pl.multiple_of(jnp.maximum(k - 1, 0) * SCAN_ROWS, SCAN_ROWS), SCAN_ROWS)
            gr, gi = _scan_tile(g_ref[rows, :CHUNK_STATES], g_ref[rows, CHUNK_STATES:], carry[0], carry[1], steps, carry_in, True)
            g_ref[rows, :CHUNK_STATES] = gr
            g_ref[rows, CHUNK_STATES:] = gi
            has_before = jnp.where(k > 0, 1.0, 0.0)
            hr = jnp.where(first_row, pltpu.roll(h_ref[before, :CHUNK_STATES], 1, axis=0) * has_before,
                           pltpu.roll(h_ref[rows, :CHUNK_STATES], 1, axis=0))
            hi = jnp.where(first_row, pltpu.roll(h_ref[before, CHUNK_STATES:], 1, axis=0) * has_before,
                           pltpu.roll(h_ref[rows, CHUNK_STATES:], 1, axis=0))
            return gr, gi, carry[2] + hr * gr + hi * gi, carry[3] + hr * gi - hi * gr

        zero = jnp.zeros((SCAN_ROWS, CHUNK_STATES), F32)
        _, _, sar, sai = lax.fori_loop(0, n_tiles, tile, (zero, zero, zero, zero))
        da_ref[:, :CHUNK_STATES] = jnp.sum(sar, axis=0, keepdims=True)
        da_ref[:, CHUNK_STATES:] = jnp.sum(sai, axis=0, keepdims=True)
        dbm = jnp.zeros((128, 2 * CHUNK_STATES), F32)
        for i in range(SEQ // MM_ROWS):
            rows = pl.ds(i * MM_ROWS, MM_ROWS)
            g = g_ref[rows, :].astype(BF16)
            du_ref[rows, :] = (_dot_nt(g, b_ref[...]) + d_ref[...] * dy_ref[rows, :]).astype(BF16)
            dbm += _dot_tn(u_ref[rows, :].astype(BF16), g)
        db_ref[...] = dbm

    chunk_col = pl.BlockSpec((SEQ, 128), lambda c: (0, c))
    return _pallas_call(
        body, name="ssm_bwd", grid=(SSM_CHUNKS,),
        in_specs=[chunk_col,
                  pl.BlockSpec((SEQ, 128), lambda c: (0, U_COL + c)),
                  pl.BlockSpec((SEQ, 2 * CHUNK_STATES), lambda c: (0, c)),
                  pl.BlockSpec((None, 128, 2 * CHUNK_STATES), lambda c: (c, 0, 0)),
                  pl.BlockSpec((None, 2 * CHUNK_STATES, 128), lambda c: (c, 0, 0)),
                  pl.BlockSpec((None, 1, 2 * CHUNK_STATES), lambda c: (c, 0, 0)),
                  pl.BlockSpec((1, 128), lambda c: (0, c)), ANY],
        out_specs=(pl.BlockSpec((SEQ, 128), lambda c: (0, U_COL + c)),
                   pl.BlockSpec((None, 128, 2 * CHUNK_STATES), lambda c: (c, 0, 0)),
                   pl.BlockSpec((None, 2 * CHUNK_STATES, 128), lambda c: (c, 0, 0)),
                   pl.BlockSpec((None, 1, 2 * CHUNK_STATES), lambda c: (c, 0, 0)),
                   pl.BlockSpec((1, 128), lambda c: (0, c))),
        input_output_aliases={7: 0},
        out_shape=(jax.ShapeDtypeStruct((SEQ, IN_WIDTH), BF16),
                   jax.ShapeDtypeStruct((SSM_CHUNKS, 128, 2 * CHUNK_STATES), F32),
                   jax.ShapeDtypeStruct((SSM_CHUNKS, 2 * CHUNK_STATES, 128), F32),
                   jax.ShapeDtypeStruct((SSM_CHUNKS, 1, 2 * CHUNK_STATES), F32),
                   jax.ShapeDtypeStruct((1, SSM_WIDTH), F32)),
        scratch_shapes=[pltpu.VMEM((SEQ, 2 * CHUNK_STATES), F32)],
        compiler_params=_cparams(dimension_semantics=("arbitrary",)),
    )(dys, proj, h, bmat, cmat, a_chunks, d_skip, dproj)


def _ssm_tables(abar_re, abar_im, bbar_re_t, bbar_im_t, c_re, c_im):
    bmat = _block_diag(bbar_re_t, bbar_im_t, 1.0, True).astype(BF16)
    cmat = _block_diag(c_re, c_im, -1.0, False).astype(BF16)
    a_chunks = jnp.concatenate([abar_re.reshape(SSM_CHUNKS, 1, CHUNK_STATES), abar_im.reshape(SSM_CHUNKS, 1, CHUNK_STATES)], axis=2)
    return bmat, cmat, a_chunks


GL_COL = (3 * QKV_WIDTH + SSM_WIDTH) // D_MODEL
GELU_C = math.sqrt(2.0 / math.pi)
GELU_A = 0.044715


def _sds(shape, dtype):
    return jax.ShapeDtypeStruct(shape, dtype)


def _gelu(x):
    t = jnp.tanh(GELU_C * (x + GELU_A * x * x * x))
    return 0.5 * x * (1.0 + t), t


def _gelu_grad(x, t):
    return 0.5 * (1.0 + t) + 0.5 * x * (1.0 - t * t) * GELU_C * (1.0 + 3.0 * GELU_A * x * x)


def _layer_norm(r, g, b):
    mu = jnp.mean(r, axis=-1, keepdims=True)
    xc = r - mu
    rstd = lax.rsqrt(jnp.mean(xc * xc, axis=-1, keepdims=True) + LN_EPS)
    xhat = xc * rstd
    return xhat * g + b, xhat, rstd


def _layer_norm_bwd(dy, xhat, rstd, g):
    dxhat = dy * g
    m1 = jnp.mean(dxhat, axis=-1, keepdims=True)
    m2 = jnp.mean(dxhat * xhat, axis=-1, keepdims=True)
    return rstd * (dxhat - m1 - xhat * m2)


def _proj(x, w_in, ride=None):
    tm, tn = 512, 1792

    def body(x_ref, w_ref, o_ref):
        o_ref[...] = _dot(x_ref[...].astype(BF16), _side_by_side(w_ref))

    return _call(
        body, "proj", (SEQ // tm, IN_WIDTH // tn),
        [pl.BlockSpec((tm, D_MODEL), lambda i, j: (i, 0)), pl.BlockSpec((2, D_MODEL, tn // 2), lambda i, j: (j, 0, 0))],
        [pl.BlockSpec((tm, tn), lambda i, j: (i, j))], [_sds((SEQ, IN_WIDTH), F32)], [], [x, w_in], ride)


def _row_spec(tm, width, col=0):
    return pl.BlockSpec((tm, width), lambda i, col=col: (i, col))


def _full_spec(shape):
    return pl.BlockSpec(shape, lambda i: (0,) * len(shape))


def _mixer_out(attn, ys, proj, x, w_ab, w_sb, w_glu, w_out, b_gate, ln_g, ln_b):
    tm = 256

    def body(attn_ref, ys_ref, gl0_ref, gl1_ref, x_ref, wab_ref, wsb_ref, wglu_ref, wout_ref, bg_ref, g_ref, b_ref,
             h_ref, xhat_ref, rstd_ref, glu_ref, ya_ref, yssm_ref):
        gy, _ = _gelu(ys_ref[...])
        glu = _dot(gy.astype(BF16), _side_by_side(wglu_ref))
        glu_ref[...] = glu
        y_s = glu[:, :SSM_WIDTH] * jax.nn.sigmoid(glu[:, SSM_WIDTH:])
        y_ssm = _dot(y_s.astype(BF16), _side_by_side(wsb_ref))
        y_attn = _dot(attn_ref[...].astype(BF16), _side_by_side(wab_ref))
        ya_ref[...] = y_attn
        yssm_ref[...] = y_ssm
        g0 = jax.nn.sigmoid(gl0_ref[...] + _side_by_side(bg_ref, 0))
        g1 = jax.nn.sigmoid(gl1_ref[...] + _side_by_side(bg_ref, 1))
        mixed = g0 * y_attn + g1 * y_ssm
        r1 = DN_ALPHA * x_ref[...] + _dot(mixed.astype(BF16), wout_ref[...])
        h, xhat, rstd = _layer_norm(r1, g_ref[...], b_ref[...])
        h_ref[...] = h
        xhat_ref[...] = xhat
        rstd_ref[...] = jnp.broadcast_to(rstd, (tm, 128))

    wide = _sds((SEQ, D_MODEL), F32)
    return _pallas_call(
        body, name="mixer_out", grid=(SEQ // tm,),
        in_specs=[_row_spec(tm, ATTN_WIDTH), _row_spec(tm, SSM_WIDTH), _row_spec(tm, D_MODEL, GL_COL), _row_spec(tm, D_MODEL, GL_COL + 1),
                  _row_spec(tm, D_MODEL), _full_spec((N_DEV, ATTN_WIDTH, 128)), _full_spec((N_DEV, SSM_WIDTH, 128)),
                  _full_spec((N_DEV, SSM_WIDTH, 128)), _full_spec((D_MODEL, D_MODEL)), _full_spec((N_DEV, 2, 128)),
                  _full_spec((1, D_MODEL)), _full_spec((1, D_MODEL))],
        out_specs=(_row_spec(tm, D_MODEL), _row_spec(tm, D_MODEL), _row_spec(tm, 128), _row_spec(tm, D_MODEL),
                   _row_spec(tm, D_MODEL), _row_spec(tm, D_MODEL)),
        out_shape=(wide, wide, _sds((SEQ, 128), F32), wide, wide, wide),
        compiler_params=_cparams(dimension_semantics=("arbitrary",)),
    )(attn, ys, proj, proj, x, w_ab, w_sb, w_glu, w_out, b_gate, ln_g, ln_b)


def _ff_up(h, w_gate, w_up):
    tm, tn = 512, 768

    def body(h_ref, wg_ref, wu_ref, a_ref, b_ref, f_ref):
        hb = h_ref[...].astype(BF16)
        a, b = _dot(hb, _side_by_side(wg_ref)), _dot(hb, _side_by_side(wu_ref))
        a_ref[...] = a.astype(BF16)
        b_ref[...] = b.astype(BF16)
        f_ref[...] = (a * jax.nn.sigmoid(a) * b).astype(BF16)

    tile = pl.BlockSpec((tm, tn), lambda i, j: (i, j))
    wtile = pl.BlockSpec((tn // FF_PAD, D_MODEL, FF_PAD), lambda i, j: (j, 0, 0))
    out = _sds((SEQ, D_FF_PAD), BF16)
    return _pallas_call(
        body, name="ff_up", grid=(SEQ // tm, D_FF_PAD // tn),
        in_specs=[pl.BlockSpec((tm, D_MODEL), lambda i, j: (i, 0)), wtile, wtile],
        out_specs=(tile, tile, tile), out_shape=(out, out, out),
        compiler_params=_cparams(dimension_semantics=("arbitrary", "arbitrary")),
    )(h, w_gate, w_up)


def _ff_down_loss(f, w_down, h, target, ln_g, ln_b):
    tm = 256

    def body(f_ref, w_ref, h_ref, t_ref, g_ref, b_ref, dr_ref, dg_ref, db_ref, loss_ref):
        @pl.when(pl.program_id(0) == 0)
        def _():
            dg_ref[...] = jnp.zeros_like(dg_ref)
            db_ref[...] = jnp.zeros_like(db_ref)
            loss_ref[...] = jnp.zeros_like(loss_ref)

        r2 = DN_ALPHA * h_ref[...] + _dot(f_ref[...], w_ref[...])
        g = g_ref[...]
        out, xhat, rstd = _layer_norm(r2, g, b_ref[...])
        err = out - t_ref[...]
        loss_ref[...] += 0.5 * jnp.sum(jnp.mean(err * err, axis=-1, keepdims=True), axis=0, keepdims=True)
        dout = err * (1.0 / D_MODEL)
        dg_ref[...] += jnp.sum(dout * xhat, axis=0, keepdims=True)
        db_ref[...] += jnp.sum(dout, axis=0, keepdims=True)
        dr_ref[...] = _layer_norm_bwd(dout, xhat, rstd, g)

    vec = _sds((1, D_MODEL), F32)
    return _pallas_call(
        body, name="ff_down_loss", grid=(SEQ // tm,),
        in_specs=[_row_spec(tm, D_FF_PAD), _full_spec((D_FF_PAD, D_MODEL)), _row_spec(tm, D_MODEL), _row_spec(tm, D_MODEL),
                  _full_spec((1, D_MODEL)), _full_spec((1, D_MODEL))],
        out_specs=(_row_spec(tm, D_MODEL), _full_spec((1, D_MODEL)), _full_spec((1, D_MODEL)), _full_spec((1, 128))),
        out_shape=(_sds((SEQ, D_MODEL), F32), vec, vec, _sds((1, 128), F32)),
        compiler_params=_cparams(dimension_semantics=("arbitrary",)),
    )(f, w_down, h, target, ln_g, ln_b)


def _ff_down_bwd(dr2, w_down, a, b):
    tm, tn = 512, 768

    def body(dr_ref, w_ref, a_ref, b_ref, da_ref, db_ref):
        df = _dot_nt(dr_ref[...].astype(BF16), w_ref[...])
        av, bv = a_ref[...].astype(F32), b_ref[...].astype(F32)
        sg = jax.nn.sigmoid(av)
        da_ref[...] = (df * bv * sg * (1.0 + av * (1.0 - sg))).astype(BF16)
        db_ref[...] = (df * av * sg).astype(BF16)

    tile = pl.BlockSpec((tm, tn), lambda i, j: (i, j))
    out = _sds((SEQ, D_FF_PAD), BF16)
    return _pallas_call(
        body, name="ff_down_bwd", grid=(SEQ // tm, D_FF_PAD // tn),
        in_specs=[pl.BlockSpec((tm, D_MODEL), lambda i, j: (i, 0)), pl.BlockSpec((tn, D_MODEL), lambda i, j: (j, 0)), tile, tile],
        out_specs=(tile, tile), out_shape=(out, out),
        compiler_params=_cparams(dimension_semantics=("arbitrary", "arbitrary")),
    )(dr2, w_down, a, b)


def _ff_up_bwd(da, db, w_gate, w_up, dr2, xhat1, rstd1, ln_g):
    tm, tk = 512, 768
    nk = D_FF_PAD // tk

    def body(da_ref, db_ref, wg_ref, wu_ref, dr2_ref, xhat_ref, rstd_ref, g_ref, dr1_ref, dg_ref, dbias_ref, acc):
        i, k = pl.program_id(0), pl.program_id(1)

        @pl.when(jnp.logical_and(i == 0, k == 0))
        def _():
            dg_ref[...] = jnp.zeros_like(dg_ref)
            dbias_ref[...] = jnp.zeros_like(dbias_ref)

        part = _dot_nt(da_ref[...], _side_by_side(wg_ref)) + _dot_nt(db_ref[...], _side_by_side(wu_ref))

        @pl.when(k == 0)
        def _():
            acc[...] = part

        @pl.when(k > 0)
        def _():
            acc[...] += part

        @pl.when(k == nk - 1)
        def _():
            dh = DN_ALPHA * dr2_ref[...] + acc[...]
            xhat = xhat_ref[...]
            dg_ref[...] += jnp.sum(dh * xhat, axis=0, keepdims=True)
            dbias_ref[...] += jnp.sum(dh, axis=0, keepdims=True)
            rstd = jnp.max(rstd_ref[...], axis=1, keepdims=True)
            dr1_ref[...] = _layer_norm_bwd(dh, xhat, rstd, g_ref[...])

    hid = pl.BlockSpec((tm, tk), lambda i, k: (i, k))
    wtile = pl.BlockSpec((tk // FF_PAD, D_MODEL, FF_PAD), lambda i, k: (k, 0, 0))
    row = pl.BlockSpec((tm, D_MODEL), lambda i, k: (i, 0))
    vec = pl.BlockSpec((1, D_MODEL), lambda i, k: (0, 0))
    return _pallas_call(
        body, name="ff_up_bwd", grid=(SEQ // tm, nk),
        in_specs=[hid, hid, wtile, wtile, row, row, pl.BlockSpec((tm, 128), lambda i, k: (i, 0)), vec],
        out_specs=(row, vec, vec), out_shape=(_sds((SEQ, D_MODEL), F32), _sds((1, D_MODEL), F32), _sds((1, D_MODEL), F32)),
        scratch_shapes=[pltpu.VMEM((tm, D_MODEL), F32)],
        compiler_params=_cparams(dimension_semantics=("arbitrary", "arbitrary")),
    )(da, db, w_gate, w_up, dr2, xhat1, rstd1, ln_g)


def _mixer_bwd(dr1, proj, y_attn, y_ssm, glu, ys, w_ab, w_sb, w_glu, w_out, b_gate):
    tm = 256

    def body(dr1_ref, gl0_ref, gl1_ref, ya_ref, yssm_ref, glu_ref, ys_ref, wab_ref, wsb_ref, wglu_ref, wout_ref, bg_ref,
             dya_ref, dyssm_ref, dgl_ref, dattn_ref, dglu_ref, dys_ref, mixed_ref, ysb_ref, gy_ref, dbg_ref):
        @pl.when(pl.program_id(0) == 0)
        def _():
            dbg_ref[...] = jnp.zeros_like(dbg_ref)

        dmixed = _dot_nt(dr1_ref[...].astype(BF16), wout_ref[...])
        g0 = jax.nn.sigmoid(gl0_ref[...] + _side_by_side(bg_ref, 0))
        g1 = jax.nn.sigmoid(gl1_ref[...] + _side_by_side(bg_ref, 1))
        y_attn, y_ssm = ya_ref[...], yssm_ref[...]
        mixed_ref[...] = (g0 * y_attn + g1 * y_ssm).astype(BF16)
        dya = (dmixed * g0).astype(BF16)
        dyssm = (dmixed * g1).astype(BF16)
        dya_ref[...] = dya
        dyssm_ref[...] = dyssm
        dgl0 = dmixed * y_attn * g0 * (1.0 - g0)
        dgl1 = dmixed * y_ssm * g1 * (1.0 - g1)
        dgl_ref[:, :GL_COL * D_MODEL] = jnp.zeros((tm, GL_COL * D_MODEL), BF16)
        dgl_ref[:, GL_COL * D_MODEL:(GL_COL + 1) * D_MODEL] = dgl0.astype(BF16)
        dgl_ref[:, (GL_COL + 1) * D_MODEL:] = dgl1.astype(BF16)
        dbg_ref[:, :D_MODEL] += jnp.sum(dgl0, axis=0, keepdims=True)
        dbg_ref[:, D_MODEL:] += jnp.sum(dgl1, axis=0, keepdims=True)
        dattn_ref[...] = _dot_nt(dya, _side_by_side(wab_ref))
        dy_s = _dot_nt(dyssm, _side_by_side(wsb_ref))
        glu = glu_ref[...]
        glu1, sg = glu[:, :SSM_WIDTH], jax.nn.sigmoid(glu[:, SSM_WIDTH:])
        ysb_ref[...] = (glu1 * sg).astype(BF16)
        dglu1 = (dy_s * sg).astype(BF16)
        dglu2 = (dy_s * glu1 * sg * (1.0 - sg)).astype(BF16)
        dglu_ref[:, :SSM_WIDTH] = dglu1
        dglu_ref[:, SSM_WIDTH:] = dglu2
        dgy = _dot_nt(jnp.concatenate([dglu1, dglu2], axis=1), _side_by_side(wglu_ref))
        ys = ys_ref[...]
        gy, t = _gelu(ys)
        gy_ref[...] = gy.astype(BF16)
        dys_ref[...] = dgy * _gelu_grad(ys, t)

    wide_b, half_b = _sds((SEQ, D_MODEL), BF16), _sds((SEQ, SSM_WIDTH), BF16)
    half_f = _sds((SEQ, SSM_WIDTH), F32)
    return _pallas_call(
        body, name="mixer_bwd", grid=(SEQ // tm,),
        in_specs=[_row_spec(tm, D_MODEL), _row_spec(tm, D_MODEL, GL_COL), _row_spec(tm, D_MODEL, GL_COL + 1), _row_spec(tm, D_MODEL),
                  _row_spec(tm, D_MODEL), _row_spec(tm, D_MODEL), _row_spec(tm, SSM_WIDTH), _full_spec((N_DEV, ATTN_WIDTH, 128)),
                  _full_spec((N_DEV, SSM_WIDTH, 128)), _full_spec((N_DEV, SSM_WIDTH, 128)), _full_spec((D_MODEL, D_MODEL)),
                  _full_spec((N_DEV, 2, 128))],
        out_specs=(_row_spec(tm, D_MODEL), _row_spec(tm, D_MODEL), _row_spec(tm, IN_WIDTH), _row_spec(tm, ATTN_WIDTH),
                   _row_spec(tm, D_MODEL), _row_spec(tm, SSM_WIDTH), _row_spec(tm, D_MODEL), _row_spec(tm, SSM_WIDTH),
                   _row_spec(tm, SSM_WIDTH), _full_spec((1, 2 * D_MODEL))),
        out_shape=(wide_b, wide_b, _sds((SEQ, IN_WIDTH), BF16), half_f, wide_b, half_f, wide_b, half_b, half_b,
                   _sds((1, 2 * D_MODEL), F32)),
        compiler_params=_cparams(dimension_semantics=("arbitrary",)),
    )(dr1, proj, proj, y_attn, y_ssm, glu, ys, w_ab, w_sb, w_glu, w_out, b_gate)


def _grad_x(dproj, w_in, dr1):
    tm, tk = 512, 1792
    nk = IN_WIDTH // tk

    def body(dp_ref, w_ref, dr1_ref, o_ref, acc):
        k = pl.program_id(1)
        part = _dot_nt(dp_ref[...], _side_by_side(w_ref))

        @pl.when(k == 0)
        def _():
            acc[...] = part

        @pl.when(k > 0)
        def _():
            acc[...] += part

        @pl.when(k == nk - 1)
        def _():
            o_ref[...] = DN_ALPHA * dr1_ref[...] + acc[...]

    row = pl.BlockSpec((tm, D_MODEL), lambda i, k: (i, 0))
    return _pallas_call(
        body, name="grad_x", grid=(SEQ // tm, nk),
        in_specs=[pl.BlockSpec((tm, tk), lambda i, k: (i, k)), pl.BlockSpec((2, D_MODEL, tk // 2), lambda i, k: (k, 0, 0)), row],
        out_specs=row, out_shape=_sds((SEQ, D_MODEL), F32), scratch_shapes=[pltpu.VMEM((tm, D_MODEL), F32)],
        compiler_params=_cparams(dimension_semantics=("arbitrary", "arbitrary")),
    )(dproj, w_in, dr1)


def _weight_grad(a, b, name, shard_cols=None):
    k, n = a.shape[1], b.shape[1]
    tm = 512
    nm = SEQ // tm
    tk = min(k, 512) if shard_cols else k // N_DEV
    tn = n // 4 if shard_cols else min(n, 1024)

    def body(a_ref, b_ref, o_ref, acc):
        m = pl.program_id(2)
        part = _dot_tn(a_ref[...].astype(BF16), b_ref[...].astype(BF16))

        @pl.when(m == 0)
        def _():
            acc[...] = part

        @pl.when(m > 0)
        def _():
            acc[...] += part

        @pl.when(m == nm - 1)
        def _():
            if shard_cols:
                o_ref[0] = acc[:, :shard_cols].astype(BF16)
                o_ref[1] = acc[:, shard_cols:].astype(BF16)
            else:
                o_ref[...] = acc[...].astype(BF16)

    if shard_cols:
        out_spec = pl.BlockSpec((2, None, tk, shard_cols), lambda kk, j, m: (0, j, kk, 0))
        out_shape = _sds((2, 4, k, shard_cols), BF16)
    else:
        out_spec = pl.BlockSpec((None, None, tk, tn), lambda kk, j, m: (kk % 2, kk // 2, 0, j))
        out_shape = _sds((2, 4, tk, n), BF16)
    return _pallas_call(
        body, name=name, grid=(k // tk, n // tn, nm),
        in_specs=[pl.BlockSpec((tm, tk), lambda kk, j, m: (m, kk)), pl.BlockSpec((tm, tn), lambda kk, j, m: (m, j))],
        out_specs=out_spec, out_shape=out_shape, scratch_shapes=[pltpu.VMEM((tk, tn), F32)],
        compiler_params=_cparams(dimension_semantics=("arbitrary", "arbitrary", "arbitrary")),
    )(a, b)


MESH = pl.DeviceIdType.MESH
ANY = pl.BlockSpec(memory_space=pl.ANY)


def _place():
    return lax.axis_index("x"), lax.axis_index("y"), lax.axis_index("c")


def _other_chips(x, y):
    return [(1 - x, y), (x, 1 - y), (1 - x, 1 - y)]


class _Ride:
    def __init__(self, operands, results, aliases, sems, start, wait):
        self.operands, self.results, self.aliases, self.sems = list(operands), list(results), dict(aliases), list(sems)
        self.start, self.wait = start, wait

    def __add__(self, other):
        n_in, n_out, n_sem = len(self.operands), len(self.results), len(self.sems)

        def both(which):
            def run(ins, outs, sems):
                getattr(self, which)(ins[:n_in], outs[:n_out], sems[:n_sem])
                getattr(other, which)(ins[n_in:], outs[n_out:], sems[n_sem:])
            return run

        aliases = {**self.aliases, **{n_in + i: n_out + j for i, j in other.aliases.items()}}
        return _Ride(self.operands + other.operands, self.results + other.results, aliases, self.sems + other.sems,
                     both("start"), both("wait"))


def _call(body, name, grid, in_specs, out_specs, out_shape, scratch_shapes, operands, ride=None, aliases=None):
    in_specs, out_specs, out_shape = list(in_specs), list(out_specs), list(out_shape)
    scratch_shapes, operands, aliases = list(scratch_shapes), list(operands), dict(aliases or {})
    kernel_body = body
    if ride is not None:
        n_in, n_out, n_scr, r_in, r_out = len(in_specs), len(out_specs), len(scratch_shapes), len(ride.operands), len(ride.results)

        def kernel_body(*refs):
            out0, scr0 = n_in + r_in, n_in + r_in + n_out + r_out
            ride_refs = (refs[n_in:out0], refs[out0 + n_out:scr0], refs[scr0 + n_scr:])
            ids = [pl.program_id(i) for i in range(len(grid))]
            first = functools.reduce(jnp.logical_and, [i == 0 for i in ids])
            last = functools.reduce(jnp.logical_and, [i == g - 1 for i, g in zip(ids, grid)])

            @pl.when(first)
            def _():
                ride.start(*ride_refs)

            body(*refs[:n_in], *refs[out0:out0 + n_out], *refs[scr0:scr0 + n_scr])

            @pl.when(last)
            def _():
                ride.wait(*ride_refs)

        aliases.update({n_in + i: n_out + j for i, j in ride.aliases.items()})
        in_specs += [ANY] * r_in
        out_specs += [ANY] * r_out
        out_shape += ride.results
        scratch_shapes += ride.sems
        operands += ride.operands
    return _pallas_call(
        kernel_body, name=name, grid=grid, in_specs=in_specs, out_specs=out_specs, out_shape=out_shape,
        scratch_shapes=scratch_shapes, input_output_aliases=aliases,
        compiler_params=_cparams(dimension_semantics=("arbitrary",) * len(grid)),
    )(*operands)


def _gather_first_level(shards):
    n = len(shards)

    def copies(ins, outs, sems, landed):
        send_sems, recv_sems, local_sems = sems
        x, y, c = _place()
        peers = [(x, y, 1 - c)] + [(px, py, c) for px, py in _other_chips(x, y)]

        def row(peer):
            return 4 * x + 2 * y + c if not landed else 4 * peer[0] + 2 * peer[1] + peer[2]

        local = [pltpu.make_async_copy(ins[a], outs[a].at[4 * x + 2 * y + c], local_sems.at[a]) for a in range(n)]
        remote = [pltpu.make_async_remote_copy(
            src_ref=ins[a], dst_ref=outs[a].at[row(peer)], send_sem=send_sems.at[a, k], recv_sem=recv_sems.at[a, k],
            device_id=peer, device_id_type=MESH) for a in range(n) for k, peer in enumerate(peers)]
        return local, remote

    def start(ins, outs, sems):
        local, remote = copies(ins, outs, sems, False)
        for cp in local + remote:
            cp.start()

    def wait(ins, outs, sems):
        local, sent = copies(ins, outs, sems, False)
        for cp in copies(ins, outs, sems, True)[1]:
            cp.wait_recv()
        for cp in sent:
            cp.wait_send()
        for cp in local:
            cp.wait()

    return _Ride(shards, [_sds((N_DEV,) + s.shape, s.dtype) for s in shards], {},
                 [pltpu.SemaphoreType.DMA((n, 4)), pltpu.SemaphoreType.DMA((n, 4)), pltpu.SemaphoreType.DMA((n,))], start, wait)


def _gather_second_level(buffers):
    n = len(buffers)

    def copies(outs, sems, core):
        send_sems, recv_sems = sems
        x, y, c = _place()
        return [pltpu.make_async_remote_copy(
            src_ref=outs[a].at[4 * px + 2 * py + core], dst_ref=outs[a].at[4 * px + 2 * py + core], send_sem=send_sems.at[a, j],
            recv_sem=recv_sems.at[a, j], device_id=(x, y, 1 - c), device_id_type=MESH)
            for a in range(n) for j, (px, py) in enumerate(_other_chips(x, y))]

    def start(ins, outs, sems):
        for cp in copies(outs, sems, lax.axis_index("c")):
            cp.start()

    def wait(ins, outs, sems):
        for cp in copies(outs, sems, 1 - lax.axis_index("c")):
            cp.wait_recv()
        for cp in copies(outs, sems, lax.axis_index("c")):
            cp.wait_send()

    return _Ride(buffers, [_sds(b.shape, b.dtype) for b in buffers], {i: i for i in range(n)},
                 [pltpu.SemaphoreType.DMA((n, 3)), pltpu.SemaphoreType.DMA((n, 3))], start, wait)


def _send_buffer(w, rows, cols, name):
    r, c = w.shape

    def body(w_ref, o_ref):
        if (r, c) != (rows, cols):
            o_ref[...] = jnp.zeros((rows, cols), BF16)
        o_ref[:r, :c] = w_ref[...].astype(BF16)

    return _pallas_call(body, name=name, out_shape=_sds((rows, cols), BF16))(w)


def _all_gather(shards, name):
    n = len(shards)

    def body(*refs):
        ins, outs = refs[:n], refs[n:2 * n]
        send_sems, recv_sems, local_sems = refs[2 * n:]
        x, y, c = _place()
        me, sibling = (x, y, c), (x, y, 1 - c)
        chips = _other_chips(x, y)

        def slot(a, px, py, pc):
            return outs[a].at[4 * px + 2 * py + pc]

        def copy(a, k, block, to, src=None):
            return pltpu.make_async_remote_copy(
                src_ref=slot(a, *block) if src is None else src, dst_ref=slot(a, *block),
                send_sem=send_sems.at[a, k], recv_sem=recv_sems.at[a, k], device_id=to, device_id_type=MESH)

        mine = [pltpu.make_async_copy(ins[a], slot(a, *me), local_sems.at[a]) for a in range(n)]
        for cp in mine:
            cp.start()
        first = []
        for a in range(n):
            first.append(copy(a, 0, me, sibling, src=ins[a]))
            first += [copy(a, 1 + j, me, (*chip, c), src=ins[a]) for j, chip in enumerate(chips)]
        for cp in first:
            cp.start()
        passed = []
        for j, chip in enumerate(chips):
            for a in range(n):
                copy(a, 1 + j, (*chip, c), me).wait_recv()
                onward = copy(a, 4 + j, (*chip, c), sibling)
                onward.start()
                passed.append(onward)
        for a in range(n):
            copy(a, 0, sibling, me).wait_recv()
            for j, chip in enumerate(chips):
                copy(a, 4 + j, (*chip, 1 - c), me).wait_recv()
        for cp in first + passed:
            cp.wait_send()
        for cp in mine:
            cp.wait()

    return _pallas_call(
        body, name=name, in_specs=[ANY] * n, out_specs=[ANY] * n,
        out_shape=[_sds((N_DEV,) + s.shape, s.dtype) for s in shards],
        scratch_shapes=[pltpu.SemaphoreType.DMA((n, 7)), pltpu.SemaphoreType.DMA((n, 7)), pltpu.SemaphoreType.DMA((n,))],
    )(*shards)


def _swap_with_sibling(grads, name):
    n = len(grads)

    def body(*refs):
        ins, outs = refs[:n], refs[n:2 * n]
        send_sems, recv_sems = refs[2 * n:]
        x, y, c = _place()
        copies = [pltpu.make_async_remote_copy(
            src_ref=ins[a].at[1 - c], dst_ref=outs[a], send_sem=send_sems.at[a], recv_sem=recv_sems.at[a],
            device_id=(x, y, 1 - c), device_id_type=MESH) for a in range(n)]
        for cp in copies:
            cp.start()
        for cp in copies:
            cp.wait()

    return _pallas_call(
        body, name=name, in_specs=[ANY] * n, out_specs=[ANY] * n,
        out_shape=[_sds(g.shape[1:], g.dtype) for g in grads],
        scratch_shapes=[pltpu.SemaphoreType.DMA((n,)), pltpu.SemaphoreType.DMA((n,))],
    )(*grads)


def _pair_sum(g, r, core, name):
    _, _, k, n = g.shape

    def body(core_ref, g_ref, r_ref, o_ref):
        o_ref[...] = (g_ref[...].astype(F32) + r_ref[...].astype(F32)).astype(o_ref.dtype)

    return _pallas_call(
        body, name=name,
        grid_spec=pltpu.PrefetchScalarGridSpec(
            num_scalar_prefetch=1, grid=(4,),
            in_specs=[pl.BlockSpec((None, None, k, n), lambda p, core_ref: (core_ref[0], p, 0, 0)),
                      pl.BlockSpec((None, k, n), lambda p, core_ref: (p, 0, 0))],
            out_specs=pl.BlockSpec((None, k, n), lambda p, core_ref: (p, 0, 0))),
        out_shape=_sds((4, k, n), g.dtype), compiler_params=_cparams(dimension_semantics=("arbitrary",)),
    )(core, g, r)


def _swap_between_chips(sums, name):
    n = len(sums)

    def body(*refs):
        ins, outs = refs[:n], refs[n:2 * n]
        send_sems, recv_sems, local_sems = refs[2 * n:]
        x, y, c = _place()
        mine = 2 * x + y
        copies = []
        for a in range(n):
            copies.append(pltpu.make_async_copy(ins[a].at[mine], outs[a].at[mine], local_sems.at[a]))
            for j, (px, py) in enumerate(_other_chips(x, y)):
                copies.append(pltpu.make_async_remote_copy(
                    src_ref=ins[a].at[2 * px + py], dst_ref=outs[a].at[mine], send_sem=send_sems.at[a, j],
                    recv_sem=recv_sems.at[a, j], device_id=(px, py, c), device_id_type=MESH))
        for cp in copies:
            cp.start()
        for cp in copies:
            cp.wait()

    return _pallas_call(
        body, name=name, in_specs=[ANY] * n, out_specs=[ANY] * n, out_shape=[_sds(s.shape, s.dtype) for s in sums],
        scratch_shapes=[pltpu.SemaphoreType.DMA((n, 3)), pltpu.SemaphoreType.DMA((n, 3)), pltpu.SemaphoreType.DMA((n,))],
    )(*sums)


def _adamw_math(w, g, m, v):
    m = ADAM_B1 * m + (1.0 - ADAM_B1) * g
    v = ADAM_B2 * v + (1.0 - ADAM_B2) * (g * g)
    m_hat = m / (1.0 - ADAM_B1 ** ADAM_STEP)
    v_hat = v / (1.0 - ADAM_B2 ** ADAM_STEP)
    return -ADAM_LR * (m_hat / (jnp.sqrt(v_hat) + ADAM_EPS) + ADAM_WD * w), m, v


def _adamw(w, m, v, parts, name):
    r, c = w.shape
    tr = r if r <= 512 else 256
    n_parts, pr, pc = parts.shape
    assert r % tr == 0 and (tr == r or pr == r)

    def body(w_ref, m_ref, v_ref, p_ref, g_out, d_out, m_out, v_out):
        g = p_ref[0, :tr, :c].astype(F32)
        for p in range(1, n_parts):
            g = g + p_ref[p, :tr, :c].astype(F32)
        g_out[...] = g
        d_out[...], m_out[...], v_out[...] = _adamw_math(w_ref[...], g, m_ref[...], v_ref[...])

    tile = pl.BlockSpec((tr, c), lambda i: (i, 0))
    part_tile = pl.BlockSpec((n_parts, pr if tr == r else tr, pc), lambda i: (0, i, 0))
    out = _sds((r, c), F32)
    return _pallas_call(
        body, name=name, grid=(r // tr,), in_specs=[tile, tile, tile, part_tile], out_specs=(tile,) * 4,
        out_shape=(out,) * 4, compiler_params=_cparams(dimension_semantics=("arbitrary",)),
    )(w, m, v, parts)


SMALL = ("ssm_a_re", "ssm_a_im", "ssm_log_dt", "ssm_b_re", "ssm_b_im", "ssm_c_re", "ssm_c_im", "ssm_d",
         "ln1_g", "ln1_b", "ln2_g", "ln2_b")


def _pack_rows(arrays):
    rows = []
    for a in arrays:
        flat = a.reshape(-1)
        rows.append(jnp.pad(flat, (0, -flat.shape[0] % 128)).reshape(-1, 128))
    packed = jnp.concatenate(rows, axis=0)
    return jnp.pad(packed, ((0, -packed.shape[0] % 8), (0, 0)))


def _unpack_rows(packed, shapes):
    out, row = [], 0
    for shape in shapes:
        size = math.prod(shape)
        n_rows = -(-size // 128)
        out.append(packed[row:row + n_rows].reshape(-1)[:size].reshape(shape))
        row += n_rows
    return out


def _sum_devices(parts):
    def body(p_ref, o_ref):
        total = p_ref[0]
        for dev in range(1, N_DEV):
            total = total + p_ref[dev]
        o_ref[...] = total

    return _pallas_call(body, name="sum_devices", out_shape=_sds(parts.shape[1:], F32))(parts)


def _adamw_replicated(ws, ms, vs, gs):
    n = len(ws)

    def body(*refs):
        w_refs, m_refs, v_refs, g_refs, d_out, m_out, v_out = (refs[i * n:(i + 1) * n] for i in range(7))
        for i in range(n):
            d_out[i][...], m_out[i][...], v_out[i][...] = _adamw_math(w_refs[i][...], g_refs[i][...], m_refs[i][...], v_refs[i][...])

    out = _pallas_call(body, name="adamw_replicated", out_shape=[_sds(w.shape, F32) for w in ws] * 3,
                       compiler_params=_cparams())(*ws, *ms, *vs, *gs)
    return out[:n], out[n:2 * n], out[2 * n:]


def kernel(x, w_in, b_gate, w_attn_br, w_ssm_br, w_out, ssm_a_re, ssm_a_im, ssm_log_dt, ssm_b_re, ssm_b_im, ssm_c_re, ssm_c_im, ssm_d, w_glu, ln1_g, ln1_b, w_ff_gate, w_ff_up, w_ff_down, ln2_g, ln2_b, loss_target, m_w_in, m_b_gate, m_w_attn_br, m_w_ssm_br, m_w_out, m_ssm_a_re, m_ssm_a_im, m_ssm_log_dt, m_ssm_b_re, m_ssm_b_im, m_ssm_c_re, m_ssm_c_im, m_ssm_d, m_w_glu, m_ln1_g, m_ln1_b, m_w_ff_gate, m_w_ff_up, m_w_ff_down, m_ln2_g, m_ln2_b, v_w_in, v_b_gate, v_w_attn_br, v_w_ssm_br, v_w_out, v_ssm_a_re, v_ssm_a_im, v_ssm_log_dt, v_ssm_b_re, v_ssm_b_im, v_ssm_c_re, v_ssm_c_im, v_ssm_d, v_w_glu, v_ln1_g, v_ln1_b, v_w_ff_gate, v_w_ff_up, v_w_ff_down, v_ln2_g, v_ln2_b):
    given = dict(locals())
    x2, target = x[0], loss_target[0]
    core = lax.axis_index("c").astype(jnp.int32).reshape(1)

    sharded = ("w_in", "w_attn_br", "w_ssm_br", "w_glu", "w_ff_gate", "w_ff_up", "b_gate", "w_out", "w_ff_down")
    send_shape = dict(w_in=(D_MODEL, 896), w_attn_br=(ATTN_WIDTH, 128), w_ssm_br=(SSM_WIDTH, 128), w_glu=(SSM_WIDTH, 128),
                      w_out=(128, D_MODEL), w_ff_gate=(D_MODEL, FF_PAD), w_ff_up=(D_MODEL, FF_PAD), w_ff_down=(FF_PAD, D_MODEL))
    local = {k: given[k][0] for k in sharded}
    sends = {k: local[k] if k == "b_gate" else _send_buffer(local[k], *send_shape[k], name="send_" + k) for k in sharded}
    mixer_weights = ("w_attn_br", "w_ssm_br", "w_glu", "b_gate", "w_out")
    ff_weights = ("w_ff_gate", "w_ff_up", "w_ff_down")
    wt = {}
    wt["w_in"], = _all_gather([sends["w_in"]], "gather_w_in")

    a_re, a_im, log_dt = ssm_a_re[0], ssm_a_im[0], ssm_log_dt[0].reshape(SSM_GROUPS, 1)
    b_re_t, b_im_t = ssm_b_re[0].transpose(0, 2, 1), ssm_b_im[0].transpose(0, 2, 1)
    abar_re, abar_im, e_re, e_im, bbar_re_t, bbar_im_t = _ssm_prep(a_re, a_im, log_dt, b_re_t, b_im_t)
    bmat, cmat, a_chunks = _ssm_tables(abar_re, abar_im, bbar_re_t, bbar_im_t, ssm_c_re[0], ssm_c_im[0])
    cos_t, sin_t = _rope_tables()

    proj, *partly = _proj(x2, wt["w_in"], _gather_first_level([sends[k] for k in mixer_weights]))
    attn, lse, *landed = _attn_fwd(proj, cos_t, sin_t,
                                   _gather_second_level(partly) + _gather_first_level([sends[k] for k in ff_weights]))
    wt.update(zip(mixer_weights, landed[:len(mixer_weights)]))
    ys, states, *landed = _ssm_fwd(proj, bmat, cmat, a_chunks, ssm_d, _gather_second_level(landed[len(mixer_weights):]))
    wt.update(zip(ff_weights, landed))
    wt["w_out"] = wt["w_out"].reshape(D_MODEL, D_MODEL)
    wt["w_ff_down"] = wt["w_ff_down"].reshape(D_FF_PAD, D_MODEL)
    b_gate_full = wt["b_gate"]
    h, xhat1, rstd1, glu, y_attn, y_ssm = _mixer_out(attn, ys, proj, x2, wt["w_attn_br"], wt["w_ssm_br"], wt["w_glu"],
                                                      wt["w_out"], b_gate_full, ln1_g, ln1_b)
    ff_a, ff_b, ff_f = _ff_up(h, wt["w_ff_gate"], wt["w_ff_up"])
    dr2, d_ln2_g, d_ln2_b, loss_lanes = _ff_down_loss(ff_f, wt["w_ff_down"], h, target, ln2_g, ln2_b)

    d_a, d_b = _ff_down_bwd(dr2, wt["w_ff_down"], ff_a, ff_b)
    dr1, d_ln1_g, d_ln1_b = _ff_up_bwd(d_a, d_b, wt["w_ff_gate"], wt["w_ff_up"], dr2, xhat1, rstd1, ln1_g)
    d_ya, d_yssm, d_proj, d_attn, d_glu, d_ys, mixed, y_s, gy, d_bg = _mixer_bwd(
        dr1, proj, y_attn, y_ssm, glu, ys, wt["w_attn_br"], wt["w_ssm_br"], wt["w_glu"], wt["w_out"], b_gate_full)
    for g in range(3):
        d_proj = _attn_bwd_group(g, proj, cos_t, sin_t, attn, lse, d_attn, d_proj)
    d_proj, d_bmat, d_cmat, d_abar, d_skip = _ssm_bwd(d_ys, proj, states, bmat, cmat, a_chunks, ssm_d, d_proj)
    grad_x = _grad_x(d_proj, wt["w_in"], dr1)

    contrib = dict(
        w_in=_weight_grad(x2, d_proj, "wgrad_w_in", 896),
        w_attn_br=_weight_grad(attn, d_ya, "wgrad_w_attn_br", 128),
        w_ssm_br=_weight_grad(y_s, d_yssm, "wgrad_w_ssm_br", 128),
        w_glu=_weight_grad(gy, d_glu, "wgrad_w_glu", 128),
        w_out=_weight_grad(mixed, dr1, "wgrad_w_out"),
        w_ff_gate=_weight_grad(h, d_a, "wgrad_w_ff_gate", FF_PAD),
        w_ff_up=_weight_grad(h, d_b, "wgrad_w_ff_up", FF_PAD),
        w_ff_down=_weight_grad(ff_f, dr2, "wgrad_w_ff_down"),
        b_gate=d_bg.reshape(2, 4, 2, 128).transpose(2, 1, 0, 3),
    )
    from_sibling = _swap_with_sibling([contrib[k] for k in sharded], "swap_with_sibling")
    chip_sums = [_pair_sum(contrib[k], r, core, "pair_sum_" + k) for k, r in zip(sharded, from_sibling)]
    parts = dict(zip(sharded, _swap_between_chips(chip_sums, "swap_between_chips")))

    grads, deltas, new_m, new_v = {}, {}, {}, {}
    for k in sharded:
        w2 = local[k]
        out = _adamw(w2, given["m_" + k][0], given["v_" + k][0], parts[k], "adamw_" + k)
        grads[k], deltas[k], new_m[k], new_v[k] = (o.reshape((1,) + w2.shape) for o in out)

    gbb_re_t, gbb_im_t = _block_diag_parts(d_bmat, True)
    gc_re, gc_im = _block_diag_parts(d_cmat, False)
    ga_re = d_abar[:, 0, :CHUNK_STATES].reshape(SSM_GROUPS, SSM_STATE)
    ga_im = d_abar[:, 0, CHUNK_STATES:].reshape(SSM_GROUPS, SSM_STATE)
    g_a_re, g_a_im, g_log_dt, g_b_re_t, g_b_im_t = _ssm_param_bwd(
        a_re, a_im, log_dt, b_re_t, b_im_t, abar_re, abar_im, e_re, e_im, ga_re, ga_im, gbb_re_t, gbb_im_t)
    mine = [g_a_re, g_a_im, g_log_dt, g_b_re_t.transpose(0, 2, 1), g_b_im_t.transpose(0, 2, 1), gc_re, -gc_im,
            d_skip, d_ln1_g, d_ln1_b, d_ln2_g, d_ln2_b]
    every, = _all_gather([_pack_rows(mine + [loss_lanes])], "gather_small_grads")
    *small_grads, loss_sum = _unpack_rows(_sum_devices(every), [given[k].shape for k in SMALL] + [(1, 128)])
    small = _adamw_replicated([given[k] for k in SMALL], [given["m_" + k] for k in SMALL],
                              [given["v_" + k] for k in SMALL], small_grads)
    for res, values in zip((grads, deltas, new_m, new_v), (small_grads,) + small):
        res.update(zip(SMALL, values))
    loss = loss_sum[0, 0]

    order = ("w_in", "b_gate", "w_attn_br", "w_ssm_br", "w_out", "ssm_a_re", "ssm_a_im", "ssm_log_dt", "ssm_b_re", "ssm_b_im",
             "ssm_c_re", "ssm_c_im", "ssm_d", "w_glu", "ln1_g", "ln1_b", "w_ff_gate", "w_ff_up", "w_ff_down", "ln2_g", "ln2_b")
    return (loss, grad_x[None], *[grads[k] for k in order], *[deltas[k] for k in order], *[new_m[k] for k in order],
            *[new_v[k] for k in order])
```

```python
import functools
import math

import jax
import jax.numpy as jnp
from jax import lax
from jax.experimental import pallas as pl
from jax.experimental.pallas import tpu as pltpu

F32 = jnp.float32
BF16 = jnp.bfloat16

N_DEV = 8
SEQ = 2048
D_MODEL = 1024
HEAD_DIM = 64
ATTN_WIDTH = 512
QKV_WIDTH = 1536
SSM_WIDTH = 512
SSM_GROUPS = 32
SSM_GROUP = 16
SSM_STATE = 64
IN_WIDTH = 7168
D_FF = 2816
FF_SHARD = D_FF // N_DEV
FF_PAD = 384
D_FF_PAD = FF_PAD * N_DEV
DN_ALPHA = 2.0 ** 0.25
LN_EPS = 1e-5
NEG_INF = -1e30
ROPE_THETA = 10000.0
BLOCK = 128
GROUPS = ((1, 16), (4, 4), (16, 1))

ADAM_LR = 0.001
ADAM_B1 = 0.9
ADAM_B2 = 0.999
ADAM_EPS = 1e-08
ADAM_WD = 0.01
ADAM_STEP = 10

VMEM_LIMIT = 56 * 1024 * 1024


_pallas_call = pl.pallas_call


def _cparams(**kw):
    return pltpu.CompilerParams(vmem_limit_bytes=VMEM_LIMIT, **kw)


def _dot(a, b):
    return jnp.dot(a, b, preferred_element_type=F32)


def _dot_nt(a, b):
    return lax.dot_general(a, b, (((1,), (1,)), ((), ())), preferred_element_type=F32)


def _side_by_side(w_ref, row=None):
    rows = slice(None) if row is None else pl.ds(row, 1)
    return jnp.concatenate([w_ref[i, rows, :] for i in range(w_ref.shape[0])], axis=1)


def _dot_tn(a, b):
    return lax.dot_general(a, b, (((0,), (0,)), ((), ())), preferred_element_type=F32)


def _rope_tables():
    half = HEAD_DIM // 2
    inv_freq = ROPE_THETA ** (-jnp.arange(half, dtype=F32) / half)
    ang = jnp.arange(SEQ, dtype=F32)[:, None] * inv_freq[None, :]
    cos, sin = jnp.cos(ang), jnp.sin(ang)
    return jnp.tile(cos, (1, 4)), jnp.tile(jnp.concatenate([-sin, sin], axis=1), (1, 2))


def _swap_halves(x):
    lane = lax.broadcasted_iota(jnp.int32, x.shape, 1)
    return jnp.where((lane & 63) < 32, pltpu.roll(x, 96, axis=1), pltpu.roll(x, 32, axis=1))


def _group_rows(d, nb, r, i):
    src = pl.ds(i * BLOCK, BLOCK) if d == 1 else pl.ds(r + i * BLOCK * d, BLOCK, stride=d)
    return src, pl.ds((r * nb + i) * BLOCK, BLOCK)


def _attn_masks():
    a_idx = lax.broadcasted_iota(jnp.int32, (2 * BLOCK, 2 * BLOCK), 0) & (BLOCK - 1)
    c_idx = lax.broadcasted_iota(jnp.int32, (2 * BLOCK, 2 * BLOCK), 1)
    cur_ok = jnp.logical_and(c_idx >= BLOCK, c_idx - BLOCK <= a_idx)
    prev_ok = jnp.logical_and(c_idx < BLOCK, c_idx >= a_idx)
    lane = lax.broadcasted_iota(jnp.int32, (BLOCK, 128), 1)
    return cur_ok, prev_ok, lane < HEAD_DIM


def _stack_heads(t, head0):
    zero = jnp.zeros_like(t)
    return jnp.concatenate([jnp.where(head0, t, zero), jnp.where(head0, zero, t)], axis=0)


def _unstack_heads(t2, head0):
    return jnp.where(head0, t2[:BLOCK], t2[BLOCK:])


def _attn_fwd(proj, cos_t, sin_t, ride=None):
    def body(q0, q1, q2, k0, k1, k2, v0, v1, v2, cos_ref, sin_ref, attn_ref, lse_ref,
             qs, ks, vs, os_, ms, ls, acc, mnat, lnat):
        cur_ok, prev_ok, head0 = _attn_masks()
        ks[:BLOCK, :] = jnp.zeros((BLOCK, 128), BF16)
        vs[:BLOCK, :] = jnp.zeros((BLOCK, 128), BF16)
        for g, (d, nb) in enumerate(GROUPS):
            q_ref, k_ref, v_ref = (q0, q1, q2)[g], (k0, k1, k2)[g], (v0, v1, v2)[g]
            for r in range(d):
                for i in range(nb):
                    src, dst = _group_rows(d, nb, r, i)
                    below = pl.ds(dst.start + BLOCK, BLOCK)
                    c, s = cos_ref[src, :], sin_ref[src, :]
                    q = q_ref[src, :]
                    k = k_ref[src, :]
                    qs[dst, :] = ((q * c + _swap_halves(q) * s) * 0.125).astype(BF16)
                    ks[below, :] = (k * c + _swap_halves(k) * s).astype(BF16)
                    vs[below, :] = v_ref[src, :].astype(BF16)

            def block(b, carry, nb=nb):
                has_prev = (b & (nb - 1)) > 0
                cur = pl.ds(pl.multiple_of(b * BLOCK, BLOCK), BLOCK)
                window = pl.ds(pl.multiple_of(b * BLOCK, BLOCK), 2 * BLOCK)
                valid = jnp.logical_or(cur_ok, jnp.logical_and(prev_ok, has_prev))
                s = jnp.where(valid, _dot_nt(_stack_heads(qs[cur, :], head0), ks[window, :]), NEG_INF)
                m = jnp.max(s, axis=1, keepdims=True)
                p = jnp.exp(s - m)
                os_[cur, :] = _unstack_heads(_dot(p.astype(BF16), vs[window, :]), head0)
                ms[cur, :] = _unstack_heads(m, head0)
                ls[cur, :] = _unstack_heads(jnp.sum(p, axis=1, keepdims=True), head0)
                return carry

            lax.fori_loop(0, SEQ // BLOCK, block, 0, unroll=2)

            for r in range(d):
                for i in range(nb):
                    src, dst = _group_rows(d, nb, r, i)
                    if g == 0:
                        acc[src, :], mnat[src, :], lnat[src, :] = os_[dst, :], ms[dst, :], ls[dst, :]
                    else:
                        m_old, m_g = mnat[src, :], ms[dst, :]
                        m_new = jnp.maximum(m_old, m_g)
                        a_old, a_g = jnp.exp(m_old - m_new), jnp.exp(m_g - m_new)
                        acc[src, :] = a_old * acc[src, :] + a_g * os_[dst, :]
                        lnat[src, :] = a_old * lnat[src, :] + a_g * ls[dst, :]
                        mnat[src, :] = m_new
        for i in range(SEQ // BLOCK):
            rows = pl.ds(i * BLOCK, BLOCK)
            l = lnat[rows, :]
            attn_ref[rows, :] = acc[rows, :] / l
            lse_ref[rows, :] = mnat[rows, :] + jnp.log(l)

    def col(base):
        return pl.BlockSpec((SEQ, 128), lambda hp, base=base: (0, base + hp))

    in_specs = [col(g * 4) for g in range(3)] + [col(12 + g * 4) for g in range(3)] + [col(24 + g * 4) for g in range(3)]
    table = pl.BlockSpec((SEQ, 128), lambda hp: (0, 0))
    out = pl.BlockSpec((SEQ, 128), lambda hp: (0, hp))
    return _call(
        body, "attn_fwd", (4,), in_specs + [table, table], [out, out],
        [_sds((SEQ, ATTN_WIDTH), F32), _sds((SEQ, ATTN_WIDTH), F32)],
        [pltpu.VMEM((SEQ, 128), BF16)] + [pltpu.VMEM((SEQ + BLOCK, 128), BF16)] * 2 + [pltpu.VMEM((SEQ, 128), F32)] * 6,
        [proj] * 9 + [cos_t, sin_t], ride)


def _attn_bwd_group_body(g):
    d, nb = GROUPS[g]

    def body(q_ref, k_ref, v_ref, cos_ref, sin_ref, attn_ref, lse_ref, dattn_ref, dproj_in, dproj_ref,
             qs, ks, vs, dos, lss, dss, dqs, dks, dvs, stage, outs, sems):
        del dproj_in
        cur_ok, prev_ok, head0 = _attn_masks()
        ks[:BLOCK, :] = jnp.zeros((BLOCK, 128), BF16)
        vs[:BLOCK, :] = jnp.zeros((BLOCK, 128), BF16)
        dks[:BLOCK, :] = jnp.zeros((BLOCK, 128), F32)
        dvs[:BLOCK, :] = jnp.zeros((BLOCK, 128), F32)
        for r in range(d):
            for i in range(nb):
                src, dst = _group_rows(d, nb, r, i)
                below = pl.ds(dst.start + BLOCK, BLOCK)
                c, s = cos_ref[src, :], sin_ref[src, :]
                q = q_ref[src, :]
                k = k_ref[src, :]
                qs[dst, :] = ((q * c + _swap_halves(q) * s) * 0.125).astype(BF16)
                ks[below, :] = (k * c + _swap_halves(k) * s).astype(BF16)
                vs[below, :] = v_ref[src, :].astype(BF16)
                do = dattn_ref[src, :]
                prod = do * attn_ref[src, :]
                d0 = jnp.sum(jnp.where(head0, prod, 0.0), axis=1, keepdims=True)
                d1 = jnp.sum(jnp.where(head0, 0.0, prod), axis=1, keepdims=True)
                dos[dst, :] = do.astype(BF16)
                dss[dst, :] = jnp.where(head0, d0, d1)
                lss[dst, :] = lse_ref[src, :]
                dks[below, :] = jnp.zeros((BLOCK, 128), F32)
                dvs[below, :] = jnp.zeros((BLOCK, 128), F32)

        def per_head_column(t):
            return jnp.concatenate([jnp.max(jnp.where(head0, t, NEG_INF), axis=1, keepdims=True),
                                    jnp.max(jnp.where(head0, NEG_INF, t), axis=1, keepdims=True)], axis=0)

        def block(b, carry):
            has_prev = (b & (nb - 1)) > 0
            cur = pl.ds(pl.multiple_of(b * BLOCK, BLOCK), BLOCK)
            window = pl.ds(pl.multiple_of(b * BLOCK, BLOCK), 2 * BLOCK)
            valid = jnp.logical_or(cur_ok, jnp.logical_and(prev_ok, has_prev))
            q2, do2 = _stack_heads(qs[cur, :], head0), _stack_heads(dos[cur, :], head0)
            kw, vw = ks[window, :], vs[window, :]
            s = jnp.where(valid, _dot_nt(q2, kw), NEG_INF)
            p = jnp.exp(s - per_head_column(lss[cur, :]))
            ds = (p * (_dot_nt(do2, vw) - per_head_column(dss[cur, :]))).astype(BF16)
            dvs[window, :] += _dot_tn(p.astype(BF16), do2)
            dks[window, :] += _dot_tn(ds, q2)
            dqs[cur, :] = _unstack_heads(_dot(ds, kw), head0)
            return carry

        lax.fori_loop(0, SEQ // BLOCK, block, 0, unroll=2)

        hp = pl.program_id(0)
        copies = []
        for kind in range(3):
            for r in range(d):
                for i in range(nb):
                    src, dst = _group_rows(d, nb, r, i)
                    below = pl.ds(dst.start + BLOCK, BLOCK)
                    if kind == 2:
                        stage[src, :] = dvs[below, :]
                    else:
                        c, s = cos_ref[src, :], sin_ref[src, :]
                        t = dqs[dst, :] * 0.125 if kind == 0 else dks[below, :]
                        stage[src, :] = t * c - _swap_halves(t) * s
            for i in range(SEQ // MM_ROWS):
                rows = pl.ds(i * MM_ROWS, MM_ROWS)
                outs[kind, rows, :] = stage[rows, :].astype(BF16)
            column = pl.multiple_of((kind * 12 + g * 4 + hp) * 128, 128)
            copies.append(pltpu.make_async_copy(outs.at[kind], dproj_ref.at[:, pl.ds(column, 128)], sems.at[kind]))
            copies[-1].start()
        for cp in copies:
            cp.wait()

    return body


def _attn_bwd(proj, cos_t, sin_t, attn, lse, dattn, dproj, ride=None):
    groups = [_attn_bwd_group_body(g) for g in range(3)]

    def body(q0, q1, q2, k0, k1, k2, v0, v1, v2, *rest):
        for g in range(3):
            groups[g]((q0, q1, q2)[g], (k0, k1, k2)[g], (v0, v1, v2)[g], *rest)

    def col(base):
        return pl.BlockSpec((SEQ, 128), lambda hp, base=base: (0, base + hp))

    table = pl.BlockSpec((SEQ, 128), lambda hp: (0, 0))
    return _call(
        body, "attn_bwd", (4,),
        [col(g * 4) for g in range(3)] + [col(12 + g * 4) for g in range(3)] + [col(24 + g * 4) for g in range(3)]
        + [table, table, col(0), col(0), col(0), ANY],
        [ANY], [_sds((SEQ, IN_WIDTH), BF16)],
        [pltpu.VMEM((SEQ, 128), BF16)] + [pltpu.VMEM((SEQ + BLOCK, 128), BF16)] * 2 + [pltpu.VMEM((SEQ, 128), BF16)]
        + [pltpu.VMEM((SEQ, 128), F32)] * 3 + [pltpu.VMEM((SEQ + BLOCK, 128), F32)] * 2 + [pltpu.VMEM((SEQ, 128), F32)]
        + [pltpu.VMEM((3, SEQ, 128), BF16), pltpu.SemaphoreType.DMA((3,))],
        [proj] * 9 + [cos_t, sin_t, attn, lse, dattn, dproj], ride, aliases={14: 0})


SSM_CHUNKS = 4
CHUNK_STATES = 512
SCAN_ROWS = 8
U_COL = (3 * QKV_WIDTH) // 128


def _cmul(xr, xi, yr, yi):
    return xr * yr - xi * yi, xr * yi + xi * yr


def _ssm_prep(a_re, a_im, log_dt, b_re_t, b_im_t):
    def body(ar_ref, ai_ref, ldt_ref, br_ref, bi_ref, abr_ref, abi_ref, er_ref, ei_ref, bbr_ref, bbi_ref):
        ar, ai = ar_ref[...], ai_ref[...]
        dt = jnp.exp(ldt_ref[...])
        mag = jnp.exp(ar * dt)
        abr, abi = mag * jnp.cos(ai * dt), mag * jnp.sin(ai * dt)
        den = ar * ar + ai * ai
        nr, ni = abr - 1.0, abi
        er, ei = (nr * ar + ni * ai) / den, (ni * ar - nr * ai) / den
        abr_ref[...], abi_ref[...], er_ref[...], ei_ref[...] = abr, abi, er, ei
        er3, ei3 = er[:, None, :], ei[:, None, :]
        br, bi = br_ref[...], bi_ref[...]
        bbr_ref[...] = er3 * br - ei3 * bi
        bbi_ref[...] = er3 * bi + ei3 * br

    gp = jax.ShapeDtypeStruct(a_re.shape, F32)
    gb = jax.ShapeDtypeStruct(b_re_t.shape, F32)
    return _pallas_call(body, name="ssm_prep", out_shape=(gp, gp, gp, gp, gb, gb))(a_re, a_im, log_dt, b_re_t, b_im_t)


def _ssm_param_bwd(a_re, a_im, log_dt, b_re_t, b_im_t, abar_re, abar_im, e_re, e_im, ga_re, ga_im, gbb_re_t, gbb_im_t):
    def body(ar_ref, ai_ref, ldt_ref, br_ref, bi_ref, abr_ref, abi_ref, er_ref, ei_ref, gar_ref, gai_ref, gbr_ref, gbi_ref,
             o_ar, o_ai, o_ldt, o_br, o_bi):
        ar, ai = ar_ref[...], ai_ref[...]
        dt = jnp.exp(ldt_ref[...])
        er, ei = er_ref[...], ei_ref[...]
        br, bi, gbr, gbi = br_ref[...], bi_ref[...], gbr_ref[...], gbi_ref[...]
        er3, ei3 = er[:, None, :], ei[:, None, :]
        o_br[...] = er3 * gbr + ei3 * gbi
        o_bi[...] = er3 * gbi - ei3 * gbr
        ge_r = jnp.sum(br * gbr + bi * gbi, axis=1)
        ge_i = jnp.sum(br * gbi - bi * gbr, axis=1)
        den = ar * ar + ai * ai
        ilr, ili = ar / den, -ai / den
        t_r, t_i = _cmul(ilr, -ili, ge_r, ge_i)
        gab_r, gab_i = gar_ref[...] + t_r, gai_ref[...] + t_i
        gz_r, gz_i = _cmul(abr_ref[...], -abi_ref[...], gab_r, gab_i)
        el_r, el_i = _cmul(er, ei, ilr, ili)
        u_r, u_i = _cmul(el_r, -el_i, ge_r, ge_i)
        o_ar[...] = dt * gz_r - u_r
        o_ai[...] = dt * gz_i - u_i
        o_ldt[...] = jnp.sum(gz_r * ar + gz_i * ai, axis=1, keepdims=True) * dt

    gp = jax.ShapeDtypeStruct(a_re.shape, F32)
    gb = jax.ShapeDtypeStruct(b_re_t.shape, F32)
    return _pallas_call(body, name="ssm_param_bwd", out_shape=(gp, gp, jax.ShapeDtypeStruct(log_dt.shape, F32), gb, gb))(
        a_re, a_im, log_dt, b_re_t, b_im_t, abar_re, abar_im, e_re, e_im, ga_re, ga_im, gbb_re_t, gbb_im_t)


def _block_diag(blocks_re, blocks_im, sign_im, rows_are_channels):
    both = jnp.stack([blocks_re, sign_im * blocks_im]).reshape(2, SSM_CHUNKS, 8, SSM_GROUP, SSM_STATE)
    eye = jnp.eye(8, dtype=F32)
    if rows_are_channels:
        return jnp.einsum("rcghp,gk->cghrkp", both, eye).reshape(SSM_CHUNKS, 128, 2 * CHUNK_STATES)
    return jnp.einsum("rcghp,gk->crkpgh", both, eye).reshape(SSM_CHUNKS, 2 * CHUNK_STATES, 128)


def _block_diag_parts(mat, rows_are_channels):
    if rows_are_channels:
        six = mat.reshape(SSM_CHUNKS, 8, SSM_GROUP, 2, 8, SSM_STATE)
        parts = jnp.einsum("cghrgp->rcghp", six)
    else:
        six = mat.reshape(SSM_CHUNKS, 2, 8, SSM_STATE, 8, SSM_GROUP)
        parts = jnp.einsum("crgpgh->rcghp", six)
    parts = parts.reshape(2, SSM_GROUPS, SSM_GROUP, SSM_STATE)
    return parts[0], parts[1]


def _scan_consts(a_ref, conj, reverse):
    ar = jnp.broadcast_to(a_ref[:, :CHUNK_STATES], (SCAN_ROWS, CHUNK_STATES))
    ai = jnp.broadcast_to(a_ref[:, CHUNK_STATES:], (SCAN_ROWS, CHUNK_STATES))
    if conj:
        ai = -ai
    row = lax.broadcasted_iota(jnp.int32, (SCAN_ROWS, CHUNK_STATES), 0)
    if reverse:
        row = SCAN_ROWS - 1 - row
    zero = jnp.zeros_like(ar)
    steps = []
    pr, pi = ar, ai
    for shift in (1, 2, 4):
        keep = row >= shift
        steps.append((SCAN_ROWS - shift if reverse else shift, jnp.where(keep, pr, zero), jnp.where(keep, pi, zero)))
        pr, pi = _cmul(pr, pi, pr, pi)
    first = row == 0
    return steps, (jnp.where(first, ar, zero), jnp.where(first, ai, zero)), first


def _scan_tile(xr, xi, prev_r, prev_i, steps, carry_in, reverse):
    edge = SCAN_ROWS - 1 if reverse else 1
    cr, ci = pltpu.roll(prev_r, edge, axis=0), pltpu.roll(prev_i, edge, axis=0)
    xr, xi = xr + carry_in[0] * cr - carry_in[1] * ci, xi + carry_in[0] * ci + carry_in[1] * cr
    for shift, mr, mi in steps:
        sr, si = pltpu.roll(xr, shift, axis=0), pltpu.roll(xi, shift, axis=0)
        xr, xi = xr + mr * sr - mi * si, xi + mr * si + mi * sr
    return xr, xi


MM_ROWS = 256


def _ssm_fwd(proj, bmat, cmat, a_chunks, d_skip, ride=None):
    def body(u_ref, b_ref, c_ref, a_ref, d_ref, y_ref, h_ref):
        for i in range(SEQ // MM_ROWS):
            rows = pl.ds(i * MM_ROWS, MM_ROWS)
            h_ref[rows, :] = _dot(u_ref[rows, :].astype(BF16), b_ref[...])
        steps, carry_in, _ = _scan_consts(a_ref, conj=False, reverse=False)

        def tile(k, carry):
            rows = pl.ds(pl.multiple_of(k * SCAN_ROWS, SCAN_ROWS), SCAN_ROWS)
            xr, xi = _scan_tile(h_ref[rows, :CHUNK_STATES], h_ref[rows, CHUNK_STATES:], carry[0], carry[1], steps, carry_in, False)
            h_ref[rows, :CHUNK_STATES] = xr
            h_ref[rows, CHUNK_STATES:] = xi
            return xr, xi

        zero = jnp.zeros((SCAN_ROWS, CHUNK_STATES), F32)
        lax.fori_loop(0, SEQ // SCAN_ROWS, tile, (zero, zero))
        for i in range(SEQ // MM_ROWS):
            rows = pl.ds(i * MM_ROWS, MM_ROWS)
            y_ref[rows, :] = _dot(h_ref[rows, :].astype(BF16), c_ref[...]) + d_ref[...] * u_ref[rows, :]

    return _call(
        body, "ssm_fwd", (SSM_CHUNKS,),
        [pl.BlockSpec((SEQ, 128), lambda c: (0, U_COL + c)),
         pl.BlockSpec((None, 128, 2 * CHUNK_STATES), lambda c: (c, 0, 0)),
         pl.BlockSpec((None, 2 * CHUNK_STATES, 128), lambda c: (c, 0, 0)),
         pl.BlockSpec((None, 1, 2 * CHUNK_STATES), lambda c: (c, 0, 0)),
         pl.BlockSpec((1, 128), lambda c: (0, c))],
        [pl.BlockSpec((SEQ, 128), lambda c: (0, c)), pl.BlockSpec((SEQ, 2 * CHUNK_STATES), lambda c: (0, c))],
        [_sds((SEQ, SSM_WIDTH), F32), _sds((SEQ, SSM_CHUNKS * 2 * CHUNK_STATES), F32)], [],
        [proj, bmat, cmat, a_chunks, d_skip], ride)


def _ssm_bwd(dys, proj, h, bmat, cmat, a_chunks, d_skip, dproj, ride=None):
    def body(dy_ref, u_ref, h_ref, b_ref, c_ref, a_ref, d_ref, dproj_in, du_ref, db_ref, dc_ref, da_ref, dd_ref, g_ref):
        del dproj_in
        dsum = jnp.zeros((1, 128), F32)
        dcm = jnp.zeros((2 * CHUNK_STATES, 128), F32)
        for i in range(SEQ // MM_ROWS):
            rows = pl.ds(i * MM_ROWS, MM_ROWS)
            dy = dy_ref[rows, :]
            g_ref[rows, :] = _dot_nt(dy.astype(BF16), c_ref[...])
            dsum += jnp.sum(dy * u_ref[rows, :], axis=0, keepdims=True)
            dcm += _dot_tn(h_ref[rows, :].astype(BF16), dy.astype(BF16))
        dd_ref[...] = dsum
        dc_ref[...] = dcm
        steps, carry_in, _ = _scan_consts(a_ref, conj=True, reverse=True)
        first_row = lax.broadcasted_iota(jnp.int32, (SCAN_ROWS, CHUNK_STATES), 0) == 0
        n_tiles = SEQ // SCAN_ROWS

        def tile(j, carry):
            k = n_tiles - 1 - j
            rows = pl.ds(pl.multiple_of(k * SCAN_ROWS, SCAN_ROWS), SCAN_ROWS)
            before = pl.ds(pl.multiple_of(jnp.maximum(k - 1, 0) * SCAN_ROWS, SCAN_ROWS), SCAN_ROWS)
            gr, gi = _scan_tile(g_ref[rows, :CHUNK_STATES], g_ref[rows, CHUNK_STATES:], carry[0], carry[1], steps, carry_in, True)
            g_ref[rows, :CHUNK_STATES] = gr
            g_ref[rows, CHUNK_STATES:] = gi
            has_before = jnp.where(k > 0, 1.0, 0.0)
            hr = jnp.where(first_row, pltpu.roll(h_ref[before, :CHUNK_STATES], 1, axis=0) * has_before,
                           pltpu.roll(h_ref[rows, :CHUNK_STATES], 1, axis=0))
            hi = jnp.where(first_row, pltpu.roll(h_ref[before, CHUNK_STATES:], 1, axis=0) * has_before,
                           pltpu.roll(h_ref[rows, CHUNK_STATES:], 1, axis=0))
            return gr, gi, carry[2] + hr * gr + hi * gi, carry[3] + hr * gi - hi * gr

        zero = jnp.zeros((SCAN_ROWS, CHUNK_STATES), F32)
        _, _, sar, sai = lax.fori_loop(0, n_tiles, tile, (zero, zero, zero, zero))
        da_ref[:, :CHUNK_STATES] = jnp.sum(sar, axis=0, keepdims=True)
        da_ref[:, CHUNK_STATES:] = jnp.sum(sai, axis=0, keepdims=True)
        dbm = jnp.zeros((128, 2 * CHUNK_STATES), F32)
        for i in range(SEQ // MM_ROWS):
            rows = pl.ds(i * MM_ROWS, MM_ROWS)
            g = g_ref[rows, :].astype(BF16)
            du_ref[rows, :] = (_dot_nt(g, b_ref[...]) + d_ref[...] * dy_ref[rows, :]).astype(BF16)
            dbm += _dot_tn(u_ref[rows, :].astype(BF16), g)
        db_ref[...] = dbm

    chunk_col = pl.BlockSpec((SEQ, 128), lambda c: (0, c))
    return _call(
        body, "ssm_bwd", (SSM_CHUNKS,),
        [chunk_col,
         pl.BlockSpec((SEQ, 128), lambda c: (0, U_COL + c)),
         pl.BlockSpec((SEQ, 2 * CHUNK_STATES), lambda c: (0, c)),
         pl.BlockSpec((None, 128, 2 * CHUNK_STATES), lambda c: (c, 0, 0)),
         pl.BlockSpec((None, 2 * CHUNK_STATES, 128), lambda c: (c, 0, 0)),
         pl.BlockSpec((None, 1, 2 * CHUNK_STATES), lambda c: (c, 0, 0)),
         pl.BlockSpec((1, 128), lambda c: (0, c)), ANY],
        [pl.BlockSpec((SEQ, 128), lambda c: (0, U_COL + c)),
         pl.BlockSpec((None, 128, 2 * CHUNK_STATES), lambda c: (c, 0, 0)),
         pl.BlockSpec((None, 2 * CHUNK_STATES, 128), lambda c: (c, 0, 0)),
         pl.BlockSpec((None, 1, 2 * CHUNK_STATES), lambda c: (c, 0, 0)),
         pl.BlockSpec((1, 128), lambda c: (0, c))],
        [_sds((SEQ, IN_WIDTH), BF16), _sds((SSM_CHUNKS, 128, 2 * CHUNK_STATES), F32),
         _sds((SSM_CHUNKS, 2 * CHUNK_STATES, 128), F32), _sds((SSM_CHUNKS, 1, 2 * CHUNK_STATES), F32), _sds((1, SSM_WIDTH), F32)],
        [pltpu.VMEM((SEQ, 2 * CHUNK_STATES), F32)], [dys, proj, h, bmat, cmat, a_chunks, d_skip, dproj], ride, aliases={7: 0})


def _ssm_tables(abar_re, abar_im, bbar_re_t, bbar_im_t, c_re, c_im):
    bmat = _block_diag(bbar_re_t, bbar_im_t, 1.0, True).astype(BF16)
    cmat = _block_diag(c_re, c_im, -1.0, False).astype(BF16)
    a_chunks = jnp.concatenate([abar_re.reshape(SSM_CHUNKS, 1, CHUNK_STATES), abar_im.reshape(SSM_CHUNKS, 1, CHUNK_STATES)], axis=2)
    return bmat, cmat, a_chunks


GL_COL = (3 * QKV_WIDTH + SSM_WIDTH) // D_MODEL
GELU_C = math.sqrt(2.0 / math.pi)
GELU_A = 0.044715


def _sds(shape, dtype):
    return jax.ShapeDtypeStruct(shape, dtype)


def _gelu(x):
    t = jnp.tanh(GELU_C * (x + GELU_A * x * x * x))
    return 0.5 * x * (1.0 + t), t


def _gelu_grad(x, t):
    return 0.5 * (1.0 + t) + 0.5 * x * (1.0 - t * t) * GELU_C * (1.0 + 3.0 * GELU_A * x * x)


def _layer_norm(r, g, b):
    mu = jnp.mean(r, axis=-1, keepdims=True)
    xc = r - mu
    rstd = lax.rsqrt(jnp.mean(xc * xc, axis=-1, keepdims=True) + LN_EPS)
    xhat = xc * rstd
    return xhat * g + b, xhat, rstd


def _layer_norm_bwd(dy, xhat, rstd, g):
    dxhat = dy * g
    m1 = jnp.mean(dxhat, axis=-1, keepdims=True)
    m2 = jnp.mean(dxhat * xhat, axis=-1, keepdims=True)
    return rstd * (dxhat - m1 - xhat * m2)


def _proj(x, w_in, ride=None):
    tm, tn = 512, 1792

    def body(x_ref, w_ref, o_ref):
        o_ref[...] = _dot(x_ref[...].astype(BF16), _side_by_side(w_ref))

    return _call(
        body, "proj", (SEQ // tm, IN_WIDTH // tn),
        [pl.BlockSpec((tm, D_MODEL), lambda i, j: (i, 0)), pl.BlockSpec((2, D_MODEL, tn // 2), lambda i, j: (j, 0, 0))],
        [pl.BlockSpec((tm, tn), lambda i, j: (i, j))], [_sds((SEQ, IN_WIDTH), F32)], [], [x, w_in], ride)


def _row_spec(tm, width, col=0):
    return pl.BlockSpec((tm, width), lambda i, col=col: (i, col))


def _full_spec(shape):
    return pl.BlockSpec(shape, lambda i: (0,) * len(shape))


def _mixer_out(attn, ys, proj, x, w_ab, w_sb, w_glu, w_out, b_gate, ln_g, ln_b):
    tm = 256

    def body(attn_ref, ys_ref, gl0_ref, gl1_ref, x_ref, wab_ref, wsb_ref, wglu_ref, wout_ref, bg_ref, g_ref, b_ref,
             h_ref, xhat_ref, rstd_ref, glu_ref, ya_ref, yssm_ref):
        gy, _ = _gelu(ys_ref[...])
        glu = _dot(gy.astype(BF16), _side_by_side(wglu_ref))
        glu_ref[...] = glu
        y_s = glu[:, :SSM_WIDTH] * jax.nn.sigmoid(glu[:, SSM_WIDTH:])
        y_ssm = _dot(y_s.astype(BF16), _side_by_side(wsb_ref))
        y_attn = _dot(attn_ref[...].astype(BF16), _side_by_side(wab_ref))
        ya_ref[...] = y_attn
        yssm_ref[...] = y_ssm
        g0 = jax.nn.sigmoid(gl0_ref[...] + _side_by_side(bg_ref, 0))
        g1 = jax.nn.sigmoid(gl1_ref[...] + _side_by_side(bg_ref, 1))
        mixed = g0 * y_attn + g1 * y_ssm
        r1 = DN_ALPHA * x_ref[...] + _dot(mixed.astype(BF16), wout_ref[...])
        h, xhat, rstd = _layer_norm(r1, g_ref[...], b_ref[...])
        h_ref[...] = h
        xhat_ref[...] = xhat
        rstd_ref[...] = jnp.broadcast_to(rstd, (tm, 128))

    wide = _sds((SEQ, D_MODEL), F32)
    return _pallas_call(
        body, name="mixer_out", grid=(SEQ // tm,),
        in_specs=[_row_spec(tm, ATTN_WIDTH), _row_spec(tm, SSM_WIDTH), _row_spec(tm, D_MODEL, GL_COL), _row_spec(tm, D_MODEL, GL_COL + 1),
                  _row_spec(tm, D_MODEL), _full_spec((N_DEV, ATTN_WIDTH, 128)), _full_spec((N_DEV, SSM_WIDTH, 128)),
                  _full_spec((N_DEV, SSM_WIDTH, 128)), _full_spec((D_MODEL, D_MODEL)), _full_spec((N_DEV, 2, 128)),
                  _full_spec((1, D_MODEL)), _full_spec((1, D_MODEL))],
        out_specs=(_row_spec(tm, D_MODEL), _row_spec(tm, D_MODEL), _row_spec(tm, 128), _row_spec(tm, D_MODEL),
                   _row_spec(tm, D_MODEL), _row_spec(tm, D_MODEL)),
        out_shape=(wide, wide, _sds((SEQ, 128), F32), wide, wide, wide),
        compiler_params=_cparams(dimension_semantics=("arbitrary",)),
    )(attn, ys, proj, proj, x, w_ab, w_sb, w_glu, w_out, b_gate, ln_g, ln_b)


def _ff_up(h, w_gate, w_up):
    tm, tn = 512, 768

    def body(h_ref, wg_ref, wu_ref, a_ref, b_ref, f_ref):
        hb = h_ref[...].astype(BF16)
        a, b = _dot(hb, _side_by_side(wg_ref)), _dot(hb, _side_by_side(wu_ref))
        a_ref[...] = a.astype(BF16)
        b_ref[...] = b.astype(BF16)
        f_ref[...] = (a * jax.nn.sigmoid(a) * b).astype(BF16)

    tile = pl.BlockSpec((tm, tn), lambda i, j: (i, j))
    wtile = pl.BlockSpec((tn // FF_PAD, D_MODEL, FF_PAD), lambda i, j: (j, 0, 0))
    out = _sds((SEQ, D_FF_PAD), BF16)
    return _pallas_call(
        body, name="ff_up", grid=(SEQ // tm, D_FF_PAD // tn),
        in_specs=[pl.BlockSpec((tm, D_MODEL), lambda i, j: (i, 0)), wtile, wtile],
        out_specs=(tile, tile, tile), out_shape=(out, out, out),
        compiler_params=_cparams(dimension_semantics=("arbitrary", "arbitrary")),
    )(h, w_gate, w_up)


def _ff_down_loss(f, w_down, h, target, ln_g, ln_b):
    tm = 256

    def body(f_ref, w_ref, h_ref, t_ref, g_ref, b_ref, dr_ref, dg_ref, db_ref, loss_ref):
        @pl.when(pl.program_id(0) == 0)
        def _():
            dg_ref[...] = jnp.zeros_like(dg_ref)
            db_ref[...] = jnp.zeros_like(db_ref)
            loss_ref[...] = jnp.zeros_like(loss_ref)

        r2 = DN_ALPHA * h_ref[...] + _dot(f_ref[...], w_ref[...])
        g = g_ref[...]
        out, xhat, rstd = _layer_norm(r2, g, b_ref[...])
        err = out - t_ref[...]
        loss_ref[...] += 0.5 * jnp.sum(jnp.mean(err * err, axis=-1, keepdims=True), axis=0, keepdims=True)
        dout = err * (1.0 / D_MODEL)
        dg_ref[...] += jnp.sum(dout * xhat, axis=0, keepdims=True)
        db_ref[...] += jnp.sum(dout, axis=0, keepdims=True)
        dr_ref[...] = _layer_norm_bwd(dout, xhat, rstd, g)

    vec = _sds((1, D_MODEL), F32)
    return _pallas_call(
        body, name="ff_down_loss", grid=(SEQ // tm,),
        in_specs=[_row_spec(tm, D_FF_PAD), _full_spec((D_FF_PAD, D_MODEL)), _row_spec(tm, D_MODEL), _row_spec(tm, D_MODEL),
                  _full_spec((1, D_MODEL)), _full_spec((1, D_MODEL))],
        out_specs=(_row_spec(tm, D_MODEL), _full_spec((1, D_MODEL)), _full_spec((1, D_MODEL)), _full_spec((1, 128))),
        out_shape=(_sds((SEQ, D_MODEL), F32), vec, vec, _sds((1, 128), F32)),
        compiler_params=_cparams(dimension_semantics=("arbitrary",)),
    )(f, w_down, h, target, ln_g, ln_b)


def _ff_down_bwd(dr2, w_down, a, b):
    tm, tn = 512, 768

    def body(dr_ref, w_ref, a_ref, b_ref, da_ref, db_ref):
        df = _dot_nt(dr_ref[...].astype(BF16), w_ref[...])
        av, bv = a_ref[...].astype(F32), b_ref[...].astype(F32)
        sg = jax.nn.sigmoid(av)
        da_ref[...] = (df * bv * sg * (1.0 + av * (1.0 - sg))).astype(BF16)
        db_ref[...] = (df * av * sg).astype(BF16)

    tile = pl.BlockSpec((tm, tn), lambda i, j: (i, j))
    out = _sds((SEQ, D_FF_PAD), BF16)
    return _pallas_call(
        body, name="ff_down_bwd", grid=(SEQ // tm, D_FF_PAD // tn),
        in_specs=[pl.BlockSpec((tm, D_MODEL), lambda i, j: (i, 0)), pl.BlockSpec((tn, D_MODEL), lambda i, j: (j, 0)), tile, tile],
        out_specs=(tile, tile), out_shape=(out, out),
        compiler_params=_cparams(dimension_semantics=("arbitrary", "arbitrary")),
    )(dr2, w_down, a, b)


def _ff_up_bwd(da, db, w_gate, w_up, dr2, xhat1, rstd1, ln_g, ride=None):
    tm, tk = 512, 768
    nk = D_FF_PAD // tk

    def body(da_ref, db_ref, wg_ref, wu_ref, dr2_ref, xhat_ref, rstd_ref, g_ref, dr1_ref, dg_ref, dbias_ref, acc):
        i, k = pl.program_id(0), pl.program_id(1)

        @pl.when(jnp.logical_and(i == 0, k == 0))
        def _():
            dg_ref[...] = jnp.zeros_like(dg_ref)
            dbias_ref[...] = jnp.zeros_like(dbias_ref)

        part = _dot_nt(da_ref[...], _side_by_side(wg_ref)) + _dot_nt(db_ref[...], _side_by_side(wu_ref))

        @pl.when(k == 0)
        def _():
            acc[...] = part

        @pl.when(k > 0)
        def _():
            acc[...] += part

        @pl.when(k == nk - 1)
        def _():
            dh = DN_ALPHA * dr2_ref[...] + acc[...]
            xhat = xhat_ref[...]
            dg_ref[...] += jnp.sum(dh * xhat, axis=0, keepdims=True)
            dbias_ref[...] += jnp.sum(dh, axis=0, keepdims=True)
            rstd = jnp.max(rstd_ref[...], axis=1, keepdims=True)
            dr1_ref[...] = _layer_norm_bwd(dh, xhat, rstd, g_ref[...])

    hid = pl.BlockSpec((tm, tk), lambda i, k: (i, k))
    wtile = pl.BlockSpec((tk // FF_PAD, D_MODEL, FF_PAD), lambda i, k: (k, 0, 0))
    row = pl.BlockSpec((tm, D_MODEL), lambda i, k: (i, 0))
    vec = pl.BlockSpec((1, D_MODEL), lambda i, k: (0, 0))
    return _call(
        body, "ff_up_bwd", (SEQ // tm, nk),
        [hid, hid, wtile, wtile, row, row, pl.BlockSpec((tm, 128), lambda i, k: (i, 0)), vec],
        [row, vec, vec], [_sds((SEQ, D_MODEL), F32), _sds((1, D_MODEL), F32), _sds((1, D_MODEL), F32)],
        [pltpu.VMEM((tm, D_MODEL), F32)], [da, db, w_gate, w_up, dr2, xhat1, rstd1, ln_g], ride)


def _mixer_bwd(dr1, proj, y_attn, y_ssm, glu, ys, w_ab, w_sb, w_glu, w_out, b_gate):
    tm = 256

    def body(dr1_ref, gl0_ref, gl1_ref, ya_ref, yssm_ref, glu_ref, ys_ref, wab_ref, wsb_ref, wglu_ref, wout_ref, bg_ref,
             dya_ref, dyssm_ref, dgl_ref, dattn_ref, dglu_ref, dys_ref, mixed_ref, ysb_ref, gy_ref, dbg_ref):
        @pl.when(pl.program_id(0) == 0)
        def _():
            dbg_ref[...] = jnp.zeros_like(dbg_ref)

        dmixed = _dot_nt(dr1_ref[...].astype(BF16), wout_ref[...])
        g0 = jax.nn.sigmoid(gl0_ref[...] + _side_by_side(bg_ref, 0))
        g1 = jax.nn.sigmoid(gl1_ref[...] + _side_by_side(bg_ref, 1))
        y_attn, y_ssm = ya_ref[...], yssm_ref[...]
        mixed_ref[...] = (g0 * y_attn + g1 * y_ssm).astype(BF16)
        dya = (dmixed * g0).astype(BF16)
        dyssm = (dmixed * g1).astype(BF16)
        dya_ref[...] = dya
        dyssm_ref[...] = dyssm
        dgl0 = dmixed * y_attn * g0 * (1.0 - g0)
        dgl1 = dmixed * y_ssm * g1 * (1.0 - g1)
        dgl_ref[:, :GL_COL * D_MODEL] = jnp.zeros((tm, GL_COL * D_MODEL), BF16)
        dgl_ref[:, GL_COL * D_MODEL:(GL_COL + 1) * D_MODEL] = dgl0.astype(BF16)
        dgl_ref[:, (GL_COL + 1) * D_MODEL:] = dgl1.astype(BF16)
        dbg_ref[:, :D_MODEL] += jnp.sum(dgl0, axis=0, keepdims=True)
        dbg_ref[:, D_MODEL:] += jnp.sum(dgl1, axis=0, keepdims=True)
        dattn_ref[...] = _dot_nt(dya, _side_by_side(wab_ref))
        dy_s = _dot_nt(dyssm, _side_by_side(wsb_ref))
        glu = glu_ref[...]
        glu1, sg = glu[:, :SSM_WIDTH], jax.nn.sigmoid(glu[:, SSM_WIDTH:])
        ysb_ref[...] = (glu1 * sg).astype(BF16)
        dglu1 = (dy_s * sg).astype(BF16)
        dglu2 = (dy_s * glu1 * sg * (1.0 - sg)).astype(BF16)
        dglu_ref[:, :SSM_WIDTH] = dglu1
        dglu_ref[:, SSM_WIDTH:] = dglu2
        dgy = _dot_nt(jnp.concatenate([dglu1, dglu2], axis=1), _side_by_side(wglu_ref))
        ys = ys_ref[...]
        gy, t = _gelu(ys)
        gy_ref[...] = gy.astype(BF16)
        dys_ref[...] = dgy * _gelu_grad(ys, t)

    wide_b, half_b = _sds((SEQ, D_MODEL), BF16), _sds((SEQ, SSM_WIDTH), BF16)
    half_f = _sds((SEQ, SSM_WIDTH), F32)
    return _pallas_call(
        body, name="mixer_bwd", grid=(SEQ // tm,),
        in_specs=[_row_spec(tm, D_MODEL), _row_spec(tm, D_MODEL, GL_COL), _row_spec(tm, D_MODEL, GL_COL + 1), _row_spec(tm, D_MODEL),
                  _row_spec(tm, D_MODEL), _row_spec(tm, D_MODEL), _row_spec(tm, SSM_WIDTH), _full_spec((N_DEV, ATTN_WIDTH, 128)),
                  _full_spec((N_DEV, SSM_WIDTH, 128)), _full_spec((N_DEV, SSM_WIDTH, 128)), _full_spec((D_MODEL, D_MODEL)),
                  _full_spec((N_DEV, 2, 128))],
        out_specs=(_row_spec(tm, D_MODEL), _row_spec(tm, D_MODEL), _row_spec(tm, IN_WIDTH), _row_spec(tm, ATTN_WIDTH),
                   _row_spec(tm, D_MODEL), _row_spec(tm, SSM_WIDTH), _row_spec(tm, D_MODEL), _row_spec(tm, SSM_WIDTH),
                   _row_spec(tm, SSM_WIDTH), _full_spec((1, 2 * D_MODEL))),
        out_shape=(wide_b, wide_b, _sds((SEQ, IN_WIDTH), BF16), half_f, wide_b, half_f, wide_b, half_b, half_b,
                   _sds((1, 2 * D_MODEL), F32)),
        compiler_params=_cparams(dimension_semantics=("arbitrary",)),
    )(dr1, proj, proj, y_attn, y_ssm, glu, ys, w_ab, w_sb, w_glu, w_out, b_gate)


def _grad_x(dproj, w_in, dr1, ride=None):
    tm, tk = 512, 1792
    nk = IN_WIDTH // tk

    def body(dp_ref, w_ref, dr1_ref, o_ref, acc):
        k = pl.program_id(1)
        part = _dot_nt(dp_ref[...], _side_by_side(w_ref))

        @pl.when(k == 0)
        def _():
            acc[...] = part

        @pl.when(k > 0)
        def _():
            acc[...] += part

        @pl.when(k == nk - 1)
        def _():
            o_ref[...] = DN_ALPHA * dr1_ref[...] + acc[...]

    row = pl.BlockSpec((tm, D_MODEL), lambda i, k: (i, 0))
    return _call(
        body, "grad_x", (SEQ // tm, nk),
        [pl.BlockSpec((tm, tk), lambda i, k: (i, k)), pl.BlockSpec((2, D_MODEL, tk // 2), lambda i, k: (k, 0, 0)), row],
        [row], [_sds((SEQ, D_MODEL), F32)], [pltpu.VMEM((tm, D_MODEL), F32)], [dproj, w_in, dr1], ride)


def _weight_grad(a, b, name, shard_cols=None):
    k, n = a.shape[1], b.shape[1]
    tm = 512
    nm = SEQ // tm
    tk = min(k, 512) if shard_cols else k // N_DEV
    tn = n // 4 if shard_cols else min(n, 1024)

    def body(a_ref, b_ref, o_ref, acc):
        m = pl.program_id(2)
        part = _dot_tn(a_ref[...].astype(BF16), b_ref[...].astype(BF16))

        @pl.when(m == 0)
        def _():
            acc[...] = part

        @pl.when(m > 0)
        def _():
            acc[...] += part

        @pl.when(m == nm - 1)
        def _():
            if shard_cols:
                o_ref[0] = acc[:, :shard_cols].astype(BF16)
                o_ref[1] = acc[:, shard_cols:].astype(BF16)
            else:
                o_ref[...] = acc[...].astype(BF16)

    if shard_cols:
        out_spec = pl.BlockSpec((2, None, tk, shard_cols), lambda kk, j, m: (0, j, kk, 0))
        out_shape = _sds((2, 4, k, shard_cols), BF16)
    else:
        out_spec = pl.BlockSpec((None, None, tk, tn), lambda kk, j, m: (kk % 2, kk // 2, 0, j))
        out_shape = _sds((2, 4, tk, n), BF16)
    return _pallas_call(
        body, name=name, grid=(k // tk, n // tn, nm),
        in_specs=[pl.BlockSpec((tm, tk), lambda kk, j, m: (m, kk)), pl.BlockSpec((tm, tn), lambda kk, j, m: (m, j))],
        out_specs=out_spec, out_shape=out_shape, scratch_shapes=[pltpu.VMEM((tk, tn), F32)],
        compiler_params=_cparams(dimension_semantics=("arbitrary", "arbitrary", "arbitrary")),
    )(a, b)


MESH = pl.DeviceIdType.MESH
ANY = pl.BlockSpec(memory_space=pl.ANY)


def _place():
    return lax.axis_index("x"), lax.axis_index("y"), lax.axis_index("c")


def _other_chips(x, y):
    return [(1 - x, y), (x, 1 - y), (1 - x, 1 - y)]


class _Ride:
    def __init__(self, operands, results, aliases, sems, start, wait):
        self.operands, self.results, self.aliases, self.sems = list(operands), list(results), dict(aliases), list(sems)
        self.start, self.wait = start, wait

    def __add__(self, other):
        n_in, n_out, n_sem = len(self.operands), len(self.results), len(self.sems)

        def both(which):
            def run(ins, outs, sems):
                getattr(self, which)(ins[:n_in], outs[:n_out], sems[:n_sem])
                getattr(other, which)(ins[n_in:], outs[n_out:], sems[n_sem:])
            return run

        aliases = {**self.aliases, **{n_in + i: n_out + j for i, j in other.aliases.items()}}
        return _Ride(self.operands + other.operands, self.results + other.results, aliases, self.sems + other.sems,
                     both("start"), both("wait"))


def _call(body, name, grid, in_specs, out_specs, out_shape, scratch_shapes, operands, ride=None, aliases=None):
    in_specs, out_specs, out_shape = list(in_specs), list(out_specs), list(out_shape)
    scratch_shapes, operands, aliases = list(scratch_shapes), list(operands), dict(aliases or {})
    kernel_body = body
    if ride is not None:
        n_in, n_out, n_scr, r_in, r_out = len(in_specs), len(out_specs), len(scratch_shapes), len(ride.operands), len(ride.results)

        def kernel_body(*refs):
            out0, scr0 = n_in + r_in, n_in + r_in + n_out + r_out
            ride_refs = (refs[n_in:out0], refs[out0 + n_out:scr0], refs[scr0 + n_scr:])
            ids = [pl.program_id(i) for i in range(len(grid))]
            first = functools.reduce(jnp.logical_and, [i == 0 for i in ids])
            last = functools.reduce(jnp.logical_and, [i == g - 1 for i, g in zip(ids, grid)])

            @pl.when(first)
            def _():
                ride.start(*ride_refs)

            body(*refs[:n_in], *refs[out0:out0 + n_out], *refs[scr0:scr0 + n_scr])

            @pl.when(last)
            def _():
                ride.wait(*ride_refs)

        aliases.update({n_in + i: n_out + j for i, j in ride.aliases.items()})
        in_specs += [ANY] * r_in
        out_specs += [ANY] * r_out
        out_shape += ride.results
        scratch_shapes += ride.sems
        operands += ride.operands
    return _pallas_call(
        kernel_body, name=name, grid=grid, in_specs=in_specs, out_specs=out_specs, out_shape=out_shape,
        scratch_shapes=scratch_shapes, input_output_aliases=aliases,
        compiler_params=_cparams(dimension_semantics=("arbitrary",) * len(grid)),
    )(*operands)


def _gather_first_level(shards):
    n = len(shards)

    def copies(ins, outs, sems, landed):
        send_sems, recv_sems, local_sems = sems
        x, y, c = _place()
        peers = [(x, y, 1 - c)] + [(px, py, c) for px, py in _other_chips(x, y)]

        def row(peer):
            return 4 * x + 2 * y + c if not landed else 4 * peer[0] + 2 * peer[1] + peer[2]

        local = [pltpu.make_async_copy(ins[a], outs[a].at[4 * x + 2 * y + c], local_sems.at[a]) for a in range(n)]
        remote = [pltpu.make_async_remote_copy(
            src_ref=ins[a], dst_ref=outs[a].at[row(peer)], send_sem=send_sems.at[a, k], recv_sem=recv_sems.at[a, k],
            device_id=peer, device_id_type=MESH) for a in range(n) for k, peer in enumerate(peers)]
        return local, remote

    def start(ins, outs, sems):
        local, remote = copies(ins, outs, sems, False)
        for cp in local + remote:
            cp.start()

    def wait(ins, outs, sems):
        local, sent = copies(ins, outs, sems, False)
        for cp in copies(ins, outs, sems, True)[1]:
            cp.wait_recv()
        for cp in sent:
            cp.wait_send()
        for cp in local:
            cp.wait()

    return _Ride(shards, [_sds((N_DEV,) + s.shape, s.dtype) for s in shards], {},
                 [pltpu.SemaphoreType.DMA((n, 4)), pltpu.SemaphoreType.DMA((n, 4)), pltpu.SemaphoreType.DMA((n,))], start, wait)


def _gather_second_level(buffers):
    n = len(buffers)

    def copies(outs, sems, core):
        send_sems, recv_sems = sems
        x, y, c = _place()
        return [pltpu.make_async_remote_copy(
            src_ref=outs[a].at[4 * px + 2 * py + core], dst_ref=outs[a].at[4 * px + 2 * py + core], send_sem=send_sems.at[a, j],
            recv_sem=recv_sems.at[a, j], device_id=(x, y, 1 - c), device_id_type=MESH)
            for a in range(n) for j, (px, py) in enumerate(_other_chips(x, y))]

    def start(ins, outs, sems):
        for cp in copies(outs, sems, lax.axis_index("c")):
            cp.start()

    def wait(ins, outs, sems):
        for cp in copies(outs, sems, 1 - lax.axis_index("c")):
            cp.wait_recv()
        for cp in copies(outs, sems, lax.axis_index("c")):
            cp.wait_send()

    return _Ride(buffers, [_sds(b.shape, b.dtype) for b in buffers], {i: i for i in range(n)},
                 [pltpu.SemaphoreType.DMA((n, 3)), pltpu.SemaphoreType.DMA((n, 3))], start, wait)


def _sibling_swap_ride(grads):
    n = len(grads)

    def copies(ins, outs, sems):
        x, y, c = _place()
        return [pltpu.make_async_remote_copy(
            src_ref=ins[a].at[1 - c], dst_ref=outs[a], send_sem=sems[0].at[a], recv_sem=sems[1].at[a],
            device_id=(x, y, 1 - c), device_id_type=MESH) for a in range(n)]

    def start(ins, outs, sems):
        for cp in copies(ins, outs, sems):
            cp.start()

    def wait(ins, outs, sems):
        for cp in copies(ins, outs, sems):
            cp.wait()

    return _Ride(grads, [_sds(g.shape[1:], g.dtype) for g in grads], {},
                 [pltpu.SemaphoreType.DMA((n,)), pltpu.SemaphoreType.DMA((n,))], start, wait)


def _chip_swap_ride(sums):
    n = len(sums)

    def copies(ins, outs, sems, landed):
        send_sems, recv_sems, local_sems = sems
        x, y, c = _place()
        mine = 2 * x + y
        local = [pltpu.make_async_copy(ins[a].at[mine], outs[a].at[mine], local_sems.at[a]) for a in range(n)]
        remote = [pltpu.make_async_remote_copy(
            src_ref=ins[a].at[2 * px + py], dst_ref=outs[a].at[2 * px + py if landed else mine], send_sem=send_sems.at[a, j],
            recv_sem=recv_sems.at[a, j], device_id=(px, py, c), device_id_type=MESH)
            for a in range(n) for j, (px, py) in enumerate(_other_chips(x, y))]
        return local, remote

    def start(ins, outs, sems):
        local, remote = copies(ins, outs, sems, False)
        for cp in local + remote:
            cp.start()

    def wait(ins, outs, sems):
        local, sent = copies(ins, outs, sems, False)
        for cp in copies(ins, outs, sems, True)[1]:
            cp.wait_recv()
        for cp in sent:
            cp.wait_send()
        for cp in local:
            cp.wait()

    return _Ride(sums, [_sds(s.shape, s.dtype) for s in sums], {},
                 [pltpu.SemaphoreType.DMA((n, 3)), pltpu.SemaphoreType.DMA((n, 3)), pltpu.SemaphoreType.DMA((n,))], start, wait)


def _send_buffer(w, rows, cols, name):
    r, c = w.shape

    def body(w_ref, o_ref):
        if (r, c) != (rows, cols):
            o_ref[...] = jnp.zeros((rows, cols), BF16)
        o_ref[:r, :c] = w_ref[...].astype(BF16)

    return _pallas_call(body, name=name, out_shape=_sds((rows, cols), BF16))(w)


def _all_gather(shards, name):
    n = len(shards)

    def body(*refs):
        ins, outs = refs[:n], refs[n:2 * n]
        send_sems, recv_sems, local_sems = refs[2 * n:]
        x, y, c = _place()
        me, sibling = (x, y, c), (x, y, 1 - c)
        chips = _other_chips(x, y)

        def slot(a, px, py, pc):
            return outs[a].at[4 * px + 2 * py + pc]

        def copy(a, k, block, to, src=None):
            return pltpu.make_async_remote_copy(
                src_ref=slot(a, *block) if src is None else src, dst_ref=slot(a, *block),
                send_sem=send_sems.at[a, k], recv_sem=recv_sems.at[a, k], device_id=to, device_id_type=MESH)

        mine = [pltpu.make_async_copy(ins[a], slot(a, *me), local_sems.at[a]) for a in range(n)]
        for cp in mine:
            cp.start()
        first = []
        for a in range(n):
            first.append(copy(a, 0, me, sibling, src=ins[a]))
            first += [copy(a, 1 + j, me, (*chip, c), src=ins[a]) for j, chip in enumerate(chips)]
        for cp in first:
            cp.start()
        passed = []
        for j, chip in enumerate(chips):
            for a in range(n):
                copy(a, 1 + j, (*chip, c), me).wait_recv()
                onward = copy(a, 4 + j, (*chip, c), sibling)
                onward.start()
                passed.append(onward)
        for a in range(n):
            copy(a, 0, sibling, me).wait_recv()
            for j, chip in enumerate(chips):
                copy(a, 4 + j, (*chip, 1 - c), me).wait_recv()
        for cp in first + passed:
            cp.wait_send()
        for cp in mine:
            cp.wait()

    return _pallas_call(
        body, name=name, in_specs=[ANY] * n, out_specs=[ANY] * n,
        out_shape=[_sds((N_DEV,) + s.shape, s.dtype) for s in shards],
        scratch_shapes=[pltpu.SemaphoreType.DMA((n, 7)), pltpu.SemaphoreType.DMA((n, 7)), pltpu.SemaphoreType.DMA((n,))],
    )(*shards)


def _swap_with_sibling(grads, name):
    n = len(grads)

    def body(*refs):
        ins, outs = refs[:n], refs[n:2 * n]
        send_sems, recv_sems = refs[2 * n:]
        x, y, c = _place()
        copies = [pltpu.make_async_remote_copy(
            src_ref=ins[a].at[1 - c], dst_ref=outs[a], send_sem=send_sems.at[a], recv_sem=recv_sems.at[a],
            device_id=(x, y, 1 - c), device_id_type=MESH) for a in range(n)]
        for cp in copies:
            cp.start()
        for cp in copies:
            cp.wait()

    return _pallas_call(
        body, name=name, in_specs=[ANY] * n, out_specs=[ANY] * n,
        out_shape=[_sds(g.shape[1:], g.dtype) for g in grads],
        scratch_shapes=[pltpu.SemaphoreType.DMA((n,)), pltpu.SemaphoreType.DMA((n,))],
    )(*grads)


def _pair_sum(g, r, core, name):
    _, _, k, n = g.shape

    def body(core_ref, g_ref, r_ref, o_ref):
        o_ref[...] = (g_ref[...].astype(F32) + r_ref[...].astype(F32)).astype(o_ref.dtype)

    return _pallas_call(
        body, name=name,
        grid_spec=pltpu.PrefetchScalarGridSpec(
            num_scalar_prefetch=1, grid=(4,),
            in_specs=[pl.BlockSpec((None, None, k, n), lambda p, core_ref: (core_ref[0], p, 0, 0)),
                      pl.BlockSpec((None, k, n), lambda p, core_ref: (p, 0, 0))],
            out_specs=pl.BlockSpec((None, k, n), lambda p, core_ref: (p, 0, 0))),
        out_shape=_sds((4, k, n), g.dtype), compiler_params=_cparams(dimension_semantics=("arbitrary",)),
    )(core, g, r)


def _adamw_math(w, g, m, v):
    m = ADAM_B1 * m + (1.0 - ADAM_B1) * g
    v = ADAM_B2 * v + (1.0 - ADAM_B2) * (g * g)
    m_hat = m / (1.0 - ADAM_B1 ** ADAM_STEP)
    v_hat = v / (1.0 - ADAM_B2 ** ADAM_STEP)
    return -ADAM_LR * (m_hat / (jnp.sqrt(v_hat) + ADAM_EPS) + ADAM_WD * w), m, v


def _adamw(w, m, v, parts, name):
    r, c = w.shape
    tr = r if r <= 512 else 256
    n_parts, pr, pc = parts.shape
    assert r % tr == 0 and (tr == r or pr == r)

    def body(w_ref, m_ref, v_ref, p_ref, g_out, d_out, m_out, v_out):
        g = p_ref[0, :tr, :c].astype(F32)
        for p in range(1, n_parts):
            g = g + p_ref[p, :tr, :c].astype(F32)
        g_out[...] = g
        d_out[...], m_out[...], v_out[...] = _adamw_math(w_ref[...], g, m_ref[...], v_ref[...])

    tile = pl.BlockSpec((tr, c), lambda i: (i, 0))
    part_tile = pl.BlockSpec((n_parts, pr if tr == r else tr, pc), lambda i: (0, i, 0))
    out = _sds((r, c), F32)
    return _pallas_call(
        body, name=name, grid=(r // tr,), in_specs=[tile, tile, tile, part_tile], out_specs=(tile,) * 4,
        out_shape=(out,) * 4, compiler_params=_cparams(dimension_semantics=("arbitrary",)),
    )(w, m, v, parts)


SMALL = ("ssm_a_re", "ssm_a_im", "ssm_log_dt", "ssm_b_re", "ssm_b_im", "ssm_c_re", "ssm_c_im", "ssm_d",
         "ln1_g", "ln1_b", "ln2_g", "ln2_b")


def _pack_rows(arrays):
    rows = []
    for a in arrays:
        flat = a.reshape(-1)
        rows.append(jnp.pad(flat, (0, -flat.shape[0] % 128)).reshape(-1, 128))
    packed = jnp.concatenate(rows, axis=0)
    return jnp.pad(packed, ((0, -packed.shape[0] % 8), (0, 0)))


def _unpack_rows(packed, shapes):
    out, row = [], 0
    for shape in shapes:
        size = math.prod(shape)
        n_rows = -(-size // 128)
        out.append(packed[row:row + n_rows].reshape(-1)[:size].reshape(shape))
        row += n_rows
    return out


def _sum_devices(parts):
    def body(p_ref, o_ref):
        total = p_ref[0]
        for dev in range(1, N_DEV):
            total = total + p_ref[dev]
        o_ref[...] = total

    return _pallas_call(body, name="sum_devices", out_shape=_sds(parts.shape[1:], F32))(parts)


def _adamw_replicated(ws, ms, vs, gs):
    n = len(ws)

    def body(*refs):
        w_refs, m_refs, v_refs, g_refs, d_out, m_out, v_out = (refs[i * n:(i + 1) * n] for i in range(7))
        for i in range(n):
            d_out[i][...], m_out[i][...], v_out[i][...] = _adamw_math(w_refs[i][...], g_refs[i][...], m_refs[i][...], v_refs[i][...])

    out = _pallas_call(body, name="adamw_replicated", out_shape=[_sds(w.shape, F32) for w in ws] * 3,
                       compiler_params=_cparams())(*ws, *ms, *vs, *gs)
    return out[:n], out[n:2 * n], out[2 * n:]


def kernel(x, w_in, b_gate, w_attn_br, w_ssm_br, w_out, ssm_a_re, ssm_a_im, ssm_log_dt, ssm_b_re, ssm_b_im, ssm_c_re, ssm_c_im, ssm_d, w_glu, ln1_g, ln1_b, w_ff_gate, w_ff_up, w_ff_down, ln2_g, ln2_b, loss_target, m_w_in, m_b_gate, m_w_attn_br, m_w_ssm_br, m_w_out, m_ssm_a_re, m_ssm_a_im, m_ssm_log_dt, m_ssm_b_re, m_ssm_b_im, m_ssm_c_re, m_ssm_c_im, m_ssm_d, m_w_glu, m_ln1_g, m_ln1_b, m_w_ff_gate, m_w_ff_up, m_w_ff_down, m_ln2_g, m_ln2_b, v_w_in, v_b_gate, v_w_attn_br, v_w_ssm_br, v_w_out, v_ssm_a_re, v_ssm_a_im, v_ssm_log_dt, v_ssm_b_re, v_ssm_b_im, v_ssm_c_re, v_ssm_c_im, v_ssm_d, v_w_glu, v_ln1_g, v_ln1_b, v_w_ff_gate, v_w_ff_up, v_w_ff_down, v_ln2_g, v_ln2_b):
    given = dict(locals())
    x2, target = x[0], loss_target[0]
    core = lax.axis_index("c").astype(jnp.int32).reshape(1)

    sharded = ("w_in", "w_attn_br", "w_ssm_br", "w_glu", "w_ff_gate", "w_ff_up", "b_gate", "w_out", "w_ff_down")
    send_shape = dict(w_in=(D_MODEL, 896), w_attn_br=(ATTN_WIDTH, 128), w_ssm_br=(SSM_WIDTH, 128), w_glu=(SSM_WIDTH, 128),
                      w_out=(128, D_MODEL), w_ff_gate=(D_MODEL, FF_PAD), w_ff_up=(D_MODEL, FF_PAD), w_ff_down=(FF_PAD, D_MODEL))
    local = {k: given[k][0] for k in sharded}
    sends = {k: local[k] if k == "b_gate" else _send_buffer(local[k], *send_shape[k], name="send_" + k) for k in sharded}
    mixer_weights = ("w_attn_br", "w_ssm_br", "w_glu", "b_gate", "w_out")
    ff_weights = ("w_ff_gate", "w_ff_up", "w_ff_down")
    wt = {}
    wt["w_in"], = _all_gather([sends["w_in"]], "gather_w_in")

    a_re, a_im, log_dt = ssm_a_re[0], ssm_a_im[0], ssm_log_dt[0].reshape(SSM_GROUPS, 1)
    b_re_t, b_im_t = ssm_b_re[0].transpose(0, 2, 1), ssm_b_im[0].transpose(0, 2, 1)
    abar_re, abar_im, e_re, e_im, bbar_re_t, bbar_im_t = _ssm_prep(a_re, a_im, log_dt, b_re_t, b_im_t)
    bmat, cmat, a_chunks = _ssm_tables(abar_re, abar_im, bbar_re_t, bbar_im_t, ssm_c_re[0], ssm_c_im[0])
    cos_t, sin_t = _rope_tables()

    proj, *partly = _proj(x2, wt["w_in"], _gather_first_level([sends[k] for k in mixer_weights]))
    attn, lse, *landed = _attn_fwd(proj, cos_t, sin_t,
                                   _gather_second_level(partly) + _gather_first_level([sends[k] for k in ff_weights]))
    wt.update(zip(mixer_weights, landed[:len(mixer_weights)]))
    ys, states, *landed = _ssm_fwd(proj, bmat, cmat, a_chunks, ssm_d, _gather_second_level(landed[len(mixer_weights):]))
    wt.update(zip(ff_weights, landed))
    wt["w_out"] = wt["w_out"].reshape(D_MODEL, D_MODEL)
    wt["w_ff_down"] = wt["w_ff_down"].reshape(D_FF_PAD, D_MODEL)
    b_gate_full = wt["b_gate"]
    h, xhat1, rstd1, glu, y_attn, y_ssm = _mixer_out(attn, ys, proj, x2, wt["w_attn_br"], wt["w_ssm_br"], wt["w_glu"],
                                                      wt["w_out"], b_gate_full, ln1_g, ln1_b)
    ff_a, ff_b, ff_f = _ff_up(h, wt["w_ff_gate"], wt["w_ff_up"])
    dr2, d_ln2_g, d_ln2_b, loss_lanes = _ff_down_loss(ff_f, wt["w_ff_down"], h, target, ln2_g, ln2_b)

    def pair_sums(names, contrib, from_sibling):
        return [_pair_sum(contrib[k], r, core, "pair_sum_" + k) for k, r in zip(names, from_sibling)]

    d_a, d_b = _ff_down_bwd(dr2, wt["w_ff_down"], ff_a, ff_b)
    contrib = dict(w_ff_gate=_weight_grad(h, d_a, "wgrad_w_ff_gate", FF_PAD),
                   w_ff_up=_weight_grad(h, d_b, "wgrad_w_ff_up", FF_PAD),
                   w_ff_down=_weight_grad(ff_f, dr2, "wgrad_w_ff_down"))
    dr1, d_ln1_g, d_ln1_b, *from_sibling = _ff_up_bwd(
        d_a, d_b, wt["w_ff_gate"], wt["w_ff_up"], dr2, xhat1, rstd1, ln1_g, _sibling_swap_ride([contrib[k] for k in ff_weights]))
    ff_sums = pair_sums(ff_weights, contrib, from_sibling)

    d_ya, d_yssm, d_proj, d_attn, d_glu, d_ys, mixed, y_s, gy, d_bg = _mixer_bwd(
        dr1, proj, y_attn, y_ssm, glu, ys, wt["w_attn_br"], wt["w_ssm_br"], wt["w_glu"], wt["w_out"], b_gate_full)
    contrib.update(w_attn_br=_weight_grad(attn, d_ya, "wgrad_w_attn_br", 128),
                   w_ssm_br=_weight_grad(y_s, d_yssm, "wgrad_w_ssm_br", 128),
                   w_glu=_weight_grad(gy, d_glu, "wgrad_w_glu", 128),
                   w_out=_weight_grad(mixed, dr1, "wgrad_w_out"),
                   b_gate=d_bg.reshape(2, 4, 2, 128).transpose(2, 1, 0, 3))
    d_proj, *landed = _attn_bwd(proj, cos_t, sin_t, attn, lse, d_attn, d_proj,
                                _chip_swap_ride(ff_sums) + _sibling_swap_ride([contrib[k] for k in mixer_weights]))
    parts = dict(zip(ff_weights, landed[:len(ff_weights)]))
    mixer_sums = pair_sums(mixer_weights, contrib, landed[len(ff_weights):])
    d_proj, d_bmat, d_cmat, d_abar, d_skip, *landed = _ssm_bwd(d_ys, proj, states, bmat, cmat, a_chunks, ssm_d, d_proj,
                                                               _chip_swap_ride(mixer_sums))
    parts.update(zip(mixer_weights, landed))

    contrib["w_in"] = _weight_grad(x2, d_proj, "wgrad_w_in", 896)
    w_in_sum = pair_sums(["w_in"], contrib, _swap_with_sibling([contrib["w_in"]], "swap_w_in_with_sibling"))
    grad_x, parts["w_in"] = _grad_x(d_proj, wt["w_in"], dr1, _chip_swap_ride(w_in_sum))

    grads, deltas, new_m, new_v = {}, {}, {}, {}
    for k in sharded:
        w2 = local[k]
        out = _adamw(w2, given["m_" + k][0], given["v_" + k][0], parts[k], "adamw_" + k)
        grads[k], deltas[k], new_m[k], new_v[k] = (o.reshape((1,) + w2.shape) for o in out)

    gbb_re_t, gbb_im_t = _block_diag_parts(d_bmat, True)
    gc_re, gc_im = _block_diag_parts(d_cmat, False)
    ga_re = d_abar[:, 0, :CHUNK_STATES].reshape(SSM_GROUPS, SSM_STATE)
    ga_im = d_abar[:, 0, CHUNK_STATES:].reshape(SSM_GROUPS, SSM_STATE)
    g_a_re, g_a_im, g_log_dt, g_b_re_t, g_b_im_t = _ssm_param_bwd(
        a_re, a_im, log_dt, b_re_t, b_im_t, abar_re, abar_im, e_re, e_im, ga_re, ga_im, gbb_re_t, gbb_im_t)
    mine = [g_a_re, g_a_im, g_log_dt, g_b_re_t.transpose(0, 2, 1), g_b_im_t.transpose(0, 2, 1), gc_re, -gc_im,
            d_skip, d_ln1_g, d_ln1_b, d_ln2_g, d_ln2_b]
    every, = _all_gather([_pack_rows(mine + [loss_lanes])], "gather_small_grads")
    *small_grads, loss_sum = _unpack_rows(_sum_devices(every), [given[k].shape for k in SMALL] + [(1, 128)])
    small = _adamw_replicated([given[k] for k in SMALL], [given["m_" + k] for k in SMALL],
                              [given["v_" + k] for k in SMALL], small_grads)
    for res, values in zip((grads, deltas, new_m, new_v), (small_grads,) + small):
        res.update(zip(SMALL, values))
    loss = loss_sum[0, 0]

    order = ("w_in", "b_gate", "w_attn_br", "w_ssm_br", "w_out", "ssm_a_re", "ssm_a_im", "ssm_log_dt", "ssm_b_re", "ssm_b_im",
             "ssm_c_re", "ssm_c_im", "ssm_d", "w_glu", "ln1_g", "ln1_b", "w_ff_gate", "w_ff_up", "w_ff_down", "ln2_g", "ln2_b")
    return (loss, grad_x[None], *[grads[k] for k in order], *[deltas[k] for k in order], *[new_m[k] for k in order],
            *[new_v[k] for k in order])
```

```python
import functools
import math

import jax
import jax.numpy as jnp
from jax import lax
from jax.experimental import pallas as pl
from jax.experimental.pallas import tpu as pltpu

F32 = jnp.float32
BF16 = jnp.bfloat16

N_DEV = 8
SEQ = 2048
D_MODEL = 1024
HEAD_DIM = 64
ATTN_WIDTH = 512
QKV_WIDTH = 1536
SSM_WIDTH = 512
SSM_GROUPS = 32
SSM_GROUP = 16
SSM_STATE = 64
IN_WIDTH = 7168
D_FF = 2816
FF_SHARD = D_FF // N_DEV
FF_PAD = 384
D_FF_PAD = FF_PAD * N_DEV
DN_ALPHA = 2.0 ** 0.25
LN_EPS = 1e-5
NEG_INF = -1e30
ROPE_THETA = 10000.0
BLOCK = 128
GROUPS = ((1, 16), (4, 4), (16, 1))

ADAM_LR = 0.001
ADAM_B1 = 0.9
ADAM_B2 = 0.999
ADAM_EPS = 1e-08
ADAM_WD = 0.01
ADAM_STEP = 10

VMEM_LIMIT = 56 * 1024 * 1024


_pallas_call = pl.pallas_call


def _cparams(**kw):
    return pltpu.CompilerParams(vmem_limit_bytes=VMEM_LIMIT, **kw)


def _dot(a, b):
    return jnp.dot(a, b, preferred_element_type=F32)


def _dot_nt(a, b):
    return lax.dot_general(a, b, (((1,), (1,)), ((), ())), preferred_element_type=F32)


def _side_by_side(w_ref, row=None):
    rows = slice(None) if row is None else pl.ds(row, 1)
    return jnp.concatenate([w_ref[i, rows, :] for i in range(w_ref.shape[0])], axis=1)


def _dot_tn(a, b):
    return lax.dot_general(a, b, (((0,), (0,)), ((), ())), preferred_element_type=F32)


def _rope_tables():
    half = HEAD_DIM // 2
    inv_freq = ROPE_THETA ** (-jnp.arange(half, dtype=F32) / half)
    ang = jnp.arange(SEQ, dtype=F32)[:, None] * inv_freq[None, :]
    cos, sin = jnp.cos(ang), jnp.sin(ang)
    return jnp.tile(cos, (1, 4)), jnp.tile(jnp.concatenate([-sin, sin], axis=1), (1, 2))


def _swap_halves(x):
    lane = lax.broadcasted_iota(jnp.int32, x.shape, 1)
    return jnp.where((lane & 63) < 32, pltpu.roll(x, 96, axis=1), pltpu.roll(x, 32, axis=1))


def _group_rows(d, nb, r, i):
    src = pl.ds(i * BLOCK, BLOCK) if d == 1 else pl.ds(r + i * BLOCK * d, BLOCK, stride=d)
    return src, pl.ds((r * nb + i) * BLOCK, BLOCK)


def _attn_masks():
    a_idx = lax.broadcasted_iota(jnp.int32, (2 * BLOCK, 2 * BLOCK), 0) & (BLOCK - 1)
    c_idx = lax.broadcasted_iota(jnp.int32, (2 * BLOCK, 2 * BLOCK), 1)
    cur_ok = jnp.logical_and(c_idx >= BLOCK, c_idx - BLOCK <= a_idx)
    prev_ok = jnp.logical_and(c_idx < BLOCK, c_idx >= a_idx)
    lane = lax.broadcasted_iota(jnp.int32, (BLOCK, 128), 1)
    return cur_ok, prev_ok, lane < HEAD_DIM


def _stack_heads(t, head0):
    zero = jnp.zeros_like(t)
    return jnp.concatenate([jnp.where(head0, t, zero), jnp.where(head0, zero, t)], axis=0)


def _unstack_heads(t2, head0):
    return jnp.where(head0, t2[:BLOCK], t2[BLOCK:])


def _attn_fwd(proj, cos_t, sin_t, ride=None):
    def body(q0, q1, q2, k0, k1, k2, v0, v1, v2, cos_ref, sin_ref, attn_ref, lse_ref,
             qs, ks, vs, os_, ms, ls, acc, mnat, lnat):
        cur_ok, prev_ok, head0 = _attn_masks()
        ks[:BLOCK, :] = jnp.zeros((BLOCK, 128), BF16)
        vs[:BLOCK, :] = jnp.zeros((BLOCK, 128), BF16)
        for g, (d, nb) in enumerate(GROUPS):
            q_ref, k_ref, v_ref = (q0, q1, q2)[g], (k0, k1, k2)[g], (v0, v1, v2)[g]
            for r in range(d):
                for i in range(nb):
                    src, dst = _group_rows(d, nb, r, i)
                    below = pl.ds(dst.start + BLOCK, BLOCK)
                    c, s = cos_ref[src, :], sin_ref[src, :]
                    q = q_ref[src, :]
                    k = k_ref[src, :]
                    qs[dst, :] = ((q * c + _swap_halves(q) * s) * 0.125).astype(BF16)
                    ks[below, :] = (k * c + _swap_halves(k) * s).astype(BF16)
                    vs[below, :] = v_ref[src, :].astype(BF16)

            def block(b, carry, nb=nb):
                has_prev = (b & (nb - 1)) > 0
                cur = pl.ds(pl.multiple_of(b * BLOCK, BLOCK), BLOCK)
                window = pl.ds(pl.multiple_of(b * BLOCK, BLOCK), 2 * BLOCK)
                valid = jnp.logical_or(cur_ok, jnp.logical_and(prev_ok, has_prev))
                s = jnp.where(valid, _dot_nt(_stack_heads(qs[cur, :], head0), ks[window, :]), NEG_INF)
                m = jnp.max(s, axis=1, keepdims=True)
                p = jnp.exp(s - m)
                os_[cur, :] = _unstack_heads(_dot(p.astype(BF16), vs[window, :]), head0)
                ms[cur, :] = _unstack_heads(m, head0)
                ls[cur, :] = _unstack_heads(jnp.sum(p, axis=1, keepdims=True), head0)
                return carry

            lax.fori_loop(0, SEQ // BLOCK, block, 0, unroll=2)

            for r in range(d):
                for i in range(nb):
                    src, dst = _group_rows(d, nb, r, i)
                    if g == 0:
                        acc[src, :], mnat[src, :], lnat[src, :] = os_[dst, :], ms[dst, :], ls[dst, :]
                    else:
                        m_old, m_g = mnat[src, :], ms[dst, :]
                        m_new = jnp.maximum(m_old, m_g)
                        a_old, a_g = jnp.exp(m_old - m_new), jnp.exp(m_g - m_new)
                        acc[src, :] = a_old * acc[src, :] + a_g * os_[dst, :]
                        lnat[src, :] = a_old * lnat[src, :] + a_g * ls[dst, :]
                        mnat[src, :] = m_new
        for i in range(SEQ // BLOCK):
            rows = pl.ds(i * BLOCK, BLOCK)
            l = lnat[rows, :]
            attn_ref[rows, :] = acc[rows, :] / l
            lse_ref[rows, :] = mnat[rows, :] + jnp.log(l)

    def col(base):
        return pl.BlockSpec((SEQ, 128), lambda hp, base=base: (0, base + hp))

    in_specs = [col(g * 4) for g in range(3)] + [col(12 + g * 4) for g in range(3)] + [col(24 + g * 4) for g in range(3)]
    table = pl.BlockSpec((SEQ, 128), lambda hp: (0, 0))
    out = pl.BlockSpec((SEQ, 128), lambda hp: (0, hp))
    return _call(
        body, "attn_fwd", (4,), in_specs + [table, table], [out, out],
        [_sds((SEQ, ATTN_WIDTH), F32), _sds((SEQ, ATTN_WIDTH), F32)],
        [pltpu.VMEM((SEQ, 128), BF16)] + [pltpu.VMEM((SEQ + BLOCK, 128), BF16)] * 2 + [pltpu.VMEM((SEQ, 128), F32)] * 6,
        [proj] * 9 + [cos_t, sin_t], ride)


def _attn_bwd_group_body(g):
    d, nb = GROUPS[g]

    def body(q_ref, k_ref, v_ref, cos_ref, sin_ref, attn_ref, lse_ref, dattn_ref, dproj_in, dproj_ref,
             qs, ks, vs, dos, lss, dss, dqs, dks, dvs, stage, outs, sems):
        del dproj_in
        cur_ok, prev_ok, head0 = _attn_masks()
        ks[:BLOCK, :] = jnp.zeros((BLOCK, 128), BF16)
        vs[:BLOCK, :] = jnp.zeros((BLOCK, 128), BF16)
        dks[:BLOCK, :] = jnp.zeros((BLOCK, 128), F32)
        dvs[:BLOCK, :] = jnp.zeros((BLOCK, 128), F32)
        for r in range(d):
            for i in range(nb):
                src, dst = _group_rows(d, nb, r, i)
                below = pl.ds(dst.start + BLOCK, BLOCK)
                c, s = cos_ref[src, :], sin_ref[src, :]
                q = q_ref[src, :]
                k = k_ref[src, :]
                qs[dst, :] = ((q * c + _swap_halves(q) * s) * 0.125).astype(BF16)
                ks[below, :] = (k * c + _swap_halves(k) * s).astype(BF16)
                vs[below, :] = v_ref[src, :].astype(BF16)
                do = dattn_ref[src, :]
                prod = do * attn_ref[src, :]
                d0 = jnp.sum(jnp.where(head0, prod, 0.0), axis=1, keepdims=True)
                d1 = jnp.sum(jnp.where(head0, 0.0, prod), axis=1, keepdims=True)
                dos[dst, :] = do.astype(BF16)
                dss[dst, :] = jnp.where(head0, d0, d1)
                lss[dst, :] = lse_ref[src, :]
                dks[below, :] = jnp.zeros((BLOCK, 128), F32)
                dvs[below, :] = jnp.zeros((BLOCK, 128), F32)

        def per_head_column(t):
            return jnp.concatenate([jnp.max(jnp.where(head0, t, NEG_INF), axis=1, keepdims=True),
                                    jnp.max(jnp.where(head0, NEG_INF, t), axis=1, keepdims=True)], axis=0)

        def block(b, carry):
            has_prev = (b & (nb - 1)) > 0
            cur = pl.ds(pl.multiple_of(b * BLOCK, BLOCK), BLOCK)
            window = pl.ds(pl.multiple_of(b * BLOCK, BLOCK), 2 * BLOCK)
            valid = jnp.logical_or(cur_ok, jnp.logical_and(prev_ok, has_prev))
            q2, do2 = _stack_heads(qs[cur, :], head0), _stack_heads(dos[cur, :], head0)
            kw, vw = ks[window, :], vs[window, :]
            s = jnp.where(valid, _dot_nt(q2, kw), NEG_INF)
            p = jnp.exp(s - per_head_column(lss[cur, :]))
            ds = (p * (_dot_nt(do2, vw) - per_head_column(dss[cur, :]))).astype(BF16)
            dvs[window, :] += _dot_tn(p.astype(BF16), do2)
            dks[window, :] += _dot_tn(ds, q2)
            dqs[cur, :] = _unstack_heads(_dot(ds, kw), head0)
            return carry

        lax.fori_loop(0, SEQ // BLOCK, block, 0, unroll=2)

        hp = pl.program_id(0)
        copies = []
        for kind in range(3):
            for r in range(d):
                for i in range(nb):
                    src, dst = _group_rows(d, nb, r, i)
                    below = pl.ds(dst.start + BLOCK, BLOCK)
                    if kind == 2:
                        stage[src, :] = dvs[below, :]
                    else:
                        c, s = cos_ref[src, :], sin_ref[src, :]
                        t = dqs[dst, :] * 0.125 if kind == 0 else dks[below, :]
                        stage[src, :] = t * c - _swap_halves(t) * s
            for i in range(SEQ // MM_ROWS):
                rows = pl.ds(i * MM_ROWS, MM_ROWS)
                outs[kind, rows, :] = stage[rows, :].astype(BF16)
            column = pl.multiple_of((kind * 12 + g * 4 + hp) * 128, 128)
            copies.append(pltpu.make_async_copy(outs.at[kind], dproj_ref.at[:, pl.ds(column, 128)], sems.at[kind]))
            copies[-1].start()
        for cp in copies:
            cp.wait()

    return body


def _attn_bwd(proj, cos_t, sin_t, attn, lse, dattn, dproj, ride=None):
    groups = [_attn_bwd_group_body(g) for g in range(3)]

    def body(q0, q1, q2, k0, k1, k2, v0, v1, v2, *rest):
        for g in range(3):
            groups[g]((q0, q1, q2)[g], (k0, k1, k2)[g], (v0, v1, v2)[g], *rest)

    def col(base):
        return pl.BlockSpec((SEQ, 128), lambda hp, base=base: (0, base + hp))

    table = pl.BlockSpec((SEQ, 128), lambda hp: (0, 0))
    return _call(
        body, "attn_bwd", (4,),
        [col(g * 4) for g in range(3)] + [col(12 + g * 4) for g in range(3)] + [col(24 + g * 4) for g in range(3)]
        + [table, table, col(0), col(0), col(0), ANY],
        [ANY], [_sds((SEQ, IN_WIDTH), BF16)],
        [pltpu.VMEM((SEQ, 128), BF16)] + [pltpu.VMEM((SEQ + BLOCK, 128), BF16)] * 2 + [pltpu.VMEM((SEQ, 128), BF16)]
        + [pltpu.VMEM((SEQ, 128), F32)] * 3 + [pltpu.VMEM((SEQ + BLOCK, 128), F32)] * 2 + [pltpu.VMEM((SEQ, 128), F32)]
        + [pltpu.VMEM((3, SEQ, 128), BF16), pltpu.SemaphoreType.DMA((3,))],
        [proj] * 9 + [cos_t, sin_t, attn, lse, dattn, dproj], ride, aliases={14: 0})


SSM_CHUNKS = 4
CHUNK_STATES = 512
SCAN_ROWS = 8
U_COL = (3 * QKV_WIDTH) // 128


def _cmul(xr, xi, yr, yi):
    return xr * yr - xi * yi, xr * yi + xi * yr


def _ssm_prep(a_re, a_im, log_dt, b_re_t, b_im_t):
    def body(ar_ref, ai_ref, ldt_ref, br_ref, bi_ref, abr_ref, abi_ref, er_ref, ei_ref, bbr_ref, bbi_ref):
        ar, ai = ar_ref[...], ai_ref[...]
        dt = jnp.exp(ldt_ref[...])
        mag = jnp.exp(ar * dt)
        abr, abi = mag * jnp.cos(ai * dt), mag * jnp.sin(ai * dt)
        den = ar * ar + ai * ai
        nr, ni = abr - 1.0, abi
        er, ei = (nr * ar + ni * ai) / den, (ni * ar - nr * ai) / den
        abr_ref[...], abi_ref[...], er_ref[...], ei_ref[...] = abr, abi, er, ei
        er3, ei3 = er[:, None, :], ei[:, None, :]
        br, bi = br_ref[...], bi_ref[...]
        bbr_ref[...] = er3 * br - ei3 * bi
        bbi_ref[...] = er3 * bi + ei3 * br

    gp = jax.ShapeDtypeStruct(a_re.shape, F32)
    gb = jax.ShapeDtypeStruct(b_re_t.shape, F32)
    return _pallas_call(body, name="ssm_prep", out_shape=(gp, gp, gp, gp, gb, gb))(a_re, a_im, log_dt, b_re_t, b_im_t)


def _ssm_param_bwd(a_re, a_im, log_dt, b_re_t, b_im_t, abar_re, abar_im, e_re, e_im, ga_re, ga_im, gbb_re_t, gbb_im_t):
    def body(ar_ref, ai_ref, ldt_ref, br_ref, bi_ref, abr_ref, abi_ref, er_ref, ei_ref, gar_ref, gai_ref, gbr_ref, gbi_ref,
             o_ar, o_ai, o_ldt, o_br, o_bi):
        ar, ai = ar_ref[...], ai_ref[...]
        dt = jnp.exp(ldt_ref[...])
        er, ei = er_ref[...], ei_ref[...]
        br, bi, gbr, gbi = br_ref[...], bi_ref[...], gbr_ref[...], gbi_ref[...]
        er3, ei3 = er[:, None, :], ei[:, None, :]
        o_br[...] = er3 * gbr + ei3 * gbi
        o_bi[...] = er3 * gbi - ei3 * gbr
        ge_r = jnp.sum(br * gbr + bi * gbi, axis=1)
        ge_i = jnp.sum(br * gbi - bi * gbr, axis=1)
        den = ar * ar + ai * ai
        ilr, ili = ar / den, -ai / den
        t_r, t_i = _cmul(ilr, -ili, ge_r, ge_i)
        gab_r, gab_i = gar_ref[...] + t_r, gai_ref[...] + t_i
        gz_r, gz_i = _cmul(abr_ref[...], -abi_ref[...], gab_r, gab_i)
        el_r, el_i = _cmul(er, ei, ilr, ili)
        u_r, u_i = _cmul(el_r, -el_i, ge_r, ge_i)
        o_ar[...] = dt * gz_r - u_r
        o_ai[...] = dt * gz_i - u_i
        o_ldt[...] = jnp.sum(gz_r * ar + gz_i * ai, axis=1, keepdims=True) * dt

    gp = jax.ShapeDtypeStruct(a_re.shape, F32)
    gb = jax.ShapeDtypeStruct(b_re_t.shape, F32)
    return _pallas_call(body, name="ssm_param_bwd", out_shape=(gp, gp, jax.ShapeDtypeStruct(log_dt.shape, F32), gb, gb))(
        a_re, a_im, log_dt, b_re_t, b_im_t, abar_re, abar_im, e_re, e_im, ga_re, ga_im, gbb_re_t, gbb_im_t)


def _block_diag(blocks_re, blocks_im, sign_im, rows_are_channels):
    both = jnp.stack([blocks_re, sign_im * blocks_im]).reshape(2, SSM_CHUNKS, 8, SSM_GROUP, SSM_STATE)
    eye = jnp.eye(8, dtype=F32)
    if rows_are_channels:
        return jnp.einsum("rcghp,gk->cghrkp", both, eye).reshape(SSM_CHUNKS, 128, 2 * CHUNK_STATES)
    return jnp.einsum("rcghp,gk->crkpgh", both, eye).reshape(SSM_CHUNKS, 2 * CHUNK_STATES, 128)


def _block_diag_parts(mat, rows_are_channels):
    if rows_are_channels:
        six = mat.reshape(SSM_CHUNKS, 8, SSM_GROUP, 2, 8, SSM_STATE)
        parts = jnp.einsum("cghrgp->rcghp", six)
    else:
        six = mat.reshape(SSM_CHUNKS, 2, 8, SSM_STATE, 8, SSM_GROUP)
        parts = jnp.einsum("crgpgh->rcghp", six)
    parts = parts.reshape(2, SSM_GROUPS, SSM_GROUP, SSM_STATE)
    return parts[0], parts[1]


def _scan_consts(a_ref, conj, reverse):
    ar = jnp.broadcast_to(a_ref[:, :CHUNK_STATES], (SCAN_ROWS, CHUNK_STATES))
    ai = jnp.broadcast_to(a_ref[:, CHUNK_STATES:], (SCAN_ROWS, CHUNK_STATES))
    if conj:
        ai = -ai
    row = lax.broadcasted_iota(jnp.int32, (SCAN_ROWS, CHUNK_STATES), 0)
    if reverse:
        row = SCAN_ROWS - 1 - row
    zero = jnp.zeros_like(ar)
    steps = []
    pr, pi = ar, ai
    for shift in (1, 2, 4):
        keep = row >= shift
        steps.append((SCAN_ROWS - shift if reverse else shift, jnp.where(keep, pr, zero), jnp.where(keep, pi, zero)))
        pr, pi = _cmul(pr, pi, pr, pi)
    first = row == 0
    return steps, (jnp.where(first, ar, zero), jnp.where(first, ai, zero)), first


def _scan_tile(xr, xi, prev_r, prev_i, steps, carry_in, reverse):
    edge = SCAN_ROWS - 1 if reverse else 1
    cr, ci = pltpu.roll(prev_r, edge, axis=0), pltpu.roll(prev_i, edge, axis=0)
    xr, xi = xr + carry_in[0] * cr - carry_in[1] * ci, xi + carry_in[0] * ci + carry_in[1] * cr
    for shift, mr, mi in steps:
        sr, si = pltpu.roll(xr, shift, axis=0), pltpu.roll(xi, shift, axis=0)
        xr, xi = xr + mr * sr - mi * si, xi + mr * si + mi * sr
    return xr, xi


MM_ROWS = 256


def _ssm_fwd(proj, bmat, cmat, a_chunks, d_skip, ride=None):
    def body(u_ref, b_ref, c_ref, a_ref, d_ref, y_ref, h_ref):
        for i in range(SEQ // MM_ROWS):
            rows = pl.ds(i * MM_ROWS, MM_ROWS)
            h_ref[rows, :] = _dot(u_ref[rows, :].astype(BF16), b_ref[...])
        steps, carry_in, _ = _scan_consts(a_ref, conj=False, reverse=False)

        def tile(k, carry):
            rows = pl.ds(pl.multiple_of(k * SCAN_ROWS, SCAN_ROWS), SCAN_ROWS)
            xr, xi = _scan_tile(h_ref[rows, :CHUNK_STATES], h_ref[rows, CHUNK_STATES:], carry[0], carry[1], steps, carry_in, False)
            h_ref[rows, :CHUNK_STATES] = xr
            h_ref[rows, CHUNK_STATES:] = xi
            return xr, xi

        zero = jnp.zeros((SCAN_ROWS, CHUNK_STATES), F32)
        lax.fori_loop(0, SEQ // SCAN_ROWS, tile, (zero, zero))
        for i in range(SEQ // MM_ROWS):
            rows = pl.ds(i * MM_ROWS, MM_ROWS)
            y_ref[rows, :] = _dot(h_ref[rows, :].astype(BF16), c_ref[...]) + d_ref[...] * u_ref[rows, :]

    return _call(
        body, "ssm_fwd", (SSM_CHUNKS,),
        [pl.BlockSpec((SEQ, 128), lambda c: (0, U_COL + c)),
         pl.BlockSpec((None, 128, 2 * CHUNK_STATES), lambda c: (c, 0, 0)),
         pl.BlockSpec((None, 2 * CHUNK_STATES, 128), lambda c: (c, 0, 0)),
         pl.BlockSpec((None, 1, 2 * CHUNK_STATES), lambda c: (c, 0, 0)),
         pl.BlockSpec((1, 128), lambda c: (0, c))],
        [pl.BlockSpec((SEQ, 128), lambda c: (0, c)), pl.BlockSpec((SEQ, 2 * CHUNK_STATES), lambda c: (0, c))],
        [_sds((SEQ, SSM_WIDTH), F32), _sds((SEQ, SSM_CHUNKS * 2 * CHUNK_STATES), F32)], [],
        [proj, bmat, cmat, a_chunks, d_skip], ride)


def _ssm_bwd(dys, proj, h, bmat, cmat, a_chunks, d_skip, dproj, ride=None):
    def body(dy_ref, u_ref, h_ref, b_ref, c_ref, a_ref, d_ref, dproj_in, du_ref, db_ref, dc_ref, da_ref, dd_ref, g_ref):
        del dproj_in
        dsum = jnp.zeros((1, 128), F32)
        dcm = jnp.zeros((2 * CHUNK_STATES, 128), F32)
        for i in range(SEQ // MM_ROWS):
            rows = pl.ds(i * MM_ROWS, MM_ROWS)
            dy = dy_ref[rows, :]
            g_ref[rows, :] = _dot_nt(dy.astype(BF16), c_ref[...])
            dsum += jnp.sum(dy * u_ref[rows, :], axis=0, keepdims=True)
            dcm += _dot_tn(h_ref[rows, :].astype(BF16), dy.astype(BF16))
        dd_ref[...] = dsum
        dc_ref[...] = dcm
        steps, carry_in, _ = _scan_consts(a_ref, conj=True, reverse=True)
        first_row = lax.broadcasted_iota(jnp.int32, (SCAN_ROWS, CHUNK_STATES), 0) == 0
        n_tiles = SEQ // SCAN_ROWS

        def tile(j, carry):
            k = n_tiles - 1 - j
            rows = pl.ds(pl.multiple_of(k * SCAN_ROWS, SCAN_ROWS), SCAN_ROWS)
            before = pl.ds(pl.multiple_of(jnp.maximum(k - 1, 0) * SCAN_ROWS, SCAN_ROWS), SCAN_ROWS)
            gr, gi = _scan_tile(g_ref[rows, :CHUNK_STATES], g_ref[rows, CHUNK_STATES:], carry[0], carry[1], steps, carry_in, True)
            g_ref[rows, :CHUNK_STATES] = gr
            g_ref[rows, CHUNK_STATES:] = gi
            has_before = jnp.where(k > 0, 1.0, 0.0)
            hr = jnp.where(first_row, pltpu.roll(h_ref[before, :CHUNK_STATES], 1, axis=0) * has_before,
                           pltpu.roll(h_ref[rows, :CHUNK_STATES], 1, axis=0))
            hi = jnp.where(first_row, pltpu.roll(h_ref[before, CHUNK_STATES:], 1, axis=0) * has_before,
                           pltpu.roll(h_ref[rows, CHUNK_STATES:], 1, axis=0))
            return gr, gi, carry[2] + hr * gr + hi * gi, carry[3] + hr * gi - hi * gr

        zero = jnp.zeros((SCAN_ROWS, CHUNK_STATES), F32)
        _, _, sar, sai = lax.fori_loop(0, n_tiles, tile, (zero, zero, zero, zero))
        da_ref[:, :CHUNK_STATES] = jnp.sum(sar, axis=0, keepdims=True)
        da_ref[:, CHUNK_STATES:] = jnp.sum(sai, axis=0, keepdims=True)
        dbm = jnp.zeros((128, 2 * CHUNK_STATES), F32)
        for i in range(SEQ // MM_ROWS):
            rows = pl.ds(i * MM_ROWS, MM_ROWS)
            g = g_ref[rows, :].astype(BF16)
            du_ref[rows, :] = (_dot_nt(g, b_ref[...]) + d_ref[...] * dy_ref[rows, :]).astype(BF16)
            dbm += _dot_tn(u_ref[rows, :].astype(BF16), g)
        db_ref[...] = dbm

    chunk_col = pl.BlockSpec((SEQ, 128), lambda c: (0, c))
    return _call(
        body, "ssm_bwd", (SSM_CHUNKS,),
        [chunk_col,
         pl.BlockSpec((SEQ, 128), lambda c: (0, U_COL + c)),
         pl.BlockSpec((SEQ, 2 * CHUNK_STATES), lambda c: (0, c)),
         pl.BlockSpec((None, 128, 2 * CHUNK_STATES), lambda c: (c, 0, 0)),
         pl.BlockSpec((None, 2 * CHUNK_STATES, 128), lambda c: (c, 0, 0)),
         pl.BlockSpec((None, 1, 2 * CHUNK_STATES), lambda c: (c, 0, 0)),
         pl.BlockSpec((1, 128), lambda c: (0, c)), ANY],
        [pl.BlockSpec((SEQ, 128), lambda c: (0, U_COL + c)),
         pl.BlockSpec((None, 128, 2 * CHUNK_STATES), lambda c: (c, 0, 0)),
         pl.BlockSpec((None, 2 * CHUNK_STATES, 128), lambda c: (c, 0, 0)),
         pl.BlockSpec((None, 1, 2 * CHUNK_STATES), lambda c: (c, 0, 0)),
         pl.BlockSpec((1, 128), lambda c: (0, c))],
        [_sds((SEQ, IN_WIDTH), BF16), _sds((SSM_CHUNKS, 128, 2 * CHUNK_STATES), F32),
         _sds((SSM_CHUNKS, 2 * CHUNK_STATES, 128), F32), _sds((SSM_CHUNKS, 1, 2 * CHUNK_STATES), F32), _sds((1, SSM_WIDTH), F32)],
        [pltpu.VMEM((SEQ, 2 * CHUNK_STATES), F32)], [dys, proj, h, bmat, cmat, a_chunks, d_skip, dproj], ride, aliases={7: 0})


def _ssm_tables(abar_re, abar_im, bbar_re_t, bbar_im_t, c_re, c_im):
    bmat = _block_diag(bbar_re_t, bbar_im_t, 1.0, True).astype(BF16)
    cmat = _block_diag(c_re, c_im, -1.0, False).astype(BF16)
    a_chunks = jnp.concatenate([abar_re.reshape(SSM_CHUNKS, 1, CHUNK_STATES), abar_im.reshape(SSM_CHUNKS, 1, CHUNK_STATES)], axis=2)
    return bmat, cmat, a_chunks


GL_COL = (3 * QKV_WIDTH + SSM_WIDTH) // D_MODEL
GELU_C = math.sqrt(2.0 / math.pi)
GELU_A = 0.044715


def _sds(shape, dtype):
    return jax.ShapeDtypeStruct(shape, dtype)


def _gelu(x):
    t = jnp.tanh(GELU_C * (x + GELU_A * x * x * x))
    return 0.5 * x * (1.0 + t), t


def _gelu_grad(x, t):
    return 0.5 * (1.0 + t) + 0.5 * x * (1.0 - t * t) * GELU_C * (1.0 + 3.0 * GELU_A * x * x)


def _layer_norm(r, g, b):
    mu = jnp.mean(r, axis=-1, keepdims=True)
    xc = r - mu
    rstd = lax.rsqrt(jnp.mean(xc * xc, axis=-1, keepdims=True) + LN_EPS)
    xhat = xc * rstd
    return xhat * g + b, xhat, rstd


def _layer_norm_bwd(dy, xhat, rstd, g):
    dxhat = dy * g
    m1 = jnp.mean(dxhat, axis=-1, keepdims=True)
    m2 = jnp.mean(dxhat * xhat, axis=-1, keepdims=True)
    return rstd * (dxhat - m1 - xhat * m2)


def _proj(x, w_in, ride=None):
    tm, tn = 1024, 1792

    def body(x_ref, w_ref, o_ref):
        o_ref[...] = _dot(x_ref[...].astype(BF16), _side_by_side(w_ref))

    return _call(
        body, "proj", (SEQ // tm, IN_WIDTH // tn),
        [pl.BlockSpec((tm, D_MODEL), lambda i, j: (i, 0)), pl.BlockSpec((2, D_MODEL, tn // 2), lambda i, j: (j, 0, 0))],
        [pl.BlockSpec((tm, tn), lambda i, j: (i, j))], [_sds((SEQ, IN_WIDTH), F32)], [], [x, w_in], ride)


def _row_spec(tm, width, col=0):
    return pl.BlockSpec((tm, width), lambda i, col=col: (i, col))


def _full_spec(shape):
    return pl.BlockSpec(shape, lambda i: (0,) * len(shape))


def _mixer_out(attn, ys, proj, x, w_ab, w_sb, w_glu, w_out, b_gate, ln_g, ln_b):
    tm = 256

    def body(attn_ref, ys_ref, gl0_ref, gl1_ref, x_ref, wab_ref, wsb_ref, wglu_ref, wout_ref, bg_ref, g_ref, b_ref,
             h_ref, xhat_ref, rstd_ref, glu_ref, ya_ref, yssm_ref):
        gy, _ = _gelu(ys_ref[...])
        glu = _dot(gy.astype(BF16), _side_by_side(wglu_ref))
        glu_ref[...] = glu
        y_s = glu[:, :SSM_WIDTH] * jax.nn.sigmoid(glu[:, SSM_WIDTH:])
        y_ssm = _dot(y_s.astype(BF16), _side_by_side(wsb_ref))
        y_attn = _dot(attn_ref[...].astype(BF16), _side_by_side(wab_ref))
        ya_ref[...] = y_attn
        yssm_ref[...] = y_ssm
        g0 = jax.nn.sigmoid(gl0_ref[...] + _side_by_side(bg_ref, 0))
        g1 = jax.nn.sigmoid(gl1_ref[...] + _side_by_side(bg_ref, 1))
        mixed = g0 * y_attn + g1 * y_ssm
        r1 = DN_ALPHA * x_ref[...] + _dot(mixed.astype(BF16), wout_ref[...])
        h, xhat, rstd = _layer_norm(r1, g_ref[...], b_ref[...])
        h_ref[...] = h
        xhat_ref[...] = xhat
        rstd_ref[...] = jnp.broadcast_to(rstd, (tm, 128))

    wide = _sds((SEQ, D_MODEL), F32)
    return _pallas_call(
        body, name="mixer_out", grid=(SEQ // tm,),
        in_specs=[_row_spec(tm, ATTN_WIDTH), _row_spec(tm, SSM_WIDTH), _row_spec(tm, D_MODEL, GL_COL), _row_spec(tm, D_MODEL, GL_COL + 1),
                  _row_spec(tm, D_MODEL), _full_spec((N_DEV, ATTN_WIDTH, 128)), _full_spec((N_DEV, SSM_WIDTH, 128)),
                  _full_spec((N_DEV, SSM_WIDTH, 128)), _full_spec((D_MODEL, D_MODEL)), _full_spec((N_DEV, 2, 128)),
                  _full_spec((1, D_MODEL)), _full_spec((1, D_MODEL))],
        out_specs=(_row_spec(tm, D_MODEL), _row_spec(tm, D_MODEL), _row_spec(tm, 128), _row_spec(tm, D_MODEL),
                   _row_spec(tm, D_MODEL), _row_spec(tm, D_MODEL)),
        out_shape=(wide, wide, _sds((SEQ, 128), F32), wide, wide, wide),
        compiler_params=_cparams(dimension_semantics=("arbitrary",)),
    )(attn, ys, proj, proj, x, w_ab, w_sb, w_glu, w_out, b_gate, ln_g, ln_b)


def _ff_up(h, w_gate, w_up):
    tm, tn = 1024, 768

    def body(h_ref, wg_ref, wu_ref, a_ref, b_ref, f_ref):
        hb = h_ref[...].astype(BF16)
        a, b = _dot(hb, _side_by_side(wg_ref)), _dot(hb, _side_by_side(wu_ref))
        a_ref[...] = a.astype(BF16)
        b_ref[...] = b.astype(BF16)
        f_ref[...] = (a * jax.nn.sigmoid(a) * b).astype(BF16)

    tile = pl.BlockSpec((tm, tn), lambda i, j: (i, j))
    wtile = pl.BlockSpec((tn // FF_PAD, D_MODEL, FF_PAD), lambda i, j: (j, 0, 0))
    out = _sds((SEQ, D_FF_PAD), BF16)
    return _pallas_call(
        body, name="ff_up", grid=(SEQ // tm, D_FF_PAD // tn),
        in_specs=[pl.BlockSpec((tm, D_MODEL), lambda i, j: (i, 0)), wtile, wtile],
        out_specs=(tile, tile, tile), out_shape=(out, out, out),
        compiler_params=_cparams(dimension_semantics=("arbitrary", "arbitrary")),
    )(h, w_gate, w_up)


def _ff_down_loss(f, w_down, h, target, ln_g, ln_b):
    tm = 256

    def body(f_ref, w_ref, h_ref, t_ref, g_ref, b_ref, dr_ref, dg_ref, db_ref, loss_ref):
        @pl.when(pl.program_id(0) == 0)
        def _():
            dg_ref[...] = jnp.zeros_like(dg_ref)
            db_ref[...] = jnp.zeros_like(db_ref)
            loss_ref[...] = jnp.zeros_like(loss_ref)

        r2 = DN_ALPHA * h_ref[...] + _dot(f_ref[...], w_ref[...])
        g = g_ref[...]
        out, xhat, rstd = _layer_norm(r2, g, b_ref[...])
        err = out - t_ref[...]
        loss_ref[...] += 0.5 * jnp.sum(jnp.mean(err * err, axis=-1, keepdims=True), axis=0, keepdims=True)
        dout = err * (1.0 / D_MODEL)
        dg_ref[...] += jnp.sum(dout * xhat, axis=0, keepdims=True)
        db_ref[...] += jnp.sum(dout, axis=0, keepdims=True)
        dr_ref[...] = _layer_norm_bwd(dout, xhat, rstd, g)

    vec = _sds((1, D_MODEL), F32)
    return _pallas_call(
        body, name="ff_down_loss", grid=(SEQ // tm,),
        in_specs=[_row_spec(tm, D_FF_PAD), _full_spec((D_FF_PAD, D_MODEL)), _row_spec(tm, D_MODEL), _row_spec(tm, D_MODEL),
                  _full_spec((1, D_MODEL)), _full_spec((1, D_MODEL))],
        out_specs=(_row_spec(tm, D_MODEL), _full_spec((1, D_MODEL)), _full_spec((1, D_MODEL)), _full_spec((1, 128))),
        out_shape=(_sds((SEQ, D_MODEL), F32), vec, vec, _sds((1, 128), F32)),
        compiler_params=_cparams(dimension_semantics=("arbitrary",)),
    )(f, w_down, h, target, ln_g, ln_b)


def _ff_down_bwd(dr2, w_down, a, b):
    tm, tn = 1024, 768

    def body(dr_ref, w_ref, a_ref, b_ref, da_ref, db_ref):
        df = _dot_nt(dr_ref[...].astype(BF16), w_ref[...])
        av, bv = a_ref[...].astype(F32), b_ref[...].astype(F32)
        sg = jax.nn.sigmoid(av)
        da_ref[...] = (df * bv * sg * (1.0 + av * (1.0 - sg))).astype(BF16)
        db_ref[...] = (df * av * sg).astype(BF16)

    tile = pl.BlockSpec((tm, tn), lambda i, j: (i, j))
    out = _sds((SEQ, D_FF_PAD), BF16)
    return _pallas_call(
        body, name="ff_down_bwd", grid=(SEQ // tm, D_FF_PAD // tn),
        in_specs=[pl.BlockSpec((tm, D_MODEL), lambda i, j: (i, 0)), pl.BlockSpec((tn, D_MODEL), lambda i, j: (j, 0)), tile, tile],
        out_specs=(tile, tile), out_shape=(out, out),
        compiler_params=_cparams(dimension_semantics=("arbitrary", "arbitrary")),
    )(dr2, w_down, a, b)


def _ff_up_bwd(da, db, w_gate, w_up, dr2, xhat1, rstd1, ln_g, ride=None):
    tm, tk = 1024, 768
    nk = D_FF_PAD // tk

    def body(da_ref, db_ref, wg_ref, wu_ref, dr2_ref, xhat_ref, rstd_ref, g_ref, dr1_ref, dg_ref, dbias_ref, acc):
        i, k = pl.program_id(0), pl.program_id(1)

        @pl.when(jnp.logical_and(i == 0, k == 0))
        def _():
            dg_ref[...] = jnp.zeros_like(dg_ref)
            dbias_ref[...] = jnp.zeros_like(dbias_ref)

        part = _dot_nt(da_ref[...], _side_by_side(wg_ref)) + _dot_nt(db_ref[...], _side_by_side(wu_ref))

        @pl.when(k == 0)
        def _():
            acc[...] = part

        @pl.when(k > 0)
        def _():
            acc[...] += part

        @pl.when(k == nk - 1)
        def _():
            dh = DN_ALPHA * dr2_ref[...] + acc[...]
            xhat = xhat_ref[...]
            dg_ref[...] += jnp.sum(dh * xhat, axis=0, keepdims=True)
            dbias_ref[...] += jnp.sum(dh, axis=0, keepdims=True)
            rstd = jnp.max(rstd_ref[...], axis=1, keepdims=True)
            dr1_ref[...] = _layer_norm_bwd(dh, xhat, rstd, g_ref[...])

    hid = pl.BlockSpec((tm, tk), lambda i, k: (i, k))
    wtile = pl.BlockSpec((tk // FF_PAD, D_MODEL, FF_PAD), lambda i, k: (k, 0, 0))
    row = pl.BlockSpec((tm, D_MODEL), lambda i, k: (i, 0))
    vec = pl.BlockSpec((1, D_MODEL), lambda i, k: (0, 0))
    return _call(
        body, "ff_up_bwd", (SEQ // tm, nk),
        [hid, hid, wtile, wtile, row, row, pl.BlockSpec((tm, 128), lambda i, k: (i, 0)), vec],
        [row, vec, vec], [_sds((SEQ, D_MODEL), F32), _sds((1, D_MODEL), F32), _sds((1, D_MODEL), F32)],
        [pltpu.VMEM((tm, D_MODEL), F32)], [da, db, w_gate, w_up, dr2, xhat1, rstd1, ln_g], ride)


def _mixer_bwd(dr1, proj, y_attn, y_ssm, glu, ys, w_ab, w_sb, w_glu, w_out, b_gate):
    tm = 256

    def body(dr1_ref, gl0_ref, gl1_ref, ya_ref, yssm_ref, glu_ref, ys_ref, wab_ref, wsb_ref, wglu_ref, wout_ref, bg_ref,
             dya_ref, dyssm_ref, dgl_ref, dattn_ref, dglu_ref, dys_ref, mixed_ref, ysb_ref, gy_ref, dbg_ref):
        @pl.when(pl.program_id(0) == 0)
        def _():
            dbg_ref[...] = jnp.zeros_like(dbg_ref)

        dmixed = _dot_nt(dr1_ref[...].astype(BF16), wout_ref[...])
        g0 = jax.nn.sigmoid(gl0_ref[...] + _side_by_side(bg_ref, 0))
        g1 = jax.nn.sigmoid(gl1_ref[...] + _side_by_side(bg_ref, 1))
        y_attn, y_ssm = ya_ref[...], yssm_ref[...]
        mixed_ref[...] = (g0 * y_attn + g1 * y_ssm).astype(BF16)
        dya = (dmixed * g0).astype(BF16)
        dyssm = (dmixed * g1).astype(BF16)
        dya_ref[...] = dya
        dyssm_ref[...] = dyssm
        dgl0 = dmixed * y_attn * g0 * (1.0 - g0)
        dgl1 = dmixed * y_ssm * g1 * (1.0 - g1)
        dgl_ref[:, :GL_COL * D_MODEL] = jnp.zeros((tm, GL_COL * D_MODEL), BF16)
        dgl_ref[:, GL_COL * D_MODEL:(GL_COL + 1) * D_MODEL] = dgl0.astype(BF16)
        dgl_ref[:, (GL_COL + 1) * D_MODEL:] = dgl1.astype(BF16)
        dbg_ref[:, :D_MODEL] += jnp.sum(dgl0, axis=0, keepdims=True)
        dbg_ref[:, D_MODEL:] += jnp.sum(dgl1, axis=0, keepdims=True)
        dattn_ref[...] = _dot_nt(dya, _side_by_side(wab_ref))
        dy_s = _dot_nt(dyssm, _side_by_side(wsb_ref))
        glu = glu_ref[...]
        glu1, sg = glu[:, :SSM_WIDTH], jax.nn.sigmoid(glu[:, SSM_WIDTH:])
        ysb_ref[...] = (glu1 * sg).astype(BF16)
        dglu1 = (dy_s * sg).astype(BF16)
        dglu2 = (dy_s * glu1 * sg * (1.0 - sg)).astype(BF16)
        dglu_ref[:, :SSM_WIDTH] = dglu1
        dglu_ref[:, SSM_WIDTH:] = dglu2
        dgy = _dot_nt(jnp.concatenate([dglu1, dglu2], axis=1), _side_by_side(wglu_ref))
        ys = ys_ref[...]
        gy, t = _gelu(ys)
        gy_ref[...] = gy.astype(BF16)
        dys_ref[...] = dgy * _gelu_grad(ys, t)

    wide_b, half_b = _sds((SEQ, D_MODEL), BF16), _sds((SEQ, SSM_WIDTH), BF16)
    half_f = _sds((SEQ, SSM_WIDTH), F32)
    return _pallas_call(
        body, name="mixer_bwd", grid=(SEQ // tm,),
        in_specs=[_row_spec(tm, D_MODEL), _row_spec(tm, D_MODEL, GL_COL), _row_spec(tm, D_MODEL, GL_COL + 1), _row_spec(tm, D_MODEL),
                  _row_spec(tm, D_MODEL), _row_spec(tm, D_MODEL), _row_spec(tm, SSM_WIDTH), _full_spec((N_DEV, ATTN_WIDTH, 128)),
                  _full_spec((N_DEV, SSM_WIDTH, 128)), _full_spec((N_DEV, SSM_WIDTH, 128)), _full_spec((D_MODEL, D_MODEL)),
                  _full_spec((N_DEV, 2, 128))],
        out_specs=(_row_spec(tm, D_MODEL), _row_spec(tm, D_MODEL), _row_spec(tm, IN_WIDTH), _row_spec(tm, ATTN_WIDTH),
                   _row_spec(tm, D_MODEL), _row_spec(tm, SSM_WIDTH), _row_spec(tm, D_MODEL), _row_spec(tm, SSM_WIDTH),
                   _row_spec(tm, SSM_WIDTH), _full_spec((1, 2 * D_MODEL))),
        out_shape=(wide_b, wide_b, _sds((SEQ, IN_WIDTH), BF16), half_f, wide_b, half_f, wide_b, half_b, half_b,
                   _sds((1, 2 * D_MODEL), F32)),
        compiler_params=_cparams(dimension_semantics=("arbitrary",)),
    )(dr1, proj, proj, y_attn, y_ssm, glu, ys, w_ab, w_sb, w_glu, w_out, b_gate)


def _grad_x(dproj, w_in, dr1, ride=None):
    tm, tk = 1024, 1792
    nk = IN_WIDTH // tk

    def body(dp_ref, w_ref, dr1_ref, o_ref, acc):
        k = pl.program_id(1)
        part = _dot_nt(dp_ref[...], _side_by_side(w_ref))

        @pl.when(k == 0)
        def _():
            acc[...] = part

        @pl.when(k > 0)
        def _():
            acc[...] += part

        @pl.when(k == nk - 1)
        def _():
            o_ref[...] = DN_ALPHA * dr1_ref[...] + acc[...]

    row = pl.BlockSpec((tm, D_MODEL), lambda i, k: (i, 0))
    return _call(
        body, "grad_x", (SEQ // tm, nk),
        [pl.BlockSpec((tm, tk), lambda i, k: (i, k)), pl.BlockSpec((2, D_MODEL, tk // 2), lambda i, k: (k, 0, 0)), row],
        [row], [_sds((SEQ, D_MODEL), F32)], [pltpu.VMEM((tm, D_MODEL), F32)], [dproj, w_in, dr1], ride)


def _weight_grad(a, b, name, shard_cols=None, ride=None):
    k, n = a.shape[1], b.shape[1]
    tk = min(k, 512) if shard_cols else k // N_DEV
    tn = n // 4 if shard_cols else min(n, 1024)

    def body(a_ref, b_ref, o_ref):
        grad = _dot_tn(a_ref[...].astype(BF16), b_ref[...].astype(BF16))
        if shard_cols:
            o_ref[0] = grad[:, :shard_cols].astype(BF16)
            o_ref[1] = grad[:, shard_cols:].astype(BF16)
        else:
            o_ref[...] = grad.astype(BF16)

    if shard_cols:
        out_spec = pl.BlockSpec((2, None, tk, shard_cols), lambda kk, j: (0, j, kk, 0))
        out_shape = _sds((2, 4, k, shard_cols), BF16)
    else:
        out_spec = pl.BlockSpec((None, None, tk, tn), lambda kk, j: (kk % 2, kk // 2, 0, j))
        out_shape = _sds((2, 4, tk, n), BF16)
    out = _call(body, name, (k // tk, n // tn),
                [pl.BlockSpec((SEQ, tk), lambda kk, j: (0, kk)), pl.BlockSpec((SEQ, tn), lambda kk, j: (0, j))],
                [out_spec], [out_shape], [], [a, b], ride)
    return out[0] if ride is None else out


MESH = pl.DeviceIdType.MESH
ANY = pl.BlockSpec(memory_space=pl.ANY)


def _place():
    return lax.axis_index("x"), lax.axis_index("y"), lax.axis_index("c")


def _other_chips(x, y):
    return [(1 - x, y), (x, 1 - y), (1 - x, 1 - y)]


class _Ride:
    def __init__(self, operands, results, aliases, sems, start, wait):
        self.operands, self.results, self.aliases, self.sems = list(operands), list(results), dict(aliases), list(sems)
        self.start, self.wait = start, wait

    def __add__(self, other):
        n_in, n_out, n_sem = len(self.operands), len(self.results), len(self.sems)

        def both(which):
            def run(ins, outs, sems):
                getattr(self, which)(ins[:n_in], outs[:n_out], sems[:n_sem])
                getattr(other, which)(ins[n_in:], outs[n_out:], sems[n_sem:])
            return run

        aliases = {**self.aliases, **{n_in + i: n_out + j for i, j in other.aliases.items()}}
        return _Ride(self.operands + other.operands, self.results + other.results, aliases, self.sems + other.sems,
                     both("start"), both("wait"))


def _call(body, name, grid, in_specs, out_specs, out_shape, scratch_shapes, operands, ride=None, aliases=None):
    in_specs, out_specs, out_shape = list(in_specs), list(out_specs), list(out_shape)
    scratch_shapes, operands, aliases = list(scratch_shapes), list(operands), dict(aliases or {})
    kernel_body = body
    if ride is not None:
        n_in, n_out, n_scr, r_in, r_out = len(in_specs), len(out_specs), len(scratch_shapes), len(ride.operands), len(ride.results)

        def kernel_body(*refs):
            out0, scr0 = n_in + r_in, n_in + r_in + n_out + r_out
            ride_refs = (refs[n_in:out0], refs[out0 + n_out:scr0], refs[scr0 + n_scr:])
            ids = [pl.program_id(i) for i in range(len(grid))]
            first = functools.reduce(jnp.logical_and, [i == 0 for i in ids])
            last = functools.reduce(jnp.logical_and, [i == g - 1 for i, g in zip(ids, grid)])

            @pl.when(first)
            def _():
                ride.start(*ride_refs)

            body(*refs[:n_in], *refs[out0:out0 + n_out], *refs[scr0:scr0 + n_scr])

            @pl.when(last)
            def _():
                ride.wait(*ride_refs)

        aliases.update({n_in + i: n_out + j for i, j in ride.aliases.items()})
        in_specs += [ANY] * r_in
        out_specs += [ANY] * r_out
        out_shape += ride.results
        scratch_shapes += ride.sems
        operands += ride.operands
    return _pallas_call(
        kernel_body, name=name, grid=grid, in_specs=in_specs, out_specs=out_specs, out_shape=out_shape,
        scratch_shapes=scratch_shapes, input_output_aliases=aliases,
        compiler_params=_cparams(dimension_semantics=("arbitrary",) * len(grid)),
    )(*operands)


def _gather_first_level(shards):
    n = len(shards)

    def copies(ins, outs, sems, landed):
        send_sems, recv_sems, local_sems = sems
        x, y, c = _place()
        peers = [(x, y, 1 - c)] + [(px, py, c) for px, py in _other_chips(x, y)]

        def row(peer):
            return 4 * x + 2 * y + c if not landed else 4 * peer[0] + 2 * peer[1] + peer[2]

        local = [pltpu.make_async_copy(ins[a], outs[a].at[4 * x + 2 * y + c], local_sems.at[a]) for a in range(n)]
        remote = [pltpu.make_async_remote_copy(
            src_ref=ins[a], dst_ref=outs[a].at[row(peer)], send_sem=send_sems.at[a, k], recv_sem=recv_sems.at[a, k],
            device_id=peer, device_id_type=MESH) for a in range(n) for k, peer in enumerate(peers)]
        return local, remote

    def start(ins, outs, sems):
        local, remote = copies(ins, outs, sems, False)
        for cp in local + remote:
            cp.start()

    def wait(ins, outs, sems):
        local, sent = copies(ins, outs, sems, False)
        for cp in copies(ins, outs, sems, True)[1]:
            cp.wait_recv()
        for cp in sent:
            cp.wait_send()
        for cp in local:
            cp.wait()

    return _Ride(shards, [_sds((N_DEV,) + s.shape, s.dtype) for s in shards], {},
                 [pltpu.SemaphoreType.DMA((n, 4)), pltpu.SemaphoreType.DMA((n, 4)), pltpu.SemaphoreType.DMA((n,))], start, wait)


def _gather_second_level(buffers):
    n = len(buffers)

    def copies(outs, sems, core):
        send_sems, recv_sems = sems
        x, y, c = _place()
        return [pltpu.make_async_remote_copy(
            src_ref=outs[a].at[4 * px + 2 * py + core], dst_ref=outs[a].at[4 * px + 2 * py + core], send_sem=send_sems.at[a, j],
            recv_sem=recv_sems.at[a, j], device_id=(x, y, 1 - c), device_id_type=MESH)
            for a in range(n) for j, (px, py) in enumerate(_other_chips(x, y))]

    def start(ins, outs, sems):
        for cp in copies(outs, sems, lax.axis_index("c")):
            cp.start()

    def wait(ins, outs, sems):
        for cp in copies(outs, sems, 1 - lax.axis_index("c")):
            cp.wait_recv()
        for cp in copies(outs, sems, lax.axis_index("c")):
            cp.wait_send()

    return _Ride(buffers, [_sds(b.shape, b.dtype) for b in buffers], {i: i for i in range(n)},
                 [pltpu.SemaphoreType.DMA((n, 3)), pltpu.SemaphoreType.DMA((n, 3))], start, wait)


def _sibling_swap_ride(grads):
    n = len(grads)

    def copies(ins, outs, sems):
        x, y, c = _place()
        return [pltpu.make_async_remote_copy(
            src_ref=ins[a].at[1 - c], dst_ref=outs[a], send_sem=sems[0].at[a], recv_sem=sems[1].at[a],
            device_id=(x, y, 1 - c), device_id_type=MESH) for a in range(n)]

    def start(ins, outs, sems):
        for cp in copies(ins, outs, sems):
            cp.start()

    def wait(ins, outs, sems):
        for cp in copies(ins, outs, sems):
            cp.wait()

    return _Ride(grads, [_sds(g.shape[1:], g.dtype) for g in grads], {},
                 [pltpu.SemaphoreType.DMA((n,)), pltpu.SemaphoreType.DMA((n,))], start, wait)


def _chip_swap_ride(sums):
    n = len(sums)

    def copies(ins, outs, sems, landed):
        send_sems, recv_sems, local_sems = sems
        x, y, c = _place()
        mine = 2 * x + y
        local = [pltpu.make_async_copy(ins[a].at[mine], outs[a].at[mine], local_sems.at[a]) for a in range(n)]
        remote = [pltpu.make_async_remote_copy(
            src_ref=ins[a].at[2 * px + py], dst_ref=outs[a].at[2 * px + py if landed else mine], send_sem=send_sems.at[a, j],
            recv_sem=recv_sems.at[a, j], device_id=(px, py, c), device_id_type=MESH)
            for a in range(n) for j, (px, py) in enumerate(_other_chips(x, y))]
        return local, remote

    def start(ins, outs, sems):
        local, remote = copies(ins, outs, sems, False)
        for cp in local + remote:
            cp.start()

    def wait(ins, outs, sems):
        local, sent = copies(ins, outs, sems, False)
        for cp in copies(ins, outs, sems, True)[1]:
            cp.wait_recv()
        for cp in sent:
            cp.wait_send()
        for cp in local:
            cp.wait()

    return _Ride(sums, [_sds(s.shape, s.dtype) for s in sums], {},
                 [pltpu.SemaphoreType.DMA((n, 3)), pltpu.SemaphoreType.DMA((n, 3)), pltpu.SemaphoreType.DMA((n,))], start, wait)


def _send_buffer(w, rows, cols, name):
    r, c = w.shape

    def body(w_ref, o_ref):
        if (r, c) != (rows, cols):
            o_ref[...] = jnp.zeros((rows, cols), BF16)
        o_ref[:r, :c] = w_ref[...].astype(BF16)

    return _pallas_call(body, name=name, out_shape=_sds((rows, cols), BF16))(w)


def _all_gather(shards, name):
    n = len(shards)

    def body(*refs):
        ins, outs = refs[:n], refs[n:2 * n]
        send_sems, recv_sems, local_sems = refs[2 * n:]
        x, y, c = _place()
        me, sibling = (x, y, c), (x, y, 1 - c)
        chips = _other_chips(x, y)

        def slot(a, px, py, pc):
            return outs[a].at[4 * px + 2 * py + pc]

        def copy(a, k, block, to, src=None):
            return pltpu.make_async_remote_copy(
                src_ref=slot(a, *block) if src is None else src, dst_ref=slot(a, *block),
                send_sem=send_sems.at[a, k], recv_sem=recv_sems.at[a, k], device_id=to, device_id_type=MESH)

        mine = [pltpu.make_async_copy(ins[a], slot(a, *me), local_sems.at[a]) for a in range(n)]
        for cp in mine:
            cp.start()
        first = []
        for a in range(n):
            first.append(copy(a, 0, me, sibling, src=ins[a]))
            first += [copy(a, 1 + j, me, (*chip, c), src=ins[a]) for j, chip in enumerate(chips)]
        for cp in first:
            cp.start()
        passed = []
        for j, chip in enumerate(chips):
            for a in range(n):
                copy(a, 1 + j, (*chip, c), me).wait_recv()
                onward = copy(a, 4 + j, (*chip, c), sibling)
                onward.start()
                passed.append(onward)
        for a in range(n):
            copy(a, 0, sibling, me).wait_recv()
            for j, chip in enumerate(chips):
                copy(a, 4 + j, (*chip, 1 - c), me).wait_recv()
        for cp in first + passed:
            cp.wait_send()
        for cp in mine:
            cp.wait()

    return _pallas_call(
        body, name=name, in_specs=[ANY] * n, out_specs=[ANY] * n,
        out_shape=[_sds((N_DEV,) + s.shape, s.dtype) for s in shards],
        scratch_shapes=[pltpu.SemaphoreType.DMA((n, 7)), pltpu.SemaphoreType.DMA((n, 7)), pltpu.SemaphoreType.DMA((n,))],
    )(*shards)


def _swap_with_sibling(grads, name):
    n = len(grads)

    def body(*refs):
        ins, outs = refs[:n], refs[n:2 * n]
        send_sems, recv_sems = refs[2 * n:]
        x, y, c = _place()
        copies = [pltpu.make_async_remote_copy(
            src_ref=ins[a].at[1 - c], dst_ref=outs[a], send_sem=send_sems.at[a], recv_sem=recv_sems.at[a],
            device_id=(x, y, 1 - c), device_id_type=MESH) for a in range(n)]
        for cp in copies:
            cp.start()
        for cp in copies:
            cp.wait()

    return _pallas_call(
        body, name=name, in_specs=[ANY] * n, out_specs=[ANY] * n,
        out_shape=[_sds(g.shape[1:], g.dtype) for g in grads],
        scratch_shapes=[pltpu.SemaphoreType.DMA((n,)), pltpu.SemaphoreType.DMA((n,))],
    )(*grads)


def _pair_sum(g, r, core, name):
    _, _, k, n = g.shape

    def body(core_ref, g_ref, r_ref, o_ref):
        o_ref[...] = (g_ref[...].astype(F32) + r_ref[...].astype(F32)).astype(o_ref.dtype)

    return _pallas_call(
        body, name=name,
        grid_spec=pltpu.PrefetchScalarGridSpec(
            num_scalar_prefetch=1, grid=(4,),
            in_specs=[pl.BlockSpec((None, None, k, n), lambda p, core_ref: (core_ref[0], p, 0, 0)),
                      pl.BlockSpec((None, k, n), lambda p, core_ref: (p, 0, 0))],
            out_specs=pl.BlockSpec((None, k, n), lambda p, core_ref: (p, 0, 0))),
        out_shape=_sds((4, k, n), g.dtype), compiler_params=_cparams(dimension_semantics=("arbitrary",)),
    )(core, g, r)


def _adamw_math(w, g, m, v):
    m = ADAM_B1 * m + (1.0 - ADAM_B1) * g
    v = ADAM_B2 * v + (1.0 - ADAM_B2) * (g * g)
    m_hat = m / (1.0 - ADAM_B1 ** ADAM_STEP)
    v_hat = v / (1.0 - ADAM_B2 ** ADAM_STEP)
    return -ADAM_LR * (m_hat / (jnp.sqrt(v_hat) + ADAM_EPS) + ADAM_WD * w), m, v


def _adamw(w, m, v, parts, name):
    r, c = w.shape
    tr = r if r <= 512 else 256
    n_parts, pr, pc = parts.shape
    assert r % tr == 0 and (tr == r or pr == r)

    def body(w_ref, m_ref, v_ref, p_ref, g_out, d_out, m_out, v_out):
        g = p_ref[0, :tr, :c].astype(F32)
        for p in range(1, n_parts):
            g = g + p_ref[p, :tr, :c].astype(F32)
        g_out[...] = g
        d_out[...], m_out[...], v_out[...] = _adamw_math(w_ref[...], g, m_ref[...], v_ref[...])

    tile = pl.BlockSpec((tr, c), lambda i: (i, 0))
    part_tile = pl.BlockSpec((n_parts, pr if tr == r else tr, pc), lambda i: (0, i, 0))
    out = _sds((r, c), F32)
    return _pallas_call(
        body, name=name, grid=(r // tr,), in_specs=[tile, tile, tile, part_tile], out_specs=(tile,) * 4,
        out_shape=(out,) * 4, compiler_params=_cparams(dimension_semantics=("arbitrary",)),
    )(w, m, v, parts)


SMALL = ("ssm_a_re", "ssm_a_im", "ssm_log_dt", "ssm_b_re", "ssm_b_im", "ssm_c_re", "ssm_c_im", "ssm_d",
         "ln1_g", "ln1_b", "ln2_g", "ln2_b")


def _pack_rows(arrays):
    rows = []
    for a in arrays:
        flat = a.reshape(-1)
        rows.append(jnp.pad(flat, (0, -flat.shape[0] % 128)).reshape(-1, 128))
    packed = jnp.concatenate(rows, axis=0)
    return jnp.pad(packed, ((0, -packed.shape[0] % 8), (0, 0)))


def _unpack_rows(packed, shapes):
    out, row = [], 0
    for shape in shapes:
        size = math.prod(shape)
        n_rows = -(-size // 128)
        out.append(packed[row:row + n_rows].reshape(-1)[:size].reshape(shape))
        row += n_rows
    return out


def _sum_devices(parts):
    def body(p_ref, o_ref):
        total = p_ref[0]
        for dev in range(1, N_DEV):
            total = total + p_ref[dev]
        o_ref[...] = total

    return _pallas_call(body, name="sum_devices", out_shape=_sds(parts.shape[1:], F32))(parts)


def _adamw_replicated(ws, ms, vs, gs):
    n = len(ws)

    def body(*refs):
        w_refs, m_refs, v_refs, g_refs, d_out, m_out, v_out = (refs[i * n:(i + 1) * n] for i in range(7))
        for i in range(n):
            d_out[i][...], m_out[i][...], v_out[i][...] = _adamw_math(w_refs[i][...], g_refs[i][...], m_refs[i][...], v_refs[i][...])

    out = _pallas_call(body, name="adamw_replicated", out_shape=[_sds(w.shape, F32) for w in ws] * 3,
                       compiler_params=_cparams())(*ws, *ms, *vs, *gs)
    return out[:n], out[n:2 * n], out[2 * n:]


def kernel(x, w_in, b_gate, w_attn_br, w_ssm_br, w_out, ssm_a_re, ssm_a_im, ssm_log_dt, ssm_b_re, ssm_b_im, ssm_c_re, ssm_c_im, ssm_d, w_glu, ln1_g, ln1_b, w_ff_gate, w_ff_up, w_ff_down, ln2_g, ln2_b, loss_target, m_w_in, m_b_gate, m_w_attn_br, m_w_ssm_br, m_w_out, m_ssm_a_re, m_ssm_a_im, m_ssm_log_dt, m_ssm_b_re, m_ssm_b_im, m_ssm_c_re, m_ssm_c_im, m_ssm_d, m_w_glu, m_ln1_g, m_ln1_b, m_w_ff_gate, m_w_ff_up, m_w_ff_down, m_ln2_g, m_ln2_b, v_w_in, v_b_gate, v_w_attn_br, v_w_ssm_br, v_w_out, v_ssm_a_re, v_ssm_a_im, v_ssm_log_dt, v_ssm_b_re, v_ssm_b_im, v_ssm_c_re, v_ssm_c_im, v_ssm_d, v_w_glu, v_ln1_g, v_ln1_b, v_w_ff_gate, v_w_ff_up, v_w_ff_down, v_ln2_g, v_ln2_b):
    given = dict(locals())
    x2, target = x[0], loss_target[0]
    core = lax.axis_index("c").astype(jnp.int32).reshape(1)

    sharded = ("w_in", "w_attn_br", "w_ssm_br", "w_glu", "w_ff_gate", "w_ff_up", "b_gate", "w_out", "w_ff_down")
    send_shape = dict(w_in=(D_MODEL, 896), w_attn_br=(ATTN_WIDTH, 128), w_ssm_br=(SSM_WIDTH, 128), w_glu=(SSM_WIDTH, 128),
                      w_out=(128, D_MODEL), w_ff_gate=(D_MODEL, FF_PAD), w_ff_up=(D_MODEL, FF_PAD), w_ff_down=(FF_PAD, D_MODEL))
    local = {k: given[k][0] for k in sharded}
    sends = {k: local[k] if k == "b_gate" else _send_buffer(local[k], *send_shape[k], name="send_" + k) for k in sharded}
    mixer_weights = ("w_attn_br", "w_ssm_br", "w_glu", "b_gate", "w_out")
    ff_weights = ("w_ff_gate", "w_ff_up", "w_ff_down")
    wt = {}
    wt["w_in"], = _all_gather([sends["w_in"]], "gather_w_in")

    a_re, a_im, log_dt = ssm_a_re[0], ssm_a_im[0], ssm_log_dt[0].reshape(SSM_GROUPS, 1)
    b_re_t, b_im_t = ssm_b_re[0].transpose(0, 2, 1), ssm_b_im[0].transpose(0, 2, 1)
    abar_re, abar_im, e_re, e_im, bbar_re_t, bbar_im_t = _ssm_prep(a_re, a_im, log_dt, b_re_t, b_im_t)
    bmat, cmat, a_chunks = _ssm_tables(abar_re, abar_im, bbar_re_t, bbar_im_t, ssm_c_re[0], ssm_c_im[0])
    cos_t, sin_t = _rope_tables()

    proj, *partly = _proj(x2, wt["w_in"], _gather_first_level([sends[k] for k in mixer_weights]))
    attn, lse, *landed = _attn_fwd(proj, cos_t, sin_t,
                                   _gather_second_level(partly) + _gather_first_level([sends[k] for k in ff_weights]))
    wt.update(zip(mixer_weights, landed[:len(mixer_weights)]))
    ys, states, *landed = _ssm_fwd(proj, bmat, cmat, a_chunks, ssm_d, _gather_second_level(landed[len(mixer_weights):]))
    wt.update(zip(ff_weights, landed))
    wt["w_out"] = wt["w_out"].reshape(D_MODEL, D_MODEL)
    wt["w_ff_down"] = wt["w_ff_down"].reshape(D_FF_PAD, D_MODEL)
    b_gate_full = wt["b_gate"]
    h, xhat1, rstd1, glu, y_attn, y_ssm = _mixer_out(attn, ys, proj, x2, wt["w_attn_br"], wt["w_ssm_br"], wt["w_glu"],
                                                      wt["w_out"], b_gate_full, ln1_g, ln1_b)
    ff_a, ff_b, ff_f = _ff_up(h, wt["w_ff_gate"], wt["w_ff_up"])
    dr2, d_ln2_g, d_ln2_b, loss_lanes = _ff_down_loss(ff_f, wt["w_ff_down"], h, target, ln2_g, ln2_b)

    def pair_sums(names, contrib, from_sibling):
        return [_pair_sum(contrib[k], r, core, "pair_sum_" + k) for k, r in zip(names, from_sibling)]

    d_a, d_b = _ff_down_bwd(dr2, wt["w_ff_down"], ff_a, ff_b)
    contrib = dict(w_ff_gate=_weight_grad(h, d_a, "wgrad_w_ff_gate", FF_PAD),
                   w_ff_up=_weight_grad(h, d_b, "wgrad_w_ff_up", FF_PAD),
                   w_ff_down=_weight_grad(ff_f, dr2, "wgrad_w_ff_down"))
    dr1, d_ln1_g, d_ln1_b, *from_sibling = _ff_up_bwd(
        d_a, d_b, wt["w_ff_gate"], wt["w_ff_up"], dr2, xhat1, rstd1, ln1_g, _sibling_swap_ride([contrib[k] for k in ff_weights]))
    ff_sums = pair_sums(ff_weights, contrib, from_sibling)

    d_ya, d_yssm, d_proj, d_attn, d_glu, d_ys, mixed, y_s, gy, d_bg = _mixer_bwd(
        dr1, proj, y_attn, y_ssm, glu, ys, wt["w_attn_br"], wt["w_ssm_br"], wt["w_glu"], wt["w_out"], b_gate_full)
    contrib.update(w_attn_br=_weight_grad(attn, d_ya, "wgrad_w_attn_br", 128),
                   w_ssm_br=_weight_grad(y_s, d_yssm, "wgrad_w_ssm_br", 128),
                   w_glu=_weight_grad(gy, d_glu, "wgrad_w_glu", 128),
                   w_out=_weight_grad(mixed, dr1, "wgrad_w_out"),
                   b_gate=d_bg.reshape(2, 4, 2, 128).transpose(2, 1, 0, 3))
    d_proj, *landed = _attn_bwd(proj, cos_t, sin_t, attn, lse, d_attn, d_proj,
                                _chip_swap_ride(ff_sums) + _sibling_swap_ride([contrib[k] for k in mixer_weights]))
    parts = dict(zip(ff_weights, landed[:len(ff_weights)]))
    mixer_sums = pair_sums(mixer_weights, contrib, landed[len(ff_weights):])
    d_proj, d_bmat, d_cmat, d_abar, d_skip, *landed = _ssm_bwd(d_ys, proj, states, bmat, cmat, a_chunks, ssm_d, d_proj,
                                                               _chip_swap_ride(mixer_sums))
    parts.update(zip(mixer_weights, landed))

    gbb_re_t, gbb_im_t = _block_diag_parts(d_bmat, True)
    gc_re, gc_im = _block_diag_parts(d_cmat, False)
    ga_re = d_abar[:, 0, :CHUNK_STATES].reshape(SSM_GROUPS, SSM_STATE)
    ga_im = d_abar[:, 0, CHUNK_STATES:].reshape(SSM_GROUPS, SSM_STATE)
    g_a_re, g_a_im, g_log_dt, g_b_re_t, g_b_im_t = _ssm_param_bwd(
        a_re, a_im, log_dt, b_re_t, b_im_t, abar_re, abar_im, e_re, e_im, ga_re, ga_im, gbb_re_t, gbb_im_t)
    mine = [g_a_re, g_a_im, g_log_dt, g_b_re_t.transpose(0, 2, 1), g_b_im_t.transpose(0, 2, 1), gc_re, -gc_im,
            d_skip, d_ln1_g, d_ln1_b, d_ln2_g, d_ln2_b]
    small_packed = _pack_rows(mine + [loss_lanes])

    contrib["w_in"], small_partly = _weight_grad(x2, d_proj, "wgrad_w_in", 896, _gather_first_level([small_packed]))
    w_in_sum = pair_sums(["w_in"], contrib, _swap_with_sibling([contrib["w_in"]], "swap_w_in_with_sibling"))
    grad_x, parts["w_in"], every = _grad_x(d_proj, wt["w_in"], dr1,
                                          _chip_swap_ride(w_in_sum) + _gather_second_level([small_partly]))

    grads, deltas, new_m, new_v = {}, {}, {}, {}
    for k in sharded:
        w2 = local[k]
        out = _adamw(w2, given["m_" + k][0], given["v_" + k][0], parts[k], "adamw_" + k)
        grads[k], deltas[k], new_m[k], new_v[k] = (o.reshape((1,) + w2.shape) for o in out)

    *small_grads, loss_sum = _unpack_rows(_sum_devices(every), [given[k].shape for k in SMALL] + [(1, 128)])
    small = _adamw_replicated([given[k] for k in SMALL], [given["m_" + k] for k in SMALL],
                              [given["v_" + k] for k in SMALL], small_grads)
    for res, values in zip((grads, deltas, new_m, new_v), (small_grads,) + small):
        res.update(zip(SMALL, values))
    loss = loss_sum[0, 0]

    order = ("w_in", "b_gate", "w_attn_br", "w_ssm_br", "w_out", "ssm_a_re", "ssm_a_im", "ssm_log_dt", "ssm_b_re", "ssm_b_im",
             "ssm_c_re", "ssm_c_im", "ssm_d", "w_glu", "ln1_g", "ln1_b", "w_ff_gate", "w_ff_up", "w_ff_down", "ln2_g", "ln2_b")
    return (loss, grad_x[None], *[grads[k] for k in order], *[deltas[k] for k in order], *[new_m[k] for k in order],
            *[new_v[k] for k in order])
```

```python
import functools
import math

import jax
import jax.numpy as jnp
from jax import lax
from jax.experimental import pallas as pl
from jax.experimental.pallas import tpu as pltpu

F32 = jnp.float32
BF16 = jnp.bfloat16

N_DEV = 8
SEQ = 2048
D_MODEL = 1024
HEAD_DIM = 64
ATTN_WIDTH = 512
QKV_WIDTH = 1536
SSM_WIDTH = 512
SSM_GROUPS = 32
SSM_GROUP = 16
SSM_STATE = 64
IN_WIDTH = 7168
D_FF = 2816
FF_SHARD = D_FF // N_DEV
FF_PAD = 384
D_FF_PAD = FF_PAD * N_DEV
DN_ALPHA = 2.0 ** 0.25
LN_EPS = 1e-5
NEG_INF = -1e30
ROPE_THETA = 10000.0
BLOCK = 128
GROUPS = ((1, 16), (4, 4), (16, 1))

ADAM_LR = 0.001
ADAM_B1 = 0.9
ADAM_B2 = 0.999
ADAM_EPS = 1e-08
ADAM_WD = 0.01
ADAM_STEP = 10

VMEM_LIMIT = 56 * 1024 * 1024


_pallas_call = pl.pallas_call


def _cparams(**kw):
    return pltpu.CompilerParams(vmem_limit_bytes=VMEM_LIMIT, **kw)


def _dot(a, b):
    return jnp.dot(a, b, preferred_element_type=F32)


def _dot_nt(a, b):
    return lax.dot_general(a, b, (((1,), (1,)), ((), ())), preferred_element_type=F32)


def _side_by_side(w_ref, row=None):
    rows = slice(None) if row is None else pl.ds(row, 1)
    return jnp.concatenate([w_ref[i, rows, :] for i in range(w_ref.shape[0])], axis=1)


def _dot_tn(a, b):
    return lax.dot_general(a, b, (((0,), (0,)), ((), ())), preferred_element_type=F32)


def _rope_tables():
    half = HEAD_DIM // 2
    inv_freq = ROPE_THETA ** (-jnp.arange(half, dtype=F32) / half)
    ang = jnp.arange(SEQ, dtype=F32)[:, None] * inv_freq[None, :]
    cos, sin = jnp.cos(ang), jnp.sin(ang)
    tables = jnp.tile(cos, (1, 4)), jnp.tile(jnp.concatenate([-sin, sin], axis=1), (1, 2))

    def by_phase(t):
        return jnp.stack([t.reshape(SEQ // d, d, 128).transpose(1, 0, 2).reshape(SEQ, 128) for d, _ in GROUPS])

    return by_phase(tables[0]), by_phase(tables[1])


def _swap_halves(x):
    lane = lax.broadcasted_iota(jnp.int32, x.shape, 1)
    return jnp.where((lane & 63) < 32, pltpu.roll(x, 96, axis=1), pltpu.roll(x, 32, axis=1))


def _group_rows(d, nb, r, i):
    src = pl.ds(i * BLOCK, BLOCK) if d == 1 else pl.ds(r + i * BLOCK * d, BLOCK, stride=d)
    return src, pl.ds((r * nb + i) * BLOCK, BLOCK)


def _attn_masks():
    a_idx = lax.broadcasted_iota(jnp.int32, (2 * BLOCK, 2 * BLOCK), 0) & (BLOCK - 1)
    c_idx = lax.broadcasted_iota(jnp.int32, (2 * BLOCK, 2 * BLOCK), 1)
    cur_ok = jnp.logical_and(c_idx >= BLOCK, c_idx - BLOCK <= a_idx)
    prev_ok = jnp.logical_and(c_idx < BLOCK, c_idx >= a_idx)
    lane = lax.broadcasted_iota(jnp.int32, (BLOCK, 128), 1)
    return cur_ok, prev_ok, lane < HEAD_DIM


def _stack_heads(t, head0):
    zero = jnp.zeros_like(t)
    return jnp.concatenate([jnp.where(head0, t, zero), jnp.where(head0, zero, t)], axis=0)


def _unstack_heads(t2, head0):
    return jnp.where(head0, t2[:BLOCK], t2[BLOCK:])


def _attn_fwd(proj, cos_t, sin_t, ride=None):
    def body(q0, q1, q2, k0, k1, k2, v0, v1, v2, cos_ref, sin_ref, attn_ref, lse_ref,
             qs, ks, vs, os_, ms, ls, acc, mnat, lnat):
        cur_ok, prev_ok, head0 = _attn_masks()
        ks[:BLOCK, :] = jnp.zeros((BLOCK, 128), BF16)
        vs[:BLOCK, :] = jnp.zeros((BLOCK, 128), BF16)
        for g, (d, nb) in enumerate(GROUPS):
            q_ref, k_ref, v_ref = (q0, q1, q2)[g], (k0, k1, k2)[g], (v0, v1, v2)[g]
            for r in range(d):
                for i in range(nb):
                    src, dst = _group_rows(d, nb, r, i)
                    below = pl.ds(dst.start + BLOCK, BLOCK)
                    c, s = cos_ref[g, dst, :], sin_ref[g, dst, :]
                    q = q_ref[src, :]
                    k = k_ref[src, :]
                    qs[dst, :] = ((q * c + _swap_halves(q) * s) * 0.125).astype(BF16)
                    ks[below, :] = (k * c + _swap_halves(k) * s).astype(BF16)
                    vs[below, :] = v_ref[src, :].astype(BF16)

            def block(b, carry, nb=nb):
                has_prev = (b & (nb - 1)) > 0
                cur = pl.ds(pl.multiple_of(b * BLOCK, BLOCK), BLOCK)
                window = pl.ds(pl.multiple_of(b * BLOCK, BLOCK), 2 * BLOCK)
                valid = jnp.logical_or(cur_ok, jnp.logical_and(prev_ok, has_prev))
                s = jnp.where(valid, _dot_nt(_stack_heads(qs[cur, :], head0), ks[window, :]), NEG_INF)
                m = jnp.max(s, axis=1, keepdims=True)
                p = jnp.exp(s - m)
                os_[cur, :] = _unstack_heads(_dot(p.astype(BF16), vs[window, :]), head0)
                ms[cur, :] = _unstack_heads(m, head0)
                ls[cur, :] = _unstack_heads(jnp.sum(p, axis=1, keepdims=True), head0)
                return carry

            lax.fori_loop(0, SEQ // BLOCK, block, 0, unroll=2)

            for r in range(d):
                for i in range(nb):
                    src, dst = _group_rows(d, nb, r, i)
                    if g == 0:
                        acc[src, :], mnat[src, :], lnat[src, :] = os_[dst, :], ms[dst, :], ls[dst, :]
                    else:
                        m_old, m_g = mnat[src, :], ms[dst, :]
                        m_new = jnp.maximum(m_old, m_g)
                        a_old, a_g = jnp.exp(m_old - m_new), jnp.exp(m_g - m_new)
                        acc[src, :] = a_old * acc[src, :] + a_g * os_[dst, :]
                        lnat[src, :] = a_old * lnat[src, :] + a_g * ls[dst, :]
                        mnat[src, :] = m_new
        for i in range(SEQ // BLOCK):
            rows = pl.ds(i * BLOCK, BLOCK)
            l = lnat[rows, :]
            attn_ref[rows, :] = acc[rows, :] / l
            lse_ref[rows, :] = mnat[rows, :] + jnp.log(l)

    def col(base):
        return pl.BlockSpec((SEQ, 128), lambda hp, base=base: (0, base + hp))

    in_specs = [col(g * 4) for g in range(3)] + [col(12 + g * 4) for g in range(3)] + [col(24 + g * 4) for g in range(3)]
    table = pl.BlockSpec((3, SEQ, 128), lambda hp: (0, 0, 0), pipeline_mode=pl.Buffered(1))
    out = pl.BlockSpec((SEQ, 128), lambda hp: (0, hp))
    return _call(
        body, "attn_fwd", (4,), in_specs + [table, table], [out, out],
        [_sds((SEQ, ATTN_WIDTH), F32), _sds((SEQ, ATTN_WIDTH), F32)],
        [pltpu.VMEM((SEQ, 128), BF16)] + [pltpu.VMEM((SEQ + BLOCK, 128), BF16)] * 2 + [pltpu.VMEM((SEQ, 128), F32)] * 6,
        [proj] * 9 + [cos_t, sin_t], ride)


def _attn_bwd_group_body(g):
    d, nb = GROUPS[g]

    def body(q_ref, k_ref, v_ref, cos_ref, sin_ref, lse_ref, dattn_ref, dsum_ref, dproj_ref,
             qs, ks, vs, dos, lss, dss, dqs, dks, dvs, stage, outs, sems):
        cur_ok, prev_ok, head0 = _attn_masks()
        ks[:BLOCK, :] = jnp.zeros((BLOCK, 128), BF16)
        vs[:BLOCK, :] = jnp.zeros((BLOCK, 128), BF16)
        dks[:BLOCK, :] = jnp.zeros((BLOCK, 128), F32)
        dvs[:BLOCK, :] = jnp.zeros((BLOCK, 128), F32)
        for r in range(d):
            for i in range(nb):
                src, dst = _group_rows(d, nb, r, i)
                below = pl.ds(dst.start + BLOCK, BLOCK)
                c, s = cos_ref[g, dst, :], sin_ref[g, dst, :]
                q = q_ref[src, :]
                k = k_ref[src, :]
                qs[dst, :] = ((q * c + _swap_halves(q) * s) * 0.125).astype(BF16)
                ks[below, :] = (k * c + _swap_halves(k) * s).astype(BF16)
                vs[below, :] = v_ref[src, :].astype(BF16)
                dos[dst, :] = dattn_ref[src, :].astype(BF16)
                dss[dst, :] = dsum_ref[src, :]
                lss[dst, :] = lse_ref[src, :]
                dks[below, :] = jnp.zeros((BLOCK, 128), F32)
                dvs[below, :] = jnp.zeros((BLOCK, 128), F32)

        def per_head_column(t):
            return jnp.concatenate([jnp.max(jnp.where(head0, t, NEG_INF), axis=1, keepdims=True),
                                    jnp.max(jnp.where(head0, NEG_INF, t), axis=1, keepdims=True)], axis=0)

        def block(b, carry):
            has_prev = (b & (nb - 1)) > 0
            cur = pl.ds(pl.multiple_of(b * BLOCK, BLOCK), BLOCK)
            window = pl.ds(pl.multiple_of(b * BLOCK, BLOCK), 2 * BLOCK)
            valid = jnp.logical_or(cur_ok, jnp.logical_and(prev_ok, has_prev))
            q2, do2 = _stack_heads(qs[cur, :], head0), _stack_heads(dos[cur, :], head0)
            kw, vw = ks[window, :], vs[window, :]
            s = jnp.where(valid, _dot_nt(q2, kw), NEG_INF)
            p = jnp.exp(s - per_head_column(lss[cur, :]))
            ds = (p * (_dot_nt(do2, vw) - per_head_column(dss[cur, :]))).astype(BF16)
            dvs[window, :] += _dot_tn(p.astype(BF16), do2)
            dks[window, :] += _dot_tn(ds, q2)
            dqs[cur, :] = _unstack_heads(_dot(ds, kw), head0)
            return carry

        lax.fori_loop(0, SEQ // BLOCK, block, 0, unroll=2)

        hp = pl.program_id(0)
        copies = []
        for kind in range(3):
            for r in range(d):
                for i in range(nb):
                    src, dst = _group_rows(d, nb, r, i)
                    below = pl.ds(dst.start + BLOCK, BLOCK)
                    if kind == 2:
                        stage[src, :] = dvs[below, :]
                    else:
                        c, s = cos_ref[g, dst, :], sin_ref[g, dst, :]
                        t = dqs[dst, :] * 0.125 if kind == 0 else dks[below, :]
                        stage[src, :] = t * c - _swap_halves(t) * s
            for i in range(SEQ // MM_ROWS):
                rows = pl.ds(i * MM_ROWS, MM_ROWS)
                outs[kind, rows, :] = stage[rows, :].astype(BF16)
            column = pl.multiple_of((kind * 12 + g * 4 + hp) * 128, 128)
            copies.append(pltpu.make_async_copy(outs.at[kind], dproj_ref.at[:, pl.ds(column, 128)], sems.at[kind]))
            copies[-1].start()
        for cp in copies:
            cp.wait()

    return body


def _attn_bwd(proj, cos_t, sin_t, attn, lse, dattn, dproj, ride=None):
    groups = [_attn_bwd_group_body(g) for g in range(3)]

    def body(q0, q1, q2, k0, k1, k2, v0, v1, v2, cos_ref, sin_ref, attn_ref, lse_ref, dattn_ref, dproj_in, dproj_ref,
             dsum, *scratch):
        del dproj_in
        head0 = _attn_masks()[2]
        for i in range(SEQ // BLOCK):
            rows = pl.ds(i * BLOCK, BLOCK)
            prod = dattn_ref[rows, :] * attn_ref[rows, :]
            d0 = jnp.sum(jnp.where(head0, prod, 0.0), axis=1, keepdims=True)
            d1 = jnp.sum(jnp.where(head0, 0.0, prod), axis=1, keepdims=True)
            dsum[rows, :] = jnp.where(head0, d0, d1)
        for g in range(3):
            groups[g]((q0, q1, q2)[g], (k0, k1, k2)[g], (v0, v1, v2)[g], cos_ref, sin_ref, lse_ref, dattn_ref, dsum,
                      dproj_ref, *scratch)

    def col(base):
        return pl.BlockSpec((SEQ, 128), lambda hp, base=base: (0, base + hp))

    table = pl.BlockSpec((3, SEQ, 128), lambda hp: (0, 0, 0), pipeline_mode=pl.Buffered(1))
    return _call(
        body, "attn_bwd", (4,),
        [col(g * 4) for g in range(3)] + [col(12 + g * 4) for g in range(3)] + [col(24 + g * 4) for g in range(3)]
        + [table, table, col(0), col(0), col(0), ANY],
        [ANY], [_sds((SEQ, IN_WIDTH), BF16)],
        [pltpu.VMEM((SEQ, 128), F32)]
        + [pltpu.VMEM((SEQ, 128), BF16)] + [pltpu.VMEM((SEQ + BLOCK, 128), BF16)] * 2 + [pltpu.VMEM((SEQ, 128), BF16)]
        + [pltpu.VMEM((SEQ, 128), F32)] * 3 + [pltpu.VMEM((SEQ + BLOCK, 128), F32)] * 2 + [pltpu.VMEM((SEQ, 128), F32)]
        + [pltpu.VMEM((3, SEQ, 128), BF16), pltpu.SemaphoreType.DMA((3,))],
        [proj] * 9 + [cos_t, sin_t, attn, lse, dattn, dproj], ride, aliases={14: 0})


SSM_CHUNKS = 4
CHUNK_STATES = 512
SCAN_ROWS = 8
U_COL = (3 * QKV_WIDTH) // 128


def _cmul(xr, xi, yr, yi):
    return xr * yr - xi * yi, xr * yi + xi * yr


def _ssm_prep(a_re, a_im, log_dt, b_re_t, b_im_t):
    def body(ar_ref, ai_ref, ldt_ref, br_ref, bi_ref, abr_ref, abi_ref, er_ref, ei_ref, bbr_ref, bbi_ref):
        ar, ai = ar_ref[...], ai_ref[...]
        dt = jnp.exp(ldt_ref[...])
        mag = jnp.exp(ar * dt)
        abr, abi = mag * jnp.cos(ai * dt), mag * jnp.sin(ai * dt)
        den = ar * ar + ai * ai
        nr, ni = abr - 1.0, abi
        er, ei = (nr * ar + ni * ai) / den, (ni * ar - nr * ai) / den
        abr_ref[...], abi_ref[...], er_ref[...], ei_ref[...] = abr, abi, er, ei
        er3, ei3 = er[:, None, :], ei[:, None, :]
        br, bi = br_ref[...], bi_ref[...]
        bbr_ref[...] = er3 * br - ei3 * bi
        bbi_ref[...] = er3 * bi + ei3 * br

    gp = jax.ShapeDtypeStruct(a_re.shape, F32)
    gb = jax.ShapeDtypeStruct(b_re_t.shape, F32)
    return _pallas_call(body, name="ssm_prep", out_shape=(gp, gp, gp, gp, gb, gb))(a_re, a_im, log_dt, b_re_t, b_im_t)


def _ssm_param_bwd(a_re, a_im, log_dt, b_re_t, b_im_t, abar_re, abar_im, e_re, e_im, ga_re, ga_im, gbb_re_t, gbb_im_t):
    def body(ar_ref, ai_ref, ldt_ref, br_ref, bi_ref, abr_ref, abi_ref, er_ref, ei_ref, gar_ref, gai_ref, gbr_ref, gbi_ref,
             o_ar, o_ai, o_ldt, o_br, o_bi):
        ar, ai = ar_ref[...], ai_ref[...]
        dt = jnp.exp(ldt_ref[...])
        er, ei = er_ref[...], ei_ref[...]
        br, bi, gbr, gbi = br_ref[...], bi_ref[...], gbr_ref[...], gbi_ref[...]
        er3, ei3 = er[:, None, :], ei[:, None, :]
        o_br[...] = er3 * gbr + ei3 * gbi
        o_bi[...] = er3 * gbi - ei3 * gbr
        ge_r = jnp.sum(br * gbr + bi * gbi, axis=1)
        ge_i = jnp.sum(br * gbi - bi * gbr, axis=1)
        den = ar * ar + ai * ai
        ilr, ili = ar / den, -ai / den
        t_r, t_i = _cmul(ilr, -ili, ge_r, ge_i)
        gab_r, gab_i = gar_ref[...] + t_r, gai_ref[...] + t_i
        gz_r, gz_i = _cmul(abr_ref[...], -abi_ref[...], gab_r, gab_i)
        el_r, el_i = _cmul(er, ei, ilr, ili)
        u_r, u_i = _cmul(el_r, -el_i, ge_r, ge_i)
        o_ar[...] = dt * gz_r - u_r
        o_ai[...] = dt * gz_i - u_i
        o_ldt[...] = jnp.sum(gz_r * ar + gz_i * ai, axis=1, keepdims=True) * dt

    gp = jax.ShapeDtypeStruct(a_re.shape, F32)
    gb = jax.ShapeDtypeStruct(b_re_t.shape, F32)
    return _pallas_call(body, name="ssm_param_bwd", out_shape=(gp, gp, jax.ShapeDtypeStruct(log_dt.shape, F32), gb, gb))(
        a_re, a_im, log_dt, b_re_t, b_im_t, abar_re, abar_im, e_re, e_im, ga_re, ga_im, gbb_re_t, gbb_im_t)


def _block_diag(blocks_re, blocks_im, sign_im, rows_are_channels):
    both = jnp.stack([blocks_re, sign_im * blocks_im]).reshape(2, SSM_CHUNKS, 8, SSM_GROUP, SSM_STATE)
    eye = jnp.eye(8, dtype=F32)
    if rows_are_channels:
        return jnp.einsum("rcghp,gk->cghrkp", both, eye).reshape(SSM_CHUNKS, 128, 2 * CHUNK_STATES)
    return jnp.einsum("rcghp,gk->crkpgh", both, eye).reshape(SSM_CHUNKS, 2 * CHUNK_STATES, 128)


def _block_diag_parts(mat, rows_are_channels):
    if rows_are_channels:
        six = mat.reshape(SSM_CHUNKS, 8, SSM_GROUP, 2, 8, SSM_STATE)
        parts = jnp.einsum("cghrgp->rcghp", six)
    else:
        six = mat.reshape(SSM_CHUNKS, 2, 8, SSM_STATE, 8, SSM_GROUP)
        parts = jnp.einsum("crgpgh->rcghp", six)
    parts = parts.reshape(2, SSM_GROUPS, SSM_GROUP, SSM_STATE)
    return parts[0], parts[1]


def _scan_consts(a_ref, conj, reverse):
    ar = jnp.broadcast_to(a_ref[:, :CHUNK_STATES], (SCAN_ROWS, CHUNK_STATES))
    ai = jnp.broadcast_to(a_ref[:, CHUNK_STATES:], (SCAN_ROWS, CHUNK_STATES))
    if conj:
        ai = -ai
    row = lax.broadcasted_iota(jnp.int32, (SCAN_ROWS, CHUNK_STATES), 0)
    if reverse:
        row = SCAN_ROWS - 1 - row
    zero = jnp.zeros_like(ar)
    steps = []
    pr, pi = ar, ai
    for shift in (1, 2, 4):
        keep = row >= shift
        steps.append((SCAN_ROWS - shift if reverse else shift, jnp.where(keep, pr, zero), jnp.where(keep, pi, zero)))
        pr, pi = _cmul(pr, pi, pr, pi)
    first = row == 0
    return steps, (jnp.where(first, ar, zero), jnp.where(first, ai, zero)), first


def _scan_tile(xr, xi, prev_r, prev_i, steps, carry_in, reverse):
    edge = SCAN_ROWS - 1 if reverse else 1
    cr, ci = pltpu.roll(prev_r, edge, axis=0), pltpu.roll(prev_i, edge, axis=0)
    xr, xi = xr + carry_in[0] * cr - carry_in[1] * ci, xi + carry_in[0] * ci + carry_in[1] * cr
    for shift, mr, mi in steps:
        sr, si = pltpu.roll(xr, shift, axis=0), pltpu.roll(xi, shift, axis=0)
        xr, xi = xr + mr * sr - mi * si, xi + mr * si + mi * sr
    return xr, xi


MM_ROWS = 256


def _ssm_fwd(proj, bmat, cmat, a_chunks, d_skip, ride=None):
    def body(u_ref, b_ref, c_ref, a_ref, d_ref, y_ref, h_ref):
        for i in range(SEQ // MM_ROWS):
            rows = pl.ds(i * MM_ROWS, MM_ROWS)
            h_ref[rows, :] = _dot(u_ref[rows, :].astype(BF16), b_ref[...])
        steps, carry_in, _ = _scan_consts(a_ref, conj=False, reverse=False)

        def tile(k, carry):
            rows = pl.ds(pl.multiple_of(k * SCAN_ROWS, SCAN_ROWS), SCAN_ROWS)
            xr, xi = _scan_tile(h_ref[rows, :CHUNK_STATES], h_ref[rows, CHUNK_STATES:], carry[0], carry[1], steps, carry_in, False)
            h_ref[rows, :CHUNK_STATES] = xr
            h_ref[rows, CHUNK_STATES:] = xi
            return xr, xi

        zero = jnp.zeros((SCAN_ROWS, CHUNK_STATES), F32)
        lax.fori_loop(0, SEQ // SCAN_ROWS, tile, (zero, zero))
        for i in range(SEQ // MM_ROWS):
            rows = pl.ds(i * MM_ROWS, MM_ROWS)
            y_ref[rows, :] = _dot(h_ref[rows, :].astype(BF16), c_ref[...]) + d_ref[...] * u_ref[rows, :]

    return _call(
        body, "ssm_fwd", (SSM_CHUNKS,),
        [pl.BlockSpec((SEQ, 128), lambda c: (0, U_COL + c)),
         pl.BlockSpec((None, 128, 2 * CHUNK_STATES), lambda c: (c, 0, 0)),
         pl.BlockSpec((None, 2 * CHUNK_STATES, 128), lambda c: (c, 0, 0)),
         pl.BlockSpec((None, 1, 2 * CHUNK_STATES), lambda c: (c, 0, 0)),
         pl.BlockSpec((1, 128), lambda c: (0, c))],
        [pl.BlockSpec((SEQ, 128), lambda c: (0, c)), pl.BlockSpec((SEQ, 2 * CHUNK_STATES), lambda c: (0, c))],
        [_sds((SEQ, SSM_WIDTH), F32), _sds((SEQ, SSM_CHUNKS * 2 * CHUNK_STATES), F32)], [],
        [proj, bmat, cmat, a_chunks, d_skip], ride)


def _ssm_bwd(dys, proj, h, bmat, cmat, a_chunks, d_skip, dproj, ride=None):
    def body(dy_ref, u_ref, h_ref, b_ref, c_ref, a_ref, d_ref, dproj_in, du_ref, db_ref, dc_ref, da_ref, dd_ref, g_ref):
        del dproj_in
        dsum = jnp.zeros((1, 128), F32)
        dcm = jnp.zeros((2 * CHUNK_STATES, 128), F32)
        for i in range(SEQ // MM_ROWS):
            rows = pl.ds(i * MM_ROWS, MM_ROWS)
            dy = dy_ref[rows, :]
            g_ref[rows, :] = _dot_nt(dy.astype(BF16), c_ref[...])
            dsum += jnp.sum(dy * u_ref[rows, :], axis=0, keepdims=True)
            dcm += _dot_tn(h_ref[rows, :].astype(BF16), dy.astype(BF16))
        dd_ref[...] = dsum
        dc_ref[...] = dcm
        steps, carry_in, _ = _scan_consts(a_ref, conj=True, reverse=True)
        first_row = lax.broadcasted_iota(jnp.int32, (SCAN_ROWS, CHUNK_STATES), 0) == 0
        n_tiles = SEQ // SCAN_ROWS

        def tile(j, carry):
            k = n_tiles - 1 - j
            rows = pl.ds(pl.multiple_of(k * SCAN_ROWS, SCAN_ROWS), SCAN_ROWS)
            before = pl.ds(pl.multiple_of(jnp.maximum(k - 1, 0) * SCAN_ROWS, SCAN_ROWS), SCAN_ROWS)
            gr, gi = _scan_tile(g_ref[rows, :CHUNK_STATES], g_ref[rows, CHUNK_STATES:], carry[0], carry[1], steps, carry_in, True)
            g_ref[rows, :CHUNK_STATES] = gr
            g_ref[rows, CHUNK_STATES:] = gi
            has_before = jnp.where(k > 0, 1.0, 0.0)
            hr = jnp.where(first_row, pltpu.roll(h_ref[before, :CHUNK_STATES], 1, axis=0) * has_before,
                           pltpu.roll(h_ref[rows, :CHUNK_STATES], 1, axis=0))
            hi = jnp.where(first_row, pltpu.roll(h_ref[before, CHUNK_STATES:], 1, axis=0) * has_before,
                           pltpu.roll(h_ref[rows, CHUNK_STATES:], 1, axis=0))
            return gr, gi, carry[2] + hr * gr + hi * gi, carry[3] + hr * gi - hi * gr

        zero = jnp.zeros((SCAN_ROWS, CHUNK_STATES), F32)
        _, _, sar, sai = lax.fori_loop(0, n_tiles, tile, (zero, zero, zero, zero))
        da_ref[:, :CHUNK_STATES] = jnp.sum(sar, axis=0, keepdims=True)
        da_ref[:, CHUNK_STATES:] = jnp.sum(sai, axis=0, keepdims=True)
        dbm = jnp.zeros((128, 2 * CHUNK_STATES), F32)
        for i in range(SEQ // MM_ROWS):
            rows = pl.ds(i * MM_ROWS, MM_ROWS)
            g = g_ref[rows, :].astype(BF16)
            du_ref[rows, :] = (_dot_nt(g, b_ref[...]) + d_ref[...] * dy_ref[rows, :]).astype(BF16)
            dbm += _dot_tn(u_ref[rows, :].astype(BF16), g)
        db_ref[...] = dbm

    chunk_col = pl.BlockSpec((SEQ, 128), lambda c: (0, c))
    return _call(
        body, "ssm_bwd", (SSM_CHUNKS,),
        [chunk_col,
         pl.BlockSpec((SEQ, 128), lambda c: (0, U_COL + c)),
         pl.BlockSpec((SEQ, 2 * CHUNK_STATES), lambda c: (0, c)),
         pl.BlockSpec((None, 128, 2 * CHUNK_STATES), lambda c: (c, 0, 0)),
         pl.BlockSpec((None, 2 * CHUNK_STATES, 128), lambda c: (c, 0, 0)),
         pl.BlockSpec((None, 1, 2 * CHUNK_STATES), lambda c: (c, 0, 0)),
         pl.BlockSpec((1, 128), lambda c: (0, c)), ANY],
        [pl.BlockSpec((SEQ, 128), lambda c: (0, U_COL + c)),
         pl.BlockSpec((None, 128, 2 * CHUNK_STATES), lambda c: (c, 0, 0)),
         pl.BlockSpec((None, 2 * CHUNK_STATES, 128), lambda c: (c, 0, 0)),
         pl.BlockSpec((None, 1, 2 * CHUNK_STATES), lambda c: (c, 0, 0)),
         pl.BlockSpec((1, 128), lambda c: (0, c))],
        [_sds((SEQ, IN_WIDTH), BF16), _sds((SSM_CHUNKS, 128, 2 * CHUNK_STATES), F32),
         _sds((SSM_CHUNKS, 2 * CHUNK_STATES, 128), F32), _sds((SSM_CHUNKS, 1, 2 * CHUNK_STATES), F32), _sds((1, SSM_WIDTH), F32)],
        [pltpu.VMEM((SEQ, 2 * CHUNK_STATES), F32)], [dys, proj, h, bmat, cmat, a_chunks, d_skip, dproj], ride, aliases={7: 0})


def _ssm_tables(abar_re, abar_im, bbar_re_t, bbar_im_t, c_re, c_im):
    bmat = _block_diag(bbar_re_t, bbar_im_t, 1.0, True).astype(BF16)
    cmat = _block_diag(c_re, c_im, -1.0, False).astype(BF16)
    a_chunks = jnp.concatenate([abar_re.reshape(SSM_CHUNKS, 1, CHUNK_STATES), abar_im.reshape(SSM_CHUNKS, 1, CHUNK_STATES)], axis=2)
    return bmat, cmat, a_chunks


GL_COL = (3 * QKV_WIDTH + SSM_WIDTH) // D_MODEL
GELU_C = math.sqrt(2.0 / math.pi)
GELU_A = 0.044715


def _sds(shape, dtype):
    return jax.ShapeDtypeStruct(shape, dtype)


def _gelu(x):
    t = jnp.tanh(GELU_C * (x + GELU_A * x * x * x))
    return 0.5 * x * (1.0 + t), t


def _gelu_grad(x, t):
    return 0.5 * (1.0 + t) + 0.5 * x * (1.0 - t * t) * GELU_C * (1.0 + 3.0 * GELU_A * x * x)


def _layer_norm(r, g, b):
    mu = jnp.mean(r, axis=-1, keepdims=True)
    xc = r - mu
    rstd = lax.rsqrt(jnp.mean(xc * xc, axis=-1, keepdims=True) + LN_EPS)
    xhat = xc * rstd
    return xhat * g + b, xhat, rstd


def _layer_norm_bwd(dy, xhat, rstd, g):
    dxhat = dy * g
    m1 = jnp.mean(dxhat, axis=-1, keepdims=True)
    m2 = jnp.mean(dxhat * xhat, axis=-1, keepdims=True)
    return rstd * (dxhat - m1 - xhat * m2)


def _proj(x, w_in, ride=None):
    tm, tn = 1024, 1792

    def body(x_ref, w_ref, o_ref):
        o_ref[...] = _dot(x_ref[...].astype(BF16), _side_by_side(w_ref))

    return _call(
        body, "proj", (SEQ // tm, IN_WIDTH // tn),
        [pl.BlockSpec((tm, D_MODEL), lambda i, j: (i, 0)), pl.BlockSpec((2, D_MODEL, tn // 2), lambda i, j: (j, 0, 0))],
        [pl.BlockSpec((tm, tn), lambda i, j: (i, j))], [_sds((SEQ, IN_WIDTH), F32)], [], [x, w_in], ride)


def _row_spec(tm, width, col=0):
    return pl.BlockSpec((tm, width), lambda i, col=col: (i, col))


def _full_spec(shape):
    return pl.BlockSpec(shape, lambda i: (0,) * len(shape))


def _mixer_out(attn, ys, proj, x, w_ab, w_sb, w_glu, w_out, b_gate, ln_g, ln_b):
    tm = 256

    def body(attn_ref, ys_ref, gl0_ref, gl1_ref, x_ref, wab_ref, wsb_ref, wglu_ref, wout_ref, bg_ref, g_ref, b_ref,
             h_ref, xhat_ref, rstd_ref, glu_ref, ya_ref, yssm_ref):
        gy, _ = _gelu(ys_ref[...])
        glu = _dot(gy.astype(BF16), _side_by_side(wglu_ref))
        glu_ref[...] = glu
        y_s = glu[:, :SSM_WIDTH] * jax.nn.sigmoid(glu[:, SSM_WIDTH:])
        y_ssm = _dot(y_s.astype(BF16), _side_by_side(wsb_ref))
        y_attn = _dot(attn_ref[...].astype(BF16), _side_by_side(wab_ref))
        ya_ref[...] = y_attn
        yssm_ref[...] = y_ssm
        g0 = jax.nn.sigmoid(gl0_ref[...] + _side_by_side(bg_ref, 0))
        g1 = jax.nn.sigmoid(gl1_ref[...] + _side_by_side(bg_ref, 1))
        mixed = g0 * y_attn + g1 * y_ssm
        r1 = DN_ALPHA * x_ref[...] + _dot(mixed.astype(BF16), wout_ref[...])
        h, xhat, rstd = _layer_norm(r1, g_ref[...], b_ref[...])
        h_ref[...] = h
        xhat_ref[...] = xhat
        rstd_ref[...] = jnp.broadcast_to(rstd, (tm, 128))

    wide = _sds((SEQ, D_MODEL), F32)
    return _pallas_call(
        body, name="mixer_out", grid=(SEQ // tm,),
        in_specs=[_row_spec(tm, ATTN_WIDTH), _row_spec(tm, SSM_WIDTH), _row_spec(tm, D_MODEL, GL_COL), _row_spec(tm, D_MODEL, GL_COL + 1),
                  _row_spec(tm, D_MODEL), _full_spec((N_DEV, ATTN_WIDTH, 128)), _full_spec((N_DEV, SSM_WIDTH, 128)),
                  _full_spec((N_DEV, SSM_WIDTH, 128)), _full_spec((D_MODEL, D_MODEL)), _full_spec((N_DEV, 2, 128)),
                  _full_spec((1, D_MODEL)), _full_spec((1, D_MODEL))],
        out_specs=(_row_spec(tm, D_MODEL), _row_spec(tm, D_MODEL), _row_spec(tm, 128), _row_spec(tm, D_MODEL),
                   _row_spec(tm, D_MODEL), _row_spec(tm, D_MODEL)),
        out_shape=(wide, wide, _sds((SEQ, 128), F32), wide, wide, wide),
        compiler_params=_cparams(dimension_semantics=("arbitrary",)),
    )(attn, ys, proj, proj, x, w_ab, w_sb, w_glu, w_out, b_gate, ln_g, ln_b)


def _ff_up(h, w_gate, w_up):
    tm, tn = 1024, 768

    def body(h_ref, wg_ref, wu_ref, a_ref, b_ref, f_ref):
        hb = h_ref[...].astype(BF16)
        a, b = _dot(hb, _side_by_side(wg_ref)), _dot(hb, _side_by_side(wu_ref))
        a_ref[...] = a.astype(BF16)
        b_ref[...] = b.astype(BF16)
        f_ref[...] = (a * jax.nn.sigmoid(a) * b).astype(BF16)

    tile = pl.BlockSpec((tm, tn), lambda i, j: (i, j))
    wtile = pl.BlockSpec((tn // FF_PAD, D_MODEL, FF_PAD), lambda i, j: (j, 0, 0))
    out = _sds((SEQ, D_FF_PAD), BF16)
    return _pallas_call(
        body, name="ff_up", grid=(SEQ // tm, D_FF_PAD // tn),
        in_specs=[pl.BlockSpec((tm, D_MODEL), lambda i, j: (i, 0)), wtile, wtile],
        out_specs=(tile, tile, tile), out_shape=(out, out, out),
        compiler_params=_cparams(dimension_semantics=("arbitrary", "arbitrary")),
    )(h, w_gate, w_up)


def _ff_down_loss(f, w_down, h, target, ln_g, ln_b):
    tm = 256

    def body(f_ref, w_ref, h_ref, t_ref, g_ref, b_ref, dr_ref, dg_ref, db_ref, loss_ref):
        @pl.when(pl.program_id(0) == 0)
        def _():
            dg_ref[...] = jnp.zeros_like(dg_ref)
            db_ref[...] = jnp.zeros_like(db_ref)
            loss_ref[...] = jnp.zeros_like(loss_ref)

        r2 = DN_ALPHA * h_ref[...] + _dot(f_ref[...], w_ref[...])
        g = g_ref[...]
        out, xhat, rstd = _layer_norm(r2, g, b_ref[...])
        err = out - t_ref[...]
        loss_ref[...] += 0.5 * jnp.sum(jnp.mean(err * err, axis=-1, keepdims=True), axis=0, keepdims=True)
        dout = err * (1.0 / D_MODEL)
        dg_ref[...] += jnp.sum(dout * xhat, axis=0, keepdims=True)
        db_ref[...] += jnp.sum(dout, axis=0, keepdims=True)
        dr_ref[...] = _layer_norm_bwd(dout, xhat, rstd, g)

    vec = _sds((1, D_MODEL), F32)
    return _pallas_call(
        body, name="ff_down_loss", grid=(SEQ // tm,),
        in_specs=[_row_spec(tm, D_FF_PAD), _full_spec((D_FF_PAD, D_MODEL)), _row_spec(tm, D_MODEL), _row_spec(tm, D_MODEL),
                  _full_spec((1, D_MODEL)), _full_spec((1, D_MODEL))],
        out_specs=(_row_spec(tm, D_MODEL), _full_spec((1, D_MODEL)), _full_spec((1, D_MODEL)), _full_spec((1, 128))),
        out_shape=(_sds((SEQ, D_MODEL), F32), vec, vec, _sds((1, 128), F32)),
        compiler_params=_cparams(dimension_semantics=("arbitrary",)),
    )(f, w_down, h, target, ln_g, ln_b)


def _ff_down_bwd(dr2, w_down, a, b):
    tm, tn = 1024, 768

    def body(dr_ref, w_ref, a_ref, b_ref, da_ref, db_ref):
        df = _dot_nt(dr_ref[...].astype(BF16), w_ref[...])
        av, bv = a_ref[...].astype(F32), b_ref[...].astype(F32)
        sg = jax.nn.sigmoid(av)
        da_ref[...] = (df * bv * sg * (1.0 + av * (1.0 - sg))).astype(BF16)
        db_ref[...] = (df * av * sg).astype(BF16)

    tile = pl.BlockSpec((tm, tn), lambda i, j: (i, j))
    out = _sds((SEQ, D_FF_PAD), BF16)
    return _pallas_call(
        body, name="ff_down_bwd", grid=(SEQ // tm, D_FF_PAD // tn),
        in_specs=[pl.BlockSpec((tm, D_MODEL), lambda i, j: (i, 0)), pl.BlockSpec((tn, D_MODEL), lambda i, j: (j, 0)), tile, tile],
        out_specs=(tile, tile), out_shape=(out, out),
        compiler_params=_cparams(dimension_semantics=("arbitrary", "arbitrary")),
    )(dr2, w_down, a, b)


def _ff_up_bwd(da, db, w_gate, w_up, dr2, xhat1, rstd1, ln_g, ride=None):
    tm, tk = 1024, 768
    nk = D_FF_PAD // tk

    def body(da_ref, db_ref, wg_ref, wu_ref, dr2_ref, xhat_ref, rstd_ref, g_ref, dr1_ref, dg_ref, dbias_ref, acc):
        i, k = pl.program_id(0), pl.program_id(1)

        @pl.when(jnp.logical_and(i == 0, k == 0))
        def _():
            dg_ref[...] = jnp.zeros_like(dg_ref)
            dbias_ref[...] = jnp.zeros_like(dbias_ref)

        part = _dot_nt(da_ref[...], _side_by_side(wg_ref)) + _dot_nt(db_ref[...], _side_by_side(wu_ref))

        @pl.when(k == 0)
        def _():
            acc[...] = part

        @pl.when(k > 0)
        def _():
            acc[...] += part

        @pl.when(k == nk - 1)
        def _():
            dh = DN_ALPHA * dr2_ref[...] + acc[...]
            xhat = xhat_ref[...]
            dg_ref[...] += jnp.sum(dh * xhat, axis=0, keepdims=True)
            dbias_ref[...] += jnp.sum(dh, axis=0, keepdims=True)
            rstd = jnp.max(rstd_ref[...], axis=1, keepdims=True)
            dr1_ref[...] = _layer_norm_bwd(dh, xhat, rstd, g_ref[...])

    hid = pl.BlockSpec((tm, tk), lambda i, k: (i, k))
    wtile = pl.BlockSpec((tk // FF_PAD, D_MODEL, FF_PAD), lambda i, k: (k, 0, 0))
    row = pl.BlockSpec((tm, D_MODEL), lambda i, k: (i, 0))
    vec = pl.BlockSpec((1, D_MODEL), lambda i, k: (0, 0))
    return _call(
        body, "ff_up_bwd", (SEQ // tm, nk),
        [hid, hid, wtile, wtile, row, row, pl.BlockSpec((tm, 128), lambda i, k: (i, 0)), vec],
        [row, vec, vec], [_sds((SEQ, D_MODEL), F32), _sds((1, D_MODEL), F32), _sds((1, D_MODEL), F32)],
        [pltpu.VMEM((tm, D_MODEL), F32)], [da, db, w_gate, w_up, dr2, xhat1, rstd1, ln_g], ride)


def _mixer_bwd(dr1, proj, y_attn, y_ssm, glu, ys, w_ab, w_sb, w_glu, w_out, b_gate):
    tm = 256

    def body(dr1_ref, gl0_ref, gl1_ref, ya_ref, yssm_ref, glu_ref, ys_ref, wab_ref, wsb_ref, wglu_ref, wout_ref, bg_ref,
             dya_ref, dyssm_ref, dgl_ref, dattn_ref, dglu_ref, dys_ref, mixed_ref, ysb_ref, gy_ref, dbg_ref):
        @pl.when(pl.program_id(0) == 0)
        def _():
            dbg_ref[...] = jnp.zeros_like(dbg_ref)

        dmixed = _dot_nt(dr1_ref[...].astype(BF16), wout_ref[...])
        g0 = jax.nn.sigmoid(gl0_ref[...] + _side_by_side(bg_ref, 0))
        g1 = jax.nn.sigmoid(gl1_ref[...] + _side_by_side(bg_ref, 1))
        y_attn, y_ssm = ya_ref[...], yssm_ref[...]
        mixed_ref[...] = (g0 * y_attn + g1 * y_ssm).astype(BF16)
        dya = (dmixed * g0).astype(BF16)
        dyssm = (dmixed * g1).astype(BF16)
        dya_ref[...] = dya
        dyssm_ref[...] = dyssm
        dgl0 = dmixed * y_attn * g0 * (1.0 - g0)
        dgl1 = dmixed * y_ssm * g1 * (1.0 - g1)
        dgl_ref[:, :GL_COL * D_MODEL] = jnp.zeros((tm, GL_COL * D_MODEL), BF16)
        dgl_ref[:, GL_COL * D_MODEL:(GL_COL + 1) * D_MODEL] = dgl0.astype(BF16)
        dgl_ref[:, (GL_COL + 1) * D_MODEL:] = dgl1.astype(BF16)
        dbg_ref[:, :D_MODEL] += jnp.sum(dgl0, axis=0, keepdims=True)
        dbg_ref[:, D_MODEL:] += jnp.sum(dgl1, axis=0, keepdims=True)
        dattn_ref[...] = _dot_nt(dya, _side_by_side(wab_ref))
        dy_s = _dot_nt(dyssm, _side_by_side(wsb_ref))
        glu = glu_ref[...]
        glu1, sg = glu[:, :SSM_WIDTH], jax.nn.sigmoid(glu[:, SSM_WIDTH:])
        ysb_ref[...] = (glu1 * sg).astype(BF16)
        dglu1 = (dy_s * sg).astype(BF16)
        dglu2 = (dy_s * glu1 * sg * (1.0 - sg)).astype(BF16)
        dglu_ref[:, :SSM_WIDTH] = dglu1
        dglu_ref[:, SSM_WIDTH:] = dglu2
        dgy = _dot_nt(jnp.concatenate([dglu1, dglu2], axis=1), _side_by_side(wglu_ref))
        ys = ys_ref[...]
        gy, t = _gelu(ys)
        gy_ref[...] = gy.astype(BF16)
        dys_ref[...] = dgy * _gelu_grad(ys, t)

    wide_b, half_b = _sds((SEQ, D_MODEL), BF16), _sds((SEQ, SSM_WIDTH), BF16)
    half_f = _sds((SEQ, SSM_WIDTH), F32)
    return _pallas_call(
        body, name="mixer_bwd", grid=(SEQ // tm,),
        in_specs=[_row_spec(tm, D_MODEL), _row_spec(tm, D_MODEL, GL_COL), _row_spec(tm, D_MODEL, GL_COL + 1), _row_spec(tm, D_MODEL),
                  _row_spec(tm, D_MODEL), _row_spec(tm, D_MODEL), _row_spec(tm, SSM_WIDTH), _full_spec((N_DEV, ATTN_WIDTH, 128)),
                  _full_spec((N_DEV, SSM_WIDTH, 128)), _full_spec((N_DEV, SSM_WIDTH, 128)), _full_spec((D_MODEL, D_MODEL)),
                  _full_spec((N_DEV, 2, 128))],
        out_specs=(_row_spec(tm, D_MODEL), _row_spec(tm, D_MODEL), _row_spec(tm, IN_WIDTH), _row_spec(tm, ATTN_WIDTH),
                   _row_spec(tm, D_MODEL), _row_spec(tm, SSM_WIDTH), _row_spec(tm, D_MODEL), _row_spec(tm, SSM_WIDTH),
                   _row_spec(tm, SSM_WIDTH), _full_spec((1, 2 * D_MODEL))),
        out_shape=(wide_b, wide_b, _sds((SEQ, IN_WIDTH), BF16), half_f, wide_b, half_f, wide_b, half_b, half_b,
                   _sds((1, 2 * D_MODEL), F32)),
        compiler_params=_cparams(dimension_semantics=("arbitrary",)),
    )(dr1, proj, proj, y_attn, y_ssm, glu, ys, w_ab, w_sb, w_glu, w_out, b_gate)


def _grad_x(dproj, w_in, dr1, ride=None):
    tm, tk = 1024, 1792
    nk = IN_WIDTH // tk

    def body(dp_ref, w_ref, dr1_ref, o_ref, acc):
        k = pl.program_id(1)
        part = _dot_nt(dp_ref[...], _side_by_side(w_ref))

        @pl.when(k == 0)
        def _():
            acc[...] = part

        @pl.when(k > 0)
        def _():
            acc[...] += part

        @pl.when(k == nk - 1)
        def _():
            o_ref[...] = DN_ALPHA * dr1_ref[...] + acc[...]

    row = pl.BlockSpec((tm, D_MODEL), lambda i, k: (i, 0))
    return _call(
        body, "grad_x", (SEQ // tm, nk),
        [pl.BlockSpec((tm, tk), lambda i, k: (i, k)), pl.BlockSpec((2, D_MODEL, tk // 2), lambda i, k: (k, 0, 0)), row],
        [row], [_sds((SEQ, D_MODEL), F32)], [pltpu.VMEM((tm, D_MODEL), F32)], [dproj, w_in, dr1], ride)


def _weight_grad(a, b, name, shard_cols=None, ride=None):
    k, n = a.shape[1], b.shape[1]
    tk = min(k, 512) if shard_cols else k // N_DEV
    tn = n // 4 if shard_cols else min(n, 1024)

    def body(a_ref, b_ref, o_ref):
        grad = _dot_tn(a_ref[...].astype(BF16), b_ref[...].astype(BF16))
        if shard_cols:
            o_ref[0] = grad[:, :shard_cols].astype(BF16)
            o_ref[1] = grad[:, shard_cols:].astype(BF16)
        else:
            o_ref[...] = grad.astype(BF16)

    if shard_cols:
        out_spec = pl.BlockSpec((2, None, tk, shard_cols), lambda kk, j: (0, j, kk, 0))
        out_shape = _sds((2, 4, k, shard_cols), BF16)
    else:
        out_spec = pl.BlockSpec((None, None, tk, tn), lambda kk, j: (kk % 2, kk // 2, 0, j))
        out_shape = _sds((2, 4, tk, n), BF16)
    out = _call(body, name, (k // tk, n // tn),
                [pl.BlockSpec((SEQ, tk), lambda kk, j: (0, kk)), pl.BlockSpec((SEQ, tn), lambda kk, j: (0, j))],
                [out_spec], [out_shape], [], [a, b], ride)
    return out[0] if ride is None else out


MESH = pl.DeviceIdType.MESH
ANY = pl.BlockSpec(memory_space=pl.ANY)


def _place():
    return lax.axis_index("x"), lax.axis_index("y"), lax.axis_index("c")


def _other_chips(x, y):
    return [(1 - x, y), (x, 1 - y), (1 - x, 1 - y)]


class _Ride:
    def __init__(self, operands, results, aliases, sems, start, wait):
        self.operands, self.results, self.aliases, self.sems = list(operands), list(results), dict(aliases), list(sems)
        self.start, self.wait = start, wait

    def __add__(self, other):
        n_in, n_out, n_sem = len(self.operands), len(self.results), len(self.sems)

        def both(which):
            def run(ins, outs, sems):
                getattr(self, which)(ins[:n_in], outs[:n_out], sems[:n_sem])
                getattr(other, which)(ins[n_in:], outs[n_out:], sems[n_sem:])
            return run

        aliases = {**self.aliases, **{n_in + i: n_out + j for i, j in other.aliases.items()}}
        return _Ride(self.operands + other.operands, self.results + other.results, aliases, self.sems + other.sems,
                     both("start"), both("wait"))


def _call(body, name, grid, in_specs, out_specs, out_shape, scratch_shapes, operands, ride=None, aliases=None):
    in_specs, out_specs, out_shape = list(in_specs), list(out_specs), list(out_shape)
    scratch_shapes, operands, aliases = list(scratch_shapes), list(operands), dict(aliases or {})
    kernel_body = body
    if ride is not None:
        n_in, n_out, n_scr, r_in, r_out = len(in_specs), len(out_specs), len(scratch_shapes), len(ride.operands), len(ride.results)

        def kernel_body(*refs):
            out0, scr0 = n_in + r_in, n_in + r_in + n_out + r_out
            ride_refs = (refs[n_in:out0], refs[out0 + n_out:scr0], refs[scr0 + n_scr:])
            ids = [pl.program_id(i) for i in range(len(grid))]
            first = functools.reduce(jnp.logical_and, [i == 0 for i in ids])
            last = functools.reduce(jnp.logical_and, [i == g - 1 for i, g in zip(ids, grid)])

            @pl.when(first)
            def _():
                ride.start(*ride_refs)

            body(*refs[:n_in], *refs[out0:out0 + n_out], *refs[scr0:scr0 + n_scr])

            @pl.when(last)
            def _():
                ride.wait(*ride_refs)

        aliases.update({n_in + i: n_out + j for i, j in ride.aliases.items()})
        in_specs += [ANY] * r_in
        out_specs += [ANY] * r_out
        out_shape += ride.results
        scratch_shapes += ride.sems
        operands += ride.operands
    return _pallas_call(
        kernel_body, name=name, grid=grid, in_specs=in_specs, out_specs=out_specs, out_shape=out_shape,
        scratch_shapes=scratch_shapes, input_output_aliases=aliases,
        compiler_params=_cparams(dimension_semantics=("arbitrary",) * len(grid)),
    )(*operands)


def _gather_first_level(shards):
    n = len(shards)

    def copies(ins, outs, sems, landed):
        send_sems, recv_sems, local_sems = sems
        x, y, c = _place()
        peers = [(x, y, 1 - c)] + [(px, py, c) for px, py in _other_chips(x, y)]

        def row(peer):
            return 4 * x + 2 * y + c if not landed else 4 * peer[0] + 2 * peer[1] + peer[2]

        local = [pltpu.make_async_copy(ins[a], outs[a].at[4 * x + 2 * y + c], local_sems.at[a]) for a in range(n)]
        remote = [pltpu.make_async_remote_copy(
            src_ref=ins[a], dst_ref=outs[a].at[row(peer)], send_sem=send_sems.at[a, k], recv_sem=recv_sems.at[a, k],
            device_id=peer, device_id_type=MESH) for a in range(n) for k, peer in enumerate(peers)]
        return local, remote

    def start(ins, outs, sems):
        local, remote = copies(ins, outs, sems, False)
        for cp in local + remote:
            cp.start()

    def wait(ins, outs, sems):
        local, sent = copies(ins, outs, sems, False)
        for cp in copies(ins, outs, sems, True)[1]:
            cp.wait_recv()
        for cp in sent:
            cp.wait_send()
        for cp in local:
            cp.wait()

    return _Ride(shards, [_sds((N_DEV,) + s.shape, s.dtype) for s in shards], {},
                 [pltpu.SemaphoreType.DMA((n, 4)), pltpu.SemaphoreType.DMA((n, 4)), pltpu.SemaphoreType.DMA((n,))], start, wait)


def _gather_second_level(buffers):
    n = len(buffers)

    def copies(outs, sems, core):
        send_sems, recv_sems = sems
        x, y, c = _place()
        return [pltpu.make_async_remote_copy(
            src_ref=outs[a].at[4 * px + 2 * py + core], dst_ref=outs[a].at[4 * px + 2 * py + core], send_sem=send_sems.at[a, j],
            recv_sem=recv_sems.at[a, j], device_id=(x, y, 1 - c), device_id_type=MESH)
            for a in range(n) for j, (px, py) in enumerate(_other_chips(x, y))]

    def start(ins, outs, sems):
        for cp in copies(outs, sems, lax.axis_index("c")):
            cp.start()

    def wait(ins, outs, sems):
        for cp in copies(outs, sems, 1 - lax.axis_index("c")):
            cp.wait_recv()
        for cp in copies(outs, sems, lax.axis_index("c")):
            cp.wait_send()

    return _Ride(buffers, [_sds(b.shape, b.dtype) for b in buffers], {i: i for i in range(n)},
                 [pltpu.SemaphoreType.DMA((n, 3)), pltpu.SemaphoreType.DMA((n, 3))], start, wait)


def _sibling_swap_ride(grads):
    n = len(grads)

    def copies(ins, outs, sems):
        x, y, c = _place()
        return [pltpu.make_async_remote_copy(
            src_ref=ins[a].at[1 - c], dst_ref=outs[a], send_sem=sems[0].at[a], recv_sem=sems[1].at[a],
            device_id=(x, y, 1 - c), device_id_type=MESH) for a in range(n)]

    def start(ins, outs, sems):
        for cp in copies(ins, outs, sems):
            cp.start()

    def wait(ins, outs, sems):
        for cp in copies(ins, outs, sems):
            cp.wait()

    return _Ride(grads, [_sds(g.shape[1:], g.dtype) for g in grads], {},
                 [pltpu.SemaphoreType.DMA((n,)), pltpu.SemaphoreType.DMA((n,))], start, wait)


def _chip_swap_ride(sums):
    n = len(sums)

    def copies(ins, outs, sems, landed):
        send_sems, recv_sems, local_sems = sems
        x, y, c = _place()
        mine = 2 * x + y
        local = [pltpu.make_async_copy(ins[a].at[mine], outs[a].at[mine], local_sems.at[a]) for a in range(n)]
        remote = [pltpu.make_async_remote_copy(
            src_ref=ins[a].at[2 * px + py], dst_ref=outs[a].at[2 * px + py if landed else mine], send_sem=send_sems.at[a, j],
            recv_sem=recv_sems.at[a, j], device_id=(px, py, c), device_id_type=MESH)
            for a in range(n) for j, (px, py) in enumerate(_other_chips(x, y))]
        return local, remote

    def start(ins, outs, sems):
        local, remote = copies(ins, outs, sems, False)
        for cp in local + remote:
            cp.start()

    def wait(ins, outs, sems):
        local, sent = copies(ins, outs, sems, False)
        for cp in copies(ins, outs, sems, True)[1]:
            cp.wait_recv()
        for cp in sent:
            cp.wait_send()
        for cp in local:
            cp.wait()

    return _Ride(sums, [_sds(s.shape, s.dtype) for s in sums], {},
                 [pltpu.SemaphoreType.DMA((n, 3)), pltpu.SemaphoreType.DMA((n, 3)), pltpu.SemaphoreType.DMA((n,))], start, wait)


def _send_buffer(w, rows, cols, name):
    r, c = w.shape

    def body(w_ref, o_ref):
        if (r, c) != (rows, cols):
            o_ref[...] = jnp.zeros((rows, cols), BF16)
        o_ref[:r, :c] = w_ref[...].astype(BF16)

    return _pallas_call(body, name=name, out_shape=_sds((rows, cols), BF16))(w)


def _all_gather(shards, name):
    n = len(shards)

    def body(*refs):
        ins, outs = refs[:n], refs[n:2 * n]
        send_sems, recv_sems, local_sems = refs[2 * n:]
        x, y, c = _place()
        me, sibling = (x, y, c), (x, y, 1 - c)
        chips = _other_chips(x, y)

        def slot(a, px, py, pc):
            return outs[a].at[4 * px + 2 * py + pc]

        def copy(a, k, block, to, src=None):
            return pltpu.make_async_remote_copy(
                src_ref=slot(a, *block) if src is None else src, dst_ref=slot(a, *block),
                send_sem=send_sems.at[a, k], recv_sem=recv_sems.at[a, k], device_id=to, device_id_type=MESH)

        mine = [pltpu.make_async_copy(ins[a], slot(a, *me), local_sems.at[a]) for a in range(n)]
        for cp in mine:
            cp.start()
        first = []
        for a in range(n):
            first.append(copy(a, 0, me, sibling, src=ins[a]))
            first += [copy(a, 1 + j, me, (*chip, c), src=ins[a]) for j, chip in enumerate(chips)]
        for cp in first:
            cp.start()
        passed = []
        for j, chip in enumerate(chips):
            for a in range(n):
                copy(a, 1 + j, (*chip, c), me).wait_recv()
                onward = copy(a, 4 + j, (*chip, c), sibling)
                onward.start()
                passed.append(onward)
        for a in range(n):
            copy(a, 0, sibling, me).wait_recv()
            for j, chip in enumerate(chips):
                copy(a, 4 + j, (*chip, 1 - c), me).wait_recv()
        for cp in first + passed:
            cp.wait_send()
        for cp in mine:
            cp.wait()

    return _pallas_call(
        body, name=name, in_specs=[ANY] * n, out_specs=[ANY] * n,
        out_shape=[_sds((N_DEV,) + s.shape, s.dtype) for s in shards],
        scratch_shapes=[pltpu.SemaphoreType.DMA((n, 7)), pltpu.SemaphoreType.DMA((n, 7)), pltpu.SemaphoreType.DMA((n,))],
    )(*shards)


def _swap_with_sibling(grads, name):
    n = len(grads)

    def body(*refs):
        ins, outs = refs[:n], refs[n:2 * n]
        send_sems, recv_sems = refs[2 * n:]
        x, y, c = _place()
        copies = [pltpu.make_async_remote_copy(
            src_ref=ins[a].at[1 - c], dst_ref=outs[a], send_sem=send_sems.at[a], recv_sem=recv_sems.at[a],
            device_id=(x, y, 1 - c), device_id_type=MESH) for a in range(n)]
        for cp in copies:
            cp.start()
        for cp in copies:
            cp.wait()

    return _pallas_call(
        body, name=name, in_specs=[ANY] * n, out_specs=[ANY] * n,
        out_shape=[_sds(g.shape[1:], g.dtype) for g in grads],
        scratch_shapes=[pltpu.SemaphoreType.DMA((n,)), pltpu.SemaphoreType.DMA((n,))],
    )(*grads)


def _pair_sums(gs, rs, core, name):
    n_arrays = len(gs)

    def body(core_ref, *refs):
        for g_ref, r_ref, o_ref in zip(refs[:n_arrays], refs[n_arrays:2 * n_arrays], refs[2 * n_arrays:]):
            o_ref[...] = (g_ref[...].astype(F32) + r_ref[...].astype(F32)).astype(o_ref.dtype)

    def own(g):
        return pl.BlockSpec((None, None) + g.shape[2:], lambda p, core_ref: (core_ref[0], p, 0, 0))

    def chip(g):
        return pl.BlockSpec((None,) + g.shape[2:], lambda p, core_ref: (p, 0, 0))

    return _pallas_call(
        body, name=name,
        grid_spec=pltpu.PrefetchScalarGridSpec(
            num_scalar_prefetch=1, grid=(4,), in_specs=[own(g) for g in gs] + [chip(g) for g in gs],
            out_specs=[chip(g) for g in gs]),
        out_shape=[_sds(g.shape[1:], g.dtype) for g in gs], compiler_params=_cparams(dimension_semantics=("arbitrary",)),
    )(core, *gs, *rs)


def _adamw_math(w, g, m, v):
    m = ADAM_B1 * m + (1.0 - ADAM_B1) * g
    v = ADAM_B2 * v + (1.0 - ADAM_B2) * (g * g)
    m_hat = m / (1.0 - ADAM_B1 ** ADAM_STEP)
    v_hat = v / (1.0 - ADAM_B2 ** ADAM_STEP)
    return -ADAM_LR * (m_hat / (jnp.sqrt(v_hat) + ADAM_EPS) + ADAM_WD * w), m, v


def _adamw(w, m, v, parts, name):
    r, c = w.shape
    tr = r if r <= 512 else 256
    n_parts, pr, pc = parts.shape
    assert r % tr == 0 and (tr == r or pr == r)

    def body(w_ref, m_ref, v_ref, p_ref, g_out, d_out, m_out, v_out):
        g = p_ref[0, :tr, :c].astype(F32)
        for p in range(1, n_parts):
            g = g + p_ref[p, :tr, :c].astype(F32)
        g_out[...] = g
        d_out[...], m_out[...], v_out[...] = _adamw_math(w_ref[...], g, m_ref[...], v_ref[...])

    tile = pl.BlockSpec((tr, c), lambda i: (i, 0))
    part_tile = pl.BlockSpec((n_parts, pr if tr == r else tr, pc), lambda i: (0, i, 0))
    out = _sds((r, c), F32)
    return _pallas_call(
        body, name=name, grid=(r // tr,), in_specs=[tile, tile, tile, part_tile], out_specs=(tile,) * 4,
        out_shape=(out,) * 4, compiler_params=_cparams(dimension_semantics=("arbitrary",)),
    )(w, m, v, parts)


SMALL = ("ssm_a_re", "ssm_a_im", "ssm_log_dt", "ssm_b_re", "ssm_b_im", "ssm_c_re", "ssm_c_im", "ssm_d",
         "ln1_g", "ln1_b", "ln2_g", "ln2_b")


def _pack_rows(arrays):
    rows = []
    for a in arrays:
        flat = a.reshape(-1)
        rows.append(jnp.pad(flat, (0, -flat.shape[0] % 128)).reshape(-1, 128))
    packed = jnp.concatenate(rows, axis=0)
    return jnp.pad(packed, ((0, -packed.shape[0] % 8), (0, 0)))


def _unpack_rows(packed, shapes):
    out, row = [], 0
    for shape in shapes:
        size = math.prod(shape)
        n_rows = -(-size // 128)
        out.append(packed[row:row + n_rows].reshape(-1)[:size].reshape(shape))
        row += n_rows
    return out


def _sum_devices(parts):
    def body(p_ref, o_ref):
        total = p_ref[0]
        for dev in range(1, N_DEV):
            total = total + p_ref[dev]
        o_ref[...] = total

    return _pallas_call(body, name="sum_devices", out_shape=_sds(parts.shape[1:], F32))(parts)


def _adamw_replicated(ws, ms, vs, gs):
    n = len(ws)

    def body(*refs):
        w_refs, m_refs, v_refs, g_refs, d_out, m_out, v_out = (refs[i * n:(i + 1) * n] for i in range(7))
        for i in range(n):
            d_out[i][...], m_out[i][...], v_out[i][...] = _adamw_math(w_refs[i][...], g_refs[i][...], m_refs[i][...], v_refs[i][...])

    out = _pallas_call(body, name="adamw_replicated", out_shape=[_sds(w.shape, F32) for w in ws] * 3,
                       compiler_params=_cparams())(*ws, *ms, *vs, *gs)
    return out[:n], out[n:2 * n], out[2 * n:]


def kernel(x, w_in, b_gate, w_attn_br, w_ssm_br, w_out, ssm_a_re, ssm_a_im, ssm_log_dt, ssm_b_re, ssm_b_im, ssm_c_re, ssm_c_im, ssm_d, w_glu, ln1_g, ln1_b, w_ff_gate, w_ff_up, w_ff_down, ln2_g, ln2_b, loss_target, m_w_in, m_b_gate, m_w_attn_br, m_w_ssm_br, m_w_out, m_ssm_a_re, m_ssm_a_im, m_ssm_log_dt, m_ssm_b_re, m_ssm_b_im, m_ssm_c_re, m_ssm_c_im, m_ssm_d, m_w_glu, m_ln1_g, m_ln1_b, m_w_ff_gate, m_w_ff_up, m_w_ff_down, m_ln2_g, m_ln2_b, v_w_in, v_b_gate, v_w_attn_br, v_w_ssm_br, v_w_out, v_ssm_a_re, v_ssm_a_im, v_ssm_log_dt, v_ssm_b_re, v_ssm_b_im, v_ssm_c_re, v_ssm_c_im, v_ssm_d, v_w_glu, v_ln1_g, v_ln1_b, v_w_ff_gate, v_w_ff_up, v_w_ff_down, v_ln2_g, v_ln2_b):
    given = dict(locals())
    x2, target = x[0], loss_target[0]
    core = lax.axis_index("c").astype(jnp.int32).reshape(1)

    sharded = ("w_in", "w_attn_br", "w_ssm_br", "w_glu", "w_ff_gate", "w_ff_up", "b_gate", "w_out", "w_ff_down")
    send_shape = dict(w_in=(D_MODEL, 896), w_attn_br=(ATTN_WIDTH, 128), w_ssm_br=(SSM_WIDTH, 128), w_glu=(SSM_WIDTH, 128),
                      w_out=(128, D_MODEL), w_ff_gate=(D_MODEL, FF_PAD), w_ff_up=(D_MODEL, FF_PAD), w_ff_down=(FF_PAD, D_MODEL))
    local = {k: given[k][0] for k in sharded}
    sends = {k: local[k] if k == "b_gate" else _send_buffer(local[k], *send_shape[k], name="send_" + k) for k in sharded}
    mixer_weights = ("w_attn_br", "w_ssm_br", "w_glu", "b_gate", "w_out")
    ff_weights = ("w_ff_gate", "w_ff_up", "w_ff_down")
    wt = {}
    wt["w_in"], = _all_gather([sends["w_in"]], "gather_w_in")

    a_re, a_im, log_dt = ssm_a_re[0], ssm_a_im[0], ssm_log_dt[0].reshape(SSM_GROUPS, 1)
    b_re_t, b_im_t = ssm_b_re[0].transpose(0, 2, 1), ssm_b_im[0].transpose(0, 2, 1)
    abar_re, abar_im, e_re, e_im, bbar_re_t, bbar_im_t = _ssm_prep(a_re, a_im, log_dt, b_re_t, b_im_t)
    bmat, cmat, a_chunks = _ssm_tables(abar_re, abar_im, bbar_re_t, bbar_im_t, ssm_c_re[0], ssm_c_im[0])
    cos_t, sin_t = _rope_tables()

    proj, *partly = _proj(x2, wt["w_in"], _gather_first_level([sends[k] for k in mixer_weights]))
    attn, lse, *landed = _attn_fwd(proj, cos_t, sin_t,
                                   _gather_second_level(partly) + _gather_first_level([sends[k] for k in ff_weights]))
    wt.update(zip(mixer_weights, landed[:len(mixer_weights)]))
    ys, states, *landed = _ssm_fwd(proj, bmat, cmat, a_chunks, ssm_d, _gather_second_level(landed[len(mixer_weights):]))
    wt.update(zip(ff_weights, landed))
    wt["w_out"] = wt["w_out"].reshape(D_MODEL, D_MODEL)
    wt["w_ff_down"] = wt["w_ff_down"].reshape(D_FF_PAD, D_MODEL)
    b_gate_full = wt["b_gate"]
    h, xhat1, rstd1, glu, y_attn, y_ssm = _mixer_out(attn, ys, proj, x2, wt["w_attn_br"], wt["w_ssm_br"], wt["w_glu"],
                                                      wt["w_out"], b_gate_full, ln1_g, ln1_b)
    ff_a, ff_b, ff_f = _ff_up(h, wt["w_ff_gate"], wt["w_ff_up"])
    dr2, d_ln2_g, d_ln2_b, loss_lanes = _ff_down_loss(ff_f, wt["w_ff_down"], h, target, ln2_g, ln2_b)

    def pair_sums(names, contrib, from_sibling):
        return _pair_sums([contrib[k] for k in names], from_sibling, core, "pair_sums_" + names[0])

    d_a, d_b = _ff_down_bwd(dr2, wt["w_ff_down"], ff_a, ff_b)
    contrib = dict(w_ff_gate=_weight_grad(h, d_a, "wgrad_w_ff_gate", FF_PAD),
                   w_ff_up=_weight_grad(h, d_b, "wgrad_w_ff_up", FF_PAD),
                   w_ff_down=_weight_grad(ff_f, dr2, "wgrad_w_ff_down"))
    dr1, d_ln1_g, d_ln1_b, *from_sibling = _ff_up_bwd(
        d_a, d_b, wt["w_ff_gate"], wt["w_ff_up"], dr2, xhat1, rstd1, ln1_g, _sibling_swap_ride([contrib[k] for k in ff_weights]))
    ff_sums = pair_sums(ff_weights, contrib, from_sibling)

    d_ya, d_yssm, d_proj, d_attn, d_glu, d_ys, mixed, y_s, gy, d_bg = _mixer_bwd(
        dr1, proj, y_attn, y_ssm, glu, ys, wt["w_attn_br"], wt["w_ssm_br"], wt["w_glu"], wt["w_out"], b_gate_full)
    contrib.update(w_attn_br=_weight_grad(attn, d_ya, "wgrad_w_attn_br", 128),
                   w_ssm_br=_weight_grad(y_s, d_yssm, "wgrad_w_ssm_br", 128),
                   w_glu=_weight_grad(gy, d_glu, "wgrad_w_glu", 128),
                   w_out=_weight_grad(mixed, dr1, "wgrad_w_out"),
                   b_gate=d_bg.reshape(2, 4, 2, 128).transpose(2, 1, 0, 3))
    d_proj, *landed = _attn_bwd(proj, cos_t, sin_t, attn, lse, d_attn, d_proj,
                                _chip_swap_ride(ff_sums) + _sibling_swap_ride([contrib[k] for k in mixer_weights]))
    parts = dict(zip(ff_weights, landed[:len(ff_weights)]))
    mixer_sums = pair_sums(mixer_weights, contrib, landed[len(ff_weights):])
    d_proj, d_bmat, d_cmat, d_abar, d_skip, *landed = _ssm_bwd(d_ys, proj, states, bmat, cmat, a_chunks, ssm_d, d_proj,
                                                               _chip_swap_ride(mixer_sums))
    parts.update(zip(mixer_weights, landed))

    gbb_re_t, gbb_im_t = _block_diag_parts(d_bmat, True)
    gc_re, gc_im = _block_diag_parts(d_cmat, False)
    ga_re = d_abar[:, 0, :CHUNK_STATES].reshape(SSM_GROUPS, SSM_STATE)
    ga_im = d_abar[:, 0, CHUNK_STATES:].reshape(SSM_GROUPS, SSM_STATE)
    g_a_re, g_a_im, g_log_dt, g_b_re_t, g_b_im_t = _ssm_param_bwd(
        a_re, a_im, log_dt, b_re_t, b_im_t, abar_re, abar_im, e_re, e_im, ga_re, ga_im, gbb_re_t, gbb_im_t)
    mine = [g_a_re, g_a_im, g_log_dt, g_b_re_t.transpose(0, 2, 1), g_b_im_t.transpose(0, 2, 1), gc_re, -gc_im,
            d_skip, d_ln1_g, d_ln1_b, d_ln2_g, d_ln2_b]
    small_packed = _pack_rows(mine + [loss_lanes])

    contrib["w_in"], small_partly = _weight_grad(x2, d_proj, "wgrad_w_in", 896, _gather_first_level([small_packed]))
    w_in_sum = pair_sums(["w_in"], contrib, _swap_with_sibling([contrib["w_in"]], "swap_w_in_with_sibling"))
    grad_x, parts["w_in"], every = _grad_x(d_proj, wt["w_in"], dr1,
                                          _chip_swap_ride(w_in_sum) + _gather_second_level([small_partly]))

    grads, deltas, new_m, new_v = {}, {}, {}, {}
    for k in sharded:
        w2 = local[k]
        out = _adamw(w2, given["m_" + k][0], given["v_" + k][0], parts[k], "adamw_" + k)
        grads[k], deltas[k], new_m[k], new_v[k] = (o.reshape((1,) + w2.shape) for o in out)

    *small_grads, loss_sum = _unpack_rows(_sum_devices(every), [given[k].shape for k in SMALL] + [(1, 128)])
    small = _adamw_replicated([given[k] for k in SMALL], [given["m_" + k] for k in SMALL],
                              [given["v_" + k] for k in SMALL], small_grads)
    for res, values in zip((grads, deltas, new_m, new_v), (small_grads,) + small):
        res.update(zip(SMALL, values))
    loss = loss_sum[0, 0]

    order = ("w_in", "b_gate", "w_attn_br", "w_ssm_br", "w_out", "ssm_a_re", "ssm_a_im", "ssm_log_dt", "ssm_b_re", "ssm_b_im",
             "ssm_c_re", "ssm_c_im", "ssm_d", "w_glu", "ln1_g", "ln1_b", "w_ff_gate", "w_ff_up", "w_ff_down", "ln2_g", "ln2_b")
    return (loss, grad_x[None], *[grads[k] for k in order], *[deltas[k] for k in order], *[new_m[k] for k in order],
            *[new_v[k] for k in order])
```

```python
import functools
import math

import jax
import jax.numpy as jnp
import numpy as np
from jax import lax
from jax.experimental import pallas as pl
from jax.experimental.pallas import tpu as pltpu

F32 = jnp.float32
BF16 = jnp.bfloat16

N_DEV = 8
SEQ = 2048
D_MODEL = 1024
HEAD_DIM = 64
ATTN_WIDTH = 512
QKV_WIDTH = 1536
SSM_WIDTH = 512
SSM_GROUPS = 32
SSM_GROUP = 16
SSM_STATE = 64
IN_WIDTH = 7168
D_FF = 2816
FF_SHARD = D_FF // N_DEV
FF_PAD = 384
D_FF_PAD = FF_PAD * N_DEV
DN_ALPHA = 2.0 ** 0.25
LN_EPS = 1e-5
NEG_INF = -1e30
ROPE_THETA = 10000.0
BLOCK = 128
GROUPS = ((1, 16), (4, 4), (16, 1))

ADAM_LR = 0.001
ADAM_B1 = 0.9
ADAM_B2 = 0.999
ADAM_EPS = 1e-08
ADAM_WD = 0.01
ADAM_STEP = 10

VMEM_LIMIT = 56 * 1024 * 1024


_pallas_call = pl.pallas_call


def _cparams(**kw):
    return pltpu.CompilerParams(vmem_limit_bytes=VMEM_LIMIT, **kw)


def _dot(a, b):
    return jnp.dot(a, b, preferred_element_type=F32)


def _dot_nt(a, b):
    return lax.dot_general(a, b, (((1,), (1,)), ((), ())), preferred_element_type=F32)


def _side_by_side(w_ref, row=None):
    rows = slice(None) if row is None else pl.ds(row, 1)
    return jnp.concatenate([w_ref[i, rows, :] for i in range(w_ref.shape[0])], axis=1)


def _dot_tn(a, b):
    return lax.dot_general(a, b, (((0,), (0,)), ((), ())), preferred_element_type=F32)


def _rope_tables():
    half = HEAD_DIM // 2
    inv_freq = np.float32(ROPE_THETA) ** (-np.arange(half, dtype=np.float32) / np.float32(half))
    ang = np.arange(SEQ, dtype=np.float32)[:, None] * inv_freq[None, :]
    cos, sin = np.cos(ang).astype(np.float32), np.sin(ang).astype(np.float32)
    tables = np.tile(cos, (1, 4)), np.tile(np.concatenate([-sin, sin], axis=1), (1, 2))

    def by_phase(t):
        return np.stack([t.reshape(SEQ // d, d, 128).transpose(1, 0, 2).reshape(SEQ, 128) for d, _ in GROUPS])

    return jnp.asarray(by_phase(tables[0])), jnp.asarray(by_phase(tables[1]))


def _swap_halves(x):
    lane = lax.broadcasted_iota(jnp.int32, x.shape, 1)
    return jnp.where((lane & 63) < 32, pltpu.roll(x, 96, axis=1), pltpu.roll(x, 32, axis=1))


def _group_rows(d, nb, r, i):
    src = pl.ds(i * BLOCK, BLOCK) if d == 1 else pl.ds(r + i * BLOCK * d, BLOCK, stride=d)
    return src, pl.ds((r * nb + i) * BLOCK, BLOCK)


def _attn_masks():
    a_idx = lax.broadcasted_iota(jnp.int32, (2 * BLOCK, 2 * BLOCK), 0) & (BLOCK - 1)
    c_idx = lax.broadcasted_iota(jnp.int32, (2 * BLOCK, 2 * BLOCK), 1)
    cur_ok = jnp.logical_and(c_idx >= BLOCK, c_idx - BLOCK <= a_idx)
    prev_ok = jnp.logical_and(c_idx < BLOCK, c_idx >= a_idx)
    lane = lax.broadcasted_iota(jnp.int32, (BLOCK, 128), 1)
    return cur_ok, prev_ok, lane < HEAD_DIM


def _stack_heads(t, head0):
    zero = jnp.zeros_like(t)
    return jnp.concatenate([jnp.where(head0, t, zero), jnp.where(head0, zero, t)], axis=0)


def _unstack_heads(t2, head0):
    return jnp.where(head0, t2[:BLOCK], t2[BLOCK:])


def _attn_fwd(proj, cos_t, sin_t, ride=None):
    def body(q0, q1, q2, k0, k1, k2, v0, v1, v2, cos_ref, sin_ref, attn_ref, lse_ref,
             qs, ks, vs, os_, ms, ls, acc, mnat, lnat):
        cur_ok, prev_ok, head0 = _attn_masks()
        ks[:BLOCK, :] = jnp.zeros((BLOCK, 128), BF16)
        vs[:BLOCK, :] = jnp.zeros((BLOCK, 128), BF16)
        for g, (d, nb) in enumerate(GROUPS):
            q_ref, k_ref, v_ref = (q0, q1, q2)[g], (k0, k1, k2)[g], (v0, v1, v2)[g]
            for r in range(d):
                for i in range(nb):
                    src, dst = _group_rows(d, nb, r, i)
                    below = pl.ds(dst.start + BLOCK, BLOCK)
                    c, s = cos_ref[g, dst, :], sin_ref[g, dst, :]
                    q = q_ref[src, :]
                    k = k_ref[src, :]
                    qs[dst, :] = ((q * c + _swap_halves(q) * s) * 0.125).astype(BF16)
                    ks[below, :] = (k * c + _swap_halves(k) * s).astype(BF16)
                    vs[below, :] = v_ref[src, :].astype(BF16)

            def block(b, carry, nb=nb):
                has_prev = (b & (nb - 1)) > 0
                cur = pl.ds(pl.multiple_of(b * BLOCK, BLOCK), BLOCK)
                window = pl.ds(pl.multiple_of(b * BLOCK, BLOCK), 2 * BLOCK)
                valid = jnp.logical_or(cur_ok, jnp.logical_and(prev_ok, has_prev))
                s = jnp.where(valid, _dot_nt(_stack_heads(qs[cur, :], head0), ks[window, :]), NEG_INF)
                m = jnp.max(s, axis=1, keepdims=True)
                p = jnp.exp(s - m)
                os_[cur, :] = _unstack_heads(_dot(p.astype(BF16), vs[window, :]), head0)
                ms[cur, :] = _unstack_heads(m, head0)
                ls[cur, :] = _unstack_heads(jnp.sum(p, axis=1, keepdims=True), head0)
                return carry

            lax.fori_loop(0, SEQ // BLOCK, block, 0, unroll=2)

            for r in range(d):
                for i in range(nb):
                    src, dst = _group_rows(d, nb, r, i)
                    if g == 0:
                        acc[src, :], mnat[src, :], lnat[src, :] = os_[dst, :], ms[dst, :], ls[dst, :]
                    else:
                        m_old, m_g = mnat[src, :], ms[dst, :]
                        m_new = jnp.maximum(m_old, m_g)
                        a_old, a_g = jnp.exp(m_old - m_new), jnp.exp(m_g - m_new)
                        acc[src, :] = a_old * acc[src, :] + a_g * os_[dst, :]
                        lnat[src, :] = a_old * lnat[src, :] + a_g * ls[dst, :]
                        mnat[src, :] = m_new
        for i in range(SEQ // BLOCK):
            rows = pl.ds(i * BLOCK, BLOCK)
            l = lnat[rows, :]
            attn_ref[rows, :] = acc[rows, :] / l
            lse_ref[rows, :] = mnat[rows, :] + jnp.log(l)

    def col(base):
        return pl.BlockSpec((SEQ, 128), lambda hp, base=base: (0, base + hp))

    in_specs = [col(g * 4) for g in range(3)] + [col(12 + g * 4) for g in range(3)] + [col(24 + g * 4) for g in range(3)]
    table = pl.BlockSpec((3, SEQ, 128), lambda hp: (0, 0, 0), pipeline_mode=pl.Buffered(1))
    out = pl.BlockSpec((SEQ, 128), lambda hp: (0, hp))
    return _call(
        body, "attn_fwd", (4,), in_specs + [table, table], [out, out],
        [_sds((SEQ, ATTN_WIDTH), F32), _sds((SEQ, ATTN_WIDTH), F32)],
        [pltpu.VMEM((SEQ, 128), BF16)] + [pltpu.VMEM((SEQ + BLOCK, 128), BF16)] * 2 + [pltpu.VMEM((SEQ, 128), F32)] * 6,
        [proj] * 9 + [cos_t, sin_t], ride)


def _attn_bwd_group_body(g):
    d, nb = GROUPS[g]

    def body(q_ref, k_ref, v_ref, cos_ref, sin_ref, lse_ref, dattn_ref, dsum_ref, dproj_ref,
             qs, ks, vs, dos, lss, dss, dqs, dks, dvs, stage, outs, sems):
        cur_ok, prev_ok, head0 = _attn_masks()
        ks[:BLOCK, :] = jnp.zeros((BLOCK, 128), BF16)
        vs[:BLOCK, :] = jnp.zeros((BLOCK, 128), BF16)
        dks[:BLOCK, :] = jnp.zeros((BLOCK, 128), F32)
        dvs[:BLOCK, :] = jnp.zeros((BLOCK, 128), F32)
        for r in range(d):
            for i in range(nb):
                src, dst = _group_rows(d, nb, r, i)
                below = pl.ds(dst.start + BLOCK, BLOCK)
                c, s = cos_ref[g, dst, :], sin_ref[g, dst, :]
                q = q_ref[src, :]
                k = k_ref[src, :]
                qs[dst, :] = ((q * c + _swap_halves(q) * s) * 0.125).astype(BF16)
                ks[below, :] = (k * c + _swap_halves(k) * s).astype(BF16)
                vs[below, :] = v_ref[src, :].astype(BF16)
                dos[dst, :] = dattn_ref[src, :].astype(BF16)
                dss[dst, :] = dsum_ref[src, :]
                lss[dst, :] = lse_ref[src, :]
                dks[below, :] = jnp.zeros((BLOCK, 128), F32)
                dvs[below, :] = jnp.zeros((BLOCK, 128), F32)

        def per_head_column(t):
            return jnp.concatenate([jnp.max(jnp.where(head0, t, NEG_INF), axis=1, keepdims=True),
                                    jnp.max(jnp.where(head0, NEG_INF, t), axis=1, keepdims=True)], axis=0)

        def block(b, carry):
            has_prev = (b & (nb - 1)) > 0
            cur = pl.ds(pl.multiple_of(b * BLOCK, BLOCK), BLOCK)
            window = pl.ds(pl.multiple_of(b * BLOCK, BLOCK), 2 * BLOCK)
            valid = jnp.logical_or(cur_ok, jnp.logical_and(prev_ok, has_prev))
            q2, do2 = _stack_heads(qs[cur, :], head0), _stack_heads(dos[cur, :], head0)
            kw, vw = ks[window, :], vs[window, :]
            s = jnp.where(valid, _dot_nt(q2, kw), NEG_INF)
            p = jnp.exp(s - per_head_column(lss[cur, :]))
            ds = (p * (_dot_nt(do2, vw) - per_head_column(dss[cur, :]))).astype(BF16)
            dvs[window, :] += _dot_tn(p.astype(BF16), do2)
            dks[window, :] += _dot_tn(ds, q2)
            dqs[cur, :] = _unstack_heads(_dot(ds, kw), head0)
            return carry

        lax.fori_loop(0, SEQ // BLOCK, block, 0, unroll=2)

        hp = pl.program_id(0)
        copies = []
        for kind in range(3):
            for r in range(d):
                for i in range(nb):
                    src, dst = _group_rows(d, nb, r, i)
                    below = pl.ds(dst.start + BLOCK, BLOCK)
                    if kind == 2:
                        stage[src, :] = dvs[below, :]
                    else:
                        c, s = cos_ref[g, dst, :], sin_ref[g, dst, :]
                        t = dqs[dst, :] * 0.125 if kind == 0 else dks[below, :]
                        stage[src, :] = t * c - _swap_halves(t) * s
            for i in range(SEQ // MM_ROWS):
                rows = pl.ds(i * MM_ROWS, MM_ROWS)
                outs[kind, rows, :] = stage[rows, :].astype(BF16)
            column = pl.multiple_of((kind * 12 + g * 4 + hp) * 128, 128)
            copies.append(pltpu.make_async_copy(outs.at[kind], dproj_ref.at[:, pl.ds(column, 128)], sems.at[kind]))
            copies[-1].start()
        for cp in copies:
            cp.wait()

    return body


def _attn_bwd(proj, cos_t, sin_t, attn, lse, dattn, dproj, ride=None):
    groups = [_attn_bwd_group_body(g) for g in range(3)]

    def body(q0, q1, q2, k0, k1, k2, v0, v1, v2, cos_ref, sin_ref, attn_ref, lse_ref, dattn_ref, dproj_in, dproj_ref,
             dsum, *scratch):
        del dproj_in
        head0 = _attn_masks()[2]
        for i in range(SEQ // BLOCK):
            rows = pl.ds(i * BLOCK, BLOCK)
            prod = dattn_ref[rows, :] * attn_ref[rows, :]
            d0 = jnp.sum(jnp.where(head0, prod, 0.0), axis=1, keepdims=True)
            d1 = jnp.sum(jnp.where(head0, 0.0, prod), axis=1, keepdims=True)
            dsum[rows, :] = jnp.where(head0, d0, d1)
        for g in range(3):
            groups[g]((q0, q1, q2)[g], (k0, k1, k2)[g], (v0, v1, v2)[g], cos_ref, sin_ref, lse_ref, dattn_ref, dsum,
                      dproj_ref, *scratch)

    def col(base):
        return pl.BlockSpec((SEQ, 128), lambda hp, base=base: (0, base + hp))

    table = pl.BlockSpec((3, SEQ, 128), lambda hp: (0, 0, 0), pipeline_mode=pl.Buffered(1))
    return _call(
        body, "attn_bwd", (4,),
        [col(g * 4) for g in range(3)] + [col(12 + g * 4) for g in range(3)] + [col(24 + g * 4) for g in range(3)]
        + [table, table, col(0), col(0), col(0), ANY],
        [ANY], [_sds((SEQ, IN_WIDTH), BF16)],
        [pltpu.VMEM((SEQ, 128), F32)]
        + [pltpu.VMEM((SEQ, 128), BF16)] + [pltpu.VMEM((SEQ + BLOCK, 128), BF16)] * 2 + [pltpu.VMEM((SEQ, 128), BF16)]
        + [pltpu.VMEM((SEQ, 128), F32)] * 3 + [pltpu.VMEM((SEQ + BLOCK, 128), F32)] * 2 + [pltpu.VMEM((SEQ, 128), F32)]
        + [pltpu.VMEM((3, SEQ, 128), BF16), pltpu.SemaphoreType.DMA((3,))],
        [proj] * 9 + [cos_t, sin_t, attn, lse, dattn, dproj], ride, aliases={14: 0})


SSM_CHUNKS = 4
CHUNK_STATES = 512
SCAN_ROWS = 8
U_COL = (3 * QKV_WIDTH) // 128


def _cmul(xr, xi, yr, yi):
    return xr * yr - xi * yi, xr * yi + xi * yr


def _ssm_prep(a_re, a_im, log_dt, b_re_t, b_im_t):
    def body(ar_ref, ai_ref, ldt_ref, br_ref, bi_ref, abr_ref, abi_ref, er_ref, ei_ref, bbr_ref, bbi_ref):
        ar, ai = ar_ref[...], ai_ref[...]
        dt = jnp.exp(ldt_ref[...])
        mag = jnp.exp(ar * dt)
        abr, abi = mag * jnp.cos(ai * dt), mag * jnp.sin(ai * dt)
        den = ar * ar + ai * ai
        nr, ni = abr - 1.0, abi
        er, ei = (nr * ar + ni * ai) / den, (ni * ar - nr * ai) / den
        abr_ref[...], abi_ref[...], er_ref[...], ei_ref[...] = abr, abi, er, ei
        er3, ei3 = er[:, None, :], ei[:, None, :]
        br, bi = br_ref[...], bi_ref[...]
        bbr_ref[...] = er3 * br - ei3 * bi
        bbi_ref[...] = er3 * bi + ei3 * br

    gp = jax.ShapeDtypeStruct(a_re.shape, F32)
    gb = jax.ShapeDtypeStruct(b_re_t.shape, F32)
    return _pallas_call(body, name="ssm_prep", out_shape=(gp, gp, gp, gp, gb, gb))(a_re, a_im, log_dt, b_re_t, b_im_t)


def _ssm_param_bwd(a_re, a_im, log_dt, b_re_t, b_im_t, abar_re, abar_im, e_re, e_im, ga_re, ga_im, gbb_re_t, gbb_im_t):
    def body(ar_ref, ai_ref, ldt_ref, br_ref, bi_ref, abr_ref, abi_ref, er_ref, ei_ref, gar_ref, gai_ref, gbr_ref, gbi_ref,
             o_ar, o_ai, o_ldt, o_br, o_bi):
        ar, ai = ar_ref[...], ai_ref[...]
        dt = jnp.exp(ldt_ref[...])
        er, ei = er_ref[...], ei_ref[...]
        br, bi, gbr, gbi = br_ref[...], bi_ref[...], gbr_ref[...], gbi_ref[...]
        er3, ei3 = er[:, None, :], ei[:, None, :]
        o_br[...] = er3 * gbr + ei3 * gbi
        o_bi[...] = er3 * gbi - ei3 * gbr
        ge_r = jnp.sum(br * gbr + bi * gbi, axis=1)
        ge_i = jnp.sum(br * gbi - bi * gbr, axis=1)
        den = ar * ar + ai * ai
        ilr, ili = ar / den, -ai / den
        t_r, t_i = _cmul(ilr, -ili, ge_r, ge_i)
        gab_r, gab_i = gar_ref[...] + t_r, gai_ref[...] + t_i
        gz_r, gz_i = _cmul(abr_ref[...], -abi_ref[...], gab_r, gab_i)
        el_r, el_i = _cmul(er, ei, ilr, ili)
        u_r, u_i = _cmul(el_r, -el_i, ge_r, ge_i)
        o_ar[...] = dt * gz_r - u_r
        o_ai[...] = dt * gz_i - u_i
        o_ldt[...] = jnp.sum(gz_r * ar + gz_i * ai, axis=1, keepdims=True) * dt

    gp = jax.ShapeDtypeStruct(a_re.shape, F32)
    gb = jax.ShapeDtypeStruct(b_re_t.shape, F32)
    return _pallas_call(body, name="ssm_param_bwd", out_shape=(gp, gp, jax.ShapeDtypeStruct(log_dt.shape, F32), gb, gb))(
        a_re, a_im, log_dt, b_re_t, b_im_t, abar_re, abar_im, e_re, e_im, ga_re, ga_im, gbb_re_t, gbb_im_t)


def _block_diag(blocks_re, blocks_im, sign_im, rows_are_channels):
    both = jnp.stack([blocks_re, sign_im * blocks_im]).reshape(2, SSM_CHUNKS, 8, SSM_GROUP, SSM_STATE)
    eye = jnp.eye(8, dtype=F32)
    if rows_are_channels:
        return jnp.einsum("rcghp,gk->cghrkp", both, eye).reshape(SSM_CHUNKS, 128, 2 * CHUNK_STATES)
    return jnp.einsum("rcghp,gk->crkpgh", both, eye).reshape(SSM_CHUNKS, 2 * CHUNK_STATES, 128)


def _block_diag_parts(mat, rows_are_channels):
    if rows_are_channels:
        six = mat.reshape(SSM_CHUNKS, 8, SSM_GROUP, 2, 8, SSM_STATE)
        parts = jnp.einsum("cghrgp->rcghp", six)
    else:
        six = mat.reshape(SSM_CHUNKS, 2, 8, SSM_STATE, 8, SSM_GROUP)
        parts = jnp.einsum("crgpgh->rcghp", six)
    parts = parts.reshape(2, SSM_GROUPS, SSM_GROUP, SSM_STATE)
    return parts[0], parts[1]


def _scan_consts(a_ref, conj, reverse):
    ar = jnp.broadcast_to(a_ref[:, :CHUNK_STATES], (SCAN_ROWS, CHUNK_STATES))
    ai = jnp.broadcast_to(a_ref[:, CHUNK_STATES:], (SCAN_ROWS, CHUNK_STATES))
    if conj:
        ai = -ai
    row = lax.broadcasted_iota(jnp.int32, (SCAN_ROWS, CHUNK_STATES), 0)
    if reverse:
        row = SCAN_ROWS - 1 - row
    zero = jnp.zeros_like(ar)
    steps = []
    pr, pi = ar, ai
    for shift in (1, 2, 4):
        keep = row >= shift
        steps.append((SCAN_ROWS - shift if reverse else shift, jnp.where(keep, pr, zero), jnp.where(keep, pi, zero)))
        pr, pi = _cmul(pr, pi, pr, pi)
    first = row == 0
    return steps, (jnp.where(first, ar, zero), jnp.where(first, ai, zero)), first


def _scan_tile(xr, xi, prev_r, prev_i, steps, carry_in, reverse):
    edge = SCAN_ROWS - 1 if reverse else 1
    cr, ci = pltpu.roll(prev_r, edge, axis=0), pltpu.roll(prev_i, edge, axis=0)
    xr, xi = xr + carry_in[0] * cr - carry_in[1] * ci, xi + carry_in[0] * ci + carry_in[1] * cr
    for shift, mr, mi in steps:
        sr, si = pltpu.roll(xr, shift, axis=0), pltpu.roll(xi, shift, axis=0)
        xr, xi = xr + mr * sr - mi * si, xi + mr * si + mi * sr
    return xr, xi


MM_ROWS = 256


def _ssm_fwd(proj, bmat, cmat, a_chunks, d_skip, ride=None):
    def body(u_ref, b_ref, c_ref, a_ref, d_ref, y_ref, h_ref):
        for i in range(SEQ // MM_ROWS):
            rows = pl.ds(i * MM_ROWS, MM_ROWS)
            h_ref[rows, :] = _dot(u_ref[rows, :].astype(BF16), b_ref[...])
        steps, carry_in, _ = _scan_consts(a_ref, conj=False, reverse=False)

        def tile(k, carry):
            rows = pl.ds(pl.multiple_of(k * SCAN_ROWS, SCAN_ROWS), SCAN_ROWS)
            xr, xi = _scan_tile(h_ref[rows, :CHUNK_STATES], h_ref[rows, CHUNK_STATES:], carry[0], carry[1], steps, carry_in, False)
            h_ref[rows, :CHUNK_STATES] = xr
            h_ref[rows, CHUNK_STATES:] = xi
            return xr, xi

        zero = jnp.zeros((SCAN_ROWS, CHUNK_STATES), F32)
        lax.fori_loop(0, SEQ // SCAN_ROWS, tile, (zero, zero))
        for i in range(SEQ // MM_ROWS):
            rows = pl.ds(i * MM_ROWS, MM_ROWS)
            y_ref[rows, :] = _dot(h_ref[rows, :].astype(BF16), c_ref[...]) + d_ref[...] * u_ref[rows, :]

    return _call(
        body, "ssm_fwd", (SSM_CHUNKS,),
        [pl.BlockSpec((SEQ, 128), lambda c: (0, U_COL + c)),
         pl.BlockSpec((None, 128, 2 * CHUNK_STATES), lambda c: (c, 0, 0)),
         pl.BlockSpec((None, 2 * CHUNK_STATES, 128), lambda c: (c, 0, 0)),
         pl.BlockSpec((None, 1, 2 * CHUNK_STATES), lambda c: (c, 0, 0)),
         pl.BlockSpec((1, 128), lambda c: (0, c))],
        [pl.BlockSpec((SEQ, 128), lambda c: (0, c)), pl.BlockSpec((SEQ, 2 * CHUNK_STATES), lambda c: (0, c))],
        [_sds((SEQ, SSM_WIDTH), F32), _sds((SEQ, SSM_CHUNKS * 2 * CHUNK_STATES), F32)], [],
        [proj, bmat, cmat, a_chunks, d_skip], ride)


def _ssm_bwd(dys, proj, h, bmat, cmat, a_chunks, d_skip, dproj, ride=None):
    def body(dy_ref, u_ref, h_ref, b_ref, c_ref, a_ref, d_ref, dproj_in, du_ref, db_ref, dc_ref, da_ref, dd_ref, g_ref):
        del dproj_in
        dsum = jnp.zeros((1, 128), F32)
        dcm = jnp.zeros((2 * CHUNK_STATES, 128), F32)
        for i in range(SEQ // MM_ROWS):
            rows = pl.ds(i * MM_ROWS, MM_ROWS)
            dy = dy_ref[rows, :]
            g_ref[rows, :] = _dot_nt(dy.astype(BF16), c_ref[...])
            dsum += jnp.sum(dy * u_ref[rows, :], axis=0, keepdims=True)
            dcm += _dot_tn(h_ref[rows, :].astype(BF16), dy.astype(BF16))
        dd_ref[...] = dsum
        dc_ref[...] = dcm
        steps, carry_in, _ = _scan_consts(a_ref, conj=True, reverse=True)
        first_row = lax.broadcasted_iota(jnp.int32, (SCAN_ROWS, CHUNK_STATES), 0) == 0
        n_tiles = SEQ // SCAN_ROWS

        def tile(j, carry):
            k = n_tiles - 1 - j
            rows = pl.ds(pl.multiple_of(k * SCAN_ROWS, SCAN_ROWS), SCAN_ROWS)
            before = pl.ds(pl.multiple_of(jnp.maximum(k - 1, 0) * SCAN_ROWS, SCAN_ROWS), SCAN_ROWS)
            gr, gi = _scan_tile(g_ref[rows, :CHUNK_STATES], g_ref[rows, CHUNK_STATES:], carry[0], carry[1], steps, carry_in, True)
            g_ref[rows, :CHUNK_STATES] = gr
            g_ref[rows, CHUNK_STATES:] = gi
            has_before = jnp.where(k > 0, 1.0, 0.0)
            hr = jnp.where(first_row, pltpu.roll(h_ref[before, :CHUNK_STATES], 1, axis=0) * has_before,
                           pltpu.roll(h_ref[rows, :CHUNK_STATES], 1, axis=0))
            hi = jnp.where(first_row, pltpu.roll(h_ref[before, CHUNK_STATES:], 1, axis=0) * has_before,
                           pltpu.roll(h_ref[rows, CHUNK_STATES:], 1, axis=0))
            return gr, gi, carry[2] + hr * gr + hi * gi, carry[3] + hr * gi - hi * gr

        zero = jnp.zeros((SCAN_ROWS, CHUNK_STATES), F32)
        _, _, sar, sai = lax.fori_loop(0, n_tiles, tile, (zero, zero, zero, zero))
        da_ref[:, :CHUNK_STATES] = jnp.sum(sar, axis=0, keepdims=True)
        da_ref[:, CHUNK_STATES:] = jnp.sum(sai, axis=0, keepdims=True)
        dbm = jnp.zeros((128, 2 * CHUNK_STATES), F32)
        for i in range(SEQ // MM_ROWS):
            rows = pl.ds(i * MM_ROWS, MM_ROWS)
            g = g_ref[rows, :].astype(BF16)
            du_ref[rows, :] = (_dot_nt(g, b_ref[...]) + d_ref[...] * dy_ref[rows, :]).astype(BF16)
            dbm += _dot_tn(u_ref[rows, :].astype(BF16), g)
        db_ref[...] = dbm

    chunk_col = pl.BlockSpec((SEQ, 128), lambda c: (0, c))
    return _call(
        body, "ssm_bwd", (SSM_CHUNKS,),
        [chunk_col,
         pl.BlockSpec((SEQ, 128), lambda c: (0, U_COL + c)),
         pl.BlockSpec((SEQ, 2 * CHUNK_STATES), lambda c: (0, c)),
         pl.BlockSpec((None, 128, 2 * CHUNK_STATES), lambda c: (c, 0, 0)),
         pl.BlockSpec((None, 2 * CHUNK_STATES, 128), lambda c: (c, 0, 0)),
         pl.BlockSpec((None, 1, 2 * CHUNK_STATES), lambda c: (c, 0, 0)),
         pl.BlockSpec((1, 128), lambda c: (0, c)), ANY],
        [pl.BlockSpec((SEQ, 128), lambda c: (0, U_COL + c)),
         pl.BlockSpec((None, 128, 2 * CHUNK_STATES), lambda c: (c, 0, 0)),
         pl.BlockSpec((None, 2 * CHUNK_STATES, 128), lambda c: (c, 0, 0)),
         pl.BlockSpec((None, 1, 2 * CHUNK_STATES), lambda c: (c, 0, 0)),
         pl.BlockSpec((1, 128), lambda c: (0, c))],
        [_sds((SEQ, IN_WIDTH), BF16), _sds((SSM_CHUNKS, 128, 2 * CHUNK_STATES), F32),
         _sds((SSM_CHUNKS, 2 * CHUNK_STATES, 128), F32), _sds((SSM_CHUNKS, 1, 2 * CHUNK_STATES), F32), _sds((1, SSM_WIDTH), F32)],
        [pltpu.VMEM((SEQ, 2 * CHUNK_STATES), F32)], [dys, proj, h, bmat, cmat, a_chunks, d_skip, dproj], ride, aliases={7: 0})


def _ssm_tables(abar_re, abar_im, bbar_re_t, bbar_im_t, c_re, c_im):
    bmat = _block_diag(bbar_re_t, bbar_im_t, 1.0, True).astype(BF16)
    cmat = _block_diag(c_re, c_im, -1.0, False).astype(BF16)
    a_chunks = jnp.concatenate([abar_re.reshape(SSM_CHUNKS, 1, CHUNK_STATES), abar_im.reshape(SSM_CHUNKS, 1, CHUNK_STATES)], axis=2)
    return bmat, cmat, a_chunks


GL_COL = (3 * QKV_WIDTH + SSM_WIDTH) // D_MODEL
GELU_C = math.sqrt(2.0 / math.pi)
GELU_A = 0.044715


def _sds(shape, dtype):
    return jax.ShapeDtypeStruct(shape, dtype)


def _gelu(x):
    t = jnp.tanh(GELU_C * (x + GELU_A * x * x * x))
    return 0.5 * x * (1.0 + t), t


def _gelu_grad(x, t):
    return 0.5 * (1.0 + t) + 0.5 * x * (1.0 - t * t) * GELU_C * (1.0 + 3.0 * GELU_A * x * x)


def _layer_norm(r, g, b):
    mu = jnp.mean(r, axis=-1, keepdims=True)
    xc = r - mu
    rstd = lax.rsqrt(jnp.mean(xc * xc, axis=-1, keepdims=True) + LN_EPS)
    xhat = xc * rstd
    return xhat * g + b, xhat, rstd


def _layer_norm_bwd(dy, xhat, rstd, g):
    dxhat = dy * g
    m1 = jnp.mean(dxhat, axis=-1, keepdims=True)
    m2 = jnp.mean(dxhat * xhat, axis=-1, keepdims=True)
    return rstd * (dxhat - m1 - xhat * m2)


def _proj(x, w_in, ride=None):
    tm, tn = 1024, 1792

    def body(x_ref, w_ref, o_ref):
        o_ref[...] = _dot(x_ref[...].astype(BF16), _side_by_side(w_ref))

    return _call(
        body, "proj", (SEQ // tm, IN_WIDTH // tn),
        [pl.BlockSpec((tm, D_MODEL), lambda i, j: (i, 0)), pl.BlockSpec((2, D_MODEL, tn // 2), lambda i, j: (j, 0, 0))],
        [pl.BlockSpec((tm, tn), lambda i, j: (i, j))], [_sds((SEQ, IN_WIDTH), F32)], [], [x, w_in], ride)


def _row_spec(tm, width, col=0):
    return pl.BlockSpec((tm, width), lambda i, col=col: (i, col))


def _full_spec(shape):
    return pl.BlockSpec(shape, lambda i: (0,) * len(shape))


def _mixer_out(attn, ys, proj, x, w_ab, w_sb, w_glu, w_out, b_gate, ln_g, ln_b):
    tm = 256

    def body(attn_ref, ys_ref, gl0_ref, gl1_ref, x_ref, wab_ref, wsb_ref, wglu_ref, wout_ref, bg_ref, g_ref, b_ref,
             h_ref, xhat_ref, rstd_ref, glu_ref, ya_ref, yssm_ref):
        gy, _ = _gelu(ys_ref[...])
        glu = _dot(gy.astype(BF16), _side_by_side(wglu_ref))
        glu_ref[...] = glu
        y_s = glu[:, :SSM_WIDTH] * jax.nn.sigmoid(glu[:, SSM_WIDTH:])
        y_ssm = _dot(y_s.astype(BF16), _side_by_side(wsb_ref))
        y_attn = _dot(attn_ref[...].astype(BF16), _side_by_side(wab_ref))
        ya_ref[...] = y_attn
        yssm_ref[...] = y_ssm
        g0 = jax.nn.sigmoid(gl0_ref[...] + _side_by_side(bg_ref, 0))
        g1 = jax.nn.sigmoid(gl1_ref[...] + _side_by_side(bg_ref, 1))
        mixed = g0 * y_attn + g1 * y_ssm
        r1 = DN_ALPHA * x_ref[...] + _dot(mixed.astype(BF16), wout_ref[...])
        h, xhat, rstd = _layer_norm(r1, g_ref[...], b_ref[...])
        h_ref[...] = h
        xhat_ref[...] = xhat
        rstd_ref[...] = jnp.broadcast_to(rstd, (tm, 128))

    wide = _sds((SEQ, D_MODEL), F32)
    return _pallas_call(
        body, name="mixer_out", grid=(SEQ // tm,),
        in_specs=[_row_spec(tm, ATTN_WIDTH), _row_spec(tm, SSM_WIDTH), _row_spec(tm, D_MODEL, GL_COL), _row_spec(tm, D_MODEL, GL_COL + 1),
                  _row_spec(tm, D_MODEL), _full_spec((N_DEV, ATTN_WIDTH, 128)), _full_spec((N_DEV, SSM_WIDTH, 128)),
                  _full_spec((N_DEV, SSM_WIDTH, 128)), _full_spec((D_MODEL, D_MODEL)), _full_spec((N_DEV, 2, 128)),
                  _full_spec((1, D_MODEL)), _full_spec((1, D_MODEL))],
        out_specs=(_row_spec(tm, D_MODEL), _row_spec(tm, D_MODEL), _row_spec(tm, 128), _row_spec(tm, D_MODEL),
                   _row_spec(tm, D_MODEL), _row_spec(tm, D_MODEL)),
        out_shape=(wide, wide, _sds((SEQ, 128), F32), wide, wide, wide),
        compiler_params=_cparams(dimension_semantics=("arbitrary",)),
    )(attn, ys, proj, proj, x, w_ab, w_sb, w_glu, w_out, b_gate, ln_g, ln_b)


def _ff_up(h, w_gate, w_up):
    tm, tn = 1024, 768

    def body(h_ref, wg_ref, wu_ref, a_ref, b_ref, f_ref):
        hb = h_ref[...].astype(BF16)
        a, b = _dot(hb, _side_by_side(wg_ref)), _dot(hb, _side_by_side(wu_ref))
        a_ref[...] = a.astype(BF16)
        b_ref[...] = b.astype(BF16)
        f_ref[...] = (a * jax.nn.sigmoid(a) * b).astype(BF16)

    tile = pl.BlockSpec((tm, tn), lambda i, j: (i, j))
    wtile = pl.BlockSpec((tn // FF_PAD, D_MODEL, FF_PAD), lambda i, j: (j, 0, 0))
    out = _sds((SEQ, D_FF_PAD), BF16)
    return _pallas_call(
        body, name="ff_up", grid=(SEQ // tm, D_FF_PAD // tn),
        in_specs=[pl.BlockSpec((tm, D_MODEL), lambda i, j: (i, 0)), wtile, wtile],
        out_specs=(tile, tile, tile), out_shape=(out, out, out),
        compiler_params=_cparams(dimension_semantics=("arbitrary", "arbitrary")),
    )(h, w_gate, w_up)


def _ff_down_loss(f, w_down, h, target, ln_g, ln_b):
    tm = 256

    def body(f_ref, w_ref, h_ref, t_ref, g_ref, b_ref, dr_ref, dg_ref, db_ref, loss_ref):
        @pl.when(pl.program_id(0) == 0)
        def _():
            dg_ref[...] = jnp.zeros_like(dg_ref)
            db_ref[...] = jnp.zeros_like(db_ref)
            loss_ref[...] = jnp.zeros_like(loss_ref)

        r2 = DN_ALPHA * h_ref[...] + _dot(f_ref[...], w_ref[...])
        g = g_ref[...]
        out, xhat, rstd = _layer_norm(r2, g, b_ref[...])
        err = out - t_ref[...]
        loss_ref[...] += 0.5 * jnp.sum(jnp.mean(err * err, axis=-1, keepdims=True), axis=0, keepdims=True)
        dout = err * (1.0 / D_MODEL)
        dg_ref[...] += jnp.sum(dout * xhat, axis=0, keepdims=True)
        db_ref[...] += jnp.sum(dout, axis=0, keepdims=True)
        dr_ref[...] = _layer_norm_bwd(dout, xhat, rstd, g)

    vec = _sds((1, D_MODEL), F32)
    return _pallas_call(
        body, name="ff_down_loss", grid=(SEQ // tm,),
        in_specs=[_row_spec(tm, D_FF_PAD), _full_spec((D_FF_PAD, D_MODEL)), _row_spec(tm, D_MODEL), _row_spec(tm, D_MODEL),
                  _full_spec((1, D_MODEL)), _full_spec((1, D_MODEL))],
        out_specs=(_row_spec(tm, D_MODEL), _full_spec((1, D_MODEL)), _full_spec((1, D_MODEL)), _full_spec((1, 128))),
        out_shape=(_sds((SEQ, D_MODEL), F32), vec, vec, _sds((1, 128), F32)),
        compiler_params=_cparams(dimension_semantics=("arbitrary",)),
    )(f, w_down, h, target, ln_g, ln_b)


def _ff_down_bwd(dr2, w_down, a, b):
    tm, tn = 1024, 768

    def body(dr_ref, w_ref, a_ref, b_ref, da_ref, db_ref):
        df = _dot_nt(dr_ref[...].astype(BF16), w_ref[...])
        av, bv = a_ref[...].astype(F32), b_ref[...].astype(F32)
        sg = jax.nn.sigmoid(av)
        da_ref[...] = (df * bv * sg * (1.0 + av * (1.0 - sg))).astype(BF16)
        db_ref[...] = (df * av * sg).astype(BF16)

    tile = pl.BlockSpec((tm, tn), lambda i, j: (i, j))
    out = _sds((SEQ, D_FF_PAD), BF16)
    return _pallas_call(
        body, name="ff_down_bwd", grid=(SEQ // tm, D_FF_PAD // tn),
        in_specs=[pl.BlockSpec((tm, D_MODEL), lambda i, j: (i, 0)), pl.BlockSpec((tn, D_MODEL), lambda i, j: (j, 0)), tile, tile],
        out_specs=(tile, tile), out_shape=(out, out),
        compiler_params=_cparams(dimension_semantics=("arbitrary", "arbitrary")),
    )(dr2, w_down, a, b)


def _ff_up_bwd(da, db, w_gate, w_up, dr2, xhat1, rstd1, ln_g, ride=None):
    tm, tk = 1024, 768
    nk = D_FF_PAD // tk

    def body(da_ref, db_ref, wg_ref, wu_ref, dr2_ref, xhat_ref, rstd_ref, g_ref, dr1_ref, dg_ref, dbias_ref, acc):
        i, k = pl.program_id(0), pl.program_id(1)

        @pl.when(jnp.logical_and(i == 0, k == 0))
        def _():
            dg_ref[...] = jnp.zeros_like(dg_ref)
            dbias_ref[...] = jnp.zeros_like(dbias_ref)

        part = _dot_nt(da_ref[...], _side_by_side(wg_ref)) + _dot_nt(db_ref[...], _side_by_side(wu_ref))

        @pl.when(k == 0)
        def _():
            acc[...] = part

        @pl.when(k > 0)
        def _():
            acc[...] += part

        @pl.when(k == nk - 1)
        def _():
            dh = DN_ALPHA * dr2_ref[...] + acc[...]
            xhat = xhat_ref[...]
            dg_ref[...] += jnp.sum(dh * xhat, axis=0, keepdims=True)
            dbias_ref[...] += jnp.sum(dh, axis=0, keepdims=True)
            rstd = jnp.max(rstd_ref[...], axis=1, keepdims=True)
            dr1_ref[...] = _layer_norm_bwd(dh, xhat, rstd, g_ref[...])

    hid = pl.BlockSpec((tm, tk), lambda i, k: (i, k))
    wtile = pl.BlockSpec((tk // FF_PAD, D_MODEL, FF_PAD), lambda i, k: (k, 0, 0))
    row = pl.BlockSpec((tm, D_MODEL), lambda i, k: (i, 0))
    vec = pl.BlockSpec((1, D_MODEL), lambda i, k: (0, 0))
    return _call(
        body, "ff_up_bwd", (SEQ // tm, nk),
        [hid, hid, wtile, wtile, row, row, pl.BlockSpec((tm, 128), lambda i, k: (i, 0)), vec],
        [row, vec, vec], [_sds((SEQ, D_MODEL), F32), _sds((1, D_MODEL), F32), _sds((1, D_MODEL), F32)],
        [pltpu.VMEM((tm, D_MODEL), F32)], [da, db, w_gate, w_up, dr2, xhat1, rstd1, ln_g], ride)


def _mixer_bwd(dr1, proj, y_attn, y_ssm, glu, ys, w_ab, w_sb, w_glu, w_out, b_gate):
    tm = 256

    def body(dr1_ref, gl0_ref, gl1_ref, ya_ref, yssm_ref, glu_ref, ys_ref, wab_ref, wsb_ref, wglu_ref, wout_ref, bg_ref,
             dya_ref, dyssm_ref, dgl_ref, dattn_ref, dglu_ref, dys_ref, mixed_ref, ysb_ref, gy_ref, dbg_ref):
        @pl.when(pl.program_id(0) == 0)
        def _():
            dbg_ref[...] = jnp.zeros_like(dbg_ref)

        dmixed = _dot_nt(dr1_ref[...].astype(BF16), wout_ref[...])
        g0 = jax.nn.sigmoid(gl0_ref[...] + _side_by_side(bg_ref, 0))
        g1 = jax.nn.sigmoid(gl1_ref[...] + _side_by_side(bg_ref, 1))
        y_attn, y_ssm = ya_ref[...], yssm_ref[...]
        mixed_ref[...] = (g0 * y_attn + g1 * y_ssm).astype(BF16)
        dya = (dmixed * g0).astype(BF16)
        dyssm = (dmixed * g1).astype(BF16)
        dya_ref[...] = dya
        dyssm_ref[...] = dyssm
        dgl0 = dmixed * y_attn * g0 * (1.0 - g0)
        dgl1 = dmixed * y_ssm * g1 * (1.0 - g1)
        dgl_ref[:, :GL_COL * D_MODEL] = jnp.zeros((tm, GL_COL * D_MODEL), BF16)
        dgl_ref[:, GL_COL * D_MODEL:(GL_COL + 1) * D_MODEL] = dgl0.astype(BF16)
        dgl_ref[:, (GL_COL + 1) * D_MODEL:] = dgl1.astype(BF16)
        dbg_ref[:, :D_MODEL] += jnp.sum(dgl0, axis=0, keepdims=True)
        dbg_ref[:, D_MODEL:] += jnp.sum(dgl1, axis=0, keepdims=True)
        dattn_ref[...] = _dot_nt(dya, _side_by_side(wab_ref))
        dy_s = _dot_nt(dyssm, _side_by_side(wsb_ref))
        glu = glu_ref[...]
        glu1, sg = glu[:, :SSM_WIDTH], jax.nn.sigmoid(glu[:, SSM_WIDTH:])
        ysb_ref[...] = (glu1 * sg).astype(BF16)
        dglu1 = (dy_s * sg).astype(BF16)
        dglu2 = (dy_s * glu1 * sg * (1.0 - sg)).astype(BF16)
        dglu_ref[:, :SSM_WIDTH] = dglu1
        dglu_ref[:, SSM_WIDTH:] = dglu2
        dgy = _dot_nt(jnp.concatenate([dglu1, dglu2], axis=1), _side_by_side(wglu_ref))
        ys = ys_ref[...]
        gy, t = _gelu(ys)
        gy_ref[...] = gy.astype(BF16)
        dys_ref[...] = dgy * _gelu_grad(ys, t)

    wide_b, half_b = _sds((SEQ, D_MODEL), BF16), _sds((SEQ, SSM_WIDTH), BF16)
    half_f = _sds((SEQ, SSM_WIDTH), F32)
    return _pallas_call(
        body, name="mixer_bwd", grid=(SEQ // tm,),
        in_specs=[_row_spec(tm, D_MODEL), _row_spec(tm, D_MODEL, GL_COL), _row_spec(tm, D_MODEL, GL_COL + 1), _row_spec(tm, D_MODEL),
                  _row_spec(tm, D_MODEL), _row_spec(tm, D_MODEL), _row_spec(tm, SSM_WIDTH), _full_spec((N_DEV, ATTN_WIDTH, 128)),
                  _full_spec((N_DEV, SSM_WIDTH, 128)), _full_spec((N_DEV, SSM_WIDTH, 128)), _full_spec((D_MODEL, D_MODEL)),
                  _full_spec((N_DEV, 2, 128))],
        out_specs=(_row_spec(tm, D_MODEL), _row_spec(tm, D_MODEL), _row_spec(tm, IN_WIDTH), _row_spec(tm, ATTN_WIDTH),
                   _row_spec(tm, D_MODEL), _row_spec(tm, SSM_WIDTH), _row_spec(tm, D_MODEL), _row_spec(tm, SSM_WIDTH),
                   _row_spec(tm, SSM_WIDTH), _full_spec((1, 2 * D_MODEL))),
        out_shape=(wide_b, wide_b, _sds((SEQ, IN_WIDTH), BF16), half_f, wide_b, half_f, wide_b, half_b, half_b,
                   _sds((1, 2 * D_MODEL), F32)),
        compiler_params=_cparams(dimension_semantics=("arbitrary",)),
    )(dr1, proj, proj, y_attn, y_ssm, glu, ys, w_ab, w_sb, w_glu, w_out, b_gate)


def _grad_x(dproj, w_in, dr1, ride=None):
    tm, tk = 1024, 1792
    nk = IN_WIDTH // tk

    def body(dp_ref, w_ref, dr1_ref, o_ref, acc):
        k = pl.program_id(1)
        part = _dot_nt(dp_ref[...], _side_by_side(w_ref))

        @pl.when(k == 0)
        def _():
            acc[...] = part

        @pl.when(k > 0)
        def _():
            acc[...] += part

        @pl.when(k == nk - 1)
        def _():
            o_ref[...] = DN_ALPHA * dr1_ref[...] + acc[...]

    row = pl.BlockSpec((tm, D_MODEL), lambda i, k: (i, 0))
    return _call(
        body, "grad_x", (SEQ // tm, nk),
        [pl.BlockSpec((tm, tk), lambda i, k: (i, k)), pl.BlockSpec((2, D_MODEL, tk // 2), lambda i, k: (k, 0, 0)), row],
        [row], [_sds((SEQ, D_MODEL), F32)], [pltpu.VMEM((tm, D_MODEL), F32)], [dproj, w_in, dr1], ride)


def _weight_grad(a, b, name, shard_cols=None, ride=None):
    k, n = a.shape[1], b.shape[1]
    tk = min(k, 512) if shard_cols else k // N_DEV
    tn = n // 4 if shard_cols else min(n, 1024)

    def body(a_ref, b_ref, o_ref):
        grad = _dot_tn(a_ref[...].astype(BF16), b_ref[...].astype(BF16))
        if shard_cols:
            o_ref[0] = grad[:, :shard_cols].astype(BF16)
            o_ref[1] = grad[:, shard_cols:].astype(BF16)
        else:
            o_ref[...] = grad.astype(BF16)

    if shard_cols:
        out_spec = pl.BlockSpec((2, None, tk, shard_cols), lambda kk, j: (0, j, kk, 0))
        out_shape = _sds((2, 4, k, shard_cols), BF16)
    else:
        out_spec = pl.BlockSpec((None, None, tk, tn), lambda kk, j: (kk % 2, kk // 2, 0, j))
        out_shape = _sds((2, 4, tk, n), BF16)
    out = _call(body, name, (k // tk, n // tn),
                [pl.BlockSpec((SEQ, tk), lambda kk, j: (0, kk)), pl.BlockSpec((SEQ, tn), lambda kk, j: (0, j))],
                [out_spec], [out_shape], [], [a, b], ride)
    return out[0] if ride is None else out


MESH = pl.DeviceIdType.MESH
ANY = pl.BlockSpec(memory_space=pl.ANY)


def _place():
    return lax.axis_index("x"), lax.axis_index("y"), lax.axis_index("c")


def _other_chips(x, y):
    return [(1 - x, y), (x, 1 - y), (1 - x, 1 - y)]


class _Ride:
    def __init__(self, operands, results, aliases, sems, start, wait):
        self.operands, self.results, self.aliases, self.sems = list(operands), list(results), dict(aliases), list(sems)
        self.start, self.wait = start, wait

    def __add__(self, other):
        n_in, n_out, n_sem = len(self.operands), len(self.results), len(self.sems)

        def both(which):
            def run(ins, outs, sems):
                getattr(self, which)(ins[:n_in], outs[:n_out], sems[:n_sem])
                getattr(other, which)(ins[n_in:], outs[n_out:], sems[n_sem:])
            return run

        aliases = {**self.aliases, **{n_in + i: n_out + j for i, j in other.aliases.items()}}
        return _Ride(self.operands + other.operands, self.results + other.results, aliases, self.sems + other.sems,
                     both("start"), both("wait"))


def _call(body, name, grid, in_specs, out_specs, out_shape, scratch_shapes, operands, ride=None, aliases=None):
    in_specs, out_specs, out_shape = list(in_specs), list(out_specs), list(out_shape)
    scratch_shapes, operands, aliases = list(scratch_shapes), list(operands), dict(aliases or {})
    kernel_body = body
    if ride is not None:
        n_in, n_out, n_scr, r_in, r_out = len(in_specs), len(out_specs), len(scratch_shapes), len(ride.operands), len(ride.results)

        def kernel_body(*refs):
            out0, scr0 = n_in + r_in, n_in + r_in + n_out + r_out
            ride_refs = (refs[n_in:out0], refs[out0 + n_out:scr0], refs[scr0 + n_scr:])
            ids = [pl.program_id(i) for i in range(len(grid))]
            first = functools.reduce(jnp.logical_and, [i == 0 for i in ids])
            last = functools.reduce(jnp.logical_and, [i == g - 1 for i, g in zip(ids, grid)])

            @pl.when(first)
            def _():
                ride.start(*ride_refs)

            body(*refs[:n_in], *refs[out0:out0 + n_out], *refs[scr0:scr0 + n_scr])

            @pl.when(last)
            def _():
                ride.wait(*ride_refs)

        aliases.update({n_in + i: n_out + j for i, j in ride.aliases.items()})
        in_specs += [ANY] * r_in
        out_specs += [ANY] * r_out
        out_shape += ride.results
        scratch_shapes += ride.sems
        operands += ride.operands
    return _pallas_call(
        kernel_body, name=name, grid=grid, in_specs=in_specs, out_specs=out_specs, out_shape=out_shape,
        scratch_shapes=scratch_shapes, input_output_aliases=aliases,
        compiler_params=_cparams(dimension_semantics=("arbitrary",) * len(grid)),
    )(*operands)


def _gather_first_level(shards):
    n = len(shards)

    def copies(ins, outs, sems, landed):
        send_sems, recv_sems, local_sems = sems
        x, y, c = _place()
        peers = [(x, y, 1 - c)] + [(px, py, c) for px, py in _other_chips(x, y)]

        def row(peer):
            return 4 * x + 2 * y + c if not landed else 4 * peer[0] + 2 * peer[1] + peer[2]

        local = [pltpu.make_async_copy(ins[a], outs[a].at[4 * x + 2 * y + c], local_sems.at[a]) for a in range(n)]
        remote = [pltpu.make_async_remote_copy(
            src_ref=ins[a], dst_ref=outs[a].at[row(peer)], send_sem=send_sems.at[a, k], recv_sem=recv_sems.at[a, k],
            device_id=peer, device_id_type=MESH) for a in range(n) for k, peer in enumerate(peers)]
        return local, remote

    def start(ins, outs, sems):
        local, remote = copies(ins, outs, sems, False)
        for cp in local + remote:
            cp.start()

    def wait(ins, outs, sems):
        local, sent = copies(ins, outs, sems, False)
        for cp in copies(ins, outs, sems, True)[1]:
            cp.wait_recv()
        for cp in sent:
            cp.wait_send()
        for cp in local:
            cp.wait()

    return _Ride(shards, [_sds((N_DEV,) + s.shape, s.dtype) for s in shards], {},
                 [pltpu.SemaphoreType.DMA((n, 4)), pltpu.SemaphoreType.DMA((n, 4)), pltpu.SemaphoreType.DMA((n,))], start, wait)


def _gather_second_level(buffers):
    n = len(buffers)

    def copies(outs, sems, core):
        send_sems, recv_sems = sems
        x, y, c = _place()
        return [pltpu.make_async_remote_copy(
            src_ref=outs[a].at[4 * px + 2 * py + core], dst_ref=outs[a].at[4 * px + 2 * py + core], send_sem=send_sems.at[a, j],
            recv_sem=recv_sems.at[a, j], device_id=(x, y, 1 - c), device_id_type=MESH)
            for a in range(n) for j, (px, py) in enumerate(_other_chips(x, y))]

    def start(ins, outs, sems):
        for cp in copies(outs, sems, lax.axis_index("c")):
            cp.start()

    def wait(ins, outs, sems):
        for cp in copies(outs, sems, 1 - lax.axis_index("c")):
            cp.wait_recv()
        for cp in copies(outs, sems, lax.axis_index("c")):
            cp.wait_send()

    return _Ride(buffers, [_sds(b.shape, b.dtype) for b in buffers], {i: i for i in range(n)},
                 [pltpu.SemaphoreType.DMA((n, 3)), pltpu.SemaphoreType.DMA((n, 3))], start, wait)


def _sibling_swap_ride(grads):
    n = len(grads)

    def copies(ins, outs, sems):
        x, y, c = _place()
        return [pltpu.make_async_remote_copy(
            src_ref=ins[a].at[1 - c], dst_ref=outs[a], send_sem=sems[0].at[a], recv_sem=sems[1].at[a],
            device_id=(x, y, 1 - c), device_id_type=MESH) for a in range(n)]

    def start(ins, outs, sems):
        for cp in copies(ins, outs, sems):
            cp.start()

    def wait(ins, outs, sems):
        for cp in copies(ins, outs, sems):
            cp.wait()

    return _Ride(grads, [_sds(g.shape[1:], g.dtype) for g in grads], {},
                 [pltpu.SemaphoreType.DMA((n,)), pltpu.SemaphoreType.DMA((n,))], start, wait)


def _chip_swap_ride(sums):
    n = len(sums)

    def copies(ins, outs, sems, landed):
        send_sems, recv_sems, local_sems = sems
        x, y, c = _place()
        mine = 2 * x + y
        local = [pltpu.make_async_copy(ins[a].at[mine], outs[a].at[mine], local_sems.at[a]) for a in range(n)]
        remote = [pltpu.make_async_remote_copy(
            src_ref=ins[a].at[2 * px + py], dst_ref=outs[a].at[2 * px + py if landed else mine], send_sem=send_sems.at[a, j],
            recv_sem=recv_sems.at[a, j], device_id=(px, py, c), device_id_type=MESH)
            for a in range(n) for j, (px, py) in enumerate(_other_chips(x, y))]
        return local, remote

    def start(ins, outs, sems):
        local, remote = copies(ins, outs, sems, False)
        for cp in local + remote:
            cp.start()

    def wait(ins, outs, sems):
        local, sent = copies(ins, outs, sems, False)
        for cp in copies(ins, outs, sems, True)[1]:
            cp.wait_recv()
        for cp in sent:
            cp.wait_send()
        for cp in local:
            cp.wait()

    return _Ride(sums, [_sds(s.shape, s.dtype) for s in sums], {},
                 [pltpu.SemaphoreType.DMA((n, 3)), pltpu.SemaphoreType.DMA((n, 3)), pltpu.SemaphoreType.DMA((n,))], start, wait)


def _send_buffer(w, rows, cols, name):
    r, c = w.shape

    def body(w_ref, o_ref):
        if (r, c) != (rows, cols):
            o_ref[...] = jnp.zeros((rows, cols), BF16)
        o_ref[:r, :c] = w_ref[...].astype(BF16)

    return _pallas_call(body, name=name, out_shape=_sds((rows, cols), BF16))(w)


def _send_buffer_transposed(w_t, cols, name):
    c, r = w_t.shape

    def body(w_ref, o_ref):
        padded = jnp.concatenate([w_ref[...], jnp.zeros((cols - c, r), F32)], axis=0)
        o_ref[...] = padded.T.astype(BF16)

    return _pallas_call(body, name=name, out_shape=_sds((r, cols), BF16))(w_t)


def _all_gather(shards, name):
    n = len(shards)

    def body(*refs):
        ins, outs = refs[:n], refs[n:2 * n]
        send_sems, recv_sems, local_sems = refs[2 * n:]
        x, y, c = _place()
        me, sibling = (x, y, c), (x, y, 1 - c)
        chips = _other_chips(x, y)

        def slot(a, px, py, pc):
            return outs[a].at[4 * px + 2 * py + pc]

        def copy(a, k, block, to, src=None):
            return pltpu.make_async_remote_copy(
                src_ref=slot(a, *block) if src is None else src, dst_ref=slot(a, *block),
                send_sem=send_sems.at[a, k], recv_sem=recv_sems.at[a, k], device_id=to, device_id_type=MESH)

        mine = [pltpu.make_async_copy(ins[a], slot(a, *me), local_sems.at[a]) for a in range(n)]
        for cp in mine:
            cp.start()
        first = []
        for a in range(n):
            first.append(copy(a, 0, me, sibling, src=ins[a]))
            first += [copy(a, 1 + j, me, (*chip, c), src=ins[a]) for j, chip in enumerate(chips)]
        for cp in first:
            cp.start()
        passed = []
        for j, chip in enumerate(chips):
            for a in range(n):
                copy(a, 1 + j, (*chip, c), me).wait_recv()
                onward = copy(a, 4 + j, (*chip, c), sibling)
                onward.start()
                passed.append(onward)
        for a in range(n):
            copy(a, 0, sibling, me).wait_recv()
            for j, chip in enumerate(chips):
                copy(a, 4 + j, (*chip, 1 - c), me).wait_recv()
        for cp in first + passed:
            cp.wait_send()
        for cp in mine:
            cp.wait()

    return _pallas_call(
        body, name=name, in_specs=[ANY] * n, out_specs=[ANY] * n,
        out_shape=[_sds((N_DEV,) + s.shape, s.dtype) for s in shards],
        scratch_shapes=[pltpu.SemaphoreType.DMA((n, 7)), pltpu.SemaphoreType.DMA((n, 7)), pltpu.SemaphoreType.DMA((n,))],
    )(*shards)


def _swap_with_sibling(grads, name):
    n = len(grads)

    def body(*refs):
        ins, outs = refs[:n], refs[n:2 * n]
        send_sems, recv_sems = refs[2 * n:]
        x, y, c = _place()
        copies = [pltpu.make_async_remote_copy(
            src_ref=ins[a].at[1 - c], dst_ref=outs[a], send_sem=send_sems.at[a], recv_sem=recv_sems.at[a],
            device_id=(x, y, 1 - c), device_id_type=MESH) for a in range(n)]
        for cp in copies:
            cp.start()
        for cp in copies:
            cp.wait()

    return _pallas_call(
        body, name=name, in_specs=[ANY] * n, out_specs=[ANY] * n,
        out_shape=[_sds(g.shape[1:], g.dtype) for g in grads],
        scratch_shapes=[pltpu.SemaphoreType.DMA((n,)), pltpu.SemaphoreType.DMA((n,))],
    )(*grads)


def _pair_sums(gs, rs, core, name):
    n_arrays = len(gs)

    def body(core_ref, *refs):
        for g_ref, r_ref, o_ref in zip(refs[:n_arrays], refs[n_arrays:2 * n_arrays], refs[2 * n_arrays:]):
            o_ref[...] = (g_ref[...].astype(F32) + r_ref[...].astype(F32)).astype(o_ref.dtype)

    def own(g):
        return pl.BlockSpec((None, None) + g.shape[2:], lambda p, core_ref: (core_ref[0], p, 0, 0))

    def chip(g):
        return pl.BlockSpec((None,) + g.shape[2:], lambda p, core_ref: (p, 0, 0))

    return _pallas_call(
        body, name=name,
        grid_spec=pltpu.PrefetchScalarGridSpec(
            num_scalar_prefetch=1, grid=(4,), in_specs=[own(g) for g in gs] + [chip(g) for g in gs],
            out_specs=[chip(g) for g in gs]),
        out_shape=[_sds(g.shape[1:], g.dtype) for g in gs], compiler_params=_cparams(dimension_semantics=("arbitrary",)),
    )(core, *gs, *rs)


def _adamw_math(w, g, m, v):
    m = ADAM_B1 * m + (1.0 - ADAM_B1) * g
    v = ADAM_B2 * v + (1.0 - ADAM_B2) * (g * g)
    m_hat = m / (1.0 - ADAM_B1 ** ADAM_STEP)
    v_hat = v / (1.0 - ADAM_B2 ** ADAM_STEP)
    return -ADAM_LR * (m_hat / (jnp.sqrt(v_hat) + ADAM_EPS) + ADAM_WD * w), m, v


def _adamw(w, m, v, parts, name):
    r, c = w.shape
    tr = r if r <= 512 else 256
    n_parts, pr, pc = parts.shape
    assert r % tr == 0 and (tr == r or pr == r)

    def body(w_ref, m_ref, v_ref, p_ref, g_out, d_out, m_out, v_out):
        g = p_ref[0, :tr, :c].astype(F32)
        for p in range(1, n_parts):
            g = g + p_ref[p, :tr, :c].astype(F32)
        g_out[...] = g
        d_out[...], m_out[...], v_out[...] = _adamw_math(w_ref[...], g, m_ref[...], v_ref[...])

    tile = pl.BlockSpec((tr, c), lambda i: (i, 0))
    part_tile = pl.BlockSpec((n_parts, pr if tr == r else tr, pc), lambda i: (0, i, 0))
    out = _sds((r, c), F32)
    return _pallas_call(
        body, name=name, grid=(r // tr,), in_specs=[tile, tile, tile, part_tile], out_specs=(tile,) * 4,
        out_shape=(out,) * 4, compiler_params=_cparams(dimension_semantics=("arbitrary",)),
    )(w, m, v, parts)


def _adamw_transposed(w_t, m_t, v_t, parts, name):
    c, r = w_t.shape
    tr = 256
    n_parts, _, pc = parts.shape

    def body(w_ref, m_ref, v_ref, p_ref, g_out, d_out, m_out, v_out):
        g = p_ref[0].astype(F32)
        for p in range(1, n_parts):
            g = g + p_ref[p].astype(F32)
        g = g.T[:c]
        g_out[...] = g
        d_out[...], m_out[...], v_out[...] = _adamw_math(w_ref[...], g, m_ref[...], v_ref[...])

    tile = pl.BlockSpec((c, tr), lambda i: (0, i))
    out = _sds((c, r), F32)
    return _pallas_call(
        body, name=name, grid=(r // tr,), in_specs=[tile, tile, tile, pl.BlockSpec((n_parts, tr, pc), lambda i: (0, i, 0))],
        out_specs=(tile,) * 4, out_shape=(out,) * 4, compiler_params=_cparams(dimension_semantics=("arbitrary",)),
    )(w_t, m_t, v_t, parts)


SMALL = ("ssm_a_re", "ssm_a_im", "ssm_log_dt", "ssm_b_re", "ssm_b_im", "ssm_c_re", "ssm_c_im", "ssm_d",
         "ln1_g", "ln1_b", "ln2_g", "ln2_b")


def _pack_rows(arrays):
    rows = []
    for a in arrays:
        flat = a.reshape(-1)
        rows.append(jnp.pad(flat, (0, -flat.shape[0] % 128)).reshape(-1, 128))
    packed = jnp.concatenate(rows, axis=0)
    return jnp.pad(packed, ((0, -packed.shape[0] % 8), (0, 0)))


def _unpack_rows(packed, shapes):
    out, row = [], 0
    for shape in shapes:
        size = math.prod(shape)
        n_rows = -(-size // 128)
        out.append(packed[row:row + n_rows].reshape(-1)[:size].reshape(shape))
        row += n_rows
    return out


def _sum_devices(parts):
    def body(p_ref, o_ref):
        total = p_ref[0]
        for dev in range(1, N_DEV):
            total = total + p_ref[dev]
        o_ref[...] = total

    return _pallas_call(body, name="sum_devices", out_shape=_sds(parts.shape[1:], F32))(parts)


def _adamw_replicated(ws, ms, vs, gs):
    n = len(ws)

    def body(*refs):
        w_refs, m_refs, v_refs, g_refs, d_out, m_out, v_out = (refs[i * n:(i + 1) * n] for i in range(7))
        for i in range(n):
            d_out[i][...], m_out[i][...], v_out[i][...] = _adamw_math(w_refs[i][...], g_refs[i][...], m_refs[i][...], v_refs[i][...])

    out = _pallas_call(body, name="adamw_replicated", out_shape=[_sds(w.shape, F32) for w in ws] * 3,
                       compiler_params=_cparams())(*ws, *ms, *vs, *gs)
    return out[:n], out[n:2 * n], out[2 * n:]


def kernel(x, w_in, b_gate, w_attn_br, w_ssm_br, w_out, ssm_a_re, ssm_a_im, ssm_log_dt, ssm_b_re, ssm_b_im, ssm_c_re, ssm_c_im, ssm_d, w_glu, ln1_g, ln1_b, w_ff_gate, w_ff_up, w_ff_down, ln2_g, ln2_b, loss_target, m_w_in, m_b_gate, m_w_attn_br, m_w_ssm_br, m_w_out, m_ssm_a_re, m_ssm_a_im, m_ssm_log_dt, m_ssm_b_re, m_ssm_b_im, m_ssm_c_re, m_ssm_c_im, m_ssm_d, m_w_glu, m_ln1_g, m_ln1_b, m_w_ff_gate, m_w_ff_up, m_w_ff_down, m_ln2_g, m_ln2_b, v_w_in, v_b_gate, v_w_attn_br, v_w_ssm_br, v_w_out, v_ssm_a_re, v_ssm_a_im, v_ssm_log_dt, v_ssm_b_re, v_ssm_b_im, v_ssm_c_re, v_ssm_c_im, v_ssm_d, v_w_glu, v_ln1_g, v_ln1_b, v_w_ff_gate, v_w_ff_up, v_w_ff_down, v_ln2_g, v_ln2_b):
    given = dict(locals())
    x2, target = x[0], loss_target[0]
    core = lax.axis_index("c").astype(jnp.int32).reshape(1)

    sharded = ("w_in", "w_attn_br", "w_ssm_br", "w_glu", "w_ff_gate", "w_ff_up", "b_gate", "w_out", "w_ff_down")
    send_shape = dict(w_in=(D_MODEL, 896), w_attn_br=(ATTN_WIDTH, 128), w_ssm_br=(SSM_WIDTH, 128), w_glu=(SSM_WIDTH, 128),
                      w_out=(128, D_MODEL), w_ff_gate=(D_MODEL, FF_PAD), w_ff_up=(D_MODEL, FF_PAD), w_ff_down=(FF_PAD, D_MODEL))
    local = {k: given[k][0] for k in sharded}
    narrow = ("w_ff_gate", "w_ff_up")
    sends = {k: local[k] if k == "b_gate" else _send_buffer_transposed(local[k].T, FF_PAD, "send_" + k) if k in narrow
             else _send_buffer(local[k], *send_shape[k], name="send_" + k) for k in sharded}
    mixer_weights = ("w_attn_br", "w_ssm_br", "w_glu", "b_gate", "w_out")
    ff_weights = ("w_ff_gate", "w_ff_up", "w_ff_down")
    wt = {}
    wt["w_in"], = _all_gather([sends["w_in"]], "gather_w_in")

    a_re, a_im, log_dt = ssm_a_re[0], ssm_a_im[0], ssm_log_dt[0].reshape(SSM_GROUPS, 1)
    b_re_t, b_im_t = ssm_b_re[0].transpose(0, 2, 1), ssm_b_im[0].transpose(0, 2, 1)
    abar_re, abar_im, e_re, e_im, bbar_re_t, bbar_im_t = _ssm_prep(a_re, a_im, log_dt, b_re_t, b_im_t)
    bmat, cmat, a_chunks = _ssm_tables(abar_re, abar_im, bbar_re_t, bbar_im_t, ssm_c_re[0], ssm_c_im[0])
    cos_t, sin_t = _rope_tables()

    proj, *partly = _proj(x2, wt["w_in"], _gather_first_level([sends[k] for k in mixer_weights]))
    attn, lse, *landed = _attn_fwd(proj, cos_t, sin_t,
                                   _gather_second_level(partly) + _gather_first_level([sends[k] for k in ff_weights]))
    wt.update(zip(mixer_weights, landed[:len(mixer_weights)]))
    ys, states, *landed = _ssm_fwd(proj, bmat, cmat, a_chunks, ssm_d, _gather_second_level(landed[len(mixer_weights):]))
    wt.update(zip(ff_weights, landed))
    wt["w_out"] = wt["w_out"].reshape(D_MODEL, D_MODEL)
    wt["w_ff_down"] = wt["w_ff_down"].reshape(D_FF_PAD, D_MODEL)
    b_gate_full = wt["b_gate"]
    h, xhat1, rstd1, glu, y_attn, y_ssm = _mixer_out(attn, ys, proj, x2, wt["w_attn_br"], wt["w_ssm_br"], wt["w_glu"],
                                                      wt["w_out"], b_gate_full, ln1_g, ln1_b)
    ff_a, ff_b, ff_f = _ff_up(h, wt["w_ff_gate"], wt["w_ff_up"])
    dr2, d_ln2_g, d_ln2_b, loss_lanes = _ff_down_loss(ff_f, wt["w_ff_down"], h, target, ln2_g, ln2_b)

    def pair_sums(names, contrib, from_sibling):
        return _pair_sums([contrib[k] for k in names], from_sibling, core, "pair_sums_" + names[0])

    d_a, d_b = _ff_down_bwd(dr2, wt["w_ff_down"], ff_a, ff_b)
    contrib = dict(w_ff_gate=_weight_grad(h, d_a, "wgrad_w_ff_gate", FF_PAD),
                   w_ff_up=_weight_grad(h, d_b, "wgrad_w_ff_up", FF_PAD),
                   w_ff_down=_weight_grad(ff_f, dr2, "wgrad_w_ff_down"))
    dr1, d_ln1_g, d_ln1_b, *from_sibling = _ff_up_bwd(
        d_a, d_b, wt["w_ff_gate"], wt["w_ff_up"], dr2, xhat1, rstd1, ln1_g, _sibling_swap_ride([contrib[k] for k in ff_weights]))
    ff_sums = pair_sums(ff_weights, contrib, from_sibling)

    d_ya, d_yssm, d_proj, d_attn, d_glu, d_ys, mixed, y_s, gy, d_bg = _mixer_bwd(
        dr1, proj, y_attn, y_ssm, glu, ys, wt["w_attn_br"], wt["w_ssm_br"], wt["w_glu"], wt["w_out"], b_gate_full)
    contrib.update(w_attn_br=_weight_grad(attn, d_ya, "wgrad_w_attn_br", 128),
                   w_ssm_br=_weight_grad(y_s, d_yssm, "wgrad_w_ssm_br", 128),
                   w_glu=_weight_grad(gy, d_glu, "wgrad_w_glu", 128),
                   w_out=_weight_grad(mixed, dr1, "wgrad_w_out"),
                   b_gate=d_bg.reshape(2, 4, 2, 128).transpose(2, 1, 0, 3))
    d_proj, *landed = _attn_bwd(proj, cos_t, sin_t, attn, lse, d_attn, d_proj,
                                _chip_swap_ride(ff_sums) + _sibling_swap_ride([contrib[k] for k in mixer_weights]))
    parts = dict(zip(ff_weights, landed[:len(ff_weights)]))
    mixer_sums = pair_sums(mixer_weights, contrib, landed[len(ff_weights):])
    d_proj, d_bmat, d_cmat, d_abar, d_skip, *landed = _ssm_bwd(d_ys, proj, states, bmat, cmat, a_chunks, ssm_d, d_proj,
                                                               _chip_swap_ride(mixer_sums))
    parts.update(zip(mixer_weights, landed))

    gbb_re_t, gbb_im_t = _block_diag_parts(d_bmat, True)
    gc_re, gc_im = _block_diag_parts(d_cmat, False)
    ga_re = d_abar[:, 0, :CHUNK_STATES].reshape(SSM_GROUPS, SSM_STATE)
    ga_im = d_abar[:, 0, CHUNK_STATES:].reshape(SSM_GROUPS, SSM_STATE)
    g_a_re, g_a_im, g_log_dt, g_b_re_t, g_b_im_t = _ssm_param_bwd(
        a_re, a_im, log_dt, b_re_t, b_im_t, abar_re, abar_im, e_re, e_im, ga_re, ga_im, gbb_re_t, gbb_im_t)
    mine = [g_a_re, g_a_im, g_log_dt, g_b_re_t, g_b_im_t, gc_re, -gc_im,
            d_skip, d_ln1_g, d_ln1_b, d_ln2_g, d_ln2_b]
    small_packed = _pack_rows(mine + [loss_lanes])

    contrib["w_in"], small_partly = _weight_grad(x2, d_proj, "wgrad_w_in", 896, _gather_first_level([small_packed]))
    w_in_sum = pair_sums(["w_in"], contrib, _swap_with_sibling([contrib["w_in"]], "swap_w_in_with_sibling"))
    grad_x, parts["w_in"], every = _grad_x(d_proj, wt["w_in"], dr1,
                                          _chip_swap_ride(w_in_sum) + _gather_second_level([small_partly]))

    grads, deltas, new_m, new_v = {}, {}, {}, {}
    for k in sharded:
        w2 = local[k]
        if k in narrow:
            out = _adamw_transposed(w2.T, given["m_" + k][0].T, given["v_" + k][0].T, parts[k], "adamw_" + k)
            out = [o.T for o in out]
        else:
            out = _adamw(w2, given["m_" + k][0], given["v_" + k][0], parts[k], "adamw_" + k)
        grads[k], deltas[k], new_m[k], new_v[k] = (o.reshape((1,) + w2.shape) for o in out)

    def held(k, a):
        return a.transpose(0, 1, 3, 2) if k in ("ssm_b_re", "ssm_b_im") else a

    *small_grads, loss_sum = _unpack_rows(_sum_devices(every), [held(k, given[k]).shape for k in SMALL] + [(1, 128)])
    small = _adamw_replicated([held(k, given[k]) for k in SMALL], [held(k, given["m_" + k]) for k in SMALL],
                              [held(k, given["v_" + k]) for k in SMALL], small_grads)
    for res, values in zip((grads, deltas, new_m, new_v), (small_grads,) + small):
        res.update((k, held(k, a)) for k, a in zip(SMALL, values))
    loss = loss_sum[0, 0]

    order = ("w_in", "b_gate", "w_attn_br", "w_ssm_br", "w_out", "ssm_a_re", "ssm_a_im", "ssm_log_dt", "ssm_b_re", "ssm_b_im",
             "ssm_c_re", "ssm_c_im", "ssm_d", "w_glu", "ln1_g", "ln1_b", "w_ff_gate", "w_ff_up", "w_ff_down", "ln2_g", "ln2_b")
    return (loss, grad_x[None], *[grads[k] for k in order], *[deltas[k] for k in order], *[new_m[k] for k in order],
            *[new_v[k] for k in order])
```

```python
import functools
import math

import jax
import jax.numpy as jnp
import numpy as np
from jax import lax
from jax.experimental import pallas as pl
from jax.experimental.pallas import tpu as pltpu

F32 = jnp.float32
BF16 = jnp.bfloat16

N_DEV = 8
SEQ = 2048
D_MODEL = 1024
HEAD_DIM = 64
ATTN_WIDTH = 512
QKV_WIDTH = 1536
SSM_WIDTH = 512
SSM_GROUPS = 32
SSM_GROUP = 16
SSM_STATE = 64
IN_WIDTH = 7168
D_FF = 2816
FF_SHARD = D_FF // N_DEV
FF_PAD = 384
D_FF_PAD = FF_PAD * N_DEV
DN_ALPHA = 2.0 ** 0.25
LN_EPS = 1e-5
NEG_INF = -1e30
ROPE_THETA = 10000.0
BLOCK = 128
GROUPS = ((1, 16), (4, 4), (16, 1))

ADAM_LR = 0.001
ADAM_B1 = 0.9
ADAM_B2 = 0.999
ADAM_EPS = 1e-08
ADAM_WD = 0.01
ADAM_STEP = 10

VMEM_LIMIT = 56 * 1024 * 1024


_pallas_call = pl.pallas_call


def _cparams(**kw):
    return pltpu.CompilerParams(vmem_limit_bytes=VMEM_LIMIT, **kw)


def _dot(a, b):
    return jnp.dot(a, b, preferred_element_type=F32)


def _dot_nt(a, b):
    return lax.dot_general(a, b, (((1,), (1,)), ((), ())), preferred_element_type=F32)


def _side_by_side(w_ref, row=None):
    rows = slice(None) if row is None else pl.ds(row, 1)
    return jnp.concatenate([w_ref[i, rows, :] for i in range(w_ref.shape[0])], axis=1)


def _dot_tn(a, b):
    return lax.dot_general(a, b, (((0,), (0,)), ((), ())), preferred_element_type=F32)


def _rope_tables():
    half = HEAD_DIM // 2
    inv_freq = np.float32(ROPE_THETA) ** (-np.arange(half, dtype=np.float32) / np.float32(half))
    ang = np.arange(SEQ, dtype=np.float32)[:, None] * inv_freq[None, :]
    cos, sin = np.cos(ang).astype(np.float32), np.sin(ang).astype(np.float32)
    tables = np.tile(cos, (1, 4)), np.tile(np.concatenate([-sin, sin], axis=1), (1, 2))

    def by_phase(t):
        return np.stack([t.reshape(SEQ // d, d, 128).transpose(1, 0, 2).reshape(SEQ, 128) for d, _ in GROUPS])

    return jnp.asarray(by_phase(tables[0])), jnp.asarray(by_phase(tables[1]))


def _swap_halves(x):
    lane = lax.broadcasted_iota(jnp.int32, x.shape, 1)
    return jnp.where((lane & 63) < 32, pltpu.roll(x, 96, axis=1), pltpu.roll(x, 32, axis=1))


def _group_rows(d, nb, r, i):
    src = pl.ds(i * BLOCK, BLOCK) if d == 1 else pl.ds(r + i * BLOCK * d, BLOCK, stride=d)
    return src, pl.ds((r * nb + i) * BLOCK, BLOCK)


def _attn_masks():
    a_idx = lax.broadcasted_iota(jnp.int32, (2 * BLOCK, 2 * BLOCK), 0) & (BLOCK - 1)
    c_idx = lax.broadcasted_iota(jnp.int32, (2 * BLOCK, 2 * BLOCK), 1)
    cur_ok = jnp.logical_and(c_idx >= BLOCK, c_idx - BLOCK <= a_idx)
    prev_ok = jnp.logical_and(c_idx < BLOCK, c_idx >= a_idx)
    lane = lax.broadcasted_iota(jnp.int32, (BLOCK, 128), 1)
    return cur_ok, prev_ok, lane < HEAD_DIM


def _stack_heads(t, head0):
    zero = jnp.zeros_like(t)
    return jnp.concatenate([jnp.where(head0, t, zero), jnp.where(head0, zero, t)], axis=0)


def _unstack_heads(t2, head0):
    return jnp.where(head0, t2[:BLOCK], t2[BLOCK:])


def _attn_fwd(proj, cos_t, sin_t, ride=None):
    def body(q0, q1, q2, k0, k1, k2, v0, v1, v2, cos_ref, sin_ref, attn_ref, lse_ref,
             qs, ks, vs, os_, ms, ls, acc, mnat, lnat):
        cur_ok, prev_ok, head0 = _attn_masks()
        ks[:BLOCK, :] = jnp.zeros((BLOCK, 128), BF16)
        vs[:BLOCK, :] = jnp.zeros((BLOCK, 128), BF16)
        for g, (d, nb) in enumerate(GROUPS):
            q_ref, k_ref, v_ref = (q0, q1, q2)[g], (k0, k1, k2)[g], (v0, v1, v2)[g]
            for r in range(d):
                for i in range(nb):
                    src, dst = _group_rows(d, nb, r, i)
                    below = pl.ds(dst.start + BLOCK, BLOCK)
                    c, s = cos_ref[g, dst, :], sin_ref[g, dst, :]
                    q = q_ref[src, :]
                    k = k_ref[src, :]
                    qs[dst, :] = ((q * c + _swap_halves(q) * s) * 0.125).astype(BF16)
                    ks[below, :] = (k * c + _swap_halves(k) * s).astype(BF16)
                    vs[below, :] = v_ref[src, :].astype(BF16)

            def block(b, carry, nb=nb):
                has_prev = (b & (nb - 1)) > 0
                cur = pl.ds(pl.multiple_of(b * BLOCK, BLOCK), BLOCK)
                window = pl.ds(pl.multiple_of(b * BLOCK, BLOCK), 2 * BLOCK)
                valid = jnp.logical_or(cur_ok, jnp.logical_and(prev_ok, has_prev))
                s = jnp.where(valid, _dot_nt(_stack_heads(qs[cur, :], head0), ks[window, :]), NEG_INF)
                m = jnp.max(s, axis=1, keepdims=True)
                p = jnp.exp(s - m)
                os_[cur, :] = _unstack_heads(_dot(p.astype(BF16), vs[window, :]), head0)
                ms[cur, :] = _unstack_heads(m, head0)
                ls[cur, :] = _unstack_heads(jnp.sum(p, axis=1, keepdims=True), head0)
                return carry

            lax.fori_loop(0, SEQ // BLOCK, block, 0, unroll=2)

            for r in range(d):
                for i in range(nb):
                    src, dst = _group_rows(d, nb, r, i)
                    if g == 0:
                        acc[src, :], mnat[src, :], lnat[src, :] = os_[dst, :], ms[dst, :], ls[dst, :]
                    else:
                        m_old, m_g = mnat[src, :], ms[dst, :]
                        m_new = jnp.maximum(m_old, m_g)
                        a_old, a_g = jnp.exp(m_old - m_new), jnp.exp(m_g - m_new)
                        acc[src, :] = a_old * acc[src, :] + a_g * os_[dst, :]
                        lnat[src, :] = a_old * lnat[src, :] + a_g * ls[dst, :]
                        mnat[src, :] = m_new
        for i in range(SEQ // BLOCK):
            rows = pl.ds(i * BLOCK, BLOCK)
            l = lnat[rows, :]
            attn_ref[rows, :] = acc[rows, :] / l
            lse_ref[rows, :] = mnat[rows, :] + jnp.log(l)

    def col(base):
        return pl.BlockSpec((SEQ, 128), lambda hp, base=base: (0, base + hp))

    in_specs = [col(g * 4) for g in range(3)] + [col(12 + g * 4) for g in range(3)] + [col(24 + g * 4) for g in range(3)]
    table = pl.BlockSpec((3, SEQ, 128), lambda hp: (0, 0, 0), pipeline_mode=pl.Buffered(1))
    out = pl.BlockSpec((SEQ, 128), lambda hp: (0, hp))
    return _call(
        body, "attn_fwd", (4,), in_specs + [table, table], [out, out],
        [_sds((SEQ, ATTN_WIDTH), F32), _sds((SEQ, ATTN_WIDTH), F32)],
        [pltpu.VMEM((SEQ, 128), BF16)] + [pltpu.VMEM((SEQ + BLOCK, 128), BF16)] * 2 + [pltpu.VMEM((SEQ, 128), F32)] * 6,
        [proj] * 9 + [cos_t, sin_t], ride)


def _attn_bwd_group_body(g):
    d, nb = GROUPS[g]

    def body(q_ref, k_ref, v_ref, cos_ref, sin_ref, lse_ref, dattn_ref, dsum_ref, dproj_ref,
             qs, ks, vs, dos, lss, dss, dqs, dks, dvs, stage, outs, sems):
        cur_ok, prev_ok, head0 = _attn_masks()
        ks[:BLOCK, :] = jnp.zeros((BLOCK, 128), BF16)
        vs[:BLOCK, :] = jnp.zeros((BLOCK, 128), BF16)
        dks[:BLOCK, :] = jnp.zeros((BLOCK, 128), F32)
        dvs[:BLOCK, :] = jnp.zeros((BLOCK, 128), F32)
        for r in range(d):
            for i in range(nb):
                src, dst = _group_rows(d, nb, r, i)
                below = pl.ds(dst.start + BLOCK, BLOCK)
                c, s = cos_ref[g, dst, :], sin_ref[g, dst, :]
                q = q_ref[src, :]
                k = k_ref[src, :]
                qs[dst, :] = ((q * c + _swap_halves(q) * s) * 0.125).astype(BF16)
                ks[below, :] = (k * c + _swap_halves(k) * s).astype(BF16)
                vs[below, :] = v_ref[src, :].astype(BF16)
                dos[dst, :] = dattn_ref[src, :].astype(BF16)
                dss[dst, :] = dsum_ref[src, :]
                lss[dst, :] = lse_ref[src, :]
                dks[below, :] = jnp.zeros((BLOCK, 128), F32)
                dvs[below, :] = jnp.zeros((BLOCK, 128), F32)

        def per_head_column(t):
            return jnp.concatenate([jnp.max(jnp.where(head0, t, NEG_INF), axis=1, keepdims=True),
                                    jnp.max(jnp.where(head0, NEG_INF, t), axis=1, keepdims=True)], axis=0)

        def block(b, carry):
            has_prev = (b & (nb - 1)) > 0
            cur = pl.ds(pl.multiple_of(b * BLOCK, BLOCK), BLOCK)
            window = pl.ds(pl.multiple_of(b * BLOCK, BLOCK), 2 * BLOCK)
            valid = jnp.logical_or(cur_ok, jnp.logical_and(prev_ok, has_prev))
            q2, do2 = _stack_heads(qs[cur, :], head0), _stack_heads(dos[cur, :], head0)
            kw, vw = ks[window, :], vs[window, :]
            s = jnp.where(valid, _dot_nt(q2, kw), NEG_INF)
            p = jnp.exp(s - per_head_column(lss[cur, :]))
            ds = (p * (_dot_nt(do2, vw) - per_head_column(dss[cur, :]))).astype(BF16)
            dvs[window, :] += _dot_tn(p.astype(BF16), do2)
            dks[window, :] += _dot_tn(ds, q2)
            dqs[cur, :] = _unstack_heads(_dot(ds, kw), head0)
            return carry

        lax.fori_loop(0, SEQ // BLOCK, block, 0, unroll=2)

        hp = pl.program_id(0)
        copies = []
        for kind in range(3):
            for r in range(d):
                for i in range(nb):
                    src, dst = _group_rows(d, nb, r, i)
                    below = pl.ds(dst.start + BLOCK, BLOCK)
                    if kind == 2:
                        stage[src, :] = dvs[below, :]
                    else:
                        c, s = cos_ref[g, dst, :], sin_ref[g, dst, :]
                        t = dqs[dst, :] * 0.125 if kind == 0 else dks[below, :]
                        stage[src, :] = t * c - _swap_halves(t) * s
            for i in range(SEQ // MM_ROWS):
                rows = pl.ds(i * MM_ROWS, MM_ROWS)
                outs[kind, rows, :] = stage[rows, :].astype(BF16)
            column = pl.multiple_of((kind * 12 + g * 4 + hp) * 128, 128)
            copies.append(pltpu.make_async_copy(outs.at[kind], dproj_ref.at[:, pl.ds(column, 128)], sems.at[kind]))
            copies[-1].start()
        for cp in copies:
            cp.wait()

    return body


def _attn_bwd(proj, cos_t, sin_t, attn, lse, dattn, dproj, ride=None):
    groups = [_attn_bwd_group_body(g) for g in range(3)]

    def body(q0, q1, q2, k0, k1, k2, v0, v1, v2, cos_ref, sin_ref, attn_ref, lse_ref, dattn_ref, dproj_in, dproj_ref,
             dsum, *scratch):
        del dproj_in
        head0 = _attn_masks()[2]
        for i in range(SEQ // BLOCK):
            rows = pl.ds(i * BLOCK, BLOCK)
            prod = dattn_ref[rows, :] * attn_ref[rows, :]
            d0 = jnp.sum(jnp.where(head0, prod, 0.0), axis=1, keepdims=True)
            d1 = jnp.sum(jnp.where(head0, 0.0, prod), axis=1, keepdims=True)
            dsum[rows, :] = jnp.where(head0, d0, d1)
        for g in range(3):
            groups[g]((q0, q1, q2)[g], (k0, k1, k2)[g], (v0, v1, v2)[g], cos_ref, sin_ref, lse_ref, dattn_ref, dsum,
                      dproj_ref, *scratch)

    def col(base):
        return pl.BlockSpec((SEQ, 128), lambda hp, base=base: (0, base + hp))

    table = pl.BlockSpec((3, SEQ, 128), lambda hp: (0, 0, 0), pipeline_mode=pl.Buffered(1))
    return _call(
        body, "attn_bwd", (4,),
        [col(g * 4) for g in range(3)] + [col(12 + g * 4) for g in range(3)] + [col(24 + g * 4) for g in range(3)]
        + [table, table, col(0), col(0), col(0), ANY],
        [ANY], [_sds((SEQ, IN_WIDTH), BF16)],
        [pltpu.VMEM((SEQ, 128), F32)]
        + [pltpu.VMEM((SEQ, 128), BF16)] + [pltpu.VMEM((SEQ + BLOCK, 128), BF16)] * 2 + [pltpu.VMEM((SEQ, 128), BF16)]
        + [pltpu.VMEM((SEQ, 128), F32)] * 3 + [pltpu.VMEM((SEQ + BLOCK, 128), F32)] * 2 + [pltpu.VMEM((SEQ, 128), F32)]
        + [pltpu.VMEM((3, SEQ, 128), BF16), pltpu.SemaphoreType.DMA((3,))],
        [proj] * 9 + [cos_t, sin_t, attn, lse, dattn, dproj], ride, aliases={14: 0})


SSM_CHUNKS = 4
CHUNK_STATES = 512
SCAN_ROWS = 8
U_COL = (3 * QKV_WIDTH) // 128


def _cmul(xr, xi, yr, yi):
    return xr * yr - xi * yi, xr * yi + xi * yr


def _ssm_prep(a_re, a_im, log_dt, b_re_t, b_im_t):
    def body(ar_ref, ai_ref, ldt_ref, br_ref, bi_ref, abr_ref, abi_ref, er_ref, ei_ref, bbr_ref, bbi_ref):
        ar, ai = ar_ref[...], ai_ref[...]
        dt = jnp.exp(ldt_ref[...])
        mag = jnp.exp(ar * dt)
        abr, abi = mag * jnp.cos(ai * dt), mag * jnp.sin(ai * dt)
        den = ar * ar + ai * ai
        nr, ni = abr - 1.0, abi
        er, ei = (nr * ar + ni * ai) / den, (ni * ar - nr * ai) / den
        abr_ref[...], abi_ref[...], er_ref[...], ei_ref[...] = abr, abi, er, ei
        er3, ei3 = er[:, None, :], ei[:, None, :]
        br, bi = br_ref[...], bi_ref[...]
        bbr_ref[...] = er3 * br - ei3 * bi
        bbi_ref[...] = er3 * bi + ei3 * br

    gp = jax.ShapeDtypeStruct(a_re.shape, F32)
    gb = jax.ShapeDtypeStruct(b_re_t.shape, F32)
    return _pallas_call(body, name="ssm_prep", out_shape=(gp, gp, gp, gp, gb, gb))(a_re, a_im, log_dt, b_re_t, b_im_t)


def _ssm_param_bwd(a_re, a_im, log_dt, b_re_t, b_im_t, abar_re, abar_im, e_re, e_im, ga_re, ga_im, gbb_re_t, gbb_im_t):
    def body(ar_ref, ai_ref, ldt_ref, br_ref, bi_ref, abr_ref, abi_ref, er_ref, ei_ref, gar_ref, gai_ref, gbr_ref, gbi_ref,
             o_ar, o_ai, o_ldt, o_br, o_bi):
        ar, ai = ar_ref[...], ai_ref[...]
        dt = jnp.exp(ldt_ref[...])
        er, ei = er_ref[...], ei_ref[...]
        br, bi, gbr, gbi = br_ref[...], bi_ref[...], gbr_ref[...], gbi_ref[...]
        er3, ei3 = er[:, None, :], ei[:, None, :]
        o_br[...] = er3 * gbr + ei3 * gbi
        o_bi[...] = er3 * gbi - ei3 * gbr
        ge_r = jnp.sum(br * gbr + bi * gbi, axis=1)
        ge_i = jnp.sum(br * gbi - bi * gbr, axis=1)
        den = ar * ar + ai * ai
        ilr, ili = ar / den, -ai / den
        t_r, t_i = _cmul(ilr, -ili, ge_r, ge_i)
        gab_r, gab_i = gar_ref[...] + t_r, gai_ref[...] + t_i
        gz_r, gz_i = _cmul(abr_ref[...], -abi_ref[...], gab_r, gab_i)
        el_r, el_i = _cmul(er, ei, ilr, ili)
        u_r, u_i = _cmul(el_r, -el_i, ge_r, ge_i)
        o_ar[...] = dt * gz_r - u_r
        o_ai[...] = dt * gz_i - u_i
        o_ldt[...] = jnp.sum(gz_r * ar + gz_i * ai, axis=1, keepdims=True) * dt

    gp = jax.ShapeDtypeStruct(a_re.shape, F32)
    gb = jax.ShapeDtypeStruct(b_re_t.shape, F32)
    return _pallas_call(body, name="ssm_param_bwd", out_shape=(gp, gp, jax.ShapeDtypeStruct(log_dt.shape, F32), gb, gb))(
        a_re, a_im, log_dt, b_re_t, b_im_t, abar_re, abar_im, e_re, e_im, ga_re, ga_im, gbb_re_t, gbb_im_t)


def _block_diag(blocks_re, blocks_im, sign_im, rows_are_channels):
    both = jnp.stack([blocks_re, sign_im * blocks_im]).reshape(2, SSM_CHUNKS, 8, SSM_GROUP, SSM_STATE)
    eye = jnp.eye(8, dtype=F32)
    if rows_are_channels:
        return jnp.einsum("rcghp,gk->cghrkp", both, eye).reshape(SSM_CHUNKS, 128, 2 * CHUNK_STATES)
    return jnp.einsum("rcghp,gk->crkpgh", both, eye).reshape(SSM_CHUNKS, 2 * CHUNK_STATES, 128)


def _block_diag_parts(mat, rows_are_channels):
    if rows_are_channels:
        six = mat.reshape(SSM_CHUNKS, 8, SSM_GROUP, 2, 8, SSM_STATE)
        parts = jnp.einsum("cghrgp->rcghp", six)
    else:
        six = mat.reshape(SSM_CHUNKS, 2, 8, SSM_STATE, 8, SSM_GROUP)
        parts = jnp.einsum("crgpgh->rcghp", six)
    parts = parts.reshape(2, SSM_GROUPS, SSM_GROUP, SSM_STATE)
    return parts[0], parts[1]


def _scan_consts(a_ref, conj, reverse):
    ar = jnp.broadcast_to(a_ref[:, :CHUNK_STATES], (SCAN_ROWS, CHUNK_STATES))
    ai = jnp.broadcast_to(a_ref[:, CHUNK_STATES:], (SCAN_ROWS, CHUNK_STATES))
    if conj:
        ai = -ai
    row = lax.broadcasted_iota(jnp.int32, (SCAN_ROWS, CHUNK_STATES), 0)
    if reverse:
        row = SCAN_ROWS - 1 - row
    zero = jnp.zeros_like(ar)
    steps = []
    pr, pi = ar, ai
    for shift in (1, 2, 4):
        keep = row >= shift
        steps.append((SCAN_ROWS - shift if reverse else shift, jnp.where(keep, pr, zero), jnp.where(keep, pi, zero)))
        pr, pi = _cmul(pr, pi, pr, pi)
    first = row == 0
    return steps, (jnp.where(first, ar, zero), jnp.where(first, ai, zero)), first


def _scan_tile(xr, xi, prev_r, prev_i, steps, carry_in, reverse):
    edge = SCAN_ROWS - 1 if reverse else 1
    cr, ci = pltpu.roll(prev_r, edge, axis=0), pltpu.roll(prev_i, edge, axis=0)
    xr, xi = xr + carry_in[0] * cr - carry_in[1] * ci, xi + carry_in[0] * ci + carry_in[1] * cr
    for shift, mr, mi in steps:
        sr, si = pltpu.roll(xr, shift, axis=0), pltpu.roll(xi, shift, axis=0)
        xr, xi = xr + mr * sr - mi * si, xi + mr * si + mi * sr
    return xr, xi


MM_ROWS = 256


def _ssm_fwd(proj, bmat, cmat, a_chunks, d_skip, ride=None):
    def body(u_ref, b_ref, c_ref, a_ref, d_ref, y_ref, h_ref):
        for i in range(SEQ // MM_ROWS):
            rows = pl.ds(i * MM_ROWS, MM_ROWS)
            h_ref[rows, :] = _dot(u_ref[rows, :].astype(BF16), b_ref[...])
        steps, carry_in, _ = _scan_consts(a_ref, conj=False, reverse=False)

        def tile(k, carry):
            rows = pl.ds(pl.multiple_of(k * SCAN_ROWS, SCAN_ROWS), SCAN_ROWS)
            xr, xi = _scan_tile(h_ref[rows, :CHUNK_STATES], h_ref[rows, CHUNK_STATES:], carry[0], carry[1], steps, carry_in, False)
            h_ref[rows, :CHUNK_STATES] = xr
            h_ref[rows, CHUNK_STATES:] = xi
            return xr, xi

        zero = jnp.zeros((SCAN_ROWS, CHUNK_STATES), F32)
        lax.fori_loop(0, SEQ // SCAN_ROWS, tile, (zero, zero), unroll=4)
        for i in range(SEQ // MM_ROWS):
            rows = pl.ds(i * MM_ROWS, MM_ROWS)
            y_ref[rows, :] = _dot(h_ref[rows, :].astype(BF16), c_ref[...]) + d_ref[...] * u_ref[rows, :]

    return _call(
        body, "ssm_fwd", (SSM_CHUNKS,),
        [pl.BlockSpec((SEQ, 128), lambda c: (0, U_COL + c)),
         pl.BlockSpec((None, 128, 2 * CHUNK_STATES), lambda c: (c, 0, 0)),
         pl.BlockSpec((None, 2 * CHUNK_STATES, 128), lambda c: (c, 0, 0)),
         pl.BlockSpec((None, 1, 2 * CHUNK_STATES), lambda c: (c, 0, 0)),
         pl.BlockSpec((1, 128), lambda c: (0, c))],
        [pl.BlockSpec((SEQ, 128), lambda c: (0, c)), pl.BlockSpec((SEQ, 2 * CHUNK_STATES), lambda c: (0, c))],
        [_sds((SEQ, SSM_WIDTH), F32), _sds((SEQ, SSM_CHUNKS * 2 * CHUNK_STATES), F32)], [],
        [proj, bmat, cmat, a_chunks, d_skip], ride)


def _ssm_bwd(dys, proj, h, bmat, cmat, a_chunks, d_skip, dproj, ride=None):
    def body(dy_ref, u_ref, h_ref, b_ref, c_ref, a_ref, d_ref, dproj_in, du_ref, db_ref, dc_ref, da_ref, dd_ref, g_ref):
        del dproj_in
        dsum = jnp.zeros((1, 128), F32)
        dcm = jnp.zeros((2 * CHUNK_STATES, 128), F32)
        for i in range(SEQ // MM_ROWS):
            rows = pl.ds(i * MM_ROWS, MM_ROWS)
            dy = dy_ref[rows, :]
            g_ref[rows, :] = _dot_nt(dy.astype(BF16), c_ref[...])
            dsum += jnp.sum(dy * u_ref[rows, :], axis=0, keepdims=True)
            dcm += _dot_tn(h_ref[rows, :].astype(BF16), dy.astype(BF16))
        dd_ref[...] = dsum
        dc_ref[...] = dcm
        steps, carry_in, _ = _scan_consts(a_ref, conj=True, reverse=True)
        first_row = lax.broadcasted_iota(jnp.int32, (SCAN_ROWS, CHUNK_STATES), 0) == 0
        n_tiles = SEQ // SCAN_ROWS

        def tile(j, carry):
            k = n_tiles - 1 - j
            rows = pl.ds(pl.multiple_of(k * SCAN_ROWS, SCAN_ROWS), SCAN_ROWS)
            before = pl.ds(pl.multiple_of(jnp.maximum(k - 1, 0) * SCAN_ROWS, SCAN_ROWS), SCAN_ROWS)
            gr, gi = _scan_tile(g_ref[rows, :CHUNK_STATES], g_ref[rows, CHUNK_STATES:], carry[0], carry[1], steps, carry_in, True)
            g_ref[rows, :CHUNK_STATES] = gr
            g_ref[rows, CHUNK_STATES:] = gi
            has_before = jnp.where(k > 0, 1.0, 0.0)
            hr = jnp.where(first_row, pltpu.roll(h_ref[before, :CHUNK_STATES], 1, axis=0) * has_before,
                           pltpu.roll(h_ref[rows, :CHUNK_STATES], 1, axis=0))
            hi = jnp.where(first_row, pltpu.roll(h_ref[before, CHUNK_STATES:], 1, axis=0) * has_before,
                           pltpu.roll(h_ref[rows, CHUNK_STATES:], 1, axis=0))
            return gr, gi, carry[2] + hr * gr + hi * gi, carry[3] + hr * gi - hi * gr

        zero = jnp.zeros((SCAN_ROWS, CHUNK_STATES), F32)
        _, _, sar, sai = lax.fori_loop(0, n_tiles, tile, (zero, zero, zero, zero), unroll=4)
        da_ref[:, :CHUNK_STATES] = jnp.sum(sar, axis=0, keepdims=True)
        da_ref[:, CHUNK_STATES:] = jnp.sum(sai, axis=0, keepdims=True)
        dbm = jnp.zeros((128, 2 * CHUNK_STATES), F32)
        for i in range(SEQ // MM_ROWS):
            rows = pl.ds(i * MM_ROWS, MM_ROWS)
            g = g_ref[rows, :].astype(BF16)
            du_ref[rows, :] = (_dot_nt(g, b_ref[...]) + d_ref[...] * dy_ref[rows, :]).astype(BF16)
            dbm += _dot_tn(u_ref[rows, :].astype(BF16), g)
        db_ref[...] = dbm

    chunk_col = pl.BlockSpec((SEQ, 128), lambda c: (0, c))
    return _call(
        body, "ssm_bwd", (SSM_CHUNKS,),
        [chunk_col,
         pl.BlockSpec((SEQ, 128), lambda c: (0, U_COL + c)),
         pl.BlockSpec((SEQ, 2 * CHUNK_STATES), lambda c: (0, c)),
         pl.BlockSpec((None, 128, 2 * CHUNK_STATES), lambda c: (c, 0, 0)),
         pl.BlockSpec((None, 2 * CHUNK_STATES, 128), lambda c: (c, 0, 0)),
         pl.BlockSpec((None, 1, 2 * CHUNK_STATES), lambda c: (c, 0, 0)),
         pl.BlockSpec((1, 128), lambda c: (0, c)), ANY],
        [pl.BlockSpec((SEQ, 128), lambda c: (0, U_COL + c)),
         pl.BlockSpec((None, 128, 2 * CHUNK_STATES), lambda c: (c, 0, 0)),
         pl.BlockSpec((None, 2 * CHUNK_STATES, 128), lambda c: (c, 0, 0)),
         pl.BlockSpec((None, 1, 2 * CHUNK_STATES), lambda c: (c, 0, 0)),
         pl.BlockSpec((1, 128), lambda c: (0, c))],
        [_sds((SEQ, IN_WIDTH), BF16), _sds((SSM_CHUNKS, 128, 2 * CHUNK_STATES), F32),
         _sds((SSM_CHUNKS, 2 * CHUNK_STATES, 128), F32), _sds((SSM_CHUNKS, 1, 2 * CHUNK_STATES), F32), _sds((1, SSM_WIDTH), F32)],
        [pltpu.VMEM((SEQ, 2 * CHUNK_STATES), F32)], [dys, proj, h, bmat, cmat, a_chunks, d_skip, dproj], ride, aliases={7: 0})


def _ssm_tables(abar_re, abar_im, bbar_re_t, bbar_im_t, c_re, c_im):
    bmat = _block_diag(bbar_re_t, bbar_im_t, 1.0, True).astype(BF16)
    cmat = _block_diag(c_re, c_im, -1.0, False).astype(BF16)
    a_chunks = jnp.concatenate([abar_re.reshape(SSM_CHUNKS, 1, CHUNK_STATES), abar_im.reshape(SSM_CHUNKS, 1, CHUNK_STATES)], axis=2)
    return bmat, cmat, a_chunks


GL_COL = (3 * QKV_WIDTH + SSM_WIDTH) // D_MODEL
GELU_C = math.sqrt(2.0 / math.pi)
GELU_A = 0.044715


def _sds(shape, dtype):
    return jax.ShapeDtypeStruct(shape, dtype)


def _gelu(x):
    t = jnp.tanh(GELU_C * (x + GELU_A * x * x * x))
    return 0.5 * x * (1.0 + t), t


def _gelu_grad(x, t):
    return 0.5 * (1.0 + t) + 0.5 * x * (1.0 - t * t) * GELU_C * (1.0 + 3.0 * GELU_A * x * x)


def _layer_norm(r, g, b):
    mu = jnp.mean(r, axis=-1, keepdims=True)
    xc = r - mu
    rstd = lax.rsqrt(jnp.mean(xc * xc, axis=-1, keepdims=True) + LN_EPS)
    xhat = xc * rstd
    return xhat * g + b, xhat, rstd


def _layer_norm_bwd(dy, xhat, rstd, g):
    dxhat = dy * g
    m1 = jnp.mean(dxhat, axis=-1, keepdims=True)
    m2 = jnp.mean(dxhat * xhat, axis=-1, keepdims=True)
    return rstd * (dxhat - m1 - xhat * m2)


def _proj(x, w_in, ride=None):
    tm, tn = 1024, 1792

    def body(x_ref, w_ref, o_ref):
        o_ref[...] = _dot(x_ref[...].astype(BF16), _side_by_side(w_ref))

    return _call(
        body, "proj", (SEQ // tm, IN_WIDTH // tn),
        [pl.BlockSpec((tm, D_MODEL), lambda i, j: (i, 0)), pl.BlockSpec((2, D_MODEL, tn // 2), lambda i, j: (j, 0, 0))],
        [pl.BlockSpec((tm, tn), lambda i, j: (i, j))], [_sds((SEQ, IN_WIDTH), F32)], [], [x, w_in], ride)


def _row_spec(tm, width, col=0):
    return pl.BlockSpec((tm, width), lambda i, col=col: (i, col))


def _full_spec(shape):
    return pl.BlockSpec(shape, lambda i: (0,) * len(shape))


def _weight_spec(shape):
    return pl.BlockSpec(shape, lambda i: (0,) * len(shape), pipeline_mode=pl.Buffered(1))


def _mixer_out(attn, ys, proj, x, w_ab, w_sb, w_glu, w_out, b_gate, ln_g, ln_b):
    tm = 512

    def body(attn_ref, ys_ref, gl0_ref, gl1_ref, x_ref, wab_ref, wsb_ref, wglu_ref, wout_ref, bg_ref, g_ref, b_ref,
             h_ref, xhat_ref, rstd_ref, glu_ref, ya_ref, yssm_ref):
        gy, _ = _gelu(ys_ref[...])
        glu = _dot(gy.astype(BF16), _side_by_side(wglu_ref))
        glu_ref[...] = glu
        y_s = glu[:, :SSM_WIDTH] * jax.nn.sigmoid(glu[:, SSM_WIDTH:])
        y_ssm = _dot(y_s.astype(BF16), _side_by_side(wsb_ref))
        y_attn = _dot(attn_ref[...].astype(BF16), _side_by_side(wab_ref))
        ya_ref[...] = y_attn
        yssm_ref[...] = y_ssm
        g0 = jax.nn.sigmoid(gl0_ref[...] + _side_by_side(bg_ref, 0))
        g1 = jax.nn.sigmoid(gl1_ref[...] + _side_by_side(bg_ref, 1))
        mixed = g0 * y_attn + g1 * y_ssm
        r1 = DN_ALPHA * x_ref[...] + _dot(mixed.astype(BF16), wout_ref[...])
        h, xhat, rstd = _layer_norm(r1, g_ref[...], b_ref[...])
        h_ref[...] = h
        xhat_ref[...] = xhat
        rstd_ref[...] = jnp.broadcast_to(rstd, (tm, 128))

    wide = _sds((SEQ, D_MODEL), F32)
    return _pallas_call(
        body, name="mixer_out", grid=(SEQ // tm,),
        in_specs=[_row_spec(tm, ATTN_WIDTH), _row_spec(tm, SSM_WIDTH), _row_spec(tm, D_MODEL, GL_COL), _row_spec(tm, D_MODEL, GL_COL + 1),
                  _row_spec(tm, D_MODEL), _weight_spec((N_DEV, ATTN_WIDTH, 128)), _weight_spec((N_DEV, SSM_WIDTH, 128)),
                  _weight_spec((N_DEV, SSM_WIDTH, 128)), _weight_spec((D_MODEL, D_MODEL)), _full_spec((N_DEV, 2, 128)),
                  _full_spec((1, D_MODEL)), _full_spec((1, D_MODEL))],
        out_specs=(_row_spec(tm, D_MODEL), _row_spec(tm, D_MODEL), _row_spec(tm, 128), _row_spec(tm, D_MODEL),
                   _row_spec(tm, D_MODEL), _row_spec(tm, D_MODEL)),
        out_shape=(wide, wide, _sds((SEQ, 128), F32), wide, wide, wide),
        compiler_params=_cparams(dimension_semantics=("arbitrary",)),
    )(attn, ys, proj, proj, x, w_ab, w_sb, w_glu, w_out, b_gate, ln_g, ln_b)


def _ff_up(h, w_gate, w_up):
    tm, tn = 1024, 768

    def body(h_ref, wg_ref, wu_ref, a_ref, b_ref, f_ref):
        hb = h_ref[...].astype(BF16)
        a, b = _dot(hb, _side_by_side(wg_ref)), _dot(hb, _side_by_side(wu_ref))
        a_ref[...] = a.astype(BF16)
        b_ref[...] = b.astype(BF16)
        f_ref[...] = (a * jax.nn.sigmoid(a) * b).astype(BF16)

    tile = pl.BlockSpec((tm, tn), lambda i, j: (i, j))
    wtile = pl.BlockSpec((tn // FF_PAD, D_MODEL, FF_PAD), lambda i, j: (j, 0, 0))
    out = _sds((SEQ, D_FF_PAD), BF16)
    return _pallas_call(
        body, name="ff_up", grid=(SEQ // tm, D_FF_PAD // tn),
        in_specs=[pl.BlockSpec((tm, D_MODEL), lambda i, j: (i, 0)), wtile, wtile],
        out_specs=(tile, tile, tile), out_shape=(out, out, out),
        compiler_params=_cparams(dimension_semantics=("arbitrary", "arbitrary")),
    )(h, w_gate, w_up)


def _ff_down_loss(f, w_down, h, target, ln_g, ln_b):
    tm = 512

    def body(f_ref, w_ref, h_ref, t_ref, g_ref, b_ref, dr_ref, dg_ref, db_ref, loss_ref):
        @pl.when(pl.program_id(0) == 0)
        def _():
            dg_ref[...] = jnp.zeros_like(dg_ref)
            db_ref[...] = jnp.zeros_like(db_ref)
            loss_ref[...] = jnp.zeros_like(loss_ref)

        r2 = DN_ALPHA * h_ref[...] + _dot(f_ref[...], w_ref[...])
        g = g_ref[...]
        out, xhat, rstd = _layer_norm(r2, g, b_ref[...])
        err = out - t_ref[...]
        loss_ref[...] += 0.5 * jnp.sum(jnp.mean(err * err, axis=-1, keepdims=True), axis=0, keepdims=True)
        dout = err * (1.0 / D_MODEL)
        dg_ref[...] += jnp.sum(dout * xhat, axis=0, keepdims=True)
        db_ref[...] += jnp.sum(dout, axis=0, keepdims=True)
        dr_ref[...] = _layer_norm_bwd(dout, xhat, rstd, g)

    vec = _sds((1, D_MODEL), F32)
    return _pallas_call(
        body, name="ff_down_loss", grid=(SEQ // tm,),
        in_specs=[_row_spec(tm, D_FF_PAD), _weight_spec((D_FF_PAD, D_MODEL)), _row_spec(tm, D_MODEL), _row_spec(tm, D_MODEL),
                  _full_spec((1, D_MODEL)), _full_spec((1, D_MODEL))],
        out_specs=(_row_spec(tm, D_MODEL), _full_spec((1, D_MODEL)), _full_spec((1, D_MODEL)), _full_spec((1, 128))),
        out_shape=(_sds((SEQ, D_MODEL), F32), vec, vec, _sds((1, 128), F32)),
        compiler_params=_cparams(dimension_semantics=("arbitrary",)),
    )(f, w_down, h, target, ln_g, ln_b)


def _ff_down_bwd(dr2, w_down, a, b):
    tm, tn = 1024, 768

    def body(dr_ref, w_ref, a_ref, b_ref, da_ref, db_ref):
        df = _dot_nt(dr_ref[...].astype(BF16), w_ref[...])
        av, bv = a_ref[...].astype(F32), b_ref[...].astype(F32)
        sg = jax.nn.sigmoid(av)
        da_ref[...] = (df * bv * sg * (1.0 + av * (1.0 - sg))).astype(BF16)
        db_ref[...] = (df * av * sg).astype(BF16)

    tile = pl.BlockSpec((tm, tn), lambda i, j: (i, j))
    out = _sds((SEQ, D_FF_PAD), BF16)
    return _pallas_call(
        body, name="ff_down_bwd", grid=(SEQ // tm, D_FF_PAD // tn),
        in_specs=[pl.BlockSpec((tm, D_MODEL), lambda i, j: (i, 0)), pl.BlockSpec((tn, D_MODEL), lambda i, j: (j, 0)), tile, tile],
        out_specs=(tile, tile), out_shape=(out, out),
        compiler_params=_cparams(dimension_semantics=("arbitrary", "arbitrary")),
    )(dr2, w_down, a, b)


def _ff_up_bwd(da, db, w_gate, w_up, dr2, xhat1, rstd1, ln_g, ride=None):
    tm, tk = 1024, 768
    nk = D_FF_PAD // tk

    def body(da_ref, db_ref, wg_ref, wu_ref, dr2_ref, xhat_ref, rstd_ref, g_ref, dr1_ref, dg_ref, dbias_ref, acc):
        i, k = pl.program_id(0), pl.program_id(1)

        @pl.when(jnp.logical_and(i == 0, k == 0))
        def _():
            dg_ref[...] = jnp.zeros_like(dg_ref)
            dbias_ref[...] = jnp.zeros_like(dbias_ref)

        part = _dot_nt(da_ref[...], _side_by_side(wg_ref)) + _dot_nt(db_ref[...], _side_by_side(wu_ref))

        @pl.when(k == 0)
        def _():
            acc[...] = part

        @pl.when(k > 0)
        def _():
            acc[...] += part

        @pl.when(k == nk - 1)
        def _():
            dh = DN_ALPHA * dr2_ref[...] + acc[...]
            xhat = xhat_ref[...]
            dg_ref[...] += jnp.sum(dh * xhat, axis=0, keepdims=True)
            dbias_ref[...] += jnp.sum(dh, axis=0, keepdims=True)
            rstd = jnp.max(rstd_ref[...], axis=1, keepdims=True)
            dr1_ref[...] = _layer_norm_bwd(dh, xhat, rstd, g_ref[...])

    hid = pl.BlockSpec((tm, tk), lambda i, k: (i, k))
    wtile = pl.BlockSpec((tk // FF_PAD, D_MODEL, FF_PAD), lambda i, k: (k, 0, 0))
    row = pl.BlockSpec((tm, D_MODEL), lambda i, k: (i, 0))
    vec = pl.BlockSpec((1, D_MODEL), lambda i, k: (0, 0))
    return _call(
        body, "ff_up_bwd", (SEQ // tm, nk),
        [hid, hid, wtile, wtile, row, row, pl.BlockSpec((tm, 128), lambda i, k: (i, 0)), vec],
        [row, vec, vec], [_sds((SEQ, D_MODEL), F32), _sds((1, D_MODEL), F32), _sds((1, D_MODEL), F32)],
        [pltpu.VMEM((tm, D_MODEL), F32)], [da, db, w_gate, w_up, dr2, xhat1, rstd1, ln_g], ride)


def _mixer_bwd(dr1, proj, y_attn, y_ssm, glu, ys, w_ab, w_sb, w_glu, w_out, b_gate):
    tm = 256

    def body(dr1_ref, gl0_ref, gl1_ref, ya_ref, yssm_ref, glu_ref, ys_ref, wab_ref, wsb_ref, wglu_ref, wout_ref, bg_ref,
             dya_ref, dyssm_ref, dgl_ref, dattn_ref, dglu_ref, dys_ref, mixed_ref, ysb_ref, gy_ref, dbg_ref):
        @pl.when(pl.program_id(0) == 0)
        def _():
            dbg_ref[...] = jnp.zeros_like(dbg_ref)

        dmixed = _dot_nt(dr1_ref[...].astype(BF16), wout_ref[...])
        g0 = jax.nn.sigmoid(gl0_ref[...] + _side_by_side(bg_ref, 0))
        g1 = jax.nn.sigmoid(gl1_ref[...] + _side_by_side(bg_ref, 1))
        y_attn, y_ssm = ya_ref[...], yssm_ref[...]
        mixed_ref[...] = (g0 * y_attn + g1 * y_ssm).astype(BF16)
        dya = (dmixed * g0).astype(BF16)
        dyssm = (dmixed * g1).astype(BF16)
        dya_ref[...] = dya
        dyssm_ref[...] = dyssm
        dgl0 = dmixed * y_attn * g0 * (1.0 - g0)
        dgl1 = dmixed * y_ssm * g1 * (1.0 - g1)
        dgl_ref[:, :GL_COL * D_MODEL] = jnp.zeros((tm, GL_COL * D_MODEL), BF16)
        dgl_ref[:, GL_COL * D_MODEL:(GL_COL + 1) * D_MODEL] = dgl0.astype(BF16)
        dgl_ref[:, (GL_COL + 1) * D_MODEL:] = dgl1.astype(BF16)
        dbg_ref[:, :D_MODEL] += jnp.sum(dgl0, axis=0, keepdims=True)
        dbg_ref[:, D_MODEL:] += jnp.sum(dgl1, axis=0, keepdims=True)
        dattn_ref[...] = _dot_nt(dya, _side_by_side(wab_ref))
        dy_s = _dot_nt(dyssm, _side_by_side(wsb_ref))
        glu = glu_ref[...]
        glu1, sg = glu[:, :SSM_WIDTH], jax.nn.sigmoid(glu[:, SSM_WIDTH:])
        ysb_ref[...] = (glu1 * sg).astype(BF16)
        dglu1 = (dy_s * sg).astype(BF16)
        dglu2 = (dy_s * glu1 * sg * (1.0 - sg)).astype(BF16)
        dglu_ref[:, :SSM_WIDTH] = dglu1
        dglu_ref[:, SSM_WIDTH:] = dglu2
        dgy = _dot_nt(jnp.concatenate([dglu1, dglu2], axis=1), _side_by_side(wglu_ref))
        ys = ys_ref[...]
        gy, t = _gelu(ys)
        gy_ref[...] = gy.astype(BF16)
        dys_ref[...] = dgy * _gelu_grad(ys, t)

    wide_b, half_b = _sds((SEQ, D_MODEL), BF16), _sds((SEQ, SSM_WIDTH), BF16)
    half_f = _sds((SEQ, SSM_WIDTH), F32)
    return _pallas_call(
        body, name="mixer_bwd", grid=(SEQ // tm,),
        in_specs=[_row_spec(tm, D_MODEL), _row_spec(tm, D_MODEL, GL_COL), _row_spec(tm, D_MODEL, GL_COL + 1), _row_spec(tm, D_MODEL),
                  _row_spec(tm, D_MODEL), _row_spec(tm, D_MODEL), _row_spec(tm, SSM_WIDTH), _full_spec((N_DEV, ATTN_WIDTH, 128)),
                  _full_spec((N_DEV, SSM_WIDTH, 128)), _full_spec((N_DEV, SSM_WIDTH, 128)), _full_spec((D_MODEL, D_MODEL)),
                  _full_spec((N_DEV, 2, 128))],
        out_specs=(_row_spec(tm, D_MODEL), _row_spec(tm, D_MODEL), _row_spec(tm, IN_WIDTH), _row_spec(tm, ATTN_WIDTH),
                   _row_spec(tm, D_MODEL), _row_spec(tm, SSM_WIDTH), _row_spec(tm, D_MODEL), _row_spec(tm, SSM_WIDTH),
                   _row_spec(tm, SSM_WIDTH), _full_spec((1, 2 * D_MODEL))),
        out_shape=(wide_b, wide_b, _sds((SEQ, IN_WIDTH), BF16), half_f, wide_b, half_f, wide_b, half_b, half_b,
                   _sds((1, 2 * D_MODEL), F32)),
        compiler_params=_cparams(dimension_semantics=("arbitrary",)),
    )(dr1, proj, proj, y_attn, y_ssm, glu, ys, w_ab, w_sb, w_glu, w_out, b_gate)


def _grad_x(dproj, w_in, dr1, ride=None):
    tm, tk = 1024, 1792
    nk = IN_WIDTH // tk

    def body(dp_ref, w_ref, dr1_ref, o_ref, acc):
        k = pl.program_id(1)
        part = _dot_nt(dp_ref[...], _side_by_side(w_ref))

        @pl.when(k == 0)
        def _():
            acc[...] = part

        @pl.when(k > 0)
        def _():
            acc[...] += part

        @pl.when(k == nk - 1)
        def _():
            o_ref[...] = DN_ALPHA * dr1_ref[...] + acc[...]

    row = pl.BlockSpec((tm, D_MODEL), lambda i, k: (i, 0))
    return _call(
        body, "grad_x", (SEQ // tm, nk),
        [pl.BlockSpec((tm, tk), lambda i, k: (i, k)), pl.BlockSpec((2, D_MODEL, tk // 2), lambda i, k: (k, 0, 0)), row],
        [row], [_sds((SEQ, D_MODEL), F32)], [pltpu.VMEM((tm, D_MODEL), F32)], [dproj, w_in, dr1], ride)


def _weight_grad(a, b, name, shard_cols=None, ride=None):
    k, n = a.shape[1], b.shape[1]
    tk = min(k, 512) if shard_cols else k // N_DEV
    tn = n // 4 if shard_cols else min(n, 1024)

    def body(a_ref, b_ref, o_ref):
        grad = _dot_tn(a_ref[...].astype(BF16), b_ref[...].astype(BF16))
        if shard_cols:
            o_ref[0] = grad[:, :shard_cols].astype(BF16)
            o_ref[1] = grad[:, shard_cols:].astype(BF16)
        else:
            o_ref[...] = grad.astype(BF16)

    if shard_cols:
        out_spec = pl.BlockSpec((2, None, tk, shard_cols), lambda kk, j: (0, j, kk, 0))
        out_shape = _sds((2, 4, k, shard_cols), BF16)
    else:
        out_spec = pl.BlockSpec((None, None, tk, tn), lambda kk, j: (kk % 2, kk // 2, 0, j))
        out_shape = _sds((2, 4, tk, n), BF16)
    out = _call(body, name, (k // tk, n // tn),
                [pl.BlockSpec((SEQ, tk), lambda kk, j: (0, kk)), pl.BlockSpec((SEQ, tn), lambda kk, j: (0, j))],
                [out_spec], [out_shape], [], [a, b], ride)
    return out[0] if ride is None else out


MESH = pl.DeviceIdType.MESH
ANY = pl.BlockSpec(memory_space=pl.ANY)


def _place():
    return lax.axis_index("x"), lax.axis_index("y"), lax.axis_index("c")


def _other_chips(x, y):
    return [(1 - x, y), (x, 1 - y), (1 - x, 1 - y)]


class _Ride:
    def __init__(self, operands, results, aliases, sems, start, wait):
        self.operands, self.results, self.aliases, self.sems = list(operands), list(results), dict(aliases), list(sems)
        self.start, self.wait = start, wait

    def __add__(self, other):
        n_in, n_out, n_sem = len(self.operands), len(self.results), len(self.sems)

        def both(which):
            def run(ins, outs, sems):
                getattr(self, which)(ins[:n_in], outs[:n_out], sems[:n_sem])
                getattr(other, which)(ins[n_in:], outs[n_out:], sems[n_sem:])
            return run

        aliases = {**self.aliases, **{n_in + i: n_out + j for i, j in other.aliases.items()}}
        return _Ride(self.operands + other.operands, self.results + other.results, aliases, self.sems + other.sems,
                     both("start"), both("wait"))


def _call(body, name, grid, in_specs, out_specs, out_shape, scratch_shapes, operands, ride=None, aliases=None):
    in_specs, out_specs, out_shape = list(in_specs), list(out_specs), list(out_shape)
    scratch_shapes, operands, aliases = list(scratch_shapes), list(operands), dict(aliases or {})
    kernel_body = body
    if ride is not None:
        n_in, n_out, n_scr, r_in, r_out = len(in_specs), len(out_specs), len(scratch_shapes), len(ride.operands), len(ride.results)

        def kernel_body(*refs):
            out0, scr0 = n_in + r_in, n_in + r_in + n_out + r_out
            ride_refs = (refs[n_in:out0], refs[out0 + n_out:scr0], refs[scr0 + n_scr:])
            ids = [pl.program_id(i) for i in range(len(grid))]
            first = functools.reduce(jnp.logical_and, [i == 0 for i in ids])
            last = functools.reduce(jnp.logical_and, [i == g - 1 for i, g in zip(ids, grid)])

            @pl.when(first)
            def _():
                ride.start(*ride_refs)

            body(*refs[:n_in], *refs[out0:out0 + n_out], *refs[scr0:scr0 + n_scr])

            @pl.when(last)
            def _():
                ride.wait(*ride_refs)

        aliases.update({n_in + i: n_out + j for i, j in ride.aliases.items()})
        in_specs += [ANY] * r_in
        out_specs += [ANY] * r_out
        out_shape += ride.results
        scratch_shapes += ride.sems
        operands += ride.operands
    return _pallas_call(
        kernel_body, name=name, grid=grid, in_specs=in_specs, out_specs=out_specs, out_shape=out_shape,
        scratch_shapes=scratch_shapes, input_output_aliases=aliases,
        compiler_params=_cparams(dimension_semantics=("arbitrary",) * len(grid)),
    )(*operands)


def _gather_first_level(shards):
    n = len(shards)

    def copies(ins, outs, sems, landed):
        send_sems, recv_sems, local_sems = sems
        x, y, c = _place()
        peers = [(x, y, 1 - c)] + [(px, py, c) for px, py in _other_chips(x, y)]

        def row(peer):
            return 4 * x + 2 * y + c if not landed else 4 * peer[0] + 2 * peer[1] + peer[2]

        local = [pltpu.make_async_copy(ins[a], outs[a].at[4 * x + 2 * y + c], local_sems.at[a]) for a in range(n)]
        remote = [pltpu.make_async_remote_copy(
            src_ref=ins[a], dst_ref=outs[a].at[row(peer)], send_sem=send_sems.at[a, k], recv_sem=recv_sems.at[a, k],
            device_id=peer, device_id_type=MESH) for a in range(n) for k, peer in enumerate(peers)]
        return local, remote

    def start(ins, outs, sems):
        local, remote = copies(ins, outs, sems, False)
        for cp in local + remote:
            cp.start()

    def wait(ins, outs, sems):
        local, sent = copies(ins, outs, sems, False)
        for cp in copies(ins, outs, sems, True)[1]:
            cp.wait_recv()
        for cp in sent:
            cp.wait_send()
        for cp in local:
            cp.wait()

    return _Ride(shards, [_sds((N_DEV,) + s.shape, s.dtype) for s in shards], {},
                 [pltpu.SemaphoreType.DMA((n, 4)), pltpu.SemaphoreType.DMA((n, 4)), pltpu.SemaphoreType.DMA((n,))], start, wait)


def _gather_second_level(buffers):
    n = len(buffers)

    def copies(outs, sems, core):
        send_sems, recv_sems = sems
        x, y, c = _place()
        return [pltpu.make_async_remote_copy(
            src_ref=outs[a].at[4 * px + 2 * py + core], dst_ref=outs[a].at[4 * px + 2 * py + core], send_sem=send_sems.at[a, j],
            recv_sem=recv_sems.at[a, j], device_id=(x, y, 1 - c), device_id_type=MESH)
            for a in range(n) for j, (px, py) in enumerate(_other_chips(x, y))]

    def start(ins, outs, sems):
        for cp in copies(outs, sems, lax.axis_index("c")):
            cp.start()

    def wait(ins, outs, sems):
        for cp in copies(outs, sems, 1 - lax.axis_index("c")):
            cp.wait_recv()
        for cp in copies(outs, sems, lax.axis_index("c")):
            cp.wait_send()

    return _Ride(buffers, [_sds(b.shape, b.dtype) for b in buffers], {i: i for i in range(n)},
                 [pltpu.SemaphoreType.DMA((n, 3)), pltpu.SemaphoreType.DMA((n, 3))], start, wait)


def _sibling_swap_ride(grads):
    n = len(grads)

    def copies(ins, outs, sems):
        x, y, c = _place()
        return [pltpu.make_async_remote_copy(
            src_ref=ins[a].at[1 - c], dst_ref=outs[a], send_sem=sems[0].at[a], recv_sem=sems[1].at[a],
            device_id=(x, y, 1 - c), device_id_type=MESH) for a in range(n)]

    def start(ins, outs, sems):
        for cp in copies(ins, outs, sems):
            cp.start()

    def wait(ins, outs, sems):
        for cp in copies(ins, outs, sems):
            cp.wait()

    return _Ride(grads, [_sds(g.shape[1:], g.dtype) for g in grads], {},
                 [pltpu.SemaphoreType.DMA((n,)), pltpu.SemaphoreType.DMA((n,))], start, wait)


def _chip_swap_ride(sums):
    n = len(sums)

    def copies(ins, outs, sems, landed):
        send_sems, recv_sems, local_sems = sems
        x, y, c = _place()
        mine = 2 * x + y
        local = [pltpu.make_async_copy(ins[a].at[mine], outs[a].at[mine], local_sems.at[a]) for a in range(n)]
        remote = [pltpu.make_async_remote_copy(
            src_ref=ins[a].at[2 * px + py], dst_ref=outs[a].at[2 * px + py if landed else mine], send_sem=send_sems.at[a, j],
            recv_sem=recv_sems.at[a, j], device_id=(px, py, c), device_id_type=MESH)
            for a in range(n) for j, (px, py) in enumerate(_other_chips(x, y))]
        return local, remote

    def start(ins, outs, sems):
        local, remote = copies(ins, outs, sems, False)
        for cp in local + remote:
            cp.start()

    def wait(ins, outs, sems):
        local, sent = copies(ins, outs, sems, False)
        for cp in copies(ins, outs, sems, True)[1]:
            cp.wait_recv()
        for cp in sent:
            cp.wait_send()
        for cp in local:
            cp.wait()

    return _Ride(sums, [_sds(s.shape, s.dtype) for s in sums], {},
                 [pltpu.SemaphoreType.DMA((n, 3)), pltpu.SemaphoreType.DMA((n, 3)), pltpu.SemaphoreType.DMA((n,))], start, wait)


def _send_buffers(shards, name):
    n = len(shards)

    def body(*refs):
        for (w, transposed, rows, cols), w_ref, o_ref in zip(shards, refs[:n], refs[n:]):
            if transposed:
                c, r = w.shape
                padded = jnp.concatenate([w_ref[...], jnp.zeros((cols - c, r), F32)], axis=0) if cols > c else w_ref[...]
                o_ref[...] = padded.T.astype(BF16)
            else:
                r, c = w.shape
                if (r, c) != (rows, cols):
                    o_ref[...] = jnp.zeros((rows, cols), BF16)
                o_ref[:r, :c] = w_ref[...].astype(BF16)

    return _pallas_call(body, name=name, out_shape=[_sds((rows, cols), BF16) for _, _, rows, cols in shards])(
        *[w for w, _, _, _ in shards])


def _all_gather(shards, name):
    n = len(shards)

    def body(*refs):
        ins, outs = refs[:n], refs[n:2 * n]
        send_sems, recv_sems, local_sems = refs[2 * n:]
        x, y, c = _place()
        me, sibling = (x, y, c), (x, y, 1 - c)
        chips = _other_chips(x, y)

        def slot(a, px, py, pc):
            return outs[a].at[4 * px + 2 * py + pc]

        def copy(a, k, block, to, src=None):
            return pltpu.make_async_remote_copy(
                src_ref=slot(a, *block) if src is None else src, dst_ref=slot(a, *block),
                send_sem=send_sems.at[a, k], recv_sem=recv_sems.at[a, k], device_id=to, device_id_type=MESH)

        mine = [pltpu.make_async_copy(ins[a], slot(a, *me), local_sems.at[a]) for a in range(n)]
        for cp in mine:
            cp.start()
        first = []
        for a in range(n):
            first.append(copy(a, 0, me, sibling, src=ins[a]))
            first += [copy(a, 1 + j, me, (*chip, c), src=ins[a]) for j, chip in enumerate(chips)]
        for cp in first:
            cp.start()
        passed = []
        for j, chip in enumerate(chips):
            for a in range(n):
                copy(a, 1 + j, (*chip, c), me).wait_recv()
                onward = copy(a, 4 + j, (*chip, c), sibling)
                onward.start()
                passed.append(onward)
        for a in range(n):
            copy(a, 0, sibling, me).wait_recv()
            for j, chip in enumerate(chips):
                copy(a, 4 + j, (*chip, 1 - c), me).wait_recv()
        for cp in first + passed:
            cp.wait_send()
        for cp in mine:
            cp.wait()

    return _pallas_call(
        body, name=name, in_specs=[ANY] * n, out_specs=[ANY] * n,
        out_shape=[_sds((N_DEV,) + s.shape, s.dtype) for s in shards],
        scratch_shapes=[pltpu.SemaphoreType.DMA((n, 7)), pltpu.SemaphoreType.DMA((n, 7)), pltpu.SemaphoreType.DMA((n,))],
    )(*shards)


def _swap_with_sibling(grads, name):
    n = len(grads)

    def body(*refs):
        ins, outs = refs[:n], refs[n:2 * n]
        send_sems, recv_sems = refs[2 * n:]
        x, y, c = _place()
        copies = [pltpu.make_async_remote_copy(
            src_ref=ins[a].at[1 - c], dst_ref=outs[a], send_sem=send_sems.at[a], recv_sem=recv_sems.at[a],
            device_id=(x, y, 1 - c), device_id_type=MESH) for a in range(n)]
        for cp in copies:
            cp.start()
        for cp in copies:
            cp.wait()

    return _pallas_call(
        body, name=name, in_specs=[ANY] * n, out_specs=[ANY] * n,
        out_shape=[_sds(g.shape[1:], g.dtype) for g in grads],
        scratch_shapes=[pltpu.SemaphoreType.DMA((n,)), pltpu.SemaphoreType.DMA((n,))],
    )(*grads)


def _pair_sums(gs, rs, core, name):
    n_arrays = len(gs)

    def body(core_ref, *refs):
        for g_ref, r_ref, o_ref in zip(refs[:n_arrays], refs[n_arrays:2 * n_arrays], refs[2 * n_arrays:]):
            o_ref[...] = (g_ref[...].astype(F32) + r_ref[...].astype(F32)).astype(o_ref.dtype)

    def own(g):
        return pl.BlockSpec((None, None) + g.shape[2:], lambda p, core_ref: (core_ref[0], p, 0, 0))

    def chip(g):
        return pl.BlockSpec((None,) + g.shape[2:], lambda p, core_ref: (p, 0, 0))

    return _pallas_call(
        body, name=name,
        grid_spec=pltpu.PrefetchScalarGridSpec(
            num_scalar_prefetch=1, grid=(4,), in_specs=[own(g) for g in gs] + [chip(g) for g in gs],
            out_specs=[chip(g) for g in gs]),
        out_shape=[_sds(g.shape[1:], g.dtype) for g in gs], compiler_params=_cparams(dimension_semantics=("arbitrary",)),
    )(core, *gs, *rs)


def _adamw_math(w, g, m, v):
    m = ADAM_B1 * m + (1.0 - ADAM_B1) * g
    v = ADAM_B2 * v + (1.0 - ADAM_B2) * (g * g)
    m_hat = m / (1.0 - ADAM_B1 ** ADAM_STEP)
    v_hat = v / (1.0 - ADAM_B2 ** ADAM_STEP)
    return -ADAM_LR * (m_hat / (jnp.sqrt(v_hat) + ADAM_EPS) + ADAM_WD * w), m, v


def _adamw(w, m, v, parts, name):
    r, c = w.shape
    tr = r if r <= 512 else 256
    n_parts, pr, pc = parts.shape
    assert r % tr == 0 and (tr == r or pr == r)

    def body(w_ref, m_ref, v_ref, p_ref, g_out, d_out, m_out, v_out):
        g = p_ref[0, :tr, :c].astype(F32)
        for p in range(1, n_parts):
            g = g + p_ref[p, :tr, :c].astype(F32)
        g_out[...] = g
        d_out[...], m_out[...], v_out[...] = _adamw_math(w_ref[...], g, m_ref[...], v_ref[...])

    tile = pl.BlockSpec((tr, c), lambda i: (i, 0))
    part_tile = pl.BlockSpec((n_parts, pr if tr == r else tr, pc), lambda i: (0, i, 0))
    out = _sds((r, c), F32)
    return _pallas_call(
        body, name=name, grid=(r // tr,), in_specs=[tile, tile, tile, part_tile], out_specs=(tile,) * 4,
        out_shape=(out,) * 4, compiler_params=_cparams(dimension_semantics=("arbitrary",)),
    )(w, m, v, parts)


def _adamw_transposed(w_t, m_t, v_t, parts, name):
    c, r = w_t.shape
    tr = 256
    n_parts, _, pc = parts.shape

    def body(w_ref, m_ref, v_ref, p_ref, g_out, d_out, m_out, v_out):
        g = p_ref[0].astype(F32)
        for p in range(1, n_parts):
            g = g + p_ref[p].astype(F32)
        g = g.T[:c]
        g_out[...] = g
        d_out[...], m_out[...], v_out[...] = _adamw_math(w_ref[...], g, m_ref[...], v_ref[...])

    tile = pl.BlockSpec((c, tr), lambda i: (0, i))
    out = _sds((c, r), F32)
    return _pallas_call(
        body, name=name, grid=(r // tr,), in_specs=[tile, tile, tile, pl.BlockSpec((n_parts, tr, pc), lambda i: (0, i, 0))],
        out_specs=(tile,) * 4, out_shape=(out,) * 4, compiler_params=_cparams(dimension_semantics=("arbitrary",)),
    )(w_t, m_t, v_t, parts)


SMALL = ("ssm_a_re", "ssm_a_im", "ssm_log_dt", "ssm_b_re", "ssm_b_im", "ssm_c_re", "ssm_c_im", "ssm_d",
         "ln1_g", "ln1_b", "ln2_g", "ln2_b")


def _pack_rows(arrays):
    rows = []
    for a in arrays:
        flat = a.reshape(-1)
        rows.append(jnp.pad(flat, (0, -flat.shape[0] % 128)).reshape(-1, 128))
    packed = jnp.concatenate(rows, axis=0)
    return jnp.pad(packed, ((0, -packed.shape[0] % 8), (0, 0)))


def _unpack_rows(packed, shapes):
    out, row = [], 0
    for shape in shapes:
        size = math.prod(shape)
        n_rows = -(-size // 128)
        out.append(packed[row:row + n_rows].reshape(-1)[:size].reshape(shape))
        row += n_rows
    return out


def _sum_devices(parts):
    def body(p_ref, o_ref):
        total = p_ref[0]
        for dev in range(1, N_DEV):
            total = total + p_ref[dev]
        o_ref[...] = total

    return _pallas_call(body, name="sum_devices", out_shape=_sds(parts.shape[1:], F32))(parts)


def _adamw_replicated(ws, ms, vs, gs):
    n = len(ws)

    def body(*refs):
        w_refs, m_refs, v_refs, g_refs, d_out, m_out, v_out = (refs[i * n:(i + 1) * n] for i in range(7))
        for i in range(n):
            d_out[i][...], m_out[i][...], v_out[i][...] = _adamw_math(w_refs[i][...], g_refs[i][...], m_refs[i][...], v_refs[i][...])

    out = _pallas_call(body, name="adamw_replicated", out_shape=[_sds(w.shape, F32) for w in ws] * 3,
                       compiler_params=_cparams())(*ws, *ms, *vs, *gs)
    return out[:n], out[n:2 * n], out[2 * n:]


def kernel(x, w_in, b_gate, w_attn_br, w_ssm_br, w_out, ssm_a_re, ssm_a_im, ssm_log_dt, ssm_b_re, ssm_b_im, ssm_c_re, ssm_c_im, ssm_d, w_glu, ln1_g, ln1_b, w_ff_gate, w_ff_up, w_ff_down, ln2_g, ln2_b, loss_target, m_w_in, m_b_gate, m_w_attn_br, m_w_ssm_br, m_w_out, m_ssm_a_re, m_ssm_a_im, m_ssm_log_dt, m_ssm_b_re, m_ssm_b_im, m_ssm_c_re, m_ssm_c_im, m_ssm_d, m_w_glu, m_ln1_g, m_ln1_b, m_w_ff_gate, m_w_ff_up, m_w_ff_down, m_ln2_g, m_ln2_b, v_w_in, v_b_gate, v_w_attn_br, v_w_ssm_br, v_w_out, v_ssm_a_re, v_ssm_a_im, v_ssm_log_dt, v_ssm_b_re, v_ssm_b_im, v_ssm_c_re, v_ssm_c_im, v_ssm_d, v_w_glu, v_ln1_g, v_ln1_b, v_w_ff_gate, v_w_ff_up, v_w_ff_down, v_ln2_g, v_ln2_b):
    given = dict(locals())
    x2, target = x[0], loss_target[0]
    core = lax.axis_index("c").astype(jnp.int32).reshape(1)

    sharded = ("w_in", "w_attn_br", "w_ssm_br", "w_glu", "w_ff_gate", "w_ff_up", "b_gate", "w_out", "w_ff_down")
    send_shape = dict(w_in=(D_MODEL, 896), w_attn_br=(ATTN_WIDTH, 128), w_ssm_br=(SSM_WIDTH, 128), w_glu=(SSM_WIDTH, 128),
                      w_out=(128, D_MODEL), w_ff_gate=(D_MODEL, FF_PAD), w_ff_up=(D_MODEL, FF_PAD), w_ff_down=(FF_PAD, D_MODEL))
    local = {k: given[k][0] for k in sharded}
    narrow = ("w_ff_gate", "w_ff_up")
    def to_send(k):
        return (local[k].T, True, *send_shape[k]) if k in narrow else (local[k], False, *send_shape[k])

    later = [k for k in sharded if k not in ("w_in", "b_gate")]
    sends = dict(zip(["w_in"] + later, _send_buffers([to_send("w_in")], "send_w_in")
                     + _send_buffers([to_send(k) for k in later], "send_weights")))
    sends["b_gate"] = local["b_gate"]
    mixer_weights = ("w_attn_br", "w_ssm_br", "w_glu", "b_gate", "w_out")
    ff_weights = ("w_ff_gate", "w_ff_up", "w_ff_down")
    wt = {}
    wt["w_in"], = _all_gather([sends["w_in"]], "gather_w_in")

    a_re, a_im, log_dt = ssm_a_re[0], ssm_a_im[0], ssm_log_dt[0].reshape(SSM_GROUPS, 1)
    b_re_t, b_im_t = ssm_b_re[0].transpose(0, 2, 1), ssm_b_im[0].transpose(0, 2, 1)
    abar_re, abar_im, e_re, e_im, bbar_re_t, bbar_im_t = _ssm_prep(a_re, a_im, log_dt, b_re_t, b_im_t)
    bmat, cmat, a_chunks = _ssm_tables(abar_re, abar_im, bbar_re_t, bbar_im_t, ssm_c_re[0], ssm_c_im[0])
    cos_t, sin_t = _rope_tables()

    proj, *partly = _proj(x2, wt["w_in"], _gather_first_level([sends[k] for k in mixer_weights]))
    attn, lse, *landed = _attn_fwd(proj, cos_t, sin_t,
                                   _gather_second_level(partly) + _gather_first_level([sends[k] for k in ff_weights]))
    wt.update(zip(mixer_weights, landed[:len(mixer_weights)]))
    ys, states, *landed = _ssm_fwd(proj, bmat, cmat, a_chunks, ssm_d, _gather_second_level(landed[len(mixer_weights):]))
    wt.update(zip(ff_weights, landed))
    wt["w_out"] = wt["w_out"].reshape(D_MODEL, D_MODEL)
    wt["w_ff_down"] = wt["w_ff_down"].reshape(D_FF_PAD, D_MODEL)
    b_gate_full = wt["b_gate"]
    h, xhat1, rstd1, glu, y_attn, y_ssm = _mixer_out(attn, ys, proj, x2, wt["w_attn_br"], wt["w_ssm_br"], wt["w_glu"],
                                                      wt["w_out"], b_gate_full, ln1_g, ln1_b)
    ff_a, ff_b, ff_f = _ff_up(h, wt["w_ff_gate"], wt["w_ff_up"])
    dr2, d_ln2_g, d_ln2_b, loss_lanes = _ff_down_loss(ff_f, wt["w_ff_down"], h, target, ln2_g, ln2_b)

    def pair_sums(names, contrib, from_sibling):
        return _pair_sums([contrib[k] for k in names], from_sibling, core, "pair_sums_" + names[0])

    d_a, d_b = _ff_down_bwd(dr2, wt["w_ff_down"], ff_a, ff_b)
    contrib = dict(w_ff_gate=_weight_grad(h, d_a, "wgrad_w_ff_gate", FF_PAD),
                   w_ff_up=_weight_grad(h, d_b, "wgrad_w_ff_up", FF_PAD),
                   w_ff_down=_weight_grad(ff_f, dr2, "wgrad_w_ff_down"))
    dr1, d_ln1_g, d_ln1_b, *from_sibling = _ff_up_bwd(
        d_a, d_b, wt["w_ff_gate"], wt["w_ff_up"], dr2, xhat1, rstd1, ln1_g, _sibling_swap_ride([contrib[k] for k in ff_weights]))
    ff_sums = pair_sums(ff_weights, contrib, from_sibling)

    d_ya, d_yssm, d_proj, d_attn, d_glu, d_ys, mixed, y_s, gy, d_bg = _mixer_bwd(
        dr1, proj, y_attn, y_ssm, glu, ys, wt["w_attn_br"], wt["w_ssm_br"], wt["w_glu"], wt["w_out"], b_gate_full)
    contrib.update(w_attn_br=_weight_grad(attn, d_ya, "wgrad_w_attn_br", 128),
                   w_ssm_br=_weight_grad(y_s, d_yssm, "wgrad_w_ssm_br", 128),
                   w_glu=_weight_grad(gy, d_glu, "wgrad_w_glu", 128),
                   w_out=_weight_grad(mixed, dr1, "wgrad_w_out"),
                   b_gate=d_bg.reshape(2, 4, 2, 128).transpose(2, 1, 0, 3))
    d_proj, *landed = _attn_bwd(proj, cos_t, sin_t, attn, lse, d_attn, d_proj,
                                _chip_swap_ride(ff_sums) + _sibling_swap_ride([contrib[k] for k in mixer_weights]))
    parts = dict(zip(ff_weights, landed[:len(ff_weights)]))
    mixer_sums = pair_sums(mixer_weights, contrib, landed[len(ff_weights):])
    d_proj, d_bmat, d_cmat, d_abar, d_skip, *landed = _ssm_bwd(d_ys, proj, states, bmat, cmat, a_chunks, ssm_d, d_proj,
                                                               _chip_swap_ride(mixer_sums))
    parts.update(zip(mixer_weights, landed))

    gbb_re_t, gbb_im_t = _block_diag_parts(d_bmat, True)
    gc_re, gc_im = _block_diag_parts(d_cmat, False)
    ga_re = d_abar[:, 0, :CHUNK_STATES].reshape(SSM_GROUPS, SSM_STATE)
    ga_im = d_abar[:, 0, CHUNK_STATES:].reshape(SSM_GROUPS, SSM_STATE)
    g_a_re, g_a_im, g_log_dt, g_b_re_t, g_b_im_t = _ssm_param_bwd(
        a_re, a_im, log_dt, b_re_t, b_im_t, abar_re, abar_im, e_re, e_im, ga_re, ga_im, gbb_re_t, gbb_im_t)
    mine = [g_a_re, g_a_im, g_log_dt, g_b_re_t, g_b_im_t, gc_re, -gc_im,
            d_skip, d_ln1_g, d_ln1_b, d_ln2_g, d_ln2_b]
    small_packed = _pack_rows(mine + [loss_lanes])

    contrib["w_in"], small_partly = _weight_grad(x2, d_proj, "wgrad_w_in", 896, _gather_first_level([small_packed]))
    w_in_sum = pair_sums(["w_in"], contrib, _swap_with_sibling([contrib["w_in"]], "swap_w_in_with_sibling"))
    grad_x, parts["w_in"], every = _grad_x(d_proj, wt["w_in"], dr1,
                                          _chip_swap_ride(w_in_sum) + _gather_second_level([small_partly]))

    grads, deltas, new_m, new_v = {}, {}, {}, {}
    for k in sharded:
        w2 = local[k]
        if k in narrow:
            out = _adamw_transposed(w2.T, given["m_" + k][0].T, given["v_" + k][0].T, parts[k], "adamw_" + k)
            out = [o.T for o in out]
        else:
            out = _adamw(w2, given["m_" + k][0], given["v_" + k][0], parts[k], "adamw_" + k)
        grads[k], deltas[k], new_m[k], new_v[k] = (o.reshape((1,) + w2.shape) for o in out)

    def held(k, a):
        return a.transpose(0, 1, 3, 2) if k in ("ssm_b_re", "ssm_b_im") else a

    *small_grads, loss_sum = _unpack_rows(_sum_devices(every), [held(k, given[k]).shape for k in SMALL] + [(1, 128)])
    small = _adamw_replicated([held(k, given[k]) for k in SMALL], [held(k, given["m_" + k]) for k in SMALL],
                              [held(k, given["v_" + k]) for k in SMALL], small_grads)
    for res, values in zip((grads, deltas, new_m, new_v), (small_grads,) + small):
        res.update((k, held(k, a)) for k, a in zip(SMALL, values))
    loss = loss_sum[0, 0]

    order = ("w_in", "b_gate", "w_attn_br", "w_ssm_br", "w_out", "ssm_a_re", "ssm_a_im", "ssm_log_dt", "ssm_b_re", "ssm_b_im",
             "ssm_c_re", "ssm_c_im", "ssm_d", "w_glu", "ln1_g", "ln1_b", "w_ff_gate", "w_ff_up", "w_ff_down", "ln2_g", "ln2_b")
    return (loss, grad_x[None], *[grads[k] for k in order], *[deltas[k] for k in order], *[new_m[k] for k in order],
            *[new_v[k] for k in order])
```

```python
import functools
import math

import jax
import jax.numpy as jnp
import numpy as np
from jax import lax
from jax.experimental import pallas as pl
from jax.experimental.pallas import tpu as pltpu

F32 = jnp.float32
BF16 = jnp.bfloat16

N_DEV = 8
SEQ = 2048
D_MODEL = 1024
HEAD_DIM = 64
ATTN_WIDTH = 512
QKV_WIDTH = 1536
SSM_WIDTH = 512
SSM_GROUPS = 32
SSM_GROUP = 16
SSM_STATE = 64
IN_WIDTH = 7168
D_FF = 2816
FF_SHARD = D_FF // N_DEV
FF_PAD = 384
D_FF_PAD = FF_PAD * N_DEV
DN_ALPHA = 2.0 ** 0.25
LN_EPS = 1e-5
NEG_INF = -1e30
ROPE_THETA = 10000.0
BLOCK = 128
GROUPS = ((1, 16), (4, 4), (16, 1))

ADAM_LR = 0.001
ADAM_B1 = 0.9
ADAM_B2 = 0.999
ADAM_EPS = 1e-08
ADAM_WD = 0.01
ADAM_STEP = 10

VMEM_LIMIT = 56 * 1024 * 1024


_pallas_call = pl.pallas_call


def _cparams(**kw):
    return pltpu.CompilerParams(vmem_limit_bytes=VMEM_LIMIT, **kw)


def _dot(a, b):
    return jnp.dot(a, b, preferred_element_type=F32)


def _dot_nt(a, b):
    return lax.dot_general(a, b, (((1,), (1,)), ((), ())), preferred_element_type=F32)


def _side_by_side(w_ref, row=None):
    rows = slice(None) if row is None else pl.ds(row, 1)
    return jnp.concatenate([w_ref[i, rows, :] for i in range(w_ref.shape[0])], axis=1)


def _dot_tn(a, b):
    return lax.dot_general(a, b, (((0,), (0,)), ((), ())), preferred_element_type=F32)


def _rope_tables():
    half = HEAD_DIM // 2
    inv_freq = np.float32(ROPE_THETA) ** (-np.arange(half, dtype=np.float32) / np.float32(half))
    ang = np.arange(SEQ, dtype=np.float32)[:, None] * inv_freq[None, :]
    cos, sin = np.cos(ang).astype(np.float32), np.sin(ang).astype(np.float32)
    tables = np.tile(cos, (1, 4)), np.tile(np.concatenate([-sin, sin], axis=1), (1, 2))

    def by_phase(t):
        return np.stack([t.reshape(SEQ // d, d, 128).transpose(1, 0, 2).reshape(SEQ, 128) for d, _ in GROUPS])

    return jnp.asarray(by_phase(tables[0])), jnp.asarray(by_phase(tables[1]))


def _swap_halves(x):
    lane = lax.broadcasted_iota(jnp.int32, x.shape, 1)
    return jnp.where((lane & 63) < 32, pltpu.roll(x, 96, axis=1), pltpu.roll(x, 32, axis=1))


def _group_rows(d, nb, r, i):
    src = pl.ds(i * BLOCK, BLOCK) if d == 1 else pl.ds(r + i * BLOCK * d, BLOCK, stride=d)
    return src, pl.ds((r * nb + i) * BLOCK, BLOCK)


def _attn_masks():
    a_idx = lax.broadcasted_iota(jnp.int32, (2 * BLOCK, 2 * BLOCK), 0) & (BLOCK - 1)
    c_idx = lax.broadcasted_iota(jnp.int32, (2 * BLOCK, 2 * BLOCK), 1)
    cur_ok = jnp.logical_and(c_idx >= BLOCK, c_idx - BLOCK <= a_idx)
    prev_ok = jnp.logical_and(c_idx < BLOCK, c_idx >= a_idx)
    lane = lax.broadcasted_iota(jnp.int32, (BLOCK, 128), 1)
    return cur_ok, prev_ok, lane < HEAD_DIM


def _stack_heads(t, head0):
    zero = jnp.zeros_like(t)
    return jnp.concatenate([jnp.where(head0, t, zero), jnp.where(head0, zero, t)], axis=0)


def _unstack_heads(t2, head0):
    return jnp.where(head0, t2[:BLOCK], t2[BLOCK:])


def _attn_fwd(proj, cos_t, sin_t, ride=None):
    def body(q0, q1, q2, k0, k1, k2, v0, v1, v2, cos_ref, sin_ref, attn_ref, lse_ref,
             qs, ks, vs, os_, ms, ls, acc, mnat, lnat):
        cur_ok, prev_ok, head0 = _attn_masks()
        ks[:BLOCK, :] = jnp.zeros((BLOCK, 128), BF16)
        vs[:BLOCK, :] = jnp.zeros((BLOCK, 128), BF16)
        for g, (d, nb) in enumerate(GROUPS):
            q_ref, k_ref, v_ref = (q0, q1, q2)[g], (k0, k1, k2)[g], (v0, v1, v2)[g]
            for r in range(d):
                for i in range(nb):
                    src, dst = _group_rows(d, nb, r, i)
                    below = pl.ds(dst.start + BLOCK, BLOCK)
                    c, s = cos_ref[g, dst, :], sin_ref[g, dst, :]
                    q = q_ref[src, :]
                    k = k_ref[src, :]
                    qs[dst, :] = ((q * c + _swap_halves(q) * s) * 0.125).astype(BF16)
                    ks[below, :] = (k * c + _swap_halves(k) * s).astype(BF16)
                    vs[below, :] = v_ref[src, :].astype(BF16)

            def block(b, carry, nb=nb):
                has_prev = (b & (nb - 1)) > 0
                cur = pl.ds(pl.multiple_of(b * BLOCK, BLOCK), BLOCK)
                window = pl.ds(pl.multiple_of(b * BLOCK, BLOCK), 2 * BLOCK)
                valid = jnp.logical_or(cur_ok, jnp.logical_and(prev_ok, has_prev))
                s = jnp.where(valid, _dot_nt(_stack_heads(qs[cur, :], head0), ks[window, :]), NEG_INF)
                m = jnp.max(s, axis=1, keepdims=True)
                p = jnp.exp(s - m)
                os_[cur, :] = _unstack_heads(_dot(p.astype(BF16), vs[window, :]), head0)
                ms[cur, :] = _unstack_heads(m, head0)
                ls[cur, :] = _unstack_heads(jnp.sum(p, axis=1, keepdims=True), head0)
                return carry

            lax.fori_loop(0, SEQ // BLOCK, block, 0, unroll=16)

            for r in range(d):
                for i in range(nb):
                    src, dst = _group_rows(d, nb, r, i)
                    if g == 0:
                        acc[src, :], mnat[src, :], lnat[src, :] = os_[dst, :], ms[dst, :], ls[dst, :]
                    else:
                        m_old, m_g = mnat[src, :], ms[dst, :]
                        m_new = jnp.maximum(m_old, m_g)
                        a_old, a_g = jnp.exp(m_old - m_new), jnp.exp(m_g - m_new)
                        acc[src, :] = a_old * acc[src, :] + a_g * os_[dst, :]
                        lnat[src, :] = a_old * lnat[src, :] + a_g * ls[dst, :]
                        mnat[src, :] = m_new
        for i in range(SEQ // BLOCK):
            rows = pl.ds(i * BLOCK, BLOCK)
            l = lnat[rows, :]
            attn_ref[rows, :] = acc[rows, :] / l
            lse_ref[rows, :] = mnat[rows, :] + jnp.log(l)

    def col(base):
        return pl.BlockSpec((SEQ, 128), lambda hp, base=base: (0, base + hp))

    in_specs = [col(g * 4) for g in range(3)] + [col(12 + g * 4) for g in range(3)] + [col(24 + g * 4) for g in range(3)]
    table = pl.BlockSpec((3, SEQ, 128), lambda hp: (0, 0, 0), pipeline_mode=pl.Buffered(1))
    out = pl.BlockSpec((SEQ, 128), lambda hp: (0, hp))
    return _call(
        body, "attn_fwd", (4,), in_specs + [table, table], [out, out],
        [_sds((SEQ, ATTN_WIDTH), F32), _sds((SEQ, ATTN_WIDTH), F32)],
        [pltpu.VMEM((SEQ, 128), BF16)] + [pltpu.VMEM((SEQ + BLOCK, 128), BF16)] * 2 + [pltpu.VMEM((SEQ, 128), F32)] * 6,
        [proj] * 9 + [cos_t, sin_t], ride)


def _attn_bwd_group_body(g):
    d, nb = GROUPS[g]

    def body(q_ref, k_ref, v_ref, cos_ref, sin_ref, lse_ref, dattn_ref, dsum_ref, dproj_ref,
             qs, ks, vs, dos, lss, dss, dqs, dks, dvs, stage, outs, sems):
        cur_ok, prev_ok, head0 = _attn_masks()
        ks[:BLOCK, :] = jnp.zeros((BLOCK, 128), BF16)
        vs[:BLOCK, :] = jnp.zeros((BLOCK, 128), BF16)
        dks[:BLOCK, :] = jnp.zeros((BLOCK, 128), F32)
        dvs[:BLOCK, :] = jnp.zeros((BLOCK, 128), F32)
        for r in range(d):
            for i in range(nb):
                src, dst = _group_rows(d, nb, r, i)
                below = pl.ds(dst.start + BLOCK, BLOCK)
                c, s = cos_ref[g, dst, :], sin_ref[g, dst, :]
                q = q_ref[src, :]
                k = k_ref[src, :]
                qs[dst, :] = ((q * c + _swap_halves(q) * s) * 0.125).astype(BF16)
                ks[below, :] = (k * c + _swap_halves(k) * s).astype(BF16)
                vs[below, :] = v_ref[src, :].astype(BF16)
                dos[dst, :] = dattn_ref[src, :].astype(BF16)
                dss[dst, :] = dsum_ref[src, :]
                lss[dst, :] = lse_ref[src, :]
                dks[below, :] = jnp.zeros((BLOCK, 128), F32)
                dvs[below, :] = jnp.zeros((BLOCK, 128), F32)

        def per_head_column(t):
            return jnp.concatenate([jnp.max(jnp.where(head0, t, NEG_INF), axis=1, keepdims=True),
                                    jnp.max(jnp.where(head0, NEG_INF, t), axis=1, keepdims=True)], axis=0)

        def block(b, carry):
            has_prev = (b & (nb - 1)) > 0
            cur = pl.ds(pl.multiple_of(b * BLOCK, BLOCK), BLOCK)
            window = pl.ds(pl.multiple_of(b * BLOCK, BLOCK), 2 * BLOCK)
            valid = jnp.logical_or(cur_ok, jnp.logical_and(prev_ok, has_prev))
            q2, do2 = _stack_heads(qs[cur, :], head0), _stack_heads(dos[cur, :], head0)
            kw, vw = ks[window, :], vs[window, :]
            s = jnp.where(valid, _dot_nt(q2, kw), NEG_INF)
            p = jnp.exp(s - per_head_column(lss[cur, :]))
            ds = (p * (_dot_nt(do2, vw) - per_head_column(dss[cur, :]))).astype(BF16)
            dvs[window, :] += _dot_tn(p.astype(BF16), do2)
            dks[window, :] += _dot_tn(ds, q2)
            dqs[cur, :] = _unstack_heads(_dot(ds, kw), head0)
            return carry

        lax.fori_loop(0, SEQ // BLOCK, block, 0, unroll=8)

        hp = pl.program_id(0)
        copies = []
        for kind in range(3):
            for r in range(d):
                for i in range(nb):
                    src, dst = _group_rows(d, nb, r, i)
                    below = pl.ds(dst.start + BLOCK, BLOCK)
                    if kind == 2:
                        stage[src, :] = dvs[below, :]
                    else:
                        c, s = cos_ref[g, dst, :], sin_ref[g, dst, :]
                        t = dqs[dst, :] * 0.125 if kind == 0 else dks[below, :]
                        stage[src, :] = t * c - _swap_halves(t) * s
            for i in range(SEQ // MM_ROWS):
                rows = pl.ds(i * MM_ROWS, MM_ROWS)
                outs[kind, rows, :] = stage[rows, :].astype(BF16)
            column = pl.multiple_of((kind * 12 + g * 4 + hp) * 128, 128)
            copies.append(pltpu.make_async_copy(outs.at[kind], dproj_ref.at[:, pl.ds(column, 128)], sems.at[kind]))
            copies[-1].start()
        for cp in copies:
            cp.wait()

    return body


def _attn_bwd(proj, cos_t, sin_t, attn, lse, dattn, dproj, ride=None):
    groups = [_attn_bwd_group_body(g) for g in range(3)]

    def body(q0, q1, q2, k0, k1, k2, v0, v1, v2, cos_ref, sin_ref, attn_ref, lse_ref, dattn_ref, dproj_in, dproj_ref,
             dsum, *scratch):
        del dproj_in
        head0 = _attn_masks()[2]
        for i in range(SEQ // BLOCK):
            rows = pl.ds(i * BLOCK, BLOCK)
            prod = dattn_ref[rows, :] * attn_ref[rows, :]
            d0 = jnp.sum(jnp.where(head0, prod, 0.0), axis=1, keepdims=True)
            d1 = jnp.sum(jnp.where(head0, 0.0, prod), axis=1, keepdims=True)
            dsum[rows, :] = jnp.where(head0, d0, d1)
        for g in range(3):
            groups[g]((q0, q1, q2)[g], (k0, k1, k2)[g], (v0, v1, v2)[g], cos_ref, sin_ref, lse_ref, dattn_ref, dsum,
                      dproj_ref, *scratch)

    def col(base):
        return pl.BlockSpec((SEQ, 128), lambda hp, base=base: (0, base + hp))

    table = pl.BlockSpec((3, SEQ, 128), lambda hp: (0, 0, 0), pipeline_mode=pl.Buffered(1))
    return _call(
        body, "attn_bwd", (4,),
        [col(g * 4) for g in range(3)] + [col(12 + g * 4) for g in range(3)] + [col(24 + g * 4) for g in range(3)]
        + [table, table, col(0), col(0), col(0), ANY],
        [ANY], [_sds((SEQ, IN_WIDTH), BF16)],
        [pltpu.VMEM((SEQ, 128), F32)]
        + [pltpu.VMEM((SEQ, 128), BF16)] + [pltpu.VMEM((SEQ + BLOCK, 128), BF16)] * 2 + [pltpu.VMEM((SEQ, 128), BF16)]
        + [pltpu.VMEM((SEQ, 128), F32)] * 3 + [pltpu.VMEM((SEQ + BLOCK, 128), F32)] * 2 + [pltpu.VMEM((SEQ, 128), F32)]
        + [pltpu.VMEM((3, SEQ, 128), BF16), pltpu.SemaphoreType.DMA((3,))],
        [proj] * 9 + [cos_t, sin_t, attn, lse, dattn, dproj], ride, aliases={14: 0})


SSM_CHUNKS = 4
CHUNK_STATES = 512
SCAN_ROWS = 8
U_COL = (3 * QKV_WIDTH) // 128


def _cmul(xr, xi, yr, yi):
    return xr * yr - xi * yi, xr * yi + xi * yr


def _ssm_prep(a_re, a_im, log_dt, b_re_t, b_im_t):
    def body(ar_ref, ai_ref, ldt_ref, br_ref, bi_ref, abr_ref, abi_ref, er_ref, ei_ref, bbr_ref, bbi_ref):
        ar, ai = ar_ref[...], ai_ref[...]
        dt = jnp.exp(ldt_ref[...])
        mag = jnp.exp(ar * dt)
        abr, abi = mag * jnp.cos(ai * dt), mag * jnp.sin(ai * dt)
        den = ar * ar + ai * ai
        nr, ni = abr - 1.0, abi
        er, ei = (nr * ar + ni * ai) / den, (ni * ar - nr * ai) / den
        abr_ref[...], abi_ref[...], er_ref[...], ei_ref[...] = abr, abi, er, ei
        er3, ei3 = er[:, None, :], ei[:, None, :]
        br, bi = br_ref[...], bi_ref[...]
        bbr_ref[...] = er3 * br - ei3 * bi
        bbi_ref[...] = er3 * bi + ei3 * br

    gp = jax.ShapeDtypeStruct(a_re.shape, F32)
    gb = jax.ShapeDtypeStruct(b_re_t.shape, F32)
    return _pallas_call(body, name="ssm_prep", out_shape=(gp, gp, gp, gp, gb, gb))(a_re, a_im, log_dt, b_re_t, b_im_t)


def _ssm_param_bwd(a_re, a_im, log_dt, b_re_t, b_im_t, abar_re, abar_im, e_re, e_im, ga_re, ga_im, gbb_re_t, gbb_im_t):
    def body(ar_ref, ai_ref, ldt_ref, br_ref, bi_ref, abr_ref, abi_ref, er_ref, ei_ref, gar_ref, gai_ref, gbr_ref, gbi_ref,
             o_ar, o_ai, o_ldt, o_br, o_bi):
        ar, ai = ar_ref[...], ai_ref[...]
        dt = jnp.exp(ldt_ref[...])
        er, ei = er_ref[...], ei_ref[...]
        br, bi, gbr, gbi = br_ref[...], bi_ref[...], gbr_ref[...], gbi_ref[...]
        er3, ei3 = er[:, None, :], ei[:, None, :]
        o_br[...] = er3 * gbr + ei3 * gbi
        o_bi[...] = er3 * gbi - ei3 * gbr
        ge_r = jnp.sum(br * gbr + bi * gbi, axis=1)
        ge_i = jnp.sum(br * gbi - bi * gbr, axis=1)
        den = ar * ar + ai * ai
        ilr, ili = ar / den, -ai / den
        t_r, t_i = _cmul(ilr, -ili, ge_r, ge_i)
        gab_r, gab_i = gar_ref[...] + t_r, gai_ref[...] + t_i
        gz_r, gz_i = _cmul(abr_ref[...], -abi_ref[...], gab_r, gab_i)
        el_r, el_i = _cmul(er, ei, ilr, ili)
        u_r, u_i = _cmul(el_r, -el_i, ge_r, ge_i)
        o_ar[...] = dt * gz_r - u_r
        o_ai[...] = dt * gz_i - u_i
        o_ldt[...] = jnp.sum(gz_r * ar + gz_i * ai, axis=1, keepdims=True) * dt

    gp = jax.ShapeDtypeStruct(a_re.shape, F32)
    gb = jax.ShapeDtypeStruct(b_re_t.shape, F32)
    return _pallas_call(body, name="ssm_param_bwd", out_shape=(gp, gp, jax.ShapeDtypeStruct(log_dt.shape, F32), gb, gb))(
        a_re, a_im, log_dt, b_re_t, b_im_t, abar_re, abar_im, e_re, e_im, ga_re, ga_im, gbb_re_t, gbb_im_t)


def _block_diag(blocks_re, blocks_im, sign_im, rows_are_channels):
    both = jnp.stack([blocks_re, sign_im * blocks_im]).reshape(2, SSM_CHUNKS, 8, SSM_GROUP, SSM_STATE)
    eye = jnp.eye(8, dtype=F32)
    if rows_are_channels:
        return jnp.einsum("rcghp,gk->cghrkp", both, eye).reshape(SSM_CHUNKS, 128, 2 * CHUNK_STATES)
    return jnp.einsum("rcghp,gk->crkpgh", both, eye).reshape(SSM_CHUNKS, 2 * CHUNK_STATES, 128)


def _block_diag_parts(mat, rows_are_channels):
    if rows_are_channels:
        six = mat.reshape(SSM_CHUNKS, 8, SSM_GROUP, 2, 8, SSM_STATE)
        parts = jnp.einsum("cghrgp->rcghp", six)
    else:
        six = mat.reshape(SSM_CHUNKS, 2, 8, SSM_STATE, 8, SSM_GROUP)
        parts = jnp.einsum("crgpgh->rcghp", six)
    parts = parts.reshape(2, SSM_GROUPS, SSM_GROUP, SSM_STATE)
    return parts[0], parts[1]


def _scan_consts(a_ref, conj, reverse):
    ar = jnp.broadcast_to(a_ref[:, :CHUNK_STATES], (SCAN_ROWS, CHUNK_STATES))
    ai = jnp.broadcast_to(a_ref[:, CHUNK_STATES:], (SCAN_ROWS, CHUNK_STATES))
    if conj:
        ai = -ai
    row = lax.broadcasted_iota(jnp.int32, (SCAN_ROWS, CHUNK_STATES), 0)
    if reverse:
        row = SCAN_ROWS - 1 - row
    zero = jnp.zeros_like(ar)
    steps = []
    pr, pi = ar, ai
    for shift in (1, 2, 4):
        keep = row >= shift
        steps.append((SCAN_ROWS - shift if reverse else shift, jnp.where(keep, pr, zero), jnp.where(keep, pi, zero)))
        pr, pi = _cmul(pr, pi, pr, pi)
    first = row == 0
    return steps, (jnp.where(first, ar, zero), jnp.where(first, ai, zero)), first


def _scan_tile(xr, xi, prev_r, prev_i, steps, carry_in, reverse):
    edge = SCAN_ROWS - 1 if reverse else 1
    cr, ci = pltpu.roll(prev_r, edge, axis=0), pltpu.roll(prev_i, edge, axis=0)
    xr, xi = xr + carry_in[0] * cr - carry_in[1] * ci, xi + carry_in[0] * ci + carry_in[1] * cr
    for shift, mr, mi in steps:
        sr, si = pltpu.roll(xr, shift, axis=0), pltpu.roll(xi, shift, axis=0)
        xr, xi = xr + mr * sr - mi * si, xi + mr * si + mi * sr
    return xr, xi


MM_ROWS = 256


def _ssm_fwd(proj, bmat, cmat, a_chunks, d_skip, ride=None):
    def body(u_ref, b_ref, c_ref, a_ref, d_ref, y_ref, h_ref):
        for i in range(SEQ // MM_ROWS):
            rows = pl.ds(i * MM_ROWS, MM_ROWS)
            h_ref[rows, :] = _dot(u_ref[rows, :].astype(BF16), b_ref[...])
        steps, carry_in, _ = _scan_consts(a_ref, conj=False, reverse=False)

        def tile(k, carry):
            rows = pl.ds(pl.multiple_of(k * SCAN_ROWS, SCAN_ROWS), SCAN_ROWS)
            xr, xi = _scan_tile(h_ref[rows, :CHUNK_STATES], h_ref[rows, CHUNK_STATES:], carry[0], carry[1], steps, carry_in, False)
            h_ref[rows, :CHUNK_STATES] = xr
            h_ref[rows, CHUNK_STATES:] = xi
            return xr, xi

        zero = jnp.zeros((SCAN_ROWS, CHUNK_STATES), F32)
        lax.fori_loop(0, SEQ // SCAN_ROWS, tile, (zero, zero), unroll=4)
        for i in range(SEQ // MM_ROWS):
            rows = pl.ds(i * MM_ROWS, MM_ROWS)
            y_ref[rows, :] = _dot(h_ref[rows, :].astype(BF16), c_ref[...]) + d_ref[...] * u_ref[rows, :]

    return _call(
        body, "ssm_fwd", (SSM_CHUNKS,),
        [pl.BlockSpec((SEQ, 128), lambda c: (0, U_COL + c)),
         pl.BlockSpec((None, 128, 2 * CHUNK_STATES), lambda c: (c, 0, 0)),
         pl.BlockSpec((None, 2 * CHUNK_STATES, 128), lambda c: (c, 0, 0)),
         pl.BlockSpec((None, 1, 2 * CHUNK_STATES), lambda c: (c, 0, 0)),
         pl.BlockSpec((1, 128), lambda c: (0, c))],
        [pl.BlockSpec((SEQ, 128), lambda c: (0, c)), pl.BlockSpec((SEQ, 2 * CHUNK_STATES), lambda c: (0, c))],
        [_sds((SEQ, SSM_WIDTH), F32), _sds((SEQ, SSM_CHUNKS * 2 * CHUNK_STATES), F32)], [],
        [proj, bmat, cmat, a_chunks, d_skip], ride)


def _ssm_bwd(dys, proj, h, bmat, cmat, a_chunks, d_skip, dproj, ride=None):
    def body(dy_ref, u_ref, h_ref, b_ref, c_ref, a_ref, d_ref, dproj_in, du_ref, db_ref, dc_ref, da_ref, dd_ref, g_ref):
        del dproj_in
        dsum = jnp.zeros((1, 128), F32)
        dcm = jnp.zeros((2 * CHUNK_STATES, 128), F32)
        for i in range(SEQ // MM_ROWS):
            rows = pl.ds(i * MM_ROWS, MM_ROWS)
            dy = dy_ref[rows, :]
            g_ref[rows, :] = _dot_nt(dy.astype(BF16), c_ref[...])
            dsum += jnp.sum(dy * u_ref[rows, :], axis=0, keepdims=True)
            dcm += _dot_tn(h_ref[rows, :].astype(BF16), dy.astype(BF16))
        dd_ref[...] = dsum
        dc_ref[...] = dcm
        steps, carry_in, _ = _scan_consts(a_ref, conj=True, reverse=True)
        first_row = lax.broadcasted_iota(jnp.int32, (SCAN_ROWS, CHUNK_STATES), 0) == 0
        n_tiles = SEQ // SCAN_ROWS

        def tile(j, carry):
            k = n_tiles - 1 - j
            rows = pl.ds(pl.multiple_of(k * SCAN_ROWS, SCAN_ROWS), SCAN_ROWS)
            before = pl.ds(pl.multiple_of(jnp.maximum(k - 1, 0) * SCAN_ROWS, SCAN_ROWS), SCAN_ROWS)
            gr, gi = _scan_tile(g_ref[rows, :CHUNK_STATES], g_ref[rows, CHUNK_STATES:], carry[0], carry[1], steps, carry_in, True)
            g_ref[rows, :CHUNK_STATES] = gr
            g_ref[rows, CHUNK_STATES:] = gi
            has_before = jnp.where(k > 0, 1.0, 0.0)
            hr = jnp.where(first_row, pltpu.roll(h_ref[before, :CHUNK_STATES], 1, axis=0) * has_before,
                           pltpu.roll(h_ref[rows, :CHUNK_STATES], 1, axis=0))
            hi = jnp.where(first_row, pltpu.roll(h_ref[before, CHUNK_STATES:], 1, axis=0) * has_before,
                           pltpu.roll(h_ref[rows, CHUNK_STATES:], 1, axis=0))
            return gr, gi, carry[2] + hr * gr + hi * gi, carry[3] + hr * gi - hi * gr

        zero = jnp.zeros((SCAN_ROWS, CHUNK_STATES), F32)
        _, _, sar, sai = lax.fori_loop(0, n_tiles, tile, (zero, zero, zero, zero), unroll=4)
        da_ref[:, :CHUNK_STATES] = jnp.sum(sar, axis=0, keepdims=True)
        da_ref[:, CHUNK_STATES:] = jnp.sum(sai, axis=0, keepdims=True)
        dbm = jnp.zeros((128, 2 * CHUNK_STATES), F32)
        for i in range(SEQ // MM_ROWS):
            rows = pl.ds(i * MM_ROWS, MM_ROWS)
            g = g_ref[rows, :].astype(BF16)
            du_ref[rows, :] = (_dot_nt(g, b_ref[...]) + d_ref[...] * dy_ref[rows, :]).astype(BF16)
            dbm += _dot_tn(u_ref[rows, :].astype(BF16), g)
        db_ref[...] = dbm

    chunk_col = pl.BlockSpec((SEQ, 128), lambda c: (0, c))
    return _call(
        body, "ssm_bwd", (SSM_CHUNKS,),
        [chunk_col,
         pl.BlockSpec((SEQ, 128), lambda c: (0, U_COL + c)),
         pl.BlockSpec((SEQ, 2 * CHUNK_STATES), lambda c: (0, c)),
         pl.BlockSpec((None, 128, 2 * CHUNK_STATES), lambda c: (c, 0, 0)),
         pl.BlockSpec((None, 2 * CHUNK_STATES, 128), lambda c: (c, 0, 0)),
         pl.BlockSpec((None, 1, 2 * CHUNK_STATES), lambda c: (c, 0, 0)),
         pl.BlockSpec((1, 128), lambda c: (0, c)), ANY],
        [pl.BlockSpec((SEQ, 128), lambda c: (0, U_COL + c)),
         pl.BlockSpec((None, 128, 2 * CHUNK_STATES), lambda c: (c, 0, 0)),
         pl.BlockSpec((None, 2 * CHUNK_STATES, 128), lambda c: (c, 0, 0)),
         pl.BlockSpec((None, 1, 2 * CHUNK_STATES), lambda c: (c, 0, 0)),
         pl.BlockSpec((1, 128), lambda c: (0, c))],
        [_sds((SEQ, IN_WIDTH), BF16), _sds((SSM_CHUNKS, 128, 2 * CHUNK_STATES), F32),
         _sds((SSM_CHUNKS, 2 * CHUNK_STATES, 128), F32), _sds((SSM_CHUNKS, 1, 2 * CHUNK_STATES), F32), _sds((1, SSM_WIDTH), F32)],
        [pltpu.VMEM((SEQ, 2 * CHUNK_STATES), F32)], [dys, proj, h, bmat, cmat, a_chunks, d_skip, dproj], ride, aliases={7: 0})


def _ssm_tables(abar_re, abar_im, bbar_re_t, bbar_im_t, c_re, c_im):
    bmat = _block_diag(bbar_re_t, bbar_im_t, 1.0, True).astype(BF16)
    cmat = _block_diag(c_re, c_im, -1.0, False).astype(BF16)
    a_chunks = jnp.concatenate([abar_re.reshape(SSM_CHUNKS, 1, CHUNK_STATES), abar_im.reshape(SSM_CHUNKS, 1, CHUNK_STATES)], axis=2)
    return bmat, cmat, a_chunks


GL_COL = (3 * QKV_WIDTH + SSM_WIDTH) // D_MODEL
GELU_C = math.sqrt(2.0 / math.pi)
GELU_A = 0.044715


def _sds(shape, dtype):
    return jax.ShapeDtypeStruct(shape, dtype)


def _gelu(x):
    t = jnp.tanh(GELU_C * (x + GELU_A * x * x * x))
    return 0.5 * x * (1.0 + t), t


def _gelu_grad(x, t):
    return 0.5 * (1.0 + t) + 0.5 * x * (1.0 - t * t) * GELU_C * (1.0 + 3.0 * GELU_A * x * x)


def _layer_norm(r, g, b):
    mu = jnp.mean(r, axis=-1, keepdims=True)
    xc = r - mu
    rstd = lax.rsqrt(jnp.mean(xc * xc, axis=-1, keepdims=True) + LN_EPS)
    xhat = xc * rstd
    return xhat * g + b, xhat, rstd


def _layer_norm_bwd(dy, xhat, rstd, g):
    dxhat = dy * g
    m1 = jnp.mean(dxhat, axis=-1, keepdims=True)
    m2 = jnp.mean(dxhat * xhat, axis=-1, keepdims=True)
    return rstd * (dxhat - m1 - xhat * m2)


def _proj(x, w_in, ride=None):
    tm, tn = 1024, 1792

    def body(x_ref, w_ref, o_ref):
        o_ref[...] = _dot(x_ref[...].astype(BF16), _side_by_side(w_ref))

    return _call(
        body, "proj", (SEQ // tm, IN_WIDTH // tn),
        [pl.BlockSpec((tm, D_MODEL), lambda i, j: (i, 0)), pl.BlockSpec((2, D_MODEL, tn // 2), lambda i, j: (j, 0, 0))],
        [pl.BlockSpec((tm, tn), lambda i, j: (i, j))], [_sds((SEQ, IN_WIDTH), F32)], [], [x, w_in], ride)


def _row_spec(tm, width, col=0):
    return pl.BlockSpec((tm, width), lambda i, col=col: (i, col))


def _full_spec(shape):
    return pl.BlockSpec(shape, lambda i: (0,) * len(shape))


def _weight_spec(shape):
    return pl.BlockSpec(shape, lambda i: (0,) * len(shape), pipeline_mode=pl.Buffered(1))


def _mixer_out(attn, ys, proj, x, w_ab, w_sb, w_glu, w_out, b_gate, ln_g, ln_b):
    tm = 512

    def body(attn_ref, ys_ref, gl0_ref, gl1_ref, x_ref, wab_ref, wsb_ref, wglu_ref, wout_ref, bg_ref, g_ref, b_ref,
             h_ref, xhat_ref, rstd_ref, glu_ref, ya_ref, yssm_ref):
        gy, _ = _gelu(ys_ref[...])
        glu = _dot(gy.astype(BF16), _side_by_side(wglu_ref))
        glu_ref[...] = glu
        y_s = glu[:, :SSM_WIDTH] * jax.nn.sigmoid(glu[:, SSM_WIDTH:])
        y_ssm = _dot(y_s.astype(BF16), _side_by_side(wsb_ref))
        y_attn = _dot(attn_ref[...].astype(BF16), _side_by_side(wab_ref))
        ya_ref[...] = y_attn
        yssm_ref[...] = y_ssm
        g0 = jax.nn.sigmoid(gl0_ref[...] + _side_by_side(bg_ref, 0))
        g1 = jax.nn.sigmoid(gl1_ref[...] + _side_by_side(bg_ref, 1))
        mixed = g0 * y_attn + g1 * y_ssm
        r1 = DN_ALPHA * x_ref[...] + _dot(mixed.astype(BF16), wout_ref[...])
        h, xhat, rstd = _layer_norm(r1, g_ref[...], b_ref[...])
        h_ref[...] = h
        xhat_ref[...] = xhat
        rstd_ref[...] = jnp.broadcast_to(rstd, (tm, 128))

    wide = _sds((SEQ, D_MODEL), F32)
    return _pallas_call(
        body, name="mixer_out", grid=(SEQ // tm,),
        in_specs=[_row_spec(tm, ATTN_WIDTH), _row_spec(tm, SSM_WIDTH), _row_spec(tm, D_MODEL, GL_COL), _row_spec(tm, D_MODEL, GL_COL + 1),
                  _row_spec(tm, D_MODEL), _weight_spec((N_DEV, ATTN_WIDTH, 128)), _weight_spec((N_DEV, SSM_WIDTH, 128)),
                  _weight_spec((N_DEV, SSM_WIDTH, 128)), _weight_spec((D_MODEL, D_MODEL)), _full_spec((N_DEV, 2, 128)),
                  _full_spec((1, D_MODEL)), _full_spec((1, D_MODEL))],
        out_specs=(_row_spec(tm, D_MODEL), _row_spec(tm, D_MODEL), _row_spec(tm, 128), _row_spec(tm, D_MODEL),
                   _row_spec(tm, D_MODEL), _row_spec(tm, D_MODEL)),
        out_shape=(wide, wide, _sds((SEQ, 128), F32), wide, wide, wide),
        compiler_params=_cparams(dimension_semantics=("arbitrary",)),
    )(attn, ys, proj, proj, x, w_ab, w_sb, w_glu, w_out, b_gate, ln_g, ln_b)


def _ff_up(h, w_gate, w_up):
    tm, tn = 1024, 768

    def body(h_ref, wg_ref, wu_ref, a_ref, b_ref, f_ref):
        hb = h_ref[...].astype(BF16)
        a, b = _dot(hb, _side_by_side(wg_ref)), _dot(hb, _side_by_side(wu_ref))
        a_ref[...] = a.astype(BF16)
        b_ref[...] = b.astype(BF16)
        f_ref[...] = (a * jax.nn.sigmoid(a) * b).astype(BF16)

    tile = pl.BlockSpec((tm, tn), lambda i, j: (i, j))
    wtile = pl.BlockSpec((tn // FF_PAD, D_MODEL, FF_PAD), lambda i, j: (j, 0, 0))
    out = _sds((SEQ, D_FF_PAD), BF16)
    return _pallas_call(
        body, name="ff_up", grid=(SEQ // tm, D_FF_PAD // tn),
        in_specs=[pl.BlockSpec((tm, D_MODEL), lambda i, j: (i, 0)), wtile, wtile],
        out_specs=(tile, tile, tile), out_shape=(out, out, out),
        compiler_params=_cparams(dimension_semantics=("arbitrary", "arbitrary")),
    )(h, w_gate, w_up)


def _ff_down_loss(f, w_down, h, target, ln_g, ln_b):
    tm = 512

    def body(f_ref, w_ref, h_ref, t_ref, g_ref, b_ref, dr_ref, dg_ref, db_ref, loss_ref):
        @pl.when(pl.program_id(0) == 0)
        def _():
            dg_ref[...] = jnp.zeros_like(dg_ref)
            db_ref[...] = jnp.zeros_like(db_ref)
            loss_ref[...] = jnp.zeros_like(loss_ref)

        r2 = DN_ALPHA * h_ref[...] + _dot(f_ref[...], w_ref[...])
        g = g_ref[...]
        out, xhat, rstd = _layer_norm(r2, g, b_ref[...])
        err = out - t_ref[...]
        loss_ref[...] += 0.5 * jnp.sum(jnp.mean(err * err, axis=-1, keepdims=True), axis=0, keepdims=True)
        dout = err * (1.0 / D_MODEL)
        dg_ref[...] += jnp.sum(dout * xhat, axis=0, keepdims=True)
        db_ref[...] += jnp.sum(dout, axis=0, keepdims=True)
        dr_ref[...] = _layer_norm_bwd(dout, xhat, rstd, g)

    vec = _sds((1, D_MODEL), F32)
    return _pallas_call(
        body, name="ff_down_loss", grid=(SEQ // tm,),
        in_specs=[_row_spec(tm, D_FF_PAD), _weight_spec((D_FF_PAD, D_MODEL)), _row_spec(tm, D_MODEL), _row_spec(tm, D_MODEL),
                  _full_spec((1, D_MODEL)), _full_spec((1, D_MODEL))],
        out_specs=(_row_spec(tm, D_MODEL), _full_spec((1, D_MODEL)), _full_spec((1, D_MODEL)), _full_spec((1, 128))),
        out_shape=(_sds((SEQ, D_MODEL), F32), vec, vec, _sds((1, 128), F32)),
        compiler_params=_cparams(dimension_semantics=("arbitrary",)),
    )(f, w_down, h, target, ln_g, ln_b)


def _ff_down_bwd(dr2, w_down, a, b):
    tm, tn = 1024, 768

    def body(dr_ref, w_ref, a_ref, b_ref, da_ref, db_ref):
        df = _dot_nt(dr_ref[...].astype(BF16), w_ref[...])
        av, bv = a_ref[...].astype(F32), b_ref[...].astype(F32)
        sg = jax.nn.sigmoid(av)
        da_ref[...] = (df * bv * sg * (1.0 + av * (1.0 - sg))).astype(BF16)
        db_ref[...] = (df * av * sg).astype(BF16)

    tile = pl.BlockSpec((tm, tn), lambda i, j: (i, j))
    out = _sds((SEQ, D_FF_PAD), BF16)
    return _pallas_call(
        body, name="ff_down_bwd", grid=(SEQ // tm, D_FF_PAD // tn),
        in_specs=[pl.BlockSpec((tm, D_MODEL), lambda i, j: (i, 0)), pl.BlockSpec((tn, D_MODEL), lambda i, j: (j, 0)), tile, tile],
        out_specs=(tile, tile), out_shape=(out, out),
        compiler_params=_cparams(dimension_semantics=("arbitrary", "arbitrary")),
    )(dr2, w_down, a, b)


def _ff_up_bwd(da, db, w_gate, w_up, dr2, xhat1, rstd1, ln_g, ride=None):
    tm, tk = 1024, 768
    nk = D_FF_PAD // tk

    def body(da_ref, db_ref, wg_ref, wu_ref, dr2_ref, xhat_ref, rstd_ref, g_ref, dr1_ref, dg_ref, dbias_ref, acc):
        i, k = pl.program_id(0), pl.program_id(1)

        @pl.when(jnp.logical_and(i == 0, k == 0))
        def _():
            dg_ref[...] = jnp.zeros_like(dg_ref)
            dbias_ref[...] = jnp.zeros_like(dbias_ref)

        part = _dot_nt(da_ref[...], _side_by_side(wg_ref)) + _dot_nt(db_ref[...], _side_by_side(wu_ref))

        @pl.when(k == 0)
        def _():
            acc[...] = part

        @pl.when(k > 0)
        def _():
            acc[...] += part

        @pl.when(k == nk - 1)
        def _():
            dh = DN_ALPHA * dr2_ref[...] + acc[...]
            xhat = xhat_ref[...]
            dg_ref[...] += jnp.sum(dh * xhat, axis=0, keepdims=True)
            dbias_ref[...] += jnp.sum(dh, axis=0, keepdims=True)
            rstd = jnp.max(rstd_ref[...], axis=1, keepdims=True)
            dr1_ref[...] = _layer_norm_bwd(dh, xhat, rstd, g_ref[...])

    hid = pl.BlockSpec((tm, tk), lambda i, k: (i, k))
    wtile = pl.BlockSpec((tk // FF_PAD, D_MODEL, FF_PAD), lambda i, k: (k, 0, 0))
    row = pl.BlockSpec((tm, D_MODEL), lambda i, k: (i, 0))
    vec = pl.BlockSpec((1, D_MODEL), lambda i, k: (0, 0))
    return _call(
        body, "ff_up_bwd", (SEQ // tm, nk),
        [hid, hid, wtile, wtile, row, row, pl.BlockSpec((tm, 128), lambda i, k: (i, 0)), vec],
        [row, vec, vec], [_sds((SEQ, D_MODEL), F32), _sds((1, D_MODEL), F32), _sds((1, D_MODEL), F32)],
        [pltpu.VMEM((tm, D_MODEL), F32)], [da, db, w_gate, w_up, dr2, xhat1, rstd1, ln_g], ride)


def _mixer_bwd(dr1, proj, y_attn, y_ssm, glu, ys, w_ab, w_sb, w_glu, w_out, b_gate):
    tm = 256

    def body(dr1_ref, gl0_ref, gl1_ref, ya_ref, yssm_ref, glu_ref, ys_ref, wab_ref, wsb_ref, wglu_ref, wout_ref, bg_ref,
             dya_ref, dyssm_ref, dgl_ref, dattn_ref, dglu_ref, dys_ref, mixed_ref, ysb_ref, gy_ref, dbg_ref):
        @pl.when(pl.program_id(0) == 0)
        def _():
            dbg_ref[...] = jnp.zeros_like(dbg_ref)

        dmixed = _dot_nt(dr1_ref[...].astype(BF16), wout_ref[...])
        g0 = jax.nn.sigmoid(gl0_ref[...] + _side_by_side(bg_ref, 0))
        g1 = jax.nn.sigmoid(gl1_ref[...] + _side_by_side(bg_ref, 1))
        y_attn, y_ssm = ya_ref[...], yssm_ref[...]
        mixed_ref[...] = (g0 * y_attn + g1 * y_ssm).astype(BF16)
        dya = (dmixed * g0).astype(BF16)
        dyssm = (dmixed * g1).astype(BF16)
        dya_ref[...] = dya
        dyssm_ref[...] = dyssm
        dgl0 = dmixed * y_attn * g0 * (1.0 - g0)
        dgl1 = dmixed * y_ssm * g1 * (1.0 - g1)
        dgl_ref[:, :GL_COL * D_MODEL] = jnp.zeros((tm, GL_COL * D_MODEL), BF16)
        dgl_ref[:, GL_COL * D_MODEL:(GL_COL + 1) * D_MODEL] = dgl0.astype(BF16)
        dgl_ref[:, (GL_COL + 1) * D_MODEL:] = dgl1.astype(BF16)
        dbg_ref[:, :D_MODEL] += jnp.sum(dgl0, axis=0, keepdims=True)
        dbg_ref[:, D_MODEL:] += jnp.sum(dgl1, axis=0, keepdims=True)
        dattn_ref[...] = _dot_nt(dya, _side_by_side(wab_ref))
        dy_s = _dot_nt(dyssm, _side_by_side(wsb_ref))
        glu = glu_ref[...]
        glu1, sg = glu[:, :SSM_WIDTH], jax.nn.sigmoid(glu[:, SSM_WIDTH:])
        ysb_ref[...] = (glu1 * sg).astype(BF16)
        dglu1 = (dy_s * sg).astype(BF16)
        dglu2 = (dy_s * glu1 * sg * (1.0 - sg)).astype(BF16)
        dglu_ref[:, :SSM_WIDTH] = dglu1
        dglu_ref[:, SSM_WIDTH:] = dglu2
        dgy = _dot_nt(jnp.concatenate([dglu1, dglu2], axis=1), _side_by_side(wglu_ref))
        ys = ys_ref[...]
        gy, t = _gelu(ys)
        gy_ref[...] = gy.astype(BF16)
        dys_ref[...] = dgy * _gelu_grad(ys, t)

    wide_b, half_b = _sds((SEQ, D_MODEL), BF16), _sds((SEQ, SSM_WIDTH), BF16)
    half_f = _sds((SEQ, SSM_WIDTH), F32)
    return _pallas_call(
        body, name="mixer_bwd", grid=(SEQ // tm,),
        in_specs=[_row_spec(tm, D_MODEL), _row_spec(tm, D_MODEL, GL_COL), _row_spec(tm, D_MODEL, GL_COL + 1), _row_spec(tm, D_MODEL),
                  _row_spec(tm, D_MODEL), _row_spec(tm, D_MODEL), _row_spec(tm, SSM_WIDTH), _full_spec((N_DEV, ATTN_WIDTH, 128)),
                  _full_spec((N_DEV, SSM_WIDTH, 128)), _full_spec((N_DEV, SSM_WIDTH, 128)), _full_spec((D_MODEL, D_MODEL)),
                  _full_spec((N_DEV, 2, 128))],
        out_specs=(_row_spec(tm, D_MODEL), _row_spec(tm, D_MODEL), _row_spec(tm, IN_WIDTH), _row_spec(tm, ATTN_WIDTH),
                   _row_spec(tm, D_MODEL), _row_spec(tm, SSM_WIDTH), _row_spec(tm, D_MODEL), _row_spec(tm, SSM_WIDTH),
                   _row_spec(tm, SSM_WIDTH), _full_spec((1, 2 * D_MODEL))),
        out_shape=(wide_b, wide_b, _sds((SEQ, IN_WIDTH), BF16), half_f, wide_b, half_f, wide_b, half_b, half_b,
                   _sds((1, 2 * D_MODEL), F32)),
        compiler_params=_cparams(dimension_semantics=("arbitrary",)),
    )(dr1, proj, proj, y_attn, y_ssm, glu, ys, w_ab, w_sb, w_glu, w_out, b_gate)


def _grad_x(dproj, w_in, dr1, ride=None):
    tm, tk = 1024, 1792
    nk = IN_WIDTH // tk

    def body(dp_ref, w_ref, dr1_ref, o_ref, acc):
        k = pl.program_id(1)
        part = _dot_nt(dp_ref[...], _side_by_side(w_ref))

        @pl.when(k == 0)
        def _():
            acc[...] = part

        @pl.when(k > 0)
        def _():
            acc[...] += part

        @pl.when(k == nk - 1)
        def _():
            o_ref[...] = DN_ALPHA * dr1_ref[...] + acc[...]

    row = pl.BlockSpec((tm, D_MODEL), lambda i, k: (i, 0))
    return _call(
        body, "grad_x", (SEQ // tm, nk),
        [pl.BlockSpec((tm, tk), lambda i, k: (i, k)), pl.BlockSpec((2, D_MODEL, tk // 2), lambda i, k: (k, 0, 0)), row],
        [row], [_sds((SEQ, D_MODEL), F32)], [pltpu.VMEM((tm, D_MODEL), F32)], [dproj, w_in, dr1], ride)


def _weight_grad(a, b, name, shard_cols=None, ride=None):
    k, n = a.shape[1], b.shape[1]
    tk = min(k, 512) if shard_cols else k // N_DEV
    tn = n // 4 if shard_cols else min(n, 1024)

    def body(a_ref, b_ref, o_ref):
        grad = _dot_tn(a_ref[...].astype(BF16), b_ref[...].astype(BF16))
        if shard_cols:
            o_ref[0] = grad[:, :shard_cols].astype(BF16)
            o_ref[1] = grad[:, shard_cols:].astype(BF16)
        else:
            o_ref[...] = grad.astype(BF16)

    if shard_cols:
        out_spec = pl.BlockSpec((2, None, tk, shard_cols), lambda kk, j: (0, j, kk, 0))
        out_shape = _sds((2, 4, k, shard_cols), BF16)
    else:
        out_spec = pl.BlockSpec((None, None, tk, tn), lambda kk, j: (kk % 2, kk // 2, 0, j))
        out_shape = _sds((2, 4, tk, n), BF16)
    out = _call(body, name, (k // tk, n // tn),
                [pl.BlockSpec((SEQ, tk), lambda kk, j: (0, kk)), pl.BlockSpec((SEQ, tn), lambda kk, j: (0, j))],
                [out_spec], [out_shape], [], [a, b], ride)
    return out[0] if ride is None else out


MESH = pl.DeviceIdType.MESH
ANY = pl.BlockSpec(memory_space=pl.ANY)


def _place():
    return lax.axis_index("x"), lax.axis_index("y"), lax.axis_index("c")


def _other_chips(x, y):
    return [(1 - x, y), (x, 1 - y), (1 - x, 1 - y)]


class _Ride:
    def __init__(self, operands, results, aliases, sems, start, wait):
        self.operands, self.results, self.aliases, self.sems = list(operands), list(results), dict(aliases), list(sems)
        self.start, self.wait = start, wait

    def __add__(self, other):
        n_in, n_out, n_sem = len(self.operands), len(self.results), len(self.sems)

        def both(which):
            def run(ins, outs, sems):
                getattr(self, which)(ins[:n_in], outs[:n_out], sems[:n_sem])
                getattr(other, which)(ins[n_in:], outs[n_out:], sems[n_sem:])
            return run

        aliases = {**self.aliases, **{n_in + i: n_out + j for i, j in other.aliases.items()}}
        return _Ride(self.operands + other.operands, self.results + other.results, aliases, self.sems + other.sems,
                     both("start"), both("wait"))


def _call(body, name, grid, in_specs, out_specs, out_shape, scratch_shapes, operands, ride=None, aliases=None):
    in_specs, out_specs, out_shape = list(in_specs), list(out_specs), list(out_shape)
    scratch_shapes, operands, aliases = list(scratch_shapes), list(operands), dict(aliases or {})
    kernel_body = body
    if ride is not None:
        n_in, n_out, n_scr, r_in, r_out = len(in_specs), len(out_specs), len(scratch_shapes), len(ride.operands), len(ride.results)

        def kernel_body(*refs):
            out0, scr0 = n_in + r_in, n_in + r_in + n_out + r_out
            ride_refs = (refs[n_in:out0], refs[out0 + n_out:scr0], refs[scr0 + n_scr:])
            ids = [pl.program_id(i) for i in range(len(grid))]
            first = functools.reduce(jnp.logical_and, [i == 0 for i in ids])
            last = functools.reduce(jnp.logical_and, [i == g - 1 for i, g in zip(ids, grid)])

            @pl.when(first)
            def _():
                ride.start(*ride_refs)

            body(*refs[:n_in], *refs[out0:out0 + n_out], *refs[scr0:scr0 + n_scr])

            @pl.when(last)
            def _():
                ride.wait(*ride_refs)

        aliases.update({n_in + i: n_out + j for i, j in ride.aliases.items()})
        in_specs += [ANY] * r_in
        out_specs += [ANY] * r_out
        out_shape += ride.results
        scratch_shapes += ride.sems
        operands += ride.operands
    return _pallas_call(
        kernel_body, name=name, grid=grid, in_specs=in_specs, out_specs=out_specs, out_shape=out_shape,
        scratch_shapes=scratch_shapes, input_output_aliases=aliases,
        compiler_params=_cparams(dimension_semantics=("arbitrary",) * len(grid)),
    )(*operands)


def _gather_first_level(shards):
    n = len(shards)

    def copies(ins, outs, sems, landed):
        send_sems, recv_sems, local_sems = sems
        x, y, c = _place()
        peers = [(x, y, 1 - c)] + [(px, py, c) for px, py in _other_chips(x, y)]

        def row(peer):
            return 4 * x + 2 * y + c if not landed else 4 * peer[0] + 2 * peer[1] + peer[2]

        local = [pltpu.make_async_copy(ins[a], outs[a].at[4 * x + 2 * y + c], local_sems.at[a]) for a in range(n)]
        remote = [pltpu.make_async_remote_copy(
            src_ref=ins[a], dst_ref=outs[a].at[row(peer)], send_sem=send_sems.at[a, k], recv_sem=recv_sems.at[a, k],
            device_id=peer, device_id_type=MESH) for a in range(n) for k, peer in enumerate(peers)]
        return local, remote

    def start(ins, outs, sems):
        local, remote = copies(ins, outs, sems, False)
        for cp in local + remote:
            cp.start()

    def wait(ins, outs, sems):
        local, sent = copies(ins, outs, sems, False)
        for cp in copies(ins, outs, sems, True)[1]:
            cp.wait_recv()
        for cp in sent:
            cp.wait_send()
        for cp in local:
            cp.wait()

    return _Ride(shards, [_sds((N_DEV,) + s.shape, s.dtype) for s in shards], {},
                 [pltpu.SemaphoreType.DMA((n, 4)), pltpu.SemaphoreType.DMA((n, 4)), pltpu.SemaphoreType.DMA((n,))], start, wait)


def _gather_second_level(buffers):
    n = len(buffers)

    def copies(outs, sems, core):
        send_sems, recv_sems = sems
        x, y, c = _place()
        return [pltpu.make_async_remote_copy(
            src_ref=outs[a].at[4 * px + 2 * py + core], dst_ref=outs[a].at[4 * px + 2 * py + core], send_sem=send_sems.at[a, j],
            recv_sem=recv_sems.at[a, j], device_id=(x, y, 1 - c), device_id_type=MESH)
            for a in range(n) for j, (px, py) in enumerate(_other_chips(x, y))]

    def start(ins, outs, sems):
        for cp in copies(outs, sems, lax.axis_index("c")):
            cp.start()

    def wait(ins, outs, sems):
        for cp in copies(outs, sems, 1 - lax.axis_index("c")):
            cp.wait_recv()
        for cp in copies(outs, sems, lax.axis_index("c")):
            cp.wait_send()

    return _Ride(buffers, [_sds(b.shape, b.dtype) for b in buffers], {i: i for i in range(n)},
                 [pltpu.SemaphoreType.DMA((n, 3)), pltpu.SemaphoreType.DMA((n, 3))], start, wait)


def _sibling_swap_ride(grads):
    n = len(grads)

    def copies(ins, outs, sems):
        x, y, c = _place()
        return [pltpu.make_async_remote_copy(
            src_ref=ins[a].at[1 - c], dst_ref=outs[a], send_sem=sems[0].at[a], recv_sem=sems[1].at[a],
            device_id=(x, y, 1 - c), device_id_type=MESH) for a in range(n)]

    def start(ins, outs, sems):
        for cp in copies(ins, outs, sems):
            cp.start()

    def wait(ins, outs, sems):
        for cp in copies(ins, outs, sems):
            cp.wait()

    return _Ride(grads, [_sds(g.shape[1:], g.dtype) for g in grads], {},
                 [pltpu.SemaphoreType.DMA((n,)), pltpu.SemaphoreType.DMA((n,))], start, wait)


def _chip_swap_ride(sums):
    n = len(sums)

    def copies(ins, outs, sems, landed):
        send_sems, recv_sems, local_sems = sems
        x, y, c = _place()
        mine = 2 * x + y
        local = [pltpu.make_async_copy(ins[a].at[mine], outs[a].at[mine], local_sems.at[a]) for a in range(n)]
        remote = [pltpu.make_async_remote_copy(
            src_ref=ins[a].at[2 * px + py], dst_ref=outs[a].at[2 * px + py if landed else mine], send_sem=send_sems.at[a, j],
            recv_sem=recv_sems.at[a, j], device_id=(px, py, c), device_id_type=MESH)
            for a in range(n) for j, (px, py) in enumerate(_other_chips(x, y))]
        return local, remote

    def start(ins, outs, sems):
        local, remote = copies(ins, outs, sems, False)
        for cp in local + remote:
            cp.start()

    def wait(ins, outs, sems):
        local, sent = copies(ins, outs, sems, False)
        for cp in copies(ins, outs, sems, True)[1]:
            cp.wait_recv()
        for cp in sent:
            cp.wait_send()
        for cp in local:
            cp.wait()

    return _Ride(sums, [_sds(s.shape, s.dtype) for s in sums], {},
                 [pltpu.SemaphoreType.DMA((n, 3)), pltpu.SemaphoreType.DMA((n, 3)), pltpu.SemaphoreType.DMA((n,))], start, wait)


def _send_buffers(shards, name):
    n = len(shards)

    def body(*refs):
        for (w, transposed, rows, cols), w_ref, o_ref in zip(shards, refs[:n], refs[n:]):
            if transposed:
                c, r = w.shape
                padded = jnp.concatenate([w_ref[...], jnp.zeros((cols - c, r), F32)], axis=0) if cols > c else w_ref[...]
                o_ref[...] = padded.T.astype(BF16)
            else:
                r, c = w.shape
                if (r, c) != (rows, cols):
                    o_ref[...] = jnp.zeros((rows, cols), BF16)
                o_ref[:r, :c] = w_ref[...].astype(BF16)

    return _pallas_call(body, name=name, out_shape=[_sds((rows, cols), BF16) for _, _, rows, cols in shards])(
        *[w for w, _, _, _ in shards])


def _all_gather(shards, name):
    n = len(shards)

    def body(*refs):
        ins, outs = refs[:n], refs[n:2 * n]
        send_sems, recv_sems, local_sems = refs[2 * n:]
        x, y, c = _place()
        me, sibling = (x, y, c), (x, y, 1 - c)
        chips = _other_chips(x, y)

        def slot(a, px, py, pc):
            return outs[a].at[4 * px + 2 * py + pc]

        def copy(a, k, block, to, src=None):
            return pltpu.make_async_remote_copy(
                src_ref=slot(a, *block) if src is None else src, dst_ref=slot(a, *block),
                send_sem=send_sems.at[a, k], recv_sem=recv_sems.at[a, k], device_id=to, device_id_type=MESH)

        mine = [pltpu.make_async_copy(ins[a], slot(a, *me), local_sems.at[a]) for a in range(n)]
        for cp in mine:
            cp.start()
        first = []
        for a in range(n):
            first.append(copy(a, 0, me, sibling, src=ins[a]))
            first += [copy(a, 1 + j, me, (*chip, c), src=ins[a]) for j, chip in enumerate(chips)]
        for cp in first:
            cp.start()
        passed = []
        for j, chip in enumerate(chips):
            for a in range(n):
                copy(a, 1 + j, (*chip, c), me).wait_recv()
                onward = copy(a, 4 + j, (*chip, c), sibling)
                onward.start()
                passed.append(onward)
        for a in range(n):
            copy(a, 0, sibling, me).wait_recv()
            for j, chip in enumerate(chips):
                copy(a, 4 + j, (*chip, 1 - c), me).wait_recv()
        for cp in first + passed:
            cp.wait_send()
        for cp in mine:
            cp.wait()

    return _pallas_call(
        body, name=name, in_specs=[ANY] * n, out_specs=[ANY] * n,
        out_shape=[_sds((N_DEV,) + s.shape, s.dtype) for s in shards],
        scratch_shapes=[pltpu.SemaphoreType.DMA((n, 7)), pltpu.SemaphoreType.DMA((n, 7)), pltpu.SemaphoreType.DMA((n,))],
    )(*shards)


def _swap_with_sibling(grads, name):
    n = len(grads)

    def body(*refs):
        ins, outs = refs[:n], refs[n:2 * n]
        send_sems, recv_sems = refs[2 * n:]
        x, y, c = _place()
        copies = [pltpu.make_async_remote_copy(
            src_ref=ins[a].at[1 - c], dst_ref=outs[a], send_sem=send_sems.at[a], recv_sem=recv_sems.at[a],
            device_id=(x, y, 1 - c), device_id_type=MESH) for a in range(n)]
        for cp in copies:
            cp.start()
        for cp in copies:
            cp.wait()

    return _pallas_call(
        body, name=name, in_specs=[ANY] * n, out_specs=[ANY] * n,
        out_shape=[_sds(g.shape[1:], g.dtype) for g in grads],
        scratch_shapes=[pltpu.SemaphoreType.DMA((n,)), pltpu.SemaphoreType.DMA((n,))],
    )(*grads)


def _pair_sums(gs, rs, core, name):
    n_arrays = len(gs)

    def body(core_ref, *refs):
        for g_ref, r_ref, o_ref in zip(refs[:n_arrays], refs[n_arrays:2 * n_arrays], refs[2 * n_arrays:]):
            o_ref[...] = (g_ref[...].astype(F32) + r_ref[...].astype(F32)).astype(o_ref.dtype)

    def own(g):
        return pl.BlockSpec((None, None) + g.shape[2:], lambda p, core_ref: (core_ref[0], p, 0, 0))

    def chip(g):
        return pl.BlockSpec((None,) + g.shape[2:], lambda p, core_ref: (p, 0, 0))

    return _pallas_call(
        body, name=name,
        grid_spec=pltpu.PrefetchScalarGridSpec(
            num_scalar_prefetch=1, grid=(4,), in_specs=[own(g) for g in gs] + [chip(g) for g in gs],
            out_specs=[chip(g) for g in gs]),
        out_shape=[_sds(g.shape[1:], g.dtype) for g in gs], compiler_params=_cparams(dimension_semantics=("arbitrary",)),
    )(core, *gs, *rs)


def _adamw_math(w, g, m, v):
    m = ADAM_B1 * m + (1.0 - ADAM_B1) * g
    v = ADAM_B2 * v + (1.0 - ADAM_B2) * (g * g)
    m_hat = m / (1.0 - ADAM_B1 ** ADAM_STEP)
    v_hat = v / (1.0 - ADAM_B2 ** ADAM_STEP)
    return -ADAM_LR * (m_hat / (jnp.sqrt(v_hat) + ADAM_EPS) + ADAM_WD * w), m, v


def _adamw(w, m, v, parts, name):
    r, c = w.shape
    tr = r if r <= 512 else 256
    n_parts, pr, pc = parts.shape
    assert r % tr == 0 and (tr == r or pr == r)

    def body(w_ref, m_ref, v_ref, p_ref, g_out, d_out, m_out, v_out):
        g = p_ref[0, :tr, :c].astype(F32)
        for p in range(1, n_parts):
            g = g + p_ref[p, :tr, :c].astype(F32)
        g_out[...] = g
        d_out[...], m_out[...], v_out[...] = _adamw_math(w_ref[...], g, m_ref[...], v_ref[...])

    tile = pl.BlockSpec((tr, c), lambda i: (i, 0))
    part_tile = pl.BlockSpec((n_parts, pr if tr == r else tr, pc), lambda i: (0, i, 0))
    out = _sds((r, c), F32)
    return _pallas_call(
        body, name=name, grid=(r // tr,), in_specs=[tile, tile, tile, part_tile], out_specs=(tile,) * 4,
        out_shape=(out,) * 4, compiler_params=_cparams(dimension_semantics=("arbitrary",)),
    )(w, m, v, parts)


def _adamw_transposed(w_t, m_t, v_t, parts, name):
    c, r = w_t.shape
    tr = 256
    n_parts, _, pc = parts.shape

    def body(w_ref, m_ref, v_ref, p_ref, g_out, d_out, m_out, v_out):
        g = p_ref[0].astype(F32)
        for p in range(1, n_parts):
            g = g + p_ref[p].astype(F32)
        g = g.T[:c]
        g_out[...] = g
        d_out[...], m_out[...], v_out[...] = _adamw_math(w_ref[...], g, m_ref[...], v_ref[...])

    tile = pl.BlockSpec((c, tr), lambda i: (0, i))
    out = _sds((c, r), F32)
    return _pallas_call(
        body, name=name, grid=(r // tr,), in_specs=[tile, tile, tile, pl.BlockSpec((n_parts, tr, pc), lambda i: (0, i, 0))],
        out_specs=(tile,) * 4, out_shape=(out,) * 4, compiler_params=_cparams(dimension_semantics=("arbitrary",)),
    )(w_t, m_t, v_t, parts)


SMALL = ("ssm_a_re", "ssm_a_im", "ssm_log_dt", "ssm_b_re", "ssm_b_im", "ssm_c_re", "ssm_c_im", "ssm_d",
         "ln1_g", "ln1_b", "ln2_g", "ln2_b")


def _pack_rows(arrays):
    rows = []
    for a in arrays:
        flat = a.reshape(-1)
        rows.append(jnp.pad(flat, (0, -flat.shape[0] % 128)).reshape(-1, 128))
    packed = jnp.concatenate(rows, axis=0)
    return jnp.pad(packed, ((0, -packed.shape[0] % 8), (0, 0)))


def _unpack_rows(packed, shapes):
    out, row = [], 0
    for shape in shapes:
        size = math.prod(shape)
        n_rows = -(-size // 128)
        out.append(packed[row:row + n_rows].reshape(-1)[:size].reshape(shape))
        row += n_rows
    return out


def _sum_devices(parts):
    def body(p_ref, o_ref):
        total = p_ref[0]
        for dev in range(1, N_DEV):
            total = total + p_ref[dev]
        o_ref[...] = total

    return _pallas_call(body, name="sum_devices", out_shape=_sds(parts.shape[1:], F32))(parts)


def _adamw_replicated(ws, ms, vs, gs):
    n = len(ws)

    def body(*refs):
        w_refs, m_refs, v_refs, g_refs, d_out, m_out, v_out = (refs[i * n:(i + 1) * n] for i in range(7))
        for i in range(n):
            d_out[i][...], m_out[i][...], v_out[i][...] = _adamw_math(w_refs[i][...], g_refs[i][...], m_refs[i][...], v_refs[i][...])

    out = _pallas_call(body, name="adamw_replicated", out_shape=[_sds(w.shape, F32) for w in ws] * 3,
                       compiler_params=_cparams())(*ws, *ms, *vs, *gs)
    return out[:n], out[n:2 * n], out[2 * n:]


def kernel(x, w_in, b_gate, w_attn_br, w_ssm_br, w_out, ssm_a_re, ssm_a_im, ssm_log_dt, ssm_b_re, ssm_b_im, ssm_c_re, ssm_c_im, ssm_d, w_glu, ln1_g, ln1_b, w_ff_gate, w_ff_up, w_ff_down, ln2_g, ln2_b, loss_target, m_w_in, m_b_gate, m_w_attn_br, m_w_ssm_br, m_w_out, m_ssm_a_re, m_ssm_a_im, m_ssm_log_dt, m_ssm_b_re, m_ssm_b_im, m_ssm_c_re, m_ssm_c_im, m_ssm_d, m_w_glu, m_ln1_g, m_ln1_b, m_w_ff_gate, m_w_ff_up, m_w_ff_down, m_ln2_g, m_ln2_b, v_w_in, v_b_gate, v_w_attn_br, v_w_ssm_br, v_w_out, v_ssm_a_re, v_ssm_a_im, v_ssm_log_dt, v_ssm_b_re, v_ssm_b_im, v_ssm_c_re, v_ssm_c_im, v_ssm_d, v_w_glu, v_ln1_g, v_ln1_b, v_w_ff_gate, v_w_ff_up, v_w_ff_down, v_ln2_g, v_ln2_b):
    given = dict(locals())
    x2, target = x[0], loss_target[0]
    core = lax.axis_index("c").astype(jnp.int32).reshape(1)

    sharded = ("w_in", "w_attn_br", "w_ssm_br", "w_glu", "w_ff_gate", "w_ff_up", "b_gate", "w_out", "w_ff_down")
    send_shape = dict(w_in=(D_MODEL, 896), w_attn_br=(ATTN_WIDTH, 128), w_ssm_br=(SSM_WIDTH, 128), w_glu=(SSM_WIDTH, 128),
                      w_out=(128, D_MODEL), w_ff_gate=(D_MODEL, FF_PAD), w_ff_up=(D_MODEL, FF_PAD), w_ff_down=(FF_PAD, D_MODEL))
    local = {k: given[k][0] for k in sharded}
    narrow = ("w_ff_gate", "w_ff_up")
    def to_send(k):
        return (local[k].T, True, *send_shape[k]) if k in narrow else (local[k], False, *send_shape[k])

    later = [k for k in sharded if k not in ("w_in", "b_gate")]
    sends = dict(zip(["w_in"] + later, _send_buffers([to_send("w_in")], "send_w_in")
                     + _send_buffers([to_send(k) for k in later], "send_weights")))
    sends["b_gate"] = local["b_gate"]
    mixer_weights = ("w_attn_br", "w_ssm_br", "w_glu", "b_gate", "w_out")
    ff_weights = ("w_ff_gate", "w_ff_up", "w_ff_down")
    wt = {}
    wt["w_in"], = _all_gather([sends["w_in"]], "gather_w_in")

    a_re, a_im, log_dt = ssm_a_re[0], ssm_a_im[0], ssm_log_dt[0].reshape(SSM_GROUPS, 1)
    b_re_t, b_im_t = ssm_b_re[0].transpose(0, 2, 1), ssm_b_im[0].transpose(0, 2, 1)
    abar_re, abar_im, e_re, e_im, bbar_re_t, bbar_im_t = _ssm_prep(a_re, a_im, log_dt, b_re_t, b_im_t)
    bmat, cmat, a_chunks = _ssm_tables(abar_re, abar_im, bbar_re_t, bbar_im_t, ssm_c_re[0], ssm_c_im[0])
    cos_t, sin_t = _rope_tables()

    proj, *partly = _proj(x2, wt["w_in"], _gather_first_level([sends[k] for k in mixer_weights]))
    attn, lse, *landed = _attn_fwd(proj, cos_t, sin_t,
                                   _gather_second_level(partly) + _gather_first_level([sends[k] for k in ff_weights]))
    wt.update(zip(mixer_weights, landed[:len(mixer_weights)]))
    ys, states, *landed = _ssm_fwd(proj, bmat, cmat, a_chunks, ssm_d, _gather_second_level(landed[len(mixer_weights):]))
    wt.update(zip(ff_weights, landed))
    wt["w_out"] = wt["w_out"].reshape(D_MODEL, D_MODEL)
    wt["w_ff_down"] = wt["w_ff_down"].reshape(D_FF_PAD, D_MODEL)
    b_gate_full = wt["b_gate"]
    h, xhat1, rstd1, glu, y_attn, y_ssm = _mixer_out(attn, ys, proj, x2, wt["w_attn_br"], wt["w_ssm_br"], wt["w_glu"],
                                                      wt["w_out"], b_gate_full, ln1_g, ln1_b)
    ff_a, ff_b, ff_f = _ff_up(h, wt["w_ff_gate"], wt["w_ff_up"])
    dr2, d_ln2_g, d_ln2_b, loss_lanes = _ff_down_loss(ff_f, wt["w_ff_down"], h, target, ln2_g, ln2_b)

    def pair_sums(names, contrib, from_sibling):
        return _pair_sums([contrib[k] for k in names], from_sibling, core, "pair_sums_" + names[0])

    d_a, d_b = _ff_down_bwd(dr2, wt["w_ff_down"], ff_a, ff_b)
    contrib = dict(w_ff_gate=_weight_grad(h, d_a, "wgrad_w_ff_gate", FF_PAD),
                   w_ff_up=_weight_grad(h, d_b, "wgrad_w_ff_up", FF_PAD),
                   w_ff_down=_weight_grad(ff_f, dr2, "wgrad_w_ff_down"))
    dr1, d_ln1_g, d_ln1_b, *from_sibling = _ff_up_bwd(
        d_a, d_b, wt["w_ff_gate"], wt["w_ff_up"], dr2, xhat1, rstd1, ln1_g, _sibling_swap_ride([contrib[k] for k in ff_weights]))
    ff_sums = pair_sums(ff_weights, contrib, from_sibling)

    d_ya, d_yssm, d_proj, d_attn, d_glu, d_ys, mixed, y_s, gy, d_bg = _mixer_bwd(
        dr1, proj, y_attn, y_ssm, glu, ys, wt["w_attn_br"], wt["w_ssm_br"], wt["w_glu"], wt["w_out"], b_gate_full)
    contrib.update(w_attn_br=_weight_grad(attn, d_ya, "wgrad_w_attn_br", 128),
                   w_ssm_br=_weight_grad(y_s, d_yssm, "wgrad_w_ssm_br", 128),
                   w_glu=_weight_grad(gy, d_glu, "wgrad_w_glu", 128),
                   w_out=_weight_grad(mixed, dr1, "wgrad_w_out"),
                   b_gate=d_bg.reshape(2, 4, 2, 128).transpose(2, 1, 0, 3))
    d_proj, *landed = _attn_bwd(proj, cos_t, sin_t, attn, lse, d_attn, d_proj,
                                _chip_swap_ride(ff_sums) + _sibling_swap_ride([contrib[k] for k in mixer_weights]))
    parts = dict(zip(ff_weights, landed[:len(ff_weights)]))
    mixer_sums = pair_sums(mixer_weights, contrib, landed[len(ff_weights):])
    d_proj, d_bmat, d_cmat, d_abar, d_skip, *landed = _ssm_bwd(d_ys, proj, states, bmat, cmat, a_chunks, ssm_d, d_proj,
                                                               _chip_swap_ride(mixer_sums))
    parts.update(zip(mixer_weights, landed))

    gbb_re_t, gbb_im_t = _block_diag_parts(d_bmat, True)
    gc_re, gc_im = _block_diag_parts(d_cmat, False)
    ga_re = d_abar[:, 0, :CHUNK_STATES].reshape(SSM_GROUPS, SSM_STATE)
    ga_im = d_abar[:, 0, CHUNK_STATES:].reshape(SSM_GROUPS, SSM_STATE)
    g_a_re, g_a_im, g_log_dt, g_b_re_t, g_b_im_t = _ssm_param_bwd(
        a_re, a_im, log_dt, b_re_t, b_im_t, abar_re, abar_im, e_re, e_im, ga_re, ga_im, gbb_re_t, gbb_im_t)
    mine = [g_a_re, g_a_im, g_log_dt, g_b_re_t, g_b_im_t, gc_re, -gc_im,
            d_skip, d_ln1_g, d_ln1_b, d_ln2_g, d_ln2_b]
    small_packed = _pack_rows(mine + [loss_lanes])

    contrib["w_in"], small_partly = _weight_grad(x2, d_proj, "wgrad_w_in", 896, _gather_first_level([small_packed]))
    w_in_sum = pair_sums(["w_in"], contrib, _swap_with_sibling([contrib["w_in"]], "swap_w_in_with_sibling"))
    grad_x, parts["w_in"], every = _grad_x(d_proj, wt["w_in"], dr1,
                                          _chip_swap_ride(w_in_sum) + _gather_second_level([small_partly]))

    grads, deltas, new_m, new_v = {}, {}, {}, {}
    for k in sharded:
        w2 = local[k]
        if k in narrow:
            out = _adamw_transposed(w2.T, given["m_" + k][0].T, given["v_" + k][0].T, parts[k], "adamw_" + k)
            out = [o.T for o in out]
        else:
            out = _adamw(w2, given["m_" + k][0], given["v_" + k][0], parts[k], "adamw_" + k)
        grads[k], deltas[k], new_m[k], new_v[k] = (o.reshape((1,) + w2.shape) for o in out)

    def held(k, a):
        return a.transpose(0, 1, 3, 2) if k in ("ssm_b_re", "ssm_b_im") else a

    *small_grads, loss_sum = _unpack_rows(_sum_devices(every), [held(k, given[k]).shape for k in SMALL] + [(1, 128)])
    small = _adamw_replicated([held(k, given[k]) for k in SMALL], [held(k, given["m_" + k]) for k in SMALL],
                              [held(k, given["v_" + k]) for k in SMALL], small_grads)
    for res, values in zip((grads, deltas, new_m, new_v), (small_grads,) + small):
        res.update((k, held(k, a)) for k, a in zip(SMALL, values))
    loss = loss_sum[0, 0]

    order = ("w_in", "b_gate", "w_attn_br", "w_ssm_br", "w_out", "ssm_a_re", "ssm_a_im", "ssm_log_dt", "ssm_b_re", "ssm_b_im",
             "ssm_c_re", "ssm_c_im", "ssm_d", "w_glu", "ln1_g", "ln1_b", "w_ff_gate", "w_ff_up", "w_ff_down", "ln2_g", "ln2_b")
    return (loss, grad_x[None], *[grads[k] for k in order], *[deltas[k] for k in order], *[new_m[k] for k in order],
            *[new_v[k] for k in order])
```

```python
import functools
import math

import jax
import jax.numpy as jnp
import numpy as np
from jax import lax
from jax.experimental import pallas as pl
from jax.experimental.pallas import tpu as pltpu

F32 = jnp.float32
BF16 = jnp.bfloat16

N_DEV = 8
SEQ = 2048
D_MODEL = 1024
HEAD_DIM = 64
ATTN_WIDTH = 512
QKV_WIDTH = 1536
SSM_WIDTH = 512
SSM_GROUPS = 32
SSM_GROUP = 16
SSM_STATE = 64
IN_WIDTH = 7168
D_FF = 2816
FF_SHARD = D_FF // N_DEV
FF_PAD = 384
D_FF_PAD = FF_PAD * N_DEV
DN_ALPHA = 2.0 ** 0.25
LN_EPS = 1e-5
NEG_INF = -1e30
ROPE_THETA = 10000.0
BLOCK = 128
GROUPS = ((1, 16), (4, 4), (16, 1))

ADAM_LR = 0.001
ADAM_B1 = 0.9
ADAM_B2 = 0.999
ADAM_EPS = 1e-08
ADAM_WD = 0.01
ADAM_STEP = 10

VMEM_LIMIT = 56 * 1024 * 1024


_pallas_call = pl.pallas_call


def _cparams(**kw):
    return pltpu.CompilerParams(vmem_limit_bytes=VMEM_LIMIT, **kw)


def _dot(a, b):
    return jnp.dot(a, b, preferred_element_type=F32)


def _dot_nt(a, b):
    return lax.dot_general(a, b, (((1,), (1,)), ((), ())), preferred_element_type=F32)


def _side_by_side(w_ref, row=None):
    rows = slice(None) if row is None else pl.ds(row, 1)
    return jnp.concatenate([w_ref[i, rows, :] for i in range(w_ref.shape[0])], axis=1)


def _dot_tn(a, b):
    return lax.dot_general(a, b, (((0,), (0,)), ((), ())), preferred_element_type=F32)


def _rope_tables():
    half = HEAD_DIM // 2
    inv_freq = np.float32(ROPE_THETA) ** (-np.arange(half, dtype=np.float32) / np.float32(half))
    ang = np.arange(SEQ, dtype=np.float32)[:, None] * inv_freq[None, :]
    cos, sin = np.cos(ang).astype(np.float32), np.sin(ang).astype(np.float32)
    tables = np.tile(cos, (1, 4)), np.tile(np.concatenate([-sin, sin], axis=1), (1, 2))

    def by_phase(t):
        return np.stack([t.reshape(SEQ // d, d, 128).transpose(1, 0, 2).reshape(SEQ, 128) for d, _ in GROUPS])

    return jnp.asarray(by_phase(tables[0])), jnp.asarray(by_phase(tables[1]))


def _swap_halves(x):
    lane = lax.broadcasted_iota(jnp.int32, x.shape, 1)
    return jnp.where((lane & 63) < 32, pltpu.roll(x, 96, axis=1), pltpu.roll(x, 32, axis=1))


def _group_rows(d, nb, r, i):
    src = pl.ds(i * BLOCK, BLOCK) if d == 1 else pl.ds(r + i * BLOCK * d, BLOCK, stride=d)
    return src, pl.ds((r * nb + i) * BLOCK, BLOCK)


def _attn_masks():
    a_idx = lax.broadcasted_iota(jnp.int32, (2 * BLOCK, 2 * BLOCK), 0) & (BLOCK - 1)
    c_idx = lax.broadcasted_iota(jnp.int32, (2 * BLOCK, 2 * BLOCK), 1)
    cur_ok = jnp.logical_and(c_idx >= BLOCK, c_idx - BLOCK <= a_idx)
    prev_ok = jnp.logical_and(c_idx < BLOCK, c_idx >= a_idx)
    lane = lax.broadcasted_iota(jnp.int32, (BLOCK, 128), 1)
    return cur_ok, prev_ok, lane < HEAD_DIM


def _stack_heads(t, head0):
    zero = jnp.zeros_like(t)
    return jnp.concatenate([jnp.where(head0, t, zero), jnp.where(head0, zero, t)], axis=0)


def _unstack_heads(t2, head0):
    return jnp.where(head0, t2[:BLOCK], t2[BLOCK:])


def _attn_fwd(proj, cos_t, sin_t, ride=None):
    def body(q0, q1, q2, k0, k1, k2, v0, v1, v2, cos_ref, sin_ref, attn_ref, lse_ref,
             qs, ks, vs, os_, ms, ls, acc, mnat, lnat):
        cur_ok, prev_ok, head0 = _attn_masks()
        ks[:BLOCK, :] = jnp.zeros((BLOCK, 128), BF16)
        vs[:BLOCK, :] = jnp.zeros((BLOCK, 128), BF16)
        for g, (d, nb) in enumerate(GROUPS):
            q_ref, k_ref, v_ref = (q0, q1, q2)[g], (k0, k1, k2)[g], (v0, v1, v2)[g]
            for r in range(d):
                for i in range(nb):
                    src, dst = _group_rows(d, nb, r, i)
                    below = pl.ds(dst.start + BLOCK, BLOCK)
                    c, s = cos_ref[g, dst, :], sin_ref[g, dst, :]
                    q = q_ref[src, :]
                    k = k_ref[src, :]
                    qs[dst, :] = ((q * c + _swap_halves(q) * s) * 0.125).astype(BF16)
                    ks[below, :] = (k * c + _swap_halves(k) * s).astype(BF16)
                    vs[below, :] = v_ref[src, :].astype(BF16)

            def block(b, carry, nb=nb):
                has_prev = (b & (nb - 1)) > 0
                cur = pl.ds(pl.multiple_of(b * BLOCK, BLOCK), BLOCK)
                window = pl.ds(pl.multiple_of(b * BLOCK, BLOCK), 2 * BLOCK)
                valid = jnp.logical_or(cur_ok, jnp.logical_and(prev_ok, has_prev))
                s = jnp.where(valid, _dot_nt(_stack_heads(qs[cur, :], head0), ks[window, :]), NEG_INF)
                m = jnp.max(s, axis=1, keepdims=True)
                p = jnp.exp(s - m)
                os_[cur, :] = _unstack_heads(_dot(p.astype(BF16), vs[window, :]), head0)
                ms[cur, :] = _unstack_heads(m, head0)
                ls[cur, :] = _unstack_heads(jnp.sum(p, axis=1, keepdims=True), head0)
                return carry

            lax.fori_loop(0, SEQ // BLOCK, block, 0, unroll=16)

            for r in range(d):
                for i in range(nb):
                    src, dst = _group_rows(d, nb, r, i)
                    if g == 0:
                        acc[src, :], mnat[src, :], lnat[src, :] = os_[dst, :], ms[dst, :], ls[dst, :]
                    else:
                        m_old, m_g = mnat[src, :], ms[dst, :]
                        m_new = jnp.maximum(m_old, m_g)
                        a_old, a_g = jnp.exp(m_old - m_new), jnp.exp(m_g - m_new)
                        acc[src, :] = a_old * acc[src, :] + a_g * os_[dst, :]
                        lnat[src, :] = a_old * lnat[src, :] + a_g * ls[dst, :]
                        mnat[src, :] = m_new
        for i in range(SEQ // BLOCK):
            rows = pl.ds(i * BLOCK, BLOCK)
            l = lnat[rows, :]
            attn_ref[rows, :] = acc[rows, :] / l
            lse_ref[rows, :] = mnat[rows, :] + jnp.log(l)

    def col(base):
        return pl.BlockSpec((SEQ, 128), lambda hp, base=base: (0, base + hp))

    in_specs = [col(g * 4) for g in range(3)] + [col(12 + g * 4) for g in range(3)] + [col(24 + g * 4) for g in range(3)]
    table = pl.BlockSpec((3, SEQ, 128), lambda hp: (0, 0, 0), pipeline_mode=pl.Buffered(1))
    out = pl.BlockSpec((SEQ, 128), lambda hp: (0, hp))
    return _call(
        body, "attn_fwd", (4,), in_specs + [table, table], [out, out],
        [_sds((SEQ, ATTN_WIDTH), F32), _sds((SEQ, ATTN_WIDTH), F32)],
        [pltpu.VMEM((SEQ, 128), BF16)] + [pltpu.VMEM((SEQ + BLOCK, 128), BF16)] * 2 + [pltpu.VMEM((SEQ, 128), F32)] * 6,
        [proj] * 9 + [cos_t, sin_t], ride)


def _attn_bwd_group_body(g):
    d, nb = GROUPS[g]

    def body(q_ref, k_ref, v_ref, cos_ref, sin_ref, lse_ref, dattn_ref, dsum_ref, dproj_ref,
             qs, ks, vs, dos, lss, dss, dqs, dks, dvs, stage, outs, sems):
        cur_ok, prev_ok, head0 = _attn_masks()
        ks[:BLOCK, :] = jnp.zeros((BLOCK, 128), BF16)
        vs[:BLOCK, :] = jnp.zeros((BLOCK, 128), BF16)
        dks[:BLOCK, :] = jnp.zeros((BLOCK, 128), F32)
        dvs[:BLOCK, :] = jnp.zeros((BLOCK, 128), F32)
        for r in range(d):
            for i in range(nb):
                src, dst = _group_rows(d, nb, r, i)
                below = pl.ds(dst.start + BLOCK, BLOCK)
                c, s = cos_ref[g, dst, :], sin_ref[g, dst, :]
                q = q_ref[src, :]
                k = k_ref[src, :]
                qs[dst, :] = ((q * c + _swap_halves(q) * s) * 0.125).astype(BF16)
                ks[below, :] = (k * c + _swap_halves(k) * s).astype(BF16)
                vs[below, :] = v_ref[src, :].astype(BF16)
                dos[dst, :] = dattn_ref[src, :].astype(BF16)
                dss[dst, :] = dsum_ref[src, :]
                lss[dst, :] = lse_ref[src, :]
                dks[below, :] = jnp.zeros((BLOCK, 128), F32)
                dvs[below, :] = jnp.zeros((BLOCK, 128), F32)

        def per_head_column(t):
            return jnp.concatenate([jnp.max(jnp.where(head0, t, NEG_INF), axis=1, keepdims=True),
                                    jnp.max(jnp.where(head0, NEG_INF, t), axis=1, keepdims=True)], axis=0)

        def block(b, carry):
            has_prev = (b & (nb - 1)) > 0
            cur = pl.ds(pl.multiple_of(b * BLOCK, BLOCK), BLOCK)
            window = pl.ds(pl.multiple_of(b * BLOCK, BLOCK), 2 * BLOCK)
            valid = jnp.logical_or(cur_ok, jnp.logical_and(prev_ok, has_prev))
            q2, do2 = _stack_heads(qs[cur, :], head0), _stack_heads(dos[cur, :], head0)
            kw, vw = ks[window, :], vs[window, :]
            s = jnp.where(valid, _dot_nt(q2, kw), NEG_INF)
            p = jnp.exp(s - per_head_column(lss[cur, :]))
            ds = (p * (_dot_nt(do2, vw) - per_head_column(dss[cur, :]))).astype(BF16)
            dvs[window, :] += _dot_tn(p.astype(BF16), do2)
            dks[window, :] += _dot_tn(ds, q2)
            dqs[cur, :] = _unstack_heads(_dot(ds, kw), head0)
            return carry

        lax.fori_loop(0, SEQ // BLOCK, block, 0, unroll=8)

        hp = pl.program_id(0)
        copies = []
        for kind in range(3):
            for r in range(d):
                for i in range(nb):
                    src, dst = _group_rows(d, nb, r, i)
                    below = pl.ds(dst.start + BLOCK, BLOCK)
                    if kind == 2:
                        stage[src, :] = dvs[below, :]
                    else:
                        c, s = cos_ref[g, dst, :], sin_ref[g, dst, :]
                        t = dqs[dst, :] * 0.125 if kind == 0 else dks[below, :]
                        stage[src, :] = t * c - _swap_halves(t) * s
            for i in range(SEQ // MM_ROWS):
                rows = pl.ds(i * MM_ROWS, MM_ROWS)
                outs[kind, rows, :] = stage[rows, :].astype(BF16)
            column = pl.multiple_of((kind * 12 + g * 4 + hp) * 128, 128)
            copies.append(pltpu.make_async_copy(outs.at[kind], dproj_ref.at[:, pl.ds(column, 128)], sems.at[kind]))
            copies[-1].start()
        for cp in copies:
            cp.wait()

    return body


def _attn_bwd(proj, cos_t, sin_t, attn, lse, dattn, dproj, ride=None):
    groups = [_attn_bwd_group_body(g) for g in range(3)]

    def body(q0, q1, q2, k0, k1, k2, v0, v1, v2, cos_ref, sin_ref, attn_ref, lse_ref, dattn_ref, dproj_in, dproj_ref,
             dsum, *scratch):
        del dproj_in
        head0 = _attn_masks()[2]
        for i in range(SEQ // BLOCK):
            rows = pl.ds(i * BLOCK, BLOCK)
            prod = dattn_ref[rows, :] * attn_ref[rows, :]
            d0 = jnp.sum(jnp.where(head0, prod, 0.0), axis=1, keepdims=True)
            d1 = jnp.sum(jnp.where(head0, 0.0, prod), axis=1, keepdims=True)
            dsum[rows, :] = jnp.where(head0, d0, d1)
        for g in range(3):
            groups[g]((q0, q1, q2)[g], (k0, k1, k2)[g], (v0, v1, v2)[g], cos_ref, sin_ref, lse_ref, dattn_ref, dsum,
                      dproj_ref, *scratch)

    def col(base):
        return pl.BlockSpec((SEQ, 128), lambda hp, base=base: (0, base + hp))

    table = pl.BlockSpec((3, SEQ, 128), lambda hp: (0, 0, 0), pipeline_mode=pl.Buffered(1))
    return _call(
        body, "attn_bwd", (4,),
        [col(g * 4) for g in range(3)] + [col(12 + g * 4) for g in range(3)] + [col(24 + g * 4) for g in range(3)]
        + [table, table, col(0), col(0), col(0), ANY],
        [ANY], [_sds((SEQ, IN_WIDTH), BF16)],
        [pltpu.VMEM((SEQ, 128), F32)]
        + [pltpu.VMEM((SEQ, 128), BF16)] + [pltpu.VMEM((SEQ + BLOCK, 128), BF16)] * 2 + [pltpu.VMEM((SEQ, 128), BF16)]
        + [pltpu.VMEM((SEQ, 128), F32)] * 3 + [pltpu.VMEM((SEQ + BLOCK, 128), F32)] * 2 + [pltpu.VMEM((SEQ, 128), F32)]
        + [pltpu.VMEM((3, SEQ, 128), BF16), pltpu.SemaphoreType.DMA((3,))],
        [proj] * 9 + [cos_t, sin_t, attn, lse, dattn, dproj], ride, aliases={14: 0})


SSM_CHUNKS = 4
CHUNK_STATES = 512
SCAN_ROWS = 8
U_COL = (3 * QKV_WIDTH) // 128


def _cmul(xr, xi, yr, yi):
    return xr * yr - xi * yi, xr * yi + xi * yr


def _ssm_prep(a_re, a_im, log_dt, b_re_t, b_im_t):
    def body(ar_ref, ai_ref, ldt_ref, br_ref, bi_ref, abr_ref, abi_ref, er_ref, ei_ref, bbr_ref, bbi_ref):
        ar, ai = ar_ref[...], ai_ref[...]
        dt = jnp.exp(ldt_ref[...])
        mag = jnp.exp(ar * dt)
        abr, abi = mag * jnp.cos(ai * dt), mag * jnp.sin(ai * dt)
        den = ar * ar + ai * ai
        nr, ni = abr - 1.0, abi
        er, ei = (nr * ar + ni * ai) / den, (ni * ar - nr * ai) / den
        abr_ref[...], abi_ref[...], er_ref[...], ei_ref[...] = abr, abi, er, ei
        er3, ei3 = er[:, None, :], ei[:, None, :]
        br, bi = br_ref[...], bi_ref[...]
        bbr_ref[...] = er3 * br - ei3 * bi
        bbi_ref[...] = er3 * bi + ei3 * br

    gp = jax.ShapeDtypeStruct(a_re.shape, F32)
    gb = jax.ShapeDtypeStruct(b_re_t.shape, F32)
    return _pallas_call(body, name="ssm_prep", out_shape=(gp, gp, gp, gp, gb, gb))(a_re, a_im, log_dt, b_re_t, b_im_t)


def _ssm_param_bwd(a_re, a_im, log_dt, b_re_t, b_im_t, abar_re, abar_im, e_re, e_im, ga_re, ga_im, gbb_re_t, gbb_im_t):
    def body(ar_ref, ai_ref, ldt_ref, br_ref, bi_ref, abr_ref, abi_ref, er_ref, ei_ref, gar_ref, gai_ref, gbr_ref, gbi_ref,
             o_ar, o_ai, o_ldt, o_br, o_bi):
        ar, ai = ar_ref[...], ai_ref[...]
        dt = jnp.exp(ldt_ref[...])
        er, ei = er_ref[...], ei_ref[...]
        br, bi, gbr, gbi = br_ref[...], bi_ref[...], gbr_ref[...], gbi_ref[...]
        er3, ei3 = er[:, None, :], ei[:, None, :]
        o_br[...] = er3 * gbr + ei3 * gbi
        o_bi[...] = er3 * gbi - ei3 * gbr
        ge_r = jnp.sum(br * gbr + bi * gbi, axis=1)
        ge_i = jnp.sum(br * gbi - bi * gbr, axis=1)
        den = ar * ar + ai * ai
        ilr, ili = ar / den, -ai / den
        t_r, t_i = _cmul(ilr, -ili, ge_r, ge_i)
        gab_r, gab_i = gar_ref[...] + t_r, gai_ref[...] + t_i
        gz_r, gz_i = _cmul(abr_ref[...], -abi_ref[...], gab_r, gab_i)
        el_r, el_i = _cmul(er, ei, ilr, ili)
        u_r, u_i = _cmul(el_r, -el_i, ge_r, ge_i)
        o_ar[...] = dt * gz_r - u_r
        o_ai[...] = dt * gz_i - u_i
        o_ldt[...] = jnp.sum(gz_r * ar + gz_i * ai, axis=1, keepdims=True) * dt

    gp = jax.ShapeDtypeStruct(a_re.shape, F32)
    gb = jax.ShapeDtypeStruct(b_re_t.shape, F32)
    return _pallas_call(body, name="ssm_param_bwd", out_shape=(gp, gp, jax.ShapeDtypeStruct(log_dt.shape, F32), gb, gb))(
        a_re, a_im, log_dt, b_re_t, b_im_t, abar_re, abar_im, e_re, e_im, ga_re, ga_im, gbb_re_t, gbb_im_t)


def _block_diag(blocks_re, blocks_im, sign_im, rows_are_channels):
    both = jnp.stack([blocks_re, sign_im * blocks_im]).reshape(2, SSM_CHUNKS, 8, SSM_GROUP, SSM_STATE)
    eye = jnp.eye(8, dtype=F32)
    if rows_are_channels:
        return jnp.einsum("rcghp,gk->cghrkp", both, eye).reshape(SSM_CHUNKS, 128, 2 * CHUNK_STATES)
    return jnp.einsum("rcghp,gk->crkpgh", both, eye).reshape(SSM_CHUNKS, 2 * CHUNK_STATES, 128)


def _block_diag_parts(mat, rows_are_channels):
    if rows_are_channels:
        six = mat.reshape(SSM_CHUNKS, 8, SSM_GROUP, 2, 8, SSM_STATE)
        parts = jnp.einsum("cghrgp->rcghp", six)
    else:
        six = mat.reshape(SSM_CHUNKS, 2, 8, SSM_STATE, 8, SSM_GROUP)
        parts = jnp.einsum("crgpgh->rcghp", six)
    parts = parts.reshape(2, SSM_GROUPS, SSM_GROUP, SSM_STATE)
    return parts[0], parts[1]


def _scan_consts(a_ref, conj, reverse):
    ar = jnp.broadcast_to(a_ref[:, :CHUNK_STATES], (SCAN_ROWS, CHUNK_STATES))
    ai = jnp.broadcast_to(a_ref[:, CHUNK_STATES:], (SCAN_ROWS, CHUNK_STATES))
    if conj:
        ai = -ai
    row = lax.broadcasted_iota(jnp.int32, (SCAN_ROWS, CHUNK_STATES), 0)
    if reverse:
        row = SCAN_ROWS - 1 - row
    zero = jnp.zeros_like(ar)
    steps = []
    pr, pi = ar, ai
    for shift in (1, 2, 4):
        keep = row >= shift
        steps.append((SCAN_ROWS - shift if reverse else shift, jnp.where(keep, pr, zero), jnp.where(keep, pi, zero)))
        pr, pi = _cmul(pr, pi, pr, pi)
    first = row == 0
    return steps, (jnp.where(first, ar, zero), jnp.where(first, ai, zero)), first


def _scan_tile(xr, xi, prev_r, prev_i, steps, carry_in, reverse):
    edge = SCAN_ROWS - 1 if reverse else 1
    cr, ci = pltpu.roll(prev_r, edge, axis=0), pltpu.roll(prev_i, edge, axis=0)
    xr, xi = xr + carry_in[0] * cr - carry_in[1] * ci, xi + carry_in[0] * ci + carry_in[1] * cr
    for shift, mr, mi in steps:
        sr, si = pltpu.roll(xr, shift, axis=0), pltpu.roll(xi, shift, axis=0)
        xr, xi = xr + mr * sr - mi * si, xi + mr * si + mi * sr
    return xr, xi


MM_ROWS = 256


def _ssm_fwd(proj, bmat, cmat, a_chunks, d_skip, ride=None):
    def body(u_ref, b_ref, c_ref, a_ref, d_ref, y_ref, h_ref):
        for i in range(SEQ // MM_ROWS):
            rows = pl.ds(i * MM_ROWS, MM_ROWS)
            h_ref[rows, :] = _dot(u_ref[rows, :].astype(BF16), b_ref[...])
        steps, carry_in, _ = _scan_consts(a_ref, conj=False, reverse=False)

        def tile(k, carry):
            rows = pl.ds(pl.multiple_of(k * SCAN_ROWS, SCAN_ROWS), SCAN_ROWS)
            xr, xi = _scan_tile(h_ref[rows, :CHUNK_STATES], h_ref[rows, CHUNK_STATES:], carry[0], carry[1], steps, carry_in, False)
            h_ref[rows, :CHUNK_STATES] = xr
            h_ref[rows, CHUNK_STATES:] = xi
            return xr, xi

        zero = jnp.zeros((SCAN_ROWS, CHUNK_STATES), F32)
        lax.fori_loop(0, SEQ // SCAN_ROWS, tile, (zero, zero), unroll=4)
        for i in range(SEQ // MM_ROWS):
            rows = pl.ds(i * MM_ROWS, MM_ROWS)
            y_ref[rows, :] = _dot(h_ref[rows, :].astype(BF16), c_ref[...]) + d_ref[...] * u_ref[rows, :]

    return _call(
        body, "ssm_fwd", (SSM_CHUNKS,),
        [pl.BlockSpec((SEQ, 128), lambda c: (0, U_COL + c)),
         pl.BlockSpec((None, 128, 2 * CHUNK_STATES), lambda c: (c, 0, 0)),
         pl.BlockSpec((None, 2 * CHUNK_STATES, 128), lambda c: (c, 0, 0)),
         pl.BlockSpec((None, 1, 2 * CHUNK_STATES), lambda c: (c, 0, 0)),
         pl.BlockSpec((1, 128), lambda c: (0, c))],
        [pl.BlockSpec((SEQ, 128), lambda c: (0, c)), pl.BlockSpec((SEQ, 2 * CHUNK_STATES), lambda c: (0, c))],
        [_sds((SEQ, SSM_WIDTH), F32), _sds((SEQ, SSM_CHUNKS * 2 * CHUNK_STATES), F32)], [],
        [proj, bmat, cmat, a_chunks, d_skip], ride)


def _ssm_bwd(dys, proj, h, bmat, cmat, a_chunks, d_skip, dproj, ride=None):
    def body(dy_ref, u_ref, h_ref, b_ref, c_ref, a_ref, d_ref, dproj_in, du_ref, db_ref, dc_ref, da_ref, dd_ref, g_ref):
        del dproj_in
        dsum = jnp.zeros((1, 128), F32)
        dcm = jnp.zeros((2 * CHUNK_STATES, 128), F32)
        for i in range(SEQ // MM_ROWS):
            rows = pl.ds(i * MM_ROWS, MM_ROWS)
            dy = dy_ref[rows, :]
            g_ref[rows, :] = _dot_nt(dy.astype(BF16), c_ref[...])
            dsum += jnp.sum(dy * u_ref[rows, :], axis=0, keepdims=True)
            dcm += _dot_tn(h_ref[rows, :].astype(BF16), dy.astype(BF16))
        dd_ref[...] = dsum
        dc_ref[...] = dcm
        steps, carry_in, _ = _scan_consts(a_ref, conj=True, reverse=True)
        first_row = lax.broadcasted_iota(jnp.int32, (SCAN_ROWS, CHUNK_STATES), 0) == 0
        n_tiles = SEQ // SCAN_ROWS

        def tile(j, carry):
            k = n_tiles - 1 - j
            rows = pl.ds(pl.multiple_of(k * SCAN_ROWS, SCAN_ROWS), SCAN_ROWS)
            before = pl.ds(pl.multiple_of(jnp.maximum(k - 1, 0) * SCAN_ROWS, SCAN_ROWS), SCAN_ROWS)
            gr, gi = _scan_tile(g_ref[rows, :CHUNK_STATES], g_ref[rows, CHUNK_STATES:], carry[0], carry[1], steps, carry_in, True)
            g_ref[rows, :CHUNK_STATES] = gr
            g_ref[rows, CHUNK_STATES:] = gi
            has_before = jnp.where(k > 0, 1.0, 0.0)
            hr = jnp.where(first_row, pltpu.roll(h_ref[before, :CHUNK_STATES], 1, axis=0) * has_before,
                           pltpu.roll(h_ref[rows, :CHUNK_STATES], 1, axis=0))
            hi = jnp.where(first_row, pltpu.roll(h_ref[before, CHUNK_STATES:], 1, axis=0) * has_before,
                           pltpu.roll(h_ref[rows, CHUNK_STATES:], 1, axis=0))
            return gr, gi, carry[2] + hr * gr + hi * gi, carry[3] + hr * gi - hi * gr

        zero = jnp.zeros((SCAN_ROWS, CHUNK_STATES), F32)
        _, _, sar, sai = lax.fori_loop(0, n_tiles, tile, (zero, zero, zero, zero), unroll=4)
        da_ref[:, :CHUNK_STATES] = jnp.sum(sar, axis=0, keepdims=True)
        da_ref[:, CHUNK_STATES:] = jnp.sum(sai, axis=0, keepdims=True)
        dbm = jnp.zeros((128, 2 * CHUNK_STATES), F32)
        for i in range(SEQ // MM_ROWS):
            rows = pl.ds(i * MM_ROWS, MM_ROWS)
            g = g_ref[rows, :].astype(BF16)
            du_ref[rows, :] = (_dot_nt(g, b_ref[...]) + d_ref[...] * dy_ref[rows, :]).astype(BF16)
            dbm += _dot_tn(u_ref[rows, :].astype(BF16), g)
        db_ref[...] = dbm

    chunk_col = pl.BlockSpec((SEQ, 128), lambda c: (0, c))
    return _call(
        body, "ssm_bwd", (SSM_CHUNKS,),
        [chunk_col,
         pl.BlockSpec((SEQ, 128), lambda c: (0, U_COL + c)),
         pl.BlockSpec((SEQ, 2 * CHUNK_STATES), lambda c: (0, c)),
         pl.BlockSpec((None, 128, 2 * CHUNK_STATES), lambda c: (c, 0, 0)),
         pl.BlockSpec((None, 2 * CHUNK_STATES, 128), lambda c: (c, 0, 0)),
         pl.BlockSpec((None, 1, 2 * CHUNK_STATES), lambda c: (c, 0, 0)),
         pl.BlockSpec((1, 128), lambda c: (0, c)), ANY],
        [pl.BlockSpec((SEQ, 128), lambda c: (0, U_COL + c)),
         pl.BlockSpec((None, 128, 2 * CHUNK_STATES), lambda c: (c, 0, 0)),
         pl.BlockSpec((None, 2 * CHUNK_STATES, 128), lambda c: (c, 0, 0)),
         pl.BlockSpec((None, 1, 2 * CHUNK_STATES), lambda c: (c, 0, 0)),
         pl.BlockSpec((1, 128), lambda c: (0, c))],
        [_sds((SEQ, IN_WIDTH), BF16), _sds((SSM_CHUNKS, 128, 2 * CHUNK_STATES), F32),
         _sds((SSM_CHUNKS, 2 * CHUNK_STATES, 128), F32), _sds((SSM_CHUNKS, 1, 2 * CHUNK_STATES), F32), _sds((1, SSM_WIDTH), F32)],
        [pltpu.VMEM((SEQ, 2 * CHUNK_STATES), F32)], [dys, proj, h, bmat, cmat, a_chunks, d_skip, dproj], ride, aliases={7: 0})


def _ssm_tables(abar_re, abar_im, bbar_re_t, bbar_im_t, c_re, c_im):
    bmat = _block_diag(bbar_re_t, bbar_im_t, 1.0, True).astype(BF16)
    cmat = _block_diag(c_re, c_im, -1.0, False).astype(BF16)
    a_chunks = jnp.concatenate([abar_re.reshape(SSM_CHUNKS, 1, CHUNK_STATES), abar_im.reshape(SSM_CHUNKS, 1, CHUNK_STATES)], axis=2)
    return bmat, cmat, a_chunks


GL_COL = (3 * QKV_WIDTH + SSM_WIDTH) // D_MODEL
GELU_C = math.sqrt(2.0 / math.pi)
GELU_A = 0.044715


def _sds(shape, dtype):
    return jax.ShapeDtypeStruct(shape, dtype)


def _gelu(x):
    t = jnp.tanh(GELU_C * (x + GELU_A * x * x * x))
    return 0.5 * x * (1.0 + t), t


def _gelu_grad(x, t):
    return 0.5 * (1.0 + t) + 0.5 * x * (1.0 - t * t) * GELU_C * (1.0 + 3.0 * GELU_A * x * x)


def _layer_norm(r, g, b):
    mu = jnp.mean(r, axis=-1, keepdims=True)
    xc = r - mu
    rstd = lax.rsqrt(jnp.mean(xc * xc, axis=-1, keepdims=True) + LN_EPS)
    xhat = xc * rstd
    return xhat * g + b, xhat, rstd


def _layer_norm_bwd(dy, xhat, rstd, g):
    dxhat = dy * g
    m1 = jnp.mean(dxhat, axis=-1, keepdims=True)
    m2 = jnp.mean(dxhat * xhat, axis=-1, keepdims=True)
    return rstd * (dxhat - m1 - xhat * m2)


def _proj(x, w_in, ride=None):
    tm, tn = 1024, 1792

    def body(x_ref, w_ref, o_ref):
        o_ref[...] = _dot(x_ref[...].astype(BF16), _side_by_side(w_ref))

    return _call(
        body, "proj", (SEQ // tm, IN_WIDTH // tn),
        [pl.BlockSpec((tm, D_MODEL), lambda i, j: (i, 0)), pl.BlockSpec((2, D_MODEL, tn // 2), lambda i, j: (j, 0, 0))],
        [pl.BlockSpec((tm, tn), lambda i, j: (i, j))], [_sds((SEQ, IN_WIDTH), F32)], [], [x, w_in], ride)


def _row_spec(tm, width, col=0):
    return pl.BlockSpec((tm, width), lambda i, col=col: (i, col))


def _full_spec(shape):
    return pl.BlockSpec(shape, lambda i: (0,) * len(shape))


def _weight_spec(shape):
    return pl.BlockSpec(shape, lambda i: (0,) * len(shape), pipeline_mode=pl.Buffered(1))


def _mixer_out(attn, ys, proj, x, w_ab, w_sb, w_glu, w_out, b_gate, ln_g, ln_b, ride=None):
    tm = 512

    def body(attn_ref, ys_ref, gl0_ref, gl1_ref, x_ref, wab_ref, wsb_ref, wglu_ref, wout_ref, bg_ref, g_ref, b_ref,
             h_ref, xhat_ref, rstd_ref, glu_ref, ya_ref, yssm_ref):
        gy, _ = _gelu(ys_ref[...])
        glu = _dot(gy.astype(BF16), _side_by_side(wglu_ref))
        glu_ref[...] = glu
        y_s = glu[:, :SSM_WIDTH] * jax.nn.sigmoid(glu[:, SSM_WIDTH:])
        y_ssm = _dot(y_s.astype(BF16), _side_by_side(wsb_ref))
        y_attn = _dot(attn_ref[...].astype(BF16), _side_by_side(wab_ref))
        ya_ref[...] = y_attn
        yssm_ref[...] = y_ssm
        g0 = jax.nn.sigmoid(gl0_ref[...] + _side_by_side(bg_ref, 0))
        g1 = jax.nn.sigmoid(gl1_ref[...] + _side_by_side(bg_ref, 1))
        mixed = g0 * y_attn + g1 * y_ssm
        r1 = DN_ALPHA * x_ref[...] + _dot(mixed.astype(BF16), wout_ref[...])
        h, xhat, rstd = _layer_norm(r1, g_ref[...], b_ref[...])
        h_ref[...] = h
        xhat_ref[...] = xhat
        rstd_ref[...] = jnp.broadcast_to(rstd, (tm, 128))

    wide = _sds((SEQ, D_MODEL), F32)
    return _call(
        body, "mixer_out", (SEQ // tm,),
        [_row_spec(tm, ATTN_WIDTH), _row_spec(tm, SSM_WIDTH), _row_spec(tm, D_MODEL, GL_COL), _row_spec(tm, D_MODEL, GL_COL + 1),
         _row_spec(tm, D_MODEL), _weight_spec((N_DEV, ATTN_WIDTH, 128)), _weight_spec((N_DEV, SSM_WIDTH, 128)),
         _weight_spec((N_DEV, SSM_WIDTH, 128)), _weight_spec((D_MODEL, D_MODEL)), _full_spec((N_DEV, 2, 128)),
         _full_spec((1, D_MODEL)), _full_spec((1, D_MODEL))],
        [_row_spec(tm, D_MODEL), _row_spec(tm, D_MODEL), _row_spec(tm, 128), _row_spec(tm, D_MODEL),
         _row_spec(tm, D_MODEL), _row_spec(tm, D_MODEL)],
        [wide, wide, _sds((SEQ, 128), F32), wide, wide, wide], [],
        [attn, ys, proj, proj, x, w_ab, w_sb, w_glu, w_out, b_gate, ln_g, ln_b], ride)


def _ff_up(h, w_gate, w_up):
    tm, tn = 1024, 768

    def body(h_ref, wg_ref, wu_ref, a_ref, b_ref, f_ref):
        hb = h_ref[...].astype(BF16)
        a, b = _dot(hb, _side_by_side(wg_ref)), _dot(hb, _side_by_side(wu_ref))
        a_ref[...] = a.astype(BF16)
        b_ref[...] = b.astype(BF16)
        f_ref[...] = (a * jax.nn.sigmoid(a) * b).astype(BF16)

    tile = pl.BlockSpec((tm, tn), lambda i, j: (i, j))
    wtile = pl.BlockSpec((tn // FF_PAD, D_MODEL, FF_PAD), lambda i, j: (j, 0, 0))
    out = _sds((SEQ, D_FF_PAD), BF16)
    return _pallas_call(
        body, name="ff_up", grid=(SEQ // tm, D_FF_PAD // tn),
        in_specs=[pl.BlockSpec((tm, D_MODEL), lambda i, j: (i, 0)), wtile, wtile],
        out_specs=(tile, tile, tile), out_shape=(out, out, out),
        compiler_params=_cparams(dimension_semantics=("arbitrary", "arbitrary")),
    )(h, w_gate, w_up)


def _ff_down_loss(f, w_down, h, target, ln_g, ln_b):
    tm = 512

    def body(f_ref, w_ref, h_ref, t_ref, g_ref, b_ref, dr_ref, dg_ref, db_ref, loss_ref):
        @pl.when(pl.program_id(0) == 0)
        def _():
            dg_ref[...] = jnp.zeros_like(dg_ref)
            db_ref[...] = jnp.zeros_like(db_ref)
            loss_ref[...] = jnp.zeros_like(loss_ref)

        r2 = DN_ALPHA * h_ref[...] + _dot(f_ref[...], w_ref[...])
        g = g_ref[...]
        out, xhat, rstd = _layer_norm(r2, g, b_ref[...])
        err = out - t_ref[...]
        loss_ref[...] += 0.5 * jnp.sum(jnp.mean(err * err, axis=-1, keepdims=True), axis=0, keepdims=True)
        dout = err * (1.0 / D_MODEL)
        dg_ref[...] += jnp.sum(dout * xhat, axis=0, keepdims=True)
        db_ref[...] += jnp.sum(dout, axis=0, keepdims=True)
        dr_ref[...] = _layer_norm_bwd(dout, xhat, rstd, g)

    vec = _sds((1, D_MODEL), F32)
    return _pallas_call(
        body, name="ff_down_loss", grid=(SEQ // tm,),
        in_specs=[_row_spec(tm, D_FF_PAD), _weight_spec((D_FF_PAD, D_MODEL)), _row_spec(tm, D_MODEL), _row_spec(tm, D_MODEL),
                  _full_spec((1, D_MODEL)), _full_spec((1, D_MODEL))],
        out_specs=(_row_spec(tm, D_MODEL), _full_spec((1, D_MODEL)), _full_spec((1, D_MODEL)), _full_spec((1, 128))),
        out_shape=(_sds((SEQ, D_MODEL), F32), vec, vec, _sds((1, 128), F32)),
        compiler_params=_cparams(dimension_semantics=("arbitrary",)),
    )(f, w_down, h, target, ln_g, ln_b)


def _ff_down_bwd(dr2, w_down, a, b):
    tm, tn = 1024, 768

    def body(dr_ref, w_ref, a_ref, b_ref, da_ref, db_ref):
        df = _dot_nt(dr_ref[...].astype(BF16), w_ref[...])
        av, bv = a_ref[...].astype(F32), b_ref[...].astype(F32)
        sg = jax.nn.sigmoid(av)
        da_ref[...] = (df * bv * sg * (1.0 + av * (1.0 - sg))).astype(BF16)
        db_ref[...] = (df * av * sg).astype(BF16)

    tile = pl.BlockSpec((tm, tn), lambda i, j: (i, j))
    out = _sds((SEQ, D_FF_PAD), BF16)
    return _pallas_call(
        body, name="ff_down_bwd", grid=(SEQ // tm, D_FF_PAD // tn),
        in_specs=[pl.BlockSpec((tm, D_MODEL), lambda i, j: (i, 0)), pl.BlockSpec((tn, D_MODEL), lambda i, j: (j, 0)), tile, tile],
        out_specs=(tile, tile), out_shape=(out, out),
        compiler_params=_cparams(dimension_semantics=("arbitrary", "arbitrary")),
    )(dr2, w_down, a, b)


def _ff_up_bwd(da, db, w_gate, w_up, dr2, xhat1, rstd1, ln_g, ride=None):
    tm, tk = 1024, 768
    nk = D_FF_PAD // tk

    def body(da_ref, db_ref, wg_ref, wu_ref, dr2_ref, xhat_ref, rstd_ref, g_ref, dr1_ref, dg_ref, dbias_ref, acc):
        i, k = pl.program_id(0), pl.program_id(1)

        @pl.when(jnp.logical_and(i == 0, k == 0))
        def _():
            dg_ref[...] = jnp.zeros_like(dg_ref)
            dbias_ref[...] = jnp.zeros_like(dbias_ref)

        part = _dot_nt(da_ref[...], _side_by_side(wg_ref)) + _dot_nt(db_ref[...], _side_by_side(wu_ref))

        @pl.when(k == 0)
        def _():
            acc[...] = part

        @pl.when(k > 0)
        def _():
            acc[...] += part

        @pl.when(k == nk - 1)
        def _():
            dh = DN_ALPHA * dr2_ref[...] + acc[...]
            xhat = xhat_ref[...]
            dg_ref[...] += jnp.sum(dh * xhat, axis=0, keepdims=True)
            dbias_ref[...] += jnp.sum(dh, axis=0, keepdims=True)
            rstd = jnp.max(rstd_ref[...], axis=1, keepdims=True)
            dr1_ref[...] = _layer_norm_bwd(dh, xhat, rstd, g_ref[...])

    hid = pl.BlockSpec((tm, tk), lambda i, k: (i, k))
    wtile = pl.BlockSpec((tk // FF_PAD, D_MODEL, FF_PAD), lambda i, k: (k, 0, 0))
    row = pl.BlockSpec((tm, D_MODEL), lambda i, k: (i, 0))
    vec = pl.BlockSpec((1, D_MODEL), lambda i, k: (0, 0))
    return _call(
        body, "ff_up_bwd", (SEQ // tm, nk),
        [hid, hid, wtile, wtile, row, row, pl.BlockSpec((tm, 128), lambda i, k: (i, 0)), vec],
        [row, vec, vec], [_sds((SEQ, D_MODEL), F32), _sds((1, D_MODEL), F32), _sds((1, D_MODEL), F32)],
        [pltpu.VMEM((tm, D_MODEL), F32)], [da, db, w_gate, w_up, dr2, xhat1, rstd1, ln_g], ride)


def _mixer_bwd(dr1, proj, y_attn, y_ssm, glu, ys, w_ab, w_sb, w_glu, w_out, b_gate):
    tm = 256

    def body(dr1_ref, gl0_ref, gl1_ref, ya_ref, yssm_ref, glu_ref, ys_ref, wab_ref, wsb_ref, wglu_ref, wout_ref, bg_ref,
             dya_ref, dyssm_ref, dgl_ref, dattn_ref, dglu_ref, dys_ref, mixed_ref, ysb_ref, gy_ref, dbg_ref):
        @pl.when(pl.program_id(0) == 0)
        def _():
            dbg_ref[...] = jnp.zeros_like(dbg_ref)

        dmixed = _dot_nt(dr1_ref[...].astype(BF16), wout_ref[...])
        g0 = jax.nn.sigmoid(gl0_ref[...] + _side_by_side(bg_ref, 0))
        g1 = jax.nn.sigmoid(gl1_ref[...] + _side_by_side(bg_ref, 1))
        y_attn, y_ssm = ya_ref[...], yssm_ref[...]
        mixed_ref[...] = (g0 * y_attn + g1 * y_ssm).astype(BF16)
        dya = (dmixed * g0).astype(BF16)
        dyssm = (dmixed * g1).astype(BF16)
        dya_ref[...] = dya
        dyssm_ref[...] = dyssm
        dgl0 = dmixed * y_attn * g0 * (1.0 - g0)
        dgl1 = dmixed * y_ssm * g1 * (1.0 - g1)
        dgl_ref[:, :GL_COL * D_MODEL] = jnp.zeros((tm, GL_COL * D_MODEL), BF16)
        dgl_ref[:, GL_COL * D_MODEL:(GL_COL + 1) * D_MODEL] = dgl0.astype(BF16)
        dgl_ref[:, (GL_COL + 1) * D_MODEL:] = dgl1.astype(BF16)
        dbg_ref[:, :D_MODEL] += jnp.sum(dgl0, axis=0, keepdims=True)
        dbg_ref[:, D_MODEL:] += jnp.sum(dgl1, axis=0, keepdims=True)
        dattn_ref[...] = _dot_nt(dya, _side_by_side(wab_ref))
        dy_s = _dot_nt(dyssm, _side_by_side(wsb_ref))
        glu = glu_ref[...]
        glu1, sg = glu[:, :SSM_WIDTH], jax.nn.sigmoid(glu[:, SSM_WIDTH:])
        ysb_ref[...] = (glu1 * sg).astype(BF16)
        dglu1 = (dy_s * sg).astype(BF16)
        dglu2 = (dy_s * glu1 * sg * (1.0 - sg)).astype(BF16)
        dglu_ref[:, :SSM_WIDTH] = dglu1
        dglu_ref[:, SSM_WIDTH:] = dglu2
        dgy = _dot_nt(jnp.concatenate([dglu1, dglu2], axis=1), _side_by_side(wglu_ref))
        ys = ys_ref[...]
        gy, t = _gelu(ys)
        gy_ref[...] = gy.astype(BF16)
        dys_ref[...] = dgy * _gelu_grad(ys, t)

    wide_b, half_b = _sds((SEQ, D_MODEL), BF16), _sds((SEQ, SSM_WIDTH), BF16)
    half_f = _sds((SEQ, SSM_WIDTH), F32)
    return _pallas_call(
        body, name="mixer_bwd", grid=(SEQ // tm,),
        in_specs=[_row_spec(tm, D_MODEL), _row_spec(tm, D_MODEL, GL_COL), _row_spec(tm, D_MODEL, GL_COL + 1), _row_spec(tm, D_MODEL),
                  _row_spec(tm, D_MODEL), _row_spec(tm, D_MODEL), _row_spec(tm, SSM_WIDTH), _full_spec((N_DEV, ATTN_WIDTH, 128)),
                  _full_spec((N_DEV, SSM_WIDTH, 128)), _full_spec((N_DEV, SSM_WIDTH, 128)), _full_spec((D_MODEL, D_MODEL)),
                  _full_spec((N_DEV, 2, 128))],
        out_specs=(_row_spec(tm, D_MODEL), _row_spec(tm, D_MODEL), _row_spec(tm, IN_WIDTH), _row_spec(tm, ATTN_WIDTH),
                   _row_spec(tm, D_MODEL), _row_spec(tm, SSM_WIDTH), _row_spec(tm, D_MODEL), _row_spec(tm, SSM_WIDTH),
                   _row_spec(tm, SSM_WIDTH), _full_spec((1, 2 * D_MODEL))),
        out_shape=(wide_b, wide_b, _sds((SEQ, IN_WIDTH), BF16), half_f, wide_b, half_f, wide_b, half_b, half_b,
                   _sds((1, 2 * D_MODEL), F32)),
        compiler_params=_cparams(dimension_semantics=("arbitrary",)),
    )(dr1, proj, proj, y_attn, y_ssm, glu, ys, w_ab, w_sb, w_glu, w_out, b_gate)


def _grad_x(dproj, w_in, dr1, ride=None):
    tm, tk = 1024, 1792
    nk = IN_WIDTH // tk

    def body(dp_ref, w_ref, dr1_ref, o_ref, acc):
        k = pl.program_id(1)
        part = _dot_nt(dp_ref[...], _side_by_side(w_ref))

        @pl.when(k == 0)
        def _():
            acc[...] = part

        @pl.when(k > 0)
        def _():
            acc[...] += part

        @pl.when(k == nk - 1)
        def _():
            o_ref[...] = DN_ALPHA * dr1_ref[...] + acc[...]

    row = pl.BlockSpec((tm, D_MODEL), lambda i, k: (i, 0))
    return _call(
        body, "grad_x", (SEQ // tm, nk),
        [pl.BlockSpec((tm, tk), lambda i, k: (i, k)), pl.BlockSpec((2, D_MODEL, tk // 2), lambda i, k: (k, 0, 0)), row],
        [row], [_sds((SEQ, D_MODEL), F32)], [pltpu.VMEM((tm, D_MODEL), F32)], [dproj, w_in, dr1], ride)


def _weight_grad(a, b, name, shard_cols=None, ride=None):
    k, n = a.shape[1], b.shape[1]
    tk = min(k, 512) if shard_cols else k // N_DEV
    tn = n // 4 if shard_cols else min(n, 1024)

    def body(a_ref, b_ref, o_ref):
        grad = _dot_tn(a_ref[...].astype(BF16), b_ref[...].astype(BF16))
        if shard_cols:
            o_ref[0] = grad[:, :shard_cols].astype(BF16)
            o_ref[1] = grad[:, shard_cols:].astype(BF16)
        else:
            o_ref[...] = grad.astype(BF16)

    if shard_cols:
        out_spec = pl.BlockSpec((2, None, tk, shard_cols), lambda kk, j: (0, j, kk, 0))
        out_shape = _sds((2, 4, k, shard_cols), BF16)
    else:
        out_spec = pl.BlockSpec((None, None, tk, tn), lambda kk, j: (kk % 2, kk // 2, 0, j))
        out_shape = _sds((2, 4, tk, n), BF16)
    out = _call(body, name, (k // tk, n // tn),
                [pl.BlockSpec((SEQ, tk), lambda kk, j: (0, kk)), pl.BlockSpec((SEQ, tn), lambda kk, j: (0, j))],
                [out_spec], [out_shape], [], [a, b], ride)
    return out[0] if ride is None else out


MESH = pl.DeviceIdType.MESH
ANY = pl.BlockSpec(memory_space=pl.ANY)


def _place():
    return lax.axis_index("x"), lax.axis_index("y"), lax.axis_index("c")


def _other_chips(x, y):
    return [(1 - x, y), (x, 1 - y), (1 - x, 1 - y)]


class _Ride:
    def __init__(self, operands, results, aliases, sems, start, wait):
        self.operands, self.results, self.aliases, self.sems = list(operands), list(results), dict(aliases), list(sems)
        self.start, self.wait = start, wait

    def __add__(self, other):
        n_in, n_out, n_sem = len(self.operands), len(self.results), len(self.sems)

        def both(which):
            def run(ins, outs, sems):
                getattr(self, which)(ins[:n_in], outs[:n_out], sems[:n_sem])
                getattr(other, which)(ins[n_in:], outs[n_out:], sems[n_sem:])
            return run

        aliases = {**self.aliases, **{n_in + i: n_out + j for i, j in other.aliases.items()}}
        return _Ride(self.operands + other.operands, self.results + other.results, aliases, self.sems + other.sems,
                     both("start"), both("wait"))


def _call(body, name, grid, in_specs, out_specs, out_shape, scratch_shapes, operands, ride=None, aliases=None):
    in_specs, out_specs, out_shape = list(in_specs), list(out_specs), list(out_shape)
    scratch_shapes, operands, aliases = list(scratch_shapes), list(operands), dict(aliases or {})
    kernel_body = body
    if ride is not None:
        n_in, n_out, n_scr, r_in, r_out = len(in_specs), len(out_specs), len(scratch_shapes), len(ride.operands), len(ride.results)

        def kernel_body(*refs):
            out0, scr0 = n_in + r_in, n_in + r_in + n_out + r_out
            ride_refs = (refs[n_in:out0], refs[out0 + n_out:scr0], refs[scr0 + n_scr:])
            ids = [pl.program_id(i) for i in range(len(grid))]
            first = functools.reduce(jnp.logical_and, [i == 0 for i in ids])
            last = functools.reduce(jnp.logical_and, [i == g - 1 for i, g in zip(ids, grid)])

            @pl.when(first)
            def _():
                ride.start(*ride_refs)

            body(*refs[:n_in], *refs[out0:out0 + n_out], *refs[scr0:scr0 + n_scr])

            @pl.when(last)
            def _():
                ride.wait(*ride_refs)

        aliases.update({n_in + i: n_out + j for i, j in ride.aliases.items()})
        in_specs += [ANY] * r_in
        out_specs += [ANY] * r_out
        out_shape += ride.results
        scratch_shapes += ride.sems
        operands += ride.operands
    return _pallas_call(
        kernel_body, name=name, grid=grid, in_specs=in_specs, out_specs=out_specs, out_shape=out_shape,
        scratch_shapes=scratch_shapes, input_output_aliases=aliases,
        compiler_params=_cparams(dimension_semantics=("arbitrary",) * len(grid)),
    )(*operands)


def _gather_first_level(shards):
    n = len(shards)

    def copies(ins, outs, sems, landed):
        send_sems, recv_sems, local_sems = sems
        x, y, c = _place()
        peers = [(x, y, 1 - c)] + [(px, py, c) for px, py in _other_chips(x, y)]

        def row(peer):
            return 4 * x + 2 * y + c if not landed else 4 * peer[0] + 2 * peer[1] + peer[2]

        local = [pltpu.make_async_copy(ins[a], outs[a].at[4 * x + 2 * y + c], local_sems.at[a]) for a in range(n)]
        remote = [pltpu.make_async_remote_copy(
            src_ref=ins[a], dst_ref=outs[a].at[row(peer)], send_sem=send_sems.at[a, k], recv_sem=recv_sems.at[a, k],
            device_id=peer, device_id_type=MESH) for a in range(n) for k, peer in enumerate(peers)]
        return local, remote

    def start(ins, outs, sems):
        local, remote = copies(ins, outs, sems, False)
        for cp in local + remote:
            cp.start()

    def wait(ins, outs, sems):
        local, sent = copies(ins, outs, sems, False)
        for cp in copies(ins, outs, sems, True)[1]:
            cp.wait_recv()
        for cp in sent:
            cp.wait_send()
        for cp in local:
            cp.wait()

    return _Ride(shards, [_sds((N_DEV,) + s.shape, s.dtype) for s in shards], {},
                 [pltpu.SemaphoreType.DMA((n, 4)), pltpu.SemaphoreType.DMA((n, 4)), pltpu.SemaphoreType.DMA((n,))], start, wait)


def _gather_second_level(buffers):
    n = len(buffers)

    def copies(outs, sems, core):
        send_sems, recv_sems = sems
        x, y, c = _place()
        return [pltpu.make_async_remote_copy(
            src_ref=outs[a].at[4 * px + 2 * py + core], dst_ref=outs[a].at[4 * px + 2 * py + core], send_sem=send_sems.at[a, j],
            recv_sem=recv_sems.at[a, j], device_id=(x, y, 1 - c), device_id_type=MESH)
            for a in range(n) for j, (px, py) in enumerate(_other_chips(x, y))]

    def start(ins, outs, sems):
        for cp in copies(outs, sems, lax.axis_index("c")):
            cp.start()

    def wait(ins, outs, sems):
        for cp in copies(outs, sems, 1 - lax.axis_index("c")):
            cp.wait_recv()
        for cp in copies(outs, sems, lax.axis_index("c")):
            cp.wait_send()

    return _Ride(buffers, [_sds(b.shape, b.dtype) for b in buffers], {i: i for i in range(n)},
                 [pltpu.SemaphoreType.DMA((n, 3)), pltpu.SemaphoreType.DMA((n, 3))], start, wait)


def _sibling_swap_ride(grads):
    n = len(grads)

    def copies(ins, outs, sems):
        x, y, c = _place()
        return [pltpu.make_async_remote_copy(
            src_ref=ins[a].at[1 - c], dst_ref=outs[a], send_sem=sems[0].at[a], recv_sem=sems[1].at[a],
            device_id=(x, y, 1 - c), device_id_type=MESH) for a in range(n)]

    def start(ins, outs, sems):
        for cp in copies(ins, outs, sems):
            cp.start()

    def wait(ins, outs, sems):
        for cp in copies(ins, outs, sems):
            cp.wait()

    return _Ride(grads, [_sds(g.shape[1:], g.dtype) for g in grads], {},
                 [pltpu.SemaphoreType.DMA((n,)), pltpu.SemaphoreType.DMA((n,))], start, wait)


def _chip_swap_ride(sums):
    n = len(sums)

    def copies(ins, outs, sems, landed):
        send_sems, recv_sems, local_sems = sems
        x, y, c = _place()
        mine = 2 * x + y
        local = [pltpu.make_async_copy(ins[a].at[mine], outs[a].at[mine], local_sems.at[a]) for a in range(n)]
        remote = [pltpu.make_async_remote_copy(
            src_ref=ins[a].at[2 * px + py], dst_ref=outs[a].at[2 * px + py if landed else mine], send_sem=send_sems.at[a, j],
            recv_sem=recv_sems.at[a, j], device_id=(px, py, c), device_id_type=MESH)
            for a in range(n) for j, (px, py) in enumerate(_other_chips(x, y))]
        return local, remote

    def start(ins, outs, sems):
        local, remote = copies(ins, outs, sems, False)
        for cp in local + remote:
            cp.start()

    def wait(ins, outs, sems):
        local, sent = copies(ins, outs, sems, False)
        for cp in copies(ins, outs, sems, True)[1]:
            cp.wait_recv()
        for cp in sent:
            cp.wait_send()
        for cp in local:
            cp.wait()

    return _Ride(sums, [_sds(s.shape, s.dtype) for s in sums], {},
                 [pltpu.SemaphoreType.DMA((n, 3)), pltpu.SemaphoreType.DMA((n, 3)), pltpu.SemaphoreType.DMA((n,))], start, wait)


def _send_buffers(shards, name):
    n = len(shards)

    def body(*refs):
        for (w, transposed, rows, cols), w_ref, o_ref in zip(shards, refs[:n], refs[n:]):
            if transposed:
                c, r = w.shape
                padded = jnp.concatenate([w_ref[...], jnp.zeros((cols - c, r), F32)], axis=0) if cols > c else w_ref[...]
                o_ref[...] = padded.T.astype(BF16)
            else:
                r, c = w.shape
                if (r, c) != (rows, cols):
                    o_ref[...] = jnp.zeros((rows, cols), BF16)
                o_ref[:r, :c] = w_ref[...].astype(BF16)

    return _pallas_call(body, name=name, out_shape=[_sds((rows, cols), BF16) for _, _, rows, cols in shards])(
        *[w for w, _, _, _ in shards])


def _all_gather(shards, name):
    n = len(shards)

    def body(*refs):
        ins, outs = refs[:n], refs[n:2 * n]
        send_sems, recv_sems, local_sems = refs[2 * n:]
        x, y, c = _place()
        me, sibling = (x, y, c), (x, y, 1 - c)
        chips = _other_chips(x, y)

        def slot(a, px, py, pc):
            return outs[a].at[4 * px + 2 * py + pc]

        def copy(a, k, block, to, src=None):
            return pltpu.make_async_remote_copy(
                src_ref=slot(a, *block) if src is None else src, dst_ref=slot(a, *block),
                send_sem=send_sems.at[a, k], recv_sem=recv_sems.at[a, k], device_id=to, device_id_type=MESH)

        mine = [pltpu.make_async_copy(ins[a], slot(a, *me), local_sems.at[a]) for a in range(n)]
        for cp in mine:
            cp.start()
        first = []
        for a in range(n):
            first.append(copy(a, 0, me, sibling, src=ins[a]))
            first += [copy(a, 1 + j, me, (*chip, c), src=ins[a]) for j, chip in enumerate(chips)]
        for cp in first:
            cp.start()
        passed = []
        for j, chip in enumerate(chips):
            for a in range(n):
                copy(a, 1 + j, (*chip, c), me).wait_recv()
                onward = copy(a, 4 + j, (*chip, c), sibling)
                onward.start()
                passed.append(onward)
        for a in range(n):
            copy(a, 0, sibling, me).wait_recv()
            for j, chip in enumerate(chips):
                copy(a, 4 + j, (*chip, 1 - c), me).wait_recv()
        for cp in first + passed:
            cp.wait_send()
        for cp in mine:
            cp.wait()

    return _pallas_call(
        body, name=name, in_specs=[ANY] * n, out_specs=[ANY] * n,
        out_shape=[_sds((N_DEV,) + s.shape, s.dtype) for s in shards],
        scratch_shapes=[pltpu.SemaphoreType.DMA((n, 7)), pltpu.SemaphoreType.DMA((n, 7)), pltpu.SemaphoreType.DMA((n,))],
    )(*shards)


def _swap_with_sibling(grads, name):
    n = len(grads)

    def body(*refs):
        ins, outs = refs[:n], refs[n:2 * n]
        send_sems, recv_sems = refs[2 * n:]
        x, y, c = _place()
        copies = [pltpu.make_async_remote_copy(
            src_ref=ins[a].at[1 - c], dst_ref=outs[a], send_sem=send_sems.at[a], recv_sem=recv_sems.at[a],
            device_id=(x, y, 1 - c), device_id_type=MESH) for a in range(n)]
        for cp in copies:
            cp.start()
        for cp in copies:
            cp.wait()

    return _pallas_call(
        body, name=name, in_specs=[ANY] * n, out_specs=[ANY] * n,
        out_shape=[_sds(g.shape[1:], g.dtype) for g in grads],
        scratch_shapes=[pltpu.SemaphoreType.DMA((n,)), pltpu.SemaphoreType.DMA((n,))],
    )(*grads)


def _pair_sums(gs, rs, core, name):
    n_arrays = len(gs)

    def body(core_ref, *refs):
        for g_ref, r_ref, o_ref in zip(refs[:n_arrays], refs[n_arrays:2 * n_arrays], refs[2 * n_arrays:]):
            o_ref[...] = (g_ref[...].astype(F32) + r_ref[...].astype(F32)).astype(o_ref.dtype)

    def own(g):
        return pl.BlockSpec((None, None) + g.shape[2:], lambda p, core_ref: (core_ref[0], p, 0, 0))

    def chip(g):
        return pl.BlockSpec((None,) + g.shape[2:], lambda p, core_ref: (p, 0, 0))

    return _pallas_call(
        body, name=name,
        grid_spec=pltpu.PrefetchScalarGridSpec(
            num_scalar_prefetch=1, grid=(4,), in_specs=[own(g) for g in gs] + [chip(g) for g in gs],
            out_specs=[chip(g) for g in gs]),
        out_shape=[_sds(g.shape[1:], g.dtype) for g in gs], compiler_params=_cparams(dimension_semantics=("arbitrary",)),
    )(core, *gs, *rs)


def _adamw_math(w, g, m, v):
    m = ADAM_B1 * m + (1.0 - ADAM_B1) * g
    v = ADAM_B2 * v + (1.0 - ADAM_B2) * (g * g)
    m_hat = m / (1.0 - ADAM_B1 ** ADAM_STEP)
    v_hat = v / (1.0 - ADAM_B2 ** ADAM_STEP)
    return -ADAM_LR * (m_hat / (jnp.sqrt(v_hat) + ADAM_EPS) + ADAM_WD * w), m, v


def _adamw(w, m, v, parts, name):
    r, c = w.shape
    tr = r if r <= 512 else 256
    n_parts, pr, pc = parts.shape
    assert r % tr == 0 and (tr == r or pr == r)

    def body(w_ref, m_ref, v_ref, p_ref, g_out, d_out, m_out, v_out):
        g = p_ref[0, :tr, :c].astype(F32)
        for p in range(1, n_parts):
            g = g + p_ref[p, :tr, :c].astype(F32)
        g_out[...] = g
        d_out[...], m_out[...], v_out[...] = _adamw_math(w_ref[...], g, m_ref[...], v_ref[...])

    tile = pl.BlockSpec((tr, c), lambda i: (i, 0))
    part_tile = pl.BlockSpec((n_parts, pr if tr == r else tr, pc), lambda i: (0, i, 0))
    out = _sds((r, c), F32)
    return _pallas_call(
        body, name=name, grid=(r // tr,), in_specs=[tile, tile, tile, part_tile], out_specs=(tile,) * 4,
        out_shape=(out,) * 4, compiler_params=_cparams(dimension_semantics=("arbitrary",)),
    )(w, m, v, parts)


def _adamw_transposed(w_t, m_t, v_t, parts, name):
    c, r = w_t.shape
    tr = 256
    n_parts, _, pc = parts.shape

    def body(w_ref, m_ref, v_ref, p_ref, g_out, d_out, m_out, v_out):
        g = p_ref[0].astype(F32)
        for p in range(1, n_parts):
            g = g + p_ref[p].astype(F32)
        g = g.T[:c]
        g_out[...] = g
        d_out[...], m_out[...], v_out[...] = _adamw_math(w_ref[...], g, m_ref[...], v_ref[...])

    tile = pl.BlockSpec((c, tr), lambda i: (0, i))
    out = _sds((c, r), F32)
    return _pallas_call(
        body, name=name, grid=(r // tr,), in_specs=[tile, tile, tile, pl.BlockSpec((n_parts, tr, pc), lambda i: (0, i, 0))],
        out_specs=(tile,) * 4, out_shape=(out,) * 4, compiler_params=_cparams(dimension_semantics=("arbitrary",)),
    )(w_t, m_t, v_t, parts)


SMALL = ("ssm_a_re", "ssm_a_im", "ssm_log_dt", "ssm_b_re", "ssm_b_im", "ssm_c_re", "ssm_c_im", "ssm_d",
         "ln1_g", "ln1_b", "ln2_g", "ln2_b")


def _pack_rows(arrays):
    rows = []
    for a in arrays:
        flat = a.reshape(-1)
        rows.append(jnp.pad(flat, (0, -flat.shape[0] % 128)).reshape(-1, 128))
    packed = jnp.concatenate(rows, axis=0)
    return jnp.pad(packed, ((0, -packed.shape[0] % 8), (0, 0)))


def _unpack_rows(packed, shapes):
    out, row = [], 0
    for shape in shapes:
        size = math.prod(shape)
        n_rows = -(-size // 128)
        out.append(packed[row:row + n_rows].reshape(-1)[:size].reshape(shape))
        row += n_rows
    return out


def _sum_devices(parts):
    def body(p_ref, o_ref):
        total = p_ref[0]
        for dev in range(1, N_DEV):
            total = total + p_ref[dev]
        o_ref[...] = total

    return _pallas_call(body, name="sum_devices", out_shape=_sds(parts.shape[1:], F32))(parts)


def _adamw_replicated(ws, ms, vs, gs):
    n = len(ws)

    def body(*refs):
        w_refs, m_refs, v_refs, g_refs, d_out, m_out, v_out = (refs[i * n:(i + 1) * n] for i in range(7))
        for i in range(n):
            d_out[i][...], m_out[i][...], v_out[i][...] = _adamw_math(w_refs[i][...], g_refs[i][...], m_refs[i][...], v_refs[i][...])

    out = _pallas_call(body, name="adamw_replicated", out_shape=[_sds(w.shape, F32) for w in ws] * 3,
                       compiler_params=_cparams())(*ws, *ms, *vs, *gs)
    return out[:n], out[n:2 * n], out[2 * n:]


def kernel(x, w_in, b_gate, w_attn_br, w_ssm_br, w_out, ssm_a_re, ssm_a_im, ssm_log_dt, ssm_b_re, ssm_b_im, ssm_c_re, ssm_c_im, ssm_d, w_glu, ln1_g, ln1_b, w_ff_gate, w_ff_up, w_ff_down, ln2_g, ln2_b, loss_target, m_w_in, m_b_gate, m_w_attn_br, m_w_ssm_br, m_w_out, m_ssm_a_re, m_ssm_a_im, m_ssm_log_dt, m_ssm_b_re, m_ssm_b_im, m_ssm_c_re, m_ssm_c_im, m_ssm_d, m_w_glu, m_ln1_g, m_ln1_b, m_w_ff_gate, m_w_ff_up, m_w_ff_down, m_ln2_g, m_ln2_b, v_w_in, v_b_gate, v_w_attn_br, v_w_ssm_br, v_w_out, v_ssm_a_re, v_ssm_a_im, v_ssm_log_dt, v_ssm_b_re, v_ssm_b_im, v_ssm_c_re, v_ssm_c_im, v_ssm_d, v_w_glu, v_ln1_g, v_ln1_b, v_w_ff_gate, v_w_ff_up, v_w_ff_down, v_ln2_g, v_ln2_b):
    given = dict(locals())
    x2, target = x[0], loss_target[0]
    core = lax.axis_index("c").astype(jnp.int32).reshape(1)

    sharded = ("w_in", "w_attn_br", "w_ssm_br", "w_glu", "w_ff_gate", "w_ff_up", "b_gate", "w_out", "w_ff_down")
    send_shape = dict(w_in=(D_MODEL, 896), w_attn_br=(ATTN_WIDTH, 128), w_ssm_br=(SSM_WIDTH, 128), w_glu=(SSM_WIDTH, 128),
                      w_out=(128, D_MODEL), w_ff_gate=(D_MODEL, FF_PAD), w_ff_up=(D_MODEL, FF_PAD), w_ff_down=(FF_PAD, D_MODEL))
    local = {k: given[k][0] for k in sharded}
    narrow = ("w_ff_gate", "w_ff_up")
    def to_send(k):
        return (local[k].T, True, *send_shape[k]) if k in narrow else (local[k], False, *send_shape[k])

    later = [k for k in sharded if k not in ("w_in", "b_gate")]
    sends = dict(zip(["w_in"] + later, _send_buffers([to_send("w_in")], "send_w_in")
                     + _send_buffers([to_send(k) for k in later], "send_weights")))
    sends["b_gate"] = local["b_gate"]
    mixer_weights = ("w_attn_br", "w_ssm_br", "w_glu", "b_gate", "w_out")
    ff_weights = ("w_ff_gate", "w_ff_up", "w_ff_down")
    wt = {}
    wt["w_in"], = _all_gather([sends["w_in"]], "gather_w_in")

    a_re, a_im, log_dt = ssm_a_re[0], ssm_a_im[0], ssm_log_dt[0].reshape(SSM_GROUPS, 1)
    b_re_t, b_im_t = ssm_b_re[0].transpose(0, 2, 1), ssm_b_im[0].transpose(0, 2, 1)
    abar_re, abar_im, e_re, e_im, bbar_re_t, bbar_im_t = _ssm_prep(a_re, a_im, log_dt, b_re_t, b_im_t)
    bmat, cmat, a_chunks = _ssm_tables(abar_re, abar_im, bbar_re_t, bbar_im_t, ssm_c_re[0], ssm_c_im[0])
    cos_t, sin_t = _rope_tables()

    n_mixer, ff_in = len(mixer_weights), ("w_ff_gate", "w_ff_up")
    proj, *partly = _proj(x2, wt["w_in"], _gather_first_level([sends[k] for k in mixer_weights]))
    attn, lse, *landed = _attn_fwd(proj, cos_t, sin_t,
                                   _gather_second_level(partly) + _gather_first_level([sends[k] for k in ff_in]))
    wt.update(zip(mixer_weights, landed[:n_mixer]))
    ys, states, *landed = _ssm_fwd(proj, bmat, cmat, a_chunks, ssm_d,
                                   _gather_second_level(landed[n_mixer:]) + _gather_first_level([sends["w_ff_down"]]))
    wt.update(zip(ff_in, landed[:2]))
    wt["w_out"] = wt["w_out"].reshape(D_MODEL, D_MODEL)
    b_gate_full = wt["b_gate"]
    h, xhat1, rstd1, glu, y_attn, y_ssm, w_ff_down = _mixer_out(
        attn, ys, proj, x2, wt["w_attn_br"], wt["w_ssm_br"], wt["w_glu"], wt["w_out"], b_gate_full, ln1_g, ln1_b,
        _gather_second_level(landed[2:]))
    wt["w_ff_down"] = w_ff_down.reshape(D_FF_PAD, D_MODEL)
    ff_a, ff_b, ff_f = _ff_up(h, wt["w_ff_gate"], wt["w_ff_up"])
    dr2, d_ln2_g, d_ln2_b, loss_lanes = _ff_down_loss(ff_f, wt["w_ff_down"], h, target, ln2_g, ln2_b)

    def pair_sums(names, contrib, from_sibling):
        return _pair_sums([contrib[k] for k in names], from_sibling, core, "pair_sums_" + names[0])

    d_a, d_b = _ff_down_bwd(dr2, wt["w_ff_down"], ff_a, ff_b)
    contrib = dict(w_ff_gate=_weight_grad(h, d_a, "wgrad_w_ff_gate", FF_PAD),
                   w_ff_up=_weight_grad(h, d_b, "wgrad_w_ff_up", FF_PAD),
                   w_ff_down=_weight_grad(ff_f, dr2, "wgrad_w_ff_down"))
    dr1, d_ln1_g, d_ln1_b, *from_sibling = _ff_up_bwd(
        d_a, d_b, wt["w_ff_gate"], wt["w_ff_up"], dr2, xhat1, rstd1, ln1_g, _sibling_swap_ride([contrib[k] for k in ff_weights]))
    ff_sums = pair_sums(ff_weights, contrib, from_sibling)

    d_ya, d_yssm, d_proj, d_attn, d_glu, d_ys, mixed, y_s, gy, d_bg = _mixer_bwd(
        dr1, proj, y_attn, y_ssm, glu, ys, wt["w_attn_br"], wt["w_ssm_br"], wt["w_glu"], wt["w_out"], b_gate_full)
    contrib.update(w_attn_br=_weight_grad(attn, d_ya, "wgrad_w_attn_br", 128),
                   w_ssm_br=_weight_grad(y_s, d_yssm, "wgrad_w_ssm_br", 128),
                   w_glu=_weight_grad(gy, d_glu, "wgrad_w_glu", 128),
                   w_out=_weight_grad(mixed, dr1, "wgrad_w_out"),
                   b_gate=d_bg.reshape(2, 4, 2, 128).transpose(2, 1, 0, 3))
    d_proj, *landed = _attn_bwd(proj, cos_t, sin_t, attn, lse, d_attn, d_proj,
                                _chip_swap_ride(ff_sums) + _sibling_swap_ride([contrib[k] for k in mixer_weights]))
    parts = dict(zip(ff_weights, landed[:len(ff_weights)]))
    mixer_sums = pair_sums(mixer_weights, contrib, landed[len(ff_weights):])
    d_proj, d_bmat, d_cmat, d_abar, d_skip, *landed = _ssm_bwd(d_ys, proj, states, bmat, cmat, a_chunks, ssm_d, d_proj,
                                                               _chip_swap_ride(mixer_sums))
    parts.update(zip(mixer_weights, landed))

    gbb_re_t, gbb_im_t = _block_diag_parts(d_bmat, True)
    gc_re, gc_im = _block_diag_parts(d_cmat, False)
    ga_re = d_abar[:, 0, :CHUNK_STATES].reshape(SSM_GROUPS, SSM_STATE)
    ga_im = d_abar[:, 0, CHUNK_STATES:].reshape(SSM_GROUPS, SSM_STATE)
    g_a_re, g_a_im, g_log_dt, g_b_re_t, g_b_im_t = _ssm_param_bwd(
        a_re, a_im, log_dt, b_re_t, b_im_t, abar_re, abar_im, e_re, e_im, ga_re, ga_im, gbb_re_t, gbb_im_t)
    mine = [g_a_re, g_a_im, g_log_dt, g_b_re_t, g_b_im_t, gc_re, -gc_im,
            d_skip, d_ln1_g, d_ln1_b, d_ln2_g, d_ln2_b]
    small_packed = _pack_rows(mine + [loss_lanes])

    contrib["w_in"], small_partly = _weight_grad(x2, d_proj, "wgrad_w_in", 896, _gather_first_level([small_packed]))
    w_in_sum = pair_sums(["w_in"], contrib, _swap_with_sibling([contrib["w_in"]], "swap_w_in_with_sibling"))
    grad_x, parts["w_in"], every = _grad_x(d_proj, wt["w_in"], dr1,
                                          _chip_swap_ride(w_in_sum) + _gather_second_level([small_partly]))

    grads, deltas, new_m, new_v = {}, {}, {}, {}
    for k in sharded:
        w2 = local[k]
        if k in narrow:
            out = _adamw_transposed(w2.T, given["m_" + k][0].T, given["v_" + k][0].T, parts[k], "adamw_" + k)
            out = [o.T for o in out]
        else:
            out = _adamw(w2, given["m_" + k][0], given["v_" + k][0], parts[k], "adamw_" + k)
        grads[k], deltas[k], new_m[k], new_v[k] = (o.reshape((1,) + w2.shape) for o in out)

    def held(k, a):
        return a.transpose(0, 1, 3, 2) if k in ("ssm_b_re", "ssm_b_im") else a

    *small_grads, loss_sum = _unpack_rows(_sum_devices(every), [held(k, given[k]).shape for k in SMALL] + [(1, 128)])
    small = _adamw_replicated([held(k, given[k]) for k in SMALL], [held(k, given["m_" + k]) for k in SMALL],
                              [held(k, given["v_" + k]) for k in SMALL], small_grads)
    for res, values in zip((grads, deltas, new_m, new_v), (small_grads,) + small):
        res.update((k, held(k, a)) for k, a in zip(SMALL, values))
    loss = loss_sum[0, 0]

    order = ("w_in", "b_gate", "w_attn_br", "w_ssm_br", "w_out", "ssm_a_re", "ssm_a_im", "ssm_log_dt", "ssm_b_re", "ssm_b_im",
             "ssm_c_re", "ssm_c_im", "ssm_d", "w_glu", "ln1_g", "ln1_b", "w_ff_gate", "w_ff_up", "w_ff_down", "ln2_g", "ln2_b")
    return (loss, grad_x[None], *[grads[k] for k in order], *[deltas[k] for k in order], *[new_m[k] for k in order],
            *[new_v[k] for k in order])
```

```python
import functools
import math

import jax
import jax.numpy as jnp
import numpy as np
from jax import lax
from jax.experimental import pallas as pl
from jax.experimental.pallas import tpu as pltpu

F32 = jnp.float32
BF16 = jnp.bfloat16

N_DEV = 8
SEQ = 2048
D_MODEL = 1024
HEAD_DIM = 64
ATTN_WIDTH = 512
QKV_WIDTH = 1536
SSM_WIDTH = 512
SSM_GROUPS = 32
SSM_GROUP = 16
SSM_STATE = 64
IN_WIDTH = 7168
D_FF = 2816
FF_SHARD = D_FF // N_DEV
FF_PAD = 384
D_FF_PAD = FF_PAD * N_DEV
DN_ALPHA = 2.0 ** 0.25
LN_EPS = 1e-5
NEG_INF = -1e30
ROPE_THETA = 10000.0
BLOCK = 128
GROUPS = ((1, 16), (4, 4), (16, 1))

ADAM_LR = 0.001
ADAM_B1 = 0.9
ADAM_B2 = 0.999
ADAM_EPS = 1e-08
ADAM_WD = 0.01
ADAM_STEP = 10

VMEM_LIMIT = 56 * 1024 * 1024


_pallas_call = pl.pallas_call


def _cparams(**kw):
    return pltpu.CompilerParams(vmem_limit_bytes=VMEM_LIMIT, **kw)


def _dot(a, b):
    return jnp.dot(a, b, preferred_element_type=F32)


def _dot_nt(a, b):
    return lax.dot_general(a, b, (((1,), (1,)), ((), ())), preferred_element_type=F32)


def _side_by_side(w_ref, row=None):
    rows = slice(None) if row is None else pl.ds(row, 1)
    return jnp.concatenate([w_ref[i, rows, :] for i in range(w_ref.shape[0])], axis=1)


def _dot_tn(a, b):
    return lax.dot_general(a, b, (((0,), (0,)), ((), ())), preferred_element_type=F32)


def _rope_tables():
    half = HEAD_DIM // 2
    inv_freq = np.float32(ROPE_THETA) ** (-np.arange(half, dtype=np.float32) / np.float32(half))
    ang = np.arange(SEQ, dtype=np.float32)[:, None] * inv_freq[None, :]
    cos, sin = np.cos(ang).astype(np.float32), np.sin(ang).astype(np.float32)
    tables = np.tile(cos, (1, 4)), np.tile(np.concatenate([-sin, sin], axis=1), (1, 2))

    def by_phase(t):
        return np.stack([t.reshape(SEQ // d, d, 128).transpose(1, 0, 2).reshape(SEQ, 128) for d, _ in GROUPS])

    return jnp.asarray(by_phase(tables[0])), jnp.asarray(by_phase(tables[1]))


def _swap_halves(x):
    lane = lax.broadcasted_iota(jnp.int32, x.shape, 1)
    return jnp.where((lane & 63) < 32, pltpu.roll(x, 96, axis=1), pltpu.roll(x, 32, axis=1))


def _group_rows(d, nb, r, i):
    src = pl.ds(i * BLOCK, BLOCK) if d == 1 else pl.ds(r + i * BLOCK * d, BLOCK, stride=d)
    return src, pl.ds((r * nb + i) * BLOCK, BLOCK)


def _attn_masks():
    a_idx = lax.broadcasted_iota(jnp.int32, (2 * BLOCK, 2 * BLOCK), 0) & (BLOCK - 1)
    c_idx = lax.broadcasted_iota(jnp.int32, (2 * BLOCK, 2 * BLOCK), 1)
    cur_ok = jnp.logical_and(c_idx >= BLOCK, c_idx - BLOCK <= a_idx)
    prev_ok = jnp.logical_and(c_idx < BLOCK, c_idx >= a_idx)
    lane = lax.broadcasted_iota(jnp.int32, (BLOCK, 128), 1)
    return cur_ok, prev_ok, lane < HEAD_DIM


def _stack_heads(t, head0):
    zero = jnp.zeros_like(t)
    return jnp.concatenate([jnp.where(head0, t, zero), jnp.where(head0, zero, t)], axis=0)


def _unstack_heads(t2, head0):
    return jnp.where(head0, t2[:BLOCK], t2[BLOCK:])


def _attn_fwd(proj, cos_t, sin_t, ride=None):
    def body(q0, q1, q2, k0, k1, k2, v0, v1, v2, cos_ref, sin_ref, attn_ref, lse_ref,
             qs, ks, vs, os_, ms, ls, acc, mnat, lnat):
        cur_ok, prev_ok, head0 = _attn_masks()
        ks[:BLOCK, :] = jnp.zeros((BLOCK, 128), BF16)
        vs[:BLOCK, :] = jnp.zeros((BLOCK, 128), BF16)
        for g, (d, nb) in enumerate(GROUPS):
            q_ref, k_ref, v_ref = (q0, q1, q2)[g], (k0, k1, k2)[g], (v0, v1, v2)[g]
            for r in range(d):
                for i in range(nb):
                    src, dst = _group_rows(d, nb, r, i)
                    below = pl.ds(dst.start + BLOCK, BLOCK)
                    c, s = cos_ref[g, dst, :], sin_ref[g, dst, :]
                    q = q_ref[src, :]
                    k = k_ref[src, :]
                    qs[dst, :] = ((q * c + _swap_halves(q) * s) * 0.125).astype(BF16)
                    ks[below, :] = (k * c + _swap_halves(k) * s).astype(BF16)
                    vs[below, :] = v_ref[src, :].astype(BF16)

            def block(b, carry, nb=nb):
                has_prev = (b & (nb - 1)) > 0
                cur = pl.ds(pl.multiple_of(b * BLOCK, BLOCK), BLOCK)
                window = pl.ds(pl.multiple_of(b * BLOCK, BLOCK), 2 * BLOCK)
                valid = jnp.logical_or(cur_ok, jnp.logical_and(prev_ok, has_prev))
                s = jnp.where(valid, _dot_nt(_stack_heads(qs[cur, :], head0), ks[window, :]), NEG_INF)
                m = jnp.max(s, axis=1, keepdims=True)
                p = jnp.exp(s - m)
                os_[cur, :] = _unstack_heads(_dot(p.astype(BF16), vs[window, :]), head0)
                ms[cur, :] = _unstack_heads(m, head0)
                ls[cur, :] = _unstack_heads(jnp.sum(p, axis=1, keepdims=True), head0)
                return carry

            lax.fori_loop(0, SEQ // BLOCK, block, 0, unroll=16)

            for r in range(d):
                for i in range(nb):
                    src, dst = _group_rows(d, nb, r, i)
                    if g == 0:
                        acc[src, :], mnat[src, :], lnat[src, :] = os_[dst, :], ms[dst, :], ls[dst, :]
                    else:
                        m_old, m_g = mnat[src, :], ms[dst, :]
                        m_new = jnp.maximum(m_old, m_g)
                        a_old, a_g = jnp.exp(m_old - m_new), jnp.exp(m_g - m_new)
                        acc[src, :] = a_old * acc[src, :] + a_g * os_[dst, :]
                        lnat[src, :] = a_old * lnat[src, :] + a_g * ls[dst, :]
                        mnat[src, :] = m_new
        for i in range(SEQ // BLOCK):
            rows = pl.ds(i * BLOCK, BLOCK)
            l = lnat[rows, :]
            attn_ref[rows, :] = acc[rows, :] / l
            lse_ref[rows, :] = mnat[rows, :] + jnp.log(l)

    def col(base):
        return pl.BlockSpec((SEQ, 128), lambda hp, base=base: (0, base + hp))

    in_specs = [col(g * 4) for g in range(3)] + [col(12 + g * 4) for g in range(3)] + [col(24 + g * 4) for g in range(3)]
    table = pl.BlockSpec((3, SEQ, 128), lambda hp: (0, 0, 0), pipeline_mode=pl.Buffered(1))
    out = pl.BlockSpec((SEQ, 128), lambda hp: (0, hp))
    return _call(
        body, "attn_fwd", (4,), in_specs + [table, table], [out, out],
        [_sds((SEQ, ATTN_WIDTH), F32), _sds((SEQ, ATTN_WIDTH), F32)],
        [pltpu.VMEM((SEQ, 128), BF16)] + [pltpu.VMEM((SEQ + BLOCK, 128), BF16)] * 2 + [pltpu.VMEM((SEQ, 128), F32)] * 6,
        [proj] * 9 + [cos_t, sin_t], ride)


def _attn_bwd_group_body(g):
    d, nb = GROUPS[g]

    def body(q_ref, k_ref, v_ref, cos_ref, sin_ref, lse_ref, dattn_ref, dsum_ref, dproj_ref,
             qs, ks, vs, dos, lss, dss, dqs, dks, dvs, stage, outs, sems):
        cur_ok, prev_ok, head0 = _attn_masks()
        ks[:BLOCK, :] = jnp.zeros((BLOCK, 128), BF16)
        vs[:BLOCK, :] = jnp.zeros((BLOCK, 128), BF16)
        dks[:BLOCK, :] = jnp.zeros((BLOCK, 128), F32)
        dvs[:BLOCK, :] = jnp.zeros((BLOCK, 128), F32)
        for r in range(d):
            for i in range(nb):
                src, dst = _group_rows(d, nb, r, i)
                below = pl.ds(dst.start + BLOCK, BLOCK)
                c, s = cos_ref[g, dst, :], sin_ref[g, dst, :]
                q = q_ref[src, :]
                k = k_ref[src, :]
                qs[dst, :] = ((q * c + _swap_halves(q) * s) * 0.125).astype(BF16)
                ks[below, :] = (k * c + _swap_halves(k) * s).astype(BF16)
                vs[below, :] = v_ref[src, :].astype(BF16)
                dos[dst, :] = dattn_ref[src, :].astype(BF16)
                dss[dst, :] = dsum_ref[src, :]
                lss[dst, :] = lse_ref[src, :]
                dks[below, :] = jnp.zeros((BLOCK, 128), F32)
                dvs[below, :] = jnp.zeros((BLOCK, 128), F32)

        def per_head_column(t):
            return jnp.concatenate([jnp.max(jnp.where(head0, t, NEG_INF), axis=1, keepdims=True),
                                    jnp.max(jnp.where(head0, NEG_INF, t), axis=1, keepdims=True)], axis=0)

        def block(b, carry):
            has_prev = (b & (nb - 1)) > 0
            cur = pl.ds(pl.multiple_of(b * BLOCK, BLOCK), BLOCK)
            window = pl.ds(pl.multiple_of(b * BLOCK, BLOCK), 2 * BLOCK)
            valid = jnp.logical_or(cur_ok, jnp.logical_and(prev_ok, has_prev))
            q2, do2 = _stack_heads(qs[cur, :], head0), _stack_heads(dos[cur, :], head0)
            kw, vw = ks[window, :], vs[window, :]
            s = jnp.where(valid, _dot_nt(q2, kw), NEG_INF)
            p = jnp.exp(s - per_head_column(lss[cur, :]))
            ds = (p * (_dot_nt(do2, vw) - per_head_column(dss[cur, :]))).astype(BF16)
            dvs[window, :] += _dot_tn(p.astype(BF16), do2)
            dks[window, :] += _dot_tn(ds, q2)
            dqs[cur, :] = _unstack_heads(_dot(ds, kw), head0)
            return carry

        lax.fori_loop(0, SEQ // BLOCK, block, 0, unroll=8)

        hp = pl.program_id(0)
        copies = []
        for kind in range(3):
            for r in range(d):
                for i in range(nb):
                    src, dst = _group_rows(d, nb, r, i)
                    below = pl.ds(dst.start + BLOCK, BLOCK)
                    if kind == 2:
                        stage[src, :] = dvs[below, :]
                    else:
                        c, s = cos_ref[g, dst, :], sin_ref[g, dst, :]
                        t = dqs[dst, :] * 0.125 if kind == 0 else dks[below, :]
                        stage[src, :] = t * c - _swap_halves(t) * s
            for i in range(SEQ // MM_ROWS):
                rows = pl.ds(i * MM_ROWS, MM_ROWS)
                outs[kind, rows, :] = stage[rows, :].astype(BF16)
            column = pl.multiple_of((kind * 12 + g * 4 + hp) * 128, 128)
            copies.append(pltpu.make_async_copy(outs.at[kind], dproj_ref.at[:, pl.ds(column, 128)], sems.at[kind]))
            copies[-1].start()
        for cp in copies:
            cp.wait()

    return body


def _attn_bwd(proj, cos_t, sin_t, attn, lse, dattn, dproj, ride=None):
    groups = [_attn_bwd_group_body(g) for g in range(3)]

    def body(q0, q1, q2, k0, k1, k2, v0, v1, v2, cos_ref, sin_ref, attn_ref, lse_ref, dattn_ref, dproj_in, dproj_ref,
             dsum, *scratch):
        del dproj_in
        head0 = _attn_masks()[2]
        for i in range(SEQ // BLOCK):
            rows = pl.ds(i * BLOCK, BLOCK)
            prod = dattn_ref[rows, :] * attn_ref[rows, :]
            d0 = jnp.sum(jnp.where(head0, prod, 0.0), axis=1, keepdims=True)
            d1 = jnp.sum(jnp.where(head0, 0.0, prod), axis=1, keepdims=True)
            dsum[rows, :] = jnp.where(head0, d0, d1)
        for g in range(3):
            groups[g]((q0, q1, q2)[g], (k0, k1, k2)[g], (v0, v1, v2)[g], cos_ref, sin_ref, lse_ref, dattn_ref, dsum,
                      dproj_ref, *scratch)

    def col(base):
        return pl.BlockSpec((SEQ, 128), lambda hp, base=base: (0, base + hp))

    table = pl.BlockSpec((3, SEQ, 128), lambda hp: (0, 0, 0), pipeline_mode=pl.Buffered(1))
    return _call(
        body, "attn_bwd", (4,),
        [col(g * 4) for g in range(3)] + [col(12 + g * 4) for g in range(3)] + [col(24 + g * 4) for g in range(3)]
        + [table, table, col(0), col(0), col(0), ANY],
        [ANY], [_sds((SEQ, IN_WIDTH), BF16)],
        [pltpu.VMEM((SEQ, 128), F32)]
        + [pltpu.VMEM((SEQ, 128), BF16)] + [pltpu.VMEM((SEQ + BLOCK, 128), BF16)] * 2 + [pltpu.VMEM((SEQ, 128), BF16)]
        + [pltpu.VMEM((SEQ, 128), F32)] * 3 + [pltpu.VMEM((SEQ + BLOCK, 128), F32)] * 2 + [pltpu.VMEM((SEQ, 128), F32)]
        + [pltpu.VMEM((3, SEQ, 128), BF16), pltpu.SemaphoreType.DMA((3,))],
        [proj] * 9 + [cos_t, sin_t, attn, lse, dattn, dproj], ride, aliases={14: 0})


SSM_CHUNKS = 4
CHUNK_STATES = 512
SCAN_ROWS = 8
U_COL = (3 * QKV_WIDTH) // 128


def _cmul(xr, xi, yr, yi):
    return xr * yr - xi * yi, xr * yi + xi * yr


def _ssm_prep(a_re, a_im, log_dt, b_re_t, b_im_t):
    def body(ar_ref, ai_ref, ldt_ref, br_ref, bi_ref, abr_ref, abi_ref, er_ref, ei_ref, bbr_ref, bbi_ref):
        ar, ai = ar_ref[...], ai_ref[...]
        dt = jnp.exp(ldt_ref[...])
        mag = jnp.exp(ar * dt)
        abr, abi = mag * jnp.cos(ai * dt), mag * jnp.sin(ai * dt)
        den = ar * ar + ai * ai
        nr, ni = abr - 1.0, abi
        er, ei = (nr * ar + ni * ai) / den, (ni * ar - nr * ai) / den
        abr_ref[...], abi_ref[...], er_ref[...], ei_ref[...] = abr, abi, er, ei
        er3, ei3 = er[:, None, :], ei[:, None, :]
        br, bi = br_ref[...], bi_ref[...]
        bbr_ref[...] = er3 * br - ei3 * bi
        bbi_ref[...] = er3 * bi + ei3 * br

    gp = jax.ShapeDtypeStruct(a_re.shape, F32)
    gb = jax.ShapeDtypeStruct(b_re_t.shape, F32)
    return _pallas_call(body, name="ssm_prep", out_shape=(gp, gp, gp, gp, gb, gb))(a_re, a_im, log_dt, b_re_t, b_im_t)


def _ssm_param_bwd(a_re, a_im, log_dt, b_re_t, b_im_t, abar_re, abar_im, e_re, e_im, ga_re, ga_im, gbb_re_t, gbb_im_t):
    def body(ar_ref, ai_ref, ldt_ref, br_ref, bi_ref, abr_ref, abi_ref, er_ref, ei_ref, gar_ref, gai_ref, gbr_ref, gbi_ref,
             o_ar, o_ai, o_ldt, o_br, o_bi):
        ar, ai = ar_ref[...], ai_ref[...]
        dt = jnp.exp(ldt_ref[...])
        er, ei = er_ref[...], ei_ref[...]
        br, bi, gbr, gbi = br_ref[...], bi_ref[...], gbr_ref[...], gbi_ref[...]
        er3, ei3 = er[:, None, :], ei[:, None, :]
        o_br[...] = er3 * gbr + ei3 * gbi
        o_bi[...] = er3 * gbi - ei3 * gbr
        ge_r = jnp.sum(br * gbr + bi * gbi, axis=1)
        ge_i = jnp.sum(br * gbi - bi * gbr, axis=1)
        den = ar * ar + ai * ai
        ilr, ili = ar / den, -ai / den
        t_r, t_i = _cmul(ilr, -ili, ge_r, ge_i)
        gab_r, gab_i = gar_ref[...] + t_r, gai_ref[...] + t_i
        gz_r, gz_i = _cmul(abr_ref[...], -abi_ref[...], gab_r, gab_i)
        el_r, el_i = _cmul(er, ei, ilr, ili)
        u_r, u_i = _cmul(el_r, -el_i, ge_r, ge_i)
        o_ar[...] = dt * gz_r - u_r
        o_ai[...] = dt * gz_i - u_i
        o_ldt[...] = jnp.sum(gz_r * ar + gz_i * ai, axis=1, keepdims=True) * dt

    gp = jax.ShapeDtypeStruct(a_re.shape, F32)
    gb = jax.ShapeDtypeStruct(b_re_t.shape, F32)
    return _pallas_call(body, name="ssm_param_bwd", out_shape=(gp, gp, jax.ShapeDtypeStruct(log_dt.shape, F32), gb, gb))(
        a_re, a_im, log_dt, b_re_t, b_im_t, abar_re, abar_im, e_re, e_im, ga_re, ga_im, gbb_re_t, gbb_im_t)


def _block_diag(blocks_re, blocks_im, sign_im, rows_are_channels):
    both = jnp.stack([blocks_re, sign_im * blocks_im]).reshape(2, SSM_CHUNKS, 8, SSM_GROUP, SSM_STATE)
    eye = jnp.eye(8, dtype=F32)
    if rows_are_channels:
        return jnp.einsum("rcghp,gk->cghrkp", both, eye).reshape(SSM_CHUNKS, 128, 2 * CHUNK_STATES)
    return jnp.einsum("rcghp,gk->crkpgh", both, eye).reshape(SSM_CHUNKS, 2 * CHUNK_STATES, 128)


def _block_diag_parts(mat, rows_are_channels):
    if rows_are_channels:
        six = mat.reshape(SSM_CHUNKS, 8, SSM_GROUP, 2, 8, SSM_STATE)
        parts = jnp.einsum("cghrgp->rcghp", six)
    else:
        six = mat.reshape(SSM_CHUNKS, 2, 8, SSM_STATE, 8, SSM_GROUP)
        parts = jnp.einsum("crgpgh->rcghp", six)
    parts = parts.reshape(2, SSM_GROUPS, SSM_GROUP, SSM_STATE)
    return parts[0], parts[1]


def _scan_consts(a_ref, conj, reverse):
    ar = jnp.broadcast_to(a_ref[:, :CHUNK_STATES], (SCAN_ROWS, CHUNK_STATES))
    ai = jnp.broadcast_to(a_ref[:, CHUNK_STATES:], (SCAN_ROWS, CHUNK_STATES))
    if conj:
        ai = -ai
    row = lax.broadcasted_iota(jnp.int32, (SCAN_ROWS, CHUNK_STATES), 0)
    if reverse:
        row = SCAN_ROWS - 1 - row
    zero = jnp.zeros_like(ar)
    steps = []
    pr, pi = ar, ai
    for shift in (1, 2, 4):
        keep = row >= shift
        steps.append((SCAN_ROWS - shift if reverse else shift, jnp.where(keep, pr, zero), jnp.where(keep, pi, zero)))
        pr, pi = _cmul(pr, pi, pr, pi)
    first = row == 0
    return steps, (jnp.where(first, ar, zero), jnp.where(first, ai, zero)), first


def _scan_tile(xr, xi, prev_r, prev_i, steps, carry_in, reverse):
    edge = SCAN_ROWS - 1 if reverse else 1
    cr, ci = pltpu.roll(prev_r, edge, axis=0), pltpu.roll(prev_i, edge, axis=0)
    xr, xi = xr + carry_in[0] * cr - carry_in[1] * ci, xi + carry_in[0] * ci + carry_in[1] * cr
    for shift, mr, mi in steps:
        sr, si = pltpu.roll(xr, shift, axis=0), pltpu.roll(xi, shift, axis=0)
        xr, xi = xr + mr * sr - mi * si, xi + mr * si + mi * sr
    return xr, xi


MM_ROWS = 256


def _ssm_fwd(proj, bmat, cmat, a_chunks, d_skip, ride=None):
    def body(u_ref, b_ref, c_ref, a_ref, d_ref, y_ref, h_ref):
        for i in range(SEQ // MM_ROWS):
            rows = pl.ds(i * MM_ROWS, MM_ROWS)
            h_ref[rows, :] = _dot(u_ref[rows, :].astype(BF16), b_ref[...])
        steps, carry_in, _ = _scan_consts(a_ref, conj=False, reverse=False)

        def tile(k, carry):
            rows = pl.ds(pl.multiple_of(k * SCAN_ROWS, SCAN_ROWS), SCAN_ROWS)
            xr, xi = _scan_tile(h_ref[rows, :CHUNK_STATES], h_ref[rows, CHUNK_STATES:], carry[0], carry[1], steps, carry_in, False)
            h_ref[rows, :CHUNK_STATES] = xr
            h_ref[rows, CHUNK_STATES:] = xi
            return xr, xi

        zero = jnp.zeros((SCAN_ROWS, CHUNK_STATES), F32)
        lax.fori_loop(0, SEQ // SCAN_ROWS, tile, (zero, zero), unroll=4)
        for i in range(SEQ // MM_ROWS):
            rows = pl.ds(i * MM_ROWS, MM_ROWS)
            y_ref[rows, :] = _dot(h_ref[rows, :].astype(BF16), c_ref[...]) + d_ref[...] * u_ref[rows, :]

    return _call(
        body, "ssm_fwd", (SSM_CHUNKS,),
        [pl.BlockSpec((SEQ, 128), lambda c: (0, U_COL + c)),
         pl.BlockSpec((None, 128, 2 * CHUNK_STATES), lambda c: (c, 0, 0)),
         pl.BlockSpec((None, 2 * CHUNK_STATES, 128), lambda c: (c, 0, 0)),
         pl.BlockSpec((None, 1, 2 * CHUNK_STATES), lambda c: (c, 0, 0)),
         pl.BlockSpec((1, 128), lambda c: (0, c))],
        [pl.BlockSpec((SEQ, 128), lambda c: (0, c)), pl.BlockSpec((SEQ, 2 * CHUNK_STATES), lambda c: (0, c))],
        [_sds((SEQ, SSM_WIDTH), F32), _sds((SEQ, SSM_CHUNKS * 2 * CHUNK_STATES), F32)], [],
        [proj, bmat, cmat, a_chunks, d_skip], ride)


def _ssm_bwd(dys, proj, h, bmat, cmat, a_chunks, d_skip, dproj, ride=None):
    def body(dy_ref, u_ref, h_ref, b_ref, c_ref, a_ref, d_ref, dproj_in, du_ref, db_ref, dc_ref, da_ref, dd_ref, g_ref):
        del dproj_in
        dsum = jnp.zeros((1, 128), F32)
        dcm = jnp.zeros((2 * CHUNK_STATES, 128), F32)
        for i in range(SEQ // MM_ROWS):
            rows = pl.ds(i * MM_ROWS, MM_ROWS)
            dy = dy_ref[rows, :]
            g_ref[rows, :] = _dot_nt(dy.astype(BF16), c_ref[...])
            dsum += jnp.sum(dy * u_ref[rows, :], axis=0, keepdims=True)
            dcm += _dot_tn(h_ref[rows, :].astype(BF16), dy.astype(BF16))
        dd_ref[...] = dsum
        dc_ref[...] = dcm
        steps, carry_in, _ = _scan_consts(a_ref, conj=True, reverse=True)
        first_row = lax.broadcasted_iota(jnp.int32, (SCAN_ROWS, CHUNK_STATES), 0) == 0
        n_tiles = SEQ // SCAN_ROWS

        def tile(j, carry):
            k = n_tiles - 1 - j
            rows = pl.ds(pl.multiple_of(k * SCAN_ROWS, SCAN_ROWS), SCAN_ROWS)
            before = pl.ds(pl.multiple_of(jnp.maximum(k - 1, 0) * SCAN_ROWS, SCAN_ROWS), SCAN_ROWS)
            gr, gi = _scan_tile(g_ref[rows, :CHUNK_STATES], g_ref[rows, CHUNK_STATES:], carry[0], carry[1], steps, carry_in, True)
            g_ref[rows, :CHUNK_STATES] = gr
            g_ref[rows, CHUNK_STATES:] = gi
            has_before = jnp.where(k > 0, 1.0, 0.0)
            hr = jnp.where(first_row, pltpu.roll(h_ref[before, :CHUNK_STATES], 1, axis=0) * has_before,
                           pltpu.roll(h_ref[rows, :CHUNK_STATES], 1, axis=0))
            hi = jnp.where(first_row, pltpu.roll(h_ref[before, CHUNK_STATES:], 1, axis=0) * has_before,
                           pltpu.roll(h_ref[rows, CHUNK_STATES:], 1, axis=0))
            return gr, gi, carry[2] + hr * gr + hi * gi, carry[3] + hr * gi - hi * gr

        zero = jnp.zeros((SCAN_ROWS, CHUNK_STATES), F32)
        _, _, sar, sai = lax.fori_loop(0, n_tiles, tile, (zero, zero, zero, zero), unroll=4)
        da_ref[:, :CHUNK_STATES] = jnp.sum(sar, axis=0, keepdims=True)
        da_ref[:, CHUNK_STATES:] = jnp.sum(sai, axis=0, keepdims=True)
        dbm = jnp.zeros((128, 2 * CHUNK_STATES), F32)
        for i in range(SEQ // MM_ROWS):
            rows = pl.ds(i * MM_ROWS, MM_ROWS)
            g = g_ref[rows, :].astype(BF16)
            du_ref[rows, :] = (_dot_nt(g, b_ref[...]) + d_ref[...] * dy_ref[rows, :]).astype(BF16)
            dbm += _dot_tn(u_ref[rows, :].astype(BF16), g)
        db_ref[...] = dbm

    chunk_col = pl.BlockSpec((SEQ, 128), lambda c: (0, c))
    return _call(
        body, "ssm_bwd", (SSM_CHUNKS,),
        [chunk_col,
         pl.BlockSpec((SEQ, 128), lambda c: (0, U_COL + c)),
         pl.BlockSpec((SEQ, 2 * CHUNK_STATES), lambda c: (0, c)),
         pl.BlockSpec((None, 128, 2 * CHUNK_STATES), lambda c: (c, 0, 0)),
         pl.BlockSpec((None, 2 * CHUNK_STATES, 128), lambda c: (c, 0, 0)),
         pl.BlockSpec((None, 1, 2 * CHUNK_STATES), lambda c: (c, 0, 0)),
         pl.BlockSpec((1, 128), lambda c: (0, c)), ANY],
        [pl.BlockSpec((SEQ, 128), lambda c: (0, U_COL + c)),
         pl.BlockSpec((None, 128, 2 * CHUNK_STATES), lambda c: (c, 0, 0)),
         pl.BlockSpec((None, 2 * CHUNK_STATES, 128), lambda c: (c, 0, 0)),
         pl.BlockSpec((None, 1, 2 * CHUNK_STATES), lambda c: (c, 0, 0)),
         pl.BlockSpec((1, 128), lambda c: (0, c))],
        [_sds((SEQ, IN_WIDTH), BF16), _sds((SSM_CHUNKS, 128, 2 * CHUNK_STATES), F32),
         _sds((SSM_CHUNKS, 2 * CHUNK_STATES, 128), F32), _sds((SSM_CHUNKS, 1, 2 * CHUNK_STATES), F32), _sds((1, SSM_WIDTH), F32)],
        [pltpu.VMEM((SEQ, 2 * CHUNK_STATES), F32)], [dys, proj, h, bmat, cmat, a_chunks, d_skip, dproj], ride, aliases={7: 0})


def _ssm_tables(abar_re, abar_im, bbar_re_t, bbar_im_t, c_re, c_im):
    bmat = _block_diag(bbar_re_t, bbar_im_t, 1.0, True).astype(BF16)
    cmat = _block_diag(c_re, c_im, -1.0, False).astype(BF16)
    a_chunks = jnp.concatenate([abar_re.reshape(SSM_CHUNKS, 1, CHUNK_STATES), abar_im.reshape(SSM_CHUNKS, 1, CHUNK_STATES)], axis=2)
    return bmat, cmat, a_chunks


GL_COL = (3 * QKV_WIDTH + SSM_WIDTH) // D_MODEL
GELU_C = math.sqrt(2.0 / math.pi)
GELU_A = 0.044715


def _sds(shape, dtype):
    return jax.ShapeDtypeStruct(shape, dtype)


def _gelu(x):
    t = jnp.tanh(GELU_C * (x + GELU_A * x * x * x))
    return 0.5 * x * (1.0 + t), t


def _gelu_grad(x, t):
    return 0.5 * (1.0 + t) + 0.5 * x * (1.0 - t * t) * GELU_C * (1.0 + 3.0 * GELU_A * x * x)


def _layer_norm(r, g, b):
    mu = jnp.mean(r, axis=-1, keepdims=True)
    xc = r - mu
    rstd = lax.rsqrt(jnp.mean(xc * xc, axis=-1, keepdims=True) + LN_EPS)
    xhat = xc * rstd
    return xhat * g + b, xhat, rstd


def _layer_norm_bwd(dy, xhat, rstd, g):
    dxhat = dy * g
    m1 = jnp.mean(dxhat, axis=-1, keepdims=True)
    m2 = jnp.mean(dxhat * xhat, axis=-1, keepdims=True)
    return rstd * (dxhat - m1 - xhat * m2)


def _proj(x, w_in, ride=None):
    tm, tn = 1024, 1792

    def body(x_ref, w_ref, o_ref):
        o_ref[...] = _dot(x_ref[...].astype(BF16), _side_by_side(w_ref))

    return _call(
        body, "proj", (SEQ // tm, IN_WIDTH // tn),
        [pl.BlockSpec((tm, D_MODEL), lambda i, j: (i, 0)), pl.BlockSpec((2, D_MODEL, tn // 2), lambda i, j: (j, 0, 0))],
        [pl.BlockSpec((tm, tn), lambda i, j: (i, j))], [_sds((SEQ, IN_WIDTH), F32)], [], [x, w_in], ride)


def _row_spec(tm, width, col=0):
    return pl.BlockSpec((tm, width), lambda i, col=col: (i, col))


def _full_spec(shape):
    return pl.BlockSpec(shape, lambda i: (0,) * len(shape))


def _weight_spec(shape):
    return pl.BlockSpec(shape, lambda i: (0,) * len(shape), pipeline_mode=pl.Buffered(1))


def _mixer_out(attn, ys, proj, x, w_ab, w_sb, w_glu, w_out, b_gate, ln_g, ln_b, ride=None):
    tm = 512

    def body(attn_ref, ys_ref, gl0_ref, gl1_ref, x_ref, wab_ref, wsb_ref, wglu_ref, wout_ref, bg_ref, g_ref, b_ref,
             h_ref, xhat_ref, rstd_ref, glu_ref, ya_ref, yssm_ref):
        gy, _ = _gelu(ys_ref[...])
        glu = _dot(gy.astype(BF16), _side_by_side(wglu_ref))
        glu_ref[...] = glu
        y_s = glu[:, :SSM_WIDTH] * jax.nn.sigmoid(glu[:, SSM_WIDTH:])
        y_ssm = _dot(y_s.astype(BF16), _side_by_side(wsb_ref))
        y_attn = _dot(attn_ref[...].astype(BF16), _side_by_side(wab_ref))
        ya_ref[...] = y_attn
        yssm_ref[...] = y_ssm
        g0 = jax.nn.sigmoid(gl0_ref[...] + _side_by_side(bg_ref, 0))
        g1 = jax.nn.sigmoid(gl1_ref[...] + _side_by_side(bg_ref, 1))
        mixed = g0 * y_attn + g1 * y_ssm
        r1 = DN_ALPHA * x_ref[...] + _dot(mixed.astype(BF16), wout_ref[...])
        h, xhat, rstd = _layer_norm(r1, g_ref[...], b_ref[...])
        h_ref[...] = h
        xhat_ref[...] = xhat
        rstd_ref[...] = jnp.broadcast_to(rstd, (tm, 128))

    wide = _sds((SEQ, D_MODEL), F32)
    return _call(
        body, "mixer_out", (SEQ // tm,),
        [_row_spec(tm, ATTN_WIDTH), _row_spec(tm, SSM_WIDTH), _row_spec(tm, D_MODEL, GL_COL), _row_spec(tm, D_MODEL, GL_COL + 1),
         _row_spec(tm, D_MODEL), _weight_spec((N_DEV, ATTN_WIDTH, 128)), _weight_spec((N_DEV, SSM_WIDTH, 128)),
         _weight_spec((N_DEV, SSM_WIDTH, 128)), _weight_spec((D_MODEL, D_MODEL)), _full_spec((N_DEV, 2, 128)),
         _full_spec((1, D_MODEL)), _full_spec((1, D_MODEL))],
        [_row_spec(tm, D_MODEL), _row_spec(tm, D_MODEL), _row_spec(tm, 128), _row_spec(tm, D_MODEL),
         _row_spec(tm, D_MODEL), _row_spec(tm, D_MODEL)],
        [wide, wide, _sds((SEQ, 128), F32), wide, wide, wide], [],
        [attn, ys, proj, proj, x, w_ab, w_sb, w_glu, w_out, b_gate, ln_g, ln_b], ride)


def _ff_up(h, w_gate, w_up, ride=None):
    tm, tn = 1024, 768

    def body(h_ref, wg_ref, wu_ref, a_ref, b_ref, f_ref):
        hb = h_ref[...].astype(BF16)
        a, b = _dot(hb, _side_by_side(wg_ref)), _dot(hb, _side_by_side(wu_ref))
        a_ref[...] = a.astype(BF16)
        b_ref[...] = b.astype(BF16)
        f_ref[...] = (a * jax.nn.sigmoid(a) * b).astype(BF16)

    tile = pl.BlockSpec((tm, tn), lambda i, j: (i, j))
    wtile = pl.BlockSpec((tn // FF_PAD, D_MODEL, FF_PAD), lambda i, j: (j, 0, 0))
    out = _sds((SEQ, D_FF_PAD), BF16)
    return _call(body, "ff_up", (SEQ // tm, D_FF_PAD // tn), [pl.BlockSpec((tm, D_MODEL), lambda i, j: (i, 0)), wtile, wtile],
                 [tile, tile, tile], [out, out, out], [], [h, w_gate, w_up], ride)


def _ff_down_loss(f, w_down, h, target, ln_g, ln_b):
    tm = 512

    def body(f_ref, w_ref, h_ref, t_ref, g_ref, b_ref, dr_ref, dg_ref, db_ref, loss_ref):
        @pl.when(pl.program_id(0) == 0)
        def _():
            dg_ref[...] = jnp.zeros_like(dg_ref)
            db_ref[...] = jnp.zeros_like(db_ref)
            loss_ref[...] = jnp.zeros_like(loss_ref)

        r2 = DN_ALPHA * h_ref[...] + _dot(f_ref[...], w_ref[...])
        g = g_ref[...]
        out, xhat, rstd = _layer_norm(r2, g, b_ref[...])
        err = out - t_ref[...]
        loss_ref[...] += 0.5 * jnp.sum(jnp.mean(err * err, axis=-1, keepdims=True), axis=0, keepdims=True)
        dout = err * (1.0 / D_MODEL)
        dg_ref[...] += jnp.sum(dout * xhat, axis=0, keepdims=True)
        db_ref[...] += jnp.sum(dout, axis=0, keepdims=True)
        dr_ref[...] = _layer_norm_bwd(dout, xhat, rstd, g)

    vec = _sds((1, D_MODEL), F32)
    return _pallas_call(
        body, name="ff_down_loss", grid=(SEQ // tm,),
        in_specs=[_row_spec(tm, D_FF_PAD), _weight_spec((D_FF_PAD, D_MODEL)), _row_spec(tm, D_MODEL), _row_spec(tm, D_MODEL),
                  _full_spec((1, D_MODEL)), _full_spec((1, D_MODEL))],
        out_specs=(_row_spec(tm, D_MODEL), _full_spec((1, D_MODEL)), _full_spec((1, D_MODEL)), _full_spec((1, 128))),
        out_shape=(_sds((SEQ, D_MODEL), F32), vec, vec, _sds((1, 128), F32)),
        compiler_params=_cparams(dimension_semantics=("arbitrary",)),
    )(f, w_down, h, target, ln_g, ln_b)


def _ff_down_bwd(dr2, w_down, a, b):
    tm, tn = 1024, 768

    def body(dr_ref, w_ref, a_ref, b_ref, da_ref, db_ref):
        df = _dot_nt(dr_ref[...].astype(BF16), w_ref[...])
        av, bv = a_ref[...].astype(F32), b_ref[...].astype(F32)
        sg = jax.nn.sigmoid(av)
        da_ref[...] = (df * bv * sg * (1.0 + av * (1.0 - sg))).astype(BF16)
        db_ref[...] = (df * av * sg).astype(BF16)

    tile = pl.BlockSpec((tm, tn), lambda i, j: (i, j))
    out = _sds((SEQ, D_FF_PAD), BF16)
    return _pallas_call(
        body, name="ff_down_bwd", grid=(SEQ // tm, D_FF_PAD // tn),
        in_specs=[pl.BlockSpec((tm, D_MODEL), lambda i, j: (i, 0)), pl.BlockSpec((tn, D_MODEL), lambda i, j: (j, 0)), tile, tile],
        out_specs=(tile, tile), out_shape=(out, out),
        compiler_params=_cparams(dimension_semantics=("arbitrary", "arbitrary")),
    )(dr2, w_down, a, b)


def _ff_up_bwd(da, db, w_gate, w_up, dr2, xhat1, rstd1, ln_g, ride=None):
    tm, tk = 1024, 768
    nk = D_FF_PAD // tk

    def body(da_ref, db_ref, wg_ref, wu_ref, dr2_ref, xhat_ref, rstd_ref, g_ref, dr1_ref, dg_ref, dbias_ref, acc):
        i, k = pl.program_id(0), pl.program_id(1)

        @pl.when(jnp.logical_and(i == 0, k == 0))
        def _():
            dg_ref[...] = jnp.zeros_like(dg_ref)
            dbias_ref[...] = jnp.zeros_like(dbias_ref)

        part = _dot_nt(da_ref[...], _side_by_side(wg_ref)) + _dot_nt(db_ref[...], _side_by_side(wu_ref))

        @pl.when(k == 0)
        def _():
            acc[...] = part

        @pl.when(k > 0)
        def _():
            acc[...] += part

        @pl.when(k == nk - 1)
        def _():
            dh = DN_ALPHA * dr2_ref[...] + acc[...]
            xhat = xhat_ref[...]
            dg_ref[...] += jnp.sum(dh * xhat, axis=0, keepdims=True)
            dbias_ref[...] += jnp.sum(dh, axis=0, keepdims=True)
            rstd = jnp.max(rstd_ref[...], axis=1, keepdims=True)
            dr1_ref[...] = _layer_norm_bwd(dh, xhat, rstd, g_ref[...])

    hid = pl.BlockSpec((tm, tk), lambda i, k: (i, k))
    wtile = pl.BlockSpec((tk // FF_PAD, D_MODEL, FF_PAD), lambda i, k: (k, 0, 0))
    row = pl.BlockSpec((tm, D_MODEL), lambda i, k: (i, 0))
    vec = pl.BlockSpec((1, D_MODEL), lambda i, k: (0, 0))
    return _call(
        body, "ff_up_bwd", (SEQ // tm, nk),
        [hid, hid, wtile, wtile, row, row, pl.BlockSpec((tm, 128), lambda i, k: (i, 0)), vec],
        [row, vec, vec], [_sds((SEQ, D_MODEL), F32), _sds((1, D_MODEL), F32), _sds((1, D_MODEL), F32)],
        [pltpu.VMEM((tm, D_MODEL), F32)], [da, db, w_gate, w_up, dr2, xhat1, rstd1, ln_g], ride)


def _mixer_bwd(dr1, proj, y_attn, y_ssm, glu, ys, w_ab, w_sb, w_glu, w_out, b_gate):
    tm = 256

    def body(dr1_ref, gl0_ref, gl1_ref, ya_ref, yssm_ref, glu_ref, ys_ref, wab_ref, wsb_ref, wglu_ref, wout_ref, bg_ref,
             dya_ref, dyssm_ref, dgl_ref, dattn_ref, dglu_ref, dys_ref, mixed_ref, ysb_ref, gy_ref, dbg_ref):
        @pl.when(pl.program_id(0) == 0)
        def _():
            dbg_ref[...] = jnp.zeros_like(dbg_ref)

        dmixed = _dot_nt(dr1_ref[...].astype(BF16), wout_ref[...])
        g0 = jax.nn.sigmoid(gl0_ref[...] + _side_by_side(bg_ref, 0))
        g1 = jax.nn.sigmoid(gl1_ref[...] + _side_by_side(bg_ref, 1))
        y_attn, y_ssm = ya_ref[...], yssm_ref[...]
        mixed_ref[...] = (g0 * y_attn + g1 * y_ssm).astype(BF16)
        dya = (dmixed * g0).astype(BF16)
        dyssm = (dmixed * g1).astype(BF16)
        dya_ref[...] = dya
        dyssm_ref[...] = dyssm
        dgl0 = dmixed * y_attn * g0 * (1.0 - g0)
        dgl1 = dmixed * y_ssm * g1 * (1.0 - g1)
        dgl_ref[:, :GL_COL * D_MODEL] = jnp.zeros((tm, GL_COL * D_MODEL), BF16)
        dgl_ref[:, GL_COL * D_MODEL:(GL_COL + 1) * D_MODEL] = dgl0.astype(BF16)
        dgl_ref[:, (GL_COL + 1) * D_MODEL:] = dgl1.astype(BF16)
        dbg_ref[:, :D_MODEL] += jnp.sum(dgl0, axis=0, keepdims=True)
        dbg_ref[:, D_MODEL:] += jnp.sum(dgl1, axis=0, keepdims=True)
        dattn_ref[...] = _dot_nt(dya, _side_by_side(wab_ref))
        dy_s = _dot_nt(dyssm, _side_by_side(wsb_ref))
        glu = glu_ref[...]
        glu1, sg = glu[:, :SSM_WIDTH], jax.nn.sigmoid(glu[:, SSM_WIDTH:])
        ysb_ref[...] = (glu1 * sg).astype(BF16)
        dglu1 = (dy_s * sg).astype(BF16)
        dglu2 = (dy_s * glu1 * sg * (1.0 - sg)).astype(BF16)
        dglu_ref[:, :SSM_WIDTH] = dglu1
        dglu_ref[:, SSM_WIDTH:] = dglu2
        dgy = _dot_nt(jnp.concatenate([dglu1, dglu2], axis=1), _side_by_side(wglu_ref))
        ys = ys_ref[...]
        gy, t = _gelu(ys)
        gy_ref[...] = gy.astype(BF16)
        dys_ref[...] = dgy * _gelu_grad(ys, t)

    wide_b, half_b = _sds((SEQ, D_MODEL), BF16), _sds((SEQ, SSM_WIDTH), BF16)
    half_f = _sds((SEQ, SSM_WIDTH), F32)
    return _pallas_call(
        body, name="mixer_bwd", grid=(SEQ // tm,),
        in_specs=[_row_spec(tm, D_MODEL), _row_spec(tm, D_MODEL, GL_COL), _row_spec(tm, D_MODEL, GL_COL + 1), _row_spec(tm, D_MODEL),
                  _row_spec(tm, D_MODEL), _row_spec(tm, D_MODEL), _row_spec(tm, SSM_WIDTH), _full_spec((N_DEV, ATTN_WIDTH, 128)),
                  _full_spec((N_DEV, SSM_WIDTH, 128)), _full_spec((N_DEV, SSM_WIDTH, 128)), _full_spec((D_MODEL, D_MODEL)),
                  _full_spec((N_DEV, 2, 128))],
        out_specs=(_row_spec(tm, D_MODEL), _row_spec(tm, D_MODEL), _row_spec(tm, IN_WIDTH), _row_spec(tm, ATTN_WIDTH),
                   _row_spec(tm, D_MODEL), _row_spec(tm, SSM_WIDTH), _row_spec(tm, D_MODEL), _row_spec(tm, SSM_WIDTH),
                   _row_spec(tm, SSM_WIDTH), _full_spec((1, 2 * D_MODEL))),
        out_shape=(wide_b, wide_b, _sds((SEQ, IN_WIDTH), BF16), half_f, wide_b, half_f, wide_b, half_b, half_b,
                   _sds((1, 2 * D_MODEL), F32)),
        compiler_params=_cparams(dimension_semantics=("arbitrary",)),
    )(dr1, proj, proj, y_attn, y_ssm, glu, ys, w_ab, w_sb, w_glu, w_out, b_gate)


def _grad_x(dproj, w_in, dr1, ride=None):
    tm, tk = 1024, 1792
    nk = IN_WIDTH // tk

    def body(dp_ref, w_ref, dr1_ref, o_ref, acc):
        k = pl.program_id(1)
        part = _dot_nt(dp_ref[...], _side_by_side(w_ref))

        @pl.when(k == 0)
        def _():
            acc[...] = part

        @pl.when(k > 0)
        def _():
            acc[...] += part

        @pl.when(k == nk - 1)
        def _():
            o_ref[...] = DN_ALPHA * dr1_ref[...] + acc[...]

    row = pl.BlockSpec((tm, D_MODEL), lambda i, k: (i, 0))
    return _call(
        body, "grad_x", (SEQ // tm, nk),
        [pl.BlockSpec((tm, tk), lambda i, k: (i, k)), pl.BlockSpec((2, D_MODEL, tk // 2), lambda i, k: (k, 0, 0)), row],
        [row], [_sds((SEQ, D_MODEL), F32)], [pltpu.VMEM((tm, D_MODEL), F32)], [dproj, w_in, dr1], ride)


def _weight_grad(a, b, name, shard_cols=None, ride=None):
    k, n = a.shape[1], b.shape[1]
    tk = min(k, 512) if shard_cols else k // N_DEV
    tn = n // 4 if shard_cols else min(n, 1024)

    def body(a_ref, b_ref, o_ref):
        grad = _dot_tn(a_ref[...].astype(BF16), b_ref[...].astype(BF16))
        if shard_cols:
            o_ref[0] = grad[:, :shard_cols].astype(BF16)
            o_ref[1] = grad[:, shard_cols:].astype(BF16)
        else:
            o_ref[...] = grad.astype(BF16)

    if shard_cols:
        out_spec = pl.BlockSpec((2, None, tk, shard_cols), lambda kk, j: (0, j, kk, 0))
        out_shape = _sds((2, 4, k, shard_cols), BF16)
    else:
        out_spec = pl.BlockSpec((None, None, tk, tn), lambda kk, j: (kk % 2, kk // 2, 0, j))
        out_shape = _sds((2, 4, tk, n), BF16)
    out = _call(body, name, (k // tk, n // tn),
                [pl.BlockSpec((SEQ, tk), lambda kk, j: (0, kk)), pl.BlockSpec((SEQ, tn), lambda kk, j: (0, j))],
                [out_spec], [out_shape], [], [a, b], ride)
    return out[0] if ride is None else out


MESH = pl.DeviceIdType.MESH
ANY = pl.BlockSpec(memory_space=pl.ANY)


def _place():
    return lax.axis_index("x"), lax.axis_index("y"), lax.axis_index("c")


def _other_chips(x, y):
    return [(1 - x, y), (x, 1 - y), (1 - x, 1 - y)]


class _Ride:
    def __init__(self, operands, results, aliases, sems, start, wait):
        self.operands, self.results, self.aliases, self.sems = list(operands), list(results), dict(aliases), list(sems)
        self.start, self.wait = start, wait

    def __add__(self, other):
        n_in, n_out, n_sem = len(self.operands), len(self.results), len(self.sems)

        def both(which):
            def run(ins, outs, sems):
                getattr(self, which)(ins[:n_in], outs[:n_out], sems[:n_sem])
                getattr(other, which)(ins[n_in:], outs[n_out:], sems[n_sem:])
            return run

        aliases = {**self.aliases, **{n_in + i: n_out + j for i, j in other.aliases.items()}}
        return _Ride(self.operands + other.operands, self.results + other.results, aliases, self.sems + other.sems,
                     both("start"), both("wait"))


def _call(body, name, grid, in_specs, out_specs, out_shape, scratch_shapes, operands, ride=None, aliases=None):
    in_specs, out_specs, out_shape = list(in_specs), list(out_specs), list(out_shape)
    scratch_shapes, operands, aliases = list(scratch_shapes), list(operands), dict(aliases or {})
    kernel_body = body
    if ride is not None:
        n_in, n_out, n_scr, r_in, r_out = len(in_specs), len(out_specs), len(scratch_shapes), len(ride.operands), len(ride.results)

        def kernel_body(*refs):
            out0, scr0 = n_in + r_in, n_in + r_in + n_out + r_out
            ride_refs = (refs[n_in:out0], refs[out0 + n_out:scr0], refs[scr0 + n_scr:])
            ids = [pl.program_id(i) for i in range(len(grid))]
            first = functools.reduce(jnp.logical_and, [i == 0 for i in ids])
            last = functools.reduce(jnp.logical_and, [i == g - 1 for i, g in zip(ids, grid)])

            @pl.when(first)
            def _():
                ride.start(*ride_refs)

            body(*refs[:n_in], *refs[out0:out0 + n_out], *refs[scr0:scr0 + n_scr])

            @pl.when(last)
            def _():
                ride.wait(*ride_refs)

        aliases.update({n_in + i: n_out + j for i, j in ride.aliases.items()})
        in_specs += [ANY] * r_in
        out_specs += [ANY] * r_out
        out_shape += ride.results
        scratch_shapes += ride.sems
        operands += ride.operands
    return _pallas_call(
        kernel_body, name=name, grid=grid, in_specs=in_specs, out_specs=out_specs, out_shape=out_shape,
        scratch_shapes=scratch_shapes, input_output_aliases=aliases,
        compiler_params=_cparams(dimension_semantics=("arbitrary",) * len(grid)),
    )(*operands)


def _gather_first_level(shards):
    n = len(shards)

    def copies(ins, outs, sems, landed):
        send_sems, recv_sems, local_sems = sems
        x, y, c = _place()
        peers = [(x, y, 1 - c)] + [(px, py, c) for px, py in _other_chips(x, y)]

        def row(peer):
            return 4 * x + 2 * y + c if not landed else 4 * peer[0] + 2 * peer[1] + peer[2]

        local = [pltpu.make_async_copy(ins[a], outs[a].at[4 * x + 2 * y + c], local_sems.at[a]) for a in range(n)]
        remote = [pltpu.make_async_remote_copy(
            src_ref=ins[a], dst_ref=outs[a].at[row(peer)], send_sem=send_sems.at[a, k], recv_sem=recv_sems.at[a, k],
            device_id=peer, device_id_type=MESH) for a in range(n) for k, peer in enumerate(peers)]
        return local, remote

    def start(ins, outs, sems):
        local, remote = copies(ins, outs, sems, False)
        for cp in local + remote:
            cp.start()

    def wait(ins, outs, sems):
        local, sent = copies(ins, outs, sems, False)
        for cp in copies(ins, outs, sems, True)[1]:
            cp.wait_recv()
        for cp in sent:
            cp.wait_send()
        for cp in local:
            cp.wait()

    return _Ride(shards, [_sds((N_DEV,) + s.shape, s.dtype) for s in shards], {},
                 [pltpu.SemaphoreType.DMA((n, 4)), pltpu.SemaphoreType.DMA((n, 4)), pltpu.SemaphoreType.DMA((n,))], start, wait)


def _gather_second_level(buffers):
    n = len(buffers)

    def copies(outs, sems, core):
        send_sems, recv_sems = sems
        x, y, c = _place()
        return [pltpu.make_async_remote_copy(
            src_ref=outs[a].at[4 * px + 2 * py + core], dst_ref=outs[a].at[4 * px + 2 * py + core], send_sem=send_sems.at[a, j],
            recv_sem=recv_sems.at[a, j], device_id=(x, y, 1 - c), device_id_type=MESH)
            for a in range(n) for j, (px, py) in enumerate(_other_chips(x, y))]

    def start(ins, outs, sems):
        for cp in copies(outs, sems, lax.axis_index("c")):
            cp.start()

    def wait(ins, outs, sems):
        for cp in copies(outs, sems, 1 - lax.axis_index("c")):
            cp.wait_recv()
        for cp in copies(outs, sems, lax.axis_index("c")):
            cp.wait_send()

    return _Ride(buffers, [_sds(b.shape, b.dtype) for b in buffers], {i: i for i in range(n)},
                 [pltpu.SemaphoreType.DMA((n, 3)), pltpu.SemaphoreType.DMA((n, 3))], start, wait)


def _relayed_gather(shards):
    n = len(shards)
    buffers = [_sds((N_DEV,) + s.shape, s.dtype) for s in shards]
    dma = pltpu.SemaphoreType.DMA

    def remote(src, dst, send_sem, recv_sem, to):
        return pltpu.make_async_remote_copy(src_ref=src, dst_ref=dst, send_sem=send_sem, recv_sem=recv_sem,
                                            device_id=to, device_id_type=MESH)

    def row(px, py, pc):
        return 4 * px + 2 * py + pc

    def ride(operands, aliases, sems, copies):
        def start(ins, outs, sem_refs):
            local, sent = copies(ins, outs, sem_refs, False)
            for cp in local + sent:
                cp.start()

        def wait(ins, outs, sem_refs):
            local, sent = copies(ins, outs, sem_refs, False)
            for cp in copies(ins, outs, sem_refs, True)[1]:
                cp.wait_recv()
            for cp in sent:
                cp.wait_send()
            for cp in local:
                cp.wait()

        return _Ride(operands, buffers, aliases, sems, start, wait)

    def first(ins, outs, sems, landed):
        x, y, c = _place()
        peers = [(x, y, 1 - c), (1 - x, y, c), (x, 1 - y, c)]
        local = [pltpu.make_async_copy(ins[a], outs[a].at[row(x, y, c)], sems[2].at[a]) for a in range(n)]
        return local, [remote(ins[a], outs[a].at[row(*peer) if landed else row(x, y, c)], sems[0].at[a, k], sems[1].at[a, k], peer)
                       for a in range(n) for k, peer in enumerate(peers)]

    def second(ins, outs, sems, landed):
        x, y, c = _place()
        mine = 1 - c if landed else c
        copies = []
        for a in range(n):
            half = shards[a].shape[0] // 2
            over_x, over_y, diagonal = outs[a].at[row(1 - x, y, mine)], outs[a].at[row(x, 1 - y, mine)], outs[a].at[row(1 - x, 1 - y, c)]
            lower, upper = pl.ds(0, half), pl.ds(half, half)
            copies += [remote(over_x, over_x, sems[0].at[a, 0], sems[1].at[a, 0], (x, y, 1 - c)),
                       remote(over_y, over_y, sems[0].at[a, 1], sems[1].at[a, 1], (x, y, 1 - c))]
            if landed:
                copies += [remote(diagonal.at[lower], diagonal.at[lower], sems[0].at[a, 2], sems[1].at[a, 2], (1 - x, y, c)),
                           remote(diagonal.at[upper], diagonal.at[upper], sems[0].at[a, 3], sems[1].at[a, 3], (x, 1 - y, c))]
            else:
                copies += [remote(over_y.at[lower], over_y.at[lower], sems[0].at[a, 2], sems[1].at[a, 2], (1 - x, y, c)),
                           remote(over_x.at[upper], over_x.at[upper], sems[0].at[a, 3], sems[1].at[a, 3], (x, 1 - y, c))]
        return [], copies

    def third(ins, outs, sems, landed):
        x, y, c = _place()
        return [], [remote(outs[a].at[row(1 - x, 1 - y, 1 - c if landed else c)], outs[a].at[row(1 - x, 1 - y, 1 - c if landed else c)],
                           sems[0].at[a], sems[1].at[a], (x, y, 1 - c)) for a in range(n)]

    def later(copies, n_sems):
        return lambda partly: ride(partly, {i: i for i in range(n)}, [dma((n,) + n_sems), dma((n,) + n_sems)], copies)

    return ride(shards, {}, [dma((n, 3)), dma((n, 3)), dma((n,))], first), later(second, (4,)), later(third, ())


def _sibling_swap_ride(grads):
    n = len(grads)

    def copies(ins, outs, sems):
        x, y, c = _place()
        return [pltpu.make_async_remote_copy(
            src_ref=ins[a].at[1 - c], dst_ref=outs[a], send_sem=sems[0].at[a], recv_sem=sems[1].at[a],
            device_id=(x, y, 1 - c), device_id_type=MESH) for a in range(n)]

    def start(ins, outs, sems):
        for cp in copies(ins, outs, sems):
            cp.start()

    def wait(ins, outs, sems):
        for cp in copies(ins, outs, sems):
            cp.wait()

    return _Ride(grads, [_sds(g.shape[1:], g.dtype) for g in grads], {},
                 [pltpu.SemaphoreType.DMA((n,)), pltpu.SemaphoreType.DMA((n,))], start, wait)


def _chip_swap_ride(sums):
    n = len(sums)

    def copies(ins, outs, sems, landed):
        send_sems, recv_sems, local_sems = sems
        x, y, c = _place()
        mine = 2 * x + y
        local = [pltpu.make_async_copy(ins[a].at[mine], outs[a].at[mine], local_sems.at[a]) for a in range(n)]
        remote = [pltpu.make_async_remote_copy(
            src_ref=ins[a].at[2 * px + py], dst_ref=outs[a].at[2 * px + py if landed else mine], send_sem=send_sems.at[a, j],
            recv_sem=recv_sems.at[a, j], device_id=(px, py, c), device_id_type=MESH)
            for a in range(n) for j, (px, py) in enumerate(_other_chips(x, y))]
        return local, remote

    def start(ins, outs, sems):
        local, remote = copies(ins, outs, sems, False)
        for cp in local + remote:
            cp.start()

    def wait(ins, outs, sems):
        local, sent = copies(ins, outs, sems, False)
        for cp in copies(ins, outs, sems, True)[1]:
            cp.wait_recv()
        for cp in sent:
            cp.wait_send()
        for cp in local:
            cp.wait()

    return _Ride(sums, [_sds(s.shape, s.dtype) for s in sums], {},
                 [pltpu.SemaphoreType.DMA((n, 3)), pltpu.SemaphoreType.DMA((n, 3)), pltpu.SemaphoreType.DMA((n,))], start, wait)


def _send_buffers(shards, name):
    n = len(shards)

    def body(*refs):
        for (w, transposed, rows, cols), w_ref, o_ref in zip(shards, refs[:n], refs[n:]):
            if transposed:
                c, r = w.shape
                padded = jnp.concatenate([w_ref[...], jnp.zeros((cols - c, r), F32)], axis=0) if cols > c else w_ref[...]
                o_ref[...] = padded.T.astype(BF16)
            else:
                r, c = w.shape
                if (r, c) != (rows, cols):
                    o_ref[...] = jnp.zeros((rows, cols), BF16)
                o_ref[:r, :c] = w_ref[...].astype(BF16)

    return _pallas_call(body, name=name, out_shape=[_sds((rows, cols), BF16) for _, _, rows, cols in shards])(
        *[w for w, _, _, _ in shards])


def _all_gather(shards, name):
    n = len(shards)
    first, second, third = _relayed_gather(shards)
    levels = [first, second(shards), third(shards)]
    counts = [len(level.sems) for level in levels]

    def body(*refs):
        ins, outs, sems = refs[:n], refs[n:2 * n], refs[2 * n:]
        for i, level in enumerate(levels):
            mine = sems[sum(counts[:i]):sum(counts[:i + 1])]
            level.start(ins, outs, mine)
            level.wait(ins, outs, mine)

    return _pallas_call(
        body, name=name, in_specs=[ANY] * n, out_specs=[ANY] * n, out_shape=first.results,
        scratch_shapes=[s for level in levels for s in level.sems],
    )(*shards)


def _swap_with_sibling(grads, name):
    n = len(grads)

    def body(*refs):
        ins, outs = refs[:n], refs[n:2 * n]
        send_sems, recv_sems = refs[2 * n:]
        x, y, c = _place()
        copies = [pltpu.make_async_remote_copy(
            src_ref=ins[a].at[1 - c], dst_ref=outs[a], send_sem=send_sems.at[a], recv_sem=recv_sems.at[a],
            device_id=(x, y, 1 - c), device_id_type=MESH) for a in range(n)]
        for cp in copies:
            cp.start()
        for cp in copies:
            cp.wait()

    return _pallas_call(
        body, name=name, in_specs=[ANY] * n, out_specs=[ANY] * n,
        out_shape=[_sds(g.shape[1:], g.dtype) for g in grads],
        scratch_shapes=[pltpu.SemaphoreType.DMA((n,)), pltpu.SemaphoreType.DMA((n,))],
    )(*grads)


def _pair_sums(gs, rs, core, name):
    n_arrays = len(gs)

    def body(core_ref, *refs):
        for g_ref, r_ref, o_ref in zip(refs[:n_arrays], refs[n_arrays:2 * n_arrays], refs[2 * n_arrays:]):
            o_ref[...] = (g_ref[...].astype(F32) + r_ref[...].astype(F32)).astype(o_ref.dtype)

    def own(g):
        return pl.BlockSpec((None, None) + g.shape[2:], lambda p, core_ref: (core_ref[0], p, 0, 0))

    def chip(g):
        return pl.BlockSpec((None,) + g.shape[2:], lambda p, core_ref: (p, 0, 0))

    return _pallas_call(
        body, name=name,
        grid_spec=pltpu.PrefetchScalarGridSpec(
            num_scalar_prefetch=1, grid=(4,), in_specs=[own(g) for g in gs] + [chip(g) for g in gs],
            out_specs=[chip(g) for g in gs]),
        out_shape=[_sds(g.shape[1:], g.dtype) for g in gs], compiler_params=_cparams(dimension_semantics=("arbitrary",)),
    )(core, *gs, *rs)


def _adamw_math(w, g, m, v):
    m = ADAM_B1 * m + (1.0 - ADAM_B1) * g
    v = ADAM_B2 * v + (1.0 - ADAM_B2) * (g * g)
    m_hat = m / (1.0 - ADAM_B1 ** ADAM_STEP)
    v_hat = v / (1.0 - ADAM_B2 ** ADAM_STEP)
    return -ADAM_LR * (m_hat / (jnp.sqrt(v_hat) + ADAM_EPS) + ADAM_WD * w), m, v


def _adamw(w, m, v, parts, name):
    r, c = w.shape
    tr = r if r <= 512 else 256
    n_parts, pr, pc = parts.shape
    assert r % tr == 0 and (tr == r or pr == r)

    def body(w_ref, m_ref, v_ref, p_ref, g_out, d_out, m_out, v_out):
        g = p_ref[0, :tr, :c].astype(F32)
        for p in range(1, n_parts):
            g = g + p_ref[p, :tr, :c].astype(F32)
        g_out[...] = g
        d_out[...], m_out[...], v_out[...] = _adamw_math(w_ref[...], g, m_ref[...], v_ref[...])

    tile = pl.BlockSpec((tr, c), lambda i: (i, 0))
    part_tile = pl.BlockSpec((n_parts, pr if tr == r else tr, pc), lambda i: (0, i, 0))
    out = _sds((r, c), F32)
    return _pallas_call(
        body, name=name, grid=(r // tr,), in_specs=[tile, tile, tile, part_tile], out_specs=(tile,) * 4,
        out_shape=(out,) * 4, compiler_params=_cparams(dimension_semantics=("arbitrary",)),
    )(w, m, v, parts)


def _adamw_transposed(w_t, m_t, v_t, parts, name):
    c, r = w_t.shape
    tr = 256
    n_parts, _, pc = parts.shape

    def body(w_ref, m_ref, v_ref, p_ref, g_out, d_out, m_out, v_out):
        g = p_ref[0].astype(F32)
        for p in range(1, n_parts):
            g = g + p_ref[p].astype(F32)
        g = g.T[:c]
        g_out[...] = g
        d_out[...], m_out[...], v_out[...] = _adamw_math(w_ref[...], g, m_ref[...], v_ref[...])

    tile = pl.BlockSpec((c, tr), lambda i: (0, i))
    out = _sds((c, r), F32)
    return _pallas_call(
        body, name=name, grid=(r // tr,), in_specs=[tile, tile, tile, pl.BlockSpec((n_parts, tr, pc), lambda i: (0, i, 0))],
        out_specs=(tile,) * 4, out_shape=(out,) * 4, compiler_params=_cparams(dimension_semantics=("arbitrary",)),
    )(w_t, m_t, v_t, parts)


SMALL = ("ssm_a_re", "ssm_a_im", "ssm_log_dt", "ssm_b_re", "ssm_b_im", "ssm_c_re", "ssm_c_im", "ssm_d",
         "ln1_g", "ln1_b", "ln2_g", "ln2_b")


def _pack_rows(arrays):
    rows = []
    for a in arrays:
        flat = a.reshape(-1)
        rows.append(jnp.pad(flat, (0, -flat.shape[0] % 128)).reshape(-1, 128))
    packed = jnp.concatenate(rows, axis=0)
    return jnp.pad(packed, ((0, -packed.shape[0] % 8), (0, 0)))


def _unpack_rows(packed, shapes):
    out, row = [], 0
    for shape in shapes:
        size = math.prod(shape)
        n_rows = -(-size // 128)
        out.append(packed[row:row + n_rows].reshape(-1)[:size].reshape(shape))
        row += n_rows
    return out


def _sum_devices(parts):
    def body(p_ref, o_ref):
        total = p_ref[0]
        for dev in range(1, N_DEV):
            total = total + p_ref[dev]
        o_ref[...] = total

    return _pallas_call(body, name="sum_devices", out_shape=_sds(parts.shape[1:], F32))(parts)


def _adamw_replicated(ws, ms, vs, gs):
    n = len(ws)

    def body(*refs):
        w_refs, m_refs, v_refs, g_refs, d_out, m_out, v_out = (refs[i * n:(i + 1) * n] for i in range(7))
        for i in range(n):
            d_out[i][...], m_out[i][...], v_out[i][...] = _adamw_math(w_refs[i][...], g_refs[i][...], m_refs[i][...], v_refs[i][...])

    out = _pallas_call(body, name="adamw_replicated", out_shape=[_sds(w.shape, F32) for w in ws] * 3,
                       compiler_params=_cparams())(*ws, *ms, *vs, *gs)
    return out[:n], out[n:2 * n], out[2 * n:]


def kernel(x, w_in, b_gate, w_attn_br, w_ssm_br, w_out, ssm_a_re, ssm_a_im, ssm_log_dt, ssm_b_re, ssm_b_im, ssm_c_re, ssm_c_im, ssm_d, w_glu, ln1_g, ln1_b, w_ff_gate, w_ff_up, w_ff_down, ln2_g, ln2_b, loss_target, m_w_in, m_b_gate, m_w_attn_br, m_w_ssm_br, m_w_out, m_ssm_a_re, m_ssm_a_im, m_ssm_log_dt, m_ssm_b_re, m_ssm_b_im, m_ssm_c_re, m_ssm_c_im, m_ssm_d, m_w_glu, m_ln1_g, m_ln1_b, m_w_ff_gate, m_w_ff_up, m_w_ff_down, m_ln2_g, m_ln2_b, v_w_in, v_b_gate, v_w_attn_br, v_w_ssm_br, v_w_out, v_ssm_a_re, v_ssm_a_im, v_ssm_log_dt, v_ssm_b_re, v_ssm_b_im, v_ssm_c_re, v_ssm_c_im, v_ssm_d, v_w_glu, v_ln1_g, v_ln1_b, v_w_ff_gate, v_w_ff_up, v_w_ff_down, v_ln2_g, v_ln2_b):
    given = dict(locals())
    x2, target = x[0], loss_target[0]
    core = lax.axis_index("c").astype(jnp.int32).reshape(1)

    sharded = ("w_in", "w_attn_br", "w_ssm_br", "w_glu", "w_ff_gate", "w_ff_up", "b_gate", "w_out", "w_ff_down")
    send_shape = dict(w_in=(D_MODEL, 896), w_attn_br=(ATTN_WIDTH, 128), w_ssm_br=(SSM_WIDTH, 128), w_glu=(SSM_WIDTH, 128),
                      w_out=(128, D_MODEL), w_ff_gate=(D_MODEL, FF_PAD), w_ff_up=(D_MODEL, FF_PAD), w_ff_down=(FF_PAD, D_MODEL))
    local = {k: given[k][0] for k in sharded}
    narrow = ("w_ff_gate", "w_ff_up")
    def to_send(k):
        return (local[k].T, True, *send_shape[k]) if k in narrow else (local[k], False, *send_shape[k])

    later = [k for k in sharded if k not in ("w_in", "b_gate")]
    sends = dict(zip(["w_in"] + later, _send_buffers([to_send("w_in")], "send_w_in")
                     + _send_buffers([to_send(k) for k in later], "send_weights")))
    sends["b_gate"] = local["b_gate"]
    mixer_weights = ("w_attn_br", "w_ssm_br", "w_glu", "b_gate", "w_out")
    ff_weights = ("w_ff_gate", "w_ff_up", "w_ff_down")
    wt = {}
    wt["w_in"], = _all_gather([sends["w_in"]], "gather_w_in")

    a_re, a_im, log_dt = ssm_a_re[0], ssm_a_im[0], ssm_log_dt[0].reshape(SSM_GROUPS, 1)
    b_re_t, b_im_t = ssm_b_re[0].transpose(0, 2, 1), ssm_b_im[0].transpose(0, 2, 1)
    abar_re, abar_im, e_re, e_im, bbar_re_t, bbar_im_t = _ssm_prep(a_re, a_im, log_dt, b_re_t, b_im_t)
    bmat, cmat, a_chunks = _ssm_tables(abar_re, abar_im, bbar_re_t, bbar_im_t, ssm_c_re[0], ssm_c_im[0])
    cos_t, sin_t = _rope_tables()

    big_mixer, ff_in = [k for k in mixer_weights if k != "b_gate"], ("w_ff_gate", "w_ff_up")
    n_mixer = len(big_mixer)
    mixer_1, mixer_2, mixer_3 = _relayed_gather([sends[k] for k in big_mixer])
    ff_in_1, ff_in_2, ff_in_3 = _relayed_gather([sends[k] for k in ff_in])
    ff_down_1, ff_down_2, ff_down_3 = _relayed_gather([sends["w_ff_down"]])
    proj, *landed = _proj(x2, wt["w_in"], mixer_1 + _gather_first_level([sends["b_gate"]]))
    mixer, bias = landed[:n_mixer], landed[n_mixer:]
    attn, lse, *landed = _attn_fwd(proj, cos_t, sin_t, mixer_2(mixer) + _gather_second_level(bias) + ff_in_1)
    mixer, b_gate_full, ff = landed[:n_mixer], landed[n_mixer], landed[n_mixer + 1:]
    ys, states, *landed = _ssm_fwd(proj, bmat, cmat, a_chunks, ssm_d, mixer_3(mixer) + ff_in_2(ff) + ff_down_1)
    wt.update(zip(big_mixer, landed[:n_mixer]))
    ff, ff_down = landed[n_mixer:n_mixer + 2], landed[n_mixer + 2:]
    wt["w_out"] = wt["w_out"].reshape(D_MODEL, D_MODEL)
    h, xhat1, rstd1, glu, y_attn, y_ssm, *landed = _mixer_out(
        attn, ys, proj, x2, wt["w_attn_br"], wt["w_ssm_br"], wt["w_glu"], wt["w_out"], b_gate_full, ln1_g, ln1_b,
        ff_in_3(ff) + ff_down_2(ff_down))
    wt.update(zip(ff_in, landed[:2]))
    ff_a, ff_b, ff_f, w_ff_down = _ff_up(h, wt["w_ff_gate"], wt["w_ff_up"], ff_down_3(landed[2:]))
    wt["w_ff_down"] = w_ff_down.reshape(D_FF_PAD, D_MODEL)
    dr2, d_ln2_g, d_ln2_b, loss_lanes = _ff_down_loss(ff_f, wt["w_ff_down"], h, target, ln2_g, ln2_b)

    def pair_sums(names, contrib, from_sibling):
        return _pair_sums([contrib[k] for k in names], from_sibling, core, "pair_sums_" + names[0])

    d_a, d_b = _ff_down_bwd(dr2, wt["w_ff_down"], ff_a, ff_b)
    contrib = dict(w_ff_gate=_weight_grad(h, d_a, "wgrad_w_ff_gate", FF_PAD),
                   w_ff_up=_weight_grad(h, d_b, "wgrad_w_ff_up", FF_PAD),
                   w_ff_down=_weight_grad(ff_f, dr2, "wgrad_w_ff_down"))
    dr1, d_ln1_g, d_ln1_b, *from_sibling = _ff_up_bwd(
        d_a, d_b, wt["w_ff_gate"], wt["w_ff_up"], dr2, xhat1, rstd1, ln1_g, _sibling_swap_ride([contrib[k] for k in ff_weights]))
    ff_sums = pair_sums(ff_weights, contrib, from_sibling)

    d_ya, d_yssm, d_proj, d_attn, d_glu, d_ys, mixed, y_s, gy, d_bg = _mixer_bwd(
        dr1, proj, y_attn, y_ssm, glu, ys, wt["w_attn_br"], wt["w_ssm_br"], wt["w_glu"], wt["w_out"], b_gate_full)
    contrib.update(w_attn_br=_weight_grad(attn, d_ya, "wgrad_w_attn_br", 128),
                   w_ssm_br=_weight_grad(y_s, d_yssm, "wgrad_w_ssm_br", 128),
                   w_glu=_weight_grad(gy, d_glu, "wgrad_w_glu", 128),
                   w_out=_weight_grad(mixed, dr1, "wgrad_w_out"),
                   b_gate=d_bg.reshape(2, 4, 2, 128).transpose(2, 1, 0, 3))
    d_proj, *landed = _attn_bwd(proj, cos_t, sin_t, attn, lse, d_attn, d_proj,
                                _chip_swap_ride(ff_sums) + _sibling_swap_ride([contrib[k] for k in mixer_weights]))
    parts = dict(zip(ff_weights, landed[:len(ff_weights)]))
    mixer_sums = pair_sums(mixer_weights, contrib, landed[len(ff_weights):])
    d_proj, d_bmat, d_cmat, d_abar, d_skip, *landed = _ssm_bwd(d_ys, proj, states, bmat, cmat, a_chunks, ssm_d, d_proj,
                                                               _chip_swap_ride(mixer_sums))
    parts.update(zip(mixer_weights, landed))

    gbb_re_t, gbb_im_t = _block_diag_parts(d_bmat, True)
    gc_re, gc_im = _block_diag_parts(d_cmat, False)
    ga_re = d_abar[:, 0, :CHUNK_STATES].reshape(SSM_GROUPS, SSM_STATE)
    ga_im = d_abar[:, 0, CHUNK_STATES:].reshape(SSM_GROUPS, SSM_STATE)
    g_a_re, g_a_im, g_log_dt, g_b_re_t, g_b_im_t = _ssm_param_bwd(
        a_re, a_im, log_dt, b_re_t, b_im_t, abar_re, abar_im, e_re, e_im, ga_re, ga_im, gbb_re_t, gbb_im_t)
    mine = [g_a_re, g_a_im, g_log_dt, g_b_re_t, g_b_im_t, gc_re, -gc_im,
            d_skip, d_ln1_g, d_ln1_b, d_ln2_g, d_ln2_b]
    small_packed = _pack_rows(mine + [loss_lanes])

    contrib["w_in"], small_partly = _weight_grad(x2, d_proj, "wgrad_w_in", 896, _gather_first_level([small_packed]))
    w_in_sum = pair_sums(["w_in"], contrib, _swap_with_sibling([contrib["w_in"]], "swap_w_in_with_sibling"))
    grad_x, parts["w_in"], every = _grad_x(d_proj, wt["w_in"], dr1,
                                          _chip_swap_ride(w_in_sum) + _gather_second_level([small_partly]))

    grads, deltas, new_m, new_v = {}, {}, {}, {}
    for k in sharded:
        w2 = local[k]
        if k in narrow:
            out = _adamw_transposed(w2.T, given["m_" + k][0].T, given["v_" + k][0].T, parts[k], "adamw_" + k)
            out = [o.T for o in out]
        else:
            out = _adamw(w2, given["m_" + k][0], given["v_" + k][0], parts[k], "adamw_" + k)
        grads[k], deltas[k], new_m[k], new_v[k] = (o.reshape((1,) + w2.shape) for o in out)

    def held(k, a):
        return a.transpose(0, 1, 3, 2) if k in ("ssm_b_re", "ssm_b_im") else a

    *small_grads, loss_sum = _unpack_rows(_sum_devices(every), [held(k, given[k]).shape for k in SMALL] + [(1, 128)])
    small = _adamw_replicated([held(k, given[k]) for k in SMALL], [held(k, given["m_" + k]) for k in SMALL],
                              [held(k, given["v_" + k]) for k in SMALL], small_grads)
    for res, values in zip((grads, deltas, new_m, new_v), (small_grads,) + small):
        res.update((k, held(k, a)) for k, a in zip(SMALL, values))
    loss = loss_sum[0, 0]

    order = ("w_in", "b_gate", "w_attn_br", "w_ssm_br", "w_out", "ssm_a_re", "ssm_a_im", "ssm_log_dt", "ssm_b_re", "ssm_b_im",
             "ssm_c_re", "ssm_c_im", "ssm_d", "w_glu", "ln1_g", "ln1_b", "w_ff_gate", "w_ff_up", "w_ff_down", "ln2_g", "ln2_b")
    return (loss, grad_x[None], *[grads[k] for k in order], *[deltas[k] for k in order], *[new_m[k] for k in order],
            *[new_v[k] for k in order])
```

```python
import functools
import math

import jax
import jax.numpy as jnp
import numpy as np
from jax import lax
from jax.experimental import pallas as pl
from jax.experimental.pallas import tpu as pltpu

F32 = jnp.float32
BF16 = jnp.bfloat16

N_DEV = 8
SEQ = 2048
D_MODEL = 1024
HEAD_DIM = 64
ATTN_WIDTH = 512
QKV_WIDTH = 1536
SSM_WIDTH = 512
SSM_GROUPS = 32
SSM_GROUP = 16
SSM_STATE = 64
IN_WIDTH = 7168
D_FF = 2816
FF_SHARD = D_FF // N_DEV
FF_PAD = 384
D_FF_PAD = FF_PAD * N_DEV
DN_ALPHA = 2.0 ** 0.25
LN_EPS = 1e-5
NEG_INF = -1e30
ROPE_THETA = 10000.0
BLOCK = 128
GROUPS = ((1, 16), (4, 4), (16, 1))

ADAM_LR = 0.001
ADAM_B1 = 0.9
ADAM_B2 = 0.999
ADAM_EPS = 1e-08
ADAM_WD = 0.01
ADAM_STEP = 10

VMEM_LIMIT = 56 * 1024 * 1024


_pallas_call = pl.pallas_call


def _cparams(**kw):
    return pltpu.CompilerParams(vmem_limit_bytes=VMEM_LIMIT, **kw)


def _dot(a, b):
    return jnp.dot(a, b, preferred_element_type=F32)


def _dot_nt(a, b):
    return lax.dot_general(a, b, (((1,), (1,)), ((), ())), preferred_element_type=F32)


def _side_by_side(w_ref, row=None):
    rows = slice(None) if row is None else pl.ds(row, 1)
    return jnp.concatenate([w_ref[i, rows, :] for i in range(w_ref.shape[0])], axis=1)


def _dot_tn(a, b):
    return lax.dot_general(a, b, (((0,), (0,)), ((), ())), preferred_element_type=F32)


def _rope_tables():
    half = HEAD_DIM // 2
    inv_freq = np.float32(ROPE_THETA) ** (-np.arange(half, dtype=np.float32) / np.float32(half))
    ang = np.arange(SEQ, dtype=np.float32)[:, None] * inv_freq[None, :]
    cos, sin = np.cos(ang).astype(np.float32), np.sin(ang).astype(np.float32)
    tables = np.tile(cos, (1, 4)), np.tile(np.concatenate([-sin, sin], axis=1), (1, 2))

    def by_phase(t):
        return np.stack([t.reshape(SEQ // d, d, 128).transpose(1, 0, 2).reshape(SEQ, 128) for d, _ in GROUPS])

    return jnp.asarray(by_phase(tables[0])), jnp.asarray(by_phase(tables[1]))


def _swap_halves(x):
    lane = lax.broadcasted_iota(jnp.int32, x.shape, 1)
    return jnp.where((lane & 63) < 32, pltpu.roll(x, 96, axis=1), pltpu.roll(x, 32, axis=1))


def _group_rows(d, nb, r, i):
    src = pl.ds(i * BLOCK, BLOCK) if d == 1 else pl.ds(r + i * BLOCK * d, BLOCK, stride=d)
    return src, pl.ds((r * nb + i) * BLOCK, BLOCK)


def _attn_masks():
    a_idx = lax.broadcasted_iota(jnp.int32, (2 * BLOCK, 2 * BLOCK), 0) & (BLOCK - 1)
    c_idx = lax.broadcasted_iota(jnp.int32, (2 * BLOCK, 2 * BLOCK), 1)
    cur_ok = jnp.logical_and(c_idx >= BLOCK, c_idx - BLOCK <= a_idx)
    prev_ok = jnp.logical_and(c_idx < BLOCK, c_idx >= a_idx)
    lane = lax.broadcasted_iota(jnp.int32, (BLOCK, 128), 1)
    return cur_ok, prev_ok, lane < HEAD_DIM


def _stack_heads(t, head0):
    zero = jnp.zeros_like(t)
    return jnp.concatenate([jnp.where(head0, t, zero), jnp.where(head0, zero, t)], axis=0)


def _unstack_heads(t2, head0):
    return jnp.where(head0, t2[:BLOCK], t2[BLOCK:])


def _attn_fwd(proj, cos_t, sin_t, ride=None):
    def body(q0, q1, q2, k0, k1, k2, v0, v1, v2, cos_ref, sin_ref, attn_ref, lse_ref,
             qs, ks, vs, os_, ms, ls, acc, mnat, lnat):
        cur_ok, prev_ok, head0 = _attn_masks()
        ks[:BLOCK, :] = jnp.zeros((BLOCK, 128), BF16)
        vs[:BLOCK, :] = jnp.zeros((BLOCK, 128), BF16)
        for g, (d, nb) in enumerate(GROUPS):
            q_ref, k_ref, v_ref = (q0, q1, q2)[g], (k0, k1, k2)[g], (v0, v1, v2)[g]
            for r in range(d):
                for i in range(nb):
                    src, dst = _group_rows(d, nb, r, i)
                    below = pl.ds(dst.start + BLOCK, BLOCK)
                    c, s = cos_ref[g, dst, :], sin_ref[g, dst, :]
                    q = q_ref[src, :]
                    k = k_ref[src, :]
                    qs[dst, :] = ((q * c + _swap_halves(q) * s) * 0.125).astype(BF16)
                    ks[below, :] = (k * c + _swap_halves(k) * s).astype(BF16)
                    vs[below, :] = v_ref[src, :].astype(BF16)

            def block(b, carry, nb=nb):
                has_prev = (b & (nb - 1)) > 0
                cur = pl.ds(pl.multiple_of(b * BLOCK, BLOCK), BLOCK)
                window = pl.ds(pl.multiple_of(b * BLOCK, BLOCK), 2 * BLOCK)
                valid = jnp.logical_or(cur_ok, jnp.logical_and(prev_ok, has_prev))
                s = jnp.where(valid, _dot_nt(_stack_heads(qs[cur, :], head0), ks[window, :]), NEG_INF)
                m = jnp.max(s, axis=1, keepdims=True)
                p = jnp.exp(s - m)
                os_[cur, :] = _unstack_heads(_dot(p.astype(BF16), vs[window, :]), head0)
                ms[cur, :] = _unstack_heads(m, head0)
                ls[cur, :] = _unstack_heads(jnp.sum(p, axis=1, keepdims=True), head0)
                return carry

            lax.fori_loop(0, SEQ // BLOCK, block, 0, unroll=16)

            for r in range(d):
                for i in range(nb):
                    src, dst = _group_rows(d, nb, r, i)
                    if g == 0:
                        acc[src, :], mnat[src, :], lnat[src, :] = os_[dst, :], ms[dst, :], ls[dst, :]
                    else:
                        m_old, m_g = mnat[src, :], ms[dst, :]
                        m_new = jnp.maximum(m_old, m_g)
                        a_old, a_g = jnp.exp(m_old - m_new), jnp.exp(m_g - m_new)
                        acc[src, :] = a_old * acc[src, :] + a_g * os_[dst, :]
                        lnat[src, :] = a_old * lnat[src, :] + a_g * ls[dst, :]
                        mnat[src, :] = m_new
        for i in range(SEQ // BLOCK):
            rows = pl.ds(i * BLOCK, BLOCK)
            l = lnat[rows, :]
            attn_ref[rows, :] = acc[rows, :] / l
            lse_ref[rows, :] = mnat[rows, :] + jnp.log(l)

    def col(base):
        return pl.BlockSpec((SEQ, 128), lambda hp, base=base: (0, base + hp))

    in_specs = [col(g * 4) for g in range(3)] + [col(12 + g * 4) for g in range(3)] + [col(24 + g * 4) for g in range(3)]
    table = pl.BlockSpec((3, SEQ, 128), lambda hp: (0, 0, 0), pipeline_mode=pl.Buffered(1))
    out = pl.BlockSpec((SEQ, 128), lambda hp: (0, hp))
    return _call(
        body, "attn_fwd", (4,), in_specs + [table, table], [out, out],
        [_sds((SEQ, ATTN_WIDTH), F32), _sds((SEQ, ATTN_WIDTH), F32)],
        [pltpu.VMEM((SEQ, 128), BF16)] + [pltpu.VMEM((SEQ + BLOCK, 128), BF16)] * 2 + [pltpu.VMEM((SEQ, 128), F32)] * 6,
        [proj] * 9 + [cos_t, sin_t], ride)


def _attn_bwd_group_body(g):
    d, nb = GROUPS[g]

    def body(q_ref, k_ref, v_ref, cos_ref, sin_ref, lse_ref, dattn_ref, dsum_ref, dproj_ref,
             qs, ks, vs, dos, lss, dss, dqs, dks, dvs, stage, outs, sems):
        cur_ok, prev_ok, head0 = _attn_masks()
        ks[:BLOCK, :] = jnp.zeros((BLOCK, 128), BF16)
        vs[:BLOCK, :] = jnp.zeros((BLOCK, 128), BF16)
        dks[:BLOCK, :] = jnp.zeros((BLOCK, 128), F32)
        dvs[:BLOCK, :] = jnp.zeros((BLOCK, 128), F32)
        for r in range(d):
            for i in range(nb):
                src, dst = _group_rows(d, nb, r, i)
                below = pl.ds(dst.start + BLOCK, BLOCK)
                c, s = cos_ref[g, dst, :], sin_ref[g, dst, :]
                q = q_ref[src, :]
                k = k_ref[src, :]
                qs[dst, :] = ((q * c + _swap_halves(q) * s) * 0.125).astype(BF16)
                ks[below, :] = (k * c + _swap_halves(k) * s).astype(BF16)
                vs[below, :] = v_ref[src, :].astype(BF16)
                dos[dst, :] = dattn_ref[src, :].astype(BF16)
                dss[dst, :] = dsum_ref[src, :]
                lss[dst, :] = lse_ref[src, :]
                dks[below, :] = jnp.zeros((BLOCK, 128), F32)
                dvs[below, :] = jnp.zeros((BLOCK, 128), F32)

        def per_head_column(t):
            return jnp.concatenate([jnp.max(jnp.where(head0, t, NEG_INF), axis=1, keepdims=True),
                                    jnp.max(jnp.where(head0, NEG_INF, t), axis=1, keepdims=True)], axis=0)

        def block(b, carry):
            has_prev = (b & (nb - 1)) > 0
            cur = pl.ds(pl.multiple_of(b * BLOCK, BLOCK), BLOCK)
            window = pl.ds(pl.multiple_of(b * BLOCK, BLOCK), 2 * BLOCK)
            valid = jnp.logical_or(cur_ok, jnp.logical_and(prev_ok, has_prev))
            q2, do2 = _stack_heads(qs[cur, :], head0), _stack_heads(dos[cur, :], head0)
            kw, vw = ks[window, :], vs[window, :]
            s = jnp.where(valid, _dot_nt(q2, kw), NEG_INF)
            p = jnp.exp(s - per_head_column(lss[cur, :]))
            ds = (p * (_dot_nt(do2, vw) - per_head_column(dss[cur, :]))).astype(BF16)
            dvs[window, :] += _dot_tn(p.astype(BF16), do2)
            dks[window, :] += _dot_tn(ds, q2)
            dqs[cur, :] = _unstack_heads(_dot(ds, kw), head0)
            return carry

        lax.fori_loop(0, SEQ // BLOCK, block, 0, unroll=8)

        hp = pl.program_id(0)
        copies = []
        for kind in range(3):
            for r in range(d):
                for i in range(nb):
                    src, dst = _group_rows(d, nb, r, i)
                    below = pl.ds(dst.start + BLOCK, BLOCK)
                    if kind == 2:
                        stage[src, :] = dvs[below, :]
                    else:
                        c, s = cos_ref[g, dst, :], sin_ref[g, dst, :]
                        t = dqs[dst, :] * 0.125 if kind == 0 else dks[below, :]
                        stage[src, :] = t * c - _swap_halves(t) * s
            for i in range(SEQ // MM_ROWS):
                rows = pl.ds(i * MM_ROWS, MM_ROWS)
                outs[kind, rows, :] = stage[rows, :].astype(BF16)
            column = pl.multiple_of((kind * 12 + g * 4 + hp) * 128, 128)
            copies.append(pltpu.make_async_copy(outs.at[kind], dproj_ref.at[:, pl.ds(column, 128)], sems.at[kind]))
            copies[-1].start()
        for cp in copies:
            cp.wait()

    return body


def _attn_bwd(proj, cos_t, sin_t, attn, lse, dattn, dproj, ride=None):
    groups = [_attn_bwd_group_body(g) for g in range(3)]

    def body(q0, q1, q2, k0, k1, k2, v0, v1, v2, cos_ref, sin_ref, attn_ref, lse_ref, dattn_ref, dproj_in, dproj_ref,
             dsum, *scratch):
        del dproj_in
        head0 = _attn_masks()[2]
        for i in range(SEQ // BLOCK):
            rows = pl.ds(i * BLOCK, BLOCK)
            prod = dattn_ref[rows, :] * attn_ref[rows, :]
            d0 = jnp.sum(jnp.where(head0, prod, 0.0), axis=1, keepdims=True)
            d1 = jnp.sum(jnp.where(head0, 0.0, prod), axis=1, keepdims=True)
            dsum[rows, :] = jnp.where(head0, d0, d1)
        for g in range(3):
            groups[g]((q0, q1, q2)[g], (k0, k1, k2)[g], (v0, v1, v2)[g], cos_ref, sin_ref, lse_ref, dattn_ref, dsum,
                      dproj_ref, *scratch)

    def col(base):
        return pl.BlockSpec((SEQ, 128), lambda hp, base=base: (0, base + hp))

    table = pl.BlockSpec((3, SEQ, 128), lambda hp: (0, 0, 0), pipeline_mode=pl.Buffered(1))
    return _call(
        body, "attn_bwd", (4,),
        [col(g * 4) for g in range(3)] + [col(12 + g * 4) for g in range(3)] + [col(24 + g * 4) for g in range(3)]
        + [table, table, col(0), col(0), col(0), ANY],
        [ANY], [_sds((SEQ, IN_WIDTH), BF16)],
        [pltpu.VMEM((SEQ, 128), F32)]
        + [pltpu.VMEM((SEQ, 128), BF16)] + [pltpu.VMEM((SEQ + BLOCK, 128), BF16)] * 2 + [pltpu.VMEM((SEQ, 128), BF16)]
        + [pltpu.VMEM((SEQ, 128), F32)] * 3 + [pltpu.VMEM((SEQ + BLOCK, 128), F32)] * 2 + [pltpu.VMEM((SEQ, 128), F32)]
        + [pltpu.VMEM((3, SEQ, 128), BF16), pltpu.SemaphoreType.DMA((3,))],
        [proj] * 9 + [cos_t, sin_t, attn, lse, dattn, dproj], ride, aliases={14: 0})


SSM_CHUNKS = 4
CHUNK_STATES = 512
SCAN_ROWS = 8
U_COL = (3 * QKV_WIDTH) // 128


def _cmul(xr, xi, yr, yi):
    return xr * yr - xi * yi, xr * yi + xi * yr


def _ssm_prep(a_re, a_im, log_dt, b_re_t, b_im_t):
    def body(ar_ref, ai_ref, ldt_ref, br_ref, bi_ref, abr_ref, abi_ref, er_ref, ei_ref, bbr_ref, bbi_ref):
        ar, ai = ar_ref[...], ai_ref[...]
        dt = jnp.exp(ldt_ref[...])
        mag = jnp.exp(ar * dt)
        abr, abi = mag * jnp.cos(ai * dt), mag * jnp.sin(ai * dt)
        den = ar * ar + ai * ai
        nr, ni = abr - 1.0, abi
        er, ei = (nr * ar + ni * ai) / den, (ni * ar - nr * ai) / den
        abr_ref[...], abi_ref[...], er_ref[...], ei_ref[...] = abr, abi, er, ei
        er3, ei3 = er[:, None, :], ei[:, None, :]
        br, bi = br_ref[...], bi_ref[...]
        bbr_ref[...] = er3 * br - ei3 * bi
        bbi_ref[...] = er3 * bi + ei3 * br

    gp = jax.ShapeDtypeStruct(a_re.shape, F32)
    gb = jax.ShapeDtypeStruct(b_re_t.shape, F32)
    return _pallas_call(body, name="ssm_prep", out_shape=(gp, gp, gp, gp, gb, gb))(a_re, a_im, log_dt, b_re_t, b_im_t)


def _ssm_param_bwd(a_re, a_im, log_dt, b_re_t, b_im_t, abar_re, abar_im, e_re, e_im, ga_re, ga_im, gbb_re_t, gbb_im_t):
    def body(ar_ref, ai_ref, ldt_ref, br_ref, bi_ref, abr_ref, abi_ref, er_ref, ei_ref, gar_ref, gai_ref, gbr_ref, gbi_ref,
             o_ar, o_ai, o_ldt, o_br, o_bi):
        ar, ai = ar_ref[...], ai_ref[...]
        dt = jnp.exp(ldt_ref[...])
        er, ei = er_ref[...], ei_ref[...]
        br, bi, gbr, gbi = br_ref[...], bi_ref[...], gbr_ref[...], gbi_ref[...]
        er3, ei3 = er[:, None, :], ei[:, None, :]
        o_br[...] = er3 * gbr + ei3 * gbi
        o_bi[...] = er3 * gbi - ei3 * gbr
        ge_r = jnp.sum(br * gbr + bi * gbi, axis=1)
        ge_i = jnp.sum(br * gbi - bi * gbr, axis=1)
        den = ar * ar + ai * ai
        ilr, ili = ar / den, -ai / den
        t_r, t_i = _cmul(ilr, -ili, ge_r, ge_i)
        gab_r, gab_i = gar_ref[...] + t_r, gai_ref[...] + t_i
        gz_r, gz_i = _cmul(abr_ref[...], -abi_ref[...], gab_r, gab_i)
        el_r, el_i = _cmul(er, ei, ilr, ili)
        u_r, u_i = _cmul(el_r, -el_i, ge_r, ge_i)
        o_ar[...] = dt * gz_r - u_r
        o_ai[...] = dt * gz_i - u_i
        o_ldt[...] = jnp.sum(gz_r * ar + gz_i * ai, axis=1, keepdims=True) * dt

    gp = jax.ShapeDtypeStruct(a_re.shape, F32)
    gb = jax.ShapeDtypeStruct(b_re_t.shape, F32)
    return _pallas_call(body, name="ssm_param_bwd", out_shape=(gp, gp, jax.ShapeDtypeStruct(log_dt.shape, F32), gb, gb))(
        a_re, a_im, log_dt, b_re_t, b_im_t, abar_re, abar_im, e_re, e_im, ga_re, ga_im, gbb_re_t, gbb_im_t)


def _block_diag(blocks_re, blocks_im, sign_im, rows_are_channels):
    both = jnp.stack([blocks_re, sign_im * blocks_im]).reshape(2, SSM_CHUNKS, 8, SSM_GROUP, SSM_STATE)
    eye = jnp.eye(8, dtype=F32)
    if rows_are_channels:
        return jnp.einsum("rcghp,gk->cghrkp", both, eye).reshape(SSM_CHUNKS, 128, 2 * CHUNK_STATES)
    return jnp.einsum("rcghp,gk->crkpgh", both, eye).reshape(SSM_CHUNKS, 2 * CHUNK_STATES, 128)


def _block_diag_parts(mat, rows_are_channels):
    if rows_are_channels:
        six = mat.reshape(SSM_CHUNKS, 8, SSM_GROUP, 2, 8, SSM_STATE)
        parts = jnp.einsum("cghrgp->rcghp", six)
    else:
        six = mat.reshape(SSM_CHUNKS, 2, 8, SSM_STATE, 8, SSM_GROUP)
        parts = jnp.einsum("crgpgh->rcghp", six)
    parts = parts.reshape(2, SSM_GROUPS, SSM_GROUP, SSM_STATE)
    return parts[0], parts[1]


def _scan_consts(a_ref, conj, reverse):
    ar = jnp.broadcast_to(a_ref[:, :CHUNK_STATES], (SCAN_ROWS, CHUNK_STATES))
    ai = jnp.broadcast_to(a_ref[:, CHUNK_STATES:], (SCAN_ROWS, CHUNK_STATES))
    if conj:
        ai = -ai
    row = lax.broadcasted_iota(jnp.int32, (SCAN_ROWS, CHUNK_STATES), 0)
    if reverse:
        row = SCAN_ROWS - 1 - row
    zero = jnp.zeros_like(ar)
    steps = []
    pr, pi = ar, ai
    for shift in (1, 2, 4):
        keep = row >= shift
        steps.append((SCAN_ROWS - shift if reverse else shift, jnp.where(keep, pr, zero), jnp.where(keep, pi, zero)))
        pr, pi = _cmul(pr, pi, pr, pi)
    first = row == 0
    return steps, (jnp.where(first, ar, zero), jnp.where(first, ai, zero)), first


def _scan_tile(xr, xi, prev_r, prev_i, steps, carry_in, reverse):
    edge = SCAN_ROWS - 1 if reverse else 1
    cr, ci = pltpu.roll(prev_r, edge, axis=0), pltpu.roll(prev_i, edge, axis=0)
    xr, xi = xr + carry_in[0] * cr - carry_in[1] * ci, xi + carry_in[0] * ci + carry_in[1] * cr
    for shift, mr, mi in steps:
        sr, si = pltpu.roll(xr, shift, axis=0), pltpu.roll(xi, shift, axis=0)
        xr, xi = xr + mr * sr - mi * si, xi + mr * si + mi * sr
    return xr, xi


MM_ROWS = 256


def _ssm_fwd(proj, bmat, cmat, a_chunks, d_skip, ride=None):
    def body(u_ref, b_ref, c_ref, a_ref, d_ref, y_ref, h_ref):
        for i in range(SEQ // MM_ROWS):
            rows = pl.ds(i * MM_ROWS, MM_ROWS)
            h_ref[rows, :] = _dot(u_ref[rows, :].astype(BF16), b_ref[...])
        steps, carry_in, _ = _scan_consts(a_ref, conj=False, reverse=False)

        def tile(k, carry):
            rows = pl.ds(pl.multiple_of(k * SCAN_ROWS, SCAN_ROWS), SCAN_ROWS)
            xr, xi = _scan_tile(h_ref[rows, :CHUNK_STATES], h_ref[rows, CHUNK_STATES:], carry[0], carry[1], steps, carry_in, False)
            h_ref[rows, :CHUNK_STATES] = xr
            h_ref[rows, CHUNK_STATES:] = xi
            return xr, xi

        zero = jnp.zeros((SCAN_ROWS, CHUNK_STATES), F32)
        lax.fori_loop(0, SEQ // SCAN_ROWS, tile, (zero, zero), unroll=4)
        for i in range(SEQ // MM_ROWS):
            rows = pl.ds(i * MM_ROWS, MM_ROWS)
            y_ref[rows, :] = _dot(h_ref[rows, :].astype(BF16), c_ref[...]) + d_ref[...] * u_ref[rows, :]

    return _call(
        body, "ssm_fwd", (SSM_CHUNKS,),
        [pl.BlockSpec((SEQ, 128), lambda c: (0, U_COL + c)),
         pl.BlockSpec((None, 128, 2 * CHUNK_STATES), lambda c: (c, 0, 0)),
         pl.BlockSpec((None, 2 * CHUNK_STATES, 128), lambda c: (c, 0, 0)),
         pl.BlockSpec((None, 1, 2 * CHUNK_STATES), lambda c: (c, 0, 0)),
         pl.BlockSpec((1, 128), lambda c: (0, c))],
        [pl.BlockSpec((SEQ, 128), lambda c: (0, c)), pl.BlockSpec((SEQ, 2 * CHUNK_STATES), lambda c: (0, c))],
        [_sds((SEQ, SSM_WIDTH), F32), _sds((SEQ, SSM_CHUNKS * 2 * CHUNK_STATES), F32)], [],
        [proj, bmat, cmat, a_chunks, d_skip], ride)


def _ssm_bwd(dys, proj, h, bmat, cmat, a_chunks, d_skip, dproj, ride=None):
    def body(dy_ref, u_ref, h_ref, b_ref, c_ref, a_ref, d_ref, dproj_in, du_ref, db_ref, dc_ref, da_ref, dd_ref, g_ref):
        del dproj_in
        dsum = jnp.zeros((1, 128), F32)
        dcm = jnp.zeros((2 * CHUNK_STATES, 128), F32)
        for i in range(SEQ // MM_ROWS):
            rows = pl.ds(i * MM_ROWS, MM_ROWS)
            dy = dy_ref[rows, :]
            g_ref[rows, :] = _dot_nt(dy.astype(BF16), c_ref[...])
            dsum += jnp.sum(dy * u_ref[rows, :], axis=0, keepdims=True)
            dcm += _dot_tn(h_ref[rows, :].astype(BF16), dy.astype(BF16))
        dd_ref[...] = dsum
        dc_ref[...] = dcm
        steps, carry_in, _ = _scan_consts(a_ref, conj=True, reverse=True)
        first_row = lax.broadcasted_iota(jnp.int32, (SCAN_ROWS, CHUNK_STATES), 0) == 0
        n_tiles = SEQ // SCAN_ROWS

        def tile(j, carry):
            k = n_tiles - 1 - j
            rows = pl.ds(pl.multiple_of(k * SCAN_ROWS, SCAN_ROWS), SCAN_ROWS)
            before = pl.ds(pl.multiple_of(jnp.maximum(k - 1, 0) * SCAN_ROWS, SCAN_ROWS), SCAN_ROWS)
            gr, gi = _scan_tile(g_ref[rows, :CHUNK_STATES], g_ref[rows, CHUNK_STATES:], carry[0], carry[1], steps, carry_in, True)
            g_ref[rows, :CHUNK_STATES] = gr
            g_ref[rows, CHUNK_STATES:] = gi
            has_before = jnp.where(k > 0, 1.0, 0.0)
            hr = jnp.where(first_row, pltpu.roll(h_ref[before, :CHUNK_STATES], 1, axis=0) * has_before,
                           pltpu.roll(h_ref[rows, :CHUNK_STATES], 1, axis=0))
            hi = jnp.where(first_row, pltpu.roll(h_ref[before, CHUNK_STATES:], 1, axis=0) * has_before,
                           pltpu.roll(h_ref[rows, CHUNK_STATES:], 1, axis=0))
            return gr, gi, carry[2] + hr * gr + hi * gi, carry[3] + hr * gi - hi * gr

        zero = jnp.zeros((SCAN_ROWS, CHUNK_STATES), F32)
        _, _, sar, sai = lax.fori_loop(0, n_tiles, tile, (zero, zero, zero, zero), unroll=4)
        da_ref[:, :CHUNK_STATES] = jnp.sum(sar, axis=0, keepdims=True)
        da_ref[:, CHUNK_STATES:] = jnp.sum(sai, axis=0, keepdims=True)
        dbm = jnp.zeros((128, 2 * CHUNK_STATES), F32)
        for i in range(SEQ // MM_ROWS):
            rows = pl.ds(i * MM_ROWS, MM_ROWS)
            g = g_ref[rows, :].astype(BF16)
            du_ref[rows, :] = (_dot_nt(g, b_ref[...]) + d_ref[...] * dy_ref[rows, :]).astype(BF16)
            dbm += _dot_tn(u_ref[rows, :].astype(BF16), g)
        db_ref[...] = dbm

    chunk_col = pl.BlockSpec((SEQ, 128), lambda c: (0, c))
    return _call(
        body, "ssm_bwd", (SSM_CHUNKS,),
        [chunk_col,
         pl.BlockSpec((SEQ, 128), lambda c: (0, U_COL + c)),
         pl.BlockSpec((SEQ, 2 * CHUNK_STATES), lambda c: (0, c)),
         pl.BlockSpec((None, 128, 2 * CHUNK_STATES), lambda c: (c, 0, 0)),
         pl.BlockSpec((None, 2 * CHUNK_STATES, 128), lambda c: (c, 0, 0)),
         pl.BlockSpec((None, 1, 2 * CHUNK_STATES), lambda c: (c, 0, 0)),
         pl.BlockSpec((1, 128), lambda c: (0, c)), ANY],
        [pl.BlockSpec((SEQ, 128), lambda c: (0, U_COL + c)),
         pl.BlockSpec((None, 128, 2 * CHUNK_STATES), lambda c: (c, 0, 0)),
         pl.BlockSpec((None, 2 * CHUNK_STATES, 128), lambda c: (c, 0, 0)),
         pl.BlockSpec((None, 1, 2 * CHUNK_STATES), lambda c: (c, 0, 0)),
         pl.BlockSpec((1, 128), lambda c: (0, c))],
        [_sds((SEQ, IN_WIDTH), BF16), _sds((SSM_CHUNKS, 128, 2 * CHUNK_STATES), F32),
         _sds((SSM_CHUNKS, 2 * CHUNK_STATES, 128), F32), _sds((SSM_CHUNKS, 1, 2 * CHUNK_STATES), F32), _sds((1, SSM_WIDTH), F32)],
        [pltpu.VMEM((SEQ, 2 * CHUNK_STATES), F32)], [dys, proj, h, bmat, cmat, a_chunks, d_skip, dproj], ride, aliases={7: 0})


def _ssm_tables(abar_re, abar_im, bbar_re_t, bbar_im_t, c_re, c_im):
    bmat = _block_diag(bbar_re_t, bbar_im_t, 1.0, True).astype(BF16)
    cmat = _block_diag(c_re, c_im, -1.0, False).astype(BF16)
    a_chunks = jnp.concatenate([abar_re.reshape(SSM_CHUNKS, 1, CHUNK_STATES), abar_im.reshape(SSM_CHUNKS, 1, CHUNK_STATES)], axis=2)
    return bmat, cmat, a_chunks


GL_COL = (3 * QKV_WIDTH + SSM_WIDTH) // D_MODEL
GELU_C = math.sqrt(2.0 / math.pi)
GELU_A = 0.044715


def _sds(shape, dtype):
    return jax.ShapeDtypeStruct(shape, dtype)


def _gelu(x):
    t = jnp.tanh(GELU_C * (x + GELU_A * x * x * x))
    return 0.5 * x * (1.0 + t), t


def _gelu_grad(x, t):
    return 0.5 * (1.0 + t) + 0.5 * x * (1.0 - t * t) * GELU_C * (1.0 + 3.0 * GELU_A * x * x)


def _layer_norm(r, g, b):
    mu = jnp.mean(r, axis=-1, keepdims=True)
    xc = r - mu
    rstd = lax.rsqrt(jnp.mean(xc * xc, axis=-1, keepdims=True) + LN_EPS)
    xhat = xc * rstd
    return xhat * g + b, xhat, rstd


def _layer_norm_bwd(dy, xhat, rstd, g):
    dxhat = dy * g
    m1 = jnp.mean(dxhat, axis=-1, keepdims=True)
    m2 = jnp.mean(dxhat * xhat, axis=-1, keepdims=True)
    return rstd * (dxhat - m1 - xhat * m2)


def _proj(x, w_in, ride=None):
    tm, tn = 1024, 1792

    def body(x_ref, w_ref, o_ref):
        o_ref[...] = _dot(x_ref[...].astype(BF16), _side_by_side(w_ref))

    return _call(
        body, "proj", (SEQ // tm, IN_WIDTH // tn),
        [pl.BlockSpec((tm, D_MODEL), lambda i, j: (i, 0)), pl.BlockSpec((2, D_MODEL, tn // 2), lambda i, j: (j, 0, 0))],
        [pl.BlockSpec((tm, tn), lambda i, j: (i, j))], [_sds((SEQ, IN_WIDTH), F32)], [], [x, w_in], ride)


def _row_spec(tm, width, col=0):
    return pl.BlockSpec((tm, width), lambda i, col=col: (i, col))


def _full_spec(shape):
    return pl.BlockSpec(shape, lambda i: (0,) * len(shape))


def _weight_spec(shape):
    return pl.BlockSpec(shape, lambda i: (0,) * len(shape), pipeline_mode=pl.Buffered(1))


def _mixer_out(attn, ys, proj, x, w_ab, w_sb, w_glu, w_out, b_gate, ln_g, ln_b, ride=None):
    tm = 512

    def body(attn_ref, ys_ref, gl0_ref, gl1_ref, x_ref, wab_ref, wsb_ref, wglu_ref, wout_ref, bg_ref, g_ref, b_ref,
             h_ref, xhat_ref, rstd_ref, glu_ref, ya_ref, yssm_ref):
        gy, _ = _gelu(ys_ref[...])
        glu = _dot(gy.astype(BF16), _side_by_side(wglu_ref))
        glu_ref[...] = glu
        y_s = glu[:, :SSM_WIDTH] * jax.nn.sigmoid(glu[:, SSM_WIDTH:])
        y_ssm = _dot(y_s.astype(BF16), _side_by_side(wsb_ref))
        y_attn = _dot(attn_ref[...].astype(BF16), _side_by_side(wab_ref))
        ya_ref[...] = y_attn
        yssm_ref[...] = y_ssm
        g0 = jax.nn.sigmoid(gl0_ref[...] + _side_by_side(bg_ref, 0))
        g1 = jax.nn.sigmoid(gl1_ref[...] + _side_by_side(bg_ref, 1))
        mixed = g0 * y_attn + g1 * y_ssm
        r1 = DN_ALPHA * x_ref[...] + _dot(mixed.astype(BF16), wout_ref[...])
        h, xhat, rstd = _layer_norm(r1, g_ref[...], b_ref[...])
        h_ref[...] = h
        xhat_ref[...] = xhat
        rstd_ref[...] = jnp.broadcast_to(rstd, (tm, 128))

    wide = _sds((SEQ, D_MODEL), F32)
    return _call(
        body, "mixer_out", (SEQ // tm,),
        [_row_spec(tm, ATTN_WIDTH), _row_spec(tm, SSM_WIDTH), _row_spec(tm, D_MODEL, GL_COL), _row_spec(tm, D_MODEL, GL_COL + 1),
         _row_spec(tm, D_MODEL), _weight_spec((N_DEV, ATTN_WIDTH, 128)), _weight_spec((N_DEV, SSM_WIDTH, 128)),
         _weight_spec((N_DEV, SSM_WIDTH, 128)), _weight_spec((D_MODEL, D_MODEL)), _full_spec((N_DEV, 2, 128)),
         _full_spec((1, D_MODEL)), _full_spec((1, D_MODEL))],
        [_row_spec(tm, D_MODEL), _row_spec(tm, D_MODEL), _row_spec(tm, 128), _row_spec(tm, D_MODEL),
         _row_spec(tm, D_MODEL), _row_spec(tm, D_MODEL)],
        [wide, wide, _sds((SEQ, 128), F32), wide, wide, wide], [],
        [attn, ys, proj, proj, x, w_ab, w_sb, w_glu, w_out, b_gate, ln_g, ln_b], ride)


def _ff_up(h, w_gate, w_up, ride=None):
    tm, tn = 1024, 768

    def body(h_ref, wg_ref, wu_ref, a_ref, b_ref, f_ref):
        hb = h_ref[...].astype(BF16)
        a, b = _dot(hb, _side_by_side(wg_ref)), _dot(hb, _side_by_side(wu_ref))
        a_ref[...] = a.astype(BF16)
        b_ref[...] = b.astype(BF16)
        f_ref[...] = (a * jax.nn.sigmoid(a) * b).astype(BF16)

    tile = pl.BlockSpec((tm, tn), lambda i, j: (i, j))
    wtile = pl.BlockSpec((tn // FF_PAD, D_MODEL, FF_PAD), lambda i, j: (j, 0, 0))
    out = _sds((SEQ, D_FF_PAD), BF16)
    return _call(body, "ff_up", (SEQ // tm, D_FF_PAD // tn), [pl.BlockSpec((tm, D_MODEL), lambda i, j: (i, 0)), wtile, wtile],
                 [tile, tile, tile], [out, out, out], [], [h, w_gate, w_up], ride)


def _ff_down_loss(f, w_down, h, target, ln_g, ln_b):
    tm = 512

    def body(f_ref, w_ref, h_ref, t_ref, g_ref, b_ref, dr_ref, dg_ref, db_ref, loss_ref):
        @pl.when(pl.program_id(0) == 0)
        def _():
            dg_ref[...] = jnp.zeros_like(dg_ref)
            db_ref[...] = jnp.zeros_like(db_ref)
            loss_ref[...] = jnp.zeros_like(loss_ref)

        r2 = DN_ALPHA * h_ref[...] + _dot(f_ref[...], w_ref[...])
        g = g_ref[...]
        out, xhat, rstd = _layer_norm(r2, g, b_ref[...])
        err = out - t_ref[...]
        loss_ref[...] += 0.5 * jnp.sum(jnp.mean(err * err, axis=-1, keepdims=True), axis=0, keepdims=True)
        dout = err * (1.0 / D_MODEL)
        dg_ref[...] += jnp.sum(dout * xhat, axis=0, keepdims=True)
        db_ref[...] += jnp.sum(dout, axis=0, keepdims=True)
        dr_ref[...] = _layer_norm_bwd(dout, xhat, rstd, g)

    vec = _sds((1, D_MODEL), F32)
    return _pallas_call(
        body, name="ff_down_loss", grid=(SEQ // tm,),
        in_specs=[_row_spec(tm, D_FF_PAD), _weight_spec((D_FF_PAD, D_MODEL)), _row_spec(tm, D_MODEL), _row_spec(tm, D_MODEL),
                  _full_spec((1, D_MODEL)), _full_spec((1, D_MODEL))],
        out_specs=(_row_spec(tm, D_MODEL), _full_spec((1, D_MODEL)), _full_spec((1, D_MODEL)), _full_spec((1, 128))),
        out_shape=(_sds((SEQ, D_MODEL), F32), vec, vec, _sds((1, 128), F32)),
        compiler_params=_cparams(dimension_semantics=("arbitrary",)),
    )(f, w_down, h, target, ln_g, ln_b)


def _ff_down_bwd(dr2, w_down, a, b):
    tm, tn = 1024, 768

    def body(dr_ref, w_ref, a_ref, b_ref, da_ref, db_ref):
        df = _dot_nt(dr_ref[...].astype(BF16), w_ref[...])
        av, bv = a_ref[...].astype(F32), b_ref[...].astype(F32)
        sg = jax.nn.sigmoid(av)
        da_ref[...] = (df * bv * sg * (1.0 + av * (1.0 - sg))).astype(BF16)
        db_ref[...] = (df * av * sg).astype(BF16)

    tile = pl.BlockSpec((tm, tn), lambda i, j: (i, j))
    out = _sds((SEQ, D_FF_PAD), BF16)
    return _pallas_call(
        body, name="ff_down_bwd", grid=(SEQ // tm, D_FF_PAD // tn),
        in_specs=[pl.BlockSpec((tm, D_MODEL), lambda i, j: (i, 0)), pl.BlockSpec((tn, D_MODEL), lambda i, j: (j, 0)), tile, tile],
        out_specs=(tile, tile), out_shape=(out, out),
        compiler_params=_cparams(dimension_semantics=("arbitrary", "arbitrary")),
    )(dr2, w_down, a, b)


def _ff_up_bwd(da, db, w_gate, w_up, dr2, xhat1, rstd1, ln_g, ride=None):
    tm, tk = 1024, 768
    nk = D_FF_PAD // tk

    def body(da_ref, db_ref, wg_ref, wu_ref, dr2_ref, xhat_ref, rstd_ref, g_ref, dr1_ref, dg_ref, dbias_ref, acc):
        i, k = pl.program_id(0), pl.program_id(1)

        @pl.when(jnp.logical_and(i == 0, k == 0))
        def _():
            dg_ref[...] = jnp.zeros_like(dg_ref)
            dbias_ref[...] = jnp.zeros_like(dbias_ref)

        part = _dot_nt(da_ref[...], _side_by_side(wg_ref)) + _dot_nt(db_ref[...], _side_by_side(wu_ref))

        @pl.when(k == 0)
        def _():
            acc[...] = part

        @pl.when(k > 0)
        def _():
            acc[...] += part

        @pl.when(k == nk - 1)
        def _():
            dh = DN_ALPHA * dr2_ref[...] + acc[...]
            xhat = xhat_ref[...]
            dg_ref[...] += jnp.sum(dh * xhat, axis=0, keepdims=True)
            dbias_ref[...] += jnp.sum(dh, axis=0, keepdims=True)
            rstd = jnp.max(rstd_ref[...], axis=1, keepdims=True)
            dr1_ref[...] = _layer_norm_bwd(dh, xhat, rstd, g_ref[...])

    hid = pl.BlockSpec((tm, tk), lambda i, k: (i, k))
    wtile = pl.BlockSpec((tk // FF_PAD, D_MODEL, FF_PAD), lambda i, k: (k, 0, 0))
    row = pl.BlockSpec((tm, D_MODEL), lambda i, k: (i, 0))
    vec = pl.BlockSpec((1, D_MODEL), lambda i, k: (0, 0))
    return _call(
        body, "ff_up_bwd", (SEQ // tm, nk),
        [hid, hid, wtile, wtile, row, row, pl.BlockSpec((tm, 128), lambda i, k: (i, 0)), vec],
        [row, vec, vec], [_sds((SEQ, D_MODEL), F32), _sds((1, D_MODEL), F32), _sds((1, D_MODEL), F32)],
        [pltpu.VMEM((tm, D_MODEL), F32)], [da, db, w_gate, w_up, dr2, xhat1, rstd1, ln_g], ride)


def _mixer_bwd(dr1, proj, y_attn, y_ssm, glu, ys, w_ab, w_sb, w_glu, w_out, b_gate):
    tm = 256

    def body(dr1_ref, gl0_ref, gl1_ref, ya_ref, yssm_ref, glu_ref, ys_ref, wab_ref, wsb_ref, wglu_ref, wout_ref, bg_ref,
             dya_ref, dyssm_ref, dgl_ref, dattn_ref, dglu_ref, dys_ref, mixed_ref, ysb_ref, gy_ref, dbg_ref):
        @pl.when(pl.program_id(0) == 0)
        def _():
            dbg_ref[...] = jnp.zeros_like(dbg_ref)

        dmixed = _dot_nt(dr1_ref[...].astype(BF16), wout_ref[...])
        g0 = jax.nn.sigmoid(gl0_ref[...] + _side_by_side(bg_ref, 0))
        g1 = jax.nn.sigmoid(gl1_ref[...] + _side_by_side(bg_ref, 1))
        y_attn, y_ssm = ya_ref[...], yssm_ref[...]
        mixed_ref[...] = (g0 * y_attn + g1 * y_ssm).astype(BF16)
        dya = (dmixed * g0).astype(BF16)
        dyssm = (dmixed * g1).astype(BF16)
        dya_ref[...] = dya
        dyssm_ref[...] = dyssm
        dgl0 = dmixed * y_attn * g0 * (1.0 - g0)
        dgl1 = dmixed * y_ssm * g1 * (1.0 - g1)
        dgl_ref[:, :GL_COL * D_MODEL] = jnp.zeros((tm, GL_COL * D_MODEL), BF16)
        dgl_ref[:, GL_COL * D_MODEL:(GL_COL + 1) * D_MODEL] = dgl0.astype(BF16)
        dgl_ref[:, (GL_COL + 1) * D_MODEL:] = dgl1.astype(BF16)
        dbg_ref[:, :D_MODEL] += jnp.sum(dgl0, axis=0, keepdims=True)
        dbg_ref[:, D_MODEL:] += jnp.sum(dgl1, axis=0, keepdims=True)
        dattn_ref[...] = _dot_nt(dya, _side_by_side(wab_ref))
        dy_s = _dot_nt(dyssm, _side_by_side(wsb_ref))
        glu = glu_ref[...]
        glu1, sg = glu[:, :SSM_WIDTH], jax.nn.sigmoid(glu[:, SSM_WIDTH:])
        ysb_ref[...] = (glu1 * sg).astype(BF16)
        dglu1 = (dy_s * sg).astype(BF16)
        dglu2 = (dy_s * glu1 * sg * (1.0 - sg)).astype(BF16)
        dglu_ref[:, :SSM_WIDTH] = dglu1
        dglu_ref[:, SSM_WIDTH:] = dglu2
        dgy = _dot_nt(jnp.concatenate([dglu1, dglu2], axis=1), _side_by_side(wglu_ref))
        ys = ys_ref[...]
        gy, t = _gelu(ys)
        gy_ref[...] = gy.astype(BF16)
        dys_ref[...] = dgy * _gelu_grad(ys, t)

    wide_b, half_b = _sds((SEQ, D_MODEL), BF16), _sds((SEQ, SSM_WIDTH), BF16)
    half_f = _sds((SEQ, SSM_WIDTH), F32)
    return _pallas_call(
        body, name="mixer_bwd", grid=(SEQ // tm,),
        in_specs=[_row_spec(tm, D_MODEL), _row_spec(tm, D_MODEL, GL_COL), _row_spec(tm, D_MODEL, GL_COL + 1), _row_spec(tm, D_MODEL),
                  _row_spec(tm, D_MODEL), _row_spec(tm, D_MODEL), _row_spec(tm, SSM_WIDTH), _full_spec((N_DEV, ATTN_WIDTH, 128)),
                  _full_spec((N_DEV, SSM_WIDTH, 128)), _full_spec((N_DEV, SSM_WIDTH, 128)), _full_spec((D_MODEL, D_MODEL)),
                  _full_spec((N_DEV, 2, 128))],
        out_specs=(_row_spec(tm, D_MODEL), _row_spec(tm, D_MODEL), _row_spec(tm, IN_WIDTH), _row_spec(tm, ATTN_WIDTH),
                   _row_spec(tm, D_MODEL), _row_spec(tm, SSM_WIDTH), _row_spec(tm, D_MODEL), _row_spec(tm, SSM_WIDTH),
                   _row_spec(tm, SSM_WIDTH), _full_spec((1, 2 * D_MODEL))),
        out_shape=(wide_b, wide_b, _sds((SEQ, IN_WIDTH), BF16), half_f, wide_b, half_f, wide_b, half_b, half_b,
                   _sds((1, 2 * D_MODEL), F32)),
        compiler_params=_cparams(dimension_semantics=("arbitrary",)),
    )(dr1, proj, proj, y_attn, y_ssm, glu, ys, w_ab, w_sb, w_glu, w_out, b_gate)


def _grad_x(dproj, w_in, dr1, ride=None):
    tm, tk = 1024, 1792
    nk = IN_WIDTH // tk

    def body(dp_ref, w_ref, dr1_ref, o_ref, acc):
        k = pl.program_id(1)
        part = _dot_nt(dp_ref[...], _side_by_side(w_ref))

        @pl.when(k == 0)
        def _():
            acc[...] = part

        @pl.when(k > 0)
        def _():
            acc[...] += part

        @pl.when(k == nk - 1)
        def _():
            o_ref[...] = DN_ALPHA * dr1_ref[...] + acc[...]

    row = pl.BlockSpec((tm, D_MODEL), lambda i, k: (i, 0))
    return _call(
        body, "grad_x", (SEQ // tm, nk),
        [pl.BlockSpec((tm, tk), lambda i, k: (i, k)), pl.BlockSpec((2, D_MODEL, tk // 2), lambda i, k: (k, 0, 0)), row],
        [row], [_sds((SEQ, D_MODEL), F32)], [pltpu.VMEM((tm, D_MODEL), F32)], [dproj, w_in, dr1], ride)


def _weight_grad(a, b, name, shard_cols=None, ride=None):
    k, n = a.shape[1], b.shape[1]
    tk = min(k, 512) if shard_cols else k // N_DEV
    tn = n // 4 if shard_cols else min(n, 1024)

    def body(a_ref, b_ref, o_ref):
        grad = _dot_tn(a_ref[...].astype(BF16), b_ref[...].astype(BF16))
        if shard_cols:
            o_ref[0] = grad[:, :shard_cols].astype(BF16)
            o_ref[1] = grad[:, shard_cols:].astype(BF16)
        else:
            o_ref[...] = grad.astype(BF16)

    if shard_cols:
        out_spec = pl.BlockSpec((2, None, tk, shard_cols), lambda kk, j: (0, j, kk, 0))
        out_shape = _sds((2, 4, k, shard_cols), BF16)
    else:
        out_spec = pl.BlockSpec((None, None, tk, tn), lambda kk, j: (kk % 2, kk // 2, 0, j))
        out_shape = _sds((2, 4, tk, n), BF16)
    out = _call(body, name, (k // tk, n // tn),
                [pl.BlockSpec((SEQ, tk), lambda kk, j: (0, kk)), pl.BlockSpec((SEQ, tn), lambda kk, j: (0, j))],
                [out_spec], [out_shape], [], [a, b], ride)
    return out[0] if ride is None else out


MESH = pl.DeviceIdType.MESH
ANY = pl.BlockSpec(memory_space=pl.ANY)


def _place():
    return lax.axis_index("x"), lax.axis_index("y"), lax.axis_index("c")


def _other_chips(x, y):
    return [(1 - x, y), (x, 1 - y), (1 - x, 1 - y)]


class _Ride:
    def __init__(self, operands, results, aliases, sems, start, wait):
        self.operands, self.results, self.aliases, self.sems = list(operands), list(results), dict(aliases), list(sems)
        self.start, self.wait = start, wait

    def __add__(self, other):
        n_in, n_out, n_sem = len(self.operands), len(self.results), len(self.sems)

        def both(which):
            def run(ins, outs, sems):
                getattr(self, which)(ins[:n_in], outs[:n_out], sems[:n_sem])
                getattr(other, which)(ins[n_in:], outs[n_out:], sems[n_sem:])
            return run

        aliases = {**self.aliases, **{n_in + i: n_out + j for i, j in other.aliases.items()}}
        return _Ride(self.operands + other.operands, self.results + other.results, aliases, self.sems + other.sems,
                     both("start"), both("wait"))


def _call(body, name, grid, in_specs, out_specs, out_shape, scratch_shapes, operands, ride=None, aliases=None):
    in_specs, out_specs, out_shape = list(in_specs), list(out_specs), list(out_shape)
    scratch_shapes, operands, aliases = list(scratch_shapes), list(operands), dict(aliases or {})
    kernel_body = body
    if ride is not None:
        n_in, n_out, n_scr, r_in, r_out = len(in_specs), len(out_specs), len(scratch_shapes), len(ride.operands), len(ride.results)

        def kernel_body(*refs):
            out0, scr0 = n_in + r_in, n_in + r_in + n_out + r_out
            ride_refs = (refs[n_in:out0], refs[out0 + n_out:scr0], refs[scr0 + n_scr:])
            ids = [pl.program_id(i) for i in range(len(grid))]
            first = functools.reduce(jnp.logical_and, [i == 0 for i in ids])
            last = functools.reduce(jnp.logical_and, [i == g - 1 for i, g in zip(ids, grid)])

            @pl.when(first)
            def _():
                ride.start(*ride_refs)

            body(*refs[:n_in], *refs[out0:out0 + n_out], *refs[scr0:scr0 + n_scr])

            @pl.when(last)
            def _():
                ride.wait(*ride_refs)

        aliases.update({n_in + i: n_out + j for i, j in ride.aliases.items()})
        in_specs += [ANY] * r_in
        out_specs += [ANY] * r_out
        out_shape += ride.results
        scratch_shapes += ride.sems
        operands += ride.operands
    return _pallas_call(
        kernel_body, name=name, grid=grid, in_specs=in_specs, out_specs=out_specs, out_shape=out_shape,
        scratch_shapes=scratch_shapes, input_output_aliases=aliases,
        compiler_params=_cparams(dimension_semantics=("arbitrary",) * len(grid)),
    )(*operands)


def _gather_first_level(shards):
    n = len(shards)

    def copies(ins, outs, sems, landed):
        send_sems, recv_sems, local_sems = sems
        x, y, c = _place()
        peers = [(x, y, 1 - c)] + [(px, py, c) for px, py in _other_chips(x, y)]

        def row(peer):
            return 4 * x + 2 * y + c if not landed else 4 * peer[0] + 2 * peer[1] + peer[2]

        local = [pltpu.make_async_copy(ins[a], outs[a].at[4 * x + 2 * y + c], local_sems.at[a]) for a in range(n)]
        remote = [pltpu.make_async_remote_copy(
            src_ref=ins[a], dst_ref=outs[a].at[row(peer)], send_sem=send_sems.at[a, k], recv_sem=recv_sems.at[a, k],
            device_id=peer, device_id_type=MESH) for a in range(n) for k, peer in enumerate(peers)]
        return local, remote

    def start(ins, outs, sems):
        local, remote = copies(ins, outs, sems, False)
        for cp in local + remote:
            cp.start()

    def wait(ins, outs, sems):
        local, sent = copies(ins, outs, sems, False)
        for cp in copies(ins, outs, sems, True)[1]:
            cp.wait_recv()
        for cp in sent:
            cp.wait_send()
        for cp in local:
            cp.wait()

    return _Ride(shards, [_sds((N_DEV,) + s.shape, s.dtype) for s in shards], {},
                 [pltpu.SemaphoreType.DMA((n, 4)), pltpu.SemaphoreType.DMA((n, 4)), pltpu.SemaphoreType.DMA((n,))], start, wait)


def _gather_second_level(buffers):
    n = len(buffers)

    def copies(outs, sems, core):
        send_sems, recv_sems = sems
        x, y, c = _place()
        return [pltpu.make_async_remote_copy(
            src_ref=outs[a].at[4 * px + 2 * py + core], dst_ref=outs[a].at[4 * px + 2 * py + core], send_sem=send_sems.at[a, j],
            recv_sem=recv_sems.at[a, j], device_id=(x, y, 1 - c), device_id_type=MESH)
            for a in range(n) for j, (px, py) in enumerate(_other_chips(x, y))]

    def start(ins, outs, sems):
        for cp in copies(outs, sems, lax.axis_index("c")):
            cp.start()

    def wait(ins, outs, sems):
        for cp in copies(outs, sems, 1 - lax.axis_index("c")):
            cp.wait_recv()
        for cp in copies(outs, sems, lax.axis_index("c")):
            cp.wait_send()

    return _Ride(buffers, [_sds(b.shape, b.dtype) for b in buffers], {i: i for i in range(n)},
                 [pltpu.SemaphoreType.DMA((n, 3)), pltpu.SemaphoreType.DMA((n, 3))], start, wait)


def _relayed_gather(shards):
    n = len(shards)
    buffers = [_sds((N_DEV,) + s.shape, s.dtype) for s in shards]
    dma = pltpu.SemaphoreType.DMA

    def remote(src, dst, send_sem, recv_sem, to):
        return pltpu.make_async_remote_copy(src_ref=src, dst_ref=dst, send_sem=send_sem, recv_sem=recv_sem,
                                            device_id=to, device_id_type=MESH)

    def row(px, py, pc):
        return 4 * px + 2 * py + pc

    def ride(operands, aliases, sems, copies):
        def start(ins, outs, sem_refs):
            local, sent = copies(ins, outs, sem_refs, False)
            for cp in local + sent:
                cp.start()

        def wait(ins, outs, sem_refs):
            local, sent = copies(ins, outs, sem_refs, False)
            for cp in copies(ins, outs, sem_refs, True)[1]:
                cp.wait_recv()
            for cp in sent:
                cp.wait_send()
            for cp in local:
                cp.wait()

        return _Ride(operands, buffers, aliases, sems, start, wait)

    def first(ins, outs, sems, landed):
        x, y, c = _place()
        peers = [(x, y, 1 - c), (1 - x, y, c), (x, 1 - y, c)]
        local = [pltpu.make_async_copy(ins[a], outs[a].at[row(x, y, c)], sems[2].at[a]) for a in range(n)]
        return local, [remote(ins[a], outs[a].at[row(*peer) if landed else row(x, y, c)], sems[0].at[a, k], sems[1].at[a, k], peer)
                       for a in range(n) for k, peer in enumerate(peers)]

    def second(ins, outs, sems, landed):
        x, y, c = _place()
        mine = 1 - c if landed else c
        copies = []
        for a in range(n):
            half = shards[a].shape[0] // 2
            over_x, over_y, diagonal = outs[a].at[row(1 - x, y, mine)], outs[a].at[row(x, 1 - y, mine)], outs[a].at[row(1 - x, 1 - y, c)]
            lower, upper = pl.ds(0, half), pl.ds(half, half)
            copies += [remote(over_x, over_x, sems[0].at[a, 0], sems[1].at[a, 0], (x, y, 1 - c)),
                       remote(over_y, over_y, sems[0].at[a, 1], sems[1].at[a, 1], (x, y, 1 - c))]
            if landed:
                copies += [remote(diagonal.at[lower], diagonal.at[lower], sems[0].at[a, 2], sems[1].at[a, 2], (1 - x, y, c)),
                           remote(diagonal.at[upper], diagonal.at[upper], sems[0].at[a, 3], sems[1].at[a, 3], (x, 1 - y, c))]
            else:
                copies += [remote(over_y.at[lower], over_y.at[lower], sems[0].at[a, 2], sems[1].at[a, 2], (1 - x, y, c)),
                           remote(over_x.at[upper], over_x.at[upper], sems[0].at[a, 3], sems[1].at[a, 3], (x, 1 - y, c))]
        return [], copies

    def third(ins, outs, sems, landed):
        x, y, c = _place()
        return [], [remote(outs[a].at[row(1 - x, 1 - y, 1 - c if landed else c)], outs[a].at[row(1 - x, 1 - y, 1 - c if landed else c)],
                           sems[0].at[a], sems[1].at[a], (x, y, 1 - c)) for a in range(n)]

    def later(copies, n_sems):
        return lambda partly: ride(partly, {i: i for i in range(n)}, [dma((n,) + n_sems), dma((n,) + n_sems)], copies)

    return ride(shards, {}, [dma((n, 3)), dma((n, 3)), dma((n,))], first), later(second, (4,)), later(third, ())


def _sibling_swap_ride(grads):
    n = len(grads)

    def copies(ins, outs, sems):
        x, y, c = _place()
        return [pltpu.make_async_remote_copy(
            src_ref=ins[a].at[1 - c], dst_ref=outs[a], send_sem=sems[0].at[a], recv_sem=sems[1].at[a],
            device_id=(x, y, 1 - c), device_id_type=MESH) for a in range(n)]

    def start(ins, outs, sems):
        for cp in copies(ins, outs, sems):
            cp.start()

    def wait(ins, outs, sems):
        for cp in copies(ins, outs, sems):
            cp.wait()

    return _Ride(grads, [_sds(g.shape[1:], g.dtype) for g in grads], {},
                 [pltpu.SemaphoreType.DMA((n,)), pltpu.SemaphoreType.DMA((n,))], start, wait)


def _chip_swap_ride(sums, part=None, into=None):
    n = len(sums)
    piece, pieces = part or (0, 1)

    def copies(ins, outs, sems, landed):
        send_sems, recv_sems, local_sems = sems
        x, y, c = _place()
        mine = 2 * x + y

        def rows(a):
            size = sums[a].shape[1] // pieces
            return pl.ds(piece * size, size)

        local = [] if into else [pltpu.make_async_copy(ins[a].at[mine], outs[a].at[mine], local_sems.at[a]) for a in range(n)]
        remote = [pltpu.make_async_remote_copy(
            src_ref=ins[a].at[2 * px + py, rows(a)], dst_ref=outs[a].at[2 * px + py if landed else mine, rows(a)],
            send_sem=send_sems.at[a, j], recv_sem=recv_sems.at[a, j], device_id=(px, py, c), device_id_type=MESH)
            for a in range(n) for j, (px, py) in enumerate(_other_chips(x, y))]
        return local, remote

    def start(ins, outs, sems):
        local, remote = copies(ins, outs, sems, False)
        for cp in local + remote:
            cp.start()

    def wait(ins, outs, sems):
        local, sent = copies(ins, outs, sems, False)
        for cp in copies(ins, outs, sems, True)[1]:
            cp.wait_recv()
        for cp in sent:
            cp.wait_send()
        for cp in local:
            cp.wait()

    return _Ride(list(sums) + list(into or []), [_sds(s.shape, s.dtype) for s in sums], {n + i: i for i in range(n)} if into else {},
                 [pltpu.SemaphoreType.DMA((n, 3)), pltpu.SemaphoreType.DMA((n, 3)), pltpu.SemaphoreType.DMA((n,))], start, wait)


def _send_buffers(shards, name):
    n = len(shards)

    def body(*refs):
        for (w, transposed, rows, cols), w_ref, o_ref in zip(shards, refs[:n], refs[n:]):
            if transposed:
                c, r = w.shape
                padded = jnp.concatenate([w_ref[...], jnp.zeros((cols - c, r), F32)], axis=0) if cols > c else w_ref[...]
                o_ref[...] = padded.T.astype(BF16)
            else:
                r, c = w.shape
                if (r, c) != (rows, cols):
                    o_ref[...] = jnp.zeros((rows, cols), BF16)
                o_ref[:r, :c] = w_ref[...].astype(BF16)

    return _pallas_call(body, name=name, out_shape=[_sds((rows, cols), BF16) for _, _, rows, cols in shards])(
        *[w for w, _, _, _ in shards])


def _all_gather(shards, name):
    n = len(shards)
    first, second, third = _relayed_gather(shards)
    levels = [first, second(shards), third(shards)]
    counts = [len(level.sems) for level in levels]

    def body(*refs):
        ins, outs, sems = refs[:n], refs[n:2 * n], refs[2 * n:]
        for i, level in enumerate(levels):
            mine = sems[sum(counts[:i]):sum(counts[:i + 1])]
            level.start(ins, outs, mine)
            level.wait(ins, outs, mine)

    return _pallas_call(
        body, name=name, in_specs=[ANY] * n, out_specs=[ANY] * n, out_shape=first.results,
        scratch_shapes=[s for level in levels for s in level.sems],
    )(*shards)


def _swap_with_sibling(grads, name):
    n = len(grads)

    def body(*refs):
        ins, outs = refs[:n], refs[n:2 * n]
        send_sems, recv_sems = refs[2 * n:]
        x, y, c = _place()
        copies = [pltpu.make_async_remote_copy(
            src_ref=ins[a].at[1 - c], dst_ref=outs[a], send_sem=send_sems.at[a], recv_sem=recv_sems.at[a],
            device_id=(x, y, 1 - c), device_id_type=MESH) for a in range(n)]
        for cp in copies:
            cp.start()
        for cp in copies:
            cp.wait()

    return _pallas_call(
        body, name=name, in_specs=[ANY] * n, out_specs=[ANY] * n,
        out_shape=[_sds(g.shape[1:], g.dtype) for g in grads],
        scratch_shapes=[pltpu.SemaphoreType.DMA((n,)), pltpu.SemaphoreType.DMA((n,))],
    )(*grads)


def _pair_sums(gs, rs, core, name):
    n_arrays = len(gs)

    def body(core_ref, *refs):
        for g_ref, r_ref, o_ref in zip(refs[:n_arrays], refs[n_arrays:2 * n_arrays], refs[2 * n_arrays:]):
            o_ref[...] = (g_ref[...].astype(F32) + r_ref[...].astype(F32)).astype(o_ref.dtype)

    def own(g):
        return pl.BlockSpec((None, None) + g.shape[2:], lambda p, core_ref: (core_ref[0], p, 0, 0))

    def chip(g):
        return pl.BlockSpec((None,) + g.shape[2:], lambda p, core_ref: (p, 0, 0))

    return _pallas_call(
        body, name=name,
        grid_spec=pltpu.PrefetchScalarGridSpec(
            num_scalar_prefetch=1, grid=(4,), in_specs=[own(g) for g in gs] + [chip(g) for g in gs],
            out_specs=[chip(g) for g in gs]),
        out_shape=[_sds(g.shape[1:], g.dtype) for g in gs], compiler_params=_cparams(dimension_semantics=("arbitrary",)),
    )(core, *gs, *rs)


def _adamw_math(w, g, m, v):
    m = ADAM_B1 * m + (1.0 - ADAM_B1) * g
    v = ADAM_B2 * v + (1.0 - ADAM_B2) * (g * g)
    m_hat = m / (1.0 - ADAM_B1 ** ADAM_STEP)
    v_hat = v / (1.0 - ADAM_B2 ** ADAM_STEP)
    return -ADAM_LR * (m_hat / (jnp.sqrt(v_hat) + ADAM_EPS) + ADAM_WD * w), m, v


def _adamw(w, m, v, parts, name):
    r, c = w.shape
    tr = r if r <= 512 else 256
    n_parts, pr, pc = parts.shape
    assert r % tr == 0 and (tr == r or pr == r)

    def body(w_ref, m_ref, v_ref, p_ref, g_out, d_out, m_out, v_out):
        g = p_ref[0, :tr, :c].astype(F32)
        for p in range(1, n_parts):
            g = g + p_ref[p, :tr, :c].astype(F32)
        g_out[...] = g
        d_out[...], m_out[...], v_out[...] = _adamw_math(w_ref[...], g, m_ref[...], v_ref[...])

    tile = pl.BlockSpec((tr, c), lambda i: (i, 0))
    part_tile = pl.BlockSpec((n_parts, pr if tr == r else tr, pc), lambda i: (0, i, 0))
    out = _sds((r, c), F32)
    return _pallas_call(
        body, name=name, grid=(r // tr,), in_specs=[tile, tile, tile, part_tile], out_specs=(tile,) * 4,
        out_shape=(out,) * 4, compiler_params=_cparams(dimension_semantics=("arbitrary",)),
    )(w, m, v, parts)


def _adamw_many(weights, name, ride=None):
    steps = 4
    in_specs, out_specs, out_shape, operands, tiles = [], [], [], [], []
    for w, m, v, parts, transposed in weights:
        n_parts, pr, pc = parts.shape
        if transposed:
            c, r = w.shape
            tile = pl.BlockSpec((c, r // steps), lambda i: (0, i))
            part_tile = pl.BlockSpec((n_parts, r // steps, pc), lambda i: (0, i, 0))
            tiles.append((c, r // steps))
        elif w.shape[0] % (8 * steps) == 0:
            r, c = w.shape
            tile = pl.BlockSpec((r // steps, c), lambda i: (i, 0))
            part_tile = pl.BlockSpec((n_parts, r // steps, pc), lambda i: (0, i, 0))
            tiles.append((r // steps, c))
        else:
            tile = pl.BlockSpec(w.shape, lambda i: (0, 0))
            part_tile = pl.BlockSpec(parts.shape, lambda i: (0, 0, 0))
            tiles.append(w.shape)
        in_specs += [tile, tile, tile, part_tile]
        out_specs += [tile] * 4
        out_shape += [_sds(w.shape, F32)] * 4
        operands += [w, m, v, parts]

    def body(*refs):
        ins, outs = refs[:4 * len(weights)], refs[4 * len(weights):]
        for k, (_, _, _, parts, transposed) in enumerate(weights):
            w_ref, m_ref, v_ref, p_ref = ins[4 * k:4 * k + 4]
            rows, cols = tiles[k]
            if transposed:
                g = p_ref[0].astype(F32)
                for p in range(1, parts.shape[0]):
                    g = g + p_ref[p].astype(F32)
                g = g.T[:rows]
            else:
                g = p_ref[0, :rows, :cols].astype(F32)
                for p in range(1, parts.shape[0]):
                    g = g + p_ref[p, :rows, :cols].astype(F32)
            g_out, d_out, m_out, v_out = outs[4 * k:4 * k + 4]
            g_out[...] = g
            d_out[...], m_out[...], v_out[...] = _adamw_math(w_ref[...], g, m_ref[...], v_ref[...])

    return _call(body, name, (steps,), in_specs, out_specs, out_shape, [], operands, ride)


SMALL = ("ssm_a_re", "ssm_a_im", "ssm_log_dt", "ssm_b_re", "ssm_b_im", "ssm_c_re", "ssm_c_im", "ssm_d",
         "ln1_g", "ln1_b", "ln2_g", "ln2_b")


def _pack_rows(arrays):
    rows = []
    for a in arrays:
        flat = a.reshape(-1)
        rows.append(jnp.pad(flat, (0, -flat.shape[0] % 128)).reshape(-1, 128))
    packed = jnp.concatenate(rows, axis=0)
    return jnp.pad(packed, ((0, -packed.shape[0] % 8), (0, 0)))


def _unpack_rows(packed, shapes):
    out, row = [], 0
    for shape in shapes:
        size = math.prod(shape)
        n_rows = -(-size // 128)
        out.append(packed[row:row + n_rows].reshape(-1)[:size].reshape(shape))
        row += n_rows
    return out


def _sum_devices(parts):
    def body(p_ref, o_ref):
        total = p_ref[0]
        for dev in range(1, N_DEV):
            total = total + p_ref[dev]
        o_ref[...] = total

    return _pallas_call(body, name="sum_devices", out_shape=_sds(parts.shape[1:], F32))(parts)


def _adamw_replicated(ws, ms, vs, gs):
    n = len(ws)

    def body(*refs):
        w_refs, m_refs, v_refs, g_refs, d_out, m_out, v_out = (refs[i * n:(i + 1) * n] for i in range(7))
        for i in range(n):
            d_out[i][...], m_out[i][...], v_out[i][...] = _adamw_math(w_refs[i][...], g_refs[i][...], m_refs[i][...], v_refs[i][...])

    out = _pallas_call(body, name="adamw_replicated", out_shape=[_sds(w.shape, F32) for w in ws] * 3,
                       compiler_params=_cparams())(*ws, *ms, *vs, *gs)
    return out[:n], out[n:2 * n], out[2 * n:]


def kernel(x, w_in, b_gate, w_attn_br, w_ssm_br, w_out, ssm_a_re, ssm_a_im, ssm_log_dt, ssm_b_re, ssm_b_im, ssm_c_re, ssm_c_im, ssm_d, w_glu, ln1_g, ln1_b, w_ff_gate, w_ff_up, w_ff_down, ln2_g, ln2_b, loss_target, m_w_in, m_b_gate, m_w_attn_br, m_w_ssm_br, m_w_out, m_ssm_a_re, m_ssm_a_im, m_ssm_log_dt, m_ssm_b_re, m_ssm_b_im, m_ssm_c_re, m_ssm_c_im, m_ssm_d, m_w_glu, m_ln1_g, m_ln1_b, m_w_ff_gate, m_w_ff_up, m_w_ff_down, m_ln2_g, m_ln2_b, v_w_in, v_b_gate, v_w_attn_br, v_w_ssm_br, v_w_out, v_ssm_a_re, v_ssm_a_im, v_ssm_log_dt, v_ssm_b_re, v_ssm_b_im, v_ssm_c_re, v_ssm_c_im, v_ssm_d, v_w_glu, v_ln1_g, v_ln1_b, v_w_ff_gate, v_w_ff_up, v_w_ff_down, v_ln2_g, v_ln2_b):
    given = dict(locals())
    x2, target = x[0], loss_target[0]
    core = lax.axis_index("c").astype(jnp.int32).reshape(1)

    sharded = ("w_in", "w_attn_br", "w_ssm_br", "w_glu", "w_ff_gate", "w_ff_up", "b_gate", "w_out", "w_ff_down")
    send_shape = dict(w_in=(D_MODEL, 896), w_attn_br=(ATTN_WIDTH, 128), w_ssm_br=(SSM_WIDTH, 128), w_glu=(SSM_WIDTH, 128),
                      w_out=(128, D_MODEL), w_ff_gate=(D_MODEL, FF_PAD), w_ff_up=(D_MODEL, FF_PAD), w_ff_down=(FF_PAD, D_MODEL))
    local = {k: given[k][0] for k in sharded}
    narrow = ("w_ff_gate", "w_ff_up")
    def to_send(k):
        return (local[k].T, True, *send_shape[k]) if k in narrow else (local[k], False, *send_shape[k])

    later = [k for k in sharded if k not in ("w_in", "b_gate")]
    sends = dict(zip(["w_in"] + later, _send_buffers([to_send("w_in")], "send_w_in")
                     + _send_buffers([to_send(k) for k in later], "send_weights")))
    sends["b_gate"] = local["b_gate"]
    mixer_weights = ("w_attn_br", "w_ssm_br", "w_glu", "b_gate", "w_out")
    ff_weights = ("w_ff_gate", "w_ff_up", "w_ff_down")
    wt = {}
    wt["w_in"], = _all_gather([sends["w_in"]], "gather_w_in")

    a_re, a_im, log_dt = ssm_a_re[0], ssm_a_im[0], ssm_log_dt[0].reshape(SSM_GROUPS, 1)
    b_re_t, b_im_t = ssm_b_re[0].transpose(0, 2, 1), ssm_b_im[0].transpose(0, 2, 1)
    abar_re, abar_im, e_re, e_im, bbar_re_t, bbar_im_t = _ssm_prep(a_re, a_im, log_dt, b_re_t, b_im_t)
    bmat, cmat, a_chunks = _ssm_tables(abar_re, abar_im, bbar_re_t, bbar_im_t, ssm_c_re[0], ssm_c_im[0])
    cos_t, sin_t = _rope_tables()

    big_mixer, ff_in = [k for k in mixer_weights if k != "b_gate"], ("w_ff_gate", "w_ff_up")
    n_mixer = len(big_mixer)
    mixer_1, mixer_2, mixer_3 = _relayed_gather([sends[k] for k in big_mixer])
    ff_in_1, ff_in_2, ff_in_3 = _relayed_gather([sends[k] for k in ff_in])
    ff_down_1, ff_down_2, ff_down_3 = _relayed_gather([sends["w_ff_down"]])
    proj, *landed = _proj(x2, wt["w_in"], mixer_1 + _gather_first_level([sends["b_gate"]]))
    mixer, bias = landed[:n_mixer], landed[n_mixer:]
    attn, lse, *landed = _attn_fwd(proj, cos_t, sin_t, mixer_2(mixer) + _gather_second_level(bias) + ff_in_1)
    mixer, b_gate_full, ff = landed[:n_mixer], landed[n_mixer], landed[n_mixer + 1:]
    ys, states, *landed = _ssm_fwd(proj, bmat, cmat, a_chunks, ssm_d, mixer_3(mixer) + ff_in_2(ff) + ff_down_1)
    wt.update(zip(big_mixer, landed[:n_mixer]))
    ff, ff_down = landed[n_mixer:n_mixer + 2], landed[n_mixer + 2:]
    wt["w_out"] = wt["w_out"].reshape(D_MODEL, D_MODEL)
    h, xhat1, rstd1, glu, y_attn, y_ssm, *landed = _mixer_out(
        attn, ys, proj, x2, wt["w_attn_br"], wt["w_ssm_br"], wt["w_glu"], wt["w_out"], b_gate_full, ln1_g, ln1_b,
        ff_in_3(ff) + ff_down_2(ff_down))
    wt.update(zip(ff_in, landed[:2]))
    ff_a, ff_b, ff_f, w_ff_down = _ff_up(h, wt["w_ff_gate"], wt["w_ff_up"], ff_down_3(landed[2:]))
    wt["w_ff_down"] = w_ff_down.reshape(D_FF_PAD, D_MODEL)
    dr2, d_ln2_g, d_ln2_b, loss_lanes = _ff_down_loss(ff_f, wt["w_ff_down"], h, target, ln2_g, ln2_b)

    def pair_sums(names, contrib, from_sibling):
        return _pair_sums([contrib[k] for k in names], from_sibling, core, "pair_sums_" + names[0])

    d_a, d_b = _ff_down_bwd(dr2, wt["w_ff_down"], ff_a, ff_b)
    contrib = dict(w_ff_gate=_weight_grad(h, d_a, "wgrad_w_ff_gate", FF_PAD),
                   w_ff_up=_weight_grad(h, d_b, "wgrad_w_ff_up", FF_PAD),
                   w_ff_down=_weight_grad(ff_f, dr2, "wgrad_w_ff_down"))
    dr1, d_ln1_g, d_ln1_b, *from_sibling = _ff_up_bwd(
        d_a, d_b, wt["w_ff_gate"], wt["w_ff_up"], dr2, xhat1, rstd1, ln1_g, _sibling_swap_ride([contrib[k] for k in ff_weights]))
    ff_sums = pair_sums(ff_weights, contrib, from_sibling)

    d_ya, d_yssm, d_proj, d_attn, d_glu, d_ys, mixed, y_s, gy, d_bg = _mixer_bwd(
        dr1, proj, y_attn, y_ssm, glu, ys, wt["w_attn_br"], wt["w_ssm_br"], wt["w_glu"], wt["w_out"], b_gate_full)
    contrib.update(w_attn_br=_weight_grad(attn, d_ya, "wgrad_w_attn_br", 128),
                   w_ssm_br=_weight_grad(y_s, d_yssm, "wgrad_w_ssm_br", 128),
                   w_glu=_weight_grad(gy, d_glu, "wgrad_w_glu", 128),
                   w_out=_weight_grad(mixed, dr1, "wgrad_w_out"),
                   b_gate=d_bg.reshape(2, 4, 2, 128).transpose(2, 1, 0, 3))
    d_proj, *landed = _attn_bwd(proj, cos_t, sin_t, attn, lse, d_attn, d_proj,
                                _chip_swap_ride(ff_sums) + _sibling_swap_ride([contrib[k] for k in mixer_weights]))
    parts = dict(zip(ff_weights, landed[:len(ff_weights)]))
    mixer_sums = pair_sums(mixer_weights, contrib, landed[len(ff_weights):])
    d_proj, d_bmat, d_cmat, d_abar, d_skip, *landed = _ssm_bwd(d_ys, proj, states, bmat, cmat, a_chunks, ssm_d, d_proj,
                                                               _chip_swap_ride(mixer_sums))
    parts.update(zip(mixer_weights, landed))

    gbb_re_t, gbb_im_t = _block_diag_parts(d_bmat, True)
    gc_re, gc_im = _block_diag_parts(d_cmat, False)
    ga_re = d_abar[:, 0, :CHUNK_STATES].reshape(SSM_GROUPS, SSM_STATE)
    ga_im = d_abar[:, 0, CHUNK_STATES:].reshape(SSM_GROUPS, SSM_STATE)
    g_a_re, g_a_im, g_log_dt, g_b_re_t, g_b_im_t = _ssm_param_bwd(
        a_re, a_im, log_dt, b_re_t, b_im_t, abar_re, abar_im, e_re, e_im, ga_re, ga_im, gbb_re_t, gbb_im_t)
    mine = [g_a_re, g_a_im, g_log_dt, g_b_re_t, g_b_im_t, gc_re, -gc_im,
            d_skip, d_ln1_g, d_ln1_b, d_ln2_g, d_ln2_b]
    small_packed = _pack_rows(mine + [loss_lanes])

    contrib["w_in"], small_partly = _weight_grad(x2, d_proj, "wgrad_w_in", 896, _gather_first_level([small_packed]))
    w_in_sum = pair_sums(["w_in"], contrib, _swap_with_sibling([contrib["w_in"]], "swap_w_in_with_sibling"))
    grad_x, w_in_half, every = _grad_x(d_proj, wt["w_in"], dr1,
                                       _chip_swap_ride(w_in_sum, part=(0, 2)) + _gather_second_level([small_partly]))
    others = [k for k in sharded if k != "w_in"]
    *updated, parts["w_in"] = _adamw_many(
        [(local[k].T, given["m_" + k][0].T, given["v_" + k][0].T, parts[k], True) if k in narrow
         else (local[k], given["m_" + k][0], given["v_" + k][0], parts[k], False) for k in others],
        "adamw_others", _chip_swap_ride(w_in_sum, part=(1, 2), into=[w_in_half]))
    updated += _adamw(local["w_in"], given["m_w_in"][0], given["v_w_in"][0], parts["w_in"], "adamw_w_in")

    grads, deltas, new_m, new_v = {}, {}, {}, {}
    for i, k in enumerate(others + ["w_in"]):
        out = [o.T if k in narrow else o for o in updated[4 * i:4 * i + 4]]
        grads[k], deltas[k], new_m[k], new_v[k] = (o.reshape((1,) + local[k].shape) for o in out)

    def held(k, a):
        return a.transpose(0, 1, 3, 2) if k in ("ssm_b_re", "ssm_b_im") else a

    *small_grads, loss_sum = _unpack_rows(_sum_devices(every), [held(k, given[k]).shape for k in SMALL] + [(1, 128)])
    small = _adamw_replicated([held(k, given[k]) for k in SMALL], [held(k, given["m_" + k]) for k in SMALL],
                              [held(k, given["v_" + k]) for k in SMALL], small_grads)
    for res, values in zip((grads, deltas, new_m, new_v), (small_grads,) + small):
        res.update((k, held(k, a)) for k, a in zip(SMALL, values))
    loss = loss_sum[0, 0]

    order = ("w_in", "b_gate", "w_attn_br", "w_ssm_br", "w_out", "ssm_a_re", "ssm_a_im", "ssm_log_dt", "ssm_b_re", "ssm_b_im",
             "ssm_c_re", "ssm_c_im", "ssm_d", "w_glu", "ln1_g", "ln1_b", "w_ff_gate", "w_ff_up", "w_ff_down", "ln2_g", "ln2_b")
    return (loss, grad_x[None], *[grads[k] for k in order], *[deltas[k] for k in order], *[new_m[k] for k in order],
            *[new_v[k] for k in order])
```

```python
import functools
import math

import jax
import jax.numpy as jnp
import numpy as np
from jax import lax
from jax.experimental import pallas as pl
from jax.experimental.pallas import tpu as pltpu

F32 = jnp.float32
BF16 = jnp.bfloat16

N_DEV = 8
SEQ = 2048
D_MODEL = 1024
HEAD_DIM = 64
ATTN_WIDTH = 512
QKV_WIDTH = 1536
SSM_WIDTH = 512
SSM_GROUPS = 32
SSM_GROUP = 16
SSM_STATE = 64
IN_WIDTH = 7168
D_FF = 2816
FF_SHARD = D_FF // N_DEV
FF_PAD = 384
D_FF_PAD = FF_PAD * N_DEV
DN_ALPHA = 2.0 ** 0.25
LN_EPS = 1e-5
NEG_INF = -1e30
ROPE_THETA = 10000.0
BLOCK = 128
GROUPS = ((1, 16), (4, 4), (16, 1))

ADAM_LR = 0.001
ADAM_B1 = 0.9
ADAM_B2 = 0.999
ADAM_EPS = 1e-08
ADAM_WD = 0.01
ADAM_STEP = 10

VMEM_LIMIT = 56 * 1024 * 1024


_pallas_call = pl.pallas_call


def _cparams(**kw):
    return pltpu.CompilerParams(vmem_limit_bytes=VMEM_LIMIT, **kw)


def _dot(a, b):
    return jnp.dot(a, b, preferred_element_type=F32)


def _dot_nt(a, b):
    return lax.dot_general(a, b, (((1,), (1,)), ((), ())), preferred_element_type=F32)


def _side_by_side(w_ref, row=None):
    rows = slice(None) if row is None else pl.ds(row, 1)
    return jnp.concatenate([w_ref[i, rows, :] for i in range(w_ref.shape[0])], axis=1)


def _dot_tn(a, b):
    return lax.dot_general(a, b, (((0,), (0,)), ((), ())), preferred_element_type=F32)


def _rope_tables():
    half = HEAD_DIM // 2
    inv_freq = np.float32(ROPE_THETA) ** (-np.arange(half, dtype=np.float32) / np.float32(half))
    ang = np.arange(SEQ, dtype=np.float32)[:, None] * inv_freq[None, :]
    cos, sin = np.cos(ang).astype(np.float32), np.sin(ang).astype(np.float32)
    tables = np.tile(cos, (1, 4)), np.tile(np.concatenate([-sin, sin], axis=1), (1, 2))

    def by_phase(t):
        return np.stack([t.reshape(SEQ // d, d, 128).transpose(1, 0, 2).reshape(SEQ, 128) for d, _ in GROUPS])

    return jnp.asarray(by_phase(tables[0])), jnp.asarray(by_phase(tables[1]))


def _swap_halves(x):
    lane = lax.broadcasted_iota(jnp.int32, x.shape, 1)
    return jnp.where((lane & 63) < 32, pltpu.roll(x, 96, axis=1), pltpu.roll(x, 32, axis=1))


def _group_rows(d, nb, r, i):
    src = pl.ds(i * BLOCK, BLOCK) if d == 1 else pl.ds(r + i * BLOCK * d, BLOCK, stride=d)
    return src, pl.ds((r * nb + i) * BLOCK, BLOCK)


def _attn_masks():
    a_idx = lax.broadcasted_iota(jnp.int32, (2 * BLOCK, 2 * BLOCK), 0) & (BLOCK - 1)
    c_idx = lax.broadcasted_iota(jnp.int32, (2 * BLOCK, 2 * BLOCK), 1)
    cur_ok = jnp.logical_and(c_idx >= BLOCK, c_idx - BLOCK <= a_idx)
    prev_ok = jnp.logical_and(c_idx < BLOCK, c_idx >= a_idx)
    lane = lax.broadcasted_iota(jnp.int32, (BLOCK, 128), 1)
    return cur_ok, prev_ok, lane < HEAD_DIM


def _stack_heads(t, head0):
    zero = jnp.zeros_like(t)
    return jnp.concatenate([jnp.where(head0, t, zero), jnp.where(head0, zero, t)], axis=0)


def _unstack_heads(t2, head0):
    return jnp.where(head0, t2[:BLOCK], t2[BLOCK:])


def _attn_fwd(proj, cos_t, sin_t, ride=None):
    def body(q0, q1, q2, k0, k1, k2, v0, v1, v2, cos_ref, sin_ref, attn_ref, lse_ref,
             qs, ks, vs, os_, ms, ls, acc, mnat, lnat):
        cur_ok, prev_ok, head0 = _attn_masks()
        ks[:BLOCK, :] = jnp.zeros((BLOCK, 128), BF16)
        vs[:BLOCK, :] = jnp.zeros((BLOCK, 128), BF16)
        for g, (d, nb) in enumerate(GROUPS):
            q_ref, k_ref, v_ref = (q0, q1, q2)[g], (k0, k1, k2)[g], (v0, v1, v2)[g]
            for r in range(d):
                for i in range(nb):
                    src, dst = _group_rows(d, nb, r, i)
                    below = pl.ds(dst.start + BLOCK, BLOCK)
                    c, s = cos_ref[g, dst, :], sin_ref[g, dst, :]
                    q = q_ref[src, :]
                    k = k_ref[src, :]
                    qs[dst, :] = ((q * c + _swap_halves(q) * s) * 0.125).astype(BF16)
                    ks[below, :] = (k * c + _swap_halves(k) * s).astype(BF16)
                    vs[below, :] = v_ref[src, :].astype(BF16)

            def block(b, carry, nb=nb):
                has_prev = (b & (nb - 1)) > 0
                cur = pl.ds(pl.multiple_of(b * BLOCK, BLOCK), BLOCK)
                window = pl.ds(pl.multiple_of(b * BLOCK, BLOCK), 2 * BLOCK)
                valid = jnp.logical_or(cur_ok, jnp.logical_and(prev_ok, has_prev))
                s = jnp.where(valid, _dot_nt(_stack_heads(qs[cur, :], head0), ks[window, :]), NEG_INF)
                m = jnp.max(s, axis=1, keepdims=True)
                p = jnp.exp(s - m)
                os_[cur, :] = _unstack_heads(_dot(p.astype(BF16), vs[window, :]), head0)
                ms[cur, :] = _unstack_heads(m, head0)
                ls[cur, :] = _unstack_heads(jnp.sum(p, axis=1, keepdims=True), head0)
                return carry

            lax.fori_loop(0, SEQ // BLOCK, block, 0, unroll=16)

            for r in range(d):
                for i in range(nb):
                    src, dst = _group_rows(d, nb, r, i)
                    if g == 0:
                        acc[src, :], mnat[src, :], lnat[src, :] = os_[dst, :], ms[dst, :], ls[dst, :]
                    else:
                        m_old, m_g = mnat[src, :], ms[dst, :]
                        m_new = jnp.maximum(m_old, m_g)
                        a_old, a_g = jnp.exp(m_old - m_new), jnp.exp(m_g - m_new)
                        acc[src, :] = a_old * acc[src, :] + a_g * os_[dst, :]
                        lnat[src, :] = a_old * lnat[src, :] + a_g * ls[dst, :]
                        mnat[src, :] = m_new
        for i in range(SEQ // BLOCK):
            rows = pl.ds(i * BLOCK, BLOCK)
            l = lnat[rows, :]
            attn_ref[rows, :] = acc[rows, :] / l
            lse_ref[rows, :] = mnat[rows, :] + jnp.log(l)

    def col(base):
        return pl.BlockSpec((SEQ, 128), lambda hp, base=base: (0, base + hp))

    in_specs = [col(g * 4) for g in range(3)] + [col(12 + g * 4) for g in range(3)] + [col(24 + g * 4) for g in range(3)]
    table = pl.BlockSpec((3, SEQ, 128), lambda hp: (0, 0, 0), pipeline_mode=pl.Buffered(1))
    out = pl.BlockSpec((SEQ, 128), lambda hp: (0, hp))
    return _call(
        body, "attn_fwd", (4,), in_specs + [table, table], [out, out],
        [_sds((SEQ, ATTN_WIDTH), F32), _sds((SEQ, ATTN_WIDTH), F32)],
        [pltpu.VMEM((SEQ, 128), BF16)] + [pltpu.VMEM((SEQ + BLOCK, 128), BF16)] * 2 + [pltpu.VMEM((SEQ, 128), F32)] * 6,
        [proj] * 9 + [cos_t, sin_t], ride)


def _attn_bwd_group_body(g):
    d, nb = GROUPS[g]

    def body(q_ref, k_ref, v_ref, cos_ref, sin_ref, lse_ref, dattn_ref, dsum_ref, dproj_ref,
             qs, ks, vs, dos, lss, dss, dqs, dks, dvs, stage, outs, sems):
        cur_ok, prev_ok, head0 = _attn_masks()
        ks[:BLOCK, :] = jnp.zeros((BLOCK, 128), BF16)
        vs[:BLOCK, :] = jnp.zeros((BLOCK, 128), BF16)
        dks[:BLOCK, :] = jnp.zeros((BLOCK, 128), F32)
        dvs[:BLOCK, :] = jnp.zeros((BLOCK, 128), F32)
        for r in range(d):
            for i in range(nb):
                src, dst = _group_rows(d, nb, r, i)
                below = pl.ds(dst.start + BLOCK, BLOCK)
                c, s = cos_ref[g, dst, :], sin_ref[g, dst, :]
                q = q_ref[src, :]
                k = k_ref[src, :]
                qs[dst, :] = ((q * c + _swap_halves(q) * s) * 0.125).astype(BF16)
                ks[below, :] = (k * c + _swap_halves(k) * s).astype(BF16)
                vs[below, :] = v_ref[src, :].astype(BF16)
                dos[dst, :] = dattn_ref[src, :].astype(BF16)
                dss[dst, :] = dsum_ref[src, :]
                lss[dst, :] = lse_ref[src, :]
                dks[below, :] = jnp.zeros((BLOCK, 128), F32)
                dvs[below, :] = jnp.zeros((BLOCK, 128), F32)

        def per_head_column(t):
            return jnp.concatenate([jnp.max(jnp.where(head0, t, NEG_INF), axis=1, keepdims=True),
                                    jnp.max(jnp.where(head0, NEG_INF, t), axis=1, keepdims=True)], axis=0)

        def block(b, carry):
            has_prev = (b & (nb - 1)) > 0
            cur = pl.ds(pl.multiple_of(b * BLOCK, BLOCK), BLOCK)
            window = pl.ds(pl.multiple_of(b * BLOCK, BLOCK), 2 * BLOCK)
            valid = jnp.logical_or(cur_ok, jnp.logical_and(prev_ok, has_prev))
            q2, do2 = _stack_heads(qs[cur, :], head0), _stack_heads(dos[cur, :], head0)
            kw, vw = ks[window, :], vs[window, :]
            s = jnp.where(valid, _dot_nt(q2, kw), NEG_INF)
            p = jnp.exp(s - per_head_column(lss[cur, :]))
            ds = (p * (_dot_nt(do2, vw) - per_head_column(dss[cur, :]))).astype(BF16)
            dvs[window, :] += _dot_tn(p.astype(BF16), do2)
            dks[window, :] += _dot_tn(ds, q2)
            dqs[cur, :] = _unstack_heads(_dot(ds, kw), head0)
            return carry

        lax.fori_loop(0, SEQ // BLOCK, block, 0, unroll=8)

        hp = pl.program_id(0)
        copies = []
        for kind in range(3):
            for r in range(d):
                for i in range(nb):
                    src, dst = _group_rows(d, nb, r, i)
                    below = pl.ds(dst.start + BLOCK, BLOCK)
                    if kind == 2:
                        stage[src, :] = dvs[below, :]
                    else:
                        c, s = cos_ref[g, dst, :], sin_ref[g, dst, :]
                        t = dqs[dst, :] * 0.125 if kind == 0 else dks[below, :]
                        stage[src, :] = t * c - _swap_halves(t) * s
            for i in range(SEQ // MM_ROWS):
                rows = pl.ds(i * MM_ROWS, MM_ROWS)
                outs[kind, rows, :] = stage[rows, :].astype(BF16)
            column = pl.multiple_of((kind * 12 + g * 4 + hp) * 128, 128)
            copies.append(pltpu.make_async_copy(outs.at[kind], dproj_ref.at[:, pl.ds(column, 128)], sems.at[kind]))
            copies[-1].start()
        for cp in copies:
            cp.wait()

    return body


def _attn_bwd(proj, cos_t, sin_t, attn, lse, dattn, dproj, ride=None):
    groups = [_attn_bwd_group_body(g) for g in range(3)]

    def body(q0, q1, q2, k0, k1, k2, v0, v1, v2, cos_ref, sin_ref, attn_ref, lse_ref, dattn_ref, dproj_in, dproj_ref,
             dsum, *scratch):
        del dproj_in
        head0 = _attn_masks()[2]
        for i in range(SEQ // BLOCK):
            rows = pl.ds(i * BLOCK, BLOCK)
            prod = dattn_ref[rows, :] * attn_ref[rows, :]
            d0 = jnp.sum(jnp.where(head0, prod, 0.0), axis=1, keepdims=True)
            d1 = jnp.sum(jnp.where(head0, 0.0, prod), axis=1, keepdims=True)
            dsum[rows, :] = jnp.where(head0, d0, d1)
        for g in range(3):
            groups[g]((q0, q1, q2)[g], (k0, k1, k2)[g], (v0, v1, v2)[g], cos_ref, sin_ref, lse_ref, dattn_ref, dsum,
                      dproj_ref, *scratch)

    def col(base):
        return pl.BlockSpec((SEQ, 128), lambda hp, base=base: (0, base + hp))

    table = pl.BlockSpec((3, SEQ, 128), lambda hp: (0, 0, 0), pipeline_mode=pl.Buffered(1))
    return _call(
        body, "attn_bwd", (4,),
        [col(g * 4) for g in range(3)] + [col(12 + g * 4) for g in range(3)] + [col(24 + g * 4) for g in range(3)]
        + [table, table, col(0), col(0), col(0), ANY],
        [ANY], [_sds((SEQ, IN_WIDTH), BF16)],
        [pltpu.VMEM((SEQ, 128), F32)]
        + [pltpu.VMEM((SEQ, 128), BF16)] + [pltpu.VMEM((SEQ + BLOCK, 128), BF16)] * 2 + [pltpu.VMEM((SEQ, 128), BF16)]
        + [pltpu.VMEM((SEQ, 128), F32)] * 3 + [pltpu.VMEM((SEQ + BLOCK, 128), F32)] * 2 + [pltpu.VMEM((SEQ, 128), F32)]
        + [pltpu.VMEM((3, SEQ, 128), BF16), pltpu.SemaphoreType.DMA((3,))],
        [proj] * 9 + [cos_t, sin_t, attn, lse, dattn, dproj], ride, aliases={14: 0})


SSM_CHUNKS = 4
CHUNK_STATES = 512
SCAN_ROWS = 8
U_COL = (3 * QKV_WIDTH) // 128


def _cmul(xr, xi, yr, yi):
    return xr * yr - xi * yi, xr * yi + xi * yr


def _ssm_prep(a_re, a_im, log_dt, b_re_t, b_im_t):
    def body(ar_ref, ai_ref, ldt_ref, br_ref, bi_ref, abr_ref, abi_ref, er_ref, ei_ref, bbr_ref, bbi_ref):
        ar, ai = ar_ref[...], ai_ref[...]
        dt = jnp.exp(ldt_ref[...])
        mag = jnp.exp(ar * dt)
        abr, abi = mag * jnp.cos(ai * dt), mag * jnp.sin(ai * dt)
        den = ar * ar + ai * ai
        nr, ni = abr - 1.0, abi
        er, ei = (nr * ar + ni * ai) / den, (ni * ar - nr * ai) / den
        abr_ref[...], abi_ref[...], er_ref[...], ei_ref[...] = abr, abi, er, ei
        er3, ei3 = er[:, None, :], ei[:, None, :]
        br, bi = br_ref[...], bi_ref[...]
        bbr_ref[...] = er3 * br - ei3 * bi
        bbi_ref[...] = er3 * bi + ei3 * br

    gp = jax.ShapeDtypeStruct(a_re.shape, F32)
    gb = jax.ShapeDtypeStruct(b_re_t.shape, F32)
    return _pallas_call(body, name="ssm_prep", out_shape=(gp, gp, gp, gp, gb, gb))(a_re, a_im, log_dt, b_re_t, b_im_t)


def _ssm_param_bwd(a_re, a_im, log_dt, b_re_t, b_im_t, abar_re, abar_im, e_re, e_im, ga_re, ga_im, gbb_re_t, gbb_im_t):
    def body(ar_ref, ai_ref, ldt_ref, br_ref, bi_ref, abr_ref, abi_ref, er_ref, ei_ref, gar_ref, gai_ref, gbr_ref, gbi_ref,
             o_ar, o_ai, o_ldt, o_br, o_bi):
        ar, ai = ar_ref[...], ai_ref[...]
        dt = jnp.exp(ldt_ref[...])
        er, ei = er_ref[...], ei_ref[...]
        br, bi, gbr, gbi = br_ref[...], bi_ref[...], gbr_ref[...], gbi_ref[...]
        er3, ei3 = er[:, None, :], ei[:, None, :]
        o_br[...] = er3 * gbr + ei3 * gbi
        o_bi[...] = er3 * gbi - ei3 * gbr
        ge_r = jnp.sum(br * gbr + bi * gbi, axis=1)
        ge_i = jnp.sum(br * gbi - bi * gbr, axis=1)
        den = ar * ar + ai * ai
        ilr, ili = ar / den, -ai / den
        t_r, t_i = _cmul(ilr, -ili, ge_r, ge_i)
        gab_r, gab_i = gar_ref[...] + t_r, gai_ref[...] + t_i
        gz_r, gz_i = _cmul(abr_ref[...], -abi_ref[...], gab_r, gab_i)
        el_r, el_i = _cmul(er, ei, ilr, ili)
        u_r, u_i = _cmul(el_r, -el_i, ge_r, ge_i)
        o_ar[...] = dt * gz_r - u_r
        o_ai[...] = dt * gz_i - u_i
        o_ldt[...] = jnp.sum(gz_r * ar + gz_i * ai, axis=1, keepdims=True) * dt

    gp = jax.ShapeDtypeStruct(a_re.shape, F32)
    gb = jax.ShapeDtypeStruct(b_re_t.shape, F32)
    return _pallas_call(body, name="ssm_param_bwd", out_shape=(gp, gp, jax.ShapeDtypeStruct(log_dt.shape, F32), gb, gb))(
        a_re, a_im, log_dt, b_re_t, b_im_t, abar_re, abar_im, e_re, e_im, ga_re, ga_im, gbb_re_t, gbb_im_t)


def _block_diag(blocks_re, blocks_im, sign_im, rows_are_channels):
    both = jnp.stack([blocks_re, sign_im * blocks_im]).reshape(2, SSM_CHUNKS, 8, SSM_GROUP, SSM_STATE)
    eye = jnp.eye(8, dtype=F32)
    if rows_are_channels:
        return jnp.einsum("rcghp,gk->cghrkp", both, eye).reshape(SSM_CHUNKS, 128, 2 * CHUNK_STATES)
    return jnp.einsum("rcghp,gk->crkpgh", both, eye).reshape(SSM_CHUNKS, 2 * CHUNK_STATES, 128)


def _block_diag_parts(mat, rows_are_channels):
    if rows_are_channels:
        six = mat.reshape(SSM_CHUNKS, 8, SSM_GROUP, 2, 8, SSM_STATE)
        parts = jnp.einsum("cghrgp->rcghp", six)
    else:
        six = mat.reshape(SSM_CHUNKS, 2, 8, SSM_STATE, 8, SSM_GROUP)
        parts = jnp.einsum("crgpgh->rcghp", six)
    parts = parts.reshape(2, SSM_GROUPS, SSM_GROUP, SSM_STATE)
    return parts[0], parts[1]


def _scan_consts(a_ref, conj, reverse):
    ar = jnp.broadcast_to(a_ref[:, :CHUNK_STATES], (SCAN_ROWS, CHUNK_STATES))
    ai = jnp.broadcast_to(a_ref[:, CHUNK_STATES:], (SCAN_ROWS, CHUNK_STATES))
    if conj:
        ai = -ai
    row = lax.broadcasted_iota(jnp.int32, (SCAN_ROWS, CHUNK_STATES), 0)
    if reverse:
        row = SCAN_ROWS - 1 - row
    zero = jnp.zeros_like(ar)
    steps = []
    pr, pi = ar, ai
    for shift in (1, 2, 4):
        keep = row >= shift
        steps.append((SCAN_ROWS - shift if reverse else shift, jnp.where(keep, pr, zero), jnp.where(keep, pi, zero)))
        pr, pi = _cmul(pr, pi, pr, pi)
    first = row == 0
    return steps, (jnp.where(first, ar, zero), jnp.where(first, ai, zero)), first


def _scan_tile(xr, xi, prev_r, prev_i, steps, carry_in, reverse):
    edge = SCAN_ROWS - 1 if reverse else 1
    cr, ci = pltpu.roll(prev_r, edge, axis=0), pltpu.roll(prev_i, edge, axis=0)
    xr, xi = xr + carry_in[0] * cr - carry_in[1] * ci, xi + carry_in[0] * ci + carry_in[1] * cr
    for shift, mr, mi in steps:
        sr, si = pltpu.roll(xr, shift, axis=0), pltpu.roll(xi, shift, axis=0)
        xr, xi = xr + mr * sr - mi * si, xi + mr * si + mi * sr
    return xr, xi


MM_ROWS = 256


def _ssm_fwd(proj, bmat, cmat, a_chunks, d_skip, ride=None):
    def body(u_ref, b_ref, c_ref, a_ref, d_ref, y_ref, h_ref):
        for i in range(SEQ // MM_ROWS):
            rows = pl.ds(i * MM_ROWS, MM_ROWS)
            h_ref[rows, :] = _dot(u_ref[rows, :].astype(BF16), b_ref[...])
        steps, carry_in, _ = _scan_consts(a_ref, conj=False, reverse=False)

        def tile(k, carry):
            rows = pl.ds(pl.multiple_of(k * SCAN_ROWS, SCAN_ROWS), SCAN_ROWS)
            xr, xi = _scan_tile(h_ref[rows, :CHUNK_STATES], h_ref[rows, CHUNK_STATES:], carry[0], carry[1], steps, carry_in, False)
            h_ref[rows, :CHUNK_STATES] = xr
            h_ref[rows, CHUNK_STATES:] = xi
            return xr, xi

        zero = jnp.zeros((SCAN_ROWS, CHUNK_STATES), F32)
        lax.fori_loop(0, SEQ // SCAN_ROWS, tile, (zero, zero), unroll=4)
        for i in range(SEQ // MM_ROWS):
            rows = pl.ds(i * MM_ROWS, MM_ROWS)
            y_ref[rows, :] = _dot(h_ref[rows, :].astype(BF16), c_ref[...]) + d_ref[...] * u_ref[rows, :]

    return _call(
        body, "ssm_fwd", (SSM_CHUNKS,),
        [pl.BlockSpec((SEQ, 128), lambda c: (0, U_COL + c)),
         pl.BlockSpec((None, 128, 2 * CHUNK_STATES), lambda c: (c, 0, 0)),
         pl.BlockSpec((None, 2 * CHUNK_STATES, 128), lambda c: (c, 0, 0)),
         pl.BlockSpec((None, 1, 2 * CHUNK_STATES), lambda c: (c, 0, 0)),
         pl.BlockSpec((1, 128), lambda c: (0, c))],
        [pl.BlockSpec((SEQ, 128), lambda c: (0, c)), pl.BlockSpec((SEQ, 2 * CHUNK_STATES), lambda c: (0, c))],
        [_sds((SEQ, SSM_WIDTH), F32), _sds((SEQ, SSM_CHUNKS * 2 * CHUNK_STATES), F32)], [],
        [proj, bmat, cmat, a_chunks, d_skip], ride)


def _ssm_bwd(dys, proj, h, bmat, cmat, a_chunks, d_skip, dproj, ride=None):
    def body(dy_ref, u_ref, h_ref, b_ref, c_ref, a_ref, d_ref, dproj_in, du_ref, db_ref, dc_ref, da_ref, dd_ref, g_ref):
        del dproj_in
        dsum = jnp.zeros((1, 128), F32)
        dcm = jnp.zeros((2 * CHUNK_STATES, 128), F32)
        for i in range(SEQ // MM_ROWS):
            rows = pl.ds(i * MM_ROWS, MM_ROWS)
            dy = dy_ref[rows, :]
            g_ref[rows, :] = _dot_nt(dy.astype(BF16), c_ref[...])
            dsum += jnp.sum(dy * u_ref[rows, :], axis=0, keepdims=True)
            dcm += _dot_tn(h_ref[rows, :].astype(BF16), dy.astype(BF16))
        dd_ref[...] = dsum
        dc_ref[...] = dcm
        steps, carry_in, _ = _scan_consts(a_ref, conj=True, reverse=True)
        first_row = lax.broadcasted_iota(jnp.int32, (SCAN_ROWS, CHUNK_STATES), 0) == 0
        n_tiles = SEQ // SCAN_ROWS

        def tile(j, carry):
            k = n_tiles - 1 - j
            rows = pl.ds(pl.multiple_of(k * SCAN_ROWS, SCAN_ROWS), SCAN_ROWS)
            before = pl.ds(pl.multiple_of(jnp.maximum(k - 1, 0) * SCAN_ROWS, SCAN_ROWS), SCAN_ROWS)
            gr, gi = _scan_tile(g_ref[rows, :CHUNK_STATES], g_ref[rows, CHUNK_STATES:], carry[0], carry[1], steps, carry_in, True)
            g_ref[rows, :CHUNK_STATES] = gr
            g_ref[rows, CHUNK_STATES:] = gi
            has_before = jnp.where(k > 0, 1.0, 0.0)
            hr = jnp.where(first_row, pltpu.roll(h_ref[before, :CHUNK_STATES], 1, axis=0) * has_before,
                           pltpu.roll(h_ref[rows, :CHUNK_STATES], 1, axis=0))
            hi = jnp.where(first_row, pltpu.roll(h_ref[before, CHUNK_STATES:], 1, axis=0) * has_before,
                           pltpu.roll(h_ref[rows, CHUNK_STATES:], 1, axis=0))
            return gr, gi, carry[2] + hr * gr + hi * gi, carry[3] + hr * gi - hi * gr

        zero = jnp.zeros((SCAN_ROWS, CHUNK_STATES), F32)
        _, _, sar, sai = lax.fori_loop(0, n_tiles, tile, (zero, zero, zero, zero), unroll=4)
        da_ref[:, :CHUNK_STATES] = jnp.sum(sar, axis=0, keepdims=True)
        da_ref[:, CHUNK_STATES:] = jnp.sum(sai, axis=0, keepdims=True)
        dbm = jnp.zeros((128, 2 * CHUNK_STATES), F32)
        for i in range(SEQ // MM_ROWS):
            rows = pl.ds(i * MM_ROWS, MM_ROWS)
            g = g_ref[rows, :].astype(BF16)
            du_ref[rows, :] = (_dot_nt(g, b_ref[...]) + d_ref[...] * dy_ref[rows, :]).astype(BF16)
            dbm += _dot_tn(u_ref[rows, :].astype(BF16), g)
        db_ref[...] = dbm

    chunk_col = pl.BlockSpec((SEQ, 128), lambda c: (0, c))
    return _call(
        body, "ssm_bwd", (SSM_CHUNKS,),
        [chunk_col,
         pl.BlockSpec((SEQ, 128), lambda c: (0, U_COL + c)),
         pl.BlockSpec((SEQ, 2 * CHUNK_STATES), lambda c: (0, c)),
         pl.BlockSpec((None, 128, 2 * CHUNK_STATES), lambda c: (c, 0, 0)),
         pl.BlockSpec((None, 2 * CHUNK_STATES, 128), lambda c: (c, 0, 0)),
         pl.BlockSpec((None, 1, 2 * CHUNK_STATES), lambda c: (c, 0, 0)),
         pl.BlockSpec((1, 128), lambda c: (0, c)), ANY],
        [pl.BlockSpec((SEQ, 128), lambda c: (0, U_COL + c)),
         pl.BlockSpec((None, 128, 2 * CHUNK_STATES), lambda c: (c, 0, 0)),
         pl.BlockSpec((None, 2 * CHUNK_STATES, 128), lambda c: (c, 0, 0)),
         pl.BlockSpec((None, 1, 2 * CHUNK_STATES), lambda c: (c, 0, 0)),
         pl.BlockSpec((1, 128), lambda c: (0, c))],
        [_sds((SEQ, IN_WIDTH), BF16), _sds((SSM_CHUNKS, 128, 2 * CHUNK_STATES), F32),
         _sds((SSM_CHUNKS, 2 * CHUNK_STATES, 128), F32), _sds((SSM_CHUNKS, 1, 2 * CHUNK_STATES), F32), _sds((1, SSM_WIDTH), F32)],
        [pltpu.VMEM((SEQ, 2 * CHUNK_STATES), F32)], [dys, proj, h, bmat, cmat, a_chunks, d_skip, dproj], ride, aliases={7: 0})


def _ssm_tables(abar_re, abar_im, bbar_re_t, bbar_im_t, c_re, c_im):
    bmat = _block_diag(bbar_re_t, bbar_im_t, 1.0, True).astype(BF16)
    cmat = _block_diag(c_re, c_im, -1.0, False).astype(BF16)
    a_chunks = jnp.concatenate([abar_re.reshape(SSM_CHUNKS, 1, CHUNK_STATES), abar_im.reshape(SSM_CHUNKS, 1, CHUNK_STATES)], axis=2)
    return bmat, cmat, a_chunks


GL_COL = (3 * QKV_WIDTH + SSM_WIDTH) // D_MODEL
GELU_C = math.sqrt(2.0 / math.pi)
GELU_A = 0.044715


def _sds(shape, dtype):
    return jax.ShapeDtypeStruct(shape, dtype)


def _gelu(x):
    t = jnp.tanh(GELU_C * (x + GELU_A * x * x * x))
    return 0.5 * x * (1.0 + t), t


def _gelu_grad(x, t):
    return 0.5 * (1.0 + t) + 0.5 * x * (1.0 - t * t) * GELU_C * (1.0 + 3.0 * GELU_A * x * x)


def _layer_norm(r, g, b):
    mu = jnp.mean(r, axis=-1, keepdims=True)
    xc = r - mu
    rstd = lax.rsqrt(jnp.mean(xc * xc, axis=-1, keepdims=True) + LN_EPS)
    xhat = xc * rstd
    return xhat * g + b, xhat, rstd


def _layer_norm_bwd(dy, xhat, rstd, g):
    dxhat = dy * g
    m1 = jnp.mean(dxhat, axis=-1, keepdims=True)
    m2 = jnp.mean(dxhat * xhat, axis=-1, keepdims=True)
    return rstd * (dxhat - m1 - xhat * m2)


def _proj(x, w_in, ride=None):
    tm, tn = 1024, 1792

    def body(x_ref, w_ref, o_ref):
        o_ref[...] = _dot(x_ref[...].astype(BF16), _side_by_side(w_ref))

    return _call(
        body, "proj", (SEQ // tm, IN_WIDTH // tn),
        [pl.BlockSpec((tm, D_MODEL), lambda i, j: (i, 0)), pl.BlockSpec((2, D_MODEL, tn // 2), lambda i, j: (j, 0, 0))],
        [pl.BlockSpec((tm, tn), lambda i, j: (i, j))], [_sds((SEQ, IN_WIDTH), F32)], [], [x, w_in], ride)


def _row_spec(tm, width, col=0):
    return pl.BlockSpec((tm, width), lambda i, col=col: (i, col))


def _full_spec(shape):
    return pl.BlockSpec(shape, lambda i: (0,) * len(shape))


def _weight_spec(shape):
    return pl.BlockSpec(shape, lambda i: (0,) * len(shape), pipeline_mode=pl.Buffered(1))


def _mixer_out(attn, ys, proj, x, w_ab, w_sb, w_glu, w_out, b_gate, ln_g, ln_b, ride=None):
    tm = 512

    def body(attn_ref, ys_ref, gl0_ref, gl1_ref, x_ref, wab_ref, wsb_ref, wglu_ref, wout_ref, bg_ref, g_ref, b_ref,
             h_ref, xhat_ref, rstd_ref, glu_ref, ya_ref, yssm_ref):
        gy, _ = _gelu(ys_ref[...])
        glu = _dot(gy.astype(BF16), _side_by_side(wglu_ref))
        glu_ref[...] = glu
        y_s = glu[:, :SSM_WIDTH] * jax.nn.sigmoid(glu[:, SSM_WIDTH:])
        y_ssm = _dot(y_s.astype(BF16), _side_by_side(wsb_ref))
        y_attn = _dot(attn_ref[...].astype(BF16), _side_by_side(wab_ref))
        ya_ref[...] = y_attn
        yssm_ref[...] = y_ssm
        g0 = jax.nn.sigmoid(gl0_ref[...] + _side_by_side(bg_ref, 0))
        g1 = jax.nn.sigmoid(gl1_ref[...] + _side_by_side(bg_ref, 1))
        mixed = g0 * y_attn + g1 * y_ssm
        r1 = DN_ALPHA * x_ref[...] + _dot(mixed.astype(BF16), wout_ref[...])
        h, xhat, rstd = _layer_norm(r1, g_ref[...], b_ref[...])
        h_ref[...] = h
        xhat_ref[...] = xhat
        rstd_ref[...] = jnp.broadcast_to(rstd, (tm, 128))

    wide = _sds((SEQ, D_MODEL), F32)
    return _call(
        body, "mixer_out", (SEQ // tm,),
        [_row_spec(tm, ATTN_WIDTH), _row_spec(tm, SSM_WIDTH), _row_spec(tm, D_MODEL, GL_COL), _row_spec(tm, D_MODEL, GL_COL + 1),
         _row_spec(tm, D_MODEL), _weight_spec((N_DEV, ATTN_WIDTH, 128)), _weight_spec((N_DEV, SSM_WIDTH, 128)),
         _weight_spec((N_DEV, SSM_WIDTH, 128)), _weight_spec((D_MODEL, D_MODEL)), _full_spec((N_DEV, 2, 128)),
         _full_spec((1, D_MODEL)), _full_spec((1, D_MODEL))],
        [_row_spec(tm, D_MODEL), _row_spec(tm, D_MODEL), _row_spec(tm, 128), _row_spec(tm, D_MODEL),
         _row_spec(tm, D_MODEL), _row_spec(tm, D_MODEL)],
        [wide, wide, _sds((SEQ, 128), F32), wide, wide, wide], [],
        [attn, ys, proj, proj, x, w_ab, w_sb, w_glu, w_out, b_gate, ln_g, ln_b], ride)


def _ff_up(h, w_gate, w_up, ride=None):
    tm, tn = 1024, 768

    def body(h_ref, wg_ref, wu_ref, a_ref, b_ref, f_ref):
        hb = h_ref[...].astype(BF16)
        a, b = _dot(hb, _side_by_side(wg_ref)), _dot(hb, _side_by_side(wu_ref))
        a_ref[...] = a.astype(BF16)
        b_ref[...] = b.astype(BF16)
        f_ref[...] = (a * jax.nn.sigmoid(a) * b).astype(BF16)

    tile = pl.BlockSpec((tm, tn), lambda i, j: (i, j))
    wtile = pl.BlockSpec((tn // FF_PAD, D_MODEL, FF_PAD), lambda i, j: (j, 0, 0))
    out = _sds((SEQ, D_FF_PAD), BF16)
    return _call(body, "ff_up", (SEQ // tm, D_FF_PAD // tn), [pl.BlockSpec((tm, D_MODEL), lambda i, j: (i, 0)), wtile, wtile],
                 [tile, tile, tile], [out, out, out], [], [h, w_gate, w_up], ride)


def _ff_down_loss(f, w_down, h, target, ln_g, ln_b):
    tm = 512

    def body(f_ref, w_ref, h_ref, t_ref, g_ref, b_ref, dr_ref, dg_ref, db_ref, loss_ref):
        @pl.when(pl.program_id(0) == 0)
        def _():
            dg_ref[...] = jnp.zeros_like(dg_ref)
            db_ref[...] = jnp.zeros_like(db_ref)
            loss_ref[...] = jnp.zeros_like(loss_ref)

        r2 = DN_ALPHA * h_ref[...] + _dot(f_ref[...], w_ref[...])
        g = g_ref[...]
        out, xhat, rstd = _layer_norm(r2, g, b_ref[...])
        err = out - t_ref[...]
        loss_ref[...] += 0.5 * jnp.sum(jnp.mean(err * err, axis=-1, keepdims=True), axis=0, keepdims=True)
        dout = err * (1.0 / D_MODEL)
        dg_ref[...] += jnp.sum(dout * xhat, axis=0, keepdims=True)
        db_ref[...] += jnp.sum(dout, axis=0, keepdims=True)
        dr_ref[...] = _layer_norm_bwd(dout, xhat, rstd, g)

    vec = _sds((1, D_MODEL), F32)
    return _pallas_call(
        body, name="ff_down_loss", grid=(SEQ // tm,),
        in_specs=[_row_spec(tm, D_FF_PAD), _weight_spec((D_FF_PAD, D_MODEL)), _row_spec(tm, D_MODEL), _row_spec(tm, D_MODEL),
                  _full_spec((1, D_MODEL)), _full_spec((1, D_MODEL))],
        out_specs=(_row_spec(tm, D_MODEL), _full_spec((1, D_MODEL)), _full_spec((1, D_MODEL)), _full_spec((1, 128))),
        out_shape=(_sds((SEQ, D_MODEL), F32), vec, vec, _sds((1, 128), F32)),
        compiler_params=_cparams(dimension_semantics=("arbitrary",)),
    )(f, w_down, h, target, ln_g, ln_b)


def _ff_down_bwd(dr2, w_down, a, b):
    tm, tn = 1024, 768

    def body(dr_ref, w_ref, a_ref, b_ref, da_ref, db_ref):
        df = _dot_nt(dr_ref[...].astype(BF16), w_ref[...])
        av, bv = a_ref[...].astype(F32), b_ref[...].astype(F32)
        sg = jax.nn.sigmoid(av)
        da_ref[...] = (df * bv * sg * (1.0 + av * (1.0 - sg))).astype(BF16)
        db_ref[...] = (df * av * sg).astype(BF16)

    tile = pl.BlockSpec((tm, tn), lambda i, j: (i, j))
    out = _sds((SEQ, D_FF_PAD), BF16)
    return _pallas_call(
        body, name="ff_down_bwd", grid=(SEQ // tm, D_FF_PAD // tn),
        in_specs=[pl.BlockSpec((tm, D_MODEL), lambda i, j: (i, 0)), pl.BlockSpec((tn, D_MODEL), lambda i, j: (j, 0)), tile, tile],
        out_specs=(tile, tile), out_shape=(out, out),
        compiler_params=_cparams(dimension_semantics=("arbitrary", "arbitrary")),
    )(dr2, w_down, a, b)


def _ff_up_bwd(da, db, w_gate, w_up, dr2, xhat1, rstd1, ln_g, ride=None):
    tm, tk = 1024, 768
    nk = D_FF_PAD // tk

    def body(da_ref, db_ref, wg_ref, wu_ref, dr2_ref, xhat_ref, rstd_ref, g_ref, dr1_ref, dg_ref, dbias_ref, acc):
        i, k = pl.program_id(0), pl.program_id(1)

        @pl.when(jnp.logical_and(i == 0, k == 0))
        def _():
            dg_ref[...] = jnp.zeros_like(dg_ref)
            dbias_ref[...] = jnp.zeros_like(dbias_ref)

        part = _dot_nt(da_ref[...], _side_by_side(wg_ref)) + _dot_nt(db_ref[...], _side_by_side(wu_ref))

        @pl.when(k == 0)
        def _():
            acc[...] = part

        @pl.when(k > 0)
        def _():
            acc[...] += part

        @pl.when(k == nk - 1)
        def _():
            dh = DN_ALPHA * dr2_ref[...] + acc[...]
            xhat = xhat_ref[...]
            dg_ref[...] += jnp.sum(dh * xhat, axis=0, keepdims=True)
            dbias_ref[...] += jnp.sum(dh, axis=0, keepdims=True)
            rstd = jnp.max(rstd_ref[...], axis=1, keepdims=True)
            dr1_ref[...] = _layer_norm_bwd(dh, xhat, rstd, g_ref[...])

    hid = pl.BlockSpec((tm, tk), lambda i, k: (i, k))
    wtile = pl.BlockSpec((tk // FF_PAD, D_MODEL, FF_PAD), lambda i, k: (k, 0, 0))
    row = pl.BlockSpec((tm, D_MODEL), lambda i, k: (i, 0))
    vec = pl.BlockSpec((1, D_MODEL), lambda i, k: (0, 0))
    return _call(
        body, "ff_up_bwd", (SEQ // tm, nk),
        [hid, hid, wtile, wtile, row, row, pl.BlockSpec((tm, 128), lambda i, k: (i, 0)), vec],
        [row, vec, vec], [_sds((SEQ, D_MODEL), F32), _sds((1, D_MODEL), F32), _sds((1, D_MODEL), F32)],
        [pltpu.VMEM((tm, D_MODEL), F32)], [da, db, w_gate, w_up, dr2, xhat1, rstd1, ln_g], ride)


def _mixer_bwd(dr1, proj, y_attn, y_ssm, glu, ys, w_ab, w_sb, w_glu, w_out, b_gate):
    tm = 256

    def body(dr1_ref, gl0_ref, gl1_ref, ya_ref, yssm_ref, glu_ref, ys_ref, wab_ref, wsb_ref, wglu_ref, wout_ref, bg_ref,
             dya_ref, dyssm_ref, dgl_ref, dattn_ref, dglu_ref, dys_ref, mixed_ref, ysb_ref, gy_ref, dbg_ref):
        @pl.when(pl.program_id(0) == 0)
        def _():
            dbg_ref[...] = jnp.zeros_like(dbg_ref)

        dmixed = _dot_nt(dr1_ref[...].astype(BF16), wout_ref[...])
        g0 = jax.nn.sigmoid(gl0_ref[...] + _side_by_side(bg_ref, 0))
        g1 = jax.nn.sigmoid(gl1_ref[...] + _side_by_side(bg_ref, 1))
        y_attn, y_ssm = ya_ref[...], yssm_ref[...]
        mixed_ref[...] = (g0 * y_attn + g1 * y_ssm).astype(BF16)
        dya = (dmixed * g0).astype(BF16)
        dyssm = (dmixed * g1).astype(BF16)
        dya_ref[...] = dya
        dyssm_ref[...] = dyssm
        dgl0 = dmixed * y_attn * g0 * (1.0 - g0)
        dgl1 = dmixed * y_ssm * g1 * (1.0 - g1)
        dgl_ref[:, :GL_COL * D_MODEL] = jnp.zeros((tm, GL_COL * D_MODEL), BF16)
        dgl_ref[:, GL_COL * D_MODEL:(GL_COL + 1) * D_MODEL] = dgl0.astype(BF16)
        dgl_ref[:, (GL_COL + 1) * D_MODEL:] = dgl1.astype(BF16)
        dbg_ref[:, :D_MODEL] += jnp.sum(dgl0, axis=0, keepdims=True)
        dbg_ref[:, D_MODEL:] += jnp.sum(dgl1, axis=0, keepdims=True)
        dattn_ref[...] = _dot_nt(dya, _side_by_side(wab_ref))
        dy_s = _dot_nt(dyssm, _side_by_side(wsb_ref))
        glu = glu_ref[...]
        glu1, sg = glu[:, :SSM_WIDTH], jax.nn.sigmoid(glu[:, SSM_WIDTH:])
        ysb_ref[...] = (glu1 * sg).astype(BF16)
        dglu1 = (dy_s * sg).astype(BF16)
        dglu2 = (dy_s * glu1 * sg * (1.0 - sg)).astype(BF16)
        dglu_ref[:, :SSM_WIDTH] = dglu1
        dglu_ref[:, SSM_WIDTH:] = dglu2
        dgy = _dot_nt(jnp.concatenate([dglu1, dglu2], axis=1), _side_by_side(wglu_ref))
        ys = ys_ref[...]
        gy, t = _gelu(ys)
        gy_ref[...] = gy.astype(BF16)
        dys_ref[...] = dgy * _gelu_grad(ys, t)

    wide_b, half_b = _sds((SEQ, D_MODEL), BF16), _sds((SEQ, SSM_WIDTH), BF16)
    half_f = _sds((SEQ, SSM_WIDTH), F32)
    return _pallas_call(
        body, name="mixer_bwd", grid=(SEQ // tm,),
        in_specs=[_row_spec(tm, D_MODEL), _row_spec(tm, D_MODEL, GL_COL), _row_spec(tm, D_MODEL, GL_COL + 1), _row_spec(tm, D_MODEL),
                  _row_spec(tm, D_MODEL), _row_spec(tm, D_MODEL), _row_spec(tm, SSM_WIDTH), _full_spec((N_DEV, ATTN_WIDTH, 128)),
                  _full_spec((N_DEV, SSM_WIDTH, 128)), _full_spec((N_DEV, SSM_WIDTH, 128)), _full_spec((D_MODEL, D_MODEL)),
                  _full_spec((N_DEV, 2, 128))],
        out_specs=(_row_spec(tm, D_MODEL), _row_spec(tm, D_MODEL), _row_spec(tm, IN_WIDTH), _row_spec(tm, ATTN_WIDTH),
                   _row_spec(tm, D_MODEL), _row_spec(tm, SSM_WIDTH), _row_spec(tm, D_MODEL), _row_spec(tm, SSM_WIDTH),
                   _row_spec(tm, SSM_WIDTH), _full_spec((1, 2 * D_MODEL))),
        out_shape=(wide_b, wide_b, _sds((SEQ, IN_WIDTH), BF16), half_f, wide_b, half_f, wide_b, half_b, half_b,
                   _sds((1, 2 * D_MODEL), F32)),
        compiler_params=_cparams(dimension_semantics=("arbitrary",)),
    )(dr1, proj, proj, y_attn, y_ssm, glu, ys, w_ab, w_sb, w_glu, w_out, b_gate)


def _grad_x(dproj, w_in, dr1, ride=None):
    tm, tk = 1024, 1792
    nk = IN_WIDTH // tk

    def body(dp_ref, w_ref, dr1_ref, o_ref, acc):
        k = pl.program_id(1)
        part = _dot_nt(dp_ref[...], _side_by_side(w_ref))

        @pl.when(k == 0)
        def _():
            acc[...] = part

        @pl.when(k > 0)
        def _():
            acc[...] += part

        @pl.when(k == nk - 1)
        def _():
            o_ref[...] = DN_ALPHA * dr1_ref[...] + acc[...]

    row = pl.BlockSpec((tm, D_MODEL), lambda i, k: (i, 0))
    return _call(
        body, "grad_x", (SEQ // tm, nk),
        [pl.BlockSpec((tm, tk), lambda i, k: (i, k)), pl.BlockSpec((2, D_MODEL, tk // 2), lambda i, k: (k, 0, 0)), row],
        [row], [_sds((SEQ, D_MODEL), F32)], [pltpu.VMEM((tm, D_MODEL), F32)], [dproj, w_in, dr1], ride)


def _weight_grad(a, b, name, shard_cols=None, ride=None):
    k, n = a.shape[1], b.shape[1]
    tk = min(k, 512) if shard_cols else k // N_DEV
    tn = n // 4 if shard_cols else min(n, 1024)

    def body(a_ref, b_ref, o_ref):
        grad = _dot_tn(a_ref[...].astype(BF16), b_ref[...].astype(BF16))
        if shard_cols:
            o_ref[0] = grad[:, :shard_cols].astype(BF16)
            o_ref[1] = grad[:, shard_cols:].astype(BF16)
        else:
            o_ref[...] = grad.astype(BF16)

    if shard_cols:
        out_spec = pl.BlockSpec((2, None, tk, shard_cols), lambda kk, j: (0, j, kk, 0))
        out_shape = _sds((2, 4, k, shard_cols), BF16)
    else:
        out_spec = pl.BlockSpec((None, None, tk, tn), lambda kk, j: (kk % 2, kk // 2, 0, j))
        out_shape = _sds((2, 4, tk, n), BF16)
    out = _call(body, name, (k // tk, n // tn),
                [pl.BlockSpec((SEQ, tk), lambda kk, j: (0, kk)), pl.BlockSpec((SEQ, tn), lambda kk, j: (0, j))],
                [out_spec], [out_shape], [], [a, b], ride)
    return out[0] if ride is None else out


MESH = pl.DeviceIdType.MESH
ANY = pl.BlockSpec(memory_space=pl.ANY)


def _place():
    return lax.axis_index("x"), lax.axis_index("y"), lax.axis_index("c")


def _other_chips(x, y):
    return [(1 - x, y), (x, 1 - y), (1 - x, 1 - y)]


class _Ride:
    def __init__(self, operands, results, aliases, sems, start, wait):
        self.operands, self.results, self.aliases, self.sems = list(operands), list(results), dict(aliases), list(sems)
        self.start, self.wait = start, wait

    def __add__(self, other):
        n_in, n_out, n_sem = len(self.operands), len(self.results), len(self.sems)

        def both(which):
            def run(ins, outs, sems):
                getattr(self, which)(ins[:n_in], outs[:n_out], sems[:n_sem])
                getattr(other, which)(ins[n_in:], outs[n_out:], sems[n_sem:])
            return run

        aliases = {**self.aliases, **{n_in + i: n_out + j for i, j in other.aliases.items()}}
        return _Ride(self.operands + other.operands, self.results + other.results, aliases, self.sems + other.sems,
                     both("start"), both("wait"))


def _call(body, name, grid, in_specs, out_specs, out_shape, scratch_shapes, operands, ride=None, aliases=None):
    in_specs, out_specs, out_shape = list(in_specs), list(out_specs), list(out_shape)
    scratch_shapes, operands, aliases = list(scratch_shapes), list(operands), dict(aliases or {})
    kernel_body = body
    if ride is not None:
        n_in, n_out, n_scr, r_in, r_out = len(in_specs), len(out_specs), len(scratch_shapes), len(ride.operands), len(ride.results)

        def kernel_body(*refs):
            out0, scr0 = n_in + r_in, n_in + r_in + n_out + r_out
            ride_refs = (refs[n_in:out0], refs[out0 + n_out:scr0], refs[scr0 + n_scr:])
            ids = [pl.program_id(i) for i in range(len(grid))]
            first = functools.reduce(jnp.logical_and, [i == 0 for i in ids])
            last = functools.reduce(jnp.logical_and, [i == g - 1 for i, g in zip(ids, grid)])

            @pl.when(first)
            def _():
                ride.start(*ride_refs)

            body(*refs[:n_in], *refs[out0:out0 + n_out], *refs[scr0:scr0 + n_scr])

            @pl.when(last)
            def _():
                ride.wait(*ride_refs)

        aliases.update({n_in + i: n_out + j for i, j in ride.aliases.items()})
        in_specs += [ANY] * r_in
        out_specs += [ANY] * r_out
        out_shape += ride.results
        scratch_shapes += ride.sems
        operands += ride.operands
    return _pallas_call(
        kernel_body, name=name, grid=grid, in_specs=in_specs, out_specs=out_specs, out_shape=out_shape,
        scratch_shapes=scratch_shapes, input_output_aliases=aliases,
        compiler_params=_cparams(dimension_semantics=("arbitrary",) * len(grid)),
    )(*operands)


def _gather_first_level(shards):
    n = len(shards)

    def copies(ins, outs, sems, landed):
        send_sems, recv_sems, local_sems = sems
        x, y, c = _place()
        peers = [(x, y, 1 - c)] + [(px, py, c) for px, py in _other_chips(x, y)]

        def row(peer):
            return 4 * x + 2 * y + c if not landed else 4 * peer[0] + 2 * peer[1] + peer[2]

        local = [pltpu.make_async_copy(ins[a], outs[a].at[4 * x + 2 * y + c], local_sems.at[a]) for a in range(n)]
        remote = [pltpu.make_async_remote_copy(
            src_ref=ins[a], dst_ref=outs[a].at[row(peer)], send_sem=send_sems.at[a, k], recv_sem=recv_sems.at[a, k],
            device_id=peer, device_id_type=MESH) for a in range(n) for k, peer in enumerate(peers)]
        return local, remote

    def start(ins, outs, sems):
        local, remote = copies(ins, outs, sems, False)
        for cp in local + remote:
            cp.start()

    def wait(ins, outs, sems):
        local, sent = copies(ins, outs, sems, False)
        for cp in copies(ins, outs, sems, True)[1]:
            cp.wait_recv()
        for cp in sent:
            cp.wait_send()
        for cp in local:
            cp.wait()

    return _Ride(shards, [_sds((N_DEV,) + s.shape, s.dtype) for s in shards], {},
                 [pltpu.SemaphoreType.DMA((n, 4)), pltpu.SemaphoreType.DMA((n, 4)), pltpu.SemaphoreType.DMA((n,))], start, wait)


def _gather_second_level(buffers):
    n = len(buffers)

    def copies(outs, sems, core):
        send_sems, recv_sems = sems
        x, y, c = _place()
        return [pltpu.make_async_remote_copy(
            src_ref=outs[a].at[4 * px + 2 * py + core], dst_ref=outs[a].at[4 * px + 2 * py + core], send_sem=send_sems.at[a, j],
            recv_sem=recv_sems.at[a, j], device_id=(x, y, 1 - c), device_id_type=MESH)
            for a in range(n) for j, (px, py) in enumerate(_other_chips(x, y))]

    def start(ins, outs, sems):
        for cp in copies(outs, sems, lax.axis_index("c")):
            cp.start()

    def wait(ins, outs, sems):
        for cp in copies(outs, sems, 1 - lax.axis_index("c")):
            cp.wait_recv()
        for cp in copies(outs, sems, lax.axis_index("c")):
            cp.wait_send()

    return _Ride(buffers, [_sds(b.shape, b.dtype) for b in buffers], {i: i for i in range(n)},
                 [pltpu.SemaphoreType.DMA((n, 3)), pltpu.SemaphoreType.DMA((n, 3))], start, wait)


def _relayed_gather(shards):
    n = len(shards)
    buffers = [_sds((N_DEV,) + s.shape, s.dtype) for s in shards]
    dma = pltpu.SemaphoreType.DMA

    def remote(src, dst, send_sem, recv_sem, to):
        return pltpu.make_async_remote_copy(src_ref=src, dst_ref=dst, send_sem=send_sem, recv_sem=recv_sem,
                                            device_id=to, device_id_type=MESH)

    def row(px, py, pc):
        return 4 * px + 2 * py + pc

    def ride(operands, aliases, sems, copies):
        def start(ins, outs, sem_refs):
            local, sent = copies(ins, outs, sem_refs, False)
            for cp in local + sent:
                cp.start()

        def wait(ins, outs, sem_refs):
            local, sent = copies(ins, outs, sem_refs, False)
            for cp in copies(ins, outs, sem_refs, True)[1]:
                cp.wait_recv()
            for cp in sent:
                cp.wait_send()
            for cp in local:
                cp.wait()

        return _Ride(operands, buffers, aliases, sems, start, wait)

    def first(ins, outs, sems, landed):
        x, y, c = _place()
        peers = [(x, y, 1 - c), (1 - x, y, c), (x, 1 - y, c)]
        local = [pltpu.make_async_copy(ins[a], outs[a].at[row(x, y, c)], sems[2].at[a]) for a in range(n)]
        return local, [remote(ins[a], outs[a].at[row(*peer) if landed else row(x, y, c)], sems[0].at[a, k], sems[1].at[a, k], peer)
                       for a in range(n) for k, peer in enumerate(peers)]

    def second(ins, outs, sems, landed):
        x, y, c = _place()
        mine = 1 - c if landed else c
        copies = []
        for a in range(n):
            half = shards[a].shape[0] // 2
            over_x, over_y, diagonal = outs[a].at[row(1 - x, y, mine)], outs[a].at[row(x, 1 - y, mine)], outs[a].at[row(1 - x, 1 - y, c)]
            lower, upper = pl.ds(0, half), pl.ds(half, half)
            copies += [remote(over_x, over_x, sems[0].at[a, 0], sems[1].at[a, 0], (x, y, 1 - c)),
                       remote(over_y, over_y, sems[0].at[a, 1], sems[1].at[a, 1], (x, y, 1 - c))]
            if landed:
                copies += [remote(diagonal.at[lower], diagonal.at[lower], sems[0].at[a, 2], sems[1].at[a, 2], (1 - x, y, c)),
                           remote(diagonal.at[upper], diagonal.at[upper], sems[0].at[a, 3], sems[1].at[a, 3], (x, 1 - y, c))]
            else:
                copies += [remote(over_y.at[lower], over_y.at[lower], sems[0].at[a, 2], sems[1].at[a, 2], (1 - x, y, c)),
                           remote(over_x.at[upper], over_x.at[upper], sems[0].at[a, 3], sems[1].at[a, 3], (x, 1 - y, c))]
        return [], copies

    def third(ins, outs, sems, landed):
        x, y, c = _place()
        return [], [remote(outs[a].at[row(1 - x, 1 - y, 1 - c if landed else c)], outs[a].at[row(1 - x, 1 - y, 1 - c if landed else c)],
                           sems[0].at[a], sems[1].at[a], (x, y, 1 - c)) for a in range(n)]

    def later(copies, n_sems):
        return lambda partly: ride(partly, {i: i for i in range(n)}, [dma((n,) + n_sems), dma((n,) + n_sems)], copies)

    return ride(shards, {}, [dma((n, 3)), dma((n, 3)), dma((n,))], first), later(second, (4,)), later(third, ())


def _sibling_swap_ride(grads):
    n = len(grads)

    def copies(ins, outs, sems):
        x, y, c = _place()
        return [pltpu.make_async_remote_copy(
            src_ref=ins[a].at[1 - c], dst_ref=outs[a], send_sem=sems[0].at[a], recv_sem=sems[1].at[a],
            device_id=(x, y, 1 - c), device_id_type=MESH) for a in range(n)]

    def start(ins, outs, sems):
        for cp in copies(ins, outs, sems):
            cp.start()

    def wait(ins, outs, sems):
        for cp in copies(ins, outs, sems):
            cp.wait()

    return _Ride(grads, [_sds(g.shape[1:], g.dtype) for g in grads], {},
                 [pltpu.SemaphoreType.DMA((n,)), pltpu.SemaphoreType.DMA((n,))], start, wait)


def _chip_swap_ride(sums):
    n = len(sums)

    def copies(ins, outs, sems, landed):
        send_sems, recv_sems, local_sems = sems
        x, y, c = _place()
        mine = 2 * x + y
        local = [pltpu.make_async_copy(ins[a].at[mine], outs[a].at[mine], local_sems.at[a]) for a in range(n)]
        remote = [pltpu.make_async_remote_copy(
            src_ref=ins[a].at[2 * px + py], dst_ref=outs[a].at[2 * px + py if landed else mine], send_sem=send_sems.at[a, j],
            recv_sem=recv_sems.at[a, j], device_id=(px, py, c), device_id_type=MESH)
            for a in range(n) for j, (px, py) in enumerate(_other_chips(x, y))]
        return local, remote

    def start(ins, outs, sems):
        local, remote = copies(ins, outs, sems, False)
        for cp in local + remote:
            cp.start()

    def wait(ins, outs, sems):
        local, sent = copies(ins, outs, sems, False)
        for cp in copies(ins, outs, sems, True)[1]:
            cp.wait_recv()
        for cp in sent:
            cp.wait_send()
        for cp in local:
            cp.wait()

    return _Ride(sums, [_sds(s.shape, s.dtype) for s in sums], {},
                 [pltpu.SemaphoreType.DMA((n, 3)), pltpu.SemaphoreType.DMA((n, 3)), pltpu.SemaphoreType.DMA((n,))], start, wait)


def _send_buffers(shards, name):
    n = len(shards)

    def body(*refs):
        for (w, transposed, rows, cols), w_ref, o_ref in zip(shards, refs[:n], refs[n:]):
            if transposed:
                c, r = w.shape
                padded = jnp.concatenate([w_ref[...], jnp.zeros((cols - c, r), F32)], axis=0) if cols > c else w_ref[...]
                o_ref[...] = padded.T.astype(BF16)
            else:
                r, c = w.shape
                if (r, c) != (rows, cols):
                    o_ref[...] = jnp.zeros((rows, cols), BF16)
                o_ref[:r, :c] = w_ref[...].astype(BF16)

    return _pallas_call(body, name=name, out_shape=[_sds((rows, cols), BF16) for _, _, rows, cols in shards])(
        *[w for w, _, _, _ in shards])


def _all_gather(shards, name):
    n = len(shards)
    first, second, third = _relayed_gather(shards)
    levels = [first, second(shards), third(shards)]
    counts = [len(level.sems) for level in levels]

    def body(*refs):
        ins, outs, sems = refs[:n], refs[n:2 * n], refs[2 * n:]
        for i, level in enumerate(levels):
            mine = sems[sum(counts[:i]):sum(counts[:i + 1])]
            level.start(ins, outs, mine)
            level.wait(ins, outs, mine)

    return _pallas_call(
        body, name=name, in_specs=[ANY] * n, out_specs=[ANY] * n, out_shape=first.results,
        scratch_shapes=[s for level in levels for s in level.sems],
    )(*shards)


def _swap_with_sibling(grads, name):
    n = len(grads)

    def body(*refs):
        ins, outs = refs[:n], refs[n:2 * n]
        send_sems, recv_sems = refs[2 * n:]
        x, y, c = _place()
        copies = [pltpu.make_async_remote_copy(
            src_ref=ins[a].at[1 - c], dst_ref=outs[a], send_sem=send_sems.at[a], recv_sem=recv_sems.at[a],
            device_id=(x, y, 1 - c), device_id_type=MESH) for a in range(n)]
        for cp in copies:
            cp.start()
        for cp in copies:
            cp.wait()

    return _pallas_call(
        body, name=name, in_specs=[ANY] * n, out_specs=[ANY] * n,
        out_shape=[_sds(g.shape[1:], g.dtype) for g in grads],
        scratch_shapes=[pltpu.SemaphoreType.DMA((n,)), pltpu.SemaphoreType.DMA((n,))],
    )(*grads)


def _pair_sums(gs, rs, core, name):
    n_arrays = len(gs)

    def body(core_ref, *refs):
        for g_ref, r_ref, o_ref in zip(refs[:n_arrays], refs[n_arrays:2 * n_arrays], refs[2 * n_arrays:]):
            o_ref[...] = (g_ref[...].astype(F32) + r_ref[...].astype(F32)).astype(o_ref.dtype)

    def own(g):
        return pl.BlockSpec((None, None) + g.shape[2:], lambda p, core_ref: (core_ref[0], p, 0, 0))

    def chip(g):
        return pl.BlockSpec((None,) + g.shape[2:], lambda p, core_ref: (p, 0, 0))

    return _pallas_call(
        body, name=name,
        grid_spec=pltpu.PrefetchScalarGridSpec(
            num_scalar_prefetch=1, grid=(4,), in_specs=[own(g) for g in gs] + [chip(g) for g in gs],
            out_specs=[chip(g) for g in gs]),
        out_shape=[_sds(g.shape[1:], g.dtype) for g in gs], compiler_params=_cparams(dimension_semantics=("arbitrary",)),
    )(core, *gs, *rs)


def _adamw_math(w, g, m, v):
    m = ADAM_B1 * m + (1.0 - ADAM_B1) * g
    v = ADAM_B2 * v + (1.0 - ADAM_B2) * (g * g)
    m_hat = m / (1.0 - ADAM_B1 ** ADAM_STEP)
    v_hat = v / (1.0 - ADAM_B2 ** ADAM_STEP)
    return -ADAM_LR * (m_hat / (jnp.sqrt(v_hat) + ADAM_EPS) + ADAM_WD * w), m, v


def _adamw(w, m, v, parts, name):
    r, c = w.shape
    tr = r if r <= 512 else 256
    n_parts, pr, pc = parts.shape
    assert r % tr == 0 and (tr == r or pr == r)

    def body(w_ref, m_ref, v_ref, p_ref, g_out, d_out, m_out, v_out):
        g = p_ref[0, :tr, :c].astype(F32)
        for p in range(1, n_parts):
            g = g + p_ref[p, :tr, :c].astype(F32)
        g_out[...] = g
        d_out[...], m_out[...], v_out[...] = _adamw_math(w_ref[...], g, m_ref[...], v_ref[...])

    tile = pl.BlockSpec((tr, c), lambda i: (i, 0))
    part_tile = pl.BlockSpec((n_parts, pr if tr == r else tr, pc), lambda i: (0, i, 0))
    out = _sds((r, c), F32)
    return _pallas_call(
        body, name=name, grid=(r // tr,), in_specs=[tile, tile, tile, part_tile], out_specs=(tile,) * 4,
        out_shape=(out,) * 4, compiler_params=_cparams(dimension_semantics=("arbitrary",)),
    )(w, m, v, parts)


def _adamw_many(weights, name, ride=None):
    steps = 4
    in_specs, out_specs, out_shape, operands, tiles = [], [], [], [], []
    for w, m, v, parts, transposed in weights:
        n_parts, pr, pc = parts.shape
        if transposed:
            c, r = w.shape
            tile = pl.BlockSpec((c, r // steps), lambda i: (0, i))
            part_tile = pl.BlockSpec((n_parts, r // steps, pc), lambda i: (0, i, 0))
            tiles.append((c, r // steps))
        elif w.shape[0] % (8 * steps) == 0:
            r, c = w.shape
            tile = pl.BlockSpec((r // steps, c), lambda i: (i, 0))
            part_tile = pl.BlockSpec((n_parts, r // steps, pc), lambda i: (0, i, 0))
            tiles.append((r // steps, c))
        else:
            tile = pl.BlockSpec(w.shape, lambda i: (0, 0))
            part_tile = pl.BlockSpec(parts.shape, lambda i: (0, 0, 0))
            tiles.append(w.shape)
        in_specs += [tile, tile, tile, part_tile]
        out_specs += [tile] * 4
        out_shape += [_sds(w.shape, F32)] * 4
        operands += [w, m, v, parts]

    def body(*refs):
        ins, outs = refs[:4 * len(weights)], refs[4 * len(weights):]
        for k, (_, _, _, parts, transposed) in enumerate(weights):
            w_ref, m_ref, v_ref, p_ref = ins[4 * k:4 * k + 4]
            rows, cols = tiles[k]
            if transposed:
                g = p_ref[0].astype(F32)
                for p in range(1, parts.shape[0]):
                    g = g + p_ref[p].astype(F32)
                g = g.T[:rows]
            else:
                g = p_ref[0, :rows, :cols].astype(F32)
                for p in range(1, parts.shape[0]):
                    g = g + p_ref[p, :rows, :cols].astype(F32)
            g_out, d_out, m_out, v_out = outs[4 * k:4 * k + 4]
            g_out[...] = g
            d_out[...], m_out[...], v_out[...] = _adamw_math(w_ref[...], g, m_ref[...], v_ref[...])

    return _call(body, name, (steps,), in_specs, out_specs, out_shape, [], operands, ride)


SMALL = ("ssm_a_re", "ssm_a_im", "ssm_log_dt", "ssm_b_re", "ssm_b_im", "ssm_c_re", "ssm_c_im", "ssm_d",
         "ln1_g", "ln1_b", "ln2_g", "ln2_b")


def _pack_rows(arrays):
    rows = []
    for a in arrays:
        flat = a.reshape(-1)
        rows.append(jnp.pad(flat, (0, -flat.shape[0] % 128)).reshape(-1, 128))
    packed = jnp.concatenate(rows, axis=0)
    return jnp.pad(packed, ((0, -packed.shape[0] % 8), (0, 0)))


def _unpack_rows(packed, shapes):
    out, row = [], 0
    for shape in shapes:
        size = math.prod(shape)
        n_rows = -(-size // 128)
        out.append(packed[row:row + n_rows].reshape(-1)[:size].reshape(shape))
        row += n_rows
    return out


def _sum_devices(parts):
    def body(p_ref, o_ref):
        total = p_ref[0]
        for dev in range(1, N_DEV):
            total = total + p_ref[dev]
        o_ref[...] = total

    return _pallas_call(body, name="sum_devices", out_shape=_sds(parts.shape[1:], F32))(parts)


def _adamw_replicated(ws, ms, vs, gs):
    n = len(ws)

    def body(*refs):
        w_refs, m_refs, v_refs, g_refs, d_out, m_out, v_out = (refs[i * n:(i + 1) * n] for i in range(7))
        for i in range(n):
            d_out[i][...], m_out[i][...], v_out[i][...] = _adamw_math(w_refs[i][...], g_refs[i][...], m_refs[i][...], v_refs[i][...])

    out = _pallas_call(body, name="adamw_replicated", out_shape=[_sds(w.shape, F32) for w in ws] * 3,
                       compiler_params=_cparams())(*ws, *ms, *vs, *gs)
    return out[:n], out[n:2 * n], out[2 * n:]


def kernel(x, w_in, b_gate, w_attn_br, w_ssm_br, w_out, ssm_a_re, ssm_a_im, ssm_log_dt, ssm_b_re, ssm_b_im, ssm_c_re, ssm_c_im, ssm_d, w_glu, ln1_g, ln1_b, w_ff_gate, w_ff_up, w_ff_down, ln2_g, ln2_b, loss_target, m_w_in, m_b_gate, m_w_attn_br, m_w_ssm_br, m_w_out, m_ssm_a_re, m_ssm_a_im, m_ssm_log_dt, m_ssm_b_re, m_ssm_b_im, m_ssm_c_re, m_ssm_c_im, m_ssm_d, m_w_glu, m_ln1_g, m_ln1_b, m_w_ff_gate, m_w_ff_up, m_w_ff_down, m_ln2_g, m_ln2_b, v_w_in, v_b_gate, v_w_attn_br, v_w_ssm_br, v_w_out, v_ssm_a_re, v_ssm_a_im, v_ssm_log_dt, v_ssm_b_re, v_ssm_b_im, v_ssm_c_re, v_ssm_c_im, v_ssm_d, v_w_glu, v_ln1_g, v_ln1_b, v_w_ff_gate, v_w_ff_up, v_w_ff_down, v_ln2_g, v_ln2_b):
    given = dict(locals())
    x2, target = x[0], loss_target[0]
    core = lax.axis_index("c").astype(jnp.int32).reshape(1)

    sharded = ("w_in", "w_attn_br", "w_ssm_br", "w_glu", "w_ff_gate", "w_ff_up", "b_gate", "w_out", "w_ff_down")
    send_shape = dict(w_in=(D_MODEL, 896), w_attn_br=(ATTN_WIDTH, 128), w_ssm_br=(SSM_WIDTH, 128), w_glu=(SSM_WIDTH, 128),
                      w_out=(128, D_MODEL), w_ff_gate=(D_MODEL, FF_PAD), w_ff_up=(D_MODEL, FF_PAD), w_ff_down=(FF_PAD, D_MODEL))
    local = {k: given[k][0] for k in sharded}
    narrow = ("w_ff_gate", "w_ff_up")
    def to_send(k):
        return (local[k].T, True, *send_shape[k]) if k in narrow else (local[k], False, *send_shape[k])

    later = [k for k in sharded if k not in ("w_in", "b_gate")]
    sends = dict(zip(["w_in"] + later, _send_buffers([to_send("w_in")], "send_w_in")
                     + _send_buffers([to_send(k) for k in later], "send_weights")))
    sends["b_gate"] = local["b_gate"]
    mixer_weights = ("w_attn_br", "w_ssm_br", "w_glu", "b_gate", "w_out")
    ff_weights = ("w_ff_gate", "w_ff_up", "w_ff_down")
    wt = {}
    wt["w_in"], = _all_gather([sends["w_in"]], "gather_w_in")

    a_re, a_im, log_dt = ssm_a_re[0], ssm_a_im[0], ssm_log_dt[0].reshape(SSM_GROUPS, 1)
    b_re_t, b_im_t = ssm_b_re[0].transpose(0, 2, 1), ssm_b_im[0].transpose(0, 2, 1)
    abar_re, abar_im, e_re, e_im, bbar_re_t, bbar_im_t = _ssm_prep(a_re, a_im, log_dt, b_re_t, b_im_t)
    bmat, cmat, a_chunks = _ssm_tables(abar_re, abar_im, bbar_re_t, bbar_im_t, ssm_c_re[0], ssm_c_im[0])
    cos_t, sin_t = _rope_tables()

    big_mixer, ff_in = [k for k in mixer_weights if k != "b_gate"], ("w_ff_gate", "w_ff_up")
    n_mixer = len(big_mixer)
    mixer_1, mixer_2, mixer_3 = _relayed_gather([sends[k] for k in big_mixer])
    ff_in_1, ff_in_2, ff_in_3 = _relayed_gather([sends[k] for k in ff_in])
    ff_down_1, ff_down_2, ff_down_3 = _relayed_gather([sends["w_ff_down"]])
    proj, *landed = _proj(x2, wt["w_in"], mixer_1 + _gather_first_level([sends["b_gate"]]))
    mixer, bias = landed[:n_mixer], landed[n_mixer:]
    attn, lse, *landed = _attn_fwd(proj, cos_t, sin_t, mixer_2(mixer) + _gather_second_level(bias) + ff_in_1)
    mixer, b_gate_full, ff = landed[:n_mixer], landed[n_mixer], landed[n_mixer + 1:]
    ys, states, *landed = _ssm_fwd(proj, bmat, cmat, a_chunks, ssm_d, mixer_3(mixer) + ff_in_2(ff) + ff_down_1)
    wt.update(zip(big_mixer, landed[:n_mixer]))
    ff, ff_down = landed[n_mixer:n_mixer + 2], landed[n_mixer + 2:]
    wt["w_out"] = wt["w_out"].reshape(D_MODEL, D_MODEL)
    h, xhat1, rstd1, glu, y_attn, y_ssm, *landed = _mixer_out(
        attn, ys, proj, x2, wt["w_attn_br"], wt["w_ssm_br"], wt["w_glu"], wt["w_out"], b_gate_full, ln1_g, ln1_b,
        ff_in_3(ff) + ff_down_2(ff_down))
    wt.update(zip(ff_in, landed[:2]))
    ff_a, ff_b, ff_f, w_ff_down = _ff_up(h, wt["w_ff_gate"], wt["w_ff_up"], ff_down_3(landed[2:]))
    wt["w_ff_down"] = w_ff_down.reshape(D_FF_PAD, D_MODEL)
    dr2, d_ln2_g, d_ln2_b, loss_lanes = _ff_down_loss(ff_f, wt["w_ff_down"], h, target, ln2_g, ln2_b)

    def pair_sums(names, contrib, from_sibling):
        return _pair_sums([contrib[k] for k in names], from_sibling, core, "pair_sums_" + names[0])

    d_a, d_b = _ff_down_bwd(dr2, wt["w_ff_down"], ff_a, ff_b)
    contrib = dict(w_ff_gate=_weight_grad(h, d_a, "wgrad_w_ff_gate", FF_PAD),
                   w_ff_up=_weight_grad(h, d_b, "wgrad_w_ff_up", FF_PAD),
                   w_ff_down=_weight_grad(ff_f, dr2, "wgrad_w_ff_down"))
    dr1, d_ln1_g, d_ln1_b, *from_sibling = _ff_up_bwd(
        d_a, d_b, wt["w_ff_gate"], wt["w_ff_up"], dr2, xhat1, rstd1, ln1_g, _sibling_swap_ride([contrib[k] for k in ff_weights]))
    ff_sums = pair_sums(ff_weights, contrib, from_sibling)

    d_ya, d_yssm, d_proj, d_attn, d_glu, d_ys, mixed, y_s, gy, d_bg = _mixer_bwd(
        dr1, proj, y_attn, y_ssm, glu, ys, wt["w_attn_br"], wt["w_ssm_br"], wt["w_glu"], wt["w_out"], b_gate_full)
    contrib.update(w_attn_br=_weight_grad(attn, d_ya, "wgrad_w_attn_br", 128),
                   w_ssm_br=_weight_grad(y_s, d_yssm, "wgrad_w_ssm_br", 128),
                   w_glu=_weight_grad(gy, d_glu, "wgrad_w_glu", 128),
                   w_out=_weight_grad(mixed, dr1, "wgrad_w_out"),
                   b_gate=d_bg.reshape(2, 4, 2, 128).transpose(2, 1, 0, 3))
    d_proj, *landed = _attn_bwd(proj, cos_t, sin_t, attn, lse, d_attn, d_proj,
                                _chip_swap_ride(ff_sums) + _sibling_swap_ride([contrib[k] for k in mixer_weights]))
    parts = dict(zip(ff_weights, landed[:len(ff_weights)]))
    mixer_sums = pair_sums(mixer_weights, contrib, landed[len(ff_weights):])
    d_proj, d_bmat, d_cmat, d_abar, d_skip, *landed = _ssm_bwd(d_ys, proj, states, bmat, cmat, a_chunks, ssm_d, d_proj,
                                                               _chip_swap_ride(mixer_sums))
    parts.update(zip(mixer_weights, landed))

    gbb_re_t, gbb_im_t = _block_diag_parts(d_bmat, True)
    gc_re, gc_im = _block_diag_parts(d_cmat, False)
    ga_re = d_abar[:, 0, :CHUNK_STATES].reshape(SSM_GROUPS, SSM_STATE)
    ga_im = d_abar[:, 0, CHUNK_STATES:].reshape(SSM_GROUPS, SSM_STATE)
    g_a_re, g_a_im, g_log_dt, g_b_re_t, g_b_im_t = _ssm_param_bwd(
        a_re, a_im, log_dt, b_re_t, b_im_t, abar_re, abar_im, e_re, e_im, ga_re, ga_im, gbb_re_t, gbb_im_t)
    mine = [g_a_re, g_a_im, g_log_dt, g_b_re_t, g_b_im_t, gc_re, -gc_im,
            d_skip, d_ln1_g, d_ln1_b, d_ln2_g, d_ln2_b]
    small_packed = _pack_rows(mine + [loss_lanes])

    contrib["w_in"], small_partly = _weight_grad(x2, d_proj, "wgrad_w_in", 896, _gather_first_level([small_packed]))
    w_in_sum = pair_sums(["w_in"], contrib, _swap_with_sibling([contrib["w_in"]], "swap_w_in_with_sibling"))
    grad_x, parts["w_in"], every = _grad_x(d_proj, wt["w_in"], dr1,
                                          _chip_swap_ride(w_in_sum) + _gather_second_level([small_partly]))
    others = [k for k in sharded if k != "w_in"]
    updated = _adamw_many(
        [(local[k].T, given["m_" + k][0].T, given["v_" + k][0].T, parts[k], True) if k in narrow
         else (local[k], given["m_" + k][0], given["v_" + k][0], parts[k], False) for k in others], "adamw_others")
    updated += _adamw(local["w_in"], given["m_w_in"][0], given["v_w_in"][0], parts["w_in"], "adamw_w_in")

    grads, deltas, new_m, new_v = {}, {}, {}, {}
    for i, k in enumerate(others + ["w_in"]):
        out = [o.T if k in narrow else o for o in updated[4 * i:4 * i + 4]]
        grads[k], deltas[k], new_m[k], new_v[k] = (o.reshape((1,) + local[k].shape) for o in out)

    def held(k, a):
        return a.transpose(0, 1, 3, 2) if k in ("ssm_b_re", "ssm_b_im") else a

    *small_grads, loss_sum = _unpack_rows(_sum_devices(every), [held(k, given[k]).shape for k in SMALL] + [(1, 128)])
    small = _adamw_replicated([held(k, given[k]) for k in SMALL], [held(k, given["m_" + k]) for k in SMALL],
                              [held(k, given["v_" + k]) for k in SMALL], small_grads)
    for res, values in zip((grads, deltas, new_m, new_v), (small_grads,) + small):
        res.update((k, held(k, a)) for k, a in zip(SMALL, values))
    loss = loss_sum[0, 0]

    order = ("w_in", "b_gate", "w_attn_br", "w_ssm_br", "w_out", "ssm_a_re", "ssm_a_im", "ssm_log_dt", "ssm_b_re", "ssm_b_im",
             "ssm_c_re", "ssm_c_im", "ssm_d", "w_glu", "ln1_g", "ln1_b", "w_ff_gate", "w_ff_up", "w_ff_down", "ln2_g", "ln2_b")
    return (loss, grad_x[None], *[grads[k] for k in order], *[deltas[k] for k in order], *[new_m[k] for k in order],
            *[new_v[k] for k in order])
```

```python
import functools
import math

import jax
import jax.numpy as jnp
import numpy as np
from jax import lax
from jax.experimental import pallas as pl
from jax.experimental.pallas import tpu as pltpu

F32 = jnp.float32
BF16 = jnp.bfloat16

N_DEV = 8
SEQ = 2048
D_MODEL = 1024
HEAD_DIM = 64
ATTN_WIDTH = 512
QKV_WIDTH = 1536
SSM_WIDTH = 512
SSM_GROUPS = 32
SSM_GROUP = 16
SSM_STATE = 64
IN_WIDTH = 7168
D_FF = 2816
FF_SHARD = D_FF // N_DEV
FF_PAD = 384
D_FF_PAD = FF_PAD * N_DEV
DN_ALPHA = 2.0 ** 0.25
LN_EPS = 1e-5
NEG_INF = -1e30
ROPE_THETA = 10000.0
BLOCK = 128
GROUPS = ((1, 16), (4, 4), (16, 1))

ADAM_LR = 0.001
ADAM_B1 = 0.9
ADAM_B2 = 0.999
ADAM_EPS = 1e-08
ADAM_WD = 0.01
ADAM_STEP = 10

VMEM_LIMIT = 56 * 1024 * 1024


_pallas_call = pl.pallas_call


def _cparams(**kw):
    return pltpu.CompilerParams(vmem_limit_bytes=VMEM_LIMIT, **kw)


def _dot(a, b):
    return jnp.dot(a, b, preferred_element_type=F32)


def _dot_nt(a, b):
    return lax.dot_general(a, b, (((1,), (1,)), ((), ())), preferred_element_type=F32)


def _side_by_side(w_ref, row=None):
    rows = slice(None) if row is None else pl.ds(row, 1)
    return jnp.concatenate([w_ref[i, rows, :] for i in range(w_ref.shape[0])], axis=1)


def _dot_tn(a, b):
    return lax.dot_general(a, b, (((0,), (0,)), ((), ())), preferred_element_type=F32)


def _rope_tables():
    half = HEAD_DIM // 2
    inv_freq = np.float32(ROPE_THETA) ** (-np.arange(half, dtype=np.float32) / np.float32(half))
    ang = np.arange(SEQ, dtype=np.float32)[:, None] * inv_freq[None, :]
    cos, sin = np.cos(ang).astype(np.float32), np.sin(ang).astype(np.float32)
    tables = np.tile(cos, (1, 4)), np.tile(np.concatenate([-sin, sin], axis=1), (1, 2))

    def by_phase(t):
        return np.stack([t.reshape(SEQ // d, d, 128).transpose(1, 0, 2).reshape(SEQ, 128) for d, _ in GROUPS])

    return jnp.asarray(by_phase(tables[0])), jnp.asarray(by_phase(tables[1]))


def _swap_halves(x):
    lane = lax.broadcasted_iota(jnp.int32, x.shape, 1)
    return jnp.where((lane & 63) < 32, pltpu.roll(x, 96, axis=1), pltpu.roll(x, 32, axis=1))


def _group_rows(d, nb, r, i):
    src = pl.ds(i * BLOCK, BLOCK) if d == 1 else pl.ds(r + i * BLOCK * d, BLOCK, stride=d)
    return src, pl.ds((r * nb + i) * BLOCK, BLOCK)


def _attn_masks():
    a_idx = lax.broadcasted_iota(jnp.int32, (2 * BLOCK, 2 * BLOCK), 0) & (BLOCK - 1)
    c_idx = lax.broadcasted_iota(jnp.int32, (2 * BLOCK, 2 * BLOCK), 1)
    cur_ok = jnp.logical_and(c_idx >= BLOCK, c_idx - BLOCK <= a_idx)
    prev_ok = jnp.logical_and(c_idx < BLOCK, c_idx >= a_idx)
    lane = lax.broadcasted_iota(jnp.int32, (BLOCK, 128), 1)
    return cur_ok, prev_ok, lane < HEAD_DIM


def _stack_heads(t, head0):
    zero = jnp.zeros_like(t)
    return jnp.concatenate([jnp.where(head0, t, zero), jnp.where(head0, zero, t)], axis=0)


def _unstack_heads(t2, head0):
    return jnp.where(head0, t2[:BLOCK], t2[BLOCK:])


def _attn_fwd(proj, cos_t, sin_t, ride=None):
    def body(q0, q1, q2, k0, k1, k2, v0, v1, v2, cos_ref, sin_ref, attn_ref, lse_ref,
             qs, ks, vs, os_, ms, ls, acc, mnat, lnat):
        cur_ok, prev_ok, head0 = _attn_masks()
        ks[:BLOCK, :] = jnp.zeros((BLOCK, 128), BF16)
        vs[:BLOCK, :] = jnp.zeros((BLOCK, 128), BF16)
        for g, (d, nb) in enumerate(GROUPS):
            q_ref, k_ref, v_ref = (q0, q1, q2)[g], (k0, k1, k2)[g], (v0, v1, v2)[g]
            for r in range(d):
                for i in range(nb):
                    src, dst = _group_rows(d, nb, r, i)
                    below = pl.ds(dst.start + BLOCK, BLOCK)
                    c, s = cos_ref[g, dst, :], sin_ref[g, dst, :]
                    q = q_ref[src, :]
                    k = k_ref[src, :]
                    qs[dst, :] = ((q * c + _swap_halves(q) * s) * 0.125).astype(BF16)
                    ks[below, :] = (k * c + _swap_halves(k) * s).astype(BF16)
                    vs[below, :] = v_ref[src, :].astype(BF16)

            def block(b, carry, nb=nb):
                has_prev = (b & (nb - 1)) > 0
                cur = pl.ds(pl.multiple_of(b * BLOCK, BLOCK), BLOCK)
                window = pl.ds(pl.multiple_of(b * BLOCK, BLOCK), 2 * BLOCK)
                valid = jnp.logical_or(cur_ok, jnp.logical_and(prev_ok, has_prev))
                s = jnp.where(valid, _dot_nt(_stack_heads(qs[cur, :], head0), ks[window, :]), NEG_INF)
                m = jnp.max(s, axis=1, keepdims=True)
                p = jnp.exp(s - m)
                os_[cur, :] = _unstack_heads(_dot(p.astype(BF16), vs[window, :]), head0)
                ms[cur, :] = _unstack_heads(m, head0)
                ls[cur, :] = _unstack_heads(jnp.sum(p, axis=1, keepdims=True), head0)
                return carry

            lax.fori_loop(0, SEQ // BLOCK, block, 0, unroll=16)

            for r in range(d):
                for i in range(nb):
                    src, dst = _group_rows(d, nb, r, i)
                    if g == 0:
                        acc[src, :], mnat[src, :], lnat[src, :] = os_[dst, :], ms[dst, :], ls[dst, :]
                    else:
                        m_old, m_g = mnat[src, :], ms[dst, :]
                        m_new = jnp.maximum(m_old, m_g)
                        a_old, a_g = jnp.exp(m_old - m_new), jnp.exp(m_g - m_new)
                        acc[src, :] = a_old * acc[src, :] + a_g * os_[dst, :]
                        lnat[src, :] = a_old * lnat[src, :] + a_g * ls[dst, :]
                        mnat[src, :] = m_new
        for i in range(SEQ // BLOCK):
            rows = pl.ds(i * BLOCK, BLOCK)
            l = lnat[rows, :]
            attn_ref[rows, :] = acc[rows, :] / l
            lse_ref[rows, :] = mnat[rows, :] + jnp.log(l)

    def col(base):
        return pl.BlockSpec((SEQ, 128), lambda hp, base=base: (0, base + hp))

    in_specs = [col(g * 4) for g in range(3)] + [col(12 + g * 4) for g in range(3)] + [col(24 + g * 4) for g in range(3)]
    table = pl.BlockSpec((3, SEQ, 128), lambda hp: (0, 0, 0), pipeline_mode=pl.Buffered(1))
    out = pl.BlockSpec((SEQ, 128), lambda hp: (0, hp))
    return _call(
        body, "attn_fwd", (4,), in_specs + [table, table], [out, out],
        [_sds((SEQ, ATTN_WIDTH), F32), _sds((SEQ, ATTN_WIDTH), F32)],
        [pltpu.VMEM((SEQ, 128), BF16)] + [pltpu.VMEM((SEQ + BLOCK, 128), BF16)] * 2 + [pltpu.VMEM((SEQ, 128), F32)] * 6,
        [proj] * 9 + [cos_t, sin_t], ride)


def _attn_bwd_group_body(g):
    d, nb = GROUPS[g]

    def body(q_ref, k_ref, v_ref, cos_ref, sin_ref, lse_ref, dattn_ref, dsum_ref, dproj_ref,
             qs, ks, vs, dos, lss, dss, dqs, dks, dvs, stage, outs, sems):
        cur_ok, prev_ok, head0 = _attn_masks()
        ks[:BLOCK, :] = jnp.zeros((BLOCK, 128), BF16)
        vs[:BLOCK, :] = jnp.zeros((BLOCK, 128), BF16)
        dks[:BLOCK, :] = jnp.zeros((BLOCK, 128), F32)
        dvs[:BLOCK, :] = jnp.zeros((BLOCK, 128), F32)
        for r in range(d):
            for i in range(nb):
                src, dst = _group_rows(d, nb, r, i)
                below = pl.ds(dst.start + BLOCK, BLOCK)
                c, s = cos_ref[g, dst, :], sin_ref[g, dst, :]
                q = q_ref[src, :]
                k = k_ref[src, :]
                qs[dst, :] = ((q * c + _swap_halves(q) * s) * 0.125).astype(BF16)
                ks[below, :] = (k * c + _swap_halves(k) * s).astype(BF16)
                vs[below, :] = v_ref[src, :].astype(BF16)
                dos[dst, :] = dattn_ref[src, :].astype(BF16)
                dss[dst, :] = dsum_ref[src, :]
                lss[dst, :] = lse_ref[src, :]
                dks[below, :] = jnp.zeros((BLOCK, 128), F32)
                dvs[below, :] = jnp.zeros((BLOCK, 128), F32)

        def per_head_column(t):
            return jnp.concatenate([jnp.max(jnp.where(head0, t, NEG_INF), axis=1, keepdims=True),
                                    jnp.max(jnp.where(head0, NEG_INF, t), axis=1, keepdims=True)], axis=0)

        def block(b, carry):
            has_prev = (b & (nb - 1)) > 0
            cur = pl.ds(pl.multiple_of(b * BLOCK, BLOCK), BLOCK)
            window = pl.ds(pl.multiple_of(b * BLOCK, BLOCK), 2 * BLOCK)
            valid = jnp.logical_or(cur_ok, jnp.logical_and(prev_ok, has_prev))
            q2, do2 = _stack_heads(qs[cur, :], head0), _stack_heads(dos[cur, :], head0)
            kw, vw = ks[window, :], vs[window, :]
            s = jnp.where(valid, _dot_nt(q2, kw), NEG_INF)
            p = jnp.exp(s - per_head_column(lss[cur, :]))
            ds = (p * (_dot_nt(do2, vw) - per_head_column(dss[cur, :]))).astype(BF16)
            dvs[window, :] += _dot_tn(p.astype(BF16), do2)
            dks[window, :] += _dot_tn(ds, q2)
            dqs[cur, :] = _unstack_heads(_dot(ds, kw), head0)
            return carry

        lax.fori_loop(0, SEQ // BLOCK, block, 0, unroll=8)

        hp = pl.program_id(0)
        copies = []
        for kind in range(3):
            for r in range(d):
                for i in range(nb):
                    src, dst = _group_rows(d, nb, r, i)
                    below = pl.ds(dst.start + BLOCK, BLOCK)
                    if kind == 2:
                        stage[src, :] = dvs[below, :]
                    else:
                        c, s = cos_ref[g, dst, :], sin_ref[g, dst, :]
                        t = dqs[dst, :] * 0.125 if kind == 0 else dks[below, :]
                        stage[src, :] = t * c - _swap_halves(t) * s
            for i in range(SEQ // MM_ROWS):
                rows = pl.ds(i * MM_ROWS, MM_ROWS)
                outs[kind, rows, :] = stage[rows, :].astype(BF16)
            column = pl.multiple_of((kind * 12 + g * 4 + hp) * 128, 128)
            copies.append(pltpu.make_async_copy(outs.at[kind], dproj_ref.at[:, pl.ds(column, 128)], sems.at[kind]))
            copies[-1].start()
        for cp in copies:
            cp.wait()

    return body


def _attn_bwd(proj, cos_t, sin_t, attn, lse, dattn, dproj, ride=None):
    groups = [_attn_bwd_group_body(g) for g in range(3)]

    def body(q0, q1, q2, k0, k1, k2, v0, v1, v2, cos_ref, sin_ref, attn_ref, lse_ref, dattn_ref, dproj_in, dproj_ref,
             dsum, *scratch):
        del dproj_in
        head0 = _attn_masks()[2]
        for i in range(SEQ // BLOCK):
            rows = pl.ds(i * BLOCK, BLOCK)
            prod = dattn_ref[rows, :] * attn_ref[rows, :]
            d0 = jnp.sum(jnp.where(head0, prod, 0.0), axis=1, keepdims=True)
            d1 = jnp.sum(jnp.where(head0, 0.0, prod), axis=1, keepdims=True)
            dsum[rows, :] = jnp.where(head0, d0, d1)
        for g in range(3):
            groups[g]((q0, q1, q2)[g], (k0, k1, k2)[g], (v0, v1, v2)[g], cos_ref, sin_ref, lse_ref, dattn_ref, dsum,
                      dproj_ref, *scratch)

    def col(base):
        return pl.BlockSpec((SEQ, 128), lambda hp, base=base: (0, base + hp))

    table = pl.BlockSpec((3, SEQ, 128), lambda hp: (0, 0, 0), pipeline_mode=pl.Buffered(1))
    return _call(
        body, "attn_bwd", (4,),
        [col(g * 4) for g in range(3)] + [col(12 + g * 4) for g in range(3)] + [col(24 + g * 4) for g in range(3)]
        + [table, table, col(0), col(0), col(0), ANY],
        [ANY], [_sds((SEQ, IN_WIDTH), BF16)],
        [pltpu.VMEM((SEQ, 128), F32)]
        + [pltpu.VMEM((SEQ, 128), BF16)] + [pltpu.VMEM((SEQ + BLOCK, 128), BF16)] * 2 + [pltpu.VMEM((SEQ, 128), BF16)]
        + [pltpu.VMEM((SEQ, 128), F32)] * 3 + [pltpu.VMEM((SEQ + BLOCK, 128), F32)] * 2 + [pltpu.VMEM((SEQ, 128), F32)]
        + [pltpu.VMEM((3, SEQ, 128), BF16), pltpu.SemaphoreType.DMA((3,))],
        [proj] * 9 + [cos_t, sin_t, attn, lse, dattn, dproj], ride, aliases={14: 0})


SSM_CHUNKS = 4
CHUNK_STATES = 512
SCAN_ROWS = 8
U_COL = (3 * QKV_WIDTH) // 128


def _cmul(xr, xi, yr, yi):
    return xr * yr - xi * yi, xr * yi + xi * yr


def _ssm_prep(a_re, a_im, log_dt, b_re_t, b_im_t):
    def body(ar_ref, ai_ref, ldt_ref, br_ref, bi_ref, abr_ref, abi_ref, er_ref, ei_ref, bbr_ref, bbi_ref):
        ar, ai = ar_ref[...], ai_ref[...]
        dt = jnp.exp(ldt_ref[...])
        mag = jnp.exp(ar * dt)
        abr, abi = mag * jnp.cos(ai * dt), mag * jnp.sin(ai * dt)
        den = ar * ar + ai * ai
        nr, ni = abr - 1.0, abi
        er, ei = (nr * ar + ni * ai) / den, (ni * ar - nr * ai) / den
        abr_ref[...], abi_ref[...], er_ref[...], ei_ref[...] = abr, abi, er, ei
        er3, ei3 = er[:, None, :], ei[:, None, :]
        br, bi = br_ref[...], bi_ref[...]
        bbr_ref[...] = er3 * br - ei3 * bi
        bbi_ref[...] = er3 * bi + ei3 * br

    gp = jax.ShapeDtypeStruct(a_re.shape, F32)
    gb = jax.ShapeDtypeStruct(b_re_t.shape, F32)
    return _pallas_call(body, name="ssm_prep", out_shape=(gp, gp, gp, gp, gb, gb))(a_re, a_im, log_dt, b_re_t, b_im_t)


def _ssm_param_bwd(a_re, a_im, log_dt, b_re_t, b_im_t, abar_re, abar_im, e_re, e_im, ga_re, ga_im, gbb_re_t, gbb_im_t):
    def body(ar_ref, ai_ref, ldt_ref, br_ref, bi_ref, abr_ref, abi_ref, er_ref, ei_ref, gar_ref, gai_ref, gbr_ref, gbi_ref,
             o_ar, o_ai, o_ldt, o_br, o_bi):
        ar, ai = ar_ref[...], ai_ref[...]
        dt = jnp.exp(ldt_ref[...])
        er, ei = er_ref[...], ei_ref[...]
        br, bi, gbr, gbi = br_ref[...], bi_ref[...], gbr_ref[...], gbi_ref[...]
        er3, ei3 = er[:, None, :], ei[:, None, :]
        o_br[...] = er3 * gbr + ei3 * gbi
        o_bi[...] = er3 * gbi - ei3 * gbr
        ge_r = jnp.sum(br * gbr + bi * gbi, axis=1)
        ge_i = jnp.sum(br * gbi - bi * gbr, axis=1)
        den = ar * ar + ai * ai
        ilr, ili = ar / den, -ai / den
        t_r, t_i = _cmul(ilr, -ili, ge_r, ge_i)
        gab_r, gab_i = gar_ref[...] + t_r, gai_ref[...] + t_i
        gz_r, gz_i = _cmul(abr_ref[...], -abi_ref[...], gab_r, gab_i)
        el_r, el_i = _cmul(er, ei, ilr, ili)
        u_r, u_i = _cmul(el_r, -el_i, ge_r, ge_i)
        o_ar[...] = dt * gz_r - u_r
        o_ai[...] = dt * gz_i - u_i
        o_ldt[...] = jnp.sum(gz_r * ar + gz_i * ai, axis=1, keepdims=True) * dt

    gp = jax.ShapeDtypeStruct(a_re.shape, F32)
    gb = jax.ShapeDtypeStruct(b_re_t.shape, F32)
    return _pallas_call(body, name="ssm_param_bwd", out_shape=(gp, gp, jax.ShapeDtypeStruct(log_dt.shape, F32), gb, gb))(
        a_re, a_im, log_dt, b_re_t, b_im_t, abar_re, abar_im, e_re, e_im, ga_re, ga_im, gbb_re_t, gbb_im_t)


def _block_diag(blocks_re, blocks_im, sign_im, rows_are_channels):
    both = jnp.stack([blocks_re, sign_im * blocks_im]).reshape(2, SSM_CHUNKS, 8, SSM_GROUP, SSM_STATE)
    eye = jnp.eye(8, dtype=F32)
    if rows_are_channels:
        return jnp.einsum("rcghp,gk->cghrkp", both, eye).reshape(SSM_CHUNKS, 128, 2 * CHUNK_STATES)
    return jnp.einsum("rcghp,gk->crkpgh", both, eye).reshape(SSM_CHUNKS, 2 * CHUNK_STATES, 128)


def _block_diag_parts(mat, rows_are_channels):
    if rows_are_channels:
        six = mat.reshape(SSM_CHUNKS, 8, SSM_GROUP, 2, 8, SSM_STATE)
        parts = jnp.einsum("cghrgp->rcghp", six)
    else:
        six = mat.reshape(SSM_CHUNKS, 2, 8, SSM_STATE, 8, SSM_GROUP)
        parts = jnp.einsum("crgpgh->rcghp", six)
    parts = parts.reshape(2, SSM_GROUPS, SSM_GROUP, SSM_STATE)
    return parts[0], parts[1]


def _scan_consts(a_ref, conj, reverse):
    ar = jnp.broadcast_to(a_ref[:, :CHUNK_STATES], (SCAN_ROWS, CHUNK_STATES))
    ai = jnp.broadcast_to(a_ref[:, CHUNK_STATES:], (SCAN_ROWS, CHUNK_STATES))
    if conj:
        ai = -ai
    row = lax.broadcasted_iota(jnp.int32, (SCAN_ROWS, CHUNK_STATES), 0)
    if reverse:
        row = SCAN_ROWS - 1 - row
    zero = jnp.zeros_like(ar)
    steps = []
    pr, pi = ar, ai
    for shift in (1, 2, 4):
        keep = row >= shift
        steps.append((SCAN_ROWS - shift if reverse else shift, jnp.where(keep, pr, zero), jnp.where(keep, pi, zero)))
        pr, pi = _cmul(pr, pi, pr, pi)
    first = row == 0
    return steps, (jnp.where(first, ar, zero), jnp.where(first, ai, zero)), first


def _scan_tile(xr, xi, prev_r, prev_i, steps, carry_in, reverse):
    edge = SCAN_ROWS - 1 if reverse else 1
    cr, ci = pltpu.roll(prev_r, edge, axis=0), pltpu.roll(prev_i, edge, axis=0)
    xr, xi = xr + carry_in[0] * cr - carry_in[1] * ci, xi + carry_in[0] * ci + carry_in[1] * cr
    for shift, mr, mi in steps:
        sr, si = pltpu.roll(xr, shift, axis=0), pltpu.roll(xi, shift, axis=0)
        xr, xi = xr + mr * sr - mi * si, xi + mr * si + mi * sr
    return xr, xi


MM_ROWS = 256


def _ssm_fwd(proj, bmat, cmat, a_chunks, d_skip, ride=None):
    def body(u_ref, b_ref, c_ref, a_ref, d_ref, y_ref, h_ref):
        for i in range(SEQ // MM_ROWS):
            rows = pl.ds(i * MM_ROWS, MM_ROWS)
            h_ref[rows, :] = _dot(u_ref[rows, :].astype(BF16), b_ref[...])
        steps, carry_in, _ = _scan_consts(a_ref, conj=False, reverse=False)

        def tile(k, carry):
            rows = pl.ds(pl.multiple_of(k * SCAN_ROWS, SCAN_ROWS), SCAN_ROWS)
            xr, xi = _scan_tile(h_ref[rows, :CHUNK_STATES], h_ref[rows, CHUNK_STATES:], carry[0], carry[1], steps, carry_in, False)
            h_ref[rows, :CHUNK_STATES] = xr
            h_ref[rows, CHUNK_STATES:] = xi
            return xr, xi

        zero = jnp.zeros((SCAN_ROWS, CHUNK_STATES), F32)
        lax.fori_loop(0, SEQ // SCAN_ROWS, tile, (zero, zero), unroll=4)
        for i in range(SEQ // MM_ROWS):
            rows = pl.ds(i * MM_ROWS, MM_ROWS)
            y_ref[rows, :] = _dot(h_ref[rows, :].astype(BF16), c_ref[...]) + d_ref[...] * u_ref[rows, :]

    return _call(
        body, "ssm_fwd", (SSM_CHUNKS,),
        [pl.BlockSpec((SEQ, 128), lambda c: (0, U_COL + c)),
         pl.BlockSpec((None, 128, 2 * CHUNK_STATES), lambda c: (c, 0, 0)),
         pl.BlockSpec((None, 2 * CHUNK_STATES, 128), lambda c: (c, 0, 0)),
         pl.BlockSpec((None, 1, 2 * CHUNK_STATES), lambda c: (c, 0, 0)),
         pl.BlockSpec((1, 128), lambda c: (0, c))],
        [pl.BlockSpec((SEQ, 128), lambda c: (0, c)), pl.BlockSpec((SEQ, 2 * CHUNK_STATES), lambda c: (0, c))],
        [_sds((SEQ, SSM_WIDTH), F32), _sds((SEQ, SSM_CHUNKS * 2 * CHUNK_STATES), F32)], [],
        [proj, bmat, cmat, a_chunks, d_skip], ride)


def _ssm_bwd(dys, proj, h, bmat, cmat, a_chunks, d_skip, dproj, ride=None):
    def body(dy_ref, u_ref, h_ref, b_ref, c_ref, a_ref, d_ref, dproj_in, du_ref, db_ref, dc_ref, da_ref, dd_ref, g_ref):
        del dproj_in
        dsum = jnp.zeros((1, 128), F32)
        dcm = jnp.zeros((2 * CHUNK_STATES, 128), F32)
        for i in range(SEQ // MM_ROWS):
            rows = pl.ds(i * MM_ROWS, MM_ROWS)
            dy = dy_ref[rows, :]
            g_ref[rows, :] = _dot_nt(dy.astype(BF16), c_ref[...])
            dsum += jnp.sum(dy * u_ref[rows, :], axis=0, keepdims=True)
            dcm += _dot_tn(h_ref[rows, :].astype(BF16), dy.astype(BF16))
        dd_ref[...] = dsum
        dc_ref[...] = dcm
        steps, carry_in, _ = _scan_consts(a_ref, conj=True, reverse=True)
        first_row = lax.broadcasted_iota(jnp.int32, (SCAN_ROWS, CHUNK_STATES), 0) == 0
        n_tiles = SEQ // SCAN_ROWS

        def tile(j, carry):
            k = n_tiles - 1 - j
            rows = pl.ds(pl.multiple_of(k * SCAN_ROWS, SCAN_ROWS), SCAN_ROWS)
            before = pl.ds(pl.multiple_of(jnp.maximum(k - 1, 0) * SCAN_ROWS, SCAN_ROWS), SCAN_ROWS)
            gr, gi = _scan_tile(g_ref[rows, :CHUNK_STATES], g_ref[rows, CHUNK_STATES:], carry[0], carry[1], steps, carry_in, True)
            g_ref[rows, :CHUNK_STATES] = gr
            g_ref[rows, CHUNK_STATES:] = gi
            has_before = jnp.where(k > 0, 1.0, 0.0)
            hr = jnp.where(first_row, pltpu.roll(h_ref[before, :CHUNK_STATES], 1, axis=0) * has_before,
                           pltpu.roll(h_ref[rows, :CHUNK_STATES], 1, axis=0))
            hi = jnp.where(first_row, pltpu.roll(h_ref[before, CHUNK_STATES:], 1, axis=0) * has_before,
                           pltpu.roll(h_ref[rows, CHUNK_STATES:], 1, axis=0))
            return gr, gi, carry[2] + hr * gr + hi * gi, carry[3] + hr * gi - hi * gr

        zero = jnp.zeros((SCAN_ROWS, CHUNK_STATES), F32)
        _, _, sar, sai = lax.fori_loop(0, n_tiles, tile, (zero, zero, zero, zero), unroll=4)
        da_ref[:, :CHUNK_STATES] = jnp.sum(sar, axis=0, keepdims=True)
        da_ref[:, CHUNK_STATES:] = jnp.sum(sai, axis=0, keepdims=True)
        dbm = jnp.zeros((128, 2 * CHUNK_STATES), F32)
        for i in range(SEQ // MM_ROWS):
            rows = pl.ds(i * MM_ROWS, MM_ROWS)
            g = g_ref[rows, :].astype(BF16)
            du_ref[rows, :] = (_dot_nt(g, b_ref[...]) + d_ref[...] * dy_ref[rows, :]).astype(BF16)
            dbm += _dot_tn(u_ref[rows, :].astype(BF16), g)
        db_ref[...] = dbm

    chunk_col = pl.BlockSpec((SEQ, 128), lambda c: (0, c))
    return _call(
        body, "ssm_bwd", (SSM_CHUNKS,),
        [chunk_col,
         pl.BlockSpec((SEQ, 128), lambda c: (0, U_COL + c)),
         pl.BlockSpec((SEQ, 2 * CHUNK_STATES), lambda c: (0, c)),
         pl.BlockSpec((None, 128, 2 * CHUNK_STATES), lambda c: (c, 0, 0)),
         pl.BlockSpec((None, 2 * CHUNK_STATES, 128), lambda c: (c, 0, 0)),
         pl.BlockSpec((None, 1, 2 * CHUNK_STATES), lambda c: (c, 0, 0)),
         pl.BlockSpec((1, 128), lambda c: (0, c)), ANY],
        [pl.BlockSpec((SEQ, 128), lambda c: (0, U_COL + c)),
         pl.BlockSpec((None, 128, 2 * CHUNK_STATES), lambda c: (c, 0, 0)),
         pl.BlockSpec((None, 2 * CHUNK_STATES, 128), lambda c: (c, 0, 0)),
         pl.BlockSpec((None, 1, 2 * CHUNK_STATES), lambda c: (c, 0, 0)),
         pl.BlockSpec((1, 128), lambda c: (0, c))],
        [_sds((SEQ, IN_WIDTH), BF16), _sds((SSM_CHUNKS, 128, 2 * CHUNK_STATES), F32),
         _sds((SSM_CHUNKS, 2 * CHUNK_STATES, 128), F32), _sds((SSM_CHUNKS, 1, 2 * CHUNK_STATES), F32), _sds((1, SSM_WIDTH), F32)],
        [pltpu.VMEM((SEQ, 2 * CHUNK_STATES), F32)], [dys, proj, h, bmat, cmat, a_chunks, d_skip, dproj], ride, aliases={7: 0})


def _ssm_tables(abar_re, abar_im, bbar_re_t, bbar_im_t, c_re, c_im):
    bmat = _block_diag(bbar_re_t, bbar_im_t, 1.0, True).astype(BF16)
    cmat = _block_diag(c_re, c_im, -1.0, False).astype(BF16)
    a_chunks = jnp.concatenate([abar_re.reshape(SSM_CHUNKS, 1, CHUNK_STATES), abar_im.reshape(SSM_CHUNKS, 1, CHUNK_STATES)], axis=2)
    return bmat, cmat, a_chunks


GL_COL = (3 * QKV_WIDTH + SSM_WIDTH) // D_MODEL
GELU_C = math.sqrt(2.0 / math.pi)
GELU_A = 0.044715


def _sds(shape, dtype):
    return jax.ShapeDtypeStruct(shape, dtype)


def _gelu(x):
    t = jnp.tanh(GELU_C * (x + GELU_A * x * x * x))
    return 0.5 * x * (1.0 + t), t


def _gelu_grad(x, t):
    return 0.5 * (1.0 + t) + 0.5 * x * (1.0 - t * t) * GELU_C * (1.0 + 3.0 * GELU_A * x * x)


def _layer_norm(r, g, b):
    mu = jnp.mean(r, axis=-1, keepdims=True)
    xc = r - mu
    rstd = lax.rsqrt(jnp.mean(xc * xc, axis=-1, keepdims=True) + LN_EPS)
    xhat = xc * rstd
    return xhat * g + b, xhat, rstd


def _layer_norm_bwd(dy, xhat, rstd, g):
    dxhat = dy * g
    m1 = jnp.mean(dxhat, axis=-1, keepdims=True)
    m2 = jnp.mean(dxhat * xhat, axis=-1, keepdims=True)
    return rstd * (dxhat - m1 - xhat * m2)


def _proj(x, w_in, ride=None):
    tm, tn = 1024, 1792

    def body(x_ref, w_ref, o_ref):
        o_ref[...] = _dot(x_ref[...].astype(BF16), _side_by_side(w_ref))

    return _call(
        body, "proj", (SEQ // tm, IN_WIDTH // tn),
        [pl.BlockSpec((tm, D_MODEL), lambda i, j: (i, 0)), pl.BlockSpec((2, D_MODEL, tn // 2), lambda i, j: (j, 0, 0))],
        [pl.BlockSpec((tm, tn), lambda i, j: (i, j))], [_sds((SEQ, IN_WIDTH), F32)], [], [x, w_in], ride)


def _row_spec(tm, width, col=0):
    return pl.BlockSpec((tm, width), lambda i, col=col: (i, col))


def _full_spec(shape):
    return pl.BlockSpec(shape, lambda i: (0,) * len(shape))


def _weight_spec(shape):
    return pl.BlockSpec(shape, lambda i: (0,) * len(shape), pipeline_mode=pl.Buffered(1))


def _mixer_out(attn, ys, proj, x, w_ab, w_sb, w_glu, w_out, b_gate, ln_g, ln_b, ride=None):
    tm = 512

    def body(attn_ref, ys_ref, gl0_ref, gl1_ref, x_ref, wab_ref, wsb_ref, wglu_ref, wout_ref, bg_ref, g_ref, b_ref,
             h_ref, xhat_ref, rstd_ref, glu_ref, ya_ref, yssm_ref):
        gy, _ = _gelu(ys_ref[...])
        glu = _dot(gy.astype(BF16), _side_by_side(wglu_ref))
        glu_ref[...] = glu
        y_s = glu[:, :SSM_WIDTH] * jax.nn.sigmoid(glu[:, SSM_WIDTH:])
        y_ssm = _dot(y_s.astype(BF16), _side_by_side(wsb_ref))
        y_attn = _dot(attn_ref[...].astype(BF16), _side_by_side(wab_ref))
        ya_ref[...] = y_attn
        yssm_ref[...] = y_ssm
        g0 = jax.nn.sigmoid(gl0_ref[...] + _side_by_side(bg_ref, 0))
        g1 = jax.nn.sigmoid(gl1_ref[...] + _side_by_side(bg_ref, 1))
        mixed = g0 * y_attn + g1 * y_ssm
        r1 = DN_ALPHA * x_ref[...] + _dot(mixed.astype(BF16), wout_ref[...])
        h, xhat, rstd = _layer_norm(r1, g_ref[...], b_ref[...])
        h_ref[...] = h
        xhat_ref[...] = xhat
        rstd_ref[...] = jnp.broadcast_to(rstd, (tm, 128))

    wide = _sds((SEQ, D_MODEL), F32)
    return _call(
        body, "mixer_out", (SEQ // tm,),
        [_row_spec(tm, ATTN_WIDTH), _row_spec(tm, SSM_WIDTH), _row_spec(tm, D_MODEL, GL_COL), _row_spec(tm, D_MODEL, GL_COL + 1),
         _row_spec(tm, D_MODEL), _weight_spec((N_DEV, ATTN_WIDTH, 128)), _weight_spec((N_DEV, SSM_WIDTH, 128)),
         _weight_spec((N_DEV, SSM_WIDTH, 128)), _weight_spec((D_MODEL, D_MODEL)), _full_spec((N_DEV, 2, 128)),
         _full_spec((1, D_MODEL)), _full_spec((1, D_MODEL))],
        [_row_spec(tm, D_MODEL), _row_spec(tm, D_MODEL), _row_spec(tm, 128), _row_spec(tm, D_MODEL),
         _row_spec(tm, D_MODEL), _row_spec(tm, D_MODEL)],
        [wide, wide, _sds((SEQ, 128), F32), wide, wide, wide], [],
        [attn, ys, proj, proj, x, w_ab, w_sb, w_glu, w_out, b_gate, ln_g, ln_b], ride)


def _ff_up(h, w_gate, w_up, ride=None):
    tm, tn = 1024, 768

    def body(h_ref, wg_ref, wu_ref, a_ref, b_ref, f_ref):
        hb = h_ref[...].astype(BF16)
        a, b = _dot(hb, _side_by_side(wg_ref)), _dot(hb, _side_by_side(wu_ref))
        a_ref[...] = a.astype(BF16)
        b_ref[...] = b.astype(BF16)
        f_ref[...] = (a * jax.nn.sigmoid(a) * b).astype(BF16)

    tile = pl.BlockSpec((tm, tn), lambda i, j: (i, j))
    wtile = pl.BlockSpec((tn // FF_PAD, D_MODEL, FF_PAD), lambda i, j: (j, 0, 0))
    out = _sds((SEQ, D_FF_PAD), BF16)
    return _call(body, "ff_up", (SEQ // tm, D_FF_PAD // tn), [pl.BlockSpec((tm, D_MODEL), lambda i, j: (i, 0)), wtile, wtile],
                 [tile, tile, tile], [out, out, out], [], [h, w_gate, w_up], ride)


def _ff_down_loss(f, w_down, h, target, ln_g, ln_b):
    tm = 512

    def body(f_ref, w_ref, h_ref, t_ref, g_ref, b_ref, dr_ref, dg_ref, db_ref, loss_ref):
        @pl.when(pl.program_id(0) == 0)
        def _():
            dg_ref[...] = jnp.zeros_like(dg_ref)
            db_ref[...] = jnp.zeros_like(db_ref)
            loss_ref[...] = jnp.zeros_like(loss_ref)

        r2 = DN_ALPHA * h_ref[...] + _dot(f_ref[...], w_ref[...])
        g = g_ref[...]
        out, xhat, rstd = _layer_norm(r2, g, b_ref[...])
        err = out - t_ref[...]
        loss_ref[...] += 0.5 * jnp.sum(jnp.mean(err * err, axis=-1, keepdims=True), axis=0, keepdims=True)
        dout = err * (1.0 / D_MODEL)
        dg_ref[...] += jnp.sum(dout * xhat, axis=0, keepdims=True)
        db_ref[...] += jnp.sum(dout, axis=0, keepdims=True)
        dr_ref[...] = _layer_norm_bwd(dout, xhat, rstd, g)

    vec = _sds((1, D_MODEL), F32)
    return _pallas_call(
        body, name="ff_down_loss", grid=(SEQ // tm,),
        in_specs=[_row_spec(tm, D_FF_PAD), _weight_spec((D_FF_PAD, D_MODEL)), _row_spec(tm, D_MODEL), _row_spec(tm, D_MODEL),
                  _full_spec((1, D_MODEL)), _full_spec((1, D_MODEL))],
        out_specs=(_row_spec(tm, D_MODEL), _full_spec((1, D_MODEL)), _full_spec((1, D_MODEL)), _full_spec((1, 128))),
        out_shape=(_sds((SEQ, D_MODEL), F32), vec, vec, _sds((1, 128), F32)),
        compiler_params=_cparams(dimension_semantics=("arbitrary",)),
    )(f, w_down, h, target, ln_g, ln_b)


def _ff_down_bwd(dr2, w_down, a, b):
    tm, tn = 1024, 768

    def body(dr_ref, w_ref, a_ref, b_ref, da_ref, db_ref):
        df = _dot_nt(dr_ref[...].astype(BF16), w_ref[...])
        av, bv = a_ref[...].astype(F32), b_ref[...].astype(F32)
        sg = jax.nn.sigmoid(av)
        da_ref[...] = (df * bv * sg * (1.0 + av * (1.0 - sg))).astype(BF16)
        db_ref[...] = (df * av * sg).astype(BF16)

    tile = pl.BlockSpec((tm, tn), lambda i, j: (i, j))
    out = _sds((SEQ, D_FF_PAD), BF16)
    return _pallas_call(
        body, name="ff_down_bwd", grid=(SEQ // tm, D_FF_PAD // tn),
        in_specs=[pl.BlockSpec((tm, D_MODEL), lambda i, j: (i, 0)), pl.BlockSpec((tn, D_MODEL), lambda i, j: (j, 0)), tile, tile],
        out_specs=(tile, tile), out_shape=(out, out),
        compiler_params=_cparams(dimension_semantics=("arbitrary", "arbitrary")),
    )(dr2, w_down, a, b)


def _ff_up_bwd(da, db, w_gate, w_up, dr2, xhat1, rstd1, ln_g, ride=None):
    tm, tk = 1024, 768
    nk = D_FF_PAD // tk

    def body(da_ref, db_ref, wg_ref, wu_ref, dr2_ref, xhat_ref, rstd_ref, g_ref, dr1_ref, dg_ref, dbias_ref, acc):
        i, k = pl.program_id(0), pl.program_id(1)

        @pl.when(jnp.logical_and(i == 0, k == 0))
        def _():
            dg_ref[...] = jnp.zeros_like(dg_ref)
            dbias_ref[...] = jnp.zeros_like(dbias_ref)

        part = _dot_nt(da_ref[...], _side_by_side(wg_ref)) + _dot_nt(db_ref[...], _side_by_side(wu_ref))

        @pl.when(k == 0)
        def _():
            acc[...] = part

        @pl.when(k > 0)
        def _():
            acc[...] += part

        @pl.when(k == nk - 1)
        def _():
            dh = DN_ALPHA * dr2_ref[...] + acc[...]
            xhat = xhat_ref[...]
            dg_ref[...] += jnp.sum(dh * xhat, axis=0, keepdims=True)
            dbias_ref[...] += jnp.sum(dh, axis=0, keepdims=True)
            rstd = jnp.max(rstd_ref[...], axis=1, keepdims=True)
            dr1_ref[...] = _layer_norm_bwd(dh, xhat, rstd, g_ref[...])

    hid = pl.BlockSpec((tm, tk), lambda i, k: (i, k))
    wtile = pl.BlockSpec((tk // FF_PAD, D_MODEL, FF_PAD), lambda i, k: (k, 0, 0))
    row = pl.BlockSpec((tm, D_MODEL), lambda i, k: (i, 0))
    vec = pl.BlockSpec((1, D_MODEL), lambda i, k: (0, 0))
    return _call(
        body, "ff_up_bwd", (SEQ // tm, nk),
        [hid, hid, wtile, wtile, row, row, pl.BlockSpec((tm, 128), lambda i, k: (i, 0)), vec],
        [row, vec, vec], [_sds((SEQ, D_MODEL), F32), _sds((1, D_MODEL), F32), _sds((1, D_MODEL), F32)],
        [pltpu.VMEM((tm, D_MODEL), F32)], [da, db, w_gate, w_up, dr2, xhat1, rstd1, ln_g], ride)


def _mixer_bwd(dr1, proj, y_attn, y_ssm, glu, ys, w_ab, w_sb, w_glu, w_out, b_gate):
    tm = 256

    def body(dr1_ref, gl0_ref, gl1_ref, ya_ref, yssm_ref, glu_ref, ys_ref, wab_ref, wsb_ref, wglu_ref, wout_ref, bg_ref,
             dya_ref, dyssm_ref, dgl_ref, dattn_ref, dglu_ref, dys_ref, mixed_ref, ysb_ref, gy_ref, dbg_ref):
        @pl.when(pl.program_id(0) == 0)
        def _():
            dbg_ref[...] = jnp.zeros_like(dbg_ref)

        dmixed = _dot_nt(dr1_ref[...].astype(BF16), wout_ref[...])
        g0 = jax.nn.sigmoid(gl0_ref[...] + _side_by_side(bg_ref, 0))
        g1 = jax.nn.sigmoid(gl1_ref[...] + _side_by_side(bg_ref, 1))
        y_attn, y_ssm = ya_ref[...], yssm_ref[...]
        mixed_ref[...] = (g0 * y_attn + g1 * y_ssm).astype(BF16)
        dya = (dmixed * g0).astype(BF16)
        dyssm = (dmixed * g1).astype(BF16)
        dya_ref[...] = dya
        dyssm_ref[...] = dyssm
        dgl0 = dmixed * y_attn * g0 * (1.0 - g0)
        dgl1 = dmixed * y_ssm * g1 * (1.0 - g1)
        dgl_ref[:, :GL_COL * D_MODEL] = jnp.zeros((tm, GL_COL * D_MODEL), BF16)
        dgl_ref[:, GL_COL * D_MODEL:(GL_COL + 1) * D_MODEL] = dgl0.astype(BF16)
        dgl_ref[:, (GL_COL + 1) * D_MODEL:] = dgl1.astype(BF16)
        dbg_ref[:, :D_MODEL] += jnp.sum(dgl0, axis=0, keepdims=True)
        dbg_ref[:, D_MODEL:] += jnp.sum(dgl1, axis=0, keepdims=True)
        dattn_ref[...] = _dot_nt(dya, _side_by_side(wab_ref))
        dy_s = _dot_nt(dyssm, _side_by_side(wsb_ref))
        glu = glu_ref[...]
        glu1, sg = glu[:, :SSM_WIDTH], jax.nn.sigmoid(glu[:, SSM_WIDTH:])
        ysb_ref[...] = (glu1 * sg).astype(BF16)
        dglu1 = (dy_s * sg).astype(BF16)
        dglu2 = (dy_s * glu1 * sg * (1.0 - sg)).astype(BF16)
        dglu_ref[:, :SSM_WIDTH] = dglu1
        dglu_ref[:, SSM_WIDTH:] = dglu2
        dgy = _dot_nt(jnp.concatenate([dglu1, dglu2], axis=1), _side_by_side(wglu_ref))
        ys = ys_ref[...]
        gy, t = _gelu(ys)
        gy_ref[...] = gy.astype(BF16)
        dys_ref[...] = dgy * _gelu_grad(ys, t)

    wide_b, half_b = _sds((SEQ, D_MODEL), BF16), _sds((SEQ, SSM_WIDTH), BF16)
    half_f = _sds((SEQ, SSM_WIDTH), F32)
    return _pallas_call(
        body, name="mixer_bwd", grid=(SEQ // tm,),
        in_specs=[_row_spec(tm, D_MODEL), _row_spec(tm, D_MODEL, GL_COL), _row_spec(tm, D_MODEL, GL_COL + 1), _row_spec(tm, D_MODEL),
                  _row_spec(tm, D_MODEL), _row_spec(tm, D_MODEL), _row_spec(tm, SSM_WIDTH), _full_spec((N_DEV, ATTN_WIDTH, 128)),
                  _full_spec((N_DEV, SSM_WIDTH, 128)), _full_spec((N_DEV, SSM_WIDTH, 128)), _full_spec((D_MODEL, D_MODEL)),
                  _full_spec((N_DEV, 2, 128))],
        out_specs=(_row_spec(tm, D_MODEL), _row_spec(tm, D_MODEL), _row_spec(tm, IN_WIDTH), _row_spec(tm, ATTN_WIDTH),
                   _row_spec(tm, D_MODEL), _row_spec(tm, SSM_WIDTH), _row_spec(tm, D_MODEL), _row_spec(tm, SSM_WIDTH),
                   _row_spec(tm, SSM_WIDTH), _full_spec((1, 2 * D_MODEL))),
        out_shape=(wide_b, wide_b, _sds((SEQ, IN_WIDTH), BF16), half_f, wide_b, half_f, wide_b, half_b, half_b,
                   _sds((1, 2 * D_MODEL), F32)),
        compiler_params=_cparams(dimension_semantics=("arbitrary",)),
    )(dr1, proj, proj, y_attn, y_ssm, glu, ys, w_ab, w_sb, w_glu, w_out, b_gate)


def _grad_x(dproj, w_in, dr1, ride=None):
    tm, tk = 1024, 1792
    nk = IN_WIDTH // tk

    def body(dp_ref, w_ref, dr1_ref, o_ref, acc):
        k = pl.program_id(1)
        part = _dot_nt(dp_ref[...], _side_by_side(w_ref))

        @pl.when(k == 0)
        def _():
            acc[...] = part

        @pl.when(k > 0)
        def _():
            acc[...] += part

        @pl.when(k == nk - 1)
        def _():
            o_ref[...] = DN_ALPHA * dr1_ref[...] + acc[...]

    row = pl.BlockSpec((tm, D_MODEL), lambda i, k: (i, 0))
    return _call(
        body, "grad_x", (SEQ // tm, nk),
        [pl.BlockSpec((tm, tk), lambda i, k: (i, k)), pl.BlockSpec((2, D_MODEL, tk // 2), lambda i, k: (k, 0, 0)), row],
        [row], [_sds((SEQ, D_MODEL), F32)], [pltpu.VMEM((tm, D_MODEL), F32)], [dproj, w_in, dr1], ride)


def _weight_grad(a, b, name, shard_cols=None, ride=None):
    k, n = a.shape[1], b.shape[1]
    tk = min(k, 512) if shard_cols else k // N_DEV
    tn = n // 4 if shard_cols else min(n, 1024)

    def body(a_ref, b_ref, o_ref):
        grad = _dot_tn(a_ref[...].astype(BF16), b_ref[...].astype(BF16))
        if shard_cols:
            o_ref[0] = grad[:, :shard_cols].astype(BF16)
            o_ref[1] = grad[:, shard_cols:].astype(BF16)
        else:
            o_ref[...] = grad.astype(BF16)

    if shard_cols:
        out_spec = pl.BlockSpec((2, None, tk, shard_cols), lambda kk, j: (0, j, kk, 0))
        out_shape = _sds((2, 4, k, shard_cols), BF16)
    else:
        out_spec = pl.BlockSpec((None, None, tk, tn), lambda kk, j: (kk % 2, kk // 2, 0, j))
        out_shape = _sds((2, 4, tk, n), BF16)
    out = _call(body, name, (k // tk, n // tn),
                [pl.BlockSpec((SEQ, tk), lambda kk, j: (0, kk)), pl.BlockSpec((SEQ, tn), lambda kk, j: (0, j))],
                [out_spec], [out_shape], [], [a, b], ride)
    return out[0] if ride is None else out


MESH = pl.DeviceIdType.MESH
ANY = pl.BlockSpec(memory_space=pl.ANY)


def _place():
    return lax.axis_index("x"), lax.axis_index("y"), lax.axis_index("c")


def _other_chips(x, y):
    return [(1 - x, y), (x, 1 - y), (1 - x, 1 - y)]


class _Ride:
    def __init__(self, operands, results, aliases, sems, start, wait):
        self.operands, self.results, self.aliases, self.sems = list(operands), list(results), dict(aliases), list(sems)
        self.start, self.wait = start, wait

    def __add__(self, other):
        n_in, n_out, n_sem = len(self.operands), len(self.results), len(self.sems)

        def both(which):
            def run(ins, outs, sems):
                getattr(self, which)(ins[:n_in], outs[:n_out], sems[:n_sem])
                getattr(other, which)(ins[n_in:], outs[n_out:], sems[n_sem:])
            return run

        aliases = {**self.aliases, **{n_in + i: n_out + j for i, j in other.aliases.items()}}
        return _Ride(self.operands + other.operands, self.results + other.results, aliases, self.sems + other.sems,
                     both("start"), both("wait"))


def _call(body, name, grid, in_specs, out_specs, out_shape, scratch_shapes, operands, ride=None, aliases=None):
    in_specs, out_specs, out_shape = list(in_specs), list(out_specs), list(out_shape)
    scratch_shapes, operands, aliases = list(scratch_shapes), list(operands), dict(aliases or {})
    kernel_body = body
    if ride is not None:
        n_in, n_out, n_scr, r_in, r_out = len(in_specs), len(out_specs), len(scratch_shapes), len(ride.operands), len(ride.results)

        def kernel_body(*refs):
            out0, scr0 = n_in + r_in, n_in + r_in + n_out + r_out
            ride_refs = (refs[n_in:out0], refs[out0 + n_out:scr0], refs[scr0 + n_scr:])
            ids = [pl.program_id(i) for i in range(len(grid))]
            first = functools.reduce(jnp.logical_and, [i == 0 for i in ids])
            last = functools.reduce(jnp.logical_and, [i == g - 1 for i, g in zip(ids, grid)])

            @pl.when(first)
            def _():
                ride.start(*ride_refs)

            body(*refs[:n_in], *refs[out0:out0 + n_out], *refs[scr0:scr0 + n_scr])

            @pl.when(last)
            def _():
                ride.wait(*ride_refs)

        aliases.update({n_in + i: n_out + j for i, j in ride.aliases.items()})
        in_specs += [ANY] * r_in
        out_specs += [ANY] * r_out
        out_shape += ride.results
        scratch_shapes += ride.sems
        operands += ride.operands
    return _pallas_call(
        kernel_body, name=name, grid=grid, in_specs=in_specs, out_specs=out_specs, out_shape=out_shape,
        scratch_shapes=scratch_shapes, input_output_aliases=aliases,
        compiler_params=_cparams(dimension_semantics=("arbitrary",) * len(grid)),
    )(*operands)


def _after(*arrays):
    return _Ride(arrays, [], {}, [], lambda *refs: None, lambda *refs: None)


def _gather_first_level(shards):
    n = len(shards)

    def copies(ins, outs, sems, landed):
        send_sems, recv_sems, local_sems = sems
        x, y, c = _place()
        peers = [(x, y, 1 - c)] + [(px, py, c) for px, py in _other_chips(x, y)]

        def row(peer):
            return 4 * x + 2 * y + c if not landed else 4 * peer[0] + 2 * peer[1] + peer[2]

        local = [pltpu.make_async_copy(ins[a], outs[a].at[4 * x + 2 * y + c], local_sems.at[a]) for a in range(n)]
        remote = [pltpu.make_async_remote_copy(
            src_ref=ins[a], dst_ref=outs[a].at[row(peer)], send_sem=send_sems.at[a, k], recv_sem=recv_sems.at[a, k],
            device_id=peer, device_id_type=MESH) for a in range(n) for k, peer in enumerate(peers)]
        return local, remote

    def start(ins, outs, sems):
        local, remote = copies(ins, outs, sems, False)
        for cp in local + remote:
            cp.start()

    def wait(ins, outs, sems):
        local, sent = copies(ins, outs, sems, False)
        for cp in copies(ins, outs, sems, True)[1]:
            cp.wait_recv()
        for cp in sent:
            cp.wait_send()
        for cp in local:
            cp.wait()

    return _Ride(shards, [_sds((N_DEV,) + s.shape, s.dtype) for s in shards], {},
                 [pltpu.SemaphoreType.DMA((n, 4)), pltpu.SemaphoreType.DMA((n, 4)), pltpu.SemaphoreType.DMA((n,))], start, wait)


def _gather_second_level(buffers):
    n = len(buffers)

    def copies(outs, sems, core):
        send_sems, recv_sems = sems
        x, y, c = _place()
        return [pltpu.make_async_remote_copy(
            src_ref=outs[a].at[4 * px + 2 * py + core], dst_ref=outs[a].at[4 * px + 2 * py + core], send_sem=send_sems.at[a, j],
            recv_sem=recv_sems.at[a, j], device_id=(x, y, 1 - c), device_id_type=MESH)
            for a in range(n) for j, (px, py) in enumerate(_other_chips(x, y))]

    def start(ins, outs, sems):
        for cp in copies(outs, sems, lax.axis_index("c")):
            cp.start()

    def wait(ins, outs, sems):
        for cp in copies(outs, sems, 1 - lax.axis_index("c")):
            cp.wait_recv()
        for cp in copies(outs, sems, lax.axis_index("c")):
            cp.wait_send()

    return _Ride(buffers, [_sds(b.shape, b.dtype) for b in buffers], {i: i for i in range(n)},
                 [pltpu.SemaphoreType.DMA((n, 3)), pltpu.SemaphoreType.DMA((n, 3))], start, wait)


def _relayed_gather(shards):
    n = len(shards)
    buffers = [_sds((N_DEV,) + s.shape, s.dtype) for s in shards]
    dma = pltpu.SemaphoreType.DMA

    def remote(src, dst, send_sem, recv_sem, to):
        return pltpu.make_async_remote_copy(src_ref=src, dst_ref=dst, send_sem=send_sem, recv_sem=recv_sem,
                                            device_id=to, device_id_type=MESH)

    def row(px, py, pc):
        return 4 * px + 2 * py + pc

    def ride(operands, aliases, sems, copies):
        def start(ins, outs, sem_refs):
            local, sent = copies(ins, outs, sem_refs, False)
            for cp in local + sent:
                cp.start()

        def wait(ins, outs, sem_refs):
            local, sent = copies(ins, outs, sem_refs, False)
            for cp in copies(ins, outs, sem_refs, True)[1]:
                cp.wait_recv()
            for cp in sent:
                cp.wait_send()
            for cp in local:
                cp.wait()

        return _Ride(operands, buffers, aliases, sems, start, wait)

    def first(ins, outs, sems, landed):
        x, y, c = _place()
        peers = [(x, y, 1 - c), (1 - x, y, c), (x, 1 - y, c)]
        local = [pltpu.make_async_copy(ins[a], outs[a].at[row(x, y, c)], sems[2].at[a]) for a in range(n)]
        return local, [remote(ins[a], outs[a].at[row(*peer) if landed else row(x, y, c)], sems[0].at[a, k], sems[1].at[a, k], peer)
                       for a in range(n) for k, peer in enumerate(peers)]

    def second(ins, outs, sems, landed):
        x, y, c = _place()
        mine = 1 - c if landed else c
        copies = []
        for a in range(n):
            half = shards[a].shape[0] // 2
            over_x, over_y, diagonal = outs[a].at[row(1 - x, y, mine)], outs[a].at[row(x, 1 - y, mine)], outs[a].at[row(1 - x, 1 - y, c)]
            lower, upper = pl.ds(0, half), pl.ds(half, half)
            copies += [remote(over_x, over_x, sems[0].at[a, 0], sems[1].at[a, 0], (x, y, 1 - c)),
                       remote(over_y, over_y, sems[0].at[a, 1], sems[1].at[a, 1], (x, y, 1 - c))]
            if landed:
                copies += [remote(diagonal.at[lower], diagonal.at[lower], sems[0].at[a, 2], sems[1].at[a, 2], (1 - x, y, c)),
                           remote(diagonal.at[upper], diagonal.at[upper], sems[0].at[a, 3], sems[1].at[a, 3], (x, 1 - y, c))]
            else:
                copies += [remote(over_y.at[lower], over_y.at[lower], sems[0].at[a, 2], sems[1].at[a, 2], (1 - x, y, c)),
                           remote(over_x.at[upper], over_x.at[upper], sems[0].at[a, 3], sems[1].at[a, 3], (x, 1 - y, c))]
        return [], copies

    def third(ins, outs, sems, landed):
        x, y, c = _place()
        return [], [remote(outs[a].at[row(1 - x, 1 - y, 1 - c if landed else c)], outs[a].at[row(1 - x, 1 - y, 1 - c if landed else c)],
                           sems[0].at[a], sems[1].at[a], (x, y, 1 - c)) for a in range(n)]

    def later(copies, n_sems):
        return lambda partly: ride(partly, {i: i for i in range(n)}, [dma((n,) + n_sems), dma((n,) + n_sems)], copies)

    return ride(shards, {}, [dma((n, 3)), dma((n, 3)), dma((n,))], first), later(second, (4,)), later(third, ())


def _sibling_swap_ride(grads):
    n = len(grads)

    def copies(ins, outs, sems):
        x, y, c = _place()
        return [pltpu.make_async_remote_copy(
            src_ref=ins[a].at[1 - c], dst_ref=outs[a], send_sem=sems[0].at[a], recv_sem=sems[1].at[a],
            device_id=(x, y, 1 - c), device_id_type=MESH) for a in range(n)]

    def start(ins, outs, sems):
        for cp in copies(ins, outs, sems):
            cp.start()

    def wait(ins, outs, sems):
        for cp in copies(ins, outs, sems):
            cp.wait()

    return _Ride(grads, [_sds(g.shape[1:], g.dtype) for g in grads], {},
                 [pltpu.SemaphoreType.DMA((n,)), pltpu.SemaphoreType.DMA((n,))], start, wait)


def _chip_swap_ride(sums):
    n = len(sums)

    def copies(ins, outs, sems, landed):
        send_sems, recv_sems, local_sems = sems
        x, y, c = _place()
        mine = 2 * x + y
        local = [pltpu.make_async_copy(ins[a].at[mine], outs[a].at[mine], local_sems.at[a]) for a in range(n)]
        remote = [pltpu.make_async_remote_copy(
            src_ref=ins[a].at[2 * px + py], dst_ref=outs[a].at[2 * px + py if landed else mine], send_sem=send_sems.at[a, j],
            recv_sem=recv_sems.at[a, j], device_id=(px, py, c), device_id_type=MESH)
            for a in range(n) for j, (px, py) in enumerate(_other_chips(x, y))]
        return local, remote

    def start(ins, outs, sems):
        local, remote = copies(ins, outs, sems, False)
        for cp in local + remote:
            cp.start()

    def wait(ins, outs, sems):
        local, sent = copies(ins, outs, sems, False)
        for cp in copies(ins, outs, sems, True)[1]:
            cp.wait_recv()
        for cp in sent:
            cp.wait_send()
        for cp in local:
            cp.wait()

    return _Ride(sums, [_sds(s.shape, s.dtype) for s in sums], {},
                 [pltpu.SemaphoreType.DMA((n, 3)), pltpu.SemaphoreType.DMA((n, 3)), pltpu.SemaphoreType.DMA((n,))], start, wait)


def _send_buffers(shards, name):
    n = len(shards)

    def body(*refs):
        for (w, transposed, rows, cols), w_ref, o_ref in zip(shards, refs[:n], refs[n:]):
            if transposed:
                c, r = w.shape
                padded = jnp.concatenate([w_ref[...], jnp.zeros((cols - c, r), F32)], axis=0) if cols > c else w_ref[...]
                o_ref[...] = padded.T.astype(BF16)
            else:
                r, c = w.shape
                if (r, c) != (rows, cols):
                    o_ref[...] = jnp.zeros((rows, cols), BF16)
                o_ref[:r, :c] = w_ref[...].astype(BF16)

    return _pallas_call(body, name=name, out_shape=[_sds((rows, cols), BF16) for _, _, rows, cols in shards])(
        *[w for w, _, _, _ in shards])


def _all_gather(shards, name):
    n = len(shards)
    first, second, third = _relayed_gather(shards)
    levels = [first, second(shards), third(shards)]
    counts = [len(level.sems) for level in levels]

    def body(*refs):
        ins, outs, sems = refs[:n], refs[n:2 * n], refs[2 * n:]
        for i, level in enumerate(levels):
            mine = sems[sum(counts[:i]):sum(counts[:i + 1])]
            level.start(ins, outs, mine)
            level.wait(ins, outs, mine)

    return _pallas_call(
        body, name=name, in_specs=[ANY] * n, out_specs=[ANY] * n, out_shape=first.results,
        scratch_shapes=[s for level in levels for s in level.sems],
    )(*shards)


def _exchange(ride, name):
    n_in, n_out = len(ride.operands), len(ride.results)

    def body(*refs):
        ride.start(refs[:n_in], refs[n_in:n_in + n_out], refs[n_in + n_out:])
        ride.wait(refs[:n_in], refs[n_in:n_in + n_out], refs[n_in + n_out:])

    return _pallas_call(body, name=name, in_specs=[ANY] * n_in, out_specs=[ANY] * n_out, out_shape=ride.results,
                        scratch_shapes=ride.sems, input_output_aliases=ride.aliases)(*ride.operands)


HBM = pl.BlockSpec(memory_space=pltpu.HBM)
SEMAPHORES = pl.BlockSpec(memory_space=pltpu.SEMAPHORE)
IN_FLIGHT = pltpu.CompilerParams(has_side_effects=pltpu.SideEffectType.DATAFLOW_SIDE_EFFECTING)


def _chip_swap_copies(src_ref, land_ref, send_sems, recv_sems, landed):
    x, y, c = _place()
    return [pltpu.make_async_remote_copy(
        src_ref=src_ref.at[2 * px + py], dst_ref=land_ref.at[2 * px + py if landed else 2 * x + y], send_sem=send_sems.at[j],
        recv_sem=recv_sems.at[j], device_id=(px, py, c), device_id_type=MESH) for j, (px, py) in enumerate(_other_chips(x, y))]


def _chip_swap_start(sums, name):
    def body(src_ref, land_ref, send_sems, recv_sems, src_thru, land_thru, token):
        for cp in _chip_swap_copies(src_ref, land_ref, send_sems, recv_sems, False):
            cp.start()
        token[...] = jnp.zeros_like(token)

    return _pallas_call(
        body, name=name,
        out_shape=(pltpu.SemaphoreType.DMA((3,)), pltpu.SemaphoreType.DMA((3,)), pltpu.HBM(sums.shape, sums.dtype),
                   pltpu.HBM(sums.shape, sums.dtype), _sds((8, 128), F32)),
        in_specs=(HBM, HBM), out_specs=(SEMAPHORES, SEMAPHORES, HBM, HBM, pl.BlockSpec(memory_space=pltpu.VMEM)),
        input_output_aliases={0: 2, 1: 3}, compiler_params=IN_FLIGHT,
    )(pltpu.with_memory_space_constraint(sums, pltpu.HBM),
      pltpu.with_memory_space_constraint(lax.empty(sums.shape, sums.dtype), pltpu.HBM))


def _chip_swap_wait(send_sems, recv_sems, sums, landing, after, name):
    def body(src_ref, land_ref, send_sems, recv_sems, *rest):
        for cp in _chip_swap_copies(src_ref, land_ref, send_sems, recv_sems, False):
            cp.wait_send()
        for cp in _chip_swap_copies(src_ref, land_ref, send_sems, recv_sems, True):
            cp.wait_recv()

    return _pallas_call(
        body, name=name, out_shape=(pltpu.HBM(sums.shape, sums.dtype), pltpu.HBM(landing.shape, landing.dtype)),
        in_specs=(HBM, HBM, SEMAPHORES, SEMAPHORES) + (ANY,) * len(after), out_specs=(HBM, HBM),
        input_output_aliases={0: 0, 1: 1}, compiler_params=IN_FLIGHT,
    )(sums, landing, send_sems, recv_sems, *after)


def _pair_sums(gs, rs, core, name):
    n_arrays = len(gs)

    def body(core_ref, *refs):
        for g_ref, r_ref, o_ref in zip(refs[:n_arrays], refs[n_arrays:2 * n_arrays], refs[2 * n_arrays:]):
            o_ref[...] = (g_ref[...].astype(F32) + r_ref[...].astype(F32)).astype(o_ref.dtype)

    def own(g):
        return pl.BlockSpec((None, None) + g.shape[2:], lambda p, core_ref: (core_ref[0], p, 0, 0))

    def chip(g):
        return pl.BlockSpec((None,) + g.shape[2:], lambda p, core_ref: (p, 0, 0))

    return _pallas_call(
        body, name=name,
        grid_spec=pltpu.PrefetchScalarGridSpec(
            num_scalar_prefetch=1, grid=(4,), in_specs=[own(g) for g in gs] + [chip(g) for g in gs],
            out_specs=[chip(g) for g in gs]),
        out_shape=[_sds(g.shape[1:], g.dtype) for g in gs], compiler_params=_cparams(dimension_semantics=("arbitrary",)),
    )(core, *gs, *rs)


def _adamw_math(w, g, m, v):
    m = ADAM_B1 * m + (1.0 - ADAM_B1) * g
    v = ADAM_B2 * v + (1.0 - ADAM_B2) * (g * g)
    m_hat = m / (1.0 - ADAM_B1 ** ADAM_STEP)
    v_hat = v / (1.0 - ADAM_B2 ** ADAM_STEP)
    return -ADAM_LR * (m_hat / (jnp.sqrt(v_hat) + ADAM_EPS) + ADAM_WD * w), m, v


def _adamw(w, m, v, parts, own, chip, name):
    r, c = w.shape
    tr = 256

    def body(chip_ref, w_ref, m_ref, v_ref, p_ref, own_ref, g_out, d_out, m_out, v_out):
        g = jnp.zeros((tr, c), F32)
        for q in range(4):
            g = g + jnp.where(chip_ref[0] == q, own_ref[q], p_ref[q]).astype(F32)
        g_out[...] = g
        d_out[...], m_out[...], v_out[...] = _adamw_math(w_ref[...], g, m_ref[...], v_ref[...])

    tile = pl.BlockSpec((tr, c), lambda i, chip_ref: (i, 0))
    part_tile = pl.BlockSpec((4, tr, c), lambda i, chip_ref: (0, i, 0))
    out = _sds((r, c), F32)
    return _pallas_call(
        body, name=name,
        grid_spec=pltpu.PrefetchScalarGridSpec(num_scalar_prefetch=1, grid=(r // tr,), in_specs=[tile, tile, tile, part_tile, part_tile],
                                               out_specs=(tile,) * 4),
        out_shape=(out,) * 4, compiler_params=_cparams(dimension_semantics=("arbitrary",)),
    )(chip, w, m, v, parts, own)


def _adamw_many(weights, name, ride=None):
    steps = 4
    in_specs, out_specs, out_shape, operands, tiles = [], [], [], [], []
    for w, m, v, parts, transposed in weights:
        n_parts, pr, pc = parts.shape
        if transposed:
            c, r = w.shape
            tile = pl.BlockSpec((c, r // steps), lambda i: (0, i))
            part_tile = pl.BlockSpec((n_parts, r // steps, pc), lambda i: (0, i, 0))
            tiles.append((c, r // steps))
        elif w.shape[0] % (8 * steps) == 0:
            r, c = w.shape
            tile = pl.BlockSpec((r // steps, c), lambda i: (i, 0))
            part_tile = pl.BlockSpec((n_parts, r // steps, pc), lambda i: (0, i, 0))
            tiles.append((r // steps, c))
        else:
            tile = pl.BlockSpec(w.shape, lambda i: (0, 0))
            part_tile = pl.BlockSpec(parts.shape, lambda i: (0, 0, 0))
            tiles.append(w.shape)
        in_specs += [tile, tile, tile, part_tile]
        out_specs += [tile] * 4
        out_shape += [_sds(w.shape, F32)] * 4
        operands += [w, m, v, parts]

    def body(*refs):
        ins, outs = refs[:4 * len(weights)], refs[4 * len(weights):]
        for k, (_, _, _, parts, transposed) in enumerate(weights):
            w_ref, m_ref, v_ref, p_ref = ins[4 * k:4 * k + 4]
            rows, cols = tiles[k]
            if transposed:
                g = p_ref[0].astype(F32)
                for p in range(1, parts.shape[0]):
                    g = g + p_ref[p].astype(F32)
                g = g.T[:rows]
            else:
                g = p_ref[0, :rows, :cols].astype(F32)
                for p in range(1, parts.shape[0]):
                    g = g + p_ref[p, :rows, :cols].astype(F32)
            g_out, d_out, m_out, v_out = outs[4 * k:4 * k + 4]
            g_out[...] = g
            d_out[...], m_out[...], v_out[...] = _adamw_math(w_ref[...], g, m_ref[...], v_ref[...])

    return _call(body, name, (steps,), in_specs, out_specs, out_shape, [], operands, ride)


SMALL = ("ssm_a_re", "ssm_a_im", "ssm_log_dt", "ssm_b_re", "ssm_b_im", "ssm_c_re", "ssm_c_im", "ssm_d",
         "ln1_g", "ln1_b", "ln2_g", "ln2_b")


def _pack_rows(arrays):
    rows = []
    for a in arrays:
        flat = a.reshape(-1)
        rows.append(jnp.pad(flat, (0, -flat.shape[0] % 128)).reshape(-1, 128))
    packed = jnp.concatenate(rows, axis=0)
    return jnp.pad(packed, ((0, -packed.shape[0] % 8), (0, 0)))


def _unpack_rows(packed, shapes):
    out, row = [], 0
    for shape in shapes:
        size = math.prod(shape)
        n_rows = -(-size // 128)
        out.append(packed[row:row + n_rows].reshape(-1)[:size].reshape(shape))
        row += n_rows
    return out


def _sum_devices(parts):
    def body(p_ref, o_ref):
        total = p_ref[0]
        for dev in range(1, N_DEV):
            total = total + p_ref[dev]
        o_ref[...] = total

    return _pallas_call(body, name="sum_devices", out_shape=_sds(parts.shape[1:], F32))(parts)


def _adamw_replicated(ws, ms, vs, gs):
    n = len(ws)

    def body(*refs):
        w_refs, m_refs, v_refs, g_refs, d_out, m_out, v_out = (refs[i * n:(i + 1) * n] for i in range(7))
        for i in range(n):
            d_out[i][...], m_out[i][...], v_out[i][...] = _adamw_math(w_refs[i][...], g_refs[i][...], m_refs[i][...], v_refs[i][...])

    out = _pallas_call(body, name="adamw_replicated", out_shape=[_sds(w.shape, F32) for w in ws] * 3,
                       compiler_params=_cparams())(*ws, *ms, *vs, *gs)
    return out[:n], out[n:2 * n], out[2 * n:]


def kernel(x, w_in, b_gate, w_attn_br, w_ssm_br, w_out, ssm_a_re, ssm_a_im, ssm_log_dt, ssm_b_re, ssm_b_im, ssm_c_re, ssm_c_im, ssm_d, w_glu, ln1_g, ln1_b, w_ff_gate, w_ff_up, w_ff_down, ln2_g, ln2_b, loss_target, m_w_in, m_b_gate, m_w_attn_br, m_w_ssm_br, m_w_out, m_ssm_a_re, m_ssm_a_im, m_ssm_log_dt, m_ssm_b_re, m_ssm_b_im, m_ssm_c_re, m_ssm_c_im, m_ssm_d, m_w_glu, m_ln1_g, m_ln1_b, m_w_ff_gate, m_w_ff_up, m_w_ff_down, m_ln2_g, m_ln2_b, v_w_in, v_b_gate, v_w_attn_br, v_w_ssm_br, v_w_out, v_ssm_a_re, v_ssm_a_im, v_ssm_log_dt, v_ssm_b_re, v_ssm_b_im, v_ssm_c_re, v_ssm_c_im, v_ssm_d, v_w_glu, v_ln1_g, v_ln1_b, v_w_ff_gate, v_w_ff_up, v_w_ff_down, v_ln2_g, v_ln2_b):
    given = dict(locals())
    x2, target = x[0], loss_target[0]
    core = lax.axis_index("c").astype(jnp.int32).reshape(1)

    sharded = ("w_in", "w_attn_br", "w_ssm_br", "w_glu", "w_ff_gate", "w_ff_up", "b_gate", "w_out", "w_ff_down")
    send_shape = dict(w_in=(D_MODEL, 896), w_attn_br=(ATTN_WIDTH, 128), w_ssm_br=(SSM_WIDTH, 128), w_glu=(SSM_WIDTH, 128),
                      w_out=(128, D_MODEL), w_ff_gate=(D_MODEL, FF_PAD), w_ff_up=(D_MODEL, FF_PAD), w_ff_down=(FF_PAD, D_MODEL))
    local = {k: given[k][0] for k in sharded}
    narrow = ("w_ff_gate", "w_ff_up")
    def to_send(k):
        return (local[k].T, True, *send_shape[k]) if k in narrow else (local[k], False, *send_shape[k])

    later = [k for k in sharded if k not in ("w_in", "b_gate")]
    sends = dict(zip(["w_in"] + later, _send_buffers([to_send("w_in")], "send_w_in")
                     + _send_buffers([to_send(k) for k in later], "send_weights")))
    sends["b_gate"] = local["b_gate"]
    mixer_weights = ("w_attn_br", "w_ssm_br", "w_glu", "b_gate", "w_out")
    ff_weights = ("w_ff_gate", "w_ff_up", "w_ff_down")
    wt = {}
    wt["w_in"], = _all_gather([sends["w_in"]], "gather_w_in")

    a_re, a_im, log_dt = ssm_a_re[0], ssm_a_im[0], ssm_log_dt[0].reshape(SSM_GROUPS, 1)
    b_re_t, b_im_t = ssm_b_re[0].transpose(0, 2, 1), ssm_b_im[0].transpose(0, 2, 1)
    abar_re, abar_im, e_re, e_im, bbar_re_t, bbar_im_t = _ssm_prep(a_re, a_im, log_dt, b_re_t, b_im_t)
    bmat, cmat, a_chunks = _ssm_tables(abar_re, abar_im, bbar_re_t, bbar_im_t, ssm_c_re[0], ssm_c_im[0])
    cos_t, sin_t = _rope_tables()

    big_mixer, ff_in = [k for k in mixer_weights if k != "b_gate"], ("w_ff_gate", "w_ff_up")
    n_mixer = len(big_mixer)
    mixer_1, mixer_2, mixer_3 = _relayed_gather([sends[k] for k in big_mixer])
    ff_in_1, ff_in_2, ff_in_3 = _relayed_gather([sends[k] for k in ff_in])
    ff_down_1, ff_down_2, ff_down_3 = _relayed_gather([sends["w_ff_down"]])
    proj, *landed = _proj(x2, wt["w_in"], mixer_1 + _gather_first_level([sends["b_gate"]]))
    mixer, bias = landed[:n_mixer], landed[n_mixer:]
    attn, lse, *landed = _attn_fwd(proj, cos_t, sin_t, mixer_2(mixer) + _gather_second_level(bias) + ff_in_1)
    mixer, b_gate_full, ff = landed[:n_mixer], landed[n_mixer], landed[n_mixer + 1:]
    ys, states, *landed = _ssm_fwd(proj, bmat, cmat, a_chunks, ssm_d, mixer_3(mixer) + ff_in_2(ff) + ff_down_1)
    wt.update(zip(big_mixer, landed[:n_mixer]))
    ff, ff_down = landed[n_mixer:n_mixer + 2], landed[n_mixer + 2:]
    wt["w_out"] = wt["w_out"].reshape(D_MODEL, D_MODEL)
    h, xhat1, rstd1, glu, y_attn, y_ssm, *landed = _mixer_out(
        attn, ys, proj, x2, wt["w_attn_br"], wt["w_ssm_br"], wt["w_glu"], wt["w_out"], b_gate_full, ln1_g, ln1_b,
        ff_in_3(ff) + ff_down_2(ff_down))
    wt.update(zip(ff_in, landed[:2]))
    ff_a, ff_b, ff_f, w_ff_down = _ff_up(h, wt["w_ff_gate"], wt["w_ff_up"], ff_down_3(landed[2:]))
    wt["w_ff_down"] = w_ff_down.reshape(D_FF_PAD, D_MODEL)
    dr2, d_ln2_g, d_ln2_b, loss_lanes = _ff_down_loss(ff_f, wt["w_ff_down"], h, target, ln2_g, ln2_b)

    def pair_sums(names, contrib, from_sibling):
        return _pair_sums([contrib[k] for k in names], from_sibling, core, "pair_sums_" + names[0])

    d_a, d_b = _ff_down_bwd(dr2, wt["w_ff_down"], ff_a, ff_b)
    contrib = dict(w_ff_gate=_weight_grad(h, d_a, "wgrad_w_ff_gate", FF_PAD),
                   w_ff_up=_weight_grad(h, d_b, "wgrad_w_ff_up", FF_PAD),
                   w_ff_down=_weight_grad(ff_f, dr2, "wgrad_w_ff_down"))
    dr1, d_ln1_g, d_ln1_b, *from_sibling = _ff_up_bwd(
        d_a, d_b, wt["w_ff_gate"], wt["w_ff_up"], dr2, xhat1, rstd1, ln1_g, _sibling_swap_ride([contrib[k] for k in ff_weights]))
    ff_sums = pair_sums(ff_weights, contrib, from_sibling)

    d_ya, d_yssm, d_proj, d_attn, d_glu, d_ys, mixed, y_s, gy, d_bg = _mixer_bwd(
        dr1, proj, y_attn, y_ssm, glu, ys, wt["w_attn_br"], wt["w_ssm_br"], wt["w_glu"], wt["w_out"], b_gate_full)
    contrib.update(w_attn_br=_weight_grad(attn, d_ya, "wgrad_w_attn_br", 128),
                   w_ssm_br=_weight_grad(y_s, d_yssm, "wgrad_w_ssm_br", 128),
                   w_glu=_weight_grad(gy, d_glu, "wgrad_w_glu", 128),
                   w_out=_weight_grad(mixed, dr1, "wgrad_w_out"),
                   b_gate=d_bg.reshape(2, 4, 2, 128).transpose(2, 1, 0, 3))
    d_proj, *landed = _attn_bwd(proj, cos_t, sin_t, attn, lse, d_attn, d_proj,
                                _chip_swap_ride(ff_sums) + _sibling_swap_ride([contrib[k] for k in mixer_weights]))
    parts = dict(zip(ff_weights, landed[:len(ff_weights)]))
    mixer_sums = pair_sums(mixer_weights, contrib, landed[len(ff_weights):])
    d_proj, d_bmat, d_cmat, d_abar, d_skip, *landed = _ssm_bwd(d_ys, proj, states, bmat, cmat, a_chunks, ssm_d, d_proj,
                                                               _chip_swap_ride(mixer_sums))
    parts.update(zip(mixer_weights, landed))

    gbb_re_t, gbb_im_t = _block_diag_parts(d_bmat, True)
    gc_re, gc_im = _block_diag_parts(d_cmat, False)
    ga_re = d_abar[:, 0, :CHUNK_STATES].reshape(SSM_GROUPS, SSM_STATE)
    ga_im = d_abar[:, 0, CHUNK_STATES:].reshape(SSM_GROUPS, SSM_STATE)
    g_a_re, g_a_im, g_log_dt, g_b_re_t, g_b_im_t = _ssm_param_bwd(
        a_re, a_im, log_dt, b_re_t, b_im_t, abar_re, abar_im, e_re, e_im, ga_re, ga_im, gbb_re_t, gbb_im_t)
    mine = [g_a_re, g_a_im, g_log_dt, g_b_re_t, g_b_im_t, gc_re, -gc_im,
            d_skip, d_ln1_g, d_ln1_b, d_ln2_g, d_ln2_b]
    small_packed = _pack_rows(mine + [loss_lanes])

    contrib["w_in"], small_partly = _weight_grad(x2, d_proj, "wgrad_w_in", 896, _gather_first_level([small_packed]))
    from_sibling, every = _exchange(_sibling_swap_ride([contrib["w_in"]]) + _gather_second_level([small_partly]),
                                    "swap_w_in_with_sibling")
    w_in_sum, = pair_sums(["w_in"], contrib, [from_sibling])
    send_sems, recv_sems, w_in_sum, landing, token = _chip_swap_start(w_in_sum, "w_in_chip_swap_start")
    others = [k for k in sharded if k != "w_in"]
    updated = _adamw_many(
        [(local[k].T, given["m_" + k][0].T, given["v_" + k][0].T, parts[k], True) if k in narrow
         else (local[k], given["m_" + k][0], given["v_" + k][0], parts[k], False) for k in others],
        "adamw_others", _after(token))
    grad_x, = _grad_x(d_proj, wt["w_in"], dr1, _after(token))
    w_in_sum, landed = _chip_swap_wait(send_sems, recv_sems, w_in_sum, landing, [grad_x, updated[0]], "w_in_chip_swap_wait")
    chip = (2 * lax.axis_index("x") + lax.axis_index("y")).astype(jnp.int32).reshape(1)
    updated += _adamw(local["w_in"], given["m_w_in"][0], given["v_w_in"][0], landed, w_in_sum, chip, "adamw_w_in")

    grads, deltas, new_m, new_v = {}, {}, {}, {}
    for i, k in enumerate(others + ["w_in"]):
        out = [o.T if k in narrow else o for o in updated[4 * i:4 * i + 4]]
        grads[k], deltas[k], new_m[k], new_v[k] = (o.reshape((1,) + local[k].shape) for o in out)

    def held(k, a):
        return a.transpose(0, 1, 3, 2) if k in ("ssm_b_re", "ssm_b_im") else a

    *small_grads, loss_sum = _unpack_rows(_sum_devices(every), [held(k, given[k]).shape for k in SMALL] + [(1, 128)])
    small = _adamw_replicated([held(k, given[k]) for k in SMALL], [held(k, given["m_" + k]) for k in SMALL],
                              [held(k, given["v_" + k]) for k in SMALL], small_grads)
    for res, values in zip((grads, deltas, new_m, new_v), (small_grads,) + small):
        res.update((k, held(k, a)) for k, a in zip(SMALL, values))
    loss = loss_sum[0, 0]

    order = ("w_in", "b_gate", "w_attn_br", "w_ssm_br", "w_out", "ssm_a_re", "ssm_a_im", "ssm_log_dt", "ssm_b_re", "ssm_b_im",
             "ssm_c_re", "ssm_c_im", "ssm_d", "w_glu", "ln1_g", "ln1_b", "w_ff_gate", "w_ff_up", "w_ff_down", "ln2_g", "ln2_b")
    return (loss, grad_x[None], *[grads[k] for k in order], *[deltas[k] for k in order], *[new_m[k] for k in order],
            *[new_v[k] for k in order])
```

```python
import functools
import math

import jax
import jax.numpy as jnp
import numpy as np
from jax import lax
from jax.experimental import pallas as pl
from jax.experimental.pallas import tpu as pltpu

F32 = jnp.float32
BF16 = jnp.bfloat16

N_DEV = 8
SEQ = 2048
D_MODEL = 1024
HEAD_DIM = 64
ATTN_WIDTH = 512
QKV_WIDTH = 1536
SSM_WIDTH = 512
SSM_GROUPS = 32
SSM_GROUP = 16
SSM_STATE = 64
IN_WIDTH = 7168
D_FF = 2816
FF_SHARD = D_FF // N_DEV
FF_PAD = 384
D_FF_PAD = FF_PAD * N_DEV
DN_ALPHA = 2.0 ** 0.25
LN_EPS = 1e-5
NEG_INF = -1e30
ROPE_THETA = 10000.0
BLOCK = 128
GROUPS = ((1, 16), (4, 4), (16, 1))

ADAM_LR = 0.001
ADAM_B1 = 0.9
ADAM_B2 = 0.999
ADAM_EPS = 1e-08
ADAM_WD = 0.01
ADAM_STEP = 10

VMEM_LIMIT = 56 * 1024 * 1024


_pallas_call = pl.pallas_call


def _cparams(**kw):
    return pltpu.CompilerParams(vmem_limit_bytes=VMEM_LIMIT, **kw)


def _dot(a, b):
    return jnp.dot(a, b, preferred_element_type=F32)


def _dot_nt(a, b):
    return lax.dot_general(a, b, (((1,), (1,)), ((), ())), preferred_element_type=F32)


def _side_by_side(w_ref, row=None):
    rows = slice(None) if row is None else pl.ds(row, 1)
    return jnp.concatenate([w_ref[i, rows, :] for i in range(w_ref.shape[0])], axis=1)


def _dot_tn(a, b):
    return lax.dot_general(a, b, (((0,), (0,)), ((), ())), preferred_element_type=F32)


def _rope_tables():
    half = HEAD_DIM // 2
    inv_freq = np.float32(ROPE_THETA) ** (-np.arange(half, dtype=np.float32) / np.float32(half))
    ang = np.arange(SEQ, dtype=np.float32)[:, None] * inv_freq[None, :]
    cos, sin = np.cos(ang).astype(np.float32), np.sin(ang).astype(np.float32)
    tables = np.tile(cos, (1, 4)), np.tile(np.concatenate([-sin, sin], axis=1), (1, 2))

    def by_phase(t):
        return np.stack([t.reshape(SEQ // d, d, 128).transpose(1, 0, 2).reshape(SEQ, 128) for d, _ in GROUPS])

    return jnp.asarray(by_phase(tables[0])), jnp.asarray(by_phase(tables[1]))


def _swap_halves(x):
    lane = lax.broadcasted_iota(jnp.int32, x.shape, 1)
    return jnp.where((lane & 63) < 32, pltpu.roll(x, 96, axis=1), pltpu.roll(x, 32, axis=1))


def _group_rows(d, nb, r, i):
    src = pl.ds(i * BLOCK, BLOCK) if d == 1 else pl.ds(r + i * BLOCK * d, BLOCK, stride=d)
    return src, pl.ds((r * nb + i) * BLOCK, BLOCK)


def _attn_masks():
    a_idx = lax.broadcasted_iota(jnp.int32, (2 * BLOCK, 2 * BLOCK), 0) & (BLOCK - 1)
    c_idx = lax.broadcasted_iota(jnp.int32, (2 * BLOCK, 2 * BLOCK), 1)
    cur_ok = jnp.logical_and(c_idx >= BLOCK, c_idx - BLOCK <= a_idx)
    prev_ok = jnp.logical_and(c_idx < BLOCK, c_idx >= a_idx)
    lane = lax.broadcasted_iota(jnp.int32, (BLOCK, 128), 1)
    return cur_ok, prev_ok, lane < HEAD_DIM


def _stack_heads(t, head0):
    zero = jnp.zeros_like(t)
    return jnp.concatenate([jnp.where(head0, t, zero), jnp.where(head0, zero, t)], axis=0)


def _unstack_heads(t2, head0):
    return jnp.where(head0, t2[:BLOCK], t2[BLOCK:])


def _attn_fwd(proj, cos_t, sin_t, ride=None):
    def body(q0, q1, q2, k0, k1, k2, v0, v1, v2, cos_ref, sin_ref, attn_ref, lse_ref,
             qs, ks, vs, os_, ms, ls, acc, mnat, lnat):
        cur_ok, prev_ok, head0 = _attn_masks()
        ks[:BLOCK, :] = jnp.zeros((BLOCK, 128), BF16)
        vs[:BLOCK, :] = jnp.zeros((BLOCK, 128), BF16)
        for g, (d, nb) in enumerate(GROUPS):
            q_ref, k_ref, v_ref = (q0, q1, q2)[g], (k0, k1, k2)[g], (v0, v1, v2)[g]
            for r in range(d):
                for i in range(nb):
                    src, dst = _group_rows(d, nb, r, i)
                    below = pl.ds(dst.start + BLOCK, BLOCK)
                    c, s = cos_ref[g, dst, :], sin_ref[g, dst, :]
                    q = q_ref[src, :]
                    k = k_ref[src, :]
                    qs[dst, :] = ((q * c + _swap_halves(q) * s) * 0.125).astype(BF16)
                    ks[below, :] = (k * c + _swap_halves(k) * s).astype(BF16)
                    vs[below, :] = v_ref[src, :].astype(BF16)

            def block(b, carry, nb=nb):
                has_prev = (b & (nb - 1)) > 0
                cur = pl.ds(pl.multiple_of(b * BLOCK, BLOCK), BLOCK)
                window = pl.ds(pl.multiple_of(b * BLOCK, BLOCK), 2 * BLOCK)
                valid = jnp.logical_or(cur_ok, jnp.logical_and(prev_ok, has_prev))
                s = jnp.where(valid, _dot_nt(_stack_heads(qs[cur, :], head0), ks[window, :]), NEG_INF)
                m = jnp.max(s, axis=1, keepdims=True)
                p = jnp.exp(s - m)
                os_[cur, :] = _unstack_heads(_dot(p.astype(BF16), vs[window, :]), head0)
                ms[cur, :] = _unstack_heads(m, head0)
                ls[cur, :] = _unstack_heads(jnp.sum(p, axis=1, keepdims=True), head0)
                return carry

            lax.fori_loop(0, SEQ // BLOCK, block, 0, unroll=16)

            for r in range(d):
                for i in range(nb):
                    src, dst = _group_rows(d, nb, r, i)
                    if g == 0:
                        acc[src, :], mnat[src, :], lnat[src, :] = os_[dst, :], ms[dst, :], ls[dst, :]
                    else:
                        m_old, m_g = mnat[src, :], ms[dst, :]
                        m_new = jnp.maximum(m_old, m_g)
                        a_old, a_g = jnp.exp(m_old - m_new), jnp.exp(m_g - m_new)
                        acc[src, :] = a_old * acc[src, :] + a_g * os_[dst, :]
                        lnat[src, :] = a_old * lnat[src, :] + a_g * ls[dst, :]
                        mnat[src, :] = m_new
        for i in range(SEQ // BLOCK):
            rows = pl.ds(i * BLOCK, BLOCK)
            l = lnat[rows, :]
            attn_ref[rows, :] = acc[rows, :] / l
            lse_ref[rows, :] = mnat[rows, :] + jnp.log(l)

    def col(base):
        return pl.BlockSpec((SEQ, 128), lambda hp, base=base: (0, base + hp))

    in_specs = [col(g * 4) for g in range(3)] + [col(12 + g * 4) for g in range(3)] + [col(24 + g * 4) for g in range(3)]
    table = pl.BlockSpec((3, SEQ, 128), lambda hp: (0, 0, 0), pipeline_mode=pl.Buffered(1))
    out = pl.BlockSpec((SEQ, 128), lambda hp: (0, hp))
    return _call(
        body, "attn_fwd", (4,), in_specs + [table, table], [out, out],
        [_sds((SEQ, ATTN_WIDTH), F32), _sds((SEQ, ATTN_WIDTH), F32)],
        [pltpu.VMEM((SEQ, 128), BF16)] + [pltpu.VMEM((SEQ + BLOCK, 128), BF16)] * 2 + [pltpu.VMEM((SEQ, 128), F32)] * 6,
        [proj] * 9 + [cos_t, sin_t], ride)


def _attn_bwd_group_body(g):
    d, nb = GROUPS[g]

    def body(q_ref, k_ref, v_ref, cos_ref, sin_ref, lse_ref, dattn_ref, dsum_ref, dproj_ref,
             qs, ks, vs, dos, lss, dss, dqs, dks, dvs, stage, outs, sems):
        cur_ok, prev_ok, head0 = _attn_masks()
        ks[:BLOCK, :] = jnp.zeros((BLOCK, 128), BF16)
        vs[:BLOCK, :] = jnp.zeros((BLOCK, 128), BF16)
        dks[:BLOCK, :] = jnp.zeros((BLOCK, 128), F32)
        dvs[:BLOCK, :] = jnp.zeros((BLOCK, 128), F32)
        for r in range(d):
            for i in range(nb):
                src, dst = _group_rows(d, nb, r, i)
                below = pl.ds(dst.start + BLOCK, BLOCK)
                c, s = cos_ref[g, dst, :], sin_ref[g, dst, :]
                q = q_ref[src, :]
                k = k_ref[src, :]
                qs[dst, :] = ((q * c + _swap_halves(q) * s) * 0.125).astype(BF16)
                ks[below, :] = (k * c + _swap_halves(k) * s).astype(BF16)
                vs[below, :] = v_ref[src, :].astype(BF16)
                dos[dst, :] = dattn_ref[src, :].astype(BF16)
                dss[dst, :] = dsum_ref[src, :]
                lss[dst, :] = lse_ref[src, :]
                dks[below, :] = jnp.zeros((BLOCK, 128), F32)
                dvs[below, :] = jnp.zeros((BLOCK, 128), F32)

        def per_head_column(t):
            return jnp.concatenate([jnp.max(jnp.where(head0, t, NEG_INF), axis=1, keepdims=True),
                                    jnp.max(jnp.where(head0, NEG_INF, t), axis=1, keepdims=True)], axis=0)

        def block(b, carry):
            has_prev = (b & (nb - 1)) > 0
            cur = pl.ds(pl.multiple_of(b * BLOCK, BLOCK), BLOCK)
            window = pl.ds(pl.multiple_of(b * BLOCK, BLOCK), 2 * BLOCK)
            valid = jnp.logical_or(cur_ok, jnp.logical_and(prev_ok, has_prev))
            q2, do2 = _stack_heads(qs[cur, :], head0), _stack_heads(dos[cur, :], head0)
            kw, vw = ks[window, :], vs[window, :]
            s = jnp.where(valid, _dot_nt(q2, kw), NEG_INF)
            p = jnp.exp(s - per_head_column(lss[cur, :]))
            ds = (p * (_dot_nt(do2, vw) - per_head_column(dss[cur, :]))).astype(BF16)
            dvs[window, :] += _dot_tn(p.astype(BF16), do2)
            dks[window, :] += _dot_tn(ds, q2)
            dqs[cur, :] = _unstack_heads(_dot(ds, kw), head0)
            return carry

        lax.fori_loop(0, SEQ // BLOCK, block, 0, unroll=8)

        hp = pl.program_id(0)
        copies = []
        for kind in range(3):
            for r in range(d):
                for i in range(nb):
                    src, dst = _group_rows(d, nb, r, i)
                    below = pl.ds(dst.start + BLOCK, BLOCK)
                    if kind == 2:
                        stage[src, :] = dvs[below, :]
                    else:
                        c, s = cos_ref[g, dst, :], sin_ref[g, dst, :]
                        t = dqs[dst, :] * 0.125 if kind == 0 else dks[below, :]
                        stage[src, :] = t * c - _swap_halves(t) * s
            for i in range(SEQ // MM_ROWS):
                rows = pl.ds(i * MM_ROWS, MM_ROWS)
                outs[kind, rows, :] = stage[rows, :].astype(BF16)
            column = pl.multiple_of((kind * 12 + g * 4 + hp) * 128, 128)
            copies.append(pltpu.make_async_copy(outs.at[kind], dproj_ref.at[:, pl.ds(column, 128)], sems.at[kind]))
            copies[-1].start()
        for cp in copies:
            cp.wait()

    return body


def _attn_bwd(proj, cos_t, sin_t, attn, lse, dattn, dproj, ride=None):
    groups = [_attn_bwd_group_body(g) for g in range(3)]

    def body(q0, q1, q2, k0, k1, k2, v0, v1, v2, cos_ref, sin_ref, attn_ref, lse_ref, dattn_ref, dproj_in, dproj_ref,
             dsum, *scratch):
        del dproj_in
        head0 = _attn_masks()[2]
        for i in range(SEQ // BLOCK):
            rows = pl.ds(i * BLOCK, BLOCK)
            prod = dattn_ref[rows, :] * attn_ref[rows, :]
            d0 = jnp.sum(jnp.where(head0, prod, 0.0), axis=1, keepdims=True)
            d1 = jnp.sum(jnp.where(head0, 0.0, prod), axis=1, keepdims=True)
            dsum[rows, :] = jnp.where(head0, d0, d1)
        for g in range(3):
            groups[g]((q0, q1, q2)[g], (k0, k1, k2)[g], (v0, v1, v2)[g], cos_ref, sin_ref, lse_ref, dattn_ref, dsum,
                      dproj_ref, *scratch)

    def col(base):
        return pl.BlockSpec((SEQ, 128), lambda hp, base=base: (0, base + hp))

    table = pl.BlockSpec((3, SEQ, 128), lambda hp: (0, 0, 0), pipeline_mode=pl.Buffered(1))
    return _call(
        body, "attn_bwd", (4,),
        [col(g * 4) for g in range(3)] + [col(12 + g * 4) for g in range(3)] + [col(24 + g * 4) for g in range(3)]
        + [table, table, col(0), col(0), col(0), ANY],
        [ANY], [_sds((SEQ, IN_WIDTH), BF16)],
        [pltpu.VMEM((SEQ, 128), F32)]
        + [pltpu.VMEM((SEQ, 128), BF16)] + [pltpu.VMEM((SEQ + BLOCK, 128), BF16)] * 2 + [pltpu.VMEM((SEQ, 128), BF16)]
        + [pltpu.VMEM((SEQ, 128), F32)] * 3 + [pltpu.VMEM((SEQ + BLOCK, 128), F32)] * 2 + [pltpu.VMEM((SEQ, 128), F32)]
        + [pltpu.VMEM((3, SEQ, 128), BF16), pltpu.SemaphoreType.DMA((3,))],
        [proj] * 9 + [cos_t, sin_t, attn, lse, dattn, dproj], ride, aliases={14: 0})


SSM_CHUNKS = 4
CHUNK_STATES = 512
SCAN_ROWS = 8
U_COL = (3 * QKV_WIDTH) // 128


def _cmul(xr, xi, yr, yi):
    return xr * yr - xi * yi, xr * yi + xi * yr


def _ssm_prep(a_re, a_im, log_dt, b_re_t, b_im_t):
    def body(ar_ref, ai_ref, ldt_ref, br_ref, bi_ref, abr_ref, abi_ref, er_ref, ei_ref, bbr_ref, bbi_ref):
        ar, ai = ar_ref[...], ai_ref[...]
        dt = jnp.exp(ldt_ref[...])
        mag = jnp.exp(ar * dt)
        abr, abi = mag * jnp.cos(ai * dt), mag * jnp.sin(ai * dt)
        den = ar * ar + ai * ai
        nr, ni = abr - 1.0, abi
        er, ei = (nr * ar + ni * ai) / den, (ni * ar - nr * ai) / den
        abr_ref[...], abi_ref[...], er_ref[...], ei_ref[...] = abr, abi, er, ei
        er3, ei3 = er[:, None, :], ei[:, None, :]
        br, bi = br_ref[...], bi_ref[...]
        bbr_ref[...] = er3 * br - ei3 * bi
        bbi_ref[...] = er3 * bi + ei3 * br

    gp = jax.ShapeDtypeStruct(a_re.shape, F32)
    gb = jax.ShapeDtypeStruct(b_re_t.shape, F32)
    return _pallas_call(body, name="ssm_prep", out_shape=(gp, gp, gp, gp, gb, gb))(a_re, a_im, log_dt, b_re_t, b_im_t)


def _ssm_param_bwd(a_re, a_im, log_dt, b_re_t, b_im_t, abar_re, abar_im, e_re, e_im, ga_re, ga_im, gbb_re_t, gbb_im_t):
    def body(ar_ref, ai_ref, ldt_ref, br_ref, bi_ref, abr_ref, abi_ref, er_ref, ei_ref, gar_ref, gai_ref, gbr_ref, gbi_ref,
             o_ar, o_ai, o_ldt, o_br, o_bi):
        ar, ai = ar_ref[...], ai_ref[...]
        dt = jnp.exp(ldt_ref[...])
        er, ei = er_ref[...], ei_ref[...]
        br, bi, gbr, gbi = br_ref[...], bi_ref[...], gbr_ref[...], gbi_ref[...]
        er3, ei3 = er[:, None, :], ei[:, None, :]
        o_br[...] = er3 * gbr + ei3 * gbi
        o_bi[...] = er3 * gbi - ei3 * gbr
        ge_r = jnp.sum(br * gbr + bi * gbi, axis=1)
        ge_i = jnp.sum(br * gbi - bi * gbr, axis=1)
        den = ar * ar + ai * ai
        ilr, ili = ar / den, -ai / den
        t_r, t_i = _cmul(ilr, -ili, ge_r, ge_i)
        gab_r, gab_i = gar_ref[...] + t_r, gai_ref[...] + t_i
        gz_r, gz_i = _cmul(abr_ref[...], -abi_ref[...], gab_r, gab_i)
        el_r, el_i = _cmul(er, ei, ilr, ili)
        u_r, u_i = _cmul(el_r, -el_i, ge_r, ge_i)
        o_ar[...] = dt * gz_r - u_r
        o_ai[...] = dt * gz_i - u_i
        o_ldt[...] = jnp.sum(gz_r * ar + gz_i * ai, axis=1, keepdims=True) * dt

    gp = jax.ShapeDtypeStruct(a_re.shape, F32)
    gb = jax.ShapeDtypeStruct(b_re_t.shape, F32)
    return _pallas_call(body, name="ssm_param_bwd", out_shape=(gp, gp, jax.ShapeDtypeStruct(log_dt.shape, F32), gb, gb))(
        a_re, a_im, log_dt, b_re_t, b_im_t, abar_re, abar_im, e_re, e_im, ga_re, ga_im, gbb_re_t, gbb_im_t)


def _block_diag(blocks_re, blocks_im, sign_im, rows_are_channels):
    both = jnp.stack([blocks_re, sign_im * blocks_im]).reshape(2, SSM_CHUNKS, 8, SSM_GROUP, SSM_STATE)
    eye = jnp.eye(8, dtype=F32)
    if rows_are_channels:
        return jnp.einsum("rcghp,gk->cghrkp", both, eye).reshape(SSM_CHUNKS, 128, 2 * CHUNK_STATES)
    return jnp.einsum("rcghp,gk->crkpgh", both, eye).reshape(SSM_CHUNKS, 2 * CHUNK_STATES, 128)


def _block_diag_parts(mat, rows_are_channels):
    if rows_are_channels:
        six = mat.reshape(SSM_CHUNKS, 8, SSM_GROUP, 2, 8, SSM_STATE)
        parts = jnp.einsum("cghrgp->rcghp", six)
    else:
        six = mat.reshape(SSM_CHUNKS, 2, 8, SSM_STATE, 8, SSM_GROUP)
        parts = jnp.einsum("crgpgh->rcghp", six)
    parts = parts.reshape(2, SSM_GROUPS, SSM_GROUP, SSM_STATE)
    return parts[0], parts[1]


def _scan_consts(a_ref, conj, reverse):
    ar = jnp.broadcast_to(a_ref[:, :CHUNK_STATES], (SCAN_ROWS, CHUNK_STATES))
    ai = jnp.broadcast_to(a_ref[:, CHUNK_STATES:], (SCAN_ROWS, CHUNK_STATES))
    if conj:
        ai = -ai
    row = lax.broadcasted_iota(jnp.int32, (SCAN_ROWS, CHUNK_STATES), 0)
    if reverse:
        row = SCAN_ROWS - 1 - row
    zero = jnp.zeros_like(ar)
    steps = []
    pr, pi = ar, ai
    for shift in (1, 2, 4):
        keep = row >= shift
        steps.append((SCAN_ROWS - shift if reverse else shift, jnp.where(keep, pr, zero), jnp.where(keep, pi, zero)))
        pr, pi = _cmul(pr, pi, pr, pi)
    first = row == 0
    return steps, (jnp.where(first, ar, zero), jnp.where(first, ai, zero)), first


def _scan_tile(xr, xi, prev_r, prev_i, steps, carry_in, reverse):
    edge = SCAN_ROWS - 1 if reverse else 1
    cr, ci = pltpu.roll(prev_r, edge, axis=0), pltpu.roll(prev_i, edge, axis=0)
    xr, xi = xr + carry_in[0] * cr - carry_in[1] * ci, xi + carry_in[0] * ci + carry_in[1] * cr
    for shift, mr, mi in steps:
        sr, si = pltpu.roll(xr, shift, axis=0), pltpu.roll(xi, shift, axis=0)
        xr, xi = xr + mr * sr - mi * si, xi + mr * si + mi * sr
    return xr, xi


MM_ROWS = 256


def _ssm_fwd(proj, bmat, cmat, a_chunks, d_skip, ride=None):
    def body(u_ref, b_ref, c_ref, a_ref, d_ref, y_ref, h_ref):
        for i in range(SEQ // MM_ROWS):
            rows = pl.ds(i * MM_ROWS, MM_ROWS)
            h_ref[rows, :] = _dot(u_ref[rows, :].astype(BF16), b_ref[...])
        steps, carry_in, _ = _scan_consts(a_ref, conj=False, reverse=False)

        def tile(k, carry):
            rows = pl.ds(pl.multiple_of(k * SCAN_ROWS, SCAN_ROWS), SCAN_ROWS)
            xr, xi = _scan_tile(h_ref[rows, :CHUNK_STATES], h_ref[rows, CHUNK_STATES:], carry[0], carry[1], steps, carry_in, False)
            h_ref[rows, :CHUNK_STATES] = xr
            h_ref[rows, CHUNK_STATES:] = xi
            return xr, xi

        zero = jnp.zeros((SCAN_ROWS, CHUNK_STATES), F32)
        lax.fori_loop(0, SEQ // SCAN_ROWS, tile, (zero, zero), unroll=4)
        for i in range(SEQ // MM_ROWS):
            rows = pl.ds(i * MM_ROWS, MM_ROWS)
            y_ref[rows, :] = _dot(h_ref[rows, :].astype(BF16), c_ref[...]) + d_ref[...] * u_ref[rows, :]

    return _call(
        body, "ssm_fwd", (SSM_CHUNKS,),
        [pl.BlockSpec((SEQ, 128), lambda c: (0, U_COL + c)),
         pl.BlockSpec((None, 128, 2 * CHUNK_STATES), lambda c: (c, 0, 0)),
         pl.BlockSpec((None, 2 * CHUNK_STATES, 128), lambda c: (c, 0, 0)),
         pl.BlockSpec((None, 1, 2 * CHUNK_STATES), lambda c: (c, 0, 0)),
         pl.BlockSpec((1, 128), lambda c: (0, c))],
        [pl.BlockSpec((SEQ, 128), lambda c: (0, c)), pl.BlockSpec((SEQ, 2 * CHUNK_STATES), lambda c: (0, c))],
        [_sds((SEQ, SSM_WIDTH), F32), _sds((SEQ, SSM_CHUNKS * 2 * CHUNK_STATES), F32)], [],
        [proj, bmat, cmat, a_chunks, d_skip], ride)


def _ssm_bwd(dys, proj, h, bmat, cmat, a_chunks, d_skip, dproj, ride=None):
    def body(dy_ref, u_ref, h_ref, b_ref, c_ref, a_ref, d_ref, dproj_in, du_ref, db_ref, dc_ref, da_ref, dd_ref, g_ref):
        del dproj_in
        dsum = jnp.zeros((1, 128), F32)
        dcm = jnp.zeros((2 * CHUNK_STATES, 128), F32)
        for i in range(SEQ // MM_ROWS):
            rows = pl.ds(i * MM_ROWS, MM_ROWS)
            dy = dy_ref[rows, :]
            g_ref[rows, :] = _dot_nt(dy.astype(BF16), c_ref[...])
            dsum += jnp.sum(dy * u_ref[rows, :], axis=0, keepdims=True)
            dcm += _dot_tn(h_ref[rows, :].astype(BF16), dy.astype(BF16))
        dd_ref[...] = dsum
        dc_ref[...] = dcm
        steps, carry_in, _ = _scan_consts(a_ref, conj=True, reverse=True)
        first_row = lax.broadcasted_iota(jnp.int32, (SCAN_ROWS, CHUNK_STATES), 0) == 0
        n_tiles = SEQ // SCAN_ROWS

        def tile(j, carry):
            k = n_tiles - 1 - j
            rows = pl.ds(pl.multiple_of(k * SCAN_ROWS, SCAN_ROWS), SCAN_ROWS)
            before = pl.ds(pl.multiple_of(jnp.maximum(k - 1, 0) * SCAN_ROWS, SCAN_ROWS), SCAN_ROWS)
            gr, gi = _scan_tile(g_ref[rows, :CHUNK_STATES], g_ref[rows, CHUNK_STATES:], carry[0], carry[1], steps, carry_in, True)
            g_ref[rows, :CHUNK_STATES] = gr
            g_ref[rows, CHUNK_STATES:] = gi
            has_before = jnp.where(k > 0, 1.0, 0.0)
            hr = jnp.where(first_row, pltpu.roll(h_ref[before, :CHUNK_STATES], 1, axis=0) * has_before,
                           pltpu.roll(h_ref[rows, :CHUNK_STATES], 1, axis=0))
            hi = jnp.where(first_row, pltpu.roll(h_ref[before, CHUNK_STATES:], 1, axis=0) * has_before,
                           pltpu.roll(h_ref[rows, CHUNK_STATES:], 1, axis=0))
            return gr, gi, carry[2] + hr * gr + hi * gi, carry[3] + hr * gi - hi * gr

        zero = jnp.zeros((SCAN_ROWS, CHUNK_STATES), F32)
        _, _, sar, sai = lax.fori_loop(0, n_tiles, tile, (zero, zero, zero, zero), unroll=4)
        da_ref[:, :CHUNK_STATES] = jnp.sum(sar, axis=0, keepdims=True)
        da_ref[:, CHUNK_STATES:] = jnp.sum(sai, axis=0, keepdims=True)
        dbm = jnp.zeros((128, 2 * CHUNK_STATES), F32)
        for i in range(SEQ // MM_ROWS):
            rows = pl.ds(i * MM_ROWS, MM_ROWS)
            g = g_ref[rows, :].astype(BF16)
            du_ref[rows, :] = (_dot_nt(g, b_ref[...]) + d_ref[...] * dy_ref[rows, :]).astype(BF16)
            dbm += _dot_tn(u_ref[rows, :].astype(BF16), g)
        db_ref[...] = dbm

    chunk_col = pl.BlockSpec((SEQ, 128), lambda c: (0, c))
    return _call(
        body, "ssm_bwd", (SSM_CHUNKS,),
        [chunk_col,
         pl.BlockSpec((SEQ, 128), lambda c: (0, U_COL + c)),
         pl.BlockSpec((SEQ, 2 * CHUNK_STATES), lambda c: (0, c)),
         pl.BlockSpec((None, 128, 2 * CHUNK_STATES), lambda c: (c, 0, 0)),
         pl.BlockSpec((None, 2 * CHUNK_STATES, 128), lambda c: (c, 0, 0)),
         pl.BlockSpec((None, 1, 2 * CHUNK_STATES), lambda c: (c, 0, 0)),
         pl.BlockSpec((1, 128), lambda c: (0, c)), ANY],
        [pl.BlockSpec((SEQ, 128), lambda c: (0, U_COL + c)),
         pl.BlockSpec((None, 128, 2 * CHUNK_STATES), lambda c: (c, 0, 0)),
         pl.BlockSpec((None, 2 * CHUNK_STATES, 128), lambda c: (c, 0, 0)),
         pl.BlockSpec((None, 1, 2 * CHUNK_STATES), lambda c: (c, 0, 0)),
         pl.BlockSpec((1, 128), lambda c: (0, c))],
        [_sds((SEQ, IN_WIDTH), BF16), _sds((SSM_CHUNKS, 128, 2 * CHUNK_STATES), F32),
         _sds((SSM_CHUNKS, 2 * CHUNK_STATES, 128), F32), _sds((SSM_CHUNKS, 1, 2 * CHUNK_STATES), F32), _sds((1, SSM_WIDTH), F32)],
        [pltpu.VMEM((SEQ, 2 * CHUNK_STATES), F32)], [dys, proj, h, bmat, cmat, a_chunks, d_skip, dproj], ride, aliases={7: 0})


def _ssm_tables(abar_re, abar_im, bbar_re_t, bbar_im_t, c_re, c_im):
    bmat = _block_diag(bbar_re_t, bbar_im_t, 1.0, True).astype(BF16)
    cmat = _block_diag(c_re, c_im, -1.0, False).astype(BF16)
    a_chunks = jnp.concatenate([abar_re.reshape(SSM_CHUNKS, 1, CHUNK_STATES), abar_im.reshape(SSM_CHUNKS, 1, CHUNK_STATES)], axis=2)
    return bmat, cmat, a_chunks


GL_COL = (3 * QKV_WIDTH + SSM_WIDTH) // D_MODEL
GELU_C = math.sqrt(2.0 / math.pi)
GELU_A = 0.044715


def _sds(shape, dtype):
    return jax.ShapeDtypeStruct(shape, dtype)


def _gelu(x):
    t = jnp.tanh(GELU_C * (x + GELU_A * x * x * x))
    return 0.5 * x * (1.0 + t), t


def _gelu_grad(x, t):
    return 0.5 * (1.0 + t) + 0.5 * x * (1.0 - t * t) * GELU_C * (1.0 + 3.0 * GELU_A * x * x)


def _layer_norm(r, g, b):
    mu = jnp.mean(r, axis=-1, keepdims=True)
    xc = r - mu
    rstd = lax.rsqrt(jnp.mean(xc * xc, axis=-1, keepdims=True) + LN_EPS)
    xhat = xc * rstd
    return xhat * g + b, xhat, rstd


def _layer_norm_bwd(dy, xhat, rstd, g):
    dxhat = dy * g
    m1 = jnp.mean(dxhat, axis=-1, keepdims=True)
    m2 = jnp.mean(dxhat * xhat, axis=-1, keepdims=True)
    return rstd * (dxhat - m1 - xhat * m2)


def _proj(x, w_in, ride=None):
    tm, tn = 1024, 1792

    def body(x_ref, w_ref, o_ref):
        o_ref[...] = _dot(x_ref[...].astype(BF16), _side_by_side(w_ref))

    return _call(
        body, "proj", (SEQ // tm, IN_WIDTH // tn),
        [pl.BlockSpec((tm, D_MODEL), lambda i, j: (i, 0)), pl.BlockSpec((2, D_MODEL, tn // 2), lambda i, j: (j, 0, 0))],
        [pl.BlockSpec((tm, tn), lambda i, j: (i, j))], [_sds((SEQ, IN_WIDTH), F32)], [], [x, w_in], ride)


def _row_spec(tm, width, col=0):
    return pl.BlockSpec((tm, width), lambda i, col=col: (i, col))


def _full_spec(shape):
    return pl.BlockSpec(shape, lambda i: (0,) * len(shape))


def _weight_spec(shape):
    return pl.BlockSpec(shape, lambda i: (0,) * len(shape), pipeline_mode=pl.Buffered(1))


def _mixer_out(attn, ys, proj, x, w_ab, w_sb, w_glu, w_out, b_gate, ln_g, ln_b, ride=None):
    tm = 512

    def body(attn_ref, ys_ref, gl0_ref, gl1_ref, x_ref, wab_ref, wsb_ref, wglu_ref, wout_ref, bg_ref, g_ref, b_ref,
             h_ref, xhat_ref, rstd_ref, glu_ref, ya_ref, yssm_ref):
        gy, _ = _gelu(ys_ref[...])
        glu = _dot(gy.astype(BF16), _side_by_side(wglu_ref))
        glu_ref[...] = glu
        y_s = glu[:, :SSM_WIDTH] * jax.nn.sigmoid(glu[:, SSM_WIDTH:])
        y_ssm = _dot(y_s.astype(BF16), _side_by_side(wsb_ref))
        y_attn = _dot(attn_ref[...].astype(BF16), _side_by_side(wab_ref))
        ya_ref[...] = y_attn
        yssm_ref[...] = y_ssm
        g0 = jax.nn.sigmoid(gl0_ref[...] + _side_by_side(bg_ref, 0))
        g1 = jax.nn.sigmoid(gl1_ref[...] + _side_by_side(bg_ref, 1))
        mixed = g0 * y_attn + g1 * y_ssm
        r1 = DN_ALPHA * x_ref[...] + _dot(mixed.astype(BF16), wout_ref[...])
        h, xhat, rstd = _layer_norm(r1, g_ref[...], b_ref[...])
        h_ref[...] = h
        xhat_ref[...] = xhat
        rstd_ref[...] = jnp.broadcast_to(rstd, (tm, 128))

    wide = _sds((SEQ, D_MODEL), F32)
    return _call(
        body, "mixer_out", (SEQ // tm,),
        [_row_spec(tm, ATTN_WIDTH), _row_spec(tm, SSM_WIDTH), _row_spec(tm, D_MODEL, GL_COL), _row_spec(tm, D_MODEL, GL_COL + 1),
         _row_spec(tm, D_MODEL), _weight_spec((N_DEV, ATTN_WIDTH, 128)), _weight_spec((N_DEV, SSM_WIDTH, 128)),
         _weight_spec((N_DEV, SSM_WIDTH, 128)), _weight_spec((D_MODEL, D_MODEL)), _full_spec((N_DEV, 2, 128)),
         _full_spec((1, D_MODEL)), _full_spec((1, D_MODEL))],
        [_row_spec(tm, D_MODEL), _row_spec(tm, D_MODEL), _row_spec(tm, 128), _row_spec(tm, D_MODEL),
         _row_spec(tm, D_MODEL), _row_spec(tm, D_MODEL)],
        [wide, wide, _sds((SEQ, 128), F32), wide, wide, wide], [],
        [attn, ys, proj, proj, x, w_ab, w_sb, w_glu, w_out, b_gate, ln_g, ln_b], ride)


def _ff_up(h, w_gate, w_up, ride=None):
    tm, tn = 1024, 768

    def body(h_ref, wg_ref, wu_ref, a_ref, b_ref, f_ref):
        hb = h_ref[...].astype(BF16)
        a, b = _dot(hb, _side_by_side(wg_ref)), _dot(hb, _side_by_side(wu_ref))
        a_ref[...] = a.astype(BF16)
        b_ref[...] = b.astype(BF16)
        f_ref[...] = (a * jax.nn.sigmoid(a) * b).astype(BF16)

    tile = pl.BlockSpec((tm, tn), lambda i, j: (i, j))
    wtile = pl.BlockSpec((tn // FF_PAD, D_MODEL, FF_PAD), lambda i, j: (j, 0, 0))
    out = _sds((SEQ, D_FF_PAD), BF16)
    return _call(body, "ff_up", (SEQ // tm, D_FF_PAD // tn), [pl.BlockSpec((tm, D_MODEL), lambda i, j: (i, 0)), wtile, wtile],
                 [tile, tile, tile], [out, out, out], [], [h, w_gate, w_up], ride)


def _ff_down_loss(f, w_down, h, target, ln_g, ln_b):
    tm = 512

    def body(f_ref, w_ref, h_ref, t_ref, g_ref, b_ref, dr_ref, dg_ref, db_ref, loss_ref):
        @pl.when(pl.program_id(0) == 0)
        def _():
            dg_ref[...] = jnp.zeros_like(dg_ref)
            db_ref[...] = jnp.zeros_like(db_ref)
            loss_ref[...] = jnp.zeros_like(loss_ref)

        r2 = DN_ALPHA * h_ref[...] + _dot(f_ref[...], w_ref[...])
        g = g_ref[...]
        out, xhat, rstd = _layer_norm(r2, g, b_ref[...])
        err = out - t_ref[...]
        loss_ref[...] += 0.5 * jnp.sum(jnp.mean(err * err, axis=-1, keepdims=True), axis=0, keepdims=True)
        dout = err * (1.0 / D_MODEL)
        dg_ref[...] += jnp.sum(dout * xhat, axis=0, keepdims=True)
        db_ref[...] += jnp.sum(dout, axis=0, keepdims=True)
        dr_ref[...] = _layer_norm_bwd(dout, xhat, rstd, g)

    vec = _sds((1, D_MODEL), F32)
    return _pallas_call(
        body, name="ff_down_loss", grid=(SEQ // tm,),
        in_specs=[_row_spec(tm, D_FF_PAD), _weight_spec((D_FF_PAD, D_MODEL)), _row_spec(tm, D_MODEL), _row_spec(tm, D_MODEL),
                  _full_spec((1, D_MODEL)), _full_spec((1, D_MODEL))],
        out_specs=(_row_spec(tm, D_MODEL), _full_spec((1, D_MODEL)), _full_spec((1, D_MODEL)), _full_spec((1, 128))),
        out_shape=(_sds((SEQ, D_MODEL), F32), vec, vec, _sds((1, 128), F32)),
        compiler_params=_cparams(dimension_semantics=("arbitrary",)),
    )(f, w_down, h, target, ln_g, ln_b)


def _ff_down_bwd(dr2, w_down, a, b):
    tm, tn = 1024, 768

    def body(dr_ref, w_ref, a_ref, b_ref, da_ref, db_ref):
        df = _dot_nt(dr_ref[...].astype(BF16), w_ref[...])
        av, bv = a_ref[...].astype(F32), b_ref[...].astype(F32)
        sg = jax.nn.sigmoid(av)
        da_ref[...] = (df * bv * sg * (1.0 + av * (1.0 - sg))).astype(BF16)
        db_ref[...] = (df * av * sg).astype(BF16)

    tile = pl.BlockSpec((tm, tn), lambda i, j: (i, j))
    out = _sds((SEQ, D_FF_PAD), BF16)
    return _pallas_call(
        body, name="ff_down_bwd", grid=(SEQ // tm, D_FF_PAD // tn),
        in_specs=[pl.BlockSpec((tm, D_MODEL), lambda i, j: (i, 0)), pl.BlockSpec((tn, D_MODEL), lambda i, j: (j, 0)), tile, tile],
        out_specs=(tile, tile), out_shape=(out, out),
        compiler_params=_cparams(dimension_semantics=("arbitrary", "arbitrary")),
    )(dr2, w_down, a, b)


def _ff_up_bwd(da, db, w_gate, w_up, dr2, xhat1, rstd1, ln_g, ride=None):
    tm, tk = 1024, 768
    nk = D_FF_PAD // tk

    def body(da_ref, db_ref, wg_ref, wu_ref, dr2_ref, xhat_ref, rstd_ref, g_ref, dr1_ref, dg_ref, dbias_ref, acc):
        i, k = pl.program_id(0), pl.program_id(1)

        @pl.when(jnp.logical_and(i == 0, k == 0))
        def _():
            dg_ref[...] = jnp.zeros_like(dg_ref)
            dbias_ref[...] = jnp.zeros_like(dbias_ref)

        part = _dot_nt(da_ref[...], _side_by_side(wg_ref)) + _dot_nt(db_ref[...], _side_by_side(wu_ref))

        @pl.when(k == 0)
        def _():
            acc[...] = part

        @pl.when(k > 0)
        def _():
            acc[...] += part

        @pl.when(k == nk - 1)
        def _():
            dh = DN_ALPHA * dr2_ref[...] + acc[...]
            xhat = xhat_ref[...]
            dg_ref[...] += jnp.sum(dh * xhat, axis=0, keepdims=True)
            dbias_ref[...] += jnp.sum(dh, axis=0, keepdims=True)
            rstd = jnp.max(rstd_ref[...], axis=1, keepdims=True)
            dr1_ref[...] = _layer_norm_bwd(dh, xhat, rstd, g_ref[...])

    hid = pl.BlockSpec((tm, tk), lambda i, k: (i, k))
    wtile = pl.BlockSpec((tk // FF_PAD, D_MODEL, FF_PAD), lambda i, k: (k, 0, 0))
    row = pl.BlockSpec((tm, D_MODEL), lambda i, k: (i, 0))
    vec = pl.BlockSpec((1, D_MODEL), lambda i, k: (0, 0))
    return _call(
        body, "ff_up_bwd", (SEQ // tm, nk),
        [hid, hid, wtile, wtile, row, row, pl.BlockSpec((tm, 128), lambda i, k: (i, 0)), vec],
        [row, vec, vec], [_sds((SEQ, D_MODEL), F32), _sds((1, D_MODEL), F32), _sds((1, D_MODEL), F32)],
        [pltpu.VMEM((tm, D_MODEL), F32)], [da, db, w_gate, w_up, dr2, xhat1, rstd1, ln_g], ride)


def _mixer_bwd(dr1, proj, y_attn, y_ssm, glu, ys, w_ab, w_sb, w_glu, w_out, b_gate):
    tm = 256

    def body(dr1_ref, gl0_ref, gl1_ref, ya_ref, yssm_ref, glu_ref, ys_ref, wab_ref, wsb_ref, wglu_ref, wout_ref, bg_ref,
             dya_ref, dyssm_ref, dgl_ref, dattn_ref, dglu_ref, dys_ref, mixed_ref, ysb_ref, gy_ref, dbg_ref):
        @pl.when(pl.program_id(0) == 0)
        def _():
            dbg_ref[...] = jnp.zeros_like(dbg_ref)

        dmixed = _dot_nt(dr1_ref[...].astype(BF16), wout_ref[...])
        g0 = jax.nn.sigmoid(gl0_ref[...] + _side_by_side(bg_ref, 0))
        g1 = jax.nn.sigmoid(gl1_ref[...] + _side_by_side(bg_ref, 1))
        y_attn, y_ssm = ya_ref[...], yssm_ref[...]
        mixed_ref[...] = (g0 * y_attn + g1 * y_ssm).astype(BF16)
        dya = (dmixed * g0).astype(BF16)
        dyssm = (dmixed * g1).astype(BF16)
        dya_ref[...] = dya
        dyssm_ref[...] = dyssm
        dgl0 = dmixed * y_attn * g0 * (1.0 - g0)
        dgl1 = dmixed * y_ssm * g1 * (1.0 - g1)
        dgl_ref[:, :GL_COL * D_MODEL] = jnp.zeros((tm, GL_COL * D_MODEL), BF16)
        dgl_ref[:, GL_COL * D_MODEL:(GL_COL + 1) * D_MODEL] = dgl0.astype(BF16)
        dgl_ref[:, (GL_COL + 1) * D_MODEL:] = dgl1.astype(BF16)
        dbg_ref[:, :D_MODEL] += jnp.sum(dgl0, axis=0, keepdims=True)
        dbg_ref[:, D_MODEL:] += jnp.sum(dgl1, axis=0, keepdims=True)
        dattn_ref[...] = _dot_nt(dya, _side_by_side(wab_ref))
        dy_s = _dot_nt(dyssm, _side_by_side(wsb_ref))
        glu = glu_ref[...]
        glu1, sg = glu[:, :SSM_WIDTH], jax.nn.sigmoid(glu[:, SSM_WIDTH:])
        ysb_ref[...] = (glu1 * sg).astype(BF16)
        dglu1 = (dy_s * sg).astype(BF16)
        dglu2 = (dy_s * glu1 * sg * (1.0 - sg)).astype(BF16)
        dglu_ref[:, :SSM_WIDTH] = dglu1
        dglu_ref[:, SSM_WIDTH:] = dglu2
        dgy = _dot_nt(jnp.concatenate([dglu1, dglu2], axis=1), _side_by_side(wglu_ref))
        ys = ys_ref[...]
        gy, t = _gelu(ys)
        gy_ref[...] = gy.astype(BF16)
        dys_ref[...] = dgy * _gelu_grad(ys, t)

    wide_b, half_b = _sds((SEQ, D_MODEL), BF16), _sds((SEQ, SSM_WIDTH), BF16)
    half_f = _sds((SEQ, SSM_WIDTH), F32)
    return _pallas_call(
        body, name="mixer_bwd", grid=(SEQ // tm,),
        in_specs=[_row_spec(tm, D_MODEL), _row_spec(tm, D_MODEL, GL_COL), _row_spec(tm, D_MODEL, GL_COL + 1), _row_spec(tm, D_MODEL),
                  _row_spec(tm, D_MODEL), _row_spec(tm, D_MODEL), _row_spec(tm, SSM_WIDTH), _full_spec((N_DEV, ATTN_WIDTH, 128)),
                  _full_spec((N_DEV, SSM_WIDTH, 128)), _full_spec((N_DEV, SSM_WIDTH, 128)), _full_spec((D_MODEL, D_MODEL)),
                  _full_spec((N_DEV, 2, 128))],
        out_specs=(_row_spec(tm, D_MODEL), _row_spec(tm, D_MODEL), _row_spec(tm, IN_WIDTH), _row_spec(tm, ATTN_WIDTH),
                   _row_spec(tm, D_MODEL), _row_spec(tm, SSM_WIDTH), _row_spec(tm, D_MODEL), _row_spec(tm, SSM_WIDTH),
                   _row_spec(tm, SSM_WIDTH), _full_spec((1, 2 * D_MODEL))),
        out_shape=(wide_b, wide_b, _sds((SEQ, IN_WIDTH), BF16), half_f, wide_b, half_f, wide_b, half_b, half_b,
                   _sds((1, 2 * D_MODEL), F32)),
        compiler_params=_cparams(dimension_semantics=("arbitrary",)),
    )(dr1, proj, proj, y_attn, y_ssm, glu, ys, w_ab, w_sb, w_glu, w_out, b_gate)


def _grad_x(dproj, w_in, dr1, ride=None):
    tm, tk = 1024, 1792
    nk = IN_WIDTH // tk

    def body(dp_ref, w_ref, dr1_ref, o_ref, acc):
        k = pl.program_id(1)
        part = _dot_nt(dp_ref[...], _side_by_side(w_ref))

        @pl.when(k == 0)
        def _():
            acc[...] = part

        @pl.when(k > 0)
        def _():
            acc[...] += part

        @pl.when(k == nk - 1)
        def _():
            o_ref[...] = DN_ALPHA * dr1_ref[...] + acc[...]

    row = pl.BlockSpec((tm, D_MODEL), lambda i, k: (i, 0))
    return _call(
        body, "grad_x", (SEQ // tm, nk),
        [pl.BlockSpec((tm, tk), lambda i, k: (i, k)), pl.BlockSpec((2, D_MODEL, tk // 2), lambda i, k: (k, 0, 0)), row],
        [row], [_sds((SEQ, D_MODEL), F32)], [pltpu.VMEM((tm, D_MODEL), F32)], [dproj, w_in, dr1], ride)


def _weight_grad(a, b, name, shard_cols=None, ride=None):
    k, n = a.shape[1], b.shape[1]
    tk = min(k, 512) if shard_cols else k // N_DEV
    tn = n // 4 if shard_cols else min(n, 1024)

    def body(a_ref, b_ref, o_ref):
        grad = _dot_tn(a_ref[...].astype(BF16), b_ref[...].astype(BF16))
        if shard_cols:
            o_ref[0] = grad[:, :shard_cols].astype(BF16)
            o_ref[1] = grad[:, shard_cols:].astype(BF16)
        else:
            o_ref[...] = grad.astype(BF16)

    if shard_cols:
        out_spec = pl.BlockSpec((2, None, tk, shard_cols), lambda kk, j: (0, j, kk, 0))
        out_shape = _sds((2, 4, k, shard_cols), BF16)
    else:
        out_spec = pl.BlockSpec((None, None, tk, tn), lambda kk, j: (kk % 2, kk // 2, 0, j))
        out_shape = _sds((2, 4, tk, n), BF16)
    out = _call(body, name, (k // tk, n // tn),
                [pl.BlockSpec((SEQ, tk), lambda kk, j: (0, kk)), pl.BlockSpec((SEQ, tn), lambda kk, j: (0, j))],
                [out_spec], [out_shape], [], [a, b], ride)
    return out[0] if ride is None else out


MESH = pl.DeviceIdType.MESH
ANY = pl.BlockSpec(memory_space=pl.ANY)


def _place():
    return lax.axis_index("x"), lax.axis_index("y"), lax.axis_index("c")


def _other_chips(x, y):
    return [(1 - x, y), (x, 1 - y), (1 - x, 1 - y)]


class _Ride:
    def __init__(self, operands, results, aliases, sems, start, wait):
        self.operands, self.results, self.aliases, self.sems = list(operands), list(results), dict(aliases), list(sems)
        self.start, self.wait = start, wait

    def __add__(self, other):
        n_in, n_out, n_sem = len(self.operands), len(self.results), len(self.sems)

        def both(which):
            def run(ins, outs, sems):
                getattr(self, which)(ins[:n_in], outs[:n_out], sems[:n_sem])
                getattr(other, which)(ins[n_in:], outs[n_out:], sems[n_sem:])
            return run

        aliases = {**self.aliases, **{n_in + i: n_out + j for i, j in other.aliases.items()}}
        return _Ride(self.operands + other.operands, self.results + other.results, aliases, self.sems + other.sems,
                     both("start"), both("wait"))


def _call(body, name, grid, in_specs, out_specs, out_shape, scratch_shapes, operands, ride=None, aliases=None):
    in_specs, out_specs, out_shape = list(in_specs), list(out_specs), list(out_shape)
    scratch_shapes, operands, aliases = list(scratch_shapes), list(operands), dict(aliases or {})
    kernel_body = body
    if ride is not None:
        n_in, n_out, n_scr, r_in, r_out = len(in_specs), len(out_specs), len(scratch_shapes), len(ride.operands), len(ride.results)

        def kernel_body(*refs):
            out0, scr0 = n_in + r_in, n_in + r_in + n_out + r_out
            ride_refs = (refs[n_in:out0], refs[out0 + n_out:scr0], refs[scr0 + n_scr:])
            ids = [pl.program_id(i) for i in range(len(grid))]
            first = functools.reduce(jnp.logical_and, [i == 0 for i in ids])
            last = functools.reduce(jnp.logical_and, [i == g - 1 for i, g in zip(ids, grid)])

            @pl.when(first)
            def _():
                ride.start(*ride_refs)

            body(*refs[:n_in], *refs[out0:out0 + n_out], *refs[scr0:scr0 + n_scr])

            @pl.when(last)
            def _():
                ride.wait(*ride_refs)

        aliases.update({n_in + i: n_out + j for i, j in ride.aliases.items()})
        in_specs += [ANY] * r_in
        out_specs += [ANY] * r_out
        out_shape += ride.results
        scratch_shapes += ride.sems
        operands += ride.operands
    return _pallas_call(
        kernel_body, name=name, grid=grid, in_specs=in_specs, out_specs=out_specs, out_shape=out_shape,
        scratch_shapes=scratch_shapes, input_output_aliases=aliases,
        compiler_params=_cparams(dimension_semantics=("arbitrary",) * len(grid)),
    )(*operands)


def _after(*arrays):
    return _Ride(arrays, [], {}, [], lambda *refs: None, lambda *refs: None)


def _gather_first_level(shards):
    n = len(shards)

    def copies(ins, outs, sems, landed):
        send_sems, recv_sems, local_sems = sems
        x, y, c = _place()
        peers = [(x, y, 1 - c)] + [(px, py, c) for px, py in _other_chips(x, y)]

        def row(peer):
            return 4 * x + 2 * y + c if not landed else 4 * peer[0] + 2 * peer[1] + peer[2]

        local = [pltpu.make_async_copy(ins[a], outs[a].at[4 * x + 2 * y + c], local_sems.at[a]) for a in range(n)]
        remote = [pltpu.make_async_remote_copy(
            src_ref=ins[a], dst_ref=outs[a].at[row(peer)], send_sem=send_sems.at[a, k], recv_sem=recv_sems.at[a, k],
            device_id=peer, device_id_type=MESH) for a in range(n) for k, peer in enumerate(peers)]
        return local, remote

    def start(ins, outs, sems):
        local, remote = copies(ins, outs, sems, False)
        for cp in local + remote:
            cp.start()

    def wait(ins, outs, sems):
        local, sent = copies(ins, outs, sems, False)
        for cp in copies(ins, outs, sems, True)[1]:
            cp.wait_recv()
        for cp in sent:
            cp.wait_send()
        for cp in local:
            cp.wait()

    return _Ride(shards, [_sds((N_DEV,) + s.shape, s.dtype) for s in shards], {},
                 [pltpu.SemaphoreType.DMA((n, 4)), pltpu.SemaphoreType.DMA((n, 4)), pltpu.SemaphoreType.DMA((n,))], start, wait)


def _gather_second_level(buffers):
    n = len(buffers)

    def copies(outs, sems, core):
        send_sems, recv_sems = sems
        x, y, c = _place()
        return [pltpu.make_async_remote_copy(
            src_ref=outs[a].at[4 * px + 2 * py + core], dst_ref=outs[a].at[4 * px + 2 * py + core], send_sem=send_sems.at[a, j],
            recv_sem=recv_sems.at[a, j], device_id=(x, y, 1 - c), device_id_type=MESH)
            for a in range(n) for j, (px, py) in enumerate(_other_chips(x, y))]

    def start(ins, outs, sems):
        for cp in copies(outs, sems, lax.axis_index("c")):
            cp.start()

    def wait(ins, outs, sems):
        for cp in copies(outs, sems, 1 - lax.axis_index("c")):
            cp.wait_recv()
        for cp in copies(outs, sems, lax.axis_index("c")):
            cp.wait_send()

    return _Ride(buffers, [_sds(b.shape, b.dtype) for b in buffers], {i: i for i in range(n)},
                 [pltpu.SemaphoreType.DMA((n, 3)), pltpu.SemaphoreType.DMA((n, 3))], start, wait)


def _relayed_gather(shards):
    n = len(shards)
    buffers = [_sds((N_DEV,) + s.shape, s.dtype) for s in shards]
    dma = pltpu.SemaphoreType.DMA

    def remote(src, dst, send_sem, recv_sem, to):
        return pltpu.make_async_remote_copy(src_ref=src, dst_ref=dst, send_sem=send_sem, recv_sem=recv_sem,
                                            device_id=to, device_id_type=MESH)

    def row(px, py, pc):
        return 4 * px + 2 * py + pc

    def ride(operands, aliases, sems, copies):
        def start(ins, outs, sem_refs):
            local, sent = copies(ins, outs, sem_refs, False)
            for cp in local + sent:
                cp.start()

        def wait(ins, outs, sem_refs):
            local, sent = copies(ins, outs, sem_refs, False)
            for cp in copies(ins, outs, sem_refs, True)[1]:
                cp.wait_recv()
            for cp in sent:
                cp.wait_send()
            for cp in local:
                cp.wait()

        return _Ride(operands, buffers, aliases, sems, start, wait)

    def first(ins, outs, sems, landed):
        x, y, c = _place()
        peers = [(x, y, 1 - c), (1 - x, y, c), (x, 1 - y, c)]
        local = [pltpu.make_async_copy(ins[a], outs[a].at[row(x, y, c)], sems[2].at[a]) for a in range(n)]
        return local, [remote(ins[a], outs[a].at[row(*peer) if landed else row(x, y, c)], sems[0].at[a, k], sems[1].at[a, k], peer)
                       for a in range(n) for k, peer in enumerate(peers)]

    def second(ins, outs, sems, landed):
        x, y, c = _place()
        mine = 1 - c if landed else c
        copies = []
        for a in range(n):
            half = shards[a].shape[0] // 2
            over_x, over_y, diagonal = outs[a].at[row(1 - x, y, mine)], outs[a].at[row(x, 1 - y, mine)], outs[a].at[row(1 - x, 1 - y, c)]
            lower, upper = pl.ds(0, half), pl.ds(half, half)
            copies += [remote(over_x, over_x, sems[0].at[a, 0], sems[1].at[a, 0], (x, y, 1 - c)),
                       remote(over_y, over_y, sems[0].at[a, 1], sems[1].at[a, 1], (x, y, 1 - c))]
            if landed:
                copies += [remote(diagonal.at[lower], diagonal.at[lower], sems[0].at[a, 2], sems[1].at[a, 2], (1 - x, y, c)),
                           remote(diagonal.at[upper], diagonal.at[upper], sems[0].at[a, 3], sems[1].at[a, 3], (x, 1 - y, c))]
            else:
                copies += [remote(over_y.at[lower], over_y.at[lower], sems[0].at[a, 2], sems[1].at[a, 2], (1 - x, y, c)),
                           remote(over_x.at[upper], over_x.at[upper], sems[0].at[a, 3], sems[1].at[a, 3], (x, 1 - y, c))]
        return [], copies

    def third(ins, outs, sems, landed):
        x, y, c = _place()
        return [], [remote(outs[a].at[row(1 - x, 1 - y, 1 - c if landed else c)], outs[a].at[row(1 - x, 1 - y, 1 - c if landed else c)],
                           sems[0].at[a], sems[1].at[a], (x, y, 1 - c)) for a in range(n)]

    def later(copies, n_sems):
        return lambda partly: ride(partly, {i: i for i in range(n)}, [dma((n,) + n_sems), dma((n,) + n_sems)], copies)

    return ride(shards, {}, [dma((n, 3)), dma((n, 3)), dma((n,))], first), later(second, (4,)), later(third, ())


def _sibling_swap_ride(grads):
    n = len(grads)

    def copies(ins, outs, sems):
        x, y, c = _place()
        return [pltpu.make_async_remote_copy(
            src_ref=ins[a].at[1 - c], dst_ref=outs[a], send_sem=sems[0].at[a], recv_sem=sems[1].at[a],
            device_id=(x, y, 1 - c), device_id_type=MESH) for a in range(n)]

    def start(ins, outs, sems):
        for cp in copies(ins, outs, sems):
            cp.start()

    def wait(ins, outs, sems):
        for cp in copies(ins, outs, sems):
            cp.wait()

    return _Ride(grads, [_sds(g.shape[1:], g.dtype) for g in grads], {},
                 [pltpu.SemaphoreType.DMA((n,)), pltpu.SemaphoreType.DMA((n,))], start, wait)


def _chip_swap_ride(sums):
    n = len(sums)

    def copies(ins, outs, sems, landed):
        send_sems, recv_sems, local_sems = sems
        x, y, c = _place()
        mine = 2 * x + y
        local = [pltpu.make_async_copy(ins[a].at[mine], outs[a].at[mine], local_sems.at[a]) for a in range(n)]
        remote = [pltpu.make_async_remote_copy(
            src_ref=ins[a].at[2 * px + py], dst_ref=outs[a].at[2 * px + py if landed else mine], send_sem=send_sems.at[a, j],
            recv_sem=recv_sems.at[a, j], device_id=(px, py, c), device_id_type=MESH)
            for a in range(n) for j, (px, py) in enumerate(_other_chips(x, y))]
        return local, remote

    def start(ins, outs, sems):
        local, remote = copies(ins, outs, sems, False)
        for cp in local + remote:
            cp.start()

    def wait(ins, outs, sems):
        local, sent = copies(ins, outs, sems, False)
        for cp in copies(ins, outs, sems, True)[1]:
            cp.wait_recv()
        for cp in sent:
            cp.wait_send()
        for cp in local:
            cp.wait()

    return _Ride(sums, [_sds(s.shape, s.dtype) for s in sums], {},
                 [pltpu.SemaphoreType.DMA((n, 3)), pltpu.SemaphoreType.DMA((n, 3)), pltpu.SemaphoreType.DMA((n,))], start, wait)


def _send_buffers(shards, name):
    n = len(shards)

    def body(*refs):
        for (w, transposed, rows, cols), w_ref, o_ref in zip(shards, refs[:n], refs[n:]):
            if transposed:
                c, r = w.shape
                padded = jnp.concatenate([w_ref[...], jnp.zeros((cols - c, r), F32)], axis=0) if cols > c else w_ref[...]
                o_ref[...] = padded.T.astype(BF16)
            else:
                r, c = w.shape
                if (r, c) != (rows, cols):
                    o_ref[...] = jnp.zeros((rows, cols), BF16)
                o_ref[:r, :c] = w_ref[...].astype(BF16)

    return _pallas_call(body, name=name, out_shape=[_sds((rows, cols), BF16) for _, _, rows, cols in shards])(
        *[w for w, _, _, _ in shards])


def _all_gather(shards, name):
    n = len(shards)
    first, second, third = _relayed_gather(shards)
    levels = [first, second(shards), third(shards)]
    counts = [len(level.sems) for level in levels]

    def body(*refs):
        ins, outs, sems = refs[:n], refs[n:2 * n], refs[2 * n:]
        for i, level in enumerate(levels):
            mine = sems[sum(counts[:i]):sum(counts[:i + 1])]
            level.start(ins, outs, mine)
            level.wait(ins, outs, mine)

    return _pallas_call(
        body, name=name, in_specs=[ANY] * n, out_specs=[ANY] * n, out_shape=first.results,
        scratch_shapes=[s for level in levels for s in level.sems],
    )(*shards)


def _exchange(ride, name):
    n_in, n_out = len(ride.operands), len(ride.results)

    def body(*refs):
        ride.start(refs[:n_in], refs[n_in:n_in + n_out], refs[n_in + n_out:])
        ride.wait(refs[:n_in], refs[n_in:n_in + n_out], refs[n_in + n_out:])

    return _pallas_call(body, name=name, in_specs=[ANY] * n_in, out_specs=[ANY] * n_out, out_shape=ride.results,
                        scratch_shapes=ride.sems, input_output_aliases=ride.aliases)(*ride.operands)


HBM = pl.BlockSpec(memory_space=pltpu.HBM)
SEMAPHORES = pl.BlockSpec(memory_space=pltpu.SEMAPHORE)
IN_FLIGHT = pltpu.CompilerParams(has_side_effects=pltpu.SideEffectType.DATAFLOW_SIDE_EFFECTING)


def _chip_swap_copies(src_refs, land_refs, send_sems, recv_sems, landed):
    x, y, c = _place()
    return [pltpu.make_async_remote_copy(
        src_ref=src.at[2 * px + py], dst_ref=land.at[2 * px + py if landed else 2 * x + y], send_sem=send_sems.at[3 * a + j],
        recv_sem=recv_sems.at[3 * a + j], device_id=(px, py, c), device_id_type=MESH)
        for a, (src, land) in enumerate(zip(src_refs, land_refs)) for j, (px, py) in enumerate(_other_chips(x, y))]


def _chip_swap_start(sums, name):
    n = len(sums)

    def body(*refs):
        src_refs, land_refs, (send_sems, recv_sems), token = refs[:n], refs[n:2 * n], refs[2 * n:2 * n + 2], refs[-1]
        for cp in _chip_swap_copies(src_refs, land_refs, send_sems, recv_sems, False):
            cp.start()
        token[...] = jnp.zeros_like(token)

    kept = [pltpu.HBM(s.shape, s.dtype) for s in sums]
    out = _pallas_call(
        body, name=name,
        out_shape=[pltpu.SemaphoreType.DMA((3 * n,)), pltpu.SemaphoreType.DMA((3 * n,))] + kept + kept + [_sds((8, 128), F32)],
        in_specs=[HBM] * (2 * n), out_specs=[SEMAPHORES, SEMAPHORES] + [HBM] * (2 * n) + [pl.BlockSpec(memory_space=pltpu.VMEM)],
        input_output_aliases={i: 2 + i for i in range(2 * n)}, compiler_params=IN_FLIGHT,
    )(*[pltpu.with_memory_space_constraint(s, pltpu.HBM) for s in sums],
      *[pltpu.with_memory_space_constraint(lax.empty(s.shape, s.dtype), pltpu.HBM) for s in sums])
    return out[0], out[1], out[2:2 + n], out[2 + n:2 + 2 * n], out[-1]


def _chip_swap_wait(send_sems, recv_sems, sums, landings, after, name):
    n = len(sums)

    def body(*refs):
        src_refs, land_refs, (send_sems, recv_sems) = refs[:n], refs[n:2 * n], refs[2 * n:2 * n + 2]
        for cp in _chip_swap_copies(src_refs, land_refs, send_sems, recv_sems, False):
            cp.wait_send()
        for cp in _chip_swap_copies(src_refs, land_refs, send_sems, recv_sems, True):
            cp.wait_recv()

    out = _pallas_call(
        body, name=name, out_shape=[pltpu.HBM(s.shape, s.dtype) for s in list(sums) + list(landings)],
        in_specs=[HBM] * (2 * n) + [SEMAPHORES, SEMAPHORES] + [ANY] * len(after), out_specs=[HBM] * (2 * n),
        input_output_aliases={i: i for i in range(2 * n)}, compiler_params=IN_FLIGHT,
    )(*sums, *landings, send_sems, recv_sems, *after)
    return out[:n], out[n:]


def _pair_sums(gs, rs, core, name):
    n_arrays = len(gs)

    def body(core_ref, *refs):
        for g_ref, r_ref, o_ref in zip(refs[:n_arrays], refs[n_arrays:2 * n_arrays], refs[2 * n_arrays:]):
            o_ref[...] = (g_ref[...].astype(F32) + r_ref[...].astype(F32)).astype(o_ref.dtype)

    def own(g):
        return pl.BlockSpec((None, None) + g.shape[2:], lambda p, core_ref: (core_ref[0], p, 0, 0))

    def chip(g):
        return pl.BlockSpec((None,) + g.shape[2:], lambda p, core_ref: (p, 0, 0))

    return _pallas_call(
        body, name=name,
        grid_spec=pltpu.PrefetchScalarGridSpec(
            num_scalar_prefetch=1, grid=(4,), in_specs=[own(g) for g in gs] + [chip(g) for g in gs],
            out_specs=[chip(g) for g in gs]),
        out_shape=[_sds(g.shape[1:], g.dtype) for g in gs], compiler_params=_cparams(dimension_semantics=("arbitrary",)),
    )(core, *gs, *rs)


def _adamw_math(w, g, m, v):
    m = ADAM_B1 * m + (1.0 - ADAM_B1) * g
    v = ADAM_B2 * v + (1.0 - ADAM_B2) * (g * g)
    m_hat = m / (1.0 - ADAM_B1 ** ADAM_STEP)
    v_hat = v / (1.0 - ADAM_B2 ** ADAM_STEP)
    return -ADAM_LR * (m_hat / (jnp.sqrt(v_hat) + ADAM_EPS) + ADAM_WD * w), m, v


def _adamw_many(weights, name, ride=None):
    steps = 4
    in_specs, out_specs, out_shape, operands, tiles = [], [], [], [], []
    for w, m, v, parts, own, transposed in weights:
        _, pr, pc = parts.shape
        if transposed:
            c, r = w.shape
            tile = pl.BlockSpec((c, r // steps), lambda i: (0, i))
            part_tile = pl.BlockSpec((4, r // steps, pc), lambda i: (0, i, 0))
            tiles.append((c, r // steps))
        elif w.shape[0] % (8 * steps) == 0:
            r, c = w.shape
            tile = pl.BlockSpec((r // steps, c), lambda i: (i, 0))
            part_tile = pl.BlockSpec((4, r // steps, pc), lambda i: (0, i, 0))
            tiles.append((r // steps, c))
        else:
            tile = pl.BlockSpec(w.shape, lambda i: (0, 0))
            part_tile = pl.BlockSpec(parts.shape, lambda i: (0, 0, 0))
            tiles.append(w.shape)
        in_specs += [tile, tile, tile] + [part_tile] * (1 if own is None else 2)
        out_specs += [tile] * 4
        out_shape += [_sds(w.shape, F32)] * 4
        operands += [w, m, v, parts] + ([] if own is None else [own])
    n_in = len(operands)

    def body(*refs):
        ins, outs = list(refs[:n_in]), refs[n_in:]
        this_chip = 2 * lax.axis_index("x") + lax.axis_index("y")
        for k, (_, _, _, _, own, transposed) in enumerate(weights):
            w_ref, m_ref, v_ref, p_ref = ins[:4]
            own_ref = None if own is None else ins[4]
            del ins[:4 if own is None else 5]
            rows, cols = tiles[k]
            g = None
            for q in range(4):
                index = (q,) if transposed else (q, slice(0, rows), slice(0, cols))
                part = p_ref[index] if own is None else jnp.where(this_chip == q, own_ref[index], p_ref[index])
                g = part.astype(F32) if g is None else g + part.astype(F32)
            if transposed:
                g = g.T[:rows]
            g_out, d_out, m_out, v_out = outs[4 * k:4 * k + 4]
            g_out[...] = g
            d_out[...], m_out[...], v_out[...] = _adamw_math(w_ref[...], g, m_ref[...], v_ref[...])

    return _call(body, name, (steps,), in_specs, out_specs, out_shape, [], operands, ride)


SMALL = ("ssm_a_re", "ssm_a_im", "ssm_log_dt", "ssm_b_re", "ssm_b_im", "ssm_c_re", "ssm_c_im", "ssm_d",
         "ln1_g", "ln1_b", "ln2_g", "ln2_b")


def _pack_rows(arrays):
    rows = []
    for a in arrays:
        flat = a.reshape(-1)
        rows.append(jnp.pad(flat, (0, -flat.shape[0] % 128)).reshape(-1, 128))
    packed = jnp.concatenate(rows, axis=0)
    return jnp.pad(packed, ((0, -packed.shape[0] % 8), (0, 0)))


def _unpack_rows(packed, shapes):
    out, row = [], 0
    for shape in shapes:
        size = math.prod(shape)
        n_rows = -(-size // 128)
        out.append(packed[row:row + n_rows].reshape(-1)[:size].reshape(shape))
        row += n_rows
    return out


def _sum_devices(parts):
    def body(p_ref, o_ref):
        total = p_ref[0]
        for dev in range(1, N_DEV):
            total = total + p_ref[dev]
        o_ref[...] = total

    return _pallas_call(body, name="sum_devices", out_shape=_sds(parts.shape[1:], F32))(parts)


def _adamw_replicated(ws, ms, vs, gs):
    n = len(ws)

    def body(*refs):
        w_refs, m_refs, v_refs, g_refs, d_out, m_out, v_out = (refs[i * n:(i + 1) * n] for i in range(7))
        for i in range(n):
            d_out[i][...], m_out[i][...], v_out[i][...] = _adamw_math(w_refs[i][...], g_refs[i][...], m_refs[i][...], v_refs[i][...])

    out = _pallas_call(body, name="adamw_replicated", out_shape=[_sds(w.shape, F32) for w in ws] * 3,
                       compiler_params=_cparams())(*ws, *ms, *vs, *gs)
    return out[:n], out[n:2 * n], out[2 * n:]


def kernel(x, w_in, b_gate, w_attn_br, w_ssm_br, w_out, ssm_a_re, ssm_a_im, ssm_log_dt, ssm_b_re, ssm_b_im, ssm_c_re, ssm_c_im, ssm_d, w_glu, ln1_g, ln1_b, w_ff_gate, w_ff_up, w_ff_down, ln2_g, ln2_b, loss_target, m_w_in, m_b_gate, m_w_attn_br, m_w_ssm_br, m_w_out, m_ssm_a_re, m_ssm_a_im, m_ssm_log_dt, m_ssm_b_re, m_ssm_b_im, m_ssm_c_re, m_ssm_c_im, m_ssm_d, m_w_glu, m_ln1_g, m_ln1_b, m_w_ff_gate, m_w_ff_up, m_w_ff_down, m_ln2_g, m_ln2_b, v_w_in, v_b_gate, v_w_attn_br, v_w_ssm_br, v_w_out, v_ssm_a_re, v_ssm_a_im, v_ssm_log_dt, v_ssm_b_re, v_ssm_b_im, v_ssm_c_re, v_ssm_c_im, v_ssm_d, v_w_glu, v_ln1_g, v_ln1_b, v_w_ff_gate, v_w_ff_up, v_w_ff_down, v_ln2_g, v_ln2_b):
    given = dict(locals())
    x2, target = x[0], loss_target[0]
    core = lax.axis_index("c").astype(jnp.int32).reshape(1)

    sharded = ("w_in", "w_attn_br", "w_ssm_br", "w_glu", "w_ff_gate", "w_ff_up", "b_gate", "w_out", "w_ff_down")
    send_shape = dict(w_in=(D_MODEL, 896), w_attn_br=(ATTN_WIDTH, 128), w_ssm_br=(SSM_WIDTH, 128), w_glu=(SSM_WIDTH, 128),
                      w_out=(128, D_MODEL), w_ff_gate=(D_MODEL, FF_PAD), w_ff_up=(D_MODEL, FF_PAD), w_ff_down=(FF_PAD, D_MODEL))
    local = {k: given[k][0] for k in sharded}
    narrow = ("w_ff_gate", "w_ff_up")
    def to_send(k):
        return (local[k].T, True, *send_shape[k]) if k in narrow else (local[k], False, *send_shape[k])

    later = [k for k in sharded if k not in ("w_in", "b_gate")]
    sends = dict(zip(["w_in"] + later, _send_buffers([to_send("w_in")], "send_w_in")
                     + _send_buffers([to_send(k) for k in later], "send_weights")))
    sends["b_gate"] = local["b_gate"]
    mixer_weights = ("w_attn_br", "w_ssm_br", "w_glu", "b_gate", "w_out")
    ff_weights = ("w_ff_gate", "w_ff_up", "w_ff_down")
    wt = {}
    wt["w_in"], = _all_gather([sends["w_in"]], "gather_w_in")

    a_re, a_im, log_dt = ssm_a_re[0], ssm_a_im[0], ssm_log_dt[0].reshape(SSM_GROUPS, 1)
    b_re_t, b_im_t = ssm_b_re[0].transpose(0, 2, 1), ssm_b_im[0].transpose(0, 2, 1)
    abar_re, abar_im, e_re, e_im, bbar_re_t, bbar_im_t = _ssm_prep(a_re, a_im, log_dt, b_re_t, b_im_t)
    bmat, cmat, a_chunks = _ssm_tables(abar_re, abar_im, bbar_re_t, bbar_im_t, ssm_c_re[0], ssm_c_im[0])
    cos_t, sin_t = _rope_tables()

    big_mixer, ff_in = [k for k in mixer_weights if k != "b_gate"], ("w_ff_gate", "w_ff_up")
    n_mixer = len(big_mixer)
    mixer_1, mixer_2, mixer_3 = _relayed_gather([sends[k] for k in big_mixer])
    ff_in_1, ff_in_2, ff_in_3 = _relayed_gather([sends[k] for k in ff_in])
    ff_down_1, ff_down_2, ff_down_3 = _relayed_gather([sends["w_ff_down"]])
    proj, *landed = _proj(x2, wt["w_in"], mixer_1 + _gather_first_level([sends["b_gate"]]))
    mixer, bias = landed[:n_mixer], landed[n_mixer:]
    attn, lse, *landed = _attn_fwd(proj, cos_t, sin_t, mixer_2(mixer) + _gather_second_level(bias) + ff_in_1)
    mixer, b_gate_full, ff = landed[:n_mixer], landed[n_mixer], landed[n_mixer + 1:]
    ys, states, *landed = _ssm_fwd(proj, bmat, cmat, a_chunks, ssm_d, mixer_3(mixer) + ff_in_2(ff) + ff_down_1)
    wt.update(zip(big_mixer, landed[:n_mixer]))
    ff, ff_down = landed[n_mixer:n_mixer + 2], landed[n_mixer + 2:]
    wt["w_out"] = wt["w_out"].reshape(D_MODEL, D_MODEL)
    h, xhat1, rstd1, glu, y_attn, y_ssm, *landed = _mixer_out(
        attn, ys, proj, x2, wt["w_attn_br"], wt["w_ssm_br"], wt["w_glu"], wt["w_out"], b_gate_full, ln1_g, ln1_b,
        ff_in_3(ff) + ff_down_2(ff_down))
    wt.update(zip(ff_in, landed[:2]))
    ff_a, ff_b, ff_f, w_ff_down = _ff_up(h, wt["w_ff_gate"], wt["w_ff_up"], ff_down_3(landed[2:]))
    wt["w_ff_down"] = w_ff_down.reshape(D_FF_PAD, D_MODEL)
    dr2, d_ln2_g, d_ln2_b, loss_lanes = _ff_down_loss(ff_f, wt["w_ff_down"], h, target, ln2_g, ln2_b)

    def pair_sums(names, contrib, from_sibling):
        return _pair_sums([contrib[k] for k in names], from_sibling, core, "pair_sums_" + names[0])

    d_a, d_b = _ff_down_bwd(dr2, wt["w_ff_down"], ff_a, ff_b)
    contrib = dict(w_ff_gate=_weight_grad(h, d_a, "wgrad_w_ff_gate", FF_PAD),
                   w_ff_up=_weight_grad(h, d_b, "wgrad_w_ff_up", FF_PAD),
                   w_ff_down=_weight_grad(ff_f, dr2, "wgrad_w_ff_down"))
    dr1, d_ln1_g, d_ln1_b, *from_sibling = _ff_up_bwd(
        d_a, d_b, wt["w_ff_gate"], wt["w_ff_up"], dr2, xhat1, rstd1, ln1_g, _sibling_swap_ride([contrib[k] for k in ff_weights]))
    ff_sums = pair_sums(ff_weights, contrib, from_sibling)
    ff_send_sems, ff_recv_sems, ff_sums, ff_landing, ff_token = _chip_swap_start(ff_sums, "ff_chip_swap_start")

    d_ya, d_yssm, d_proj, d_attn, d_glu, d_ys, mixed, y_s, gy, d_bg = _mixer_bwd(
        dr1, proj, y_attn, y_ssm, glu, ys, wt["w_attn_br"], wt["w_ssm_br"], wt["w_glu"], wt["w_out"], b_gate_full)
    contrib.update(w_attn_br=_weight_grad(attn, d_ya, "wgrad_w_attn_br", 128),
                   w_ssm_br=_weight_grad(y_s, d_yssm, "wgrad_w_ssm_br", 128),
                   w_glu=_weight_grad(gy, d_glu, "wgrad_w_glu", 128),
                   w_out=_weight_grad(mixed, dr1, "wgrad_w_out"),
                   b_gate=d_bg.reshape(2, 4, 2, 128).transpose(2, 1, 0, 3))
    from_sibling = _exchange(_sibling_swap_ride([contrib[k] for k in mixer_weights]), "swap_mixer_with_sibling")
    mixer_sums = pair_sums(mixer_weights, contrib, from_sibling)
    d_proj, = _attn_bwd(proj, cos_t, sin_t, attn, lse, d_attn, d_proj, _after(ff_token))
    ff_sums, ff_landing = _chip_swap_wait(ff_send_sems, ff_recv_sems, ff_sums, ff_landing, [d_proj], "ff_chip_swap_wait")
    parts, own_sums = dict(zip(ff_weights, ff_landing)), dict(zip(ff_weights, ff_sums))
    d_proj, d_bmat, d_cmat, d_abar, d_skip, *landed = _ssm_bwd(d_ys, proj, states, bmat, cmat, a_chunks, ssm_d, d_proj,
                                                               _chip_swap_ride(mixer_sums))
    parts.update(zip(mixer_weights, landed))

    gbb_re_t, gbb_im_t = _block_diag_parts(d_bmat, True)
    gc_re, gc_im = _block_diag_parts(d_cmat, False)
    ga_re = d_abar[:, 0, :CHUNK_STATES].reshape(SSM_GROUPS, SSM_STATE)
    ga_im = d_abar[:, 0, CHUNK_STATES:].reshape(SSM_GROUPS, SSM_STATE)
    g_a_re, g_a_im, g_log_dt, g_b_re_t, g_b_im_t = _ssm_param_bwd(
        a_re, a_im, log_dt, b_re_t, b_im_t, abar_re, abar_im, e_re, e_im, ga_re, ga_im, gbb_re_t, gbb_im_t)
    mine = [g_a_re, g_a_im, g_log_dt, g_b_re_t, g_b_im_t, gc_re, -gc_im,
            d_skip, d_ln1_g, d_ln1_b, d_ln2_g, d_ln2_b]
    small_packed = _pack_rows(mine + [loss_lanes])

    contrib["w_in"], small_partly = _weight_grad(x2, d_proj, "wgrad_w_in", 896, _gather_first_level([small_packed]))
    from_sibling, every = _exchange(_sibling_swap_ride([contrib["w_in"]]) + _gather_second_level([small_partly]),
                                    "swap_w_in_with_sibling")
    w_in_sum, = pair_sums(["w_in"], contrib, [from_sibling])
    send_sems, recv_sems, w_in_sum, landing, token = _chip_swap_start([w_in_sum], "w_in_chip_swap_start")

    def adamw_of(k):
        taken = (lambda a: a.T) if k in narrow else (lambda a: a)
        return taken(local[k]), taken(given["m_" + k][0]), taken(given["v_" + k][0]), parts[k], own_sums.get(k), k in narrow

    others = [k for k in sharded if k != "w_in"]
    updated = _adamw_many([adamw_of(k) for k in others], "adamw_others", _after(token))
    grad_x, = _grad_x(d_proj, wt["w_in"], dr1, _after(token))
    (own_sums["w_in"],), (parts["w_in"],) = _chip_swap_wait(send_sems, recv_sems, w_in_sum, landing, [grad_x, updated[0]],
                                                            "w_in_chip_swap_wait")
    updated += _adamw_many([adamw_of("w_in")], "adamw_w_in")

    grads, deltas, new_m, new_v = {}, {}, {}, {}
    for i, k in enumerate(others + ["w_in"]):
        out = [o.T if k in narrow else o for o in updated[4 * i:4 * i + 4]]
        grads[k], deltas[k], new_m[k], new_v[k] = (o.reshape((1,) + local[k].shape) for o in out)

    def held(k, a):
        return a.transpose(0, 1, 3, 2) if k in ("ssm_b_re", "ssm_b_im") else a

    *small_grads, loss_sum = _unpack_rows(_sum_devices(every), [held(k, given[k]).shape for k in SMALL] + [(1, 128)])
    small = _adamw_replicated([held(k, given[k]) for k in SMALL], [held(k, given["m_" + k]) for k in SMALL],
                              [held(k, given["v_" + k]) for k in SMALL], small_grads)
    for res, values in zip((grads, deltas, new_m, new_v), (small_grads,) + small):
        res.update((k, held(k, a)) for k, a in zip(SMALL, values))
    loss = loss_sum[0, 0]

    order = ("w_in", "b_gate", "w_attn_br", "w_ssm_br", "w_out", "ssm_a_re", "ssm_a_im", "ssm_log_dt", "ssm_b_re", "ssm_b_im",
             "ssm_c_re", "ssm_c_im", "ssm_d", "w_glu", "ln1_g", "ln1_b", "w_ff_gate", "w_ff_up", "w_ff_down", "ln2_g", "ln2_b")
    return (loss, grad_x[None], *[grads[k] for k in order], *[deltas[k] for k in order], *[new_m[k] for k in order],
            *[new_v[k] for k in order])
```

```python
import functools
import math

import jax
import jax.numpy as jnp
import numpy as np
from jax import lax
from jax.experimental import pallas as pl
from jax.experimental.pallas import tpu as pltpu

F32 = jnp.float32
BF16 = jnp.bfloat16

N_DEV = 8
SEQ = 2048
D_MODEL = 1024
HEAD_DIM = 64
ATTN_WIDTH = 512
QKV_WIDTH = 1536
SSM_WIDTH = 512
SSM_GROUPS = 32
SSM_GROUP = 16
SSM_STATE = 64
IN_WIDTH = 7168
D_FF = 2816
FF_SHARD = D_FF // N_DEV
FF_PAD = 384
D_FF_PAD = FF_PAD * N_DEV
DN_ALPHA = 2.0 ** 0.25
LN_EPS = 1e-5
NEG_INF = -1e30
ROPE_THETA = 10000.0
BLOCK = 128
GROUPS = ((1, 16), (4, 4), (16, 1))

ADAM_LR = 0.001
ADAM_B1 = 0.9
ADAM_B2 = 0.999
ADAM_EPS = 1e-08
ADAM_WD = 0.01
ADAM_STEP = 10

VMEM_LIMIT = 56 * 1024 * 1024


_pallas_call = pl.pallas_call


def _cparams(**kw):
    return pltpu.CompilerParams(vmem_limit_bytes=VMEM_LIMIT, **kw)


def _dot(a, b):
    return jnp.dot(a, b, preferred_element_type=F32)


def _dot_nt(a, b):
    return lax.dot_general(a, b, (((1,), (1,)), ((), ())), preferred_element_type=F32)


def _side_by_side(w_ref, row=None):
    rows = slice(None) if row is None else pl.ds(row, 1)
    return jnp.concatenate([w_ref[i, rows, :] for i in range(w_ref.shape[0])], axis=1)


def _dot_tn(a, b):
    return lax.dot_general(a, b, (((0,), (0,)), ((), ())), preferred_element_type=F32)


def _rope_tables():
    half = HEAD_DIM // 2
    inv_freq = np.float32(ROPE_THETA) ** (-np.arange(half, dtype=np.float32) / np.float32(half))
    ang = np.arange(SEQ, dtype=np.float32)[:, None] * inv_freq[None, :]
    cos, sin = np.cos(ang).astype(np.float32), np.sin(ang).astype(np.float32)
    tables = np.tile(cos, (1, 4)), np.tile(np.concatenate([-sin, sin], axis=1), (1, 2))

    def by_phase(t):
        return np.stack([t.reshape(SEQ // d, d, 128).transpose(1, 0, 2).reshape(SEQ, 128) for d, _ in GROUPS])

    return jnp.asarray(by_phase(tables[0])), jnp.asarray(by_phase(tables[1]))


def _swap_halves(x):
    lane = lax.broadcasted_iota(jnp.int32, x.shape, 1)
    return jnp.where((lane & 63) < 32, pltpu.roll(x, 96, axis=1), pltpu.roll(x, 32, axis=1))


def _group_rows(d, nb, r, i):
    src = pl.ds(i * BLOCK, BLOCK) if d == 1 else pl.ds(r + i * BLOCK * d, BLOCK, stride=d)
    return src, pl.ds((r * nb + i) * BLOCK, BLOCK)


def _attn_masks():
    a_idx = lax.broadcasted_iota(jnp.int32, (2 * BLOCK, 2 * BLOCK), 0) & (BLOCK - 1)
    c_idx = lax.broadcasted_iota(jnp.int32, (2 * BLOCK, 2 * BLOCK), 1)
    cur_ok = jnp.logical_and(c_idx >= BLOCK, c_idx - BLOCK <= a_idx)
    prev_ok = jnp.logical_and(c_idx < BLOCK, c_idx >= a_idx)
    lane = lax.broadcasted_iota(jnp.int32, (BLOCK, 128), 1)
    return cur_ok, prev_ok, lane < HEAD_DIM


def _stack_heads(t, head0):
    zero = jnp.zeros_like(t)
    return jnp.concatenate([jnp.where(head0, t, zero), jnp.where(head0, zero, t)], axis=0)


def _unstack_heads(t2, head0):
    return jnp.where(head0, t2[:BLOCK], t2[BLOCK:])


def _attn_fwd(proj, cos_t, sin_t, ride=None):
    def body(q0, q1, q2, k0, k1, k2, v0, v1, v2, cos_ref, sin_ref, attn_ref, lse_ref,
             qs, ks, vs, os_, ms, ls, acc, mnat, lnat):
        cur_ok, prev_ok, head0 = _attn_masks()
        ks[:BLOCK, :] = jnp.zeros((BLOCK, 128), BF16)
        vs[:BLOCK, :] = jnp.zeros((BLOCK, 128), BF16)
        for g, (d, nb) in enumerate(GROUPS):
            q_ref, k_ref, v_ref = (q0, q1, q2)[g], (k0, k1, k2)[g], (v0, v1, v2)[g]
            for r in range(d):
                for i in range(nb):
                    src, dst = _group_rows(d, nb, r, i)
                    below = pl.ds(dst.start + BLOCK, BLOCK)
                    c, s = cos_ref[g, dst, :], sin_ref[g, dst, :]
                    q = q_ref[src, :]
                    k = k_ref[src, :]
                    qs[dst, :] = ((q * c + _swap_halves(q) * s) * 0.125).astype(BF16)
                    ks[below, :] = (k * c + _swap_halves(k) * s).astype(BF16)
                    vs[below, :] = v_ref[src, :].astype(BF16)

            def block(b, carry, nb=nb):
                has_prev = (b & (nb - 1)) > 0
                cur = pl.ds(pl.multiple_of(b * BLOCK, BLOCK), BLOCK)
                window = pl.ds(pl.multiple_of(b * BLOCK, BLOCK), 2 * BLOCK)
                valid = jnp.logical_or(cur_ok, jnp.logical_and(prev_ok, has_prev))
                s = jnp.where(valid, _dot_nt(_stack_heads(qs[cur, :], head0), ks[window, :]), NEG_INF)
                m = jnp.max(s, axis=1, keepdims=True)
                p = jnp.exp(s - m)
                os_[cur, :] = _unstack_heads(_dot(p.astype(BF16), vs[window, :]), head0)
                ms[cur, :] = _unstack_heads(m, head0)
                ls[cur, :] = _unstack_heads(jnp.sum(p, axis=1, keepdims=True), head0)
                return carry

            lax.fori_loop(0, SEQ // BLOCK, block, 0, unroll=16)

            for r in range(d):
                for i in range(nb):
                    src, dst = _group_rows(d, nb, r, i)
                    if g == 0:
                        acc[src, :], mnat[src, :], lnat[src, :] = os_[dst, :], ms[dst, :], ls[dst, :]
                    else:
                        m_old, m_g = mnat[src, :], ms[dst, :]
                        m_new = jnp.maximum(m_old, m_g)
                        a_old, a_g = jnp.exp(m_old - m_new), jnp.exp(m_g - m_new)
                        acc[src, :] = a_old * acc[src, :] + a_g * os_[dst, :]
                        lnat[src, :] = a_old * lnat[src, :] + a_g * ls[dst, :]
                        mnat[src, :] = m_new
        for i in range(SEQ // BLOCK):
            rows = pl.ds(i * BLOCK, BLOCK)
            l = lnat[rows, :]
            attn_ref[rows, :] = acc[rows, :] / l
            lse_ref[rows, :] = mnat[rows, :] + jnp.log(l)

    def col(base):
        return pl.BlockSpec((SEQ, 128), lambda hp, base=base: (0, base + hp))

    in_specs = [col(g * 4) for g in range(3)] + [col(12 + g * 4) for g in range(3)] + [col(24 + g * 4) for g in range(3)]
    table = pl.BlockSpec((3, SEQ, 128), lambda hp: (0, 0, 0), pipeline_mode=pl.Buffered(1))
    out = pl.BlockSpec((SEQ, 128), lambda hp: (0, hp))
    return _call(
        body, "attn_fwd", (4,), in_specs + [table, table], [out, out],
        [_sds((SEQ, ATTN_WIDTH), F32), _sds((SEQ, ATTN_WIDTH), F32)],
        [pltpu.VMEM((SEQ, 128), BF16)] + [pltpu.VMEM((SEQ + BLOCK, 128), BF16)] * 2 + [pltpu.VMEM((SEQ, 128), F32)] * 6,
        [proj] * 9 + [cos_t, sin_t], ride)


def _attn_bwd_group_body(g):
    d, nb = GROUPS[g]

    def body(q_ref, k_ref, v_ref, cos_ref, sin_ref, lse_ref, dattn_ref, dsum_ref, dproj_ref,
             qs, ks, vs, dos, lss, dss, dqs, dks, dvs, stage, outs, sems):
        cur_ok, prev_ok, head0 = _attn_masks()
        ks[:BLOCK, :] = jnp.zeros((BLOCK, 128), BF16)
        vs[:BLOCK, :] = jnp.zeros((BLOCK, 128), BF16)
        dks[:BLOCK, :] = jnp.zeros((BLOCK, 128), F32)
        dvs[:BLOCK, :] = jnp.zeros((BLOCK, 128), F32)
        for r in range(d):
            for i in range(nb):
                src, dst = _group_rows(d, nb, r, i)
                below = pl.ds(dst.start + BLOCK, BLOCK)
                c, s = cos_ref[g, dst, :], sin_ref[g, dst, :]
                q = q_ref[src, :]
                k = k_ref[src, :]
                qs[dst, :] = ((q * c + _swap_halves(q) * s) * 0.125).astype(BF16)
                ks[below, :] = (k * c + _swap_halves(k) * s).astype(BF16)
                vs[below, :] = v_ref[src, :].astype(BF16)
                dos[dst, :] = dattn_ref[src, :].astype(BF16)
                dss[dst, :] = dsum_ref[src, :]
                lss[dst, :] = lse_ref[src, :]
                dks[below, :] = jnp.zeros((BLOCK, 128), F32)
                dvs[below, :] = jnp.zeros((BLOCK, 128), F32)

        def per_head_column(t):
            return jnp.concatenate([jnp.max(jnp.where(head0, t, NEG_INF), axis=1, keepdims=True),
                                    jnp.max(jnp.where(head0, NEG_INF, t), axis=1, keepdims=True)], axis=0)

        def block(b, carry):
            has_prev = (b & (nb - 1)) > 0
            cur = pl.ds(pl.multiple_of(b * BLOCK, BLOCK), BLOCK)
            window = pl.ds(pl.multiple_of(b * BLOCK, BLOCK), 2 * BLOCK)
            valid = jnp.logical_or(cur_ok, jnp.logical_and(prev_ok, has_prev))
            q2, do2 = _stack_heads(qs[cur, :], head0), _stack_heads(dos[cur, :], head0)
            kw, vw = ks[window, :], vs[window, :]
            s = jnp.where(valid, _dot_nt(q2, kw), NEG_INF)
            p = jnp.exp(s - per_head_column(lss[cur, :]))
            ds = (p * (_dot_nt(do2, vw) - per_head_column(dss[cur, :]))).astype(BF16)
            dvs[window, :] += _dot_tn(p.astype(BF16), do2)
            dks[window, :] += _dot_tn(ds, q2)
            dqs[cur, :] = _unstack_heads(_dot(ds, kw), head0)
            return carry

        lax.fori_loop(0, SEQ // BLOCK, block, 0, unroll=8)

        hp = pl.program_id(0)
        copies = []
        for kind in range(3):
            for r in range(d):
                for i in range(nb):
                    src, dst = _group_rows(d, nb, r, i)
                    below = pl.ds(dst.start + BLOCK, BLOCK)
                    if kind == 2:
                        stage[src, :] = dvs[below, :]
                    else:
                        c, s = cos_ref[g, dst, :], sin_ref[g, dst, :]
                        t = dqs[dst, :] * 0.125 if kind == 0 else dks[below, :]
                        stage[src, :] = t * c - _swap_halves(t) * s
            for i in range(SEQ // MM_ROWS):
                rows = pl.ds(i * MM_ROWS, MM_ROWS)
                outs[kind, rows, :] = stage[rows, :].astype(BF16)
            column = pl.multiple_of((kind * 12 + g * 4 + hp) * 128, 128)
            copies.append(pltpu.make_async_copy(outs.at[kind], dproj_ref.at[:, pl.ds(column, 128)], sems.at[kind]))
            copies[-1].start()
        for cp in copies:
            cp.wait()

    return body


def _attn_bwd(proj, cos_t, sin_t, attn, lse, dattn, dproj, ride=None):
    groups = [_attn_bwd_group_body(g) for g in range(3)]

    def body(q0, q1, q2, k0, k1, k2, v0, v1, v2, cos_ref, sin_ref, attn_ref, lse_ref, dattn_ref, dproj_in, dproj_ref,
             dsum, *scratch):
        del dproj_in
        head0 = _attn_masks()[2]
        for i in range(SEQ // BLOCK):
            rows = pl.ds(i * BLOCK, BLOCK)
            prod = dattn_ref[rows, :] * attn_ref[rows, :]
            d0 = jnp.sum(jnp.where(head0, prod, 0.0), axis=1, keepdims=True)
            d1 = jnp.sum(jnp.where(head0, 0.0, prod), axis=1, keepdims=True)
            dsum[rows, :] = jnp.where(head0, d0, d1)
        for g in range(3):
            groups[g]((q0, q1, q2)[g], (k0, k1, k2)[g], (v0, v1, v2)[g], cos_ref, sin_ref, lse_ref, dattn_ref, dsum,
                      dproj_ref, *scratch)

    def col(base):
        return pl.BlockSpec((SEQ, 128), lambda hp, base=base: (0, base + hp))

    table = pl.BlockSpec((3, SEQ, 128), lambda hp: (0, 0, 0), pipeline_mode=pl.Buffered(1))
    return _call(
        body, "attn_bwd", (4,),
        [col(g * 4) for g in range(3)] + [col(12 + g * 4) for g in range(3)] + [col(24 + g * 4) for g in range(3)]
        + [table, table, col(0), col(0), col(0), ANY],
        [ANY], [_sds((SEQ, IN_WIDTH), BF16)],
        [pltpu.VMEM((SEQ, 128), F32)]
        + [pltpu.VMEM((SEQ, 128), BF16)] + [pltpu.VMEM((SEQ + BLOCK, 128), BF16)] * 2 + [pltpu.VMEM((SEQ, 128), BF16)]
        + [pltpu.VMEM((SEQ, 128), F32)] * 3 + [pltpu.VMEM((SEQ + BLOCK, 128), F32)] * 2 + [pltpu.VMEM((SEQ, 128), F32)]
        + [pltpu.VMEM((3, SEQ, 128), BF16), pltpu.SemaphoreType.DMA((3,))],
        [proj] * 9 + [cos_t, sin_t, attn, lse, dattn, dproj], ride, aliases={14: 0})


SSM_CHUNKS = 4
CHUNK_STATES = 512
SCAN_ROWS = 8
U_COL = (3 * QKV_WIDTH) // 128


def _cmul(xr, xi, yr, yi):
    return xr * yr - xi * yi, xr * yi + xi * yr


def _ssm_prep(a_re, a_im, log_dt, b_re_t, b_im_t):
    def body(ar_ref, ai_ref, ldt_ref, br_ref, bi_ref, abr_ref, abi_ref, er_ref, ei_ref, bbr_ref, bbi_ref):
        ar, ai = ar_ref[...], ai_ref[...]
        dt = jnp.exp(ldt_ref[...])
        mag = jnp.exp(ar * dt)
        abr, abi = mag * jnp.cos(ai * dt), mag * jnp.sin(ai * dt)
        den = ar * ar + ai * ai
        nr, ni = abr - 1.0, abi
        er, ei = (nr * ar + ni * ai) / den, (ni * ar - nr * ai) / den
        abr_ref[...], abi_ref[...], er_ref[...], ei_ref[...] = abr, abi, er, ei
        er3, ei3 = er[:, None, :], ei[:, None, :]
        br, bi = br_ref[...], bi_ref[...]
        bbr_ref[...] = er3 * br - ei3 * bi
        bbi_ref[...] = er3 * bi + ei3 * br

    gp = jax.ShapeDtypeStruct(a_re.shape, F32)
    gb = jax.ShapeDtypeStruct(b_re_t.shape, F32)
    return _pallas_call(body, name="ssm_prep", out_shape=(gp, gp, gp, gp, gb, gb))(a_re, a_im, log_dt, b_re_t, b_im_t)


def _ssm_param_bwd(a_re, a_im, log_dt, b_re_t, b_im_t, abar_re, abar_im, e_re, e_im, ga_re, ga_im, gbb_re_t, gbb_im_t):
    def body(ar_ref, ai_ref, ldt_ref, br_ref, bi_ref, abr_ref, abi_ref, er_ref, ei_ref, gar_ref, gai_ref, gbr_ref, gbi_ref,
             o_ar, o_ai, o_ldt, o_br, o_bi):
        ar, ai = ar_ref[...], ai_ref[...]
        dt = jnp.exp(ldt_ref[...])
        er, ei = er_ref[...], ei_ref[...]
        br, bi, gbr, gbi = br_ref[...], bi_ref[...], gbr_ref[...], gbi_ref[...]
        er3, ei3 = er[:, None, :], ei[:, None, :]
        o_br[...] = er3 * gbr + ei3 * gbi
        o_bi[...] = er3 * gbi - ei3 * gbr
        ge_r = jnp.sum(br * gbr + bi * gbi, axis=1)
        ge_i = jnp.sum(br * gbi - bi * gbr, axis=1)
        den = ar * ar + ai * ai
        ilr, ili = ar / den, -ai / den
        t_r, t_i = _cmul(ilr, -ili, ge_r, ge_i)
        gab_r, gab_i = gar_ref[...] + t_r, gai_ref[...] + t_i
        gz_r, gz_i = _cmul(abr_ref[...], -abi_ref[...], gab_r, gab_i)
        el_r, el_i = _cmul(er, ei, ilr, ili)
        u_r, u_i = _cmul(el_r, -el_i, ge_r, ge_i)
        o_ar[...] = dt * gz_r - u_r
        o_ai[...] = dt * gz_i - u_i
        o_ldt[...] = jnp.sum(gz_r * ar + gz_i * ai, axis=1, keepdims=True) * dt

    gp = jax.ShapeDtypeStruct(a_re.shape, F32)
    gb = jax.ShapeDtypeStruct(b_re_t.shape, F32)
    return _pallas_call(body, name="ssm_param_bwd", out_shape=(gp, gp, jax.ShapeDtypeStruct(log_dt.shape, F32), gb, gb))(
        a_re, a_im, log_dt, b_re_t, b_im_t, abar_re, abar_im, e_re, e_im, ga_re, ga_im, gbb_re_t, gbb_im_t)


def _block_diag(blocks_re, blocks_im, sign_im, rows_are_channels):
    both = jnp.stack([blocks_re, sign_im * blocks_im]).reshape(2, SSM_CHUNKS, 8, SSM_GROUP, SSM_STATE)
    eye = jnp.eye(8, dtype=F32)
    if rows_are_channels:
        return jnp.einsum("rcghp,gk->cghrkp", both, eye).reshape(SSM_CHUNKS, 128, 2 * CHUNK_STATES)
    return jnp.einsum("rcghp,gk->crkpgh", both, eye).reshape(SSM_CHUNKS, 2 * CHUNK_STATES, 128)


def _block_diag_parts(mat, rows_are_channels):
    if rows_are_channels:
        six = mat.reshape(SSM_CHUNKS, 8, SSM_GROUP, 2, 8, SSM_STATE)
        parts = jnp.einsum("cghrgp->rcghp", six)
    else:
        six = mat.reshape(SSM_CHUNKS, 2, 8, SSM_STATE, 8, SSM_GROUP)
        parts = jnp.einsum("crgpgh->rcghp", six)
    parts = parts.reshape(2, SSM_GROUPS, SSM_GROUP, SSM_STATE)
    return parts[0], parts[1]


def _scan_consts(a_ref, conj, reverse):
    ar = jnp.broadcast_to(a_ref[:, :CHUNK_STATES], (SCAN_ROWS, CHUNK_STATES))
    ai = jnp.broadcast_to(a_ref[:, CHUNK_STATES:], (SCAN_ROWS, CHUNK_STATES))
    if conj:
        ai = -ai
    row = lax.broadcasted_iota(jnp.int32, (SCAN_ROWS, CHUNK_STATES), 0)
    if reverse:
        row = SCAN_ROWS - 1 - row
    zero = jnp.zeros_like(ar)
    steps = []
    pr, pi = ar, ai
    for shift in (1, 2, 4):
        keep = row >= shift
        steps.append((SCAN_ROWS - shift if reverse else shift, jnp.where(keep, pr, zero), jnp.where(keep, pi, zero)))
        pr, pi = _cmul(pr, pi, pr, pi)
    first = row == 0
    return steps, (jnp.where(first, ar, zero), jnp.where(first, ai, zero)), first


def _scan_tile(xr, xi, prev_r, prev_i, steps, carry_in, reverse):
    edge = SCAN_ROWS - 1 if reverse else 1
    cr, ci = pltpu.roll(prev_r, edge, axis=0), pltpu.roll(prev_i, edge, axis=0)
    xr, xi = xr + carry_in[0] * cr - carry_in[1] * ci, xi + carry_in[0] * ci + carry_in[1] * cr
    for shift, mr, mi in steps:
        sr, si = pltpu.roll(xr, shift, axis=0), pltpu.roll(xi, shift, axis=0)
        xr, xi = xr + mr * sr - mi * si, xi + mr * si + mi * sr
    return xr, xi


MM_ROWS = 256


def _ssm_fwd(proj, bmat, cmat, a_chunks, d_skip, ride=None):
    def body(u_ref, b_ref, c_ref, a_ref, d_ref, y_ref, h_ref):
        for i in range(SEQ // MM_ROWS):
            rows = pl.ds(i * MM_ROWS, MM_ROWS)
            h_ref[rows, :] = _dot(u_ref[rows, :].astype(BF16), b_ref[...])
        steps, carry_in, _ = _scan_consts(a_ref, conj=False, reverse=False)

        def tile(k, carry):
            rows = pl.ds(pl.multiple_of(k * SCAN_ROWS, SCAN_ROWS), SCAN_ROWS)
            xr, xi = _scan_tile(h_ref[rows, :CHUNK_STATES], h_ref[rows, CHUNK_STATES:], carry[0], carry[1], steps, carry_in, False)
            h_ref[rows, :CHUNK_STATES] = xr
            h_ref[rows, CHUNK_STATES:] = xi
            return xr, xi

        zero = jnp.zeros((SCAN_ROWS, CHUNK_STATES), F32)
        lax.fori_loop(0, SEQ // SCAN_ROWS, tile, (zero, zero), unroll=4)
        for i in range(SEQ // MM_ROWS):
            rows = pl.ds(i * MM_ROWS, MM_ROWS)
            y_ref[rows, :] = _dot(h_ref[rows, :].astype(BF16), c_ref[...]) + d_ref[...] * u_ref[rows, :]

    return _call(
        body, "ssm_fwd", (SSM_CHUNKS,),
        [pl.BlockSpec((SEQ, 128), lambda c: (0, U_COL + c)),
         pl.BlockSpec((None, 128, 2 * CHUNK_STATES), lambda c: (c, 0, 0)),
         pl.BlockSpec((None, 2 * CHUNK_STATES, 128), lambda c: (c, 0, 0)),
         pl.BlockSpec((None, 1, 2 * CHUNK_STATES), lambda c: (c, 0, 0)),
         pl.BlockSpec((1, 128), lambda c: (0, c))],
        [pl.BlockSpec((SEQ, 128), lambda c: (0, c)), pl.BlockSpec((SEQ, 2 * CHUNK_STATES), lambda c: (0, c))],
        [_sds((SEQ, SSM_WIDTH), F32), _sds((SEQ, SSM_CHUNKS * 2 * CHUNK_STATES), F32)], [],
        [proj, bmat, cmat, a_chunks, d_skip], ride)


def _ssm_bwd(dys, proj, h, bmat, cmat, a_chunks, d_skip, dproj, ride=None):
    def body(dy_ref, u_ref, h_ref, b_ref, c_ref, a_ref, d_ref, dproj_in, du_ref, db_ref, dc_ref, da_ref, dd_ref, g_ref):
        del dproj_in
        dsum = jnp.zeros((1, 128), F32)
        dcm = jnp.zeros((2 * CHUNK_STATES, 128), F32)
        for i in range(SEQ // MM_ROWS):
            rows = pl.ds(i * MM_ROWS, MM_ROWS)
            dy = dy_ref[rows, :]
            g_ref[rows, :] = _dot_nt(dy.astype(BF16), c_ref[...])
            dsum += jnp.sum(dy * u_ref[rows, :], axis=0, keepdims=True)
            dcm += _dot_tn(h_ref[rows, :].astype(BF16), dy.astype(BF16))
        dd_ref[...] = dsum
        dc_ref[...] = dcm
        steps, carry_in, _ = _scan_consts(a_ref, conj=True, reverse=True)
        first_row = lax.broadcasted_iota(jnp.int32, (SCAN_ROWS, CHUNK_STATES), 0) == 0
        n_tiles = SEQ // SCAN_ROWS

        def tile(j, carry):
            k = n_tiles - 1 - j
            rows = pl.ds(pl.multiple_of(k * SCAN_ROWS, SCAN_ROWS), SCAN_ROWS)
            before = pl.ds(pl.multiple_of(jnp.maximum(k - 1, 0) * SCAN_ROWS, SCAN_ROWS), SCAN_ROWS)
            gr, gi = _scan_tile(g_ref[rows, :CHUNK_STATES], g_ref[rows, CHUNK_STATES:], carry[0], carry[1], steps, carry_in, True)
            g_ref[rows, :CHUNK_STATES] = gr
            g_ref[rows, CHUNK_STATES:] = gi
            has_before = jnp.where(k > 0, 1.0, 0.0)
            hr = jnp.where(first_row, pltpu.roll(h_ref[before, :CHUNK_STATES], 1, axis=0) * has_before,
                           pltpu.roll(h_ref[rows, :CHUNK_STATES], 1, axis=0))
            hi = jnp.where(first_row, pltpu.roll(h_ref[before, CHUNK_STATES:], 1, axis=0) * has_before,
                           pltpu.roll(h_ref[rows, CHUNK_STATES:], 1, axis=0))
            return gr, gi, carry[2] + hr * gr + hi * gi, carry[3] + hr * gi - hi * gr

        zero = jnp.zeros((SCAN_ROWS, CHUNK_STATES), F32)
        _, _, sar, sai = lax.fori_loop(0, n_tiles, tile, (zero, zero, zero, zero), unroll=4)
        da_ref[:, :CHUNK_STATES] = jnp.sum(sar, axis=0, keepdims=True)
        da_ref[:, CHUNK_STATES:] = jnp.sum(sai, axis=0, keepdims=True)
        dbm = jnp.zeros((128, 2 * CHUNK_STATES), F32)
        for i in range(SEQ // MM_ROWS):
            rows = pl.ds(i * MM_ROWS, MM_ROWS)
            g = g_ref[rows, :].astype(BF16)
            du_ref[rows, :] = (_dot_nt(g, b_ref[...]) + d_ref[...] * dy_ref[rows, :]).astype(BF16)
            dbm += _dot_tn(u_ref[rows, :].astype(BF16), g)
        db_ref[...] = dbm

    chunk_col = pl.BlockSpec((SEQ, 128), lambda c: (0, c))
    return _call(
        body, "ssm_bwd", (SSM_CHUNKS,),
        [chunk_col,
         pl.BlockSpec((SEQ, 128), lambda c: (0, U_COL + c)),
         pl.BlockSpec((SEQ, 2 * CHUNK_STATES), lambda c: (0, c)),
         pl.BlockSpec((None, 128, 2 * CHUNK_STATES), lambda c: (c, 0, 0)),
         pl.BlockSpec((None, 2 * CHUNK_STATES, 128), lambda c: (c, 0, 0)),
         pl.BlockSpec((None, 1, 2 * CHUNK_STATES), lambda c: (c, 0, 0)),
         pl.BlockSpec((1, 128), lambda c: (0, c)), ANY],
        [pl.BlockSpec((SEQ, 128), lambda c: (0, U_COL + c)),
         pl.BlockSpec((None, 128, 2 * CHUNK_STATES), lambda c: (c, 0, 0)),
         pl.BlockSpec((None, 2 * CHUNK_STATES, 128), lambda c: (c, 0, 0)),
         pl.BlockSpec((None, 1, 2 * CHUNK_STATES), lambda c: (c, 0, 0)),
         pl.BlockSpec((1, 128), lambda c: (0, c))],
        [_sds((SEQ, IN_WIDTH), BF16), _sds((SSM_CHUNKS, 128, 2 * CHUNK_STATES), F32),
         _sds((SSM_CHUNKS, 2 * CHUNK_STATES, 128), F32), _sds((SSM_CHUNKS, 1, 2 * CHUNK_STATES), F32), _sds((1, SSM_WIDTH), F32)],
        [pltpu.VMEM((SEQ, 2 * CHUNK_STATES), F32)], [dys, proj, h, bmat, cmat, a_chunks, d_skip, dproj], ride, aliases={7: 0})


def _ssm_tables(abar_re, abar_im, bbar_re_t, bbar_im_t, c_re, c_im):
    bmat = _block_diag(bbar_re_t, bbar_im_t, 1.0, True).astype(BF16)
    cmat = _block_diag(c_re, c_im, -1.0, False).astype(BF16)
    a_chunks = jnp.concatenate([abar_re.reshape(SSM_CHUNKS, 1, CHUNK_STATES), abar_im.reshape(SSM_CHUNKS, 1, CHUNK_STATES)], axis=2)
    return bmat, cmat, a_chunks


GL_COL = (3 * QKV_WIDTH + SSM_WIDTH) // D_MODEL
GELU_C = math.sqrt(2.0 / math.pi)
GELU_A = 0.044715


def _sds(shape, dtype):
    return jax.ShapeDtypeStruct(shape, dtype)


def _gelu(x):
    t = jnp.tanh(GELU_C * (x + GELU_A * x * x * x))
    return 0.5 * x * (1.0 + t), t


def _gelu_grad(x, t):
    return 0.5 * (1.0 + t) + 0.5 * x * (1.0 - t * t) * GELU_C * (1.0 + 3.0 * GELU_A * x * x)


def _layer_norm(r, g, b):
    mu = jnp.mean(r, axis=-1, keepdims=True)
    xc = r - mu
    rstd = lax.rsqrt(jnp.mean(xc * xc, axis=-1, keepdims=True) + LN_EPS)
    xhat = xc * rstd
    return xhat * g + b, xhat, rstd


def _layer_norm_bwd(dy, xhat, rstd, g):
    dxhat = dy * g
    m1 = jnp.mean(dxhat, axis=-1, keepdims=True)
    m2 = jnp.mean(dxhat * xhat, axis=-1, keepdims=True)
    return rstd * (dxhat - m1 - xhat * m2)


def _proj(x, w_in, ride=None):
    tm, tn = 1024, 1792

    def body(x_ref, w_ref, o_ref):
        o_ref[...] = _dot(x_ref[...].astype(BF16), _side_by_side(w_ref))

    return _call(
        body, "proj", (SEQ // tm, IN_WIDTH // tn),
        [pl.BlockSpec((tm, D_MODEL), lambda i, j: (i, 0)), pl.BlockSpec((2, D_MODEL, tn // 2), lambda i, j: (j, 0, 0))],
        [pl.BlockSpec((tm, tn), lambda i, j: (i, j))], [_sds((SEQ, IN_WIDTH), F32)], [], [x, w_in], ride)


def _row_spec(tm, width, col=0):
    return pl.BlockSpec((tm, width), lambda i, col=col: (i, col))


def _full_spec(shape):
    return pl.BlockSpec(shape, lambda i: (0,) * len(shape))


def _weight_spec(shape):
    return pl.BlockSpec(shape, lambda i: (0,) * len(shape), pipeline_mode=pl.Buffered(1))


def _mixer_out(attn, ys, proj, x, w_ab, w_sb, w_glu, w_out, b_gate, ln_g, ln_b, ride=None):
    tm = 512

    def body(attn_ref, ys_ref, gl0_ref, gl1_ref, x_ref, wab_ref, wsb_ref, wglu_ref, wout_ref, bg_ref, g_ref, b_ref,
             h_ref, xhat_ref, rstd_ref, glu_ref, ya_ref, yssm_ref):
        gy, _ = _gelu(ys_ref[...])
        glu = _dot(gy.astype(BF16), _side_by_side(wglu_ref))
        glu_ref[...] = glu
        y_s = glu[:, :SSM_WIDTH] * jax.nn.sigmoid(glu[:, SSM_WIDTH:])
        y_ssm = _dot(y_s.astype(BF16), _side_by_side(wsb_ref))
        y_attn = _dot(attn_ref[...].astype(BF16), _side_by_side(wab_ref))
        ya_ref[...] = y_attn
        yssm_ref[...] = y_ssm
        g0 = jax.nn.sigmoid(gl0_ref[...] + _side_by_side(bg_ref, 0))
        g1 = jax.nn.sigmoid(gl1_ref[...] + _side_by_side(bg_ref, 1))
        mixed = g0 * y_attn + g1 * y_ssm
        r1 = DN_ALPHA * x_ref[...] + _dot(mixed.astype(BF16), wout_ref[...])
        h, xhat, rstd = _layer_norm(r1, g_ref[...], b_ref[...])
        h_ref[...] = h
        xhat_ref[...] = xhat
        rstd_ref[...] = jnp.broadcast_to(rstd, (tm, 128))

    wide = _sds((SEQ, D_MODEL), F32)
    return _call(
        body, "mixer_out", (SEQ // tm,),
        [_row_spec(tm, ATTN_WIDTH), _row_spec(tm, SSM_WIDTH), _row_spec(tm, D_MODEL, GL_COL), _row_spec(tm, D_MODEL, GL_COL + 1),
         _row_spec(tm, D_MODEL), _weight_spec((N_DEV, ATTN_WIDTH, 128)), _weight_spec((N_DEV, SSM_WIDTH, 128)),
         _weight_spec((N_DEV, SSM_WIDTH, 128)), _weight_spec((D_MODEL, D_MODEL)), _full_spec((N_DEV, 2, 128)),
         _full_spec((1, D_MODEL)), _full_spec((1, D_MODEL))],
        [_row_spec(tm, D_MODEL), _row_spec(tm, D_MODEL), _row_spec(tm, 128), _row_spec(tm, D_MODEL),
         _row_spec(tm, D_MODEL), _row_spec(tm, D_MODEL)],
        [wide, wide, _sds((SEQ, 128), F32), wide, wide, wide], [],
        [attn, ys, proj, proj, x, w_ab, w_sb, w_glu, w_out, b_gate, ln_g, ln_b], ride)


def _ff_up(h, w_gate, w_up, ride=None):
    tm, tn = 1024, 768

    def body(h_ref, wg_ref, wu_ref, a_ref, b_ref, f_ref):
        hb = h_ref[...].astype(BF16)
        a, b = _dot(hb, _side_by_side(wg_ref)), _dot(hb, _side_by_side(wu_ref))
        a_ref[...] = a.astype(BF16)
        b_ref[...] = b.astype(BF16)
        f_ref[...] = (a * jax.nn.sigmoid(a) * b).astype(BF16)

    tile = pl.BlockSpec((tm, tn), lambda i, j: (i, j))
    wtile = pl.BlockSpec((tn // FF_PAD, D_MODEL, FF_PAD), lambda i, j: (j, 0, 0))
    out = _sds((SEQ, D_FF_PAD), BF16)
    return _call(body, "ff_up", (SEQ // tm, D_FF_PAD // tn), [pl.BlockSpec((tm, D_MODEL), lambda i, j: (i, 0)), wtile, wtile],
                 [tile, tile, tile], [out, out, out], [], [h, w_gate, w_up], ride)


def _ff_down_loss(f, w_down, h, target, ln_g, ln_b):
    tm = 512

    def body(f_ref, w_ref, h_ref, t_ref, g_ref, b_ref, dr_ref, dg_ref, db_ref, loss_ref):
        @pl.when(pl.program_id(0) == 0)
        def _():
            dg_ref[...] = jnp.zeros_like(dg_ref)
            db_ref[...] = jnp.zeros_like(db_ref)
            loss_ref[...] = jnp.zeros_like(loss_ref)

        r2 = DN_ALPHA * h_ref[...] + _dot(f_ref[...], w_ref[...])
        g = g_ref[...]
        out, xhat, rstd = _layer_norm(r2, g, b_ref[...])
        err = out - t_ref[...]
        loss_ref[...] += 0.5 * jnp.sum(jnp.mean(err * err, axis=-1, keepdims=True), axis=0, keepdims=True)
        dout = err * (1.0 / D_MODEL)
        dg_ref[...] += jnp.sum(dout * xhat, axis=0, keepdims=True)
        db_ref[...] += jnp.sum(dout, axis=0, keepdims=True)
        dr_ref[...] = _layer_norm_bwd(dout, xhat, rstd, g)

    vec = _sds((1, D_MODEL), F32)
    return _pallas_call(
        body, name="ff_down_loss", grid=(SEQ // tm,),
        in_specs=[_row_spec(tm, D_FF_PAD), _weight_spec((D_FF_PAD, D_MODEL)), _row_spec(tm, D_MODEL), _row_spec(tm, D_MODEL),
                  _full_spec((1, D_MODEL)), _full_spec((1, D_MODEL))],
        out_specs=(_row_spec(tm, D_MODEL), _full_spec((1, D_MODEL)), _full_spec((1, D_MODEL)), _full_spec((1, 128))),
        out_shape=(_sds((SEQ, D_MODEL), F32), vec, vec, _sds((1, 128), F32)),
        compiler_params=_cparams(dimension_semantics=("arbitrary",)),
    )(f, w_down, h, target, ln_g, ln_b)


def _ff_down_bwd(dr2, w_down, a, b):
    tm, tn = 1024, 768

    def body(dr_ref, w_ref, a_ref, b_ref, da_ref, db_ref):
        df = _dot_nt(dr_ref[...].astype(BF16), w_ref[...])
        av, bv = a_ref[...].astype(F32), b_ref[...].astype(F32)
        sg = jax.nn.sigmoid(av)
        da_ref[...] = (df * bv * sg * (1.0 + av * (1.0 - sg))).astype(BF16)
        db_ref[...] = (df * av * sg).astype(BF16)

    tile = pl.BlockSpec((tm, tn), lambda i, j: (i, j))
    out = _sds((SEQ, D_FF_PAD), BF16)
    return _pallas_call(
        body, name="ff_down_bwd", grid=(SEQ // tm, D_FF_PAD // tn),
        in_specs=[pl.BlockSpec((tm, D_MODEL), lambda i, j: (i, 0)), pl.BlockSpec((tn, D_MODEL), lambda i, j: (j, 0)), tile, tile],
        out_specs=(tile, tile), out_shape=(out, out),
        compiler_params=_cparams(dimension_semantics=("arbitrary", "arbitrary")),
    )(dr2, w_down, a, b)


def _ff_up_bwd(da, db, w_gate, w_up, dr2, xhat1, rstd1, ln_g, ride=None):
    tm, tk = 1024, 768
    nk = D_FF_PAD // tk

    def body(da_ref, db_ref, wg_ref, wu_ref, dr2_ref, xhat_ref, rstd_ref, g_ref, dr1_ref, dg_ref, dbias_ref, acc):
        i, k = pl.program_id(0), pl.program_id(1)

        @pl.when(jnp.logical_and(i == 0, k == 0))
        def _():
            dg_ref[...] = jnp.zeros_like(dg_ref)
            dbias_ref[...] = jnp.zeros_like(dbias_ref)

        part = _dot_nt(da_ref[...], _side_by_side(wg_ref)) + _dot_nt(db_ref[...], _side_by_side(wu_ref))

        @pl.when(k == 0)
        def _():
            acc[...] = part

        @pl.when(k > 0)
        def _():
            acc[...] += part

        @pl.when(k == nk - 1)
        def _():
            dh = DN_ALPHA * dr2_ref[...] + acc[...]
            xhat = xhat_ref[...]
            dg_ref[...] += jnp.sum(dh * xhat, axis=0, keepdims=True)
            dbias_ref[...] += jnp.sum(dh, axis=0, keepdims=True)
            rstd = jnp.max(rstd_ref[...], axis=1, keepdims=True)
            dr1_ref[...] = _layer_norm_bwd(dh, xhat, rstd, g_ref[...])

    hid = pl.BlockSpec((tm, tk), lambda i, k: (i, k))
    wtile = pl.BlockSpec((tk // FF_PAD, D_MODEL, FF_PAD), lambda i, k: (k, 0, 0))
    row = pl.BlockSpec((tm, D_MODEL), lambda i, k: (i, 0))
    vec = pl.BlockSpec((1, D_MODEL), lambda i, k: (0, 0))
    return _call(
        body, "ff_up_bwd", (SEQ // tm, nk),
        [hid, hid, wtile, wtile, row, row, pl.BlockSpec((tm, 128), lambda i, k: (i, 0)), vec],
        [row, vec, vec], [_sds((SEQ, D_MODEL), F32), _sds((1, D_MODEL), F32), _sds((1, D_MODEL), F32)],
        [pltpu.VMEM((tm, D_MODEL), F32)], [da, db, w_gate, w_up, dr2, xhat1, rstd1, ln_g], ride)


def _mixer_bwd(dr1, proj, y_attn, y_ssm, glu, ys, w_ab, w_sb, w_glu, w_out, b_gate):
    tm = 256

    def body(dr1_ref, gl0_ref, gl1_ref, ya_ref, yssm_ref, glu_ref, ys_ref, wab_ref, wsb_ref, wglu_ref, wout_ref, bg_ref,
             dya_ref, dyssm_ref, dgl_ref, dattn_ref, dglu_ref, dys_ref, mixed_ref, ysb_ref, gy_ref, dbg_ref):
        @pl.when(pl.program_id(0) == 0)
        def _():
            dbg_ref[...] = jnp.zeros_like(dbg_ref)

        dmixed = _dot_nt(dr1_ref[...].astype(BF16), wout_ref[...])
        g0 = jax.nn.sigmoid(gl0_ref[...] + _side_by_side(bg_ref, 0))
        g1 = jax.nn.sigmoid(gl1_ref[...] + _side_by_side(bg_ref, 1))
        y_attn, y_ssm = ya_ref[...], yssm_ref[...]
        mixed_ref[...] = (g0 * y_attn + g1 * y_ssm).astype(BF16)
        dya = (dmixed * g0).astype(BF16)
        dyssm = (dmixed * g1).astype(BF16)
        dya_ref[...] = dya
        dyssm_ref[...] = dyssm
        dgl0 = dmixed * y_attn * g0 * (1.0 - g0)
        dgl1 = dmixed * y_ssm * g1 * (1.0 - g1)
        dgl_ref[:, :GL_COL * D_MODEL] = jnp.zeros((tm, GL_COL * D_MODEL), BF16)
        dgl_ref[:, GL_COL * D_MODEL:(GL_COL + 1) * D_MODEL] = dgl0.astype(BF16)
        dgl_ref[:, (GL_COL + 1) * D_MODEL:] = dgl1.astype(BF16)
        dbg_ref[:, :D_MODEL] += jnp.sum(dgl0, axis=0, keepdims=True)
        dbg_ref[:, D_MODEL:] += jnp.sum(dgl1, axis=0, keepdims=True)
        dattn_ref[...] = _dot_nt(dya, _side_by_side(wab_ref))
        dy_s = _dot_nt(dyssm, _side_by_side(wsb_ref))
        glu = glu_ref[...]
        glu1, sg = glu[:, :SSM_WIDTH], jax.nn.sigmoid(glu[:, SSM_WIDTH:])
        ysb_ref[...] = (glu1 * sg).astype(BF16)
        dglu1 = (dy_s * sg).astype(BF16)
        dglu2 = (dy_s * glu1 * sg * (1.0 - sg)).astype(BF16)
        dglu_ref[:, :SSM_WIDTH] = dglu1
        dglu_ref[:, SSM_WIDTH:] = dglu2
        dgy = _dot_nt(jnp.concatenate([dglu1, dglu2], axis=1), _side_by_side(wglu_ref))
        ys = ys_ref[...]
        gy, t = _gelu(ys)
        gy_ref[...] = gy.astype(BF16)
        dys_ref[...] = dgy * _gelu_grad(ys, t)

    wide_b, half_b = _sds((SEQ, D_MODEL), BF16), _sds((SEQ, SSM_WIDTH), BF16)
    half_f = _sds((SEQ, SSM_WIDTH), F32)
    return _pallas_call(
        body, name="mixer_bwd", grid=(SEQ // tm,),
        in_specs=[_row_spec(tm, D_MODEL), _row_spec(tm, D_MODEL, GL_COL), _row_spec(tm, D_MODEL, GL_COL + 1), _row_spec(tm, D_MODEL),
                  _row_spec(tm, D_MODEL), _row_spec(tm, D_MODEL), _row_spec(tm, SSM_WIDTH), _full_spec((N_DEV, ATTN_WIDTH, 128)),
                  _full_spec((N_DEV, SSM_WIDTH, 128)), _full_spec((N_DEV, SSM_WIDTH, 128)), _full_spec((D_MODEL, D_MODEL)),
                  _full_spec((N_DEV, 2, 128))],
        out_specs=(_row_spec(tm, D_MODEL), _row_spec(tm, D_MODEL), _row_spec(tm, IN_WIDTH), _row_spec(tm, ATTN_WIDTH),
                   _row_spec(tm, D_MODEL), _row_spec(tm, SSM_WIDTH), _row_spec(tm, D_MODEL), _row_spec(tm, SSM_WIDTH),
                   _row_spec(tm, SSM_WIDTH), _full_spec((1, 2 * D_MODEL))),
        out_shape=(wide_b, wide_b, _sds((SEQ, IN_WIDTH), BF16), half_f, wide_b, half_f, wide_b, half_b, half_b,
                   _sds((1, 2 * D_MODEL), F32)),
        compiler_params=_cparams(dimension_semantics=("arbitrary",)),
    )(dr1, proj, proj, y_attn, y_ssm, glu, ys, w_ab, w_sb, w_glu, w_out, b_gate)


def _grad_x(dproj, w_in, dr1, ride=None):
    tm, tk = 1024, 1792
    nk = IN_WIDTH // tk

    def body(dp_ref, w_ref, dr1_ref, o_ref, acc):
        k = pl.program_id(1)
        part = _dot_nt(dp_ref[...], _side_by_side(w_ref))

        @pl.when(k == 0)
        def _():
            acc[...] = part

        @pl.when(k > 0)
        def _():
            acc[...] += part

        @pl.when(k == nk - 1)
        def _():
            o_ref[...] = DN_ALPHA * dr1_ref[...] + acc[...]

    row = pl.BlockSpec((tm, D_MODEL), lambda i, k: (i, 0))
    return _call(
        body, "grad_x", (SEQ // tm, nk),
        [pl.BlockSpec((tm, tk), lambda i, k: (i, k)), pl.BlockSpec((2, D_MODEL, tk // 2), lambda i, k: (k, 0, 0)), row],
        [row], [_sds((SEQ, D_MODEL), F32)], [pltpu.VMEM((tm, D_MODEL), F32)], [dproj, w_in, dr1], ride)


def _weight_grad(a, b, name, shard_cols=None, ride=None):
    k, n = a.shape[1], b.shape[1]
    tk = min(k, 512) if shard_cols else k // N_DEV
    tn = n // 4 if shard_cols else min(n, 1024)

    def body(a_ref, b_ref, o_ref):
        grad = _dot_tn(a_ref[...].astype(BF16), b_ref[...].astype(BF16))
        if shard_cols:
            o_ref[0] = grad[:, :shard_cols].astype(BF16)
            o_ref[1] = grad[:, shard_cols:].astype(BF16)
        else:
            o_ref[...] = grad.astype(BF16)

    if shard_cols:
        out_spec = pl.BlockSpec((2, None, tk, shard_cols), lambda kk, j: (0, j, kk, 0))
        out_shape = _sds((2, 4, k, shard_cols), BF16)
    else:
        out_spec = pl.BlockSpec((None, None, tk, tn), lambda kk, j: (kk % 2, kk // 2, 0, j))
        out_shape = _sds((2, 4, tk, n), BF16)
    out = _call(body, name, (k // tk, n // tn),
                [pl.BlockSpec((SEQ, tk), lambda kk, j: (0, kk)), pl.BlockSpec((SEQ, tn), lambda kk, j: (0, j))],
                [out_spec], [out_shape], [], [a, b], ride)
    return out[0] if ride is None else out


MESH = pl.DeviceIdType.MESH
ANY = pl.BlockSpec(memory_space=pl.ANY)


def _place():
    return lax.axis_index("x"), lax.axis_index("y"), lax.axis_index("c")


def _other_chips(x, y):
    return [(1 - x, y), (x, 1 - y), (1 - x, 1 - y)]


class _Ride:
    def __init__(self, operands, results, aliases, sems, start, wait):
        self.operands, self.results, self.aliases, self.sems = list(operands), list(results), dict(aliases), list(sems)
        self.start, self.wait = start, wait

    def __add__(self, other):
        n_in, n_out, n_sem = len(self.operands), len(self.results), len(self.sems)

        def both(which):
            def run(ins, outs, sems):
                getattr(self, which)(ins[:n_in], outs[:n_out], sems[:n_sem])
                getattr(other, which)(ins[n_in:], outs[n_out:], sems[n_sem:])
            return run

        aliases = {**self.aliases, **{n_in + i: n_out + j for i, j in other.aliases.items()}}
        return _Ride(self.operands + other.operands, self.results + other.results, aliases, self.sems + other.sems,
                     both("start"), both("wait"))


def _call(body, name, grid, in_specs, out_specs, out_shape, scratch_shapes, operands, ride=None, aliases=None):
    in_specs, out_specs, out_shape = list(in_specs), list(out_specs), list(out_shape)
    scratch_shapes, operands, aliases = list(scratch_shapes), list(operands), dict(aliases or {})
    kernel_body = body
    if ride is not None:
        n_in, n_out, n_scr, r_in, r_out = len(in_specs), len(out_specs), len(scratch_shapes), len(ride.operands), len(ride.results)

        def kernel_body(*refs):
            out0, scr0 = n_in + r_in, n_in + r_in + n_out + r_out
            ride_refs = (refs[n_in:out0], refs[out0 + n_out:scr0], refs[scr0 + n_scr:])
            ids = [pl.program_id(i) for i in range(len(grid))]
            first = functools.reduce(jnp.logical_and, [i == 0 for i in ids])
            last = functools.reduce(jnp.logical_and, [i == g - 1 for i, g in zip(ids, grid)])

            @pl.when(first)
            def _():
                ride.start(*ride_refs)

            body(*refs[:n_in], *refs[out0:out0 + n_out], *refs[scr0:scr0 + n_scr])

            @pl.when(last)
            def _():
                ride.wait(*ride_refs)

        aliases.update({n_in + i: n_out + j for i, j in ride.aliases.items()})
        in_specs += [ANY] * r_in
        out_specs += [ANY] * r_out
        out_shape += ride.results
        scratch_shapes += ride.sems
        operands += ride.operands
    return _pallas_call(
        kernel_body, name=name, grid=grid, in_specs=in_specs, out_specs=out_specs, out_shape=out_shape,
        scratch_shapes=scratch_shapes, input_output_aliases=aliases,
        compiler_params=_cparams(dimension_semantics=("arbitrary",) * len(grid)),
    )(*operands)


def _after(*arrays):
    return _Ride(arrays, [], {}, [], lambda *refs: None, lambda *refs: None)


def _gather_first_level(shards):
    n = len(shards)

    def copies(ins, outs, sems, landed):
        send_sems, recv_sems, local_sems = sems
        x, y, c = _place()
        peers = [(x, y, 1 - c)] + [(px, py, c) for px, py in _other_chips(x, y)]

        def row(peer):
            return 4 * x + 2 * y + c if not landed else 4 * peer[0] + 2 * peer[1] + peer[2]

        local = [pltpu.make_async_copy(ins[a], outs[a].at[4 * x + 2 * y + c], local_sems.at[a]) for a in range(n)]
        remote = [pltpu.make_async_remote_copy(
            src_ref=ins[a], dst_ref=outs[a].at[row(peer)], send_sem=send_sems.at[a, k], recv_sem=recv_sems.at[a, k],
            device_id=peer, device_id_type=MESH) for a in range(n) for k, peer in enumerate(peers)]
        return local, remote

    def start(ins, outs, sems):
        local, remote = copies(ins, outs, sems, False)
        for cp in local + remote:
            cp.start()

    def wait(ins, outs, sems):
        local, sent = copies(ins, outs, sems, False)
        for cp in copies(ins, outs, sems, True)[1]:
            cp.wait_recv()
        for cp in sent:
            cp.wait_send()
        for cp in local:
            cp.wait()

    return _Ride(shards, [_sds((N_DEV,) + s.shape, s.dtype) for s in shards], {},
                 [pltpu.SemaphoreType.DMA((n, 4)), pltpu.SemaphoreType.DMA((n, 4)), pltpu.SemaphoreType.DMA((n,))], start, wait)


def _gather_second_level(buffers):
    n = len(buffers)

    def copies(outs, sems, core):
        send_sems, recv_sems = sems
        x, y, c = _place()
        return [pltpu.make_async_remote_copy(
            src_ref=outs[a].at[4 * px + 2 * py + core], dst_ref=outs[a].at[4 * px + 2 * py + core], send_sem=send_sems.at[a, j],
            recv_sem=recv_sems.at[a, j], device_id=(x, y, 1 - c), device_id_type=MESH)
            for a in range(n) for j, (px, py) in enumerate(_other_chips(x, y))]

    def start(ins, outs, sems):
        for cp in copies(outs, sems, lax.axis_index("c")):
            cp.start()

    def wait(ins, outs, sems):
        for cp in copies(outs, sems, 1 - lax.axis_index("c")):
            cp.wait_recv()
        for cp in copies(outs, sems, lax.axis_index("c")):
            cp.wait_send()

    return _Ride(buffers, [_sds(b.shape, b.dtype) for b in buffers], {i: i for i in range(n)},
                 [pltpu.SemaphoreType.DMA((n, 3)), pltpu.SemaphoreType.DMA((n, 3))], start, wait)


def _relayed_gather(shards):
    n = len(shards)
    buffers = [_sds((N_DEV,) + s.shape, s.dtype) for s in shards]
    dma = pltpu.SemaphoreType.DMA

    def remote(src, dst, send_sem, recv_sem, to):
        return pltpu.make_async_remote_copy(src_ref=src, dst_ref=dst, send_sem=send_sem, recv_sem=recv_sem,
                                            device_id=to, device_id_type=MESH)

    def row(px, py, pc):
        return 4 * px + 2 * py + pc

    def ride(operands, aliases, sems, copies):
        def start(ins, outs, sem_refs):
            local, sent = copies(ins, outs, sem_refs, False)
            for cp in local + sent:
                cp.start()

        def wait(ins, outs, sem_refs):
            local, sent = copies(ins, outs, sem_refs, False)
            for cp in copies(ins, outs, sem_refs, True)[1]:
                cp.wait_recv()
            for cp in sent:
                cp.wait_send()
            for cp in local:
                cp.wait()

        return _Ride(operands, buffers, aliases, sems, start, wait)

    def first(ins, outs, sems, landed):
        x, y, c = _place()
        peers = [(x, y, 1 - c), (1 - x, y, c), (x, 1 - y, c)]
        local = [pltpu.make_async_copy(ins[a], outs[a].at[row(x, y, c)], sems[2].at[a]) for a in range(n)]
        return local, [remote(ins[a], outs[a].at[row(*peer) if landed else row(x, y, c)], sems[0].at[a, k], sems[1].at[a, k], peer)
                       for a in range(n) for k, peer in enumerate(peers)]

    def second(ins, outs, sems, landed):
        x, y, c = _place()
        mine = 1 - c if landed else c
        copies = []
        for a in range(n):
            half = shards[a].shape[0] // 2
            over_x, over_y, diagonal = outs[a].at[row(1 - x, y, mine)], outs[a].at[row(x, 1 - y, mine)], outs[a].at[row(1 - x, 1 - y, c)]
            lower, upper = pl.ds(0, half), pl.ds(half, half)
            copies += [remote(over_x, over_x, sems[0].at[a, 0], sems[1].at[a, 0], (x, y, 1 - c)),
                       remote(over_y, over_y, sems[0].at[a, 1], sems[1].at[a, 1], (x, y, 1 - c))]
            if landed:
                copies += [remote(diagonal.at[lower], diagonal.at[lower], sems[0].at[a, 2], sems[1].at[a, 2], (1 - x, y, c)),
                           remote(diagonal.at[upper], diagonal.at[upper], sems[0].at[a, 3], sems[1].at[a, 3], (x, 1 - y, c))]
            else:
                copies += [remote(over_y.at[lower], over_y.at[lower], sems[0].at[a, 2], sems[1].at[a, 2], (1 - x, y, c)),
                           remote(over_x.at[upper], over_x.at[upper], sems[0].at[a, 3], sems[1].at[a, 3], (x, 1 - y, c))]
        return [], copies

    def third(ins, outs, sems, landed):
        x, y, c = _place()
        return [], [remote(outs[a].at[row(1 - x, 1 - y, 1 - c if landed else c)], outs[a].at[row(1 - x, 1 - y, 1 - c if landed else c)],
                           sems[0].at[a], sems[1].at[a], (x, y, 1 - c)) for a in range(n)]

    def later(copies, n_sems):
        return lambda partly: ride(partly, {i: i for i in range(n)}, [dma((n,) + n_sems), dma((n,) + n_sems)], copies)

    return ride(shards, {}, [dma((n, 3)), dma((n, 3)), dma((n,))], first), later(second, (4,)), later(third, ())


def _sibling_swap_ride(grads):
    n = len(grads)

    def copies(ins, outs, sems):
        x, y, c = _place()
        return [pltpu.make_async_remote_copy(
            src_ref=ins[a].at[1 - c], dst_ref=outs[a], send_sem=sems[0].at[a], recv_sem=sems[1].at[a],
            device_id=(x, y, 1 - c), device_id_type=MESH) for a in range(n)]

    def start(ins, outs, sems):
        for cp in copies(ins, outs, sems):
            cp.start()

    def wait(ins, outs, sems):
        for cp in copies(ins, outs, sems):
            cp.wait()

    return _Ride(grads, [_sds(g.shape[1:], g.dtype) for g in grads], {},
                 [pltpu.SemaphoreType.DMA((n,)), pltpu.SemaphoreType.DMA((n,))], start, wait)


def _chip_swap_ride(sums):
    n = len(sums)

    def copies(ins, outs, sems, landed):
        send_sems, recv_sems, local_sems = sems
        x, y, c = _place()
        mine = 2 * x + y
        local = [pltpu.make_async_copy(ins[a].at[mine], outs[a].at[mine], local_sems.at[a]) for a in range(n)]
        remote = [pltpu.make_async_remote_copy(
            src_ref=ins[a].at[2 * px + py], dst_ref=outs[a].at[2 * px + py if landed else mine], send_sem=send_sems.at[a, j],
            recv_sem=recv_sems.at[a, j], device_id=(px, py, c), device_id_type=MESH)
            for a in range(n) for j, (px, py) in enumerate(_other_chips(x, y))]
        return local, remote

    def start(ins, outs, sems):
        local, remote = copies(ins, outs, sems, False)
        for cp in local + remote:
            cp.start()

    def wait(ins, outs, sems):
        local, sent = copies(ins, outs, sems, False)
        for cp in copies(ins, outs, sems, True)[1]:
            cp.wait_recv()
        for cp in sent:
            cp.wait_send()
        for cp in local:
            cp.wait()

    return _Ride(sums, [_sds(s.shape, s.dtype) for s in sums], {},
                 [pltpu.SemaphoreType.DMA((n, 3)), pltpu.SemaphoreType.DMA((n, 3)), pltpu.SemaphoreType.DMA((n,))], start, wait)


def _send_buffers(shards, name):
    n = len(shards)

    def body(*refs):
        for (w, transposed, rows, cols), w_ref, o_ref in zip(shards, refs[:n], refs[n:]):
            if transposed:
                c, r = w.shape
                padded = jnp.concatenate([w_ref[...], jnp.zeros((cols - c, r), F32)], axis=0) if cols > c else w_ref[...]
                o_ref[...] = padded.T.astype(BF16)
            else:
                r, c = w.shape
                if (r, c) != (rows, cols):
                    o_ref[...] = jnp.zeros((rows, cols), BF16)
                o_ref[:r, :c] = w_ref[...].astype(BF16)

    return _pallas_call(body, name=name, out_shape=[_sds((rows, cols), BF16) for _, _, rows, cols in shards])(
        *[w for w, _, _, _ in shards])


def _all_gather(shards, name):
    n = len(shards)
    first, second, third = _relayed_gather(shards)
    levels = [first, second(shards), third(shards)]
    counts = [len(level.sems) for level in levels]

    def body(*refs):
        ins, outs, sems = refs[:n], refs[n:2 * n], refs[2 * n:]
        for i, level in enumerate(levels):
            mine = sems[sum(counts[:i]):sum(counts[:i + 1])]
            level.start(ins, outs, mine)
            level.wait(ins, outs, mine)

    return _pallas_call(
        body, name=name, in_specs=[ANY] * n, out_specs=[ANY] * n, out_shape=first.results,
        scratch_shapes=[s for level in levels for s in level.sems],
    )(*shards)


def _exchange(ride, name):
    n_in, n_out = len(ride.operands), len(ride.results)

    def body(*refs):
        ride.start(refs[:n_in], refs[n_in:n_in + n_out], refs[n_in + n_out:])
        ride.wait(refs[:n_in], refs[n_in:n_in + n_out], refs[n_in + n_out:])

    return _pallas_call(body, name=name, in_specs=[ANY] * n_in, out_specs=[ANY] * n_out, out_shape=ride.results,
                        scratch_shapes=ride.sems, input_output_aliases=ride.aliases)(*ride.operands)


HBM = pl.BlockSpec(memory_space=pltpu.HBM)
SEMAPHORES = pl.BlockSpec(memory_space=pltpu.SEMAPHORE)
IN_FLIGHT = pltpu.CompilerParams(has_side_effects=pltpu.SideEffectType.DATAFLOW_SIDE_EFFECTING)


def _chip_swap_copies(src_refs, land_refs, send_sems, recv_sems, landed):
    x, y, c = _place()
    return [pltpu.make_async_remote_copy(
        src_ref=src.at[2 * px + py], dst_ref=land.at[2 * px + py if landed else 2 * x + y], send_sem=send_sems.at[3 * a + j],
        recv_sem=recv_sems.at[3 * a + j], device_id=(px, py, c), device_id_type=MESH)
        for a, (src, land) in enumerate(zip(src_refs, land_refs)) for j, (px, py) in enumerate(_other_chips(x, y))]


def _chip_swap_start(sums, name):
    n = len(sums)

    def body(*refs):
        src_refs, land_refs, (send_sems, recv_sems), token = refs[:n], refs[n:2 * n], refs[2 * n:2 * n + 2], refs[-1]
        for cp in _chip_swap_copies(src_refs, land_refs, send_sems, recv_sems, False):
            cp.start()
        token[...] = jnp.zeros_like(token)

    kept = [pltpu.HBM(s.shape, s.dtype) for s in sums]
    out = _pallas_call(
        body, name=name,
        out_shape=[pltpu.SemaphoreType.DMA((3 * n,)), pltpu.SemaphoreType.DMA((3 * n,))] + kept + kept + [_sds((8, 128), F32)],
        in_specs=[HBM] * (2 * n), out_specs=[SEMAPHORES, SEMAPHORES] + [HBM] * (2 * n) + [pl.BlockSpec(memory_space=pltpu.VMEM)],
        input_output_aliases={i: 2 + i for i in range(2 * n)}, compiler_params=IN_FLIGHT,
    )(*[pltpu.with_memory_space_constraint(s, pltpu.HBM) for s in sums],
      *[pltpu.with_memory_space_constraint(lax.empty(s.shape, s.dtype), pltpu.HBM) for s in sums])
    return out[0], out[1], out[2:2 + n], out[2 + n:2 + 2 * n], out[-1]


def _chip_swap_wait(send_sems, recv_sems, sums, landings, after, name):
    n = len(sums)

    def body(*refs):
        src_refs, land_refs, (send_sems, recv_sems) = refs[:n], refs[n:2 * n], refs[2 * n:2 * n + 2]
        for cp in _chip_swap_copies(src_refs, land_refs, send_sems, recv_sems, False):
            cp.wait_send()
        for cp in _chip_swap_copies(src_refs, land_refs, send_sems, recv_sems, True):
            cp.wait_recv()

    out = _pallas_call(
        body, name=name, out_shape=[pltpu.HBM(s.shape, s.dtype) for s in list(sums) + list(landings)],
        in_specs=[HBM] * (2 * n) + [SEMAPHORES, SEMAPHORES] + [ANY] * len(after), out_specs=[HBM] * (2 * n),
        input_output_aliases={i: i for i in range(2 * n)}, compiler_params=IN_FLIGHT,
    )(*sums, *landings, send_sems, recv_sems, *after)
    return out[:n], out[n:]


def _pair_sums(gs, rs, core, name):
    n_arrays = len(gs)

    def body(core_ref, *refs):
        for g_ref, r_ref, o_ref in zip(refs[:n_arrays], refs[n_arrays:2 * n_arrays], refs[2 * n_arrays:]):
            o_ref[...] = (g_ref[...].astype(F32) + r_ref[...].astype(F32)).astype(o_ref.dtype)

    def own(g):
        return pl.BlockSpec((None, None) + g.shape[2:], lambda p, core_ref: (core_ref[0], p, 0, 0))

    def chip(g):
        return pl.BlockSpec((None,) + g.shape[2:], lambda p, core_ref: (p, 0, 0))

    return _pallas_call(
        body, name=name,
        grid_spec=pltpu.PrefetchScalarGridSpec(
            num_scalar_prefetch=1, grid=(4,), in_specs=[own(g) for g in gs] + [chip(g) for g in gs],
            out_specs=[chip(g) for g in gs]),
        out_shape=[_sds(g.shape[1:], g.dtype) for g in gs], compiler_params=_cparams(dimension_semantics=("arbitrary",)),
    )(core, *gs, *rs)


def _adamw_math(w, g, m, v):
    m = ADAM_B1 * m + (1.0 - ADAM_B1) * g
    v = ADAM_B2 * v + (1.0 - ADAM_B2) * (g * g)
    m_hat = m / (1.0 - ADAM_B1 ** ADAM_STEP)
    v_hat = v / (1.0 - ADAM_B2 ** ADAM_STEP)
    return -ADAM_LR * (m_hat / (jnp.sqrt(v_hat) + ADAM_EPS) + ADAM_WD * w), m, v


def _adamw_many(weights, name, ride=None):
    steps = 4
    in_specs, out_specs, out_shape, operands, tiles = [], [], [], [], []
    for w, m, v, parts, own, transposed in weights:
        _, pr, pc = parts.shape
        if transposed:
            c, r = w.shape
            tile = pl.BlockSpec((c, r // steps), lambda i: (0, i))
            part_tile = pl.BlockSpec((4, r // steps, pc), lambda i: (0, i, 0))
            tiles.append((c, r // steps))
        elif w.shape[0] % (8 * steps) == 0:
            r, c = w.shape
            tile = pl.BlockSpec((r // steps, c), lambda i: (i, 0))
            part_tile = pl.BlockSpec((4, r // steps, pc), lambda i: (0, i, 0))
            tiles.append((r // steps, c))
        else:
            tile = pl.BlockSpec(w.shape, lambda i: (0, 0))
            part_tile = pl.BlockSpec(parts.shape, lambda i: (0, 0, 0))
            tiles.append(w.shape)
        in_specs += [tile, tile, tile] + [part_tile] * (1 if own is None else 2)
        out_specs += [tile] * 4
        out_shape += [_sds(w.shape, F32)] * 4
        operands += [w, m, v, parts] + ([] if own is None else [own])
    n_in = len(operands)

    def body(*refs):
        ins, outs = list(refs[:n_in]), refs[n_in:]
        this_chip = 2 * lax.axis_index("x") + lax.axis_index("y")
        for k, (_, _, _, _, own, transposed) in enumerate(weights):
            w_ref, m_ref, v_ref, p_ref = ins[:4]
            own_ref = None if own is None else ins[4]
            del ins[:4 if own is None else 5]
            rows, cols = tiles[k]
            g = None
            for q in range(4):
                index = (q,) if transposed else (q, slice(0, rows), slice(0, cols))
                part = p_ref[index] if own is None else jnp.where(this_chip == q, own_ref[index], p_ref[index])
                g = part.astype(F32) if g is None else g + part.astype(F32)
            if transposed:
                g = g.T[:rows]
            g_out, d_out, m_out, v_out = outs[4 * k:4 * k + 4]
            g_out[...] = g
            d_out[...], m_out[...], v_out[...] = _adamw_math(w_ref[...], g, m_ref[...], v_ref[...])

    return _call(body, name, (steps,), in_specs, out_specs, out_shape, [], operands, ride)


SMALL = ("ssm_a_re", "ssm_a_im", "ssm_log_dt", "ssm_b_re", "ssm_b_im", "ssm_c_re", "ssm_c_im", "ssm_d",
         "ln1_g", "ln1_b", "ln2_g", "ln2_b")


def _pack_rows(arrays):
    rows = []
    for a in arrays:
        flat = a.reshape(-1)
        rows.append(jnp.pad(flat, (0, -flat.shape[0] % 128)).reshape(-1, 128))
    packed = jnp.concatenate(rows, axis=0)
    return jnp.pad(packed, ((0, -packed.shape[0] % 8), (0, 0)))


def _unpack_rows(packed, shapes):
    out, row = [], 0
    for shape in shapes:
        size = math.prod(shape)
        n_rows = -(-size // 128)
        out.append(packed[row:row + n_rows].reshape(-1)[:size].reshape(shape))
        row += n_rows
    return out


def _sum_devices(parts):
    def body(p_ref, o_ref):
        total = p_ref[0]
        for dev in range(1, N_DEV):
            total = total + p_ref[dev]
        o_ref[...] = total

    return _pallas_call(body, name="sum_devices", out_shape=_sds(parts.shape[1:], F32))(parts)


def _adamw_replicated(ws, ms, vs, gs):
    n = len(ws)

    def body(*refs):
        w_refs, m_refs, v_refs, g_refs, d_out, m_out, v_out = (refs[i * n:(i + 1) * n] for i in range(7))
        for i in range(n):
            d_out[i][...], m_out[i][...], v_out[i][...] = _adamw_math(w_refs[i][...], g_refs[i][...], m_refs[i][...], v_refs[i][...])

    out = _pallas_call(body, name="adamw_replicated", out_shape=[_sds(w.shape, F32) for w in ws] * 3,
                       compiler_params=_cparams())(*ws, *ms, *vs, *gs)
    return out[:n], out[n:2 * n], out[2 * n:]


def kernel(x, w_in, b_gate, w_attn_br, w_ssm_br, w_out, ssm_a_re, ssm_a_im, ssm_log_dt, ssm_b_re, ssm_b_im, ssm_c_re, ssm_c_im, ssm_d, w_glu, ln1_g, ln1_b, w_ff_gate, w_ff_up, w_ff_down, ln2_g, ln2_b, loss_target, m_w_in, m_b_gate, m_w_attn_br, m_w_ssm_br, m_w_out, m_ssm_a_re, m_ssm_a_im, m_ssm_log_dt, m_ssm_b_re, m_ssm_b_im, m_ssm_c_re, m_ssm_c_im, m_ssm_d, m_w_glu, m_ln1_g, m_ln1_b, m_w_ff_gate, m_w_ff_up, m_w_ff_down, m_ln2_g, m_ln2_b, v_w_in, v_b_gate, v_w_attn_br, v_w_ssm_br, v_w_out, v_ssm_a_re, v_ssm_a_im, v_ssm_log_dt, v_ssm_b_re, v_ssm_b_im, v_ssm_c_re, v_ssm_c_im, v_ssm_d, v_w_glu, v_ln1_g, v_ln1_b, v_w_ff_gate, v_w_ff_up, v_w_ff_down, v_ln2_g, v_ln2_b):
    given = dict(locals())
    x2, target = x[0], loss_target[0]
    core = lax.axis_index("c").astype(jnp.int32).reshape(1)

    sharded = ("w_in", "w_attn_br", "w_ssm_br", "w_glu", "w_ff_gate", "w_ff_up", "b_gate", "w_out", "w_ff_down")
    send_shape = dict(w_in=(D_MODEL, 896), w_attn_br=(ATTN_WIDTH, 128), w_ssm_br=(SSM_WIDTH, 128), w_glu=(SSM_WIDTH, 128),
                      w_out=(128, D_MODEL), w_ff_gate=(D_MODEL, FF_PAD), w_ff_up=(D_MODEL, FF_PAD), w_ff_down=(FF_PAD, D_MODEL))
    local = {k: given[k][0] for k in sharded}
    narrow = ("w_ff_gate", "w_ff_up")
    def to_send(k):
        return (local[k].T, True, *send_shape[k]) if k in narrow else (local[k], False, *send_shape[k])

    later = [k for k in sharded if k not in ("w_in", "b_gate")]
    sends = dict(zip(["w_in"] + later, _send_buffers([to_send("w_in")], "send_w_in")
                     + _send_buffers([to_send(k) for k in later], "send_weights")))
    sends["b_gate"] = local["b_gate"]
    mixer_weights = ("w_attn_br", "w_ssm_br", "w_glu", "b_gate", "w_out")
    ff_weights = ("w_ff_gate", "w_ff_up", "w_ff_down")
    wt = {}
    wt["w_in"], = _all_gather([sends["w_in"]], "gather_w_in")

    a_re, a_im, log_dt = ssm_a_re[0], ssm_a_im[0], ssm_log_dt[0].reshape(SSM_GROUPS, 1)
    b_re_t, b_im_t = ssm_b_re[0].transpose(0, 2, 1), ssm_b_im[0].transpose(0, 2, 1)
    abar_re, abar_im, e_re, e_im, bbar_re_t, bbar_im_t = _ssm_prep(a_re, a_im, log_dt, b_re_t, b_im_t)
    bmat, cmat, a_chunks = _ssm_tables(abar_re, abar_im, bbar_re_t, bbar_im_t, ssm_c_re[0], ssm_c_im[0])
    cos_t, sin_t = _rope_tables()

    big_mixer, ff_in = [k for k in mixer_weights if k != "b_gate"], ("w_ff_gate", "w_ff_up")
    n_mixer = len(big_mixer)
    mixer_1, mixer_2, mixer_3 = _relayed_gather([sends[k] for k in big_mixer])
    ff_in_1, ff_in_2, ff_in_3 = _relayed_gather([sends[k] for k in ff_in])
    ff_down_1, ff_down_2, ff_down_3 = _relayed_gather([sends["w_ff_down"]])
    proj, *landed = _proj(x2, wt["w_in"], mixer_1 + _gather_first_level([sends["b_gate"]]))
    mixer, bias = landed[:n_mixer], landed[n_mixer:]
    attn, lse, *landed = _attn_fwd(proj, cos_t, sin_t, mixer_2(mixer) + _gather_second_level(bias) + ff_in_1)
    mixer, b_gate_full, ff = landed[:n_mixer], landed[n_mixer], landed[n_mixer + 1:]
    ys, states, *landed = _ssm_fwd(proj, bmat, cmat, a_chunks, ssm_d, mixer_3(mixer) + ff_in_2(ff) + ff_down_1)
    wt.update(zip(big_mixer, landed[:n_mixer]))
    ff, ff_down = landed[n_mixer:n_mixer + 2], landed[n_mixer + 2:]
    wt["w_out"] = wt["w_out"].reshape(D_MODEL, D_MODEL)
    h, xhat1, rstd1, glu, y_attn, y_ssm, *landed = _mixer_out(
        attn, ys, proj, x2, wt["w_attn_br"], wt["w_ssm_br"], wt["w_glu"], wt["w_out"], b_gate_full, ln1_g, ln1_b,
        ff_in_3(ff) + ff_down_2(ff_down))
    wt.update(zip(ff_in, landed[:2]))
    ff_a, ff_b, ff_f, w_ff_down = _ff_up(h, wt["w_ff_gate"], wt["w_ff_up"], ff_down_3(landed[2:]))
    wt["w_ff_down"] = w_ff_down.reshape(D_FF_PAD, D_MODEL)
    dr2, d_ln2_g, d_ln2_b, loss_lanes = _ff_down_loss(ff_f, wt["w_ff_down"], h, target, ln2_g, ln2_b)

    def pair_sums(names, contrib, from_sibling):
        return _pair_sums([contrib[k] for k in names], from_sibling, core, "pair_sums_" + names[0])

    d_a, d_b = _ff_down_bwd(dr2, wt["w_ff_down"], ff_a, ff_b)
    contrib = dict(w_ff_gate=_weight_grad(h, d_a, "wgrad_w_ff_gate", FF_PAD),
                   w_ff_up=_weight_grad(h, d_b, "wgrad_w_ff_up", FF_PAD),
                   w_ff_down=_weight_grad(ff_f, dr2, "wgrad_w_ff_down"))
    dr1, d_ln1_g, d_ln1_b, *from_sibling = _ff_up_bwd(
        d_a, d_b, wt["w_ff_gate"], wt["w_ff_up"], dr2, xhat1, rstd1, ln1_g, _sibling_swap_ride([contrib[k] for k in ff_weights]))
    ff_sums = pair_sums(ff_weights, contrib, from_sibling)

    d_ya, d_yssm, d_proj, d_attn, d_glu, d_ys, mixed, y_s, gy, d_bg = _mixer_bwd(
        dr1, proj, y_attn, y_ssm, glu, ys, wt["w_attn_br"], wt["w_ssm_br"], wt["w_glu"], wt["w_out"], b_gate_full)
    contrib.update(w_attn_br=_weight_grad(attn, d_ya, "wgrad_w_attn_br", 128),
                   w_ssm_br=_weight_grad(y_s, d_yssm, "wgrad_w_ssm_br", 128),
                   w_glu=_weight_grad(gy, d_glu, "wgrad_w_glu", 128),
                   w_out=_weight_grad(mixed, dr1, "wgrad_w_out"),
                   b_gate=d_bg.reshape(2, 4, 2, 128).transpose(2, 1, 0, 3))
    d_proj, *landed = _attn_bwd(proj, cos_t, sin_t, attn, lse, d_attn, d_proj,
                                _chip_swap_ride(ff_sums) + _sibling_swap_ride([contrib[k] for k in mixer_weights]))
    parts, own_sums = dict(zip(ff_weights, landed[:len(ff_weights)])), {}
    mixer_sums = pair_sums(mixer_weights, contrib, landed[len(ff_weights):])
    d_proj, d_bmat, d_cmat, d_abar, d_skip, *landed = _ssm_bwd(d_ys, proj, states, bmat, cmat, a_chunks, ssm_d, d_proj,
                                                               _chip_swap_ride(mixer_sums))
    parts.update(zip(mixer_weights, landed))

    gbb_re_t, gbb_im_t = _block_diag_parts(d_bmat, True)
    gc_re, gc_im = _block_diag_parts(d_cmat, False)
    ga_re = d_abar[:, 0, :CHUNK_STATES].reshape(SSM_GROUPS, SSM_STATE)
    ga_im = d_abar[:, 0, CHUNK_STATES:].reshape(SSM_GROUPS, SSM_STATE)
    g_a_re, g_a_im, g_log_dt, g_b_re_t, g_b_im_t = _ssm_param_bwd(
        a_re, a_im, log_dt, b_re_t, b_im_t, abar_re, abar_im, e_re, e_im, ga_re, ga_im, gbb_re_t, gbb_im_t)
    mine = [g_a_re, g_a_im, g_log_dt, g_b_re_t, g_b_im_t, gc_re, -gc_im,
            d_skip, d_ln1_g, d_ln1_b, d_ln2_g, d_ln2_b]
    small_packed = _pack_rows(mine + [loss_lanes])

    contrib["w_in"], small_partly = _weight_grad(x2, d_proj, "wgrad_w_in", 896, _gather_first_level([small_packed]))
    from_sibling, every = _exchange(_sibling_swap_ride([contrib["w_in"]]) + _gather_second_level([small_partly]),
                                    "swap_w_in_with_sibling")
    w_in_sum, = pair_sums(["w_in"], contrib, [from_sibling])
    send_sems, recv_sems, w_in_sum, landing, token = _chip_swap_start([w_in_sum], "w_in_chip_swap_start")

    def adamw_of(k):
        taken = (lambda a: a.T) if k in narrow else (lambda a: a)
        return taken(local[k]), taken(given["m_" + k][0]), taken(given["v_" + k][0]), parts[k], own_sums.get(k), k in narrow

    others = [k for k in sharded if k != "w_in"]
    updated = _adamw_many([adamw_of(k) for k in others], "adamw_others", _after(token))
    grad_x, = _grad_x(d_proj, wt["w_in"], dr1, _after(token))

    def held(k, a):
        return a.transpose(0, 1, 3, 2) if k in ("ssm_b_re", "ssm_b_im") else a

    *small_grads, loss_sum = _unpack_rows(_sum_devices(every), [held(k, given[k]).shape for k in SMALL] + [(1, 128)])
    small = _adamw_replicated([held(k, given[k]) for k in SMALL], [held(k, given["m_" + k]) for k in SMALL],
                              [held(k, given["v_" + k]) for k in SMALL], small_grads)
    loss = loss_sum[0, 0]

    (own_sums["w_in"],), (parts["w_in"],) = _chip_swap_wait(
        send_sems, recv_sems, w_in_sum, landing, [grad_x, updated[0], small[0][0]], "w_in_chip_swap_wait")
    updated += _adamw_many([adamw_of("w_in")], "adamw_w_in")

    grads, deltas, new_m, new_v = {}, {}, {}, {}
    for i, k in enumerate(others + ["w_in"]):
        out = [o.T if k in narrow else o for o in updated[4 * i:4 * i + 4]]
        grads[k], deltas[k], new_m[k], new_v[k] = (o.reshape((1,) + local[k].shape) for o in out)
    for res, values in zip((grads, deltas, new_m, new_v), (small_grads,) + small):
        res.update((k, held(k, a)) for k, a in zip(SMALL, values))

    order = ("w_in", "b_gate", "w_attn_br", "w_ssm_br", "w_out", "ssm_a_re", "ssm_a_im", "ssm_log_dt", "ssm_b_re", "ssm_b_im",
             "ssm_c_re", "ssm_c_im", "ssm_d", "w_glu", "ln1_g", "ln1_b", "w_ff_gate", "w_ff_up", "w_ff_down", "ln2_g", "ln2_b")
    return (loss, grad_x[None], *[grads[k] for k in order], *[deltas[k] for k in order], *[new_m[k] for k in order],
            *[new_v[k] for k in order])
```

```python
import functools
import math

import jax
import jax.numpy as jnp
import numpy as np
from jax import lax
from jax.experimental import pallas as pl
from jax.experimental.pallas import tpu as pltpu

F32 = jnp.float32
BF16 = jnp.bfloat16

N_DEV = 8
SEQ = 2048
D_MODEL = 1024
HEAD_DIM = 64
ATTN_WIDTH = 512
QKV_WIDTH = 1536
SSM_WIDTH = 512
SSM_GROUPS = 32
SSM_GROUP = 16
SSM_STATE = 64
IN_WIDTH = 7168
D_FF = 2816
FF_SHARD = D_FF // N_DEV
FF_PAD = 384
D_FF_PAD = FF_PAD * N_DEV
DN_ALPHA = 2.0 ** 0.25
LN_EPS = 1e-5
NEG_INF = -1e30
ROPE_THETA = 10000.0
BLOCK = 128
GROUPS = ((1, 16), (4, 4), (16, 1))

ADAM_LR = 0.001
ADAM_B1 = 0.9
ADAM_B2 = 0.999
ADAM_EPS = 1e-08
ADAM_WD = 0.01
ADAM_STEP = 10

VMEM_LIMIT = 56 * 1024 * 1024


_pallas_call = pl.pallas_call


def _cparams(**kw):
    return pltpu.CompilerParams(vmem_limit_bytes=VMEM_LIMIT, **kw)


def _dot(a, b):
    return jnp.dot(a, b, preferred_element_type=F32)


def _dot_nt(a, b):
    return lax.dot_general(a, b, (((1,), (1,)), ((), ())), preferred_element_type=F32)


def _side_by_side(w_ref, row=None):
    rows = slice(None) if row is None else pl.ds(row, 1)
    return jnp.concatenate([w_ref[i, rows, :] for i in range(w_ref.shape[0])], axis=1)


def _dot_tn(a, b):
    return lax.dot_general(a, b, (((0,), (0,)), ((), ())), preferred_element_type=F32)


def _rope_tables():
    half = HEAD_DIM // 2
    inv_freq = np.float32(ROPE_THETA) ** (-np.arange(half, dtype=np.float32) / np.float32(half))
    ang = np.arange(SEQ, dtype=np.float32)[:, None] * inv_freq[None, :]
    cos, sin = np.cos(ang).astype(np.float32), np.sin(ang).astype(np.float32)
    tables = np.tile(cos, (1, 4)), np.tile(np.concatenate([-sin, sin], axis=1), (1, 2))

    def by_phase(t):
        return np.stack([t.reshape(SEQ // d, d, 128).transpose(1, 0, 2).reshape(SEQ, 128) for d, _ in GROUPS])

    return jnp.asarray(by_phase(tables[0])), jnp.asarray(by_phase(tables[1]))


def _swap_halves(x):
    lane = lax.broadcasted_iota(jnp.int32, x.shape, 1)
    return jnp.where((lane & 63) < 32, pltpu.roll(x, 96, axis=1), pltpu.roll(x, 32, axis=1))


def _group_rows(d, nb, r, i):
    src = pl.ds(i * BLOCK, BLOCK) if d == 1 else pl.ds(r + i * BLOCK * d, BLOCK, stride=d)
    return src, pl.ds((r * nb + i) * BLOCK, BLOCK)


def _attn_masks():
    a_idx = lax.broadcasted_iota(jnp.int32, (2 * BLOCK, 2 * BLOCK), 0) & (BLOCK - 1)
    c_idx = lax.broadcasted_iota(jnp.int32, (2 * BLOCK, 2 * BLOCK), 1)
    cur_ok = jnp.logical_and(c_idx >= BLOCK, c_idx - BLOCK <= a_idx)
    prev_ok = jnp.logical_and(c_idx < BLOCK, c_idx >= a_idx)
    lane = lax.broadcasted_iota(jnp.int32, (BLOCK, 128), 1)
    return cur_ok, prev_ok, lane < HEAD_DIM


def _stack_heads(t, head0):
    zero = jnp.zeros_like(t)
    return jnp.concatenate([jnp.where(head0, t, zero), jnp.where(head0, zero, t)], axis=0)


def _unstack_heads(t2, head0):
    return jnp.where(head0, t2[:BLOCK], t2[BLOCK:])


def _attn_fwd(proj, cos_t, sin_t, ride=None):
    def body(q0, q1, q2, k0, k1, k2, v0, v1, v2, cos_ref, sin_ref, attn_ref, lse_ref,
             qs, ks, vs, os_, ms, ls, acc, mnat, lnat):
        cur_ok, prev_ok, head0 = _attn_masks()
        ks[:BLOCK, :] = jnp.zeros((BLOCK, 128), BF16)
        vs[:BLOCK, :] = jnp.zeros((BLOCK, 128), BF16)
        for g, (d, nb) in enumerate(GROUPS):
            q_ref, k_ref, v_ref = (q0, q1, q2)[g], (k0, k1, k2)[g], (v0, v1, v2)[g]
            for r in range(d):
                for i in range(nb):
                    src, dst = _group_rows(d, nb, r, i)
                    below = pl.ds(dst.start + BLOCK, BLOCK)
                    c, s = cos_ref[g, dst, :], sin_ref[g, dst, :]
                    q = q_ref[src, :]
                    k = k_ref[src, :]
                    qs[dst, :] = ((q * c + _swap_halves(q) * s) * 0.125).astype(BF16)
                    ks[below, :] = (k * c + _swap_halves(k) * s).astype(BF16)
                    vs[below, :] = v_ref[src, :].astype(BF16)

            def block(b, carry, nb=nb):
                has_prev = (b & (nb - 1)) > 0
                cur = pl.ds(pl.multiple_of(b * BLOCK, BLOCK), BLOCK)
                window = pl.ds(pl.multiple_of(b * BLOCK, BLOCK), 2 * BLOCK)
                valid = jnp.logical_or(cur_ok, jnp.logical_and(prev_ok, has_prev))
                s = jnp.where(valid, _dot_nt(_stack_heads(qs[cur, :], head0), ks[window, :]), NEG_INF)
                m = jnp.max(s, axis=1, keepdims=True)
                p = jnp.exp(s - m)
                os_[cur, :] = _unstack_heads(_dot(p.astype(BF16), vs[window, :]), head0)
                ms[cur, :] = _unstack_heads(m, head0)
                ls[cur, :] = _unstack_heads(jnp.sum(p, axis=1, keepdims=True), head0)
                return carry

            lax.fori_loop(0, SEQ // BLOCK, block, 0, unroll=16)

            for r in range(d):
                for i in range(nb):
                    src, dst = _group_rows(d, nb, r, i)
                    if g == 0:
                        acc[src, :], mnat[src, :], lnat[src, :] = os_[dst, :], ms[dst, :], ls[dst, :]
                    else:
                        m_old, m_g = mnat[src, :], ms[dst, :]
                        m_new = jnp.maximum(m_old, m_g)
                        a_old, a_g = jnp.exp(m_old - m_new), jnp.exp(m_g - m_new)
                        acc[src, :] = a_old * acc[src, :] + a_g * os_[dst, :]
                        lnat[src, :] = a_old * lnat[src, :] + a_g * ls[dst, :]
                        mnat[src, :] = m_new
        for i in range(SEQ // BLOCK):
            rows = pl.ds(i * BLOCK, BLOCK)
            l = lnat[rows, :]
            attn_ref[rows, :] = acc[rows, :] / l
            lse_ref[rows, :] = mnat[rows, :] + jnp.log(l)

    def col(base):
        return pl.BlockSpec((SEQ, 128), lambda hp, base=base: (0, base + hp))

    in_specs = [col(g * 4) for g in range(3)] + [col(12 + g * 4) for g in range(3)] + [col(24 + g * 4) for g in range(3)]
    table = pl.BlockSpec((3, SEQ, 128), lambda hp: (0, 0, 0), pipeline_mode=pl.Buffered(1))
    out = pl.BlockSpec((SEQ, 128), lambda hp: (0, hp))
    return _call(
        body, "attn_fwd", (4,), in_specs + [table, table], [out, out],
        [_sds((SEQ, ATTN_WIDTH), F32), _sds((SEQ, ATTN_WIDTH), F32)],
        [pltpu.VMEM((SEQ, 128), BF16)] + [pltpu.VMEM((SEQ + BLOCK, 128), BF16)] * 2 + [pltpu.VMEM((SEQ, 128), F32)] * 6,
        [proj] * 9 + [cos_t, sin_t], ride)


def _attn_bwd_group_body(g):
    d, nb = GROUPS[g]

    def body(q_ref, k_ref, v_ref, cos_ref, sin_ref, lse_ref, dattn_ref, dsum_ref, dproj_ref,
             qs, ks, vs, dos, lss, dss, dqs, dks, dvs, stage, outs, sems):
        cur_ok, prev_ok, head0 = _attn_masks()
        ks[:BLOCK, :] = jnp.zeros((BLOCK, 128), BF16)
        vs[:BLOCK, :] = jnp.zeros((BLOCK, 128), BF16)
        dks[:BLOCK, :] = jnp.zeros((BLOCK, 128), F32)
        dvs[:BLOCK, :] = jnp.zeros((BLOCK, 128), F32)
        for r in range(d):
            for i in range(nb):
                src, dst = _group_rows(d, nb, r, i)
                below = pl.ds(dst.start + BLOCK, BLOCK)
                c, s = cos_ref[g, dst, :], sin_ref[g, dst, :]
                q = q_ref[src, :]
                k = k_ref[src, :]
                qs[dst, :] = ((q * c + _swap_halves(q) * s) * 0.125).astype(BF16)
                ks[below, :] = (k * c + _swap_halves(k) * s).astype(BF16)
                vs[below, :] = v_ref[src, :].astype(BF16)
                dos[dst, :] = dattn_ref[src, :].astype(BF16)
                dss[dst, :] = dsum_ref[src, :]
                lss[dst, :] = lse_ref[src, :]
                dks[below, :] = jnp.zeros((BLOCK, 128), F32)
                dvs[below, :] = jnp.zeros((BLOCK, 128), F32)

        def per_head_column(t):
            return jnp.concatenate([jnp.max(jnp.where(head0, t, NEG_INF), axis=1, keepdims=True),
                                    jnp.max(jnp.where(head0, NEG_INF, t), axis=1, keepdims=True)], axis=0)

        def block(b, carry):
            has_prev = (b & (nb - 1)) > 0
            cur = pl.ds(pl.multiple_of(b * BLOCK, BLOCK), BLOCK)
            window = pl.ds(pl.multiple_of(b * BLOCK, BLOCK), 2 * BLOCK)
            valid = jnp.logical_or(cur_ok, jnp.logical_and(prev_ok, has_prev))
            q2, do2 = _stack_heads(qs[cur, :], head0), _stack_heads(dos[cur, :], head0)
            kw, vw = ks[window, :], vs[window, :]
            s = jnp.where(valid, _dot_nt(q2, kw), NEG_INF)
            p = jnp.exp(s - per_head_column(lss[cur, :]))
            ds = (p * (_dot_nt(do2, vw) - per_head_column(dss[cur, :]))).astype(BF16)
            dvs[window, :] += _dot_tn(p.astype(BF16), do2)
            dks[window, :] += _dot_tn(ds, q2)
            dqs[cur, :] = _unstack_heads(_dot(ds, kw), head0)
            return carry

        lax.fori_loop(0, SEQ // BLOCK, block, 0, unroll=8)

        hp = pl.program_id(0)
        copies = []
        for kind in range(3):
            for r in range(d):
                for i in range(nb):
                    src, dst = _group_rows(d, nb, r, i)
                    below = pl.ds(dst.start + BLOCK, BLOCK)
                    if kind == 2:
                        stage[src, :] = dvs[below, :]
                    else:
                        c, s = cos_ref[g, dst, :], sin_ref[g, dst, :]
                        t = dqs[dst, :] * 0.125 if kind == 0 else dks[below, :]
                        stage[src, :] = t * c - _swap_halves(t) * s
            for i in range(SEQ // MM_ROWS):
                rows = pl.ds(i * MM_ROWS, MM_ROWS)
                outs[kind, rows, :] = stage[rows, :].astype(BF16)
            column = pl.multiple_of((kind * 12 + g * 4 + hp) * 128, 128)
            copies.append(pltpu.make_async_copy(outs.at[kind], dproj_ref.at[:, pl.ds(column, 128)], sems.at[kind]))
            copies[-1].start()
        for cp in copies:
            cp.wait()

    return body


def _attn_bwd(proj, cos_t, sin_t, attn, lse, dattn, dproj, ride=None):
    groups = [_attn_bwd_group_body(g) for g in range(3)]

    def body(q0, q1, q2, k0, k1, k2, v0, v1, v2, cos_ref, sin_ref, attn_ref, lse_ref, dattn_ref, dproj_in, dproj_ref,
             dsum, *scratch):
        del dproj_in
        head0 = _attn_masks()[2]
        for i in range(SEQ // BLOCK):
            rows = pl.ds(i * BLOCK, BLOCK)
            prod = dattn_ref[rows, :] * attn_ref[rows, :]
            d0 = jnp.sum(jnp.where(head0, prod, 0.0), axis=1, keepdims=True)
            d1 = jnp.sum(jnp.where(head0, 0.0, prod), axis=1, keepdims=True)
            dsum[rows, :] = jnp.where(head0, d0, d1)
        for g in range(3):
            groups[g]((q0, q1, q2)[g], (k0, k1, k2)[g], (v0, v1, v2)[g], cos_ref, sin_ref, lse_ref, dattn_ref, dsum,
                      dproj_ref, *scratch)

    def col(base):
        return pl.BlockSpec((SEQ, 128), lambda hp, base=base: (0, base + hp))

    table = pl.BlockSpec((3, SEQ, 128), lambda hp: (0, 0, 0), pipeline_mode=pl.Buffered(1))
    return _call(
        body, "attn_bwd", (4,),
        [col(g * 4) for g in range(3)] + [col(12 + g * 4) for g in range(3)] + [col(24 + g * 4) for g in range(3)]
        + [table, table, col(0), col(0), col(0), ANY],
        [ANY], [_sds((SEQ, IN_WIDTH), BF16)],
        [pltpu.VMEM((SEQ, 128), F32)]
        + [pltpu.VMEM((SEQ, 128), BF16)] + [pltpu.VMEM((SEQ + BLOCK, 128), BF16)] * 2 + [pltpu.VMEM((SEQ, 128), BF16)]
        + [pltpu.VMEM((SEQ, 128), F32)] * 3 + [pltpu.VMEM((SEQ + BLOCK, 128), F32)] * 2 + [pltpu.VMEM((SEQ, 128), F32)]
        + [pltpu.VMEM((3, SEQ, 128), BF16), pltpu.SemaphoreType.DMA((3,))],
        [proj] * 9 + [cos_t, sin_t, attn, lse, dattn, dproj], ride, aliases={14: 0})


SSM_CHUNKS = 4
CHUNK_STATES = 512
SCAN_ROWS = 8
U_COL = (3 * QKV_WIDTH) // 128


def _cmul(xr, xi, yr, yi):
    return xr * yr - xi * yi, xr * yi + xi * yr


def _ssm_prep(a_re, a_im, log_dt, b_re_t, b_im_t):
    def body(ar_ref, ai_ref, ldt_ref, br_ref, bi_ref, abr_ref, abi_ref, er_ref, ei_ref, bbr_ref, bbi_ref):
        ar, ai = ar_ref[...], ai_ref[...]
        dt = jnp.exp(ldt_ref[...])
        mag = jnp.exp(ar * dt)
        abr, abi = mag * jnp.cos(ai * dt), mag * jnp.sin(ai * dt)
        den = ar * ar + ai * ai
        nr, ni = abr - 1.0, abi
        er, ei = (nr * ar + ni * ai) / den, (ni * ar - nr * ai) / den
        abr_ref[...], abi_ref[...], er_ref[...], ei_ref[...] = abr, abi, er, ei
        er3, ei3 = er[:, None, :], ei[:, None, :]
        br, bi = br_ref[...], bi_ref[...]
        bbr_ref[...] = er3 * br - ei3 * bi
        bbi_ref[...] = er3 * bi + ei3 * br

    gp = jax.ShapeDtypeStruct(a_re.shape, F32)
    gb = jax.ShapeDtypeStruct(b_re_t.shape, F32)
    return _pallas_call(body, name="ssm_prep", out_shape=(gp, gp, gp, gp, gb, gb))(a_re, a_im, log_dt, b_re_t, b_im_t)


def _ssm_param_bwd(a_re, a_im, log_dt, b_re_t, b_im_t, abar_re, abar_im, e_re, e_im, ga_re, ga_im, gbb_re_t, gbb_im_t):
    def body(ar_ref, ai_ref, ldt_ref, br_ref, bi_ref, abr_ref, abi_ref, er_ref, ei_ref, gar_ref, gai_ref, gbr_ref, gbi_ref,
             o_ar, o_ai, o_ldt, o_br, o_bi):
        ar, ai = ar_ref[...], ai_ref[...]
        dt = jnp.exp(ldt_ref[...])
        er, ei = er_ref[...], ei_ref[...]
        br, bi, gbr, gbi = br_ref[...], bi_ref[...], gbr_ref[...], gbi_ref[...]
        er3, ei3 = er[:, None, :], ei[:, None, :]
        o_br[...] = er3 * gbr + ei3 * gbi
        o_bi[...] = er3 * gbi - ei3 * gbr
        ge_r = jnp.sum(br * gbr + bi * gbi, axis=1)
        ge_i = jnp.sum(br * gbi - bi * gbr, axis=1)
        den = ar * ar + ai * ai
        ilr, ili = ar / den, -ai / den
        t_r, t_i = _cmul(ilr, -ili, ge_r, ge_i)
        gab_r, gab_i = gar_ref[...] + t_r, gai_ref[...] + t_i
        gz_r, gz_i = _cmul(abr_ref[...], -abi_ref[...], gab_r, gab_i)
        el_r, el_i = _cmul(er, ei, ilr, ili)
        u_r, u_i = _cmul(el_r, -el_i, ge_r, ge_i)
        o_ar[...] = dt * gz_r - u_r
        o_ai[...] = dt * gz_i - u_i
        o_ldt[...] = jnp.sum(gz_r * ar + gz_i * ai, axis=1, keepdims=True) * dt

    gp = jax.ShapeDtypeStruct(a_re.shape, F32)
    gb = jax.ShapeDtypeStruct(b_re_t.shape, F32)
    return _pallas_call(body, name="ssm_param_bwd", out_shape=(gp, gp, jax.ShapeDtypeStruct(log_dt.shape, F32), gb, gb))(
        a_re, a_im, log_dt, b_re_t, b_im_t, abar_re, abar_im, e_re, e_im, ga_re, ga_im, gbb_re_t, gbb_im_t)


def _block_diag(blocks_re, blocks_im, sign_im, rows_are_channels):
    both = jnp.stack([blocks_re, sign_im * blocks_im]).reshape(2, SSM_CHUNKS, 8, SSM_GROUP, SSM_STATE)
    eye = jnp.eye(8, dtype=F32)
    if rows_are_channels:
        return jnp.einsum("rcghp,gk->cghrkp", both, eye).reshape(SSM_CHUNKS, 128, 2 * CHUNK_STATES)
    return jnp.einsum("rcghp,gk->crkpgh", both, eye).reshape(SSM_CHUNKS, 2 * CHUNK_STATES, 128)


def _block_diag_parts(mat, rows_are_channels):
    if rows_are_channels:
        six = mat.reshape(SSM_CHUNKS, 8, SSM_GROUP, 2, 8, SSM_STATE)
        parts = jnp.einsum("cghrgp->rcghp", six)
    else:
        six = mat.reshape(SSM_CHUNKS, 2, 8, SSM_STATE, 8, SSM_GROUP)
        parts = jnp.einsum("crgpgh->rcghp", six)
    parts = parts.reshape(2, SSM_GROUPS, SSM_GROUP, SSM_STATE)
    return parts[0], parts[1]


def _scan_consts(a_ref, conj, reverse):
    ar = jnp.broadcast_to(a_ref[:, :CHUNK_STATES], (SCAN_ROWS, CHUNK_STATES))
    ai = jnp.broadcast_to(a_ref[:, CHUNK_STATES:], (SCAN_ROWS, CHUNK_STATES))
    if conj:
        ai = -ai
    row = lax.broadcasted_iota(jnp.int32, (SCAN_ROWS, CHUNK_STATES), 0)
    if reverse:
        row = SCAN_ROWS - 1 - row
    zero = jnp.zeros_like(ar)
    steps = []
    pr, pi = ar, ai
    for shift in (1, 2, 4):
        keep = row >= shift
        steps.append((SCAN_ROWS - shift if reverse else shift, jnp.where(keep, pr, zero), jnp.where(keep, pi, zero)))
        pr, pi = _cmul(pr, pi, pr, pi)
    first = row == 0
    return steps, (jnp.where(first, ar, zero), jnp.where(first, ai, zero)), first


def _scan_tile(xr, xi, prev_r, prev_i, steps, carry_in, reverse):
    edge = SCAN_ROWS - 1 if reverse else 1
    cr, ci = pltpu.roll(prev_r, edge, axis=0), pltpu.roll(prev_i, edge, axis=0)
    xr, xi = xr + carry_in[0] * cr - carry_in[1] * ci, xi + carry_in[0] * ci + carry_in[1] * cr
    for shift, mr, mi in steps:
        sr, si = pltpu.roll(xr, shift, axis=0), pltpu.roll(xi, shift, axis=0)
        xr, xi = xr + mr * sr - mi * si, xi + mr * si + mi * sr
    return xr, xi


MM_ROWS = 256


def _ssm_fwd(proj, bmat, cmat, a_chunks, d_skip, ride=None):
    def body(u_ref, b_ref, c_ref, a_ref, d_ref, y_ref, states_ref, h_ref):
        for i in range(SEQ // MM_ROWS):
            rows = pl.ds(i * MM_ROWS, MM_ROWS)
            h_ref[rows, :] = _dot(u_ref[rows, :].astype(BF16), b_ref[...])
        steps, carry_in, _ = _scan_consts(a_ref, conj=False, reverse=False)

        def tile(k, carry):
            rows = pl.ds(pl.multiple_of(k * SCAN_ROWS, SCAN_ROWS), SCAN_ROWS)
            xr, xi = _scan_tile(h_ref[rows, :CHUNK_STATES], h_ref[rows, CHUNK_STATES:], carry[0], carry[1], steps, carry_in, False)
            h_ref[rows, :CHUNK_STATES] = xr
            h_ref[rows, CHUNK_STATES:] = xi
            return xr, xi

        zero = jnp.zeros((SCAN_ROWS, CHUNK_STATES), F32)
        lax.fori_loop(0, SEQ // SCAN_ROWS, tile, (zero, zero), unroll=4)
        for i in range(SEQ // MM_ROWS):
            rows = pl.ds(i * MM_ROWS, MM_ROWS)
            states = h_ref[rows, :].astype(BF16)
            states_ref[rows, :] = states
            y_ref[rows, :] = _dot(states, c_ref[...]) + d_ref[...] * u_ref[rows, :]

    return _call(
        body, "ssm_fwd", (SSM_CHUNKS,),
        [pl.BlockSpec((SEQ, 128), lambda c: (0, U_COL + c)),
         pl.BlockSpec((None, 128, 2 * CHUNK_STATES), lambda c: (c, 0, 0)),
         pl.BlockSpec((None, 2 * CHUNK_STATES, 128), lambda c: (c, 0, 0)),
         pl.BlockSpec((None, 1, 2 * CHUNK_STATES), lambda c: (c, 0, 0)),
         pl.BlockSpec((1, 128), lambda c: (0, c))],
        [pl.BlockSpec((SEQ, 128), lambda c: (0, c)), pl.BlockSpec((SEQ, 2 * CHUNK_STATES), lambda c: (0, c))],
        [_sds((SEQ, SSM_WIDTH), F32), _sds((SEQ, SSM_CHUNKS * 2 * CHUNK_STATES), BF16)],
        [pltpu.VMEM((SEQ, 2 * CHUNK_STATES), F32)],
        [proj, bmat, cmat, a_chunks, d_skip], ride)


def _ssm_bwd(dys, proj, h, bmat, cmat, a_chunks, d_skip, dproj, ride=None):
    def body(dy_ref, u_ref, states_ref, b_ref, c_ref, a_ref, d_ref, dproj_in, du_ref, db_ref, dc_ref, da_ref, dd_ref,
             g_ref, h_ref):
        del dproj_in
        dsum = jnp.zeros((1, 128), F32)
        dcm = jnp.zeros((2 * CHUNK_STATES, 128), F32)
        for i in range(SEQ // MM_ROWS):
            rows = pl.ds(i * MM_ROWS, MM_ROWS)
            h_ref[rows, :] = states_ref[rows, :].astype(F32)
            dy = dy_ref[rows, :]
            g_ref[rows, :] = _dot_nt(dy.astype(BF16), c_ref[...])
            dsum += jnp.sum(dy * u_ref[rows, :], axis=0, keepdims=True)
            dcm += _dot_tn(states_ref[rows, :], dy.astype(BF16))
        dd_ref[...] = dsum
        dc_ref[...] = dcm
        steps, carry_in, _ = _scan_consts(a_ref, conj=True, reverse=True)
        first_row = lax.broadcasted_iota(jnp.int32, (SCAN_ROWS, CHUNK_STATES), 0) == 0
        n_tiles = SEQ // SCAN_ROWS

        def tile(j, carry):
            k = n_tiles - 1 - j
            rows = pl.ds(pl.multiple_of(k * SCAN_ROWS, SCAN_ROWS), SCAN_ROWS)
            before = pl.ds(pl.multiple_of(jnp.maximum(k - 1, 0) * SCAN_ROWS, SCAN_ROWS), SCAN_ROWS)
            gr, gi = _scan_tile(g_ref[rows, :CHUNK_STATES], g_ref[rows, CHUNK_STATES:], carry[0], carry[1], steps, carry_in, True)
            g_ref[rows, :CHUNK_STATES] = gr
            g_ref[rows, CHUNK_STATES:] = gi
            has_before = jnp.where(k > 0, 1.0, 0.0)
            hr = jnp.where(first_row, pltpu.roll(h_ref[before, :CHUNK_STATES], 1, axis=0) * has_before,
                           pltpu.roll(h_ref[rows, :CHUNK_STATES], 1, axis=0))
            hi = jnp.where(first_row, pltpu.roll(h_ref[before, CHUNK_STATES:], 1, axis=0) * has_before,
                           pltpu.roll(h_ref[rows, CHUNK_STATES:], 1, axis=0))
            return gr, gi, carry[2] + hr * gr + hi * gi, carry[3] + hr * gi - hi * gr

        zero = jnp.zeros((SCAN_ROWS, CHUNK_STATES), F32)
        _, _, sar, sai = lax.fori_loop(0, n_tiles, tile, (zero, zero, zero, zero), unroll=4)
        da_ref[:, :CHUNK_STATES] = jnp.sum(sar, axis=0, keepdims=True)
        da_ref[:, CHUNK_STATES:] = jnp.sum(sai, axis=0, keepdims=True)
        dbm = jnp.zeros((128, 2 * CHUNK_STATES), F32)
        for i in range(SEQ // MM_ROWS):
            rows = pl.ds(i * MM_ROWS, MM_ROWS)
            g = g_ref[rows, :].astype(BF16)
            du_ref[rows, :] = (_dot_nt(g, b_ref[...]) + d_ref[...] * dy_ref[rows, :]).astype(BF16)
            dbm += _dot_tn(u_ref[rows, :].astype(BF16), g)
        db_ref[...] = dbm

    chunk_col = pl.BlockSpec((SEQ, 128), lambda c: (0, c))
    return _call(
        body, "ssm_bwd", (SSM_CHUNKS,),
        [chunk_col,
         pl.BlockSpec((SEQ, 128), lambda c: (0, U_COL + c)),
         pl.BlockSpec((SEQ, 2 * CHUNK_STATES), lambda c: (0, c)),
         pl.BlockSpec((None, 128, 2 * CHUNK_STATES), lambda c: (c, 0, 0)),
         pl.BlockSpec((None, 2 * CHUNK_STATES, 128), lambda c: (c, 0, 0)),
         pl.BlockSpec((None, 1, 2 * CHUNK_STATES), lambda c: (c, 0, 0)),
         pl.BlockSpec((1, 128), lambda c: (0, c)), ANY],
        [pl.BlockSpec((SEQ, 128), lambda c: (0, U_COL + c)),
         pl.BlockSpec((None, 128, 2 * CHUNK_STATES), lambda c: (c, 0, 0)),
         pl.BlockSpec((None, 2 * CHUNK_STATES, 128), lambda c: (c, 0, 0)),
         pl.BlockSpec((None, 1, 2 * CHUNK_STATES), lambda c: (c, 0, 0)),
         pl.BlockSpec((1, 128), lambda c: (0, c))],
        [_sds((SEQ, IN_WIDTH), BF16), _sds((SSM_CHUNKS, 128, 2 * CHUNK_STATES), F32),
         _sds((SSM_CHUNKS, 2 * CHUNK_STATES, 128), F32), _sds((SSM_CHUNKS, 1, 2 * CHUNK_STATES), F32), _sds((1, SSM_WIDTH), F32)],
        [pltpu.VMEM((SEQ, 2 * CHUNK_STATES), F32)] * 2, [dys, proj, h, bmat, cmat, a_chunks, d_skip, dproj], ride, aliases={7: 0})


def _ssm_tables(abar_re, abar_im, bbar_re_t, bbar_im_t, c_re, c_im):
    bmat = _block_diag(bbar_re_t, bbar_im_t, 1.0, True).astype(BF16)
    cmat = _block_diag(c_re, c_im, -1.0, False).astype(BF16)
    a_chunks = jnp.concatenate([abar_re.reshape(SSM_CHUNKS, 1, CHUNK_STATES), abar_im.reshape(SSM_CHUNKS, 1, CHUNK_STATES)], axis=2)
    return bmat, cmat, a_chunks


GL_COL = (3 * QKV_WIDTH + SSM_WIDTH) // D_MODEL
GELU_C = math.sqrt(2.0 / math.pi)
GELU_A = 0.044715


def _sds(shape, dtype):
    return jax.ShapeDtypeStruct(shape, dtype)


def _gelu(x):
    t = jnp.tanh(GELU_C * (x + GELU_A * x * x * x))
    return 0.5 * x * (1.0 + t), t


def _gelu_grad(x, t):
    return 0.5 * (1.0 + t) + 0.5 * x * (1.0 - t * t) * GELU_C * (1.0 + 3.0 * GELU_A * x * x)


def _layer_norm(r, g, b):
    mu = jnp.mean(r, axis=-1, keepdims=True)
    xc = r - mu
    rstd = lax.rsqrt(jnp.mean(xc * xc, axis=-1, keepdims=True) + LN_EPS)
    xhat = xc * rstd
    return xhat * g + b, xhat, rstd


def _layer_norm_bwd(dy, xhat, rstd, g):
    dxhat = dy * g
    m1 = jnp.mean(dxhat, axis=-1, keepdims=True)
    m2 = jnp.mean(dxhat * xhat, axis=-1, keepdims=True)
    return rstd * (dxhat - m1 - xhat * m2)


def _proj(x, w_in, ride=None):
    tm, tn = 1024, 1792

    def body(x_ref, w_ref, o_ref):
        o_ref[...] = _dot(x_ref[...].astype(BF16), _side_by_side(w_ref))

    return _call(
        body, "proj", (SEQ // tm, IN_WIDTH // tn),
        [pl.BlockSpec((tm, D_MODEL), lambda i, j: (i, 0)), pl.BlockSpec((2, D_MODEL, tn // 2), lambda i, j: (j, 0, 0))],
        [pl.BlockSpec((tm, tn), lambda i, j: (i, j))], [_sds((SEQ, IN_WIDTH), F32)], [], [x, w_in], ride)


def _row_spec(tm, width, col=0):
    return pl.BlockSpec((tm, width), lambda i, col=col: (i, col))


def _full_spec(shape):
    return pl.BlockSpec(shape, lambda i: (0,) * len(shape))


def _weight_spec(shape):
    return pl.BlockSpec(shape, lambda i: (0,) * len(shape), pipeline_mode=pl.Buffered(1))


def _mixer_out(attn, ys, proj, x, w_ab, w_sb, w_glu, w_out, b_gate, ln_g, ln_b, ride=None):
    tm = 512

    def body(attn_ref, ys_ref, gl0_ref, gl1_ref, x_ref, wab_ref, wsb_ref, wglu_ref, wout_ref, bg_ref, g_ref, b_ref,
             h_ref, xhat_ref, rstd_ref, glu_ref, ya_ref, yssm_ref):
        gy, _ = _gelu(ys_ref[...])
        glu = _dot(gy.astype(BF16), _side_by_side(wglu_ref))
        glu_ref[...] = glu.astype(BF16)
        y_s = glu[:, :SSM_WIDTH] * jax.nn.sigmoid(glu[:, SSM_WIDTH:])
        y_ssm = _dot(y_s.astype(BF16), _side_by_side(wsb_ref))
        y_attn = _dot(attn_ref[...].astype(BF16), _side_by_side(wab_ref))
        ya_ref[...] = y_attn.astype(BF16)
        yssm_ref[...] = y_ssm.astype(BF16)
        g0 = jax.nn.sigmoid(gl0_ref[...] + _side_by_side(bg_ref, 0))
        g1 = jax.nn.sigmoid(gl1_ref[...] + _side_by_side(bg_ref, 1))
        mixed = g0 * y_attn + g1 * y_ssm
        r1 = DN_ALPHA * x_ref[...] + _dot(mixed.astype(BF16), wout_ref[...])
        h, xhat, rstd = _layer_norm(r1, g_ref[...], b_ref[...])
        h_ref[...] = h
        xhat_ref[...] = xhat
        rstd_ref[...] = jnp.broadcast_to(rstd, (tm, 128))

    wide = _sds((SEQ, D_MODEL), F32)
    return _call(
        body, "mixer_out", (SEQ // tm,),
        [_row_spec(tm, ATTN_WIDTH), _row_spec(tm, SSM_WIDTH), _row_spec(tm, D_MODEL, GL_COL), _row_spec(tm, D_MODEL, GL_COL + 1),
         _row_spec(tm, D_MODEL), _weight_spec((N_DEV, ATTN_WIDTH, 128)), _weight_spec((N_DEV, SSM_WIDTH, 128)),
         _weight_spec((N_DEV, SSM_WIDTH, 128)), _weight_spec((D_MODEL, D_MODEL)), _full_spec((N_DEV, 2, 128)),
         _full_spec((1, D_MODEL)), _full_spec((1, D_MODEL))],
        [_row_spec(tm, D_MODEL), _row_spec(tm, D_MODEL), _row_spec(tm, 128), _row_spec(tm, D_MODEL),
         _row_spec(tm, D_MODEL), _row_spec(tm, D_MODEL)],
        [wide, wide, _sds((SEQ, 128), F32)] + [_sds((SEQ, D_MODEL), BF16)] * 3, [],
        [attn, ys, proj, proj, x, w_ab, w_sb, w_glu, w_out, b_gate, ln_g, ln_b], ride)


def _ff_up(h, w_gate, w_up, ride=None):
    tm, tn = 1024, 768

    def body(h_ref, wg_ref, wu_ref, a_ref, b_ref, f_ref):
        hb = h_ref[...].astype(BF16)
        a, b = _dot(hb, _side_by_side(wg_ref)), _dot(hb, _side_by_side(wu_ref))
        a_ref[...] = a.astype(BF16)
        b_ref[...] = b.astype(BF16)
        f_ref[...] = (a * jax.nn.sigmoid(a) * b).astype(BF16)

    tile = pl.BlockSpec((tm, tn), lambda i, j: (i, j))
    wtile = pl.BlockSpec((tn // FF_PAD, D_MODEL, FF_PAD), lambda i, j: (j, 0, 0))
    out = _sds((SEQ, D_FF_PAD), BF16)
    return _call(body, "ff_up", (SEQ // tm, D_FF_PAD // tn), [pl.BlockSpec((tm, D_MODEL), lambda i, j: (i, 0)), wtile, wtile],
                 [tile, tile, tile], [out, out, out], [], [h, w_gate, w_up], ride)


def _ff_down_loss(f, w_down, h, target, ln_g, ln_b):
    tm = 512

    def body(f_ref, w_ref, h_ref, t_ref, g_ref, b_ref, dr_ref, dg_ref, db_ref, loss_ref):
        @pl.when(pl.program_id(0) == 0)
        def _():
            dg_ref[...] = jnp.zeros_like(dg_ref)
            db_ref[...] = jnp.zeros_like(db_ref)
            loss_ref[...] = jnp.zeros_like(loss_ref)

        r2 = DN_ALPHA * h_ref[...] + _dot(f_ref[...], w_ref[...])
        g = g_ref[...]
        out, xhat, rstd = _layer_norm(r2, g, b_ref[...])
        err = out - t_ref[...]
        loss_ref[...] += 0.5 * jnp.sum(jnp.mean(err * err, axis=-1, keepdims=True), axis=0, keepdims=True)
        dout = err * (1.0 / D_MODEL)
        dg_ref[...] += jnp.sum(dout * xhat, axis=0, keepdims=True)
        db_ref[...] += jnp.sum(dout, axis=0, keepdims=True)
        dr_ref[...] = _layer_norm_bwd(dout, xhat, rstd, g)

    vec = _sds((1, D_MODEL), F32)
    return _pallas_call(
        body, name="ff_down_loss", grid=(SEQ // tm,),
        in_specs=[_row_spec(tm, D_FF_PAD), _weight_spec((D_FF_PAD, D_MODEL)), _row_spec(tm, D_MODEL), _row_spec(tm, D_MODEL),
                  _full_spec((1, D_MODEL)), _full_spec((1, D_MODEL))],
        out_specs=(_row_spec(tm, D_MODEL), _full_spec((1, D_MODEL)), _full_spec((1, D_MODEL)), _full_spec((1, 128))),
        out_shape=(_sds((SEQ, D_MODEL), F32), vec, vec, _sds((1, 128), F32)),
        compiler_params=_cparams(dimension_semantics=("arbitrary",)),
    )(f, w_down, h, target, ln_g, ln_b)


def _ff_down_bwd(dr2, w_down, a, b):
    tm, tn = 1024, 768

    def body(dr_ref, w_ref, a_ref, b_ref, da_ref, db_ref):
        df = _dot_nt(dr_ref[...].astype(BF16), w_ref[...])
        av, bv = a_ref[...].astype(F32), b_ref[...].astype(F32)
        sg = jax.nn.sigmoid(av)
        da_ref[...] = (df * bv * sg * (1.0 + av * (1.0 - sg))).astype(BF16)
        db_ref[...] = (df * av * sg).astype(BF16)

    tile = pl.BlockSpec((tm, tn), lambda i, j: (i, j))
    out = _sds((SEQ, D_FF_PAD), BF16)
    return _pallas_call(
        body, name="ff_down_bwd", grid=(SEQ // tm, D_FF_PAD // tn),
        in_specs=[pl.BlockSpec((tm, D_MODEL), lambda i, j: (i, 0)), pl.BlockSpec((tn, D_MODEL), lambda i, j: (j, 0)), tile, tile],
        out_specs=(tile, tile), out_shape=(out, out),
        compiler_params=_cparams(dimension_semantics=("arbitrary", "arbitrary")),
    )(dr2, w_down, a, b)


def _ff_up_bwd(da, db, w_gate, w_up, dr2, xhat1, rstd1, ln_g, ride=None):
    tm, tk = 1024, 768
    nk = D_FF_PAD // tk

    def body(da_ref, db_ref, wg_ref, wu_ref, dr2_ref, xhat_ref, rstd_ref, g_ref, dr1_ref, dg_ref, dbias_ref, acc):
        i, k = pl.program_id(0), pl.program_id(1)

        @pl.when(jnp.logical_and(i == 0, k == 0))
        def _():
            dg_ref[...] = jnp.zeros_like(dg_ref)
            dbias_ref[...] = jnp.zeros_like(dbias_ref)

        part = _dot_nt(da_ref[...], _side_by_side(wg_ref)) + _dot_nt(db_ref[...], _side_by_side(wu_ref))

        @pl.when(k == 0)
        def _():
            acc[...] = part

        @pl.when(k > 0)
        def _():
            acc[...] += part

        @pl.when(k == nk - 1)
        def _():
            dh = DN_ALPHA * dr2_ref[...] + acc[...]
            xhat = xhat_ref[...]
            dg_ref[...] += jnp.sum(dh * xhat, axis=0, keepdims=True)
            dbias_ref[...] += jnp.sum(dh, axis=0, keepdims=True)
            rstd = jnp.max(rstd_ref[...], axis=1, keepdims=True)
            dr1_ref[...] = _layer_norm_bwd(dh, xhat, rstd, g_ref[...])

    hid = pl.BlockSpec((tm, tk), lambda i, k: (i, k))
    wtile = pl.BlockSpec((tk // FF_PAD, D_MODEL, FF_PAD), lambda i, k: (k, 0, 0))
    row = pl.BlockSpec((tm, D_MODEL), lambda i, k: (i, 0))
    vec = pl.BlockSpec((1, D_MODEL), lambda i, k: (0, 0))
    return _call(
        body, "ff_up_bwd", (SEQ // tm, nk),
        [hid, hid, wtile, wtile, row, row, pl.BlockSpec((tm, 128), lambda i, k: (i, 0)), vec],
        [row, vec, vec], [_sds((SEQ, D_MODEL), F32), _sds((1, D_MODEL), F32), _sds((1, D_MODEL), F32)],
        [pltpu.VMEM((tm, D_MODEL), F32)], [da, db, w_gate, w_up, dr2, xhat1, rstd1, ln_g], ride)


def _mixer_bwd(dr1, proj, y_attn, y_ssm, glu, ys, w_ab, w_sb, w_glu, w_out, b_gate):
    tm = 256

    def body(dr1_ref, gl0_ref, gl1_ref, ya_ref, yssm_ref, glu_ref, ys_ref, wab_ref, wsb_ref, wglu_ref, wout_ref, bg_ref,
             dya_ref, dyssm_ref, dgl_ref, dattn_ref, dglu_ref, dys_ref, mixed_ref, ysb_ref, gy_ref, dbg_ref):
        @pl.when(pl.program_id(0) == 0)
        def _():
            dbg_ref[...] = jnp.zeros_like(dbg_ref)

        dmixed = _dot_nt(dr1_ref[...].astype(BF16), wout_ref[...])
        g0 = jax.nn.sigmoid(gl0_ref[...] + _side_by_side(bg_ref, 0))
        g1 = jax.nn.sigmoid(gl1_ref[...] + _side_by_side(bg_ref, 1))
        y_attn, y_ssm = ya_ref[...].astype(F32), yssm_ref[...].astype(F32)
        mixed_ref[...] = (g0 * y_attn + g1 * y_ssm).astype(BF16)
        dya = (dmixed * g0).astype(BF16)
        dyssm = (dmixed * g1).astype(BF16)
        dya_ref[...] = dya
        dyssm_ref[...] = dyssm
        dgl0 = dmixed * y_attn * g0 * (1.0 - g0)
        dgl1 = dmixed * y_ssm * g1 * (1.0 - g1)
        dgl_ref[:, :GL_COL * D_MODEL] = jnp.zeros((tm, GL_COL * D_MODEL), BF16)
        dgl_ref[:, GL_COL * D_MODEL:(GL_COL + 1) * D_MODEL] = dgl0.astype(BF16)
        dgl_ref[:, (GL_COL + 1) * D_MODEL:] = dgl1.astype(BF16)
        dbg_ref[:, :D_MODEL] += jnp.sum(dgl0, axis=0, keepdims=True)
        dbg_ref[:, D_MODEL:] += jnp.sum(dgl1, axis=0, keepdims=True)
        dattn_ref[...] = _dot_nt(dya, _side_by_side(wab_ref))
        dy_s = _dot_nt(dyssm, _side_by_side(wsb_ref))
        glu = glu_ref[...].astype(F32)
        glu1, sg = glu[:, :SSM_WIDTH], jax.nn.sigmoid(glu[:, SSM_WIDTH:])
        ysb_ref[...] = (glu1 * sg).astype(BF16)
        dglu1 = (dy_s * sg).astype(BF16)
        dglu2 = (dy_s * glu1 * sg * (1.0 - sg)).astype(BF16)
        dglu_ref[:, :SSM_WIDTH] = dglu1
        dglu_ref[:, SSM_WIDTH:] = dglu2
        dgy = _dot_nt(jnp.concatenate([dglu1, dglu2], axis=1), _side_by_side(wglu_ref))
        ys = ys_ref[...]
        gy, t = _gelu(ys)
        gy_ref[...] = gy.astype(BF16)
        dys_ref[...] = dgy * _gelu_grad(ys, t)

    wide_b, half_b = _sds((SEQ, D_MODEL), BF16), _sds((SEQ, SSM_WIDTH), BF16)
    half_f = _sds((SEQ, SSM_WIDTH), F32)
    return _pallas_call(
        body, name="mixer_bwd", grid=(SEQ // tm,),
        in_specs=[_row_spec(tm, D_MODEL), _row_spec(tm, D_MODEL, GL_COL), _row_spec(tm, D_MODEL, GL_COL + 1), _row_spec(tm, D_MODEL),
                  _row_spec(tm, D_MODEL), _row_spec(tm, D_MODEL), _row_spec(tm, SSM_WIDTH), _full_spec((N_DEV, ATTN_WIDTH, 128)),
                  _full_spec((N_DEV, SSM_WIDTH, 128)), _full_spec((N_DEV, SSM_WIDTH, 128)), _full_spec((D_MODEL, D_MODEL)),
                  _full_spec((N_DEV, 2, 128))],
        out_specs=(_row_spec(tm, D_MODEL), _row_spec(tm, D_MODEL), _row_spec(tm, IN_WIDTH), _row_spec(tm, ATTN_WIDTH),
                   _row_spec(tm, D_MODEL), _row_spec(tm, SSM_WIDTH), _row_spec(tm, D_MODEL), _row_spec(tm, SSM_WIDTH),
                   _row_spec(tm, SSM_WIDTH), _full_spec((1, 2 * D_MODEL))),
        out_shape=(wide_b, wide_b, _sds((SEQ, IN_WIDTH), BF16), half_f, wide_b, half_f, wide_b, half_b, half_b,
                   _sds((1, 2 * D_MODEL), F32)),
        compiler_params=_cparams(dimension_semantics=("arbitrary",)),
    )(dr1, proj, proj, y_attn, y_ssm, glu, ys, w_ab, w_sb, w_glu, w_out, b_gate)


def _grad_x(dproj, w_in, dr1, ride=None):
    tm, tk = 1024, 1792
    nk = IN_WIDTH // tk

    def body(dp_ref, w_ref, dr1_ref, o_ref, acc):
        k = pl.program_id(1)
        part = _dot_nt(dp_ref[...], _side_by_side(w_ref))

        @pl.when(k == 0)
        def _():
            acc[...] = part

        @pl.when(k > 0)
        def _():
            acc[...] += part

        @pl.when(k == nk - 1)
        def _():
            o_ref[...] = DN_ALPHA * dr1_ref[...] + acc[...]

    row = pl.BlockSpec((tm, D_MODEL), lambda i, k: (i, 0))
    return _call(
        body, "grad_x", (SEQ // tm, nk),
        [pl.BlockSpec((tm, tk), lambda i, k: (i, k)), pl.BlockSpec((2, D_MODEL, tk // 2), lambda i, k: (k, 0, 0)), row],
        [row], [_sds((SEQ, D_MODEL), F32)], [pltpu.VMEM((tm, D_MODEL), F32)], [dproj, w_in, dr1], ride)


def _weight_grad(a, b, name, shard_cols=None, ride=None):
    k, n = a.shape[1], b.shape[1]
    tk = min(k, 512) if shard_cols else k // N_DEV
    tn = n // 4 if shard_cols else min(n, 1024)

    def body(a_ref, b_ref, o_ref):
        grad = _dot_tn(a_ref[...].astype(BF16), b_ref[...].astype(BF16))
        if shard_cols:
            o_ref[0] = grad[:, :shard_cols].astype(BF16)
            o_ref[1] = grad[:, shard_cols:].astype(BF16)
        else:
            o_ref[...] = grad.astype(BF16)

    if shard_cols:
        out_spec = pl.BlockSpec((2, None, tk, shard_cols), lambda kk, j: (0, j, kk, 0))
        out_shape = _sds((2, 4, k, shard_cols), BF16)
    else:
        out_spec = pl.BlockSpec((None, None, tk, tn), lambda kk, j: (kk % 2, kk // 2, 0, j))
        out_shape = _sds((2, 4, tk, n), BF16)
    out = _call(body, name, (k // tk, n // tn),
                [pl.BlockSpec((SEQ, tk), lambda kk, j: (0, kk)), pl.BlockSpec((SEQ, tn), lambda kk, j: (0, j))],
                [out_spec], [out_shape], [], [a, b], ride)
    return out[0] if ride is None else out


MESH = pl.DeviceIdType.MESH
ANY = pl.BlockSpec(memory_space=pl.ANY)


def _place():
    return lax.axis_index("x"), lax.axis_index("y"), lax.axis_index("c")


def _other_chips(x, y):
    return [(1 - x, y), (x, 1 - y), (1 - x, 1 - y)]


class _Ride:
    def __init__(self, operands, results, aliases, sems, start, wait):
        self.operands, self.results, self.aliases, self.sems = list(operands), list(results), dict(aliases), list(sems)
        self.start, self.wait = start, wait

    def __add__(self, other):
        n_in, n_out, n_sem = len(self.operands), len(self.results), len(self.sems)

        def both(which):
            def run(ins, outs, sems):
                getattr(self, which)(ins[:n_in], outs[:n_out], sems[:n_sem])
                getattr(other, which)(ins[n_in:], outs[n_out:], sems[n_sem:])
            return run

        aliases = {**self.aliases, **{n_in + i: n_out + j for i, j in other.aliases.items()}}
        return _Ride(self.operands + other.operands, self.results + other.results, aliases, self.sems + other.sems,
                     both("start"), both("wait"))


def _call(body, name, grid, in_specs, out_specs, out_shape, scratch_shapes, operands, ride=None, aliases=None):
    in_specs, out_specs, out_shape = list(in_specs), list(out_specs), list(out_shape)
    scratch_shapes, operands, aliases = list(scratch_shapes), list(operands), dict(aliases or {})
    kernel_body = body
    if ride is not None:
        n_in, n_out, n_scr, r_in, r_out = len(in_specs), len(out_specs), len(scratch_shapes), len(ride.operands), len(ride.results)

        def kernel_body(*refs):
            out0, scr0 = n_in + r_in, n_in + r_in + n_out + r_out
            ride_refs = (refs[n_in:out0], refs[out0 + n_out:scr0], refs[scr0 + n_scr:])
            ids = [pl.program_id(i) for i in range(len(grid))]
            first = functools.reduce(jnp.logical_and, [i == 0 for i in ids])
            last = functools.reduce(jnp.logical_and, [i == g - 1 for i, g in zip(ids, grid)])

            @pl.when(first)
            def _():
                ride.start(*ride_refs)

            body(*refs[:n_in], *refs[out0:out0 + n_out], *refs[scr0:scr0 + n_scr])

            @pl.when(last)
            def _():
                ride.wait(*ride_refs)

        aliases.update({n_in + i: n_out + j for i, j in ride.aliases.items()})
        in_specs += [ANY] * r_in
        out_specs += [ANY] * r_out
        out_shape += ride.results
        scratch_shapes += ride.sems
        operands += ride.operands
    return _pallas_call(
        kernel_body, name=name, grid=grid, in_specs=in_specs, out_specs=out_specs, out_shape=out_shape,
        scratch_shapes=scratch_shapes, input_output_aliases=aliases,
        compiler_params=_cparams(dimension_semantics=("arbitrary",) * len(grid)),
    )(*operands)


def _after(*arrays):
    return _Ride(arrays, [], {}, [], lambda *refs: None, lambda *refs: None)


def _gather_first_level(shards):
    n = len(shards)

    def copies(ins, outs, sems, landed):
        send_sems, recv_sems, local_sems = sems
        x, y, c = _place()
        peers = [(x, y, 1 - c)] + [(px, py, c) for px, py in _other_chips(x, y)]

        def row(peer):
            return 4 * x + 2 * y + c if not landed else 4 * peer[0] + 2 * peer[1] + peer[2]

        local = [pltpu.make_async_copy(ins[a], outs[a].at[4 * x + 2 * y + c], local_sems.at[a]) for a in range(n)]
        remote = [pltpu.make_async_remote_copy(
            src_ref=ins[a], dst_ref=outs[a].at[row(peer)], send_sem=send_sems.at[a, k], recv_sem=recv_sems.at[a, k],
            device_id=peer, device_id_type=MESH) for a in range(n) for k, peer in enumerate(peers)]
        return local, remote

    def start(ins, outs, sems):
        local, remote = copies(ins, outs, sems, False)
        for cp in local + remote:
            cp.start()

    def wait(ins, outs, sems):
        local, sent = copies(ins, outs, sems, False)
        for cp in copies(ins, outs, sems, True)[1]:
            cp.wait_recv()
        for cp in sent:
            cp.wait_send()
        for cp in local:
            cp.wait()

    return _Ride(shards, [_sds((N_DEV,) + s.shape, s.dtype) for s in shards], {},
                 [pltpu.SemaphoreType.DMA((n, 4)), pltpu.SemaphoreType.DMA((n, 4)), pltpu.SemaphoreType.DMA((n,))], start, wait)


def _gather_second_level(buffers):
    n = len(buffers)

    def copies(outs, sems, core):
        send_sems, recv_sems = sems
        x, y, c = _place()
        return [pltpu.make_async_remote_copy(
            src_ref=outs[a].at[4 * px + 2 * py + core], dst_ref=outs[a].at[4 * px + 2 * py + core], send_sem=send_sems.at[a, j],
            recv_sem=recv_sems.at[a, j], device_id=(x, y, 1 - c), device_id_type=MESH)
            for a in range(n) for j, (px, py) in enumerate(_other_chips(x, y))]

    def start(ins, outs, sems):
        for cp in copies(outs, sems, lax.axis_index("c")):
            cp.start()

    def wait(ins, outs, sems):
        for cp in copies(outs, sems, 1 - lax.axis_index("c")):
            cp.wait_recv()
        for cp in copies(outs, sems, lax.axis_index("c")):
            cp.wait_send()

    return _Ride(buffers, [_sds(b.shape, b.dtype) for b in buffers], {i: i for i in range(n)},
                 [pltpu.SemaphoreType.DMA((n, 3)), pltpu.SemaphoreType.DMA((n, 3))], start, wait)


def _relayed_gather(shards):
    n = len(shards)
    buffers = [_sds((N_DEV,) + s.shape, s.dtype) for s in shards]
    dma = pltpu.SemaphoreType.DMA

    def remote(src, dst, send_sem, recv_sem, to):
        return pltpu.make_async_remote_copy(src_ref=src, dst_ref=dst, send_sem=send_sem, recv_sem=recv_sem,
                                            device_id=to, device_id_type=MESH)

    def row(px, py, pc):
        return 4 * px + 2 * py + pc

    def ride(operands, aliases, sems, copies):
        def start(ins, outs, sem_refs):
            local, sent = copies(ins, outs, sem_refs, False)
            for cp in local + sent:
                cp.start()

        def wait(ins, outs, sem_refs):
            local, sent = copies(ins, outs, sem_refs, False)
            for cp in copies(ins, outs, sem_refs, True)[1]:
                cp.wait_recv()
            for cp in sent:
                cp.wait_send()
            for cp in local:
                cp.wait()

        return _Ride(operands, buffers, aliases, sems, start, wait)

    def first(ins, outs, sems, landed):
        x, y, c = _place()
        peers = [(x, y, 1 - c), (1 - x, y, c), (x, 1 - y, c)]
        local = [pltpu.make_async_copy(ins[a], outs[a].at[row(x, y, c)], sems[2].at[a]) for a in range(n)]
        return local, [remote(ins[a], outs[a].at[row(*peer) if landed else row(x, y, c)], sems[0].at[a, k], sems[1].at[a, k], peer)
                       for a in range(n) for k, peer in enumerate(peers)]

    def second(ins, outs, sems, landed):
        x, y, c = _place()
        mine = 1 - c if landed else c
        copies = []
        for a in range(n):
            half = shards[a].shape[0] // 2
            over_x, over_y, diagonal = outs[a].at[row(1 - x, y, mine)], outs[a].at[row(x, 1 - y, mine)], outs[a].at[row(1 - x, 1 - y, c)]
            lower, upper = pl.ds(0, half), pl.ds(half, half)
            copies += [remote(over_x, over_x, sems[0].at[a, 0], sems[1].at[a, 0], (x, y, 1 - c)),
                       remote(over_y, over_y, sems[0].at[a, 1], sems[1].at[a, 1], (x, y, 1 - c))]
            if landed:
                copies += [remote(diagonal.at[lower], diagonal.at[lower], sems[0].at[a, 2], sems[1].at[a, 2], (1 - x, y, c)),
                           remote(diagonal.at[upper], diagonal.at[upper], sems[0].at[a, 3], sems[1].at[a, 3], (x, 1 - y, c))]
            else:
                copies += [remote(over_y.at[lower], over_y.at[lower], sems[0].at[a, 2], sems[1].at[a, 2], (1 - x, y, c)),
                           remote(over_x.at[upper], over_x.at[upper], sems[0].at[a, 3], sems[1].at[a, 3], (x, 1 - y, c))]
        return [], copies

    def third(ins, outs, sems, landed):
        x, y, c = _place()
        return [], [remote(outs[a].at[row(1 - x, 1 - y, 1 - c if landed else c)], outs[a].at[row(1 - x, 1 - y, 1 - c if landed else c)],
                           sems[0].at[a], sems[1].at[a], (x, y, 1 - c)) for a in range(n)]

    def later(copies, n_sems):
        return lambda partly: ride(partly, {i: i for i in range(n)}, [dma((n,) + n_sems), dma((n,) + n_sems)], copies)

    return ride(shards, {}, [dma((n, 3)), dma((n, 3)), dma((n,))], first), later(second, (4,)), later(third, ())


def _sibling_swap_ride(grads):
    n = len(grads)

    def copies(ins, outs, sems):
        x, y, c = _place()
        return [pltpu.make_async_remote_copy(
            src_ref=ins[a].at[1 - c], dst_ref=outs[a], send_sem=sems[0].at[a], recv_sem=sems[1].at[a],
            device_id=(x, y, 1 - c), device_id_type=MESH) for a in range(n)]

    def start(ins, outs, sems):
        for cp in copies(ins, outs, sems):
            cp.start()

    def wait(ins, outs, sems):
        for cp in copies(ins, outs, sems):
            cp.wait()

    return _Ride(grads, [_sds(g.shape[1:], g.dtype) for g in grads], {},
                 [pltpu.SemaphoreType.DMA((n,)), pltpu.SemaphoreType.DMA((n,))], start, wait)


def _chip_swap_ride(sums):
    n = len(sums)

    def copies(ins, outs, sems, landed):
        send_sems, recv_sems, local_sems = sems
        x, y, c = _place()
        mine = 2 * x + y
        local = [pltpu.make_async_copy(ins[a].at[mine], outs[a].at[mine], local_sems.at[a]) for a in range(n)]
        remote = [pltpu.make_async_remote_copy(
            src_ref=ins[a].at[2 * px + py], dst_ref=outs[a].at[2 * px + py if landed else mine], send_sem=send_sems.at[a, j],
            recv_sem=recv_sems.at[a, j], device_id=(px, py, c), device_id_type=MESH)
            for a in range(n) for j, (px, py) in enumerate(_other_chips(x, y))]
        return local, remote

    def start(ins, outs, sems):
        local, remote = copies(ins, outs, sems, False)
        for cp in local + remote:
            cp.start()

    def wait(ins, outs, sems):
        local, sent = copies(ins, outs, sems, False)
        for cp in copies(ins, outs, sems, True)[1]:
            cp.wait_recv()
        for cp in sent:
            cp.wait_send()
        for cp in local:
            cp.wait()

    return _Ride(sums, [_sds(s.shape, s.dtype) for s in sums], {},
                 [pltpu.SemaphoreType.DMA((n, 3)), pltpu.SemaphoreType.DMA((n, 3)), pltpu.SemaphoreType.DMA((n,))], start, wait)


def _send_buffers(shards, name):
    n = len(shards)

    def body(*refs):
        for (w, transposed, rows, cols), w_ref, o_ref in zip(shards, refs[:n], refs[n:]):
            if transposed:
                c, r = w.shape
                padded = jnp.concatenate([w_ref[...], jnp.zeros((cols - c, r), F32)], axis=0) if cols > c else w_ref[...]
                o_ref[...] = padded.T.astype(BF16)
            else:
                r, c = w.shape
                if (r, c) != (rows, cols):
                    o_ref[...] = jnp.zeros((rows, cols), BF16)
                o_ref[:r, :c] = w_ref[...].astype(BF16)

    return _pallas_call(body, name=name, out_shape=[_sds((rows, cols), BF16) for _, _, rows, cols in shards])(
        *[w for w, _, _, _ in shards])


def _all_gather(shards, name):
    n = len(shards)
    first, second, third = _relayed_gather(shards)
    levels = [first, second(shards), third(shards)]
    counts = [len(level.sems) for level in levels]

    def body(*refs):
        ins, outs, sems = refs[:n], refs[n:2 * n], refs[2 * n:]
        for i, level in enumerate(levels):
            mine = sems[sum(counts[:i]):sum(counts[:i + 1])]
            level.start(ins, outs, mine)
            level.wait(ins, outs, mine)

    return _pallas_call(
        body, name=name, in_specs=[ANY] * n, out_specs=[ANY] * n, out_shape=first.results,
        scratch_shapes=[s for level in levels for s in level.sems],
    )(*shards)


def _exchange(ride, name):
    n_in, n_out = len(ride.operands), len(ride.results)

    def body(*refs):
        ride.start(refs[:n_in], refs[n_in:n_in + n_out], refs[n_in + n_out:])
        ride.wait(refs[:n_in], refs[n_in:n_in + n_out], refs[n_in + n_out:])

    return _pallas_call(body, name=name, in_specs=[ANY] * n_in, out_specs=[ANY] * n_out, out_shape=ride.results,
                        scratch_shapes=ride.sems, input_output_aliases=ride.aliases)(*ride.operands)


HBM = pl.BlockSpec(memory_space=pltpu.HBM)
SEMAPHORES = pl.BlockSpec(memory_space=pltpu.SEMAPHORE)
IN_FLIGHT = pltpu.CompilerParams(has_side_effects=pltpu.SideEffectType.DATAFLOW_SIDE_EFFECTING)


def _chip_swap_copies(src_refs, land_refs, send_sems, recv_sems, landed):
    x, y, c = _place()
    return [pltpu.make_async_remote_copy(
        src_ref=src.at[2 * px + py], dst_ref=land.at[2 * px + py if landed else 2 * x + y], send_sem=send_sems.at[3 * a + j],
        recv_sem=recv_sems.at[3 * a + j], device_id=(px, py, c), device_id_type=MESH)
        for a, (src, land) in enumerate(zip(src_refs, land_refs)) for j, (px, py) in enumerate(_other_chips(x, y))]


def _chip_swap_start(sums, name):
    n = len(sums)

    def body(*refs):
        src_refs, land_refs, (send_sems, recv_sems), token = refs[:n], refs[n:2 * n], refs[2 * n:2 * n + 2], refs[-1]
        for cp in _chip_swap_copies(src_refs, land_refs, send_sems, recv_sems, False):
            cp.start()
        token[...] = jnp.zeros_like(token)

    kept = [pltpu.HBM(s.shape, s.dtype) for s in sums]
    out = _pallas_call(
        body, name=name,
        out_shape=[pltpu.SemaphoreType.DMA((3 * n,)), pltpu.SemaphoreType.DMA((3 * n,))] + kept + kept + [_sds((8, 128), F32)],
        in_specs=[HBM] * (2 * n), out_specs=[SEMAPHORES, SEMAPHORES] + [HBM] * (2 * n) + [pl.BlockSpec(memory_space=pltpu.VMEM)],
        input_output_aliases={i: 2 + i for i in range(2 * n)}, compiler_params=IN_FLIGHT,
    )(*[pltpu.with_memory_space_constraint(s, pltpu.HBM) for s in sums],
      *[pltpu.with_memory_space_constraint(lax.empty(s.shape, s.dtype), pltpu.HBM) for s in sums])
    return out[0], out[1], out[2:2 + n], out[2 + n:2 + 2 * n], out[-1]


def _chip_swap_wait(send_sems, recv_sems, sums, landings, after, name):
    n = len(sums)

    def body(*refs):
        src_refs, land_refs, (send_sems, recv_sems) = refs[:n], refs[n:2 * n], refs[2 * n:2 * n + 2]
        for cp in _chip_swap_copies(src_refs, land_refs, send_sems, recv_sems, False):
            cp.wait_send()
        for cp in _chip_swap_copies(src_refs, land_refs, send_sems, recv_sems, True):
            cp.wait_recv()

    out = _pallas_call(
        body, name=name, out_shape=[pltpu.HBM(s.shape, s.dtype) for s in list(sums) + list(landings)],
        in_specs=[HBM] * (2 * n) + [SEMAPHORES, SEMAPHORES] + [ANY] * len(after), out_specs=[HBM] * (2 * n),
        input_output_aliases={i: i for i in range(2 * n)}, compiler_params=IN_FLIGHT,
    )(*sums, *landings, send_sems, recv_sems, *after)
    return out[:n], out[n:]


def _pair_sums(gs, rs, core, name):
    n_arrays = len(gs)

    def body(core_ref, *refs):
        for g_ref, r_ref, o_ref in zip(refs[:n_arrays], refs[n_arrays:2 * n_arrays], refs[2 * n_arrays:]):
            o_ref[...] = (g_ref[...].astype(F32) + r_ref[...].astype(F32)).astype(o_ref.dtype)

    def own(g):
        return pl.BlockSpec((None, None) + g.shape[2:], lambda p, core_ref: (core_ref[0], p, 0, 0))

    def chip(g):
        return pl.BlockSpec((None,) + g.shape[2:], lambda p, core_ref: (p, 0, 0))

    return _pallas_call(
        body, name=name,
        grid_spec=pltpu.PrefetchScalarGridSpec(
            num_scalar_prefetch=1, grid=(4,), in_specs=[own(g) for g in gs] + [chip(g) for g in gs],
            out_specs=[chip(g) for g in gs]),
        out_shape=[_sds(g.shape[1:], g.dtype) for g in gs], compiler_params=_cparams(dimension_semantics=("arbitrary",)),
    )(core, *gs, *rs)


def _adamw_math(w, g, m, v):
    m = ADAM_B1 * m + (1.0 - ADAM_B1) * g
    v = ADAM_B2 * v + (1.0 - ADAM_B2) * (g * g)
    m_hat = m / (1.0 - ADAM_B1 ** ADAM_STEP)
    v_hat = v / (1.0 - ADAM_B2 ** ADAM_STEP)
    return -ADAM_LR * (m_hat / (jnp.sqrt(v_hat) + ADAM_EPS) + ADAM_WD * w), m, v


def _adamw_many(weights, name, ride=None):
    steps = 4
    in_specs, out_specs, out_shape, operands, tiles = [], [], [], [], []
    for w, m, v, parts, own, transposed in weights:
        _, pr, pc = parts.shape
        if transposed:
            c, r = w.shape
            tile = pl.BlockSpec((c, r // steps), lambda i: (0, i))
            part_tile = pl.BlockSpec((4, r // steps, pc), lambda i: (0, i, 0))
            tiles.append((c, r // steps))
        elif w.shape[0] % (8 * steps) == 0:
            r, c = w.shape
            tile = pl.BlockSpec((r // steps, c), lambda i: (i, 0))
            part_tile = pl.BlockSpec((4, r // steps, pc), lambda i: (0, i, 0))
            tiles.append((r // steps, c))
        else:
            tile = pl.BlockSpec(w.shape, lambda i: (0, 0))
            part_tile = pl.BlockSpec(parts.shape, lambda i: (0, 0, 0))
            tiles.append(w.shape)
        in_specs += [tile, tile, tile] + [part_tile] * (1 if own is None else 2)
        out_specs += [tile] * 4
        out_shape += [_sds(w.shape, F32)] * 4
        operands += [w, m, v, parts] + ([] if own is None else [own])
    n_in = len(operands)

    def body(*refs):
        ins, outs = list(refs[:n_in]), refs[n_in:]
        this_chip = 2 * lax.axis_index("x") + lax.axis_index("y")
        for k, (_, _, _, _, own, transposed) in enumerate(weights):
            w_ref, m_ref, v_ref, p_ref = ins[:4]
            own_ref = None if own is None else ins[4]
            del ins[:4 if own is None else 5]
            rows, cols = tiles[k]
            g = None
            for q in range(4):
                index = (q,) if transposed else (q, slice(0, rows), slice(0, cols))
                part = p_ref[index] if own is None else jnp.where(this_chip == q, own_ref[index], p_ref[index])
                g = part.astype(F32) if g is None else g + part.astype(F32)
            if transposed:
                g = g.T[:rows]
            g_out, d_out, m_out, v_out = outs[4 * k:4 * k + 4]
            g_out[...] = g
            d_out[...], m_out[...], v_out[...] = _adamw_math(w_ref[...], g, m_ref[...], v_ref[...])

    return _call(body, name, (steps,), in_specs, out_specs, out_shape, [], operands, ride)


SMALL = ("ssm_a_re", "ssm_a_im", "ssm_log_dt", "ssm_b_re", "ssm_b_im", "ssm_c_re", "ssm_c_im", "ssm_d",
         "ln1_g", "ln1_b", "ln2_g", "ln2_b")


def _pack_rows(arrays):
    rows = []
    for a in arrays:
        flat = a.reshape(-1)
        rows.append(jnp.pad(flat, (0, -flat.shape[0] % 128)).reshape(-1, 128))
    packed = jnp.concatenate(rows, axis=0)
    return jnp.pad(packed, ((0, -packed.shape[0] % 8), (0, 0)))


def _unpack_rows(packed, shapes):
    out, row = [], 0
    for shape in shapes:
        size = math.prod(shape)
        n_rows = -(-size // 128)
        out.append(packed[row:row + n_rows].reshape(-1)[:size].reshape(shape))
        row += n_rows
    return out


def _sum_devices(parts):
    def body(p_ref, o_ref):
        total = p_ref[0]
        for dev in range(1, N_DEV):
            total = total + p_ref[dev]
        o_ref[...] = total

    return _pallas_call(body, name="sum_devices", out_shape=_sds(parts.shape[1:], F32))(parts)


def _adamw_replicated(ws, ms, vs, gs):
    n = len(ws)

    def body(*refs):
        w_refs, m_refs, v_refs, g_refs, d_out, m_out, v_out = (refs[i * n:(i + 1) * n] for i in range(7))
        for i in range(n):
            d_out[i][...], m_out[i][...], v_out[i][...] = _adamw_math(w_refs[i][...], g_refs[i][...], m_refs[i][...], v_refs[i][...])

    out = _pallas_call(body, name="adamw_replicated", out_shape=[_sds(w.shape, F32) for w in ws] * 3,
                       compiler_params=_cparams())(*ws, *ms, *vs, *gs)
    return out[:n], out[n:2 * n], out[2 * n:]


def kernel(x, w_in, b_gate, w_attn_br, w_ssm_br, w_out, ssm_a_re, ssm_a_im, ssm_log_dt, ssm_b_re, ssm_b_im, ssm_c_re, ssm_c_im, ssm_d, w_glu, ln1_g, ln1_b, w_ff_gate, w_ff_up, w_ff_down, ln2_g, ln2_b, loss_target, m_w_in, m_b_gate, m_w_attn_br, m_w_ssm_br, m_w_out, m_ssm_a_re, m_ssm_a_im, m_ssm_log_dt, m_ssm_b_re, m_ssm_b_im, m_ssm_c_re, m_ssm_c_im, m_ssm_d, m_w_glu, m_ln1_g, m_ln1_b, m_w_ff_gate, m_w_ff_up, m_w_ff_down, m_ln2_g, m_ln2_b, v_w_in, v_b_gate, v_w_attn_br, v_w_ssm_br, v_w_out, v_ssm_a_re, v_ssm_a_im, v_ssm_log_dt, v_ssm_b_re, v_ssm_b_im, v_ssm_c_re, v_ssm_c_im, v_ssm_d, v_w_glu, v_ln1_g, v_ln1_b, v_w_ff_gate, v_w_ff_up, v_w_ff_down, v_ln2_g, v_ln2_b):
    given = dict(locals())
    x2, target = x[0], loss_target[0]
    core = lax.axis_index("c").astype(jnp.int32).reshape(1)

    sharded = ("w_in", "w_attn_br", "w_ssm_br", "w_glu", "w_ff_gate", "w_ff_up", "b_gate", "w_out", "w_ff_down")
    send_shape = dict(w_in=(D_MODEL, 896), w_attn_br=(ATTN_WIDTH, 128), w_ssm_br=(SSM_WIDTH, 128), w_glu=(SSM_WIDTH, 128),
                      w_out=(128, D_MODEL), w_ff_gate=(D_MODEL, FF_PAD), w_ff_up=(D_MODEL, FF_PAD), w_ff_down=(FF_PAD, D_MODEL))
    local = {k: given[k][0] for k in sharded}
    narrow = ("w_ff_gate", "w_ff_up")
    def to_send(k):
        return (local[k].T, True, *send_shape[k]) if k in narrow else (local[k], False, *send_shape[k])

    later = [k for k in sharded if k not in ("w_in", "b_gate")]
    sends = dict(zip(["w_in"] + later, _send_buffers([to_send("w_in")], "send_w_in")
                     + _send_buffers([to_send(k) for k in later], "send_weights")))
    sends["b_gate"] = local["b_gate"]
    mixer_weights = ("w_attn_br", "w_ssm_br", "w_glu", "b_gate", "w_out")
    ff_weights = ("w_ff_gate", "w_ff_up", "w_ff_down")
    wt = {}
    wt["w_in"], = _all_gather([sends["w_in"]], "gather_w_in")

    a_re, a_im, log_dt = ssm_a_re[0], ssm_a_im[0], ssm_log_dt[0].reshape(SSM_GROUPS, 1)
    b_re_t, b_im_t = ssm_b_re[0].transpose(0, 2, 1), ssm_b_im[0].transpose(0, 2, 1)
    abar_re, abar_im, e_re, e_im, bbar_re_t, bbar_im_t = _ssm_prep(a_re, a_im, log_dt, b_re_t, b_im_t)
    bmat, cmat, a_chunks = _ssm_tables(abar_re, abar_im, bbar_re_t, bbar_im_t, ssm_c_re[0], ssm_c_im[0])
    cos_t, sin_t = _rope_tables()

    big_mixer, ff_in = [k for k in mixer_weights if k != "b_gate"], ("w_ff_gate", "w_ff_up")
    n_mixer = len(big_mixer)
    mixer_1, mixer_2, mixer_3 = _relayed_gather([sends[k] for k in big_mixer])
    ff_in_1, ff_in_2, ff_in_3 = _relayed_gather([sends[k] for k in ff_in])
    ff_down_1, ff_down_2, ff_down_3 = _relayed_gather([sends["w_ff_down"]])
    proj, *landed = _proj(x2, wt["w_in"], mixer_1 + _gather_first_level([sends["b_gate"]]))
    mixer, bias = landed[:n_mixer], landed[n_mixer:]
    attn, lse, *landed = _attn_fwd(proj, cos_t, sin_t, mixer_2(mixer) + _gather_second_level(bias) + ff_in_1)
    mixer, b_gate_full, ff = landed[:n_mixer], landed[n_mixer], landed[n_mixer + 1:]
    ys, states, *landed = _ssm_fwd(proj, bmat, cmat, a_chunks, ssm_d, mixer_3(mixer) + ff_in_2(ff) + ff_down_1)
    wt.update(zip(big_mixer, landed[:n_mixer]))
    ff, ff_down = landed[n_mixer:n_mixer + 2], landed[n_mixer + 2:]
    wt["w_out"] = wt["w_out"].reshape(D_MODEL, D_MODEL)
    h, xhat1, rstd1, glu, y_attn, y_ssm, *landed = _mixer_out(
        attn, ys, proj, x2, wt["w_attn_br"], wt["w_ssm_br"], wt["w_glu"], wt["w_out"], b_gate_full, ln1_g, ln1_b,
        ff_in_3(ff) + ff_down_2(ff_down))
    wt.update(zip(ff_in, landed[:2]))
    ff_a, ff_b, ff_f, w_ff_down = _ff_up(h, wt["w_ff_gate"], wt["w_ff_up"], ff_down_3(landed[2:]))
    wt["w_ff_down"] = w_ff_down.reshape(D_FF_PAD, D_MODEL)
    dr2, d_ln2_g, d_ln2_b, loss_lanes = _ff_down_loss(ff_f, wt["w_ff_down"], h, target, ln2_g, ln2_b)

    def pair_sums(names, contrib, from_sibling):
        return _pair_sums([contrib[k] for k in names], from_sibling, core, "pair_sums_" + names[0])

    d_a, d_b = _ff_down_bwd(dr2, wt["w_ff_down"], ff_a, ff_b)
    contrib = dict(w_ff_gate=_weight_grad(h, d_a, "wgrad_w_ff_gate", FF_PAD),
                   w_ff_up=_weight_grad(h, d_b, "wgrad_w_ff_up", FF_PAD),
                   w_ff_down=_weight_grad(ff_f, dr2, "wgrad_w_ff_down"))
    dr1, d_ln1_g, d_ln1_b, *from_sibling = _ff_up_bwd(
        d_a, d_b, wt["w_ff_gate"], wt["w_ff_up"], dr2, xhat1, rstd1, ln1_g, _sibling_swap_ride([contrib[k] for k in ff_weights]))
    ff_sums = pair_sums(ff_weights, contrib, from_sibling)

    d_ya, d_yssm, d_proj, d_attn, d_glu, d_ys, mixed, y_s, gy, d_bg = _mixer_bwd(
        dr1, proj, y_attn, y_ssm, glu, ys, wt["w_attn_br"], wt["w_ssm_br"], wt["w_glu"], wt["w_out"], b_gate_full)
    contrib.update(w_attn_br=_weight_grad(attn, d_ya, "wgrad_w_attn_br", 128),
                   w_ssm_br=_weight_grad(y_s, d_yssm, "wgrad_w_ssm_br", 128),
                   w_glu=_weight_grad(gy, d_glu, "wgrad_w_glu", 128),
                   w_out=_weight_grad(mixed, dr1, "wgrad_w_out"),
                   b_gate=d_bg.reshape(2, 4, 2, 128).transpose(2, 1, 0, 3))
    d_proj, *landed = _attn_bwd(proj, cos_t, sin_t, attn, lse, d_attn, d_proj,
                                _chip_swap_ride(ff_sums) + _sibling_swap_ride([contrib[k] for k in mixer_weights]))
    parts, own_sums = dict(zip(ff_weights, landed[:len(ff_weights)])), {}
    mixer_sums = pair_sums(mixer_weights, contrib, landed[len(ff_weights):])
    d_proj, d_bmat, d_cmat, d_abar, d_skip, *landed = _ssm_bwd(d_ys, proj, states, bmat, cmat, a_chunks, ssm_d, d_proj,
                                                               _chip_swap_ride(mixer_sums))
    parts.update(zip(mixer_weights, landed))

    gbb_re_t, gbb_im_t = _block_diag_parts(d_bmat, True)
    gc_re, gc_im = _block_diag_parts(d_cmat, False)
    ga_re = d_abar[:, 0, :CHUNK_STATES].reshape(SSM_GROUPS, SSM_STATE)
    ga_im = d_abar[:, 0, CHUNK_STATES:].reshape(SSM_GROUPS, SSM_STATE)
    g_a_re, g_a_im, g_log_dt, g_b_re_t, g_b_im_t = _ssm_param_bwd(
        a_re, a_im, log_dt, b_re_t, b_im_t, abar_re, abar_im, e_re, e_im, ga_re, ga_im, gbb_re_t, gbb_im_t)
    mine = [g_a_re, g_a_im, g_log_dt, g_b_re_t, g_b_im_t, gc_re, -gc_im,
            d_skip, d_ln1_g, d_ln1_b, d_ln2_g, d_ln2_b]
    small_packed = _pack_rows(mine + [loss_lanes])

    contrib["w_in"], small_partly = _weight_grad(x2, d_proj, "wgrad_w_in", 896, _gather_first_level([small_packed]))
    from_sibling, every = _exchange(_sibling_swap_ride([contrib["w_in"]]) + _gather_second_level([small_partly]),
                                    "swap_w_in_with_sibling")
    w_in_sum, = pair_sums(["w_in"], contrib, [from_sibling])
    send_sems, recv_sems, w_in_sum, landing, token = _chip_swap_start([w_in_sum], "w_in_chip_swap_start")

    def adamw_of(k):
        taken = (lambda a: a.T) if k in narrow else (lambda a: a)
        return taken(local[k]), taken(given["m_" + k][0]), taken(given["v_" + k][0]), parts[k], own_sums.get(k), k in narrow

    others = [k for k in sharded if k != "w_in"]
    updated = _adamw_many([adamw_of(k) for k in others], "adamw_others", _after(token))
    grad_x, = _grad_x(d_proj, wt["w_in"], dr1, _after(token))

    def held(k, a):
        return a.transpose(0, 1, 3, 2) if k in ("ssm_b_re", "ssm_b_im") else a

    *small_grads, loss_sum = _unpack_rows(_sum_devices(every), [held(k, given[k]).shape for k in SMALL] + [(1, 128)])
    small = _adamw_replicated([held(k, given[k]) for k in SMALL], [held(k, given["m_" + k]) for k in SMALL],
                              [held(k, given["v_" + k]) for k in SMALL], small_grads)
    loss = loss_sum[0, 0]

    (own_sums["w_in"],), (parts["w_in"],) = _chip_swap_wait(
        send_sems, recv_sems, w_in_sum, landing, [grad_x, updated[0], small[0][0]], "w_in_chip_swap_wait")
    updated += _adamw_many([adamw_of("w_in")], "adamw_w_in")

    grads, deltas, new_m, new_v = {}, {}, {}, {}
    for i, k in enumerate(others + ["w_in"]):
        out = [o.T if k in narrow else o for o in updated[4 * i:4 * i + 4]]
        grads[k], deltas[k], new_m[k], new_v[k] = (o.reshape((1,) + local[k].shape) for o in out)
    for res, values in zip((grads, deltas, new_m, new_v), (small_grads,) + small):
        res.update((k, held(k, a)) for k, a in zip(SMALL, values))

    order = ("w_in", "b_gate", "w_attn_br", "w_ssm_br", "w_out", "ssm_a_re", "ssm_a_im", "ssm_log_dt", "ssm_b_re", "ssm_b_im",
             "ssm_c_re", "ssm_c_im", "ssm_d", "w_glu", "ln1_g", "ln1_b", "w_ff_gate", "w_ff_up", "w_ff_down", "ln2_g", "ln2_b")
    return (loss, grad_x[None], *[grads[k] for k in order], *[deltas[k] for k in order], *[new_m[k] for k in order],
            *[new_v[k] for k in order])
```

```python
import functools
import math

import jax
import jax.numpy as jnp
import numpy as np
from jax import lax
from jax.experimental import pallas as pl
from jax.experimental.pallas import tpu as pltpu

F32 = jnp.float32
BF16 = jnp.bfloat16

N_DEV = 8
SEQ = 2048
D_MODEL = 1024
HEAD_DIM = 64
ATTN_WIDTH = 512
QKV_WIDTH = 1536
SSM_WIDTH = 512
SSM_GROUPS = 32
SSM_GROUP = 16
SSM_STATE = 64
IN_WIDTH = 7168
D_FF = 2816
FF_SHARD = D_FF // N_DEV
FF_PAD = 384
D_FF_PAD = FF_PAD * N_DEV
DN_ALPHA = 2.0 ** 0.25
LN_EPS = 1e-5
NEG_INF = -1e30
ROPE_THETA = 10000.0
BLOCK = 128
GROUPS = ((1, 16), (4, 4), (16, 1))

ADAM_LR = 0.001
ADAM_B1 = 0.9
ADAM_B2 = 0.999
ADAM_EPS = 1e-08
ADAM_WD = 0.01
ADAM_STEP = 10

VMEM_LIMIT = 56 * 1024 * 1024


_pallas_call = pl.pallas_call


def _cparams(**kw):
    return pltpu.CompilerParams(vmem_limit_bytes=VMEM_LIMIT, **kw)


def _dot(a, b):
    return jnp.dot(a, b, preferred_element_type=F32)


def _dot_nt(a, b):
    return lax.dot_general(a, b, (((1,), (1,)), ((), ())), preferred_element_type=F32)


def _side_by_side(w_ref, row=None):
    rows = slice(None) if row is None else pl.ds(row, 1)
    return jnp.concatenate([w_ref[i, rows, :] for i in range(w_ref.shape[0])], axis=1)


def _dot_tn(a, b):
    return lax.dot_general(a, b, (((0,), (0,)), ((), ())), preferred_element_type=F32)


def _rope_tables():
    half = HEAD_DIM // 2
    inv_freq = np.float32(ROPE_THETA) ** (-np.arange(half, dtype=np.float32) / np.float32(half))
    ang = np.arange(SEQ, dtype=np.float32)[:, None] * inv_freq[None, :]
    cos, sin = np.cos(ang).astype(np.float32), np.sin(ang).astype(np.float32)
    tables = np.tile(cos, (1, 4)), np.tile(np.concatenate([-sin, sin], axis=1), (1, 2))

    def by_phase(t):
        return np.stack([t.reshape(SEQ // d, d, 128).transpose(1, 0, 2).reshape(SEQ, 128) for d, _ in GROUPS])

    return jnp.asarray(by_phase(tables[0])), jnp.asarray(by_phase(tables[1]))


def _swap_halves(x):
    lane = lax.broadcasted_iota(jnp.int32, x.shape, 1)
    return jnp.where((lane & 63) < 32, pltpu.roll(x, 96, axis=1), pltpu.roll(x, 32, axis=1))


def _group_rows(d, nb, r, i):
    src = pl.ds(i * BLOCK, BLOCK) if d == 1 else pl.ds(r + i * BLOCK * d, BLOCK, stride=d)
    return src, pl.ds((r * nb + i) * BLOCK, BLOCK)


def _attn_masks():
    a_idx = lax.broadcasted_iota(jnp.int32, (2 * BLOCK, 2 * BLOCK), 0) & (BLOCK - 1)
    c_idx = lax.broadcasted_iota(jnp.int32, (2 * BLOCK, 2 * BLOCK), 1)
    cur_ok = jnp.logical_and(c_idx >= BLOCK, c_idx - BLOCK <= a_idx)
    prev_ok = jnp.logical_and(c_idx < BLOCK, c_idx >= a_idx)
    lane = lax.broadcasted_iota(jnp.int32, (BLOCK, 128), 1)
    return cur_ok, prev_ok, lane < HEAD_DIM


def _stack_heads(t, head0):
    zero = jnp.zeros_like(t)
    return jnp.concatenate([jnp.where(head0, t, zero), jnp.where(head0, zero, t)], axis=0)


def _unstack_heads(t2, head0):
    return jnp.where(head0, t2[:BLOCK], t2[BLOCK:])


def _attn_fwd(proj, cos_t, sin_t, ride=None):
    def body(q0, q1, q2, k0, k1, k2, v0, v1, v2, cos_ref, sin_ref, attn_ref, lse_ref,
             qs, ks, vs, os_, ms, ls, acc, mnat, lnat):
        cur_ok, prev_ok, head0 = _attn_masks()
        ks[:BLOCK, :] = jnp.zeros((BLOCK, 128), BF16)
        vs[:BLOCK, :] = jnp.zeros((BLOCK, 128), BF16)
        for g, (d, nb) in enumerate(GROUPS):
            q_ref, k_ref, v_ref = (q0, q1, q2)[g], (k0, k1, k2)[g], (v0, v1, v2)[g]
            for r in range(d):
                for i in range(nb):
                    src, dst = _group_rows(d, nb, r, i)
                    below = pl.ds(dst.start + BLOCK, BLOCK)
                    c, s = cos_ref[g, dst, :], sin_ref[g, dst, :]
                    q = q_ref[src, :]
                    k = k_ref[src, :]
                    qs[dst, :] = ((q * c + _swap_halves(q) * s) * 0.125).astype(BF16)
                    ks[below, :] = (k * c + _swap_halves(k) * s).astype(BF16)
                    vs[below, :] = v_ref[src, :].astype(BF16)

            def block(b, carry, nb=nb):
                has_prev = (b & (nb - 1)) > 0
                cur = pl.ds(pl.multiple_of(b * BLOCK, BLOCK), BLOCK)
                window = pl.ds(pl.multiple_of(b * BLOCK, BLOCK), 2 * BLOCK)
                valid = jnp.logical_or(cur_ok, jnp.logical_and(prev_ok, has_prev))
                s = jnp.where(valid, _dot_nt(_stack_heads(qs[cur, :], head0), ks[window, :]), NEG_INF)
                m = jnp.max(s, axis=1, keepdims=True)
                p = jnp.exp(s - m)
                os_[cur, :] = _unstack_heads(_dot(p.astype(BF16), vs[window, :]), head0)
                ms[cur, :] = _unstack_heads(m, head0)
                ls[cur, :] = _unstack_heads(jnp.sum(p, axis=1, keepdims=True), head0)
                return carry

            lax.fori_loop(0, SEQ // BLOCK, block, 0, unroll=16)

            for r in range(d):
                for i in range(nb):
                    src, dst = _group_rows(d, nb, r, i)
                    if g == 0:
                        acc[src, :], mnat[src, :], lnat[src, :] = os_[dst, :], ms[dst, :], ls[dst, :]
                    else:
                        m_old, m_g = mnat[src, :], ms[dst, :]
                        m_new = jnp.maximum(m_old, m_g)
                        a_old, a_g = jnp.exp(m_old - m_new), jnp.exp(m_g - m_new)
                        acc[src, :] = a_old * acc[src, :] + a_g * os_[dst, :]
                        lnat[src, :] = a_old * lnat[src, :] + a_g * ls[dst, :]
                        mnat[src, :] = m_new
        for i in range(SEQ // BLOCK):
            rows = pl.ds(i * BLOCK, BLOCK)
            l = lnat[rows, :]
            attn_ref[rows, :] = acc[rows, :] / l
            lse_ref[rows, :] = mnat[rows, :] + jnp.log(l)

    def col(base):
        return pl.BlockSpec((SEQ, 128), lambda hp, base=base: (0, base + hp))

    in_specs = [col(g * 4) for g in range(3)] + [col(12 + g * 4) for g in range(3)] + [col(24 + g * 4) for g in range(3)]
    table = pl.BlockSpec((3, SEQ, 128), lambda hp: (0, 0, 0), pipeline_mode=pl.Buffered(1))
    out = pl.BlockSpec((SEQ, 128), lambda hp: (0, hp))
    return _call(
        body, "attn_fwd", (4,), in_specs + [table, table], [out, out],
        [_sds((SEQ, ATTN_WIDTH), F32), _sds((SEQ, ATTN_WIDTH), F32)],
        [pltpu.VMEM((SEQ, 128), BF16)] + [pltpu.VMEM((SEQ + BLOCK, 128), BF16)] * 2 + [pltpu.VMEM((SEQ, 128), F32)] * 6,
        [proj] * 9 + [cos_t, sin_t], ride)


def _attn_bwd_group_body(g):
    d, nb = GROUPS[g]

    def body(q_ref, k_ref, v_ref, cos_ref, sin_ref, lse_ref, dattn_ref, dsum_ref, dproj_ref,
             qs, ks, vs, dos, lss, dss, dqs, dks, dvs, stage, outs, sems):
        cur_ok, prev_ok, head0 = _attn_masks()
        ks[:BLOCK, :] = jnp.zeros((BLOCK, 128), BF16)
        vs[:BLOCK, :] = jnp.zeros((BLOCK, 128), BF16)
        dks[:BLOCK, :] = jnp.zeros((BLOCK, 128), F32)
        dvs[:BLOCK, :] = jnp.zeros((BLOCK, 128), F32)
        for r in range(d):
            for i in range(nb):
                src, dst = _group_rows(d, nb, r, i)
                below = pl.ds(dst.start + BLOCK, BLOCK)
                c, s = cos_ref[g, dst, :], sin_ref[g, dst, :]
                q = q_ref[src, :]
                k = k_ref[src, :]
                qs[dst, :] = ((q * c + _swap_halves(q) * s) * 0.125).astype(BF16)
                ks[below, :] = (k * c + _swap_halves(k) * s).astype(BF16)
                vs[below, :] = v_ref[src, :].astype(BF16)
                dos[dst, :] = dattn_ref[src, :].astype(BF16)
                dss[dst, :] = dsum_ref[src, :]
                lss[dst, :] = lse_ref[src, :]
                dks[below, :] = jnp.zeros((BLOCK, 128), F32)
                dvs[below, :] = jnp.zeros((BLOCK, 128), F32)

        def per_head_column(t):
            return jnp.concatenate([jnp.max(jnp.where(head0, t, NEG_INF), axis=1, keepdims=True),
                                    jnp.max(jnp.where(head0, NEG_INF, t), axis=1, keepdims=True)], axis=0)

        def block(b, carry):
            has_prev = (b & (nb - 1)) > 0
            cur = pl.ds(pl.multiple_of(b * BLOCK, BLOCK), BLOCK)
            window = pl.ds(pl.multiple_of(b * BLOCK, BLOCK), 2 * BLOCK)
            valid = jnp.logical_or(cur_ok, jnp.logical_and(prev_ok, has_prev))
            q2, do2 = _stack_heads(qs[cur, :], head0), _stack_heads(dos[cur, :], head0)
            kw, vw = ks[window, :], vs[window, :]
            s = jnp.where(valid, _dot_nt(q2, kw), NEG_INF)
            p = jnp.exp(s - per_head_column(lss[cur, :]))
            ds = (p * (_dot_nt(do2, vw) - per_head_column(dss[cur, :]))).astype(BF16)
            dvs[window, :] += _dot_tn(p.astype(BF16), do2)
            dks[window, :] += _dot_tn(ds, q2)
            dqs[cur, :] = _unstack_heads(_dot(ds, kw), head0)
            return carry

        lax.fori_loop(0, SEQ // BLOCK, block, 0, unroll=8)

        hp = pl.program_id(0)
        copies = []
        for kind in range(3):
            for r in range(d):
                for i in range(nb):
                    src, dst = _group_rows(d, nb, r, i)
                    below = pl.ds(dst.start + BLOCK, BLOCK)
                    if kind == 2:
                        stage[src, :] = dvs[below, :]
                    else:
                        c, s = cos_ref[g, dst, :], sin_ref[g, dst, :]
                        t = dqs[dst, :] * 0.125 if kind == 0 else dks[below, :]
                        stage[src, :] = t * c - _swap_halves(t) * s
            for i in range(SEQ // MM_ROWS):
                rows = pl.ds(i * MM_ROWS, MM_ROWS)
                outs[kind, rows, :] = stage[rows, :].astype(BF16)
            column = pl.multiple_of((kind * 12 + g * 4 + hp) * 128, 128)
            copies.append(pltpu.make_async_copy(outs.at[kind], dproj_ref.at[:, pl.ds(column, 128)], sems.at[kind]))
            copies[-1].start()
        for cp in copies:
            cp.wait()

    return body


def _attn_bwd(proj, cos_t, sin_t, attn, lse, dattn, dproj, ride=None):
    groups = [_attn_bwd_group_body(g) for g in range(3)]

    def body(q0, q1, q2, k0, k1, k2, v0, v1, v2, cos_ref, sin_ref, attn_ref, lse_ref, dattn_ref, dproj_in, dproj_ref,
             dsum, *scratch):
        del dproj_in
        head0 = _attn_masks()[2]
        for i in range(SEQ // BLOCK):
            rows = pl.ds(i * BLOCK, BLOCK)
            prod = dattn_ref[rows, :] * attn_ref[rows, :]
            d0 = jnp.sum(jnp.where(head0, prod, 0.0), axis=1, keepdims=True)
            d1 = jnp.sum(jnp.where(head0, 0.0, prod), axis=1, keepdims=True)
            dsum[rows, :] = jnp.where(head0, d0, d1)
        for g in range(3):
            groups[g]((q0, q1, q2)[g], (k0, k1, k2)[g], (v0, v1, v2)[g], cos_ref, sin_ref, lse_ref, dattn_ref, dsum,
                      dproj_ref, *scratch)

    def col(base):
        return pl.BlockSpec((SEQ, 128), lambda hp, base=base: (0, base + hp))

    table = pl.BlockSpec((3, SEQ, 128), lambda hp: (0, 0, 0), pipeline_mode=pl.Buffered(1))
    return _call(
        body, "attn_bwd", (4,),
        [col(g * 4) for g in range(3)] + [col(12 + g * 4) for g in range(3)] + [col(24 + g * 4) for g in range(3)]
        + [table, table, col(0), col(0), col(0), ANY],
        [ANY], [_sds((SEQ, IN_WIDTH), BF16)],
        [pltpu.VMEM((SEQ, 128), F32)]
        + [pltpu.VMEM((SEQ, 128), BF16)] + [pltpu.VMEM((SEQ + BLOCK, 128), BF16)] * 2 + [pltpu.VMEM((SEQ, 128), BF16)]
        + [pltpu.VMEM((SEQ, 128), F32)] * 3 + [pltpu.VMEM((SEQ + BLOCK, 128), F32)] * 2 + [pltpu.VMEM((SEQ, 128), F32)]
        + [pltpu.VMEM((3, SEQ, 128), BF16), pltpu.SemaphoreType.DMA((3,))],
        [proj] * 9 + [cos_t, sin_t, attn, lse, dattn, dproj], ride, aliases={14: 0})


SSM_CHUNKS = 4
CHUNK_STATES = 512
SCAN_ROWS = 8
U_COL = (3 * QKV_WIDTH) // 128


def _cmul(xr, xi, yr, yi):
    return xr * yr - xi * yi, xr * yi + xi * yr


def _ssm_prep(a_re, a_im, log_dt, b_re_t, b_im_t):
    def body(ar_ref, ai_ref, ldt_ref, br_ref, bi_ref, abr_ref, abi_ref, er_ref, ei_ref, bbr_ref, bbi_ref):
        ar, ai = ar_ref[...], ai_ref[...]
        dt = jnp.exp(ldt_ref[...])
        mag = jnp.exp(ar * dt)
        abr, abi = mag * jnp.cos(ai * dt), mag * jnp.sin(ai * dt)
        den = ar * ar + ai * ai
        nr, ni = abr - 1.0, abi
        er, ei = (nr * ar + ni * ai) / den, (ni * ar - nr * ai) / den
        abr_ref[...], abi_ref[...], er_ref[...], ei_ref[...] = abr, abi, er, ei
        er3, ei3 = er[:, None, :], ei[:, None, :]
        br, bi = br_ref[...], bi_ref[...]
        bbr_ref[...] = er3 * br - ei3 * bi
        bbi_ref[...] = er3 * bi + ei3 * br

    gp = jax.ShapeDtypeStruct(a_re.shape, F32)
    gb = jax.ShapeDtypeStruct(b_re_t.shape, F32)
    return _pallas_call(body, name="ssm_prep", out_shape=(gp, gp, gp, gp, gb, gb))(a_re, a_im, log_dt, b_re_t, b_im_t)


def _ssm_param_bwd(a_re, a_im, log_dt, b_re_t, b_im_t, abar_re, abar_im, e_re, e_im, ga_re, ga_im, gbb_re_t, gbb_im_t):
    def body(ar_ref, ai_ref, ldt_ref, br_ref, bi_ref, abr_ref, abi_ref, er_ref, ei_ref, gar_ref, gai_ref, gbr_ref, gbi_ref,
             o_ar, o_ai, o_ldt, o_br, o_bi):
        ar, ai = ar_ref[...], ai_ref[...]
        dt = jnp.exp(ldt_ref[...])
        er, ei = er_ref[...], ei_ref[...]
        br, bi, gbr, gbi = br_ref[...], bi_ref[...], gbr_ref[...], gbi_ref[...]
        er3, ei3 = er[:, None, :], ei[:, None, :]
        o_br[...] = er3 * gbr + ei3 * gbi
        o_bi[...] = er3 * gbi - ei3 * gbr
        ge_r = jnp.sum(br * gbr + bi * gbi, axis=1)
        ge_i = jnp.sum(br * gbi - bi * gbr, axis=1)
        den = ar * ar + ai * ai
        ilr, ili = ar / den, -ai / den
        t_r, t_i = _cmul(ilr, -ili, ge_r, ge_i)
        gab_r, gab_i = gar_ref[...] + t_r, gai_ref[...] + t_i
        gz_r, gz_i = _cmul(abr_ref[...], -abi_ref[...], gab_r, gab_i)
        el_r, el_i = _cmul(er, ei, ilr, ili)
        u_r, u_i = _cmul(el_r, -el_i, ge_r, ge_i)
        o_ar[...] = dt * gz_r - u_r
        o_ai[...] = dt * gz_i - u_i
        o_ldt[...] = jnp.sum(gz_r * ar + gz_i * ai, axis=1, keepdims=True) * dt

    gp = jax.ShapeDtypeStruct(a_re.shape, F32)
    gb = jax.ShapeDtypeStruct(b_re_t.shape, F32)
    return _pallas_call(body, name="ssm_param_bwd", out_shape=(gp, gp, jax.ShapeDtypeStruct(log_dt.shape, F32), gb, gb))(
        a_re, a_im, log_dt, b_re_t, b_im_t, abar_re, abar_im, e_re, e_im, ga_re, ga_im, gbb_re_t, gbb_im_t)


def _block_diag(blocks_re, blocks_im, sign_im, rows_are_channels):
    both = jnp.stack([blocks_re, sign_im * blocks_im]).reshape(2, SSM_CHUNKS, 8, SSM_GROUP, SSM_STATE)
    eye = jnp.eye(8, dtype=F32)
    if rows_are_channels:
        return jnp.einsum("rcghp,gk->cghrkp", both, eye).reshape(SSM_CHUNKS, 128, 2 * CHUNK_STATES)
    return jnp.einsum("rcghp,gk->crkpgh", both, eye).reshape(SSM_CHUNKS, 2 * CHUNK_STATES, 128)


def _block_diag_parts(mat, rows_are_channels):
    if rows_are_channels:
        six = mat.reshape(SSM_CHUNKS, 8, SSM_GROUP, 2, 8, SSM_STATE)
        parts = jnp.einsum("cghrgp->rcghp", six)
    else:
        six = mat.reshape(SSM_CHUNKS, 2, 8, SSM_STATE, 8, SSM_GROUP)
        parts = jnp.einsum("crgpgh->rcghp", six)
    parts = parts.reshape(2, SSM_GROUPS, SSM_GROUP, SSM_STATE)
    return parts[0], parts[1]


def _scan_consts(a_ref, conj, reverse):
    ar = jnp.broadcast_to(a_ref[:, :CHUNK_STATES], (SCAN_ROWS, CHUNK_STATES))
    ai = jnp.broadcast_to(a_ref[:, CHUNK_STATES:], (SCAN_ROWS, CHUNK_STATES))
    if conj:
        ai = -ai
    row = lax.broadcasted_iota(jnp.int32, (SCAN_ROWS, CHUNK_STATES), 0)
    if reverse:
        row = SCAN_ROWS - 1 - row
    zero = jnp.zeros_like(ar)
    steps = []
    pr, pi = ar, ai
    for shift in (1, 2, 4):
        keep = row >= shift
        steps.append((SCAN_ROWS - shift if reverse else shift, jnp.where(keep, pr, zero), jnp.where(keep, pi, zero)))
        pr, pi = _cmul(pr, pi, pr, pi)
    first = row == 0
    return steps, (jnp.where(first, ar, zero), jnp.where(first, ai, zero)), first


def _scan_tile(xr, xi, prev_r, prev_i, steps, carry_in, reverse):
    edge = SCAN_ROWS - 1 if reverse else 1
    cr, ci = pltpu.roll(prev_r, edge, axis=0), pltpu.roll(prev_i, edge, axis=0)
    xr, xi = xr + carry_in[0] * cr - carry_in[1] * ci, xi + carry_in[0] * ci + carry_in[1] * cr
    for shift, mr, mi in steps:
        sr, si = pltpu.roll(xr, shift, axis=0), pltpu.roll(xi, shift, axis=0)
        xr, xi = xr + mr * sr - mi * si, xi + mr * si + mi * sr
    return xr, xi


MM_ROWS = 256


def _ssm_fwd(proj, bmat, cmat, a_chunks, d_skip, ride=None):
    def body(u_ref, b_ref, c_ref, a_ref, d_ref, y_ref, states_ref, h_ref):
        for i in range(SEQ // MM_ROWS):
            rows = pl.ds(i * MM_ROWS, MM_ROWS)
            h_ref[rows, :] = _dot(u_ref[rows, :].astype(BF16), b_ref[...])
        steps, carry_in, _ = _scan_consts(a_ref, conj=False, reverse=False)

        def tile(k, carry):
            rows = pl.ds(pl.multiple_of(k * SCAN_ROWS, SCAN_ROWS), SCAN_ROWS)
            xr, xi = _scan_tile(h_ref[rows, :CHUNK_STATES], h_ref[rows, CHUNK_STATES:], carry[0], carry[1], steps, carry_in, False)
            h_ref[rows, :CHUNK_STATES] = xr
            h_ref[rows, CHUNK_STATES:] = xi
            return xr, xi

        zero = jnp.zeros((SCAN_ROWS, CHUNK_STATES), F32)
        lax.fori_loop(0, SEQ // SCAN_ROWS, tile, (zero, zero), unroll=4)
        for i in range(SEQ // MM_ROWS):
            rows = pl.ds(i * MM_ROWS, MM_ROWS)
            states = h_ref[rows, :].astype(BF16)
            states_ref[rows, :] = states
            y_ref[rows, :] = _dot(states, c_ref[...]) + d_ref[...] * u_ref[rows, :]

    return _call(
        body, "ssm_fwd", (SSM_CHUNKS,),
        [pl.BlockSpec((SEQ, 128), lambda c: (0, U_COL + c)),
         pl.BlockSpec((None, 128, 2 * CHUNK_STATES), lambda c: (c, 0, 0)),
         pl.BlockSpec((None, 2 * CHUNK_STATES, 128), lambda c: (c, 0, 0)),
         pl.BlockSpec((None, 1, 2 * CHUNK_STATES), lambda c: (c, 0, 0)),
         pl.BlockSpec((1, 128), lambda c: (0, c))],
        [pl.BlockSpec((SEQ, 128), lambda c: (0, c)), pl.BlockSpec((SEQ, 2 * CHUNK_STATES), lambda c: (0, c))],
        [_sds((SEQ, SSM_WIDTH), F32), _sds((SEQ, SSM_CHUNKS * 2 * CHUNK_STATES), BF16)],
        [pltpu.VMEM((SEQ, 2 * CHUNK_STATES), F32)],
        [proj, bmat, cmat, a_chunks, d_skip], ride)


def _ssm_bwd(dys, proj, h, bmat, cmat, a_chunks, d_skip, dproj, ride=None):
    def body(dy_ref, u_ref, states_ref, b_ref, c_ref, a_ref, d_ref, dproj_in, du_ref, db_ref, dc_ref, da_ref, dd_ref,
             g_ref, h_ref):
        del dproj_in
        dsum = jnp.zeros((1, 128), F32)
        dcm = jnp.zeros((2 * CHUNK_STATES, 128), F32)
        for i in range(SEQ // MM_ROWS):
            rows = pl.ds(i * MM_ROWS, MM_ROWS)
            h_ref[rows, :] = states_ref[rows, :].astype(F32)
            dy = dy_ref[rows, :]
            g_ref[rows, :] = _dot_nt(dy.astype(BF16), c_ref[...])
            dsum += jnp.sum(dy * u_ref[rows, :], axis=0, keepdims=True)
            dcm += _dot_tn(states_ref[rows, :], dy.astype(BF16))
        dd_ref[...] = dsum
        dc_ref[...] = dcm
        steps, carry_in, _ = _scan_consts(a_ref, conj=True, reverse=True)
        first_row = lax.broadcasted_iota(jnp.int32, (SCAN_ROWS, CHUNK_STATES), 0) == 0
        n_tiles = SEQ // SCAN_ROWS

        def tile(j, carry):
            k = n_tiles - 1 - j
            rows = pl.ds(pl.multiple_of(k * SCAN_ROWS, SCAN_ROWS), SCAN_ROWS)
            before = pl.ds(pl.multiple_of(jnp.maximum(k - 1, 0) * SCAN_ROWS, SCAN_ROWS), SCAN_ROWS)
            gr, gi = _scan_tile(g_ref[rows, :CHUNK_STATES], g_ref[rows, CHUNK_STATES:], carry[0], carry[1], steps, carry_in, True)
            g_ref[rows, :CHUNK_STATES] = gr
            g_ref[rows, CHUNK_STATES:] = gi
            has_before = jnp.where(k > 0, 1.0, 0.0)
            hr = jnp.where(first_row, pltpu.roll(h_ref[before, :CHUNK_STATES], 1, axis=0) * has_before,
                           pltpu.roll(h_ref[rows, :CHUNK_STATES], 1, axis=0))
            hi = jnp.where(first_row, pltpu.roll(h_ref[before, CHUNK_STATES:], 1, axis=0) * has_before,
                           pltpu.roll(h_ref[rows, CHUNK_STATES:], 1, axis=0))
            return gr, gi, carry[2] + hr * gr + hi * gi, carry[3] + hr * gi - hi * gr

        zero = jnp.zeros((SCAN_ROWS, CHUNK_STATES), F32)
        _, _, sar, sai = lax.fori_loop(0, n_tiles, tile, (zero, zero, zero, zero), unroll=4)
        da_ref[:, :CHUNK_STATES] = jnp.sum(sar, axis=0, keepdims=True)
        da_ref[:, CHUNK_STATES:] = jnp.sum(sai, axis=0, keepdims=True)
        dbm = jnp.zeros((128, 2 * CHUNK_STATES), F32)
        for i in range(SEQ // MM_ROWS):
            rows = pl.ds(i * MM_ROWS, MM_ROWS)
            g = g_ref[rows, :].astype(BF16)
            du_ref[rows, :] = (_dot_nt(g, b_ref[...]) + d_ref[...] * dy_ref[rows, :]).astype(BF16)
            dbm += _dot_tn(u_ref[rows, :].astype(BF16), g)
        db_ref[...] = dbm

    chunk_col = pl.BlockSpec((SEQ, 128), lambda c: (0, c))
    return _call(
        body, "ssm_bwd", (SSM_CHUNKS,),
        [chunk_col,
         pl.BlockSpec((SEQ, 128), lambda c: (0, U_COL + c)),
         pl.BlockSpec((SEQ, 2 * CHUNK_STATES), lambda c: (0, c)),
         pl.BlockSpec((None, 128, 2 * CHUNK_STATES), lambda c: (c, 0, 0)),
         pl.BlockSpec((None, 2 * CHUNK_STATES, 128), lambda c: (c, 0, 0)),
         pl.BlockSpec((None, 1, 2 * CHUNK_STATES), lambda c: (c, 0, 0)),
         pl.BlockSpec((1, 128), lambda c: (0, c)), ANY],
        [pl.BlockSpec((SEQ, 128), lambda c: (0, U_COL + c)),
         pl.BlockSpec((None, 128, 2 * CHUNK_STATES), lambda c: (c, 0, 0)),
         pl.BlockSpec((None, 2 * CHUNK_STATES, 128), lambda c: (c, 0, 0)),
         pl.BlockSpec((None, 1, 2 * CHUNK_STATES), lambda c: (c, 0, 0)),
         pl.BlockSpec((1, 128), lambda c: (0, c))],
        [_sds((SEQ, IN_WIDTH), BF16), _sds((SSM_CHUNKS, 128, 2 * CHUNK_STATES), F32),
         _sds((SSM_CHUNKS, 2 * CHUNK_STATES, 128), F32), _sds((SSM_CHUNKS, 1, 2 * CHUNK_STATES), F32), _sds((1, SSM_WIDTH), F32)],
        [pltpu.VMEM((SEQ, 2 * CHUNK_STATES), F32)] * 2, [dys, proj, h, bmat, cmat, a_chunks, d_skip, dproj], ride, aliases={7: 0})


def _ssm_tables(abar_re, abar_im, bbar_re_t, bbar_im_t, c_re, c_im):
    bmat = _block_diag(bbar_re_t, bbar_im_t, 1.0, True).astype(BF16)
    cmat = _block_diag(c_re, c_im, -1.0, False).astype(BF16)
    a_chunks = jnp.concatenate([abar_re.reshape(SSM_CHUNKS, 1, CHUNK_STATES), abar_im.reshape(SSM_CHUNKS, 1, CHUNK_STATES)], axis=2)
    return bmat, cmat, a_chunks


GL_COL = (3 * QKV_WIDTH + SSM_WIDTH) // D_MODEL
GELU_C = math.sqrt(2.0 / math.pi)
GELU_A = 0.044715


def _sds(shape, dtype):
    return jax.ShapeDtypeStruct(shape, dtype)


def _gelu(x):
    t = jnp.tanh(GELU_C * (x + GELU_A * x * x * x))
    return 0.5 * x * (1.0 + t), t


def _gelu_grad(x, t):
    return 0.5 * (1.0 + t) + 0.5 * x * (1.0 - t * t) * GELU_C * (1.0 + 3.0 * GELU_A * x * x)


def _layer_norm(r, g, b):
    mu = jnp.mean(r, axis=-1, keepdims=True)
    xc = r - mu
    rstd = lax.rsqrt(jnp.mean(xc * xc, axis=-1, keepdims=True) + LN_EPS)
    xhat = xc * rstd
    return xhat * g + b, xhat, rstd


def _layer_norm_bwd(dy, xhat, rstd, g):
    dxhat = dy * g
    m1 = jnp.mean(dxhat, axis=-1, keepdims=True)
    m2 = jnp.mean(dxhat * xhat, axis=-1, keepdims=True)
    return rstd * (dxhat - m1 - xhat * m2)


def _proj(x, w_in, shards, name, into=None, ride=None):
    tm, tn = 1024, IN_WIDTH // N_DEV

    def body(shards_ref, x_ref, w_ref, *rest):
        rest[-1][...] = _dot(x_ref[...].astype(BF16), w_ref[...])

    return _call(
        body, name, (SEQ // tm, shards.shape[0]),
        [pl.BlockSpec((tm, D_MODEL), lambda i, j, shards_ref: (i, 0)),
         pl.BlockSpec((None, D_MODEL, tn), lambda i, j, shards_ref: (shards_ref[j], 0, 0))] + ([] if into is None else [ANY]),
        [pl.BlockSpec((tm, tn), lambda i, j, shards_ref: (i, shards_ref[j]))], [_sds((SEQ, IN_WIDTH), F32)], [],
        [x, w_in] + ([] if into is None else [into]), ride, aliases=None if into is None else {2: 0}, prefetch=shards)


def _proj_early(x, w_in, shards, ride):
    tn, last = IN_WIDTH // N_DEV, shards.shape[0] - 1

    def body(shards_ref, x_ref, w_in_ref, o_ref, w_ref, shard, fetched, *sems):
        del w_in_ref
        j = pl.program_id(0)

        @pl.when(j == 0)
        def _():
            ride.start((), [w_ref], sems)

        fetch = pltpu.make_async_copy(w_ref.at[shards_ref[j]], shard, fetched)
        fetch.start()
        fetch.wait()
        o_ref[...] = _dot(x_ref[...].astype(BF16), shard[...])

        @pl.when(j == last)
        def _():
            ride.wait((), [w_ref], sems)

    return _pallas_call(
        body, name="proj_early", out_shape=[_sds((SEQ, IN_WIDTH), F32), _sds(w_in.shape, w_in.dtype)],
        input_output_aliases={2: 1},
        grid_spec=pltpu.PrefetchScalarGridSpec(
            num_scalar_prefetch=1, grid=(shards.shape[0],),
            in_specs=[pl.BlockSpec((SEQ, D_MODEL), lambda j, shards_ref: (0, 0)), ANY],
            out_specs=[pl.BlockSpec((SEQ, tn), lambda j, shards_ref: (0, shards_ref[j])), ANY],
            scratch_shapes=[pltpu.VMEM((D_MODEL, tn), BF16), pltpu.SemaphoreType.DMA] + ride.sems),
        compiler_params=_cparams(dimension_semantics=("arbitrary",)),
    )(shards, x, w_in)


def _row_spec(tm, width, col=0):
    return pl.BlockSpec((tm, width), lambda i, col=col: (i, col))


def _full_spec(shape):
    return pl.BlockSpec(shape, lambda i: (0,) * len(shape))


def _weight_spec(shape):
    return pl.BlockSpec(shape, lambda i: (0,) * len(shape), pipeline_mode=pl.Buffered(1))


def _mixer_out(attn, ys, proj, x, w_ab, w_sb, w_glu, w_out, b_gate, ln_g, ln_b, ride=None):
    tm = 512

    def body(attn_ref, ys_ref, gl0_ref, gl1_ref, x_ref, wab_ref, wsb_ref, wglu_ref, wout_ref, bg_ref, g_ref, b_ref,
             h_ref, xhat_ref, rstd_ref, glu_ref, ya_ref, yssm_ref):
        gy, _ = _gelu(ys_ref[...])
        glu = _dot(gy.astype(BF16), _side_by_side(wglu_ref))
        glu_ref[...] = glu.astype(BF16)
        y_s = glu[:, :SSM_WIDTH] * jax.nn.sigmoid(glu[:, SSM_WIDTH:])
        y_ssm = _dot(y_s.astype(BF16), _side_by_side(wsb_ref))
        y_attn = _dot(attn_ref[...].astype(BF16), _side_by_side(wab_ref))
        ya_ref[...] = y_attn.astype(BF16)
        yssm_ref[...] = y_ssm.astype(BF16)
        g0 = jax.nn.sigmoid(gl0_ref[...] + _side_by_side(bg_ref, 0))
        g1 = jax.nn.sigmoid(gl1_ref[...] + _side_by_side(bg_ref, 1))
        mixed = g0 * y_attn + g1 * y_ssm
        r1 = DN_ALPHA * x_ref[...] + _dot(mixed.astype(BF16), wout_ref[...])
        h, xhat, rstd = _layer_norm(r1, g_ref[...], b_ref[...])
        h_ref[...] = h
        xhat_ref[...] = xhat
        rstd_ref[...] = jnp.broadcast_to(rstd, (tm, 128))

    wide = _sds((SEQ, D_MODEL), F32)
    return _call(
        body, "mixer_out", (SEQ // tm,),
        [_row_spec(tm, ATTN_WIDTH), _row_spec(tm, SSM_WIDTH), _row_spec(tm, D_MODEL, GL_COL), _row_spec(tm, D_MODEL, GL_COL + 1),
         _row_spec(tm, D_MODEL), _weight_spec((N_DEV, ATTN_WIDTH, 128)), _weight_spec((N_DEV, SSM_WIDTH, 128)),
         _weight_spec((N_DEV, SSM_WIDTH, 128)), _weight_spec((D_MODEL, D_MODEL)), _full_spec((N_DEV, 2, 128)),
         _full_spec((1, D_MODEL)), _full_spec((1, D_MODEL))],
        [_row_spec(tm, D_MODEL), _row_spec(tm, D_MODEL), _row_spec(tm, 128), _row_spec(tm, D_MODEL),
         _row_spec(tm, D_MODEL), _row_spec(tm, D_MODEL)],
        [wide, wide, _sds((SEQ, 128), F32)] + [_sds((SEQ, D_MODEL), BF16)] * 3, [],
        [attn, ys, proj, proj, x, w_ab, w_sb, w_glu, w_out, b_gate, ln_g, ln_b], ride)


def _ff_up(h, w_gate, w_up, ride=None):
    tm, tn = 1024, 768

    def body(h_ref, wg_ref, wu_ref, a_ref, b_ref, f_ref):
        hb = h_ref[...].astype(BF16)
        a, b = _dot(hb, _side_by_side(wg_ref)), _dot(hb, _side_by_side(wu_ref))
        a_ref[...] = a.astype(BF16)
        b_ref[...] = b.astype(BF16)
        f_ref[...] = (a * jax.nn.sigmoid(a) * b).astype(BF16)

    tile = pl.BlockSpec((tm, tn), lambda i, j: (i, j))
    wtile = pl.BlockSpec((tn // FF_PAD, D_MODEL, FF_PAD), lambda i, j: (j, 0, 0))
    out = _sds((SEQ, D_FF_PAD), BF16)
    return _call(body, "ff_up", (SEQ // tm, D_FF_PAD // tn), [pl.BlockSpec((tm, D_MODEL), lambda i, j: (i, 0)), wtile, wtile],
                 [tile, tile, tile], [out, out, out], [], [h, w_gate, w_up], ride)


def _ff_down_loss(f, w_down, h, target, ln_g, ln_b):
    tm = 512

    def body(f_ref, w_ref, h_ref, t_ref, g_ref, b_ref, dr_ref, dg_ref, db_ref, loss_ref):
        @pl.when(pl.program_id(0) == 0)
        def _():
            dg_ref[...] = jnp.zeros_like(dg_ref)
            db_ref[...] = jnp.zeros_like(db_ref)
            loss_ref[...] = jnp.zeros_like(loss_ref)

        r2 = DN_ALPHA * h_ref[...] + _dot(f_ref[...], w_ref[...])
        g = g_ref[...]
        out, xhat, rstd = _layer_norm(r2, g, b_ref[...])
        err = out - t_ref[...]
        loss_ref[...] += 0.5 * jnp.sum(jnp.mean(err * err, axis=-1, keepdims=True), axis=0, keepdims=True)
        dout = err * (1.0 / D_MODEL)
        dg_ref[...] += jnp.sum(dout * xhat, axis=0, keepdims=True)
        db_ref[...] += jnp.sum(dout, axis=0, keepdims=True)
        dr_ref[...] = _layer_norm_bwd(dout, xhat, rstd, g)

    vec = _sds((1, D_MODEL), F32)
    return _pallas_call(
        body, name="ff_down_loss", grid=(SEQ // tm,),
        in_specs=[_row_spec(tm, D_FF_PAD), _weight_spec((D_FF_PAD, D_MODEL)), _row_spec(tm, D_MODEL), _row_spec(tm, D_MODEL),
                  _full_spec((1, D_MODEL)), _full_spec((1, D_MODEL))],
        out_specs=(_row_spec(tm, D_MODEL), _full_spec((1, D_MODEL)), _full_spec((1, D_MODEL)), _full_spec((1, 128))),
        out_shape=(_sds((SEQ, D_MODEL), F32), vec, vec, _sds((1, 128), F32)),
        compiler_params=_cparams(dimension_semantics=("arbitrary",)),
    )(f, w_down, h, target, ln_g, ln_b)


def _ff_down_bwd(dr2, w_down, a, b):
    tm, tn = 1024, 768

    def body(dr_ref, w_ref, a_ref, b_ref, da_ref, db_ref):
        df = _dot_nt(dr_ref[...].astype(BF16), w_ref[...])
        av, bv = a_ref[...].astype(F32), b_ref[...].astype(F32)
        sg = jax.nn.sigmoid(av)
        da_ref[...] = (df * bv * sg * (1.0 + av * (1.0 - sg))).astype(BF16)
        db_ref[...] = (df * av * sg).astype(BF16)

    tile = pl.BlockSpec((tm, tn), lambda i, j: (i, j))
    out = _sds((SEQ, D_FF_PAD), BF16)
    return _pallas_call(
        body, name="ff_down_bwd", grid=(SEQ // tm, D_FF_PAD // tn),
        in_specs=[pl.BlockSpec((tm, D_MODEL), lambda i, j: (i, 0)), pl.BlockSpec((tn, D_MODEL), lambda i, j: (j, 0)), tile, tile],
        out_specs=(tile, tile), out_shape=(out, out),
        compiler_params=_cparams(dimension_semantics=("arbitrary", "arbitrary")),
    )(dr2, w_down, a, b)


def _ff_up_bwd(da, db, w_gate, w_up, dr2, xhat1, rstd1, ln_g, ride=None):
    tm, tk = 1024, 768
    nk = D_FF_PAD // tk

    def body(da_ref, db_ref, wg_ref, wu_ref, dr2_ref, xhat_ref, rstd_ref, g_ref, dr1_ref, dg_ref, dbias_ref, acc):
        i, k = pl.program_id(0), pl.program_id(1)

        @pl.when(jnp.logical_and(i == 0, k == 0))
        def _():
            dg_ref[...] = jnp.zeros_like(dg_ref)
            dbias_ref[...] = jnp.zeros_like(dbias_ref)

        part = _dot_nt(da_ref[...], _side_by_side(wg_ref)) + _dot_nt(db_ref[...], _side_by_side(wu_ref))

        @pl.when(k == 0)
        def _():
            acc[...] = part

        @pl.when(k > 0)
        def _():
            acc[...] += part

        @pl.when(k == nk - 1)
        def _():
            dh = DN_ALPHA * dr2_ref[...] + acc[...]
            xhat = xhat_ref[...]
            dg_ref[...] += jnp.sum(dh * xhat, axis=0, keepdims=True)
            dbias_ref[...] += jnp.sum(dh, axis=0, keepdims=True)
            rstd = jnp.max(rstd_ref[...], axis=1, keepdims=True)
            dr1_ref[...] = _layer_norm_bwd(dh, xhat, rstd, g_ref[...])

    hid = pl.BlockSpec((tm, tk), lambda i, k: (i, k))
    wtile = pl.BlockSpec((tk // FF_PAD, D_MODEL, FF_PAD), lambda i, k: (k, 0, 0))
    row = pl.BlockSpec((tm, D_MODEL), lambda i, k: (i, 0))
    vec = pl.BlockSpec((1, D_MODEL), lambda i, k: (0, 0))
    return _call(
        body, "ff_up_bwd", (SEQ // tm, nk),
        [hid, hid, wtile, wtile, row, row, pl.BlockSpec((tm, 128), lambda i, k: (i, 0)), vec],
        [row, vec, vec], [_sds((SEQ, D_MODEL), F32), _sds((1, D_MODEL), F32), _sds((1, D_MODEL), F32)],
        [pltpu.VMEM((tm, D_MODEL), F32)], [da, db, w_gate, w_up, dr2, xhat1, rstd1, ln_g], ride)


def _mixer_bwd(dr1, proj, y_attn, y_ssm, glu, ys, w_ab, w_sb, w_glu, w_out, b_gate):
    tm = 256

    def body(dr1_ref, gl0_ref, gl1_ref, ya_ref, yssm_ref, glu_ref, ys_ref, wab_ref, wsb_ref, wglu_ref, wout_ref, bg_ref,
             dya_ref, dyssm_ref, dgl_ref, dattn_ref, dglu_ref, dys_ref, mixed_ref, ysb_ref, gy_ref, dbg_ref):
        @pl.when(pl.program_id(0) == 0)
        def _():
            dbg_ref[...] = jnp.zeros_like(dbg_ref)

        dmixed = _dot_nt(dr1_ref[...].astype(BF16), wout_ref[...])
        g0 = jax.nn.sigmoid(gl0_ref[...] + _side_by_side(bg_ref, 0))
        g1 = jax.nn.sigmoid(gl1_ref[...] + _side_by_side(bg_ref, 1))
        y_attn, y_ssm = ya_ref[...].astype(F32), yssm_ref[...].astype(F32)
        mixed_ref[...] = (g0 * y_attn + g1 * y_ssm).astype(BF16)
        dya = (dmixed * g0).astype(BF16)
        dyssm = (dmixed * g1).astype(BF16)
        dya_ref[...] = dya
        dyssm_ref[...] = dyssm
        dgl0 = dmixed * y_attn * g0 * (1.0 - g0)
        dgl1 = dmixed * y_ssm * g1 * (1.0 - g1)
        dgl_ref[:, :GL_COL * D_MODEL] = jnp.zeros((tm, GL_COL * D_MODEL), BF16)
        dgl_ref[:, GL_COL * D_MODEL:(GL_COL + 1) * D_MODEL] = dgl0.astype(BF16)
        dgl_ref[:, (GL_COL + 1) * D_MODEL:] = dgl1.astype(BF16)
        dbg_ref[:, :D_MODEL] += jnp.sum(dgl0, axis=0, keepdims=True)
        dbg_ref[:, D_MODEL:] += jnp.sum(dgl1, axis=0, keepdims=True)
        dattn_ref[...] = _dot_nt(dya, _side_by_side(wab_ref))
        dy_s = _dot_nt(dyssm, _side_by_side(wsb_ref))
        glu = glu_ref[...].astype(F32)
        glu1, sg = glu[:, :SSM_WIDTH], jax.nn.sigmoid(glu[:, SSM_WIDTH:])
        ysb_ref[...] = (glu1 * sg).astype(BF16)
        dglu1 = (dy_s * sg).astype(BF16)
        dglu2 = (dy_s * glu1 * sg * (1.0 - sg)).astype(BF16)
        dglu_ref[:, :SSM_WIDTH] = dglu1
        dglu_ref[:, SSM_WIDTH:] = dglu2
        dgy = _dot_nt(jnp.concatenate([dglu1, dglu2], axis=1), _side_by_side(wglu_ref))
        ys = ys_ref[...]
        gy, t = _gelu(ys)
        gy_ref[...] = gy.astype(BF16)
        dys_ref[...] = dgy * _gelu_grad(ys, t)

    wide_b, half_b = _sds((SEQ, D_MODEL), BF16), _sds((SEQ, SSM_WIDTH), BF16)
    half_f = _sds((SEQ, SSM_WIDTH), F32)
    return _pallas_call(
        body, name="mixer_bwd", grid=(SEQ // tm,),
        in_specs=[_row_spec(tm, D_MODEL), _row_spec(tm, D_MODEL, GL_COL), _row_spec(tm, D_MODEL, GL_COL + 1), _row_spec(tm, D_MODEL),
                  _row_spec(tm, D_MODEL), _row_spec(tm, D_MODEL), _row_spec(tm, SSM_WIDTH), _full_spec((N_DEV, ATTN_WIDTH, 128)),
                  _full_spec((N_DEV, SSM_WIDTH, 128)), _full_spec((N_DEV, SSM_WIDTH, 128)), _full_spec((D_MODEL, D_MODEL)),
                  _full_spec((N_DEV, 2, 128))],
        out_specs=(_row_spec(tm, D_MODEL), _row_spec(tm, D_MODEL), _row_spec(tm, IN_WIDTH), _row_spec(tm, ATTN_WIDTH),
                   _row_spec(tm, D_MODEL), _row_spec(tm, SSM_WIDTH), _row_spec(tm, D_MODEL), _row_spec(tm, SSM_WIDTH),
                   _row_spec(tm, SSM_WIDTH), _full_spec((1, 2 * D_MODEL))),
        out_shape=(wide_b, wide_b, _sds((SEQ, IN_WIDTH), BF16), half_f, wide_b, half_f, wide_b, half_b, half_b,
                   _sds((1, 2 * D_MODEL), F32)),
        compiler_params=_cparams(dimension_semantics=("arbitrary",)),
    )(dr1, proj, proj, y_attn, y_ssm, glu, ys, w_ab, w_sb, w_glu, w_out, b_gate)


def _grad_x(dproj, w_in, dr1, ride=None):
    tm, tk = 1024, 1792
    nk = IN_WIDTH // tk

    def body(dp_ref, w_ref, dr1_ref, o_ref, acc):
        k = pl.program_id(1)
        part = _dot_nt(dp_ref[...], _side_by_side(w_ref))

        @pl.when(k == 0)
        def _():
            acc[...] = part

        @pl.when(k > 0)
        def _():
            acc[...] += part

        @pl.when(k == nk - 1)
        def _():
            o_ref[...] = DN_ALPHA * dr1_ref[...] + acc[...]

    row = pl.BlockSpec((tm, D_MODEL), lambda i, k: (i, 0))
    return _call(
        body, "grad_x", (SEQ // tm, nk),
        [pl.BlockSpec((tm, tk), lambda i, k: (i, k)), pl.BlockSpec((2, D_MODEL, tk // 2), lambda i, k: (k, 0, 0)), row],
        [row], [_sds((SEQ, D_MODEL), F32)], [pltpu.VMEM((tm, D_MODEL), F32)], [dproj, w_in, dr1], ride)


def _weight_grad(a, b, name, shard_cols=None, ride=None):
    k, n = a.shape[1], b.shape[1]
    tk = min(k, 512) if shard_cols else k // N_DEV
    tn = n // 4 if shard_cols else min(n, 1024)

    def body(a_ref, b_ref, o_ref):
        grad = _dot_tn(a_ref[...].astype(BF16), b_ref[...].astype(BF16))
        if shard_cols:
            o_ref[0] = grad[:, :shard_cols].astype(BF16)
            o_ref[1] = grad[:, shard_cols:].astype(BF16)
        else:
            o_ref[...] = grad.astype(BF16)

    if shard_cols:
        out_spec = pl.BlockSpec((2, None, tk, shard_cols), lambda kk, j: (0, j, kk, 0))
        out_shape = _sds((2, 4, k, shard_cols), BF16)
    else:
        out_spec = pl.BlockSpec((None, None, tk, tn), lambda kk, j: (kk % 2, kk // 2, 0, j))
        out_shape = _sds((2, 4, tk, n), BF16)
    out = _call(body, name, (k // tk, n // tn),
                [pl.BlockSpec((SEQ, tk), lambda kk, j: (0, kk)), pl.BlockSpec((SEQ, tn), lambda kk, j: (0, j))],
                [out_spec], [out_shape], [], [a, b], ride)
    return out[0] if ride is None else out


MESH = pl.DeviceIdType.MESH
ANY = pl.BlockSpec(memory_space=pl.ANY)


def _place():
    return lax.axis_index("x"), lax.axis_index("y"), lax.axis_index("c")


def _other_chips(x, y):
    return [(1 - x, y), (x, 1 - y), (1 - x, 1 - y)]


class _Ride:
    def __init__(self, operands, results, aliases, sems, start, wait):
        self.operands, self.results, self.aliases, self.sems = list(operands), list(results), dict(aliases), list(sems)
        self.start, self.wait = start, wait

    def __add__(self, other):
        n_in, n_out, n_sem = len(self.operands), len(self.results), len(self.sems)

        def both(which):
            def run(ins, outs, sems):
                getattr(self, which)(ins[:n_in], outs[:n_out], sems[:n_sem])
                getattr(other, which)(ins[n_in:], outs[n_out:], sems[n_sem:])
            return run

        aliases = {**self.aliases, **{n_in + i: n_out + j for i, j in other.aliases.items()}}
        return _Ride(self.operands + other.operands, self.results + other.results, aliases, self.sems + other.sems,
                     both("start"), both("wait"))


def _call(body, name, grid, in_specs, out_specs, out_shape, scratch_shapes, operands, ride=None, aliases=None, prefetch=None):
    in_specs, out_specs, out_shape = list(in_specs), list(out_specs), list(out_shape)
    scratch_shapes, operands, aliases = list(scratch_shapes), list(operands), dict(aliases or {})
    kernel_body = body
    lead = 0 if prefetch is None else 1
    if ride is not None:
        n_in, n_out, n_scr, r_in, r_out = len(in_specs), len(out_specs), len(scratch_shapes), len(ride.operands), len(ride.results)

        def kernel_body(*refs):
            first_refs, refs = refs[:lead], refs[lead:]
            out0, scr0 = n_in + r_in, n_in + r_in + n_out + r_out
            ride_refs = (refs[n_in:out0], refs[out0 + n_out:scr0], refs[scr0 + n_scr:])
            ids = [pl.program_id(i) for i in range(len(grid))]
            first = functools.reduce(jnp.logical_and, [i == 0 for i in ids])
            last = functools.reduce(jnp.logical_and, [i == g - 1 for i, g in zip(ids, grid)])

            @pl.when(first)
            def _():
                ride.start(*ride_refs)

            body(*first_refs, *refs[:n_in], *refs[out0:out0 + n_out], *refs[scr0:scr0 + n_scr])

            @pl.when(last)
            def _():
                ride.wait(*ride_refs)

        aliases.update({n_in + i: n_out + j for i, j in ride.aliases.items()})
        in_specs += [ANY] * r_in
        out_specs += [ANY] * r_out
        out_shape += ride.results
        scratch_shapes += ride.sems
        operands += ride.operands
    params = _cparams(dimension_semantics=("arbitrary",) * len(grid))
    if prefetch is None:
        return _pallas_call(
            kernel_body, name=name, grid=grid, in_specs=in_specs, out_specs=out_specs, out_shape=out_shape,
            scratch_shapes=scratch_shapes, input_output_aliases=aliases, compiler_params=params)(*operands)
    return _pallas_call(
        kernel_body, name=name, out_shape=out_shape, input_output_aliases={1 + i: j for i, j in aliases.items()},
        grid_spec=pltpu.PrefetchScalarGridSpec(num_scalar_prefetch=1, grid=grid, in_specs=in_specs, out_specs=out_specs,
                                               scratch_shapes=scratch_shapes),
        compiler_params=params)(prefetch, *operands)


def _after(*arrays):
    return _Ride(arrays, [], {}, [], lambda *refs: None, lambda *refs: None)


def _gather_first_level(shards):
    n = len(shards)

    def copies(ins, outs, sems, landed):
        send_sems, recv_sems, local_sems = sems
        x, y, c = _place()
        peers = [(x, y, 1 - c)] + [(px, py, c) for px, py in _other_chips(x, y)]

        def row(peer):
            return 4 * x + 2 * y + c if not landed else 4 * peer[0] + 2 * peer[1] + peer[2]

        local = [pltpu.make_async_copy(ins[a], outs[a].at[4 * x + 2 * y + c], local_sems.at[a]) for a in range(n)]
        remote = [pltpu.make_async_remote_copy(
            src_ref=ins[a], dst_ref=outs[a].at[row(peer)], send_sem=send_sems.at[a, k], recv_sem=recv_sems.at[a, k],
            device_id=peer, device_id_type=MESH) for a in range(n) for k, peer in enumerate(peers)]
        return local, remote

    def start(ins, outs, sems):
        local, remote = copies(ins, outs, sems, False)
        for cp in local + remote:
            cp.start()

    def wait(ins, outs, sems):
        local, sent = copies(ins, outs, sems, False)
        for cp in copies(ins, outs, sems, True)[1]:
            cp.wait_recv()
        for cp in sent:
            cp.wait_send()
        for cp in local:
            cp.wait()

    return _Ride(shards, [_sds((N_DEV,) + s.shape, s.dtype) for s in shards], {},
                 [pltpu.SemaphoreType.DMA((n, 4)), pltpu.SemaphoreType.DMA((n, 4)), pltpu.SemaphoreType.DMA((n,))], start, wait)


def _gather_second_level(buffers):
    n = len(buffers)

    def copies(outs, sems, core):
        send_sems, recv_sems = sems
        x, y, c = _place()
        return [pltpu.make_async_remote_copy(
            src_ref=outs[a].at[4 * px + 2 * py + core], dst_ref=outs[a].at[4 * px + 2 * py + core], send_sem=send_sems.at[a, j],
            recv_sem=recv_sems.at[a, j], device_id=(x, y, 1 - c), device_id_type=MESH)
            for a in range(n) for j, (px, py) in enumerate(_other_chips(x, y))]

    def start(ins, outs, sems):
        for cp in copies(outs, sems, lax.axis_index("c")):
            cp.start()

    def wait(ins, outs, sems):
        for cp in copies(outs, sems, 1 - lax.axis_index("c")):
            cp.wait_recv()
        for cp in copies(outs, sems, lax.axis_index("c")):
            cp.wait_send()

    return _Ride(buffers, [_sds(b.shape, b.dtype) for b in buffers], {i: i for i in range(n)},
                 [pltpu.SemaphoreType.DMA((n, 3)), pltpu.SemaphoreType.DMA((n, 3))], start, wait)


def _relayed_gather(shards):
    n = len(shards)
    buffers = [_sds((N_DEV,) + s.shape, s.dtype) for s in shards]
    dma = pltpu.SemaphoreType.DMA

    def remote(src, dst, send_sem, recv_sem, to):
        return pltpu.make_async_remote_copy(src_ref=src, dst_ref=dst, send_sem=send_sem, recv_sem=recv_sem,
                                            device_id=to, device_id_type=MESH)

    def row(px, py, pc):
        return 4 * px + 2 * py + pc

    def ride(operands, aliases, sems, copies):
        def start(ins, outs, sem_refs):
            local, sent = copies(ins, outs, sem_refs, False)
            for cp in local + sent:
                cp.start()

        def wait(ins, outs, sem_refs):
            local, sent = copies(ins, outs, sem_refs, False)
            for cp in copies(ins, outs, sem_refs, True)[1]:
                cp.wait_recv()
            for cp in sent:
                cp.wait_send()
            for cp in local:
                cp.wait()

        return _Ride(operands, buffers, aliases, sems, start, wait)

    def first(ins, outs, sems, landed):
        x, y, c = _place()
        peers = [(x, y, 1 - c), (1 - x, y, c), (x, 1 - y, c)]
        local = [pltpu.make_async_copy(ins[a], outs[a].at[row(x, y, c)], sems[2].at[a]) for a in range(n)]
        return local, [remote(ins[a], outs[a].at[row(*peer) if landed else row(x, y, c)], sems[0].at[a, k], sems[1].at[a, k], peer)
                       for a in range(n) for k, peer in enumerate(peers)]

    def second(ins, outs, sems, landed):
        x, y, c = _place()
        mine = 1 - c if landed else c
        copies = []
        for a in range(n):
            half = shards[a].shape[0] // 2
            over_x, over_y, diagonal = outs[a].at[row(1 - x, y, mine)], outs[a].at[row(x, 1 - y, mine)], outs[a].at[row(1 - x, 1 - y, c)]
            lower, upper = pl.ds(0, half), pl.ds(half, half)
            copies += [remote(over_x, over_x, sems[0].at[a, 0], sems[1].at[a, 0], (x, y, 1 - c)),
                       remote(over_y, over_y, sems[0].at[a, 1], sems[1].at[a, 1], (x, y, 1 - c))]
            if landed:
                copies += [remote(diagonal.at[lower], diagonal.at[lower], sems[0].at[a, 2], sems[1].at[a, 2], (1 - x, y, c)),
                           remote(diagonal.at[upper], diagonal.at[upper], sems[0].at[a, 3], sems[1].at[a, 3], (x, 1 - y, c))]
            else:
                copies += [remote(over_y.at[lower], over_y.at[lower], sems[0].at[a, 2], sems[1].at[a, 2], (1 - x, y, c)),
                           remote(over_x.at[upper], over_x.at[upper], sems[0].at[a, 3], sems[1].at[a, 3], (x, 1 - y, c))]
        return [], copies

    def third(ins, outs, sems, landed):
        x, y, c = _place()
        return [], [remote(outs[a].at[row(1 - x, 1 - y, 1 - c if landed else c)], outs[a].at[row(1 - x, 1 - y, 1 - c if landed else c)],
                           sems[0].at[a], sems[1].at[a], (x, y, 1 - c)) for a in range(n)]

    def later(copies, n_sems):
        return lambda partly: ride(partly, {i: i for i in range(n)}, [dma((n,) + n_sems), dma((n,) + n_sems)], copies)

    return ride(shards, {}, [dma((n, 3)), dma((n, 3)), dma((n,))], first), later(second, (4,)), later(third, ())


def _sibling_swap_ride(grads):
    n = len(grads)

    def copies(ins, outs, sems):
        x, y, c = _place()
        return [pltpu.make_async_remote_copy(
            src_ref=ins[a].at[1 - c], dst_ref=outs[a], send_sem=sems[0].at[a], recv_sem=sems[1].at[a],
            device_id=(x, y, 1 - c), device_id_type=MESH) for a in range(n)]

    def start(ins, outs, sems):
        for cp in copies(ins, outs, sems):
            cp.start()

    def wait(ins, outs, sems):
        for cp in copies(ins, outs, sems):
            cp.wait()

    return _Ride(grads, [_sds(g.shape[1:], g.dtype) for g in grads], {},
                 [pltpu.SemaphoreType.DMA((n,)), pltpu.SemaphoreType.DMA((n,))], start, wait)


def _chip_swap_ride(sums):
    n = len(sums)

    def copies(ins, outs, sems, landed):
        send_sems, recv_sems, local_sems = sems
        x, y, c = _place()
        mine = 2 * x + y
        local = [pltpu.make_async_copy(ins[a].at[mine], outs[a].at[mine], local_sems.at[a]) for a in range(n)]
        remote = [pltpu.make_async_remote_copy(
            src_ref=ins[a].at[2 * px + py], dst_ref=outs[a].at[2 * px + py if landed else mine], send_sem=send_sems.at[a, j],
            recv_sem=recv_sems.at[a, j], device_id=(px, py, c), device_id_type=MESH)
            for a in range(n) for j, (px, py) in enumerate(_other_chips(x, y))]
        return local, remote

    def start(ins, outs, sems):
        local, remote = copies(ins, outs, sems, False)
        for cp in local + remote:
            cp.start()

    def wait(ins, outs, sems):
        local, sent = copies(ins, outs, sems, False)
        for cp in copies(ins, outs, sems, True)[1]:
            cp.wait_recv()
        for cp in sent:
            cp.wait_send()
        for cp in local:
            cp.wait()

    return _Ride(sums, [_sds(s.shape, s.dtype) for s in sums], {},
                 [pltpu.SemaphoreType.DMA((n, 3)), pltpu.SemaphoreType.DMA((n, 3)), pltpu.SemaphoreType.DMA((n,))], start, wait)


def _send_buffers(shards, name):
    n = len(shards)

    def body(*refs):
        for (w, transposed, rows, cols), w_ref, o_ref in zip(shards, refs[:n], refs[n:]):
            if transposed:
                c, r = w.shape
                padded = jnp.concatenate([w_ref[...], jnp.zeros((cols - c, r), F32)], axis=0) if cols > c else w_ref[...]
                o_ref[...] = padded.T.astype(BF16)
            else:
                r, c = w.shape
                if (r, c) != (rows, cols):
                    o_ref[...] = jnp.zeros((rows, cols), BF16)
                o_ref[:r, :c] = w_ref[...].astype(BF16)

    return _pallas_call(body, name=name, out_shape=[_sds((rows, cols), BF16) for _, _, rows, cols in shards])(
        *[w for w, _, _, _ in shards])


def _all_gather(shards, name):
    n = len(shards)
    first, second, third = _relayed_gather(shards)
    levels = [first, second(shards), third(shards)]
    counts = [len(level.sems) for level in levels]

    def body(*refs):
        ins, outs, sems = refs[:n], refs[n:2 * n], refs[2 * n:]
        for i, level in enumerate(levels):
            mine = sems[sum(counts[:i]):sum(counts[:i + 1])]
            level.start(ins, outs, mine)
            level.wait(ins, outs, mine)

    return _pallas_call(
        body, name=name, in_specs=[ANY] * n, out_specs=[ANY] * n, out_shape=first.results,
        scratch_shapes=[s for level in levels for s in level.sems],
    )(*shards)


def _exchange(ride, name):
    n_in, n_out = len(ride.operands), len(ride.results)

    def body(*refs):
        ride.start(refs[:n_in], refs[n_in:n_in + n_out], refs[n_in + n_out:])
        ride.wait(refs[:n_in], refs[n_in:n_in + n_out], refs[n_in + n_out:])

    return _pallas_call(body, name=name, in_specs=[ANY] * n_in, out_specs=[ANY] * n_out, out_shape=ride.results,
                        scratch_shapes=ride.sems, input_output_aliases=ride.aliases)(*ride.operands)


HBM = pl.BlockSpec(memory_space=pltpu.HBM)
SEMAPHORES = pl.BlockSpec(memory_space=pltpu.SEMAPHORE)
IN_FLIGHT = pltpu.CompilerParams(has_side_effects=pltpu.SideEffectType.DATAFLOW_SIDE_EFFECTING)


def _chip_swap_copies(src_refs, land_refs, send_sems, recv_sems, landed):
    x, y, c = _place()
    return [pltpu.make_async_remote_copy(
        src_ref=src.at[2 * px + py], dst_ref=land.at[2 * px + py if landed else 2 * x + y], send_sem=send_sems.at[3 * a + j],
        recv_sem=recv_sems.at[3 * a + j], device_id=(px, py, c), device_id_type=MESH)
        for a, (src, land) in enumerate(zip(src_refs, land_refs)) for j, (px, py) in enumerate(_other_chips(x, y))]


def _chip_swap_start(sums, name):
    n = len(sums)

    def body(*refs):
        src_refs, land_refs, (send_sems, recv_sems), token = refs[:n], refs[n:2 * n], refs[2 * n:2 * n + 2], refs[-1]
        for cp in _chip_swap_copies(src_refs, land_refs, send_sems, recv_sems, False):
            cp.start()
        token[...] = jnp.zeros_like(token)

    kept = [pltpu.HBM(s.shape, s.dtype) for s in sums]
    out = _pallas_call(
        body, name=name,
        out_shape=[pltpu.SemaphoreType.DMA((3 * n,)), pltpu.SemaphoreType.DMA((3 * n,))] + kept + kept + [_sds((8, 128), F32)],
        in_specs=[HBM] * (2 * n), out_specs=[SEMAPHORES, SEMAPHORES] + [HBM] * (2 * n) + [pl.BlockSpec(memory_space=pltpu.VMEM)],
        input_output_aliases={i: 2 + i for i in range(2 * n)}, compiler_params=IN_FLIGHT,
    )(*[pltpu.with_memory_space_constraint(s, pltpu.HBM) for s in sums],
      *[pltpu.with_memory_space_constraint(lax.empty(s.shape, s.dtype), pltpu.HBM) for s in sums])
    return out[0], out[1], out[2:2 + n], out[2 + n:2 + 2 * n], out[-1]


def _chip_swap_wait(send_sems, recv_sems, sums, landings, after, name):
    n = len(sums)

    def body(*refs):
        src_refs, land_refs, (send_sems, recv_sems) = refs[:n], refs[n:2 * n], refs[2 * n:2 * n + 2]
        for cp in _chip_swap_copies(src_refs, land_refs, send_sems, recv_sems, False):
            cp.wait_send()
        for cp in _chip_swap_copies(src_refs, land_refs, send_sems, recv_sems, True):
            cp.wait_recv()

    out = _pallas_call(
        body, name=name, out_shape=[pltpu.HBM(s.shape, s.dtype) for s in list(sums) + list(landings)],
        in_specs=[HBM] * (2 * n) + [SEMAPHORES, SEMAPHORES] + [ANY] * len(after), out_specs=[HBM] * (2 * n),
        input_output_aliases={i: i for i in range(2 * n)}, compiler_params=IN_FLIGHT,
    )(*sums, *landings, send_sems, recv_sems, *after)
    return out[:n], out[n:]


def _pair_sums(gs, rs, core, name):
    n_arrays = len(gs)

    def body(core_ref, *refs):
        for g_ref, r_ref, o_ref in zip(refs[:n_arrays], refs[n_arrays:2 * n_arrays], refs[2 * n_arrays:]):
            o_ref[...] = (g_ref[...].astype(F32) + r_ref[...].astype(F32)).astype(o_ref.dtype)

    def own(g):
        return pl.BlockSpec((None, None) + g.shape[2:], lambda p, core_ref: (core_ref[0], p, 0, 0))

    def chip(g):
        return pl.BlockSpec((None,) + g.shape[2:], lambda p, core_ref: (p, 0, 0))

    return _pallas_call(
        body, name=name,
        grid_spec=pltpu.PrefetchScalarGridSpec(
            num_scalar_prefetch=1, grid=(4,), in_specs=[own(g) for g in gs] + [chip(g) for g in gs],
            out_specs=[chip(g) for g in gs]),
        out_shape=[_sds(g.shape[1:], g.dtype) for g in gs], compiler_params=_cparams(dimension_semantics=("arbitrary",)),
    )(core, *gs, *rs)


def _adamw_math(w, g, m, v):
    m = ADAM_B1 * m + (1.0 - ADAM_B1) * g
    v = ADAM_B2 * v + (1.0 - ADAM_B2) * (g * g)
    m_hat = m / (1.0 - ADAM_B1 ** ADAM_STEP)
    v_hat = v / (1.0 - ADAM_B2 ** ADAM_STEP)
    return -ADAM_LR * (m_hat / (jnp.sqrt(v_hat) + ADAM_EPS) + ADAM_WD * w), m, v


def _adamw_many(weights, name, ride=None):
    steps = 4
    in_specs, out_specs, out_shape, operands, tiles = [], [], [], [], []
    for w, m, v, parts, own, transposed in weights:
        _, pr, pc = parts.shape
        if transposed:
            c, r = w.shape
            tile = pl.BlockSpec((c, r // steps), lambda i: (0, i))
            part_tile = pl.BlockSpec((4, r // steps, pc), lambda i: (0, i, 0))
            tiles.append((c, r // steps))
        elif w.shape[0] % (8 * steps) == 0:
            r, c = w.shape
            tile = pl.BlockSpec((r // steps, c), lambda i: (i, 0))
            part_tile = pl.BlockSpec((4, r // steps, pc), lambda i: (0, i, 0))
            tiles.append((r // steps, c))
        else:
            tile = pl.BlockSpec(w.shape, lambda i: (0, 0))
            part_tile = pl.BlockSpec(parts.shape, lambda i: (0, 0, 0))
            tiles.append(w.shape)
        in_specs += [tile, tile, tile] + [part_tile] * (1 if own is None else 2)
        out_specs += [tile] * 4
        out_shape += [_sds(w.shape, F32)] * 4
        operands += [w, m, v, parts] + ([] if own is None else [own])
    n_in = len(operands)

    def body(*refs):
        ins, outs = list(refs[:n_in]), refs[n_in:]
        this_chip = 2 * lax.axis_index("x") + lax.axis_index("y")
        for k, (_, _, _, _, own, transposed) in enumerate(weights):
            w_ref, m_ref, v_ref, p_ref = ins[:4]
            own_ref = None if own is None else ins[4]
            del ins[:4 if own is None else 5]
            rows, cols = tiles[k]
            g = None
            for q in range(4):
                index = (q,) if transposed else (q, slice(0, rows), slice(0, cols))
                part = p_ref[index] if own is None else jnp.where(this_chip == q, own_ref[index], p_ref[index])
                g = part.astype(F32) if g is None else g + part.astype(F32)
            if transposed:
                g = g.T[:rows]
            g_out, d_out, m_out, v_out = outs[4 * k:4 * k + 4]
            g_out[...] = g
            d_out[...], m_out[...], v_out[...] = _adamw_math(w_ref[...], g, m_ref[...], v_ref[...])

    return _call(body, name, (steps,), in_specs, out_specs, out_shape, [], operands, ride)


SMALL = ("ssm_a_re", "ssm_a_im", "ssm_log_dt", "ssm_b_re", "ssm_b_im", "ssm_c_re", "ssm_c_im", "ssm_d",
         "ln1_g", "ln1_b", "ln2_g", "ln2_b")


def _pack_rows(arrays):
    rows = []
    for a in arrays:
        flat = a.reshape(-1)
        rows.append(jnp.pad(flat, (0, -flat.shape[0] % 128)).reshape(-1, 128))
    packed = jnp.concatenate(rows, axis=0)
    return jnp.pad(packed, ((0, -packed.shape[0] % 8), (0, 0)))


def _unpack_rows(packed, shapes):
    out, row = [], 0
    for shape in shapes:
        size = math.prod(shape)
        n_rows = -(-size // 128)
        out.append(packed[row:row + n_rows].reshape(-1)[:size].reshape(shape))
        row += n_rows
    return out


def _sum_devices(parts):
    def body(p_ref, o_ref):
        total = p_ref[0]
        for dev in range(1, N_DEV):
            total = total + p_ref[dev]
        o_ref[...] = total

    return _pallas_call(body, name="sum_devices", out_shape=_sds(parts.shape[1:], F32))(parts)


def _adamw_replicated(ws, ms, vs, gs):
    n = len(ws)

    def body(*refs):
        w_refs, m_refs, v_refs, g_refs, d_out, m_out, v_out = (refs[i * n:(i + 1) * n] for i in range(7))
        for i in range(n):
            d_out[i][...], m_out[i][...], v_out[i][...] = _adamw_math(w_refs[i][...], g_refs[i][...], m_refs[i][...], v_refs[i][...])

    out = _pallas_call(body, name="adamw_replicated", out_shape=[_sds(w.shape, F32) for w in ws] * 3,
                       compiler_params=_cparams())(*ws, *ms, *vs, *gs)
    return out[:n], out[n:2 * n], out[2 * n:]


def kernel(x, w_in, b_gate, w_attn_br, w_ssm_br, w_out, ssm_a_re, ssm_a_im, ssm_log_dt, ssm_b_re, ssm_b_im, ssm_c_re, ssm_c_im, ssm_d, w_glu, ln1_g, ln1_b, w_ff_gate, w_ff_up, w_ff_down, ln2_g, ln2_b, loss_target, m_w_in, m_b_gate, m_w_attn_br, m_w_ssm_br, m_w_out, m_ssm_a_re, m_ssm_a_im, m_ssm_log_dt, m_ssm_b_re, m_ssm_b_im, m_ssm_c_re, m_ssm_c_im, m_ssm_d, m_w_glu, m_ln1_g, m_ln1_b, m_w_ff_gate, m_w_ff_up, m_w_ff_down, m_ln2_g, m_ln2_b, v_w_in, v_b_gate, v_w_attn_br, v_w_ssm_br, v_w_out, v_ssm_a_re, v_ssm_a_im, v_ssm_log_dt, v_ssm_b_re, v_ssm_b_im, v_ssm_c_re, v_ssm_c_im, v_ssm_d, v_w_glu, v_ln1_g, v_ln1_b, v_w_ff_gate, v_w_ff_up, v_w_ff_down, v_ln2_g, v_ln2_b):
    given = dict(locals())
    x2, target = x[0], loss_target[0]
    core = lax.axis_index("c").astype(jnp.int32).reshape(1)

    sharded = ("w_in", "w_attn_br", "w_ssm_br", "w_glu", "w_ff_gate", "w_ff_up", "b_gate", "w_out", "w_ff_down")
    send_shape = dict(w_in=(D_MODEL, 896), w_attn_br=(ATTN_WIDTH, 128), w_ssm_br=(SSM_WIDTH, 128), w_glu=(SSM_WIDTH, 128),
                      w_out=(128, D_MODEL), w_ff_gate=(D_MODEL, FF_PAD), w_ff_up=(D_MODEL, FF_PAD), w_ff_down=(FF_PAD, D_MODEL))
    local = {k: given[k][0] for k in sharded}
    narrow = ("w_ff_gate", "w_ff_up")
    def to_send(k):
        return (local[k].T, True, *send_shape[k]) if k in narrow else (local[k], False, *send_shape[k])

    later = [k for k in sharded if k not in ("w_in", "b_gate")]
    sends = dict(zip(["w_in"] + later, _send_buffers([to_send("w_in")], "send_w_in")
                     + _send_buffers([to_send(k) for k in later], "send_weights")))
    sends["b_gate"] = local["b_gate"]
    mixer_weights = ("w_attn_br", "w_ssm_br", "w_glu", "b_gate", "w_out")
    ff_weights = ("w_ff_gate", "w_ff_up", "w_ff_down")
    wt = {}

    a_re, a_im, log_dt = ssm_a_re[0], ssm_a_im[0], ssm_log_dt[0].reshape(SSM_GROUPS, 1)
    b_re_t, b_im_t = ssm_b_re[0].transpose(0, 2, 1), ssm_b_im[0].transpose(0, 2, 1)
    abar_re, abar_im, e_re, e_im, bbar_re_t, bbar_im_t = _ssm_prep(a_re, a_im, log_dt, b_re_t, b_im_t)
    bmat, cmat, a_chunks = _ssm_tables(abar_re, abar_im, bbar_re_t, bbar_im_t, ssm_c_re[0], ssm_c_im[0])
    cos_t, sin_t = _rope_tables()

    big_mixer, ff_in = [k for k in mixer_weights if k != "b_gate"], ("w_ff_gate", "w_ff_up")
    n_mixer = len(big_mixer)
    mixer_1, mixer_2, mixer_3 = _relayed_gather([sends[k] for k in big_mixer])
    ff_in_1, ff_in_2, ff_in_3 = _relayed_gather([sends[k] for k in ff_in])
    ff_down_1, ff_down_2, ff_down_3 = _relayed_gather([sends["w_ff_down"]])
    w_in_1, w_in_2, w_in_3 = _relayed_gather([sends["w_in"]])
    ax, ay, ac = _place()
    early = jnp.stack([4 * ax + 2 * ay + ac, 4 * ax + 2 * ay + 1 - ac, 4 * (1 - ax) + 2 * ay + ac, 4 * ax + 2 * (1 - ay) + ac])
    late = jnp.stack([4 * (1 - ax) + 2 * ay + 1 - ac, 4 * ax + 2 * (1 - ay) + 1 - ac,
                      4 * (1 - ax) + 2 * (1 - ay) + ac, 4 * (1 - ax) + 2 * (1 - ay) + 1 - ac])
    w_in_partly, = _exchange(w_in_1, "gather_w_in_first")
    proj, w_in_partly = _proj_early(x2, w_in_partly, early.astype(jnp.int32), w_in_2([w_in_partly]))
    wt["w_in"], = _exchange(w_in_3([w_in_partly]), "gather_w_in_third")
    proj, *landed = _proj(x2, wt["w_in"], late.astype(jnp.int32), "proj_late", into=proj,
                          ride=mixer_1 + _gather_first_level([sends["b_gate"]]))
    mixer, bias = landed[:n_mixer], landed[n_mixer:]
    attn, lse, *landed = _attn_fwd(proj, cos_t, sin_t, mixer_2(mixer) + _gather_second_level(bias) + ff_in_1)
    mixer, b_gate_full, ff = landed[:n_mixer], landed[n_mixer], landed[n_mixer + 1:]
    ys, states, *landed = _ssm_fwd(proj, bmat, cmat, a_chunks, ssm_d, mixer_3(mixer) + ff_in_2(ff) + ff_down_1)
    wt.update(zip(big_mixer, landed[:n_mixer]))
    ff, ff_down = landed[n_mixer:n_mixer + 2], landed[n_mixer + 2:]
    wt["w_out"] = wt["w_out"].reshape(D_MODEL, D_MODEL)
    h, xhat1, rstd1, glu, y_attn, y_ssm, *landed = _mixer_out(
        attn, ys, proj, x2, wt["w_attn_br"], wt["w_ssm_br"], wt["w_glu"], wt["w_out"], b_gate_full, ln1_g, ln1_b,
        ff_in_3(ff) + ff_down_2(ff_down))
    wt.update(zip(ff_in, landed[:2]))
    ff_a, ff_b, ff_f, w_ff_down = _ff_up(h, wt["w_ff_gate"], wt["w_ff_up"], ff_down_3(landed[2:]))
    wt["w_ff_down"] = w_ff_down.reshape(D_FF_PAD, D_MODEL)
    dr2, d_ln2_g, d_ln2_b, loss_lanes = _ff_down_loss(ff_f, wt["w_ff_down"], h, target, ln2_g, ln2_b)

    def pair_sums(names, contrib, from_sibling):
        return _pair_sums([contrib[k] for k in names], from_sibling, core, "pair_sums_" + names[0])

    d_a, d_b = _ff_down_bwd(dr2, wt["w_ff_down"], ff_a, ff_b)
    contrib = dict(w_ff_gate=_weight_grad(h, d_a, "wgrad_w_ff_gate", FF_PAD),
                   w_ff_up=_weight_grad(h, d_b, "wgrad_w_ff_up", FF_PAD),
                   w_ff_down=_weight_grad(ff_f, dr2, "wgrad_w_ff_down"))
    dr1, d_ln1_g, d_ln1_b, *from_sibling = _ff_up_bwd(
        d_a, d_b, wt["w_ff_gate"], wt["w_ff_up"], dr2, xhat1, rstd1, ln1_g, _sibling_swap_ride([contrib[k] for k in ff_weights]))
    ff_sums = pair_sums(ff_weights, contrib, from_sibling)

    d_ya, d_yssm, d_proj, d_attn, d_glu, d_ys, mixed, y_s, gy, d_bg = _mixer_bwd(
        dr1, proj, y_attn, y_ssm, glu, ys, wt["w_attn_br"], wt["w_ssm_br"], wt["w_glu"], wt["w_out"], b_gate_full)
    contrib.update(w_attn_br=_weight_grad(attn, d_ya, "wgrad_w_attn_br", 128),
                   w_ssm_br=_weight_grad(y_s, d_yssm, "wgrad_w_ssm_br", 128),
                   w_glu=_weight_grad(gy, d_glu, "wgrad_w_glu", 128),
                   w_out=_weight_grad(mixed, dr1, "wgrad_w_out"),
                   b_gate=d_bg.reshape(2, 4, 2, 128).transpose(2, 1, 0, 3))
    d_proj, *landed = _attn_bwd(proj, cos_t, sin_t, attn, lse, d_attn, d_proj,
                                _chip_swap_ride(ff_sums) + _sibling_swap_ride([contrib[k] for k in mixer_weights]))
    parts, own_sums = dict(zip(ff_weights, landed[:len(ff_weights)])), {}
    mixer_sums = pair_sums(mixer_weights, contrib, landed[len(ff_weights):])
    d_proj, d_bmat, d_cmat, d_abar, d_skip, *landed = _ssm_bwd(d_ys, proj, states, bmat, cmat, a_chunks, ssm_d, d_proj,
                                                               _chip_swap_ride(mixer_sums))
    parts.update(zip(mixer_weights, landed))

    gbb_re_t, gbb_im_t = _block_diag_parts(d_bmat, True)
    gc_re, gc_im = _block_diag_parts(d_cmat, False)
    ga_re = d_abar[:, 0, :CHUNK_STATES].reshape(SSM_GROUPS, SSM_STATE)
    ga_im = d_abar[:, 0, CHUNK_STATES:].reshape(SSM_GROUPS, SSM_STATE)
    g_a_re, g_a_im, g_log_dt, g_b_re_t, g_b_im_t = _ssm_param_bwd(
        a_re, a_im, log_dt, b_re_t, b_im_t, abar_re, abar_im, e_re, e_im, ga_re, ga_im, gbb_re_t, gbb_im_t)
    mine = [g_a_re, g_a_im, g_log_dt, g_b_re_t, g_b_im_t, gc_re, -gc_im,
            d_skip, d_ln1_g, d_ln1_b, d_ln2_g, d_ln2_b]
    small_packed = _pack_rows(mine + [loss_lanes])

    contrib["w_in"], small_partly = _weight_grad(x2, d_proj, "wgrad_w_in", 896, _gather_first_level([small_packed]))
    from_sibling, every = _exchange(_sibling_swap_ride([contrib["w_in"]]) + _gather_second_level([small_partly]),
                                    "swap_w_in_with_sibling")
    w_in_sum, = pair_sums(["w_in"], contrib, [from_sibling])
    send_sems, recv_sems, w_in_sum, landing, token = _chip_swap_start([w_in_sum], "w_in_chip_swap_start")

    def adamw_of(k):
        taken = (lambda a: a.T) if k in narrow else (lambda a: a)
        return taken(local[k]), taken(given["m_" + k][0]), taken(given["v_" + k][0]), parts[k], own_sums.get(k), k in narrow

    others = [k for k in sharded if k != "w_in"]
    updated = _adamw_many([adamw_of(k) for k in others], "adamw_others", _after(token))
    grad_x, = _grad_x(d_proj, wt["w_in"], dr1, _after(token))

    def held(k, a):
        return a.transpose(0, 1, 3, 2) if k in ("ssm_b_re", "ssm_b_im") else a

    *small_grads, loss_sum = _unpack_rows(_sum_devices(every), [held(k, given[k]).shape for k in SMALL] + [(1, 128)])
    small = _adamw_replicated([held(k, given[k]) for k in SMALL], [held(k, given["m_" + k]) for k in SMALL],
                              [held(k, given["v_" + k]) for k in SMALL], small_grads)
    loss = loss_sum[0, 0]

    (own_sums["w_in"],), (parts["w_in"],) = _chip_swap_wait(
        send_sems, recv_sems, w_in_sum, landing, [grad_x, updated[0], small[0][0]], "w_in_chip_swap_wait")
    updated += _adamw_many([adamw_of("w_in")], "adamw_w_in")

    grads, deltas, new_m, new_v = {}, {}, {}, {}
    for i, k in enumerate(others + ["w_in"]):
        out = [o.T if k in narrow else o for o in updated[4 * i:4 * i + 4]]
        grads[k], deltas[k], new_m[k], new_v[k] = (o.reshape((1,) + local[k].shape) for o in out)
    for res, values in zip((grads, deltas, new_m, new_v), (small_grads,) + small):
        res.update((k, held(k, a)) for k, a in zip(SMALL, values))

    order = ("w_in", "b_gate", "w_attn_br", "w_ssm_br", "w_out", "ssm_a_re", "ssm_a_im", "ssm_log_dt", "ssm_b_re", "ssm_b_im",
             "ssm_c_re", "ssm_c_im", "ssm_d", "w_glu", "ln1_g", "ln1_b", "w_ff_gate", "w_ff_up", "w_ff_down", "ln2_g", "ln2_b")
    return (loss, grad_x[None], *[grads[k] for k in order], *[deltas[k] for k in order], *[new_m[k] for k in order],
            *[new_v[k] for k in order])
```

```python
import functools
import math

import jax
import jax.numpy as jnp
import numpy as np
from jax import lax
from jax.experimental import pallas as pl
from jax.experimental.pallas import tpu as pltpu

F32 = jnp.float32
BF16 = jnp.bfloat16

N_DEV = 8
SEQ = 2048
D_MODEL = 1024
HEAD_DIM = 64
ATTN_WIDTH = 512
QKV_WIDTH = 1536
SSM_WIDTH = 512
SSM_GROUPS = 32
SSM_GROUP = 16
SSM_STATE = 64
IN_WIDTH = 7168
D_FF = 2816
FF_SHARD = D_FF // N_DEV
FF_PAD = 384
D_FF_PAD = FF_PAD * N_DEV
DN_ALPHA = 2.0 ** 0.25
LN_EPS = 1e-5
NEG_INF = -1e30
ROPE_THETA = 10000.0
BLOCK = 128
GROUPS = ((1, 16), (4, 4), (16, 1))

ADAM_LR = 0.001
ADAM_B1 = 0.9
ADAM_B2 = 0.999
ADAM_EPS = 1e-08
ADAM_WD = 0.01
ADAM_STEP = 10

VMEM_LIMIT = 56 * 1024 * 1024


_pallas_call = pl.pallas_call


def _cparams(**kw):
    return pltpu.CompilerParams(vmem_limit_bytes=VMEM_LIMIT, **kw)


def _dot(a, b):
    return jnp.dot(a, b, preferred_element_type=F32)


def _dot_nt(a, b):
    return lax.dot_general(a, b, (((1,), (1,)), ((), ())), preferred_element_type=F32)


def _side_by_side(w_ref, row=None):
    rows = slice(None) if row is None else pl.ds(row, 1)
    return jnp.concatenate([w_ref[i, rows, :] for i in range(w_ref.shape[0])], axis=1)


def _dot_tn(a, b):
    return lax.dot_general(a, b, (((0,), (0,)), ((), ())), preferred_element_type=F32)


def _rope_tables():
    half = HEAD_DIM // 2
    inv_freq = np.float32(ROPE_THETA) ** (-np.arange(half, dtype=np.float32) / np.float32(half))
    ang = np.arange(SEQ, dtype=np.float32)[:, None] * inv_freq[None, :]
    cos, sin = np.cos(ang).astype(np.float32), np.sin(ang).astype(np.float32)
    tables = np.tile(cos, (1, 4)), np.tile(np.concatenate([-sin, sin], axis=1), (1, 2))

    def by_phase(t):
        return np.stack([t.reshape(SEQ // d, d, 128).transpose(1, 0, 2).reshape(SEQ, 128) for d, _ in GROUPS])

    return jnp.asarray(by_phase(tables[0])), jnp.asarray(by_phase(tables[1]))


def _swap_halves(x):
    lane = lax.broadcasted_iota(jnp.int32, x.shape, 1)
    return jnp.where((lane & 63) < 32, pltpu.roll(x, 96, axis=1), pltpu.roll(x, 32, axis=1))


def _group_rows(d, nb, r, i):
    src = pl.ds(i * BLOCK, BLOCK) if d == 1 else pl.ds(r + i * BLOCK * d, BLOCK, stride=d)
    return src, pl.ds((r * nb + i) * BLOCK, BLOCK)


def _attn_masks():
    a_idx = lax.broadcasted_iota(jnp.int32, (2 * BLOCK, 2 * BLOCK), 0) & (BLOCK - 1)
    c_idx = lax.broadcasted_iota(jnp.int32, (2 * BLOCK, 2 * BLOCK), 1)
    cur_ok = jnp.logical_and(c_idx >= BLOCK, c_idx - BLOCK <= a_idx)
    prev_ok = jnp.logical_and(c_idx < BLOCK, c_idx >= a_idx)
    lane = lax.broadcasted_iota(jnp.int32, (BLOCK, 128), 1)
    return cur_ok, prev_ok, lane < HEAD_DIM


def _stack_heads(t, head0):
    zero = jnp.zeros_like(t)
    return jnp.concatenate([jnp.where(head0, t, zero), jnp.where(head0, zero, t)], axis=0)


def _unstack_heads(t2, head0):
    return jnp.where(head0, t2[:BLOCK], t2[BLOCK:])


def _attn_fwd(proj, cos_t, sin_t, ride=None):
    def body(q0, q1, q2, k0, k1, k2, v0, v1, v2, cos_ref, sin_ref, attn_ref, lse_ref, qpm_ref, kpm_ref, vpm_ref,
             qs, ks, vs, os_, ms, ls, acc, mnat, lnat):
        cur_ok, prev_ok, head0 = _attn_masks()
        ks[:BLOCK, :] = jnp.zeros((BLOCK, 128), BF16)
        vs[:BLOCK, :] = jnp.zeros((BLOCK, 128), BF16)
        for g, (d, nb) in enumerate(GROUPS):
            q_ref, k_ref, v_ref = (q0, q1, q2)[g], (k0, k1, k2)[g], (v0, v1, v2)[g]
            for r in range(d):
                for i in range(nb):
                    src, dst = _group_rows(d, nb, r, i)
                    below = pl.ds(dst.start + BLOCK, BLOCK)
                    c, s = cos_ref[g, dst, :], sin_ref[g, dst, :]
                    q = q_ref[src, :]
                    k = k_ref[src, :]
                    qs[dst, :] = ((q * c + _swap_halves(q) * s) * 0.125).astype(BF16)
                    ks[below, :] = (k * c + _swap_halves(k) * s).astype(BF16)
                    vs[below, :] = v_ref[src, :].astype(BF16)
                    qpm_ref[g, dst, :], kpm_ref[g, dst, :], vpm_ref[g, dst, :] = qs[dst, :], ks[below, :], vs[below, :]

            def block(b, carry, nb=nb):
                has_prev = (b & (nb - 1)) > 0
                cur = pl.ds(pl.multiple_of(b * BLOCK, BLOCK), BLOCK)
                window = pl.ds(pl.multiple_of(b * BLOCK, BLOCK), 2 * BLOCK)
                valid = jnp.logical_or(cur_ok, jnp.logical_and(prev_ok, has_prev))
                s = jnp.where(valid, _dot_nt(_stack_heads(qs[cur, :], head0), ks[window, :]), NEG_INF)
                m = jnp.max(s, axis=1, keepdims=True)
                p = jnp.exp(s - m)
                os_[cur, :] = _unstack_heads(_dot(p.astype(BF16), vs[window, :]), head0)
                ms[cur, :] = _unstack_heads(m, head0)
                ls[cur, :] = _unstack_heads(jnp.sum(p, axis=1, keepdims=True), head0)
                return carry

            lax.fori_loop(0, SEQ // BLOCK, block, 0, unroll=16)

            for r in range(d):
                for i in range(nb):
                    src, dst = _group_rows(d, nb, r, i)
                    if g == 0:
                        acc[src, :], mnat[src, :], lnat[src, :] = os_[dst, :], ms[dst, :], ls[dst, :]
                    else:
                        m_old, m_g = mnat[src, :], ms[dst, :]
                        m_new = jnp.maximum(m_old, m_g)
                        a_old, a_g = jnp.exp(m_old - m_new), jnp.exp(m_g - m_new)
                        acc[src, :] = a_old * acc[src, :] + a_g * os_[dst, :]
                        lnat[src, :] = a_old * lnat[src, :] + a_g * ls[dst, :]
                        mnat[src, :] = m_new
        for i in range(SEQ // BLOCK):
            rows = pl.ds(i * BLOCK, BLOCK)
            l = lnat[rows, :]
            attn_ref[rows, :] = acc[rows, :] / l
            lse_ref[rows, :] = mnat[rows, :] + jnp.log(l)

    def col(base):
        return pl.BlockSpec((SEQ, 128), lambda hp, base=base: (0, base + hp))

    in_specs = [col(g * 4) for g in range(3)] + [col(12 + g * 4) for g in range(3)] + [col(24 + g * 4) for g in range(3)]
    table = pl.BlockSpec((3, SEQ, 128), lambda hp: (0, 0, 0), pipeline_mode=pl.Buffered(1))
    out = pl.BlockSpec((SEQ, 128), lambda hp: (0, hp))
    by_phase = pl.BlockSpec((3, SEQ, 128), lambda hp: (0, 0, hp))
    return _call(
        body, "attn_fwd", (4,), in_specs + [table, table], [out, out] + [by_phase] * 3,
        [_sds((SEQ, ATTN_WIDTH), F32), _sds((SEQ, ATTN_WIDTH), F32)] + [_sds((3, SEQ, ATTN_WIDTH), BF16)] * 3,
        [pltpu.VMEM((SEQ, 128), BF16)] + [pltpu.VMEM((SEQ + BLOCK, 128), BF16)] * 2 + [pltpu.VMEM((SEQ, 128), F32)] * 6,
        [proj] * 9 + [cos_t, sin_t], ride)


def _attn_bwd_group_body(g):
    d, nb = GROUPS[g]

    def body(qs_ref, ks_ref, vs_ref, cos_ref, sin_ref, lse_ref, dattn_ref, dsum_ref, dproj_ref,
             ks, vs, dos, lss, dss, dqs, dks, dvs, stage, outs, sems):
        cur_ok, prev_ok, head0 = _attn_masks()
        qs = qs_ref.at[g]
        ks[:BLOCK, :] = jnp.zeros((BLOCK, 128), BF16)
        vs[:BLOCK, :] = jnp.zeros((BLOCK, 128), BF16)
        dks[:BLOCK, :] = jnp.zeros((BLOCK, 128), F32)
        dvs[:BLOCK, :] = jnp.zeros((BLOCK, 128), F32)
        for r in range(d):
            for i in range(nb):
                src, dst = _group_rows(d, nb, r, i)
                below = pl.ds(dst.start + BLOCK, BLOCK)
                ks[below, :] = ks_ref[g, dst, :]
                vs[below, :] = vs_ref[g, dst, :]
                dos[dst, :] = dattn_ref[src, :].astype(BF16)
                dss[dst, :] = dsum_ref[src, :]
                lss[dst, :] = lse_ref[src, :]
                dks[below, :] = jnp.zeros((BLOCK, 128), F32)
                dvs[below, :] = jnp.zeros((BLOCK, 128), F32)

        def per_head_column(t):
            return jnp.concatenate([jnp.max(jnp.where(head0, t, NEG_INF), axis=1, keepdims=True),
                                    jnp.max(jnp.where(head0, NEG_INF, t), axis=1, keepdims=True)], axis=0)

        def block(b, carry):
            has_prev = (b & (nb - 1)) > 0
            cur = pl.ds(pl.multiple_of(b * BLOCK, BLOCK), BLOCK)
            window = pl.ds(pl.multiple_of(b * BLOCK, BLOCK), 2 * BLOCK)
            valid = jnp.logical_or(cur_ok, jnp.logical_and(prev_ok, has_prev))
            q2, do2 = _stack_heads(qs[cur, :], head0), _stack_heads(dos[cur, :], head0)
            kw, vw = ks[window, :], vs[window, :]
            s = jnp.where(valid, _dot_nt(q2, kw), NEG_INF)
            p = jnp.exp(s - per_head_column(lss[cur, :]))
            ds = (p * (_dot_nt(do2, vw) - per_head_column(dss[cur, :]))).astype(BF16)
            dvs[window, :] += _dot_tn(p.astype(BF16), do2)
            dks[window, :] += _dot_tn(ds, q2)
            dqs[cur, :] = _unstack_heads(_dot(ds, kw), head0)
            return carry

        lax.fori_loop(0, SEQ // BLOCK, block, 0, unroll=8)

        hp = pl.program_id(0)
        copies = []
        for kind in range(3):
            for r in range(d):
                for i in range(nb):
                    src, dst = _group_rows(d, nb, r, i)
                    below = pl.ds(dst.start + BLOCK, BLOCK)
                    if kind == 2:
                        stage[src, :] = dvs[below, :]
                    else:
                        c, s = cos_ref[g, dst, :], sin_ref[g, dst, :]
                        t = dqs[dst, :] * 0.125 if kind == 0 else dks[below, :]
                        stage[src, :] = t * c - _swap_halves(t) * s
            for i in range(SEQ // MM_ROWS):
                rows = pl.ds(i * MM_ROWS, MM_ROWS)
                outs[kind, rows, :] = stage[rows, :].astype(BF16)
            column = pl.multiple_of((kind * 12 + g * 4 + hp) * 128, 128)
            copies.append(pltpu.make_async_copy(outs.at[kind], dproj_ref.at[:, pl.ds(column, 128)], sems.at[kind]))
            copies[-1].start()
        for cp in copies:
            cp.wait()

    return body


def _attn_bwd(q_pm, k_pm, v_pm, cos_t, sin_t, attn, lse, dattn, dproj, ride=None):
    groups = [_attn_bwd_group_body(g) for g in range(3)]

    def body(qs_ref, ks_ref, vs_ref, cos_ref, sin_ref, attn_ref, lse_ref, dattn_ref, dproj_in, dproj_ref, dsum, *scratch):
        del dproj_in
        head0 = _attn_masks()[2]
        for i in range(SEQ // BLOCK):
            rows = pl.ds(i * BLOCK, BLOCK)
            prod = dattn_ref[rows, :] * attn_ref[rows, :]
            d0 = jnp.sum(jnp.where(head0, prod, 0.0), axis=1, keepdims=True)
            d1 = jnp.sum(jnp.where(head0, 0.0, prod), axis=1, keepdims=True)
            dsum[rows, :] = jnp.where(head0, d0, d1)
        for g in range(3):
            groups[g](qs_ref, ks_ref, vs_ref, cos_ref, sin_ref, lse_ref, dattn_ref, dsum, dproj_ref, *scratch)

    def col(base):
        return pl.BlockSpec((SEQ, 128), lambda hp, base=base: (0, base + hp))

    table = pl.BlockSpec((3, SEQ, 128), lambda hp: (0, 0, 0), pipeline_mode=pl.Buffered(1))
    by_phase = pl.BlockSpec((3, SEQ, 128), lambda hp: (0, 0, hp))
    return _call(
        body, "attn_bwd", (4,), [by_phase] * 3 + [table, table, col(0), col(0), col(0), ANY],
        [ANY], [_sds((SEQ, IN_WIDTH), BF16)],
        [pltpu.VMEM((SEQ, 128), F32)]
        + [pltpu.VMEM((SEQ + BLOCK, 128), BF16)] * 2 + [pltpu.VMEM((SEQ, 128), BF16)]
        + [pltpu.VMEM((SEQ, 128), F32)] * 3 + [pltpu.VMEM((SEQ + BLOCK, 128), F32)] * 2 + [pltpu.VMEM((SEQ, 128), F32)]
        + [pltpu.VMEM((3, SEQ, 128), BF16), pltpu.SemaphoreType.DMA((3,))],
        [q_pm, k_pm, v_pm, cos_t, sin_t, attn, lse, dattn, dproj], ride, aliases={8: 0})


SSM_CHUNKS = 4
CHUNK_STATES = 512
SCAN_ROWS = 8
U_COL = (3 * QKV_WIDTH) // 128


def _cmul(xr, xi, yr, yi):
    return xr * yr - xi * yi, xr * yi + xi * yr


def _ssm_prep(a_re, a_im, log_dt, b_re_t, b_im_t):
    def body(ar_ref, ai_ref, ldt_ref, br_ref, bi_ref, abr_ref, abi_ref, er_ref, ei_ref, bbr_ref, bbi_ref):
        ar, ai = ar_ref[...], ai_ref[...]
        dt = jnp.exp(ldt_ref[...])
        mag = jnp.exp(ar * dt)
        abr, abi = mag * jnp.cos(ai * dt), mag * jnp.sin(ai * dt)
        den = ar * ar + ai * ai
        nr, ni = abr - 1.0, abi
        er, ei = (nr * ar + ni * ai) / den, (ni * ar - nr * ai) / den
        abr_ref[...], abi_ref[...], er_ref[...], ei_ref[...] = abr, abi, er, ei
        er3, ei3 = er[:, None, :], ei[:, None, :]
        br, bi = br_ref[...], bi_ref[...]
        bbr_ref[...] = er3 * br - ei3 * bi
        bbi_ref[...] = er3 * bi + ei3 * br

    gp = jax.ShapeDtypeStruct(a_re.shape, F32)
    gb = jax.ShapeDtypeStruct(b_re_t.shape, F32)
    return _pallas_call(body, name="ssm_prep", out_shape=(gp, gp, gp, gp, gb, gb))(a_re, a_im, log_dt, b_re_t, b_im_t)


def _ssm_param_bwd(a_re, a_im, log_dt, b_re_t, b_im_t, abar_re, abar_im, e_re, e_im, ga_re, ga_im, gbb_re_t, gbb_im_t):
    def body(ar_ref, ai_ref, ldt_ref, br_ref, bi_ref, abr_ref, abi_ref, er_ref, ei_ref, gar_ref, gai_ref, gbr_ref, gbi_ref,
             o_ar, o_ai, o_ldt, o_br, o_bi):
        ar, ai = ar_ref[...], ai_ref[...]
        dt = jnp.exp(ldt_ref[...])
        er, ei = er_ref[...], ei_ref[...]
        br, bi, gbr, gbi = br_ref[...], bi_ref[...], gbr_ref[...], gbi_ref[...]
        er3, ei3 = er[:, None, :], ei[:, None, :]
        o_br[...] = er3 * gbr + ei3 * gbi
        o_bi[...] = er3 * gbi - ei3 * gbr
        ge_r = jnp.sum(br * gbr + bi * gbi, axis=1)
        ge_i = jnp.sum(br * gbi - bi * gbr, axis=1)
        den = ar * ar + ai * ai
        ilr, ili = ar / den, -ai / den
        t_r, t_i = _cmul(ilr, -ili, ge_r, ge_i)
        gab_r, gab_i = gar_ref[...] + t_r, gai_ref[...] + t_i
        gz_r, gz_i = _cmul(abr_ref[...], -abi_ref[...], gab_r, gab_i)
        el_r, el_i = _cmul(er, ei, ilr, ili)
        u_r, u_i = _cmul(el_r, -el_i, ge_r, ge_i)
        o_ar[...] = dt * gz_r - u_r
        o_ai[...] = dt * gz_i - u_i
        o_ldt[...] = jnp.sum(gz_r * ar + gz_i * ai, axis=1, keepdims=True) * dt

    gp = jax.ShapeDtypeStruct(a_re.shape, F32)
    gb = jax.ShapeDtypeStruct(b_re_t.shape, F32)
    return _pallas_call(body, name="ssm_param_bwd", out_shape=(gp, gp, jax.ShapeDtypeStruct(log_dt.shape, F32), gb, gb))(
        a_re, a_im, log_dt, b_re_t, b_im_t, abar_re, abar_im, e_re, e_im, ga_re, ga_im, gbb_re_t, gbb_im_t)


def _block_diag(blocks_re, blocks_im, sign_im, rows_are_channels):
    both = jnp.stack([blocks_re, sign_im * blocks_im]).reshape(2, SSM_CHUNKS, 8, SSM_GROUP, SSM_STATE)
    eye = jnp.eye(8, dtype=F32)
    if rows_are_channels:
        return jnp.einsum("rcghp,gk->cghrkp", both, eye).reshape(SSM_CHUNKS, 128, 2 * CHUNK_STATES)
    return jnp.einsum("rcghp,gk->crkpgh", both, eye).reshape(SSM_CHUNKS, 2 * CHUNK_STATES, 128)


def _block_diag_parts(mat, rows_are_channels):
    if rows_are_channels:
        six = mat.reshape(SSM_CHUNKS, 8, SSM_GROUP, 2, 8, SSM_STATE)
        parts = jnp.einsum("cghrgp->rcghp", six)
    else:
        six = mat.reshape(SSM_CHUNKS, 2, 8, SSM_STATE, 8, SSM_GROUP)
        parts = jnp.einsum("crgpgh->rcghp", six)
    parts = parts.reshape(2, SSM_GROUPS, SSM_GROUP, SSM_STATE)
    return parts[0], parts[1]


def _scan_consts(a_ref, conj, reverse):
    ar = jnp.broadcast_to(a_ref[:, :CHUNK_STATES], (SCAN_ROWS, CHUNK_STATES))
    ai = jnp.broadcast_to(a_ref[:, CHUNK_STATES:], (SCAN_ROWS, CHUNK_STATES))
    if conj:
        ai = -ai
    row = lax.broadcasted_iota(jnp.int32, (SCAN_ROWS, CHUNK_STATES), 0)
    if reverse:
        row = SCAN_ROWS - 1 - row
    zero = jnp.zeros_like(ar)
    steps = []
    pr, pi = ar, ai
    for shift in (1, 2, 4):
        keep = row >= shift
        steps.append((SCAN_ROWS - shift if reverse else shift, jnp.where(keep, pr, zero), jnp.where(keep, pi, zero)))
        pr, pi = _cmul(pr, pi, pr, pi)
    first = row == 0
    return steps, (jnp.where(first, ar, zero), jnp.where(first, ai, zero)), first


def _scan_tile(xr, xi, prev_r, prev_i, steps, carry_in, reverse):
    edge = SCAN_ROWS - 1 if reverse else 1
    cr, ci = pltpu.roll(prev_r, edge, axis=0), pltpu.roll(prev_i, edge, axis=0)
    xr, xi = xr + carry_in[0] * cr - carry_in[1] * ci, xi + carry_in[0] * ci + carry_in[1] * cr
    for shift, mr, mi in steps:
        sr, si = pltpu.roll(xr, shift, axis=0), pltpu.roll(xi, shift, axis=0)
        xr, xi = xr + mr * sr - mi * si, xi + mr * si + mi * sr
    return xr, xi


MM_ROWS = 256


def _ssm_fwd(proj, bmat, cmat, a_chunks, d_skip, ride=None):
    def body(u_ref, b_ref, c_ref, a_ref, d_ref, y_ref, states_ref, h_ref):
        for i in range(SEQ // MM_ROWS):
            rows = pl.ds(i * MM_ROWS, MM_ROWS)
            h_ref[rows, :] = _dot(u_ref[rows, :].astype(BF16), b_ref[...])
        steps, carry_in, _ = _scan_consts(a_ref, conj=False, reverse=False)

        def tile(k, carry):
            rows = pl.ds(pl.multiple_of(k * SCAN_ROWS, SCAN_ROWS), SCAN_ROWS)
            xr, xi = _scan_tile(h_ref[rows, :CHUNK_STATES], h_ref[rows, CHUNK_STATES:], carry[0], carry[1], steps, carry_in, False)
            h_ref[rows, :CHUNK_STATES] = xr
            h_ref[rows, CHUNK_STATES:] = xi
            return xr, xi

        zero = jnp.zeros((SCAN_ROWS, CHUNK_STATES), F32)
        lax.fori_loop(0, SEQ // SCAN_ROWS, tile, (zero, zero), unroll=4)
        for i in range(SEQ // MM_ROWS):
            rows = pl.ds(i * MM_ROWS, MM_ROWS)
            states = h_ref[rows, :].astype(BF16)
            states_ref[rows, :] = states
            y_ref[rows, :] = _dot(states, c_ref[...]) + d_ref[...] * u_ref[rows, :]

    return _call(
        body, "ssm_fwd", (SSM_CHUNKS,),
        [pl.BlockSpec((SEQ, 128), lambda c: (0, U_COL + c)),
         pl.BlockSpec((None, 128, 2 * CHUNK_STATES), lambda c: (c, 0, 0)),
         pl.BlockSpec((None, 2 * CHUNK_STATES, 128), lambda c: (c, 0, 0)),
         pl.BlockSpec((None, 1, 2 * CHUNK_STATES), lambda c: (c, 0, 0)),
         pl.BlockSpec((1, 128), lambda c: (0, c))],
        [pl.BlockSpec((SEQ, 128), lambda c: (0, c)), pl.BlockSpec((SEQ, 2 * CHUNK_STATES), lambda c: (0, c))],
        [_sds((SEQ, SSM_WIDTH), F32), _sds((SEQ, SSM_CHUNKS * 2 * CHUNK_STATES), BF16)],
        [pltpu.VMEM((SEQ, 2 * CHUNK_STATES), F32)],
        [proj, bmat, cmat, a_chunks, d_skip], ride)


def _ssm_bwd(dys, proj, h, bmat, cmat, a_chunks, d_skip, dproj, ride=None):
    def body(dy_ref, u_ref, states_ref, b_ref, c_ref, a_ref, d_ref, dproj_in, du_ref, db_ref, dc_ref, da_ref, dd_ref,
             g_ref, h_ref):
        del dproj_in
        dsum = jnp.zeros((1, 128), F32)
        dcm = jnp.zeros((2 * CHUNK_STATES, 128), F32)
        for i in range(SEQ // MM_ROWS):
            rows = pl.ds(i * MM_ROWS, MM_ROWS)
            h_ref[rows, :] = states_ref[rows, :].astype(F32)
            dy = dy_ref[rows, :]
            g_ref[rows, :] = _dot_nt(dy.astype(BF16), c_ref[...])
            dsum += jnp.sum(dy * u_ref[rows, :], axis=0, keepdims=True)
            dcm += _dot_tn(states_ref[rows, :], dy.astype(BF16))
        dd_ref[...] = dsum
        dc_ref[...] = dcm
        steps, carry_in, _ = _scan_consts(a_ref, conj=True, reverse=True)
        first_row = lax.broadcasted_iota(jnp.int32, (SCAN_ROWS, CHUNK_STATES), 0) == 0
        n_tiles = SEQ // SCAN_ROWS

        def tile(j, carry):
            k = n_tiles - 1 - j
            rows = pl.ds(pl.multiple_of(k * SCAN_ROWS, SCAN_ROWS), SCAN_ROWS)
            before = pl.ds(pl.multiple_of(jnp.maximum(k - 1, 0) * SCAN_ROWS, SCAN_ROWS), SCAN_ROWS)
            gr, gi = _scan_tile(g_ref[rows, :CHUNK_STATES], g_ref[rows, CHUNK_STATES:], carry[0], carry[1], steps, carry_in, True)
            g_ref[rows, :CHUNK_STATES] = gr
            g_ref[rows, CHUNK_STATES:] = gi
            has_before = jnp.where(k > 0, 1.0, 0.0)
            hr = jnp.where(first_row, pltpu.roll(h_ref[before, :CHUNK_STATES], 1, axis=0) * has_before,
                           pltpu.roll(h_ref[rows, :CHUNK_STATES], 1, axis=0))
            hi = jnp.where(first_row, pltpu.roll(h_ref[before, CHUNK_STATES:], 1, axis=0) * has_before,
                           pltpu.roll(h_ref[rows, CHUNK_STATES:], 1, axis=0))
            return gr, gi, carry[2] + hr * gr + hi * gi, carry[3] + hr * gi - hi * gr

        zero = jnp.zeros((SCAN_ROWS, CHUNK_STATES), F32)
        _, _, sar, sai = lax.fori_loop(0, n_tiles, tile, (zero, zero, zero, zero), unroll=4)
        da_ref[:, :CHUNK_STATES] = jnp.sum(sar, axis=0, keepdims=True)
        da_ref[:, CHUNK_STATES:] = jnp.sum(sai, axis=0, keepdims=True)
        dbm = jnp.zeros((128, 2 * CHUNK_STATES), F32)
        for i in range(SEQ // MM_ROWS):
            rows = pl.ds(i * MM_ROWS, MM_ROWS)
            g = g_ref[rows, :].astype(BF16)
            du_ref[rows, :] = (_dot_nt(g, b_ref[...]) + d_ref[...] * dy_ref[rows, :]).astype(BF16)
            dbm += _dot_tn(u_ref[rows, :].astype(BF16), g)
        db_ref[...] = dbm

    chunk_col = pl.BlockSpec((SEQ, 128), lambda c: (0, c))
    return _call(
        body, "ssm_bwd", (SSM_CHUNKS,),
        [chunk_col,
         pl.BlockSpec((SEQ, 128), lambda c: (0, U_COL + c)),
         pl.BlockSpec((SEQ, 2 * CHUNK_STATES), lambda c: (0, c)),
         pl.BlockSpec((None, 128, 2 * CHUNK_STATES), lambda c: (c, 0, 0)),
         pl.BlockSpec((None, 2 * CHUNK_STATES, 128), lambda c: (c, 0, 0)),
         pl.BlockSpec((None, 1, 2 * CHUNK_STATES), lambda c: (c, 0, 0)),
         pl.BlockSpec((1, 128), lambda c: (0, c)), ANY],
        [pl.BlockSpec((SEQ, 128), lambda c: (0, U_COL + c)),
         pl.BlockSpec((None, 128, 2 * CHUNK_STATES), lambda c: (c, 0, 0)),
         pl.BlockSpec((None, 2 * CHUNK_STATES, 128), lambda c: (c, 0, 0)),
         pl.BlockSpec((None, 1, 2 * CHUNK_STATES), lambda c: (c, 0, 0)),
         pl.BlockSpec((1, 128), lambda c: (0, c))],
        [_sds((SEQ, IN_WIDTH), BF16), _sds((SSM_CHUNKS, 128, 2 * CHUNK_STATES), F32),
         _sds((SSM_CHUNKS, 2 * CHUNK_STATES, 128), F32), _sds((SSM_CHUNKS, 1, 2 * CHUNK_STATES), F32), _sds((1, SSM_WIDTH), F32)],
        [pltpu.VMEM((SEQ, 2 * CHUNK_STATES), F32)] * 2, [dys, proj, h, bmat, cmat, a_chunks, d_skip, dproj], ride, aliases={7: 0})


def _ssm_tables(abar_re, abar_im, bbar_re_t, bbar_im_t, c_re, c_im):
    bmat = _block_diag(bbar_re_t, bbar_im_t, 1.0, True).astype(BF16)
    cmat = _block_diag(c_re, c_im, -1.0, False).astype(BF16)
    a_chunks = jnp.concatenate([abar_re.reshape(SSM_CHUNKS, 1, CHUNK_STATES), abar_im.reshape(SSM_CHUNKS, 1, CHUNK_STATES)], axis=2)
    return bmat, cmat, a_chunks


GL_COL = (3 * QKV_WIDTH + SSM_WIDTH) // D_MODEL
GELU_C = math.sqrt(2.0 / math.pi)
GELU_A = 0.044715


def _sds(shape, dtype):
    return jax.ShapeDtypeStruct(shape, dtype)


def _gelu(x):
    t = jnp.tanh(GELU_C * (x + GELU_A * x * x * x))
    return 0.5 * x * (1.0 + t), t


def _gelu_grad(x, t):
    return 0.5 * (1.0 + t) + 0.5 * x * (1.0 - t * t) * GELU_C * (1.0 + 3.0 * GELU_A * x * x)


def _layer_norm(r, g, b):
    mu = jnp.mean(r, axis=-1, keepdims=True)
    xc = r - mu
    rstd = lax.rsqrt(jnp.mean(xc * xc, axis=-1, keepdims=True) + LN_EPS)
    xhat = xc * rstd
    return xhat * g + b, xhat, rstd


def _layer_norm_bwd(dy, xhat, rstd, g):
    dxhat = dy * g
    m1 = jnp.mean(dxhat, axis=-1, keepdims=True)
    m2 = jnp.mean(dxhat * xhat, axis=-1, keepdims=True)
    return rstd * (dxhat - m1 - xhat * m2)


def _proj(x, w_in, ride=None):
    tm, tn = 1024, 1792

    def body(x_ref, w_ref, o_ref):
        o_ref[...] = _dot(x_ref[...].astype(BF16), _side_by_side(w_ref))

    return _call(
        body, "proj", (SEQ // tm, IN_WIDTH // tn),
        [pl.BlockSpec((tm, D_MODEL), lambda i, j: (i, 0)), pl.BlockSpec((2, D_MODEL, tn // 2), lambda i, j: (j, 0, 0))],
        [pl.BlockSpec((tm, tn), lambda i, j: (i, j))], [_sds((SEQ, IN_WIDTH), F32)], [], [x, w_in], ride)


def _row_spec(tm, width, col=0):
    return pl.BlockSpec((tm, width), lambda i, col=col: (i, col))


def _full_spec(shape):
    return pl.BlockSpec(shape, lambda i: (0,) * len(shape))


def _weight_spec(shape):
    return pl.BlockSpec(shape, lambda i: (0,) * len(shape), pipeline_mode=pl.Buffered(1))


def _mixer_out(attn, ys, proj, x, w_ab, w_sb, w_glu, w_out, b_gate, ln_g, ln_b, ride=None):
    tm = 512

    def body(attn_ref, ys_ref, gl0_ref, gl1_ref, x_ref, wab_ref, wsb_ref, wglu_ref, wout_ref, bg_ref, g_ref, b_ref,
             h_ref, xhat_ref, rstd_ref, glu_ref, ya_ref, yssm_ref):
        gy, _ = _gelu(ys_ref[...])
        glu = _dot(gy.astype(BF16), _side_by_side(wglu_ref))
        glu_ref[...] = glu.astype(BF16)
        y_s = glu[:, :SSM_WIDTH] * jax.nn.sigmoid(glu[:, SSM_WIDTH:])
        y_ssm = _dot(y_s.astype(BF16), _side_by_side(wsb_ref))
        y_attn = _dot(attn_ref[...].astype(BF16), _side_by_side(wab_ref))
        ya_ref[...] = y_attn.astype(BF16)
        yssm_ref[...] = y_ssm.astype(BF16)
        g0 = jax.nn.sigmoid(gl0_ref[...] + _side_by_side(bg_ref, 0))
        g1 = jax.nn.sigmoid(gl1_ref[...] + _side_by_side(bg_ref, 1))
        mixed = g0 * y_attn + g1 * y_ssm
        r1 = DN_ALPHA * x_ref[...] + _dot(mixed.astype(BF16), wout_ref[...])
        h, xhat, rstd = _layer_norm(r1, g_ref[...], b_ref[...])
        h_ref[...] = h
        xhat_ref[...] = xhat
        rstd_ref[...] = jnp.broadcast_to(rstd, (tm, 128))

    wide = _sds((SEQ, D_MODEL), F32)
    return _call(
        body, "mixer_out", (SEQ // tm,),
        [_row_spec(tm, ATTN_WIDTH), _row_spec(tm, SSM_WIDTH), _row_spec(tm, D_MODEL, GL_COL), _row_spec(tm, D_MODEL, GL_COL + 1),
         _row_spec(tm, D_MODEL), _weight_spec((N_DEV, ATTN_WIDTH, 128)), _weight_spec((N_DEV, SSM_WIDTH, 128)),
         _weight_spec((N_DEV, SSM_WIDTH, 128)), _weight_spec((D_MODEL, D_MODEL)), _full_spec((N_DEV, 2, 128)),
         _full_spec((1, D_MODEL)), _full_spec((1, D_MODEL))],
        [_row_spec(tm, D_MODEL), _row_spec(tm, D_MODEL), _row_spec(tm, 128), _row_spec(tm, D_MODEL),
         _row_spec(tm, D_MODEL), _row_spec(tm, D_MODEL)],
        [wide, wide, _sds((SEQ, 128), F32)] + [_sds((SEQ, D_MODEL), BF16)] * 3, [],
        [attn, ys, proj, proj, x, w_ab, w_sb, w_glu, w_out, b_gate, ln_g, ln_b], ride)


def _ff_up(h, w_gate, w_up, ride=None):
    tm, tn = 1024, 768

    def body(h_ref, wg_ref, wu_ref, a_ref, b_ref, f_ref):
        hb = h_ref[...].astype(BF16)
        a, b = _dot(hb, _side_by_side(wg_ref)), _dot(hb, _side_by_side(wu_ref))
        a_ref[...] = a.astype(BF16)
        b_ref[...] = b.astype(BF16)
        f_ref[...] = (a * jax.nn.sigmoid(a) * b).astype(BF16)

    tile = pl.BlockSpec((tm, tn), lambda i, j: (i, j))
    wtile = pl.BlockSpec((tn // FF_PAD, D_MODEL, FF_PAD), lambda i, j: (j, 0, 0))
    out = _sds((SEQ, D_FF_PAD), BF16)
    return _call(body, "ff_up", (SEQ // tm, D_FF_PAD // tn), [pl.BlockSpec((tm, D_MODEL), lambda i, j: (i, 0)), wtile, wtile],
                 [tile, tile, tile], [out, out, out], [], [h, w_gate, w_up], ride)


def _ff_down_loss(f, w_down, h, target, ln_g, ln_b):
    tm = 512

    def body(f_ref, w_ref, h_ref, t_ref, g_ref, b_ref, dr_ref, dg_ref, db_ref, loss_ref):
        @pl.when(pl.program_id(0) == 0)
        def _():
            dg_ref[...] = jnp.zeros_like(dg_ref)
            db_ref[...] = jnp.zeros_like(db_ref)
            loss_ref[...] = jnp.zeros_like(loss_ref)

        r2 = DN_ALPHA * h_ref[...] + _dot(f_ref[...], w_ref[...])
        g = g_ref[...]
        out, xhat, rstd = _layer_norm(r2, g, b_ref[...])
        err = out - t_ref[...]
        loss_ref[...] += 0.5 * jnp.sum(jnp.mean(err * err, axis=-1, keepdims=True), axis=0, keepdims=True)
        dout = err * (1.0 / D_MODEL)
        dg_ref[...] += jnp.sum(dout * xhat, axis=0, keepdims=True)
        db_ref[...] += jnp.sum(dout, axis=0, keepdims=True)
        dr_ref[...] = _layer_norm_bwd(dout, xhat, rstd, g)

    vec = _sds((1, D_MODEL), F32)
    return _pallas_call(
        body, name="ff_down_loss", grid=(SEQ // tm,),
        in_specs=[_row_spec(tm, D_FF_PAD), _weight_spec((D_FF_PAD, D_MODEL)), _row_spec(tm, D_MODEL), _row_spec(tm, D_MODEL),
                  _full_spec((1, D_MODEL)), _full_spec((1, D_MODEL))],
        out_specs=(_row_spec(tm, D_MODEL), _full_spec((1, D_MODEL)), _full_spec((1, D_MODEL)), _full_spec((1, 128))),
        out_shape=(_sds((SEQ, D_MODEL), F32), vec, vec, _sds((1, 128), F32)),
        compiler_params=_cparams(dimension_semantics=("arbitrary",)),
    )(f, w_down, h, target, ln_g, ln_b)


def _ff_down_bwd(dr2, w_down, a, b):
    tm, tn = 1024, 768

    def body(dr_ref, w_ref, a_ref, b_ref, da_ref, db_ref):
        df = _dot_nt(dr_ref[...].astype(BF16), w_ref[...])
        av, bv = a_ref[...].astype(F32), b_ref[...].astype(F32)
        sg = jax.nn.sigmoid(av)
        da_ref[...] = (df * bv * sg * (1.0 + av * (1.0 - sg))).astype(BF16)
        db_ref[...] = (df * av * sg).astype(BF16)

    tile = pl.BlockSpec((tm, tn), lambda i, j: (i, j))
    out = _sds((SEQ, D_FF_PAD), BF16)
    return _pallas_call(
        body, name="ff_down_bwd", grid=(SEQ // tm, D_FF_PAD // tn),
        in_specs=[pl.BlockSpec((tm, D_MODEL), lambda i, j: (i, 0)), pl.BlockSpec((tn, D_MODEL), lambda i, j: (j, 0)), tile, tile],
        out_specs=(tile, tile), out_shape=(out, out),
        compiler_params=_cparams(dimension_semantics=("arbitrary", "arbitrary")),
    )(dr2, w_down, a, b)


def _ff_up_bwd(da, db, w_gate, w_up, dr2, xhat1, rstd1, ln_g, ride=None):
    tm, tk = 1024, 768
    nk = D_FF_PAD // tk

    def body(da_ref, db_ref, wg_ref, wu_ref, dr2_ref, xhat_ref, rstd_ref, g_ref, dr1_ref, dg_ref, dbias_ref, acc):
        i, k = pl.program_id(0), pl.program_id(1)

        @pl.when(jnp.logical_and(i == 0, k == 0))
        def _():
            dg_ref[...] = jnp.zeros_like(dg_ref)
            dbias_ref[...] = jnp.zeros_like(dbias_ref)

        part = _dot_nt(da_ref[...], _side_by_side(wg_ref)) + _dot_nt(db_ref[...], _side_by_side(wu_ref))

        @pl.when(k == 0)
        def _():
            acc[...] = part

        @pl.when(k > 0)
        def _():
            acc[...] += part

        @pl.when(k == nk - 1)
        def _():
            dh = DN_ALPHA * dr2_ref[...] + acc[...]
            xhat = xhat_ref[...]
            dg_ref[...] += jnp.sum(dh * xhat, axis=0, keepdims=True)
            dbias_ref[...] += jnp.sum(dh, axis=0, keepdims=True)
            rstd = jnp.max(rstd_ref[...], axis=1, keepdims=True)
            dr1_ref[...] = _layer_norm_bwd(dh, xhat, rstd, g_ref[...])

    hid = pl.BlockSpec((tm, tk), lambda i, k: (i, k))
    wtile = pl.BlockSpec((tk // FF_PAD, D_MODEL, FF_PAD), lambda i, k: (k, 0, 0))
    row = pl.BlockSpec((tm, D_MODEL), lambda i, k: (i, 0))
    vec = pl.BlockSpec((1, D_MODEL), lambda i, k: (0, 0))
    return _call(
        body, "ff_up_bwd", (SEQ // tm, nk),
        [hid, hid, wtile, wtile, row, row, pl.BlockSpec((tm, 128), lambda i, k: (i, 0)), vec],
        [row, vec, vec], [_sds((SEQ, D_MODEL), F32), _sds((1, D_MODEL), F32), _sds((1, D_MODEL), F32)],
        [pltpu.VMEM((tm, D_MODEL), F32)], [da, db, w_gate, w_up, dr2, xhat1, rstd1, ln_g], ride)


def _mixer_bwd(dr1, proj, y_attn, y_ssm, glu, ys, w_ab, w_sb, w_glu, w_out, b_gate):
    tm = 256

    def body(dr1_ref, gl0_ref, gl1_ref, ya_ref, yssm_ref, glu_ref, ys_ref, wab_ref, wsb_ref, wglu_ref, wout_ref, bg_ref,
             dya_ref, dyssm_ref, dgl_ref, dattn_ref, dglu_ref, dys_ref, mixed_ref, ysb_ref, gy_ref, dbg_ref):
        @pl.when(pl.program_id(0) == 0)
        def _():
            dbg_ref[...] = jnp.zeros_like(dbg_ref)

        dmixed = _dot_nt(dr1_ref[...].astype(BF16), wout_ref[...])
        g0 = jax.nn.sigmoid(gl0_ref[...] + _side_by_side(bg_ref, 0))
        g1 = jax.nn.sigmoid(gl1_ref[...] + _side_by_side(bg_ref, 1))
        y_attn, y_ssm = ya_ref[...].astype(F32), yssm_ref[...].astype(F32)
        mixed_ref[...] = (g0 * y_attn + g1 * y_ssm).astype(BF16)
        dya = (dmixed * g0).astype(BF16)
        dyssm = (dmixed * g1).astype(BF16)
        dya_ref[...] = dya
        dyssm_ref[...] = dyssm
        dgl0 = dmixed * y_attn * g0 * (1.0 - g0)
        dgl1 = dmixed * y_ssm * g1 * (1.0 - g1)
        dgl_ref[:, :GL_COL * D_MODEL] = jnp.zeros((tm, GL_COL * D_MODEL), BF16)
        dgl_ref[:, GL_COL * D_MODEL:(GL_COL + 1) * D_MODEL] = dgl0.astype(BF16)
        dgl_ref[:, (GL_COL + 1) * D_MODEL:] = dgl1.astype(BF16)
        dbg_ref[:, :D_MODEL] += jnp.sum(dgl0, axis=0, keepdims=True)
        dbg_ref[:, D_MODEL:] += jnp.sum(dgl1, axis=0, keepdims=True)
        dattn_ref[...] = _dot_nt(dya, _side_by_side(wab_ref))
        dy_s = _dot_nt(dyssm, _side_by_side(wsb_ref))
        glu = glu_ref[...].astype(F32)
        glu1, sg = glu[:, :SSM_WIDTH], jax.nn.sigmoid(glu[:, SSM_WIDTH:])
        ysb_ref[...] = (glu1 * sg).astype(BF16)
        dglu1 = (dy_s * sg).astype(BF16)
        dglu2 = (dy_s * glu1 * sg * (1.0 - sg)).astype(BF16)
        dglu_ref[:, :SSM_WIDTH] = dglu1
        dglu_ref[:, SSM_WIDTH:] = dglu2
        dgy = _dot_nt(jnp.concatenate([dglu1, dglu2], axis=1), _side_by_side(wglu_ref))
        ys = ys_ref[...]
        gy, t = _gelu(ys)
        gy_ref[...] = gy.astype(BF16)
        dys_ref[...] = dgy * _gelu_grad(ys, t)

    wide_b, half_b = _sds((SEQ, D_MODEL), BF16), _sds((SEQ, SSM_WIDTH), BF16)
    half_f = _sds((SEQ, SSM_WIDTH), F32)
    return _pallas_call(
        body, name="mixer_bwd", grid=(SEQ // tm,),
        in_specs=[_row_spec(tm, D_MODEL), _row_spec(tm, D_MODEL, GL_COL), _row_spec(tm, D_MODEL, GL_COL + 1), _row_spec(tm, D_MODEL),
                  _row_spec(tm, D_MODEL), _row_spec(tm, D_MODEL), _row_spec(tm, SSM_WIDTH), _full_spec((N_DEV, ATTN_WIDTH, 128)),
                  _full_spec((N_DEV, SSM_WIDTH, 128)), _full_spec((N_DEV, SSM_WIDTH, 128)), _full_spec((D_MODEL, D_MODEL)),
                  _full_spec((N_DEV, 2, 128))],
        out_specs=(_row_spec(tm, D_MODEL), _row_spec(tm, D_MODEL), _row_spec(tm, IN_WIDTH), _row_spec(tm, ATTN_WIDTH),
                   _row_spec(tm, D_MODEL), _row_spec(tm, SSM_WIDTH), _row_spec(tm, D_MODEL), _row_spec(tm, SSM_WIDTH),
                   _row_spec(tm, SSM_WIDTH), _full_spec((1, 2 * D_MODEL))),
        out_shape=(wide_b, wide_b, _sds((SEQ, IN_WIDTH), BF16), half_f, wide_b, half_f, wide_b, half_b, half_b,
                   _sds((1, 2 * D_MODEL), F32)),
        compiler_params=_cparams(dimension_semantics=("arbitrary",)),
    )(dr1, proj, proj, y_attn, y_ssm, glu, ys, w_ab, w_sb, w_glu, w_out, b_gate)


def _grad_x(dproj, w_in, dr1, ride=None):
    tm, tk = 1024, 1792
    nk = IN_WIDTH // tk

    def body(dp_ref, w_ref, dr1_ref, o_ref, acc):
        k = pl.program_id(1)
        part = _dot_nt(dp_ref[...], _side_by_side(w_ref))

        @pl.when(k == 0)
        def _():
            acc[...] = part

        @pl.when(k > 0)
        def _():
            acc[...] += part

        @pl.when(k == nk - 1)
        def _():
            o_ref[...] = DN_ALPHA * dr1_ref[...] + acc[...]

    row = pl.BlockSpec((tm, D_MODEL), lambda i, k: (i, 0))
    return _call(
        body, "grad_x", (SEQ // tm, nk),
        [pl.BlockSpec((tm, tk), lambda i, k: (i, k)), pl.BlockSpec((2, D_MODEL, tk // 2), lambda i, k: (k, 0, 0)), row],
        [row], [_sds((SEQ, D_MODEL), F32)], [pltpu.VMEM((tm, D_MODEL), F32)], [dproj, w_in, dr1], ride)


def _weight_grad(a, b, name, shard_cols=None, ride=None):
    k, n = a.shape[1], b.shape[1]
    tk = min(k, 512) if shard_cols else k // N_DEV
    tn = n // 4 if shard_cols else min(n, 1024)

    def body(a_ref, b_ref, o_ref):
        grad = _dot_tn(a_ref[...].astype(BF16), b_ref[...].astype(BF16))
        if shard_cols:
            o_ref[0] = grad[:, :shard_cols].astype(BF16)
            o_ref[1] = grad[:, shard_cols:].astype(BF16)
        else:
            o_ref[...] = grad.astype(BF16)

    if shard_cols:
        out_spec = pl.BlockSpec((2, None, tk, shard_cols), lambda kk, j: (0, j, kk, 0))
        out_shape = _sds((2, 4, k, shard_cols), BF16)
    else:
        out_spec = pl.BlockSpec((None, None, tk, tn), lambda kk, j: (kk % 2, kk // 2, 0, j))
        out_shape = _sds((2, 4, tk, n), BF16)
    out = _call(body, name, (k // tk, n // tn),
                [pl.BlockSpec((SEQ, tk), lambda kk, j: (0, kk)), pl.BlockSpec((SEQ, tn), lambda kk, j: (0, j))],
                [out_spec], [out_shape], [], [a, b], ride)
    return out[0] if ride is None else out


MESH = pl.DeviceIdType.MESH
ANY = pl.BlockSpec(memory_space=pl.ANY)


def _place():
    return lax.axis_index("x"), lax.axis_index("y"), lax.axis_index("c")


def _other_chips(x, y):
    return [(1 - x, y), (x, 1 - y), (1 - x, 1 - y)]


class _Ride:
    def __init__(self, operands, results, aliases, sems, start, wait):
        self.operands, self.results, self.aliases, self.sems = list(operands), list(results), dict(aliases), list(sems)
        self.start, self.wait = start, wait

    def __add__(self, other):
        n_in, n_out, n_sem = len(self.operands), len(self.results), len(self.sems)

        def both(which):
            def run(ins, outs, sems):
                getattr(self, which)(ins[:n_in], outs[:n_out], sems[:n_sem])
                getattr(other, which)(ins[n_in:], outs[n_out:], sems[n_sem:])
            return run

        aliases = {**self.aliases, **{n_in + i: n_out + j for i, j in other.aliases.items()}}
        return _Ride(self.operands + other.operands, self.results + other.results, aliases, self.sems + other.sems,
                     both("start"), both("wait"))


def _call(body, name, grid, in_specs, out_specs, out_shape, scratch_shapes, operands, ride=None, aliases=None):
    in_specs, out_specs, out_shape = list(in_specs), list(out_specs), list(out_shape)
    scratch_shapes, operands, aliases = list(scratch_shapes), list(operands), dict(aliases or {})
    kernel_body = body
    if ride is not None:
        n_in, n_out, n_scr, r_in, r_out = len(in_specs), len(out_specs), len(scratch_shapes), len(ride.operands), len(ride.results)

        def kernel_body(*refs):
            out0, scr0 = n_in + r_in, n_in + r_in + n_out + r_out
            ride_refs = (refs[n_in:out0], refs[out0 + n_out:scr0], refs[scr0 + n_scr:])
            ids = [pl.program_id(i) for i in range(len(grid))]
            first = functools.reduce(jnp.logical_and, [i == 0 for i in ids])
            last = functools.reduce(jnp.logical_and, [i == g - 1 for i, g in zip(ids, grid)])

            @pl.when(first)
            def _():
                ride.start(*ride_refs)

            body(*refs[:n_in], *refs[out0:out0 + n_out], *refs[scr0:scr0 + n_scr])

            @pl.when(last)
            def _():
                ride.wait(*ride_refs)

        aliases.update({n_in + i: n_out + j for i, j in ride.aliases.items()})
        in_specs += [ANY] * r_in
        out_specs += [ANY] * r_out
        out_shape += ride.results
        scratch_shapes += ride.sems
        operands += ride.operands
    return _pallas_call(
        kernel_body, name=name, grid=grid, in_specs=in_specs, out_specs=out_specs, out_shape=out_shape,
        scratch_shapes=scratch_shapes, input_output_aliases=aliases,
        compiler_params=_cparams(dimension_semantics=("arbitrary",) * len(grid)),
    )(*operands)


def _after(*arrays):
    return _Ride(arrays, [], {}, [], lambda *refs: None, lambda *refs: None)


def _gather_first_level(shards):
    n = len(shards)

    def copies(ins, outs, sems, landed):
        send_sems, recv_sems, local_sems = sems
        x, y, c = _place()
        peers = [(x, y, 1 - c)] + [(px, py, c) for px, py in _other_chips(x, y)]

        def row(peer):
            return 4 * x + 2 * y + c if not landed else 4 * peer[0] + 2 * peer[1] + peer[2]

        local = [pltpu.make_async_copy(ins[a], outs[a].at[4 * x + 2 * y + c], local_sems.at[a]) for a in range(n)]
        remote = [pltpu.make_async_remote_copy(
            src_ref=ins[a], dst_ref=outs[a].at[row(peer)], send_sem=send_sems.at[a, k], recv_sem=recv_sems.at[a, k],
            device_id=peer, device_id_type=MESH) for a in range(n) for k, peer in enumerate(peers)]
        return local, remote

    def start(ins, outs, sems):
        local, remote = copies(ins, outs, sems, False)
        for cp in local + remote:
            cp.start()

    def wait(ins, outs, sems):
        local, sent = copies(ins, outs, sems, False)
        for cp in copies(ins, outs, sems, True)[1]:
            cp.wait_recv()
        for cp in sent:
            cp.wait_send()
        for cp in local:
            cp.wait()

    return _Ride(shards, [_sds((N_DEV,) + s.shape, s.dtype) for s in shards], {},
                 [pltpu.SemaphoreType.DMA((n, 4)), pltpu.SemaphoreType.DMA((n, 4)), pltpu.SemaphoreType.DMA((n,))], start, wait)


def _gather_second_level(buffers):
    n = len(buffers)

    def copies(outs, sems, core):
        send_sems, recv_sems = sems
        x, y, c = _place()
        return [pltpu.make_async_remote_copy(
            src_ref=outs[a].at[4 * px + 2 * py + core], dst_ref=outs[a].at[4 * px + 2 * py + core], send_sem=send_sems.at[a, j],
            recv_sem=recv_sems.at[a, j], device_id=(x, y, 1 - c), device_id_type=MESH)
            for a in range(n) for j, (px, py) in enumerate(_other_chips(x, y))]

    def start(ins, outs, sems):
        for cp in copies(outs, sems, lax.axis_index("c")):
            cp.start()

    def wait(ins, outs, sems):
        for cp in copies(outs, sems, 1 - lax.axis_index("c")):
            cp.wait_recv()
        for cp in copies(outs, sems, lax.axis_index("c")):
            cp.wait_send()

    return _Ride(buffers, [_sds(b.shape, b.dtype) for b in buffers], {i: i for i in range(n)},
                 [pltpu.SemaphoreType.DMA((n, 3)), pltpu.SemaphoreType.DMA((n, 3))], start, wait)


def _relayed_gather(shards):
    n = len(shards)
    buffers = [_sds((N_DEV,) + s.shape, s.dtype) for s in shards]
    dma = pltpu.SemaphoreType.DMA

    def remote(src, dst, send_sem, recv_sem, to):
        return pltpu.make_async_remote_copy(src_ref=src, dst_ref=dst, send_sem=send_sem, recv_sem=recv_sem,
                                            device_id=to, device_id_type=MESH)

    def row(px, py, pc):
        return 4 * px + 2 * py + pc

    def ride(operands, aliases, sems, copies):
        def start(ins, outs, sem_refs):
            local, sent = copies(ins, outs, sem_refs, False)
            for cp in local + sent:
                cp.start()

        def wait(ins, outs, sem_refs):
            local, sent = copies(ins, outs, sem_refs, False)
            for cp in copies(ins, outs, sem_refs, True)[1]:
                cp.wait_recv()
            for cp in sent:
                cp.wait_send()
            for cp in local:
                cp.wait()

        return _Ride(operands, buffers, aliases, sems, start, wait)

    def first(ins, outs, sems, landed):
        x, y, c = _place()
        peers = [(x, y, 1 - c), (1 - x, y, c), (x, 1 - y, c)]
        local = [pltpu.make_async_copy(ins[a], outs[a].at[row(x, y, c)], sems[2].at[a]) for a in range(n)]
        return local, [remote(ins[a], outs[a].at[row(*peer) if landed else row(x, y, c)], sems[0].at[a, k], sems[1].at[a, k], peer)
                       for a in range(n) for k, peer in enumerate(peers)]

    def second(ins, outs, sems, landed):
        x, y, c = _place()
        mine = 1 - c if landed else c
        copies = []
        for a in range(n):
            half = shards[a].shape[0] // 2
            over_x, over_y, diagonal = outs[a].at[row(1 - x, y, mine)], outs[a].at[row(x, 1 - y, mine)], outs[a].at[row(1 - x, 1 - y, c)]
            lower, upper = pl.ds(0, half), pl.ds(half, half)
            copies += [remote(over_x, over_x, sems[0].at[a, 0], sems[1].at[a, 0], (x, y, 1 - c)),
                       remote(over_y, over_y, sems[0].at[a, 1], sems[1].at[a, 1], (x, y, 1 - c))]
            if landed:
                copies += [remote(diagonal.at[lower], diagonal.at[lower], sems[0].at[a, 2], sems[1].at[a, 2], (1 - x, y, c)),
                           remote(diagonal.at[upper], diagonal.at[upper], sems[0].at[a, 3], sems[1].at[a, 3], (x, 1 - y, c))]
            else:
                copies += [remote(over_y.at[lower], over_y.at[lower], sems[0].at[a, 2], sems[1].at[a, 2], (1 - x, y, c)),
                           remote(over_x.at[upper], over_x.at[upper], sems[0].at[a, 3], sems[1].at[a, 3], (x, 1 - y, c))]
        return [], copies

    def third(ins, outs, sems, landed):
        x, y, c = _place()
        return [], [remote(outs[a].at[row(1 - x, 1 - y, 1 - c if landed else c)], outs[a].at[row(1 - x, 1 - y, 1 - c if landed else c)],
                           sems[0].at[a], sems[1].at[a], (x, y, 1 - c)) for a in range(n)]

    def later(copies, n_sems):
        return lambda partly: ride(partly, {i: i for i in range(n)}, [dma((n,) + n_sems), dma((n,) + n_sems)], copies)

    return ride(shards, {}, [dma((n, 3)), dma((n, 3)), dma((n,))], first), later(second, (4,)), later(third, ())


def _sibling_swap_ride(grads):
    n = len(grads)

    def copies(ins, outs, sems):
        x, y, c = _place()
        return [pltpu.make_async_remote_copy(
            src_ref=ins[a].at[1 - c], dst_ref=outs[a], send_sem=sems[0].at[a], recv_sem=sems[1].at[a],
            device_id=(x, y, 1 - c), device_id_type=MESH) for a in range(n)]

    def start(ins, outs, sems):
        for cp in copies(ins, outs, sems):
            cp.start()

    def wait(ins, outs, sems):
        for cp in copies(ins, outs, sems):
            cp.wait()

    return _Ride(grads, [_sds(g.shape[1:], g.dtype) for g in grads], {},
                 [pltpu.SemaphoreType.DMA((n,)), pltpu.SemaphoreType.DMA((n,))], start, wait)


def _chip_swap_ride(sums):
    n = len(sums)

    def copies(ins, outs, sems, landed):
        send_sems, recv_sems, local_sems = sems
        x, y, c = _place()
        mine = 2 * x + y
        local = [pltpu.make_async_copy(ins[a].at[mine], outs[a].at[mine], local_sems.at[a]) for a in range(n)]
        remote = [pltpu.make_async_remote_copy(
            src_ref=ins[a].at[2 * px + py], dst_ref=outs[a].at[2 * px + py if landed else mine], send_sem=send_sems.at[a, j],
            recv_sem=recv_sems.at[a, j], device_id=(px, py, c), device_id_type=MESH)
            for a in range(n) for j, (px, py) in enumerate(_other_chips(x, y))]
        return local, remote

    def start(ins, outs, sems):
        local, remote = copies(ins, outs, sems, False)
        for cp in local + remote:
            cp.start()

    def wait(ins, outs, sems):
        local, sent = copies(ins, outs, sems, False)
        for cp in copies(ins, outs, sems, True)[1]:
            cp.wait_recv()
        for cp in sent:
            cp.wait_send()
        for cp in local:
            cp.wait()

    return _Ride(sums, [_sds(s.shape, s.dtype) for s in sums], {},
                 [pltpu.SemaphoreType.DMA((n, 3)), pltpu.SemaphoreType.DMA((n, 3)), pltpu.SemaphoreType.DMA((n,))], start, wait)


def _send_buffers(shards, name):
    n = len(shards)

    def body(*refs):
        for (w, transposed, rows, cols), w_ref, o_ref in zip(shards, refs[:n], refs[n:]):
            if transposed:
                c, r = w.shape
                padded = jnp.concatenate([w_ref[...], jnp.zeros((cols - c, r), F32)], axis=0) if cols > c else w_ref[...]
                o_ref[...] = padded.T.astype(BF16)
            else:
                r, c = w.shape
                if (r, c) != (rows, cols):
                    o_ref[...] = jnp.zeros((rows, cols), BF16)
                o_ref[:r, :c] = w_ref[...].astype(BF16)

    return _pallas_call(body, name=name, out_shape=[_sds((rows, cols), BF16) for _, _, rows, cols in shards])(
        *[w for w, _, _, _ in shards])


def _all_gather(shards, name):
    n = len(shards)
    first, second, third = _relayed_gather(shards)
    levels = [first, second(shards), third(shards)]
    counts = [len(level.sems) for level in levels]

    def body(*refs):
        ins, outs, sems = refs[:n], refs[n:2 * n], refs[2 * n:]
        for i, level in enumerate(levels):
            mine = sems[sum(counts[:i]):sum(counts[:i + 1])]
            level.start(ins, outs, mine)
            level.wait(ins, outs, mine)

    return _pallas_call(
        body, name=name, in_specs=[ANY] * n, out_specs=[ANY] * n, out_shape=first.results,
        scratch_shapes=[s for level in levels for s in level.sems],
    )(*shards)


def _exchange(ride, name):
    n_in, n_out = len(ride.operands), len(ride.results)

    def body(*refs):
        ride.start(refs[:n_in], refs[n_in:n_in + n_out], refs[n_in + n_out:])
        ride.wait(refs[:n_in], refs[n_in:n_in + n_out], refs[n_in + n_out:])

    return _pallas_call(body, name=name, in_specs=[ANY] * n_in, out_specs=[ANY] * n_out, out_shape=ride.results,
                        scratch_shapes=ride.sems, input_output_aliases=ride.aliases)(*ride.operands)


HBM = pl.BlockSpec(memory_space=pltpu.HBM)
SEMAPHORES = pl.BlockSpec(memory_space=pltpu.SEMAPHORE)
IN_FLIGHT = pltpu.CompilerParams(has_side_effects=pltpu.SideEffectType.DATAFLOW_SIDE_EFFECTING)


def _chip_swap_copies(src_refs, land_refs, send_sems, recv_sems, landed):
    x, y, c = _place()
    return [pltpu.make_async_remote_copy(
        src_ref=src.at[2 * px + py], dst_ref=land.at[2 * px + py if landed else 2 * x + y], send_sem=send_sems.at[3 * a + j],
        recv_sem=recv_sems.at[3 * a + j], device_id=(px, py, c), device_id_type=MESH)
        for a, (src, land) in enumerate(zip(src_refs, land_refs)) for j, (px, py) in enumerate(_other_chips(x, y))]


def _chip_swap_start(sums, name):
    n = len(sums)

    def body(*refs):
        src_refs, land_refs, (send_sems, recv_sems), token = refs[:n], refs[n:2 * n], refs[2 * n:2 * n + 2], refs[-1]
        for cp in _chip_swap_copies(src_refs, land_refs, send_sems, recv_sems, False):
            cp.start()
        token[...] = jnp.zeros_like(token)

    kept = [pltpu.HBM(s.shape, s.dtype) for s in sums]
    out = _pallas_call(
        body, name=name,
        out_shape=[pltpu.SemaphoreType.DMA((3 * n,)), pltpu.SemaphoreType.DMA((3 * n,))] + kept + kept + [_sds((8, 128), F32)],
        in_specs=[HBM] * (2 * n), out_specs=[SEMAPHORES, SEMAPHORES] + [HBM] * (2 * n) + [pl.BlockSpec(memory_space=pltpu.VMEM)],
        input_output_aliases={i: 2 + i for i in range(2 * n)}, compiler_params=IN_FLIGHT,
    )(*[pltpu.with_memory_space_constraint(s, pltpu.HBM) for s in sums],
      *[pltpu.with_memory_space_constraint(lax.empty(s.shape, s.dtype), pltpu.HBM) for s in sums])
    return out[0], out[1], out[2:2 + n], out[2 + n:2 + 2 * n], out[-1]


def _chip_swap_wait(send_sems, recv_sems, sums, landings, after, name):
    n = len(sums)

    def body(*refs):
        src_refs, land_refs, (send_sems, recv_sems) = refs[:n], refs[n:2 * n], refs[2 * n:2 * n + 2]
        for cp in _chip_swap_copies(src_refs, land_refs, send_sems, recv_sems, False):
            cp.wait_send()
        for cp in _chip_swap_copies(src_refs, land_refs, send_sems, recv_sems, True):
            cp.wait_recv()

    out = _pallas_call(
        body, name=name, out_shape=[pltpu.HBM(s.shape, s.dtype) for s in list(sums) + list(landings)],
        in_specs=[HBM] * (2 * n) + [SEMAPHORES, SEMAPHORES] + [ANY] * len(after), out_specs=[HBM] * (2 * n),
        input_output_aliases={i: i for i in range(2 * n)}, compiler_params=IN_FLIGHT,
    )(*sums, *landings, send_sems, recv_sems, *after)
    return out[:n], out[n:]


def _pair_sums(gs, rs, core, name):
    n_arrays = len(gs)

    def body(core_ref, *refs):
        for g_ref, r_ref, o_ref in zip(refs[:n_arrays], refs[n_arrays:2 * n_arrays], refs[2 * n_arrays:]):
            o_ref[...] = (g_ref[...].astype(F32) + r_ref[...].astype(F32)).astype(o_ref.dtype)

    def own(g):
        return pl.BlockSpec((None, None) + g.shape[2:], lambda p, core_ref: (core_ref[0], p, 0, 0))

    def chip(g):
        return pl.BlockSpec((None,) + g.shape[2:], lambda p, core_ref: (p, 0, 0))

    return _pallas_call(
        body, name=name,
        grid_spec=pltpu.PrefetchScalarGridSpec(
            num_scalar_prefetch=1, grid=(4,), in_specs=[own(g) for g in gs] + [chip(g) for g in gs],
            out_specs=[chip(g) for g in gs]),
        out_shape=[_sds(g.shape[1:], g.dtype) for g in gs], compiler_params=_cparams(dimension_semantics=("arbitrary",)),
    )(core, *gs, *rs)


def _adamw_math(w, g, m, v):
    m = ADAM_B1 * m + (1.0 - ADAM_B1) * g
    v = ADAM_B2 * v + (1.0 - ADAM_B2) * (g * g)
    m_hat = m / (1.0 - ADAM_B1 ** ADAM_STEP)
    v_hat = v / (1.0 - ADAM_B2 ** ADAM_STEP)
    return -ADAM_LR * (m_hat / (jnp.sqrt(v_hat) + ADAM_EPS) + ADAM_WD * w), m, v


def _adamw_many(weights, name, ride=None):
    steps = 4
    in_specs, out_specs, out_shape, operands, tiles = [], [], [], [], []
    for w, m, v, parts, own, transposed in weights:
        _, pr, pc = parts.shape
        if transposed:
            c, r = w.shape
            tile = pl.BlockSpec((c, r // steps), lambda i: (0, i))
            part_tile = pl.BlockSpec((4, r // steps, pc), lambda i: (0, i, 0))
            tiles.append((c, r // steps))
        elif w.shape[0] % (8 * steps) == 0:
            r, c = w.shape
            tile = pl.BlockSpec((r // steps, c), lambda i: (i, 0))
            part_tile = pl.BlockSpec((4, r // steps, pc), lambda i: (0, i, 0))
            tiles.append((r // steps, c))
        else:
            tile = pl.BlockSpec(w.shape, lambda i: (0, 0))
            part_tile = pl.BlockSpec(parts.shape, lambda i: (0, 0, 0))
            tiles.append(w.shape)
        in_specs += [tile, tile, tile] + [part_tile] * (1 if own is None else 2)
        out_specs += [tile] * 4
        out_shape += [_sds(w.shape, F32)] * 4
        operands += [w, m, v, parts] + ([] if own is None else [own])
    n_in = len(operands)

    def body(*refs):
        ins, outs = list(refs[:n_in]), refs[n_in:]
        this_chip = 2 * lax.axis_index("x") + lax.axis_index("y")
        for k, (_, _, _, _, own, transposed) in enumerate(weights):
            w_ref, m_ref, v_ref, p_ref = ins[:4]
            own_ref = None if own is None else ins[4]
            del ins[:4 if own is None else 5]
            rows, cols = tiles[k]
            g = None
            for q in range(4):
                index = (q,) if transposed else (q, slice(0, rows), slice(0, cols))
                part = p_ref[index] if own is None else jnp.where(this_chip == q, own_ref[index], p_ref[index])
                g = part.astype(F32) if g is None else g + part.astype(F32)
            if transposed:
                g = g.T[:rows]
            g_out, d_out, m_out, v_out = outs[4 * k:4 * k + 4]
            g_out[...] = g
            d_out[...], m_out[...], v_out[...] = _adamw_math(w_ref[...], g, m_ref[...], v_ref[...])

    return _call(body, name, (steps,), in_specs, out_specs, out_shape, [], operands, ride)


SMALL = ("ssm_a_re", "ssm_a_im", "ssm_log_dt", "ssm_b_re", "ssm_b_im", "ssm_c_re", "ssm_c_im", "ssm_d",
         "ln1_g", "ln1_b", "ln2_g", "ln2_b")


def _pack_rows(arrays):
    rows = []
    for a in arrays:
        flat = a.reshape(-1)
        rows.append(jnp.pad(flat, (0, -flat.shape[0] % 128)).reshape(-1, 128))
    packed = jnp.concatenate(rows, axis=0)
    return jnp.pad(packed, ((0, -packed.shape[0] % 8), (0, 0)))


def _unpack_rows(packed, shapes):
    out, row = [], 0
    for shape in shapes:
        size = math.prod(shape)
        n_rows = -(-size // 128)
        out.append(packed[row:row + n_rows].reshape(-1)[:size].reshape(shape))
        row += n_rows
    return out


def _sum_devices(parts):
    def body(p_ref, o_ref):
        total = p_ref[0]
        for dev in range(1, N_DEV):
            total = total + p_ref[dev]
        o_ref[...] = total

    return _pallas_call(body, name="sum_devices", out_shape=_sds(parts.shape[1:], F32))(parts)


def _adamw_replicated(ws, ms, vs, gs):
    n = len(ws)

    def body(*refs):
        w_refs, m_refs, v_refs, g_refs, d_out, m_out, v_out = (refs[i * n:(i + 1) * n] for i in range(7))
        for i in range(n):
            d_out[i][...], m_out[i][...], v_out[i][...] = _adamw_math(w_refs[i][...], g_refs[i][...], m_refs[i][...], v_refs[i][...])

    out = _pallas_call(body, name="adamw_replicated", out_shape=[_sds(w.shape, F32) for w in ws] * 3,
                       compiler_params=_cparams())(*ws, *ms, *vs, *gs)
    return out[:n], out[n:2 * n], out[2 * n:]


def kernel(x, w_in, b_gate, w_attn_br, w_ssm_br, w_out, ssm_a_re, ssm_a_im, ssm_log_dt, ssm_b_re, ssm_b_im, ssm_c_re, ssm_c_im, ssm_d, w_glu, ln1_g, ln1_b, w_ff_gate, w_ff_up, w_ff_down, ln2_g, ln2_b, loss_target, m_w_in, m_b_gate, m_w_attn_br, m_w_ssm_br, m_w_out, m_ssm_a_re, m_ssm_a_im, m_ssm_log_dt, m_ssm_b_re, m_ssm_b_im, m_ssm_c_re, m_ssm_c_im, m_ssm_d, m_w_glu, m_ln1_g, m_ln1_b, m_w_ff_gate, m_w_ff_up, m_w_ff_down, m_ln2_g, m_ln2_b, v_w_in, v_b_gate, v_w_attn_br, v_w_ssm_br, v_w_out, v_ssm_a_re, v_ssm_a_im, v_ssm_log_dt, v_ssm_b_re, v_ssm_b_im, v_ssm_c_re, v_ssm_c_im, v_ssm_d, v_w_glu, v_ln1_g, v_ln1_b, v_w_ff_gate, v_w_ff_up, v_w_ff_down, v_ln2_g, v_ln2_b):
    given = dict(locals())
    x2, target = x[0], loss_target[0]
    core = lax.axis_index("c").astype(jnp.int32).reshape(1)

    sharded = ("w_in", "w_attn_br", "w_ssm_br", "w_glu", "w_ff_gate", "w_ff_up", "b_gate", "w_out", "w_ff_down")
    send_shape = dict(w_in=(D_MODEL, 896), w_attn_br=(ATTN_WIDTH, 128), w_ssm_br=(SSM_WIDTH, 128), w_glu=(SSM_WIDTH, 128),
                      w_out=(128, D_MODEL), w_ff_gate=(D_MODEL, FF_PAD), w_ff_up=(D_MODEL, FF_PAD), w_ff_down=(FF_PAD, D_MODEL))
    local = {k: given[k][0] for k in sharded}
    narrow = ("w_ff_gate", "w_ff_up")
    def to_send(k):
        return (local[k].T, True, *send_shape[k]) if k in narrow else (local[k], False, *send_shape[k])

    later = [k for k in sharded if k not in ("w_in", "b_gate")]
    sends = dict(zip(["w_in"] + later, _send_buffers([to_send("w_in")], "send_w_in")
                     + _send_buffers([to_send(k) for k in later], "send_weights")))
    sends["b_gate"] = local["b_gate"]
    mixer_weights = ("w_attn_br", "w_ssm_br", "w_glu", "b_gate", "w_out")
    ff_weights = ("w_ff_gate", "w_ff_up", "w_ff_down")
    wt = {}
    wt["w_in"], = _all_gather([sends["w_in"]], "gather_w_in")

    a_re, a_im, log_dt = ssm_a_re[0], ssm_a_im[0], ssm_log_dt[0].reshape(SSM_GROUPS, 1)
    b_re_t, b_im_t = ssm_b_re[0].transpose(0, 2, 1), ssm_b_im[0].transpose(0, 2, 1)
    abar_re, abar_im, e_re, e_im, bbar_re_t, bbar_im_t = _ssm_prep(a_re, a_im, log_dt, b_re_t, b_im_t)
    bmat, cmat, a_chunks = _ssm_tables(abar_re, abar_im, bbar_re_t, bbar_im_t, ssm_c_re[0], ssm_c_im[0])
    cos_t, sin_t = _rope_tables()

    big_mixer, ff_in = [k for k in mixer_weights if k != "b_gate"], ("w_ff_gate", "w_ff_up")
    n_mixer = len(big_mixer)
    mixer_1, mixer_2, mixer_3 = _relayed_gather([sends[k] for k in big_mixer])
    ff_in_1, ff_in_2, ff_in_3 = _relayed_gather([sends[k] for k in ff_in])
    ff_down_1, ff_down_2, ff_down_3 = _relayed_gather([sends["w_ff_down"]])
    proj, *landed = _proj(x2, wt["w_in"], mixer_1 + _gather_first_level([sends["b_gate"]]))
    mixer, bias = landed[:n_mixer], landed[n_mixer:]
    attn, lse, q_pm, k_pm, v_pm, *landed = _attn_fwd(proj, cos_t, sin_t,
                                                     mixer_2(mixer) + _gather_second_level(bias) + ff_in_1)
    mixer, b_gate_full, ff = landed[:n_mixer], landed[n_mixer], landed[n_mixer + 1:]
    ys, states, *landed = _ssm_fwd(proj, bmat, cmat, a_chunks, ssm_d, mixer_3(mixer) + ff_in_2(ff) + ff_down_1)
    wt.update(zip(big_mixer, landed[:n_mixer]))
    ff, ff_down = landed[n_mixer:n_mixer + 2], landed[n_mixer + 2:]
    wt["w_out"] = wt["w_out"].reshape(D_MODEL, D_MODEL)
    h, xhat1, rstd1, glu, y_attn, y_ssm, *landed = _mixer_out(
        attn, ys, proj, x2, wt["w_attn_br"], wt["w_ssm_br"], wt["w_glu"], wt["w_out"], b_gate_full, ln1_g, ln1_b,
        ff_in_3(ff) + ff_down_2(ff_down))
    wt.update(zip(ff_in, landed[:2]))
    ff_a, ff_b, ff_f, w_ff_down = _ff_up(h, wt["w_ff_gate"], wt["w_ff_up"], ff_down_3(landed[2:]))
    wt["w_ff_down"] = w_ff_down.reshape(D_FF_PAD, D_MODEL)
    dr2, d_ln2_g, d_ln2_b, loss_lanes = _ff_down_loss(ff_f, wt["w_ff_down"], h, target, ln2_g, ln2_b)

    def pair_sums(names, contrib, from_sibling):
        return _pair_sums([contrib[k] for k in names], from_sibling, core, "pair_sums_" + names[0])

    d_a, d_b = _ff_down_bwd(dr2, wt["w_ff_down"], ff_a, ff_b)
    contrib = dict(w_ff_gate=_weight_grad(h, d_a, "wgrad_w_ff_gate", FF_PAD),
                   w_ff_up=_weight_grad(h, d_b, "wgrad_w_ff_up", FF_PAD),
                   w_ff_down=_weight_grad(ff_f, dr2, "wgrad_w_ff_down"))
    dr1, d_ln1_g, d_ln1_b, *from_sibling = _ff_up_bwd(
        d_a, d_b, wt["w_ff_gate"], wt["w_ff_up"], dr2, xhat1, rstd1, ln1_g, _sibling_swap_ride([contrib[k] for k in ff_weights]))
    ff_sums = pair_sums(ff_weights, contrib, from_sibling)

    d_ya, d_yssm, d_proj, d_attn, d_glu, d_ys, mixed, y_s, gy, d_bg = _mixer_bwd(
        dr1, proj, y_attn, y_ssm, glu, ys, wt["w_attn_br"], wt["w_ssm_br"], wt["w_glu"], wt["w_out"], b_gate_full)
    contrib.update(w_attn_br=_weight_grad(attn, d_ya, "wgrad_w_attn_br", 128),
                   w_ssm_br=_weight_grad(y_s, d_yssm, "wgrad_w_ssm_br", 128),
                   w_glu=_weight_grad(gy, d_glu, "wgrad_w_glu", 128),
                   w_out=_weight_grad(mixed, dr1, "wgrad_w_out"),
                   b_gate=d_bg.reshape(2, 4, 2, 128).transpose(2, 1, 0, 3))
    d_proj, *landed = _attn_bwd(q_pm, k_pm, v_pm, cos_t, sin_t, attn, lse, d_attn, d_proj,
                                _chip_swap_ride(ff_sums) + _sibling_swap_ride([contrib[k] for k in mixer_weights]))
    parts, own_sums = dict(zip(ff_weights, landed[:len(ff_weights)])), {}
    mixer_sums = pair_sums(mixer_weights, contrib, landed[len(ff_weights):])
    d_proj, d_bmat, d_cmat, d_abar, d_skip, *landed = _ssm_bwd(d_ys, proj, states, bmat, cmat, a_chunks, ssm_d, d_proj,
                                                               _chip_swap_ride(mixer_sums))
    parts.update(zip(mixer_weights, landed))

    gbb_re_t, gbb_im_t = _block_diag_parts(d_bmat, True)
    gc_re, gc_im = _block_diag_parts(d_cmat, False)
    ga_re = d_abar[:, 0, :CHUNK_STATES].reshape(SSM_GROUPS, SSM_STATE)
    ga_im = d_abar[:, 0, CHUNK_STATES:].reshape(SSM_GROUPS, SSM_STATE)
    g_a_re, g_a_im, g_log_dt, g_b_re_t, g_b_im_t = _ssm_param_bwd(
        a_re, a_im, log_dt, b_re_t, b_im_t, abar_re, abar_im, e_re, e_im, ga_re, ga_im, gbb_re_t, gbb_im_t)
    mine = [g_a_re, g_a_im, g_log_dt, g_b_re_t, g_b_im_t, gc_re, -gc_im,
            d_skip, d_ln1_g, d_ln1_b, d_ln2_g, d_ln2_b]
    small_packed = _pack_rows(mine + [loss_lanes])

    contrib["w_in"], small_partly = _weight_grad(x2, d_proj, "wgrad_w_in", 896, _gather_first_level([small_packed]))
    from_sibling, every = _exchange(_sibling_swap_ride([contrib["w_in"]]) + _gather_second_level([small_partly]),
                                    "swap_w_in_with_sibling")
    w_in_sum, = pair_sums(["w_in"], contrib, [from_sibling])
    send_sems, recv_sems, w_in_sum, landing, token = _chip_swap_start([w_in_sum], "w_in_chip_swap_start")

    def adamw_of(k):
        taken = (lambda a: a.T) if k in narrow else (lambda a: a)
        return taken(local[k]), taken(given["m_" + k][0]), taken(given["v_" + k][0]), parts[k], own_sums.get(k), k in narrow

    others = [k for k in sharded if k != "w_in"]
    updated = _adamw_many([adamw_of(k) for k in others], "adamw_others", _after(token))
    grad_x, = _grad_x(d_proj, wt["w_in"], dr1, _after(token))

    def held(k, a):
        return a.transpose(0, 1, 3, 2) if k in ("ssm_b_re", "ssm_b_im") else a

    *small_grads, loss_sum = _unpack_rows(_sum_devices(every), [held(k, given[k]).shape for k in SMALL] + [(1, 128)])
    small = _adamw_replicated([held(k, given[k]) for k in SMALL], [held(k, given["m_" + k]) for k in SMALL],
                              [held(k, given["v_" + k]) for k in SMALL], small_grads)
    loss = loss_sum[0, 0]

    (own_sums["w_in"],), (parts["w_in"],) = _chip_swap_wait(
        send_sems, recv_sems, w_in_sum, landing, [grad_x, updated[0], small[0][0]], "w_in_chip_swap_wait")
    updated += _adamw_many([adamw_of("w_in")], "adamw_w_in")

    grads, deltas, new_m, new_v = {}, {}, {}, {}
    for i, k in enumerate(others + ["w_in"]):
        out = [o.T if k in narrow else o for o in updated[4 * i:4 * i + 4]]
        grads[k], deltas[k], new_m[k], new_v[k] = (o.reshape((1,) + local[k].shape) for o in out)
    for res, values in zip((grads, deltas, new_m, new_v), (small_grads,) + small):
        res.update((k, held(k, a)) for k, a in zip(SMALL, values))

    order = ("w_in", "b_gate", "w_attn_br", "w_ssm_br", "w_out", "ssm_a_re", "ssm_a_im", "ssm_log_dt", "ssm_b_re", "ssm_b_im",
             "ssm_c_re", "ssm_c_im", "ssm_d", "w_glu", "ln1_g", "ln1_b", "w_ff_gate", "w_ff_up", "w_ff_down", "ln2_g", "ln2_b")
    return (loss, grad_x[None], *[grads[k] for k in order], *[deltas[k] for k in order], *[new_m[k] for k in order],
            *[new_v[k] for k in order])
```

```python
import functools
import math

import jax
import jax.numpy as jnp
import numpy as np
from jax import lax
from jax.experimental import pallas as pl
from jax.experimental.pallas import tpu as pltpu

F32 = jnp.float32
BF16 = jnp.bfloat16

N_DEV = 8
SEQ = 2048
D_MODEL = 1024
HEAD_DIM = 64
ATTN_WIDTH = 512
QKV_WIDTH = 1536
SSM_WIDTH = 512
SSM_GROUPS = 32
SSM_GROUP = 16
SSM_STATE = 64
IN_WIDTH = 7168
D_FF = 2816
FF_SHARD = D_FF // N_DEV
FF_PAD = 384
D_FF_PAD = FF_PAD * N_DEV
DN_ALPHA = 2.0 ** 0.25
LN_EPS = 1e-5
NEG_INF = -1e30
ROPE_THETA = 10000.0
BLOCK = 128
GROUPS = ((1, 16), (4, 4), (16, 1))

ADAM_LR = 0.001
ADAM_B1 = 0.9
ADAM_B2 = 0.999
ADAM_EPS = 1e-08
ADAM_WD = 0.01
ADAM_STEP = 10

VMEM_LIMIT = 56 * 1024 * 1024


_pallas_call = pl.pallas_call


def _cparams(**kw):
    return pltpu.CompilerParams(vmem_limit_bytes=VMEM_LIMIT, **kw)


def _dot(a, b):
    return jnp.dot(a, b, preferred_element_type=F32)


def _dot_nt(a, b):
    return lax.dot_general(a, b, (((1,), (1,)), ((), ())), preferred_element_type=F32)


def _side_by_side(w_ref, row=None):
    rows = slice(None) if row is None else pl.ds(row, 1)
    return jnp.concatenate([w_ref[i, rows, :] for i in range(w_ref.shape[0])], axis=1)


def _dot_tn(a, b):
    return lax.dot_general(a, b, (((0,), (0,)), ((), ())), preferred_element_type=F32)


def _rope_tables():
    half = HEAD_DIM // 2
    inv_freq = np.float32(ROPE_THETA) ** (-np.arange(half, dtype=np.float32) / np.float32(half))
    ang = np.arange(SEQ, dtype=np.float32)[:, None] * inv_freq[None, :]
    cos, sin = np.cos(ang).astype(np.float32), np.sin(ang).astype(np.float32)
    tables = np.tile(cos, (1, 4)), np.tile(np.concatenate([-sin, sin], axis=1), (1, 2))

    def by_phase(t):
        return np.stack([t.reshape(SEQ // d, d, 128).transpose(1, 0, 2).reshape(SEQ, 128) for d, _ in GROUPS])

    return jnp.asarray(by_phase(tables[0])), jnp.asarray(by_phase(tables[1]))


def _swap_halves(x):
    lane = lax.broadcasted_iota(jnp.int32, x.shape, 1)
    return jnp.where((lane & 63) < 32, pltpu.roll(x, 96, axis=1), pltpu.roll(x, 32, axis=1))


def _group_rows(d, nb, r, i):
    src = pl.ds(i * BLOCK, BLOCK) if d == 1 else pl.ds(r + i * BLOCK * d, BLOCK, stride=d)
    return src, pl.ds((r * nb + i) * BLOCK, BLOCK)


def _attn_masks():
    a_idx = lax.broadcasted_iota(jnp.int32, (2 * BLOCK, 2 * BLOCK), 0) & (BLOCK - 1)
    c_idx = lax.broadcasted_iota(jnp.int32, (2 * BLOCK, 2 * BLOCK), 1)
    cur_ok = jnp.logical_and(c_idx >= BLOCK, c_idx - BLOCK <= a_idx)
    prev_ok = jnp.logical_and(c_idx < BLOCK, c_idx >= a_idx)
    lane = lax.broadcasted_iota(jnp.int32, (BLOCK, 128), 1)
    return cur_ok, prev_ok, lane < HEAD_DIM


def _stack_heads(t, head0):
    zero = jnp.zeros_like(t)
    return jnp.concatenate([jnp.where(head0, t, zero), jnp.where(head0, zero, t)], axis=0)


def _unstack_heads(t2, head0):
    return jnp.where(head0, t2[:BLOCK], t2[BLOCK:])


def _attn_fwd(proj, cos_t, sin_t, ride=None):
    def body(q0, q1, q2, k0, k1, k2, v0, v1, v2, cos_ref, sin_ref, attn_ref, lse_ref, qpm_ref, kpm_ref, vpm_ref,
             qs, ks, vs, os_, ms, ls, acc, mnat, lnat):
        cur_ok, prev_ok, head0 = _attn_masks()
        ks[:BLOCK, :] = jnp.zeros((BLOCK, 128), BF16)
        vs[:BLOCK, :] = jnp.zeros((BLOCK, 128), BF16)
        for g, (d, nb) in enumerate(GROUPS):
            q_ref, k_ref, v_ref = (q0, q1, q2)[g], (k0, k1, k2)[g], (v0, v1, v2)[g]
            for r in range(d):
                for i in range(nb):
                    src, dst = _group_rows(d, nb, r, i)
                    below = pl.ds(dst.start + BLOCK, BLOCK)
                    c, s = cos_ref[g, dst, :], sin_ref[g, dst, :]
                    q = q_ref[src, :]
                    k = k_ref[src, :]
                    qs[dst, :] = ((q * c + _swap_halves(q) * s) * 0.125).astype(BF16)
                    ks[below, :] = (k * c + _swap_halves(k) * s).astype(BF16)
                    vs[below, :] = v_ref[src, :].astype(BF16)
                    qpm_ref[g, dst, :], kpm_ref[g, dst, :], vpm_ref[g, dst, :] = qs[dst, :], ks[below, :], vs[below, :]

            def block(b, carry, nb=nb):
                has_prev = (b & (nb - 1)) > 0
                cur = pl.ds(pl.multiple_of(b * BLOCK, BLOCK), BLOCK)
                window = pl.ds(pl.multiple_of(b * BLOCK, BLOCK), 2 * BLOCK)
                valid = jnp.logical_or(cur_ok, jnp.logical_and(prev_ok, has_prev))
                s = jnp.where(valid, _dot_nt(_stack_heads(qs[cur, :], head0), ks[window, :]), NEG_INF)
                m = jnp.max(s, axis=1, keepdims=True)
                p = jnp.exp(s - m)
                os_[cur, :] = _unstack_heads(_dot(p.astype(BF16), vs[window, :]), head0)
                ms[cur, :] = _unstack_heads(m, head0)
                ls[cur, :] = _unstack_heads(jnp.sum(p, axis=1, keepdims=True), head0)
                return carry

            lax.fori_loop(0, SEQ // BLOCK, block, 0, unroll=16)

            for r in range(d):
                for i in range(nb):
                    src, dst = _group_rows(d, nb, r, i)
                    if g == 0:
                        acc[src, :], mnat[src, :], lnat[src, :] = os_[dst, :], ms[dst, :], ls[dst, :]
                    else:
                        m_old, m_g = mnat[src, :], ms[dst, :]
                        m_new = jnp.maximum(m_old, m_g)
                        a_old, a_g = jnp.exp(m_old - m_new), jnp.exp(m_g - m_new)
                        acc[src, :] = a_old * acc[src, :] + a_g * os_[dst, :]
                        lnat[src, :] = a_old * lnat[src, :] + a_g * ls[dst, :]
                        mnat[src, :] = m_new
        for i in range(SEQ // BLOCK):
            rows = pl.ds(i * BLOCK, BLOCK)
            l = lnat[rows, :]
            attn_ref[rows, :] = acc[rows, :] / l
            lse_ref[rows, :] = mnat[rows, :] + jnp.log(l)

    def col(base):
        return pl.BlockSpec((SEQ, 128), lambda hp, base=base: (0, base + hp))

    in_specs = [col(g * 4) for g in range(3)] + [col(12 + g * 4) for g in range(3)] + [col(24 + g * 4) for g in range(3)]
    table = pl.BlockSpec((3, SEQ, 128), lambda hp: (0, 0, 0), pipeline_mode=pl.Buffered(1))
    out = pl.BlockSpec((SEQ, 128), lambda hp: (0, hp))
    by_phase = pl.BlockSpec((3, SEQ, 128), lambda hp: (0, 0, hp))
    return _call(
        body, "attn_fwd", (4,), in_specs + [table, table], [out, out] + [by_phase] * 3,
        [_sds((SEQ, ATTN_WIDTH), F32), _sds((SEQ, ATTN_WIDTH), F32)] + [_sds((3, SEQ, ATTN_WIDTH), BF16)] * 3,
        [pltpu.VMEM((SEQ, 128), BF16)] + [pltpu.VMEM((SEQ + BLOCK, 128), BF16)] * 2 + [pltpu.VMEM((SEQ, 128), F32)] * 6,
        [proj] * 9 + [cos_t, sin_t], ride)


def _attn_bwd_group_body(g):
    d, nb = GROUPS[g]

    def body(qs_ref, ks_ref, vs_ref, cos_ref, sin_ref, lse_ref, dattn_ref, dsum_ref, dproj_ref,
             ks, vs, dos, lss, dss, dqs, dks, dvs, stage, outs, sems):
        cur_ok, prev_ok, head0 = _attn_masks()
        qs = qs_ref.at[g]
        ks[:BLOCK, :] = jnp.zeros((BLOCK, 128), BF16)
        vs[:BLOCK, :] = jnp.zeros((BLOCK, 128), BF16)
        dks[:BLOCK, :] = jnp.zeros((BLOCK, 128), F32)
        dvs[:BLOCK, :] = jnp.zeros((BLOCK, 128), F32)
        for r in range(d):
            for i in range(nb):
                src, dst = _group_rows(d, nb, r, i)
                below = pl.ds(dst.start + BLOCK, BLOCK)
                ks[below, :] = ks_ref[g, dst, :]
                vs[below, :] = vs_ref[g, dst, :]
                dos[dst, :] = dattn_ref[src, :].astype(BF16)
                dss[dst, :] = dsum_ref[src, :]
                lss[dst, :] = lse_ref[src, :]
                dks[below, :] = jnp.zeros((BLOCK, 128), F32)
                dvs[below, :] = jnp.zeros((BLOCK, 128), F32)

        def per_head_column(t):
            return jnp.concatenate([jnp.max(jnp.where(head0, t, NEG_INF), axis=1, keepdims=True),
                                    jnp.max(jnp.where(head0, NEG_INF, t), axis=1, keepdims=True)], axis=0)

        def block(b, carry):
            has_prev = (b & (nb - 1)) > 0
            cur = pl.ds(pl.multiple_of(b * BLOCK, BLOCK), BLOCK)
            window = pl.ds(pl.multiple_of(b * BLOCK, BLOCK), 2 * BLOCK)
            valid = jnp.logical_or(cur_ok, jnp.logical_and(prev_ok, has_prev))
            q2, do2 = _stack_heads(qs[cur, :], head0), _stack_heads(dos[cur, :], head0)
            kw, vw = ks[window, :], vs[window, :]
            s = jnp.where(valid, _dot_nt(q2, kw), NEG_INF)
            p = jnp.exp(s - per_head_column(lss[cur, :]))
            ds = (p * (_dot_nt(do2, vw) - per_head_column(dss[cur, :]))).astype(BF16)
            dvs[window, :] += _dot_tn(p.astype(BF16), do2)
            dks[window, :] += _dot_tn(ds, q2)
            dqs[cur, :] = _unstack_heads(_dot(ds, kw), head0)
            return carry

        lax.fori_loop(0, SEQ // BLOCK, block, 0, unroll=8)

        hp = pl.program_id(0)
        copies = []
        for kind in range(3):
            for r in range(d):
                for i in range(nb):
                    src, dst = _group_rows(d, nb, r, i)
                    below = pl.ds(dst.start + BLOCK, BLOCK)
                    if kind == 2:
                        stage[src, :] = dvs[below, :]
                    else:
                        c, s = cos_ref[g, dst, :], sin_ref[g, dst, :]
                        t = dqs[dst, :] * 0.125 if kind == 0 else dks[below, :]
                        stage[src, :] = t * c - _swap_halves(t) * s
            for i in range(SEQ // MM_ROWS):
                rows = pl.ds(i * MM_ROWS, MM_ROWS)
                outs[kind, rows, :] = stage[rows, :].astype(BF16)
            column = pl.multiple_of((kind * 12 + g * 4 + hp) * 128, 128)
            copies.append(pltpu.make_async_copy(outs.at[kind], dproj_ref.at[:, pl.ds(column, 128)], sems.at[kind]))
            copies[-1].start()
        for cp in copies:
            cp.wait()

    return body


def _attn_bwd(q_pm, k_pm, v_pm, cos_t, sin_t, attn, lse, dattn, dproj, ride=None):
    groups = [_attn_bwd_group_body(g) for g in range(3)]

    def body(qs_ref, ks_ref, vs_ref, cos_ref, sin_ref, attn_ref, lse_ref, dattn_ref, dproj_in, dproj_ref, dsum, *scratch):
        del dproj_in
        head0 = _attn_masks()[2]
        for i in range(SEQ // BLOCK):
            rows = pl.ds(i * BLOCK, BLOCK)
            prod = dattn_ref[rows, :] * attn_ref[rows, :]
            d0 = jnp.sum(jnp.where(head0, prod, 0.0), axis=1, keepdims=True)
            d1 = jnp.sum(jnp.where(head0, 0.0, prod), axis=1, keepdims=True)
            dsum[rows, :] = jnp.where(head0, d0, d1)
        for g in range(3):
            groups[g](qs_ref, ks_ref, vs_ref, cos_ref, sin_ref, lse_ref, dattn_ref, dsum, dproj_ref, *scratch)

    def col(base):
        return pl.BlockSpec((SEQ, 128), lambda hp, base=base: (0, base + hp))

    table = pl.BlockSpec((3, SEQ, 128), lambda hp: (0, 0, 0), pipeline_mode=pl.Buffered(1))
    by_phase = pl.BlockSpec((3, SEQ, 128), lambda hp: (0, 0, hp))
    return _call(
        body, "attn_bwd", (4,), [by_phase] * 3 + [table, table, col(0), col(0), col(0), ANY],
        [ANY], [_sds((SEQ, IN_WIDTH), BF16)],
        [pltpu.VMEM((SEQ, 128), F32)]
        + [pltpu.VMEM((SEQ + BLOCK, 128), BF16)] * 2 + [pltpu.VMEM((SEQ, 128), BF16)]
        + [pltpu.VMEM((SEQ, 128), F32)] * 3 + [pltpu.VMEM((SEQ + BLOCK, 128), F32)] * 2 + [pltpu.VMEM((SEQ, 128), F32)]
        + [pltpu.VMEM((3, SEQ, 128), BF16), pltpu.SemaphoreType.DMA((3,))],
        [q_pm, k_pm, v_pm, cos_t, sin_t, attn, lse, dattn, dproj], ride, aliases={8: 0})


SSM_CHUNKS = 4
CHUNK_STATES = 512
SCAN_ROWS = 8
U_COL = (3 * QKV_WIDTH) // 128


def _cmul(xr, xi, yr, yi):
    return xr * yr - xi * yi, xr * yi + xi * yr


def _ssm_prep(a_re, a_im, log_dt, b_re_t, b_im_t):
    def body(ar_ref, ai_ref, ldt_ref, br_ref, bi_ref, abr_ref, abi_ref, er_ref, ei_ref, bbr_ref, bbi_ref):
        ar, ai = ar_ref[...], ai_ref[...]
        dt = jnp.exp(ldt_ref[...])
        mag = jnp.exp(ar * dt)
        abr, abi = mag * jnp.cos(ai * dt), mag * jnp.sin(ai * dt)
        den = ar * ar + ai * ai
        nr, ni = abr - 1.0, abi
        er, ei = (nr * ar + ni * ai) / den, (ni * ar - nr * ai) / den
        abr_ref[...], abi_ref[...], er_ref[...], ei_ref[...] = abr, abi, er, ei
        er3, ei3 = er[:, None, :], ei[:, None, :]
        br, bi = br_ref[...], bi_ref[...]
        bbr_ref[...] = er3 * br - ei3 * bi
        bbi_ref[...] = er3 * bi + ei3 * br

    gp = jax.ShapeDtypeStruct(a_re.shape, F32)
    gb = jax.ShapeDtypeStruct(b_re_t.shape, F32)
    return _pallas_call(body, name="ssm_prep", out_shape=(gp, gp, gp, gp, gb, gb))(a_re, a_im, log_dt, b_re_t, b_im_t)


def _ssm_param_bwd(a_re, a_im, log_dt, b_re_t, b_im_t, abar_re, abar_im, e_re, e_im, ga_re, ga_im, gbb_re_t, gbb_im_t):
    def body(ar_ref, ai_ref, ldt_ref, br_ref, bi_ref, abr_ref, abi_ref, er_ref, ei_ref, gar_ref, gai_ref, gbr_ref, gbi_ref,
             o_ar, o_ai, o_ldt, o_br, o_bi):
        ar, ai = ar_ref[...], ai_ref[...]
        dt = jnp.exp(ldt_ref[...])
        er, ei = er_ref[...], ei_ref[...]
        br, bi, gbr, gbi = br_ref[...], bi_ref[...], gbr_ref[...], gbi_ref[...]
        er3, ei3 = er[:, None, :], ei[:, None, :]
        o_br[...] = er3 * gbr + ei3 * gbi
        o_bi[...] = er3 * gbi - ei3 * gbr
        ge_r = jnp.sum(br * gbr + bi * gbi, axis=1)
        ge_i = jnp.sum(br * gbi - bi * gbr, axis=1)
        den = ar * ar + ai * ai
        ilr, ili = ar / den, -ai / den
        t_r, t_i = _cmul(ilr, -ili, ge_r, ge_i)
        gab_r, gab_i = gar_ref[...] + t_r, gai_ref[...] + t_i
        gz_r, gz_i = _cmul(abr_ref[...], -abi_ref[...], gab_r, gab_i)
        el_r, el_i = _cmul(er, ei, ilr, ili)
        u_r, u_i = _cmul(el_r, -el_i, ge_r, ge_i)
        o_ar[...] = dt * gz_r - u_r
        o_ai[...] = dt * gz_i - u_i
        o_ldt[...] = jnp.sum(gz_r * ar + gz_i * ai, axis=1, keepdims=True) * dt

    gp = jax.ShapeDtypeStruct(a_re.shape, F32)
    gb = jax.ShapeDtypeStruct(b_re_t.shape, F32)
    return _pallas_call(body, name="ssm_param_bwd", out_shape=(gp, gp, jax.ShapeDtypeStruct(log_dt.shape, F32), gb, gb))(
        a_re, a_im, log_dt, b_re_t, b_im_t, abar_re, abar_im, e_re, e_im, ga_re, ga_im, gbb_re_t, gbb_im_t)


def _block_diag(blocks_re, blocks_im, sign_im, rows_are_channels):
    both = jnp.stack([blocks_re, sign_im * blocks_im]).reshape(2, SSM_CHUNKS, 8, SSM_GROUP, SSM_STATE)
    eye = jnp.eye(8, dtype=F32)
    if rows_are_channels:
        return jnp.einsum("rcghp,gk->cghrkp", both, eye).reshape(SSM_CHUNKS, 128, 2 * CHUNK_STATES)
    return jnp.einsum("rcghp,gk->crkpgh", both, eye).reshape(SSM_CHUNKS, 2 * CHUNK_STATES, 128)


def _block_diag_parts(mat, rows_are_channels):
    if rows_are_channels:
        six = mat.reshape(SSM_CHUNKS, 8, SSM_GROUP, 2, 8, SSM_STATE)
        parts = jnp.einsum("cghrgp->rcghp", six)
    else:
        six = mat.reshape(SSM_CHUNKS, 2, 8, SSM_STATE, 8, SSM_GROUP)
        parts = jnp.einsum("crgpgh->rcghp", six)
    parts = parts.reshape(2, SSM_GROUPS, SSM_GROUP, SSM_STATE)
    return parts[0], parts[1]


def _scan_consts(a_ref, conj, reverse):
    ar = jnp.broadcast_to(a_ref[:, :CHUNK_STATES], (SCAN_ROWS, CHUNK_STATES))
    ai = jnp.broadcast_to(a_ref[:, CHUNK_STATES:], (SCAN_ROWS, CHUNK_STATES))
    if conj:
        ai = -ai
    row = lax.broadcasted_iota(jnp.int32, (SCAN_ROWS, CHUNK_STATES), 0)
    if reverse:
        row = SCAN_ROWS - 1 - row
    zero = jnp.zeros_like(ar)
    steps = []
    pr, pi = ar, ai
    for shift in (1, 2, 4):
        keep = row >= shift
        steps.append((SCAN_ROWS - shift if reverse else shift, jnp.where(keep, pr, zero), jnp.where(keep, pi, zero)))
        pr, pi = _cmul(pr, pi, pr, pi)
    first = row == 0
    return steps, (jnp.where(first, ar, zero), jnp.where(first, ai, zero)), first


def _scan_tile(xr, xi, prev_r, prev_i, steps, carry_in, reverse):
    edge = SCAN_ROWS - 1 if reverse else 1
    cr, ci = pltpu.roll(prev_r, edge, axis=0), pltpu.roll(prev_i, edge, axis=0)
    xr, xi = xr + carry_in[0] * cr - carry_in[1] * ci, xi + carry_in[0] * ci + carry_in[1] * cr
    for shift, mr, mi in steps:
        sr, si = pltpu.roll(xr, shift, axis=0), pltpu.roll(xi, shift, axis=0)
        xr, xi = xr + mr * sr - mi * si, xi + mr * si + mi * sr
    return xr, xi


MM_ROWS = 256


def _ssm_fwd(proj, bmat, cmat, a_chunks, d_skip, ride=None):
    def body(u_ref, b_ref, c_ref, a_ref, d_ref, y_ref, states_ref, h_ref):
        for i in range(SEQ // MM_ROWS):
            rows = pl.ds(i * MM_ROWS, MM_ROWS)
            h_ref[rows, :] = _dot(u_ref[rows, :].astype(BF16), b_ref[...])
        steps, carry_in, _ = _scan_consts(a_ref, conj=False, reverse=False)

        def tile(k, carry):
            rows = pl.ds(pl.multiple_of(k * SCAN_ROWS, SCAN_ROWS), SCAN_ROWS)
            xr, xi = _scan_tile(h_ref[rows, :CHUNK_STATES], h_ref[rows, CHUNK_STATES:], carry[0], carry[1], steps, carry_in, False)
            h_ref[rows, :CHUNK_STATES] = xr
            h_ref[rows, CHUNK_STATES:] = xi
            return xr, xi

        zero = jnp.zeros((SCAN_ROWS, CHUNK_STATES), F32)
        lax.fori_loop(0, SEQ // SCAN_ROWS, tile, (zero, zero), unroll=4)
        for i in range(SEQ // MM_ROWS):
            rows = pl.ds(i * MM_ROWS, MM_ROWS)
            states = h_ref[rows, :].astype(BF16)
            states_ref[rows, :] = states
            y_ref[rows, :] = _dot(states, c_ref[...]) + d_ref[...] * u_ref[rows, :]

    return _call(
        body, "ssm_fwd", (SSM_CHUNKS,),
        [pl.BlockSpec((SEQ, 128), lambda c: (0, U_COL + c)),
         pl.BlockSpec((None, 128, 2 * CHUNK_STATES), lambda c: (c, 0, 0)),
         pl.BlockSpec((None, 2 * CHUNK_STATES, 128), lambda c: (c, 0, 0)),
         pl.BlockSpec((None, 1, 2 * CHUNK_STATES), lambda c: (c, 0, 0)),
         pl.BlockSpec((1, 128), lambda c: (0, c))],
        [pl.BlockSpec((SEQ, 128), lambda c: (0, c)), pl.BlockSpec((SEQ, 2 * CHUNK_STATES), lambda c: (0, c))],
        [_sds((SEQ, SSM_WIDTH), F32), _sds((SEQ, SSM_CHUNKS * 2 * CHUNK_STATES), BF16)],
        [pltpu.VMEM((SEQ, 2 * CHUNK_STATES), F32)],
        [proj, bmat, cmat, a_chunks, d_skip], ride)


def _ssm_bwd(dys, proj, h, bmat, cmat, a_chunks, d_skip, dproj, ride=None):
    def body(dy_ref, u_ref, states_ref, b_ref, c_ref, a_ref, d_ref, dproj_in, du_ref, db_ref, dc_ref, da_ref, dd_ref,
             g_ref, h_ref):
        del dproj_in
        dsum = jnp.zeros((1, 128), F32)
        dcm = jnp.zeros((2 * CHUNK_STATES, 128), F32)
        for i in range(SEQ // MM_ROWS):
            rows = pl.ds(i * MM_ROWS, MM_ROWS)
            h_ref[rows, :] = states_ref[rows, :].astype(F32)
            dy = dy_ref[rows, :]
            g_ref[rows, :] = _dot_nt(dy.astype(BF16), c_ref[...])
            dsum += jnp.sum(dy * u_ref[rows, :], axis=0, keepdims=True)
            dcm += _dot_tn(states_ref[rows, :], dy.astype(BF16))
        dd_ref[...] = dsum
        dc_ref[...] = dcm
        steps, carry_in, _ = _scan_consts(a_ref, conj=True, reverse=True)
        first_row = lax.broadcasted_iota(jnp.int32, (SCAN_ROWS, CHUNK_STATES), 0) == 0
        n_tiles = SEQ // SCAN_ROWS

        def tile(j, carry):
            k = n_tiles - 1 - j
            rows = pl.ds(pl.multiple_of(k * SCAN_ROWS, SCAN_ROWS), SCAN_ROWS)
            before = pl.ds(pl.multiple_of(jnp.maximum(k - 1, 0) * SCAN_ROWS, SCAN_ROWS), SCAN_ROWS)
            gr, gi = _scan_tile(g_ref[rows, :CHUNK_STATES], g_ref[rows, CHUNK_STATES:], carry[0], carry[1], steps, carry_in, True)
            g_ref[rows, :CHUNK_STATES] = gr
            g_ref[rows, CHUNK_STATES:] = gi
            has_before = jnp.where(k > 0, 1.0, 0.0)
            hr = jnp.where(first_row, pltpu.roll(h_ref[before, :CHUNK_STATES], 1, axis=0) * has_before,
                           pltpu.roll(h_ref[rows, :CHUNK_STATES], 1, axis=0))
            hi = jnp.where(first_row, pltpu.roll(h_ref[before, CHUNK_STATES:], 1, axis=0) * has_before,
                           pltpu.roll(h_ref[rows, CHUNK_STATES:], 1, axis=0))
            return gr, gi, carry[2] + hr * gr + hi * gi, carry[3] + hr * gi - hi * gr

        zero = jnp.zeros((SCAN_ROWS, CHUNK_STATES), F32)
        _, _, sar, sai = lax.fori_loop(0, n_tiles, tile, (zero, zero, zero, zero), unroll=4)
        da_ref[:, :CHUNK_STATES] = jnp.sum(sar, axis=0, keepdims=True)
        da_ref[:, CHUNK_STATES:] = jnp.sum(sai, axis=0, keepdims=True)
        dbm = jnp.zeros((128, 2 * CHUNK_STATES), F32)
        for i in range(SEQ // MM_ROWS):
            rows = pl.ds(i * MM_ROWS, MM_ROWS)
            g = g_ref[rows, :].astype(BF16)
            du_ref[rows, :] = (_dot_nt(g, b_ref[...]) + d_ref[...] * dy_ref[rows, :]).astype(BF16)
            dbm += _dot_tn(u_ref[rows, :].astype(BF16), g)
        db_ref[...] = dbm

    chunk_col = pl.BlockSpec((SEQ, 128), lambda c: (0, c))
    return _call(
        body, "ssm_bwd", (SSM_CHUNKS,),
        [chunk_col,
         pl.BlockSpec((SEQ, 128), lambda c: (0, U_COL + c)),
         pl.BlockSpec((SEQ, 2 * CHUNK_STATES), lambda c: (0, c)),
         pl.BlockSpec((None, 128, 2 * CHUNK_STATES), lambda c: (c, 0, 0)),
         pl.BlockSpec((None, 2 * CHUNK_STATES, 128), lambda c: (c, 0, 0)),
         pl.BlockSpec((None, 1, 2 * CHUNK_STATES), lambda c: (c, 0, 0)),
         pl.BlockSpec((1, 128), lambda c: (0, c)), ANY],
        [pl.BlockSpec((SEQ, 128), lambda c: (0, U_COL + c)),
         pl.BlockSpec((None, 128, 2 * CHUNK_STATES), lambda c: (c, 0, 0)),
         pl.BlockSpec((None, 2 * CHUNK_STATES, 128), lambda c: (c, 0, 0)),
         pl.BlockSpec((None, 1, 2 * CHUNK_STATES), lambda c: (c, 0, 0)),
         pl.BlockSpec((1, 128), lambda c: (0, c))],
        [_sds((SEQ, IN_WIDTH), BF16), _sds((SSM_CHUNKS, 128, 2 * CHUNK_STATES), F32),
         _sds((SSM_CHUNKS, 2 * CHUNK_STATES, 128), F32), _sds((SSM_CHUNKS, 1, 2 * CHUNK_STATES), F32), _sds((1, SSM_WIDTH), F32)],
        [pltpu.VMEM((SEQ, 2 * CHUNK_STATES), F32)] * 2, [dys, proj, h, bmat, cmat, a_chunks, d_skip, dproj], ride, aliases={7: 0})


def _ssm_tables(abar_re, abar_im, bbar_re_t, bbar_im_t, c_re, c_im):
    bmat = _block_diag(bbar_re_t, bbar_im_t, 1.0, True).astype(BF16)
    cmat = _block_diag(c_re, c_im, -1.0, False).astype(BF16)
    a_chunks = jnp.concatenate([abar_re.reshape(SSM_CHUNKS, 1, CHUNK_STATES), abar_im.reshape(SSM_CHUNKS, 1, CHUNK_STATES)], axis=2)
    return bmat, cmat, a_chunks


GL_COL = (3 * QKV_WIDTH + SSM_WIDTH) // D_MODEL
GELU_C = math.sqrt(2.0 / math.pi)
GELU_A = 0.044715


def _sds(shape, dtype):
    return jax.ShapeDtypeStruct(shape, dtype)


def _gelu(x):
    t = jnp.tanh(GELU_C * (x + GELU_A * x * x * x))
    return 0.5 * x * (1.0 + t), t


def _gelu_grad(x, t):
    return 0.5 * (1.0 + t) + 0.5 * x * (1.0 - t * t) * GELU_C * (1.0 + 3.0 * GELU_A * x * x)


def _layer_norm(r, g, b):
    mu = jnp.mean(r, axis=-1, keepdims=True)
    xc = r - mu
    rstd = lax.rsqrt(jnp.mean(xc * xc, axis=-1, keepdims=True) + LN_EPS)
    xhat = xc * rstd
    return xhat * g + b, xhat, rstd


def _layer_norm_bwd(dy, xhat, rstd, g):
    dxhat = dy * g
    m1 = jnp.mean(dxhat, axis=-1, keepdims=True)
    m2 = jnp.mean(dxhat * xhat, axis=-1, keepdims=True)
    return rstd * (dxhat - m1 - xhat * m2)


def _proj(x, w_in, ride=None):
    tm, tn = 1024, 1792

    def body(x_ref, w_ref, o_ref):
        o_ref[...] = _dot(x_ref[...].astype(BF16), _side_by_side(w_ref))

    return _call(
        body, "proj", (SEQ // tm, IN_WIDTH // tn),
        [pl.BlockSpec((tm, D_MODEL), lambda i, j: (i, 0)), pl.BlockSpec((2, D_MODEL, tn // 2), lambda i, j: (j, 0, 0))],
        [pl.BlockSpec((tm, tn), lambda i, j: (i, j))], [_sds((SEQ, IN_WIDTH), F32)], [], [x, w_in], ride)


def _row_spec(tm, width, col=0):
    return pl.BlockSpec((tm, width), lambda i, col=col: (i, col))


def _full_spec(shape):
    return pl.BlockSpec(shape, lambda i: (0,) * len(shape))


def _weight_spec(shape):
    return pl.BlockSpec(shape, lambda i: (0,) * len(shape), pipeline_mode=pl.Buffered(1))


def _mixer_out(attn, ys, proj, x, w_ab, w_sb, w_glu, w_out, b_gate, ln_g, ln_b, ride=None):
    tm = 512

    def body(attn_ref, ys_ref, gl0_ref, gl1_ref, x_ref, wab_ref, wsb_ref, wglu_ref, wout_ref, bg_ref, g_ref, b_ref,
             h_ref, xhat_ref, rstd_ref, glu_ref, ya_ref, yssm_ref):
        gy, _ = _gelu(ys_ref[...])
        glu = _dot(gy.astype(BF16), _side_by_side(wglu_ref))
        glu_ref[...] = glu.astype(BF16)
        y_s = glu[:, :SSM_WIDTH] * jax.nn.sigmoid(glu[:, SSM_WIDTH:])
        y_ssm = _dot(y_s.astype(BF16), _side_by_side(wsb_ref))
        y_attn = _dot(attn_ref[...].astype(BF16), _side_by_side(wab_ref))
        ya_ref[...] = y_attn.astype(BF16)
        yssm_ref[...] = y_ssm.astype(BF16)
        g0 = jax.nn.sigmoid(gl0_ref[...] + _side_by_side(bg_ref, 0))
        g1 = jax.nn.sigmoid(gl1_ref[...] + _side_by_side(bg_ref, 1))
        mixed = g0 * y_attn + g1 * y_ssm
        r1 = DN_ALPHA * x_ref[...] + _dot(mixed.astype(BF16), wout_ref[...])
        h, xhat, rstd = _layer_norm(r1, g_ref[...], b_ref[...])
        h_ref[...] = h
        xhat_ref[...] = xhat
        rstd_ref[...] = jnp.broadcast_to(rstd, (tm, 128))

    wide = _sds((SEQ, D_MODEL), F32)
    return _call(
        body, "mixer_out", (SEQ // tm,),
        [_row_spec(tm, ATTN_WIDTH), _row_spec(tm, SSM_WIDTH), _row_spec(tm, D_MODEL, GL_COL), _row_spec(tm, D_MODEL, GL_COL + 1),
         _row_spec(tm, D_MODEL), _weight_spec((N_DEV, ATTN_WIDTH, 128)), _weight_spec((N_DEV, SSM_WIDTH, 128)),
         _weight_spec((N_DEV, SSM_WIDTH, 128)), _weight_spec((D_MODEL, D_MODEL)), _full_spec((N_DEV, 2, 128)),
         _full_spec((1, D_MODEL)), _full_spec((1, D_MODEL))],
        [_row_spec(tm, D_MODEL), _row_spec(tm, D_MODEL), _row_spec(tm, 128), _row_spec(tm, D_MODEL),
         _row_spec(tm, D_MODEL), _row_spec(tm, D_MODEL)],
        [wide, wide, _sds((SEQ, 128), F32)] + [_sds((SEQ, D_MODEL), BF16)] * 3, [],
        [attn, ys, proj, proj, x, w_ab, w_sb, w_glu, w_out, b_gate, ln_g, ln_b], ride)


def _ff_up(h, w_gate, w_up, ride=None):
    tm, tn = 1024, 768

    def body(h_ref, wg_ref, wu_ref, a_ref, b_ref, f_ref):
        hb = h_ref[...].astype(BF16)
        a, b = _dot(hb, _side_by_side(wg_ref)), _dot(hb, _side_by_side(wu_ref))
        a_ref[...] = a.astype(BF16)
        b_ref[...] = b.astype(BF16)
        f_ref[...] = (a * jax.nn.sigmoid(a) * b).astype(BF16)

    tile = pl.BlockSpec((tm, tn), lambda i, j: (i, j))
    wtile = pl.BlockSpec((tn // FF_PAD, D_MODEL, FF_PAD), lambda i, j: (j, 0, 0))
    out = _sds((SEQ, D_FF_PAD), BF16)
    return _call(body, "ff_up", (SEQ // tm, D_FF_PAD // tn), [pl.BlockSpec((tm, D_MODEL), lambda i, j: (i, 0)), wtile, wtile],
                 [tile, tile, tile], [out, out, out], [], [h, w_gate, w_up], ride)


def _ff_down_loss(f, w_down, h, target, ln_g, ln_b):
    tm = 512

    def body(f_ref, w_ref, h_ref, t_ref, g_ref, b_ref, dr_ref, dg_ref, db_ref, loss_ref):
        @pl.when(pl.program_id(0) == 0)
        def _():
            dg_ref[...] = jnp.zeros_like(dg_ref)
            db_ref[...] = jnp.zeros_like(db_ref)
            loss_ref[...] = jnp.zeros_like(loss_ref)

        r2 = DN_ALPHA * h_ref[...] + _dot(f_ref[...], w_ref[...])
        g = g_ref[...]
        out, xhat, rstd = _layer_norm(r2, g, b_ref[...])
        err = out - t_ref[...]
        loss_ref[...] += 0.5 * jnp.sum(jnp.mean(err * err, axis=-1, keepdims=True), axis=0, keepdims=True)
        dout = err * (1.0 / D_MODEL)
        dg_ref[...] += jnp.sum(dout * xhat, axis=0, keepdims=True)
        db_ref[...] += jnp.sum(dout, axis=0, keepdims=True)
        dr_ref[...] = _layer_norm_bwd(dout, xhat, rstd, g)

    vec = _sds((1, D_MODEL), F32)
    return _pallas_call(
        body, name="ff_down_loss", grid=(SEQ // tm,),
        in_specs=[_row_spec(tm, D_FF_PAD), _weight_spec((D_FF_PAD, D_MODEL)), _row_spec(tm, D_MODEL), _row_spec(tm, D_MODEL),
                  _full_spec((1, D_MODEL)), _full_spec((1, D_MODEL))],
        out_specs=(_row_spec(tm, D_MODEL), _full_spec((1, D_MODEL)), _full_spec((1, D_MODEL)), _full_spec((1, 128))),
        out_shape=(_sds((SEQ, D_MODEL), F32), vec, vec, _sds((1, 128), F32)),
        compiler_params=_cparams(dimension_semantics=("arbitrary",)),
    )(f, w_down, h, target, ln_g, ln_b)


def _ff_down_bwd(dr2, w_down, a, b):
    tm, tn = 1024, 768

    def body(dr_ref, w_ref, a_ref, b_ref, da_ref, db_ref):
        df = _dot_nt(dr_ref[...].astype(BF16), w_ref[...])
        av, bv = a_ref[...].astype(F32), b_ref[...].astype(F32)
        sg = jax.nn.sigmoid(av)
        da_ref[...] = (df * bv * sg * (1.0 + av * (1.0 - sg))).astype(BF16)
        db_ref[...] = (df * av * sg).astype(BF16)

    tile = pl.BlockSpec((tm, tn), lambda i, j: (i, j))
    out = _sds((SEQ, D_FF_PAD), BF16)
    return _pallas_call(
        body, name="ff_down_bwd", grid=(SEQ // tm, D_FF_PAD // tn),
        in_specs=[pl.BlockSpec((tm, D_MODEL), lambda i, j: (i, 0)), pl.BlockSpec((tn, D_MODEL), lambda i, j: (j, 0)), tile, tile],
        out_specs=(tile, tile), out_shape=(out, out),
        compiler_params=_cparams(dimension_semantics=("arbitrary", "arbitrary")),
    )(dr2, w_down, a, b)


def _ff_up_bwd(da, db, w_gate, w_up, dr2, xhat1, rstd1, ln_g, ride=None):
    tm, tk = 1024, 768
    nk = D_FF_PAD // tk

    def body(da_ref, db_ref, wg_ref, wu_ref, dr2_ref, xhat_ref, rstd_ref, g_ref, dr1_ref, dg_ref, dbias_ref, acc):
        i, k = pl.program_id(0), pl.program_id(1)

        @pl.when(jnp.logical_and(i == 0, k == 0))
        def _():
            dg_ref[...] = jnp.zeros_like(dg_ref)
            dbias_ref[...] = jnp.zeros_like(dbias_ref)

        part = _dot_nt(da_ref[...], _side_by_side(wg_ref)) + _dot_nt(db_ref[...], _side_by_side(wu_ref))

        @pl.when(k == 0)
        def _():
            acc[...] = part

        @pl.when(k > 0)
        def _():
            acc[...] += part

        @pl.when(k == nk - 1)
        def _():
            dh = DN_ALPHA * dr2_ref[...] + acc[...]
            xhat = xhat_ref[...]
            dg_ref[...] += jnp.sum(dh * xhat, axis=0, keepdims=True)
            dbias_ref[...] += jnp.sum(dh, axis=0, keepdims=True)
            rstd = jnp.max(rstd_ref[...], axis=1, keepdims=True)
            dr1_ref[...] = _layer_norm_bwd(dh, xhat, rstd, g_ref[...])

    hid = pl.BlockSpec((tm, tk), lambda i, k: (i, k))
    wtile = pl.BlockSpec((tk // FF_PAD, D_MODEL, FF_PAD), lambda i, k: (k, 0, 0))
    row = pl.BlockSpec((tm, D_MODEL), lambda i, k: (i, 0))
    vec = pl.BlockSpec((1, D_MODEL), lambda i, k: (0, 0))
    return _call(
        body, "ff_up_bwd", (SEQ // tm, nk),
        [hid, hid, wtile, wtile, row, row, pl.BlockSpec((tm, 128), lambda i, k: (i, 0)), vec],
        [row, vec, vec], [_sds((SEQ, D_MODEL), F32), _sds((1, D_MODEL), F32), _sds((1, D_MODEL), F32)],
        [pltpu.VMEM((tm, D_MODEL), F32)], [da, db, w_gate, w_up, dr2, xhat1, rstd1, ln_g], ride)


def _mixer_bwd(dr1, proj, y_attn, y_ssm, glu, ys, w_ab, w_sb, w_glu, w_out, b_gate):
    tm = 256

    def body(dr1_ref, gl0_ref, gl1_ref, ya_ref, yssm_ref, glu_ref, ys_ref, wab_ref, wsb_ref, wglu_ref, wout_ref, bg_ref,
             dya_ref, dyssm_ref, dgl_ref, dattn_ref, dglu_ref, dys_ref, mixed_ref, ysb_ref, gy_ref, dbg_ref):
        @pl.when(pl.program_id(0) == 0)
        def _():
            dbg_ref[...] = jnp.zeros_like(dbg_ref)

        dmixed = _dot_nt(dr1_ref[...].astype(BF16), wout_ref[...])
        g0 = jax.nn.sigmoid(gl0_ref[...] + _side_by_side(bg_ref, 0))
        g1 = jax.nn.sigmoid(gl1_ref[...] + _side_by_side(bg_ref, 1))
        y_attn, y_ssm = ya_ref[...].astype(F32), yssm_ref[...].astype(F32)
        mixed_ref[...] = (g0 * y_attn + g1 * y_ssm).astype(BF16)
        dya = (dmixed * g0).astype(BF16)
        dyssm = (dmixed * g1).astype(BF16)
        dya_ref[...] = dya
        dyssm_ref[...] = dyssm
        dgl0 = dmixed * y_attn * g0 * (1.0 - g0)
        dgl1 = dmixed * y_ssm * g1 * (1.0 - g1)
        dgl_ref[:, :GL_COL * D_MODEL] = jnp.zeros((tm, GL_COL * D_MODEL), BF16)
        dgl_ref[:, GL_COL * D_MODEL:(GL_COL + 1) * D_MODEL] = dgl0.astype(BF16)
        dgl_ref[:, (GL_COL + 1) * D_MODEL:] = dgl1.astype(BF16)
        dbg_ref[:, :D_MODEL] += jnp.sum(dgl0, axis=0, keepdims=True)
        dbg_ref[:, D_MODEL:] += jnp.sum(dgl1, axis=0, keepdims=True)
        dattn_ref[...] = _dot_nt(dya, _side_by_side(wab_ref))
        dy_s = _dot_nt(dyssm, _side_by_side(wsb_ref))
        glu = glu_ref[...].astype(F32)
        glu1, sg = glu[:, :SSM_WIDTH], jax.nn.sigmoid(glu[:, SSM_WIDTH:])
        ysb_ref[...] = (glu1 * sg).astype(BF16)
        dglu1 = (dy_s * sg).astype(BF16)
        dglu2 = (dy_s * glu1 * sg * (1.0 - sg)).astype(BF16)
        dglu_ref[:, :SSM_WIDTH] = dglu1
        dglu_ref[:, SSM_WIDTH:] = dglu2
        dgy = _dot_nt(jnp.concatenate([dglu1, dglu2], axis=1), _side_by_side(wglu_ref))
        ys = ys_ref[...]
        gy, t = _gelu(ys)
        gy_ref[...] = gy.astype(BF16)
        dys_ref[...] = dgy * _gelu_grad(ys, t)

    wide_b, half_b = _sds((SEQ, D_MODEL), BF16), _sds((SEQ, SSM_WIDTH), BF16)
    half_f = _sds((SEQ, SSM_WIDTH), F32)
    return _pallas_call(
        body, name="mixer_bwd", grid=(SEQ // tm,),
        in_specs=[_row_spec(tm, D_MODEL), _row_spec(tm, D_MODEL, GL_COL), _row_spec(tm, D_MODEL, GL_COL + 1), _row_spec(tm, D_MODEL),
                  _row_spec(tm, D_MODEL), _row_spec(tm, D_MODEL), _row_spec(tm, SSM_WIDTH), _full_spec((N_DEV, ATTN_WIDTH, 128)),
                  _full_spec((N_DEV, SSM_WIDTH, 128)), _full_spec((N_DEV, SSM_WIDTH, 128)), _full_spec((D_MODEL, D_MODEL)),
                  _full_spec((N_DEV, 2, 128))],
        out_specs=(_row_spec(tm, D_MODEL), _row_spec(tm, D_MODEL), _row_spec(tm, IN_WIDTH), _row_spec(tm, ATTN_WIDTH),
                   _row_spec(tm, D_MODEL), _row_spec(tm, SSM_WIDTH), _row_spec(tm, D_MODEL), _row_spec(tm, SSM_WIDTH),
                   _row_spec(tm, SSM_WIDTH), _full_spec((1, 2 * D_MODEL))),
        out_shape=(wide_b, wide_b, _sds((SEQ, IN_WIDTH), BF16), half_f, wide_b, half_f, wide_b, half_b, half_b,
                   _sds((1, 2 * D_MODEL), F32)),
        compiler_params=_cparams(dimension_semantics=("arbitrary",)),
    )(dr1, proj, proj, y_attn, y_ssm, glu, ys, w_ab, w_sb, w_glu, w_out, b_gate)


def _grad_x(dproj, w_in, dr1, ride=None):
    tm, tk = 1024, 1792
    nk = IN_WIDTH // tk

    def body(dp_ref, w_ref, dr1_ref, o_ref, acc):
        k = pl.program_id(1)
        part = _dot_nt(dp_ref[...], _side_by_side(w_ref))

        @pl.when(k == 0)
        def _():
            acc[...] = part

        @pl.when(k > 0)
        def _():
            acc[...] += part

        @pl.when(k == nk - 1)
        def _():
            o_ref[...] = DN_ALPHA * dr1_ref[...] + acc[...]

    row = pl.BlockSpec((tm, D_MODEL), lambda i, k: (i, 0))
    return _call(
        body, "grad_x", (SEQ // tm, nk),
        [pl.BlockSpec((tm, tk), lambda i, k: (i, k)), pl.BlockSpec((2, D_MODEL, tk // 2), lambda i, k: (k, 0, 0)), row],
        [row], [_sds((SEQ, D_MODEL), F32)], [pltpu.VMEM((tm, D_MODEL), F32)], [dproj, w_in, dr1], ride)


def _weight_grad(a, b, name, shard_cols=None, ride=None):
    k, n = a.shape[1], b.shape[1]
    tk = min(k, 512) if shard_cols else k // N_DEV
    tn = n // 4 if shard_cols else min(n, 1024)

    def body(a_ref, b_ref, o_ref):
        grad = _dot_tn(a_ref[...].astype(BF16), b_ref[...].astype(BF16))
        if shard_cols:
            o_ref[0] = grad[:, :shard_cols].astype(BF16)
            o_ref[1] = grad[:, shard_cols:].astype(BF16)
        else:
            o_ref[...] = grad.astype(BF16)

    if shard_cols:
        out_spec = pl.BlockSpec((2, None, tk, shard_cols), lambda kk, j: (0, j, kk, 0))
        out_shape = _sds((2, 4, k, shard_cols), BF16)
    else:
        out_spec = pl.BlockSpec((None, None, tk, tn), lambda kk, j: (kk % 2, kk // 2, 0, j))
        out_shape = _sds((2, 4, tk, n), BF16)
    out = _call(body, name, (k // tk, n // tn),
                [pl.BlockSpec((SEQ, tk), lambda kk, j: (0, kk)), pl.BlockSpec((SEQ, tn), lambda kk, j: (0, j))],
                [out_spec], [out_shape], [], [a, b], ride)
    return out[0] if ride is None else out


MESH = pl.DeviceIdType.MESH
ANY = pl.BlockSpec(memory_space=pl.ANY)


def _place():
    return lax.axis_index("x"), lax.axis_index("y"), lax.axis_index("c")


def _other_chips(x, y):
    return [(1 - x, y), (x, 1 - y), (1 - x, 1 - y)]


class _Ride:
    def __init__(self, operands, results, aliases, sems, start, wait):
        self.operands, self.results, self.aliases, self.sems = list(operands), list(results), dict(aliases), list(sems)
        self.start, self.wait = start, wait

    def __add__(self, other):
        n_in, n_out, n_sem = len(self.operands), len(self.results), len(self.sems)

        def both(which):
            def run(ins, outs, sems):
                getattr(self, which)(ins[:n_in], outs[:n_out], sems[:n_sem])
                getattr(other, which)(ins[n_in:], outs[n_out:], sems[n_sem:])
            return run

        aliases = {**self.aliases, **{n_in + i: n_out + j for i, j in other.aliases.items()}}
        return _Ride(self.operands + other.operands, self.results + other.results, aliases, self.sems + other.sems,
                     both("start"), both("wait"))


def _call(body, name, grid, in_specs, out_specs, out_shape, scratch_shapes, operands, ride=None, aliases=None):
    in_specs, out_specs, out_shape = list(in_specs), list(out_specs), list(out_shape)
    scratch_shapes, operands, aliases = list(scratch_shapes), list(operands), dict(aliases or {})
    kernel_body = body
    if ride is not None:
        n_in, n_out, n_scr, r_in, r_out = len(in_specs), len(out_specs), len(scratch_shapes), len(ride.operands), len(ride.results)

        def kernel_body(*refs):
            out0, scr0 = n_in + r_in, n_in + r_in + n_out + r_out
            ride_refs = (refs[n_in:out0], refs[out0 + n_out:scr0], refs[scr0 + n_scr:])
            ids = [pl.program_id(i) for i in range(len(grid))]
            first = functools.reduce(jnp.logical_and, [i == 0 for i in ids])
            last = functools.reduce(jnp.logical_and, [i == g - 1 for i, g in zip(ids, grid)])

            @pl.when(first)
            def _():
                ride.start(*ride_refs)

            body(*refs[:n_in], *refs[out0:out0 + n_out], *refs[scr0:scr0 + n_scr])

            @pl.when(last)
            def _():
                ride.wait(*ride_refs)

        aliases.update({n_in + i: n_out + j for i, j in ride.aliases.items()})
        in_specs += [ANY] * r_in
        out_specs += [ANY] * r_out
        out_shape += ride.results
        scratch_shapes += ride.sems
        operands += ride.operands
    return _pallas_call(
        kernel_body, name=name, grid=grid, in_specs=in_specs, out_specs=out_specs, out_shape=out_shape,
        scratch_shapes=scratch_shapes, input_output_aliases=aliases,
        compiler_params=_cparams(dimension_semantics=("arbitrary",) * len(grid)),
    )(*operands)


def _after(*arrays):
    return _Ride(arrays, [], {}, [], lambda *refs: None, lambda *refs: None)


def _gather_first_level(shards):
    n = len(shards)

    def copies(ins, outs, sems, landed):
        send_sems, recv_sems, local_sems = sems
        x, y, c = _place()
        peers = [(x, y, 1 - c)] + [(px, py, c) for px, py in _other_chips(x, y)]

        def row(peer):
            return 4 * x + 2 * y + c if not landed else 4 * peer[0] + 2 * peer[1] + peer[2]

        local = [pltpu.make_async_copy(ins[a], outs[a].at[4 * x + 2 * y + c], local_sems.at[a]) for a in range(n)]
        remote = [pltpu.make_async_remote_copy(
            src_ref=ins[a], dst_ref=outs[a].at[row(peer)], send_sem=send_sems.at[a, k], recv_sem=recv_sems.at[a, k],
            device_id=peer, device_id_type=MESH) for a in range(n) for k, peer in enumerate(peers)]
        return local, remote

    def start(ins, outs, sems):
        local, remote = copies(ins, outs, sems, False)
        for cp in local + remote:
            cp.start()

    def wait(ins, outs, sems):
        local, sent = copies(ins, outs, sems, False)
        for cp in copies(ins, outs, sems, True)[1]:
            cp.wait_recv()
        for cp in sent:
            cp.wait_send()
        for cp in local:
            cp.wait()

    return _Ride(shards, [_sds((N_DEV,) + s.shape, s.dtype) for s in shards], {},
                 [pltpu.SemaphoreType.DMA((n, 4)), pltpu.SemaphoreType.DMA((n, 4)), pltpu.SemaphoreType.DMA((n,))], start, wait)


def _gather_second_level(buffers):
    n = len(buffers)

    def copies(outs, sems, core):
        send_sems, recv_sems = sems
        x, y, c = _place()
        return [pltpu.make_async_remote_copy(
            src_ref=outs[a].at[4 * px + 2 * py + core], dst_ref=outs[a].at[4 * px + 2 * py + core], send_sem=send_sems.at[a, j],
            recv_sem=recv_sems.at[a, j], device_id=(x, y, 1 - c), device_id_type=MESH)
            for a in range(n) for j, (px, py) in enumerate(_other_chips(x, y))]

    def start(ins, outs, sems):
        for cp in copies(outs, sems, lax.axis_index("c")):
            cp.start()

    def wait(ins, outs, sems):
        for cp in copies(outs, sems, 1 - lax.axis_index("c")):
            cp.wait_recv()
        for cp in copies(outs, sems, lax.axis_index("c")):
            cp.wait_send()

    return _Ride(buffers, [_sds(b.shape, b.dtype) for b in buffers], {i: i for i in range(n)},
                 [pltpu.SemaphoreType.DMA((n, 3)), pltpu.SemaphoreType.DMA((n, 3))], start, wait)


def _relayed_gather(shards):
    n = len(shards)
    buffers = [_sds((N_DEV,) + s.shape, s.dtype) for s in shards]
    dma = pltpu.SemaphoreType.DMA

    def remote(src, dst, send_sem, recv_sem, to):
        return pltpu.make_async_remote_copy(src_ref=src, dst_ref=dst, send_sem=send_sem, recv_sem=recv_sem,
                                            device_id=to, device_id_type=MESH)

    def row(px, py, pc):
        return 4 * px + 2 * py + pc

    def ride(operands, aliases, sems, copies):
        def start(ins, outs, sem_refs):
            local, sent = copies(ins, outs, sem_refs, False)
            for cp in local + sent:
                cp.start()

        def wait(ins, outs, sem_refs):
            local, sent = copies(ins, outs, sem_refs, False)
            for cp in copies(ins, outs, sem_refs, True)[1]:
                cp.wait_recv()
            for cp in sent:
                cp.wait_send()
            for cp in local:
                cp.wait()

        level = _Ride(operands, buffers, aliases, sems, start, wait)
        level.copies = copies
        return level

    def neighbours():
        x, y, c = _place()
        x_first = c == 1
        return ((jnp.where(x_first, 1 - x, x), jnp.where(x_first, y, 1 - y), c),
                (jnp.where(x_first, x, 1 - x), jnp.where(x_first, 1 - y, y), c)), x_first

    def first(ins, outs, sems, landed):
        x, y, c = _place()
        peers = [(x, y, 1 - c), *neighbours()[0]]
        local = [pltpu.make_async_copy(ins[a], outs[a].at[row(x, y, c)], sems[2].at[a]) for a in range(n)]
        return local, [remote(ins[a], outs[a].at[row(*peer) if landed else row(x, y, c)], sems[0].at[a, k], sems[1].at[a, k], peer)
                       for a in range(n) for k, peer in enumerate(peers)]

    def second(ins, outs, sems, landed):
        x, y, c = _place()
        (one, two), x_first = neighbours()
        mine = 1 - c if landed else c
        copies = []
        for a in range(n):
            half = shards[a].shape[0] // 2
            from_one, from_two = outs[a].at[row(one[0], one[1], mine)], outs[a].at[row(two[0], two[1], mine)]
            diagonal = outs[a].at[row(1 - x, 1 - y, c)]
            to_one, to_two = pl.ds(jnp.where(x_first, 0, half), half), pl.ds(jnp.where(x_first, half, 0), half)
            copies += [remote(from_one, from_one, sems[0].at[a, 0], sems[1].at[a, 0], (x, y, 1 - c)),
                       remote(from_two, from_two, sems[0].at[a, 1], sems[1].at[a, 1], (x, y, 1 - c))]
            if landed:
                copies += [remote(diagonal.at[to_one], diagonal.at[to_one], sems[0].at[a, 2], sems[1].at[a, 2], one),
                           remote(diagonal.at[to_two], diagonal.at[to_two], sems[0].at[a, 3], sems[1].at[a, 3], two)]
            else:
                copies += [remote(from_two.at[to_one], from_two.at[to_one], sems[0].at[a, 2], sems[1].at[a, 2], one),
                           remote(from_one.at[to_two], from_one.at[to_two], sems[0].at[a, 3], sems[1].at[a, 3], two)]
        return [], copies

    def third(ins, outs, sems, landed):
        x, y, c = _place()
        return [], [remote(outs[a].at[row(1 - x, 1 - y, 1 - c if landed else c)], outs[a].at[row(1 - x, 1 - y, 1 - c if landed else c)],
                           sems[0].at[a], sems[1].at[a], (x, y, 1 - c)) for a in range(n)]

    def later(copies, n_sems):
        return lambda partly: ride(partly, {i: i for i in range(n)}, [dma((n,) + n_sems), dma((n,) + n_sems)], copies)

    return ride(shards, {}, [dma((n, 3)), dma((n, 3)), dma((n,))], first), later(second, (4,)), later(third, ())


def _sibling_swap_ride(grads):
    n = len(grads)

    def copies(ins, outs, sems):
        x, y, c = _place()
        return [pltpu.make_async_remote_copy(
            src_ref=ins[a].at[1 - c], dst_ref=outs[a], send_sem=sems[0].at[a], recv_sem=sems[1].at[a],
            device_id=(x, y, 1 - c), device_id_type=MESH) for a in range(n)]

    def start(ins, outs, sems):
        for cp in copies(ins, outs, sems):
            cp.start()

    def wait(ins, outs, sems):
        for cp in copies(ins, outs, sems):
            cp.wait()

    return _Ride(grads, [_sds(g.shape[1:], g.dtype) for g in grads], {},
                 [pltpu.SemaphoreType.DMA((n,)), pltpu.SemaphoreType.DMA((n,))], start, wait)


def _chip_swap_ride(sums):
    n = len(sums)

    def copies(ins, outs, sems, landed):
        send_sems, recv_sems, local_sems = sems
        x, y, c = _place()
        mine = 2 * x + y
        local = [pltpu.make_async_copy(ins[a].at[mine], outs[a].at[mine], local_sems.at[a]) for a in range(n)]
        remote = [pltpu.make_async_remote_copy(
            src_ref=ins[a].at[2 * px + py], dst_ref=outs[a].at[2 * px + py if landed else mine], send_sem=send_sems.at[a, j],
            recv_sem=recv_sems.at[a, j], device_id=(px, py, c), device_id_type=MESH)
            for a in range(n) for j, (px, py) in enumerate(_other_chips(x, y))]
        return local, remote

    def start(ins, outs, sems):
        local, remote = copies(ins, outs, sems, False)
        for cp in local + remote:
            cp.start()

    def wait(ins, outs, sems):
        local, sent = copies(ins, outs, sems, False)
        for cp in copies(ins, outs, sems, True)[1]:
            cp.wait_recv()
        for cp in sent:
            cp.wait_send()
        for cp in local:
            cp.wait()

    return _Ride(sums, [_sds(s.shape, s.dtype) for s in sums], {},
                 [pltpu.SemaphoreType.DMA((n, 3)), pltpu.SemaphoreType.DMA((n, 3)), pltpu.SemaphoreType.DMA((n,))], start, wait)


def _send_buffers(shards, name):
    n = len(shards)

    def body(*refs):
        for (w, transposed, rows, cols), w_ref, o_ref in zip(shards, refs[:n], refs[n:]):
            if transposed:
                c, r = w.shape
                padded = jnp.concatenate([w_ref[...], jnp.zeros((cols - c, r), F32)], axis=0) if cols > c else w_ref[...]
                o_ref[...] = padded.T.astype(BF16)
            else:
                r, c = w.shape
                if (r, c) != (rows, cols):
                    o_ref[...] = jnp.zeros((rows, cols), BF16)
                o_ref[:r, :c] = w_ref[...].astype(BF16)

    return _pallas_call(body, name=name, out_shape=[_sds((rows, cols), BF16) for _, _, rows, cols in shards])(
        *[w for w, _, _, _ in shards])


def _all_gather(shards, name):
    n = len(shards)
    first, second, third = _relayed_gather(shards)
    levels = [first, second(shards), third(shards)]
    counts = [len(level.sems) for level in levels]

    def body(*refs):
        ins, outs, sems = refs[:n], refs[n:2 * n], refs[2 * n:]
        for i, level in enumerate(levels):
            mine = sems[sum(counts[:i]):sum(counts[:i + 1])]
            local, sent = level.copies(ins, outs, mine, False)
            landed = level.copies(ins, outs, mine, True)[1]
            per_array = len(sent) // n
            if per_array == 1:
                rounds = [[0]]
            else:
                rounds = [list(range(per_array - 1)), [per_array - 1]]
            for cp in local:
                cp.start()
            for r, ks in enumerate(rounds):
                for a in range(n):
                    for k in ks:
                        sent[per_array * a + k].start()
                last = ks[-1:] if r == 0 and len(rounds) > 1 else ks
                for a in range(n):
                    for k in last:
                        landed[per_array * a + k].wait_recv()
                        sent[per_array * a + k].wait_send()
            if len(rounds) > 1:
                for a in range(n):
                    for k in rounds[0][:-1]:
                        landed[per_array * a + k].wait_recv()
                        sent[per_array * a + k].wait_send()
            for cp in local:
                cp.wait()

    return _pallas_call(
        body, name=name, in_specs=[ANY] * n, out_specs=[ANY] * n, out_shape=first.results,
        scratch_shapes=[s for level in levels for s in level.sems],
    )(*shards)


def _exchange(ride, name):
    n_in, n_out = len(ride.operands), len(ride.results)

    def body(*refs):
        ride.start(refs[:n_in], refs[n_in:n_in + n_out], refs[n_in + n_out:])
        ride.wait(refs[:n_in], refs[n_in:n_in + n_out], refs[n_in + n_out:])

    return _pallas_call(body, name=name, in_specs=[ANY] * n_in, out_specs=[ANY] * n_out, out_shape=ride.results,
                        scratch_shapes=ride.sems, input_output_aliases=ride.aliases)(*ride.operands)


HBM = pl.BlockSpec(memory_space=pltpu.HBM)
SEMAPHORES = pl.BlockSpec(memory_space=pltpu.SEMAPHORE)
IN_FLIGHT = pltpu.CompilerParams(has_side_effects=pltpu.SideEffectType.DATAFLOW_SIDE_EFFECTING)


def _chip_swap_copies(src_refs, land_refs, send_sems, recv_sems, landed):
    x, y, c = _place()
    return [pltpu.make_async_remote_copy(
        src_ref=src.at[2 * px + py], dst_ref=land.at[2 * px + py if landed else 2 * x + y], send_sem=send_sems.at[3 * a + j],
        recv_sem=recv_sems.at[3 * a + j], device_id=(px, py, c), device_id_type=MESH)
        for a, (src, land) in enumerate(zip(src_refs, land_refs)) for j, (px, py) in enumerate(_other_chips(x, y))]


def _chip_swap_start(sums, name):
    n = len(sums)

    def body(*refs):
        src_refs, land_refs, (send_sems, recv_sems), token = refs[:n], refs[n:2 * n], refs[2 * n:2 * n + 2], refs[-1]
        for cp in _chip_swap_copies(src_refs, land_refs, send_sems, recv_sems, False):
            cp.start()
        token[...] = jnp.zeros_like(token)

    kept = [pltpu.HBM(s.shape, s.dtype) for s in sums]
    out = _pallas_call(
        body, name=name,
        out_shape=[pltpu.SemaphoreType.DMA((3 * n,)), pltpu.SemaphoreType.DMA((3 * n,))] + kept + kept + [_sds((8, 128), F32)],
        in_specs=[HBM] * (2 * n), out_specs=[SEMAPHORES, SEMAPHORES] + [HBM] * (2 * n) + [pl.BlockSpec(memory_space=pltpu.VMEM)],
        input_output_aliases={i: 2 + i for i in range(2 * n)}, compiler_params=IN_FLIGHT,
    )(*[pltpu.with_memory_space_constraint(s, pltpu.HBM) for s in sums],
      *[pltpu.with_memory_space_constraint(lax.empty(s.shape, s.dtype), pltpu.HBM) for s in sums])
    return out[0], out[1], out[2:2 + n], out[2 + n:2 + 2 * n], out[-1]


def _chip_swap_wait(send_sems, recv_sems, sums, landings, after, name):
    n = len(sums)

    def body(*refs):
        src_refs, land_refs, (send_sems, recv_sems) = refs[:n], refs[n:2 * n], refs[2 * n:2 * n + 2]
        for cp in _chip_swap_copies(src_refs, land_refs, send_sems, recv_sems, False):
            cp.wait_send()
        for cp in _chip_swap_copies(src_refs, land_refs, send_sems, recv_sems, True):
            cp.wait_recv()

    out = _pallas_call(
        body, name=name, out_shape=[pltpu.HBM(s.shape, s.dtype) for s in list(sums) + list(landings)],
        in_specs=[HBM] * (2 * n) + [SEMAPHORES, SEMAPHORES] + [ANY] * len(after), out_specs=[HBM] * (2 * n),
        input_output_aliases={i: i for i in range(2 * n)}, compiler_params=IN_FLIGHT,
    )(*sums, *landings, send_sems, recv_sems, *after)
    return out[:n], out[n:]


def _pair_sums(gs, rs, core, name):
    n_arrays = len(gs)

    def body(core_ref, *refs):
        for g_ref, r_ref, o_ref in zip(refs[:n_arrays], refs[n_arrays:2 * n_arrays], refs[2 * n_arrays:]):
            o_ref[...] = (g_ref[...].astype(F32) + r_ref[...].astype(F32)).astype(o_ref.dtype)

    def own(g):
        return pl.BlockSpec((None, None) + g.shape[2:], lambda p, core_ref: (core_ref[0], p, 0, 0))

    def chip(g):
        return pl.BlockSpec((None,) + g.shape[2:], lambda p, core_ref: (p, 0, 0))

    return _pallas_call(
        body, name=name,
        grid_spec=pltpu.PrefetchScalarGridSpec(
            num_scalar_prefetch=1, grid=(4,), in_specs=[own(g) for g in gs] + [chip(g) for g in gs],
            out_specs=[chip(g) for g in gs]),
        out_shape=[_sds(g.shape[1:], g.dtype) for g in gs], compiler_params=_cparams(dimension_semantics=("arbitrary",)),
    )(core, *gs, *rs)


def _adamw_math(w, g, m, v):
    m = ADAM_B1 * m + (1.0 - ADAM_B1) * g
    v = ADAM_B2 * v + (1.0 - ADAM_B2) * (g * g)
    m_hat = m / (1.0 - ADAM_B1 ** ADAM_STEP)
    v_hat = v / (1.0 - ADAM_B2 ** ADAM_STEP)
    return -ADAM_LR * (m_hat / (jnp.sqrt(v_hat) + ADAM_EPS) + ADAM_WD * w), m, v


def _adamw_many(weights, name, ride=None):
    steps = 4
    in_specs, out_specs, out_shape, operands, tiles = [], [], [], [], []
    for w, m, v, parts, own, transposed in weights:
        _, pr, pc = parts.shape
        if transposed:
            c, r = w.shape
            tile = pl.BlockSpec((c, r // steps), lambda i: (0, i))
            part_tile = pl.BlockSpec((4, r // steps, pc), lambda i: (0, i, 0))
            tiles.append((c, r // steps))
        elif w.shape[0] % (8 * steps) == 0:
            r, c = w.shape
            tile = pl.BlockSpec((r // steps, c), lambda i: (i, 0))
            part_tile = pl.BlockSpec((4, r // steps, pc), lambda i: (0, i, 0))
            tiles.append((r // steps, c))
        else:
            tile = pl.BlockSpec(w.shape, lambda i: (0, 0))
            part_tile = pl.BlockSpec(parts.shape, lambda i: (0, 0, 0))
            tiles.append(w.shape)
        in_specs += [tile, tile, tile] + [part_tile] * (1 if own is None else 2)
        out_specs += [tile] * 4
        out_shape += [_sds(w.shape, F32)] * 4
        operands += [w, m, v, parts] + ([] if own is None else [own])
    n_in = len(operands)

    def body(*refs):
        ins, outs = list(refs[:n_in]), refs[n_in:]
        this_chip = 2 * lax.axis_index("x") + lax.axis_index("y")
        for k, (_, _, _, _, own, transposed) in enumerate(weights):
            w_ref, m_ref, v_ref, p_ref = ins[:4]
            own_ref = None if own is None else ins[4]
            del ins[:4 if own is None else 5]
            rows, cols = tiles[k]
            g = None
            for q in range(4):
                index = (q,) if transposed else (q, slice(0, rows), slice(0, cols))
                part = p_ref[index] if own is None else jnp.where(this_chip == q, own_ref[index], p_ref[index])
                g = part.astype(F32) if g is None else g + part.astype(F32)
            if transposed:
                g = g.T[:rows]
            g_out, d_out, m_out, v_out = outs[4 * k:4 * k + 4]
            g_out[...] = g
            d_out[...], m_out[...], v_out[...] = _adamw_math(w_ref[...], g, m_ref[...], v_ref[...])

    return _call(body, name, (steps,), in_specs, out_specs, out_shape, [], operands, ride)


SMALL = ("ssm_a_re", "ssm_a_im", "ssm_log_dt", "ssm_b_re", "ssm_b_im", "ssm_c_re", "ssm_c_im", "ssm_d",
         "ln1_g", "ln1_b", "ln2_g", "ln2_b")


def _pack_rows(arrays):
    rows = []
    for a in arrays:
        flat = a.reshape(-1)
        rows.append(jnp.pad(flat, (0, -flat.shape[0] % 128)).reshape(-1, 128))
    packed = jnp.concatenate(rows, axis=0)
    return jnp.pad(packed, ((0, -packed.shape[0] % 8), (0, 0)))


def _unpack_rows(packed, shapes):
    out, row = [], 0
    for shape in shapes:
        size = math.prod(shape)
        n_rows = -(-size // 128)
        out.append(packed[row:row + n_rows].reshape(-1)[:size].reshape(shape))
        row += n_rows
    return out


def _sum_devices(parts):
    def body(p_ref, o_ref):
        total = p_ref[0]
        for dev in range(1, N_DEV):
            total = total + p_ref[dev]
        o_ref[...] = total

    return _pallas_call(body, name="sum_devices", out_shape=_sds(parts.shape[1:], F32))(parts)


def _adamw_replicated(ws, ms, vs, gs):
    n = len(ws)

    def body(*refs):
        w_refs, m_refs, v_refs, g_refs, d_out, m_out, v_out = (refs[i * n:(i + 1) * n] for i in range(7))
        for i in range(n):
            d_out[i][...], m_out[i][...], v_out[i][...] = _adamw_math(w_refs[i][...], g_refs[i][...], m_refs[i][...], v_refs[i][...])

    out = _pallas_call(body, name="adamw_replicated", out_shape=[_sds(w.shape, F32) for w in ws] * 3,
                       compiler_params=_cparams())(*ws, *ms, *vs, *gs)
    return out[:n], out[n:2 * n], out[2 * n:]


def kernel(x, w_in, b_gate, w_attn_br, w_ssm_br, w_out, ssm_a_re, ssm_a_im, ssm_log_dt, ssm_b_re, ssm_b_im, ssm_c_re, ssm_c_im, ssm_d, w_glu, ln1_g, ln1_b, w_ff_gate, w_ff_up, w_ff_down, ln2_g, ln2_b, loss_target, m_w_in, m_b_gate, m_w_attn_br, m_w_ssm_br, m_w_out, m_ssm_a_re, m_ssm_a_im, m_ssm_log_dt, m_ssm_b_re, m_ssm_b_im, m_ssm_c_re, m_ssm_c_im, m_ssm_d, m_w_glu, m_ln1_g, m_ln1_b, m_w_ff_gate, m_w_ff_up, m_w_ff_down, m_ln2_g, m_ln2_b, v_w_in, v_b_gate, v_w_attn_br, v_w_ssm_br, v_w_out, v_ssm_a_re, v_ssm_a_im, v_ssm_log_dt, v_ssm_b_re, v_ssm_b_im, v_ssm_c_re, v_ssm_c_im, v_ssm_d, v_w_glu, v_ln1_g, v_ln1_b, v_w_ff_gate, v_w_ff_up, v_w_ff_down, v_ln2_g, v_ln2_b):
    given = dict(locals())
    x2, target = x[0], loss_target[0]
    core = lax.axis_index("c").astype(jnp.int32).reshape(1)

    sharded = ("w_in", "w_attn_br", "w_ssm_br", "w_glu", "w_ff_gate", "w_ff_up", "b_gate", "w_out", "w_ff_down")
    send_shape = dict(w_in=(D_MODEL, 896), w_attn_br=(ATTN_WIDTH, 128), w_ssm_br=(SSM_WIDTH, 128), w_glu=(SSM_WIDTH, 128),
                      w_out=(128, D_MODEL), w_ff_gate=(D_MODEL, FF_PAD), w_ff_up=(D_MODEL, FF_PAD), w_ff_down=(FF_PAD, D_MODEL))
    local = {k: given[k][0] for k in sharded}
    narrow = ("w_ff_gate", "w_ff_up")
    def to_send(k):
        return (local[k].T, True, *send_shape[k]) if k in narrow else (local[k], False, *send_shape[k])

    later = [k for k in sharded if k not in ("w_in", "b_gate")]
    sends = dict(zip(["w_in"] + later, _send_buffers([to_send("w_in")], "send_w_in")
                     + _send_buffers([to_send(k) for k in later], "send_weights")))
    sends["b_gate"] = local["b_gate"]
    mixer_weights = ("w_attn_br", "w_ssm_br", "w_glu", "b_gate", "w_out")
    ff_weights = ("w_ff_gate", "w_ff_up", "w_ff_down")
    wt = {}
    wt["w_in"], = _all_gather([sends["w_in"]], "gather_w_in")

    a_re, a_im, log_dt = ssm_a_re[0], ssm_a_im[0], ssm_log_dt[0].reshape(SSM_GROUPS, 1)
    b_re_t, b_im_t = ssm_b_re[0].transpose(0, 2, 1), ssm_b_im[0].transpose(0, 2, 1)
    abar_re, abar_im, e_re, e_im, bbar_re_t, bbar_im_t = _ssm_prep(a_re, a_im, log_dt, b_re_t, b_im_t)
    bmat, cmat, a_chunks = _ssm_tables(abar_re, abar_im, bbar_re_t, bbar_im_t, ssm_c_re[0], ssm_c_im[0])
    cos_t, sin_t = _rope_tables()

    big_mixer, ff_in = [k for k in mixer_weights if k != "b_gate"], ("w_ff_gate", "w_ff_up")
    n_mixer = len(big_mixer)
    mixer_1, mixer_2, mixer_3 = _relayed_gather([sends[k] for k in big_mixer])
    ff_in_1, ff_in_2, ff_in_3 = _relayed_gather([sends[k] for k in ff_in])
    ff_down_1, ff_down_2, ff_down_3 = _relayed_gather([sends["w_ff_down"]])
    proj, *landed = _proj(x2, wt["w_in"], mixer_1 + _gather_first_level([sends["b_gate"]]))
    mixer, bias = landed[:n_mixer], landed[n_mixer:]
    attn, lse, q_pm, k_pm, v_pm, *landed = _attn_fwd(proj, cos_t, sin_t,
                                                     mixer_2(mixer) + _gather_second_level(bias) + ff_in_1)
    mixer, b_gate_full, ff = landed[:n_mixer], landed[n_mixer], landed[n_mixer + 1:]
    ys, states, *landed = _ssm_fwd(proj, bmat, cmat, a_chunks, ssm_d, mixer_3(mixer) + ff_in_2(ff) + ff_down_1)
    wt.update(zip(big_mixer, landed[:n_mixer]))
    ff, ff_down = landed[n_mixer:n_mixer + 2], landed[n_mixer + 2:]
    wt["w_out"] = wt["w_out"].reshape(D_MODEL, D_MODEL)
    h, xhat1, rstd1, glu, y_attn, y_ssm, *landed = _mixer_out(
        attn, ys, proj, x2, wt["w_attn_br"], wt["w_ssm_br"], wt["w_glu"], wt["w_out"], b_gate_full, ln1_g, ln1_b,
        ff_in_3(ff) + ff_down_2(ff_down))
    wt.update(zip(ff_in, landed[:2]))
    ff_a, ff_b, ff_f, w_ff_down = _ff_up(h, wt["w_ff_gate"], wt["w_ff_up"], ff_down_3(landed[2:]))
    wt["w_ff_down"] = w_ff_down.reshape(D_FF_PAD, D_MODEL)
    dr2, d_ln2_g, d_ln2_b, loss_lanes = _ff_down_loss(ff_f, wt["w_ff_down"], h, target, ln2_g, ln2_b)

    def pair_sums(names, contrib, from_sibling):
        return _pair_sums([contrib[k] for k in names], from_sibling, core, "pair_sums_" + names[0])

    d_a, d_b = _ff_down_bwd(dr2, wt["w_ff_down"], ff_a, ff_b)
    contrib = dict(w_ff_gate=_weight_grad(h, d_a, "wgrad_w_ff_gate", FF_PAD),
                   w_ff_up=_weight_grad(h, d_b, "wgrad_w_ff_up", FF_PAD),
                   w_ff_down=_weight_grad(ff_f, dr2, "wgrad_w_ff_down"))
    dr1, d_ln1_g, d_ln1_b, *from_sibling = _ff_up_bwd(
        d_a, d_b, wt["w_ff_gate"], wt["w_ff_up"], dr2, xhat1, rstd1, ln1_g, _sibling_swap_ride([contrib[k] for k in ff_weights]))
    ff_sums = pair_sums(ff_weights, contrib, from_sibling)

    d_ya, d_yssm, d_proj, d_attn, d_glu, d_ys, mixed, y_s, gy, d_bg = _mixer_bwd(
        dr1, proj, y_attn, y_ssm, glu, ys, wt["w_attn_br"], wt["w_ssm_br"], wt["w_glu"], wt["w_out"], b_gate_full)
    contrib.update(w_attn_br=_weight_grad(attn, d_ya, "wgrad_w_attn_br", 128),
                   w_ssm_br=_weight_grad(y_s, d_yssm, "wgrad_w_ssm_br", 128),
                   w_glu=_weight_grad(gy, d_glu, "wgrad_w_glu", 128),
                   w_out=_weight_grad(mixed, dr1, "wgrad_w_out"),
                   b_gate=d_bg.reshape(2, 4, 2, 128).transpose(2, 1, 0, 3))
    d_proj, *landed = _attn_bwd(q_pm, k_pm, v_pm, cos_t, sin_t, attn, lse, d_attn, d_proj,
                                _chip_swap_ride(ff_sums) + _sibling_swap_ride([contrib[k] for k in mixer_weights]))
    parts, own_sums = dict(zip(ff_weights, landed[:len(ff_weights)])), {}
    mixer_sums = pair_sums(mixer_weights, contrib, landed[len(ff_weights):])
    d_proj, d_bmat, d_cmat, d_abar, d_skip, *landed = _ssm_bwd(d_ys, proj, states, bmat, cmat, a_chunks, ssm_d, d_proj,
                                                               _chip_swap_ride(mixer_sums))
    parts.update(zip(mixer_weights, landed))

    gbb_re_t, gbb_im_t = _block_diag_parts(d_bmat, True)
    gc_re, gc_im = _block_diag_parts(d_cmat, False)
    ga_re = d_abar[:, 0, :CHUNK_STATES].reshape(SSM_GROUPS, SSM_STATE)
    ga_im = d_abar[:, 0, CHUNK_STATES:].reshape(SSM_GROUPS, SSM_STATE)
    g_a_re, g_a_im, g_log_dt, g_b_re_t, g_b_im_t = _ssm_param_bwd(
        a_re, a_im, log_dt, b_re_t, b_im_t, abar_re, abar_im, e_re, e_im, ga_re, ga_im, gbb_re_t, gbb_im_t)
    mine = [g_a_re, g_a_im, g_log_dt, g_b_re_t, g_b_im_t, gc_re, -gc_im,
            d_skip, d_ln1_g, d_ln1_b, d_ln2_g, d_ln2_b]
    small_packed = _pack_rows(mine + [loss_lanes])

    contrib["w_in"], small_partly = _weight_grad(x2, d_proj, "wgrad_w_in", 896, _gather_first_level([small_packed]))
    from_sibling, every = _exchange(_sibling_swap_ride([contrib["w_in"]]) + _gather_second_level([small_partly]),
                                    "swap_w_in_with_sibling")
    w_in_sum, = pair_sums(["w_in"], contrib, [from_sibling])
    send_sems, recv_sems, w_in_sum, landing, token = _chip_swap_start([w_in_sum], "w_in_chip_swap_start")

    def adamw_of(k):
        taken = (lambda a: a.T) if k in narrow else (lambda a: a)
        return taken(local[k]), taken(given["m_" + k][0]), taken(given["v_" + k][0]), parts[k], own_sums.get(k), k in narrow

    others = [k for k in sharded if k != "w_in"]
    updated = _adamw_many([adamw_of(k) for k in others], "adamw_others", _after(token))
    grad_x, = _grad_x(d_proj, wt["w_in"], dr1, _after(token))

    def held(k, a):
        return a.transpose(0, 1, 3, 2) if k in ("ssm_b_re", "ssm_b_im") else a

    *small_grads, loss_sum = _unpack_rows(_sum_devices(every), [held(k, given[k]).shape for k in SMALL] + [(1, 128)])
    small = _adamw_replicated([held(k, given[k]) for k in SMALL], [held(k, given["m_" + k]) for k in SMALL],
                              [held(k, given["v_" + k]) for k in SMALL], small_grads)
    loss = loss_sum[0, 0]

    (own_sums["w_in"],), (parts["w_in"],) = _chip_swap_wait(
        send_sems, recv_sems, w_in_sum, landing, [grad_x, updated[0], small[0][0]], "w_in_chip_swap_wait")
    updated += _adamw_many([adamw_of("w_in")], "adamw_w_in")

    grads, deltas, new_m, new_v = {}, {}, {}, {}
    for i, k in enumerate(others + ["w_in"]):
        out = [o.T if k in narrow else o for o in updated[4 * i:4 * i + 4]]
        grads[k], deltas[k], new_m[k], new_v[k] = (o.reshape((1,) + local[k].shape) for o in out)
    for res, values in zip((grads, deltas, new_m, new_v), (small_grads,) + small):
        res.update((k, held(k, a)) for k, a in zip(SMALL, values))

    order = ("w_in", "b_gate", "w_attn_br", "w_ssm_br", "w_out", "ssm_a_re", "ssm_a_im", "ssm_log_dt", "ssm_b_re", "ssm_b_im",
             "ssm_c_re", "ssm_c_im", "ssm_d", "w_glu", "ln1_g", "ln1_b", "w_ff_gate", "w_ff_up", "w_ff_down", "ln2_g", "ln2_b")
    return (loss, grad_x[None], *[grads[k] for k in order], *[deltas[k] for k in order], *[new_m[k] for k in order],
            *[new_v[k] for k in order])
```

```python
import functools
import math

import jax
import jax.numpy as jnp
import numpy as np
from jax import lax
from jax.experimental import pallas as pl
from jax.experimental.pallas import tpu as pltpu

F32 = jnp.float32
BF16 = jnp.bfloat16

N_DEV = 8
SEQ = 2048
D_MODEL = 1024
HEAD_DIM = 64
ATTN_WIDTH = 512
QKV_WIDTH = 1536
SSM_WIDTH = 512
SSM_GROUPS = 32
SSM_GROUP = 16
SSM_STATE = 64
IN_WIDTH = 7168
D_FF = 2816
FF_SHARD = D_FF // N_DEV
FF_PAD = 384
D_FF_PAD = FF_PAD * N_DEV
DN_ALPHA = 2.0 ** 0.25
LN_EPS = 1e-5
NEG_INF = -1e30
ROPE_THETA = 10000.0
BLOCK = 128
GROUPS = ((1, 16), (4, 4), (16, 1))

ADAM_LR = 0.001
ADAM_B1 = 0.9
ADAM_B2 = 0.999
ADAM_EPS = 1e-08
ADAM_WD = 0.01
ADAM_STEP = 10

VMEM_LIMIT = 56 * 1024 * 1024


_pallas_call = pl.pallas_call


def _cparams(**kw):
    return pltpu.CompilerParams(vmem_limit_bytes=VMEM_LIMIT, **kw)


def _dot(a, b):
    return jnp.dot(a, b, preferred_element_type=F32)


def _dot_nt(a, b):
    return lax.dot_general(a, b, (((1,), (1,)), ((), ())), preferred_element_type=F32)


def _side_by_side(w_ref, row=None):
    rows = slice(None) if row is None else pl.ds(row, 1)
    return jnp.concatenate([w_ref[i, rows, :] for i in range(w_ref.shape[0])], axis=1)


def _dot_tn(a, b):
    return lax.dot_general(a, b, (((0,), (0,)), ((), ())), preferred_element_type=F32)


def _rope_tables():
    half = HEAD_DIM // 2
    inv_freq = np.float32(ROPE_THETA) ** (-np.arange(half, dtype=np.float32) / np.float32(half))
    ang = np.arange(SEQ, dtype=np.float32)[:, None] * inv_freq[None, :]
    cos, sin = np.cos(ang).astype(np.float32), np.sin(ang).astype(np.float32)
    tables = np.tile(cos, (1, 4)), np.tile(np.concatenate([-sin, sin], axis=1), (1, 2))

    def by_phase(t):
        return np.stack([t.reshape(SEQ // d, d, 128).transpose(1, 0, 2).reshape(SEQ, 128) for d, _ in GROUPS])

    return jnp.asarray(by_phase(tables[0])), jnp.asarray(by_phase(tables[1]))


def _swap_halves(x):
    lane = lax.broadcasted_iota(jnp.int32, x.shape, 1)
    return jnp.where((lane & 63) < 32, pltpu.roll(x, 96, axis=1), pltpu.roll(x, 32, axis=1))


def _group_rows(d, nb, r, i):
    src = pl.ds(i * BLOCK, BLOCK) if d == 1 else pl.ds(r + i * BLOCK * d, BLOCK, stride=d)
    return src, pl.ds((r * nb + i) * BLOCK, BLOCK)


def _attn_masks():
    a_idx = lax.broadcasted_iota(jnp.int32, (2 * BLOCK, 2 * BLOCK), 0) & (BLOCK - 1)
    c_idx = lax.broadcasted_iota(jnp.int32, (2 * BLOCK, 2 * BLOCK), 1)
    cur_ok = jnp.logical_and(c_idx >= BLOCK, c_idx - BLOCK <= a_idx)
    prev_ok = jnp.logical_and(c_idx < BLOCK, c_idx >= a_idx)
    lane = lax.broadcasted_iota(jnp.int32, (BLOCK, 128), 1)
    return cur_ok, prev_ok, lane < HEAD_DIM


def _stack_heads(t, head0):
    zero = jnp.zeros_like(t)
    return jnp.concatenate([jnp.where(head0, t, zero), jnp.where(head0, zero, t)], axis=0)


def _unstack_heads(t2, head0):
    return jnp.where(head0, t2[:BLOCK], t2[BLOCK:])


def _attn_fwd(proj, cos_t, sin_t, ride=None):
    def body(q0, q1, q2, k0, k1, k2, v0, v1, v2, cos_ref, sin_ref, attn_ref, lse_ref, qpm_ref, kpm_ref, vpm_ref,
             qs, ks, vs, os_, ms, ls, acc, mnat, lnat):
        cur_ok, prev_ok, head0 = _attn_masks()
        ks[:BLOCK, :] = jnp.zeros((BLOCK, 128), BF16)
        vs[:BLOCK, :] = jnp.zeros((BLOCK, 128), BF16)
        for g, (d, nb) in enumerate(GROUPS):
            q_ref, k_ref, v_ref = (q0, q1, q2)[g], (k0, k1, k2)[g], (v0, v1, v2)[g]
            for r in range(d):
                for i in range(nb):
                    src, dst = _group_rows(d, nb, r, i)
                    below = pl.ds(dst.start + BLOCK, BLOCK)
                    c, s = cos_ref[g, dst, :], sin_ref[g, dst, :]
                    q = q_ref[src, :]
                    k = k_ref[src, :]
                    qs[dst, :] = ((q * c + _swap_halves(q) * s) * 0.125).astype(BF16)
                    ks[below, :] = (k * c + _swap_halves(k) * s).astype(BF16)
                    vs[below, :] = v_ref[src, :].astype(BF16)
                    qpm_ref[g, dst, :], kpm_ref[g, dst, :], vpm_ref[g, dst, :] = qs[dst, :], ks[below, :], vs[below, :]

            def block(b, carry, nb=nb):
                has_prev = (b & (nb - 1)) > 0
                cur = pl.ds(pl.multiple_of(b * BLOCK, BLOCK), BLOCK)
                window = pl.ds(pl.multiple_of(b * BLOCK, BLOCK), 2 * BLOCK)
                valid = jnp.logical_or(cur_ok, jnp.logical_and(prev_ok, has_prev))
                s = jnp.where(valid, _dot_nt(_stack_heads(qs[cur, :], head0), ks[window, :]), NEG_INF)
                m = jnp.max(s, axis=1, keepdims=True)
                p = jnp.exp(s - m)
                os_[cur, :] = _unstack_heads(_dot(p.astype(BF16), vs[window, :]), head0)
                ms[cur, :] = _unstack_heads(m, head0)
                ls[cur, :] = _unstack_heads(jnp.sum(p, axis=1, keepdims=True), head0)
                return carry

            lax.fori_loop(0, SEQ // BLOCK, block, 0, unroll=16)

            for r in range(d):
                for i in range(nb):
                    src, dst = _group_rows(d, nb, r, i)
                    if g == 0:
                        acc[src, :], mnat[src, :], lnat[src, :] = os_[dst, :], ms[dst, :], ls[dst, :]
                    else:
                        m_old, m_g = mnat[src, :], ms[dst, :]
                        m_new = jnp.maximum(m_old, m_g)
                        a_old, a_g = jnp.exp(m_old - m_new), jnp.exp(m_g - m_new)
                        acc[src, :] = a_old * acc[src, :] + a_g * os_[dst, :]
                        lnat[src, :] = a_old * lnat[src, :] + a_g * ls[dst, :]
                        mnat[src, :] = m_new
        for i in range(SEQ // BLOCK):
            rows = pl.ds(i * BLOCK, BLOCK)
            l = lnat[rows, :]
            attn_ref[rows, :] = acc[rows, :] / l
            lse_ref[rows, :] = mnat[rows, :] + jnp.log(l)

    def col(base):
        return pl.BlockSpec((SEQ, 128), lambda hp, base=base: (0, base + hp))

    in_specs = [col(g * 4) for g in range(3)] + [col(12 + g * 4) for g in range(3)] + [col(24 + g * 4) for g in range(3)]
    table = pl.BlockSpec((3, SEQ, 128), lambda hp: (0, 0, 0), pipeline_mode=pl.Buffered(1))
    out = pl.BlockSpec((SEQ, 128), lambda hp: (0, hp))
    by_phase = pl.BlockSpec((3, SEQ, 128), lambda hp: (0, 0, hp))
    return _call(
        body, "attn_fwd", (4,), in_specs + [table, table], [out, out] + [by_phase] * 3,
        [_sds((SEQ, ATTN_WIDTH), F32), _sds((SEQ, ATTN_WIDTH), F32)] + [_sds((3, SEQ, ATTN_WIDTH), BF16)] * 3,
        [pltpu.VMEM((SEQ, 128), BF16)] + [pltpu.VMEM((SEQ + BLOCK, 128), BF16)] * 2 + [pltpu.VMEM((SEQ, 128), F32)] * 6,
        [proj] * 9 + [cos_t, sin_t], ride)


def _attn_bwd_group_body(g):
    d, nb = GROUPS[g]

    def body(qs_ref, ks_ref, vs_ref, cos_ref, sin_ref, lse_ref, dattn_ref, dsum_ref, dproj_ref,
             ks, vs, dos, lss, dss, dqs, dks, dvs, stage, outs, sems):
        cur_ok, prev_ok, head0 = _attn_masks()
        qs = qs_ref.at[g]
        ks[:BLOCK, :] = jnp.zeros((BLOCK, 128), BF16)
        vs[:BLOCK, :] = jnp.zeros((BLOCK, 128), BF16)
        dks[:BLOCK, :] = jnp.zeros((BLOCK, 128), F32)
        dvs[:BLOCK, :] = jnp.zeros((BLOCK, 128), F32)
        for r in range(d):
            for i in range(nb):
                src, dst = _group_rows(d, nb, r, i)
                below = pl.ds(dst.start + BLOCK, BLOCK)
                ks[below, :] = ks_ref[g, dst, :]
                vs[below, :] = vs_ref[g, dst, :]
                dos[dst, :] = dattn_ref[src, :].astype(BF16)
                dss[dst, :] = dsum_ref[src, :]
                lss[dst, :] = lse_ref[src, :]
                dks[below, :] = jnp.zeros((BLOCK, 128), F32)
                dvs[below, :] = jnp.zeros((BLOCK, 128), F32)

        def per_head_column(t):
            return jnp.concatenate([jnp.max(jnp.where(head0, t, NEG_INF), axis=1, keepdims=True),
                                    jnp.max(jnp.where(head0, NEG_INF, t), axis=1, keepdims=True)], axis=0)

        def block(b, carry):
            has_prev = (b & (nb - 1)) > 0
            cur = pl.ds(pl.multiple_of(b * BLOCK, BLOCK), BLOCK)
            window = pl.ds(pl.multiple_of(b * BLOCK, BLOCK), 2 * BLOCK)
            valid = jnp.logical_or(cur_ok, jnp.logical_and(prev_ok, has_prev))
            q2, do2 = _stack_heads(qs[cur, :], head0), _stack_heads(dos[cur, :], head0)
            kw, vw = ks[window, :], vs[window, :]
            s = jnp.where(valid, _dot_nt(q2, kw), NEG_INF)
            p = jnp.exp(s - per_head_column(lss[cur, :]))
            ds = (p * (_dot_nt(do2, vw) - per_head_column(dss[cur, :]))).astype(BF16)
            dvs[window, :] += _dot_tn(p.astype(BF16), do2)
            dks[window, :] += _dot_tn(ds, q2)
            dqs[cur, :] = _unstack_heads(_dot(ds, kw), head0)
            return carry

        lax.fori_loop(0, SEQ // BLOCK, block, 0, unroll=8)

        hp = pl.program_id(0)
        copies = []
        for kind in range(3):
            for r in range(d):
                for i in range(nb):
                    src, dst = _group_rows(d, nb, r, i)
                    below = pl.ds(dst.start + BLOCK, BLOCK)
                    if kind == 2:
                        stage[src, :] = dvs[below, :]
                    else:
                        c, s = cos_ref[g, dst, :], sin_ref[g, dst, :]
                        t = dqs[dst, :] * 0.125 if kind == 0 else dks[below, :]
                        stage[src, :] = t * c - _swap_halves(t) * s
            for i in range(SEQ // MM_ROWS):
                rows = pl.ds(i * MM_ROWS, MM_ROWS)
                outs[kind, rows, :] = stage[rows, :].astype(BF16)
            column = pl.multiple_of((kind * 12 + g * 4 + hp) * 128, 128)
            copies.append(pltpu.make_async_copy(outs.at[kind], dproj_ref.at[:, pl.ds(column, 128)], sems.at[kind]))
            copies[-1].start()
        for cp in copies:
            cp.wait()

    return body


def _attn_bwd(q_pm, k_pm, v_pm, cos_t, sin_t, attn, lse, dattn, dproj, ride=None):
    groups = [_attn_bwd_group_body(g) for g in range(3)]

    def body(qs_ref, ks_ref, vs_ref, cos_ref, sin_ref, attn_ref, lse_ref, dattn_ref, dproj_in, dproj_ref, dsum, *scratch):
        del dproj_in
        head0 = _attn_masks()[2]
        for i in range(SEQ // BLOCK):
            rows = pl.ds(i * BLOCK, BLOCK)
            prod = dattn_ref[rows, :] * attn_ref[rows, :]
            d0 = jnp.sum(jnp.where(head0, prod, 0.0), axis=1, keepdims=True)
            d1 = jnp.sum(jnp.where(head0, 0.0, prod), axis=1, keepdims=True)
            dsum[rows, :] = jnp.where(head0, d0, d1)
        for g in range(3):
            groups[g](qs_ref, ks_ref, vs_ref, cos_ref, sin_ref, lse_ref, dattn_ref, dsum, dproj_ref, *scratch)

    def col(base):
        return pl.BlockSpec((SEQ, 128), lambda hp, base=base: (0, base + hp))

    table = pl.BlockSpec((3, SEQ, 128), lambda hp: (0, 0, 0), pipeline_mode=pl.Buffered(1))
    by_phase = pl.BlockSpec((3, SEQ, 128), lambda hp: (0, 0, hp))
    return _call(
        body, "attn_bwd", (4,), [by_phase] * 3 + [table, table, col(0), col(0), col(0), ANY],
        [ANY], [_sds((SEQ, IN_WIDTH), BF16)],
        [pltpu.VMEM((SEQ, 128), F32)]
        + [pltpu.VMEM((SEQ + BLOCK, 128), BF16)] * 2 + [pltpu.VMEM((SEQ, 128), BF16)]
        + [pltpu.VMEM((SEQ, 128), F32)] * 3 + [pltpu.VMEM((SEQ + BLOCK, 128), F32)] * 2 + [pltpu.VMEM((SEQ, 128), F32)]
        + [pltpu.VMEM((3, SEQ, 128), BF16), pltpu.SemaphoreType.DMA((3,))],
        [q_pm, k_pm, v_pm, cos_t, sin_t, attn, lse, dattn, dproj], ride, aliases={8: 0})


SSM_CHUNKS = 4
CHUNK_STATES = 512
SCAN_ROWS = 8
U_COL = (3 * QKV_WIDTH) // 128


def _cmul(xr, xi, yr, yi):
    return xr * yr - xi * yi, xr * yi + xi * yr


def _ssm_prep(a_re, a_im, log_dt, b_re_t, b_im_t):
    def body(ar_ref, ai_ref, ldt_ref, br_ref, bi_ref, abr_ref, abi_ref, er_ref, ei_ref, bbr_ref, bbi_ref):
        ar, ai = ar_ref[...], ai_ref[...]
        dt = jnp.exp(ldt_ref[...])
        mag = jnp.exp(ar * dt)
        abr, abi = mag * jnp.cos(ai * dt), mag * jnp.sin(ai * dt)
        den = ar * ar + ai * ai
        nr, ni = abr - 1.0, abi
        er, ei = (nr * ar + ni * ai) / den, (ni * ar - nr * ai) / den
        abr_ref[...], abi_ref[...], er_ref[...], ei_ref[...] = abr, abi, er, ei
        er3, ei3 = er[:, None, :], ei[:, None, :]
        br, bi = br_ref[...], bi_ref[...]
        bbr_ref[...] = er3 * br - ei3 * bi
        bbi_ref[...] = er3 * bi + ei3 * br

    gp = jax.ShapeDtypeStruct(a_re.shape, F32)
    gb = jax.ShapeDtypeStruct(b_re_t.shape, F32)
    return _pallas_call(body, name="ssm_prep", out_shape=(gp, gp, gp, gp, gb, gb))(a_re, a_im, log_dt, b_re_t, b_im_t)


def _ssm_param_bwd(a_re, a_im, log_dt, b_re_t, b_im_t, abar_re, abar_im, e_re, e_im, ga_re, ga_im, gbb_re_t, gbb_im_t):
    def body(ar_ref, ai_ref, ldt_ref, br_ref, bi_ref, abr_ref, abi_ref, er_ref, ei_ref, gar_ref, gai_ref, gbr_ref, gbi_ref,
             o_ar, o_ai, o_ldt, o_br, o_bi):
        ar, ai = ar_ref[...], ai_ref[...]
        dt = jnp.exp(ldt_ref[...])
        er, ei = er_ref[...], ei_ref[...]
        br, bi, gbr, gbi = br_ref[...], bi_ref[...], gbr_ref[...], gbi_ref[...]
        er3, ei3 = er[:, None, :], ei[:, None, :]
        o_br[...] = er3 * gbr + ei3 * gbi
        o_bi[...] = er3 * gbi - ei3 * gbr
        ge_r = jnp.sum(br * gbr + bi * gbi, axis=1)
        ge_i = jnp.sum(br * gbi - bi * gbr, axis=1)
        den = ar * ar + ai * ai
        ilr, ili = ar / den, -ai / den
        t_r, t_i = _cmul(ilr, -ili, ge_r, ge_i)
        gab_r, gab_i = gar_ref[...] + t_r, gai_ref[...] + t_i
        gz_r, gz_i = _cmul(abr_ref[...], -abi_ref[...], gab_r, gab_i)
        el_r, el_i = _cmul(er, ei, ilr, ili)
        u_r, u_i = _cmul(el_r, -el_i, ge_r, ge_i)
        o_ar[...] = dt * gz_r - u_r
        o_ai[...] = dt * gz_i - u_i
        o_ldt[...] = jnp.sum(gz_r * ar + gz_i * ai, axis=1, keepdims=True) * dt

    gp = jax.ShapeDtypeStruct(a_re.shape, F32)
    gb = jax.ShapeDtypeStruct(b_re_t.shape, F32)
    return _pallas_call(body, name="ssm_param_bwd", out_shape=(gp, gp, jax.ShapeDtypeStruct(log_dt.shape, F32), gb, gb))(
        a_re, a_im, log_dt, b_re_t, b_im_t, abar_re, abar_im, e_re, e_im, ga_re, ga_im, gbb_re_t, gbb_im_t)


def _block_diag(blocks_re, blocks_im, sign_im, rows_are_channels):
    both = jnp.stack([blocks_re, sign_im * blocks_im]).reshape(2, SSM_CHUNKS, 8, SSM_GROUP, SSM_STATE)
    eye = jnp.eye(8, dtype=F32)
    if rows_are_channels:
        return jnp.einsum("rcghp,gk->cghrkp", both, eye).reshape(SSM_CHUNKS, 128, 2 * CHUNK_STATES)
    return jnp.einsum("rcghp,gk->crkpgh", both, eye).reshape(SSM_CHUNKS, 2 * CHUNK_STATES, 128)


def _block_diag_parts(mat, rows_are_channels):
    if rows_are_channels:
        six = mat.reshape(SSM_CHUNKS, 8, SSM_GROUP, 2, 8, SSM_STATE)
        parts = jnp.einsum("cghrgp->rcghp", six)
    else:
        six = mat.reshape(SSM_CHUNKS, 2, 8, SSM_STATE, 8, SSM_GROUP)
        parts = jnp.einsum("crgpgh->rcghp", six)
    parts = parts.reshape(2, SSM_GROUPS, SSM_GROUP, SSM_STATE)
    return parts[0], parts[1]


def _scan_consts(a_ref, conj, reverse):
    ar = jnp.broadcast_to(a_ref[:, :CHUNK_STATES], (SCAN_ROWS, CHUNK_STATES))
    ai = jnp.broadcast_to(a_ref[:, CHUNK_STATES:], (SCAN_ROWS, CHUNK_STATES))
    if conj:
        ai = -ai
    row = lax.broadcasted_iota(jnp.int32, (SCAN_ROWS, CHUNK_STATES), 0)
    if reverse:
        row = SCAN_ROWS - 1 - row
    zero = jnp.zeros_like(ar)
    steps = []
    pr, pi = ar, ai
    for shift in (1, 2, 4):
        keep = row >= shift
        steps.append((SCAN_ROWS - shift if reverse else shift, jnp.where(keep, pr, zero), jnp.where(keep, pi, zero)))
        pr, pi = _cmul(pr, pi, pr, pi)
    first = row == 0
    return steps, (jnp.where(first, ar, zero), jnp.where(first, ai, zero)), first


def _scan_tile(xr, xi, prev_r, prev_i, steps, carry_in, reverse):
    edge = SCAN_ROWS - 1 if reverse else 1
    cr, ci = pltpu.roll(prev_r, edge, axis=0), pltpu.roll(prev_i, edge, axis=0)
    xr, xi = xr + carry_in[0] * cr - carry_in[1] * ci, xi + carry_in[0] * ci + carry_in[1] * cr
    for shift, mr, mi in steps:
        sr, si = pltpu.roll(xr, shift, axis=0), pltpu.roll(xi, shift, axis=0)
        xr, xi = xr + mr * sr - mi * si, xi + mr * si + mi * sr
    return xr, xi


MM_ROWS = 256


def _ssm_fwd(proj, bmat, cmat, a_chunks, d_skip, ride=None):
    def body(u_ref, b_ref, c_ref, a_ref, d_ref, y_ref, states_ref, h_ref):
        for i in range(SEQ // MM_ROWS):
            rows = pl.ds(i * MM_ROWS, MM_ROWS)
            h_ref[rows, :] = _dot(u_ref[rows, :].astype(BF16), b_ref[...])
        steps, carry_in, _ = _scan_consts(a_ref, conj=False, reverse=False)

        def tile(k, carry):
            rows = pl.ds(pl.multiple_of(k * SCAN_ROWS, SCAN_ROWS), SCAN_ROWS)
            xr, xi = _scan_tile(h_ref[rows, :CHUNK_STATES], h_ref[rows, CHUNK_STATES:], carry[0], carry[1], steps, carry_in, False)
            h_ref[rows, :CHUNK_STATES] = xr
            h_ref[rows, CHUNK_STATES:] = xi
            return xr, xi

        zero = jnp.zeros((SCAN_ROWS, CHUNK_STATES), F32)
        lax.fori_loop(0, SEQ // SCAN_ROWS, tile, (zero, zero), unroll=4)
        for i in range(SEQ // MM_ROWS):
            rows = pl.ds(i * MM_ROWS, MM_ROWS)
            states = h_ref[rows, :].astype(BF16)
            states_ref[rows, :] = states
            y_ref[rows, :] = _dot(states, c_ref[...]) + d_ref[...] * u_ref[rows, :]

    return _call(
        body, "ssm_fwd", (SSM_CHUNKS,),
        [pl.BlockSpec((SEQ, 128), lambda c: (0, U_COL + c)),
         pl.BlockSpec((None, 128, 2 * CHUNK_STATES), lambda c: (c, 0, 0)),
         pl.BlockSpec((None, 2 * CHUNK_STATES, 128), lambda c: (c, 0, 0)),
         pl.BlockSpec((None, 1, 2 * CHUNK_STATES), lambda c: (c, 0, 0)),
         pl.BlockSpec((1, 128), lambda c: (0, c))],
        [pl.BlockSpec((SEQ, 128), lambda c: (0, c)), pl.BlockSpec((SEQ, 2 * CHUNK_STATES), lambda c: (0, c))],
        [_sds((SEQ, SSM_WIDTH), F32), _sds((SEQ, SSM_CHUNKS * 2 * CHUNK_STATES), BF16)],
        [pltpu.VMEM((SEQ, 2 * CHUNK_STATES), F32)],
        [proj, bmat, cmat, a_chunks, d_skip], ride)


def _ssm_bwd(dys, proj, h, bmat, cmat, a_chunks, d_skip, dproj, ride=None):
    def body(dy_ref, u_ref, states_ref, b_ref, c_ref, a_ref, d_ref, dproj_in, du_ref, db_ref, dc_ref, da_ref, dd_ref,
             g_ref, h_ref):
        del dproj_in
        dsum = jnp.zeros((1, 128), F32)
        dcm = jnp.zeros((2 * CHUNK_STATES, 128), F32)
        for i in range(SEQ // MM_ROWS):
            rows = pl.ds(i * MM_ROWS, MM_ROWS)
            h_ref[rows, :] = states_ref[rows, :].astype(F32)
            dy = dy_ref[rows, :]
            g_ref[rows, :] = _dot_nt(dy.astype(BF16), c_ref[...])
            dsum += jnp.sum(dy * u_ref[rows, :], axis=0, keepdims=True)
            dcm += _dot_tn(states_ref[rows, :], dy.astype(BF16))
        dd_ref[...] = dsum
        dc_ref[...] = dcm
        steps, carry_in, _ = _scan_consts(a_ref, conj=True, reverse=True)
        first_row = lax.broadcasted_iota(jnp.int32, (SCAN_ROWS, CHUNK_STATES), 0) == 0
        n_tiles = SEQ // SCAN_ROWS

        def tile(j, carry):
            k = n_tiles - 1 - j
            rows = pl.ds(pl.multiple_of(k * SCAN_ROWS, SCAN_ROWS), SCAN_ROWS)
            before = pl.ds(pl.multiple_of(jnp.maximum(k - 1, 0) * SCAN_ROWS, SCAN_ROWS), SCAN_ROWS)
            gr, gi = _scan_tile(g_ref[rows, :CHUNK_STATES], g_ref[rows, CHUNK_STATES:], carry[0], carry[1], steps, carry_in, True)
            g_ref[rows, :CHUNK_STATES] = gr
            g_ref[rows, CHUNK_STATES:] = gi
            has_before = jnp.where(k > 0, 1.0, 0.0)
            hr = jnp.where(first_row, pltpu.roll(h_ref[before, :CHUNK_STATES], 1, axis=0) * has_before,
                           pltpu.roll(h_ref[rows, :CHUNK_STATES], 1, axis=0))
            hi = jnp.where(first_row, pltpu.roll(h_ref[before, CHUNK_STATES:], 1, axis=0) * has_before,
                           pltpu.roll(h_ref[rows, CHUNK_STATES:], 1, axis=0))
            return gr, gi, carry[2] + hr * gr + hi * gi, carry[3] + hr * gi - hi * gr

        zero = jnp.zeros((SCAN_ROWS, CHUNK_STATES), F32)
        _, _, sar, sai = lax.fori_loop(0, n_tiles, tile, (zero, zero, zero, zero), unroll=4)
        da_ref[:, :CHUNK_STATES] = jnp.sum(sar, axis=0, keepdims=True)
        da_ref[:, CHUNK_STATES:] = jnp.sum(sai, axis=0, keepdims=True)
        dbm = jnp.zeros((128, 2 * CHUNK_STATES), F32)
        for i in range(SEQ // MM_ROWS):
            rows = pl.ds(i * MM_ROWS, MM_ROWS)
            g = g_ref[rows, :].astype(BF16)
            du_ref[rows, :] = (_dot_nt(g, b_ref[...]) + d_ref[...] * dy_ref[rows, :]).astype(BF16)
            dbm += _dot_tn(u_ref[rows, :].astype(BF16), g)
        db_ref[...] = dbm

    chunk_col = pl.BlockSpec((SEQ, 128), lambda c: (0, c))
    return _call(
        body, "ssm_bwd", (SSM_CHUNKS,),
        [chunk_col,
         pl.BlockSpec((SEQ, 128), lambda c: (0, U_COL + c)),
         pl.BlockSpec((SEQ, 2 * CHUNK_STATES), lambda c: (0, c)),
         pl.BlockSpec((None, 128, 2 * CHUNK_STATES), lambda c: (c, 0, 0)),
         pl.BlockSpec((None, 2 * CHUNK_STATES, 128), lambda c: (c, 0, 0)),
         pl.BlockSpec((None, 1, 2 * CHUNK_STATES), lambda c: (c, 0, 0)),
         pl.BlockSpec((1, 128), lambda c: (0, c)), ANY],
        [pl.BlockSpec((SEQ, 128), lambda c: (0, U_COL + c)),
         pl.BlockSpec((None, 128, 2 * CHUNK_STATES), lambda c: (c, 0, 0)),
         pl.BlockSpec((None, 2 * CHUNK_STATES, 128), lambda c: (c, 0, 0)),
         pl.BlockSpec((None, 1, 2 * CHUNK_STATES), lambda c: (c, 0, 0)),
         pl.BlockSpec((1, 128), lambda c: (0, c))],
        [_sds((SEQ, IN_WIDTH), BF16), _sds((SSM_CHUNKS, 128, 2 * CHUNK_STATES), F32),
         _sds((SSM_CHUNKS, 2 * CHUNK_STATES, 128), F32), _sds((SSM_CHUNKS, 1, 2 * CHUNK_STATES), F32), _sds((1, SSM_WIDTH), F32)],
        [pltpu.VMEM((SEQ, 2 * CHUNK_STATES), F32)] * 2, [dys, proj, h, bmat, cmat, a_chunks, d_skip, dproj], ride, aliases={7: 0})


def _ssm_tables(abar_re, abar_im, bbar_re_t, bbar_im_t, c_re, c_im):
    bmat = _block_diag(bbar_re_t, bbar_im_t, 1.0, True).astype(BF16)
    cmat = _block_diag(c_re, c_im, -1.0, False).astype(BF16)
    a_chunks = jnp.concatenate([abar_re.reshape(SSM_CHUNKS, 1, CHUNK_STATES), abar_im.reshape(SSM_CHUNKS, 1, CHUNK_STATES)], axis=2)
    return bmat, cmat, a_chunks


GL_COL = (3 * QKV_WIDTH + SSM_WIDTH) // D_MODEL
GELU_C = math.sqrt(2.0 / math.pi)
GELU_A = 0.044715


def _sds(shape, dtype):
    return jax.ShapeDtypeStruct(shape, dtype)


def _gelu(x):
    t = jnp.tanh(GELU_C * (x + GELU_A * x * x * x))
    return 0.5 * x * (1.0 + t), t


def _gelu_grad(x, t):
    return 0.5 * (1.0 + t) + 0.5 * x * (1.0 - t * t) * GELU_C * (1.0 + 3.0 * GELU_A * x * x)


def _layer_norm(r, g, b):
    mu = jnp.mean(r, axis=-1, keepdims=True)
    xc = r - mu
    rstd = lax.rsqrt(jnp.mean(xc * xc, axis=-1, keepdims=True) + LN_EPS)
    xhat = xc * rstd
    return xhat * g + b, xhat, rstd


def _layer_norm_bwd(dy, xhat, rstd, g):
    dxhat = dy * g
    m1 = jnp.mean(dxhat, axis=-1, keepdims=True)
    m2 = jnp.mean(dxhat * xhat, axis=-1, keepdims=True)
    return rstd * (dxhat - m1 - xhat * m2)


def _proj(x, w_in, ride=None):
    tm, tn = 1024, 1792

    def body(x_ref, w_ref, o_ref):
        o_ref[...] = _dot(x_ref[...].astype(BF16), _side_by_side(w_ref))

    return _call(
        body, "proj", (SEQ // tm, IN_WIDTH // tn),
        [pl.BlockSpec((tm, D_MODEL), lambda i, j: (i, 0)), pl.BlockSpec((2, D_MODEL, tn // 2), lambda i, j: (j, 0, 0))],
        [pl.BlockSpec((tm, tn), lambda i, j: (i, j))], [_sds((SEQ, IN_WIDTH), F32)], [], [x, w_in], ride)


def _row_spec(tm, width, col=0):
    return pl.BlockSpec((tm, width), lambda i, col=col: (i, col))


def _full_spec(shape):
    return pl.BlockSpec(shape, lambda i: (0,) * len(shape))


def _weight_spec(shape):
    return pl.BlockSpec(shape, lambda i: (0,) * len(shape), pipeline_mode=pl.Buffered(1))


def _mixer_out(attn, ys, proj, x, w_ab, w_sb, w_glu, w_out, b_gate, ln_g, ln_b, ride=None):
    tm = 512

    def body(attn_ref, ys_ref, gl0_ref, gl1_ref, x_ref, wab_ref, wsb_ref, wglu_ref, wout_ref, bg_ref, g_ref, b_ref,
             h_ref, xhat_ref, rstd_ref, glu_ref, ya_ref, yssm_ref):
        gy, _ = _gelu(ys_ref[...])
        glu = _dot(gy.astype(BF16), _side_by_side(wglu_ref))
        glu_ref[...] = glu.astype(BF16)
        y_s = glu[:, :SSM_WIDTH] * jax.nn.sigmoid(glu[:, SSM_WIDTH:])
        y_ssm = _dot(y_s.astype(BF16), _side_by_side(wsb_ref))
        y_attn = _dot(attn_ref[...].astype(BF16), _side_by_side(wab_ref))
        ya_ref[...] = y_attn.astype(BF16)
        yssm_ref[...] = y_ssm.astype(BF16)
        g0 = jax.nn.sigmoid(gl0_ref[...] + _side_by_side(bg_ref, 0))
        g1 = jax.nn.sigmoid(gl1_ref[...] + _side_by_side(bg_ref, 1))
        mixed = g0 * y_attn + g1 * y_ssm
        r1 = DN_ALPHA * x_ref[...] + _dot(mixed.astype(BF16), wout_ref[...])
        h, xhat, rstd = _layer_norm(r1, g_ref[...], b_ref[...])
        h_ref[...] = h
        xhat_ref[...] = xhat
        rstd_ref[...] = jnp.broadcast_to(rstd, (tm, 128))

    wide = _sds((SEQ, D_MODEL), F32)
    return _call(
        body, "mixer_out", (SEQ // tm,),
        [_row_spec(tm, ATTN_WIDTH), _row_spec(tm, SSM_WIDTH), _row_spec(tm, D_MODEL, GL_COL), _row_spec(tm, D_MODEL, GL_COL + 1),
         _row_spec(tm, D_MODEL), _weight_spec((N_DEV, ATTN_WIDTH, 128)), _weight_spec((N_DEV, SSM_WIDTH, 128)),
         _weight_spec((N_DEV, SSM_WIDTH, 128)), _weight_spec((D_MODEL, D_MODEL)), _full_spec((N_DEV, 2, 128)),
         _full_spec((1, D_MODEL)), _full_spec((1, D_MODEL))],
        [_row_spec(tm, D_MODEL), _row_spec(tm, D_MODEL), _row_spec(tm, 128), _row_spec(tm, D_MODEL),
         _row_spec(tm, D_MODEL), _row_spec(tm, D_MODEL)],
        [wide, wide, _sds((SEQ, 128), F32)] + [_sds((SEQ, D_MODEL), BF16)] * 3, [],
        [attn, ys, proj, proj, x, w_ab, w_sb, w_glu, w_out, b_gate, ln_g, ln_b], ride)


def _ff_up(h, w_gate, w_up, ride=None):
    tm, tn = 1024, 768

    def body(h_ref, wg_ref, wu_ref, a_ref, b_ref, f_ref):
        hb = h_ref[...].astype(BF16)
        a, b = _dot(hb, _side_by_side(wg_ref)), _dot(hb, _side_by_side(wu_ref))
        a_ref[...] = a.astype(BF16)
        b_ref[...] = b.astype(BF16)
        f_ref[...] = (a * jax.nn.sigmoid(a) * b).astype(BF16)

    tile = pl.BlockSpec((tm, tn), lambda i, j: (i, j))
    wtile = pl.BlockSpec((tn // FF_PAD, D_MODEL, FF_PAD), lambda i, j: (j, 0, 0))
    out = _sds((SEQ, D_FF_PAD), BF16)
    return _call(body, "ff_up", (SEQ // tm, D_FF_PAD // tn), [pl.BlockSpec((tm, D_MODEL), lambda i, j: (i, 0)), wtile, wtile],
                 [tile, tile, tile], [out, out, out], [], [h, w_gate, w_up], ride)


def _ff_down_loss(f, w_down, h, target, ln_g, ln_b):
    tm = 512

    def body(f_ref, w_ref, h_ref, t_ref, g_ref, b_ref, dr_ref, dg_ref, db_ref, loss_ref):
        @pl.when(pl.program_id(0) == 0)
        def _():
            dg_ref[...] = jnp.zeros_like(dg_ref)
            db_ref[...] = jnp.zeros_like(db_ref)
            loss_ref[...] = jnp.zeros_like(loss_ref)

        r2 = DN_ALPHA * h_ref[...] + _dot(f_ref[...], w_ref[...])
        g = g_ref[...]
        out, xhat, rstd = _layer_norm(r2, g, b_ref[...])
        err = out - t_ref[...]
        loss_ref[...] += 0.5 * jnp.sum(jnp.mean(err * err, axis=-1, keepdims=True), axis=0, keepdims=True)
        dout = err * (1.0 / D_MODEL)
        dg_ref[...] += jnp.sum(dout * xhat, axis=0, keepdims=True)
        db_ref[...] += jnp.sum(dout, axis=0, keepdims=True)
        dr_ref[...] = _layer_norm_bwd(dout, xhat, rstd, g)

    vec = _sds((1, D_MODEL), F32)
    return _pallas_call(
        body, name="ff_down_loss", grid=(SEQ // tm,),
        in_specs=[_row_spec(tm, D_FF_PAD), _weight_spec((D_FF_PAD, D_MODEL)), _row_spec(tm, D_MODEL), _row_spec(tm, D_MODEL),
                  _full_spec((1, D_MODEL)), _full_spec((1, D_MODEL))],
        out_specs=(_row_spec(tm, D_MODEL), _full_spec((1, D_MODEL)), _full_spec((1, D_MODEL)), _full_spec((1, 128))),
        out_shape=(_sds((SEQ, D_MODEL), F32), vec, vec, _sds((1, 128), F32)),
        compiler_params=_cparams(dimension_semantics=("arbitrary",)),
    )(f, w_down, h, target, ln_g, ln_b)


def _ff_down_bwd(dr2, w_down, a, b):
    tm, tn = 1024, 768

    def body(dr_ref, w_ref, a_ref, b_ref, da_ref, db_ref):
        df = _dot_nt(dr_ref[...].astype(BF16), w_ref[...])
        av, bv = a_ref[...].astype(F32), b_ref[...].astype(F32)
        sg = jax.nn.sigmoid(av)
        da_ref[...] = (df * bv * sg * (1.0 + av * (1.0 - sg))).astype(BF16)
        db_ref[...] = (df * av * sg).astype(BF16)

    tile = pl.BlockSpec((tm, tn), lambda i, j: (i, j))
    out = _sds((SEQ, D_FF_PAD), BF16)
    return _pallas_call(
        body, name="ff_down_bwd", grid=(SEQ // tm, D_FF_PAD // tn),
        in_specs=[pl.BlockSpec((tm, D_MODEL), lambda i, j: (i, 0)), pl.BlockSpec((tn, D_MODEL), lambda i, j: (j, 0)), tile, tile],
        out_specs=(tile, tile), out_shape=(out, out),
        compiler_params=_cparams(dimension_semantics=("arbitrary", "arbitrary")),
    )(dr2, w_down, a, b)


def _ff_up_bwd(da, db, w_gate, w_up, dr2, xhat1, rstd1, ln_g, ride=None):
    tm, tk = 1024, 768
    nk = D_FF_PAD // tk

    def body(da_ref, db_ref, wg_ref, wu_ref, dr2_ref, xhat_ref, rstd_ref, g_ref, dr1_ref, dg_ref, dbias_ref, acc):
        i, k = pl.program_id(0), pl.program_id(1)

        @pl.when(jnp.logical_and(i == 0, k == 0))
        def _():
            dg_ref[...] = jnp.zeros_like(dg_ref)
            dbias_ref[...] = jnp.zeros_like(dbias_ref)

        part = _dot_nt(da_ref[...], _side_by_side(wg_ref)) + _dot_nt(db_ref[...], _side_by_side(wu_ref))

        @pl.when(k == 0)
        def _():
            acc[...] = part

        @pl.when(k > 0)
        def _():
            acc[...] += part

        @pl.when(k == nk - 1)
        def _():
            dh = DN_ALPHA * dr2_ref[...] + acc[...]
            xhat = xhat_ref[...]
            dg_ref[...] += jnp.sum(dh * xhat, axis=0, keepdims=True)
            dbias_ref[...] += jnp.sum(dh, axis=0, keepdims=True)
            rstd = jnp.max(rstd_ref[...], axis=1, keepdims=True)
            dr1_ref[...] = _layer_norm_bwd(dh, xhat, rstd, g_ref[...])

    hid = pl.BlockSpec((tm, tk), lambda i, k: (i, k))
    wtile = pl.BlockSpec((tk // FF_PAD, D_MODEL, FF_PAD), lambda i, k: (k, 0, 0))
    row = pl.BlockSpec((tm, D_MODEL), lambda i, k: (i, 0))
    vec = pl.BlockSpec((1, D_MODEL), lambda i, k: (0, 0))
    return _call(
        body, "ff_up_bwd", (SEQ // tm, nk),
        [hid, hid, wtile, wtile, row, row, pl.BlockSpec((tm, 128), lambda i, k: (i, 0)), vec],
        [row, vec, vec], [_sds((SEQ, D_MODEL), F32), _sds((1, D_MODEL), F32), _sds((1, D_MODEL), F32)],
        [pltpu.VMEM((tm, D_MODEL), F32)], [da, db, w_gate, w_up, dr2, xhat1, rstd1, ln_g], ride)


def _mixer_bwd(dr1, proj, y_attn, y_ssm, glu, ys, w_ab, w_sb, w_glu, w_out, b_gate):
    tm = 256

    def body(dr1_ref, gl0_ref, gl1_ref, ya_ref, yssm_ref, glu_ref, ys_ref, wab_ref, wsb_ref, wglu_ref, wout_ref, bg_ref,
             dya_ref, dyssm_ref, dgl_ref, dattn_ref, dglu_ref, dys_ref, mixed_ref, ysb_ref, gy_ref, dbg_ref, stage, copied):
        @pl.when(pl.program_id(0) == 0)
        def _():
            dbg_ref[...] = jnp.zeros_like(dbg_ref)

        dmixed = _dot_nt(dr1_ref[...].astype(BF16), wout_ref[...])
        g0 = jax.nn.sigmoid(gl0_ref[...] + _side_by_side(bg_ref, 0))
        g1 = jax.nn.sigmoid(gl1_ref[...] + _side_by_side(bg_ref, 1))
        y_attn, y_ssm = ya_ref[...].astype(F32), yssm_ref[...].astype(F32)
        mixed_ref[...] = (g0 * y_attn + g1 * y_ssm).astype(BF16)
        dya = (dmixed * g0).astype(BF16)
        dyssm = (dmixed * g1).astype(BF16)
        dya_ref[...] = dya
        dyssm_ref[...] = dyssm
        dgl0 = dmixed * y_attn * g0 * (1.0 - g0)
        dgl1 = dmixed * y_ssm * g1 * (1.0 - g1)
        i, last = pl.program_id(0), SEQ // tm - 1
        slot = i & 1

        def copy_out(buffer, tile):
            window = dgl_ref.at[pl.ds(pl.multiple_of(tile * tm, tm), tm), pl.ds(GL_COL * D_MODEL, 2 * D_MODEL)]
            return pltpu.make_async_copy(stage.at[buffer], window, copied.at[buffer])

        @pl.when(i >= 2)
        def _():
            copy_out(slot, i - 2).wait()

        stage[slot, :, :D_MODEL] = dgl0.astype(BF16)
        stage[slot, :, D_MODEL:] = dgl1.astype(BF16)
        copy_out(slot, i).start()

        @pl.when(i == last)
        def _():
            copy_out(1 - slot, i - 1).wait()
            copy_out(slot, i).wait()
        dbg_ref[:, :D_MODEL] += jnp.sum(dgl0, axis=0, keepdims=True)
        dbg_ref[:, D_MODEL:] += jnp.sum(dgl1, axis=0, keepdims=True)
        dattn_ref[...] = _dot_nt(dya, _side_by_side(wab_ref))
        dy_s = _dot_nt(dyssm, _side_by_side(wsb_ref))
        glu = glu_ref[...].astype(F32)
        glu1, sg = glu[:, :SSM_WIDTH], jax.nn.sigmoid(glu[:, SSM_WIDTH:])
        ysb_ref[...] = (glu1 * sg).astype(BF16)
        dglu1 = (dy_s * sg).astype(BF16)
        dglu2 = (dy_s * glu1 * sg * (1.0 - sg)).astype(BF16)
        dglu_ref[:, :SSM_WIDTH] = dglu1
        dglu_ref[:, SSM_WIDTH:] = dglu2
        dgy = _dot_nt(jnp.concatenate([dglu1, dglu2], axis=1), _side_by_side(wglu_ref))
        ys = ys_ref[...]
        gy, t = _gelu(ys)
        gy_ref[...] = gy.astype(BF16)
        dys_ref[...] = dgy * _gelu_grad(ys, t)

    wide_b, half_b = _sds((SEQ, D_MODEL), BF16), _sds((SEQ, SSM_WIDTH), BF16)
    half_f = _sds((SEQ, SSM_WIDTH), F32)
    return _pallas_call(
        body, name="mixer_bwd", grid=(SEQ // tm,),
        in_specs=[_row_spec(tm, D_MODEL), _row_spec(tm, D_MODEL, GL_COL), _row_spec(tm, D_MODEL, GL_COL + 1), _row_spec(tm, D_MODEL),
                  _row_spec(tm, D_MODEL), _row_spec(tm, D_MODEL), _row_spec(tm, SSM_WIDTH), _full_spec((N_DEV, ATTN_WIDTH, 128)),
                  _full_spec((N_DEV, SSM_WIDTH, 128)), _full_spec((N_DEV, SSM_WIDTH, 128)), _full_spec((D_MODEL, D_MODEL)),
                  _full_spec((N_DEV, 2, 128))],
        out_specs=(_row_spec(tm, D_MODEL), _row_spec(tm, D_MODEL), ANY, _row_spec(tm, ATTN_WIDTH),
                   _row_spec(tm, D_MODEL), _row_spec(tm, SSM_WIDTH), _row_spec(tm, D_MODEL), _row_spec(tm, SSM_WIDTH),
                   _row_spec(tm, SSM_WIDTH), _full_spec((1, 2 * D_MODEL))),
        out_shape=(wide_b, wide_b, _sds((SEQ, IN_WIDTH), BF16), half_f, wide_b, half_f, wide_b, half_b, half_b,
                   _sds((1, 2 * D_MODEL), F32)),
        scratch_shapes=[pltpu.VMEM((2, tm, 2 * D_MODEL), BF16), pltpu.SemaphoreType.DMA((2,))],
        compiler_params=_cparams(dimension_semantics=("arbitrary",)),
    )(dr1, proj, proj, y_attn, y_ssm, glu, ys, w_ab, w_sb, w_glu, w_out, b_gate)


def _grad_x(dproj, w_in, dr1, ride=None):
    tm, tk = 1024, 1792
    nk = IN_WIDTH // tk

    def body(dp_ref, w_ref, dr1_ref, o_ref, acc):
        k = pl.program_id(1)
        part = _dot_nt(dp_ref[...], _side_by_side(w_ref))

        @pl.when(k == 0)
        def _():
            acc[...] = part

        @pl.when(k > 0)
        def _():
            acc[...] += part

        @pl.when(k == nk - 1)
        def _():
            o_ref[...] = DN_ALPHA * dr1_ref[...] + acc[...]

    row = pl.BlockSpec((tm, D_MODEL), lambda i, k: (i, 0))
    return _call(
        body, "grad_x", (SEQ // tm, nk),
        [pl.BlockSpec((tm, tk), lambda i, k: (i, k)), pl.BlockSpec((2, D_MODEL, tk // 2), lambda i, k: (k, 0, 0)), row],
        [row], [_sds((SEQ, D_MODEL), F32)], [pltpu.VMEM((tm, D_MODEL), F32)], [dproj, w_in, dr1], ride)


def _weight_grad(a, b, name, shard_cols=None, ride=None):
    k, n = a.shape[1], b.shape[1]
    tk = min(k, 512) if shard_cols else k // N_DEV
    tn = n // 4 if shard_cols else min(n, 1024)

    def body(a_ref, b_ref, o_ref):
        grad = _dot_tn(a_ref[...].astype(BF16), b_ref[...].astype(BF16))
        if shard_cols:
            o_ref[0] = grad[:, :shard_cols].astype(BF16)
            o_ref[1] = grad[:, shard_cols:].astype(BF16)
        else:
            o_ref[...] = grad.astype(BF16)

    if shard_cols:
        out_spec = pl.BlockSpec((2, None, tk, shard_cols), lambda kk, j: (0, j, kk, 0))
        out_shape = _sds((2, 4, k, shard_cols), BF16)
    else:
        out_spec = pl.BlockSpec((None, None, tk, tn), lambda kk, j: (kk % 2, kk // 2, 0, j))
        out_shape = _sds((2, 4, tk, n), BF16)
    out = _call(body, name, (k // tk, n // tn),
                [pl.BlockSpec((SEQ, tk), lambda kk, j: (0, kk)), pl.BlockSpec((SEQ, tn), lambda kk, j: (0, j))],
                [out_spec], [out_shape], [], [a, b], ride)
    return out[0] if ride is None else out


MESH = pl.DeviceIdType.MESH
ANY = pl.BlockSpec(memory_space=pl.ANY)


def _place():
    return lax.axis_index("x"), lax.axis_index("y"), lax.axis_index("c")


def _other_chips(x, y):
    return [(1 - x, y), (x, 1 - y), (1 - x, 1 - y)]


class _Ride:
    def __init__(self, operands, results, aliases, sems, start, wait):
        self.operands, self.results, self.aliases, self.sems = list(operands), list(results), dict(aliases), list(sems)
        self.start, self.wait = start, wait

    def __add__(self, other):
        n_in, n_out, n_sem = len(self.operands), len(self.results), len(self.sems)

        def both(which):
            def run(ins, outs, sems):
                getattr(self, which)(ins[:n_in], outs[:n_out], sems[:n_sem])
                getattr(other, which)(ins[n_in:], outs[n_out:], sems[n_sem:])
            return run

        aliases = {**self.aliases, **{n_in + i: n_out + j for i, j in other.aliases.items()}}
        return _Ride(self.operands + other.operands, self.results + other.results, aliases, self.sems + other.sems,
                     both("start"), both("wait"))


def _call(body, name, grid, in_specs, out_specs, out_shape, scratch_shapes, operands, ride=None, aliases=None):
    in_specs, out_specs, out_shape = list(in_specs), list(out_specs), list(out_shape)
    scratch_shapes, operands, aliases = list(scratch_shapes), list(operands), dict(aliases or {})
    kernel_body = body
    if ride is not None:
        n_in, n_out, n_scr, r_in, r_out = len(in_specs), len(out_specs), len(scratch_shapes), len(ride.operands), len(ride.results)

        def kernel_body(*refs):
            out0, scr0 = n_in + r_in, n_in + r_in + n_out + r_out
            ride_refs = (refs[n_in:out0], refs[out0 + n_out:scr0], refs[scr0 + n_scr:])
            ids = [pl.program_id(i) for i in range(len(grid))]
            first = functools.reduce(jnp.logical_and, [i == 0 for i in ids])
            last = functools.reduce(jnp.logical_and, [i == g - 1 for i, g in zip(ids, grid)])

            @pl.when(first)
            def _():
                ride.start(*ride_refs)

            body(*refs[:n_in], *refs[out0:out0 + n_out], *refs[scr0:scr0 + n_scr])

            @pl.when(last)
            def _():
                ride.wait(*ride_refs)

        aliases.update({n_in + i: n_out + j for i, j in ride.aliases.items()})
        in_specs += [ANY] * r_in
        out_specs += [ANY] * r_out
        out_shape += ride.results
        scratch_shapes += ride.sems
        operands += ride.operands
    return _pallas_call(
        kernel_body, name=name, grid=grid, in_specs=in_specs, out_specs=out_specs, out_shape=out_shape,
        scratch_shapes=scratch_shapes, input_output_aliases=aliases,
        compiler_params=_cparams(dimension_semantics=("arbitrary",) * len(grid)),
    )(*operands)


def _after(*arrays):
    return _Ride(arrays, [], {}, [], lambda *refs: None, lambda *refs: None)


def _gather_first_level(shards):
    n = len(shards)

    def copies(ins, outs, sems, landed):
        send_sems, recv_sems, local_sems = sems
        x, y, c = _place()
        peers = [(x, y, 1 - c)] + [(px, py, c) for px, py in _other_chips(x, y)]

        def row(peer):
            return 4 * x + 2 * y + c if not landed else 4 * peer[0] + 2 * peer[1] + peer[2]

        local = [pltpu.make_async_copy(ins[a], outs[a].at[4 * x + 2 * y + c], local_sems.at[a]) for a in range(n)]
        remote = [pltpu.make_async_remote_copy(
            src_ref=ins[a], dst_ref=outs[a].at[row(peer)], send_sem=send_sems.at[a, k], recv_sem=recv_sems.at[a, k],
            device_id=peer, device_id_type=MESH) for a in range(n) for k, peer in enumerate(peers)]
        return local, remote

    def start(ins, outs, sems):
        local, remote = copies(ins, outs, sems, False)
        for cp in local + remote:
            cp.start()

    def wait(ins, outs, sems):
        local, sent = copies(ins, outs, sems, False)
        for cp in copies(ins, outs, sems, True)[1]:
            cp.wait_recv()
        for cp in sent:
            cp.wait_send()
        for cp in local:
            cp.wait()

    return _Ride(shards, [_sds((N_DEV,) + s.shape, s.dtype) for s in shards], {},
                 [pltpu.SemaphoreType.DMA((n, 4)), pltpu.SemaphoreType.DMA((n, 4)), pltpu.SemaphoreType.DMA((n,))], start, wait)


def _gather_second_level(buffers):
    n = len(buffers)

    def copies(outs, sems, core):
        send_sems, recv_sems = sems
        x, y, c = _place()
        return [pltpu.make_async_remote_copy(
            src_ref=outs[a].at[4 * px + 2 * py + core], dst_ref=outs[a].at[4 * px + 2 * py + core], send_sem=send_sems.at[a, j],
            recv_sem=recv_sems.at[a, j], device_id=(x, y, 1 - c), device_id_type=MESH)
            for a in range(n) for j, (px, py) in enumerate(_other_chips(x, y))]

    def start(ins, outs, sems):
        for cp in copies(outs, sems, lax.axis_index("c")):
            cp.start()

    def wait(ins, outs, sems):
        for cp in copies(outs, sems, 1 - lax.axis_index("c")):
            cp.wait_recv()
        for cp in copies(outs, sems, lax.axis_index("c")):
            cp.wait_send()

    return _Ride(buffers, [_sds(b.shape, b.dtype) for b in buffers], {i: i for i in range(n)},
                 [pltpu.SemaphoreType.DMA((n, 3)), pltpu.SemaphoreType.DMA((n, 3))], start, wait)


def _relayed_gather(shards):
    n = len(shards)
    buffers = [_sds((N_DEV,) + s.shape, s.dtype) for s in shards]
    dma = pltpu.SemaphoreType.DMA

    def remote(src, dst, send_sem, recv_sem, to):
        return pltpu.make_async_remote_copy(src_ref=src, dst_ref=dst, send_sem=send_sem, recv_sem=recv_sem,
                                            device_id=to, device_id_type=MESH)

    def row(px, py, pc):
        return 4 * px + 2 * py + pc

    def ride(operands, aliases, sems, copies):
        def start(ins, outs, sem_refs):
            local, sent = copies(ins, outs, sem_refs, False)
            for cp in local + sent:
                cp.start()

        def wait(ins, outs, sem_refs):
            local, sent = copies(ins, outs, sem_refs, False)
            for cp in copies(ins, outs, sem_refs, True)[1]:
                cp.wait_recv()
            for cp in sent:
                cp.wait_send()
            for cp in local:
                cp.wait()

        return _Ride(operands, buffers, aliases, sems, start, wait)

    def first(ins, outs, sems, landed):
        x, y, c = _place()
        peers = [(x, y, 1 - c), (1 - x, y, c), (x, 1 - y, c)]
        local = [pltpu.make_async_copy(ins[a], outs[a].at[row(x, y, c)], sems[2].at[a]) for a in range(n)]
        return local, [remote(ins[a], outs[a].at[row(*peer) if landed else row(x, y, c)], sems[0].at[a, k], sems[1].at[a, k], peer)
                       for a in range(n) for k, peer in enumerate(peers)]

    def second(ins, outs, sems, landed):
        x, y, c = _place()
        mine = 1 - c if landed else c
        copies = []
        for a in range(n):
            half = shards[a].shape[0] // 2
            over_x, over_y, diagonal = outs[a].at[row(1 - x, y, mine)], outs[a].at[row(x, 1 - y, mine)], outs[a].at[row(1 - x, 1 - y, c)]
            lower, upper = pl.ds(0, half), pl.ds(half, half)
            copies += [remote(over_x, over_x, sems[0].at[a, 0], sems[1].at[a, 0], (x, y, 1 - c)),
                       remote(over_y, over_y, sems[0].at[a, 1], sems[1].at[a, 1], (x, y, 1 - c))]
            if landed:
                copies += [remote(diagonal.at[lower], diagonal.at[lower], sems[0].at[a, 2], sems[1].at[a, 2], (1 - x, y, c)),
                           remote(diagonal.at[upper], diagonal.at[upper], sems[0].at[a, 3], sems[1].at[a, 3], (x, 1 - y, c))]
            else:
                copies += [remote(over_y.at[lower], over_y.at[lower], sems[0].at[a, 2], sems[1].at[a, 2], (1 - x, y, c)),
                           remote(over_x.at[upper], over_x.at[upper], sems[0].at[a, 3], sems[1].at[a, 3], (x, 1 - y, c))]
        return [], copies

    def third(ins, outs, sems, landed):
        x, y, c = _place()
        return [], [remote(outs[a].at[row(1 - x, 1 - y, 1 - c if landed else c)], outs[a].at[row(1 - x, 1 - y, 1 - c if landed else c)],
                           sems[0].at[a], sems[1].at[a], (x, y, 1 - c)) for a in range(n)]

    def later(copies, n_sems):
        return lambda partly: ride(partly, {i: i for i in range(n)}, [dma((n,) + n_sems), dma((n,) + n_sems)], copies)

    return ride(shards, {}, [dma((n, 3)), dma((n, 3)), dma((n,))], first), later(second, (4,)), later(third, ())


def _sibling_swap_ride(grads):
    n = len(grads)

    def copies(ins, outs, sems):
        x, y, c = _place()
        return [pltpu.make_async_remote_copy(
            src_ref=ins[a].at[1 - c], dst_ref=outs[a], send_sem=sems[0].at[a], recv_sem=sems[1].at[a],
            device_id=(x, y, 1 - c), device_id_type=MESH) for a in range(n)]

    def start(ins, outs, sems):
        for cp in copies(ins, outs, sems):
            cp.start()

    def wait(ins, outs, sems):
        for cp in copies(ins, outs, sems):
            cp.wait()

    return _Ride(grads, [_sds(g.shape[1:], g.dtype) for g in grads], {},
                 [pltpu.SemaphoreType.DMA((n,)), pltpu.SemaphoreType.DMA((n,))], start, wait)


def _chip_swap_ride(sums):
    n = len(sums)

    def copies(ins, outs, sems, landed):
        send_sems, recv_sems, local_sems = sems
        x, y, c = _place()
        mine = 2 * x + y
        local = [pltpu.make_async_copy(ins[a].at[mine], outs[a].at[mine], local_sems.at[a]) for a in range(n)]
        remote = [pltpu.make_async_remote_copy(
            src_ref=ins[a].at[2 * px + py], dst_ref=outs[a].at[2 * px + py if landed else mine], send_sem=send_sems.at[a, j],
            recv_sem=recv_sems.at[a, j], device_id=(px, py, c), device_id_type=MESH)
            for a in range(n) for j, (px, py) in enumerate(_other_chips(x, y))]
        return local, remote

    def start(ins, outs, sems):
        local, remote = copies(ins, outs, sems, False)
        for cp in local + remote:
            cp.start()

    def wait(ins, outs, sems):
        local, sent = copies(ins, outs, sems, False)
        for cp in copies(ins, outs, sems, True)[1]:
            cp.wait_recv()
        for cp in sent:
            cp.wait_send()
        for cp in local:
            cp.wait()

    return _Ride(sums, [_sds(s.shape, s.dtype) for s in sums], {},
                 [pltpu.SemaphoreType.DMA((n, 3)), pltpu.SemaphoreType.DMA((n, 3)), pltpu.SemaphoreType.DMA((n,))], start, wait)


def _send_buffers(shards, name):
    n = len(shards)

    def body(*refs):
        for (w, transposed, rows, cols), w_ref, o_ref in zip(shards, refs[:n], refs[n:]):
            if transposed:
                c, r = w.shape
                padded = jnp.concatenate([w_ref[...], jnp.zeros((cols - c, r), F32)], axis=0) if cols > c else w_ref[...]
                o_ref[...] = padded.T.astype(BF16)
            else:
                r, c = w.shape
                if (r, c) != (rows, cols):
                    o_ref[...] = jnp.zeros((rows, cols), BF16)
                o_ref[:r, :c] = w_ref[...].astype(BF16)

    return _pallas_call(body, name=name, out_shape=[_sds((rows, cols), BF16) for _, _, rows, cols in shards])(
        *[w for w, _, _, _ in shards])


def _all_gather(shards, name):
    n = len(shards)
    first, second, third = _relayed_gather(shards)
    levels = [first, second(shards), third(shards)]
    counts = [len(level.sems) for level in levels]

    def body(*refs):
        ins, outs, sems = refs[:n], refs[n:2 * n], refs[2 * n:]
        for i, level in enumerate(levels):
            mine = sems[sum(counts[:i]):sum(counts[:i + 1])]
            level.start(ins, outs, mine)
            level.wait(ins, outs, mine)

    return _pallas_call(
        body, name=name, in_specs=[ANY] * n, out_specs=[ANY] * n, out_shape=first.results,
        scratch_shapes=[s for level in levels for s in level.sems],
    )(*shards)


def _exchange(ride, name):
    n_in, n_out = len(ride.operands), len(ride.results)

    def body(*refs):
        ride.start(refs[:n_in], refs[n_in:n_in + n_out], refs[n_in + n_out:])
        ride.wait(refs[:n_in], refs[n_in:n_in + n_out], refs[n_in + n_out:])

    return _pallas_call(body, name=name, in_specs=[ANY] * n_in, out_specs=[ANY] * n_out, out_shape=ride.results,
                        scratch_shapes=ride.sems, input_output_aliases=ride.aliases)(*ride.operands)


HBM = pl.BlockSpec(memory_space=pltpu.HBM)
SEMAPHORES = pl.BlockSpec(memory_space=pltpu.SEMAPHORE)
IN_FLIGHT = pltpu.CompilerParams(has_side_effects=pltpu.SideEffectType.DATAFLOW_SIDE_EFFECTING)


def _chip_swap_copies(src_refs, land_refs, send_sems, recv_sems, landed):
    x, y, c = _place()
    return [pltpu.make_async_remote_copy(
        src_ref=src.at[2 * px + py], dst_ref=land.at[2 * px + py if landed else 2 * x + y], send_sem=send_sems.at[3 * a + j],
        recv_sem=recv_sems.at[3 * a + j], device_id=(px, py, c), device_id_type=MESH)
        for a, (src, land) in enumerate(zip(src_refs, land_refs)) for j, (px, py) in enumerate(_other_chips(x, y))]


def _chip_swap_start(sums, name):
    n = len(sums)

    def body(*refs):
        src_refs, land_refs, (send_sems, recv_sems), token = refs[:n], refs[n:2 * n], refs[2 * n:2 * n + 2], refs[-1]
        for cp in _chip_swap_copies(src_refs, land_refs, send_sems, recv_sems, False):
            cp.start()
        token[...] = jnp.zeros_like(token)

    kept = [pltpu.HBM(s.shape, s.dtype) for s in sums]
    out = _pallas_call(
        body, name=name,
        out_shape=[pltpu.SemaphoreType.DMA((3 * n,)), pltpu.SemaphoreType.DMA((3 * n,))] + kept + kept + [_sds((8, 128), F32)],
        in_specs=[HBM] * (2 * n), out_specs=[SEMAPHORES, SEMAPHORES] + [HBM] * (2 * n) + [pl.BlockSpec(memory_space=pltpu.VMEM)],
        input_output_aliases={i: 2 + i for i in range(2 * n)}, compiler_params=IN_FLIGHT,
    )(*[pltpu.with_memory_space_constraint(s, pltpu.HBM) for s in sums],
      *[pltpu.with_memory_space_constraint(lax.empty(s.shape, s.dtype), pltpu.HBM) for s in sums])
    return out[0], out[1], out[2:2 + n], out[2 + n:2 + 2 * n], out[-1]


def _chip_swap_wait(send_sems, recv_sems, sums, landings, after, name):
    n = len(sums)

    def body(*refs):
        src_refs, land_refs, (send_sems, recv_sems) = refs[:n], refs[n:2 * n], refs[2 * n:2 * n + 2]
        for cp in _chip_swap_copies(src_refs, land_refs, send_sems, recv_sems, False):
            cp.wait_send()
        for cp in _chip_swap_copies(src_refs, land_refs, send_sems, recv_sems, True):
            cp.wait_recv()

    out = _pallas_call(
        body, name=name, out_shape=[pltpu.HBM(s.shape, s.dtype) for s in list(sums) + list(landings)],
        in_specs=[HBM] * (2 * n) + [SEMAPHORES, SEMAPHORES] + [ANY] * len(after), out_specs=[HBM] * (2 * n),
        input_output_aliases={i: i for i in range(2 * n)}, compiler_params=IN_FLIGHT,
    )(*sums, *landings, send_sems, recv_sems, *after)
    return out[:n], out[n:]


def _pair_sums(gs, rs, core, name):
    n_arrays = len(gs)

    def body(core_ref, *refs):
        for g_ref, r_ref, o_ref in zip(refs[:n_arrays], refs[n_arrays:2 * n_arrays], refs[2 * n_arrays:]):
            o_ref[...] = (g_ref[...].astype(F32) + r_ref[...].astype(F32)).astype(o_ref.dtype)

    def own(g):
        return pl.BlockSpec((None, None) + g.shape[2:], lambda p, core_ref: (core_ref[0], p, 0, 0))

    def chip(g):
        return pl.BlockSpec((None,) + g.shape[2:], lambda p, core_ref: (p, 0, 0))

    return _pallas_call(
        body, name=name,
        grid_spec=pltpu.PrefetchScalarGridSpec(
            num_scalar_prefetch=1, grid=(4,), in_specs=[own(g) for g in gs] + [chip(g) for g in gs],
            out_specs=[chip(g) for g in gs]),
        out_shape=[_sds(g.shape[1:], g.dtype) for g in gs], compiler_params=_cparams(dimension_semantics=("arbitrary",)),
    )(core, *gs, *rs)


def _adamw_math(w, g, m, v):
    m = ADAM_B1 * m + (1.0 - ADAM_B1) * g
    v = ADAM_B2 * v + (1.0 - ADAM_B2) * (g * g)
    m_hat = m / (1.0 - ADAM_B1 ** ADAM_STEP)
    v_hat = v / (1.0 - ADAM_B2 ** ADAM_STEP)
    return -ADAM_LR * (m_hat / (jnp.sqrt(v_hat) + ADAM_EPS) + ADAM_WD * w), m, v


def _adamw_many(weights, name, ride=None):
    steps = 4
    in_specs, out_specs, out_shape, operands, tiles = [], [], [], [], []
    for w, m, v, parts, own, transposed in weights:
        _, pr, pc = parts.shape
        if transposed:
            c, r = w.shape
            tile = pl.BlockSpec((c, r // steps), lambda i: (0, i))
            part_tile = pl.BlockSpec((4, r // steps, pc), lambda i: (0, i, 0))
            tiles.append((c, r // steps))
        elif w.shape[0] % (8 * steps) == 0:
            r, c = w.shape
            tile = pl.BlockSpec((r // steps, c), lambda i: (i, 0))
            part_tile = pl.BlockSpec((4, r // steps, pc), lambda i: (0, i, 0))
            tiles.append((r // steps, c))
        else:
            tile = pl.BlockSpec(w.shape, lambda i: (0, 0))
            part_tile = pl.BlockSpec(parts.shape, lambda i: (0, 0, 0))
            tiles.append(w.shape)
        in_specs += [tile, tile, tile] + [part_tile] * (1 if own is None else 2)
        out_specs += [tile] * 4
        out_shape += [_sds(w.shape, F32)] * 4
        operands += [w, m, v, parts] + ([] if own is None else [own])
    n_in = len(operands)

    def body(*refs):
        ins, outs = list(refs[:n_in]), refs[n_in:]
        this_chip = 2 * lax.axis_index("x") + lax.axis_index("y")
        for k, (_, _, _, _, own, transposed) in enumerate(weights):
            w_ref, m_ref, v_ref, p_ref = ins[:4]
            own_ref = None if own is None else ins[4]
            del ins[:4 if own is None else 5]
            rows, cols = tiles[k]
            g = None
            for q in range(4):
                index = (q,) if transposed else (q, slice(0, rows), slice(0, cols))
                part = p_ref[index] if own is None else jnp.where(this_chip == q, own_ref[index], p_ref[index])
                g = part.astype(F32) if g is None else g + part.astype(F32)
            if transposed:
                g = g.T[:rows]
            g_out, d_out, m_out, v_out = outs[4 * k:4 * k + 4]
            g_out[...] = g
            d_out[...], m_out[...], v_out[...] = _adamw_math(w_ref[...], g, m_ref[...], v_ref[...])

    return _call(body, name, (steps,), in_specs, out_specs, out_shape, [], operands, ride)


SMALL = ("ssm_a_re", "ssm_a_im", "ssm_log_dt", "ssm_b_re", "ssm_b_im", "ssm_c_re", "ssm_c_im", "ssm_d",
         "ln1_g", "ln1_b", "ln2_g", "ln2_b")


def _pack_rows(arrays):
    rows = []
    for a in arrays:
        flat = a.reshape(-1)
        rows.append(jnp.pad(flat, (0, -flat.shape[0] % 128)).reshape(-1, 128))
    packed = jnp.concatenate(rows, axis=0)
    return jnp.pad(packed, ((0, -packed.shape[0] % 8), (0, 0)))


def _unpack_rows(packed, shapes):
    out, row = [], 0
    for shape in shapes:
        size = math.prod(shape)
        n_rows = -(-size // 128)
        out.append(packed[row:row + n_rows].reshape(-1)[:size].reshape(shape))
        row += n_rows
    return out


def _sum_devices(parts):
    def body(p_ref, o_ref):
        total = p_ref[0]
        for dev in range(1, N_DEV):
            total = total + p_ref[dev]
        o_ref[...] = total

    return _pallas_call(body, name="sum_devices", out_shape=_sds(parts.shape[1:], F32))(parts)


def _adamw_replicated(ws, ms, vs, gs):
    n = len(ws)

    def body(*refs):
        w_refs, m_refs, v_refs, g_refs, d_out, m_out, v_out = (refs[i * n:(i + 1) * n] for i in range(7))
        for i in range(n):
            d_out[i][...], m_out[i][...], v_out[i][...] = _adamw_math(w_refs[i][...], g_refs[i][...], m_refs[i][...], v_refs[i][...])

    out = _pallas_call(body, name="adamw_replicated", out_shape=[_sds(w.shape, F32) for w in ws] * 3,
                       compiler_params=_cparams())(*ws, *ms, *vs, *gs)
    return out[:n], out[n:2 * n], out[2 * n:]


def kernel(x, w_in, b_gate, w_attn_br, w_ssm_br, w_out, ssm_a_re, ssm_a_im, ssm_log_dt, ssm_b_re, ssm_b_im, ssm_c_re, ssm_c_im, ssm_d, w_glu, ln1_g, ln1_b, w_ff_gate, w_ff_up, w_ff_down, ln2_g, ln2_b, loss_target, m_w_in, m_b_gate, m_w_attn_br, m_w_ssm_br, m_w_out, m_ssm_a_re, m_ssm_a_im, m_ssm_log_dt, m_ssm_b_re, m_ssm_b_im, m_ssm_c_re, m_ssm_c_im, m_ssm_d, m_w_glu, m_ln1_g, m_ln1_b, m_w_ff_gate, m_w_ff_up, m_w_ff_down, m_ln2_g, m_ln2_b, v_w_in, v_b_gate, v_w_attn_br, v_w_ssm_br, v_w_out, v_ssm_a_re, v_ssm_a_im, v_ssm_log_dt, v_ssm_b_re, v_ssm_b_im, v_ssm_c_re, v_ssm_c_im, v_ssm_d, v_w_glu, v_ln1_g, v_ln1_b, v_w_ff_gate, v_w_ff_up, v_w_ff_down, v_ln2_g, v_ln2_b):
    given = dict(locals())
    x2, target = x[0], loss_target[0]
    core = lax.axis_index("c").astype(jnp.int32).reshape(1)

    sharded = ("w_in", "w_attn_br", "w_ssm_br", "w_glu", "w_ff_gate", "w_ff_up", "b_gate", "w_out", "w_ff_down")
    send_shape = dict(w_in=(D_MODEL, 896), w_attn_br=(ATTN_WIDTH, 128), w_ssm_br=(SSM_WIDTH, 128), w_glu=(SSM_WIDTH, 128),
                      w_out=(128, D_MODEL), w_ff_gate=(D_MODEL, FF_PAD), w_ff_up=(D_MODEL, FF_PAD), w_ff_down=(FF_PAD, D_MODEL))
    local = {k: given[k][0] for k in sharded}
    narrow = ("w_ff_gate", "w_ff_up")
    def to_send(k):
        return (local[k].T, True, *send_shape[k]) if k in narrow else (local[k], False, *send_shape[k])

    later = [k for k in sharded if k not in ("w_in", "b_gate")]
    sends = dict(zip(["w_in"] + later, _send_buffers([to_send("w_in")], "send_w_in")
                     + _send_buffers([to_send(k) for k in later], "send_weights")))
    sends["b_gate"] = local["b_gate"]
    mixer_weights = ("w_attn_br", "w_ssm_br", "w_glu", "b_gate", "w_out")
    ff_weights = ("w_ff_gate", "w_ff_up", "w_ff_down")
    wt = {}
    wt["w_in"], = _all_gather([sends["w_in"]], "gather_w_in")

    a_re, a_im, log_dt = ssm_a_re[0], ssm_a_im[0], ssm_log_dt[0].reshape(SSM_GROUPS, 1)
    b_re_t, b_im_t = ssm_b_re[0].transpose(0, 2, 1), ssm_b_im[0].transpose(0, 2, 1)
    abar_re, abar_im, e_re, e_im, bbar_re_t, bbar_im_t = _ssm_prep(a_re, a_im, log_dt, b_re_t, b_im_t)
    bmat, cmat, a_chunks = _ssm_tables(abar_re, abar_im, bbar_re_t, bbar_im_t, ssm_c_re[0], ssm_c_im[0])
    cos_t, sin_t = _rope_tables()

    big_mixer, ff_in = [k for k in mixer_weights if k != "b_gate"], ("w_ff_gate", "w_ff_up")
    n_mixer = len(big_mixer)
    mixer_1, mixer_2, mixer_3 = _relayed_gather([sends[k] for k in big_mixer])
    ff_in_1, ff_in_2, ff_in_3 = _relayed_gather([sends[k] for k in ff_in])
    ff_down_1, ff_down_2, ff_down_3 = _relayed_gather([sends["w_ff_down"]])
    proj, *landed = _proj(x2, wt["w_in"], mixer_1 + _gather_first_level([sends["b_gate"]]))
    mixer, bias = landed[:n_mixer], landed[n_mixer:]
    attn, lse, q_pm, k_pm, v_pm, *landed = _attn_fwd(proj, cos_t, sin_t,
                                                     mixer_2(mixer) + _gather_second_level(bias) + ff_in_1)
    mixer, b_gate_full, ff = landed[:n_mixer], landed[n_mixer], landed[n_mixer + 1:]
    ys, states, *landed = _ssm_fwd(proj, bmat, cmat, a_chunks, ssm_d, mixer_3(mixer) + ff_in_2(ff) + ff_down_1)
    wt.update(zip(big_mixer, landed[:n_mixer]))
    ff, ff_down = landed[n_mixer:n_mixer + 2], landed[n_mixer + 2:]
    wt["w_out"] = wt["w_out"].reshape(D_MODEL, D_MODEL)
    h, xhat1, rstd1, glu, y_attn, y_ssm, *landed = _mixer_out(
        attn, ys, proj, x2, wt["w_attn_br"], wt["w_ssm_br"], wt["w_glu"], wt["w_out"], b_gate_full, ln1_g, ln1_b,
        ff_in_3(ff) + ff_down_2(ff_down))
    wt.update(zip(ff_in, landed[:2]))
    ff_a, ff_b, ff_f, w_ff_down = _ff_up(h, wt["w_ff_gate"], wt["w_ff_up"], ff_down_3(landed[2:]))
    wt["w_ff_down"] = w_ff_down.reshape(D_FF_PAD, D_MODEL)
    dr2, d_ln2_g, d_ln2_b, loss_lanes = _ff_down_loss(ff_f, wt["w_ff_down"], h, target, ln2_g, ln2_b)

    def pair_sums(names, contrib, from_sibling):
        return _pair_sums([contrib[k] for k in names], from_sibling, core, "pair_sums_" + names[0])

    d_a, d_b = _ff_down_bwd(dr2, wt["w_ff_down"], ff_a, ff_b)
    contrib = dict(w_ff_gate=_weight_grad(h, d_a, "wgrad_w_ff_gate", FF_PAD),
                   w_ff_up=_weight_grad(h, d_b, "wgrad_w_ff_up", FF_PAD),
                   w_ff_down=_weight_grad(ff_f, dr2, "wgrad_w_ff_down"))
    dr1, d_ln1_g, d_ln1_b, *from_sibling = _ff_up_bwd(
        d_a, d_b, wt["w_ff_gate"], wt["w_ff_up"], dr2, xhat1, rstd1, ln1_g, _sibling_swap_ride([contrib[k] for k in ff_weights]))
    ff_sums = pair_sums(ff_weights, contrib, from_sibling)

    d_ya, d_yssm, d_proj, d_attn, d_glu, d_ys, mixed, y_s, gy, d_bg = _mixer_bwd(
        dr1, proj, y_attn, y_ssm, glu, ys, wt["w_attn_br"], wt["w_ssm_br"], wt["w_glu"], wt["w_out"], b_gate_full)
    contrib.update(w_attn_br=_weight_grad(attn, d_ya, "wgrad_w_attn_br", 128),
                   w_ssm_br=_weight_grad(y_s, d_yssm, "wgrad_w_ssm_br", 128),
                   w_glu=_weight_grad(gy, d_glu, "wgrad_w_glu", 128),
                   w_out=_weight_grad(mixed, dr1, "wgrad_w_out"),
                   b_gate=d_bg.reshape(2, 4, 2, 128).transpose(2, 1, 0, 3))
    d_proj, *landed = _attn_bwd(q_pm, k_pm, v_pm, cos_t, sin_t, attn, lse, d_attn, d_proj,
                                _chip_swap_ride(ff_sums) + _sibling_swap_ride([contrib[k] for k in mixer_weights]))
    parts, own_sums = dict(zip(ff_weights, landed[:len(ff_weights)])), {}
    mixer_sums = pair_sums(mixer_weights, contrib, landed[len(ff_weights):])
    d_proj, d_bmat, d_cmat, d_abar, d_skip, *landed = _ssm_bwd(d_ys, proj, states, bmat, cmat, a_chunks, ssm_d, d_proj,
                                                               _chip_swap_ride(mixer_sums))
    parts.update(zip(mixer_weights, landed))

    gbb_re_t, gbb_im_t = _block_diag_parts(d_bmat, True)
    gc_re, gc_im = _block_diag_parts(d_cmat, False)
    ga_re = d_abar[:, 0, :CHUNK_STATES].reshape(SSM_GROUPS, SSM_STATE)
    ga_im = d_abar[:, 0, CHUNK_STATES:].reshape(SSM_GROUPS, SSM_STATE)
    g_a_re, g_a_im, g_log_dt, g_b_re_t, g_b_im_t = _ssm_param_bwd(
        a_re, a_im, log_dt, b_re_t, b_im_t, abar_re, abar_im, e_re, e_im, ga_re, ga_im, gbb_re_t, gbb_im_t)
    mine = [g_a_re, g_a_im, g_log_dt, g_b_re_t, g_b_im_t, gc_re, -gc_im,
            d_skip, d_ln1_g, d_ln1_b, d_ln2_g, d_ln2_b]
    small_packed = _pack_rows(mine + [loss_lanes])

    contrib["w_in"], small_partly = _weight_grad(x2, d_proj, "wgrad_w_in", 896, _gather_first_level([small_packed]))
    from_sibling, every = _exchange(_sibling_swap_ride([contrib["w_in"]]) + _gather_second_level([small_partly]),
                                    "swap_w_in_with_sibling")
    w_in_sum, = pair_sums(["w_in"], contrib, [from_sibling])
    send_sems, recv_sems, w_in_sum, landing, token = _chip_swap_start([w_in_sum], "w_in_chip_swap_start")

    def adamw_of(k):
        taken = (lambda a: a.T) if k in narrow else (lambda a: a)
        return taken(local[k]), taken(given["m_" + k][0]), taken(given["v_" + k][0]), parts[k], own_sums.get(k), k in narrow

    others = [k for k in sharded if k != "w_in"]
    updated = _adamw_many([adamw_of(k) for k in others], "adamw_others", _after(token))
    grad_x, = _grad_x(d_proj, wt["w_in"], dr1, _after(token))

    def held(k, a):
        return a.transpose(0, 1, 3, 2) if k in ("ssm_b_re", "ssm_b_im") else a

    *small_grads, loss_sum = _unpack_rows(_sum_devices(every), [held(k, given[k]).shape for k in SMALL] + [(1, 128)])
    small = _adamw_replicated([held(k, given[k]) for k in SMALL], [held(k, given["m_" + k]) for k in SMALL],
                              [held(k, given["v_" + k]) for k in SMALL], small_grads)
    loss = loss_sum[0, 0]

    (own_sums["w_in"],), (parts["w_in"],) = _chip_swap_wait(
        send_sems, recv_sems, w_in_sum, landing, [grad_x, updated[0], small[0][0]], "w_in_chip_swap_wait")
    updated += _adamw_many([adamw_of("w_in")], "adamw_w_in")

    grads, deltas, new_m, new_v = {}, {}, {}, {}
    for i, k in enumerate(others + ["w_in"]):
        out = [o.T if k in narrow else o for o in updated[4 * i:4 * i + 4]]
        grads[k], deltas[k], new_m[k], new_v[k] = (o.reshape((1,) + local[k].shape) for o in out)
    for res, values in zip((grads, deltas, new_m, new_v), (small_grads,) + small):
        res.update((k, held(k, a)) for k, a in zip(SMALL, values))

    order = ("w_in", "b_gate", "w_attn_br", "w_ssm_br", "w_out", "ssm_a_re", "ssm_a_im", "ssm_log_dt", "ssm_b_re", "ssm_b_im",
             "ssm_c_re", "ssm_c_im", "ssm_d", "w_glu", "ln1_g", "ln1_b", "w_ff_gate", "w_ff_up", "w_ff_down", "ln2_g", "ln2_b")
    return (loss, grad_x[None], *[grads[k] for k in order], *[deltas[k] for k in order], *[new_m[k] for k in order],
            *[new_v[k] for k in order])
```

```python
import functools
import math

import jax
import jax.numpy as jnp
import numpy as np
from jax import lax
from jax.experimental import pallas as pl
from jax.experimental.pallas import tpu as pltpu

F32 = jnp.float32
BF16 = jnp.bfloat16

N_DEV = 8
SEQ = 2048
D_MODEL = 1024
HEAD_DIM = 64
ATTN_WIDTH = 512
QKV_WIDTH = 1536
SSM_WIDTH = 512
SSM_GROUPS = 32
SSM_GROUP = 16
SSM_STATE = 64
IN_WIDTH = 7168
D_FF = 2816
FF_SHARD = D_FF // N_DEV
FF_PAD = 384
D_FF_PAD = FF_PAD * N_DEV
DN_ALPHA = 2.0 ** 0.25
LN_EPS = 1e-5
NEG_INF = -1e30
ROPE_THETA = 10000.0
BLOCK = 128
GROUPS = ((1, 16), (4, 4), (16, 1))

ADAM_LR = 0.001
ADAM_B1 = 0.9
ADAM_B2 = 0.999
ADAM_EPS = 1e-08
ADAM_WD = 0.01
ADAM_STEP = 10

VMEM_LIMIT = 56 * 1024 * 1024


_pallas_call = pl.pallas_call


def _cparams(**kw):
    return pltpu.CompilerParams(vmem_limit_bytes=VMEM_LIMIT, **kw)


def _dot(a, b):
    return jnp.dot(a, b, preferred_element_type=F32)


def _dot_nt(a, b):
    return lax.dot_general(a, b, (((1,), (1,)), ((), ())), preferred_element_type=F32)


def _side_by_side(w_ref, row=None):
    rows = slice(None) if row is None else pl.ds(row, 1)
    return jnp.concatenate([w_ref[i, rows, :] for i in range(w_ref.shape[0])], axis=1)


def _dot_tn(a, b):
    return lax.dot_general(a, b, (((0,), (0,)), ((), ())), preferred_element_type=F32)


def _rope_tables():
    half = HEAD_DIM // 2
    inv_freq = np.float32(ROPE_THETA) ** (-np.arange(half, dtype=np.float32) / np.float32(half))
    ang = np.arange(SEQ, dtype=np.float32)[:, None] * inv_freq[None, :]
    cos, sin = np.cos(ang).astype(np.float32), np.sin(ang).astype(np.float32)
    tables = np.tile(cos, (1, 4)), np.tile(np.concatenate([-sin, sin], axis=1), (1, 2))

    def by_phase(t):
        return np.stack([t.reshape(SEQ // d, d, 128).transpose(1, 0, 2).reshape(SEQ, 128) for d, _ in GROUPS])

    return jnp.asarray(by_phase(tables[0])), jnp.asarray(by_phase(tables[1]))


def _swap_halves(x):
    lane = lax.broadcasted_iota(jnp.int32, x.shape, 1)
    return jnp.where((lane & 63) < 32, pltpu.roll(x, 96, axis=1), pltpu.roll(x, 32, axis=1))


def _group_rows(d, nb, r, i):
    src = pl.ds(i * BLOCK, BLOCK) if d == 1 else pl.ds(r + i * BLOCK * d, BLOCK, stride=d)
    return src, pl.ds((r * nb + i) * BLOCK, BLOCK)


def _attn_masks():
    a_idx = lax.broadcasted_iota(jnp.int32, (2 * BLOCK, 2 * BLOCK), 0) & (BLOCK - 1)
    c_idx = lax.broadcasted_iota(jnp.int32, (2 * BLOCK, 2 * BLOCK), 1)
    cur_ok = jnp.logical_and(c_idx >= BLOCK, c_idx - BLOCK <= a_idx)
    prev_ok = jnp.logical_and(c_idx < BLOCK, c_idx >= a_idx)
    lane = lax.broadcasted_iota(jnp.int32, (BLOCK, 128), 1)
    return cur_ok, prev_ok, lane < HEAD_DIM


def _stack_heads(t, head0):
    zero = jnp.zeros_like(t)
    return jnp.concatenate([jnp.where(head0, t, zero), jnp.where(head0, zero, t)], axis=0)


def _unstack_heads(t2, head0):
    return jnp.where(head0, t2[:BLOCK], t2[BLOCK:])


def _attn_fwd(proj, cos_t, sin_t, ride=None):
    def body(q0, q1, q2, k0, k1, k2, v0, v1, v2, cos_ref, sin_ref, attn_ref, lse_ref, qpm_ref, kpm_ref, vpm_ref,
             qs, ks, vs, os_, ms, ls, acc, mnat, lnat):
        cur_ok, prev_ok, head0 = _attn_masks()
        ks[:BLOCK, :] = jnp.zeros((BLOCK, 128), BF16)
        vs[:BLOCK, :] = jnp.zeros((BLOCK, 128), BF16)
        for g, (d, nb) in enumerate(GROUPS):
            q_ref, k_ref, v_ref = (q0, q1, q2)[g], (k0, k1, k2)[g], (v0, v1, v2)[g]
            for r in range(d):
                for i in range(nb):
                    src, dst = _group_rows(d, nb, r, i)
                    below = pl.ds(dst.start + BLOCK, BLOCK)
                    c, s = cos_ref[g, dst, :], sin_ref[g, dst, :]
                    q = q_ref[src, :]
                    k = k_ref[src, :]
                    qs[dst, :] = ((q * c + _swap_halves(q) * s) * 0.125).astype(BF16)
                    ks[below, :] = (k * c + _swap_halves(k) * s).astype(BF16)
                    vs[below, :] = v_ref[src, :].astype(BF16)
                    qpm_ref[g, dst, :], kpm_ref[g, dst, :], vpm_ref[g, dst, :] = qs[dst, :], ks[below, :], vs[below, :]

            def block(b, carry, nb=nb):
                has_prev = (b & (nb - 1)) > 0
                cur = pl.ds(pl.multiple_of(b * BLOCK, BLOCK), BLOCK)
                window = pl.ds(pl.multiple_of(b * BLOCK, BLOCK), 2 * BLOCK)
                valid = jnp.logical_or(cur_ok, jnp.logical_and(prev_ok, has_prev))
                s = jnp.where(valid, _dot_nt(_stack_heads(qs[cur, :], head0), ks[window, :]), NEG_INF)
                m = jnp.max(s, axis=1, keepdims=True)
                p = jnp.exp(s - m)
                os_[cur, :] = _unstack_heads(_dot(p.astype(BF16), vs[window, :]), head0)
                ms[cur, :] = _unstack_heads(m, head0)
                ls[cur, :] = _unstack_heads(jnp.sum(p, axis=1, keepdims=True), head0)
                return carry

            lax.fori_loop(0, SEQ // BLOCK, block, 0, unroll=16)

            for r in range(d):
                for i in range(nb):
                    src, dst = _group_rows(d, nb, r, i)
                    if g == 0:
                        acc[src, :], mnat[src, :], lnat[src, :] = os_[dst, :], ms[dst, :], ls[dst, :]
                    else:
                        m_old, m_g = mnat[src, :], ms[dst, :]
                        m_new = jnp.maximum(m_old, m_g)
                        a_old, a_g = jnp.exp(m_old - m_new), jnp.exp(m_g - m_new)
                        acc[src, :] = a_old * acc[src, :] + a_g * os_[dst, :]
                        lnat[src, :] = a_old * lnat[src, :] + a_g * ls[dst, :]
                        mnat[src, :] = m_new
        for i in range(SEQ // BLOCK):
            rows = pl.ds(i * BLOCK, BLOCK)
            l = lnat[rows, :]
            attn_ref[rows, :] = acc[rows, :] / l
            lse_ref[rows, :] = mnat[rows, :] + jnp.log(l)

    def col(base):
        return pl.BlockSpec((SEQ, 128), lambda hp, base=base: (0, base + hp))

    in_specs = [col(g * 4) for g in range(3)] + [col(12 + g * 4) for g in range(3)] + [col(24 + g * 4) for g in range(3)]
    table = pl.BlockSpec((3, SEQ, 128), lambda hp: (0, 0, 0), pipeline_mode=pl.Buffered(1))
    out = pl.BlockSpec((SEQ, 128), lambda hp: (0, hp))
    by_phase = pl.BlockSpec((3, SEQ, 128), lambda hp: (0, 0, hp))
    return _call(
        body, "attn_fwd", (4,), in_specs + [table, table], [out, out] + [by_phase] * 3,
        [_sds((SEQ, ATTN_WIDTH), F32), _sds((SEQ, ATTN_WIDTH), F32)] + [_sds((3, SEQ, ATTN_WIDTH), BF16)] * 3,
        [pltpu.VMEM((SEQ, 128), BF16)] + [pltpu.VMEM((SEQ + BLOCK, 128), BF16)] * 2 + [pltpu.VMEM((SEQ, 128), F32)] * 6,
        [proj] * 9 + [cos_t, sin_t], ride)


def _attn_bwd_group_body(g):
    d, nb = GROUPS[g]

    def body(qs_ref, ks_ref, vs_ref, cos_ref, sin_ref, lse_ref, dattn_ref, dsum_ref, dproj_ref,
             ks, vs, dos, lss, dss, dqs, dks, dvs, stage, outs, sems):
        cur_ok, prev_ok, head0 = _attn_masks()
        qs = qs_ref.at[g]
        ks[:BLOCK, :] = jnp.zeros((BLOCK, 128), BF16)
        vs[:BLOCK, :] = jnp.zeros((BLOCK, 128), BF16)
        dks[:BLOCK, :] = jnp.zeros((BLOCK, 128), F32)
        dvs[:BLOCK, :] = jnp.zeros((BLOCK, 128), F32)
        for r in range(d):
            for i in range(nb):
                src, dst = _group_rows(d, nb, r, i)
                below = pl.ds(dst.start + BLOCK, BLOCK)
                ks[below, :] = ks_ref[g, dst, :]
                vs[below, :] = vs_ref[g, dst, :]
                dos[dst, :] = dattn_ref[src, :].astype(BF16)
                for per_head, spread in ((dsum_ref[src, :], dss), (lse_ref[src, :], lss)):
                    other = pltpu.roll(per_head, HEAD_DIM, axis=1)
                    spread[0, dst, :] = jnp.where(head0, per_head, other)
                    spread[1, dst, :] = jnp.where(head0, other, per_head)
                dks[below, :] = jnp.zeros((BLOCK, 128), F32)
                dvs[below, :] = jnp.zeros((BLOCK, 128), F32)

        def per_stacked_row(spread, cur):
            h0, h1 = spread[0, cur, :], spread[1, cur, :]
            return jnp.concatenate([jnp.concatenate([h0, h0], axis=1), jnp.concatenate([h1, h1], axis=1)], axis=0)

        def block(b, carry):
            has_prev = (b & (nb - 1)) > 0
            cur = pl.ds(pl.multiple_of(b * BLOCK, BLOCK), BLOCK)
            window = pl.ds(pl.multiple_of(b * BLOCK, BLOCK), 2 * BLOCK)
            valid = jnp.logical_or(cur_ok, jnp.logical_and(prev_ok, has_prev))
            q2, do2 = _stack_heads(qs[cur, :], head0), _stack_heads(dos[cur, :], head0)
            kw, vw = ks[window, :], vs[window, :]
            s = jnp.where(valid, _dot_nt(q2, kw), NEG_INF)
            p = jnp.exp(s - per_stacked_row(lss, cur))
            ds = (p * (_dot_nt(do2, vw) - per_stacked_row(dss, cur))).astype(BF16)
            dvs[window, :] += _dot_tn(p.astype(BF16), do2)
            dks[window, :] += _dot_tn(ds, q2)
            dqs[cur, :] = _unstack_heads(_dot(ds, kw), head0)
            return carry

        lax.fori_loop(0, SEQ // BLOCK, block, 0, unroll=16)

        hp = pl.program_id(0)
        copies = []
        for kind in range(3):
            for r in range(d):
                for i in range(nb):
                    src, dst = _group_rows(d, nb, r, i)
                    below = pl.ds(dst.start + BLOCK, BLOCK)
                    if kind == 2:
                        stage[src, :] = dvs[below, :]
                    else:
                        c, s = cos_ref[g, dst, :], sin_ref[g, dst, :]
                        t = dqs[dst, :] * 0.125 if kind == 0 else dks[below, :]
                        stage[src, :] = t * c - _swap_halves(t) * s
            for i in range(SEQ // MM_ROWS):
                rows = pl.ds(i * MM_ROWS, MM_ROWS)
                outs[kind, rows, :] = stage[rows, :].astype(BF16)
            column = pl.multiple_of((kind * 12 + g * 4 + hp) * 128, 128)
            copies.append(pltpu.make_async_copy(outs.at[kind], dproj_ref.at[:, pl.ds(column, 128)], sems.at[kind]))
            copies[-1].start()
        for cp in copies:
            cp.wait()

    return body


def _attn_bwd(q_pm, k_pm, v_pm, cos_t, sin_t, attn, lse, dattn, dproj, ride=None):
    groups = [_attn_bwd_group_body(g) for g in range(3)]

    def body(qs_ref, ks_ref, vs_ref, cos_ref, sin_ref, attn_ref, lse_ref, dattn_ref, dproj_in, dproj_ref, dsum, *scratch):
        del dproj_in
        head0 = _attn_masks()[2]
        for i in range(SEQ // BLOCK):
            rows = pl.ds(i * BLOCK, BLOCK)
            prod = dattn_ref[rows, :] * attn_ref[rows, :]
            d0 = jnp.sum(jnp.where(head0, prod, 0.0), axis=1, keepdims=True)
            d1 = jnp.sum(jnp.where(head0, 0.0, prod), axis=1, keepdims=True)
            dsum[rows, :] = jnp.where(head0, d0, d1)
        for g in range(3):
            groups[g](qs_ref, ks_ref, vs_ref, cos_ref, sin_ref, lse_ref, dattn_ref, dsum, dproj_ref, *scratch)

    def col(base):
        return pl.BlockSpec((SEQ, 128), lambda hp, base=base: (0, base + hp))

    table = pl.BlockSpec((3, SEQ, 128), lambda hp: (0, 0, 0), pipeline_mode=pl.Buffered(1))
    by_phase = pl.BlockSpec((3, SEQ, 128), lambda hp: (0, 0, hp))
    return _call(
        body, "attn_bwd", (4,), [by_phase] * 3 + [table, table, col(0), col(0), col(0), ANY],
        [ANY], [_sds((SEQ, IN_WIDTH), BF16)],
        [pltpu.VMEM((SEQ, 128), F32)]
        + [pltpu.VMEM((SEQ + BLOCK, 128), BF16)] * 2 + [pltpu.VMEM((SEQ, 128), BF16)]
        + [pltpu.VMEM((2, SEQ, 128), F32)] * 2 + [pltpu.VMEM((SEQ, 128), F32)]
        + [pltpu.VMEM((SEQ + BLOCK, 128), F32)] * 2 + [pltpu.VMEM((SEQ, 128), F32)]
        + [pltpu.VMEM((3, SEQ, 128), BF16), pltpu.SemaphoreType.DMA((3,))],
        [q_pm, k_pm, v_pm, cos_t, sin_t, attn, lse, dattn, dproj], ride, aliases={8: 0})


SSM_CHUNKS = 4
CHUNK_STATES = 512
SCAN_ROWS = 8
U_COL = (3 * QKV_WIDTH) // 128


def _cmul(xr, xi, yr, yi):
    return xr * yr - xi * yi, xr * yi + xi * yr


def _ssm_prep(a_re, a_im, log_dt, b_re_t, b_im_t):
    def body(ar_ref, ai_ref, ldt_ref, br_ref, bi_ref, abr_ref, abi_ref, er_ref, ei_ref, bbr_ref, bbi_ref):
        ar, ai = ar_ref[...], ai_ref[...]
        dt = jnp.exp(ldt_ref[...])
        mag = jnp.exp(ar * dt)
        abr, abi = mag * jnp.cos(ai * dt), mag * jnp.sin(ai * dt)
        den = ar * ar + ai * ai
        nr, ni = abr - 1.0, abi
        er, ei = (nr * ar + ni * ai) / den, (ni * ar - nr * ai) / den
        abr_ref[...], abi_ref[...], er_ref[...], ei_ref[...] = abr, abi, er, ei
        er3, ei3 = er[:, None, :], ei[:, None, :]
        br, bi = br_ref[...], bi_ref[...]
        bbr_ref[...] = er3 * br - ei3 * bi
        bbi_ref[...] = er3 * bi + ei3 * br

    gp = jax.ShapeDtypeStruct(a_re.shape, F32)
    gb = jax.ShapeDtypeStruct(b_re_t.shape, F32)
    return _pallas_call(body, name="ssm_prep", out_shape=(gp, gp, gp, gp, gb, gb))(a_re, a_im, log_dt, b_re_t, b_im_t)


def _ssm_param_bwd(a_re, a_im, log_dt, b_re_t, b_im_t, abar_re, abar_im, e_re, e_im, ga_re, ga_im, gbb_re_t, gbb_im_t):
    def body(ar_ref, ai_ref, ldt_ref, br_ref, bi_ref, abr_ref, abi_ref, er_ref, ei_ref, gar_ref, gai_ref, gbr_ref, gbi_ref,
             o_ar, o_ai, o_ldt, o_br, o_bi):
        ar, ai = ar_ref[...], ai_ref[...]
        dt = jnp.exp(ldt_ref[...])
        er, ei = er_ref[...], ei_ref[...]
        br, bi, gbr, gbi = br_ref[...], bi_ref[...], gbr_ref[...], gbi_ref[...]
        er3, ei3 = er[:, None, :], ei[:, None, :]
        o_br[...] = er3 * gbr + ei3 * gbi
        o_bi[...] = er3 * gbi - ei3 * gbr
        ge_r = jnp.sum(br * gbr + bi * gbi, axis=1)
        ge_i = jnp.sum(br * gbi - bi * gbr, axis=1)
        den = ar * ar + ai * ai
        ilr, ili = ar / den, -ai / den
        t_r, t_i = _cmul(ilr, -ili, ge_r, ge_i)
        gab_r, gab_i = gar_ref[...] + t_r, gai_ref[...] + t_i
        gz_r, gz_i = _cmul(abr_ref[...], -abi_ref[...], gab_r, gab_i)
        el_r, el_i = _cmul(er, ei, ilr, ili)
        u_r, u_i = _cmul(el_r, -el_i, ge_r, ge_i)
        o_ar[...] = dt * gz_r - u_r
        o_ai[...] = dt * gz_i - u_i
        o_ldt[...] = jnp.sum(gz_r * ar + gz_i * ai, axis=1, keepdims=True) * dt

    gp = jax.ShapeDtypeStruct(a_re.shape, F32)
    gb = jax.ShapeDtypeStruct(b_re_t.shape, F32)
    return _pallas_call(body, name="ssm_param_bwd", out_shape=(gp, gp, jax.ShapeDtypeStruct(log_dt.shape, F32), gb, gb))(
        a_re, a_im, log_dt, b_re_t, b_im_t, abar_re, abar_im, e_re, e_im, ga_re, ga_im, gbb_re_t, gbb_im_t)


def _block_diag(blocks_re, blocks_im, sign_im, rows_are_channels):
    both = jnp.stack([blocks_re, sign_im * blocks_im]).reshape(2, SSM_CHUNKS, 8, SSM_GROUP, SSM_STATE)
    eye = jnp.eye(8, dtype=F32)
    if rows_are_channels:
        return jnp.einsum("rcghp,gk->cghrkp", both, eye).reshape(SSM_CHUNKS, 128, 2 * CHUNK_STATES)
    return jnp.einsum("rcghp,gk->crkpgh", both, eye).reshape(SSM_CHUNKS, 2 * CHUNK_STATES, 128)


def _block_diag_parts(mat, rows_are_channels):
    if rows_are_channels:
        six = mat.reshape(SSM_CHUNKS, 8, SSM_GROUP, 2, 8, SSM_STATE)
        parts = jnp.einsum("cghrgp->rcghp", six)
    else:
        six = mat.reshape(SSM_CHUNKS, 2, 8, SSM_STATE, 8, SSM_GROUP)
        parts = jnp.einsum("crgpgh->rcghp", six)
    parts = parts.reshape(2, SSM_GROUPS, SSM_GROUP, SSM_STATE)
    return parts[0], parts[1]


def _scan_consts(a_ref, conj, reverse):
    ar = jnp.broadcast_to(a_ref[:, :CHUNK_STATES], (SCAN_ROWS, CHUNK_STATES))
    ai = jnp.broadcast_to(a_ref[:, CHUNK_STATES:], (SCAN_ROWS, CHUNK_STATES))
    if conj:
        ai = -ai
    row = lax.broadcasted_iota(jnp.int32, (SCAN_ROWS, CHUNK_STATES), 0)
    if reverse:
        row = SCAN_ROWS - 1 - row
    zero = jnp.zeros_like(ar)
    steps = []
    pr, pi = ar, ai
    for shift in (1, 2, 4):
        keep = row >= shift
        steps.append((SCAN_ROWS - shift if reverse else shift, jnp.where(keep, pr, zero), jnp.where(keep, pi, zero)))
        pr, pi = _cmul(pr, pi, pr, pi)
    first = row == 0
    return steps, (jnp.where(first, ar, zero), jnp.where(first, ai, zero)), first


def _scan_tile(xr, xi, prev_r, prev_i, steps, carry_in, reverse):
    edge = SCAN_ROWS - 1 if reverse else 1
    cr, ci = pltpu.roll(prev_r, edge, axis=0), pltpu.roll(prev_i, edge, axis=0)
    xr, xi = xr + carry_in[0] * cr - carry_in[1] * ci, xi + carry_in[0] * ci + carry_in[1] * cr
    for shift, mr, mi in steps:
        sr, si = pltpu.roll(xr, shift, axis=0), pltpu.roll(xi, shift, axis=0)
        xr, xi = xr + mr * sr - mi * si, xi + mr * si + mi * sr
    return xr, xi


MM_ROWS = 256


def _ssm_fwd(proj, bmat, cmat, a_chunks, d_skip, ride=None):
    def body(u_ref, b_ref, c_ref, a_ref, d_ref, y_ref, states_ref, h_ref):
        for i in range(SEQ // MM_ROWS):
            rows = pl.ds(i * MM_ROWS, MM_ROWS)
            h_ref[rows, :] = _dot(u_ref[rows, :].astype(BF16), b_ref[...])
        steps, carry_in, _ = _scan_consts(a_ref, conj=False, reverse=False)

        def tile(k, carry):
            rows = pl.ds(pl.multiple_of(k * SCAN_ROWS, SCAN_ROWS), SCAN_ROWS)
            xr, xi = _scan_tile(h_ref[rows, :CHUNK_STATES], h_ref[rows, CHUNK_STATES:], carry[0], carry[1], steps, carry_in, False)
            h_ref[rows, :CHUNK_STATES] = xr
            h_ref[rows, CHUNK_STATES:] = xi
            return xr, xi

        zero = jnp.zeros((SCAN_ROWS, CHUNK_STATES), F32)
        lax.fori_loop(0, SEQ // SCAN_ROWS, tile, (zero, zero), unroll=4)
        for i in range(SEQ // MM_ROWS):
            rows = pl.ds(i * MM_ROWS, MM_ROWS)
            states = h_ref[rows, :].astype(BF16)
            states_ref[rows, :] = states
            y_ref[rows, :] = _dot(states, c_ref[...]) + d_ref[...] * u_ref[rows, :]

    return _call(
        body, "ssm_fwd", (SSM_CHUNKS,),
        [pl.BlockSpec((SEQ, 128), lambda c: (0, U_COL + c)),
         pl.BlockSpec((None, 128, 2 * CHUNK_STATES), lambda c: (c, 0, 0)),
         pl.BlockSpec((None, 2 * CHUNK_STATES, 128), lambda c: (c, 0, 0)),
         pl.BlockSpec((None, 1, 2 * CHUNK_STATES), lambda c: (c, 0, 0)),
         pl.BlockSpec((1, 128), lambda c: (0, c))],
        [pl.BlockSpec((SEQ, 128), lambda c: (0, c)), pl.BlockSpec((SEQ, 2 * CHUNK_STATES), lambda c: (0, c))],
        [_sds((SEQ, SSM_WIDTH), F32), _sds((SEQ, SSM_CHUNKS * 2 * CHUNK_STATES), BF16)],
        [pltpu.VMEM((SEQ, 2 * CHUNK_STATES), F32)],
        [proj, bmat, cmat, a_chunks, d_skip], ride)


def _ssm_bwd(dys, proj, h, bmat, cmat, a_chunks, d_skip, dproj, ride=None):
    def body(dy_ref, u_ref, states_ref, b_ref, c_ref, a_ref, d_ref, dproj_in, du_ref, db_ref, dc_ref, da_ref, dd_ref,
             g_ref, h_ref):
        del dproj_in
        dsum = jnp.zeros((1, 128), F32)
        dcm = jnp.zeros((2 * CHUNK_STATES, 128), F32)
        for i in range(SEQ // MM_ROWS):
            rows = pl.ds(i * MM_ROWS, MM_ROWS)
            h_ref[rows, :] = states_ref[rows, :].astype(F32)
            dy = dy_ref[rows, :]
            g_ref[rows, :] = _dot_nt(dy.astype(BF16), c_ref[...])
            dsum += jnp.sum(dy * u_ref[rows, :], axis=0, keepdims=True)
            dcm += _dot_tn(states_ref[rows, :], dy.astype(BF16))
        dd_ref[...] = dsum
        dc_ref[...] = dcm
        steps, carry_in, _ = _scan_consts(a_ref, conj=True, reverse=True)
        first_row = lax.broadcasted_iota(jnp.int32, (SCAN_ROWS, CHUNK_STATES), 0) == 0
        n_tiles = SEQ // SCAN_ROWS

        def tile(j, carry):
            k = n_tiles - 1 - j
            rows = pl.ds(pl.multiple_of(k * SCAN_ROWS, SCAN_ROWS), SCAN_ROWS)
            before = pl.ds(pl.multiple_of(jnp.maximum(k - 1, 0) * SCAN_ROWS, SCAN_ROWS), SCAN_ROWS)
            gr, gi = _scan_tile(g_ref[rows, :CHUNK_STATES], g_ref[rows, CHUNK_STATES:], carry[0], carry[1], steps, carry_in, True)
            g_ref[rows, :CHUNK_STATES] = gr
            g_ref[rows, CHUNK_STATES:] = gi
            has_before = jnp.where(k > 0, 1.0, 0.0)
            hr = jnp.where(first_row, pltpu.roll(h_ref[before, :CHUNK_STATES], 1, axis=0) * has_before,
                           pltpu.roll(h_ref[rows, :CHUNK_STATES], 1, axis=0))
            hi = jnp.where(first_row, pltpu.roll(h_ref[before, CHUNK_STATES:], 1, axis=0) * has_before,
                           pltpu.roll(h_ref[rows, CHUNK_STATES:], 1, axis=0))
            return gr, gi, carry[2] + hr * gr + hi * gi, carry[3] + hr * gi - hi * gr

        zero = jnp.zeros((SCAN_ROWS, CHUNK_STATES), F32)
        _, _, sar, sai = lax.fori_loop(0, n_tiles, tile, (zero, zero, zero, zero), unroll=4)
        da_ref[:, :CHUNK_STATES] = jnp.sum(sar, axis=0, keepdims=True)
        da_ref[:, CHUNK_STATES:] = jnp.sum(sai, axis=0, keepdims=True)
        dbm = jnp.zeros((128, 2 * CHUNK_STATES), F32)
        for i in range(SEQ // MM_ROWS):
            rows = pl.ds(i * MM_ROWS, MM_ROWS)
            g = g_ref[rows, :].astype(BF16)
            du_ref[rows, :] = (_dot_nt(g, b_ref[...]) + d_ref[...] * dy_ref[rows, :]).astype(BF16)
            dbm += _dot_tn(u_ref[rows, :].astype(BF16), g)
        db_ref[...] = dbm

    chunk_col = pl.BlockSpec((SEQ, 128), lambda c: (0, c))
    return _call(
        body, "ssm_bwd", (SSM_CHUNKS,),
        [chunk_col,
         pl.BlockSpec((SEQ, 128), lambda c: (0, U_COL + c)),
         pl.BlockSpec((SEQ, 2 * CHUNK_STATES), lambda c: (0, c)),
         pl.BlockSpec((None, 128, 2 * CHUNK_STATES), lambda c: (c, 0, 0)),
         pl.BlockSpec((None, 2 * CHUNK_STATES, 128), lambda c: (c, 0, 0)),
         pl.BlockSpec((None, 1, 2 * CHUNK_STATES), lambda c: (c, 0, 0)),
         pl.BlockSpec((1, 128), lambda c: (0, c)), ANY],
        [pl.BlockSpec((SEQ, 128), lambda c: (0, U_COL + c)),
         pl.BlockSpec((None, 128, 2 * CHUNK_STATES), lambda c: (c, 0, 0)),
         pl.BlockSpec((None, 2 * CHUNK_STATES, 128), lambda c: (c, 0, 0)),
         pl.BlockSpec((None, 1, 2 * CHUNK_STATES), lambda c: (c, 0, 0)),
         pl.BlockSpec((1, 128), lambda c: (0, c))],
        [_sds((SEQ, IN_WIDTH), BF16), _sds((SSM_CHUNKS, 128, 2 * CHUNK_STATES), F32),
         _sds((SSM_CHUNKS, 2 * CHUNK_STATES, 128), F32), _sds((SSM_CHUNKS, 1, 2 * CHUNK_STATES), F32), _sds((1, SSM_WIDTH), F32)],
        [pltpu.VMEM((SEQ, 2 * CHUNK_STATES), F32)] * 2, [dys, proj, h, bmat, cmat, a_chunks, d_skip, dproj], ride, aliases={7: 0})


def _ssm_tables(abar_re, abar_im, bbar_re_t, bbar_im_t, c_re, c_im):
    bmat = _block_diag(bbar_re_t, bbar_im_t, 1.0, True).astype(BF16)
    cmat = _block_diag(c_re, c_im, -1.0, False).astype(BF16)
    a_chunks = jnp.concatenate([abar_re.reshape(SSM_CHUNKS, 1, CHUNK_STATES), abar_im.reshape(SSM_CHUNKS, 1, CHUNK_STATES)], axis=2)
    return bmat, cmat, a_chunks


GL_COL = (3 * QKV_WIDTH + SSM_WIDTH) // D_MODEL
GELU_C = math.sqrt(2.0 / math.pi)
GELU_A = 0.044715


def _sds(shape, dtype):
    return jax.ShapeDtypeStruct(shape, dtype)


def _gelu(x):
    t = jnp.tanh(GELU_C * (x + GELU_A * x * x * x))
    return 0.5 * x * (1.0 + t), t


def _gelu_grad(x, t):
    return 0.5 * (1.0 + t) + 0.5 * x * (1.0 - t * t) * GELU_C * (1.0 + 3.0 * GELU_A * x * x)


def _layer_norm(r, g, b):
    mu = jnp.mean(r, axis=-1, keepdims=True)
    xc = r - mu
    rstd = lax.rsqrt(jnp.mean(xc * xc, axis=-1, keepdims=True) + LN_EPS)
    xhat = xc * rstd
    return xhat * g + b, xhat, rstd


def _layer_norm_bwd(dy, xhat, rstd, g):
    dxhat = dy * g
    m1 = jnp.mean(dxhat, axis=-1, keepdims=True)
    m2 = jnp.mean(dxhat * xhat, axis=-1, keepdims=True)
    return rstd * (dxhat - m1 - xhat * m2)


def _proj(x, w_in, ride=None):
    tm, tn = 1024, 1792

    def body(x_ref, w_ref, o_ref):
        o_ref[...] = _dot(x_ref[...].astype(BF16), _side_by_side(w_ref))

    return _call(
        body, "proj", (SEQ // tm, IN_WIDTH // tn),
        [pl.BlockSpec((tm, D_MODEL), lambda i, j: (i, 0)), pl.BlockSpec((2, D_MODEL, tn // 2), lambda i, j: (j, 0, 0))],
        [pl.BlockSpec((tm, tn), lambda i, j: (i, j))], [_sds((SEQ, IN_WIDTH), F32)], [], [x, w_in], ride)


def _row_spec(tm, width, col=0):
    return pl.BlockSpec((tm, width), lambda i, col=col: (i, col))


def _full_spec(shape):
    return pl.BlockSpec(shape, lambda i: (0,) * len(shape))


def _weight_spec(shape):
    return pl.BlockSpec(shape, lambda i: (0,) * len(shape), pipeline_mode=pl.Buffered(1))


def _mixer_out(attn, ys, proj, x, w_ab, w_sb, w_glu, w_out, b_gate, ln_g, ln_b, ride=None):
    tm = 512

    def body(attn_ref, ys_ref, gl0_ref, gl1_ref, x_ref, wab_ref, wsb_ref, wglu_ref, wout_ref, bg_ref, g_ref, b_ref,
             h_ref, xhat_ref, rstd_ref, glu_ref, ya_ref, yssm_ref):
        gy, _ = _gelu(ys_ref[...])
        glu = _dot(gy.astype(BF16), _side_by_side(wglu_ref))
        glu_ref[...] = glu.astype(BF16)
        y_s = glu[:, :SSM_WIDTH] * jax.nn.sigmoid(glu[:, SSM_WIDTH:])
        y_ssm = _dot(y_s.astype(BF16), _side_by_side(wsb_ref))
        y_attn = _dot(attn_ref[...].astype(BF16), _side_by_side(wab_ref))
        ya_ref[...] = y_attn.astype(BF16)
        yssm_ref[...] = y_ssm.astype(BF16)
        g0 = jax.nn.sigmoid(gl0_ref[...] + _side_by_side(bg_ref, 0))
        g1 = jax.nn.sigmoid(gl1_ref[...] + _side_by_side(bg_ref, 1))
        mixed = g0 * y_attn + g1 * y_ssm
        r1 = DN_ALPHA * x_ref[...] + _dot(mixed.astype(BF16), wout_ref[...])
        h, xhat, rstd = _layer_norm(r1, g_ref[...], b_ref[...])
        h_ref[...] = h
        xhat_ref[...] = xhat
        rstd_ref[...] = jnp.broadcast_to(rstd, (tm, 128))

    wide = _sds((SEQ, D_MODEL), F32)
    return _call(
        body, "mixer_out", (SEQ // tm,),
        [_row_spec(tm, ATTN_WIDTH), _row_spec(tm, SSM_WIDTH), _row_spec(tm, D_MODEL, GL_COL), _row_spec(tm, D_MODEL, GL_COL + 1),
         _row_spec(tm, D_MODEL), _weight_spec((N_DEV, ATTN_WIDTH, 128)), _weight_spec((N_DEV, SSM_WIDTH, 128)),
         _weight_spec((N_DEV, SSM_WIDTH, 128)), _weight_spec((D_MODEL, D_MODEL)), _full_spec((N_DEV, 2, 128)),
         _full_spec((1, D_MODEL)), _full_spec((1, D_MODEL))],
        [_row_spec(tm, D_MODEL), _row_spec(tm, D_MODEL), _row_spec(tm, 128), _row_spec(tm, D_MODEL),
         _row_spec(tm, D_MODEL), _row_spec(tm, D_MODEL)],
        [wide, wide, _sds((SEQ, 128), F32)] + [_sds((SEQ, D_MODEL), BF16)] * 3, [],
        [attn, ys, proj, proj, x, w_ab, w_sb, w_glu, w_out, b_gate, ln_g, ln_b], ride)


def _ff_up(h, w_gate, w_up, ride=None):
    tm, tn = 1024, 768

    def body(h_ref, wg_ref, wu_ref, a_ref, b_ref, f_ref):
        hb = h_ref[...].astype(BF16)
        a, b = _dot(hb, _side_by_side(wg_ref)), _dot(hb, _side_by_side(wu_ref))
        a_ref[...] = a.astype(BF16)
        b_ref[...] = b.astype(BF16)
        f_ref[...] = (a * jax.nn.sigmoid(a) * b).astype(BF16)

    tile = pl.BlockSpec((tm, tn), lambda i, j: (i, j))
    wtile = pl.BlockSpec((tn // FF_PAD, D_MODEL, FF_PAD), lambda i, j: (j, 0, 0))
    out = _sds((SEQ, D_FF_PAD), BF16)
    return _call(body, "ff_up", (SEQ // tm, D_FF_PAD // tn), [pl.BlockSpec((tm, D_MODEL), lambda i, j: (i, 0)), wtile, wtile],
                 [tile, tile, tile], [out, out, out], [], [h, w_gate, w_up], ride)


def _ff_down_loss(f, w_down, h, target, ln_g, ln_b):
    tm = 512

    def body(f_ref, w_ref, h_ref, t_ref, g_ref, b_ref, dr_ref, dg_ref, db_ref, loss_ref):
        @pl.when(pl.program_id(0) == 0)
        def _():
            dg_ref[...] = jnp.zeros_like(dg_ref)
            db_ref[...] = jnp.zeros_like(db_ref)
            loss_ref[...] = jnp.zeros_like(loss_ref)

        r2 = DN_ALPHA * h_ref[...] + _dot(f_ref[...], w_ref[...])
        g = g_ref[...]
        out, xhat, rstd = _layer_norm(r2, g, b_ref[...])
        err = out - t_ref[...]
        loss_ref[...] += 0.5 * jnp.sum(jnp.mean(err * err, axis=-1, keepdims=True), axis=0, keepdims=True)
        dout = err * (1.0 / D_MODEL)
        dg_ref[...] += jnp.sum(dout * xhat, axis=0, keepdims=True)
        db_ref[...] += jnp.sum(dout, axis=0, keepdims=True)
        dr_ref[...] = _layer_norm_bwd(dout, xhat, rstd, g)

    vec = _sds((1, D_MODEL), F32)
    return _pallas_call(
        body, name="ff_down_loss", grid=(SEQ // tm,),
        in_specs=[_row_spec(tm, D_FF_PAD), _weight_spec((D_FF_PAD, D_MODEL)), _row_spec(tm, D_MODEL), _row_spec(tm, D_MODEL),
                  _full_spec((1, D_MODEL)), _full_spec((1, D_MODEL))],
        out_specs=(_row_spec(tm, D_MODEL), _full_spec((1, D_MODEL)), _full_spec((1, D_MODEL)), _full_spec((1, 128))),
        out_shape=(_sds((SEQ, D_MODEL), F32), vec, vec, _sds((1, 128), F32)),
        compiler_params=_cparams(dimension_semantics=("arbitrary",)),
    )(f, w_down, h, target, ln_g, ln_b)


def _ff_down_bwd(dr2, w_down, a, b):
    tm, tn = 1024, 768

    def body(dr_ref, w_ref, a_ref, b_ref, da_ref, db_ref):
        df = _dot_nt(dr_ref[...].astype(BF16), w_ref[...])
        av, bv = a_ref[...].astype(F32), b_ref[...].astype(F32)
        sg = jax.nn.sigmoid(av)
        da_ref[...] = (df * bv * sg * (1.0 + av * (1.0 - sg))).astype(BF16)
        db_ref[...] = (df * av * sg).astype(BF16)

    tile = pl.BlockSpec((tm, tn), lambda i, j: (i, j))
    out = _sds((SEQ, D_FF_PAD), BF16)
    return _pallas_call(
        body, name="ff_down_bwd", grid=(SEQ // tm, D_FF_PAD // tn),
        in_specs=[pl.BlockSpec((tm, D_MODEL), lambda i, j: (i, 0)), pl.BlockSpec((tn, D_MODEL), lambda i, j: (j, 0)), tile, tile],
        out_specs=(tile, tile), out_shape=(out, out),
        compiler_params=_cparams(dimension_semantics=("arbitrary", "arbitrary")),
    )(dr2, w_down, a, b)


def _ff_up_bwd(da, db, w_gate, w_up, dr2, xhat1, rstd1, ln_g, ride=None):
    tm, tk = 1024, 768
    nk = D_FF_PAD // tk

    def body(da_ref, db_ref, wg_ref, wu_ref, dr2_ref, xhat_ref, rstd_ref, g_ref, dr1_ref, dg_ref, dbias_ref, acc):
        i, k = pl.program_id(0), pl.program_id(1)

        @pl.when(jnp.logical_and(i == 0, k == 0))
        def _():
            dg_ref[...] = jnp.zeros_like(dg_ref)
            dbias_ref[...] = jnp.zeros_like(dbias_ref)

        part = _dot_nt(da_ref[...], _side_by_side(wg_ref)) + _dot_nt(db_ref[...], _side_by_side(wu_ref))

        @pl.when(k == 0)
        def _():
            acc[...] = part

        @pl.when(k > 0)
        def _():
            acc[...] += part

        @pl.when(k == nk - 1)
        def _():
            dh = DN_ALPHA * dr2_ref[...] + acc[...]
            xhat = xhat_ref[...]
            dg_ref[...] += jnp.sum(dh * xhat, axis=0, keepdims=True)
            dbias_ref[...] += jnp.sum(dh, axis=0, keepdims=True)
            rstd = jnp.max(rstd_ref[...], axis=1, keepdims=True)
            dr1_ref[...] = _layer_norm_bwd(dh, xhat, rstd, g_ref[...])

    hid = pl.BlockSpec((tm, tk), lambda i, k: (i, k))
    wtile = pl.BlockSpec((tk // FF_PAD, D_MODEL, FF_PAD), lambda i, k: (k, 0, 0))
    row = pl.BlockSpec((tm, D_MODEL), lambda i, k: (i, 0))
    vec = pl.BlockSpec((1, D_MODEL), lambda i, k: (0, 0))
    return _call(
        body, "ff_up_bwd", (SEQ // tm, nk),
        [hid, hid, wtile, wtile, row, row, pl.BlockSpec((tm, 128), lambda i, k: (i, 0)), vec],
        [row, vec, vec], [_sds((SEQ, D_MODEL), F32), _sds((1, D_MODEL), F32), _sds((1, D_MODEL), F32)],
        [pltpu.VMEM((tm, D_MODEL), F32)], [da, db, w_gate, w_up, dr2, xhat1, rstd1, ln_g], ride)


def _mixer_bwd(dr1, proj, y_attn, y_ssm, glu, ys, w_ab, w_sb, w_glu, w_out, b_gate):
    tm = 256

    def body(dr1_ref, gl0_ref, gl1_ref, ya_ref, yssm_ref, glu_ref, ys_ref, wab_ref, wsb_ref, wglu_ref, wout_ref, bg_ref,
             dya_ref, dyssm_ref, dgl_ref, dattn_ref, dglu_ref, dys_ref, mixed_ref, ysb_ref, gy_ref, dbg_ref, stage, copied):
        @pl.when(pl.program_id(0) == 0)
        def _():
            dbg_ref[...] = jnp.zeros_like(dbg_ref)

        dmixed = _dot_nt(dr1_ref[...].astype(BF16), wout_ref[...])
        g0 = jax.nn.sigmoid(gl0_ref[...] + _side_by_side(bg_ref, 0))
        g1 = jax.nn.sigmoid(gl1_ref[...] + _side_by_side(bg_ref, 1))
        y_attn, y_ssm = ya_ref[...].astype(F32), yssm_ref[...].astype(F32)
        mixed_ref[...] = (g0 * y_attn + g1 * y_ssm).astype(BF16)
        dya = (dmixed * g0).astype(BF16)
        dyssm = (dmixed * g1).astype(BF16)
        dya_ref[...] = dya
        dyssm_ref[...] = dyssm
        dgl0 = dmixed * y_attn * g0 * (1.0 - g0)
        dgl1 = dmixed * y_ssm * g1 * (1.0 - g1)
        i, last = pl.program_id(0), SEQ // tm - 1
        slot = i & 1

        def copy_out(buffer, tile):
            window = dgl_ref.at[pl.ds(pl.multiple_of(tile * tm, tm), tm), pl.ds(GL_COL * D_MODEL, 2 * D_MODEL)]
            return pltpu.make_async_copy(stage.at[buffer], window, copied.at[buffer])

        @pl.when(i >= 2)
        def _():
            copy_out(slot, i - 2).wait()

        stage[slot, :, :D_MODEL] = dgl0.astype(BF16)
        stage[slot, :, D_MODEL:] = dgl1.astype(BF16)
        copy_out(slot, i).start()

        @pl.when(i == last)
        def _():
            copy_out(1 - slot, i - 1).wait()
            copy_out(slot, i).wait()
        dbg_ref[:, :D_MODEL] += jnp.sum(dgl0, axis=0, keepdims=True)
        dbg_ref[:, D_MODEL:] += jnp.sum(dgl1, axis=0, keepdims=True)
        dattn_ref[...] = _dot_nt(dya, _side_by_side(wab_ref))
        dy_s = _dot_nt(dyssm, _side_by_side(wsb_ref))
        glu = glu_ref[...].astype(F32)
        glu1, sg = glu[:, :SSM_WIDTH], jax.nn.sigmoid(glu[:, SSM_WIDTH:])
        ysb_ref[...] = (glu1 * sg).astype(BF16)
        dglu1 = (dy_s * sg).astype(BF16)
        dglu2 = (dy_s * glu1 * sg * (1.0 - sg)).astype(BF16)
        dglu_ref[:, :SSM_WIDTH] = dglu1
        dglu_ref[:, SSM_WIDTH:] = dglu2
        dgy = _dot_nt(jnp.concatenate([dglu1, dglu2], axis=1), _side_by_side(wglu_ref))
        ys = ys_ref[...]
        gy, t = _gelu(ys)
        gy_ref[...] = gy.astype(BF16)
        dys_ref[...] = dgy * _gelu_grad(ys, t)

    wide_b, half_b = _sds((SEQ, D_MODEL), BF16), _sds((SEQ, SSM_WIDTH), BF16)
    half_f = _sds((SEQ, SSM_WIDTH), F32)
    return _pallas_call(
        body, name="mixer_bwd", grid=(SEQ // tm,),
        in_specs=[_row_spec(tm, D_MODEL), _row_spec(tm, D_MODEL, GL_COL), _row_spec(tm, D_MODEL, GL_COL + 1), _row_spec(tm, D_MODEL),
                  _row_spec(tm, D_MODEL), _row_spec(tm, D_MODEL), _row_spec(tm, SSM_WIDTH), _full_spec((N_DEV, ATTN_WIDTH, 128)),
                  _full_spec((N_DEV, SSM_WIDTH, 128)), _full_spec((N_DEV, SSM_WIDTH, 128)), _full_spec((D_MODEL, D_MODEL)),
                  _full_spec((N_DEV, 2, 128))],
        out_specs=(_row_spec(tm, D_MODEL), _row_spec(tm, D_MODEL), ANY, _row_spec(tm, ATTN_WIDTH),
                   _row_spec(tm, D_MODEL), _row_spec(tm, SSM_WIDTH), _row_spec(tm, D_MODEL), _row_spec(tm, SSM_WIDTH),
                   _row_spec(tm, SSM_WIDTH), _full_spec((1, 2 * D_MODEL))),
        out_shape=(wide_b, wide_b, _sds((SEQ, IN_WIDTH), BF16), half_f, wide_b, half_f, wide_b, half_b, half_b,
                   _sds((1, 2 * D_MODEL), F32)),
        scratch_shapes=[pltpu.VMEM((2, tm, 2 * D_MODEL), BF16), pltpu.SemaphoreType.DMA((2,))],
        compiler_params=_cparams(dimension_semantics=("arbitrary",)),
    )(dr1, proj, proj, y_attn, y_ssm, glu, ys, w_ab, w_sb, w_glu, w_out, b_gate)


def _grad_x(dproj, w_in, dr1, ride=None):
    tm, tk = 1024, 1792
    nk = IN_WIDTH // tk

    def body(dp_ref, w_ref, dr1_ref, o_ref, acc):
        k = pl.program_id(1)
        part = _dot_nt(dp_ref[...], _side_by_side(w_ref))

        @pl.when(k == 0)
        def _():
            acc[...] = part

        @pl.when(k > 0)
        def _():
            acc[...] += part

        @pl.when(k == nk - 1)
        def _():
            o_ref[...] = DN_ALPHA * dr1_ref[...] + acc[...]

    row = pl.BlockSpec((tm, D_MODEL), lambda i, k: (i, 0))
    return _call(
        body, "grad_x", (SEQ // tm, nk),
        [pl.BlockSpec((tm, tk), lambda i, k: (i, k)), pl.BlockSpec((2, D_MODEL, tk // 2), lambda i, k: (k, 0, 0)), row],
        [row], [_sds((SEQ, D_MODEL), F32)], [pltpu.VMEM((tm, D_MODEL), F32)], [dproj, w_in, dr1], ride)


def _weight_grad(a, b, name, shard_cols=None, ride=None):
    k, n = a.shape[1], b.shape[1]
    tk = min(k, 512) if shard_cols else k // N_DEV
    tn = n // 4 if shard_cols else min(n, 1024)

    def body(a_ref, b_ref, o_ref):
        grad = _dot_tn(a_ref[...].astype(BF16), b_ref[...].astype(BF16))
        if shard_cols:
            o_ref[0] = grad[:, :shard_cols].astype(BF16)
            o_ref[1] = grad[:, shard_cols:].astype(BF16)
        else:
            o_ref[...] = grad.astype(BF16)

    if shard_cols:
        out_spec = pl.BlockSpec((2, None, tk, shard_cols), lambda kk, j: (0, j, kk, 0))
        out_shape = _sds((2, 4, k, shard_cols), BF16)
    else:
        out_spec = pl.BlockSpec((None, None, tk, tn), lambda kk, j: (kk % 2, kk // 2, 0, j))
        out_shape = _sds((2, 4, tk, n), BF16)
    out = _call(body, name, (k // tk, n // tn),
                [pl.BlockSpec((SEQ, tk), lambda kk, j: (0, kk)), pl.BlockSpec((SEQ, tn), lambda kk, j: (0, j))],
                [out_spec], [out_shape], [], [a, b], ride)
    return out[0] if ride is None else out


MESH = pl.DeviceIdType.MESH
ANY = pl.BlockSpec(memory_space=pl.ANY)


def _place():
    return lax.axis_index("x"), lax.axis_index("y"), lax.axis_index("c")


def _other_chips(x, y):
    return [(1 - x, y), (x, 1 - y), (1 - x, 1 - y)]


class _Ride:
    def __init__(self, operands, results, aliases, sems, start, wait):
        self.operands, self.results, self.aliases, self.sems = list(operands), list(results), dict(aliases), list(sems)
        self.start, self.wait = start, wait

    def __add__(self, other):
        n_in, n_out, n_sem = len(self.operands), len(self.results), len(self.sems)

        def both(which):
            def run(ins, outs, sems):
                getattr(self, which)(ins[:n_in], outs[:n_out], sems[:n_sem])
                getattr(other, which)(ins[n_in:], outs[n_out:], sems[n_sem:])
            return run

        aliases = {**self.aliases, **{n_in + i: n_out + j for i, j in other.aliases.items()}}
        return _Ride(self.operands + other.operands, self.results + other.results, aliases, self.sems + other.sems,
                     both("start"), both("wait"))


def _call(body, name, grid, in_specs, out_specs, out_shape, scratch_shapes, operands, ride=None, aliases=None):
    in_specs, out_specs, out_shape = list(in_specs), list(out_specs), list(out_shape)
    scratch_shapes, operands, aliases = list(scratch_shapes), list(operands), dict(aliases or {})
    kernel_body = body
    if ride is not None:
        n_in, n_out, n_scr, r_in, r_out = len(in_specs), len(out_specs), len(scratch_shapes), len(ride.operands), len(ride.results)

        def kernel_body(*refs):
            out0, scr0 = n_in + r_in, n_in + r_in + n_out + r_out
            ride_refs = (refs[n_in:out0], refs[out0 + n_out:scr0], refs[scr0 + n_scr:])
            ids = [pl.program_id(i) for i in range(len(grid))]
            first = functools.reduce(jnp.logical_and, [i == 0 for i in ids])
            last = functools.reduce(jnp.logical_and, [i == g - 1 for i, g in zip(ids, grid)])

            @pl.when(first)
            def _():
                ride.start(*ride_refs)

            body(*refs[:n_in], *refs[out0:out0 + n_out], *refs[scr0:scr0 + n_scr])

            @pl.when(last)
            def _():
                ride.wait(*ride_refs)

        aliases.update({n_in + i: n_out + j for i, j in ride.aliases.items()})
        in_specs += [ANY] * r_in
        out_specs += [ANY] * r_out
        out_shape += ride.results
        scratch_shapes += ride.sems
        operands += ride.operands
    return _pallas_call(
        kernel_body, name=name, grid=grid, in_specs=in_specs, out_specs=out_specs, out_shape=out_shape,
        scratch_shapes=scratch_shapes, input_output_aliases=aliases,
        compiler_params=_cparams(dimension_semantics=("arbitrary",) * len(grid)),
    )(*operands)


def _after(*arrays):
    return _Ride(arrays, [], {}, [], lambda *refs: None, lambda *refs: None)


def _gather_first_level(shards):
    n = len(shards)

    def copies(ins, outs, sems, landed):
        send_sems, recv_sems, local_sems = sems
        x, y, c = _place()
        peers = [(x, y, 1 - c)] + [(px, py, c) for px, py in _other_chips(x, y)]

        def row(peer):
            return 4 * x + 2 * y + c if not landed else 4 * peer[0] + 2 * peer[1] + peer[2]

        local = [pltpu.make_async_copy(ins[a], outs[a].at[4 * x + 2 * y + c], local_sems.at[a]) for a in range(n)]
        remote = [pltpu.make_async_remote_copy(
            src_ref=ins[a], dst_ref=outs[a].at[row(peer)], send_sem=send_sems.at[a, k], recv_sem=recv_sems.at[a, k],
            device_id=peer, device_id_type=MESH) for a in range(n) for k, peer in enumerate(peers)]
        return local, remote

    def start(ins, outs, sems):
        local, remote = copies(ins, outs, sems, False)
        for cp in local + remote:
            cp.start()

    def wait(ins, outs, sems):
        local, sent = copies(ins, outs, sems, False)
        for cp in copies(ins, outs, sems, True)[1]:
            cp.wait_recv()
        for cp in sent:
            cp.wait_send()
        for cp in local:
            cp.wait()

    return _Ride(shards, [_sds((N_DEV,) + s.shape, s.dtype) for s in shards], {},
                 [pltpu.SemaphoreType.DMA((n, 4)), pltpu.SemaphoreType.DMA((n, 4)), pltpu.SemaphoreType.DMA((n,))], start, wait)


def _gather_second_level(buffers):
    n = len(buffers)

    def copies(outs, sems, core):
        send_sems, recv_sems = sems
        x, y, c = _place()
        return [pltpu.make_async_remote_copy(
            src_ref=outs[a].at[4 * px + 2 * py + core], dst_ref=outs[a].at[4 * px + 2 * py + core], send_sem=send_sems.at[a, j],
            recv_sem=recv_sems.at[a, j], device_id=(x, y, 1 - c), device_id_type=MESH)
            for a in range(n) for j, (px, py) in enumerate(_other_chips(x, y))]

    def start(ins, outs, sems):
        for cp in copies(outs, sems, lax.axis_index("c")):
            cp.start()

    def wait(ins, outs, sems):
        for cp in copies(outs, sems, 1 - lax.axis_index("c")):
            cp.wait_recv()
        for cp in copies(outs, sems, lax.axis_index("c")):
            cp.wait_send()

    return _Ride(buffers, [_sds(b.shape, b.dtype) for b in buffers], {i: i for i in range(n)},
                 [pltpu.SemaphoreType.DMA((n, 3)), pltpu.SemaphoreType.DMA((n, 3))], start, wait)


def _relayed_gather(shards):
    n = len(shards)
    buffers = [_sds((N_DEV,) + s.shape, s.dtype) for s in shards]
    dma = pltpu.SemaphoreType.DMA

    def remote(src, dst, send_sem, recv_sem, to):
        return pltpu.make_async_remote_copy(src_ref=src, dst_ref=dst, send_sem=send_sem, recv_sem=recv_sem,
                                            device_id=to, device_id_type=MESH)

    def row(px, py, pc):
        return 4 * px + 2 * py + pc

    def ride(operands, aliases, sems, copies):
        def start(ins, outs, sem_refs):
            local, sent = copies(ins, outs, sem_refs, False)
            for cp in local + sent:
                cp.start()

        def wait(ins, outs, sem_refs):
            local, sent = copies(ins, outs, sem_refs, False)
            for cp in copies(ins, outs, sem_refs, True)[1]:
                cp.wait_recv()
            for cp in sent:
                cp.wait_send()
            for cp in local:
                cp.wait()

        return _Ride(operands, buffers, aliases, sems, start, wait)

    def first(ins, outs, sems, landed):
        x, y, c = _place()
        peers = [(x, y, 1 - c), (1 - x, y, c), (x, 1 - y, c)]
        local = [pltpu.make_async_copy(ins[a], outs[a].at[row(x, y, c)], sems[2].at[a]) for a in range(n)]
        return local, [remote(ins[a], outs[a].at[row(*peer) if landed else row(x, y, c)], sems[0].at[a, k], sems[1].at[a, k], peer)
                       for a in range(n) for k, peer in enumerate(peers)]

    def second(ins, outs, sems, landed):
        x, y, c = _place()
        mine = 1 - c if landed else c
        copies = []
        for a in range(n):
            half = shards[a].shape[0] // 2
            over_x, over_y, diagonal = outs[a].at[row(1 - x, y, mine)], outs[a].at[row(x, 1 - y, mine)], outs[a].at[row(1 - x, 1 - y, c)]
            lower, upper = pl.ds(0, half), pl.ds(half, half)
            copies += [remote(over_x, over_x, sems[0].at[a, 0], sems[1].at[a, 0], (x, y, 1 - c)),
                       remote(over_y, over_y, sems[0].at[a, 1], sems[1].at[a, 1], (x, y, 1 - c))]
            if landed:
                copies += [remote(diagonal.at[lower], diagonal.at[lower], sems[0].at[a, 2], sems[1].at[a, 2], (1 - x, y, c)),
                           remote(diagonal.at[upper], diagonal.at[upper], sems[0].at[a, 3], sems[1].at[a, 3], (x, 1 - y, c))]
            else:
                copies += [remote(over_y.at[lower], over_y.at[lower], sems[0].at[a, 2], sems[1].at[a, 2], (1 - x, y, c)),
                           remote(over_x.at[upper], over_x.at[upper], sems[0].at[a, 3], sems[1].at[a, 3], (x, 1 - y, c))]
        return [], copies

    def third(ins, outs, sems, landed):
        x, y, c = _place()
        return [], [remote(outs[a].at[row(1 - x, 1 - y, 1 - c if landed else c)], outs[a].at[row(1 - x, 1 - y, 1 - c if landed else c)],
                           sems[0].at[a], sems[1].at[a], (x, y, 1 - c)) for a in range(n)]

    def later(copies, n_sems):
        return lambda partly: ride(partly, {i: i for i in range(n)}, [dma((n,) + n_sems), dma((n,) + n_sems)], copies)

    return ride(shards, {}, [dma((n, 3)), dma((n, 3)), dma((n,))], first), later(second, (4,)), later(third, ())


def _sibling_swap_ride(grads):
    n = len(grads)

    def copies(ins, outs, sems):
        x, y, c = _place()
        return [pltpu.make_async_remote_copy(
            src_ref=ins[a].at[1 - c], dst_ref=outs[a], send_sem=sems[0].at[a], recv_sem=sems[1].at[a],
            device_id=(x, y, 1 - c), device_id_type=MESH) for a in range(n)]

    def start(ins, outs, sems):
        for cp in copies(ins, outs, sems):
            cp.start()

    def wait(ins, outs, sems):
        for cp in copies(ins, outs, sems):
            cp.wait()

    return _Ride(grads, [_sds(g.shape[1:], g.dtype) for g in grads], {},
                 [pltpu.SemaphoreType.DMA((n,)), pltpu.SemaphoreType.DMA((n,))], start, wait)


def _chip_swap_ride(sums):
    n = len(sums)

    def copies(ins, outs, sems, landed):
        send_sems, recv_sems, local_sems = sems
        x, y, c = _place()
        mine = 2 * x + y
        local = [pltpu.make_async_copy(ins[a].at[mine], outs[a].at[mine], local_sems.at[a]) for a in range(n)]
        remote = [pltpu.make_async_remote_copy(
            src_ref=ins[a].at[2 * px + py], dst_ref=outs[a].at[2 * px + py if landed else mine], send_sem=send_sems.at[a, j],
            recv_sem=recv_sems.at[a, j], device_id=(px, py, c), device_id_type=MESH)
            for a in range(n) for j, (px, py) in enumerate(_other_chips(x, y))]
        return local, remote

    def start(ins, outs, sems):
        local, remote = copies(ins, outs, sems, False)
        for cp in local + remote:
            cp.start()

    def wait(ins, outs, sems):
        local, sent = copies(ins, outs, sems, False)
        for cp in copies(ins, outs, sems, True)[1]:
            cp.wait_recv()
        for cp in sent:
            cp.wait_send()
        for cp in local:
            cp.wait()

    return _Ride(sums, [_sds(s.shape, s.dtype) for s in sums], {},
                 [pltpu.SemaphoreType.DMA((n, 3)), pltpu.SemaphoreType.DMA((n, 3)), pltpu.SemaphoreType.DMA((n,))], start, wait)


def _send_buffers(shards, name):
    n = len(shards)

    def body(*refs):
        for (w, transposed, rows, cols), w_ref, o_ref in zip(shards, refs[:n], refs[n:]):
            if transposed:
                c, r = w.shape
                padded = jnp.concatenate([w_ref[...], jnp.zeros((cols - c, r), F32)], axis=0) if cols > c else w_ref[...]
                o_ref[...] = padded.T.astype(BF16)
            else:
                r, c = w.shape
                if (r, c) != (rows, cols):
                    o_ref[...] = jnp.zeros((rows, cols), BF16)
                o_ref[:r, :c] = w_ref[...].astype(BF16)

    return _pallas_call(body, name=name, out_shape=[_sds((rows, cols), BF16) for _, _, rows, cols in shards])(
        *[w for w, _, _, _ in shards])


def _all_gather(shards, name):
    n = len(shards)
    first, second, third = _relayed_gather(shards)
    levels = [first, second(shards), third(shards)]
    counts = [len(level.sems) for level in levels]

    def body(*refs):
        ins, outs, sems = refs[:n], refs[n:2 * n], refs[2 * n:]
        for i, level in enumerate(levels):
            mine = sems[sum(counts[:i]):sum(counts[:i + 1])]
            level.start(ins, outs, mine)
            level.wait(ins, outs, mine)

    return _pallas_call(
        body, name=name, in_specs=[ANY] * n, out_specs=[ANY] * n, out_shape=first.results,
        scratch_shapes=[s for level in levels for s in level.sems],
    )(*shards)


def _exchange(ride, name):
    n_in, n_out = len(ride.operands), len(ride.results)

    def body(*refs):
        ride.start(refs[:n_in], refs[n_in:n_in + n_out], refs[n_in + n_out:])
        ride.wait(refs[:n_in], refs[n_in:n_in + n_out], refs[n_in + n_out:])

    return _pallas_call(body, name=name, in_specs=[ANY] * n_in, out_specs=[ANY] * n_out, out_shape=ride.results,
                        scratch_shapes=ride.sems, input_output_aliases=ride.aliases)(*ride.operands)


HBM = pl.BlockSpec(memory_space=pltpu.HBM)
SEMAPHORES = pl.BlockSpec(memory_space=pltpu.SEMAPHORE)
IN_FLIGHT = pltpu.CompilerParams(has_side_effects=pltpu.SideEffectType.DATAFLOW_SIDE_EFFECTING)


def _chip_swap_copies(src_refs, land_refs, send_sems, recv_sems, landed):
    x, y, c = _place()
    return [pltpu.make_async_remote_copy(
        src_ref=src.at[2 * px + py], dst_ref=land.at[2 * px + py if landed else 2 * x + y], send_sem=send_sems.at[3 * a + j],
        recv_sem=recv_sems.at[3 * a + j], device_id=(px, py, c), device_id_type=MESH)
        for a, (src, land) in enumerate(zip(src_refs, land_refs)) for j, (px, py) in enumerate(_other_chips(x, y))]


def _chip_swap_start(sums, name):
    n = len(sums)

    def body(*refs):
        src_refs, land_refs, (send_sems, recv_sems), token = refs[:n], refs[n:2 * n], refs[2 * n:2 * n + 2], refs[-1]
        for cp in _chip_swap_copies(src_refs, land_refs, send_sems, recv_sems, False):
            cp.start()
        token[...] = jnp.zeros_like(token)

    kept = [pltpu.HBM(s.shape, s.dtype) for s in sums]
    out = _pallas_call(
        body, name=name,
        out_shape=[pltpu.SemaphoreType.DMA((3 * n,)), pltpu.SemaphoreType.DMA((3 * n,))] + kept + kept + [_sds((8, 128), F32)],
        in_specs=[HBM] * (2 * n), out_specs=[SEMAPHORES, SEMAPHORES] + [HBM] * (2 * n) + [pl.BlockSpec(memory_space=pltpu.VMEM)],
        input_output_aliases={i: 2 + i for i in range(2 * n)}, compiler_params=IN_FLIGHT,
    )(*[pltpu.with_memory_space_constraint(s, pltpu.HBM) for s in sums],
      *[pltpu.with_memory_space_constraint(lax.empty(s.shape, s.dtype), pltpu.HBM) for s in sums])
    return out[0], out[1], out[2:2 + n], out[2 + n:2 + 2 * n], out[-1]


def _chip_swap_wait(send_sems, recv_sems, sums, landings, after, name):
    n = len(sums)

    def body(*refs):
        src_refs, land_refs, (send_sems, recv_sems) = refs[:n], refs[n:2 * n], refs[2 * n:2 * n + 2]
        for cp in _chip_swap_copies(src_refs, land_refs, send_sems, recv_sems, False):
            cp.wait_send()
        for cp in _chip_swap_copies(src_refs, land_refs, send_sems, recv_sems, True):
            cp.wait_recv()

    out = _pallas_call(
        body, name=name, out_shape=[pltpu.HBM(s.shape, s.dtype) for s in list(sums) + list(landings)],
        in_specs=[HBM] * (2 * n) + [SEMAPHORES, SEMAPHORES] + [ANY] * len(after), out_specs=[HBM] * (2 * n),
        input_output_aliases={i: i for i in range(2 * n)}, compiler_params=IN_FLIGHT,
    )(*sums, *landings, send_sems, recv_sems, *after)
    return out[:n], out[n:]


def _pair_sums(gs, rs, core, name):
    n_arrays = len(gs)

    def body(core_ref, *refs):
        for g_ref, r_ref, o_ref in zip(refs[:n_arrays], refs[n_arrays:2 * n_arrays], refs[2 * n_arrays:]):
            o_ref[...] = (g_ref[...].astype(F32) + r_ref[...].astype(F32)).astype(o_ref.dtype)

    def own(g):
        return pl.BlockSpec((None, None) + g.shape[2:], lambda p, core_ref: (core_ref[0], p, 0, 0))

    def chip(g):
        return pl.BlockSpec((None,) + g.shape[2:], lambda p, core_ref: (p, 0, 0))

    return _pallas_call(
        body, name=name,
        grid_spec=pltpu.PrefetchScalarGridSpec(
            num_scalar_prefetch=1, grid=(4,), in_specs=[own(g) for g in gs] + [chip(g) for g in gs],
            out_specs=[chip(g) for g in gs]),
        out_shape=[_sds(g.shape[1:], g.dtype) for g in gs], compiler_params=_cparams(dimension_semantics=("arbitrary",)),
    )(core, *gs, *rs)


def _adamw_math(w, g, m, v):
    m = ADAM_B1 * m + (1.0 - ADAM_B1) * g
    v = ADAM_B2 * v + (1.0 - ADAM_B2) * (g * g)
    m_hat = m / (1.0 - ADAM_B1 ** ADAM_STEP)
    v_hat = v / (1.0 - ADAM_B2 ** ADAM_STEP)
    return -ADAM_LR * (m_hat / (jnp.sqrt(v_hat) + ADAM_EPS) + ADAM_WD * w), m, v


def _adamw_many(weights, name, ride=None):
    steps = 4
    in_specs, out_specs, out_shape, operands, tiles = [], [], [], [], []
    for w, m, v, parts, own, transposed in weights:
        _, pr, pc = parts.shape
        if transposed:
            c, r = w.shape
            tile = pl.BlockSpec((c, r // steps), lambda i: (0, i))
            part_tile = pl.BlockSpec((4, r // steps, pc), lambda i: (0, i, 0))
            tiles.append((c, r // steps))
        elif w.shape[0] % (8 * steps) == 0:
            r, c = w.shape
            tile = pl.BlockSpec((r // steps, c), lambda i: (i, 0))
            part_tile = pl.BlockSpec((4, r // steps, pc), lambda i: (0, i, 0))
            tiles.append((r // steps, c))
        else:
            tile = pl.BlockSpec(w.shape, lambda i: (0, 0))
            part_tile = pl.BlockSpec(parts.shape, lambda i: (0, 0, 0))
            tiles.append(w.shape)
        in_specs += [tile, tile, tile] + [part_tile] * (1 if own is None else 2)
        out_specs += [tile] * 4
        out_shape += [_sds(w.shape, F32)] * 4
        operands += [w, m, v, parts] + ([] if own is None else [own])
    n_in = len(operands)

    def body(*refs):
        ins, outs = list(refs[:n_in]), refs[n_in:]
        this_chip = 2 * lax.axis_index("x") + lax.axis_index("y")
        for k, (_, _, _, _, own, transposed) in enumerate(weights):
            w_ref, m_ref, v_ref, p_ref = ins[:4]
            own_ref = None if own is None else ins[4]
            del ins[:4 if own is None else 5]
            rows, cols = tiles[k]
            g = None
            for q in range(4):
                index = (q,) if transposed else (q, slice(0, rows), slice(0, cols))
                part = p_ref[index] if own is None else jnp.where(this_chip == q, own_ref[index], p_ref[index])
                g = part.astype(F32) if g is None else g + part.astype(F32)
            if transposed:
                g = g.T[:rows]
            g_out, d_out, m_out, v_out = outs[4 * k:4 * k + 4]
            g_out[...] = g
            d_out[...], m_out[...], v_out[...] = _adamw_math(w_ref[...], g, m_ref[...], v_ref[...])

    return _call(body, name, (steps,), in_specs, out_specs, out_shape, [], operands, ride)


SMALL = ("ssm_a_re", "ssm_a_im", "ssm_log_dt", "ssm_b_re", "ssm_b_im", "ssm_c_re", "ssm_c_im", "ssm_d",
         "ln1_g", "ln1_b", "ln2_g", "ln2_b")


def _pack_rows(arrays):
    rows = []
    for a in arrays:
        flat = a.reshape(-1)
        rows.append(jnp.pad(flat, (0, -flat.shape[0] % 128)).reshape(-1, 128))
    packed = jnp.concatenate(rows, axis=0)
    return jnp.pad(packed, ((0, -packed.shape[0] % 8), (0, 0)))


def _unpack_rows(packed, shapes):
    out, row = [], 0
    for shape in shapes:
        size = math.prod(shape)
        n_rows = -(-size // 128)
        out.append(packed[row:row + n_rows].reshape(-1)[:size].reshape(shape))
        row += n_rows
    return out


def _sum_devices(parts):
    def body(p_ref, o_ref):
        total = p_ref[0]
        for dev in range(1, N_DEV):
            total = total + p_ref[dev]
        o_ref[...] = total

    return _pallas_call(body, name="sum_devices", out_shape=_sds(parts.shape[1:], F32))(parts)


def _adamw_replicated(ws, ms, vs, gs):
    n = len(ws)

    def body(*refs):
        w_refs, m_refs, v_refs, g_refs, d_out, m_out, v_out = (refs[i * n:(i + 1) * n] for i in range(7))
        for i in range(n):
            d_out[i][...], m_out[i][...], v_out[i][...] = _adamw_math(w_refs[i][...], g_refs[i][...], m_refs[i][...], v_refs[i][...])

    out = _pallas_call(body, name="adamw_replicated", out_shape=[_sds(w.shape, F32) for w in ws] * 3,
                       compiler_params=_cparams())(*ws, *ms, *vs, *gs)
    return out[:n], out[n:2 * n], out[2 * n:]


def kernel(x, w_in, b_gate, w_attn_br, w_ssm_br, w_out, ssm_a_re, ssm_a_im, ssm_log_dt, ssm_b_re, ssm_b_im, ssm_c_re, ssm_c_im, ssm_d, w_glu, ln1_g, ln1_b, w_ff_gate, w_ff_up, w_ff_down, ln2_g, ln2_b, loss_target, m_w_in, m_b_gate, m_w_attn_br, m_w_ssm_br, m_w_out, m_ssm_a_re, m_ssm_a_im, m_ssm_log_dt, m_ssm_b_re, m_ssm_b_im, m_ssm_c_re, m_ssm_c_im, m_ssm_d, m_w_glu, m_ln1_g, m_ln1_b, m_w_ff_gate, m_w_ff_up, m_w_ff_down, m_ln2_g, m_ln2_b, v_w_in, v_b_gate, v_w_attn_br, v_w_ssm_br, v_w_out, v_ssm_a_re, v_ssm_a_im, v_ssm_log_dt, v_ssm_b_re, v_ssm_b_im, v_ssm_c_re, v_ssm_c_im, v_ssm_d, v_w_glu, v_ln1_g, v_ln1_b, v_w_ff_gate, v_w_ff_up, v_w_ff_down, v_ln2_g, v_ln2_b):
    given = dict(locals())
    x2, target = x[0], loss_target[0]
    core = lax.axis_index("c").astype(jnp.int32).reshape(1)

    sharded = ("w_in", "w_attn_br", "w_ssm_br", "w_glu", "w_ff_gate", "w_ff_up", "b_gate", "w_out", "w_ff_down")
    send_shape = dict(w_in=(D_MODEL, 896), w_attn_br=(ATTN_WIDTH, 128), w_ssm_br=(SSM_WIDTH, 128), w_glu=(SSM_WIDTH, 128),
                      w_out=(128, D_MODEL), w_ff_gate=(D_MODEL, FF_PAD), w_ff_up=(D_MODEL, FF_PAD), w_ff_down=(FF_PAD, D_MODEL))
    local = {k: given[k][0] for k in sharded}
    narrow = ("w_ff_gate", "w_ff_up")
    def to_send(k):
        return (local[k].T, True, *send_shape[k]) if k in narrow else (local[k], False, *send_shape[k])

    later = [k for k in sharded if k not in ("w_in", "b_gate")]
    sends = dict(zip(["w_in"] + later, _send_buffers([to_send("w_in")], "send_w_in")
                     + _send_buffers([to_send(k) for k in later], "send_weights")))
    sends["b_gate"] = local["b_gate"]
    mixer_weights = ("w_attn_br", "w_ssm_br", "w_glu", "b_gate", "w_out")
    ff_weights = ("w_ff_gate", "w_ff_up", "w_ff_down")
    wt = {}
    wt["w_in"], = _all_gather([sends["w_in"]], "gather_w_in")

    a_re, a_im, log_dt = ssm_a_re[0], ssm_a_im[0], ssm_log_dt[0].reshape(SSM_GROUPS, 1)
    b_re_t, b_im_t = ssm_b_re[0].transpose(0, 2, 1), ssm_b_im[0].transpose(0, 2, 1)
    abar_re, abar_im, e_re, e_im, bbar_re_t, bbar_im_t = _ssm_prep(a_re, a_im, log_dt, b_re_t, b_im_t)
    bmat, cmat, a_chunks = _ssm_tables(abar_re, abar_im, bbar_re_t, bbar_im_t, ssm_c_re[0], ssm_c_im[0])
    cos_t, sin_t = _rope_tables()

    big_mixer, ff_in = [k for k in mixer_weights if k != "b_gate"], ("w_ff_gate", "w_ff_up")
    n_mixer = len(big_mixer)
    mixer_1, mixer_2, mixer_3 = _relayed_gather([sends[k] for k in big_mixer])
    ff_in_1, ff_in_2, ff_in_3 = _relayed_gather([sends[k] for k in ff_in])
    ff_down_1, ff_down_2, ff_down_3 = _relayed_gather([sends["w_ff_down"]])
    proj, *landed = _proj(x2, wt["w_in"], mixer_1 + _gather_first_level([sends["b_gate"]]))
    mixer, bias = landed[:n_mixer], landed[n_mixer:]
    attn, lse, q_pm, k_pm, v_pm, *landed = _attn_fwd(proj, cos_t, sin_t,
                                                     mixer_2(mixer) + _gather_second_level(bias) + ff_in_1)
    mixer, b_gate_full, ff = landed[:n_mixer], landed[n_mixer], landed[n_mixer + 1:]
    ys, states, *landed = _ssm_fwd(proj, bmat, cmat, a_chunks, ssm_d, mixer_3(mixer) + ff_in_2(ff) + ff_down_1)
    wt.update(zip(big_mixer, landed[:n_mixer]))
    ff, ff_down = landed[n_mixer:n_mixer + 2], landed[n_mixer + 2:]
    wt["w_out"] = wt["w_out"].reshape(D_MODEL, D_MODEL)
    h, xhat1, rstd1, glu, y_attn, y_ssm, *landed = _mixer_out(
        attn, ys, proj, x2, wt["w_attn_br"], wt["w_ssm_br"], wt["w_glu"], wt["w_out"], b_gate_full, ln1_g, ln1_b,
        ff_in_3(ff) + ff_down_2(ff_down))
    wt.update(zip(ff_in, landed[:2]))
    ff_a, ff_b, ff_f, w_ff_down = _ff_up(h, wt["w_ff_gate"], wt["w_ff_up"], ff_down_3(landed[2:]))
    wt["w_ff_down"] = w_ff_down.reshape(D_FF_PAD, D_MODEL)
    dr2, d_ln2_g, d_ln2_b, loss_lanes = _ff_down_loss(ff_f, wt["w_ff_down"], h, target, ln2_g, ln2_b)

    def pair_sums(names, contrib, from_sibling):
        return _pair_sums([contrib[k] for k in names], from_sibling, core, "pair_sums_" + names[0])

    d_a, d_b = _ff_down_bwd(dr2, wt["w_ff_down"], ff_a, ff_b)
    contrib = dict(w_ff_gate=_weight_grad(h, d_a, "wgrad_w_ff_gate", FF_PAD),
                   w_ff_up=_weight_grad(h, d_b, "wgrad_w_ff_up", FF_PAD),
                   w_ff_down=_weight_grad(ff_f, dr2, "wgrad_w_ff_down"))
    dr1, d_ln1_g, d_ln1_b, *from_sibling = _ff_up_bwd(
        d_a, d_b, wt["w_ff_gate"], wt["w_ff_up"], dr2, xhat1, rstd1, ln1_g, _sibling_swap_ride([contrib[k] for k in ff_weights]))
    ff_sums = pair_sums(ff_weights, contrib, from_sibling)

    d_ya, d_yssm, d_proj, d_attn, d_glu, d_ys, mixed, y_s, gy, d_bg = _mixer_bwd(
        dr1, proj, y_attn, y_ssm, glu, ys, wt["w_attn_br"], wt["w_ssm_br"], wt["w_glu"], wt["w_out"], b_gate_full)
    contrib.update(w_attn_br=_weight_grad(attn, d_ya, "wgrad_w_attn_br", 128),
                   w_ssm_br=_weight_grad(y_s, d_yssm, "wgrad_w_ssm_br", 128),
                   w_glu=_weight_grad(gy, d_glu, "wgrad_w_glu", 128),
                   w_out=_weight_grad(mixed, dr1, "wgrad_w_out"),
                   b_gate=d_bg.reshape(2, 4, 2, 128).transpose(2, 1, 0, 3))
    d_proj, *landed = _attn_bwd(q_pm, k_pm, v_pm, cos_t, sin_t, attn, lse, d_attn, d_proj,
                                _chip_swap_ride(ff_sums) + _sibling_swap_ride([contrib[k] for k in mixer_weights]))
    parts, own_sums = dict(zip(ff_weights, landed[:len(ff_weights)])), {}
    mixer_sums = pair_sums(mixer_weights, contrib, landed[len(ff_weights):])
    d_proj, d_bmat, d_cmat, d_abar, d_skip, *landed = _ssm_bwd(d_ys, proj, states, bmat, cmat, a_chunks, ssm_d, d_proj,
                                                               _chip_swap_ride(mixer_sums))
    parts.update(zip(mixer_weights, landed))

    gbb_re_t, gbb_im_t = _block_diag_parts(d_bmat, True)
    gc_re, gc_im = _block_diag_parts(d_cmat, False)
    ga_re = d_abar[:, 0, :CHUNK_STATES].reshape(SSM_GROUPS, SSM_STATE)
    ga_im = d_abar[:, 0, CHUNK_STATES:].reshape(SSM_GROUPS, SSM_STATE)
    g_a_re, g_a_im, g_log_dt, g_b_re_t, g_b_im_t = _ssm_param_bwd(
        a_re, a_im, log_dt, b_re_t, b_im_t, abar_re, abar_im, e_re, e_im, ga_re, ga_im, gbb_re_t, gbb_im_t)
    mine = [g_a_re, g_a_im, g_log_dt, g_b_re_t, g_b_im_t, gc_re, -gc_im,
            d_skip, d_ln1_g, d_ln1_b, d_ln2_g, d_ln2_b]
    small_packed = _pack_rows(mine + [loss_lanes])

    contrib["w_in"], small_partly = _weight_grad(x2, d_proj, "wgrad_w_in", 896, _gather_first_level([small_packed]))
    from_sibling, every = _exchange(_sibling_swap_ride([contrib["w_in"]]) + _gather_second_level([small_partly]),
                                    "swap_w_in_with_sibling")
    w_in_sum, = pair_sums(["w_in"], contrib, [from_sibling])
    send_sems, recv_sems, w_in_sum, landing, token = _chip_swap_start([w_in_sum], "w_in_chip_swap_start")

    def adamw_of(k):
        taken = (lambda a: a.T) if k in narrow else (lambda a: a)
        return taken(local[k]), taken(given["m_" + k][0]), taken(given["v_" + k][0]), parts[k], own_sums.get(k), k in narrow

    others = [k for k in sharded if k != "w_in"]
    updated = _adamw_many([adamw_of(k) for k in others], "adamw_others", _after(token))
    grad_x, = _grad_x(d_proj, wt["w_in"], dr1, _after(token))

    def held(k, a):
        return a.transpose(0, 1, 3, 2) if k in ("ssm_b_re", "ssm_b_im") else a

    *small_grads, loss_sum = _unpack_rows(_sum_devices(every), [held(k, given[k]).shape for k in SMALL] + [(1, 128)])
    small = _adamw_replicated([held(k, given[k]) for k in SMALL], [held(k, given["m_" + k]) for k in SMALL],
                              [held(k, given["v_" + k]) for k in SMALL], small_grads)
    loss = loss_sum[0, 0]

    (own_sums["w_in"],), (parts["w_in"],) = _chip_swap_wait(
        send_sems, recv_sems, w_in_sum, landing, [grad_x, updated[0], small[0][0]], "w_in_chip_swap_wait")
    updated += _adamw_many([adamw_of("w_in")], "adamw_w_in")

    grads, deltas, new_m, new_v = {}, {}, {}, {}
    for i, k in enumerate(others + ["w_in"]):
        out = [o.T if k in narrow else o for o in updated[4 * i:4 * i + 4]]
        grads[k], deltas[k], new_m[k], new_v[k] = (o.reshape((1,) + local[k].shape) for o in out)
    for res, values in zip((grads, deltas, new_m, new_v), (small_grads,) + small):
        res.update((k, held(k, a)) for k, a in zip(SMALL, values))

    order = ("w_in", "b_gate", "w_attn_br", "w_ssm_br", "w_out", "ssm_a_re", "ssm_a_im", "ssm_log_dt", "ssm_b_re", "ssm_b_im",
             "ssm_c_re", "ssm_c_im", "ssm_d", "w_glu", "ln1_g", "ln1_b", "w_ff_gate", "w_ff_up", "w_ff_down", "ln2_g", "ln2_b")
    return (loss, grad_x[None], *[grads[k] for k in order], *[deltas[k] for k in order], *[new_m[k] for k in order],
            *[new_v[k] for k in order])
```

```python
import functools
import math

import jax
import jax.numpy as jnp
import numpy as np
from jax import lax
from jax.experimental import pallas as pl
from jax.experimental.pallas import tpu as pltpu

F32 = jnp.float32
BF16 = jnp.bfloat16

N_DEV = 8
SEQ = 2048
D_MODEL = 1024
HEAD_DIM = 64
ATTN_WIDTH = 512
QKV_WIDTH = 1536
SSM_WIDTH = 512
SSM_GROUPS = 32
SSM_GROUP = 16
SSM_STATE = 64
IN_WIDTH = 7168
D_FF = 2816
FF_SHARD = D_FF // N_DEV
FF_PAD = 384
D_FF_PAD = FF_PAD * N_DEV
DN_ALPHA = 2.0 ** 0.25
LN_EPS = 1e-5
NEG_INF = -1e30
ROPE_THETA = 10000.0
BLOCK = 128
GROUPS = ((1, 16), (4, 4), (16, 1))

ADAM_LR = 0.001
ADAM_B1 = 0.9
ADAM_B2 = 0.999
ADAM_EPS = 1e-08
ADAM_WD = 0.01
ADAM_STEP = 10

VMEM_LIMIT = 56 * 1024 * 1024


_pallas_call = pl.pallas_call


def _cparams(**kw):
    return pltpu.CompilerParams(vmem_limit_bytes=VMEM_LIMIT, **kw)


def _dot(a, b):
    return jnp.dot(a, b, preferred_element_type=F32)


def _dot_nt(a, b):
    return lax.dot_general(a, b, (((1,), (1,)), ((), ())), preferred_element_type=F32)


def _side_by_side(w_ref, row=None):
    rows = slice(None) if row is None else pl.ds(row, 1)
    return jnp.concatenate([w_ref[i, rows, :] for i in range(w_ref.shape[0])], axis=1)


def _dot_tn(a, b):
    return lax.dot_general(a, b, (((0,), (0,)), ((), ())), preferred_element_type=F32)


def _rope_tables():
    half = HEAD_DIM // 2
    inv_freq = np.float32(ROPE_THETA) ** (-np.arange(half, dtype=np.float32) / np.float32(half))
    ang = np.arange(SEQ, dtype=np.float32)[:, None] * inv_freq[None, :]
    cos, sin = np.cos(ang).astype(np.float32), np.sin(ang).astype(np.float32)
    tables = np.tile(cos, (1, 4)), np.tile(np.concatenate([-sin, sin], axis=1), (1, 2))

    def by_phase(t):
        return np.stack([t.reshape(SEQ // d, d, 128).transpose(1, 0, 2).reshape(SEQ, 128) for d, _ in GROUPS])

    return jnp.asarray(by_phase(tables[0])), jnp.asarray(by_phase(tables[1]))


def _swap_halves(x):
    lane = lax.broadcasted_iota(jnp.int32, x.shape, 1)
    return jnp.where((lane & 63) < 32, pltpu.roll(x, 96, axis=1), pltpu.roll(x, 32, axis=1))


def _group_rows(d, nb, r, i):
    src = pl.ds(i * BLOCK, BLOCK) if d == 1 else pl.ds(r + i * BLOCK * d, BLOCK, stride=d)
    return src, pl.ds((r * nb + i) * BLOCK, BLOCK)


def _attn_masks():
    a_idx = lax.broadcasted_iota(jnp.int32, (2 * BLOCK, 2 * BLOCK), 0) & (BLOCK - 1)
    c_idx = lax.broadcasted_iota(jnp.int32, (2 * BLOCK, 2 * BLOCK), 1)
    cur_ok = jnp.logical_and(c_idx >= BLOCK, c_idx - BLOCK <= a_idx)
    prev_ok = jnp.logical_and(c_idx < BLOCK, c_idx >= a_idx)
    lane = lax.broadcasted_iota(jnp.int32, (BLOCK, 128), 1)
    return cur_ok, prev_ok, lane < HEAD_DIM


def _stack_heads(t, head0):
    zero = jnp.zeros_like(t)
    return jnp.concatenate([jnp.where(head0, t, zero), jnp.where(head0, zero, t)], axis=0)


def _unstack_heads(t2, head0):
    return jnp.where(head0, t2[:BLOCK], t2[BLOCK:])


def _attn_fwd(proj, cos_t, sin_t, ride=None):
    def body(q0, q1, q2, k0, k1, k2, v0, v1, v2, cos_ref, sin_ref, attn_ref, lse_ref, qpm_ref, kpm_ref, vpm_ref,
             qs, ks, vs, os_, ms, ls, acc, mnat, lnat):
        cur_ok, prev_ok, head0 = _attn_masks()
        ks[:BLOCK, :] = jnp.zeros((BLOCK, 128), BF16)
        vs[:BLOCK, :] = jnp.zeros((BLOCK, 128), BF16)
        for g, (d, nb) in enumerate(GROUPS):
            q_ref, k_ref, v_ref = (q0, q1, q2)[g], (k0, k1, k2)[g], (v0, v1, v2)[g]
            for r in range(d):
                for i in range(nb):
                    src, dst = _group_rows(d, nb, r, i)
                    below = pl.ds(dst.start + BLOCK, BLOCK)
                    c, s = cos_ref[g, dst, :], sin_ref[g, dst, :]
                    q = q_ref[src, :]
                    k = k_ref[src, :]
                    qs[dst, :] = ((q * c + _swap_halves(q) * s) * 0.125).astype(BF16)
                    ks[below, :] = (k * c + _swap_halves(k) * s).astype(BF16)
                    vs[below, :] = v_ref[src, :].astype(BF16)
                    qpm_ref[g, dst, :], kpm_ref[g, dst, :], vpm_ref[g, dst, :] = qs[dst, :], ks[below, :], vs[below, :]

            def block(b, carry, nb=nb):
                has_prev = (b & (nb - 1)) > 0
                cur = pl.ds(pl.multiple_of(b * BLOCK, BLOCK), BLOCK)
                window = pl.ds(pl.multiple_of(b * BLOCK, BLOCK), 2 * BLOCK)
                valid = jnp.logical_or(cur_ok, jnp.logical_and(prev_ok, has_prev))
                s = jnp.where(valid, _dot_nt(_stack_heads(qs[cur, :], head0), ks[window, :]), NEG_INF)
                m = jnp.max(s, axis=1, keepdims=True)
                p = jnp.exp(s - m)
                os_[cur, :] = _unstack_heads(_dot(p.astype(BF16), vs[window, :]), head0)
                ms[cur, :] = _unstack_heads(m, head0)
                ls[cur, :] = _unstack_heads(jnp.sum(p, axis=1, keepdims=True), head0)
                return carry

            lax.fori_loop(0, SEQ // BLOCK, block, 0, unroll=16)

            for r in range(d):
                for i in range(nb):
                    src, dst = _group_rows(d, nb, r, i)
                    if g == 0:
                        acc[src, :], mnat[src, :], lnat[src, :] = os_[dst, :], ms[dst, :], ls[dst, :]
                    else:
                        m_old, m_g = mnat[src, :], ms[dst, :]
                        m_new = jnp.maximum(m_old, m_g)
                        a_old, a_g = jnp.exp(m_old - m_new), jnp.exp(m_g - m_new)
                        acc[src, :] = a_old * acc[src, :] + a_g * os_[dst, :]
                        lnat[src, :] = a_old * lnat[src, :] + a_g * ls[dst, :]
                        mnat[src, :] = m_new
        for i in range(SEQ // BLOCK):
            rows = pl.ds(i * BLOCK, BLOCK)
            l = lnat[rows, :]
            attn_ref[rows, :] = acc[rows, :] / l
            lse_ref[rows, :] = mnat[rows, :] + jnp.log(l)

    def col(base):
        return pl.BlockSpec((SEQ, 128), lambda hp, base=base: (0, base + hp))

    in_specs = [col(g * 4) for g in range(3)] + [col(12 + g * 4) for g in range(3)] + [col(24 + g * 4) for g in range(3)]
    table = pl.BlockSpec((3, SEQ, 128), lambda hp: (0, 0, 0), pipeline_mode=pl.Buffered(1))
    out = pl.BlockSpec((SEQ, 128), lambda hp: (0, hp))
    by_phase = pl.BlockSpec((3, SEQ, 128), lambda hp: (0, 0, hp))
    return _call(
        body, "attn_fwd", (4,), in_specs + [table, table], [out, out] + [by_phase] * 3,
        [_sds((SEQ, ATTN_WIDTH), F32), _sds((SEQ, ATTN_WIDTH), F32)] + [_sds((3, SEQ, ATTN_WIDTH), BF16)] * 3,
        [pltpu.VMEM((SEQ, 128), BF16)] + [pltpu.VMEM((SEQ + BLOCK, 128), BF16)] * 2 + [pltpu.VMEM((SEQ, 128), F32)] * 6,
        [proj] * 9 + [cos_t, sin_t], ride)


def _attn_bwd_group_body(g):
    d, nb = GROUPS[g]

    def body(qs_ref, ks_ref, vs_ref, cos_ref, sin_ref, lse_ref, dattn_ref, dsum_ref, dproj_ref,
             ks, vs, dos, lss, dss, dqs, dks, dvs, stage, outs, sems):
        cur_ok, prev_ok, head0 = _attn_masks()
        qs = qs_ref.at[g]
        ks[:BLOCK, :] = jnp.zeros((BLOCK, 128), BF16)
        vs[:BLOCK, :] = jnp.zeros((BLOCK, 128), BF16)
        dks[:BLOCK, :] = jnp.zeros((BLOCK, 128), F32)
        dvs[:BLOCK, :] = jnp.zeros((BLOCK, 128), F32)
        for r in range(d):
            for i in range(nb):
                src, dst = _group_rows(d, nb, r, i)
                below = pl.ds(dst.start + BLOCK, BLOCK)
                ks[below, :] = ks_ref[g, dst, :]
                vs[below, :] = vs_ref[g, dst, :]
                dos[dst, :] = dattn_ref[src, :].astype(BF16)
                for per_head, spread in ((dsum_ref[src, :], dss), (lse_ref[src, :], lss)):
                    other = pltpu.roll(per_head, HEAD_DIM, axis=1)
                    spread[0, dst, :] = jnp.where(head0, per_head, other)
                    spread[1, dst, :] = jnp.where(head0, other, per_head)
                dks[below, :] = jnp.zeros((BLOCK, 128), F32)
                dvs[below, :] = jnp.zeros((BLOCK, 128), F32)

        def per_stacked_row(spread, cur):
            h0, h1 = spread[0, cur, :], spread[1, cur, :]
            return jnp.concatenate([jnp.concatenate([h0, h0], axis=1), jnp.concatenate([h1, h1], axis=1)], axis=0)

        def block(b, carry):
            has_prev = (b & (nb - 1)) > 0
            cur = pl.ds(pl.multiple_of(b * BLOCK, BLOCK), BLOCK)
            window = pl.ds(pl.multiple_of(b * BLOCK, BLOCK), 2 * BLOCK)
            valid = jnp.logical_or(cur_ok, jnp.logical_and(prev_ok, has_prev))
            q2, do2 = _stack_heads(qs[cur, :], head0), _stack_heads(dos[cur, :], head0)
            kw, vw = ks[window, :], vs[window, :]
            s = jnp.where(valid, _dot_nt(q2, kw), NEG_INF)
            p = jnp.exp(s - per_stacked_row(lss, cur))
            ds = (p * (_dot_nt(do2, vw) - per_stacked_row(dss, cur))).astype(BF16)
            dvs[window, :] += _dot_tn(p.astype(BF16), do2)
            dks[window, :] += _dot_tn(ds, q2)
            dqs[cur, :] = _unstack_heads(_dot(ds, kw), head0)
            return carry

        lax.fori_loop(0, SEQ // BLOCK, block, 0, unroll=16)

        hp = pl.program_id(0)
        copies = []
        for kind in range(3):
            for r in range(d):
                for i in range(nb):
                    src, dst = _group_rows(d, nb, r, i)
                    below = pl.ds(dst.start + BLOCK, BLOCK)
                    if kind == 2:
                        stage[src, :] = dvs[below, :]
                    else:
                        c, s = cos_ref[g, dst, :], sin_ref[g, dst, :]
                        t = dqs[dst, :] * 0.125 if kind == 0 else dks[below, :]
                        stage[src, :] = t * c - _swap_halves(t) * s
            for i in range(SEQ // MM_ROWS):
                rows = pl.ds(i * MM_ROWS, MM_ROWS)
                outs[kind, rows, :] = stage[rows, :].astype(BF16)
            column = pl.multiple_of((kind * 12 + g * 4 + hp) * 128, 128)
            copies.append(pltpu.make_async_copy(outs.at[kind], dproj_ref.at[:, pl.ds(column, 128)], sems.at[kind]))
            copies[-1].start()
        for cp in copies:
            cp.wait()

    return body


def _attn_bwd(q_pm, k_pm, v_pm, cos_t, sin_t, attn, lse, dattn, dproj, ride=None):
    groups = [_attn_bwd_group_body(g) for g in range(3)]

    def body(qs_ref, ks_ref, vs_ref, cos_ref, sin_ref, attn_ref, lse_ref, dattn_ref, dproj_in, dproj_ref, dsum, *scratch):
        del dproj_in
        head0 = _attn_masks()[2]
        for i in range(SEQ // BLOCK):
            rows = pl.ds(i * BLOCK, BLOCK)
            prod = dattn_ref[rows, :] * attn_ref[rows, :]
            d0 = jnp.sum(jnp.where(head0, prod, 0.0), axis=1, keepdims=True)
            d1 = jnp.sum(jnp.where(head0, 0.0, prod), axis=1, keepdims=True)
            dsum[rows, :] = jnp.where(head0, d0, d1)
        for g in range(3):
            groups[g](qs_ref, ks_ref, vs_ref, cos_ref, sin_ref, lse_ref, dattn_ref, dsum, dproj_ref, *scratch)

    def col(base):
        return pl.BlockSpec((SEQ, 128), lambda hp, base=base: (0, base + hp))

    table = pl.BlockSpec((3, SEQ, 128), lambda hp: (0, 0, 0), pipeline_mode=pl.Buffered(1))
    by_phase = pl.BlockSpec((3, SEQ, 128), lambda hp: (0, 0, hp))
    return _call(
        body, "attn_bwd", (4,), [by_phase] * 3 + [table, table, col(0), col(0), col(0), ANY],
        [ANY], [_sds((SEQ, IN_WIDTH), BF16)],
        [pltpu.VMEM((SEQ, 128), F32)]
        + [pltpu.VMEM((SEQ + BLOCK, 128), BF16)] * 2 + [pltpu.VMEM((SEQ, 128), BF16)]
        + [pltpu.VMEM((2, SEQ, 128), F32)] * 2 + [pltpu.VMEM((SEQ, 128), F32)]
        + [pltpu.VMEM((SEQ + BLOCK, 128), F32)] * 2 + [pltpu.VMEM((SEQ, 128), F32)]
        + [pltpu.VMEM((3, SEQ, 128), BF16), pltpu.SemaphoreType.DMA((3,))],
        [q_pm, k_pm, v_pm, cos_t, sin_t, attn, lse, dattn, dproj], ride, aliases={8: 0})


SSM_CHUNKS = 4
CHUNK_STATES = 512
SCAN_ROWS = 8
U_COL = (3 * QKV_WIDTH) // 128


def _cmul(xr, xi, yr, yi):
    return xr * yr - xi * yi, xr * yi + xi * yr


def _ssm_prep(a_re, a_im, log_dt, b_re_t, b_im_t):
    def body(ar_ref, ai_ref, ldt_ref, br_ref, bi_ref, abr_ref, abi_ref, er_ref, ei_ref, bbr_ref, bbi_ref):
        ar, ai = ar_ref[...], ai_ref[...]
        dt = jnp.exp(ldt_ref[...])
        mag = jnp.exp(ar * dt)
        abr, abi = mag * jnp.cos(ai * dt), mag * jnp.sin(ai * dt)
        den = ar * ar + ai * ai
        nr, ni = abr - 1.0, abi
        er, ei = (nr * ar + ni * ai) / den, (ni * ar - nr * ai) / den
        abr_ref[...], abi_ref[...], er_ref[...], ei_ref[...] = abr, abi, er, ei
        er3, ei3 = er[:, None, :], ei[:, None, :]
        br, bi = br_ref[...], bi_ref[...]
        bbr_ref[...] = er3 * br - ei3 * bi
        bbi_ref[...] = er3 * bi + ei3 * br

    gp = jax.ShapeDtypeStruct(a_re.shape, F32)
    gb = jax.ShapeDtypeStruct(b_re_t.shape, F32)
    return _pallas_call(body, name="ssm_prep", out_shape=(gp, gp, gp, gp, gb, gb))(a_re, a_im, log_dt, b_re_t, b_im_t)


def _ssm_param_bwd(a_re, a_im, log_dt, b_re_t, b_im_t, abar_re, abar_im, e_re, e_im, ga_re, ga_im, gbb_re_t, gbb_im_t):
    def body(ar_ref, ai_ref, ldt_ref, br_ref, bi_ref, abr_ref, abi_ref, er_ref, ei_ref, gar_ref, gai_ref, gbr_ref, gbi_ref,
             o_ar, o_ai, o_ldt, o_br, o_bi):
        ar, ai = ar_ref[...], ai_ref[...]
        dt = jnp.exp(ldt_ref[...])
        er, ei = er_ref[...], ei_ref[...]
        br, bi, gbr, gbi = br_ref[...], bi_ref[...], gbr_ref[...], gbi_ref[...]
        er3, ei3 = er[:, None, :], ei[:, None, :]
        o_br[...] = er3 * gbr + ei3 * gbi
        o_bi[...] = er3 * gbi - ei3 * gbr
        ge_r = jnp.sum(br * gbr + bi * gbi, axis=1)
        ge_i = jnp.sum(br * gbi - bi * gbr, axis=1)
        den = ar * ar + ai * ai
        ilr, ili = ar / den, -ai / den
        t_r, t_i = _cmul(ilr, -ili, ge_r, ge_i)
        gab_r, gab_i = gar_ref[...] + t_r, gai_ref[...] + t_i
        gz_r, gz_i = _cmul(abr_ref[...], -abi_ref[...], gab_r, gab_i)
        el_r, el_i = _cmul(er, ei, ilr, ili)
        u_r, u_i = _cmul(el_r, -el_i, ge_r, ge_i)
        o_ar[...] = dt * gz_r - u_r
        o_ai[...] = dt * gz_i - u_i
        o_ldt[...] = jnp.sum(gz_r * ar + gz_i * ai, axis=1, keepdims=True) * dt

    gp = jax.ShapeDtypeStruct(a_re.shape, F32)
    gb = jax.ShapeDtypeStruct(b_re_t.shape, F32)
    return _pallas_call(body, name="ssm_param_bwd", out_shape=(gp, gp, jax.ShapeDtypeStruct(log_dt.shape, F32), gb, gb))(
        a_re, a_im, log_dt, b_re_t, b_im_t, abar_re, abar_im, e_re, e_im, ga_re, ga_im, gbb_re_t, gbb_im_t)


def _block_diag(blocks_re, blocks_im, sign_im, rows_are_channels):
    both = jnp.stack([blocks_re, sign_im * blocks_im]).reshape(2, SSM_CHUNKS, 8, SSM_GROUP, SSM_STATE)
    eye = jnp.eye(8, dtype=F32)
    if rows_are_channels:
        return jnp.einsum("rcghp,gk->cghrkp", both, eye).reshape(SSM_CHUNKS, 128, 2 * CHUNK_STATES)
    return jnp.einsum("rcghp,gk->crkpgh", both, eye).reshape(SSM_CHUNKS, 2 * CHUNK_STATES, 128)


def _block_diag_parts(mat, rows_are_channels):
    if rows_are_channels:
        six = mat.reshape(SSM_CHUNKS, 8, SSM_GROUP, 2, 8, SSM_STATE)
        parts = jnp.einsum("cghrgp->rcghp", six)
    else:
        six = mat.reshape(SSM_CHUNKS, 2, 8, SSM_STATE, 8, SSM_GROUP)
        parts = jnp.einsum("crgpgh->rcghp", six)
    parts = parts.reshape(2, SSM_GROUPS, SSM_GROUP, SSM_STATE)
    return parts[0], parts[1]


def _scan_consts(a_ref, conj, reverse):
    ar = jnp.broadcast_to(a_ref[:, :CHUNK_STATES], (SCAN_ROWS, CHUNK_STATES))
    ai = jnp.broadcast_to(a_ref[:, CHUNK_STATES:], (SCAN_ROWS, CHUNK_STATES))
    if conj:
        ai = -ai
    row = lax.broadcasted_iota(jnp.int32, (SCAN_ROWS, CHUNK_STATES), 0)
    if reverse:
        row = SCAN_ROWS - 1 - row
    zero = jnp.zeros_like(ar)
    steps = []
    pr, pi = ar, ai
    for shift in (1, 2, 4):
        keep = row >= shift
        steps.append((SCAN_ROWS - shift if reverse else shift, jnp.where(keep, pr, zero), jnp.where(keep, pi, zero)))
        pr, pi = _cmul(pr, pi, pr, pi)
    first = row == 0
    return steps, (jnp.where(first, ar, zero), jnp.where(first, ai, zero)), first


def _scan_tile(xr, xi, prev_r, prev_i, steps, carry_in, reverse):
    edge = SCAN_ROWS - 1 if reverse else 1
    cr, ci = pltpu.roll(prev_r, edge, axis=0), pltpu.roll(prev_i, edge, axis=0)
    xr, xi = xr + carry_in[0] * cr - carry_in[1] * ci, xi + carry_in[0] * ci + carry_in[1] * cr
    for shift, mr, mi in steps:
        sr, si = pltpu.roll(xr, shift, axis=0), pltpu.roll(xi, shift, axis=0)
        xr, xi = xr + mr * sr - mi * si, xi + mr * si + mi * sr
    return xr, xi


MM_ROWS = 256


def _ssm_fwd(proj, bmat, cmat, a_chunks, d_skip, ride=None):
    def body(u_ref, b_ref, c_ref, a_ref, d_ref, y_ref, states_ref, h_ref):
        for i in range(SEQ // MM_ROWS):
            rows = pl.ds(i * MM_ROWS, MM_ROWS)
            h_ref[rows, :] = _dot(u_ref[rows, :].astype(BF16), b_ref[...])
        steps, carry_in, _ = _scan_consts(a_ref, conj=False, reverse=False)

        def tile(k, carry):
            rows = pl.ds(pl.multiple_of(k * SCAN_ROWS, SCAN_ROWS), SCAN_ROWS)
            xr, xi = _scan_tile(h_ref[rows, :CHUNK_STATES], h_ref[rows, CHUNK_STATES:], carry[0], carry[1], steps, carry_in, False)
            h_ref[rows, :CHUNK_STATES] = xr
            h_ref[rows, CHUNK_STATES:] = xi
            return xr, xi

        zero = jnp.zeros((SCAN_ROWS, CHUNK_STATES), F32)
        lax.fori_loop(0, SEQ // SCAN_ROWS, tile, (zero, zero), unroll=4)
        for i in range(SEQ // MM_ROWS):
            rows = pl.ds(i * MM_ROWS, MM_ROWS)
            states = h_ref[rows, :].astype(BF16)
            states_ref[rows, :] = states
            y_ref[rows, :] = _dot(states, c_ref[...]) + d_ref[...] * u_ref[rows, :]

    return _call(
        body, "ssm_fwd", (SSM_CHUNKS,),
        [pl.BlockSpec((SEQ, 128), lambda c: (0, U_COL + c)),
         pl.BlockSpec((None, 128, 2 * CHUNK_STATES), lambda c: (c, 0, 0)),
         pl.BlockSpec((None, 2 * CHUNK_STATES, 128), lambda c: (c, 0, 0)),
         pl.BlockSpec((None, 1, 2 * CHUNK_STATES), lambda c: (c, 0, 0)),
         pl.BlockSpec((1, 128), lambda c: (0, c))],
        [pl.BlockSpec((SEQ, 128), lambda c: (0, c)), pl.BlockSpec((SEQ, 2 * CHUNK_STATES), lambda c: (0, c))],
        [_sds((SEQ, SSM_WIDTH), F32), _sds((SEQ, SSM_CHUNKS * 2 * CHUNK_STATES), BF16)],
        [pltpu.VMEM((SEQ, 2 * CHUNK_STATES), F32)],
        [proj, bmat, cmat, a_chunks, d_skip], ride)


def _ssm_bwd(dys, proj, h, bmat, cmat, a_chunks, d_skip, dproj, ride=None):
    def body(dy_ref, u_ref, states_ref, b_ref, c_ref, a_ref, d_ref, dproj_in, du_ref, db_ref, dc_ref, da_ref, dd_ref,
             g_ref, h_ref):
        del dproj_in
        dsum = jnp.zeros((1, 128), F32)
        dcm = jnp.zeros((2 * CHUNK_STATES, 128), F32)
        for i in range(SEQ // MM_ROWS):
            rows = pl.ds(i * MM_ROWS, MM_ROWS)
            h_ref[rows, :] = states_ref[rows, :].astype(F32)
            dy = dy_ref[rows, :]
            g_ref[rows, :] = _dot_nt(dy.astype(BF16), c_ref[...])
            dsum += jnp.sum(dy * u_ref[rows, :], axis=0, keepdims=True)
            dcm += _dot_tn(states_ref[rows, :], dy.astype(BF16))
        dd_ref[...] = dsum
        dc_ref[...] = dcm
        steps, carry_in, _ = _scan_consts(a_ref, conj=True, reverse=True)
        first_row = lax.broadcasted_iota(jnp.int32, (SCAN_ROWS, CHUNK_STATES), 0) == 0
        n_tiles = SEQ // SCAN_ROWS

        def tile(j, carry):
            k = n_tiles - 1 - j
            rows = pl.ds(pl.multiple_of(k * SCAN_ROWS, SCAN_ROWS), SCAN_ROWS)
            before = pl.ds(pl.multiple_of(jnp.maximum(k - 1, 0) * SCAN_ROWS, SCAN_ROWS), SCAN_ROWS)
            gr, gi = _scan_tile(g_ref[rows, :CHUNK_STATES], g_ref[rows, CHUNK_STATES:], carry[0], carry[1], steps, carry_in, True)
            g_ref[rows, :CHUNK_STATES] = gr
            g_ref[rows, CHUNK_STATES:] = gi
            has_before = jnp.where(k > 0, 1.0, 0.0)
            hr = jnp.where(first_row, pltpu.roll(h_ref[before, :CHUNK_STATES], 1, axis=0) * has_before,
                           pltpu.roll(h_ref[rows, :CHUNK_STATES], 1, axis=0))
            hi = jnp.where(first_row, pltpu.roll(h_ref[before, CHUNK_STATES:], 1, axis=0) * has_before,
                           pltpu.roll(h_ref[rows, CHUNK_STATES:], 1, axis=0))
            return gr, gi, carry[2] + hr * gr + hi * gi, carry[3] + hr * gi - hi * gr

        zero = jnp.zeros((SCAN_ROWS, CHUNK_STATES), F32)
        _, _, sar, sai = lax.fori_loop(0, n_tiles, tile, (zero, zero, zero, zero), unroll=4)
        da_ref[:, :CHUNK_STATES] = jnp.sum(sar, axis=0, keepdims=True)
        da_ref[:, CHUNK_STATES:] = jnp.sum(sai, axis=0, keepdims=True)
        dbm = jnp.zeros((128, 2 * CHUNK_STATES), F32)
        for i in range(SEQ // MM_ROWS):
            rows = pl.ds(i * MM_ROWS, MM_ROWS)
            g = g_ref[rows, :].astype(BF16)
            du_ref[rows, :] = (_dot_nt(g, b_ref[...]) + d_ref[...] * dy_ref[rows, :]).astype(BF16)
            dbm += _dot_tn(u_ref[rows, :].astype(BF16), g)
        db_ref[...] = dbm

    chunk_col = pl.BlockSpec((SEQ, 128), lambda c: (0, c))
    return _call(
        body, "ssm_bwd", (SSM_CHUNKS,),
        [chunk_col,
         pl.BlockSpec((SEQ, 128), lambda c: (0, U_COL + c)),
         pl.BlockSpec((SEQ, 2 * CHUNK_STATES), lambda c: (0, c)),
         pl.BlockSpec((None, 128, 2 * CHUNK_STATES), lambda c: (c, 0, 0)),
         pl.BlockSpec((None, 2 * CHUNK_STATES, 128), lambda c: (c, 0, 0)),
         pl.BlockSpec((None, 1, 2 * CHUNK_STATES), lambda c: (c, 0, 0)),
         pl.BlockSpec((1, 128), lambda c: (0, c)), ANY],
        [pl.BlockSpec((SEQ, 128), lambda c: (0, U_COL + c)),
         pl.BlockSpec((None, 128, 2 * CHUNK_STATES), lambda c: (c, 0, 0)),
         pl.BlockSpec((None, 2 * CHUNK_STATES, 128), lambda c: (c, 0, 0)),
         pl.BlockSpec((None, 1, 2 * CHUNK_STATES), lambda c: (c, 0, 0)),
         pl.BlockSpec((1, 128), lambda c: (0, c))],
        [_sds((SEQ, IN_WIDTH), BF16), _sds((SSM_CHUNKS, 128, 2 * CHUNK_STATES), F32),
         _sds((SSM_CHUNKS, 2 * CHUNK_STATES, 128), F32), _sds((SSM_CHUNKS, 1, 2 * CHUNK_STATES), F32), _sds((1, SSM_WIDTH), F32)],
        [pltpu.VMEM((SEQ, 2 * CHUNK_STATES), F32)] * 2, [dys, proj, h, bmat, cmat, a_chunks, d_skip, dproj], ride, aliases={7: 0})


def _ssm_tables(abar_re, abar_im, bbar_re_t, bbar_im_t, c_re, c_im):
    bmat = _block_diag(bbar_re_t, bbar_im_t, 1.0, True).astype(BF16)
    cmat = _block_diag(c_re, c_im, -1.0, False).astype(BF16)
    a_chunks = jnp.concatenate([abar_re.reshape(SSM_CHUNKS, 1, CHUNK_STATES), abar_im.reshape(SSM_CHUNKS, 1, CHUNK_STATES)], axis=2)
    return bmat, cmat, a_chunks


GL_COL = (3 * QKV_WIDTH + SSM_WIDTH) // D_MODEL
GELU_C = math.sqrt(2.0 / math.pi)
GELU_A = 0.044715


def _sds(shape, dtype):
    return jax.ShapeDtypeStruct(shape, dtype)


def _gelu(x):
    t = jnp.tanh(GELU_C * (x + GELU_A * x * x * x))
    return 0.5 * x * (1.0 + t), t


def _gelu_grad(x, t):
    return 0.5 * (1.0 + t) + 0.5 * x * (1.0 - t * t) * GELU_C * (1.0 + 3.0 * GELU_A * x * x)


def _layer_norm(r, g, b):
    mu = jnp.mean(r, axis=-1, keepdims=True)
    xc = r - mu
    rstd = lax.rsqrt(jnp.mean(xc * xc, axis=-1, keepdims=True) + LN_EPS)
    xhat = xc * rstd
    return xhat * g + b, xhat, rstd


def _layer_norm_bwd(dy, xhat, rstd, g):
    dxhat = dy * g
    m1 = jnp.mean(dxhat, axis=-1, keepdims=True)
    m2 = jnp.mean(dxhat * xhat, axis=-1, keepdims=True)
    return rstd * (dxhat - m1 - xhat * m2)


def _proj(x, w_in, ride=None):
    tm, tn = 1024, 1792

    def body(x_ref, w_ref, o_ref):
        o_ref[...] = _dot(x_ref[...].astype(BF16), _side_by_side(w_ref))

    return _call(
        body, "proj", (SEQ // tm, IN_WIDTH // tn),
        [pl.BlockSpec((tm, D_MODEL), lambda i, j: (i, 0)), pl.BlockSpec((2, D_MODEL, tn // 2), lambda i, j: (j, 0, 0))],
        [pl.BlockSpec((tm, tn), lambda i, j: (i, j))], [_sds((SEQ, IN_WIDTH), F32)], [], [x, w_in], ride)


def _row_spec(tm, width, col=0):
    return pl.BlockSpec((tm, width), lambda i, col=col: (i, col))


def _full_spec(shape):
    return pl.BlockSpec(shape, lambda i: (0,) * len(shape))


def _weight_spec(shape):
    return pl.BlockSpec(shape, lambda i: (0,) * len(shape), pipeline_mode=pl.Buffered(1))


def _mixer_out(attn, ys, proj, x, w_ab, w_sb, w_glu, w_out, b_gate, ln_g, ln_b, ride=None):
    tm = 512

    def body(attn_ref, ys_ref, gl0_ref, gl1_ref, x_ref, wab_ref, wsb_ref, wglu_ref, wout_ref, bg_ref, g_ref, b_ref,
             h_ref, xhat_ref, rstd_ref, glu_ref, ya_ref, yssm_ref):
        gy, _ = _gelu(ys_ref[...])
        glu = _dot(gy.astype(BF16), _side_by_side(wglu_ref))
        glu_ref[...] = glu.astype(BF16)
        y_s = glu[:, :SSM_WIDTH] * jax.nn.sigmoid(glu[:, SSM_WIDTH:])
        y_ssm = _dot(y_s.astype(BF16), _side_by_side(wsb_ref))
        y_attn = _dot(attn_ref[...].astype(BF16), _side_by_side(wab_ref))
        ya_ref[...] = y_attn.astype(BF16)
        yssm_ref[...] = y_ssm.astype(BF16)
        g0 = jax.nn.sigmoid(gl0_ref[...] + _side_by_side(bg_ref, 0))
        g1 = jax.nn.sigmoid(gl1_ref[...] + _side_by_side(bg_ref, 1))
        mixed = g0 * y_attn + g1 * y_ssm
        r1 = DN_ALPHA * x_ref[...] + _dot(mixed.astype(BF16), wout_ref[...])
        h, xhat, rstd = _layer_norm(r1, g_ref[...], b_ref[...])
        h_ref[...] = h
        xhat_ref[...] = xhat
        rstd_ref[...] = jnp.broadcast_to(rstd, (tm, 128))

    wide = _sds((SEQ, D_MODEL), F32)
    return _call(
        body, "mixer_out", (SEQ // tm,),
        [_row_spec(tm, ATTN_WIDTH), _row_spec(tm, SSM_WIDTH), _row_spec(tm, D_MODEL, GL_COL), _row_spec(tm, D_MODEL, GL_COL + 1),
         _row_spec(tm, D_MODEL), _weight_spec((N_DEV, ATTN_WIDTH, 128)), _weight_spec((N_DEV, SSM_WIDTH, 128)),
         _weight_spec((N_DEV, SSM_WIDTH, 128)), _weight_spec((D_MODEL, D_MODEL)), _full_spec((N_DEV, 2, 128)),
         _full_spec((1, D_MODEL)), _full_spec((1, D_MODEL))],
        [_row_spec(tm, D_MODEL), _row_spec(tm, D_MODEL), _row_spec(tm, 128), _row_spec(tm, D_MODEL),
         _row_spec(tm, D_MODEL), _row_spec(tm, D_MODEL)],
        [wide, wide, _sds((SEQ, 128), F32)] + [_sds((SEQ, D_MODEL), BF16)] * 3, [],
        [attn, ys, proj, proj, x, w_ab, w_sb, w_glu, w_out, b_gate, ln_g, ln_b], ride)


def _ff_up(h, w_gate, w_up, ride=None):
    tm, tn = 1024, 768

    def body(h_ref, wg_ref, wu_ref, a_ref, b_ref, f_ref):
        hb = h_ref[...].astype(BF16)
        a, b = _dot(hb, _side_by_side(wg_ref)), _dot(hb, _side_by_side(wu_ref))
        a_ref[...] = a.astype(BF16)
        b_ref[...] = b.astype(BF16)
        f_ref[...] = (a * jax.nn.sigmoid(a) * b).astype(BF16)

    tile = pl.BlockSpec((tm, tn), lambda i, j: (i, j))
    wtile = pl.BlockSpec((tn // FF_PAD, D_MODEL, FF_PAD), lambda i, j: (j, 0, 0))
    out = _sds((SEQ, D_FF_PAD), BF16)
    return _call(body, "ff_up", (SEQ // tm, D_FF_PAD // tn), [pl.BlockSpec((tm, D_MODEL), lambda i, j: (i, 0)), wtile, wtile],
                 [tile, tile, tile], [out, out, out], [], [h, w_gate, w_up], ride)


def _ff_down_loss(f, w_down, h, target, ln_g, ln_b):
    tm = 512

    def body(f_ref, w_ref, h_ref, t_ref, g_ref, b_ref, dr_ref, dg_ref, db_ref, loss_ref):
        @pl.when(pl.program_id(0) == 0)
        def _():
            dg_ref[...] = jnp.zeros_like(dg_ref)
            db_ref[...] = jnp.zeros_like(db_ref)
            loss_ref[...] = jnp.zeros_like(loss_ref)

        r2 = DN_ALPHA * h_ref[...] + _dot(f_ref[...], w_ref[...])
        g = g_ref[...]
        out, xhat, rstd = _layer_norm(r2, g, b_ref[...])
        err = out - t_ref[...]
        loss_ref[...] += 0.5 * jnp.sum(jnp.mean(err * err, axis=-1, keepdims=True), axis=0, keepdims=True)
        dout = err * (1.0 / D_MODEL)
        dg_ref[...] += jnp.sum(dout * xhat, axis=0, keepdims=True)
        db_ref[...] += jnp.sum(dout, axis=0, keepdims=True)
        dr_ref[...] = _layer_norm_bwd(dout, xhat, rstd, g)

    vec = _sds((1, D_MODEL), F32)
    return _pallas_call(
        body, name="ff_down_loss", grid=(SEQ // tm,),
        in_specs=[_row_spec(tm, D_FF_PAD), _weight_spec((D_FF_PAD, D_MODEL)), _row_spec(tm, D_MODEL), _row_spec(tm, D_MODEL),
                  _full_spec((1, D_MODEL)), _full_spec((1, D_MODEL))],
        out_specs=(_row_spec(tm, D_MODEL), _full_spec((1, D_MODEL)), _full_spec((1, D_MODEL)), _full_spec((1, 128))),
        out_shape=(_sds((SEQ, D_MODEL), F32), vec, vec, _sds((1, 128), F32)),
        compiler_params=_cparams(dimension_semantics=("arbitrary",)),
    )(f, w_down, h, target, ln_g, ln_b)


def _ff_down_bwd(dr2, w_down, a, b):
    tm, tn = 1024, 768

    def body(dr_ref, w_ref, a_ref, b_ref, da_ref, db_ref):
        df = _dot_nt(dr_ref[...].astype(BF16), w_ref[...])
        av, bv = a_ref[...].astype(F32), b_ref[...].astype(F32)
        sg = jax.nn.sigmoid(av)
        da_ref[...] = (df * bv * sg * (1.0 + av * (1.0 - sg))).astype(BF16)
        db_ref[...] = (df * av * sg).astype(BF16)

    tile = pl.BlockSpec((tm, tn), lambda i, j: (i, j))
    out = _sds((SEQ, D_FF_PAD), BF16)
    return _pallas_call(
        body, name="ff_down_bwd", grid=(SEQ // tm, D_FF_PAD // tn),
        in_specs=[pl.BlockSpec((tm, D_MODEL), lambda i, j: (i, 0)), pl.BlockSpec((tn, D_MODEL), lambda i, j: (j, 0)), tile, tile],
        out_specs=(tile, tile), out_shape=(out, out),
        compiler_params=_cparams(dimension_semantics=("arbitrary", "arbitrary")),
    )(dr2, w_down, a, b)


def _ff_up_bwd(da, db, w_gate, w_up, dr2, xhat1, rstd1, ln_g, ride=None):
    tm, tk = 1024, 768
    nk = D_FF_PAD // tk

    def body(da_ref, db_ref, wg_ref, wu_ref, dr2_ref, xhat_ref, rstd_ref, g_ref, dr1_ref, dg_ref, dbias_ref, acc):
        i, k = pl.program_id(0), pl.program_id(1)

        @pl.when(jnp.logical_and(i == 0, k == 0))
        def _():
            dg_ref[...] = jnp.zeros_like(dg_ref)
            dbias_ref[...] = jnp.zeros_like(dbias_ref)

        part = _dot_nt(da_ref[...], _side_by_side(wg_ref)) + _dot_nt(db_ref[...], _side_by_side(wu_ref))

        @pl.when(k == 0)
        def _():
            acc[...] = part

        @pl.when(k > 0)
        def _():
            acc[...] += part

        @pl.when(k == nk - 1)
        def _():
            dh = DN_ALPHA * dr2_ref[...] + acc[...]
            xhat = xhat_ref[...]
            dg_ref[...] += jnp.sum(dh * xhat, axis=0, keepdims=True)
            dbias_ref[...] += jnp.sum(dh, axis=0, keepdims=True)
            rstd = jnp.max(rstd_ref[...], axis=1, keepdims=True)
            dr1_ref[...] = _layer_norm_bwd(dh, xhat, rstd, g_ref[...])

    hid = pl.BlockSpec((tm, tk), lambda i, k: (i, k))
    wtile = pl.BlockSpec((tk // FF_PAD, D_MODEL, FF_PAD), lambda i, k: (k, 0, 0))
    row = pl.BlockSpec((tm, D_MODEL), lambda i, k: (i, 0))
    vec = pl.BlockSpec((1, D_MODEL), lambda i, k: (0, 0))
    return _call(
        body, "ff_up_bwd", (SEQ // tm, nk),
        [hid, hid, wtile, wtile, row, row, pl.BlockSpec((tm, 128), lambda i, k: (i, 0)), vec],
        [row, vec, vec], [_sds((SEQ, D_MODEL), F32), _sds((1, D_MODEL), F32), _sds((1, D_MODEL), F32)],
        [pltpu.VMEM((tm, D_MODEL), F32)], [da, db, w_gate, w_up, dr2, xhat1, rstd1, ln_g], ride)


def _mixer_bwd(dr1, proj, y_attn, y_ssm, glu, ys, w_ab, w_sb, w_glu, w_out, b_gate):
    tm = 256

    def body(dr1_ref, gl0_ref, gl1_ref, ya_ref, yssm_ref, glu_ref, ys_ref, wab_ref, wsb_ref, wglu_ref, wout_ref, bg_ref,
             dya_ref, dyssm_ref, dgl_ref, dattn_ref, dglu_ref, dys_ref, mixed_ref, ysb_ref, gy_ref, dbg_ref, stage, copied):
        @pl.when(pl.program_id(0) == 0)
        def _():
            dbg_ref[...] = jnp.zeros_like(dbg_ref)

        dmixed = _dot_nt(dr1_ref[...].astype(BF16), wout_ref[...])
        g0 = jax.nn.sigmoid(gl0_ref[...] + _side_by_side(bg_ref, 0))
        g1 = jax.nn.sigmoid(gl1_ref[...] + _side_by_side(bg_ref, 1))
        y_attn, y_ssm = ya_ref[...].astype(F32), yssm_ref[...].astype(F32)
        mixed_ref[...] = (g0 * y_attn + g1 * y_ssm).astype(BF16)
        dya = (dmixed * g0).astype(BF16)
        dyssm = (dmixed * g1).astype(BF16)
        dya_ref[...] = dya
        dyssm_ref[...] = dyssm
        dgl0 = dmixed * y_attn * g0 * (1.0 - g0)
        dgl1 = dmixed * y_ssm * g1 * (1.0 - g1)
        i, last = pl.program_id(0), SEQ // tm - 1
        slot = i & 1

        def copy_out(buffer, tile):
            window = dgl_ref.at[pl.ds(pl.multiple_of(tile * tm, tm), tm), pl.ds(GL_COL * D_MODEL, 2 * D_MODEL)]
            return pltpu.make_async_copy(stage.at[buffer], window, copied.at[buffer])

        @pl.when(i >= 2)
        def _():
            copy_out(slot, i - 2).wait()

        stage[slot, :, :D_MODEL] = dgl0.astype(BF16)
        stage[slot, :, D_MODEL:] = dgl1.astype(BF16)
        copy_out(slot, i).start()

        @pl.when(i == last)
        def _():
            copy_out(1 - slot, i - 1).wait()
            copy_out(slot, i).wait()
        dbg_ref[:, :D_MODEL] += jnp.sum(dgl0, axis=0, keepdims=True)
        dbg_ref[:, D_MODEL:] += jnp.sum(dgl1, axis=0, keepdims=True)
        dattn_ref[...] = _dot_nt(dya, _side_by_side(wab_ref))
        dy_s = _dot_nt(dyssm, _side_by_side(wsb_ref))
        glu = glu_ref[...].astype(F32)
        glu1, sg = glu[:, :SSM_WIDTH], jax.nn.sigmoid(glu[:, SSM_WIDTH:])
        ysb_ref[...] = (glu1 * sg).astype(BF16)
        dglu1 = (dy_s * sg).astype(BF16)
        dglu2 = (dy_s * glu1 * sg * (1.0 - sg)).astype(BF16)
        dglu_ref[:, :SSM_WIDTH] = dglu1
        dglu_ref[:, SSM_WIDTH:] = dglu2
        dgy = _dot_nt(jnp.concatenate([dglu1, dglu2], axis=1), _side_by_side(wglu_ref))
        ys = ys_ref[...]
        gy, t = _gelu(ys)
        gy_ref[...] = gy.astype(BF16)
        dys_ref[...] = dgy * _gelu_grad(ys, t)

    wide_b, half_b = _sds((SEQ, D_MODEL), BF16), _sds((SEQ, SSM_WIDTH), BF16)
    half_f = _sds((SEQ, SSM_WIDTH), F32)
    return _pallas_call(
        body, name="mixer_bwd", grid=(SEQ // tm,),
        in_specs=[_row_spec(tm, D_MODEL), _row_spec(tm, D_MODEL, GL_COL), _row_spec(tm, D_MODEL, GL_COL + 1), _row_spec(tm, D_MODEL),
                  _row_spec(tm, D_MODEL), _row_spec(tm, D_MODEL), _row_spec(tm, SSM_WIDTH), _full_spec((N_DEV, ATTN_WIDTH, 128)),
                  _full_spec((N_DEV, SSM_WIDTH, 128)), _full_spec((N_DEV, SSM_WIDTH, 128)), _full_spec((D_MODEL, D_MODEL)),
                  _full_spec((N_DEV, 2, 128))],
        out_specs=(_row_spec(tm, D_MODEL), _row_spec(tm, D_MODEL), ANY, _row_spec(tm, ATTN_WIDTH),
                   _row_spec(tm, D_MODEL), _row_spec(tm, SSM_WIDTH), _row_spec(tm, D_MODEL), _row_spec(tm, SSM_WIDTH),
                   _row_spec(tm, SSM_WIDTH), _full_spec((1, 2 * D_MODEL))),
        out_shape=(wide_b, wide_b, _sds((SEQ, IN_WIDTH), BF16), half_f, wide_b, half_f, wide_b, half_b, half_b,
                   _sds((1, 2 * D_MODEL), F32)),
        scratch_shapes=[pltpu.VMEM((2, tm, 2 * D_MODEL), BF16), pltpu.SemaphoreType.DMA((2,))],
        compiler_params=_cparams(dimension_semantics=("arbitrary",)),
    )(dr1, proj, proj, y_attn, y_ssm, glu, ys, w_ab, w_sb, w_glu, w_out, b_gate)


def _grad_x(dproj, w_in, dr1, ride=None):
    tm, tk = 1024, 1792
    nk = IN_WIDTH // tk

    def body(dp_ref, w_ref, dr1_ref, o_ref, acc):
        k = pl.program_id(1)
        part = _dot_nt(dp_ref[...], _side_by_side(w_ref))

        @pl.when(k == 0)
        def _():
            acc[...] = part

        @pl.when(k > 0)
        def _():
            acc[...] += part

        @pl.when(k == nk - 1)
        def _():
            o_ref[...] = DN_ALPHA * dr1_ref[...] + acc[...]

    row = pl.BlockSpec((tm, D_MODEL), lambda i, k: (i, 0))
    return _call(
        body, "grad_x", (SEQ // tm, nk),
        [pl.BlockSpec((tm, tk), lambda i, k: (i, k)), pl.BlockSpec((2, D_MODEL, tk // 2), lambda i, k: (k, 0, 0)), row],
        [row], [_sds((SEQ, D_MODEL), F32)], [pltpu.VMEM((tm, D_MODEL), F32)], [dproj, w_in, dr1], ride)


def _weight_grad(a, b, name, shard_cols=None):
    k, n = a.shape[1], b.shape[1]
    tk = min(k, 512) if shard_cols else k // N_DEV
    tn = n // 4 if shard_cols else min(n, 1024)

    def body(a_ref, b_ref, o_ref):
        grad = _dot_tn(a_ref[...].astype(BF16), b_ref[...].astype(BF16))
        if shard_cols:
            o_ref[0] = grad[:, :shard_cols].astype(BF16)
            o_ref[1] = grad[:, shard_cols:].astype(BF16)
        else:
            o_ref[...] = grad.astype(BF16)

    if shard_cols:
        out_spec = pl.BlockSpec((2, None, tk, shard_cols), lambda kk, j: (0, j, kk, 0))
        out_shape = _sds((2, 4, k, shard_cols), BF16)
    else:
        out_spec = pl.BlockSpec((None, None, tk, tn), lambda kk, j: (kk % 2, kk // 2, 0, j))
        out_shape = _sds((2, 4, tk, n), BF16)
    return _call(body, name, (k // tk, n // tn),
                 [pl.BlockSpec((SEQ, tk), lambda kk, j: (0, kk)), pl.BlockSpec((SEQ, tn), lambda kk, j: (0, j))],
                 [out_spec], [out_shape], [], [a, b])[0]


def _weight_grad_of_core(a, b, core, name, shard_cols, ride):
    k, tk = a.shape[1], min(a.shape[1], 512)

    def body(a_ref, b_ref, o_ref):
        o_ref[...] = _dot_tn(a_ref[...].astype(BF16), b_ref[...].astype(BF16)).astype(BF16)

    return _call(body, name, (k // tk, 4),
                 [pl.BlockSpec((SEQ, tk), lambda kk, j, core_ref: (0, kk)),
                  pl.BlockSpec((SEQ, shard_cols), lambda kk, j, core_ref: (0, 2 * j + core_ref[0]))],
                 [pl.BlockSpec((None, tk, shard_cols), lambda kk, j, core_ref: (j, kk, 0))],
                 [_sds((4, k, shard_cols), BF16)], [], [a, b], ride, prefetch=core)


MESH = pl.DeviceIdType.MESH
ANY = pl.BlockSpec(memory_space=pl.ANY)


def _place():
    return lax.axis_index("x"), lax.axis_index("y"), lax.axis_index("c")


def _other_chips(x, y):
    return [(1 - x, y), (x, 1 - y), (1 - x, 1 - y)]


class _Ride:
    def __init__(self, operands, results, aliases, sems, start, wait):
        self.operands, self.results, self.aliases, self.sems = list(operands), list(results), dict(aliases), list(sems)
        self.start, self.wait = start, wait

    def __add__(self, other):
        n_in, n_out, n_sem = len(self.operands), len(self.results), len(self.sems)

        def both(which):
            def run(ins, outs, sems):
                getattr(self, which)(ins[:n_in], outs[:n_out], sems[:n_sem])
                getattr(other, which)(ins[n_in:], outs[n_out:], sems[n_sem:])
            return run

        aliases = {**self.aliases, **{n_in + i: n_out + j for i, j in other.aliases.items()}}
        return _Ride(self.operands + other.operands, self.results + other.results, aliases, self.sems + other.sems,
                     both("start"), both("wait"))


def _call(body, name, grid, in_specs, out_specs, out_shape, scratch_shapes, operands, ride=None, aliases=None, prefetch=None):
    in_specs, out_specs, out_shape = list(in_specs), list(out_specs), list(out_shape)
    scratch_shapes, operands, aliases = list(scratch_shapes), list(operands), dict(aliases or {})
    kernel_body = body
    if ride is not None:
        n_in, n_out, n_scr, r_in, r_out = len(in_specs), len(out_specs), len(scratch_shapes), len(ride.operands), len(ride.results)

        def kernel_body(*refs):
            out0, scr0 = n_in + r_in, n_in + r_in + n_out + r_out
            ride_refs = (refs[n_in:out0], refs[out0 + n_out:scr0], refs[scr0 + n_scr:])
            ids = [pl.program_id(i) for i in range(len(grid))]
            first = functools.reduce(jnp.logical_and, [i == 0 for i in ids])
            last = functools.reduce(jnp.logical_and, [i == g - 1 for i, g in zip(ids, grid)])

            @pl.when(first)
            def _():
                ride.start(*ride_refs)

            body(*refs[:n_in], *refs[out0:out0 + n_out], *refs[scr0:scr0 + n_scr])

            @pl.when(last)
            def _():
                ride.wait(*ride_refs)

        aliases.update({n_in + i: n_out + j for i, j in ride.aliases.items()})
        in_specs += [ANY] * r_in
        out_specs += [ANY] * r_out
        out_shape += ride.results
        scratch_shapes += ride.sems
        operands += ride.operands
    params = _cparams(dimension_semantics=("arbitrary",) * len(grid))
    if prefetch is None:
        return _pallas_call(
            kernel_body, name=name, grid=grid, in_specs=in_specs, out_specs=out_specs, out_shape=out_shape,
            scratch_shapes=scratch_shapes, input_output_aliases=aliases, compiler_params=params,
        )(*operands)

    def with_prefetch(prefetch_ref, *refs):
        kernel_body(*refs)

    return _pallas_call(
        with_prefetch, name=name,
        grid_spec=pltpu.PrefetchScalarGridSpec(num_scalar_prefetch=1, grid=grid, in_specs=in_specs, out_specs=out_specs,
                                               scratch_shapes=scratch_shapes),
        out_shape=out_shape, input_output_aliases={i + 1: j for i, j in aliases.items()}, compiler_params=params,
    )(prefetch, *operands)


def _after(*arrays):
    return _Ride(arrays, [], {}, [], lambda *refs: None, lambda *refs: None)


def _gather_first_level(shards):
    n = len(shards)

    def copies(ins, outs, sems, landed):
        send_sems, recv_sems, local_sems = sems
        x, y, c = _place()
        peers = [(x, y, 1 - c)] + [(px, py, c) for px, py in _other_chips(x, y)]

        def row(peer):
            return 4 * x + 2 * y + c if not landed else 4 * peer[0] + 2 * peer[1] + peer[2]

        local = [pltpu.make_async_copy(ins[a], outs[a].at[4 * x + 2 * y + c], local_sems.at[a]) for a in range(n)]
        remote = [pltpu.make_async_remote_copy(
            src_ref=ins[a], dst_ref=outs[a].at[row(peer)], send_sem=send_sems.at[a, k], recv_sem=recv_sems.at[a, k],
            device_id=peer, device_id_type=MESH) for a in range(n) for k, peer in enumerate(peers)]
        return local, remote

    def start(ins, outs, sems):
        local, remote = copies(ins, outs, sems, False)
        for cp in local + remote:
            cp.start()

    def wait(ins, outs, sems):
        local, sent = copies(ins, outs, sems, False)
        for cp in copies(ins, outs, sems, True)[1]:
            cp.wait_recv()
        for cp in sent:
            cp.wait_send()
        for cp in local:
            cp.wait()

    return _Ride(shards, [_sds((N_DEV,) + s.shape, s.dtype) for s in shards], {},
                 [pltpu.SemaphoreType.DMA((n, 4)), pltpu.SemaphoreType.DMA((n, 4)), pltpu.SemaphoreType.DMA((n,))], start, wait)


def _gather_second_level(buffers):
    n = len(buffers)

    def copies(outs, sems, core):
        send_sems, recv_sems = sems
        x, y, c = _place()
        return [pltpu.make_async_remote_copy(
            src_ref=outs[a].at[4 * px + 2 * py + core], dst_ref=outs[a].at[4 * px + 2 * py + core], send_sem=send_sems.at[a, j],
            recv_sem=recv_sems.at[a, j], device_id=(x, y, 1 - c), device_id_type=MESH)
            for a in range(n) for j, (px, py) in enumerate(_other_chips(x, y))]

    def start(ins, outs, sems):
        for cp in copies(outs, sems, lax.axis_index("c")):
            cp.start()

    def wait(ins, outs, sems):
        for cp in copies(outs, sems, 1 - lax.axis_index("c")):
            cp.wait_recv()
        for cp in copies(outs, sems, lax.axis_index("c")):
            cp.wait_send()

    return _Ride(buffers, [_sds(b.shape, b.dtype) for b in buffers], {i: i for i in range(n)},
                 [pltpu.SemaphoreType.DMA((n, 3)), pltpu.SemaphoreType.DMA((n, 3))], start, wait)


def _relayed_gather(shards):
    n = len(shards)
    buffers = [_sds((N_DEV,) + s.shape, s.dtype) for s in shards]
    dma = pltpu.SemaphoreType.DMA

    def remote(src, dst, send_sem, recv_sem, to):
        return pltpu.make_async_remote_copy(src_ref=src, dst_ref=dst, send_sem=send_sem, recv_sem=recv_sem,
                                            device_id=to, device_id_type=MESH)

    def row(px, py, pc):
        return 4 * px + 2 * py + pc

    def ride(operands, aliases, sems, copies):
        def start(ins, outs, sem_refs):
            local, sent = copies(ins, outs, sem_refs, False)
            for cp in local + sent:
                cp.start()

        def wait(ins, outs, sem_refs):
            local, sent = copies(ins, outs, sem_refs, False)
            for cp in copies(ins, outs, sem_refs, True)[1]:
                cp.wait_recv()
            for cp in sent:
                cp.wait_send()
            for cp in local:
                cp.wait()

        return _Ride(operands, buffers, aliases, sems, start, wait)

    def first(ins, outs, sems, landed):
        x, y, c = _place()
        peers = [(x, y, 1 - c), (1 - x, y, c), (x, 1 - y, c)]
        local = [pltpu.make_async_copy(ins[a], outs[a].at[row(x, y, c)], sems[2].at[a]) for a in range(n)]
        return local, [remote(ins[a], outs[a].at[row(*peer) if landed else row(x, y, c)], sems[0].at[a, k], sems[1].at[a, k], peer)
                       for a in range(n) for k, peer in enumerate(peers)]

    def second(ins, outs, sems, landed):
        x, y, c = _place()
        mine = 1 - c if landed else c
        copies = []
        for a in range(n):
            half = shards[a].shape[0] // 2
            over_x, over_y, diagonal = outs[a].at[row(1 - x, y, mine)], outs[a].at[row(x, 1 - y, mine)], outs[a].at[row(1 - x, 1 - y, c)]
            lower, upper = pl.ds(0, half), pl.ds(half, half)
            copies += [remote(over_x, over_x, sems[0].at[a, 0], sems[1].at[a, 0], (x, y, 1 - c)),
                       remote(over_y, over_y, sems[0].at[a, 1], sems[1].at[a, 1], (x, y, 1 - c))]
            if landed:
                copies += [remote(diagonal.at[lower], diagonal.at[lower], sems[0].at[a, 2], sems[1].at[a, 2], (1 - x, y, c)),
                           remote(diagonal.at[upper], diagonal.at[upper], sems[0].at[a, 3], sems[1].at[a, 3], (x, 1 - y, c))]
            else:
                copies += [remote(over_y.at[lower], over_y.at[lower], sems[0].at[a, 2], sems[1].at[a, 2], (1 - x, y, c)),
                           remote(over_x.at[upper], over_x.at[upper], sems[0].at[a, 3], sems[1].at[a, 3], (x, 1 - y, c))]
        return [], copies

    def third(ins, outs, sems, landed):
        x, y, c = _place()
        return [], [remote(outs[a].at[row(1 - x, 1 - y, 1 - c if landed else c)], outs[a].at[row(1 - x, 1 - y, 1 - c if landed else c)],
                           sems[0].at[a], sems[1].at[a], (x, y, 1 - c)) for a in range(n)]

    def later(copies, n_sems):
        return lambda partly: ride(partly, {i: i for i in range(n)}, [dma((n,) + n_sems), dma((n,) + n_sems)], copies)

    return ride(shards, {}, [dma((n, 3)), dma((n, 3)), dma((n,))], first), later(second, (4,)), later(third, ())


def _sibling_swap_ride(grads, halves=True):
    n = len(grads)

    def copies(ins, outs, sems):
        x, y, c = _place()
        return [pltpu.make_async_remote_copy(
            src_ref=ins[a].at[1 - c] if halves else ins[a], dst_ref=outs[a], send_sem=sems[0].at[a], recv_sem=sems[1].at[a],
            device_id=(x, y, 1 - c), device_id_type=MESH) for a in range(n)]

    def start(ins, outs, sems):
        for cp in copies(ins, outs, sems):
            cp.start()

    def wait(ins, outs, sems):
        for cp in copies(ins, outs, sems):
            cp.wait()

    return _Ride(grads, [_sds(g.shape[1:] if halves else g.shape, g.dtype) for g in grads], {},
                 [pltpu.SemaphoreType.DMA((n,)), pltpu.SemaphoreType.DMA((n,))], start, wait)


def _chip_swap_ride(sums):
    n = len(sums)

    def copies(ins, outs, sems, landed):
        send_sems, recv_sems, local_sems = sems
        x, y, c = _place()
        mine = 2 * x + y
        local = [pltpu.make_async_copy(ins[a].at[mine], outs[a].at[mine], local_sems.at[a]) for a in range(n)]
        remote = [pltpu.make_async_remote_copy(
            src_ref=ins[a].at[2 * px + py], dst_ref=outs[a].at[2 * px + py if landed else mine], send_sem=send_sems.at[a, j],
            recv_sem=recv_sems.at[a, j], device_id=(px, py, c), device_id_type=MESH)
            for a in range(n) for j, (px, py) in enumerate(_other_chips(x, y))]
        return local, remote

    def start(ins, outs, sems):
        local, remote = copies(ins, outs, sems, False)
        for cp in local + remote:
            cp.start()

    def wait(ins, outs, sems):
        local, sent = copies(ins, outs, sems, False)
        for cp in copies(ins, outs, sems, True)[1]:
            cp.wait_recv()
        for cp in sent:
            cp.wait_send()
        for cp in local:
            cp.wait()

    return _Ride(sums, [_sds(s.shape, s.dtype) for s in sums], {},
                 [pltpu.SemaphoreType.DMA((n, 3)), pltpu.SemaphoreType.DMA((n, 3)), pltpu.SemaphoreType.DMA((n,))], start, wait)


def _send_buffers(shards, name):
    n = len(shards)

    def body(*refs):
        for (w, transposed, rows, cols), w_ref, o_ref in zip(shards, refs[:n], refs[n:]):
            if transposed:
                c, r = w.shape
                padded = jnp.concatenate([w_ref[...], jnp.zeros((cols - c, r), F32)], axis=0) if cols > c else w_ref[...]
                o_ref[...] = padded.T.astype(BF16)
            else:
                r, c = w.shape
                if (r, c) != (rows, cols):
                    o_ref[...] = jnp.zeros((rows, cols), BF16)
                o_ref[:r, :c] = w_ref[...].astype(BF16)

    return _pallas_call(body, name=name, out_shape=[_sds((rows, cols), BF16) for _, _, rows, cols in shards])(
        *[w for w, _, _, _ in shards])


def _all_gather(shards, name):
    n = len(shards)
    first, second, third = _relayed_gather(shards)
    levels = [first, second(shards), third(shards)]
    counts = [len(level.sems) for level in levels]

    def body(*refs):
        ins, outs, sems = refs[:n], refs[n:2 * n], refs[2 * n:]
        for i, level in enumerate(levels):
            mine = sems[sum(counts[:i]):sum(counts[:i + 1])]
            level.start(ins, outs, mine)
            level.wait(ins, outs, mine)

    return _pallas_call(
        body, name=name, in_specs=[ANY] * n, out_specs=[ANY] * n, out_shape=first.results,
        scratch_shapes=[s for level in levels for s in level.sems],
    )(*shards)


HBM = pl.BlockSpec(memory_space=pltpu.HBM)
SEMAPHORES = pl.BlockSpec(memory_space=pltpu.SEMAPHORE)
IN_FLIGHT = pltpu.CompilerParams(has_side_effects=pltpu.SideEffectType.DATAFLOW_SIDE_EFFECTING)


def _chip_swap_copies(src_refs, land_refs, send_sems, recv_sems, landed):
    x, y, c = _place()
    return [pltpu.make_async_remote_copy(
        src_ref=src.at[2 * px + py], dst_ref=land.at[2 * px + py if landed else 2 * x + y], send_sem=send_sems.at[3 * a + j],
        recv_sem=recv_sems.at[3 * a + j], device_id=(px, py, c), device_id_type=MESH)
        for a, (src, land) in enumerate(zip(src_refs, land_refs)) for j, (px, py) in enumerate(_other_chips(x, y))]


def _chip_swap_start(sums, name):
    n = len(sums)

    def body(*refs):
        src_refs, land_refs, (send_sems, recv_sems), token = refs[:n], refs[n:2 * n], refs[2 * n:2 * n + 2], refs[-1]
        for cp in _chip_swap_copies(src_refs, land_refs, send_sems, recv_sems, False):
            cp.start()
        token[...] = jnp.zeros_like(token)

    kept = [pltpu.HBM(s.shape, s.dtype) for s in sums]
    out = _pallas_call(
        body, name=name,
        out_shape=[pltpu.SemaphoreType.DMA((3 * n,)), pltpu.SemaphoreType.DMA((3 * n,))] + kept + kept + [_sds((8, 128), F32)],
        in_specs=[HBM] * (2 * n), out_specs=[SEMAPHORES, SEMAPHORES] + [HBM] * (2 * n) + [pl.BlockSpec(memory_space=pltpu.VMEM)],
        input_output_aliases={i: 2 + i for i in range(2 * n)}, compiler_params=IN_FLIGHT,
    )(*[pltpu.with_memory_space_constraint(s, pltpu.HBM) for s in sums],
      *[pltpu.with_memory_space_constraint(lax.empty(s.shape, s.dtype), pltpu.HBM) for s in sums])
    return out[0], out[1], out[2:2 + n], out[2 + n:2 + 2 * n], out[-1]


def _chip_swap_wait(send_sems, recv_sems, sums, landings, after, name):
    n = len(sums)

    def body(*refs):
        src_refs, land_refs, (send_sems, recv_sems) = refs[:n], refs[n:2 * n], refs[2 * n:2 * n + 2]
        for cp in _chip_swap_copies(src_refs, land_refs, send_sems, recv_sems, False):
            cp.wait_send()
        for cp in _chip_swap_copies(src_refs, land_refs, send_sems, recv_sems, True):
            cp.wait_recv()

    out = _pallas_call(
        body, name=name, out_shape=[pltpu.HBM(s.shape, s.dtype) for s in list(sums) + list(landings)],
        in_specs=[HBM] * (2 * n) + [SEMAPHORES, SEMAPHORES] + [ANY] * len(after), out_specs=[HBM] * (2 * n),
        input_output_aliases={i: i for i in range(2 * n)}, compiler_params=IN_FLIGHT,
    )(*sums, *landings, send_sems, recv_sems, *after)
    return out[:n], out[n:]


def _pair_sums(gs, rs, core, name):
    n_arrays = len(gs)

    def body(core_ref, *refs):
        for g_ref, r_ref, o_ref in zip(refs[:n_arrays], refs[n_arrays:2 * n_arrays], refs[2 * n_arrays:]):
            o_ref[...] = (g_ref[...].astype(F32) + r_ref[...].astype(F32)).astype(o_ref.dtype)

    def chip(g):
        return pl.BlockSpec((None,) + g.shape[-2:], lambda p, core_ref: (p, 0, 0))

    def own(g):
        return chip(g) if g.ndim == 3 else pl.BlockSpec((None, None) + g.shape[2:], lambda p, core_ref: (core_ref[0], p, 0, 0))

    return _pallas_call(
        body, name=name,
        grid_spec=pltpu.PrefetchScalarGridSpec(
            num_scalar_prefetch=1, grid=(4,), in_specs=[own(g) for g in gs] + [chip(g) for g in gs],
            out_specs=[chip(g) for g in gs]),
        out_shape=[_sds(g.shape[-3:], g.dtype) for g in gs], compiler_params=_cparams(dimension_semantics=("arbitrary",)),
    )(core, *gs, *rs)


def _adamw_math(w, g, m, v):
    m = ADAM_B1 * m + (1.0 - ADAM_B1) * g
    v = ADAM_B2 * v + (1.0 - ADAM_B2) * (g * g)
    m_hat = m / (1.0 - ADAM_B1 ** ADAM_STEP)
    v_hat = v / (1.0 - ADAM_B2 ** ADAM_STEP)
    return -ADAM_LR * (m_hat / (jnp.sqrt(v_hat) + ADAM_EPS) + ADAM_WD * w), m, v


def _adamw_many(weights, name, ride=None):
    steps = 4
    in_specs, out_specs, out_shape, operands, tiles = [], [], [], [], []
    for w, m, v, parts, own, transposed in weights:
        _, pr, pc = parts.shape
        if transposed:
            c, r = w.shape
            tile = pl.BlockSpec((c, r // steps), lambda i: (0, i))
            part_tile = pl.BlockSpec((4, r // steps, pc), lambda i: (0, i, 0))
            tiles.append((c, r // steps))
        elif w.shape[0] % (8 * steps) == 0:
            r, c = w.shape
            tile = pl.BlockSpec((r // steps, c), lambda i: (i, 0))
            part_tile = pl.BlockSpec((4, r // steps, pc), lambda i: (0, i, 0))
            tiles.append((r // steps, c))
        else:
            tile = pl.BlockSpec(w.shape, lambda i: (0, 0))
            part_tile = pl.BlockSpec(parts.shape, lambda i: (0, 0, 0))
            tiles.append(w.shape)
        in_specs += [tile, tile, tile] + [part_tile] * (1 if own is None else 2)
        out_specs += [tile] * 4
        out_shape += [_sds(w.shape, F32)] * 4
        operands += [w, m, v, parts] + ([] if own is None else [own])
    n_in = len(operands)

    def body(*refs):
        ins, outs = list(refs[:n_in]), refs[n_in:]
        this_chip = 2 * lax.axis_index("x") + lax.axis_index("y")
        for k, (_, _, _, _, own, transposed) in enumerate(weights):
            w_ref, m_ref, v_ref, p_ref = ins[:4]
            own_ref = None if own is None else ins[4]
            del ins[:4 if own is None else 5]
            rows, cols = tiles[k]
            g = None
            for q in range(4):
                index = (q,) if transposed else (q, slice(0, rows), slice(0, cols))
                part = p_ref[index] if own is None else jnp.where(this_chip == q, own_ref[index], p_ref[index])
                g = part.astype(F32) if g is None else g + part.astype(F32)
            if transposed:
                g = g.T[:rows]
            g_out, d_out, m_out, v_out = outs[4 * k:4 * k + 4]
            g_out[...] = g
            d_out[...], m_out[...], v_out[...] = _adamw_math(w_ref[...], g, m_ref[...], v_ref[...])

    return _call(body, name, (steps,), in_specs, out_specs, out_shape, [], operands, ride)


SMALL = ("ssm_a_re", "ssm_a_im", "ssm_log_dt", "ssm_b_re", "ssm_b_im", "ssm_c_re", "ssm_c_im", "ssm_d",
         "ln1_g", "ln1_b", "ln2_g", "ln2_b")


def _pack_rows(arrays):
    rows = []
    for a in arrays:
        flat = a.reshape(-1)
        rows.append(jnp.pad(flat, (0, -flat.shape[0] % 128)).reshape(-1, 128))
    packed = jnp.concatenate(rows, axis=0)
    return jnp.pad(packed, ((0, -packed.shape[0] % 8), (0, 0)))


def _unpack_rows(packed, shapes):
    out, row = [], 0
    for shape in shapes:
        size = math.prod(shape)
        n_rows = -(-size // 128)
        out.append(packed[row:row + n_rows].reshape(-1)[:size].reshape(shape))
        row += n_rows
    return out


def _sum_devices(parts):
    def body(p_ref, o_ref):
        total = p_ref[0]
        for dev in range(1, N_DEV):
            total = total + p_ref[dev]
        o_ref[...] = total

    return _pallas_call(body, name="sum_devices", out_shape=_sds(parts.shape[1:], F32))(parts)


def _adamw_replicated(ws, ms, vs, gs):
    n = len(ws)

    def body(*refs):
        w_refs, m_refs, v_refs, g_refs, d_out, m_out, v_out = (refs[i * n:(i + 1) * n] for i in range(7))
        for i in range(n):
            d_out[i][...], m_out[i][...], v_out[i][...] = _adamw_math(w_refs[i][...], g_refs[i][...], m_refs[i][...], v_refs[i][...])

    out = _pallas_call(body, name="adamw_replicated", out_shape=[_sds(w.shape, F32) for w in ws] * 3,
                       compiler_params=_cparams())(*ws, *ms, *vs, *gs)
    return out[:n], out[n:2 * n], out[2 * n:]


def kernel(x, w_in, b_gate, w_attn_br, w_ssm_br, w_out, ssm_a_re, ssm_a_im, ssm_log_dt, ssm_b_re, ssm_b_im, ssm_c_re, ssm_c_im, ssm_d, w_glu, ln1_g, ln1_b, w_ff_gate, w_ff_up, w_ff_down, ln2_g, ln2_b, loss_target, m_w_in, m_b_gate, m_w_attn_br, m_w_ssm_br, m_w_out, m_ssm_a_re, m_ssm_a_im, m_ssm_log_dt, m_ssm_b_re, m_ssm_b_im, m_ssm_c_re, m_ssm_c_im, m_ssm_d, m_w_glu, m_ln1_g, m_ln1_b, m_w_ff_gate, m_w_ff_up, m_w_ff_down, m_ln2_g, m_ln2_b, v_w_in, v_b_gate, v_w_attn_br, v_w_ssm_br, v_w_out, v_ssm_a_re, v_ssm_a_im, v_ssm_log_dt, v_ssm_b_re, v_ssm_b_im, v_ssm_c_re, v_ssm_c_im, v_ssm_d, v_w_glu, v_ln1_g, v_ln1_b, v_w_ff_gate, v_w_ff_up, v_w_ff_down, v_ln2_g, v_ln2_b):
    given = dict(locals())
    x2, target = x[0], loss_target[0]
    core = lax.axis_index("c").astype(jnp.int32).reshape(1)

    sharded = ("w_in", "w_attn_br", "w_ssm_br", "w_glu", "w_ff_gate", "w_ff_up", "b_gate", "w_out", "w_ff_down")
    send_shape = dict(w_in=(D_MODEL, 896), w_attn_br=(ATTN_WIDTH, 128), w_ssm_br=(SSM_WIDTH, 128), w_glu=(SSM_WIDTH, 128),
                      w_out=(128, D_MODEL), w_ff_gate=(D_MODEL, FF_PAD), w_ff_up=(D_MODEL, FF_PAD), w_ff_down=(FF_PAD, D_MODEL))
    local = {k: given[k][0] for k in sharded}
    narrow = ("w_ff_gate", "w_ff_up")
    def to_send(k):
        return (local[k].T, True, *send_shape[k]) if k in narrow else (local[k], False, *send_shape[k])

    later = [k for k in sharded if k not in ("w_in", "b_gate")]
    sends = dict(zip(["w_in"] + later, _send_buffers([to_send("w_in")], "send_w_in")
                     + _send_buffers([to_send(k) for k in later], "send_weights")))
    sends["b_gate"] = local["b_gate"]
    mixer_weights = ("w_attn_br", "w_ssm_br", "w_glu", "b_gate", "w_out")
    ff_weights = ("w_ff_gate", "w_ff_up", "w_ff_down")
    wt = {}
    wt["w_in"], = _all_gather([sends["w_in"]], "gather_w_in")

    a_re, a_im, log_dt = ssm_a_re[0], ssm_a_im[0], ssm_log_dt[0].reshape(SSM_GROUPS, 1)
    b_re_t, b_im_t = ssm_b_re[0].transpose(0, 2, 1), ssm_b_im[0].transpose(0, 2, 1)
    abar_re, abar_im, e_re, e_im, bbar_re_t, bbar_im_t = _ssm_prep(a_re, a_im, log_dt, b_re_t, b_im_t)
    bmat, cmat, a_chunks = _ssm_tables(abar_re, abar_im, bbar_re_t, bbar_im_t, ssm_c_re[0], ssm_c_im[0])
    cos_t, sin_t = _rope_tables()

    big_mixer, ff_in = [k for k in mixer_weights if k != "b_gate"], ("w_ff_gate", "w_ff_up")
    n_mixer = len(big_mixer)
    mixer_1, mixer_2, mixer_3 = _relayed_gather([sends[k] for k in big_mixer])
    ff_in_1, ff_in_2, ff_in_3 = _relayed_gather([sends[k] for k in ff_in])
    ff_down_1, ff_down_2, ff_down_3 = _relayed_gather([sends["w_ff_down"]])
    proj, *landed = _proj(x2, wt["w_in"], mixer_1 + _gather_first_level([sends["b_gate"]]))
    mixer, bias = landed[:n_mixer], landed[n_mixer:]
    attn, lse, q_pm, k_pm, v_pm, *landed = _attn_fwd(proj, cos_t, sin_t,
                                                     mixer_2(mixer) + _gather_second_level(bias) + ff_in_1)
    mixer, b_gate_full, ff = landed[:n_mixer], landed[n_mixer], landed[n_mixer + 1:]
    ys, states, *landed = _ssm_fwd(proj, bmat, cmat, a_chunks, ssm_d, mixer_3(mixer) + ff_in_2(ff) + ff_down_1)
    wt.update(zip(big_mixer, landed[:n_mixer]))
    ff, ff_down = landed[n_mixer:n_mixer + 2], landed[n_mixer + 2:]
    wt["w_out"] = wt["w_out"].reshape(D_MODEL, D_MODEL)
    h, xhat1, rstd1, glu, y_attn, y_ssm, *landed = _mixer_out(
        attn, ys, proj, x2, wt["w_attn_br"], wt["w_ssm_br"], wt["w_glu"], wt["w_out"], b_gate_full, ln1_g, ln1_b,
        ff_in_3(ff) + ff_down_2(ff_down))
    wt.update(zip(ff_in, landed[:2]))
    ff_a, ff_b, ff_f, w_ff_down = _ff_up(h, wt["w_ff_gate"], wt["w_ff_up"], ff_down_3(landed[2:]))
    wt["w_ff_down"] = w_ff_down.reshape(D_FF_PAD, D_MODEL)
    dr2, d_ln2_g, d_ln2_b, loss_lanes = _ff_down_loss(ff_f, wt["w_ff_down"], h, target, ln2_g, ln2_b)

    def pair_sums(names, contrib, from_sibling):
        return _pair_sums([contrib[k] for k in names], from_sibling, core, "pair_sums_" + names[0])

    d_a, d_b = _ff_down_bwd(dr2, wt["w_ff_down"], ff_a, ff_b)
    contrib = dict(w_ff_gate=_weight_grad(h, d_a, "wgrad_w_ff_gate", FF_PAD),
                   w_ff_up=_weight_grad(h, d_b, "wgrad_w_ff_up", FF_PAD),
                   w_ff_down=_weight_grad(ff_f, dr2, "wgrad_w_ff_down"))
    dr1, d_ln1_g, d_ln1_b, *from_sibling = _ff_up_bwd(
        d_a, d_b, wt["w_ff_gate"], wt["w_ff_up"], dr2, xhat1, rstd1, ln1_g, _sibling_swap_ride([contrib[k] for k in ff_weights]))
    ff_sums = pair_sums(ff_weights, contrib, from_sibling)

    d_ya, d_yssm, d_proj, d_attn, d_glu, d_ys, mixed, y_s, gy, d_bg = _mixer_bwd(
        dr1, proj, y_attn, y_ssm, glu, ys, wt["w_attn_br"], wt["w_ssm_br"], wt["w_glu"], wt["w_out"], b_gate_full)
    contrib.update(w_attn_br=_weight_grad(attn, d_ya, "wgrad_w_attn_br", 128),
                   w_ssm_br=_weight_grad(y_s, d_yssm, "wgrad_w_ssm_br", 128),
                   w_glu=_weight_grad(gy, d_glu, "wgrad_w_glu", 128),
                   w_out=_weight_grad(mixed, dr1, "wgrad_w_out"),
                   b_gate=d_bg.reshape(2, 4, 2, 128).transpose(2, 1, 0, 3))
    d_proj, *landed = _attn_bwd(q_pm, k_pm, v_pm, cos_t, sin_t, attn, lse, d_attn, d_proj,
                                _chip_swap_ride(ff_sums) + _sibling_swap_ride([contrib[k] for k in mixer_weights]))
    parts, own_sums = dict(zip(ff_weights, landed[:len(ff_weights)])), {}
    mixer_sums = pair_sums(mixer_weights, contrib, landed[len(ff_weights):])
    d_proj, d_bmat, d_cmat, d_abar, d_skip, *landed = _ssm_bwd(d_ys, proj, states, bmat, cmat, a_chunks, ssm_d, d_proj,
                                                               _chip_swap_ride(mixer_sums))
    parts.update(zip(mixer_weights, landed))

    gbb_re_t, gbb_im_t = _block_diag_parts(d_bmat, True)
    gc_re, gc_im = _block_diag_parts(d_cmat, False)
    ga_re = d_abar[:, 0, :CHUNK_STATES].reshape(SSM_GROUPS, SSM_STATE)
    ga_im = d_abar[:, 0, CHUNK_STATES:].reshape(SSM_GROUPS, SSM_STATE)
    g_a_re, g_a_im, g_log_dt, g_b_re_t, g_b_im_t = _ssm_param_bwd(
        a_re, a_im, log_dt, b_re_t, b_im_t, abar_re, abar_im, e_re, e_im, ga_re, ga_im, gbb_re_t, gbb_im_t)
    mine = [g_a_re, g_a_im, g_log_dt, g_b_re_t, g_b_im_t, gc_re, -gc_im,
            d_skip, d_ln1_g, d_ln1_b, d_ln2_g, d_ln2_b]
    small_packed = _pack_rows(mine + [loss_lanes])

    for_sibling, small_partly = _weight_grad_of_core(x2, d_proj, 1 - core, "wgrad_w_in_for_sibling", 896,
                                                     _gather_first_level([small_packed]))
    own_half, from_sibling, every = _weight_grad_of_core(
        x2, d_proj, core, "wgrad_w_in_own", 896,
        _sibling_swap_ride([for_sibling], halves=False) + _gather_second_level([small_partly]))
    w_in_sum, = _pair_sums([own_half], [from_sibling], core, "pair_sums_w_in")
    send_sems, recv_sems, w_in_sum, landing, token = _chip_swap_start([w_in_sum], "w_in_chip_swap_start")

    def adamw_of(k):
        taken = (lambda a: a.T) if k in narrow else (lambda a: a)
        return taken(local[k]), taken(given["m_" + k][0]), taken(given["v_" + k][0]), parts[k], own_sums.get(k), k in narrow

    others = [k for k in sharded if k != "w_in"]
    updated = _adamw_many([adamw_of(k) for k in others], "adamw_others", _after(token))
    grad_x, = _grad_x(d_proj, wt["w_in"], dr1, _after(token))

    def held(k, a):
        return a.transpose(0, 1, 3, 2) if k in ("ssm_b_re", "ssm_b_im") else a

    *small_grads, loss_sum = _unpack_rows(_sum_devices(every), [held(k, given[k]).shape for k in SMALL] + [(1, 128)])
    small = _adamw_replicated([held(k, given[k]) for k in SMALL], [held(k, given["m_" + k]) for k in SMALL],
                              [held(k, given["v_" + k]) for k in SMALL], small_grads)
    loss = loss_sum[0, 0]

    (own_sums["w_in"],), (parts["w_in"],) = _chip_swap_wait(
        send_sems, recv_sems, w_in_sum, landing, [grad_x, updated[0], small[0][0]], "w_in_chip_swap_wait")
    updated += _adamw_many([adamw_of("w_in")], "adamw_w_in")

    grads, deltas, new_m, new_v = {}, {}, {}, {}
    for i, k in enumerate(others + ["w_in"]):
        out = [o.T if k in narrow else o for o in updated[4 * i:4 * i + 4]]
        grads[k], deltas[k], new_m[k], new_v[k] = (o.reshape((1,) + local[k].shape) for o in out)
    for res, values in zip((grads, deltas, new_m, new_v), (small_grads,) + small):
        res.update((k, held(k, a)) for k, a in zip(SMALL, values))

    order = ("w_in", "b_gate", "w_attn_br", "w_ssm_br", "w_out", "ssm_a_re", "ssm_a_im", "ssm_log_dt", "ssm_b_re", "ssm_b_im",
             "ssm_c_re", "ssm_c_im", "ssm_d", "w_glu", "ln1_g", "ln1_b", "w_ff_gate", "w_ff_up", "w_ff_down", "ln2_g", "ln2_b")
    return (loss, grad_x[None], *[grads[k] for k in order], *[deltas[k] for k in order], *[new_m[k] for k in order],
            *[new_v[k] for k in order])
```

```python
import functools
import math

import jax
import jax.numpy as jnp
import numpy as np
from jax import lax
from jax.experimental import pallas as pl
from jax.experimental.pallas import tpu as pltpu

F32 = jnp.float32
BF16 = jnp.bfloat16

N_DEV = 8
SEQ = 2048
D_MODEL = 1024
HEAD_DIM = 64
ATTN_WIDTH = 512
QKV_WIDTH = 1536
SSM_WIDTH = 512
SSM_GROUPS = 32
SSM_GROUP = 16
SSM_STATE = 64
IN_WIDTH = 7168
D_FF = 2816
FF_SHARD = D_FF // N_DEV
FF_PAD = 384
D_FF_PAD = FF_PAD * N_DEV
DN_ALPHA = 2.0 ** 0.25
LN_EPS = 1e-5
NEG_INF = -1e30
ROPE_THETA = 10000.0
BLOCK = 128
GROUPS = ((1, 16), (4, 4), (16, 1))

ADAM_LR = 0.001
ADAM_B1 = 0.9
ADAM_B2 = 0.999
ADAM_EPS = 1e-08
ADAM_WD = 0.01
ADAM_STEP = 10

VMEM_LIMIT = 56 * 1024 * 1024


_pallas_call = pl.pallas_call


def _cparams(**kw):
    return pltpu.CompilerParams(vmem_limit_bytes=VMEM_LIMIT, **kw)


def _dot(a, b):
    return jnp.dot(a, b, preferred_element_type=F32)


def _dot_nt(a, b):
    return lax.dot_general(a, b, (((1,), (1,)), ((), ())), preferred_element_type=F32)


def _side_by_side(w_ref, row=None):
    rows = slice(None) if row is None else pl.ds(row, 1)
    return jnp.concatenate([w_ref[i, rows, :] for i in range(w_ref.shape[0])], axis=1)


def _dot_tn(a, b):
    return lax.dot_general(a, b, (((0,), (0,)), ((), ())), preferred_element_type=F32)


def _rope_tables():
    half = HEAD_DIM // 2
    inv_freq = np.float32(ROPE_THETA) ** (-np.arange(half, dtype=np.float32) / np.float32(half))
    ang = np.arange(SEQ, dtype=np.float32)[:, None] * inv_freq[None, :]
    cos, sin = np.cos(ang).astype(np.float32), np.sin(ang).astype(np.float32)
    tables = np.tile(cos, (1, 4)), np.tile(np.concatenate([-sin, sin], axis=1), (1, 2))

    def by_phase(t):
        return np.stack([t.reshape(SEQ // d, d, 128).transpose(1, 0, 2).reshape(SEQ, 128) for d, _ in GROUPS])

    return jnp.asarray(by_phase(tables[0])), jnp.asarray(by_phase(tables[1]))


def _swap_halves(x):
    lane = lax.broadcasted_iota(jnp.int32, x.shape, 1)
    return jnp.where((lane & 63) < 32, pltpu.roll(x, 96, axis=1), pltpu.roll(x, 32, axis=1))


def _group_rows(d, nb, r, i):
    src = pl.ds(i * BLOCK, BLOCK) if d == 1 else pl.ds(r + i * BLOCK * d, BLOCK, stride=d)
    return src, pl.ds((r * nb + i) * BLOCK, BLOCK)


def _attn_masks():
    a_idx = lax.broadcasted_iota(jnp.int32, (2 * BLOCK, 2 * BLOCK), 0) & (BLOCK - 1)
    c_idx = lax.broadcasted_iota(jnp.int32, (2 * BLOCK, 2 * BLOCK), 1)
    cur_ok = jnp.logical_and(c_idx >= BLOCK, c_idx - BLOCK <= a_idx)
    prev_ok = jnp.logical_and(c_idx < BLOCK, c_idx >= a_idx)
    lane = lax.broadcasted_iota(jnp.int32, (BLOCK, 128), 1)
    return cur_ok, prev_ok, lane < HEAD_DIM


def _stack_heads(t, head0):
    zero = jnp.zeros_like(t)
    return jnp.concatenate([jnp.where(head0, t, zero), jnp.where(head0, zero, t)], axis=0)


def _unstack_heads(t2, head0):
    return jnp.where(head0, t2[:BLOCK], t2[BLOCK:])


def _attn_fwd(proj, cos_t, sin_t, ride=None):
    def body(q0, q1, q2, k0, k1, k2, v0, v1, v2, cos_ref, sin_ref, attn_ref, lse_ref, qpm_ref, kpm_ref, vpm_ref,
             qs, ks, vs, os_, ms, ls, acc, mnat, lnat):
        cur_ok, prev_ok, head0 = _attn_masks()
        ks[:BLOCK, :] = jnp.zeros((BLOCK, 128), BF16)
        vs[:BLOCK, :] = jnp.zeros((BLOCK, 128), BF16)
        for g, (d, nb) in enumerate(GROUPS):
            q_ref, k_ref, v_ref = (q0, q1, q2)[g], (k0, k1, k2)[g], (v0, v1, v2)[g]
            for r in range(d):
                for i in range(nb):
                    src, dst = _group_rows(d, nb, r, i)
                    below = pl.ds(dst.start + BLOCK, BLOCK)
                    c, s = cos_ref[g, dst, :], sin_ref[g, dst, :]
                    q = q_ref[src, :]
                    k = k_ref[src, :]
                    qs[dst, :] = ((q * c + _swap_halves(q) * s) * 0.125).astype(BF16)
                    ks[below, :] = (k * c + _swap_halves(k) * s).astype(BF16)
                    vs[below, :] = v_ref[src, :].astype(BF16)
                    qpm_ref[g, dst, :], kpm_ref[g, dst, :], vpm_ref[g, dst, :] = qs[dst, :], ks[below, :], vs[below, :]

            def block(b, carry, nb=nb):
                has_prev = (b & (nb - 1)) > 0
                cur = pl.ds(pl.multiple_of(b * BLOCK, BLOCK), BLOCK)
                window = pl.ds(pl.multiple_of(b * BLOCK, BLOCK), 2 * BLOCK)
                valid = jnp.logical_or(cur_ok, jnp.logical_and(prev_ok, has_prev))
                s = jnp.where(valid, _dot_nt(_stack_heads(qs[cur, :], head0), ks[window, :]), NEG_INF)
                m = jnp.max(s, axis=1, keepdims=True)
                p = jnp.exp(s - m)
                os_[cur, :] = _unstack_heads(_dot(p.astype(BF16), vs[window, :]), head0)
                ms[cur, :] = _unstack_heads(m, head0)
                ls[cur, :] = _unstack_heads(jnp.sum(p, axis=1, keepdims=True), head0)
                return carry

            lax.fori_loop(0, SEQ // BLOCK, block, 0, unroll=16)

            for r in range(d):
                for i in range(nb):
                    src, dst = _group_rows(d, nb, r, i)
                    if g == 0:
                        acc[src, :], mnat[src, :], lnat[src, :] = os_[dst, :], ms[dst, :], ls[dst, :]
                    else:
                        m_old, m_g = mnat[src, :], ms[dst, :]
                        m_new = jnp.maximum(m_old, m_g)
                        a_old, a_g = jnp.exp(m_old - m_new), jnp.exp(m_g - m_new)
                        acc[src, :] = a_old * acc[src, :] + a_g * os_[dst, :]
                        lnat[src, :] = a_old * lnat[src, :] + a_g * ls[dst, :]
                        mnat[src, :] = m_new
        for i in range(SEQ // BLOCK):
            rows = pl.ds(i * BLOCK, BLOCK)
            l = lnat[rows, :]
            attn_ref[rows, :] = acc[rows, :] / l
            lse_ref[rows, :] = mnat[rows, :] + jnp.log(l)

    def col(base):
        return pl.BlockSpec((SEQ, 128), lambda hp, base=base: (0, base + hp))

    in_specs = [col(g * 4) for g in range(3)] + [col(12 + g * 4) for g in range(3)] + [col(24 + g * 4) for g in range(3)]
    table = pl.BlockSpec((3, SEQ, 128), lambda hp: (0, 0, 0), pipeline_mode=pl.Buffered(1))
    out = pl.BlockSpec((SEQ, 128), lambda hp: (0, hp))
    by_phase = pl.BlockSpec((3, SEQ, 128), lambda hp: (0, 0, hp))
    return _call(
        body, "attn_fwd", (4,), in_specs + [table, table], [out, out] + [by_phase] * 3,
        [_sds((SEQ, ATTN_WIDTH), F32), _sds((SEQ, ATTN_WIDTH), F32)] + [_sds((3, SEQ, ATTN_WIDTH), BF16)] * 3,
        [pltpu.VMEM((SEQ, 128), BF16)] + [pltpu.VMEM((SEQ + BLOCK, 128), BF16)] * 2 + [pltpu.VMEM((SEQ, 128), F32)] * 6,
        [proj] * 9 + [cos_t, sin_t], ride)


def _attn_bwd_group_body(g):
    d, nb = GROUPS[g]

    def body(qs_ref, ks_ref, vs_ref, cos_ref, sin_ref, lse_ref, dattn_ref, dsum_ref, dproj_ref,
             ks, vs, dos, lss, dss, dqs, dks, dvs, stage, outs, sems):
        cur_ok, prev_ok, head0 = _attn_masks()
        qs = qs_ref.at[g]
        ks[:BLOCK, :] = jnp.zeros((BLOCK, 128), BF16)
        vs[:BLOCK, :] = jnp.zeros((BLOCK, 128), BF16)
        dks[:BLOCK, :] = jnp.zeros((BLOCK, 128), F32)
        dvs[:BLOCK, :] = jnp.zeros((BLOCK, 128), F32)
        for r in range(d):
            for i in range(nb):
                src, dst = _group_rows(d, nb, r, i)
                below = pl.ds(dst.start + BLOCK, BLOCK)
                ks[below, :] = ks_ref[g, dst, :]
                vs[below, :] = vs_ref[g, dst, :]
                dos[dst, :] = dattn_ref[src, :].astype(BF16)
                for per_head, spread in ((dsum_ref[src, :], dss), (lse_ref[src, :], lss)):
                    other = pltpu.roll(per_head, HEAD_DIM, axis=1)
                    spread[0, dst, :] = jnp.where(head0, per_head, other)
                    spread[1, dst, :] = jnp.where(head0, other, per_head)
                dks[below, :] = jnp.zeros((BLOCK, 128), F32)
                dvs[below, :] = jnp.zeros((BLOCK, 128), F32)

        def per_stacked_row(spread, cur):
            h0, h1 = spread[0, cur, :], spread[1, cur, :]
            return jnp.concatenate([jnp.concatenate([h0, h0], axis=1), jnp.concatenate([h1, h1], axis=1)], axis=0)

        def block(b, carry):
            has_prev = (b & (nb - 1)) > 0
            cur = pl.ds(pl.multiple_of(b * BLOCK, BLOCK), BLOCK)
            window = pl.ds(pl.multiple_of(b * BLOCK, BLOCK), 2 * BLOCK)
            valid = jnp.logical_or(cur_ok, jnp.logical_and(prev_ok, has_prev))
            q2, do2 = _stack_heads(qs[cur, :], head0), _stack_heads(dos[cur, :], head0)
            kw, vw = ks[window, :], vs[window, :]
            s = jnp.where(valid, _dot_nt(q2, kw), NEG_INF)
            p = jnp.exp(s - per_stacked_row(lss, cur))
            ds = (p * (_dot_nt(do2, vw) - per_stacked_row(dss, cur))).astype(BF16)
            dvs[window, :] += _dot_tn(p.astype(BF16), do2)
            dks[window, :] += _dot_tn(ds, q2)
            dqs[cur, :] = _unstack_heads(_dot(ds, kw), head0)
            return carry

        lax.fori_loop(0, SEQ // BLOCK, block, 0, unroll=16)

        hp = pl.program_id(0)
        copies = []
        for kind in range(3):
            for r in range(d):
                for i in range(nb):
                    src, dst = _group_rows(d, nb, r, i)
                    below = pl.ds(dst.start + BLOCK, BLOCK)
                    if kind == 2:
                        stage[src, :] = dvs[below, :]
                    else:
                        c, s = cos_ref[g, dst, :], sin_ref[g, dst, :]
                        t = dqs[dst, :] * 0.125 if kind == 0 else dks[below, :]
                        stage[src, :] = t * c - _swap_halves(t) * s
            for i in range(SEQ // MM_ROWS):
                rows = pl.ds(i * MM_ROWS, MM_ROWS)
                outs[kind, rows, :] = stage[rows, :].astype(BF16)
            column = pl.multiple_of((kind * 12 + g * 4 + hp) * 128, 128)
            copies.append(pltpu.make_async_copy(outs.at[kind], dproj_ref.at[:, pl.ds(column, 128)], sems.at[kind]))
            copies[-1].start()
        for cp in copies:
            cp.wait()

    return body


def _attn_bwd(q_pm, k_pm, v_pm, cos_t, sin_t, attn, lse, dattn, dproj, ride=None):
    groups = [_attn_bwd_group_body(g) for g in range(3)]

    def body(qs_ref, ks_ref, vs_ref, cos_ref, sin_ref, attn_ref, lse_ref, dattn_ref, dproj_in, dproj_ref, dsum, *scratch):
        del dproj_in
        head0 = _attn_masks()[2]
        for i in range(SEQ // BLOCK):
            rows = pl.ds(i * BLOCK, BLOCK)
            prod = dattn_ref[rows, :] * attn_ref[rows, :]
            d0 = jnp.sum(jnp.where(head0, prod, 0.0), axis=1, keepdims=True)
            d1 = jnp.sum(jnp.where(head0, 0.0, prod), axis=1, keepdims=True)
            dsum[rows, :] = jnp.where(head0, d0, d1)
        for g in range(3):
            groups[g](qs_ref, ks_ref, vs_ref, cos_ref, sin_ref, lse_ref, dattn_ref, dsum, dproj_ref, *scratch)

    def col(base):
        return pl.BlockSpec((SEQ, 128), lambda hp, base=base: (0, base + hp))

    table = pl.BlockSpec((3, SEQ, 128), lambda hp: (0, 0, 0), pipeline_mode=pl.Buffered(1))
    by_phase = pl.BlockSpec((3, SEQ, 128), lambda hp: (0, 0, hp))
    return _call(
        body, "attn_bwd", (4,), [by_phase] * 3 + [table, table, col(0), col(0), col(0), ANY],
        [ANY], [_sds((SEQ, IN_WIDTH), BF16)],
        [pltpu.VMEM((SEQ, 128), F32)]
        + [pltpu.VMEM((SEQ + BLOCK, 128), BF16)] * 2 + [pltpu.VMEM((SEQ, 128), BF16)]
        + [pltpu.VMEM((2, SEQ, 128), F32)] * 2 + [pltpu.VMEM((SEQ, 128), F32)]
        + [pltpu.VMEM((SEQ + BLOCK, 128), F32)] * 2 + [pltpu.VMEM((SEQ, 128), F32)]
        + [pltpu.VMEM((3, SEQ, 128), BF16), pltpu.SemaphoreType.DMA((3,))],
        [q_pm, k_pm, v_pm, cos_t, sin_t, attn, lse, dattn, dproj], ride, aliases={8: 0})


SSM_CHUNKS = 4
CHUNK_STATES = 512
SCAN_ROWS = 8
U_COL = (3 * QKV_WIDTH) // 128


def _cmul(xr, xi, yr, yi):
    return xr * yr - xi * yi, xr * yi + xi * yr


def _ssm_prep(a_re, a_im, log_dt, b_re_t, b_im_t):
    def body(ar_ref, ai_ref, ldt_ref, br_ref, bi_ref, abr_ref, abi_ref, er_ref, ei_ref, bbr_ref, bbi_ref):
        ar, ai = ar_ref[...], ai_ref[...]
        dt = jnp.exp(ldt_ref[...])
        mag = jnp.exp(ar * dt)
        abr, abi = mag * jnp.cos(ai * dt), mag * jnp.sin(ai * dt)
        den = ar * ar + ai * ai
        nr, ni = abr - 1.0, abi
        er, ei = (nr * ar + ni * ai) / den, (ni * ar - nr * ai) / den
        abr_ref[...], abi_ref[...], er_ref[...], ei_ref[...] = abr, abi, er, ei
        er3, ei3 = er[:, None, :], ei[:, None, :]
        br, bi = br_ref[...], bi_ref[...]
        bbr_ref[...] = er3 * br - ei3 * bi
        bbi_ref[...] = er3 * bi + ei3 * br

    gp = jax.ShapeDtypeStruct(a_re.shape, F32)
    gb = jax.ShapeDtypeStruct(b_re_t.shape, F32)
    return _pallas_call(body, name="ssm_prep", out_shape=(gp, gp, gp, gp, gb, gb))(a_re, a_im, log_dt, b_re_t, b_im_t)


def _ssm_param_bwd(a_re, a_im, log_dt, b_re_t, b_im_t, abar_re, abar_im, e_re, e_im, ga_re, ga_im, gbb_re_t, gbb_im_t):
    def body(ar_ref, ai_ref, ldt_ref, br_ref, bi_ref, abr_ref, abi_ref, er_ref, ei_ref, gar_ref, gai_ref, gbr_ref, gbi_ref,
             o_ar, o_ai, o_ldt, o_br, o_bi):
        ar, ai = ar_ref[...], ai_ref[...]
        dt = jnp.exp(ldt_ref[...])
        er, ei = er_ref[...], ei_ref[...]
        br, bi, gbr, gbi = br_ref[...], bi_ref[...], gbr_ref[...], gbi_ref[...]
        er3, ei3 = er[:, None, :], ei[:, None, :]
        o_br[...] = er3 * gbr + ei3 * gbi
        o_bi[...] = er3 * gbi - ei3 * gbr
        ge_r = jnp.sum(br * gbr + bi * gbi, axis=1)
        ge_i = jnp.sum(br * gbi - bi * gbr, axis=1)
        den = ar * ar + ai * ai
        ilr, ili = ar / den, -ai / den
        t_r, t_i = _cmul(ilr, -ili, ge_r, ge_i)
        gab_r, gab_i = gar_ref[...] + t_r, gai_ref[...] + t_i
        gz_r, gz_i = _cmul(abr_ref[...], -abi_ref[...], gab_r, gab_i)
        el_r, el_i = _cmul(er, ei, ilr, ili)
        u_r, u_i = _cmul(el_r, -el_i, ge_r, ge_i)
        o_ar[...] = dt * gz_r - u_r
        o_ai[...] = dt * gz_i - u_i
        o_ldt[...] = jnp.sum(gz_r * ar + gz_i * ai, axis=1, keepdims=True) * dt

    gp = jax.ShapeDtypeStruct(a_re.shape, F32)
    gb = jax.ShapeDtypeStruct(b_re_t.shape, F32)
    return _pallas_call(body, name="ssm_param_bwd", out_shape=(gp, gp, jax.ShapeDtypeStruct(log_dt.shape, F32), gb, gb))(
        a_re, a_im, log_dt, b_re_t, b_im_t, abar_re, abar_im, e_re, e_im, ga_re, ga_im, gbb_re_t, gbb_im_t)


def _block_diag(blocks_re, blocks_im, sign_im, rows_are_channels):
    both = jnp.stack([blocks_re, sign_im * blocks_im]).reshape(2, SSM_CHUNKS, 8, SSM_GROUP, SSM_STATE)
    eye = jnp.eye(8, dtype=F32)
    if rows_are_channels:
        return jnp.einsum("rcghp,gk->cghrkp", both, eye).reshape(SSM_CHUNKS, 128, 2 * CHUNK_STATES)
    return jnp.einsum("rcghp,gk->crkpgh", both, eye).reshape(SSM_CHUNKS, 2 * CHUNK_STATES, 128)


def _block_diag_parts(mat, rows_are_channels):
    if rows_are_channels:
        six = mat.reshape(SSM_CHUNKS, 8, SSM_GROUP, 2, 8, SSM_STATE)
        parts = jnp.einsum("cghrgp->rcghp", six)
    else:
        six = mat.reshape(SSM_CHUNKS, 2, 8, SSM_STATE, 8, SSM_GROUP)
        parts = jnp.einsum("crgpgh->rcghp", six)
    parts = parts.reshape(2, SSM_GROUPS, SSM_GROUP, SSM_STATE)
    return parts[0], parts[1]


def _scan_consts(a_ref, conj, reverse):
    ar = jnp.broadcast_to(a_ref[:, :CHUNK_STATES], (SCAN_ROWS, CHUNK_STATES))
    ai = jnp.broadcast_to(a_ref[:, CHUNK_STATES:], (SCAN_ROWS, CHUNK_STATES))
    if conj:
        ai = -ai
    row = lax.broadcasted_iota(jnp.int32, (SCAN_ROWS, CHUNK_STATES), 0)
    if reverse:
        row = SCAN_ROWS - 1 - row
    zero = jnp.zeros_like(ar)
    steps = []
    pr, pi = ar, ai
    for shift in (1, 2, 4):
        keep = row >= shift
        steps.append((SCAN_ROWS - shift if reverse else shift, jnp.where(keep, pr, zero), jnp.where(keep, pi, zero)))
        pr, pi = _cmul(pr, pi, pr, pi)
    first = row == 0
    return steps, (jnp.where(first, ar, zero), jnp.where(first, ai, zero)), first


def _scan_tile(xr, xi, prev_r, prev_i, steps, carry_in, reverse):
    edge = SCAN_ROWS - 1 if reverse else 1
    cr, ci = pltpu.roll(prev_r, edge, axis=0), pltpu.roll(prev_i, edge, axis=0)
    xr, xi = xr + carry_in[0] * cr - carry_in[1] * ci, xi + carry_in[0] * ci + carry_in[1] * cr
    for shift, mr, mi in steps:
        sr, si = pltpu.roll(xr, shift, axis=0), pltpu.roll(xi, shift, axis=0)
        xr, xi = xr + mr * sr - mi * si, xi + mr * si + mi * sr
    return xr, xi


MM_ROWS = 256


def _ssm_fwd(proj, bmat, cmat, a_chunks, d_skip, ride=None):
    def body(u_ref, b_ref, c_ref, a_ref, d_ref, y_ref, states_ref, h_ref):
        for i in range(SEQ // MM_ROWS):
            rows = pl.ds(i * MM_ROWS, MM_ROWS)
            h_ref[rows, :] = _dot(u_ref[rows, :].astype(BF16), b_ref[...])
        steps, carry_in, _ = _scan_consts(a_ref, conj=False, reverse=False)

        def tile(k, carry):
            rows = pl.ds(pl.multiple_of(k * SCAN_ROWS, SCAN_ROWS), SCAN_ROWS)
            xr, xi = _scan_tile(h_ref[rows, :CHUNK_STATES], h_ref[rows, CHUNK_STATES:], carry[0], carry[1], steps, carry_in, False)
            h_ref[rows, :CHUNK_STATES] = xr
            h_ref[rows, CHUNK_STATES:] = xi
            return xr, xi

        zero = jnp.zeros((SCAN_ROWS, CHUNK_STATES), F32)
        lax.fori_loop(0, SEQ // SCAN_ROWS, tile, (zero, zero), unroll=4)
        for i in range(SEQ // MM_ROWS):
            rows = pl.ds(i * MM_ROWS, MM_ROWS)
            states = h_ref[rows, :].astype(BF16)
            states_ref[rows, :] = states
            y_ref[rows, :] = _dot(states, c_ref[...]) + d_ref[...] * u_ref[rows, :]

    return _call(
        body, "ssm_fwd", (SSM_CHUNKS,),
        [pl.BlockSpec((SEQ, 128), lambda c: (0, U_COL + c)),
         pl.BlockSpec((None, 128, 2 * CHUNK_STATES), lambda c: (c, 0, 0)),
         pl.BlockSpec((None, 2 * CHUNK_STATES, 128), lambda c: (c, 0, 0)),
         pl.BlockSpec((None, 1, 2 * CHUNK_STATES), lambda c: (c, 0, 0)),
         pl.BlockSpec((1, 128), lambda c: (0, c))],
        [pl.BlockSpec((SEQ, 128), lambda c: (0, c)), pl.BlockSpec((SEQ, 2 * CHUNK_STATES), lambda c: (0, c))],
        [_sds((SEQ, SSM_WIDTH), F32), _sds((SEQ, SSM_CHUNKS * 2 * CHUNK_STATES), BF16)],
        [pltpu.VMEM((SEQ, 2 * CHUNK_STATES), F32)],
        [proj, bmat, cmat, a_chunks, d_skip], ride)


def _ssm_bwd(dys, proj, h, bmat, cmat, a_chunks, d_skip, dproj, ride=None):
    def body(dy_ref, u_ref, states_ref, b_ref, c_ref, a_ref, d_ref, dproj_in, du_ref, db_ref, dc_ref, da_ref, dd_ref,
             g_ref, h_ref):
        del dproj_in
        dsum = jnp.zeros((1, 128), F32)
        dcm = jnp.zeros((2 * CHUNK_STATES, 128), F32)
        for i in range(SEQ // MM_ROWS):
            rows = pl.ds(i * MM_ROWS, MM_ROWS)
            h_ref[rows, :] = states_ref[rows, :].astype(F32)
            dy = dy_ref[rows, :]
            g_ref[rows, :] = _dot_nt(dy.astype(BF16), c_ref[...])
            dsum += jnp.sum(dy * u_ref[rows, :], axis=0, keepdims=True)
            dcm += _dot_tn(states_ref[rows, :], dy.astype(BF16))
        dd_ref[...] = dsum
        dc_ref[...] = dcm
        steps, carry_in, _ = _scan_consts(a_ref, conj=True, reverse=True)
        first_row = lax.broadcasted_iota(jnp.int32, (SCAN_ROWS, CHUNK_STATES), 0) == 0
        n_tiles = SEQ // SCAN_ROWS

        def tile(j, carry):
            k = n_tiles - 1 - j
            rows = pl.ds(pl.multiple_of(k * SCAN_ROWS, SCAN_ROWS), SCAN_ROWS)
            before = pl.ds(pl.multiple_of(jnp.maximum(k - 1, 0) * SCAN_ROWS, SCAN_ROWS), SCAN_ROWS)
            gr, gi = _scan_tile(g_ref[rows, :CHUNK_STATES], g_ref[rows, CHUNK_STATES:], carry[0], carry[1], steps, carry_in, True)
            g_ref[rows, :CHUNK_STATES] = gr
            g_ref[rows, CHUNK_STATES:] = gi
            has_before = jnp.where(k > 0, 1.0, 0.0)
            hr = jnp.where(first_row, pltpu.roll(h_ref[before, :CHUNK_STATES], 1, axis=0) * has_before,
                           pltpu.roll(h_ref[rows, :CHUNK_STATES], 1, axis=0))
            hi = jnp.where(first_row, pltpu.roll(h_ref[before, CHUNK_STATES:], 1, axis=0) * has_before,
                           pltpu.roll(h_ref[rows, CHUNK_STATES:], 1, axis=0))
            return gr, gi, carry[2] + hr * gr + hi * gi, carry[3] + hr * gi - hi * gr

        zero = jnp.zeros((SCAN_ROWS, CHUNK_STATES), F32)
        _, _, sar, sai = lax.fori_loop(0, n_tiles, tile, (zero, zero, zero, zero), unroll=4)
        da_ref[:, :CHUNK_STATES] = jnp.sum(sar, axis=0, keepdims=True)
        da_ref[:, CHUNK_STATES:] = jnp.sum(sai, axis=0, keepdims=True)
        dbm = jnp.zeros((128, 2 * CHUNK_STATES), F32)
        for i in range(SEQ // MM_ROWS):
            rows = pl.ds(i * MM_ROWS, MM_ROWS)
            g = g_ref[rows, :].astype(BF16)
            du_ref[rows, :] = (_dot_nt(g, b_ref[...]) + d_ref[...] * dy_ref[rows, :]).astype(BF16)
            dbm += _dot_tn(u_ref[rows, :].astype(BF16), g)
        db_ref[...] = dbm

    chunk_col = pl.BlockSpec((SEQ, 128), lambda c: (0, c))
    return _call(
        body, "ssm_bwd", (SSM_CHUNKS,),
        [chunk_col,
         pl.BlockSpec((SEQ, 128), lambda c: (0, U_COL + c)),
         pl.BlockSpec((SEQ, 2 * CHUNK_STATES), lambda c: (0, c)),
         pl.BlockSpec((None, 128, 2 * CHUNK_STATES), lambda c: (c, 0, 0)),
         pl.BlockSpec((None, 2 * CHUNK_STATES, 128), lambda c: (c, 0, 0)),
         pl.BlockSpec((None, 1, 2 * CHUNK_STATES), lambda c: (c, 0, 0)),
         pl.BlockSpec((1, 128), lambda c: (0, c)), ANY],
        [pl.BlockSpec((SEQ, 128), lambda c: (0, U_COL + c)),
         pl.BlockSpec((None, 128, 2 * CHUNK_STATES), lambda c: (c, 0, 0)),
         pl.BlockSpec((None, 2 * CHUNK_STATES, 128), lambda c: (c, 0, 0)),
         pl.BlockSpec((None, 1, 2 * CHUNK_STATES), lambda c: (c, 0, 0)),
         pl.BlockSpec((1, 128), lambda c: (0, c))],
        [_sds((SEQ, IN_WIDTH), BF16), _sds((SSM_CHUNKS, 128, 2 * CHUNK_STATES), F32),
         _sds((SSM_CHUNKS, 2 * CHUNK_STATES, 128), F32), _sds((SSM_CHUNKS, 1, 2 * CHUNK_STATES), F32), _sds((1, SSM_WIDTH), F32)],
        [pltpu.VMEM((SEQ, 2 * CHUNK_STATES), F32)] * 2, [dys, proj, h, bmat, cmat, a_chunks, d_skip, dproj], ride, aliases={7: 0})


def _ssm_tables(abar_re, abar_im, bbar_re_t, bbar_im_t, c_re, c_im):
    bmat = _block_diag(bbar_re_t, bbar_im_t, 1.0, True).astype(BF16)
    cmat = _block_diag(c_re, c_im, -1.0, False).astype(BF16)
    a_chunks = jnp.concatenate([abar_re.reshape(SSM_CHUNKS, 1, CHUNK_STATES), abar_im.reshape(SSM_CHUNKS, 1, CHUNK_STATES)], axis=2)
    return bmat, cmat, a_chunks


GL_COL = (3 * QKV_WIDTH + SSM_WIDTH) // D_MODEL
GELU_C = math.sqrt(2.0 / math.pi)
GELU_A = 0.044715


def _sds(shape, dtype):
    return jax.ShapeDtypeStruct(shape, dtype)


def _gelu(x):
    t = jnp.tanh(GELU_C * (x + GELU_A * x * x * x))
    return 0.5 * x * (1.0 + t), t


def _gelu_grad(x, t):
    return 0.5 * (1.0 + t) + 0.5 * x * (1.0 - t * t) * GELU_C * (1.0 + 3.0 * GELU_A * x * x)


def _layer_norm(r, g, b):
    mu = jnp.mean(r, axis=-1, keepdims=True)
    xc = r - mu
    rstd = lax.rsqrt(jnp.mean(xc * xc, axis=-1, keepdims=True) + LN_EPS)
    xhat = xc * rstd
    return xhat * g + b, xhat, rstd


def _layer_norm_bwd(dy, xhat, rstd, g):
    dxhat = dy * g
    m1 = jnp.mean(dxhat, axis=-1, keepdims=True)
    m2 = jnp.mean(dxhat * xhat, axis=-1, keepdims=True)
    return rstd * (dxhat - m1 - xhat * m2)


def _proj(x, w_in, ride=None):
    tm, tn = 1024, 1792

    def body(x_ref, w_ref, o_ref):
        o_ref[...] = _dot(x_ref[...].astype(BF16), _side_by_side(w_ref))

    return _call(
        body, "proj", (SEQ // tm, IN_WIDTH // tn),
        [pl.BlockSpec((tm, D_MODEL), lambda i, j: (i, 0)), pl.BlockSpec((2, D_MODEL, tn // 2), lambda i, j: (j, 0, 0))],
        [pl.BlockSpec((tm, tn), lambda i, j: (i, j))], [_sds((SEQ, IN_WIDTH), F32)], [], [x, w_in], ride)


def _row_spec(tm, width, col=0):
    return pl.BlockSpec((tm, width), lambda i, col=col: (i, col))


def _full_spec(shape):
    return pl.BlockSpec(shape, lambda i: (0,) * len(shape))


def _weight_spec(shape):
    return pl.BlockSpec(shape, lambda i: (0,) * len(shape), pipeline_mode=pl.Buffered(1))


def _mixer_out(attn, ys, proj, x, w_ab, w_sb, w_glu, w_out, b_gate, ln_g, ln_b, ride=None):
    tm = 512

    def body(attn_ref, ys_ref, gl0_ref, gl1_ref, x_ref, wab_ref, wsb_ref, wglu_ref, wout_ref, bg_ref, g_ref, b_ref,
             h_ref, xhat_ref, rstd_ref, glu_ref, ya_ref, yssm_ref):
        gy, _ = _gelu(ys_ref[...])
        glu = _dot(gy.astype(BF16), _side_by_side(wglu_ref))
        glu_ref[...] = glu.astype(BF16)
        y_s = glu[:, :SSM_WIDTH] * jax.nn.sigmoid(glu[:, SSM_WIDTH:])
        y_ssm = _dot(y_s.astype(BF16), _side_by_side(wsb_ref))
        y_attn = _dot(attn_ref[...].astype(BF16), _side_by_side(wab_ref))
        ya_ref[...] = y_attn.astype(BF16)
        yssm_ref[...] = y_ssm.astype(BF16)
        g0 = jax.nn.sigmoid(gl0_ref[...] + _side_by_side(bg_ref, 0))
        g1 = jax.nn.sigmoid(gl1_ref[...] + _side_by_side(bg_ref, 1))
        mixed = g0 * y_attn + g1 * y_ssm
        r1 = DN_ALPHA * x_ref[...] + _dot(mixed.astype(BF16), wout_ref[...])
        h, xhat, rstd = _layer_norm(r1, g_ref[...], b_ref[...])
        h_ref[...] = h
        xhat_ref[...] = xhat
        rstd_ref[...] = jnp.broadcast_to(rstd, (tm, 128))

    wide = _sds((SEQ, D_MODEL), F32)
    return _call(
        body, "mixer_out", (SEQ // tm,),
        [_row_spec(tm, ATTN_WIDTH), _row_spec(tm, SSM_WIDTH), _row_spec(tm, D_MODEL, GL_COL), _row_spec(tm, D_MODEL, GL_COL + 1),
         _row_spec(tm, D_MODEL), _weight_spec((N_DEV, ATTN_WIDTH, 128)), _weight_spec((N_DEV, SSM_WIDTH, 128)),
         _weight_spec((N_DEV, SSM_WIDTH, 128)), _weight_spec((D_MODEL, D_MODEL)), _full_spec((N_DEV, 2, 128)),
         _full_spec((1, D_MODEL)), _full_spec((1, D_MODEL))],
        [_row_spec(tm, D_MODEL), _row_spec(tm, D_MODEL), _row_spec(tm, 128), _row_spec(tm, D_MODEL),
         _row_spec(tm, D_MODEL), _row_spec(tm, D_MODEL)],
        [wide, wide, _sds((SEQ, 128), F32)] + [_sds((SEQ, D_MODEL), BF16)] * 3, [],
        [attn, ys, proj, proj, x, w_ab, w_sb, w_glu, w_out, b_gate, ln_g, ln_b], ride)


def _ff_up(h, w_gate, w_up, ride=None):
    tm, tn = 1024, 768

    def body(h_ref, wg_ref, wu_ref, a_ref, b_ref, f_ref):
        hb = h_ref[...].astype(BF16)
        a, b = _dot(hb, _side_by_side(wg_ref)), _dot(hb, _side_by_side(wu_ref))
        a_ref[...] = a.astype(BF16)
        b_ref[...] = b.astype(BF16)
        f_ref[...] = (a * jax.nn.sigmoid(a) * b).astype(BF16)

    tile = pl.BlockSpec((tm, tn), lambda i, j: (i, j))
    wtile = pl.BlockSpec((tn // FF_PAD, D_MODEL, FF_PAD), lambda i, j: (j, 0, 0))
    out = _sds((SEQ, D_FF_PAD), BF16)
    return _call(body, "ff_up", (SEQ // tm, D_FF_PAD // tn), [pl.BlockSpec((tm, D_MODEL), lambda i, j: (i, 0)), wtile, wtile],
                 [tile, tile, tile], [out, out, out], [], [h, w_gate, w_up], ride)


def _ff_down_loss(f, w_down, h, target, ln_g, ln_b):
    tm = 512

    def body(f_ref, w_ref, h_ref, t_ref, g_ref, b_ref, dr_ref, dg_ref, db_ref, loss_ref):
        @pl.when(pl.program_id(0) == 0)
        def _():
            dg_ref[...] = jnp.zeros_like(dg_ref)
            db_ref[...] = jnp.zeros_like(db_ref)
            loss_ref[...] = jnp.zeros_like(loss_ref)

        r2 = DN_ALPHA * h_ref[...] + _dot(f_ref[...], w_ref[...])
        g = g_ref[...]
        out, xhat, rstd = _layer_norm(r2, g, b_ref[...])
        err = out - t_ref[...]
        loss_ref[...] += 0.5 * jnp.sum(jnp.mean(err * err, axis=-1, keepdims=True), axis=0, keepdims=True)
        dout = err * (1.0 / D_MODEL)
        dg_ref[...] += jnp.sum(dout * xhat, axis=0, keepdims=True)
        db_ref[...] += jnp.sum(dout, axis=0, keepdims=True)
        dr_ref[...] = _layer_norm_bwd(dout, xhat, rstd, g)

    vec = _sds((1, D_MODEL), F32)
    return _pallas_call(
        body, name="ff_down_loss", grid=(SEQ // tm,),
        in_specs=[_row_spec(tm, D_FF_PAD), _weight_spec((D_FF_PAD, D_MODEL)), _row_spec(tm, D_MODEL), _row_spec(tm, D_MODEL),
                  _full_spec((1, D_MODEL)), _full_spec((1, D_MODEL))],
        out_specs=(_row_spec(tm, D_MODEL), _full_spec((1, D_MODEL)), _full_spec((1, D_MODEL)), _full_spec((1, 128))),
        out_shape=(_sds((SEQ, D_MODEL), F32), vec, vec, _sds((1, 128), F32)),
        compiler_params=_cparams(dimension_semantics=("arbitrary",)),
    )(f, w_down, h, target, ln_g, ln_b)


def _ff_down_bwd(dr2, w_down, a, b):
    tm, tn = 1024, 768

    def body(dr_ref, w_ref, a_ref, b_ref, da_ref, db_ref):
        df = _dot_nt(dr_ref[...].astype(BF16), w_ref[...])
        av, bv = a_ref[...].astype(F32), b_ref[...].astype(F32)
        sg = jax.nn.sigmoid(av)
        da_ref[...] = (df * bv * sg * (1.0 + av * (1.0 - sg))).astype(BF16)
        db_ref[...] = (df * av * sg).astype(BF16)

    tile = pl.BlockSpec((tm, tn), lambda i, j: (i, j))
    out = _sds((SEQ, D_FF_PAD), BF16)
    return _pallas_call(
        body, name="ff_down_bwd", grid=(SEQ // tm, D_FF_PAD // tn),
        in_specs=[pl.BlockSpec((tm, D_MODEL), lambda i, j: (i, 0)), pl.BlockSpec((tn, D_MODEL), lambda i, j: (j, 0)), tile, tile],
        out_specs=(tile, tile), out_shape=(out, out),
        compiler_params=_cparams(dimension_semantics=("arbitrary", "arbitrary")),
    )(dr2, w_down, a, b)


def _ff_up_bwd(da, db, w_gate, w_up, dr2, xhat1, rstd1, ln_g, ride=None):
    tm, tk = 1024, 768
    nk = D_FF_PAD // tk

    def body(da_ref, db_ref, wg_ref, wu_ref, dr2_ref, xhat_ref, rstd_ref, g_ref, dr1_ref, dg_ref, dbias_ref, acc):
        i, k = pl.program_id(0), pl.program_id(1)

        @pl.when(jnp.logical_and(i == 0, k == 0))
        def _():
            dg_ref[...] = jnp.zeros_like(dg_ref)
            dbias_ref[...] = jnp.zeros_like(dbias_ref)

        part = _dot_nt(da_ref[...], _side_by_side(wg_ref)) + _dot_nt(db_ref[...], _side_by_side(wu_ref))

        @pl.when(k == 0)
        def _():
            acc[...] = part

        @pl.when(k > 0)
        def _():
            acc[...] += part

        @pl.when(k == nk - 1)
        def _():
            dh = DN_ALPHA * dr2_ref[...] + acc[...]
            xhat = xhat_ref[...]
            dg_ref[...] += jnp.sum(dh * xhat, axis=0, keepdims=True)
            dbias_ref[...] += jnp.sum(dh, axis=0, keepdims=True)
            rstd = jnp.max(rstd_ref[...], axis=1, keepdims=True)
            dr1_ref[...] = _layer_norm_bwd(dh, xhat, rstd, g_ref[...])

    hid = pl.BlockSpec((tm, tk), lambda i, k: (i, k))
    wtile = pl.BlockSpec((tk // FF_PAD, D_MODEL, FF_PAD), lambda i, k: (k, 0, 0))
    row = pl.BlockSpec((tm, D_MODEL), lambda i, k: (i, 0))
    vec = pl.BlockSpec((1, D_MODEL), lambda i, k: (0, 0))
    return _call(
        body, "ff_up_bwd", (SEQ // tm, nk),
        [hid, hid, wtile, wtile, row, row, pl.BlockSpec((tm, 128), lambda i, k: (i, 0)), vec],
        [row, vec, vec], [_sds((SEQ, D_MODEL), F32), _sds((1, D_MODEL), F32), _sds((1, D_MODEL), F32)],
        [pltpu.VMEM((tm, D_MODEL), F32)], [da, db, w_gate, w_up, dr2, xhat1, rstd1, ln_g], ride)


def _mixer_bwd(dr1, proj, y_attn, y_ssm, glu, ys, w_ab, w_sb, w_glu, w_out, b_gate):
    tm = 256

    def body(dr1_ref, gl0_ref, gl1_ref, ya_ref, yssm_ref, glu_ref, ys_ref, wab_ref, wsb_ref, wglu_ref, wout_ref, bg_ref,
             dya_ref, dyssm_ref, dgl_ref, dattn_ref, dglu_ref, dys_ref, mixed_ref, ysb_ref, gy_ref, dbg_ref, stage, copied):
        @pl.when(pl.program_id(0) == 0)
        def _():
            dbg_ref[...] = jnp.zeros_like(dbg_ref)

        dmixed = _dot_nt(dr1_ref[...].astype(BF16), wout_ref[...])
        g0 = jax.nn.sigmoid(gl0_ref[...] + _side_by_side(bg_ref, 0))
        g1 = jax.nn.sigmoid(gl1_ref[...] + _side_by_side(bg_ref, 1))
        y_attn, y_ssm = ya_ref[...].astype(F32), yssm_ref[...].astype(F32)
        mixed_ref[...] = (g0 * y_attn + g1 * y_ssm).astype(BF16)
        dya = (dmixed * g0).astype(BF16)
        dyssm = (dmixed * g1).astype(BF16)
        dya_ref[...] = dya
        dyssm_ref[...] = dyssm
        dgl0 = dmixed * y_attn * g0 * (1.0 - g0)
        dgl1 = dmixed * y_ssm * g1 * (1.0 - g1)
        i, last = pl.program_id(0), SEQ // tm - 1
        slot = i & 1

        def copy_out(buffer, tile):
            window = dgl_ref.at[pl.ds(pl.multiple_of(tile * tm, tm), tm), pl.ds(GL_COL * D_MODEL, 2 * D_MODEL)]
            return pltpu.make_async_copy(stage.at[buffer], window, copied.at[buffer])

        @pl.when(i >= 2)
        def _():
            copy_out(slot, i - 2).wait()

        stage[slot, :, :D_MODEL] = dgl0.astype(BF16)
        stage[slot, :, D_MODEL:] = dgl1.astype(BF16)
        copy_out(slot, i).start()

        @pl.when(i == last)
        def _():
            copy_out(1 - slot, i - 1).wait()
            copy_out(slot, i).wait()
        dbg_ref[:, :D_MODEL] += jnp.sum(dgl0, axis=0, keepdims=True)
        dbg_ref[:, D_MODEL:] += jnp.sum(dgl1, axis=0, keepdims=True)
        dattn_ref[...] = _dot_nt(dya, _side_by_side(wab_ref))
        dy_s = _dot_nt(dyssm, _side_by_side(wsb_ref))
        glu = glu_ref[...].astype(F32)
        glu1, sg = glu[:, :SSM_WIDTH], jax.nn.sigmoid(glu[:, SSM_WIDTH:])
        ysb_ref[...] = (glu1 * sg).astype(BF16)
        dglu1 = (dy_s * sg).astype(BF16)
        dglu2 = (dy_s * glu1 * sg * (1.0 - sg)).astype(BF16)
        dglu_ref[:, :SSM_WIDTH] = dglu1
        dglu_ref[:, SSM_WIDTH:] = dglu2
        dgy = _dot_nt(jnp.concatenate([dglu1, dglu2], axis=1), _side_by_side(wglu_ref))
        ys = ys_ref[...]
        gy, t = _gelu(ys)
        gy_ref[...] = gy.astype(BF16)
        dys_ref[...] = dgy * _gelu_grad(ys, t)

    wide_b, half_b = _sds((SEQ, D_MODEL), BF16), _sds((SEQ, SSM_WIDTH), BF16)
    half_f = _sds((SEQ, SSM_WIDTH), F32)
    return _pallas_call(
        body, name="mixer_bwd", grid=(SEQ // tm,),
        in_specs=[_row_spec(tm, D_MODEL), _row_spec(tm, D_MODEL, GL_COL), _row_spec(tm, D_MODEL, GL_COL + 1), _row_spec(tm, D_MODEL),
                  _row_spec(tm, D_MODEL), _row_spec(tm, D_MODEL), _row_spec(tm, SSM_WIDTH), _full_spec((N_DEV, ATTN_WIDTH, 128)),
                  _full_spec((N_DEV, SSM_WIDTH, 128)), _full_spec((N_DEV, SSM_WIDTH, 128)), _full_spec((D_MODEL, D_MODEL)),
                  _full_spec((N_DEV, 2, 128))],
        out_specs=(_row_spec(tm, D_MODEL), _row_spec(tm, D_MODEL), ANY, _row_spec(tm, ATTN_WIDTH),
                   _row_spec(tm, D_MODEL), _row_spec(tm, SSM_WIDTH), _row_spec(tm, D_MODEL), _row_spec(tm, SSM_WIDTH),
                   _row_spec(tm, SSM_WIDTH), _full_spec((1, 2 * D_MODEL))),
        out_shape=(wide_b, wide_b, _sds((SEQ, IN_WIDTH), BF16), half_f, wide_b, half_f, wide_b, half_b, half_b,
                   _sds((1, 2 * D_MODEL), F32)),
        scratch_shapes=[pltpu.VMEM((2, tm, 2 * D_MODEL), BF16), pltpu.SemaphoreType.DMA((2,))],
        compiler_params=_cparams(dimension_semantics=("arbitrary",)),
    )(dr1, proj, proj, y_attn, y_ssm, glu, ys, w_ab, w_sb, w_glu, w_out, b_gate)


def _grad_x(dproj, w_in, dr1, ride=None):
    tm, tk = 1024, 1792
    nk = IN_WIDTH // tk

    def body(dp_ref, w_ref, dr1_ref, o_ref, acc):
        k = pl.program_id(1)
        part = _dot_nt(dp_ref[...], _side_by_side(w_ref))

        @pl.when(k == 0)
        def _():
            acc[...] = part

        @pl.when(k > 0)
        def _():
            acc[...] += part

        @pl.when(k == nk - 1)
        def _():
            o_ref[...] = DN_ALPHA * dr1_ref[...] + acc[...]

    row = pl.BlockSpec((tm, D_MODEL), lambda i, k: (i, 0))
    return _call(
        body, "grad_x", (SEQ // tm, nk),
        [pl.BlockSpec((tm, tk), lambda i, k: (i, k)), pl.BlockSpec((2, D_MODEL, tk // 2), lambda i, k: (k, 0, 0)), row],
        [row], [_sds((SEQ, D_MODEL), F32)], [pltpu.VMEM((tm, D_MODEL), F32)], [dproj, w_in, dr1], ride)


def _weight_grad(a, b, name, shard_cols=None):
    k, n = a.shape[1], b.shape[1]
    tk = k if shard_cols else k // N_DEV
    tn = n // 4 if shard_cols else min(n, 1024)

    def body(a_ref, b_ref, o_ref):
        grad = _dot_tn(a_ref[...].astype(BF16), b_ref[...].astype(BF16))
        if shard_cols:
            o_ref[0] = grad[:, :shard_cols].astype(BF16)
            o_ref[1] = grad[:, shard_cols:].astype(BF16)
        else:
            o_ref[...] = grad.astype(BF16)

    if shard_cols:
        out_spec = pl.BlockSpec((2, None, tk, shard_cols), lambda kk, j: (0, j, kk, 0))
        out_shape = _sds((2, 4, k, shard_cols), BF16)
    else:
        out_spec = pl.BlockSpec((None, None, tk, tn), lambda kk, j: (kk % 2, kk // 2, 0, j))
        out_shape = _sds((2, 4, tk, n), BF16)
    return _call(body, name, (k // tk, n // tn),
                 [pl.BlockSpec((SEQ, tk), lambda kk, j: (0, kk)), pl.BlockSpec((SEQ, tn), lambda kk, j: (0, j))],
                 [out_spec], [out_shape], [], [a, b])[0]


def _weight_grad_of_core(a, b, core, name, shard_cols, ride):
    k = a.shape[1]

    def body(a_ref, b_ref, o_ref):
        o_ref[...] = _dot_tn(a_ref[...].astype(BF16), b_ref[...].astype(BF16)).astype(BF16)

    return _call(body, name, (4,),
                 [pl.BlockSpec((SEQ, k), lambda j, core_ref: (0, 0), pipeline_mode=pl.Buffered(1)),
                  pl.BlockSpec((SEQ, shard_cols), lambda j, core_ref: (0, 2 * j + core_ref[0]))],
                 [pl.BlockSpec((None, k, shard_cols), lambda j, core_ref: (j, 0, 0))],
                 [_sds((4, k, shard_cols), BF16)], [], [a, b], ride, prefetch=core)


MESH = pl.DeviceIdType.MESH
ANY = pl.BlockSpec(memory_space=pl.ANY)


def _place():
    return lax.axis_index("x"), lax.axis_index("y"), lax.axis_index("c")


def _other_chips(x, y):
    return [(1 - x, y), (x, 1 - y), (1 - x, 1 - y)]


class _Ride:
    def __init__(self, operands, results, aliases, sems, start, wait):
        self.operands, self.results, self.aliases, self.sems = list(operands), list(results), dict(aliases), list(sems)
        self.start, self.wait = start, wait

    def __add__(self, other):
        n_in, n_out, n_sem = len(self.operands), len(self.results), len(self.sems)

        def both(which):
            def run(ins, outs, sems):
                getattr(self, which)(ins[:n_in], outs[:n_out], sems[:n_sem])
                getattr(other, which)(ins[n_in:], outs[n_out:], sems[n_sem:])
            return run

        aliases = {**self.aliases, **{n_in + i: n_out + j for i, j in other.aliases.items()}}
        return _Ride(self.operands + other.operands, self.results + other.results, aliases, self.sems + other.sems,
                     both("start"), both("wait"))


def _call(body, name, grid, in_specs, out_specs, out_shape, scratch_shapes, operands, ride=None, aliases=None, prefetch=None):
    in_specs, out_specs, out_shape = list(in_specs), list(out_specs), list(out_shape)
    scratch_shapes, operands, aliases = list(scratch_shapes), list(operands), dict(aliases or {})
    kernel_body = body
    if ride is not None:
        n_in, n_out, n_scr, r_in, r_out = len(in_specs), len(out_specs), len(scratch_shapes), len(ride.operands), len(ride.results)

        def kernel_body(*refs):
            out0, scr0 = n_in + r_in, n_in + r_in + n_out + r_out
            ride_refs = (refs[n_in:out0], refs[out0 + n_out:scr0], refs[scr0 + n_scr:])
            ids = [pl.program_id(i) for i in range(len(grid))]
            first = functools.reduce(jnp.logical_and, [i == 0 for i in ids])
            last = functools.reduce(jnp.logical_and, [i == g - 1 for i, g in zip(ids, grid)])

            @pl.when(first)
            def _():
                ride.start(*ride_refs)

            body(*refs[:n_in], *refs[out0:out0 + n_out], *refs[scr0:scr0 + n_scr])

            @pl.when(last)
            def _():
                ride.wait(*ride_refs)

        aliases.update({n_in + i: n_out + j for i, j in ride.aliases.items()})
        in_specs += [ANY] * r_in
        out_specs += [ANY] * r_out
        out_shape += ride.results
        scratch_shapes += ride.sems
        operands += ride.operands
    params = _cparams(dimension_semantics=("arbitrary",) * len(grid))
    if prefetch is None:
        return _pallas_call(
            kernel_body, name=name, grid=grid, in_specs=in_specs, out_specs=out_specs, out_shape=out_shape,
            scratch_shapes=scratch_shapes, input_output_aliases=aliases, compiler_params=params,
        )(*operands)

    def with_prefetch(prefetch_ref, *refs):
        kernel_body(*refs)

    return _pallas_call(
        with_prefetch, name=name,
        grid_spec=pltpu.PrefetchScalarGridSpec(num_scalar_prefetch=1, grid=grid, in_specs=in_specs, out_specs=out_specs,
                                               scratch_shapes=scratch_shapes),
        out_shape=out_shape, input_output_aliases={i + 1: j for i, j in aliases.items()}, compiler_params=params,
    )(prefetch, *operands)


def _after(*arrays):
    return _Ride(arrays, [], {}, [], lambda *refs: None, lambda *refs: None)


def _gather_first_level(shards):
    n = len(shards)

    def copies(ins, outs, sems, landed):
        send_sems, recv_sems, local_sems = sems
        x, y, c = _place()
        peers = [(x, y, 1 - c)] + [(px, py, c) for px, py in _other_chips(x, y)]

        def row(peer):
            return 4 * x + 2 * y + c if not landed else 4 * peer[0] + 2 * peer[1] + peer[2]

        local = [pltpu.make_async_copy(ins[a], outs[a].at[4 * x + 2 * y + c], local_sems.at[a]) for a in range(n)]
        remote = [pltpu.make_async_remote_copy(
            src_ref=ins[a], dst_ref=outs[a].at[row(peer)], send_sem=send_sems.at[a, k], recv_sem=recv_sems.at[a, k],
            device_id=peer, device_id_type=MESH) for a in range(n) for k, peer in enumerate(peers)]
        return local, remote

    def start(ins, outs, sems):
        local, remote = copies(ins, outs, sems, False)
        for cp in local + remote:
            cp.start()

    def wait(ins, outs, sems):
        local, sent = copies(ins, outs, sems, False)
        for cp in copies(ins, outs, sems, True)[1]:
            cp.wait_recv()
        for cp in sent:
            cp.wait_send()
        for cp in local:
            cp.wait()

    return _Ride(shards, [_sds((N_DEV,) + s.shape, s.dtype) for s in shards], {},
                 [pltpu.SemaphoreType.DMA((n, 4)), pltpu.SemaphoreType.DMA((n, 4)), pltpu.SemaphoreType.DMA((n,))], start, wait)


def _gather_second_level(buffers):
    n = len(buffers)

    def copies(outs, sems, core):
        send_sems, recv_sems = sems
        x, y, c = _place()
        return [pltpu.make_async_remote_copy(
            src_ref=outs[a].at[4 * px + 2 * py + core], dst_ref=outs[a].at[4 * px + 2 * py + core], send_sem=send_sems.at[a, j],
            recv_sem=recv_sems.at[a, j], device_id=(x, y, 1 - c), device_id_type=MESH)
            for a in range(n) for j, (px, py) in enumerate(_other_chips(x, y))]

    def start(ins, outs, sems):
        for cp in copies(outs, sems, lax.axis_index("c")):
            cp.start()

    def wait(ins, outs, sems):
        for cp in copies(outs, sems, 1 - lax.axis_index("c")):
            cp.wait_recv()
        for cp in copies(outs, sems, lax.axis_index("c")):
            cp.wait_send()

    return _Ride(buffers, [_sds(b.shape, b.dtype) for b in buffers], {i: i for i in range(n)},
                 [pltpu.SemaphoreType.DMA((n, 3)), pltpu.SemaphoreType.DMA((n, 3))], start, wait)


def _relayed_gather(shards):
    n = len(shards)
    buffers = [_sds((N_DEV,) + s.shape, s.dtype) for s in shards]
    dma = pltpu.SemaphoreType.DMA

    def remote(src, dst, send_sem, recv_sem, to):
        return pltpu.make_async_remote_copy(src_ref=src, dst_ref=dst, send_sem=send_sem, recv_sem=recv_sem,
                                            device_id=to, device_id_type=MESH)

    def row(px, py, pc):
        return 4 * px + 2 * py + pc

    def ride(operands, aliases, sems, copies):
        def start(ins, outs, sem_refs):
            local, sent = copies(ins, outs, sem_refs, False)
            for cp in local + sent:
                cp.start()

        def wait(ins, outs, sem_refs):
            local, sent = copies(ins, outs, sem_refs, False)
            for cp in copies(ins, outs, sem_refs, True)[1]:
                cp.wait_recv()
            for cp in sent:
                cp.wait_send()
            for cp in local:
                cp.wait()

        return _Ride(operands, buffers, aliases, sems, start, wait)

    def first(ins, outs, sems, landed):
        x, y, c = _place()
        peers = [(x, y, 1 - c), (1 - x, y, c), (x, 1 - y, c)]
        local = [pltpu.make_async_copy(ins[a], outs[a].at[row(x, y, c)], sems[2].at[a]) for a in range(n)]
        return local, [remote(ins[a], outs[a].at[row(*peer) if landed else row(x, y, c)], sems[0].at[a, k], sems[1].at[a, k], peer)
                       for a in range(n) for k, peer in enumerate(peers)]

    def second(ins, outs, sems, landed):
        x, y, c = _place()
        mine = 1 - c if landed else c
        copies = []
        for a in range(n):
            half = shards[a].shape[0] // 2
            over_x, over_y, diagonal = outs[a].at[row(1 - x, y, mine)], outs[a].at[row(x, 1 - y, mine)], outs[a].at[row(1 - x, 1 - y, c)]
            lower, upper = pl.ds(0, half), pl.ds(half, half)
            copies += [remote(over_x, over_x, sems[0].at[a, 0], sems[1].at[a, 0], (x, y, 1 - c)),
                       remote(over_y, over_y, sems[0].at[a, 1], sems[1].at[a, 1], (x, y, 1 - c))]
            if landed:
                copies += [remote(diagonal.at[lower], diagonal.at[lower], sems[0].at[a, 2], sems[1].at[a, 2], (1 - x, y, c)),
                           remote(diagonal.at[upper], diagonal.at[upper], sems[0].at[a, 3], sems[1].at[a, 3], (x, 1 - y, c))]
            else:
                copies += [remote(over_y.at[lower], over_y.at[lower], sems[0].at[a, 2], sems[1].at[a, 2], (1 - x, y, c)),
                           remote(over_x.at[upper], over_x.at[upper], sems[0].at[a, 3], sems[1].at[a, 3], (x, 1 - y, c))]
        return [], copies

    def third(ins, outs, sems, landed):
        x, y, c = _place()
        return [], [remote(outs[a].at[row(1 - x, 1 - y, 1 - c if landed else c)], outs[a].at[row(1 - x, 1 - y, 1 - c if landed else c)],
                           sems[0].at[a], sems[1].at[a], (x, y, 1 - c)) for a in range(n)]

    def later(copies, n_sems):
        return lambda partly: ride(partly, {i: i for i in range(n)}, [dma((n,) + n_sems), dma((n,) + n_sems)], copies)

    return ride(shards, {}, [dma((n, 3)), dma((n, 3)), dma((n,))], first), later(second, (4,)), later(third, ())


def _sibling_swap_ride(grads, halves=True):
    n = len(grads)

    def copies(ins, outs, sems):
        x, y, c = _place()
        return [pltpu.make_async_remote_copy(
            src_ref=ins[a].at[1 - c] if halves else ins[a], dst_ref=outs[a], send_sem=sems[0].at[a], recv_sem=sems[1].at[a],
            device_id=(x, y, 1 - c), device_id_type=MESH) for a in range(n)]

    def start(ins, outs, sems):
        for cp in copies(ins, outs, sems):
            cp.start()

    def wait(ins, outs, sems):
        for cp in copies(ins, outs, sems):
            cp.wait()

    return _Ride(grads, [_sds(g.shape[1:] if halves else g.shape, g.dtype) for g in grads], {},
                 [pltpu.SemaphoreType.DMA((n,)), pltpu.SemaphoreType.DMA((n,))], start, wait)


def _chip_swap_ride(sums):
    n = len(sums)

    def copies(ins, outs, sems, landed):
        send_sems, recv_sems, local_sems = sems
        x, y, c = _place()
        mine = 2 * x + y
        local = [pltpu.make_async_copy(ins[a].at[mine], outs[a].at[mine], local_sems.at[a]) for a in range(n)]
        remote = [pltpu.make_async_remote_copy(
            src_ref=ins[a].at[2 * px + py], dst_ref=outs[a].at[2 * px + py if landed else mine], send_sem=send_sems.at[a, j],
            recv_sem=recv_sems.at[a, j], device_id=(px, py, c), device_id_type=MESH)
            for a in range(n) for j, (px, py) in enumerate(_other_chips(x, y))]
        return local, remote

    def start(ins, outs, sems):
        local, remote = copies(ins, outs, sems, False)
        for cp in local + remote:
            cp.start()

    def wait(ins, outs, sems):
        local, sent = copies(ins, outs, sems, False)
        for cp in copies(ins, outs, sems, True)[1]:
            cp.wait_recv()
        for cp in sent:
            cp.wait_send()
        for cp in local:
            cp.wait()

    return _Ride(sums, [_sds(s.shape, s.dtype) for s in sums], {},
                 [pltpu.SemaphoreType.DMA((n, 3)), pltpu.SemaphoreType.DMA((n, 3)), pltpu.SemaphoreType.DMA((n,))], start, wait)


def _send_buffers(shards, name):
    n = len(shards)

    def body(*refs):
        for (w, transposed, rows, cols), w_ref, o_ref in zip(shards, refs[:n], refs[n:]):
            if transposed:
                c, r = w.shape
                padded = jnp.concatenate([w_ref[...], jnp.zeros((cols - c, r), F32)], axis=0) if cols > c else w_ref[...]
                o_ref[...] = padded.T.astype(BF16)
            else:
                r, c = w.shape
                if (r, c) != (rows, cols):
                    o_ref[...] = jnp.zeros((rows, cols), BF16)
                o_ref[:r, :c] = w_ref[...].astype(BF16)

    return _pallas_call(body, name=name, out_shape=[_sds((rows, cols), BF16) for _, _, rows, cols in shards])(
        *[w for w, _, _, _ in shards])


def _all_gather(shards, name):
    n = len(shards)
    first, second, third = _relayed_gather(shards)
    levels = [first, second(shards), third(shards)]
    counts = [len(level.sems) for level in levels]

    def body(*refs):
        ins, outs, sems = refs[:n], refs[n:2 * n], refs[2 * n:]
        for i, level in enumerate(levels):
            mine = sems[sum(counts[:i]):sum(counts[:i + 1])]
            level.start(ins, outs, mine)
            level.wait(ins, outs, mine)

    return _pallas_call(
        body, name=name, in_specs=[ANY] * n, out_specs=[ANY] * n, out_shape=first.results,
        scratch_shapes=[s for level in levels for s in level.sems],
    )(*shards)


HBM = pl.BlockSpec(memory_space=pltpu.HBM)
SEMAPHORES = pl.BlockSpec(memory_space=pltpu.SEMAPHORE)
IN_FLIGHT = pltpu.CompilerParams(has_side_effects=pltpu.SideEffectType.DATAFLOW_SIDE_EFFECTING)


def _chip_swap_copies(src_refs, land_refs, send_sems, recv_sems, landed):
    x, y, c = _place()
    return [pltpu.make_async_remote_copy(
        src_ref=src.at[2 * px + py], dst_ref=land.at[2 * px + py if landed else 2 * x + y], send_sem=send_sems.at[3 * a + j],
        recv_sem=recv_sems.at[3 * a + j], device_id=(px, py, c), device_id_type=MESH)
        for a, (src, land) in enumerate(zip(src_refs, land_refs)) for j, (px, py) in enumerate(_other_chips(x, y))]


def _chip_swap_start(sums, name):
    n = len(sums)

    def body(*refs):
        src_refs, land_refs, (send_sems, recv_sems), token = refs[:n], refs[n:2 * n], refs[2 * n:2 * n + 2], refs[-1]
        for cp in _chip_swap_copies(src_refs, land_refs, send_sems, recv_sems, False):
            cp.start()
        token[...] = jnp.zeros_like(token)

    kept = [pltpu.HBM(s.shape, s.dtype) for s in sums]
    out = _pallas_call(
        body, name=name,
        out_shape=[pltpu.SemaphoreType.DMA((3 * n,)), pltpu.SemaphoreType.DMA((3 * n,))] + kept + kept + [_sds((8, 128), F32)],
        in_specs=[HBM] * (2 * n), out_specs=[SEMAPHORES, SEMAPHORES] + [HBM] * (2 * n) + [pl.BlockSpec(memory_space=pltpu.VMEM)],
        input_output_aliases={i: 2 + i for i in range(2 * n)}, compiler_params=IN_FLIGHT,
    )(*[pltpu.with_memory_space_constraint(s, pltpu.HBM) for s in sums],
      *[pltpu.with_memory_space_constraint(lax.empty(s.shape, s.dtype), pltpu.HBM) for s in sums])
    return out[0], out[1], out[2:2 + n], out[2 + n:2 + 2 * n], out[-1]


def _chip_swap_wait(send_sems, recv_sems, sums, landings, after, name):
    n = len(sums)

    def body(*refs):
        src_refs, land_refs, (send_sems, recv_sems) = refs[:n], refs[n:2 * n], refs[2 * n:2 * n + 2]
        for cp in _chip_swap_copies(src_refs, land_refs, send_sems, recv_sems, False):
            cp.wait_send()
        for cp in _chip_swap_copies(src_refs, land_refs, send_sems, recv_sems, True):
            cp.wait_recv()

    out = _pallas_call(
        body, name=name, out_shape=[pltpu.HBM(s.shape, s.dtype) for s in list(sums) + list(landings)],
        in_specs=[HBM] * (2 * n) + [SEMAPHORES, SEMAPHORES] + [ANY] * len(after), out_specs=[HBM] * (2 * n),
        input_output_aliases={i: i for i in range(2 * n)}, compiler_params=IN_FLIGHT,
    )(*sums, *landings, send_sems, recv_sems, *after)
    return out[:n], out[n:]


def _pair_sums(gs, rs, core, name):
    n_arrays = len(gs)

    def body(core_ref, *refs):
        for g_ref, r_ref, o_ref in zip(refs[:n_arrays], refs[n_arrays:2 * n_arrays], refs[2 * n_arrays:]):
            o_ref[...] = (g_ref[...].astype(F32) + r_ref[...].astype(F32)).astype(o_ref.dtype)

    def chip(g):
        return pl.BlockSpec((None,) + g.shape[-2:], lambda p, core_ref: (p, 0, 0))

    def own(g):
        return chip(g) if g.ndim == 3 else pl.BlockSpec((None, None) + g.shape[2:], lambda p, core_ref: (core_ref[0], p, 0, 0))

    return _pallas_call(
        body, name=name,
        grid_spec=pltpu.PrefetchScalarGridSpec(
            num_scalar_prefetch=1, grid=(4,), in_specs=[own(g) for g in gs] + [chip(g) for g in gs],
            out_specs=[chip(g) for g in gs]),
        out_shape=[_sds(g.shape[-3:], g.dtype) for g in gs], compiler_params=_cparams(dimension_semantics=("arbitrary",)),
    )(core, *gs, *rs)


def _adamw_math(w, g, m, v):
    m = ADAM_B1 * m + (1.0 - ADAM_B1) * g
    v = ADAM_B2 * v + (1.0 - ADAM_B2) * (g * g)
    m_hat = m / (1.0 - ADAM_B1 ** ADAM_STEP)
    v_hat = v / (1.0 - ADAM_B2 ** ADAM_STEP)
    return -ADAM_LR * (m_hat / (jnp.sqrt(v_hat) + ADAM_EPS) + ADAM_WD * w), m, v


def _adamw_many(weights, name, ride=None):
    steps = 4
    in_specs, out_specs, out_shape, operands, tiles = [], [], [], [], []
    for w, m, v, parts, own, transposed in weights:
        _, pr, pc = parts.shape
        if transposed:
            c, r = w.shape
            tile = pl.BlockSpec((c, r // steps), lambda i: (0, i))
            part_tile = pl.BlockSpec((4, r // steps, pc), lambda i: (0, i, 0))
            tiles.append((c, r // steps))
        elif w.shape[0] % (8 * steps) == 0:
            r, c = w.shape
            tile = pl.BlockSpec((r // steps, c), lambda i: (i, 0))
            part_tile = pl.BlockSpec((4, r // steps, pc), lambda i: (0, i, 0))
            tiles.append((r // steps, c))
        else:
            tile = pl.BlockSpec(w.shape, lambda i: (0, 0))
            part_tile = pl.BlockSpec(parts.shape, lambda i: (0, 0, 0))
            tiles.append(w.shape)
        in_specs += [tile, tile, tile] + [part_tile] * (1 if own is None else 2)
        out_specs += [tile] * 4
        out_shape += [_sds(w.shape, F32)] * 4
        operands += [w, m, v, parts] + ([] if own is None else [own])
    n_in = len(operands)

    def body(*refs):
        ins, outs = list(refs[:n_in]), refs[n_in:]
        this_chip = 2 * lax.axis_index("x") + lax.axis_index("y")
        for k, (_, _, _, _, own, transposed) in enumerate(weights):
            w_ref, m_ref, v_ref, p_ref = ins[:4]
            own_ref = None if own is None else ins[4]
            del ins[:4 if own is None else 5]
            rows, cols = tiles[k]
            g = None
            for q in range(4):
                index = (q,) if transposed else (q, slice(0, rows), slice(0, cols))
                part = p_ref[index] if own is None else jnp.where(this_chip == q, own_ref[index], p_ref[index])
                g = part.astype(F32) if g is None else g + part.astype(F32)
            if transposed:
                g = g.T[:rows]
            g_out, d_out, m_out, v_out = outs[4 * k:4 * k + 4]
            g_out[...] = g
            d_out[...], m_out[...], v_out[...] = _adamw_math(w_ref[...], g, m_ref[...], v_ref[...])

    return _call(body, name, (steps,), in_specs, out_specs, out_shape, [], operands, ride)


SMALL = ("ssm_a_re", "ssm_a_im", "ssm_log_dt", "ssm_b_re", "ssm_b_im", "ssm_c_re", "ssm_c_im", "ssm_d",
         "ln1_g", "ln1_b", "ln2_g", "ln2_b")


def _pack_rows(arrays):
    rows = []
    for a in arrays:
        flat = a.reshape(-1)
        rows.append(jnp.pad(flat, (0, -flat.shape[0] % 128)).reshape(-1, 128))
    packed = jnp.concatenate(rows, axis=0)
    return jnp.pad(packed, ((0, -packed.shape[0] % 8), (0, 0)))


def _unpack_rows(packed, shapes):
    out, row = [], 0
    for shape in shapes:
        size = math.prod(shape)
        n_rows = -(-size // 128)
        out.append(packed[row:row + n_rows].reshape(-1)[:size].reshape(shape))
        row += n_rows
    return out


def _sum_devices(parts):
    def body(p_ref, o_ref):
        total = p_ref[0]
        for dev in range(1, N_DEV):
            total = total + p_ref[dev]
        o_ref[...] = total

    return _pallas_call(body, name="sum_devices", out_shape=_sds(parts.shape[1:], F32))(parts)


def _adamw_replicated(ws, ms, vs, gs):
    n = len(ws)

    def body(*refs):
        w_refs, m_refs, v_refs, g_refs, d_out, m_out, v_out = (refs[i * n:(i + 1) * n] for i in range(7))
        for i in range(n):
            d_out[i][...], m_out[i][...], v_out[i][...] = _adamw_math(w_refs[i][...], g_refs[i][...], m_refs[i][...], v_refs[i][...])

    out = _pallas_call(body, name="adamw_replicated", out_shape=[_sds(w.shape, F32) for w in ws] * 3,
                       compiler_params=_cparams())(*ws, *ms, *vs, *gs)
    return out[:n], out[n:2 * n], out[2 * n:]


def kernel(x, w_in, b_gate, w_attn_br, w_ssm_br, w_out, ssm_a_re, ssm_a_im, ssm_log_dt, ssm_b_re, ssm_b_im, ssm_c_re, ssm_c_im, ssm_d, w_glu, ln1_g, ln1_b, w_ff_gate, w_ff_up, w_ff_down, ln2_g, ln2_b, loss_target, m_w_in, m_b_gate, m_w_attn_br, m_w_ssm_br, m_w_out, m_ssm_a_re, m_ssm_a_im, m_ssm_log_dt, m_ssm_b_re, m_ssm_b_im, m_ssm_c_re, m_ssm_c_im, m_ssm_d, m_w_glu, m_ln1_g, m_ln1_b, m_w_ff_gate, m_w_ff_up, m_w_ff_down, m_ln2_g, m_ln2_b, v_w_in, v_b_gate, v_w_attn_br, v_w_ssm_br, v_w_out, v_ssm_a_re, v_ssm_a_im, v_ssm_log_dt, v_ssm_b_re, v_ssm_b_im, v_ssm_c_re, v_ssm_c_im, v_ssm_d, v_w_glu, v_ln1_g, v_ln1_b, v_w_ff_gate, v_w_ff_up, v_w_ff_down, v_ln2_g, v_ln2_b):
    given = dict(locals())
    x2, target = x[0], loss_target[0]
    core = lax.axis_index("c").astype(jnp.int32).reshape(1)

    sharded = ("w_in", "w_attn_br", "w_ssm_br", "w_glu", "w_ff_gate", "w_ff_up", "b_gate", "w_out", "w_ff_down")
    send_shape = dict(w_in=(D_MODEL, 896), w_attn_br=(ATTN_WIDTH, 128), w_ssm_br=(SSM_WIDTH, 128), w_glu=(SSM_WIDTH, 128),
                      w_out=(128, D_MODEL), w_ff_gate=(D_MODEL, FF_PAD), w_ff_up=(D_MODEL, FF_PAD), w_ff_down=(FF_PAD, D_MODEL))
    local = {k: given[k][0] for k in sharded}
    narrow = ("w_ff_gate", "w_ff_up")
    def to_send(k):
        return (local[k].T, True, *send_shape[k]) if k in narrow else (local[k], False, *send_shape[k])

    later = [k for k in sharded if k not in ("w_in", "b_gate")]
    sends = dict(zip(["w_in"] + later, _send_buffers([to_send("w_in")], "send_w_in")
                     + _send_buffers([to_send(k) for k in later], "send_weights")))
    sends["b_gate"] = local["b_gate"]
    mixer_weights = ("w_attn_br", "w_ssm_br", "w_glu", "b_gate", "w_out")
    ff_weights = ("w_ff_gate", "w_ff_up", "w_ff_down")
    wt = {}
    wt["w_in"], = _all_gather([sends["w_in"]], "gather_w_in")

    a_re, a_im, log_dt = ssm_a_re[0], ssm_a_im[0], ssm_log_dt[0].reshape(SSM_GROUPS, 1)
    b_re_t, b_im_t = ssm_b_re[0].transpose(0, 2, 1), ssm_b_im[0].transpose(0, 2, 1)
    abar_re, abar_im, e_re, e_im, bbar_re_t, bbar_im_t = _ssm_prep(a_re, a_im, log_dt, b_re_t, b_im_t)
    bmat, cmat, a_chunks = _ssm_tables(abar_re, abar_im, bbar_re_t, bbar_im_t, ssm_c_re[0], ssm_c_im[0])
    cos_t, sin_t = _rope_tables()

    big_mixer, ff_in = [k for k in mixer_weights if k != "b_gate"], ("w_ff_gate", "w_ff_up")
    n_mixer = len(big_mixer)
    mixer_1, mixer_2, mixer_3 = _relayed_gather([sends[k] for k in big_mixer])
    ff_in_1, ff_in_2, ff_in_3 = _relayed_gather([sends[k] for k in ff_in])
    ff_down_1, ff_down_2, ff_down_3 = _relayed_gather([sends["w_ff_down"]])
    proj, *landed = _proj(x2, wt["w_in"], mixer_1 + _gather_first_level([sends["b_gate"]]))
    mixer, bias = landed[:n_mixer], landed[n_mixer:]
    attn, lse, q_pm, k_pm, v_pm, *landed = _attn_fwd(proj, cos_t, sin_t,
                                                     mixer_2(mixer) + _gather_second_level(bias) + ff_in_1)
    mixer, b_gate_full, ff = landed[:n_mixer], landed[n_mixer], landed[n_mixer + 1:]
    ys, states, *landed = _ssm_fwd(proj, bmat, cmat, a_chunks, ssm_d, mixer_3(mixer) + ff_in_2(ff) + ff_down_1)
    wt.update(zip(big_mixer, landed[:n_mixer]))
    ff, ff_down = landed[n_mixer:n_mixer + 2], landed[n_mixer + 2:]
    wt["w_out"] = wt["w_out"].reshape(D_MODEL, D_MODEL)
    h, xhat1, rstd1, glu, y_attn, y_ssm, *landed = _mixer_out(
        attn, ys, proj, x2, wt["w_attn_br"], wt["w_ssm_br"], wt["w_glu"], wt["w_out"], b_gate_full, ln1_g, ln1_b,
        ff_in_3(ff) + ff_down_2(ff_down))
    wt.update(zip(ff_in, landed[:2]))
    ff_a, ff_b, ff_f, w_ff_down = _ff_up(h, wt["w_ff_gate"], wt["w_ff_up"], ff_down_3(landed[2:]))
    wt["w_ff_down"] = w_ff_down.reshape(D_FF_PAD, D_MODEL)
    dr2, d_ln2_g, d_ln2_b, loss_lanes = _ff_down_loss(ff_f, wt["w_ff_down"], h, target, ln2_g, ln2_b)

    def pair_sums(names, contrib, from_sibling):
        return _pair_sums([contrib[k] for k in names], from_sibling, core, "pair_sums_" + names[0])

    d_a, d_b = _ff_down_bwd(dr2, wt["w_ff_down"], ff_a, ff_b)
    contrib = dict(w_ff_gate=_weight_grad(h, d_a, "wgrad_w_ff_gate", FF_PAD),
                   w_ff_up=_weight_grad(h, d_b, "wgrad_w_ff_up", FF_PAD),
                   w_ff_down=_weight_grad(ff_f, dr2, "wgrad_w_ff_down"))
    dr1, d_ln1_g, d_ln1_b, *from_sibling = _ff_up_bwd(
        d_a, d_b, wt["w_ff_gate"], wt["w_ff_up"], dr2, xhat1, rstd1, ln1_g, _sibling_swap_ride([contrib[k] for k in ff_weights]))
    ff_sums = pair_sums(ff_weights, contrib, from_sibling)

    d_ya, d_yssm, d_proj, d_attn, d_glu, d_ys, mixed, y_s, gy, d_bg = _mixer_bwd(
        dr1, proj, y_attn, y_ssm, glu, ys, wt["w_attn_br"], wt["w_ssm_br"], wt["w_glu"], wt["w_out"], b_gate_full)
    contrib.update(w_attn_br=_weight_grad(attn, d_ya, "wgrad_w_attn_br", 128),
                   w_ssm_br=_weight_grad(y_s, d_yssm, "wgrad_w_ssm_br", 128),
                   w_glu=_weight_grad(gy, d_glu, "wgrad_w_glu", 128),
                   w_out=_weight_grad(mixed, dr1, "wgrad_w_out"),
                   b_gate=d_bg.reshape(2, 4, 2, 128).transpose(2, 1, 0, 3))
    d_proj, *landed = _attn_bwd(q_pm, k_pm, v_pm, cos_t, sin_t, attn, lse, d_attn, d_proj,
                                _chip_swap_ride(ff_sums) + _sibling_swap_ride([contrib[k] for k in mixer_weights]))
    parts, own_sums = dict(zip(ff_weights, landed[:len(ff_weights)])), {}
    mixer_sums = pair_sums(mixer_weights, contrib, landed[len(ff_weights):])
    d_proj, d_bmat, d_cmat, d_abar, d_skip, *landed = _ssm_bwd(d_ys, proj, states, bmat, cmat, a_chunks, ssm_d, d_proj,
                                                               _chip_swap_ride(mixer_sums))
    parts.update(zip(mixer_weights, landed))

    gbb_re_t, gbb_im_t = _block_diag_parts(d_bmat, True)
    gc_re, gc_im = _block_diag_parts(d_cmat, False)
    ga_re = d_abar[:, 0, :CHUNK_STATES].reshape(SSM_GROUPS, SSM_STATE)
    ga_im = d_abar[:, 0, CHUNK_STATES:].reshape(SSM_GROUPS, SSM_STATE)
    g_a_re, g_a_im, g_log_dt, g_b_re_t, g_b_im_t = _ssm_param_bwd(
        a_re, a_im, log_dt, b_re_t, b_im_t, abar_re, abar_im, e_re, e_im, ga_re, ga_im, gbb_re_t, gbb_im_t)
    mine = [g_a_re, g_a_im, g_log_dt, g_b_re_t, g_b_im_t, gc_re, -gc_im,
            d_skip, d_ln1_g, d_ln1_b, d_ln2_g, d_ln2_b]
    small_packed = _pack_rows(mine + [loss_lanes])

    for_sibling, small_partly = _weight_grad_of_core(x2, d_proj, 1 - core, "wgrad_w_in_for_sibling", 896,
                                                     _gather_first_level([small_packed]))
    own_half, from_sibling, every = _weight_grad_of_core(
        x2, d_proj, core, "wgrad_w_in_own", 896,
        _sibling_swap_ride([for_sibling], halves=False) + _gather_second_level([small_partly]))
    w_in_sum, = _pair_sums([own_half], [from_sibling], core, "pair_sums_w_in")
    send_sems, recv_sems, w_in_sum, landing, token = _chip_swap_start([w_in_sum], "w_in_chip_swap_start")

    def adamw_of(k):
        taken = (lambda a: a.T) if k in narrow else (lambda a: a)
        return taken(local[k]), taken(given["m_" + k][0]), taken(given["v_" + k][0]), parts[k], own_sums.get(k), k in narrow

    others = [k for k in sharded if k != "w_in"]
    updated = _adamw_many([adamw_of(k) for k in others], "adamw_others", _after(token))
    grad_x, = _grad_x(d_proj, wt["w_in"], dr1, _after(token))

    def held(k, a):
        return a.transpose(0, 1, 3, 2) if k in ("ssm_b_re", "ssm_b_im") else a

    *small_grads, loss_sum = _unpack_rows(_sum_devices(every), [held(k, given[k]).shape for k in SMALL] + [(1, 128)])
    small = _adamw_replicated([held(k, given[k]) for k in SMALL], [held(k, given["m_" + k]) for k in SMALL],
                              [held(k, given["v_" + k]) for k in SMALL], small_grads)
    loss = loss_sum[0, 0]

    (own_sums["w_in"],), (parts["w_in"],) = _chip_swap_wait(
        send_sems, recv_sems, w_in_sum, landing, [grad_x, updated[0], small[0][0]], "w_in_chip_swap_wait")
    updated += _adamw_many([adamw_of("w_in")], "adamw_w_in")

    grads, deltas, new_m, new_v = {}, {}, {}, {}
    for i, k in enumerate(others + ["w_in"]):
        out = [o.T if k in narrow else o for o in updated[4 * i:4 * i + 4]]
        grads[k], deltas[k], new_m[k], new_v[k] = (o.reshape((1,) + local[k].shape) for o in out)
    for res, values in zip((grads, deltas, new_m, new_v), (small_grads,) + small):
        res.update((k, held(k, a)) for k, a in zip(SMALL, values))

    order = ("w_in", "b_gate", "w_attn_br", "w_ssm_br", "w_out", "ssm_a_re", "ssm_a_im", "ssm_log_dt", "ssm_b_re", "ssm_b_im",
             "ssm_c_re", "ssm_c_im", "ssm_d", "w_glu", "ln1_g", "ln1_b", "w_ff_gate", "w_ff_up", "w_ff_down", "ln2_g", "ln2_b")
    return (loss, grad_x[None], *[grads[k] for k in order], *[deltas[k] for k in order], *[new_m[k] for k in order],
            *[new_v[k] for k in order])
```

```python
import functools
import math

import jax
import jax.numpy as jnp
import numpy as np
from jax import lax
from jax.experimental import pallas as pl
from jax.experimental.pallas import tpu as pltpu

F32 = jnp.float32
BF16 = jnp.bfloat16

N_DEV = 8
SEQ = 2048
D_MODEL = 1024
HEAD_DIM = 64
ATTN_WIDTH = 512
QKV_WIDTH = 1536
SSM_WIDTH = 512
SSM_GROUPS = 32
SSM_GROUP = 16
SSM_STATE = 64
IN_WIDTH = 7168
D_FF = 2816
FF_SHARD = D_FF // N_DEV
FF_PAD = 384
D_FF_PAD = FF_PAD * N_DEV
DN_ALPHA = 2.0 ** 0.25
LN_EPS = 1e-5
NEG_INF = -1e30
ROPE_THETA = 10000.0
BLOCK = 128
GROUPS = ((1, 16), (4, 4), (16, 1))

ADAM_LR = 0.001
ADAM_B1 = 0.9
ADAM_B2 = 0.999
ADAM_EPS = 1e-08
ADAM_WD = 0.01
ADAM_STEP = 10

VMEM_LIMIT = 56 * 1024 * 1024


_pallas_call = pl.pallas_call


def _cparams(**kw):
    return pltpu.CompilerParams(vmem_limit_bytes=VMEM_LIMIT, **kw)


def _dot(a, b):
    return jnp.dot(a, b, preferred_element_type=F32)


def _dot_nt(a, b):
    return lax.dot_general(a, b, (((1,), (1,)), ((), ())), preferred_element_type=F32)


def _side_by_side(w_ref, row=None):
    rows = slice(None) if row is None else pl.ds(row, 1)
    return jnp.concatenate([w_ref[i, rows, :] for i in range(w_ref.shape[0])], axis=1)


def _dot_tn(a, b):
    return lax.dot_general(a, b, (((0,), (0,)), ((), ())), preferred_element_type=F32)


def _rope_tables():
    half = HEAD_DIM // 2
    inv_freq = np.float32(ROPE_THETA) ** (-np.arange(half, dtype=np.float32) / np.float32(half))
    ang = np.arange(SEQ, dtype=np.float32)[:, None] * inv_freq[None, :]
    cos, sin = np.cos(ang).astype(np.float32), np.sin(ang).astype(np.float32)
    tables = np.tile(cos, (1, 4)), np.tile(np.concatenate([-sin, sin], axis=1), (1, 2))

    def by_phase(t):
        return np.stack([t.reshape(SEQ // d, d, 128).transpose(1, 0, 2).reshape(SEQ, 128) for d, _ in GROUPS])

    return jnp.asarray(by_phase(tables[0])), jnp.asarray(by_phase(tables[1]))


def _swap_halves(x):
    lane = lax.broadcasted_iota(jnp.int32, x.shape, 1)
    return jnp.where((lane & 63) < 32, pltpu.roll(x, 96, axis=1), pltpu.roll(x, 32, axis=1))


def _group_rows(d, nb, r, i):
    src = pl.ds(i * BLOCK, BLOCK) if d == 1 else pl.ds(r + i * BLOCK * d, BLOCK, stride=d)
    return src, pl.ds((r * nb + i) * BLOCK, BLOCK)


def _attn_masks():
    a_idx = lax.broadcasted_iota(jnp.int32, (2 * BLOCK, 2 * BLOCK), 0) & (BLOCK - 1)
    c_idx = lax.broadcasted_iota(jnp.int32, (2 * BLOCK, 2 * BLOCK), 1)
    cur_ok = jnp.logical_and(c_idx >= BLOCK, c_idx - BLOCK <= a_idx)
    prev_ok = jnp.logical_and(c_idx < BLOCK, c_idx >= a_idx)
    lane = lax.broadcasted_iota(jnp.int32, (BLOCK, 128), 1)
    return cur_ok, prev_ok, lane < HEAD_DIM


def _stack_heads(t, head0):
    zero = jnp.zeros_like(t)
    return jnp.concatenate([jnp.where(head0, t, zero), jnp.where(head0, zero, t)], axis=0)


def _unstack_heads(t2, head0):
    return jnp.where(head0, t2[:BLOCK], t2[BLOCK:])


def _attn_fwd(proj, cos_t, sin_t, ride=None):
    def body(q0, q1, q2, k0, k1, k2, v0, v1, v2, cos_ref, sin_ref, attn_ref, lse_ref, qpm_ref, kpm_ref, vpm_ref,
             qs, ks, vs, os_, ms, ls, acc, mnat, lnat):
        cur_ok, prev_ok, head0 = _attn_masks()
        ks[:BLOCK, :] = jnp.zeros((BLOCK, 128), BF16)
        vs[:BLOCK, :] = jnp.zeros((BLOCK, 128), BF16)
        for g, (d, nb) in enumerate(GROUPS):
            q_ref, k_ref, v_ref = (q0, q1, q2)[g], (k0, k1, k2)[g], (v0, v1, v2)[g]
            for r in range(d):
                for i in range(nb):
                    src, dst = _group_rows(d, nb, r, i)
                    below = pl.ds(dst.start + BLOCK, BLOCK)
                    c, s = cos_ref[g, dst, :], sin_ref[g, dst, :]
                    q = q_ref[src, :]
                    k = k_ref[src, :]
                    qs[dst, :] = ((q * c + _swap_halves(q) * s) * 0.125).astype(BF16)
                    ks[below, :] = (k * c + _swap_halves(k) * s).astype(BF16)
                    vs[below, :] = v_ref[src, :].astype(BF16)
                    qpm_ref[g, dst, :], kpm_ref[g, dst, :], vpm_ref[g, dst, :] = qs[dst, :], ks[below, :], vs[below, :]

            def block(b, carry, nb=nb):
                has_prev = (b & (nb - 1)) > 0
                cur = pl.ds(pl.multiple_of(b * BLOCK, BLOCK), BLOCK)
                window = pl.ds(pl.multiple_of(b * BLOCK, BLOCK), 2 * BLOCK)
                valid = jnp.logical_or(cur_ok, jnp.logical_and(prev_ok, has_prev))
                s = jnp.where(valid, _dot_nt(_stack_heads(qs[cur, :], head0), ks[window, :]), NEG_INF)
                m = jnp.max(s, axis=1, keepdims=True)
                p = jnp.exp(s - m)
                os_[cur, :] = _unstack_heads(_dot(p.astype(BF16), vs[window, :]), head0)
                ms[cur, :] = _unstack_heads(m, head0)
                ls[cur, :] = _unstack_heads(jnp.sum(p, axis=1, keepdims=True), head0)
                return carry

            lax.fori_loop(0, SEQ // BLOCK, block, 0, unroll=16)

            for r in range(d):
                for i in range(nb):
                    src, dst = _group_rows(d, nb, r, i)
                    if g == 0:
                        acc[src, :], mnat[src, :], lnat[src, :] = os_[dst, :], ms[dst, :], ls[dst, :]
                    else:
                        m_old, m_g = mnat[src, :], ms[dst, :]
                        m_new = jnp.maximum(m_old, m_g)
                        a_old, a_g = jnp.exp(m_old - m_new), jnp.exp(m_g - m_new)
                        acc[src, :] = a_old * acc[src, :] + a_g * os_[dst, :]
                        lnat[src, :] = a_old * lnat[src, :] + a_g * ls[dst, :]
                        mnat[src, :] = m_new
        for i in range(SEQ // BLOCK):
            rows = pl.ds(i * BLOCK, BLOCK)
            l = lnat[rows, :]
            attn_ref[rows, :] = acc[rows, :] / l
            lse_ref[rows, :] = mnat[rows, :] + jnp.log(l)

    def col(base):
        return pl.BlockSpec((SEQ, 128), lambda hp, base=base: (0, base + hp))

    in_specs = [col(g * 4) for g in range(3)] + [col(12 + g * 4) for g in range(3)] + [col(24 + g * 4) for g in range(3)]
    table = pl.BlockSpec((3, SEQ, 128), lambda hp: (0, 0, 0), pipeline_mode=pl.Buffered(1))
    out = pl.BlockSpec((SEQ, 128), lambda hp: (0, hp))
    by_phase = pl.BlockSpec((3, SEQ, 128), lambda hp: (0, 0, hp))
    return _call(
        body, "attn_fwd", (4,), in_specs + [table, table], [out, out] + [by_phase] * 3,
        [_sds((SEQ, ATTN_WIDTH), F32), _sds((SEQ, ATTN_WIDTH), F32)] + [_sds((3, SEQ, ATTN_WIDTH), BF16)] * 3,
        [pltpu.VMEM((SEQ, 128), BF16)] + [pltpu.VMEM((SEQ + BLOCK, 128), BF16)] * 2 + [pltpu.VMEM((SEQ, 128), F32)] * 6,
        [proj] * 9 + [cos_t, sin_t], ride)


def _attn_bwd_group_body(g):
    d, nb = GROUPS[g]

    def body(qs_ref, ks_ref, vs_ref, cos_ref, sin_ref, lse_ref, dattn_ref, dsum_ref, dproj_ref,
             ks, vs, dos, lss, dss, dqs, dks, dvs, stage, outs, sems):
        cur_ok, prev_ok, head0 = _attn_masks()
        qs = qs_ref.at[g]
        ks[:BLOCK, :] = jnp.zeros((BLOCK, 128), BF16)
        vs[:BLOCK, :] = jnp.zeros((BLOCK, 128), BF16)
        dks[:BLOCK, :] = jnp.zeros((BLOCK, 128), F32)
        dvs[:BLOCK, :] = jnp.zeros((BLOCK, 128), F32)
        for r in range(d):
            for i in range(nb):
                src, dst = _group_rows(d, nb, r, i)
                below = pl.ds(dst.start + BLOCK, BLOCK)
                ks[below, :] = ks_ref[g, dst, :]
                vs[below, :] = vs_ref[g, dst, :]
                dos[dst, :] = dattn_ref[src, :].astype(BF16)
                for per_head, spread in ((dsum_ref[src, :], dss), (lse_ref[src, :], lss)):
                    other = pltpu.roll(per_head, HEAD_DIM, axis=1)
                    spread[0, dst, :] = jnp.where(head0, per_head, other)
                    spread[1, dst, :] = jnp.where(head0, other, per_head)
                dks[below, :] = jnp.zeros((BLOCK, 128), F32)
                dvs[below, :] = jnp.zeros((BLOCK, 128), F32)

        def per_stacked_row(spread, cur):
            h0, h1 = spread[0, cur, :], spread[1, cur, :]
            return jnp.concatenate([jnp.concatenate([h0, h0], axis=1), jnp.concatenate([h1, h1], axis=1)], axis=0)

        def block(b, carry):
            has_prev = (b & (nb - 1)) > 0
            cur = pl.ds(pl.multiple_of(b * BLOCK, BLOCK), BLOCK)
            window = pl.ds(pl.multiple_of(b * BLOCK, BLOCK), 2 * BLOCK)
            valid = jnp.logical_or(cur_ok, jnp.logical_and(prev_ok, has_prev))
            q2, do2 = _stack_heads(qs[cur, :], head0), _stack_heads(dos[cur, :], head0)
            kw, vw = ks[window, :], vs[window, :]
            s = jnp.where(valid, _dot_nt(q2, kw), NEG_INF)
            p = jnp.exp(s - per_stacked_row(lss, cur))
            ds = (p * (_dot_nt(do2, vw) - per_stacked_row(dss, cur))).astype(BF16)
            dvs[window, :] += _dot_tn(p.astype(BF16), do2)
            dks[window, :] += _dot_tn(ds, q2)
            dqs[cur, :] = _unstack_heads(_dot(ds, kw), head0)
            return carry

        lax.fori_loop(0, SEQ // BLOCK, block, 0, unroll=16)

        hp = pl.program_id(0)
        copies = []
        for kind in range(3):
            for r in range(d):
                for i in range(nb):
                    src, dst = _group_rows(d, nb, r, i)
                    below = pl.ds(dst.start + BLOCK, BLOCK)
                    if kind == 2:
                        stage[src, :] = dvs[below, :]
                    else:
                        c, s = cos_ref[g, dst, :], sin_ref[g, dst, :]
                        t = dqs[dst, :] * 0.125 if kind == 0 else dks[below, :]
                        stage[src, :] = t * c - _swap_halves(t) * s
            for i in range(SEQ // MM_ROWS):
                rows = pl.ds(i * MM_ROWS, MM_ROWS)
                outs[kind, rows, :] = stage[rows, :].astype(BF16)
            column = pl.multiple_of((kind * 12 + g * 4 + hp) * 128, 128)
            copies.append(pltpu.make_async_copy(outs.at[kind], dproj_ref.at[:, pl.ds(column, 128)], sems.at[kind]))
            copies[-1].start()
        for cp in copies:
            cp.wait()

    return body


def _attn_bwd(q_pm, k_pm, v_pm, cos_t, sin_t, attn, lse, dattn, dproj, ride=None):
    groups = [_attn_bwd_group_body(g) for g in range(3)]

    def body(qs_ref, ks_ref, vs_ref, cos_ref, sin_ref, attn_ref, lse_ref, dattn_ref, dproj_in, dproj_ref, dsum, *scratch):
        del dproj_in
        head0 = _attn_masks()[2]
        for i in range(SEQ // BLOCK):
            rows = pl.ds(i * BLOCK, BLOCK)
            prod = dattn_ref[rows, :] * attn_ref[rows, :]
            d0 = jnp.sum(jnp.where(head0, prod, 0.0), axis=1, keepdims=True)
            d1 = jnp.sum(jnp.where(head0, 0.0, prod), axis=1, keepdims=True)
            dsum[rows, :] = jnp.where(head0, d0, d1)
        for g in range(3):
            groups[g](qs_ref, ks_ref, vs_ref, cos_ref, sin_ref, lse_ref, dattn_ref, dsum, dproj_ref, *scratch)

    def col(base):
        return pl.BlockSpec((SEQ, 128), lambda hp, base=base: (0, base + hp))

    table = pl.BlockSpec((3, SEQ, 128), lambda hp: (0, 0, 0), pipeline_mode=pl.Buffered(1))
    by_phase = pl.BlockSpec((3, SEQ, 128), lambda hp: (0, 0, hp))
    return _call(
        body, "attn_bwd", (4,), [by_phase] * 3 + [table, table, col(0), col(0), col(0), ANY],
        [ANY], [_sds((SEQ, IN_WIDTH), BF16)],
        [pltpu.VMEM((SEQ, 128), F32)]
        + [pltpu.VMEM((SEQ + BLOCK, 128), BF16)] * 2 + [pltpu.VMEM((SEQ, 128), BF16)]
        + [pltpu.VMEM((2, SEQ, 128), F32)] * 2 + [pltpu.VMEM((SEQ, 128), F32)]
        + [pltpu.VMEM((SEQ + BLOCK, 128), F32)] * 2 + [pltpu.VMEM((SEQ, 128), F32)]
        + [pltpu.VMEM((3, SEQ, 128), BF16), pltpu.SemaphoreType.DMA((3,))],
        [q_pm, k_pm, v_pm, cos_t, sin_t, attn, lse, dattn, dproj], ride, aliases={8: 0})


SSM_CHUNKS = 4
CHUNK_STATES = 512
SCAN_ROWS = 8
U_COL = (3 * QKV_WIDTH) // 128


def _cmul(xr, xi, yr, yi):
    return xr * yr - xi * yi, xr * yi + xi * yr


def _ssm_prep(a_re, a_im, log_dt, b_re_t, b_im_t):
    def body(ar_ref, ai_ref, ldt_ref, br_ref, bi_ref, abr_ref, abi_ref, er_ref, ei_ref, bbr_ref, bbi_ref):
        ar, ai = ar_ref[...], ai_ref[...]
        dt = jnp.exp(ldt_ref[...])
        mag = jnp.exp(ar * dt)
        abr, abi = mag * jnp.cos(ai * dt), mag * jnp.sin(ai * dt)
        den = ar * ar + ai * ai
        nr, ni = abr - 1.0, abi
        er, ei = (nr * ar + ni * ai) / den, (ni * ar - nr * ai) / den
        abr_ref[...], abi_ref[...], er_ref[...], ei_ref[...] = abr, abi, er, ei
        er3, ei3 = er[:, None, :], ei[:, None, :]
        br, bi = br_ref[...], bi_ref[...]
        bbr_ref[...] = er3 * br - ei3 * bi
        bbi_ref[...] = er3 * bi + ei3 * br

    gp = jax.ShapeDtypeStruct(a_re.shape, F32)
    gb = jax.ShapeDtypeStruct(b_re_t.shape, F32)
    return _pallas_call(body, name="ssm_prep", out_shape=(gp, gp, gp, gp, gb, gb))(a_re, a_im, log_dt, b_re_t, b_im_t)


def _ssm_param_bwd(a_re, a_im, log_dt, b_re_t, b_im_t, abar_re, abar_im, e_re, e_im, ga_re, ga_im, gbb_re_t, gbb_im_t):
    def body(ar_ref, ai_ref, ldt_ref, br_ref, bi_ref, abr_ref, abi_ref, er_ref, ei_ref, gar_ref, gai_ref, gbr_ref, gbi_ref,
             o_ar, o_ai, o_ldt, o_br, o_bi):
        ar, ai = ar_ref[...], ai_ref[...]
        dt = jnp.exp(ldt_ref[...])
        er, ei = er_ref[...], ei_ref[...]
        br, bi, gbr, gbi = br_ref[...], bi_ref[...], gbr_ref[...], gbi_ref[...]
        er3, ei3 = er[:, None, :], ei[:, None, :]
        o_br[...] = er3 * gbr + ei3 * gbi
        o_bi[...] = er3 * gbi - ei3 * gbr
        ge_r = jnp.sum(br * gbr + bi * gbi, axis=1)
        ge_i = jnp.sum(br * gbi - bi * gbr, axis=1)
        den = ar * ar + ai * ai
        ilr, ili = ar / den, -ai / den
        t_r, t_i = _cmul(ilr, -ili, ge_r, ge_i)
        gab_r, gab_i = gar_ref[...] + t_r, gai_ref[...] + t_i
        gz_r, gz_i = _cmul(abr_ref[...], -abi_ref[...], gab_r, gab_i)
        el_r, el_i = _cmul(er, ei, ilr, ili)
        u_r, u_i = _cmul(el_r, -el_i, ge_r, ge_i)
        o_ar[...] = dt * gz_r - u_r
        o_ai[...] = dt * gz_i - u_i
        o_ldt[...] = jnp.sum(gz_r * ar + gz_i * ai, axis=1, keepdims=True) * dt

    gp = jax.ShapeDtypeStruct(a_re.shape, F32)
    gb = jax.ShapeDtypeStruct(b_re_t.shape, F32)
    return _pallas_call(body, name="ssm_param_bwd", out_shape=(gp, gp, jax.ShapeDtypeStruct(log_dt.shape, F32), gb, gb))(
        a_re, a_im, log_dt, b_re_t, b_im_t, abar_re, abar_im, e_re, e_im, ga_re, ga_im, gbb_re_t, gbb_im_t)


def _block_diag(blocks_re, blocks_im, sign_im, rows_are_channels):
    both = jnp.stack([blocks_re, sign_im * blocks_im]).reshape(2, SSM_CHUNKS, 8, SSM_GROUP, SSM_STATE)
    eye = jnp.eye(8, dtype=F32)
    if rows_are_channels:
        return jnp.einsum("rcghp,gk->cghrkp", both, eye).reshape(SSM_CHUNKS, 128, 2 * CHUNK_STATES)
    return jnp.einsum("rcghp,gk->crkpgh", both, eye).reshape(SSM_CHUNKS, 2 * CHUNK_STATES, 128)


def _block_diag_parts(mat, rows_are_channels):
    if rows_are_channels:
        six = mat.reshape(SSM_CHUNKS, 8, SSM_GROUP, 2, 8, SSM_STATE)
        parts = jnp.einsum("cghrgp->rcghp", six)
    else:
        six = mat.reshape(SSM_CHUNKS, 2, 8, SSM_STATE, 8, SSM_GROUP)
        parts = jnp.einsum("crgpgh->rcghp", six)
    parts = parts.reshape(2, SSM_GROUPS, SSM_GROUP, SSM_STATE)
    return parts[0], parts[1]


def _scan_consts(a_ref, conj, reverse):
    ar = jnp.broadcast_to(a_ref[:, :CHUNK_STATES], (SCAN_ROWS, CHUNK_STATES))
    ai = jnp.broadcast_to(a_ref[:, CHUNK_STATES:], (SCAN_ROWS, CHUNK_STATES))
    if conj:
        ai = -ai
    row = lax.broadcasted_iota(jnp.int32, (SCAN_ROWS, CHUNK_STATES), 0)
    if reverse:
        row = SCAN_ROWS - 1 - row
    zero = jnp.zeros_like(ar)
    steps = []
    pr, pi = ar, ai
    for shift in (1, 2, 4):
        keep = row >= shift
        steps.append((SCAN_ROWS - shift if reverse else shift, jnp.where(keep, pr, zero), jnp.where(keep, pi, zero)))
        pr, pi = _cmul(pr, pi, pr, pi)
    first = row == 0
    return steps, (jnp.where(first, ar, zero), jnp.where(first, ai, zero)), first


def _scan_tile(xr, xi, prev_r, prev_i, steps, carry_in, reverse):
    edge = SCAN_ROWS - 1 if reverse else 1
    cr, ci = pltpu.roll(prev_r, edge, axis=0), pltpu.roll(prev_i, edge, axis=0)
    xr, xi = xr + carry_in[0] * cr - carry_in[1] * ci, xi + carry_in[0] * ci + carry_in[1] * cr
    for shift, mr, mi in steps:
        sr, si = pltpu.roll(xr, shift, axis=0), pltpu.roll(xi, shift, axis=0)
        xr, xi = xr + mr * sr - mi * si, xi + mr * si + mi * sr
    return xr, xi


MM_ROWS = 256


def _ssm_fwd(proj, bmat, cmat, a_chunks, d_skip, ride=None):
    def body(u_ref, b_ref, c_ref, a_ref, d_ref, y_ref, states_ref, h_ref):
        for i in range(SEQ // MM_ROWS):
            rows = pl.ds(i * MM_ROWS, MM_ROWS)
            h_ref[rows, :] = _dot(u_ref[rows, :].astype(BF16), b_ref[...])
        steps, carry_in, _ = _scan_consts(a_ref, conj=False, reverse=False)

        def tile(k, carry):
            rows = pl.ds(pl.multiple_of(k * SCAN_ROWS, SCAN_ROWS), SCAN_ROWS)
            xr, xi = _scan_tile(h_ref[rows, :CHUNK_STATES], h_ref[rows, CHUNK_STATES:], carry[0], carry[1], steps, carry_in, False)
            h_ref[rows, :CHUNK_STATES] = xr
            h_ref[rows, CHUNK_STATES:] = xi
            return xr, xi

        zero = jnp.zeros((SCAN_ROWS, CHUNK_STATES), F32)
        lax.fori_loop(0, SEQ // SCAN_ROWS, tile, (zero, zero), unroll=4)
        for i in range(SEQ // MM_ROWS):
            rows = pl.ds(i * MM_ROWS, MM_ROWS)
            states = h_ref[rows, :].astype(BF16)
            states_ref[rows, :] = states
            y_ref[rows, :] = _dot(states, c_ref[...]) + d_ref[...] * u_ref[rows, :]

    return _call(
        body, "ssm_fwd", (SSM_CHUNKS,),
        [pl.BlockSpec((SEQ, 128), lambda c: (0, U_COL + c)),
         pl.BlockSpec((None, 128, 2 * CHUNK_STATES), lambda c: (c, 0, 0)),
         pl.BlockSpec((None, 2 * CHUNK_STATES, 128), lambda c: (c, 0, 0)),
         pl.BlockSpec((None, 1, 2 * CHUNK_STATES), lambda c: (c, 0, 0)),
         pl.BlockSpec((1, 128), lambda c: (0, c))],
        [pl.BlockSpec((SEQ, 128), lambda c: (0, c)), pl.BlockSpec((SEQ, 2 * CHUNK_STATES), lambda c: (0, c))],
        [_sds((SEQ, SSM_WIDTH), F32), _sds((SEQ, SSM_CHUNKS * 2 * CHUNK_STATES), BF16)],
        [pltpu.VMEM((SEQ, 2 * CHUNK_STATES), F32)],
        [proj, bmat, cmat, a_chunks, d_skip], ride)


def _ssm_bwd(dys, proj, h, bmat, cmat, a_chunks, d_skip, dproj, ride=None):
    def body(dy_ref, u_ref, states_ref, b_ref, c_ref, a_ref, d_ref, dproj_in, du_ref, db_ref, dc_ref, da_ref, dd_ref,
             g_ref, h_ref):
        del dproj_in
        dsum = jnp.zeros((1, 128), F32)
        dcm = jnp.zeros((2 * CHUNK_STATES, 128), F32)
        for i in range(SEQ // MM_ROWS):
            rows = pl.ds(i * MM_ROWS, MM_ROWS)
            h_ref[rows, :] = states_ref[rows, :].astype(F32)
            dy = dy_ref[rows, :]
            g_ref[rows, :] = _dot_nt(dy.astype(BF16), c_ref[...])
            dsum += jnp.sum(dy * u_ref[rows, :], axis=0, keepdims=True)
            dcm += _dot_tn(states_ref[rows, :], dy.astype(BF16))
        dd_ref[...] = dsum
        dc_ref[...] = dcm
        steps, carry_in, _ = _scan_consts(a_ref, conj=True, reverse=True)
        first_row = lax.broadcasted_iota(jnp.int32, (SCAN_ROWS, CHUNK_STATES), 0) == 0
        n_tiles = SEQ // SCAN_ROWS

        def tile(j, carry):
            k = n_tiles - 1 - j
            rows = pl.ds(pl.multiple_of(k * SCAN_ROWS, SCAN_ROWS), SCAN_ROWS)
            before = pl.ds(pl.multiple_of(jnp.maximum(k - 1, 0) * SCAN_ROWS, SCAN_ROWS), SCAN_ROWS)
            gr, gi = _scan_tile(g_ref[rows, :CHUNK_STATES], g_ref[rows, CHUNK_STATES:], carry[0], carry[1], steps, carry_in, True)
            g_ref[rows, :CHUNK_STATES] = gr
            g_ref[rows, CHUNK_STATES:] = gi
            has_before = jnp.where(k > 0, 1.0, 0.0)
            hr = jnp.where(first_row, pltpu.roll(h_ref[before, :CHUNK_STATES], 1, axis=0) * has_before,
                           pltpu.roll(h_ref[rows, :CHUNK_STATES], 1, axis=0))
            hi = jnp.where(first_row, pltpu.roll(h_ref[before, CHUNK_STATES:], 1, axis=0) * has_before,
                           pltpu.roll(h_ref[rows, CHUNK_STATES:], 1, axis=0))
            return gr, gi, carry[2] + hr * gr + hi * gi, carry[3] + hr * gi - hi * gr

        zero = jnp.zeros((SCAN_ROWS, CHUNK_STATES), F32)
        _, _, sar, sai = lax.fori_loop(0, n_tiles, tile, (zero, zero, zero, zero), unroll=4)
        da_ref[:, :CHUNK_STATES] = jnp.sum(sar, axis=0, keepdims=True)
        da_ref[:, CHUNK_STATES:] = jnp.sum(sai, axis=0, keepdims=True)
        dbm = jnp.zeros((128, 2 * CHUNK_STATES), F32)
        for i in range(SEQ // MM_ROWS):
            rows = pl.ds(i * MM_ROWS, MM_ROWS)
            g = g_ref[rows, :].astype(BF16)
            du_ref[rows, :] = (_dot_nt(g, b_ref[...]) + d_ref[...] * dy_ref[rows, :]).astype(BF16)
            dbm += _dot_tn(u_ref[rows, :].astype(BF16), g)
        db_ref[...] = dbm

    chunk_col = pl.BlockSpec((SEQ, 128), lambda c: (0, c))
    return _call(
        body, "ssm_bwd", (SSM_CHUNKS,),
        [chunk_col,
         pl.BlockSpec((SEQ, 128), lambda c: (0, U_COL + c)),
         pl.BlockSpec((SEQ, 2 * CHUNK_STATES), lambda c: (0, c)),
         pl.BlockSpec((None, 128, 2 * CHUNK_STATES), lambda c: (c, 0, 0)),
         pl.BlockSpec((None, 2 * CHUNK_STATES, 128), lambda c: (c, 0, 0)),
         pl.BlockSpec((None, 1, 2 * CHUNK_STATES), lambda c: (c, 0, 0)),
         pl.BlockSpec((1, 128), lambda c: (0, c)), ANY],
        [pl.BlockSpec((SEQ, 128), lambda c: (0, U_COL + c)),
         pl.BlockSpec((None, 128, 2 * CHUNK_STATES), lambda c: (c, 0, 0)),
         pl.BlockSpec((None, 2 * CHUNK_STATES, 128), lambda c: (c, 0, 0)),
         pl.BlockSpec((None, 1, 2 * CHUNK_STATES), lambda c: (c, 0, 0)),
         pl.BlockSpec((1, 128), lambda c: (0, c))],
        [_sds((SEQ, IN_WIDTH), BF16), _sds((SSM_CHUNKS, 128, 2 * CHUNK_STATES), F32),
         _sds((SSM_CHUNKS, 2 * CHUNK_STATES, 128), F32), _sds((SSM_CHUNKS, 1, 2 * CHUNK_STATES), F32), _sds((1, SSM_WIDTH), F32)],
        [pltpu.VMEM((SEQ, 2 * CHUNK_STATES), F32)] * 2, [dys, proj, h, bmat, cmat, a_chunks, d_skip, dproj], ride, aliases={7: 0})


def _ssm_tables(abar_re, abar_im, bbar_re_t, bbar_im_t, c_re, c_im):
    bmat = _block_diag(bbar_re_t, bbar_im_t, 1.0, True).astype(BF16)
    cmat = _block_diag(c_re, c_im, -1.0, False).astype(BF16)
    a_chunks = jnp.concatenate([abar_re.reshape(SSM_CHUNKS, 1, CHUNK_STATES), abar_im.reshape(SSM_CHUNKS, 1, CHUNK_STATES)], axis=2)
    return bmat, cmat, a_chunks


GL_COL = (3 * QKV_WIDTH + SSM_WIDTH) // D_MODEL
GELU_C = math.sqrt(2.0 / math.pi)
GELU_A = 0.044715


def _sds(shape, dtype):
    return jax.ShapeDtypeStruct(shape, dtype)


def _gelu(x):
    t = jnp.tanh(GELU_C * (x + GELU_A * x * x * x))
    return 0.5 * x * (1.0 + t), t


def _gelu_grad(x, t):
    return 0.5 * (1.0 + t) + 0.5 * x * (1.0 - t * t) * GELU_C * (1.0 + 3.0 * GELU_A * x * x)


def _layer_norm(r, g, b):
    mu = jnp.mean(r, axis=-1, keepdims=True)
    xc = r - mu
    rstd = lax.rsqrt(jnp.mean(xc * xc, axis=-1, keepdims=True) + LN_EPS)
    xhat = xc * rstd
    return xhat * g + b, xhat, rstd


def _layer_norm_bwd(dy, xhat, rstd, g):
    dxhat = dy * g
    m1 = jnp.mean(dxhat, axis=-1, keepdims=True)
    m2 = jnp.mean(dxhat * xhat, axis=-1, keepdims=True)
    return rstd * (dxhat - m1 - xhat * m2)


def _proj(x, w_in, ride=None):
    tm, tn = 1024, 1792

    def body(x_ref, w_ref, o_ref):
        o_ref[...] = _dot(x_ref[...].astype(BF16), _side_by_side(w_ref))

    return _call(
        body, "proj", (SEQ // tm, IN_WIDTH // tn),
        [pl.BlockSpec((tm, D_MODEL), lambda i, j: (i, 0)), pl.BlockSpec((2, D_MODEL, tn // 2), lambda i, j: (j, 0, 0))],
        [pl.BlockSpec((tm, tn), lambda i, j: (i, j))], [_sds((SEQ, IN_WIDTH), F32)], [], [x, w_in], ride)


def _row_spec(tm, width, col=0):
    return pl.BlockSpec((tm, width), lambda i, col=col: (i, col))


def _full_spec(shape):
    return pl.BlockSpec(shape, lambda i: (0,) * len(shape))


def _weight_spec(shape):
    return pl.BlockSpec(shape, lambda i: (0,) * len(shape), pipeline_mode=pl.Buffered(1))


def _mixer_out(attn, ys, proj, x, w_ab, w_sb, w_glu, w_out, b_gate, ln_g, ln_b, ride=None):
    tm = 512

    def body(attn_ref, ys_ref, gl0_ref, gl1_ref, x_ref, wab_ref, wsb_ref, wglu_ref, wout_ref, bg_ref, g_ref, b_ref,
             h_ref, xhat_ref, rstd_ref, glu_ref, ya_ref, yssm_ref):
        gy, _ = _gelu(ys_ref[...])
        glu = _dot(gy.astype(BF16), _side_by_side(wglu_ref))
        glu_ref[...] = glu.astype(BF16)
        y_s = glu[:, :SSM_WIDTH] * jax.nn.sigmoid(glu[:, SSM_WIDTH:])
        y_ssm = _dot(y_s.astype(BF16), _side_by_side(wsb_ref))
        y_attn = _dot(attn_ref[...].astype(BF16), _side_by_side(wab_ref))
        ya_ref[...] = y_attn.astype(BF16)
        yssm_ref[...] = y_ssm.astype(BF16)
        g0 = jax.nn.sigmoid(gl0_ref[...] + _side_by_side(bg_ref, 0))
        g1 = jax.nn.sigmoid(gl1_ref[...] + _side_by_side(bg_ref, 1))
        mixed = g0 * y_attn + g1 * y_ssm
        r1 = DN_ALPHA * x_ref[...] + _dot(mixed.astype(BF16), wout_ref[...])
        h, xhat, rstd = _layer_norm(r1, g_ref[...], b_ref[...])
        h_ref[...] = h
        xhat_ref[...] = xhat
        rstd_ref[...] = jnp.broadcast_to(rstd, (tm, 128))

    wide = _sds((SEQ, D_MODEL), F32)
    return _call(
        body, "mixer_out", (SEQ // tm,),
        [_row_spec(tm, ATTN_WIDTH), _row_spec(tm, SSM_WIDTH), _row_spec(tm, D_MODEL, GL_COL), _row_spec(tm, D_MODEL, GL_COL + 1),
         _row_spec(tm, D_MODEL), _weight_spec((N_DEV, ATTN_WIDTH, 128)), _weight_spec((N_DEV, SSM_WIDTH, 128)),
         _weight_spec((N_DEV, SSM_WIDTH, 128)), _weight_spec((D_MODEL, D_MODEL)), _full_spec((N_DEV, 2, 128)),
         _full_spec((1, D_MODEL)), _full_spec((1, D_MODEL))],
        [_row_spec(tm, D_MODEL), _row_spec(tm, D_MODEL), _row_spec(tm, 128), _row_spec(tm, D_MODEL),
         _row_spec(tm, D_MODEL), _row_spec(tm, D_MODEL)],
        [wide, wide, _sds((SEQ, 128), F32)] + [_sds((SEQ, D_MODEL), BF16)] * 3, [],
        [attn, ys, proj, proj, x, w_ab, w_sb, w_glu, w_out, b_gate, ln_g, ln_b], ride)


def _ff_up(h, w_gate, w_up, ride=None):
    tm, tn = 1024, 768

    def body(h_ref, wg_ref, wu_ref, a_ref, b_ref, f_ref):
        hb = h_ref[...].astype(BF16)
        a, b = _dot(hb, _side_by_side(wg_ref)), _dot(hb, _side_by_side(wu_ref))
        a_ref[...] = a.astype(BF16)
        b_ref[...] = b.astype(BF16)
        f_ref[...] = (a * jax.nn.sigmoid(a) * b).astype(BF16)

    tile = pl.BlockSpec((tm, tn), lambda i, j: (i, j))
    wtile = pl.BlockSpec((tn // FF_PAD, D_MODEL, FF_PAD), lambda i, j: (j, 0, 0))
    out = _sds((SEQ, D_FF_PAD), BF16)
    return _call(body, "ff_up", (SEQ // tm, D_FF_PAD // tn), [pl.BlockSpec((tm, D_MODEL), lambda i, j: (i, 0)), wtile, wtile],
                 [tile, tile, tile], [out, out, out], [], [h, w_gate, w_up], ride)


def _ff_down_loss(f, w_down, h, target, ln_g, ln_b):
    tm = 512

    def body(f_ref, w_ref, h_ref, t_ref, g_ref, b_ref, dr_ref, dg_ref, db_ref, loss_ref):
        @pl.when(pl.program_id(0) == 0)
        def _():
            dg_ref[...] = jnp.zeros_like(dg_ref)
            db_ref[...] = jnp.zeros_like(db_ref)
            loss_ref[...] = jnp.zeros_like(loss_ref)

        r2 = DN_ALPHA * h_ref[...] + _dot(f_ref[...], w_ref[...])
        g = g_ref[...]
        out, xhat, rstd = _layer_norm(r2, g, b_ref[...])
        err = out - t_ref[...]
        loss_ref[...] += 0.5 * jnp.sum(jnp.mean(err * err, axis=-1, keepdims=True), axis=0, keepdims=True)
        dout = err * (1.0 / D_MODEL)
        dg_ref[...] += jnp.sum(dout * xhat, axis=0, keepdims=True)
        db_ref[...] += jnp.sum(dout, axis=0, keepdims=True)
        dr_ref[...] = _layer_norm_bwd(dout, xhat, rstd, g)

    vec = _sds((1, D_MODEL), F32)
    return _pallas_call(
        body, name="ff_down_loss", grid=(SEQ // tm,),
        in_specs=[_row_spec(tm, D_FF_PAD), _weight_spec((D_FF_PAD, D_MODEL)), _row_spec(tm, D_MODEL), _row_spec(tm, D_MODEL),
                  _full_spec((1, D_MODEL)), _full_spec((1, D_MODEL))],
        out_specs=(_row_spec(tm, D_MODEL), _full_spec((1, D_MODEL)), _full_spec((1, D_MODEL)), _full_spec((1, 128))),
        out_shape=(_sds((SEQ, D_MODEL), F32), vec, vec, _sds((1, 128), F32)),
        compiler_params=_cparams(dimension_semantics=("arbitrary",)),
    )(f, w_down, h, target, ln_g, ln_b)


def _ff_down_bwd(dr2, w_down, a, b):
    tm, tn = 1024, 768

    def body(dr_ref, w_ref, a_ref, b_ref, da_ref, db_ref):
        df = _dot_nt(dr_ref[...].astype(BF16), w_ref[...])
        av, bv = a_ref[...].astype(F32), b_ref[...].astype(F32)
        sg = jax.nn.sigmoid(av)
        da_ref[...] = (df * bv * sg * (1.0 + av * (1.0 - sg))).astype(BF16)
        db_ref[...] = (df * av * sg).astype(BF16)

    tile = pl.BlockSpec((tm, tn), lambda i, j: (i, j))
    out = _sds((SEQ, D_FF_PAD), BF16)
    return _pallas_call(
        body, name="ff_down_bwd", grid=(SEQ // tm, D_FF_PAD // tn),
        in_specs=[pl.BlockSpec((tm, D_MODEL), lambda i, j: (i, 0)), pl.BlockSpec((tn, D_MODEL), lambda i, j: (j, 0)), tile, tile],
        out_specs=(tile, tile), out_shape=(out, out),
        compiler_params=_cparams(dimension_semantics=("arbitrary", "arbitrary")),
    )(dr2, w_down, a, b)


def _ff_up_bwd(da, db, w_gate, w_up, dr2, xhat1, rstd1, ln_g, ride=None):
    tm, tk = 1024, 768
    nk = D_FF_PAD // tk

    def body(da_ref, db_ref, wg_ref, wu_ref, dr2_ref, xhat_ref, rstd_ref, g_ref, dr1_ref, dg_ref, dbias_ref, acc):
        i, k = pl.program_id(0), pl.program_id(1)

        @pl.when(jnp.logical_and(i == 0, k == 0))
        def _():
            dg_ref[...] = jnp.zeros_like(dg_ref)
            dbias_ref[...] = jnp.zeros_like(dbias_ref)

        part = _dot_nt(da_ref[...], _side_by_side(wg_ref)) + _dot_nt(db_ref[...], _side_by_side(wu_ref))

        @pl.when(k == 0)
        def _():
            acc[...] = part

        @pl.when(k > 0)
        def _():
            acc[...] += part

        @pl.when(k == nk - 1)
        def _():
            dh = DN_ALPHA * dr2_ref[...] + acc[...]
            xhat = xhat_ref[...]
            dg_ref[...] += jnp.sum(dh * xhat, axis=0, keepdims=True)
            dbias_ref[...] += jnp.sum(dh, axis=0, keepdims=True)
            rstd = jnp.max(rstd_ref[...], axis=1, keepdims=True)
            dr1_ref[...] = _layer_norm_bwd(dh, xhat, rstd, g_ref[...])

    hid = pl.BlockSpec((tm, tk), lambda i, k: (i, k))
    wtile = pl.BlockSpec((tk // FF_PAD, D_MODEL, FF_PAD), lambda i, k: (k, 0, 0))
    row = pl.BlockSpec((tm, D_MODEL), lambda i, k: (i, 0))
    vec = pl.BlockSpec((1, D_MODEL), lambda i, k: (0, 0))
    return _call(
        body, "ff_up_bwd", (SEQ // tm, nk),
        [hid, hid, wtile, wtile, row, row, pl.BlockSpec((tm, 128), lambda i, k: (i, 0)), vec],
        [row, vec, vec], [_sds((SEQ, D_MODEL), F32), _sds((1, D_MODEL), F32), _sds((1, D_MODEL), F32)],
        [pltpu.VMEM((tm, D_MODEL), F32)], [da, db, w_gate, w_up, dr2, xhat1, rstd1, ln_g], ride)


def _mixer_bwd(dr1, proj, y_attn, y_ssm, glu, ys, w_ab, w_sb, w_glu, w_out, b_gate):
    tm = 256

    def body(dr1_ref, gl0_ref, gl1_ref, ya_ref, yssm_ref, glu_ref, ys_ref, wab_ref, wsb_ref, wglu_ref, wout_ref, bg_ref,
             dya_ref, dyssm_ref, dgl_ref, dattn_ref, dglu_ref, dys_ref, mixed_ref, ysb_ref, gy_ref, dbg_ref, stage, copied):
        @pl.when(pl.program_id(0) == 0)
        def _():
            dbg_ref[...] = jnp.zeros_like(dbg_ref)

        dmixed = _dot_nt(dr1_ref[...].astype(BF16), wout_ref[...])
        g0 = jax.nn.sigmoid(gl0_ref[...] + _side_by_side(bg_ref, 0))
        g1 = jax.nn.sigmoid(gl1_ref[...] + _side_by_side(bg_ref, 1))
        y_attn, y_ssm = ya_ref[...].astype(F32), yssm_ref[...].astype(F32)
        mixed_ref[...] = (g0 * y_attn + g1 * y_ssm).astype(BF16)
        dya = (dmixed * g0).astype(BF16)
        dyssm = (dmixed * g1).astype(BF16)
        dya_ref[...] = dya
        dyssm_ref[...] = dyssm
        dgl0 = dmixed * y_attn * g0 * (1.0 - g0)
        dgl1 = dmixed * y_ssm * g1 * (1.0 - g1)
        i, last = pl.program_id(0), SEQ // tm - 1
        slot = i & 1

        def copy_out(buffer, tile):
            window = dgl_ref.at[pl.ds(pl.multiple_of(tile * tm, tm), tm), pl.ds(GL_COL * D_MODEL, 2 * D_MODEL)]
            return pltpu.make_async_copy(stage.at[buffer], window, copied.at[buffer])

        @pl.when(i >= 2)
        def _():
            copy_out(slot, i - 2).wait()

        stage[slot, :, :D_MODEL] = dgl0.astype(BF16)
        stage[slot, :, D_MODEL:] = dgl1.astype(BF16)
        copy_out(slot, i).start()

        @pl.when(i == last)
        def _():
            copy_out(1 - slot, i - 1).wait()
            copy_out(slot, i).wait()
        dbg_ref[:, :D_MODEL] += jnp.sum(dgl0, axis=0, keepdims=True)
        dbg_ref[:, D_MODEL:] += jnp.sum(dgl1, axis=0, keepdims=True)
        dattn_ref[...] = _dot_nt(dya, _side_by_side(wab_ref))
        dy_s = _dot_nt(dyssm, _side_by_side(wsb_ref))
        glu = glu_ref[...].astype(F32)
        glu1, sg = glu[:, :SSM_WIDTH], jax.nn.sigmoid(glu[:, SSM_WIDTH:])
        ysb_ref[...] = (glu1 * sg).astype(BF16)
        dglu1 = (dy_s * sg).astype(BF16)
        dglu2 = (dy_s * glu1 * sg * (1.0 - sg)).astype(BF16)
        dglu_ref[:, :SSM_WIDTH] = dglu1
        dglu_ref[:, SSM_WIDTH:] = dglu2
        dgy = _dot_nt(jnp.concatenate([dglu1, dglu2], axis=1), _side_by_side(wglu_ref))
        ys = ys_ref[...]
        gy, t = _gelu(ys)
        gy_ref[...] = gy.astype(BF16)
        dys_ref[...] = dgy * _gelu_grad(ys, t)

    wide_b, half_b = _sds((SEQ, D_MODEL), BF16), _sds((SEQ, SSM_WIDTH), BF16)
    half_f = _sds((SEQ, SSM_WIDTH), F32)
    return _pallas_call(
        body, name="mixer_bwd", grid=(SEQ // tm,),
        in_specs=[_row_spec(tm, D_MODEL), _row_spec(tm, D_MODEL, GL_COL), _row_spec(tm, D_MODEL, GL_COL + 1), _row_spec(tm, D_MODEL),
                  _row_spec(tm, D_MODEL), _row_spec(tm, D_MODEL), _row_spec(tm, SSM_WIDTH), _full_spec((N_DEV, ATTN_WIDTH, 128)),
                  _full_spec((N_DEV, SSM_WIDTH, 128)), _full_spec((N_DEV, SSM_WIDTH, 128)), _full_spec((D_MODEL, D_MODEL)),
                  _full_spec((N_DEV, 2, 128))],
        out_specs=(_row_spec(tm, D_MODEL), _row_spec(tm, D_MODEL), ANY, _row_spec(tm, ATTN_WIDTH),
                   _row_spec(tm, D_MODEL), _row_spec(tm, SSM_WIDTH), _row_spec(tm, D_MODEL), _row_spec(tm, SSM_WIDTH),
                   _row_spec(tm, SSM_WIDTH), _full_spec((1, 2 * D_MODEL))),
        out_shape=(wide_b, wide_b, _sds((SEQ, IN_WIDTH), BF16), half_f, wide_b, half_f, wide_b, half_b, half_b,
                   _sds((1, 2 * D_MODEL), F32)),
        scratch_shapes=[pltpu.VMEM((2, tm, 2 * D_MODEL), BF16), pltpu.SemaphoreType.DMA((2,))],
        compiler_params=_cparams(dimension_semantics=("arbitrary",)),
    )(dr1, proj, proj, y_attn, y_ssm, glu, ys, w_ab, w_sb, w_glu, w_out, b_gate)


def _grad_x(dproj, w_in, dr1, ride=None):
    tm, tk = 1024, 1792
    nk = IN_WIDTH // tk

    def body(dp_ref, w_ref, dr1_ref, o_ref, acc):
        k = pl.program_id(1)
        part = _dot_nt(dp_ref[...], _side_by_side(w_ref))

        @pl.when(k == 0)
        def _():
            acc[...] = part

        @pl.when(k > 0)
        def _():
            acc[...] += part

        @pl.when(k == nk - 1)
        def _():
            o_ref[...] = DN_ALPHA * dr1_ref[...] + acc[...]

    row = pl.BlockSpec((tm, D_MODEL), lambda i, k: (i, 0))
    return _call(
        body, "grad_x", (SEQ // tm, nk),
        [pl.BlockSpec((tm, tk), lambda i, k: (i, k)), pl.BlockSpec((2, D_MODEL, tk // 2), lambda i, k: (k, 0, 0)), row],
        [row], [_sds((SEQ, D_MODEL), F32)], [pltpu.VMEM((tm, D_MODEL), F32)], [dproj, w_in, dr1], ride)


def _weight_grad(a, b, name, shard_cols=None):
    k, n = a.shape[1], b.shape[1]
    tk = k if shard_cols else k // N_DEV
    tn = n // 4 if shard_cols else min(n, 1024)

    def body(a_ref, b_ref, o_ref):
        grad = _dot_tn(a_ref[...].astype(BF16), b_ref[...].astype(BF16))
        if shard_cols:
            o_ref[0] = grad[:, :shard_cols].astype(BF16)
            o_ref[1] = grad[:, shard_cols:].astype(BF16)
        else:
            o_ref[...] = grad.astype(BF16)

    if shard_cols:
        out_spec = pl.BlockSpec((2, None, tk, shard_cols), lambda kk, j: (0, j, kk, 0))
        out_shape = _sds((2, 4, k, shard_cols), BF16)
    else:
        out_spec = pl.BlockSpec((None, None, tk, tn), lambda kk, j: (kk % 2, kk // 2, 0, j))
        out_shape = _sds((2, 4, tk, n), BF16)
    return _call(body, name, (k // tk, n // tn),
                 [pl.BlockSpec((SEQ, tk), lambda kk, j: (0, kk)), pl.BlockSpec((SEQ, tn), lambda kk, j: (0, j))],
                 [out_spec], [out_shape], [], [a, b])[0]


def _weight_grad_rows(a, b, core, name, shard_cols, first, count, ride, in_place=False):
    k = a.shape[1]

    def body(a_ref, b_ref, o_ref):
        o_ref[...] = _dot_tn(a_ref[...].astype(BF16), b_ref[...].astype(BF16)).astype(BF16)

    def shard(j, core_ref):
        row = j + first
        return 0, jnp.where(row < 4, 2 * row + 1 - core_ref[0], 2 * (row - 4) + core_ref[0])

    return _call(body, name, (count,),
                 [pl.BlockSpec((SEQ, k), lambda j, core_ref: (0, 0), pipeline_mode=pl.Buffered(1)),
                  pl.BlockSpec((SEQ, shard_cols), shard)],
                 [pl.BlockSpec((None, k, shard_cols), lambda j, core_ref: (j + first, 0, 0))],
                 [_sds((N_DEV, k, shard_cols), BF16)], [], [a, b], ride, aliases={2: 0} if in_place else None, prefetch=core)


MESH = pl.DeviceIdType.MESH
ANY = pl.BlockSpec(memory_space=pl.ANY)


def _place():
    return lax.axis_index("x"), lax.axis_index("y"), lax.axis_index("c")


def _other_chips(x, y):
    return [(1 - x, y), (x, 1 - y), (1 - x, 1 - y)]


class _Ride:
    def __init__(self, operands, results, aliases, sems, start, wait):
        self.operands, self.results, self.aliases, self.sems = list(operands), list(results), dict(aliases), list(sems)
        self.start, self.wait = start, wait

    def __add__(self, other):
        n_in, n_out, n_sem = len(self.operands), len(self.results), len(self.sems)

        def both(which):
            def run(ins, outs, sems):
                getattr(self, which)(ins[:n_in], outs[:n_out], sems[:n_sem])
                getattr(other, which)(ins[n_in:], outs[n_out:], sems[n_sem:])
            return run

        aliases = {**self.aliases, **{n_in + i: n_out + j for i, j in other.aliases.items()}}
        return _Ride(self.operands + other.operands, self.results + other.results, aliases, self.sems + other.sems,
                     both("start"), both("wait"))


def _call(body, name, grid, in_specs, out_specs, out_shape, scratch_shapes, operands, ride=None, aliases=None, prefetch=None):
    in_specs, out_specs, out_shape = list(in_specs), list(out_specs), list(out_shape)
    scratch_shapes, operands, aliases = list(scratch_shapes), list(operands), dict(aliases or {})
    kernel_body = body
    if ride is not None:
        n_in, n_out, n_scr, r_in, r_out = len(in_specs), len(out_specs), len(scratch_shapes), len(ride.operands), len(ride.results)

        def kernel_body(*refs):
            out0, scr0 = n_in + r_in, n_in + r_in + n_out + r_out
            ride_refs = (refs[n_in:out0], refs[out0 + n_out:scr0], refs[scr0 + n_scr:])
            ids = [pl.program_id(i) for i in range(len(grid))]
            first = functools.reduce(jnp.logical_and, [i == 0 for i in ids])
            last = functools.reduce(jnp.logical_and, [i == g - 1 for i, g in zip(ids, grid)])

            @pl.when(first)
            def _():
                ride.start(*ride_refs)

            body(*refs[:n_in], *refs[out0:out0 + n_out], *refs[scr0:scr0 + n_scr])

            @pl.when(last)
            def _():
                ride.wait(*ride_refs)

        aliases.update({n_in + i: n_out + j for i, j in ride.aliases.items()})
        in_specs += [ANY] * r_in
        out_specs += [ANY] * r_out
        out_shape += ride.results
        scratch_shapes += ride.sems
        operands += ride.operands
    params = _cparams(dimension_semantics=("arbitrary",) * len(grid))
    if prefetch is None:
        return _pallas_call(
            kernel_body, name=name, grid=grid, in_specs=in_specs, out_specs=out_specs, out_shape=out_shape,
            scratch_shapes=scratch_shapes, input_output_aliases=aliases, compiler_params=params,
        )(*operands)

    def with_prefetch(prefetch_ref, *refs):
        kernel_body(*refs)

    return _pallas_call(
        with_prefetch, name=name,
        grid_spec=pltpu.PrefetchScalarGridSpec(num_scalar_prefetch=1, grid=grid, in_specs=in_specs, out_specs=out_specs,
                                               scratch_shapes=scratch_shapes),
        out_shape=out_shape, input_output_aliases={i + 1: j for i, j in aliases.items()}, compiler_params=params,
    )(prefetch, *operands)


def _after(*arrays):
    return _Ride(arrays, [], {}, [], lambda *refs: None, lambda *refs: None)


def _gather_first_level(shards):
    n = len(shards)

    def copies(ins, outs, sems, landed):
        send_sems, recv_sems, local_sems = sems
        x, y, c = _place()
        peers = [(x, y, 1 - c)] + [(px, py, c) for px, py in _other_chips(x, y)]

        def row(peer):
            return 4 * x + 2 * y + c if not landed else 4 * peer[0] + 2 * peer[1] + peer[2]

        local = [pltpu.make_async_copy(ins[a], outs[a].at[4 * x + 2 * y + c], local_sems.at[a]) for a in range(n)]
        remote = [pltpu.make_async_remote_copy(
            src_ref=ins[a], dst_ref=outs[a].at[row(peer)], send_sem=send_sems.at[a, k], recv_sem=recv_sems.at[a, k],
            device_id=peer, device_id_type=MESH) for a in range(n) for k, peer in enumerate(peers)]
        return local, remote

    def start(ins, outs, sems):
        local, remote = copies(ins, outs, sems, False)
        for cp in local + remote:
            cp.start()

    def wait(ins, outs, sems):
        local, sent = copies(ins, outs, sems, False)
        for cp in copies(ins, outs, sems, True)[1]:
            cp.wait_recv()
        for cp in sent:
            cp.wait_send()
        for cp in local:
            cp.wait()

    return _Ride(shards, [_sds((N_DEV,) + s.shape, s.dtype) for s in shards], {},
                 [pltpu.SemaphoreType.DMA((n, 4)), pltpu.SemaphoreType.DMA((n, 4)), pltpu.SemaphoreType.DMA((n,))], start, wait)


def _gather_second_level(buffers):
    n = len(buffers)

    def copies(outs, sems, core):
        send_sems, recv_sems = sems
        x, y, c = _place()
        return [pltpu.make_async_remote_copy(
            src_ref=outs[a].at[4 * px + 2 * py + core], dst_ref=outs[a].at[4 * px + 2 * py + core], send_sem=send_sems.at[a, j],
            recv_sem=recv_sems.at[a, j], device_id=(x, y, 1 - c), device_id_type=MESH)
            for a in range(n) for j, (px, py) in enumerate(_other_chips(x, y))]

    def start(ins, outs, sems):
        for cp in copies(outs, sems, lax.axis_index("c")):
            cp.start()

    def wait(ins, outs, sems):
        for cp in copies(outs, sems, 1 - lax.axis_index("c")):
            cp.wait_recv()
        for cp in copies(outs, sems, lax.axis_index("c")):
            cp.wait_send()

    return _Ride(buffers, [_sds(b.shape, b.dtype) for b in buffers], {i: i for i in range(n)},
                 [pltpu.SemaphoreType.DMA((n, 3)), pltpu.SemaphoreType.DMA((n, 3))], start, wait)


def _relayed_gather(shards):
    n = len(shards)
    buffers = [_sds((N_DEV,) + s.shape, s.dtype) for s in shards]
    dma = pltpu.SemaphoreType.DMA

    def remote(src, dst, send_sem, recv_sem, to):
        return pltpu.make_async_remote_copy(src_ref=src, dst_ref=dst, send_sem=send_sem, recv_sem=recv_sem,
                                            device_id=to, device_id_type=MESH)

    def row(px, py, pc):
        return 4 * px + 2 * py + pc

    def ride(operands, aliases, sems, copies):
        def start(ins, outs, sem_refs):
            local, sent = copies(ins, outs, sem_refs, False)
            for cp in local + sent:
                cp.start()

        def wait(ins, outs, sem_refs):
            local, sent = copies(ins, outs, sem_refs, False)
            for cp in copies(ins, outs, sem_refs, True)[1]:
                cp.wait_recv()
            for cp in sent:
                cp.wait_send()
            for cp in local:
                cp.wait()

        return _Ride(operands, buffers, aliases, sems, start, wait)

    def first(ins, outs, sems, landed):
        x, y, c = _place()
        peers = [(x, y, 1 - c), (1 - x, y, c), (x, 1 - y, c)]
        local = [pltpu.make_async_copy(ins[a], outs[a].at[row(x, y, c)], sems[2].at[a]) for a in range(n)]
        return local, [remote(ins[a], outs[a].at[row(*peer) if landed else row(x, y, c)], sems[0].at[a, k], sems[1].at[a, k], peer)
                       for a in range(n) for k, peer in enumerate(peers)]

    def second(ins, outs, sems, landed):
        x, y, c = _place()
        mine = 1 - c if landed else c
        copies = []
        for a in range(n):
            half = shards[a].shape[0] // 2
            over_x, over_y, diagonal = outs[a].at[row(1 - x, y, mine)], outs[a].at[row(x, 1 - y, mine)], outs[a].at[row(1 - x, 1 - y, c)]
            lower, upper = pl.ds(0, half), pl.ds(half, half)
            copies += [remote(over_x, over_x, sems[0].at[a, 0], sems[1].at[a, 0], (x, y, 1 - c)),
                       remote(over_y, over_y, sems[0].at[a, 1], sems[1].at[a, 1], (x, y, 1 - c))]
            if landed:
                copies += [remote(diagonal.at[lower], diagonal.at[lower], sems[0].at[a, 2], sems[1].at[a, 2], (1 - x, y, c)),
                           remote(diagonal.at[upper], diagonal.at[upper], sems[0].at[a, 3], sems[1].at[a, 3], (x, 1 - y, c))]
            else:
                copies += [remote(over_y.at[lower], over_y.at[lower], sems[0].at[a, 2], sems[1].at[a, 2], (1 - x, y, c)),
                           remote(over_x.at[upper], over_x.at[upper], sems[0].at[a, 3], sems[1].at[a, 3], (x, 1 - y, c))]
        return [], copies

    def third(ins, outs, sems, landed):
        x, y, c = _place()
        return [], [remote(outs[a].at[row(1 - x, 1 - y, 1 - c if landed else c)], outs[a].at[row(1 - x, 1 - y, 1 - c if landed else c)],
                           sems[0].at[a], sems[1].at[a], (x, y, 1 - c)) for a in range(n)]

    def later(copies, n_sems):
        return lambda partly: ride(partly, {i: i for i in range(n)}, [dma((n,) + n_sems), dma((n,) + n_sems)], copies)

    return ride(shards, {}, [dma((n, 3)), dma((n, 3)), dma((n,))], first), later(second, (4,)), later(third, ())


def _sibling_swap_ride(grads, halves=True):
    n = len(grads)

    def copies(ins, outs, sems):
        x, y, c = _place()
        return [pltpu.make_async_remote_copy(
            src_ref=ins[a].at[1 - c] if halves else ins[a].at[pl.ds(0, 4)], dst_ref=outs[a], send_sem=sems[0].at[a],
            recv_sem=sems[1].at[a], device_id=(x, y, 1 - c), device_id_type=MESH) for a in range(n)]

    def start(ins, outs, sems):
        for cp in copies(ins, outs, sems):
            cp.start()

    def wait(ins, outs, sems):
        for cp in copies(ins, outs, sems):
            cp.wait()

    return _Ride(grads, [_sds((4,) + g.shape[-2:], g.dtype) for g in grads], {},
                 [pltpu.SemaphoreType.DMA((n,)), pltpu.SemaphoreType.DMA((n,))], start, wait)


def _chip_swap_ride(sums):
    n = len(sums)

    def copies(ins, outs, sems, landed):
        send_sems, recv_sems, local_sems = sems
        x, y, c = _place()
        mine = 2 * x + y
        local = [pltpu.make_async_copy(ins[a].at[mine], outs[a].at[mine], local_sems.at[a]) for a in range(n)]
        remote = [pltpu.make_async_remote_copy(
            src_ref=ins[a].at[2 * px + py], dst_ref=outs[a].at[2 * px + py if landed else mine], send_sem=send_sems.at[a, j],
            recv_sem=recv_sems.at[a, j], device_id=(px, py, c), device_id_type=MESH)
            for a in range(n) for j, (px, py) in enumerate(_other_chips(x, y))]
        return local, remote

    def start(ins, outs, sems):
        local, remote = copies(ins, outs, sems, False)
        for cp in local + remote:
            cp.start()

    def wait(ins, outs, sems):
        local, sent = copies(ins, outs, sems, False)
        for cp in copies(ins, outs, sems, True)[1]:
            cp.wait_recv()
        for cp in sent:
            cp.wait_send()
        for cp in local:
            cp.wait()

    return _Ride(sums, [_sds(s.shape, s.dtype) for s in sums], {},
                 [pltpu.SemaphoreType.DMA((n, 3)), pltpu.SemaphoreType.DMA((n, 3)), pltpu.SemaphoreType.DMA((n,))], start, wait)


def _send_buffers(shards, name):
    n = len(shards)

    def body(*refs):
        for (w, transposed, rows, cols), w_ref, o_ref in zip(shards, refs[:n], refs[n:]):
            if transposed:
                c, r = w.shape
                padded = jnp.concatenate([w_ref[...], jnp.zeros((cols - c, r), F32)], axis=0) if cols > c else w_ref[...]
                o_ref[...] = padded.T.astype(BF16)
            else:
                r, c = w.shape
                if (r, c) != (rows, cols):
                    o_ref[...] = jnp.zeros((rows, cols), BF16)
                o_ref[:r, :c] = w_ref[...].astype(BF16)

    return _pallas_call(body, name=name, out_shape=[_sds((rows, cols), BF16) for _, _, rows, cols in shards])(
        *[w for w, _, _, _ in shards])


def _all_gather(shards, name):
    n = len(shards)
    first, second, third = _relayed_gather(shards)
    levels = [first, second(shards), third(shards)]
    counts = [len(level.sems) for level in levels]

    def body(*refs):
        ins, outs, sems = refs[:n], refs[n:2 * n], refs[2 * n:]
        for i, level in enumerate(levels):
            mine = sems[sum(counts[:i]):sum(counts[:i + 1])]
            level.start(ins, outs, mine)
            level.wait(ins, outs, mine)

    return _pallas_call(
        body, name=name, in_specs=[ANY] * n, out_specs=[ANY] * n, out_shape=first.results,
        scratch_shapes=[s for level in levels for s in level.sems],
    )(*shards)


HBM = pl.BlockSpec(memory_space=pltpu.HBM)
SEMAPHORES = pl.BlockSpec(memory_space=pltpu.SEMAPHORE)
IN_FLIGHT = pltpu.CompilerParams(has_side_effects=pltpu.SideEffectType.DATAFLOW_SIDE_EFFECTING)


def _chip_swap_copies(src_refs, land_refs, send_sems, recv_sems, landed):
    x, y, c = _place()
    return [pltpu.make_async_remote_copy(
        src_ref=src.at[2 * px + py], dst_ref=land.at[2 * px + py if landed else 2 * x + y], send_sem=send_sems.at[3 * a + j],
        recv_sem=recv_sems.at[3 * a + j], device_id=(px, py, c), device_id_type=MESH)
        for a, (src, land) in enumerate(zip(src_refs, land_refs)) for j, (px, py) in enumerate(_other_chips(x, y))]


def _chip_swap_start(sums, name):
    n = len(sums)

    def body(*refs):
        src_refs, land_refs, (send_sems, recv_sems), token = refs[:n], refs[n:2 * n], refs[2 * n:2 * n + 2], refs[-1]
        for cp in _chip_swap_copies(src_refs, land_refs, send_sems, recv_sems, False):
            cp.start()
        token[...] = jnp.zeros_like(token)

    kept = [pltpu.HBM(s.shape, s.dtype) for s in sums]
    out = _pallas_call(
        body, name=name,
        out_shape=[pltpu.SemaphoreType.DMA((3 * n,)), pltpu.SemaphoreType.DMA((3 * n,))] + kept + kept + [_sds((8, 128), F32)],
        in_specs=[HBM] * (2 * n), out_specs=[SEMAPHORES, SEMAPHORES] + [HBM] * (2 * n) + [pl.BlockSpec(memory_space=pltpu.VMEM)],
        input_output_aliases={i: 2 + i for i in range(2 * n)}, compiler_params=IN_FLIGHT,
    )(*[pltpu.with_memory_space_constraint(s, pltpu.HBM) for s in sums],
      *[pltpu.with_memory_space_constraint(lax.empty(s.shape, s.dtype), pltpu.HBM) for s in sums])
    return out[0], out[1], out[2:2 + n], out[2 + n:2 + 2 * n], out[-1]


def _chip_swap_wait(send_sems, recv_sems, sums, landings, after, name):
    n = len(sums)

    def body(*refs):
        src_refs, land_refs, (send_sems, recv_sems) = refs[:n], refs[n:2 * n], refs[2 * n:2 * n + 2]
        for cp in _chip_swap_copies(src_refs, land_refs, send_sems, recv_sems, False):
            cp.wait_send()
        for cp in _chip_swap_copies(src_refs, land_refs, send_sems, recv_sems, True):
            cp.wait_recv()

    out = _pallas_call(
        body, name=name, out_shape=[pltpu.HBM(s.shape, s.dtype) for s in list(sums) + list(landings)],
        in_specs=[HBM] * (2 * n) + [SEMAPHORES, SEMAPHORES] + [ANY] * len(after), out_specs=[HBM] * (2 * n),
        input_output_aliases={i: i for i in range(2 * n)}, compiler_params=IN_FLIGHT,
    )(*sums, *landings, send_sems, recv_sems, *after)
    return out[:n], out[n:]


def _pair_sums(gs, rs, core, name):
    n_arrays = len(gs)

    def body(core_ref, *refs):
        for g_ref, r_ref, o_ref in zip(refs[:n_arrays], refs[n_arrays:2 * n_arrays], refs[2 * n_arrays:]):
            o_ref[...] = (g_ref[...].astype(F32) + r_ref[...].astype(F32)).astype(o_ref.dtype)

    def chip(g):
        return pl.BlockSpec((None,) + g.shape[-2:], lambda p, core_ref: (p, 0, 0))

    def own(g):
        if g.ndim == 3:
            return pl.BlockSpec((None,) + g.shape[-2:], lambda p, core_ref: (p + 4, 0, 0))
        return pl.BlockSpec((None, None) + g.shape[2:], lambda p, core_ref: (core_ref[0], p, 0, 0))

    return _pallas_call(
        body, name=name,
        grid_spec=pltpu.PrefetchScalarGridSpec(
            num_scalar_prefetch=1, grid=(4,), in_specs=[own(g) for g in gs] + [chip(g) for g in gs],
            out_specs=[chip(g) for g in gs]),
        out_shape=[_sds((4,) + g.shape[-2:], g.dtype) for g in gs], compiler_params=_cparams(dimension_semantics=("arbitrary",)),
    )(core, *gs, *rs)


def _adamw_math(w, g, m, v):
    m = ADAM_B1 * m + (1.0 - ADAM_B1) * g
    v = ADAM_B2 * v + (1.0 - ADAM_B2) * (g * g)
    m_hat = m / (1.0 - ADAM_B1 ** ADAM_STEP)
    v_hat = v / (1.0 - ADAM_B2 ** ADAM_STEP)
    return -ADAM_LR * (m_hat / (jnp.sqrt(v_hat) + ADAM_EPS) + ADAM_WD * w), m, v


def _adamw_many(weights, name, ride=None):
    steps = 4
    in_specs, out_specs, out_shape, operands, tiles = [], [], [], [], []
    for w, m, v, parts, own, transposed in weights:
        _, pr, pc = parts.shape
        if transposed:
            c, r = w.shape
            tile = pl.BlockSpec((c, r // steps), lambda i: (0, i))
            part_tile = pl.BlockSpec((4, r // steps, pc), lambda i: (0, i, 0))
            tiles.append((c, r // steps))
        elif w.shape[0] % (8 * steps) == 0:
            r, c = w.shape
            tile = pl.BlockSpec((r // steps, c), lambda i: (i, 0))
            part_tile = pl.BlockSpec((4, r // steps, pc), lambda i: (0, i, 0))
            tiles.append((r // steps, c))
        else:
            tile = pl.BlockSpec(w.shape, lambda i: (0, 0))
            part_tile = pl.BlockSpec(parts.shape, lambda i: (0, 0, 0))
            tiles.append(w.shape)
        in_specs += [tile, tile, tile] + [part_tile] * (1 if own is None else 2)
        out_specs += [tile] * 4
        out_shape += [_sds(w.shape, F32)] * 4
        operands += [w, m, v, parts] + ([] if own is None else [own])
    n_in = len(operands)

    def body(*refs):
        ins, outs = list(refs[:n_in]), refs[n_in:]
        this_chip = 2 * lax.axis_index("x") + lax.axis_index("y")
        for k, (_, _, _, _, own, transposed) in enumerate(weights):
            w_ref, m_ref, v_ref, p_ref = ins[:4]
            own_ref = None if own is None else ins[4]
            del ins[:4 if own is None else 5]
            rows, cols = tiles[k]
            g = None
            for q in range(4):
                index = (q,) if transposed else (q, slice(0, rows), slice(0, cols))
                part = p_ref[index] if own is None else jnp.where(this_chip == q, own_ref[index], p_ref[index])
                g = part.astype(F32) if g is None else g + part.astype(F32)
            if transposed:
                g = g.T[:rows]
            g_out, d_out, m_out, v_out = outs[4 * k:4 * k + 4]
            g_out[...] = g
            d_out[...], m_out[...], v_out[...] = _adamw_math(w_ref[...], g, m_ref[...], v_ref[...])

    return _call(body, name, (steps,), in_specs, out_specs, out_shape, [], operands, ride)


SMALL = ("ssm_a_re", "ssm_a_im", "ssm_log_dt", "ssm_b_re", "ssm_b_im", "ssm_c_re", "ssm_c_im", "ssm_d",
         "ln1_g", "ln1_b", "ln2_g", "ln2_b")


def _pack_rows(arrays):
    rows = []
    for a in arrays:
        flat = a.reshape(-1)
        rows.append(jnp.pad(flat, (0, -flat.shape[0] % 128)).reshape(-1, 128))
    packed = jnp.concatenate(rows, axis=0)
    return jnp.pad(packed, ((0, -packed.shape[0] % 8), (0, 0)))


def _unpack_rows(packed, shapes):
    out, row = [], 0
    for shape in shapes:
        size = math.prod(shape)
        n_rows = -(-size // 128)
        out.append(packed[row:row + n_rows].reshape(-1)[:size].reshape(shape))
        row += n_rows
    return out


def _sum_devices(parts):
    def body(p_ref, o_ref):
        total = p_ref[0]
        for dev in range(1, N_DEV):
            total = total + p_ref[dev]
        o_ref[...] = total

    return _pallas_call(body, name="sum_devices", out_shape=_sds(parts.shape[1:], F32))(parts)


def _adamw_replicated(ws, ms, vs, gs):
    n = len(ws)

    def body(*refs):
        w_refs, m_refs, v_refs, g_refs, d_out, m_out, v_out = (refs[i * n:(i + 1) * n] for i in range(7))
        for i in range(n):
            d_out[i][...], m_out[i][...], v_out[i][...] = _adamw_math(w_refs[i][...], g_refs[i][...], m_refs[i][...], v_refs[i][...])

    out = _pallas_call(body, name="adamw_replicated", out_shape=[_sds(w.shape, F32) for w in ws] * 3,
                       compiler_params=_cparams())(*ws, *ms, *vs, *gs)
    return out[:n], out[n:2 * n], out[2 * n:]


def kernel(x, w_in, b_gate, w_attn_br, w_ssm_br, w_out, ssm_a_re, ssm_a_im, ssm_log_dt, ssm_b_re, ssm_b_im, ssm_c_re, ssm_c_im, ssm_d, w_glu, ln1_g, ln1_b, w_ff_gate, w_ff_up, w_ff_down, ln2_g, ln2_b, loss_target, m_w_in, m_b_gate, m_w_attn_br, m_w_ssm_br, m_w_out, m_ssm_a_re, m_ssm_a_im, m_ssm_log_dt, m_ssm_b_re, m_ssm_b_im, m_ssm_c_re, m_ssm_c_im, m_ssm_d, m_w_glu, m_ln1_g, m_ln1_b, m_w_ff_gate, m_w_ff_up, m_w_ff_down, m_ln2_g, m_ln2_b, v_w_in, v_b_gate, v_w_attn_br, v_w_ssm_br, v_w_out, v_ssm_a_re, v_ssm_a_im, v_ssm_log_dt, v_ssm_b_re, v_ssm_b_im, v_ssm_c_re, v_ssm_c_im, v_ssm_d, v_w_glu, v_ln1_g, v_ln1_b, v_w_ff_gate, v_w_ff_up, v_w_ff_down, v_ln2_g, v_ln2_b):
    given = dict(locals())
    x2, target = x[0], loss_target[0]
    core = lax.axis_index("c").astype(jnp.int32).reshape(1)

    sharded = ("w_in", "w_attn_br", "w_ssm_br", "w_glu", "w_ff_gate", "w_ff_up", "b_gate", "w_out", "w_ff_down")
    send_shape = dict(w_in=(D_MODEL, 896), w_attn_br=(ATTN_WIDTH, 128), w_ssm_br=(SSM_WIDTH, 128), w_glu=(SSM_WIDTH, 128),
                      w_out=(128, D_MODEL), w_ff_gate=(D_MODEL, FF_PAD), w_ff_up=(D_MODEL, FF_PAD), w_ff_down=(FF_PAD, D_MODEL))
    local = {k: given[k][0] for k in sharded}
    narrow = ("w_ff_gate", "w_ff_up")
    def to_send(k):
        return (local[k].T, True, *send_shape[k]) if k in narrow else (local[k], False, *send_shape[k])

    later = [k for k in sharded if k not in ("w_in", "b_gate")]
    sends = dict(zip(["w_in"] + later, _send_buffers([to_send("w_in")], "send_w_in")
                     + _send_buffers([to_send(k) for k in later], "send_weights")))
    sends["b_gate"] = local["b_gate"]
    mixer_weights = ("w_attn_br", "w_ssm_br", "w_glu", "b_gate", "w_out")
    ff_weights = ("w_ff_gate", "w_ff_up", "w_ff_down")
    wt = {}
    wt["w_in"], = _all_gather([sends["w_in"]], "gather_w_in")

    a_re, a_im, log_dt = ssm_a_re[0], ssm_a_im[0], ssm_log_dt[0].reshape(SSM_GROUPS, 1)
    b_re_t, b_im_t = ssm_b_re[0].transpose(0, 2, 1), ssm_b_im[0].transpose(0, 2, 1)
    abar_re, abar_im, e_re, e_im, bbar_re_t, bbar_im_t = _ssm_prep(a_re, a_im, log_dt, b_re_t, b_im_t)
    bmat, cmat, a_chunks = _ssm_tables(abar_re, abar_im, bbar_re_t, bbar_im_t, ssm_c_re[0], ssm_c_im[0])
    cos_t, sin_t = _rope_tables()

    big_mixer, ff_in = [k for k in mixer_weights if k != "b_gate"], ("w_ff_gate", "w_ff_up")
    n_mixer = len(big_mixer)
    mixer_1, mixer_2, mixer_3 = _relayed_gather([sends[k] for k in big_mixer])
    ff_in_1, ff_in_2, ff_in_3 = _relayed_gather([sends[k] for k in ff_in])
    ff_down_1, ff_down_2, ff_down_3 = _relayed_gather([sends["w_ff_down"]])
    proj, *landed = _proj(x2, wt["w_in"], mixer_1 + _gather_first_level([sends["b_gate"]]))
    mixer, bias = landed[:n_mixer], landed[n_mixer:]
    attn, lse, q_pm, k_pm, v_pm, *landed = _attn_fwd(proj, cos_t, sin_t,
                                                     mixer_2(mixer) + _gather_second_level(bias) + ff_in_1)
    mixer, b_gate_full, ff = landed[:n_mixer], landed[n_mixer], landed[n_mixer + 1:]
    ys, states, *landed = _ssm_fwd(proj, bmat, cmat, a_chunks, ssm_d, mixer_3(mixer) + ff_in_2(ff) + ff_down_1)
    wt.update(zip(big_mixer, landed[:n_mixer]))
    ff, ff_down = landed[n_mixer:n_mixer + 2], landed[n_mixer + 2:]
    wt["w_out"] = wt["w_out"].reshape(D_MODEL, D_MODEL)
    h, xhat1, rstd1, glu, y_attn, y_ssm, *landed = _mixer_out(
        attn, ys, proj, x2, wt["w_attn_br"], wt["w_ssm_br"], wt["w_glu"], wt["w_out"], b_gate_full, ln1_g, ln1_b,
        ff_in_3(ff) + ff_down_2(ff_down))
    wt.update(zip(ff_in, landed[:2]))
    ff_a, ff_b, ff_f, w_ff_down = _ff_up(h, wt["w_ff_gate"], wt["w_ff_up"], ff_down_3(landed[2:]))
    wt["w_ff_down"] = w_ff_down.reshape(D_FF_PAD, D_MODEL)
    dr2, d_ln2_g, d_ln2_b, loss_lanes = _ff_down_loss(ff_f, wt["w_ff_down"], h, target, ln2_g, ln2_b)

    def pair_sums(names, contrib, from_sibling):
        return _pair_sums([contrib[k] for k in names], from_sibling, core, "pair_sums_" + names[0])

    d_a, d_b = _ff_down_bwd(dr2, wt["w_ff_down"], ff_a, ff_b)
    contrib = dict(w_ff_gate=_weight_grad(h, d_a, "wgrad_w_ff_gate", FF_PAD),
                   w_ff_up=_weight_grad(h, d_b, "wgrad_w_ff_up", FF_PAD),
                   w_ff_down=_weight_grad(ff_f, dr2, "wgrad_w_ff_down"))
    dr1, d_ln1_g, d_ln1_b, *from_sibling = _ff_up_bwd(
        d_a, d_b, wt["w_ff_gate"], wt["w_ff_up"], dr2, xhat1, rstd1, ln1_g, _sibling_swap_ride([contrib[k] for k in ff_weights]))
    ff_sums = pair_sums(ff_weights, contrib, from_sibling)

    d_ya, d_yssm, d_proj, d_attn, d_glu, d_ys, mixed, y_s, gy, d_bg = _mixer_bwd(
        dr1, proj, y_attn, y_ssm, glu, ys, wt["w_attn_br"], wt["w_ssm_br"], wt["w_glu"], wt["w_out"], b_gate_full)
    contrib.update(w_attn_br=_weight_grad(attn, d_ya, "wgrad_w_attn_br", 128),
                   w_ssm_br=_weight_grad(y_s, d_yssm, "wgrad_w_ssm_br", 128),
                   w_glu=_weight_grad(gy, d_glu, "wgrad_w_glu", 128),
                   w_out=_weight_grad(mixed, dr1, "wgrad_w_out"),
                   b_gate=d_bg.reshape(2, 4, 2, 128).transpose(2, 1, 0, 3))
    d_proj, *landed = _attn_bwd(q_pm, k_pm, v_pm, cos_t, sin_t, attn, lse, d_attn, d_proj,
                                _chip_swap_ride(ff_sums) + _sibling_swap_ride([contrib[k] for k in mixer_weights]))
    parts, own_sums = dict(zip(ff_weights, landed[:len(ff_weights)])), {}
    mixer_sums = pair_sums(mixer_weights, contrib, landed[len(ff_weights):])
    d_proj, d_bmat, d_cmat, d_abar, d_skip, *landed = _ssm_bwd(d_ys, proj, states, bmat, cmat, a_chunks, ssm_d, d_proj,
                                                               _chip_swap_ride(mixer_sums))
    parts.update(zip(mixer_weights, landed))

    gbb_re_t, gbb_im_t = _block_diag_parts(d_bmat, True)
    gc_re, gc_im = _block_diag_parts(d_cmat, False)
    ga_re = d_abar[:, 0, :CHUNK_STATES].reshape(SSM_GROUPS, SSM_STATE)
    ga_im = d_abar[:, 0, CHUNK_STATES:].reshape(SSM_GROUPS, SSM_STATE)
    g_a_re, g_a_im, g_log_dt, g_b_re_t, g_b_im_t = _ssm_param_bwd(
        a_re, a_im, log_dt, b_re_t, b_im_t, abar_re, abar_im, e_re, e_im, ga_re, ga_im, gbb_re_t, gbb_im_t)
    mine = [g_a_re, g_a_im, g_log_dt, g_b_re_t, g_b_im_t, gc_re, -gc_im,
            d_skip, d_ln1_g, d_ln1_b, d_ln2_g, d_ln2_b]
    small_packed = _pack_rows(mine + [loss_lanes])

    w_in_contrib, small_partly = _weight_grad_rows(x2, d_proj, core, "wgrad_w_in_first", 896, 0, 6,
                                                   _gather_first_level([small_packed]))
    w_in_contrib, from_sibling, every = _weight_grad_rows(
        x2, d_proj, core, "wgrad_w_in_rest", 896, 6, 2,
        _sibling_swap_ride([w_in_contrib], halves=False) + _gather_second_level([small_partly]), in_place=True)
    w_in_sum, = _pair_sums([w_in_contrib], [from_sibling], core, "pair_sums_w_in")
    send_sems, recv_sems, w_in_sum, landing, token = _chip_swap_start([w_in_sum], "w_in_chip_swap_start")

    def adamw_of(k):
        taken = (lambda a: a.T) if k in narrow else (lambda a: a)
        return taken(local[k]), taken(given["m_" + k][0]), taken(given["v_" + k][0]), parts[k], own_sums.get(k), k in narrow

    others = [k for k in sharded if k != "w_in"]
    updated = _adamw_many([adamw_of(k) for k in others], "adamw_others", _after(token))
    grad_x, = _grad_x(d_proj, wt["w_in"], dr1, _after(token))

    def held(k, a):
        return a.transpose(0, 1, 3, 2) if k in ("ssm_b_re", "ssm_b_im") else a

    *small_grads, loss_sum = _unpack_rows(_sum_devices(every), [held(k, given[k]).shape for k in SMALL] + [(1, 128)])
    small = _adamw_replicated([held(k, given[k]) for k in SMALL], [held(k, given["m_" + k]) for k in SMALL],
                              [held(k, given["v_" + k]) for k in SMALL], small_grads)
    loss = loss_sum[0, 0]

    (own_sums["w_in"],), (parts["w_in"],) = _chip_swap_wait(
        send_sems, recv_sems, w_in_sum, landing, [grad_x, updated[0], small[0][0]], "w_in_chip_swap_wait")
    updated += _adamw_many([adamw_of("w_in")], "adamw_w_in")

    grads, deltas, new_m, new_v = {}, {}, {}, {}
    for i, k in enumerate(others + ["w_in"]):
        out = [o.T if k in narrow else o for o in updated[4 * i:4 * i + 4]]
        grads[k], deltas[k], new_m[k], new_v[k] = (o.reshape((1,) + local[k].shape) for o in out)
    for res, values in zip((grads, deltas, new_m, new_v), (small_grads,) + small):
        res.update((k, held(k, a)) for k, a in zip(SMALL, values))

    order = ("w_in", "b_gate", "w_attn_br", "w_ssm_br", "w_out", "ssm_a_re", "ssm_a_im", "ssm_log_dt", "ssm_b_re", "ssm_b_im",
             "ssm_c_re", "ssm_c_im", "ssm_d", "w_glu", "ln1_g", "ln1_b", "w_ff_gate", "w_ff_up", "w_ff_down", "ln2_g", "ln2_b")
    return (loss, grad_x[None], *[grads[k] for k in order], *[deltas[k] for k in order], *[new_m[k] for k in order],
            *[new_v[k] for k in order])
```

```python
import functools
import math

import jax
import jax.numpy as jnp
import numpy as np
from jax import lax
from jax.experimental import pallas as pl
from jax.experimental.pallas import tpu as pltpu

F32 = jnp.float32
BF16 = jnp.bfloat16

N_DEV = 8
SEQ = 2048
D_MODEL = 1024
HEAD_DIM = 64
ATTN_WIDTH = 512
QKV_WIDTH = 1536
SSM_WIDTH = 512
SSM_GROUPS = 32
SSM_GROUP = 16
SSM_STATE = 64
IN_WIDTH = 7168
D_FF = 2816
FF_SHARD = D_FF // N_DEV
FF_PAD = 384
D_FF_PAD = FF_PAD * N_DEV
DN_ALPHA = 2.0 ** 0.25
LN_EPS = 1e-5
NEG_INF = -1e30
ROPE_THETA = 10000.0
BLOCK = 128
GROUPS = ((1, 16), (4, 4), (16, 1))

ADAM_LR = 0.001
ADAM_B1 = 0.9
ADAM_B2 = 0.999
ADAM_EPS = 1e-08
ADAM_WD = 0.01
ADAM_STEP = 10

VMEM_LIMIT = 56 * 1024 * 1024


_pallas_call = pl.pallas_call


def _cparams(**kw):
    return pltpu.CompilerParams(vmem_limit_bytes=VMEM_LIMIT, **kw)


def _dot(a, b):
    return jnp.dot(a, b, preferred_element_type=F32)


def _dot_nt(a, b):
    return lax.dot_general(a, b, (((1,), (1,)), ((), ())), preferred_element_type=F32)


def _side_by_side(w_ref, row=None):
    rows = slice(None) if row is None else pl.ds(row, 1)
    return jnp.concatenate([w_ref[i, rows, :] for i in range(w_ref.shape[0])], axis=1)


def _dot_tn(a, b):
    return lax.dot_general(a, b, (((0,), (0,)), ((), ())), preferred_element_type=F32)


def _rope_tables():
    half = HEAD_DIM // 2
    inv_freq = np.float32(ROPE_THETA) ** (-np.arange(half, dtype=np.float32) / np.float32(half))
    ang = np.arange(SEQ, dtype=np.float32)[:, None] * inv_freq[None, :]
    cos, sin = np.cos(ang).astype(np.float32), np.sin(ang).astype(np.float32)
    tables = np.tile(cos, (1, 4)), np.tile(np.concatenate([-sin, sin], axis=1), (1, 2))

    def by_phase(t):
        return np.stack([t.reshape(SEQ // d, d, 128).transpose(1, 0, 2).reshape(SEQ, 128) for d, _ in GROUPS])

    return jnp.asarray(by_phase(tables[0])), jnp.asarray(by_phase(tables[1]))


def _swap_halves(x):
    lane = lax.broadcasted_iota(jnp.int32, x.shape, 1)
    return jnp.where((lane & 63) < 32, pltpu.roll(x, 96, axis=1), pltpu.roll(x, 32, axis=1))


def _group_rows(d, nb, r, i):
    src = pl.ds(i * BLOCK, BLOCK) if d == 1 else pl.ds(r + i * BLOCK * d, BLOCK, stride=d)
    return src, pl.ds((r * nb + i) * BLOCK, BLOCK)


def _attn_masks():
    a_idx = lax.broadcasted_iota(jnp.int32, (2 * BLOCK, 2 * BLOCK), 0) & (BLOCK - 1)
    c_idx = lax.broadcasted_iota(jnp.int32, (2 * BLOCK, 2 * BLOCK), 1)
    cur_ok = jnp.logical_and(c_idx >= BLOCK, c_idx - BLOCK <= a_idx)
    prev_ok = jnp.logical_and(c_idx < BLOCK, c_idx >= a_idx)
    lane = lax.broadcasted_iota(jnp.int32, (BLOCK, 128), 1)
    return cur_ok, prev_ok, lane < HEAD_DIM


def _stack_heads(t, head0):
    zero = jnp.zeros_like(t)
    return jnp.concatenate([jnp.where(head0, t, zero), jnp.where(head0, zero, t)], axis=0)


def _unstack_heads(t2, head0):
    return jnp.where(head0, t2[:BLOCK], t2[BLOCK:])


def _attn_fwd(proj, cos_t, sin_t, ride=None):
    def body(q0, q1, q2, k0, k1, k2, v0, v1, v2, cos_ref, sin_ref, attn_ref, lse_ref, qpm_ref, kpm_ref, vpm_ref,
             qs, ks, vs, os_, ms, ls, acc, mnat, lnat):
        cur_ok, prev_ok, head0 = _attn_masks()
        ks[:BLOCK, :] = jnp.zeros((BLOCK, 128), BF16)
        vs[:BLOCK, :] = jnp.zeros((BLOCK, 128), BF16)
        for g, (d, nb) in enumerate(GROUPS):
            q_ref, k_ref, v_ref = (q0, q1, q2)[g], (k0, k1, k2)[g], (v0, v1, v2)[g]
            for r in range(d):
                for i in range(nb):
                    src, dst = _group_rows(d, nb, r, i)
                    below = pl.ds(dst.start + BLOCK, BLOCK)
                    c, s = cos_ref[g, dst, :], sin_ref[g, dst, :]
                    q = q_ref[src, :]
                    k = k_ref[src, :]
                    qs[dst, :] = ((q * c + _swap_halves(q) * s) * 0.125).astype(BF16)
                    ks[below, :] = (k * c + _swap_halves(k) * s).astype(BF16)
                    vs[below, :] = v_ref[src, :].astype(BF16)
                    qpm_ref[g, dst, :], kpm_ref[g, dst, :], vpm_ref[g, dst, :] = qs[dst, :], ks[below, :], vs[below, :]

            def block(b, carry, nb=nb):
                has_prev = (b & (nb - 1)) > 0
                cur = pl.ds(pl.multiple_of(b * BLOCK, BLOCK), BLOCK)
                window = pl.ds(pl.multiple_of(b * BLOCK, BLOCK), 2 * BLOCK)
                valid = jnp.logical_or(cur_ok, jnp.logical_and(prev_ok, has_prev))
                s = jnp.where(valid, _dot_nt(_stack_heads(qs[cur, :], head0), ks[window, :]), NEG_INF)
                m = jnp.max(s, axis=1, keepdims=True)
                p = jnp.exp(s - m)
                os_[cur, :] = _unstack_heads(_dot(p.astype(BF16), vs[window, :]), head0)
                ms[cur, :] = _unstack_heads(m, head0)
                ls[cur, :] = _unstack_heads(jnp.sum(p, axis=1, keepdims=True), head0)
                return carry

            lax.fori_loop(0, SEQ // BLOCK, block, 0, unroll=16)

            for r in range(d):
                for i in range(nb):
                    src, dst = _group_rows(d, nb, r, i)
                    if g == 0:
                        acc[src, :], mnat[src, :], lnat[src, :] = os_[dst, :], ms[dst, :], ls[dst, :]
                    else:
                        m_old, m_g = mnat[src, :], ms[dst, :]
                        m_new = jnp.maximum(m_old, m_g)
                        a_old, a_g = jnp.exp(m_old - m_new), jnp.exp(m_g - m_new)
                        acc[src, :] = a_old * acc[src, :] + a_g * os_[dst, :]
                        lnat[src, :] = a_old * lnat[src, :] + a_g * ls[dst, :]
                        mnat[src, :] = m_new
        for i in range(SEQ // BLOCK):
            rows = pl.ds(i * BLOCK, BLOCK)
            l = lnat[rows, :]
            attn_ref[rows, :] = acc[rows, :] / l
            lse_ref[rows, :] = mnat[rows, :] + jnp.log(l)

    def col(base):
        return pl.BlockSpec((SEQ, 128), lambda hp, base=base: (0, base + hp))

    in_specs = [col(g * 4) for g in range(3)] + [col(12 + g * 4) for g in range(3)] + [col(24 + g * 4) for g in range(3)]
    table = pl.BlockSpec((3, SEQ, 128), lambda hp: (0, 0, 0), pipeline_mode=pl.Buffered(1))
    out = pl.BlockSpec((SEQ, 128), lambda hp: (0, hp))
    by_phase = pl.BlockSpec((3, SEQ, 128), lambda hp: (0, 0, hp))
    return _call(
        body, "attn_fwd", (4,), in_specs + [table, table], [out, out] + [by_phase] * 3,
        [_sds((SEQ, ATTN_WIDTH), F32), _sds((SEQ, ATTN_WIDTH), F32)] + [_sds((3, SEQ, ATTN_WIDTH), BF16)] * 3,
        [pltpu.VMEM((SEQ, 128), BF16)] + [pltpu.VMEM((SEQ + BLOCK, 128), BF16)] * 2 + [pltpu.VMEM((SEQ, 128), F32)] * 6,
        [proj] * 9 + [cos_t, sin_t], ride)


def _attn_bwd_group_body(g):
    d, nb = GROUPS[g]

    def body(qs_ref, ks_ref, vs_ref, cos_ref, sin_ref, lse_ref, dattn_ref, dsum_ref, dproj_ref,
             ks, vs, dos, lss, dss, dqs, dks, dvs, stage, outs, sems):
        cur_ok, prev_ok, head0 = _attn_masks()
        qs = qs_ref.at[g]
        ks[:BLOCK, :] = jnp.zeros((BLOCK, 128), BF16)
        vs[:BLOCK, :] = jnp.zeros((BLOCK, 128), BF16)
        dks[:BLOCK, :] = jnp.zeros((BLOCK, 128), F32)
        dvs[:BLOCK, :] = jnp.zeros((BLOCK, 128), F32)
        for r in range(d):
            for i in range(nb):
                src, dst = _group_rows(d, nb, r, i)
                below = pl.ds(dst.start + BLOCK, BLOCK)
                ks[below, :] = ks_ref[g, dst, :]
                vs[below, :] = vs_ref[g, dst, :]
                dos[dst, :] = dattn_ref[src, :].astype(BF16)
                for per_head, spread in ((dsum_ref[src, :], dss), (lse_ref[src, :], lss)):
                    other = pltpu.roll(per_head, HEAD_DIM, axis=1)
                    spread[0, dst, :] = jnp.where(head0, per_head, other)
                    spread[1, dst, :] = jnp.where(head0, other, per_head)
                dks[below, :] = jnp.zeros((BLOCK, 128), F32)
                dvs[below, :] = jnp.zeros((BLOCK, 128), F32)

        def per_stacked_row(spread, cur):
            h0, h1 = spread[0, cur, :], spread[1, cur, :]
            return jnp.concatenate([jnp.concatenate([h0, h0], axis=1), jnp.concatenate([h1, h1], axis=1)], axis=0)

        def block(b, carry):
            has_prev = (b & (nb - 1)) > 0
            cur = pl.ds(pl.multiple_of(b * BLOCK, BLOCK), BLOCK)
            window = pl.ds(pl.multiple_of(b * BLOCK, BLOCK), 2 * BLOCK)
            valid = jnp.logical_or(cur_ok, jnp.logical_and(prev_ok, has_prev))
            q2, do2 = _stack_heads(qs[cur, :], head0), _stack_heads(dos[cur, :], head0)
            kw, vw = ks[window, :], vs[window, :]
            s = jnp.where(valid, _dot_nt(q2, kw), NEG_INF)
            p = jnp.exp(s - per_stacked_row(lss, cur))
            ds = (p * (_dot_nt(do2, vw) - per_stacked_row(dss, cur))).astype(BF16)
            dvs[window, :] += _dot_tn(p.astype(BF16), do2)
            dks[window, :] += _dot_tn(ds, q2)
            dqs[cur, :] = _unstack_heads(_dot(ds, kw), head0)
            return carry

        lax.fori_loop(0, SEQ // BLOCK, block, 0, unroll=16)

        hp = pl.program_id(0)
        copies = []
        for kind in range(3):
            for r in range(d):
                for i in range(nb):
                    src, dst = _group_rows(d, nb, r, i)
                    below = pl.ds(dst.start + BLOCK, BLOCK)
                    if kind == 2:
                        stage[src, :] = dvs[below, :]
                    else:
                        c, s = cos_ref[g, dst, :], sin_ref[g, dst, :]
                        t = dqs[dst, :] * 0.125 if kind == 0 else dks[below, :]
                        stage[src, :] = t * c - _swap_halves(t) * s
            for i in range(SEQ // MM_ROWS):
                rows = pl.ds(i * MM_ROWS, MM_ROWS)
                outs[kind, rows, :] = stage[rows, :].astype(BF16)
            column = pl.multiple_of((kind * 12 + g * 4 + hp) * 128, 128)
            copies.append(pltpu.make_async_copy(outs.at[kind], dproj_ref.at[:, pl.ds(column, 128)], sems.at[kind]))
            copies[-1].start()
        for cp in copies:
            cp.wait()

    return body


def _attn_bwd(q_pm, k_pm, v_pm, cos_t, sin_t, attn, lse, dattn, dproj, ride=None):
    groups = [_attn_bwd_group_body(g) for g in range(3)]

    def body(qs_ref, ks_ref, vs_ref, cos_ref, sin_ref, attn_ref, lse_ref, dattn_ref, dproj_in, dproj_ref, dsum, *scratch):
        del dproj_in
        head0 = _attn_masks()[2]
        for i in range(SEQ // BLOCK):
            rows = pl.ds(i * BLOCK, BLOCK)
            prod = dattn_ref[rows, :] * attn_ref[rows, :]
            d0 = jnp.sum(jnp.where(head0, prod, 0.0), axis=1, keepdims=True)
            d1 = jnp.sum(jnp.where(head0, 0.0, prod), axis=1, keepdims=True)
            dsum[rows, :] = jnp.where(head0, d0, d1)
        for g in range(3):
            groups[g](qs_ref, ks_ref, vs_ref, cos_ref, sin_ref, lse_ref, dattn_ref, dsum, dproj_ref, *scratch)

    def col(base):
        return pl.BlockSpec((SEQ, 128), lambda hp, base=base: (0, base + hp))

    table = pl.BlockSpec((3, SEQ, 128), lambda hp: (0, 0, 0), pipeline_mode=pl.Buffered(1))
    by_phase = pl.BlockSpec((3, SEQ, 128), lambda hp: (0, 0, hp))
    return _call(
        body, "attn_bwd", (4,), [by_phase] * 3 + [table, table, col(0), col(0), col(0), ANY],
        [ANY], [_sds((SEQ, IN_WIDTH), BF16)],
        [pltpu.VMEM((SEQ, 128), F32)]
        + [pltpu.VMEM((SEQ + BLOCK, 128), BF16)] * 2 + [pltpu.VMEM((SEQ, 128), BF16)]
        + [pltpu.VMEM((2, SEQ, 128), F32)] * 2 + [pltpu.VMEM((SEQ, 128), F32)]
        + [pltpu.VMEM((SEQ + BLOCK, 128), F32)] * 2 + [pltpu.VMEM((SEQ, 128), F32)]
        + [pltpu.VMEM((3, SEQ, 128), BF16), pltpu.SemaphoreType.DMA((3,))],
        [q_pm, k_pm, v_pm, cos_t, sin_t, attn, lse, dattn, dproj], ride, aliases={8: 0})


SSM_CHUNKS = 4
CHUNK_STATES = 512
SCAN_ROWS = 8
U_COL = (3 * QKV_WIDTH) // 128


def _cmul(xr, xi, yr, yi):
    return xr * yr - xi * yi, xr * yi + xi * yr


def _ssm_prep(a_re, a_im, log_dt, b_re_t, b_im_t):
    def body(ar_ref, ai_ref, ldt_ref, br_ref, bi_ref, abr_ref, abi_ref, er_ref, ei_ref, bbr_ref, bbi_ref):
        ar, ai = ar_ref[...], ai_ref[...]
        dt = jnp.exp(ldt_ref[...])
        mag = jnp.exp(ar * dt)
        abr, abi = mag * jnp.cos(ai * dt), mag * jnp.sin(ai * dt)
        den = ar * ar + ai * ai
        nr, ni = abr - 1.0, abi
        er, ei = (nr * ar + ni * ai) / den, (ni * ar - nr * ai) / den
        abr_ref[...], abi_ref[...], er_ref[...], ei_ref[...] = abr, abi, er, ei
        er3, ei3 = er[:, None, :], ei[:, None, :]
        br, bi = br_ref[...], bi_ref[...]
        bbr_ref[...] = er3 * br - ei3 * bi
        bbi_ref[...] = er3 * bi + ei3 * br

    gp = jax.ShapeDtypeStruct(a_re.shape, F32)
    gb = jax.ShapeDtypeStruct(b_re_t.shape, F32)
    return _pallas_call(body, name="ssm_prep", out_shape=(gp, gp, gp, gp, gb, gb))(a_re, a_im, log_dt, b_re_t, b_im_t)


def _ssm_param_bwd(a_re, a_im, log_dt, b_re_t, b_im_t, abar_re, abar_im, e_re, e_im, ga_re, ga_im, gbb_re_t, gbb_im_t):
    def body(ar_ref, ai_ref, ldt_ref, br_ref, bi_ref, abr_ref, abi_ref, er_ref, ei_ref, gar_ref, gai_ref, gbr_ref, gbi_ref,
             o_ar, o_ai, o_ldt, o_br, o_bi):
        ar, ai = ar_ref[...], ai_ref[...]
        dt = jnp.exp(ldt_ref[...])
        er, ei = er_ref[...], ei_ref[...]
        br, bi, gbr, gbi = br_ref[...], bi_ref[...], gbr_ref[...], gbi_ref[...]
        er3, ei3 = er[:, None, :], ei[:, None, :]
        o_br[...] = er3 * gbr + ei3 * gbi
        o_bi[...] = er3 * gbi - ei3 * gbr
        ge_r = jnp.sum(br * gbr + bi * gbi, axis=1)
        ge_i = jnp.sum(br * gbi - bi * gbr, axis=1)
        den = ar * ar + ai * ai
        ilr, ili = ar / den, -ai / den
        t_r, t_i = _cmul(ilr, -ili, ge_r, ge_i)
        gab_r, gab_i = gar_ref[...] + t_r, gai_ref[...] + t_i
        gz_r, gz_i = _cmul(abr_ref[...], -abi_ref[...], gab_r, gab_i)
        el_r, el_i = _cmul(er, ei, ilr, ili)
        u_r, u_i = _cmul(el_r, -el_i, ge_r, ge_i)
        o_ar[...] = dt * gz_r - u_r
        o_ai[...] = dt * gz_i - u_i
        o_ldt[...] = jnp.sum(gz_r * ar + gz_i * ai, axis=1, keepdims=True) * dt

    gp = jax.ShapeDtypeStruct(a_re.shape, F32)
    gb = jax.ShapeDtypeStruct(b_re_t.shape, F32)
    return _pallas_call(body, name="ssm_param_bwd", out_shape=(gp, gp, jax.ShapeDtypeStruct(log_dt.shape, F32), gb, gb))(
        a_re, a_im, log_dt, b_re_t, b_im_t, abar_re, abar_im, e_re, e_im, ga_re, ga_im, gbb_re_t, gbb_im_t)


def _block_diag(blocks_re, blocks_im, sign_im, rows_are_channels):
    both = jnp.stack([blocks_re, sign_im * blocks_im]).reshape(2, SSM_CHUNKS, 8, SSM_GROUP, SSM_STATE)
    eye = jnp.eye(8, dtype=F32)
    if rows_are_channels:
        return jnp.einsum("rcghp,gk->cghrkp", both, eye).reshape(SSM_CHUNKS, 128, 2 * CHUNK_STATES)
    return jnp.einsum("rcghp,gk->crkpgh", both, eye).reshape(SSM_CHUNKS, 2 * CHUNK_STATES, 128)


def _block_diag_parts(mat, rows_are_channels):
    if rows_are_channels:
        six = mat.reshape(SSM_CHUNKS, 8, SSM_GROUP, 2, 8, SSM_STATE)
        parts = jnp.einsum("cghrgp->rcghp", six)
    else:
        six = mat.reshape(SSM_CHUNKS, 2, 8, SSM_STATE, 8, SSM_GROUP)
        parts = jnp.einsum("crgpgh->rcghp", six)
    parts = parts.reshape(2, SSM_GROUPS, SSM_GROUP, SSM_STATE)
    return parts[0], parts[1]


def _scan_consts(a_ref, conj, reverse):
    ar = jnp.broadcast_to(a_ref[:, :CHUNK_STATES], (SCAN_ROWS, CHUNK_STATES))
    ai = jnp.broadcast_to(a_ref[:, CHUNK_STATES:], (SCAN_ROWS, CHUNK_STATES))
    if conj:
        ai = -ai
    row = lax.broadcasted_iota(jnp.int32, (SCAN_ROWS, CHUNK_STATES), 0)
    if reverse:
        row = SCAN_ROWS - 1 - row
    zero = jnp.zeros_like(ar)
    steps = []
    pr, pi = ar, ai
    for shift in (1, 2, 4):
        keep = row >= shift
        steps.append((SCAN_ROWS - shift if reverse else shift, jnp.where(keep, pr, zero), jnp.where(keep, pi, zero)))
        pr, pi = _cmul(pr, pi, pr, pi)
    first = row == 0
    return steps, (jnp.where(first, ar, zero), jnp.where(first, ai, zero)), first


def _scan_tile(xr, xi, prev_r, prev_i, steps, carry_in, reverse):
    edge = SCAN_ROWS - 1 if reverse else 1
    cr, ci = pltpu.roll(prev_r, edge, axis=0), pltpu.roll(prev_i, edge, axis=0)
    xr, xi = xr + carry_in[0] * cr - carry_in[1] * ci, xi + carry_in[0] * ci + carry_in[1] * cr
    for shift, mr, mi in steps:
        sr, si = pltpu.roll(xr, shift, axis=0), pltpu.roll(xi, shift, axis=0)
        xr, xi = xr + mr * sr - mi * si, xi + mr * si + mi * sr
    return xr, xi


MM_ROWS = 256


def _ssm_fwd(proj, bmat, cmat, a_chunks, d_skip, ride=None):
    def body(u_ref, b_ref, c_ref, a_ref, d_ref, y_ref, states_ref, h_ref):
        for i in range(SEQ // MM_ROWS):
            rows = pl.ds(i * MM_ROWS, MM_ROWS)
            h_ref[rows, :] = _dot(u_ref[rows, :].astype(BF16), b_ref[...])
        steps, carry_in, _ = _scan_consts(a_ref, conj=False, reverse=False)

        def tile(k, carry):
            rows = pl.ds(pl.multiple_of(k * SCAN_ROWS, SCAN_ROWS), SCAN_ROWS)
            xr, xi = _scan_tile(h_ref[rows, :CHUNK_STATES], h_ref[rows, CHUNK_STATES:], carry[0], carry[1], steps, carry_in, False)
            h_ref[rows, :CHUNK_STATES] = xr
            h_ref[rows, CHUNK_STATES:] = xi
            return xr, xi

        zero = jnp.zeros((SCAN_ROWS, CHUNK_STATES), F32)
        lax.fori_loop(0, SEQ // SCAN_ROWS, tile, (zero, zero), unroll=4)
        for i in range(SEQ // MM_ROWS):
            rows = pl.ds(i * MM_ROWS, MM_ROWS)
            states = h_ref[rows, :].astype(BF16)
            states_ref[rows, :] = states
            y_ref[rows, :] = _dot(states, c_ref[...]) + d_ref[...] * u_ref[rows, :]

    return _call(
        body, "ssm_fwd", (SSM_CHUNKS,),
        [pl.BlockSpec((SEQ, 128), lambda c: (0, U_COL + c)),
         pl.BlockSpec((None, 128, 2 * CHUNK_STATES), lambda c: (c, 0, 0)),
         pl.BlockSpec((None, 2 * CHUNK_STATES, 128), lambda c: (c, 0, 0)),
         pl.BlockSpec((None, 1, 2 * CHUNK_STATES), lambda c: (c, 0, 0)),
         pl.BlockSpec((1, 128), lambda c: (0, c))],
        [pl.BlockSpec((SEQ, 128), lambda c: (0, c)), pl.BlockSpec((SEQ, 2 * CHUNK_STATES), lambda c: (0, c))],
        [_sds((SEQ, SSM_WIDTH), F32), _sds((SEQ, SSM_CHUNKS * 2 * CHUNK_STATES), BF16)],
        [pltpu.VMEM((SEQ, 2 * CHUNK_STATES), F32)],
        [proj, bmat, cmat, a_chunks, d_skip], ride)


def _ssm_bwd(dys, proj, h, bmat, cmat, a_chunks, d_skip, dproj, ride=None):
    def body(dy_ref, u_ref, states_ref, b_ref, c_ref, a_ref, d_ref, dproj_in, du_ref, db_ref, dc_ref, da_ref, dd_ref,
             g_ref, h_ref):
        del dproj_in
        dsum = jnp.zeros((1, 128), F32)
        dcm = jnp.zeros((2 * CHUNK_STATES, 128), F32)
        for i in range(SEQ // MM_ROWS):
            rows = pl.ds(i * MM_ROWS, MM_ROWS)
            h_ref[rows, :] = states_ref[rows, :].astype(F32)
            dy = dy_ref[rows, :]
            g_ref[rows, :] = _dot_nt(dy.astype(BF16), c_ref[...])
            dsum += jnp.sum(dy * u_ref[rows, :], axis=0, keepdims=True)
            dcm += _dot_tn(states_ref[rows, :], dy.astype(BF16))
        dd_ref[...] = dsum
        dc_ref[...] = dcm
        steps, carry_in, _ = _scan_consts(a_ref, conj=True, reverse=True)
        first_row = lax.broadcasted_iota(jnp.int32, (SCAN_ROWS, CHUNK_STATES), 0) == 0
        n_tiles = SEQ // SCAN_ROWS

        def tile(j, carry):
            k = n_tiles - 1 - j
            rows = pl.ds(pl.multiple_of(k * SCAN_ROWS, SCAN_ROWS), SCAN_ROWS)
            before = pl.ds(pl.multiple_of(jnp.maximum(k - 1, 0) * SCAN_ROWS, SCAN_ROWS), SCAN_ROWS)
            gr, gi = _scan_tile(g_ref[rows, :CHUNK_STATES], g_ref[rows, CHUNK_STATES:], carry[0], carry[1], steps, carry_in, True)
            g_ref[rows, :CHUNK_STATES] = gr
            g_ref[rows, CHUNK_STATES:] = gi
            has_before = jnp.where(k > 0, 1.0, 0.0)
            hr = jnp.where(first_row, pltpu.roll(h_ref[before, :CHUNK_STATES], 1, axis=0) * has_before,
                           pltpu.roll(h_ref[rows, :CHUNK_STATES], 1, axis=0))
            hi = jnp.where(first_row, pltpu.roll(h_ref[before, CHUNK_STATES:], 1, axis=0) * has_before,
                           pltpu.roll(h_ref[rows, CHUNK_STATES:], 1, axis=0))
            return gr, gi, carry[2] + hr * gr + hi * gi, carry[3] + hr * gi - hi * gr

        zero = jnp.zeros((SCAN_ROWS, CHUNK_STATES), F32)
        _, _, sar, sai = lax.fori_loop(0, n_tiles, tile, (zero, zero, zero, zero), unroll=4)
        da_ref[:, :CHUNK_STATES] = jnp.sum(sar, axis=0, keepdims=True)
        da_ref[:, CHUNK_STATES:] = jnp.sum(sai, axis=0, keepdims=True)
        dbm = jnp.zeros((128, 2 * CHUNK_STATES), F32)
        for i in range(SEQ // MM_ROWS):
            rows = pl.ds(i * MM_ROWS, MM_ROWS)
            g = g_ref[rows, :].astype(BF16)
            du_ref[rows, :] = (_dot_nt(g, b_ref[...]) + d_ref[...] * dy_ref[rows, :]).astype(BF16)
            dbm += _dot_tn(u_ref[rows, :].astype(BF16), g)
        db_ref[...] = dbm

    chunk_col = pl.BlockSpec((SEQ, 128), lambda c: (0, c))
    return _call(
        body, "ssm_bwd", (SSM_CHUNKS,),
        [chunk_col,
         pl.BlockSpec((SEQ, 128), lambda c: (0, U_COL + c)),
         pl.BlockSpec((SEQ, 2 * CHUNK_STATES), lambda c: (0, c)),
         pl.BlockSpec((None, 128, 2 * CHUNK_STATES), lambda c: (c, 0, 0)),
         pl.BlockSpec((None, 2 * CHUNK_STATES, 128), lambda c: (c, 0, 0)),
         pl.BlockSpec((None, 1, 2 * CHUNK_STATES), lambda c: (c, 0, 0)),
         pl.BlockSpec((1, 128), lambda c: (0, c)), ANY],
        [pl.BlockSpec((SEQ, 128), lambda c: (0, U_COL + c)),
         pl.BlockSpec((None, 128, 2 * CHUNK_STATES), lambda c: (c, 0, 0)),
         pl.BlockSpec((None, 2 * CHUNK_STATES, 128), lambda c: (c, 0, 0)),
         pl.BlockSpec((None, 1, 2 * CHUNK_STATES), lambda c: (c, 0, 0)),
         pl.BlockSpec((1, 128), lambda c: (0, c))],
        [_sds((SEQ, IN_WIDTH), BF16), _sds((SSM_CHUNKS, 128, 2 * CHUNK_STATES), F32),
         _sds((SSM_CHUNKS, 2 * CHUNK_STATES, 128), F32), _sds((SSM_CHUNKS, 1, 2 * CHUNK_STATES), F32), _sds((1, SSM_WIDTH), F32)],
        [pltpu.VMEM((SEQ, 2 * CHUNK_STATES), F32)] * 2, [dys, proj, h, bmat, cmat, a_chunks, d_skip, dproj], ride, aliases={7: 0})


def _ssm_tables(abar_re, abar_im, bbar_re_t, bbar_im_t, c_re, c_im):
    bmat = _block_diag(bbar_re_t, bbar_im_t, 1.0, True).astype(BF16)
    cmat = _block_diag(c_re, c_im, -1.0, False).astype(BF16)
    a_chunks = jnp.concatenate([abar_re.reshape(SSM_CHUNKS, 1, CHUNK_STATES), abar_im.reshape(SSM_CHUNKS, 1, CHUNK_STATES)], axis=2)
    return bmat, cmat, a_chunks


GL_COL = (3 * QKV_WIDTH + SSM_WIDTH) // D_MODEL
GELU_C = math.sqrt(2.0 / math.pi)
GELU_A = 0.044715


def _sds(shape, dtype):
    return jax.ShapeDtypeStruct(shape, dtype)


def _gelu(x):
    t = jnp.tanh(GELU_C * (x + GELU_A * x * x * x))
    return 0.5 * x * (1.0 + t), t


def _gelu_grad(x, t):
    return 0.5 * (1.0 + t) + 0.5 * x * (1.0 - t * t) * GELU_C * (1.0 + 3.0 * GELU_A * x * x)


def _layer_norm(r, g, b):
    mu = jnp.mean(r, axis=-1, keepdims=True)
    xc = r - mu
    rstd = lax.rsqrt(jnp.mean(xc * xc, axis=-1, keepdims=True) + LN_EPS)
    xhat = xc * rstd
    return xhat * g + b, xhat, rstd


def _layer_norm_bwd(dy, xhat, rstd, g):
    dxhat = dy * g
    m1 = jnp.mean(dxhat, axis=-1, keepdims=True)
    m2 = jnp.mean(dxhat * xhat, axis=-1, keepdims=True)
    return rstd * (dxhat - m1 - xhat * m2)


def _proj(x, w_in, ride=None):
    tm, tn = 1024, 1792

    def body(x_ref, w_ref, o_ref):
        o_ref[...] = _dot(x_ref[...].astype(BF16), _side_by_side(w_ref))

    return _call(
        body, "proj", (SEQ // tm, IN_WIDTH // tn),
        [pl.BlockSpec((tm, D_MODEL), lambda i, j: (i, 0)), pl.BlockSpec((2, D_MODEL, tn // 2), lambda i, j: (j, 0, 0))],
        [pl.BlockSpec((tm, tn), lambda i, j: (i, j))], [_sds((SEQ, IN_WIDTH), F32)], [], [x, w_in], ride)


def _row_spec(tm, width, col=0):
    return pl.BlockSpec((tm, width), lambda i, col=col: (i, col))


def _full_spec(shape):
    return pl.BlockSpec(shape, lambda i: (0,) * len(shape))


def _weight_spec(shape):
    return pl.BlockSpec(shape, lambda i: (0,) * len(shape), pipeline_mode=pl.Buffered(1))


def _mixer_out(attn, ys, proj, x, w_ab, w_sb, w_glu, w_out, b_gate, ln_g, ln_b, ride=None):
    tm = 512

    def body(attn_ref, ys_ref, gl0_ref, gl1_ref, x_ref, wab_ref, wsb_ref, wglu_ref, wout_ref, bg_ref, g_ref, b_ref,
             h_ref, xhat_ref, rstd_ref, glu_ref, ya_ref, yssm_ref):
        gy, _ = _gelu(ys_ref[...])
        glu = _dot(gy.astype(BF16), _side_by_side(wglu_ref))
        glu_ref[...] = glu.astype(BF16)
        y_s = glu[:, :SSM_WIDTH] * jax.nn.sigmoid(glu[:, SSM_WIDTH:])
        y_ssm = _dot(y_s.astype(BF16), _side_by_side(wsb_ref))
        y_attn = _dot(attn_ref[...].astype(BF16), _side_by_side(wab_ref))
        ya_ref[...] = y_attn.astype(BF16)
        yssm_ref[...] = y_ssm.astype(BF16)
        g0 = jax.nn.sigmoid(gl0_ref[...] + _side_by_side(bg_ref, 0))
        g1 = jax.nn.sigmoid(gl1_ref[...] + _side_by_side(bg_ref, 1))
        mixed = g0 * y_attn + g1 * y_ssm
        r1 = DN_ALPHA * x_ref[...] + _dot(mixed.astype(BF16), wout_ref[...])
        h, xhat, rstd = _layer_norm(r1, g_ref[...], b_ref[...])
        h_ref[...] = h
        xhat_ref[...] = xhat
        rstd_ref[...] = jnp.broadcast_to(rstd, (tm, 128))

    wide = _sds((SEQ, D_MODEL), F32)
    return _call(
        body, "mixer_out", (SEQ // tm,),
        [_row_spec(tm, ATTN_WIDTH), _row_spec(tm, SSM_WIDTH), _row_spec(tm, D_MODEL, GL_COL), _row_spec(tm, D_MODEL, GL_COL + 1),
         _row_spec(tm, D_MODEL), _weight_spec((N_DEV, ATTN_WIDTH, 128)), _weight_spec((N_DEV, SSM_WIDTH, 128)),
         _weight_spec((N_DEV, SSM_WIDTH, 128)), _weight_spec((D_MODEL, D_MODEL)), _full_spec((N_DEV, 2, 128)),
         _full_spec((1, D_MODEL)), _full_spec((1, D_MODEL))],
        [_row_spec(tm, D_MODEL), _row_spec(tm, D_MODEL), _row_spec(tm, 128), _row_spec(tm, D_MODEL),
         _row_spec(tm, D_MODEL), _row_spec(tm, D_MODEL)],
        [wide, wide, _sds((SEQ, 128), F32)] + [_sds((SEQ, D_MODEL), BF16)] * 3, [],
        [attn, ys, proj, proj, x, w_ab, w_sb, w_glu, w_out, b_gate, ln_g, ln_b], ride)


def _ff_up(h, w_gate, w_up, ride=None):
    tm, tn = 1024, 768

    def body(h_ref, wg_ref, wu_ref, a_ref, b_ref, f_ref):
        hb = h_ref[...].astype(BF16)
        a, b = _dot(hb, _side_by_side(wg_ref)), _dot(hb, _side_by_side(wu_ref))
        a_ref[...] = a.astype(BF16)
        b_ref[...] = b.astype(BF16)
        f_ref[...] = (a * jax.nn.sigmoid(a) * b).astype(BF16)

    tile = pl.BlockSpec((tm, tn), lambda i, j: (i, j))
    wtile = pl.BlockSpec((tn // FF_PAD, D_MODEL, FF_PAD), lambda i, j: (j, 0, 0))
    out = _sds((SEQ, D_FF_PAD), BF16)
    return _call(body, "ff_up", (SEQ // tm, D_FF_PAD // tn), [pl.BlockSpec((tm, D_MODEL), lambda i, j: (i, 0)), wtile, wtile],
                 [tile, tile, tile], [out, out, out], [], [h, w_gate, w_up], ride)


def _ff_down_loss(f, w_down, h, target, ln_g, ln_b):
    tm = 512

    def body(f_ref, w_ref, h_ref, t_ref, g_ref, b_ref, dr_ref, dg_ref, db_ref, loss_ref):
        @pl.when(pl.program_id(0) == 0)
        def _():
            dg_ref[...] = jnp.zeros_like(dg_ref)
            db_ref[...] = jnp.zeros_like(db_ref)
            loss_ref[...] = jnp.zeros_like(loss_ref)

        r2 = DN_ALPHA * h_ref[...] + _dot(f_ref[...], w_ref[...])
        g = g_ref[...]
        out, xhat, rstd = _layer_norm(r2, g, b_ref[...])
        err = out - t_ref[...]
        loss_ref[...] += 0.5 * jnp.sum(jnp.mean(err * err, axis=-1, keepdims=True), axis=0, keepdims=True)
        dout = err * (1.0 / D_MODEL)
        dg_ref[...] += jnp.sum(dout * xhat, axis=0, keepdims=True)
        db_ref[...] += jnp.sum(dout, axis=0, keepdims=True)
        dr_ref[...] = _layer_norm_bwd(dout, xhat, rstd, g)

    vec = _sds((1, D_MODEL), F32)
    return _pallas_call(
        body, name="ff_down_loss", grid=(SEQ // tm,),
        in_specs=[_row_spec(tm, D_FF_PAD), _weight_spec((D_FF_PAD, D_MODEL)), _row_spec(tm, D_MODEL), _row_spec(tm, D_MODEL),
                  _full_spec((1, D_MODEL)), _full_spec((1, D_MODEL))],
        out_specs=(_row_spec(tm, D_MODEL), _full_spec((1, D_MODEL)), _full_spec((1, D_MODEL)), _full_spec((1, 128))),
        out_shape=(_sds((SEQ, D_MODEL), F32), vec, vec, _sds((1, 128), F32)),
        compiler_params=_cparams(dimension_semantics=("arbitrary",)),
    )(f, w_down, h, target, ln_g, ln_b)


def _ff_down_bwd(dr2, w_down, a, b):
    tm, tn = 1024, 768

    def body(dr_ref, w_ref, a_ref, b_ref, da_ref, db_ref):
        df = _dot_nt(dr_ref[...].astype(BF16), w_ref[...])
        av, bv = a_ref[...].astype(F32), b_ref[...].astype(F32)
        sg = jax.nn.sigmoid(av)
        da_ref[...] = (df * bv * sg * (1.0 + av * (1.0 - sg))).astype(BF16)
        db_ref[...] = (df * av * sg).astype(BF16)

    tile = pl.BlockSpec((tm, tn), lambda i, j: (i, j))
    out = _sds((SEQ, D_FF_PAD), BF16)
    return _pallas_call(
        body, name="ff_down_bwd", grid=(SEQ // tm, D_FF_PAD // tn),
        in_specs=[pl.BlockSpec((tm, D_MODEL), lambda i, j: (i, 0)), pl.BlockSpec((tn, D_MODEL), lambda i, j: (j, 0)), tile, tile],
        out_specs=(tile, tile), out_shape=(out, out),
        compiler_params=_cparams(dimension_semantics=("arbitrary", "arbitrary")),
    )(dr2, w_down, a, b)


def _ff_up_bwd(da, db, w_gate, w_up, dr2, xhat1, rstd1, ln_g, ride=None):
    tm, tk = 1024, 768
    nk = D_FF_PAD // tk

    def body(da_ref, db_ref, wg_ref, wu_ref, dr2_ref, xhat_ref, rstd_ref, g_ref, dr1_ref, dg_ref, dbias_ref, acc):
        i, k = pl.program_id(0), pl.program_id(1)

        @pl.when(jnp.logical_and(i == 0, k == 0))
        def _():
            dg_ref[...] = jnp.zeros_like(dg_ref)
            dbias_ref[...] = jnp.zeros_like(dbias_ref)

        part = _dot_nt(da_ref[...], _side_by_side(wg_ref)) + _dot_nt(db_ref[...], _side_by_side(wu_ref))

        @pl.when(k == 0)
        def _():
            acc[...] = part

        @pl.when(k > 0)
        def _():
            acc[...] += part

        @pl.when(k == nk - 1)
        def _():
            dh = DN_ALPHA * dr2_ref[...] + acc[...]
            xhat = xhat_ref[...]
            dg_ref[...] += jnp.sum(dh * xhat, axis=0, keepdims=True)
            dbias_ref[...] += jnp.sum(dh, axis=0, keepdims=True)
            rstd = jnp.max(rstd_ref[...], axis=1, keepdims=True)
            dr1_ref[...] = _layer_norm_bwd(dh, xhat, rstd, g_ref[...])

    hid = pl.BlockSpec((tm, tk), lambda i, k: (i, k))
    wtile = pl.BlockSpec((tk // FF_PAD, D_MODEL, FF_PAD), lambda i, k: (k, 0, 0))
    row = pl.BlockSpec((tm, D_MODEL), lambda i, k: (i, 0))
    vec = pl.BlockSpec((1, D_MODEL), lambda i, k: (0, 0))
    return _call(
        body, "ff_up_bwd", (SEQ // tm, nk),
        [hid, hid, wtile, wtile, row, row, pl.BlockSpec((tm, 128), lambda i, k: (i, 0)), vec],
        [row, vec, vec], [_sds((SEQ, D_MODEL), F32), _sds((1, D_MODEL), F32), _sds((1, D_MODEL), F32)],
        [pltpu.VMEM((tm, D_MODEL), F32)], [da, db, w_gate, w_up, dr2, xhat1, rstd1, ln_g], ride)


def _mixer_bwd(dr1, proj, y_attn, y_ssm, glu, ys, w_ab, w_sb, w_glu, w_out, b_gate, after):
    tm = 256

    def body(dr1_ref, gl0_ref, gl1_ref, ya_ref, yssm_ref, glu_ref, ys_ref, wab_ref, wsb_ref, wglu_ref, wout_ref, bg_ref, after_ref,
             dya_ref, dyssm_ref, dgl_ref, dattn_ref, dglu_ref, dys_ref, mixed_ref, ysb_ref, gy_ref, dbg_ref, stage, copied):
        @pl.when(pl.program_id(0) == 0)
        def _():
            dbg_ref[...] = jnp.zeros_like(dbg_ref)

        dmixed = _dot_nt(dr1_ref[...].astype(BF16), wout_ref[...])
        g0 = jax.nn.sigmoid(gl0_ref[...] + _side_by_side(bg_ref, 0))
        g1 = jax.nn.sigmoid(gl1_ref[...] + _side_by_side(bg_ref, 1))
        y_attn, y_ssm = ya_ref[...].astype(F32), yssm_ref[...].astype(F32)
        mixed_ref[...] = (g0 * y_attn + g1 * y_ssm).astype(BF16)
        dya = (dmixed * g0).astype(BF16)
        dyssm = (dmixed * g1).astype(BF16)
        dya_ref[...] = dya
        dyssm_ref[...] = dyssm
        dgl0 = dmixed * y_attn * g0 * (1.0 - g0)
        dgl1 = dmixed * y_ssm * g1 * (1.0 - g1)
        i, last = pl.program_id(0), SEQ // tm - 1
        slot = i & 1

        def copy_out(buffer, tile):
            window = dgl_ref.at[pl.ds(pl.multiple_of(tile * tm, tm), tm), pl.ds(GL_COL * D_MODEL, 2 * D_MODEL)]
            return pltpu.make_async_copy(stage.at[buffer], window, copied.at[buffer])

        @pl.when(i >= 2)
        def _():
            copy_out(slot, i - 2).wait()

        stage[slot, :, :D_MODEL] = dgl0.astype(BF16)
        stage[slot, :, D_MODEL:] = dgl1.astype(BF16)
        copy_out(slot, i).start()

        @pl.when(i == last)
        def _():
            copy_out(1 - slot, i - 1).wait()
            copy_out(slot, i).wait()
        dbg_ref[:, :D_MODEL] += jnp.sum(dgl0, axis=0, keepdims=True)
        dbg_ref[:, D_MODEL:] += jnp.sum(dgl1, axis=0, keepdims=True)
        dattn_ref[...] = _dot_nt(dya, _side_by_side(wab_ref))
        dy_s = _dot_nt(dyssm, _side_by_side(wsb_ref))
        glu = glu_ref[...].astype(F32)
        glu1, sg = glu[:, :SSM_WIDTH], jax.nn.sigmoid(glu[:, SSM_WIDTH:])
        ysb_ref[...] = (glu1 * sg).astype(BF16)
        dglu1 = (dy_s * sg).astype(BF16)
        dglu2 = (dy_s * glu1 * sg * (1.0 - sg)).astype(BF16)
        dglu_ref[:, :SSM_WIDTH] = dglu1
        dglu_ref[:, SSM_WIDTH:] = dglu2
        dgy = _dot_nt(jnp.concatenate([dglu1, dglu2], axis=1), _side_by_side(wglu_ref))
        ys = ys_ref[...]
        gy, t = _gelu(ys)
        gy_ref[...] = gy.astype(BF16)
        dys_ref[...] = dgy * _gelu_grad(ys, t)

    wide_b, half_b = _sds((SEQ, D_MODEL), BF16), _sds((SEQ, SSM_WIDTH), BF16)
    half_f = _sds((SEQ, SSM_WIDTH), F32)
    return _pallas_call(
        body, name="mixer_bwd", grid=(SEQ // tm,),
        in_specs=[_row_spec(tm, D_MODEL), _row_spec(tm, D_MODEL, GL_COL), _row_spec(tm, D_MODEL, GL_COL + 1), _row_spec(tm, D_MODEL),
                  _row_spec(tm, D_MODEL), _row_spec(tm, D_MODEL), _row_spec(tm, SSM_WIDTH), _full_spec((N_DEV, ATTN_WIDTH, 128)),
                  _full_spec((N_DEV, SSM_WIDTH, 128)), _full_spec((N_DEV, SSM_WIDTH, 128)), _full_spec((D_MODEL, D_MODEL)),
                  _full_spec((N_DEV, 2, 128)), ANY],
        out_specs=(_row_spec(tm, D_MODEL), _row_spec(tm, D_MODEL), ANY, _row_spec(tm, ATTN_WIDTH),
                   _row_spec(tm, D_MODEL), _row_spec(tm, SSM_WIDTH), _row_spec(tm, D_MODEL), _row_spec(tm, SSM_WIDTH),
                   _row_spec(tm, SSM_WIDTH), _full_spec((1, 2 * D_MODEL))),
        out_shape=(wide_b, wide_b, _sds((SEQ, IN_WIDTH), BF16), half_f, wide_b, half_f, wide_b, half_b, half_b,
                   _sds((1, 2 * D_MODEL), F32)),
        scratch_shapes=[pltpu.VMEM((2, tm, 2 * D_MODEL), BF16), pltpu.SemaphoreType.DMA((2,))],
        compiler_params=_cparams(dimension_semantics=("arbitrary",)),
    )(dr1, proj, proj, y_attn, y_ssm, glu, ys, w_ab, w_sb, w_glu, w_out, b_gate, after)


def _grad_x(dproj, w_in, dr1, ride=None):
    tm, tk = 1024, 1792
    nk = IN_WIDTH // tk

    def body(dp_ref, w_ref, dr1_ref, o_ref, acc):
        k = pl.program_id(1)
        part = _dot_nt(dp_ref[...], _side_by_side(w_ref))

        @pl.when(k == 0)
        def _():
            acc[...] = part

        @pl.when(k > 0)
        def _():
            acc[...] += part

        @pl.when(k == nk - 1)
        def _():
            o_ref[...] = DN_ALPHA * dr1_ref[...] + acc[...]

    row = pl.BlockSpec((tm, D_MODEL), lambda i, k: (i, 0))
    return _call(
        body, "grad_x", (SEQ // tm, nk),
        [pl.BlockSpec((tm, tk), lambda i, k: (i, k)), pl.BlockSpec((2, D_MODEL, tk // 2), lambda i, k: (k, 0, 0)), row],
        [row], [_sds((SEQ, D_MODEL), F32)], [pltpu.VMEM((tm, D_MODEL), F32)], [dproj, w_in, dr1], ride)


def _weight_grad(a, b, name, shard_cols=None):
    k, n = a.shape[1], b.shape[1]
    tk = k if shard_cols else k // N_DEV
    tn = n // 4 if shard_cols else min(n, 1024)

    def body(a_ref, b_ref, o_ref):
        grad = _dot_tn(a_ref[...].astype(BF16), b_ref[...].astype(BF16))
        if shard_cols:
            o_ref[0] = grad[:, :shard_cols].astype(BF16)
            o_ref[1] = grad[:, shard_cols:].astype(BF16)
        else:
            o_ref[...] = grad.astype(BF16)

    if shard_cols:
        out_spec = pl.BlockSpec((2, None, tk, shard_cols), lambda kk, j: (0, j, kk, 0))
        out_shape = _sds((2, 4, k, shard_cols), BF16)
    else:
        out_spec = pl.BlockSpec((None, None, tk, tn), lambda kk, j: (kk % 2, kk // 2, 0, j))
        out_shape = _sds((2, 4, tk, n), BF16)
    return _call(body, name, (k // tk, n // tn),
                 [pl.BlockSpec((SEQ, tk), lambda kk, j: (0, kk)), pl.BlockSpec((SEQ, tn), lambda kk, j: (0, j))],
                 [out_spec], [out_shape], [], [a, b])[0]


def _weight_grad_rows(a, b, core, name, shard_cols, first, count, ride, in_place=False):
    k = a.shape[1]

    def body(a_ref, b_ref, o_ref):
        o_ref[...] = _dot_tn(a_ref[...].astype(BF16), b_ref[...].astype(BF16)).astype(BF16)

    def shard(j, core_ref):
        row = j + first
        return 0, jnp.where(row < 4, 2 * row + 1 - core_ref[0], 2 * (row - 4) + core_ref[0])

    return _call(body, name, (count,),
                 [pl.BlockSpec((SEQ, k), lambda j, core_ref: (0, 0), pipeline_mode=pl.Buffered(1)),
                  pl.BlockSpec((SEQ, shard_cols), shard)],
                 [pl.BlockSpec((None, k, shard_cols), lambda j, core_ref: (j + first, 0, 0))],
                 [_sds((N_DEV, k, shard_cols), BF16)], [], [a, b], ride, aliases={2: 0} if in_place else None, prefetch=core)


MESH = pl.DeviceIdType.MESH
ANY = pl.BlockSpec(memory_space=pl.ANY)


def _place():
    return lax.axis_index("x"), lax.axis_index("y"), lax.axis_index("c")


def _other_chips(x, y):
    return [(1 - x, y), (x, 1 - y), (1 - x, 1 - y)]


class _Ride:
    def __init__(self, operands, results, aliases, sems, start, wait):
        self.operands, self.results, self.aliases, self.sems = list(operands), list(results), dict(aliases), list(sems)
        self.start, self.wait = start, wait

    def __add__(self, other):
        n_in, n_out, n_sem = len(self.operands), len(self.results), len(self.sems)

        def both(which):
            def run(ins, outs, sems):
                getattr(self, which)(ins[:n_in], outs[:n_out], sems[:n_sem])
                getattr(other, which)(ins[n_in:], outs[n_out:], sems[n_sem:])
            return run

        aliases = {**self.aliases, **{n_in + i: n_out + j for i, j in other.aliases.items()}}
        return _Ride(self.operands + other.operands, self.results + other.results, aliases, self.sems + other.sems,
                     both("start"), both("wait"))


def _call(body, name, grid, in_specs, out_specs, out_shape, scratch_shapes, operands, ride=None, aliases=None, prefetch=None):
    in_specs, out_specs, out_shape = list(in_specs), list(out_specs), list(out_shape)
    scratch_shapes, operands, aliases = list(scratch_shapes), list(operands), dict(aliases or {})
    kernel_body = body
    if ride is not None:
        n_in, n_out, n_scr, r_in, r_out = len(in_specs), len(out_specs), len(scratch_shapes), len(ride.operands), len(ride.results)

        def kernel_body(*refs):
            out0, scr0 = n_in + r_in, n_in + r_in + n_out + r_out
            ride_refs = (refs[n_in:out0], refs[out0 + n_out:scr0], refs[scr0 + n_scr:])
            ids = [pl.program_id(i) for i in range(len(grid))]
            first = functools.reduce(jnp.logical_and, [i == 0 for i in ids])
            last = functools.reduce(jnp.logical_and, [i == g - 1 for i, g in zip(ids, grid)])

            @pl.when(first)
            def _():
                ride.start(*ride_refs)

            body(*refs[:n_in], *refs[out0:out0 + n_out], *refs[scr0:scr0 + n_scr])

            @pl.when(last)
            def _():
                ride.wait(*ride_refs)

        aliases.update({n_in + i: n_out + j for i, j in ride.aliases.items()})
        in_specs += [ANY] * r_in
        out_specs += [ANY] * r_out
        out_shape += ride.results
        scratch_shapes += ride.sems
        operands += ride.operands
    params = _cparams(dimension_semantics=("arbitrary",) * len(grid))
    if prefetch is None:
        return _pallas_call(
            kernel_body, name=name, grid=grid, in_specs=in_specs, out_specs=out_specs, out_shape=out_shape,
            scratch_shapes=scratch_shapes, input_output_aliases=aliases, compiler_params=params,
        )(*operands)

    def with_prefetch(prefetch_ref, *refs):
        kernel_body(*refs)

    return _pallas_call(
        with_prefetch, name=name,
        grid_spec=pltpu.PrefetchScalarGridSpec(num_scalar_prefetch=1, grid=grid, in_specs=in_specs, out_specs=out_specs,
                                               scratch_shapes=scratch_shapes),
        out_shape=out_shape, input_output_aliases={i + 1: j for i, j in aliases.items()}, compiler_params=params,
    )(prefetch, *operands)


def _after(*arrays):
    return _Ride(arrays, [], {}, [], lambda *refs: None, lambda *refs: None)


def _gather_first_level(shards):
    n = len(shards)

    def copies(ins, outs, sems, landed):
        send_sems, recv_sems, local_sems = sems
        x, y, c = _place()
        peers = [(x, y, 1 - c)] + [(px, py, c) for px, py in _other_chips(x, y)]

        def row(peer):
            return 4 * x + 2 * y + c if not landed else 4 * peer[0] + 2 * peer[1] + peer[2]

        local = [pltpu.make_async_copy(ins[a], outs[a].at[4 * x + 2 * y + c], local_sems.at[a]) for a in range(n)]
        remote = [pltpu.make_async_remote_copy(
            src_ref=ins[a], dst_ref=outs[a].at[row(peer)], send_sem=send_sems.at[a, k], recv_sem=recv_sems.at[a, k],
            device_id=peer, device_id_type=MESH) for a in range(n) for k, peer in enumerate(peers)]
        return local, remote

    def start(ins, outs, sems):
        local, remote = copies(ins, outs, sems, False)
        for cp in local + remote:
            cp.start()

    def wait(ins, outs, sems):
        local, sent = copies(ins, outs, sems, False)
        for cp in copies(ins, outs, sems, True)[1]:
            cp.wait_recv()
        for cp in sent:
            cp.wait_send()
        for cp in local:
            cp.wait()

    return _Ride(shards, [_sds((N_DEV,) + s.shape, s.dtype) for s in shards], {},
                 [pltpu.SemaphoreType.DMA((n, 4)), pltpu.SemaphoreType.DMA((n, 4)), pltpu.SemaphoreType.DMA((n,))], start, wait)


def _gather_second_level(buffers):
    n = len(buffers)

    def copies(outs, sems, core):
        send_sems, recv_sems = sems
        x, y, c = _place()
        return [pltpu.make_async_remote_copy(
            src_ref=outs[a].at[4 * px + 2 * py + core], dst_ref=outs[a].at[4 * px + 2 * py + core], send_sem=send_sems.at[a, j],
            recv_sem=recv_sems.at[a, j], device_id=(x, y, 1 - c), device_id_type=MESH)
            for a in range(n) for j, (px, py) in enumerate(_other_chips(x, y))]

    def start(ins, outs, sems):
        for cp in copies(outs, sems, lax.axis_index("c")):
            cp.start()

    def wait(ins, outs, sems):
        for cp in copies(outs, sems, 1 - lax.axis_index("c")):
            cp.wait_recv()
        for cp in copies(outs, sems, lax.axis_index("c")):
            cp.wait_send()

    return _Ride(buffers, [_sds(b.shape, b.dtype) for b in buffers], {i: i for i in range(n)},
                 [pltpu.SemaphoreType.DMA((n, 3)), pltpu.SemaphoreType.DMA((n, 3))], start, wait)


def _relayed_gather(shards):
    n = len(shards)
    buffers = [_sds((N_DEV,) + s.shape, s.dtype) for s in shards]
    dma = pltpu.SemaphoreType.DMA

    def remote(src, dst, send_sem, recv_sem, to):
        return pltpu.make_async_remote_copy(src_ref=src, dst_ref=dst, send_sem=send_sem, recv_sem=recv_sem,
                                            device_id=to, device_id_type=MESH)

    def row(px, py, pc):
        return 4 * px + 2 * py + pc

    def ride(operands, aliases, sems, copies):
        def start(ins, outs, sem_refs):
            local, sent = copies(ins, outs, sem_refs, False)
            for cp in local + sent:
                cp.start()

        def wait(ins, outs, sem_refs):
            local, sent = copies(ins, outs, sem_refs, False)
            for cp in copies(ins, outs, sem_refs, True)[1]:
                cp.wait_recv()
            for cp in sent:
                cp.wait_send()
            for cp in local:
                cp.wait()

        return _Ride(operands, buffers, aliases, sems, start, wait)

    def first(ins, outs, sems, landed):
        x, y, c = _place()
        peers = [(x, y, 1 - c), (1 - x, y, c), (x, 1 - y, c)]
        local = [pltpu.make_async_copy(ins[a], outs[a].at[row(x, y, c)], sems[2].at[a]) for a in range(n)]
        return local, [remote(ins[a], outs[a].at[row(*peer) if landed else row(x, y, c)], sems[0].at[a, k], sems[1].at[a, k], peer)
                       for a in range(n) for k, peer in enumerate(peers)]

    def second(ins, outs, sems, landed):
        x, y, c = _place()
        mine = 1 - c if landed else c
        copies = []
        for a in range(n):
            half = shards[a].shape[0] // 2
            over_x, over_y, diagonal = outs[a].at[row(1 - x, y, mine)], outs[a].at[row(x, 1 - y, mine)], outs[a].at[row(1 - x, 1 - y, c)]
            lower, upper = pl.ds(0, half), pl.ds(half, half)
            copies += [remote(over_x, over_x, sems[0].at[a, 0], sems[1].at[a, 0], (x, y, 1 - c)),
                       remote(over_y, over_y, sems[0].at[a, 1], sems[1].at[a, 1], (x, y, 1 - c))]
            if landed:
                copies += [remote(diagonal.at[lower], diagonal.at[lower], sems[0].at[a, 2], sems[1].at[a, 2], (1 - x, y, c)),
                           remote(diagonal.at[upper], diagonal.at[upper], sems[0].at[a, 3], sems[1].at[a, 3], (x, 1 - y, c))]
            else:
                copies += [remote(over_y.at[lower], over_y.at[lower], sems[0].at[a, 2], sems[1].at[a, 2], (1 - x, y, c)),
                           remote(over_x.at[upper], over_x.at[upper], sems[0].at[a, 3], sems[1].at[a, 3], (x, 1 - y, c))]
        return [], copies

    def third(ins, outs, sems, landed):
        x, y, c = _place()
        return [], [remote(outs[a].at[row(1 - x, 1 - y, 1 - c if landed else c)], outs[a].at[row(1 - x, 1 - y, 1 - c if landed else c)],
                           sems[0].at[a], sems[1].at[a], (x, y, 1 - c)) for a in range(n)]

    def later(copies, n_sems):
        return lambda partly: ride(partly, {i: i for i in range(n)}, [dma((n,) + n_sems), dma((n,) + n_sems)], copies)

    return ride(shards, {}, [dma((n, 3)), dma((n, 3)), dma((n,))], first), later(second, (4,)), later(third, ())


def _sibling_swap_ride(grads, halves=True):
    n = len(grads)

    def copies(ins, outs, sems):
        x, y, c = _place()
        return [pltpu.make_async_remote_copy(
            src_ref=ins[a].at[1 - c] if halves else ins[a].at[pl.ds(0, 4)], dst_ref=outs[a], send_sem=sems[0].at[a],
            recv_sem=sems[1].at[a], device_id=(x, y, 1 - c), device_id_type=MESH) for a in range(n)]

    def start(ins, outs, sems):
        for cp in copies(ins, outs, sems):
            cp.start()

    def wait(ins, outs, sems):
        for cp in copies(ins, outs, sems):
            cp.wait()

    return _Ride(grads, [_sds((4,) + g.shape[-2:], g.dtype) for g in grads], {},
                 [pltpu.SemaphoreType.DMA((n,)), pltpu.SemaphoreType.DMA((n,))], start, wait)


def _chip_swap_ride(sums):
    n = len(sums)

    def copies(ins, outs, sems, landed):
        send_sems, recv_sems, local_sems = sems
        x, y, c = _place()
        mine = 2 * x + y
        local = [pltpu.make_async_copy(ins[a].at[mine], outs[a].at[mine], local_sems.at[a]) for a in range(n)]
        remote = [pltpu.make_async_remote_copy(
            src_ref=ins[a].at[2 * px + py], dst_ref=outs[a].at[2 * px + py if landed else mine], send_sem=send_sems.at[a, j],
            recv_sem=recv_sems.at[a, j], device_id=(px, py, c), device_id_type=MESH)
            for a in range(n) for j, (px, py) in enumerate(_other_chips(x, y))]
        return local, remote

    def start(ins, outs, sems):
        local, remote = copies(ins, outs, sems, False)
        for cp in local + remote:
            cp.start()

    def wait(ins, outs, sems):
        local, sent = copies(ins, outs, sems, False)
        for cp in copies(ins, outs, sems, True)[1]:
            cp.wait_recv()
        for cp in sent:
            cp.wait_send()
        for cp in local:
            cp.wait()

    return _Ride(sums, [_sds(s.shape, s.dtype) for s in sums], {},
                 [pltpu.SemaphoreType.DMA((n, 3)), pltpu.SemaphoreType.DMA((n, 3)), pltpu.SemaphoreType.DMA((n,))], start, wait)


def _send_buffers(shards, name):
    n = len(shards)

    def body(*refs):
        for (w, transposed, rows, cols), w_ref, o_ref in zip(shards, refs[:n], refs[n:]):
            if transposed:
                c, r = w.shape
                padded = jnp.concatenate([w_ref[...], jnp.zeros((cols - c, r), F32)], axis=0) if cols > c else w_ref[...]
                o_ref[...] = padded.T.astype(BF16)
            else:
                r, c = w.shape
                if (r, c) != (rows, cols):
                    o_ref[...] = jnp.zeros((rows, cols), BF16)
                o_ref[:r, :c] = w_ref[...].astype(BF16)

    return _pallas_call(body, name=name, out_shape=[_sds((rows, cols), BF16) for _, _, rows, cols in shards])(
        *[w for w, _, _, _ in shards])


def _all_gather(shards, name):
    n = len(shards)
    first, second, third = _relayed_gather(shards)
    levels = [first, second(shards), third(shards)]
    counts = [len(level.sems) for level in levels]

    def body(*refs):
        ins, outs, sems = refs[:n], refs[n:2 * n], refs[2 * n:]
        for i, level in enumerate(levels):
            mine = sems[sum(counts[:i]):sum(counts[:i + 1])]
            level.start(ins, outs, mine)
            level.wait(ins, outs, mine)

    return _pallas_call(
        body, name=name, in_specs=[ANY] * n, out_specs=[ANY] * n, out_shape=first.results,
        scratch_shapes=[s for level in levels for s in level.sems],
    )(*shards)


HBM = pl.BlockSpec(memory_space=pltpu.HBM)
SEMAPHORES = pl.BlockSpec(memory_space=pltpu.SEMAPHORE)
IN_FLIGHT = pltpu.CompilerParams(has_side_effects=pltpu.SideEffectType.DATAFLOW_SIDE_EFFECTING)


def _chip_swap_copies(src_refs, land_refs, send_sems, recv_sems, landed):
    x, y, c = _place()
    return [pltpu.make_async_remote_copy(
        src_ref=src.at[2 * px + py], dst_ref=land.at[2 * px + py if landed else 2 * x + y], send_sem=send_sems.at[3 * a + j],
        recv_sem=recv_sems.at[3 * a + j], device_id=(px, py, c), device_id_type=MESH)
        for a, (src, land) in enumerate(zip(src_refs, land_refs)) for j, (px, py) in enumerate(_other_chips(x, y))]


def _chip_swap_start(sums, name):
    n = len(sums)

    def body(*refs):
        src_refs, land_refs, (send_sems, recv_sems), token = refs[:n], refs[n:2 * n], refs[2 * n:2 * n + 2], refs[-1]
        for cp in _chip_swap_copies(src_refs, land_refs, send_sems, recv_sems, False):
            cp.start()
        token[...] = jnp.zeros_like(token)

    kept = [pltpu.HBM(s.shape, s.dtype) for s in sums]
    out = _pallas_call(
        body, name=name,
        out_shape=[pltpu.SemaphoreType.DMA((3 * n,)), pltpu.SemaphoreType.DMA((3 * n,))] + kept + kept + [_sds((8, 128), F32)],
        in_specs=[HBM] * (2 * n), out_specs=[SEMAPHORES, SEMAPHORES] + [HBM] * (2 * n) + [pl.BlockSpec(memory_space=pltpu.VMEM)],
        input_output_aliases={i: 2 + i for i in range(2 * n)}, compiler_params=IN_FLIGHT,
    )(*[pltpu.with_memory_space_constraint(s, pltpu.HBM) for s in sums],
      *[pltpu.with_memory_space_constraint(lax.empty(s.shape, s.dtype), pltpu.HBM) for s in sums])
    return out[0], out[1], out[2:2 + n], out[2 + n:2 + 2 * n], out[-1]


def _chip_swap_wait(send_sems, recv_sems, sums, landings, after, name):
    n = len(sums)

    def body(*refs):
        src_refs, land_refs, (send_sems, recv_sems) = refs[:n], refs[n:2 * n], refs[2 * n:2 * n + 2]
        for cp in _chip_swap_copies(src_refs, land_refs, send_sems, recv_sems, False):
            cp.wait_send()
        for cp in _chip_swap_copies(src_refs, land_refs, send_sems, recv_sems, True):
            cp.wait_recv()

    out = _pallas_call(
        body, name=name, out_shape=[pltpu.HBM(s.shape, s.dtype) for s in list(sums) + list(landings)],
        in_specs=[HBM] * (2 * n) + [SEMAPHORES, SEMAPHORES] + [ANY] * len(after), out_specs=[HBM] * (2 * n),
        input_output_aliases={i: i for i in range(2 * n)}, compiler_params=IN_FLIGHT,
    )(*sums, *landings, send_sems, recv_sems, *after)
    return out[:n], out[n:]


def _pair_sums(gs, rs, core, name):
    n_arrays = len(gs)

    def body(core_ref, *refs):
        for g_ref, r_ref, o_ref in zip(refs[:n_arrays], refs[n_arrays:2 * n_arrays], refs[2 * n_arrays:]):
            o_ref[...] = (g_ref[...].astype(F32) + r_ref[...].astype(F32)).astype(o_ref.dtype)

    def chip(g):
        return pl.BlockSpec((None,) + g.shape[-2:], lambda p, core_ref: (p, 0, 0))

    def own(g):
        if g.ndim == 3:
            return pl.BlockSpec((None,) + g.shape[-2:], lambda p, core_ref: (p + 4, 0, 0))
        return pl.BlockSpec((None, None) + g.shape[2:], lambda p, core_ref: (core_ref[0], p, 0, 0))

    return _pallas_call(
        body, name=name,
        grid_spec=pltpu.PrefetchScalarGridSpec(
            num_scalar_prefetch=1, grid=(4,), in_specs=[own(g) for g in gs] + [chip(g) for g in gs],
            out_specs=[chip(g) for g in gs]),
        out_shape=[_sds((4,) + g.shape[-2:], g.dtype) for g in gs], compiler_params=_cparams(dimension_semantics=("arbitrary",)),
    )(core, *gs, *rs)


def _adamw_math(w, g, m, v):
    m = ADAM_B1 * m + (1.0 - ADAM_B1) * g
    v = ADAM_B2 * v + (1.0 - ADAM_B2) * (g * g)
    m_hat = m / (1.0 - ADAM_B1 ** ADAM_STEP)
    v_hat = v / (1.0 - ADAM_B2 ** ADAM_STEP)
    return -ADAM_LR * (m_hat / (jnp.sqrt(v_hat) + ADAM_EPS) + ADAM_WD * w), m, v


def _adamw_many(weights, name, ride=None):
    steps = 4
    in_specs, out_specs, out_shape, operands, tiles = [], [], [], [], []
    for w, m, v, parts, own, transposed in weights:
        _, pr, pc = parts.shape
        if transposed:
            c, r = w.shape
            tile = pl.BlockSpec((c, r // steps), lambda i: (0, i))
            part_tile = pl.BlockSpec((4, r // steps, pc), lambda i: (0, i, 0))
            tiles.append((c, r // steps))
        elif w.shape[0] % (8 * steps) == 0:
            r, c = w.shape
            tile = pl.BlockSpec((r // steps, c), lambda i: (i, 0))
            part_tile = pl.BlockSpec((4, r // steps, pc), lambda i: (0, i, 0))
            tiles.append((r // steps, c))
        else:
            tile = pl.BlockSpec(w.shape, lambda i: (0, 0))
            part_tile = pl.BlockSpec(parts.shape, lambda i: (0, 0, 0))
            tiles.append(w.shape)
        in_specs += [tile, tile, tile] + [part_tile] * (1 if own is None else 2)
        out_specs += [tile] * 4
        out_shape += [_sds(w.shape, F32)] * 4
        operands += [w, m, v, parts] + ([] if own is None else [own])
    n_in = len(operands)

    def body(*refs):
        ins, outs = list(refs[:n_in]), refs[n_in:]
        this_chip = 2 * lax.axis_index("x") + lax.axis_index("y")
        for k, (_, _, _, _, own, transposed) in enumerate(weights):
            w_ref, m_ref, v_ref, p_ref = ins[:4]
            own_ref = None if own is None else ins[4]
            del ins[:4 if own is None else 5]
            rows, cols = tiles[k]
            g = None
            for q in range(4):
                index = (q,) if transposed else (q, slice(0, rows), slice(0, cols))
                part = p_ref[index] if own is None else jnp.where(this_chip == q, own_ref[index], p_ref[index])
                g = part.astype(F32) if g is None else g + part.astype(F32)
            if transposed:
                g = g.T[:rows]
            g_out, d_out, m_out, v_out = outs[4 * k:4 * k + 4]
            g_out[...] = g
            d_out[...], m_out[...], v_out[...] = _adamw_math(w_ref[...], g, m_ref[...], v_ref[...])

    return _call(body, name, (steps,), in_specs, out_specs, out_shape, [], operands, ride)


SMALL = ("ssm_a_re", "ssm_a_im", "ssm_log_dt", "ssm_b_re", "ssm_b_im", "ssm_c_re", "ssm_c_im", "ssm_d",
         "ln1_g", "ln1_b", "ln2_g", "ln2_b")


def _pack_rows(arrays):
    rows = []
    for a in arrays:
        flat = a.reshape(-1)
        rows.append(jnp.pad(flat, (0, -flat.shape[0] % 128)).reshape(-1, 128))
    packed = jnp.concatenate(rows, axis=0)
    return jnp.pad(packed, ((0, -packed.shape[0] % 8), (0, 0)))


def _unpack_rows(packed, shapes):
    out, row = [], 0
    for shape in shapes:
        size = math.prod(shape)
        n_rows = -(-size // 128)
        out.append(packed[row:row + n_rows].reshape(-1)[:size].reshape(shape))
        row += n_rows
    return out


def _sum_devices(parts):
    def body(p_ref, o_ref):
        total = p_ref[0]
        for dev in range(1, N_DEV):
            total = total + p_ref[dev]
        o_ref[...] = total

    return _pallas_call(body, name="sum_devices", out_shape=_sds(parts.shape[1:], F32))(parts)


def _adamw_replicated(ws, ms, vs, gs):
    n = len(ws)

    def body(*refs):
        w_refs, m_refs, v_refs, g_refs, d_out, m_out, v_out = (refs[i * n:(i + 1) * n] for i in range(7))
        for i in range(n):
            d_out[i][...], m_out[i][...], v_out[i][...] = _adamw_math(w_refs[i][...], g_refs[i][...], m_refs[i][...], v_refs[i][...])

    out = _pallas_call(body, name="adamw_replicated", out_shape=[_sds(w.shape, F32) for w in ws] * 3,
                       compiler_params=_cparams())(*ws, *ms, *vs, *gs)
    return out[:n], out[n:2 * n], out[2 * n:]


def kernel(x, w_in, b_gate, w_attn_br, w_ssm_br, w_out, ssm_a_re, ssm_a_im, ssm_log_dt, ssm_b_re, ssm_b_im, ssm_c_re, ssm_c_im, ssm_d, w_glu, ln1_g, ln1_b, w_ff_gate, w_ff_up, w_ff_down, ln2_g, ln2_b, loss_target, m_w_in, m_b_gate, m_w_attn_br, m_w_ssm_br, m_w_out, m_ssm_a_re, m_ssm_a_im, m_ssm_log_dt, m_ssm_b_re, m_ssm_b_im, m_ssm_c_re, m_ssm_c_im, m_ssm_d, m_w_glu, m_ln1_g, m_ln1_b, m_w_ff_gate, m_w_ff_up, m_w_ff_down, m_ln2_g, m_ln2_b, v_w_in, v_b_gate, v_w_attn_br, v_w_ssm_br, v_w_out, v_ssm_a_re, v_ssm_a_im, v_ssm_log_dt, v_ssm_b_re, v_ssm_b_im, v_ssm_c_re, v_ssm_c_im, v_ssm_d, v_w_glu, v_ln1_g, v_ln1_b, v_w_ff_gate, v_w_ff_up, v_w_ff_down, v_ln2_g, v_ln2_b):
    given = dict(locals())
    x2, target = x[0], loss_target[0]
    core = lax.axis_index("c").astype(jnp.int32).reshape(1)

    sharded = ("w_in", "w_attn_br", "w_ssm_br", "w_glu", "w_ff_gate", "w_ff_up", "b_gate", "w_out", "w_ff_down")
    send_shape = dict(w_in=(D_MODEL, 896), w_attn_br=(ATTN_WIDTH, 128), w_ssm_br=(SSM_WIDTH, 128), w_glu=(SSM_WIDTH, 128),
                      w_out=(128, D_MODEL), w_ff_gate=(D_MODEL, FF_PAD), w_ff_up=(D_MODEL, FF_PAD), w_ff_down=(FF_PAD, D_MODEL))
    local = {k: given[k][0] for k in sharded}
    narrow = ("w_ff_gate", "w_ff_up")
    def to_send(k):
        return (local[k].T, True, *send_shape[k]) if k in narrow else (local[k], False, *send_shape[k])

    later = [k for k in sharded if k not in ("w_in", "b_gate")]
    sends = dict(zip(["w_in"] + later, _send_buffers([to_send("w_in")], "send_w_in")
                     + _send_buffers([to_send(k) for k in later], "send_weights")))
    sends["b_gate"] = local["b_gate"]
    mixer_weights = ("w_attn_br", "w_ssm_br", "w_glu", "b_gate", "w_out")
    ff_weights = ("w_ff_gate", "w_ff_up", "w_ff_down")
    wt = {}
    wt["w_in"], = _all_gather([sends["w_in"]], "gather_w_in")

    a_re, a_im, log_dt = ssm_a_re[0], ssm_a_im[0], ssm_log_dt[0].reshape(SSM_GROUPS, 1)
    b_re_t, b_im_t = ssm_b_re[0].transpose(0, 2, 1), ssm_b_im[0].transpose(0, 2, 1)
    abar_re, abar_im, e_re, e_im, bbar_re_t, bbar_im_t = _ssm_prep(a_re, a_im, log_dt, b_re_t, b_im_t)
    bmat, cmat, a_chunks = _ssm_tables(abar_re, abar_im, bbar_re_t, bbar_im_t, ssm_c_re[0], ssm_c_im[0])
    cos_t, sin_t = _rope_tables()

    big_mixer, ff_in = [k for k in mixer_weights if k != "b_gate"], ("w_ff_gate", "w_ff_up")
    n_mixer = len(big_mixer)
    mixer_1, mixer_2, mixer_3 = _relayed_gather([sends[k] for k in big_mixer])
    ff_in_1, ff_in_2, ff_in_3 = _relayed_gather([sends[k] for k in ff_in])
    ff_down_1, ff_down_2, ff_down_3 = _relayed_gather([sends["w_ff_down"]])
    proj, *landed = _proj(x2, wt["w_in"], mixer_1 + _gather_first_level([sends["b_gate"]]))
    mixer, bias = landed[:n_mixer], landed[n_mixer:]
    attn, lse, q_pm, k_pm, v_pm, *landed = _attn_fwd(proj, cos_t, sin_t,
                                                     mixer_2(mixer) + _gather_second_level(bias) + ff_in_1)
    mixer, b_gate_full, ff = landed[:n_mixer], landed[n_mixer], landed[n_mixer + 1:]
    ys, states, *landed = _ssm_fwd(proj, bmat, cmat, a_chunks, ssm_d, mixer_3(mixer) + ff_in_2(ff) + ff_down_1)
    wt.update(zip(big_mixer, landed[:n_mixer]))
    ff, ff_down = landed[n_mixer:n_mixer + 2], landed[n_mixer + 2:]
    wt["w_out"] = wt["w_out"].reshape(D_MODEL, D_MODEL)
    h, xhat1, rstd1, glu, y_attn, y_ssm, *landed = _mixer_out(
        attn, ys, proj, x2, wt["w_attn_br"], wt["w_ssm_br"], wt["w_glu"], wt["w_out"], b_gate_full, ln1_g, ln1_b,
        ff_in_3(ff) + ff_down_2(ff_down))
    wt.update(zip(ff_in, landed[:2]))
    ff_a, ff_b, ff_f, w_ff_down = _ff_up(h, wt["w_ff_gate"], wt["w_ff_up"], ff_down_3(landed[2:]))
    wt["w_ff_down"] = w_ff_down.reshape(D_FF_PAD, D_MODEL)
    dr2, d_ln2_g, d_ln2_b, loss_lanes = _ff_down_loss(ff_f, wt["w_ff_down"], h, target, ln2_g, ln2_b)

    def pair_sums(names, contrib, from_sibling):
        return _pair_sums([contrib[k] for k in names], from_sibling, core, "pair_sums_" + names[0])

    d_a, d_b = _ff_down_bwd(dr2, wt["w_ff_down"], ff_a, ff_b)
    contrib = dict(w_ff_gate=_weight_grad(h, d_a, "wgrad_w_ff_gate", FF_PAD),
                   w_ff_up=_weight_grad(h, d_b, "wgrad_w_ff_up", FF_PAD),
                   w_ff_down=_weight_grad(ff_f, dr2, "wgrad_w_ff_down"))
    dr1, d_ln1_g, d_ln1_b, *from_sibling = _ff_up_bwd(
        d_a, d_b, wt["w_ff_gate"], wt["w_ff_up"], dr2, xhat1, rstd1, ln1_g, _sibling_swap_ride([contrib[k] for k in ff_weights]))
    ff_send, ff_recv, ff_sums, ff_landing, ff_token = _chip_swap_start(pair_sums(ff_weights, contrib, from_sibling),
                                                                       "ff_chip_swap_start")

    d_ya, d_yssm, d_proj, d_attn, d_glu, d_ys, mixed, y_s, gy, d_bg = _mixer_bwd(
        dr1, proj, y_attn, y_ssm, glu, ys, wt["w_attn_br"], wt["w_ssm_br"], wt["w_glu"], wt["w_out"], b_gate_full, ff_token)
    contrib.update(w_attn_br=_weight_grad(attn, d_ya, "wgrad_w_attn_br", 128),
                   w_ssm_br=_weight_grad(y_s, d_yssm, "wgrad_w_ssm_br", 128),
                   w_glu=_weight_grad(gy, d_glu, "wgrad_w_glu", 128),
                   w_out=_weight_grad(mixed, dr1, "wgrad_w_out"),
                   b_gate=d_bg.reshape(2, 4, 2, 128).transpose(2, 1, 0, 3))
    d_proj, *landed = _attn_bwd(q_pm, k_pm, v_pm, cos_t, sin_t, attn, lse, d_attn, d_proj,
                                _sibling_swap_ride([contrib[k] for k in mixer_weights]))
    ff_own, ff_parts = _chip_swap_wait(ff_send, ff_recv, ff_sums, ff_landing, [d_proj], "ff_chip_swap_wait")
    parts, own_sums = dict(zip(ff_weights, ff_parts)), dict(zip(ff_weights, ff_own))
    mixer_sums = pair_sums(mixer_weights, contrib, landed)
    d_proj, d_bmat, d_cmat, d_abar, d_skip, *landed = _ssm_bwd(d_ys, proj, states, bmat, cmat, a_chunks, ssm_d, d_proj,
                                                               _chip_swap_ride(mixer_sums))
    parts.update(zip(mixer_weights, landed))

    gbb_re_t, gbb_im_t = _block_diag_parts(d_bmat, True)
    gc_re, gc_im = _block_diag_parts(d_cmat, False)
    ga_re = d_abar[:, 0, :CHUNK_STATES].reshape(SSM_GROUPS, SSM_STATE)
    ga_im = d_abar[:, 0, CHUNK_STATES:].reshape(SSM_GROUPS, SSM_STATE)
    g_a_re, g_a_im, g_log_dt, g_b_re_t, g_b_im_t = _ssm_param_bwd(
        a_re, a_im, log_dt, b_re_t, b_im_t, abar_re, abar_im, e_re, e_im, ga_re, ga_im, gbb_re_t, gbb_im_t)
    mine = [g_a_re, g_a_im, g_log_dt, g_b_re_t, g_b_im_t, gc_re, -gc_im,
            d_skip, d_ln1_g, d_ln1_b, d_ln2_g, d_ln2_b]
    small_packed = _pack_rows(mine + [loss_lanes])

    w_in_contrib, small_partly = _weight_grad_rows(x2, d_proj, core, "wgrad_w_in_first", 896, 0, 6,
                                                   _gather_first_level([small_packed]))
    w_in_contrib, from_sibling, every = _weight_grad_rows(
        x2, d_proj, core, "wgrad_w_in_rest", 896, 6, 2,
        _sibling_swap_ride([w_in_contrib], halves=False) + _gather_second_level([small_partly]), in_place=True)
    w_in_sum, = _pair_sums([w_in_contrib], [from_sibling], core, "pair_sums_w_in")
    send_sems, recv_sems, w_in_sum, landing, token = _chip_swap_start([w_in_sum], "w_in_chip_swap_start")

    def adamw_of(k):
        taken = (lambda a: a.T) if k in narrow else (lambda a: a)
        return taken(local[k]), taken(given["m_" + k][0]), taken(given["v_" + k][0]), parts[k], own_sums.get(k), k in narrow

    others = [k for k in sharded if k != "w_in"]
    updated = _adamw_many([adamw_of(k) for k in others], "adamw_others", _after(token))
    grad_x, = _grad_x(d_proj, wt["w_in"], dr1, _after(token))

    def held(k, a):
        return a.transpose(0, 1, 3, 2) if k in ("ssm_b_re", "ssm_b_im") else a

    *small_grads, loss_sum = _unpack_rows(_sum_devices(every), [held(k, given[k]).shape for k in SMALL] + [(1, 128)])
    small = _adamw_replicated([held(k, given[k]) for k in SMALL], [held(k, given["m_" + k]) for k in SMALL],
                              [held(k, given["v_" + k]) for k in SMALL], small_grads)
    loss = loss_sum[0, 0]

    (own_sums["w_in"],), (parts["w_in"],) = _chip_swap_wait(
        send_sems, recv_sems, w_in_sum, landing, [grad_x, updated[0], small[0][0]], "w_in_chip_swap_wait")
    updated += _adamw_many([adamw_of("w_in")], "adamw_w_in")

    grads, deltas, new_m, new_v = {}, {}, {}, {}
    for i, k in enumerate(others + ["w_in"]):
        out = [o.T if k in narrow else o for o in updated[4 * i:4 * i + 4]]
        grads[k], deltas[k], new_m[k], new_v[k] = (o.reshape((1,) + local[k].shape) for o in out)
    for res, values in zip((grads, deltas, new_m, new_v), (small_grads,) + small):
        res.update((k, held(k, a)) for k, a in zip(SMALL, values))

    order = ("w_in", "b_gate", "w_attn_br", "w_ssm_br", "w_out", "ssm_a_re", "ssm_a_im", "ssm_log_dt", "ssm_b_re", "ssm_b_im",
             "ssm_c_re", "ssm_c_im", "ssm_d", "w_glu", "ln1_g", "ln1_b", "w_ff_gate", "w_ff_up", "w_ff_down", "ln2_g", "ln2_b")
    return (loss, grad_x[None], *[grads[k] for k in order], *[deltas[k] for k in order], *[new_m[k] for k in order],
            *[new_v[k] for k in order])
```

```python
import functools
import math

import jax
import jax.numpy as jnp
import numpy as np
from jax import lax
from jax.experimental import pallas as pl
from jax.experimental.pallas import tpu as pltpu

F32 = jnp.float32
BF16 = jnp.bfloat16

N_DEV = 8
SEQ = 2048
D_MODEL = 1024
HEAD_DIM = 64
ATTN_WIDTH = 512
QKV_WIDTH = 1536
SSM_WIDTH = 512
SSM_GROUPS = 32
SSM_GROUP = 16
SSM_STATE = 64
IN_WIDTH = 7168
D_FF = 2816
FF_SHARD = D_FF // N_DEV
FF_PAD = 384
D_FF_PAD = FF_PAD * N_DEV
DN_ALPHA = 2.0 ** 0.25
LN_EPS = 1e-5
NEG_INF = -1e30
ROPE_THETA = 10000.0
BLOCK = 128
GROUPS = ((1, 16), (4, 4), (16, 1))

ADAM_LR = 0.001
ADAM_B1 = 0.9
ADAM_B2 = 0.999
ADAM_EPS = 1e-08
ADAM_WD = 0.01
ADAM_STEP = 10

VMEM_LIMIT = 56 * 1024 * 1024


_pallas_call = pl.pallas_call


def _cparams(**kw):
    return pltpu.CompilerParams(vmem_limit_bytes=VMEM_LIMIT, **kw)


def _dot(a, b):
    return jnp.dot(a, b, preferred_element_type=F32)


def _dot_nt(a, b):
    return lax.dot_general(a, b, (((1,), (1,)), ((), ())), preferred_element_type=F32)


def _side_by_side(w_ref, row=None):
    rows = slice(None) if row is None else pl.ds(row, 1)
    return jnp.concatenate([w_ref[i, rows, :] for i in range(w_ref.shape[0])], axis=1)


def _dot_tn(a, b):
    return lax.dot_general(a, b, (((0,), (0,)), ((), ())), preferred_element_type=F32)


def _rope_tables():
    half = HEAD_DIM // 2
    inv_freq = np.float32(ROPE_THETA) ** (-np.arange(half, dtype=np.float32) / np.float32(half))
    ang = np.arange(SEQ, dtype=np.float32)[:, None] * inv_freq[None, :]
    cos, sin = np.cos(ang).astype(np.float32), np.sin(ang).astype(np.float32)
    tables = np.tile(cos, (1, 4)), np.tile(np.concatenate([-sin, sin], axis=1), (1, 2))

    def by_phase(t):
        return np.stack([t.reshape(SEQ // d, d, 128).transpose(1, 0, 2).reshape(SEQ, 128) for d, _ in GROUPS])

    return jnp.asarray(by_phase(tables[0])), jnp.asarray(by_phase(tables[1]))


def _swap_halves(x):
    lane = lax.broadcasted_iota(jnp.int32, x.shape, 1)
    return jnp.where((lane & 63) < 32, pltpu.roll(x, 96, axis=1), pltpu.roll(x, 32, axis=1))


def _group_rows(d, nb, r, i):
    src = pl.ds(i * BLOCK, BLOCK) if d == 1 else pl.ds(r + i * BLOCK * d, BLOCK, stride=d)
    return src, pl.ds((r * nb + i) * BLOCK, BLOCK)


def _attn_masks():
    a_idx = lax.broadcasted_iota(jnp.int32, (2 * BLOCK, 2 * BLOCK), 0) & (BLOCK - 1)
    c_idx = lax.broadcasted_iota(jnp.int32, (2 * BLOCK, 2 * BLOCK), 1)
    cur_ok = jnp.logical_and(c_idx >= BLOCK, c_idx - BLOCK <= a_idx)
    prev_ok = jnp.logical_and(c_idx < BLOCK, c_idx >= a_idx)
    lane = lax.broadcasted_iota(jnp.int32, (BLOCK, 128), 1)
    return cur_ok, prev_ok, lane < HEAD_DIM


def _stack_heads(t, head0):
    zero = jnp.zeros_like(t)
    return jnp.concatenate([jnp.where(head0, t, zero), jnp.where(head0, zero, t)], axis=0)


def _unstack_heads(t2, head0):
    return jnp.where(head0, t2[:BLOCK], t2[BLOCK:])


def _attn_fwd(proj, cos_t, sin_t, ride=None):
    def body(q0, q1, q2, k0, k1, k2, v0, v1, v2, cos_ref, sin_ref, attn_ref, lse_ref, qpm_ref, kpm_ref, vpm_ref,
             qs, ks, vs, os_, ms, ls, acc, mnat, lnat):
        cur_ok, prev_ok, head0 = _attn_masks()
        ks[:BLOCK, :] = jnp.zeros((BLOCK, 128), BF16)
        vs[:BLOCK, :] = jnp.zeros((BLOCK, 128), BF16)
        for g, (d, nb) in enumerate(GROUPS):
            q_ref, k_ref, v_ref = (q0, q1, q2)[g], (k0, k1, k2)[g], (v0, v1, v2)[g]
            for r in range(d):
                for i in range(nb):
                    src, dst = _group_rows(d, nb, r, i)
                    below = pl.ds(dst.start + BLOCK, BLOCK)
                    c, s = cos_ref[g, dst, :], sin_ref[g, dst, :]
                    q = q_ref[src, :]
                    k = k_ref[src, :]
                    qs[dst, :] = ((q * c + _swap_halves(q) * s) * 0.125).astype(BF16)
                    ks[below, :] = (k * c + _swap_halves(k) * s).astype(BF16)
                    vs[below, :] = v_ref[src, :].astype(BF16)
                    qpm_ref[g, dst, :], kpm_ref[g, dst, :], vpm_ref[g, dst, :] = qs[dst, :], ks[below, :], vs[below, :]

            def block(b, carry, nb=nb):
                has_prev = (b & (nb - 1)) > 0
                cur = pl.ds(pl.multiple_of(b * BLOCK, BLOCK), BLOCK)
                window = pl.ds(pl.multiple_of(b * BLOCK, BLOCK), 2 * BLOCK)
                valid = jnp.logical_or(cur_ok, jnp.logical_and(prev_ok, has_prev))
                s = jnp.where(valid, _dot_nt(_stack_heads(qs[cur, :], head0), ks[window, :]), NEG_INF)
                m = jnp.max(s, axis=1, keepdims=True)
                p = jnp.exp(s - m)
                os_[cur, :] = _unstack_heads(_dot(p.astype(BF16), vs[window, :]), head0)
                ms[cur, :] = _unstack_heads(m, head0)
                ls[cur, :] = _unstack_heads(jnp.sum(p, axis=1, keepdims=True), head0)
                return carry

            lax.fori_loop(0, SEQ // BLOCK, block, 0, unroll=16)

            for r in range(d):
                for i in range(nb):
                    src, dst = _group_rows(d, nb, r, i)
                    if g == 0:
                        acc[src, :], mnat[src, :], lnat[src, :] = os_[dst, :], ms[dst, :], ls[dst, :]
                    else:
                        m_old, m_g = mnat[src, :], ms[dst, :]
                        m_new = jnp.maximum(m_old, m_g)
                        a_old, a_g = jnp.exp(m_old - m_new), jnp.exp(m_g - m_new)
                        acc[src, :] = a_old * acc[src, :] + a_g * os_[dst, :]
                        lnat[src, :] = a_old * lnat[src, :] + a_g * ls[dst, :]
                        mnat[src, :] = m_new
        for i in range(SEQ // BLOCK):
            rows = pl.ds(i * BLOCK, BLOCK)
            l = lnat[rows, :]
            attn_ref[rows, :] = acc[rows, :] / l
            lse_ref[rows, :] = mnat[rows, :] + jnp.log(l)

    def col(base):
        return pl.BlockSpec((SEQ, 128), lambda hp, base=base: (0, base + hp))

    in_specs = [col(g * 4) for g in range(3)] + [col(12 + g * 4) for g in range(3)] + [col(24 + g * 4) for g in range(3)]
    table = pl.BlockSpec((3, SEQ, 128), lambda hp: (0, 0, 0), pipeline_mode=pl.Buffered(1))
    out = pl.BlockSpec((SEQ, 128), lambda hp: (0, hp))
    by_phase = pl.BlockSpec((3, SEQ, 128), lambda hp: (0, 0, hp))
    return _call(
        body, "attn_fwd", (4,), in_specs + [table, table], [out, out] + [by_phase] * 3,
        [_sds((SEQ, ATTN_WIDTH), F32), _sds((SEQ, ATTN_WIDTH), F32)] + [_sds((3, SEQ, ATTN_WIDTH), BF16)] * 3,
        [pltpu.VMEM((SEQ, 128), BF16)] + [pltpu.VMEM((SEQ + BLOCK, 128), BF16)] * 2 + [pltpu.VMEM((SEQ, 128), F32)] * 6,
        [proj] * 9 + [cos_t, sin_t], ride)


def _attn_bwd_group_body(g):
    d, nb = GROUPS[g]

    def body(qs_ref, ks_ref, vs_ref, cos_ref, sin_ref, lse_ref, dattn_ref, dsum_ref, dproj_ref,
             ks, vs, dos, lss, dss, dqs, dks, dvs, stage, outs, sems):
        cur_ok, prev_ok, head0 = _attn_masks()
        qs = qs_ref.at[g]
        ks[:BLOCK, :] = jnp.zeros((BLOCK, 128), BF16)
        vs[:BLOCK, :] = jnp.zeros((BLOCK, 128), BF16)
        dks[:BLOCK, :] = jnp.zeros((BLOCK, 128), F32)
        dvs[:BLOCK, :] = jnp.zeros((BLOCK, 128), F32)
        for r in range(d):
            for i in range(nb):
                src, dst = _group_rows(d, nb, r, i)
                below = pl.ds(dst.start + BLOCK, BLOCK)
                ks[below, :] = ks_ref[g, dst, :]
                vs[below, :] = vs_ref[g, dst, :]
                dos[dst, :] = dattn_ref[src, :].astype(BF16)
                for per_head, spread in ((dsum_ref[src, :], dss), (lse_ref[src, :], lss)):
                    other = pltpu.roll(per_head, HEAD_DIM, axis=1)
                    spread[0, dst, :] = jnp.where(head0, per_head, other)
                    spread[1, dst, :] = jnp.where(head0, other, per_head)
                dks[below, :] = jnp.zeros((BLOCK, 128), F32)
                dvs[below, :] = jnp.zeros((BLOCK, 128), F32)

        def per_stacked_row(spread, cur):
            h0, h1 = spread[0, cur, :], spread[1, cur, :]
            return jnp.concatenate([jnp.concatenate([h0, h0], axis=1), jnp.concatenate([h1, h1], axis=1)], axis=0)

        def block(b, carry):
            has_prev = (b & (nb - 1)) > 0
            cur = pl.ds(pl.multiple_of(b * BLOCK, BLOCK), BLOCK)
            window = pl.ds(pl.multiple_of(b * BLOCK, BLOCK), 2 * BLOCK)
            valid = jnp.logical_or(cur_ok, jnp.logical_and(prev_ok, has_prev))
            q2, do2 = _stack_heads(qs[cur, :], head0), _stack_heads(dos[cur, :], head0)
            kw, vw = ks[window, :], vs[window, :]
            s = jnp.where(valid, _dot_nt(q2, kw), NEG_INF)
            p = jnp.exp(s - per_stacked_row(lss, cur))
            ds = (p * (_dot_nt(do2, vw) - per_stacked_row(dss, cur))).astype(BF16)
            dvs[window, :] += _dot_tn(p.astype(BF16), do2)
            dks[window, :] += _dot_tn(ds, q2)
            dqs[cur, :] = _unstack_heads(_dot(ds, kw), head0)
            return carry

        lax.fori_loop(0, SEQ // BLOCK, block, 0, unroll=16)

        hp = pl.program_id(0)
        copies = []
        for kind in range(3):
            for r in range(d):
                for i in range(nb):
                    src, dst = _group_rows(d, nb, r, i)
                    below = pl.ds(dst.start + BLOCK, BLOCK)
                    if kind == 2:
                        stage[src, :] = dvs[below, :]
                    else:
                        c, s = cos_ref[g, dst, :], sin_ref[g, dst, :]
                        t = dqs[dst, :] * 0.125 if kind == 0 else dks[below, :]
                        stage[src, :] = t * c - _swap_halves(t) * s
            for i in range(SEQ // MM_ROWS):
                rows = pl.ds(i * MM_ROWS, MM_ROWS)
                outs[kind, rows, :] = stage[rows, :].astype(BF16)
            column = pl.multiple_of((kind * 12 + g * 4 + hp) * 128, 128)
            copies.append(pltpu.make_async_copy(outs.at[kind], dproj_ref.at[:, pl.ds(column, 128)], sems.at[kind]))
            copies[-1].start()
        for cp in copies:
            cp.wait()

    return body


def _attn_bwd(q_pm, k_pm, v_pm, cos_t, sin_t, attn, lse, dattn, dproj, ride=None):
    groups = [_attn_bwd_group_body(g) for g in range(3)]

    def body(qs_ref, ks_ref, vs_ref, cos_ref, sin_ref, attn_ref, lse_ref, dattn_ref, dproj_in, dproj_ref, dsum, *scratch):
        del dproj_in
        head0 = _attn_masks()[2]
        for i in range(SEQ // BLOCK):
            rows = pl.ds(i * BLOCK, BLOCK)
            prod = dattn_ref[rows, :] * attn_ref[rows, :]
            d0 = jnp.sum(jnp.where(head0, prod, 0.0), axis=1, keepdims=True)
            d1 = jnp.sum(jnp.where(head0, 0.0, prod), axis=1, keepdims=True)
            dsum[rows, :] = jnp.where(head0, d0, d1)
        for g in range(3):
            groups[g](qs_ref, ks_ref, vs_ref, cos_ref, sin_ref, lse_ref, dattn_ref, dsum, dproj_ref, *scratch)

    def col(base):
        return pl.BlockSpec((SEQ, 128), lambda hp, base=base: (0, base + hp))

    table = pl.BlockSpec((3, SEQ, 128), lambda hp: (0, 0, 0), pipeline_mode=pl.Buffered(1))
    by_phase = pl.BlockSpec((3, SEQ, 128), lambda hp: (0, 0, hp))
    return _call(
        body, "attn_bwd", (4,), [by_phase] * 3 + [table, table, col(0), col(0), col(0), ANY],
        [ANY], [_sds((SEQ, IN_WIDTH), BF16)],
        [pltpu.VMEM((SEQ, 128), F32)]
        + [pltpu.VMEM((SEQ + BLOCK, 128), BF16)] * 2 + [pltpu.VMEM((SEQ, 128), BF16)]
        + [pltpu.VMEM((2, SEQ, 128), F32)] * 2 + [pltpu.VMEM((SEQ, 128), F32)]
        + [pltpu.VMEM((SEQ + BLOCK, 128), F32)] * 2 + [pltpu.VMEM((SEQ, 128), F32)]
        + [pltpu.VMEM((3, SEQ, 128), BF16), pltpu.SemaphoreType.DMA((3,))],
        [q_pm, k_pm, v_pm, cos_t, sin_t, attn, lse, dattn, dproj], ride, aliases={8: 0})


SSM_CHUNKS = 4
CHUNK_STATES = 512
SCAN_ROWS = 8
U_COL = (3 * QKV_WIDTH) // 128


def _cmul(xr, xi, yr, yi):
    return xr * yr - xi * yi, xr * yi + xi * yr


def _ssm_prep(a_re, a_im, log_dt, b_re_t, b_im_t):
    def body(ar_ref, ai_ref, ldt_ref, br_ref, bi_ref, abr_ref, abi_ref, er_ref, ei_ref, bbr_ref, bbi_ref):
        ar, ai = ar_ref[...], ai_ref[...]
        dt = jnp.exp(ldt_ref[...])
        mag = jnp.exp(ar * dt)
        abr, abi = mag * jnp.cos(ai * dt), mag * jnp.sin(ai * dt)
        den = ar * ar + ai * ai
        nr, ni = abr - 1.0, abi
        er, ei = (nr * ar + ni * ai) / den, (ni * ar - nr * ai) / den
        abr_ref[...], abi_ref[...], er_ref[...], ei_ref[...] = abr, abi, er, ei
        er3, ei3 = er[:, None, :], ei[:, None, :]
        br, bi = br_ref[...], bi_ref[...]
        bbr_ref[...] = er3 * br - ei3 * bi
        bbi_ref[...] = er3 * bi + ei3 * br

    gp = jax.ShapeDtypeStruct(a_re.shape, F32)
    gb = jax.ShapeDtypeStruct(b_re_t.shape, F32)
    return _pallas_call(body, name="ssm_prep", out_shape=(gp, gp, gp, gp, gb, gb))(a_re, a_im, log_dt, b_re_t, b_im_t)


def _ssm_param_bwd(a_re, a_im, log_dt, b_re_t, b_im_t, abar_re, abar_im, e_re, e_im, ga_re, ga_im, gbb_re_t, gbb_im_t):
    def body(ar_ref, ai_ref, ldt_ref, br_ref, bi_ref, abr_ref, abi_ref, er_ref, ei_ref, gar_ref, gai_ref, gbr_ref, gbi_ref,
             o_ar, o_ai, o_ldt, o_br, o_bi):
        ar, ai = ar_ref[...], ai_ref[...]
        dt = jnp.exp(ldt_ref[...])
        er, ei = er_ref[...], ei_ref[...]
        br, bi, gbr, gbi = br_ref[...], bi_ref[...], gbr_ref[...], gbi_ref[...]
        er3, ei3 = er[:, None, :], ei[:, None, :]
        o_br[...] = er3 * gbr + ei3 * gbi
        o_bi[...] = er3 * gbi - ei3 * gbr
        ge_r = jnp.sum(br * gbr + bi * gbi, axis=1)
        ge_i = jnp.sum(br * gbi - bi * gbr, axis=1)
        den = ar * ar + ai * ai
        ilr, ili = ar / den, -ai / den
        t_r, t_i = _cmul(ilr, -ili, ge_r, ge_i)
        gab_r, gab_i = gar_ref[...] + t_r, gai_ref[...] + t_i
        gz_r, gz_i = _cmul(abr_ref[...], -abi_ref[...], gab_r, gab_i)
        el_r, el_i = _cmul(er, ei, ilr, ili)
        u_r, u_i = _cmul(el_r, -el_i, ge_r, ge_i)
        o_ar[...] = dt * gz_r - u_r
        o_ai[...] = dt * gz_i - u_i
        o_ldt[...] = jnp.sum(gz_r * ar + gz_i * ai, axis=1, keepdims=True) * dt

    gp = jax.ShapeDtypeStruct(a_re.shape, F32)
    gb = jax.ShapeDtypeStruct(b_re_t.shape, F32)
    return _pallas_call(body, name="ssm_param_bwd", out_shape=(gp, gp, jax.ShapeDtypeStruct(log_dt.shape, F32), gb, gb))(
        a_re, a_im, log_dt, b_re_t, b_im_t, abar_re, abar_im, e_re, e_im, ga_re, ga_im, gbb_re_t, gbb_im_t)


def _block_diag(blocks_re, blocks_im, sign_im, rows_are_channels):
    both = jnp.stack([blocks_re, sign_im * blocks_im]).reshape(2, SSM_CHUNKS, 8, SSM_GROUP, SSM_STATE)
    eye = jnp.eye(8, dtype=F32)
    if rows_are_channels:
        return jnp.einsum("rcghp,gk->cghrkp", both, eye).reshape(SSM_CHUNKS, 128, 2 * CHUNK_STATES)
    return jnp.einsum("rcghp,gk->crkpgh", both, eye).reshape(SSM_CHUNKS, 2 * CHUNK_STATES, 128)


def _diagonal_blocks(mat, part):
    first = [part * CHUNK_STATES + SSM_STATE * g for g in range(8)]
    return jnp.concatenate([mat[SSM_GROUP * g:SSM_GROUP * (g + 1), first[g]:first[g] + SSM_STATE] for g in range(8)], axis=0)


def _scan_consts(a_ref, conj, reverse):
    ar = jnp.broadcast_to(a_ref[:, :CHUNK_STATES], (SCAN_ROWS, CHUNK_STATES))
    ai = jnp.broadcast_to(a_ref[:, CHUNK_STATES:], (SCAN_ROWS, CHUNK_STATES))
    if conj:
        ai = -ai
    row = lax.broadcasted_iota(jnp.int32, (SCAN_ROWS, CHUNK_STATES), 0)
    if reverse:
        row = SCAN_ROWS - 1 - row
    zero = jnp.zeros_like(ar)
    steps = []
    pr, pi = ar, ai
    for shift in (1, 2, 4):
        keep = row >= shift
        steps.append((SCAN_ROWS - shift if reverse else shift, jnp.where(keep, pr, zero), jnp.where(keep, pi, zero)))
        pr, pi = _cmul(pr, pi, pr, pi)
    first = row == 0
    return steps, (jnp.where(first, ar, zero), jnp.where(first, ai, zero)), first


def _scan_tile(xr, xi, prev_r, prev_i, steps, carry_in, reverse):
    edge = SCAN_ROWS - 1 if reverse else 1
    cr, ci = pltpu.roll(prev_r, edge, axis=0), pltpu.roll(prev_i, edge, axis=0)
    xr, xi = xr + carry_in[0] * cr - carry_in[1] * ci, xi + carry_in[0] * ci + carry_in[1] * cr
    for shift, mr, mi in steps:
        sr, si = pltpu.roll(xr, shift, axis=0), pltpu.roll(xi, shift, axis=0)
        xr, xi = xr + mr * sr - mi * si, xi + mr * si + mi * sr
    return xr, xi


MM_ROWS = 256


def _ssm_fwd(proj, bmat, cmat, a_chunks, d_skip, ride=None):
    def body(u_ref, b_ref, c_ref, a_ref, d_ref, y_ref, states_ref, h_ref):
        for i in range(SEQ // MM_ROWS):
            rows = pl.ds(i * MM_ROWS, MM_ROWS)
            h_ref[rows, :] = _dot(u_ref[rows, :].astype(BF16), b_ref[...])
        steps, carry_in, _ = _scan_consts(a_ref, conj=False, reverse=False)

        def tile(k, carry):
            rows = pl.ds(pl.multiple_of(k * SCAN_ROWS, SCAN_ROWS), SCAN_ROWS)
            xr, xi = _scan_tile(h_ref[rows, :CHUNK_STATES], h_ref[rows, CHUNK_STATES:], carry[0], carry[1], steps, carry_in, False)
            h_ref[rows, :CHUNK_STATES] = xr
            h_ref[rows, CHUNK_STATES:] = xi
            return xr, xi

        zero = jnp.zeros((SCAN_ROWS, CHUNK_STATES), F32)
        lax.fori_loop(0, SEQ // SCAN_ROWS, tile, (zero, zero), unroll=4)
        for i in range(SEQ // MM_ROWS):
            rows = pl.ds(i * MM_ROWS, MM_ROWS)
            states = h_ref[rows, :].astype(BF16)
            states_ref[rows, :] = states
            y_ref[rows, :] = _dot(states, c_ref[...]) + d_ref[...] * u_ref[rows, :]

    return _call(
        body, "ssm_fwd", (SSM_CHUNKS,),
        [pl.BlockSpec((SEQ, 128), lambda c: (0, U_COL + c)),
         pl.BlockSpec((None, 128, 2 * CHUNK_STATES), lambda c: (c, 0, 0)),
         pl.BlockSpec((None, 2 * CHUNK_STATES, 128), lambda c: (c, 0, 0)),
         pl.BlockSpec((None, 1, 2 * CHUNK_STATES), lambda c: (c, 0, 0)),
         pl.BlockSpec((1, 128), lambda c: (0, c))],
        [pl.BlockSpec((SEQ, 128), lambda c: (0, c)), pl.BlockSpec((SEQ, 2 * CHUNK_STATES), lambda c: (0, c))],
        [_sds((SEQ, SSM_WIDTH), F32), _sds((SEQ, SSM_CHUNKS * 2 * CHUNK_STATES), BF16)],
        [pltpu.VMEM((SEQ, 2 * CHUNK_STATES), F32)],
        [proj, bmat, cmat, a_chunks, d_skip], ride)


def _ssm_bwd(dys, proj, h, bmat, cmat, a_chunks, d_skip, dproj, ride=None):
    def body(dy_ref, u_ref, states_ref, b_ref, c_ref, a_ref, d_ref, dproj_in, du_ref, db_re_ref, db_im_ref, dc_re_ref, dc_im_ref,
             da_ref, dd_ref, g_ref, h_ref):
        del dproj_in
        dsum = jnp.zeros((1, 128), F32)
        dcm = jnp.zeros((128, 2 * CHUNK_STATES), F32)
        for i in range(SEQ // MM_ROWS):
            rows = pl.ds(i * MM_ROWS, MM_ROWS)
            h_ref[rows, :] = states_ref[rows, :].astype(F32)
            dy = dy_ref[rows, :]
            g_ref[rows, :] = _dot_nt(dy.astype(BF16), c_ref[...])
            dsum += jnp.sum(dy * u_ref[rows, :], axis=0, keepdims=True)
            dcm += _dot_tn(dy.astype(BF16), states_ref[rows, :])
        dd_ref[...] = dsum
        dc_re_ref[...] = _diagonal_blocks(dcm, 0)
        dc_im_ref[...] = _diagonal_blocks(dcm, 1)
        steps, carry_in, _ = _scan_consts(a_ref, conj=True, reverse=True)
        first_row = lax.broadcasted_iota(jnp.int32, (SCAN_ROWS, CHUNK_STATES), 0) == 0
        n_tiles = SEQ // SCAN_ROWS

        def tile(j, carry):
            k = n_tiles - 1 - j
            rows = pl.ds(pl.multiple_of(k * SCAN_ROWS, SCAN_ROWS), SCAN_ROWS)
            before = pl.ds(pl.multiple_of(jnp.maximum(k - 1, 0) * SCAN_ROWS, SCAN_ROWS), SCAN_ROWS)
            gr, gi = _scan_tile(g_ref[rows, :CHUNK_STATES], g_ref[rows, CHUNK_STATES:], carry[0], carry[1], steps, carry_in, True)
            g_ref[rows, :CHUNK_STATES] = gr
            g_ref[rows, CHUNK_STATES:] = gi
            has_before = jnp.where(k > 0, 1.0, 0.0)
            hr = jnp.where(first_row, pltpu.roll(h_ref[before, :CHUNK_STATES], 1, axis=0) * has_before,
                           pltpu.roll(h_ref[rows, :CHUNK_STATES], 1, axis=0))
            hi = jnp.where(first_row, pltpu.roll(h_ref[before, CHUNK_STATES:], 1, axis=0) * has_before,
                           pltpu.roll(h_ref[rows, CHUNK_STATES:], 1, axis=0))
            return gr, gi, carry[2] + hr * gr + hi * gi, carry[3] + hr * gi - hi * gr

        zero = jnp.zeros((SCAN_ROWS, CHUNK_STATES), F32)
        _, _, sar, sai = lax.fori_loop(0, n_tiles, tile, (zero, zero, zero, zero), unroll=4)
        da_ref[:, :CHUNK_STATES] = jnp.sum(sar, axis=0, keepdims=True)
        da_ref[:, CHUNK_STATES:] = jnp.sum(sai, axis=0, keepdims=True)
        dbm = jnp.zeros((128, 2 * CHUNK_STATES), F32)
        for i in range(SEQ // MM_ROWS):
            rows = pl.ds(i * MM_ROWS, MM_ROWS)
            g = g_ref[rows, :].astype(BF16)
            du_ref[rows, :] = (_dot_nt(g, b_ref[...]) + d_ref[...] * dy_ref[rows, :]).astype(BF16)
            dbm += _dot_tn(u_ref[rows, :].astype(BF16), g)
        db_re_ref[...] = _diagonal_blocks(dbm, 0)
        db_im_ref[...] = _diagonal_blocks(dbm, 1)

    chunk_col = pl.BlockSpec((SEQ, 128), lambda c: (0, c))
    blocks = pl.BlockSpec((None, 128, SSM_STATE), lambda c: (c, 0, 0))
    return _call(
        body, "ssm_bwd", (SSM_CHUNKS,),
        [chunk_col,
         pl.BlockSpec((SEQ, 128), lambda c: (0, U_COL + c)),
         pl.BlockSpec((SEQ, 2 * CHUNK_STATES), lambda c: (0, c)),
         pl.BlockSpec((None, 128, 2 * CHUNK_STATES), lambda c: (c, 0, 0)),
         pl.BlockSpec((None, 2 * CHUNK_STATES, 128), lambda c: (c, 0, 0)),
         pl.BlockSpec((None, 1, 2 * CHUNK_STATES), lambda c: (c, 0, 0)),
         pl.BlockSpec((1, 128), lambda c: (0, c)), ANY],
        [pl.BlockSpec((SEQ, 128), lambda c: (0, U_COL + c)), blocks, blocks, blocks, blocks,
         pl.BlockSpec((None, 1, 2 * CHUNK_STATES), lambda c: (c, 0, 0)),
         pl.BlockSpec((1, 128), lambda c: (0, c))],
        [_sds((SEQ, IN_WIDTH), BF16)] + [_sds((SSM_CHUNKS, 128, SSM_STATE), F32)] * 4
        + [_sds((SSM_CHUNKS, 1, 2 * CHUNK_STATES), F32), _sds((1, SSM_WIDTH), F32)],
        [pltpu.VMEM((SEQ, 2 * CHUNK_STATES), F32)] * 2, [dys, proj, h, bmat, cmat, a_chunks, d_skip, dproj], ride, aliases={7: 0})


def _ssm_tables(abar_re, abar_im, bbar_re_t, bbar_im_t, c_re, c_im):
    bmat = _block_diag(bbar_re_t, bbar_im_t, 1.0, True).astype(BF16)
    cmat = _block_diag(c_re, c_im, -1.0, False).astype(BF16)
    a_chunks = jnp.concatenate([abar_re.reshape(SSM_CHUNKS, 1, CHUNK_STATES), abar_im.reshape(SSM_CHUNKS, 1, CHUNK_STATES)], axis=2)
    return bmat, cmat, a_chunks


GL_COL = (3 * QKV_WIDTH + SSM_WIDTH) // D_MODEL
GELU_C = math.sqrt(2.0 / math.pi)
GELU_A = 0.044715


def _sds(shape, dtype):
    return jax.ShapeDtypeStruct(shape, dtype)


def _gelu(x):
    t = jnp.tanh(GELU_C * (x + GELU_A * x * x * x))
    return 0.5 * x * (1.0 + t), t


def _gelu_grad(x, t):
    return 0.5 * (1.0 + t) + 0.5 * x * (1.0 - t * t) * GELU_C * (1.0 + 3.0 * GELU_A * x * x)


def _layer_norm(r, g, b):
    mu = jnp.mean(r, axis=-1, keepdims=True)
    xc = r - mu
    rstd = lax.rsqrt(jnp.mean(xc * xc, axis=-1, keepdims=True) + LN_EPS)
    xhat = xc * rstd
    return xhat * g + b, xhat, rstd


def _layer_norm_bwd(dy, xhat, rstd, g):
    dxhat = dy * g
    m1 = jnp.mean(dxhat, axis=-1, keepdims=True)
    m2 = jnp.mean(dxhat * xhat, axis=-1, keepdims=True)
    return rstd * (dxhat - m1 - xhat * m2)


def _proj(x, w_in, ride=None):
    tm, tn = 1024, 1792

    def body(x_ref, w_ref, o_ref):
        o_ref[...] = _dot(x_ref[...].astype(BF16), _side_by_side(w_ref))

    return _call(
        body, "proj", (SEQ // tm, IN_WIDTH // tn),
        [pl.BlockSpec((tm, D_MODEL), lambda i, j: (i, 0)), pl.BlockSpec((2, D_MODEL, tn // 2), lambda i, j: (j, 0, 0))],
        [pl.BlockSpec((tm, tn), lambda i, j: (i, j))], [_sds((SEQ, IN_WIDTH), F32)], [], [x, w_in], ride)


def _row_spec(tm, width, col=0):
    return pl.BlockSpec((tm, width), lambda i, col=col: (i, col))


def _full_spec(shape):
    return pl.BlockSpec(shape, lambda i: (0,) * len(shape))


def _weight_spec(shape):
    return pl.BlockSpec(shape, lambda i: (0,) * len(shape), pipeline_mode=pl.Buffered(1))


def _mixer_out(attn, ys, proj, x, w_ab, w_sb, w_glu, w_out, b_gate, ln_g, ln_b, ride=None):
    tm = 512

    def body(attn_ref, ys_ref, gl0_ref, gl1_ref, x_ref, wab_ref, wsb_ref, wglu_ref, wout_ref, bg_ref, g_ref, b_ref,
             h_ref, xhat_ref, rstd_ref, glu_ref, ya_ref, yssm_ref):
        gy, _ = _gelu(ys_ref[...])
        glu = _dot(gy.astype(BF16), _side_by_side(wglu_ref))
        glu_ref[...] = glu.astype(BF16)
        y_s = glu[:, :SSM_WIDTH] * jax.nn.sigmoid(glu[:, SSM_WIDTH:])
        y_ssm = _dot(y_s.astype(BF16), _side_by_side(wsb_ref))
        y_attn = _dot(attn_ref[...].astype(BF16), _side_by_side(wab_ref))
        ya_ref[...] = y_attn.astype(BF16)
        yssm_ref[...] = y_ssm.astype(BF16)
        g0 = jax.nn.sigmoid(gl0_ref[...] + _side_by_side(bg_ref, 0))
        g1 = jax.nn.sigmoid(gl1_ref[...] + _side_by_side(bg_ref, 1))
        mixed = g0 * y_attn + g1 * y_ssm
        r1 = DN_ALPHA * x_ref[...] + _dot(mixed.astype(BF16), wout_ref[...])
        h, xhat, rstd = _layer_norm(r1, g_ref[...], b_ref[...])
        h_ref[...] = h
        xhat_ref[...] = xhat
        rstd_ref[...] = jnp.broadcast_to(rstd, (tm, 128))

    wide = _sds((SEQ, D_MODEL), F32)
    return _call(
        body, "mixer_out", (SEQ // tm,),
        [_row_spec(tm, ATTN_WIDTH), _row_spec(tm, SSM_WIDTH), _row_spec(tm, D_MODEL, GL_COL), _row_spec(tm, D_MODEL, GL_COL + 1),
         _row_spec(tm, D_MODEL), _weight_spec((N_DEV, ATTN_WIDTH, 128)), _weight_spec((N_DEV, SSM_WIDTH, 128)),
         _weight_spec((N_DEV, SSM_WIDTH, 128)), _weight_spec((D_MODEL, D_MODEL)), _full_spec((N_DEV, 2, 128)),
         _full_spec((1, D_MODEL)), _full_spec((1, D_MODEL))],
        [_row_spec(tm, D_MODEL), _row_spec(tm, D_MODEL), _row_spec(tm, 128), _row_spec(tm, D_MODEL),
         _row_spec(tm, D_MODEL), _row_spec(tm, D_MODEL)],
        [wide, wide, _sds((SEQ, 128), F32)] + [_sds((SEQ, D_MODEL), BF16)] * 3, [],
        [attn, ys, proj, proj, x, w_ab, w_sb, w_glu, w_out, b_gate, ln_g, ln_b], ride)


def _ff_up(h, w_gate, w_up, ride=None):
    tm, tn = 1024, 768

    def body(h_ref, wg_ref, wu_ref, a_ref, b_ref, f_ref):
        hb = h_ref[...].astype(BF16)
        a, b = _dot(hb, _side_by_side(wg_ref)), _dot(hb, _side_by_side(wu_ref))
        a_ref[...] = a.astype(BF16)
        b_ref[...] = b.astype(BF16)
        f_ref[...] = (a * jax.nn.sigmoid(a) * b).astype(BF16)

    tile = pl.BlockSpec((tm, tn), lambda i, j: (i, j))
    wtile = pl.BlockSpec((tn // FF_PAD, D_MODEL, FF_PAD), lambda i, j: (j, 0, 0))
    out = _sds((SEQ, D_FF_PAD), BF16)
    return _call(body, "ff_up", (SEQ // tm, D_FF_PAD // tn), [pl.BlockSpec((tm, D_MODEL), lambda i, j: (i, 0)), wtile, wtile],
                 [tile, tile, tile], [out, out, out], [], [h, w_gate, w_up], ride)


def _ff_down_loss(f, w_down, h, target, ln_g, ln_b):
    tm = 512

    def body(f_ref, w_ref, h_ref, t_ref, g_ref, b_ref, dr_ref, dg_ref, db_ref, loss_ref):
        @pl.when(pl.program_id(0) == 0)
        def _():
            dg_ref[...] = jnp.zeros_like(dg_ref)
            db_ref[...] = jnp.zeros_like(db_ref)
            loss_ref[...] = jnp.zeros_like(loss_ref)

        r2 = DN_ALPHA * h_ref[...] + _dot(f_ref[...], w_ref[...])
        g = g_ref[...]
        out, xhat, rstd = _layer_norm(r2, g, b_ref[...])
        err = out - t_ref[...]
        loss_ref[...] += 0.5 * jnp.sum(jnp.mean(err * err, axis=-1, keepdims=True), axis=0, keepdims=True)
        dout = err * (1.0 / D_MODEL)
        dg_ref[...] += jnp.sum(dout * xhat, axis=0, keepdims=True)
        db_ref[...] += jnp.sum(dout, axis=0, keepdims=True)
        dr_ref[...] = _layer_norm_bwd(dout, xhat, rstd, g)

    vec = _sds((1, D_MODEL), F32)
    return _pallas_call(
        body, name="ff_down_loss", grid=(SEQ // tm,),
        in_specs=[_row_spec(tm, D_FF_PAD), _weight_spec((D_FF_PAD, D_MODEL)), _row_spec(tm, D_MODEL), _row_spec(tm, D_MODEL),
                  _full_spec((1, D_MODEL)), _full_spec((1, D_MODEL))],
        out_specs=(_row_spec(tm, D_MODEL), _full_spec((1, D_MODEL)), _full_spec((1, D_MODEL)), _full_spec((1, 128))),
        out_shape=(_sds((SEQ, D_MODEL), F32), vec, vec, _sds((1, 128), F32)),
        compiler_params=_cparams(dimension_semantics=("arbitrary",)),
    )(f, w_down, h, target, ln_g, ln_b)


def _ff_down_bwd(dr2, w_down, a, b):
    tm, tn = 1024, 768

    def body(dr_ref, w_ref, a_ref, b_ref, da_ref, db_ref):
        df = _dot_nt(dr_ref[...].astype(BF16), w_ref[...])
        av, bv = a_ref[...].astype(F32), b_ref[...].astype(F32)
        sg = jax.nn.sigmoid(av)
        da_ref[...] = (df * bv * sg * (1.0 + av * (1.0 - sg))).astype(BF16)
        db_ref[...] = (df * av * sg).astype(BF16)

    tile = pl.BlockSpec((tm, tn), lambda i, j: (i, j))
    out = _sds((SEQ, D_FF_PAD), BF16)
    return _pallas_call(
        body, name="ff_down_bwd", grid=(SEQ // tm, D_FF_PAD // tn),
        in_specs=[pl.BlockSpec((tm, D_MODEL), lambda i, j: (i, 0)), pl.BlockSpec((tn, D_MODEL), lambda i, j: (j, 0)), tile, tile],
        out_specs=(tile, tile), out_shape=(out, out),
        compiler_params=_cparams(dimension_semantics=("arbitrary", "arbitrary")),
    )(dr2, w_down, a, b)


def _ff_up_bwd(da, db, w_gate, w_up, dr2, xhat1, rstd1, ln_g, ride=None):
    tm, tk = 1024, 768
    nk = D_FF_PAD // tk

    def body(da_ref, db_ref, wg_ref, wu_ref, dr2_ref, xhat_ref, rstd_ref, g_ref, dr1_ref, dg_ref, dbias_ref, acc):
        i, k = pl.program_id(0), pl.program_id(1)

        @pl.when(jnp.logical_and(i == 0, k == 0))
        def _():
            dg_ref[...] = jnp.zeros_like(dg_ref)
            dbias_ref[...] = jnp.zeros_like(dbias_ref)

        part = _dot_nt(da_ref[...], _side_by_side(wg_ref)) + _dot_nt(db_ref[...], _side_by_side(wu_ref))

        @pl.when(k == 0)
        def _():
            acc[...] = part

        @pl.when(k > 0)
        def _():
            acc[...] += part

        @pl.when(k == nk - 1)
        def _():
            dh = DN_ALPHA * dr2_ref[...] + acc[...]
            xhat = xhat_ref[...]
            dg_ref[...] += jnp.sum(dh * xhat, axis=0, keepdims=True)
            dbias_ref[...] += jnp.sum(dh, axis=0, keepdims=True)
            rstd = jnp.max(rstd_ref[...], axis=1, keepdims=True)
            dr1_ref[...] = _layer_norm_bwd(dh, xhat, rstd, g_ref[...])

    hid = pl.BlockSpec((tm, tk), lambda i, k: (i, k))
    wtile = pl.BlockSpec((tk // FF_PAD, D_MODEL, FF_PAD), lambda i, k: (k, 0, 0))
    row = pl.BlockSpec((tm, D_MODEL), lambda i, k: (i, 0))
    vec = pl.BlockSpec((1, D_MODEL), lambda i, k: (0, 0))
    return _call(
        body, "ff_up_bwd", (SEQ // tm, nk),
        [hid, hid, wtile, wtile, row, row, pl.BlockSpec((tm, 128), lambda i, k: (i, 0)), vec],
        [row, vec, vec], [_sds((SEQ, D_MODEL), F32), _sds((1, D_MODEL), F32), _sds((1, D_MODEL), F32)],
        [pltpu.VMEM((tm, D_MODEL), F32)], [da, db, w_gate, w_up, dr2, xhat1, rstd1, ln_g], ride)


def _mixer_bwd(dr1, proj, y_attn, y_ssm, glu, ys, w_ab, w_sb, w_glu, w_out, b_gate):
    tm = 256

    def body(dr1_ref, gl0_ref, gl1_ref, ya_ref, yssm_ref, glu_ref, ys_ref, wab_ref, wsb_ref, wglu_ref, wout_ref, bg_ref,
             dya_ref, dyssm_ref, dgl_ref, dattn_ref, dglu_ref, dys_ref, mixed_ref, ysb_ref, gy_ref, dbg_ref, stage, copied):
        @pl.when(pl.program_id(0) == 0)
        def _():
            dbg_ref[...] = jnp.zeros_like(dbg_ref)

        dmixed = _dot_nt(dr1_ref[...].astype(BF16), wout_ref[...])
        g0 = jax.nn.sigmoid(gl0_ref[...] + _side_by_side(bg_ref, 0))
        g1 = jax.nn.sigmoid(gl1_ref[...] + _side_by_side(bg_ref, 1))
        y_attn, y_ssm = ya_ref[...].astype(F32), yssm_ref[...].astype(F32)
        mixed_ref[...] = (g0 * y_attn + g1 * y_ssm).astype(BF16)
        dya = (dmixed * g0).astype(BF16)
        dyssm = (dmixed * g1).astype(BF16)
        dya_ref[...] = dya
        dyssm_ref[...] = dyssm
        dgl0 = dmixed * y_attn * g0 * (1.0 - g0)
        dgl1 = dmixed * y_ssm * g1 * (1.0 - g1)
        i, last = pl.program_id(0), SEQ // tm - 1
        slot = i & 1

        def copy_out(buffer, tile):
            window = dgl_ref.at[pl.ds(pl.multiple_of(tile * tm, tm), tm), pl.ds(GL_COL * D_MODEL, 2 * D_MODEL)]
            return pltpu.make_async_copy(stage.at[buffer], window, copied.at[buffer])

        @pl.when(i >= 2)
        def _():
            copy_out(slot, i - 2).wait()

        stage[slot, :, :D_MODEL] = dgl0.astype(BF16)
        stage[slot, :, D_MODEL:] = dgl1.astype(BF16)
        copy_out(slot, i).start()

        @pl.when(i == last)
        def _():
            copy_out(1 - slot, i - 1).wait()
            copy_out(slot, i).wait()
        dbg_ref[:, :D_MODEL] += jnp.sum(dgl0, axis=0, keepdims=True)
        dbg_ref[:, D_MODEL:] += jnp.sum(dgl1, axis=0, keepdims=True)
        dattn_ref[...] = _dot_nt(dya, _side_by_side(wab_ref))
        dy_s = _dot_nt(dyssm, _side_by_side(wsb_ref))
        glu = glu_ref[...].astype(F32)
        glu1, sg = glu[:, :SSM_WIDTH], jax.nn.sigmoid(glu[:, SSM_WIDTH:])
        ysb_ref[...] = (glu1 * sg).astype(BF16)
        dglu1 = (dy_s * sg).astype(BF16)
        dglu2 = (dy_s * glu1 * sg * (1.0 - sg)).astype(BF16)
        dglu_ref[:, :SSM_WIDTH] = dglu1
        dglu_ref[:, SSM_WIDTH:] = dglu2
        dgy = _dot_nt(jnp.concatenate([dglu1, dglu2], axis=1), _side_by_side(wglu_ref))
        ys = ys_ref[...]
        gy, t = _gelu(ys)
        gy_ref[...] = gy.astype(BF16)
        dys_ref[...] = dgy * _gelu_grad(ys, t)

    wide_b, half_b = _sds((SEQ, D_MODEL), BF16), _sds((SEQ, SSM_WIDTH), BF16)
    half_f = _sds((SEQ, SSM_WIDTH), F32)
    return _pallas_call(
        body, name="mixer_bwd", grid=(SEQ // tm,),
        in_specs=[_row_spec(tm, D_MODEL), _row_spec(tm, D_MODEL, GL_COL), _row_spec(tm, D_MODEL, GL_COL + 1), _row_spec(tm, D_MODEL),
                  _row_spec(tm, D_MODEL), _row_spec(tm, D_MODEL), _row_spec(tm, SSM_WIDTH), _full_spec((N_DEV, ATTN_WIDTH, 128)),
                  _full_spec((N_DEV, SSM_WIDTH, 128)), _full_spec((N_DEV, SSM_WIDTH, 128)), _full_spec((D_MODEL, D_MODEL)),
                  _full_spec((N_DEV, 2, 128))],
        out_specs=(_row_spec(tm, D_MODEL), _row_spec(tm, D_MODEL), ANY, _row_spec(tm, ATTN_WIDTH),
                   _row_spec(tm, D_MODEL), _row_spec(tm, SSM_WIDTH), _row_spec(tm, D_MODEL), _row_spec(tm, SSM_WIDTH),
                   _row_spec(tm, SSM_WIDTH), _full_spec((1, 2 * D_MODEL))),
        out_shape=(wide_b, wide_b, _sds((SEQ, IN_WIDTH), BF16), half_f, wide_b, half_f, wide_b, half_b, half_b,
                   _sds((1, 2 * D_MODEL), F32)),
        scratch_shapes=[pltpu.VMEM((2, tm, 2 * D_MODEL), BF16), pltpu.SemaphoreType.DMA((2,))],
        compiler_params=_cparams(dimension_semantics=("arbitrary",)),
    )(dr1, proj, proj, y_attn, y_ssm, glu, ys, w_ab, w_sb, w_glu, w_out, b_gate)


def _grad_x(dproj, w_in, dr1, ride=None):
    tm, tk = 1024, 1792
    nk = IN_WIDTH // tk

    def body(dp_ref, w_ref, dr1_ref, o_ref, acc):
        k = pl.program_id(1)
        part = _dot_nt(dp_ref[...], _side_by_side(w_ref))

        @pl.when(k == 0)
        def _():
            acc[...] = part

        @pl.when(k > 0)
        def _():
            acc[...] += part

        @pl.when(k == nk - 1)
        def _():
            o_ref[...] = DN_ALPHA * dr1_ref[...] + acc[...]

    row = pl.BlockSpec((tm, D_MODEL), lambda i, k: (i, 0))
    return _call(
        body, "grad_x", (SEQ // tm, nk),
        [pl.BlockSpec((tm, tk), lambda i, k: (i, k)), pl.BlockSpec((2, D_MODEL, tk // 2), lambda i, k: (k, 0, 0)), row],
        [row], [_sds((SEQ, D_MODEL), F32)], [pltpu.VMEM((tm, D_MODEL), F32)], [dproj, w_in, dr1], ride)


def _weight_grad(a, b, name, shard_cols=None):
    k, n = a.shape[1], b.shape[1]
    tk = k if shard_cols else k // N_DEV
    tn = n // 4 if shard_cols else min(n, 1024)

    def body(a_ref, b_ref, o_ref):
        grad = _dot_tn(a_ref[...].astype(BF16), b_ref[...].astype(BF16))
        if shard_cols:
            o_ref[0] = grad[:, :shard_cols].astype(BF16)
            o_ref[1] = grad[:, shard_cols:].astype(BF16)
        else:
            o_ref[...] = grad.astype(BF16)

    if shard_cols:
        out_spec = pl.BlockSpec((2, None, tk, shard_cols), lambda kk, j: (0, j, kk, 0))
        out_shape = _sds((2, 4, k, shard_cols), BF16)
    else:
        out_spec = pl.BlockSpec((None, None, tk, tn), lambda kk, j: (kk % 2, kk // 2, 0, j))
        out_shape = _sds((2, 4, tk, n), BF16)
    return _call(body, name, (k // tk, n // tn),
                 [pl.BlockSpec((SEQ, tk), lambda kk, j: (0, kk)), pl.BlockSpec((SEQ, tn), lambda kk, j: (0, j))],
                 [out_spec], [out_shape], [], [a, b])[0]


def _weight_grad_rows(a, b, core, name, shard_cols, first, count, ride, in_place=False):
    k = a.shape[1]

    def body(a_ref, b_ref, o_ref):
        o_ref[...] = _dot_tn(a_ref[...].astype(BF16), b_ref[...].astype(BF16)).astype(BF16)

    def shard(j, core_ref):
        row = j + first
        return 0, jnp.where(row < 4, 2 * row + 1 - core_ref[0], 2 * (row - 4) + core_ref[0])

    return _call(body, name, (count,),
                 [pl.BlockSpec((SEQ, k), lambda j, core_ref: (0, 0), pipeline_mode=pl.Buffered(1)),
                  pl.BlockSpec((SEQ, shard_cols), shard)],
                 [pl.BlockSpec((None, k, shard_cols), lambda j, core_ref: (j + first, 0, 0))],
                 [_sds((N_DEV, k, shard_cols), BF16)], [], [a, b], ride, aliases={2: 0} if in_place else None, prefetch=core)


MESH = pl.DeviceIdType.MESH
ANY = pl.BlockSpec(memory_space=pl.ANY)


def _place():
    return lax.axis_index("x"), lax.axis_index("y"), lax.axis_index("c")


def _other_chips(x, y):
    return [(1 - x, y), (x, 1 - y), (1 - x, 1 - y)]


class _Ride:
    def __init__(self, operands, results, aliases, sems, start, wait):
        self.operands, self.results, self.aliases, self.sems = list(operands), list(results), dict(aliases), list(sems)
        self.start, self.wait = start, wait

    def __add__(self, other):
        n_in, n_out, n_sem = len(self.operands), len(self.results), len(self.sems)

        def both(which):
            def run(ins, outs, sems):
                getattr(self, which)(ins[:n_in], outs[:n_out], sems[:n_sem])
                getattr(other, which)(ins[n_in:], outs[n_out:], sems[n_sem:])
            return run

        aliases = {**self.aliases, **{n_in + i: n_out + j for i, j in other.aliases.items()}}
        return _Ride(self.operands + other.operands, self.results + other.results, aliases, self.sems + other.sems,
                     both("start"), both("wait"))


def _call(body, name, grid, in_specs, out_specs, out_shape, scratch_shapes, operands, ride=None, aliases=None, prefetch=None):
    in_specs, out_specs, out_shape = list(in_specs), list(out_specs), list(out_shape)
    scratch_shapes, operands, aliases = list(scratch_shapes), list(operands), dict(aliases or {})
    kernel_body = body
    if ride is not None:
        n_in, n_out, n_scr, r_in, r_out = len(in_specs), len(out_specs), len(scratch_shapes), len(ride.operands), len(ride.results)

        def kernel_body(*refs):
            out0, scr0 = n_in + r_in, n_in + r_in + n_out + r_out
            ride_refs = (refs[n_in:out0], refs[out0 + n_out:scr0], refs[scr0 + n_scr:])
            ids = [pl.program_id(i) for i in range(len(grid))]
            first = functools.reduce(jnp.logical_and, [i == 0 for i in ids])
            last = functools.reduce(jnp.logical_and, [i == g - 1 for i, g in zip(ids, grid)])

            @pl.when(first)
            def _():
                ride.start(*ride_refs)

            body(*refs[:n_in], *refs[out0:out0 + n_out], *refs[scr0:scr0 + n_scr])

            @pl.when(last)
            def _():
                ride.wait(*ride_refs)

        aliases.update({n_in + i: n_out + j for i, j in ride.aliases.items()})
        in_specs += [ANY] * r_in
        out_specs += [ANY] * r_out
        out_shape += ride.results
        scratch_shapes += ride.sems
        operands += ride.operands
    params = _cparams(dimension_semantics=("arbitrary",) * len(grid))
    if prefetch is None:
        return _pallas_call(
            kernel_body, name=name, grid=grid, in_specs=in_specs, out_specs=out_specs, out_shape=out_shape,
            scratch_shapes=scratch_shapes, input_output_aliases=aliases, compiler_params=params,
        )(*operands)

    def with_prefetch(prefetch_ref, *refs):
        kernel_body(*refs)

    return _pallas_call(
        with_prefetch, name=name,
        grid_spec=pltpu.PrefetchScalarGridSpec(num_scalar_prefetch=1, grid=grid, in_specs=in_specs, out_specs=out_specs,
                                               scratch_shapes=scratch_shapes),
        out_shape=out_shape, input_output_aliases={i + 1: j for i, j in aliases.items()}, compiler_params=params,
    )(prefetch, *operands)


def _after(*arrays):
    return _Ride(arrays, [], {}, [], lambda *refs: None, lambda *refs: None)


def _gather_first_level(shards):
    n = len(shards)

    def copies(ins, outs, sems, landed):
        send_sems, recv_sems, local_sems = sems
        x, y, c = _place()
        peers = [(x, y, 1 - c)] + [(px, py, c) for px, py in _other_chips(x, y)]

        def row(peer):
            return 4 * x + 2 * y + c if not landed else 4 * peer[0] + 2 * peer[1] + peer[2]

        local = [pltpu.make_async_copy(ins[a], outs[a].at[4 * x + 2 * y + c], local_sems.at[a]) for a in range(n)]
        remote = [pltpu.make_async_remote_copy(
            src_ref=ins[a], dst_ref=outs[a].at[row(peer)], send_sem=send_sems.at[a, k], recv_sem=recv_sems.at[a, k],
            device_id=peer, device_id_type=MESH) for a in range(n) for k, peer in enumerate(peers)]
        return local, remote

    def start(ins, outs, sems):
        local, remote = copies(ins, outs, sems, False)
        for cp in local + remote:
            cp.start()

    def wait(ins, outs, sems):
        local, sent = copies(ins, outs, sems, False)
        for cp in copies(ins, outs, sems, True)[1]:
            cp.wait_recv()
        for cp in sent:
            cp.wait_send()
        for cp in local:
            cp.wait()

    return _Ride(shards, [_sds((N_DEV,) + s.shape, s.dtype) for s in shards], {},
                 [pltpu.SemaphoreType.DMA((n, 4)), pltpu.SemaphoreType.DMA((n, 4)), pltpu.SemaphoreType.DMA((n,))], start, wait)


def _gather_second_level(buffers):
    n = len(buffers)

    def copies(outs, sems, core):
        send_sems, recv_sems = sems
        x, y, c = _place()
        return [pltpu.make_async_remote_copy(
            src_ref=outs[a].at[4 * px + 2 * py + core], dst_ref=outs[a].at[4 * px + 2 * py + core], send_sem=send_sems.at[a, j],
            recv_sem=recv_sems.at[a, j], device_id=(x, y, 1 - c), device_id_type=MESH)
            for a in range(n) for j, (px, py) in enumerate(_other_chips(x, y))]

    def start(ins, outs, sems):
        for cp in copies(outs, sems, lax.axis_index("c")):
            cp.start()

    def wait(ins, outs, sems):
        for cp in copies(outs, sems, 1 - lax.axis_index("c")):
            cp.wait_recv()
        for cp in copies(outs, sems, lax.axis_index("c")):
            cp.wait_send()

    return _Ride(buffers, [_sds(b.shape, b.dtype) for b in buffers], {i: i for i in range(n)},
                 [pltpu.SemaphoreType.DMA((n, 3)), pltpu.SemaphoreType.DMA((n, 3))], start, wait)


def _relayed_gather(shards):
    n = len(shards)
    buffers = [_sds((N_DEV,) + s.shape, s.dtype) for s in shards]
    dma = pltpu.SemaphoreType.DMA

    def remote(src, dst, send_sem, recv_sem, to):
        return pltpu.make_async_remote_copy(src_ref=src, dst_ref=dst, send_sem=send_sem, recv_sem=recv_sem,
                                            device_id=to, device_id_type=MESH)

    def row(px, py, pc):
        return 4 * px + 2 * py + pc

    def ride(operands, aliases, sems, copies):
        def start(ins, outs, sem_refs):
            local, sent = copies(ins, outs, sem_refs, False)
            for cp in local + sent:
                cp.start()

        def wait(ins, outs, sem_refs):
            local, sent = copies(ins, outs, sem_refs, False)
            for cp in copies(ins, outs, sem_refs, True)[1]:
                cp.wait_recv()
            for cp in sent:
                cp.wait_send()
            for cp in local:
                cp.wait()

        return _Ride(operands, buffers, aliases, sems, start, wait)

    def first(ins, outs, sems, landed):
        x, y, c = _place()
        peers = [(x, y, 1 - c), (1 - x, y, c), (x, 1 - y, c)]
        local = [pltpu.make_async_copy(ins[a], outs[a].at[row(x, y, c)], sems[2].at[a]) for a in range(n)]
        return local, [remote(ins[a], outs[a].at[row(*peer) if landed else row(x, y, c)], sems[0].at[a, k], sems[1].at[a, k], peer)
                       for a in range(n) for k, peer in enumerate(peers)]

    def second(ins, outs, sems, landed):
        x, y, c = _place()
        mine = 1 - c if landed else c
        copies = []
        for a in range(n):
            half = shards[a].shape[0] // 2
            over_x, over_y, diagonal = outs[a].at[row(1 - x, y, mine)], outs[a].at[row(x, 1 - y, mine)], outs[a].at[row(1 - x, 1 - y, c)]
            lower, upper = pl.ds(0, half), pl.ds(half, half)
            copies += [remote(over_x, over_x, sems[0].at[a, 0], sems[1].at[a, 0], (x, y, 1 - c)),
                       remote(over_y, over_y, sems[0].at[a, 1], sems[1].at[a, 1], (x, y, 1 - c))]
            if landed:
                copies += [remote(diagonal.at[lower], diagonal.at[lower], sems[0].at[a, 2], sems[1].at[a, 2], (1 - x, y, c)),
                           remote(diagonal.at[upper], diagonal.at[upper], sems[0].at[a, 3], sems[1].at[a, 3], (x, 1 - y, c))]
            else:
                copies += [remote(over_y.at[lower], over_y.at[lower], sems[0].at[a, 2], sems[1].at[a, 2], (1 - x, y, c)),
                           remote(over_x.at[upper], over_x.at[upper], sems[0].at[a, 3], sems[1].at[a, 3], (x, 1 - y, c))]
        return [], copies

    def third(ins, outs, sems, landed):
        x, y, c = _place()
        return [], [remote(outs[a].at[row(1 - x, 1 - y, 1 - c if landed else c)], outs[a].at[row(1 - x, 1 - y, 1 - c if landed else c)],
                           sems[0].at[a], sems[1].at[a], (x, y, 1 - c)) for a in range(n)]

    def later(copies, n_sems):
        return lambda partly: ride(partly, {i: i for i in range(n)}, [dma((n,) + n_sems), dma((n,) + n_sems)], copies)

    return ride(shards, {}, [dma((n, 3)), dma((n, 3)), dma((n,))], first), later(second, (4,)), later(third, ())


def _sibling_swap_ride(grads, halves=True):
    n = len(grads)

    def copies(ins, outs, sems):
        x, y, c = _place()
        return [pltpu.make_async_remote_copy(
            src_ref=ins[a].at[1 - c] if halves else ins[a].at[pl.ds(0, 4)], dst_ref=outs[a], send_sem=sems[0].at[a],
            recv_sem=sems[1].at[a], device_id=(x, y, 1 - c), device_id_type=MESH) for a in range(n)]

    def start(ins, outs, sems):
        for cp in copies(ins, outs, sems):
            cp.start()

    def wait(ins, outs, sems):
        for cp in copies(ins, outs, sems):
            cp.wait()

    return _Ride(grads, [_sds((4,) + g.shape[-2:], g.dtype) for g in grads], {},
                 [pltpu.SemaphoreType.DMA((n,)), pltpu.SemaphoreType.DMA((n,))], start, wait)


def _chip_swap_ride(sums):
    n = len(sums)

    def copies(ins, outs, sems, landed):
        send_sems, recv_sems, local_sems = sems
        x, y, c = _place()
        mine = 2 * x + y
        local = [pltpu.make_async_copy(ins[a].at[mine], outs[a].at[mine], local_sems.at[a]) for a in range(n)]
        remote = [pltpu.make_async_remote_copy(
            src_ref=ins[a].at[2 * px + py], dst_ref=outs[a].at[2 * px + py if landed else mine], send_sem=send_sems.at[a, j],
            recv_sem=recv_sems.at[a, j], device_id=(px, py, c), device_id_type=MESH)
            for a in range(n) for j, (px, py) in enumerate(_other_chips(x, y))]
        return local, remote

    def start(ins, outs, sems):
        local, remote = copies(ins, outs, sems, False)
        for cp in local + remote:
            cp.start()

    def wait(ins, outs, sems):
        local, sent = copies(ins, outs, sems, False)
        for cp in copies(ins, outs, sems, True)[1]:
            cp.wait_recv()
        for cp in sent:
            cp.wait_send()
        for cp in local:
            cp.wait()

    return _Ride(sums, [_sds(s.shape, s.dtype) for s in sums], {},
                 [pltpu.SemaphoreType.DMA((n, 3)), pltpu.SemaphoreType.DMA((n, 3)), pltpu.SemaphoreType.DMA((n,))], start, wait)


def _send_buffers(shards, name):
    n = len(shards)

    def body(*refs):
        for (w, transposed, rows, cols), w_ref, o_ref in zip(shards, refs[:n], refs[n:]):
            if transposed:
                c, r = w.shape
                padded = jnp.concatenate([w_ref[...], jnp.zeros((cols - c, r), F32)], axis=0) if cols > c else w_ref[...]
                o_ref[...] = padded.T.astype(BF16)
            else:
                r, c = w.shape
                if (r, c) != (rows, cols):
                    o_ref[...] = jnp.zeros((rows, cols), BF16)
                o_ref[:r, :c] = w_ref[...].astype(BF16)

    return _pallas_call(body, name=name, out_shape=[_sds((rows, cols), BF16) for _, _, rows, cols in shards])(
        *[w for w, _, _, _ in shards])


def _all_gather(shards, name):
    n = len(shards)
    first, second, third = _relayed_gather(shards)
    levels = [first, second(shards), third(shards)]
    counts = [len(level.sems) for level in levels]

    def body(*refs):
        ins, outs, sems = refs[:n], refs[n:2 * n], refs[2 * n:]
        for i, level in enumerate(levels):
            mine = sems[sum(counts[:i]):sum(counts[:i + 1])]
            level.start(ins, outs, mine)
            level.wait(ins, outs, mine)

    return _pallas_call(
        body, name=name, in_specs=[ANY] * n, out_specs=[ANY] * n, out_shape=first.results,
        scratch_shapes=[s for level in levels for s in level.sems],
    )(*shards)


HBM = pl.BlockSpec(memory_space=pltpu.HBM)
SEMAPHORES = pl.BlockSpec(memory_space=pltpu.SEMAPHORE)
IN_FLIGHT = pltpu.CompilerParams(has_side_effects=pltpu.SideEffectType.DATAFLOW_SIDE_EFFECTING)


def _chip_swap_copies(src_refs, land_refs, send_sems, recv_sems, landed):
    x, y, c = _place()
    return [pltpu.make_async_remote_copy(
        src_ref=src.at[2 * px + py], dst_ref=land.at[2 * px + py if landed else 2 * x + y], send_sem=send_sems.at[3 * a + j],
        recv_sem=recv_sems.at[3 * a + j], device_id=(px, py, c), device_id_type=MESH)
        for a, (src, land) in enumerate(zip(src_refs, land_refs)) for j, (px, py) in enumerate(_other_chips(x, y))]


def _chip_swap_start(sums, name):
    n = len(sums)

    def body(*refs):
        src_refs, land_refs, (send_sems, recv_sems), token = refs[:n], refs[n:2 * n], refs[2 * n:2 * n + 2], refs[-1]
        for cp in _chip_swap_copies(src_refs, land_refs, send_sems, recv_sems, False):
            cp.start()
        token[...] = jnp.zeros_like(token)

    kept = [pltpu.HBM(s.shape, s.dtype) for s in sums]
    out = _pallas_call(
        body, name=name,
        out_shape=[pltpu.SemaphoreType.DMA((3 * n,)), pltpu.SemaphoreType.DMA((3 * n,))] + kept + kept + [_sds((8, 128), F32)],
        in_specs=[HBM] * (2 * n), out_specs=[SEMAPHORES, SEMAPHORES] + [HBM] * (2 * n) + [pl.BlockSpec(memory_space=pltpu.VMEM)],
        input_output_aliases={i: 2 + i for i in range(2 * n)}, compiler_params=IN_FLIGHT,
    )(*[pltpu.with_memory_space_constraint(s, pltpu.HBM) for s in sums],
      *[pltpu.with_memory_space_constraint(lax.empty(s.shape, s.dtype), pltpu.HBM) for s in sums])
    return out[0], out[1], out[2:2 + n], out[2 + n:2 + 2 * n], out[-1]


def _chip_swap_wait(send_sems, recv_sems, sums, landings, after, name):
    n = len(sums)

    def body(*refs):
        src_refs, land_refs, (send_sems, recv_sems) = refs[:n], refs[n:2 * n], refs[2 * n:2 * n + 2]
        for cp in _chip_swap_copies(src_refs, land_refs, send_sems, recv_sems, False):
            cp.wait_send()
        for cp in _chip_swap_copies(src_refs, land_refs, send_sems, recv_sems, True):
            cp.wait_recv()

    out = _pallas_call(
        body, name=name, out_shape=[pltpu.HBM(s.shape, s.dtype) for s in list(sums) + list(landings)],
        in_specs=[HBM] * (2 * n) + [SEMAPHORES, SEMAPHORES] + [ANY] * len(after), out_specs=[HBM] * (2 * n),
        input_output_aliases={i: i for i in range(2 * n)}, compiler_params=IN_FLIGHT,
    )(*sums, *landings, send_sems, recv_sems, *after)
    return out[:n], out[n:]


def _pair_sums(gs, rs, core, name):
    n_arrays = len(gs)

    def body(core_ref, *refs):
        for g_ref, r_ref, o_ref in zip(refs[:n_arrays], refs[n_arrays:2 * n_arrays], refs[2 * n_arrays:]):
            o_ref[...] = (g_ref[...].astype(F32) + r_ref[...].astype(F32)).astype(o_ref.dtype)

    def chip(g):
        return pl.BlockSpec((None,) + g.shape[-2:], lambda p, core_ref: (p, 0, 0))

    def own(g):
        if g.ndim == 3:
            return pl.BlockSpec((None,) + g.shape[-2:], lambda p, core_ref: (p + 4, 0, 0))
        return pl.BlockSpec((None, None) + g.shape[2:], lambda p, core_ref: (core_ref[0], p, 0, 0))

    return _pallas_call(
        body, name=name,
        grid_spec=pltpu.PrefetchScalarGridSpec(
            num_scalar_prefetch=1, grid=(4,), in_specs=[own(g) for g in gs] + [chip(g) for g in gs],
            out_specs=[chip(g) for g in gs]),
        out_shape=[_sds((4,) + g.shape[-2:], g.dtype) for g in gs], compiler_params=_cparams(dimension_semantics=("arbitrary",)),
    )(core, *gs, *rs)


def _adamw_math(w, g, m, v):
    m = ADAM_B1 * m + (1.0 - ADAM_B1) * g
    v = ADAM_B2 * v + (1.0 - ADAM_B2) * (g * g)
    m_hat = m / (1.0 - ADAM_B1 ** ADAM_STEP)
    v_hat = v / (1.0 - ADAM_B2 ** ADAM_STEP)
    return -ADAM_LR * (m_hat / (jnp.sqrt(v_hat) + ADAM_EPS) + ADAM_WD * w), m, v


def _adamw_many(weights, name, ride=None):
    steps = 4
    in_specs, out_specs, out_shape, operands, tiles = [], [], [], [], []
    for w, m, v, parts, own, transposed in weights:
        _, pr, pc = parts.shape
        if transposed:
            c, r = w.shape
            tile = pl.BlockSpec((c, r // steps), lambda i: (0, i))
            part_tile = pl.BlockSpec((4, r // steps, pc), lambda i: (0, i, 0))
            tiles.append((c, r // steps))
        elif w.shape[0] % (8 * steps) == 0:
            r, c = w.shape
            tile = pl.BlockSpec((r // steps, c), lambda i: (i, 0))
            part_tile = pl.BlockSpec((4, r // steps, pc), lambda i: (0, i, 0))
            tiles.append((r // steps, c))
        else:
            tile = pl.BlockSpec(w.shape, lambda i: (0, 0))
            part_tile = pl.BlockSpec(parts.shape, lambda i: (0, 0, 0))
            tiles.append(w.shape)
        in_specs += [tile, tile, tile] + [part_tile] * (1 if own is None else 2)
        out_specs += [tile] * 4
        out_shape += [_sds(w.shape, F32)] * 4
        operands += [w, m, v, parts] + ([] if own is None else [own])
    n_in = len(operands)

    def body(*refs):
        ins, outs = list(refs[:n_in]), refs[n_in:]
        this_chip = 2 * lax.axis_index("x") + lax.axis_index("y")
        for k, (_, _, _, _, own, transposed) in enumerate(weights):
            w_ref, m_ref, v_ref, p_ref = ins[:4]
            own_ref = None if own is None else ins[4]
            del ins[:4 if own is None else 5]
            rows, cols = tiles[k]
            g = None
            for q in range(4):
                index = (q,) if transposed else (q, slice(0, rows), slice(0, cols))
                part = p_ref[index] if own is None else jnp.where(this_chip == q, own_ref[index], p_ref[index])
                g = part.astype(F32) if g is None else g + part.astype(F32)
            if transposed:
                g = g.T[:rows]
            g_out, d_out, m_out, v_out = outs[4 * k:4 * k + 4]
            g_out[...] = g
            d_out[...], m_out[...], v_out[...] = _adamw_math(w_ref[...], g, m_ref[...], v_ref[...])

    return _call(body, name, (steps,), in_specs, out_specs, out_shape, [], operands, ride)


SMALL = ("ssm_a_re", "ssm_a_im", "ssm_log_dt", "ssm_b_re", "ssm_b_im", "ssm_c_re", "ssm_c_im", "ssm_d",
         "ln1_g", "ln1_b", "ln2_g", "ln2_b")


def _pack_rows(arrays):
    rows = []
    for a in arrays:
        flat = a.reshape(-1)
        rows.append(jnp.pad(flat, (0, -flat.shape[0] % 128)).reshape(-1, 128))
    packed = jnp.concatenate(rows, axis=0)
    return jnp.pad(packed, ((0, -packed.shape[0] % 8), (0, 0)))


def _unpack_rows(packed, shapes):
    out, row = [], 0
    for shape in shapes:
        size = math.prod(shape)
        n_rows = -(-size // 128)
        out.append(packed[row:row + n_rows].reshape(-1)[:size].reshape(shape))
        row += n_rows
    return out


def _sum_devices(parts):
    def body(p_ref, o_ref):
        total = p_ref[0]
        for dev in range(1, N_DEV):
            total = total + p_ref[dev]
        o_ref[...] = total

    return _pallas_call(body, name="sum_devices", out_shape=_sds(parts.shape[1:], F32))(parts)


def _adamw_replicated(ws, ms, vs, gs):
    n = len(ws)

    def body(*refs):
        w_refs, m_refs, v_refs, g_refs, d_out, m_out, v_out = (refs[i * n:(i + 1) * n] for i in range(7))
        for i in range(n):
            d_out[i][...], m_out[i][...], v_out[i][...] = _adamw_math(w_refs[i][...], g_refs[i][...], m_refs[i][...], v_refs[i][...])

    out = _pallas_call(body, name="adamw_replicated", out_shape=[_sds(w.shape, F32) for w in ws] * 3,
                       compiler_params=_cparams())(*ws, *ms, *vs, *gs)
    return out[:n], out[n:2 * n], out[2 * n:]


def kernel(x, w_in, b_gate, w_attn_br, w_ssm_br, w_out, ssm_a_re, ssm_a_im, ssm_log_dt, ssm_b_re, ssm_b_im, ssm_c_re, ssm_c_im, ssm_d, w_glu, ln1_g, ln1_b, w_ff_gate, w_ff_up, w_ff_down, ln2_g, ln2_b, loss_target, m_w_in, m_b_gate, m_w_attn_br, m_w_ssm_br, m_w_out, m_ssm_a_re, m_ssm_a_im, m_ssm_log_dt, m_ssm_b_re, m_ssm_b_im, m_ssm_c_re, m_ssm_c_im, m_ssm_d, m_w_glu, m_ln1_g, m_ln1_b, m_w_ff_gate, m_w_ff_up, m_w_ff_down, m_ln2_g, m_ln2_b, v_w_in, v_b_gate, v_w_attn_br, v_w_ssm_br, v_w_out, v_ssm_a_re, v_ssm_a_im, v_ssm_log_dt, v_ssm_b_re, v_ssm_b_im, v_ssm_c_re, v_ssm_c_im, v_ssm_d, v_w_glu, v_ln1_g, v_ln1_b, v_w_ff_gate, v_w_ff_up, v_w_ff_down, v_ln2_g, v_ln2_b):
    given = dict(locals())
    x2, target = x[0], loss_target[0]
    core = lax.axis_index("c").astype(jnp.int32).reshape(1)

    sharded = ("w_in", "w_attn_br", "w_ssm_br", "w_glu", "w_ff_gate", "w_ff_up", "b_gate", "w_out", "w_ff_down")
    send_shape = dict(w_in=(D_MODEL, 896), w_attn_br=(ATTN_WIDTH, 128), w_ssm_br=(SSM_WIDTH, 128), w_glu=(SSM_WIDTH, 128),
                      w_out=(128, D_MODEL), w_ff_gate=(D_MODEL, FF_PAD), w_ff_up=(D_MODEL, FF_PAD), w_ff_down=(FF_PAD, D_MODEL))
    local = {k: given[k][0] for k in sharded}
    narrow = ("w_ff_gate", "w_ff_up")
    def to_send(k):
        return (local[k].T, True, *send_shape[k]) if k in narrow else (local[k], False, *send_shape[k])

    later = [k for k in sharded if k not in ("w_in", "b_gate")]
    sends = dict(zip(["w_in"] + later, _send_buffers([to_send("w_in")], "send_w_in")
                     + _send_buffers([to_send(k) for k in later], "send_weights")))
    sends["b_gate"] = local["b_gate"]
    mixer_weights = ("w_attn_br", "w_ssm_br", "w_glu", "b_gate", "w_out")
    ff_weights = ("w_ff_gate", "w_ff_up", "w_ff_down")
    wt = {}
    wt["w_in"], = _all_gather([sends["w_in"]], "gather_w_in")

    a_re, a_im, log_dt = ssm_a_re[0], ssm_a_im[0], ssm_log_dt[0].reshape(SSM_GROUPS, 1)
    b_re_t, b_im_t = ssm_b_re[0].transpose(0, 2, 1), ssm_b_im[0].transpose(0, 2, 1)
    abar_re, abar_im, e_re, e_im, bbar_re_t, bbar_im_t = _ssm_prep(a_re, a_im, log_dt, b_re_t, b_im_t)
    bmat, cmat, a_chunks = _ssm_tables(abar_re, abar_im, bbar_re_t, bbar_im_t, ssm_c_re[0], ssm_c_im[0])
    cos_t, sin_t = _rope_tables()

    big_mixer, ff_in = [k for k in mixer_weights if k != "b_gate"], ("w_ff_gate", "w_ff_up")
    n_mixer = len(big_mixer)
    mixer_1, mixer_2, mixer_3 = _relayed_gather([sends[k] for k in big_mixer])
    ff_in_1, ff_in_2, ff_in_3 = _relayed_gather([sends[k] for k in ff_in])
    ff_down_1, ff_down_2, ff_down_3 = _relayed_gather([sends["w_ff_down"]])
    proj, *landed = _proj(x2, wt["w_in"], mixer_1 + _gather_first_level([sends["b_gate"]]))
    mixer, bias = landed[:n_mixer], landed[n_mixer:]
    attn, lse, q_pm, k_pm, v_pm, *landed = _attn_fwd(proj, cos_t, sin_t,
                                                     mixer_2(mixer) + _gather_second_level(bias) + ff_in_1)
    mixer, b_gate_full, ff = landed[:n_mixer], landed[n_mixer], landed[n_mixer + 1:]
    ys, states, *landed = _ssm_fwd(proj, bmat, cmat, a_chunks, ssm_d, mixer_3(mixer) + ff_in_2(ff) + ff_down_1)
    wt.update(zip(big_mixer, landed[:n_mixer]))
    ff, ff_down = landed[n_mixer:n_mixer + 2], landed[n_mixer + 2:]
    wt["w_out"] = wt["w_out"].reshape(D_MODEL, D_MODEL)
    h, xhat1, rstd1, glu, y_attn, y_ssm, *landed = _mixer_out(
        attn, ys, proj, x2, wt["w_attn_br"], wt["w_ssm_br"], wt["w_glu"], wt["w_out"], b_gate_full, ln1_g, ln1_b,
        ff_in_3(ff) + ff_down_2(ff_down))
    wt.update(zip(ff_in, landed[:2]))
    ff_a, ff_b, ff_f, w_ff_down = _ff_up(h, wt["w_ff_gate"], wt["w_ff_up"], ff_down_3(landed[2:]))
    wt["w_ff_down"] = w_ff_down.reshape(D_FF_PAD, D_MODEL)
    dr2, d_ln2_g, d_ln2_b, loss_lanes = _ff_down_loss(ff_f, wt["w_ff_down"], h, target, ln2_g, ln2_b)

    def pair_sums(names, contrib, from_sibling):
        return _pair_sums([contrib[k] for k in names], from_sibling, core, "pair_sums_" + names[0])

    d_a, d_b = _ff_down_bwd(dr2, wt["w_ff_down"], ff_a, ff_b)
    contrib = dict(w_ff_gate=_weight_grad(h, d_a, "wgrad_w_ff_gate", FF_PAD),
                   w_ff_up=_weight_grad(h, d_b, "wgrad_w_ff_up", FF_PAD),
                   w_ff_down=_weight_grad(ff_f, dr2, "wgrad_w_ff_down"))
    dr1, d_ln1_g, d_ln1_b, *from_sibling = _ff_up_bwd(
        d_a, d_b, wt["w_ff_gate"], wt["w_ff_up"], dr2, xhat1, rstd1, ln1_g, _sibling_swap_ride([contrib[k] for k in ff_weights]))
    ff_sums = pair_sums(ff_weights, contrib, from_sibling)

    d_ya, d_yssm, d_proj, d_attn, d_glu, d_ys, mixed, y_s, gy, d_bg = _mixer_bwd(
        dr1, proj, y_attn, y_ssm, glu, ys, wt["w_attn_br"], wt["w_ssm_br"], wt["w_glu"], wt["w_out"], b_gate_full)
    contrib.update(w_attn_br=_weight_grad(attn, d_ya, "wgrad_w_attn_br", 128),
                   w_ssm_br=_weight_grad(y_s, d_yssm, "wgrad_w_ssm_br", 128),
                   w_glu=_weight_grad(gy, d_glu, "wgrad_w_glu", 128),
                   w_out=_weight_grad(mixed, dr1, "wgrad_w_out"),
                   b_gate=d_bg.reshape(2, 4, 2, 128).transpose(2, 1, 0, 3))
    d_proj, *landed = _attn_bwd(q_pm, k_pm, v_pm, cos_t, sin_t, attn, lse, d_attn, d_proj,
                                _chip_swap_ride(ff_sums) + _sibling_swap_ride([contrib[k] for k in mixer_weights]))
    parts, own_sums = dict(zip(ff_weights, landed[:len(ff_weights)])), {}
    mixer_sums = pair_sums(mixer_weights, contrib, landed[len(ff_weights):])
    landed = _ssm_bwd(d_ys, proj, states, bmat, cmat, a_chunks, ssm_d, d_proj, _chip_swap_ride(mixer_sums))
    d_proj, *ssm_blocks, d_abar, d_skip = landed[:7]
    parts.update(zip(mixer_weights, landed[7:]))

    gbb_re_t, gbb_im_t, gc_re, gc_im = (blocks.reshape(SSM_GROUPS, SSM_GROUP, SSM_STATE) for blocks in ssm_blocks)
    ga_re = d_abar[:, 0, :CHUNK_STATES].reshape(SSM_GROUPS, SSM_STATE)
    ga_im = d_abar[:, 0, CHUNK_STATES:].reshape(SSM_GROUPS, SSM_STATE)
    g_a_re, g_a_im, g_log_dt, g_b_re_t, g_b_im_t = _ssm_param_bwd(
        a_re, a_im, log_dt, b_re_t, b_im_t, abar_re, abar_im, e_re, e_im, ga_re, ga_im, gbb_re_t, gbb_im_t)
    mine = [g_a_re, g_a_im, g_log_dt, g_b_re_t, g_b_im_t, gc_re, -gc_im,
            d_skip, d_ln1_g, d_ln1_b, d_ln2_g, d_ln2_b]
    small_packed = _pack_rows(mine + [loss_lanes])

    w_in_contrib, small_partly = _weight_grad_rows(x2, d_proj, core, "wgrad_w_in_first", 896, 0, 6,
                                                   _gather_first_level([small_packed]))
    w_in_contrib, from_sibling, every = _weight_grad_rows(
        x2, d_proj, core, "wgrad_w_in_rest", 896, 6, 2,
        _sibling_swap_ride([w_in_contrib], halves=False) + _gather_second_level([small_partly]), in_place=True)
    w_in_sum, = _pair_sums([w_in_contrib], [from_sibling], core, "pair_sums_w_in")
    send_sems, recv_sems, w_in_sum, landing, token = _chip_swap_start([w_in_sum], "w_in_chip_swap_start")

    def adamw_of(k):
        taken = (lambda a: a.T) if k in narrow else (lambda a: a)
        return taken(local[k]), taken(given["m_" + k][0]), taken(given["v_" + k][0]), parts[k], own_sums.get(k), k in narrow

    others = [k for k in sharded if k != "w_in"]
    updated = _adamw_many([adamw_of(k) for k in others], "adamw_others", _after(token))
    grad_x, = _grad_x(d_proj, wt["w_in"], dr1, _after(token))

    def held(k, a):
        return a.transpose(0, 1, 3, 2) if k in ("ssm_b_re", "ssm_b_im") else a

    *small_grads, loss_sum = _unpack_rows(_sum_devices(every), [held(k, given[k]).shape for k in SMALL] + [(1, 128)])
    small = _adamw_replicated([held(k, given[k]) for k in SMALL], [held(k, given["m_" + k]) for k in SMALL],
                              [held(k, given["v_" + k]) for k in SMALL], small_grads)
    loss = loss_sum[0, 0]

    (own_sums["w_in"],), (parts["w_in"],) = _chip_swap_wait(
        send_sems, recv_sems, w_in_sum, landing, [grad_x, updated[0], small[0][0]], "w_in_chip_swap_wait")
    updated += _adamw_many([adamw_of("w_in")], "adamw_w_in")

    grads, deltas, new_m, new_v = {}, {}, {}, {}
    for i, k in enumerate(others + ["w_in"]):
        out = [o.T if k in narrow else o for o in updated[4 * i:4 * i + 4]]
        grads[k], deltas[k], new_m[k], new_v[k] = (o.reshape((1,) + local[k].shape) for o in out)
    for res, values in zip((grads, deltas, new_m, new_v), (small_grads,) + small):
        res.update((k, held(k, a)) for k, a in zip(SMALL, values))

    order = ("w_in", "b_gate", "w_attn_br", "w_ssm_br", "w_out", "ssm_a_re", "ssm_a_im", "ssm_log_dt", "ssm_b_re", "ssm_b_im",
             "ssm_c_re", "ssm_c_im", "ssm_d", "w_glu", "ln1_g", "ln1_b", "w_ff_gate", "w_ff_up", "w_ff_down", "ln2_g", "ln2_b")
    return (loss, grad_x[None], *[grads[k] for k in order], *[deltas[k] for k in order], *[new_m[k] for k in order],
            *[new_v[k] for k in order])
```

```python
import functools
import math

import jax
import jax.numpy as jnp
import numpy as np
from jax import lax
from jax.experimental import pallas as pl
from jax.experimental.pallas import tpu as pltpu

F32 = jnp.float32
BF16 = jnp.bfloat16

N_DEV = 8
SEQ = 2048
D_MODEL = 1024
HEAD_DIM = 64
ATTN_WIDTH = 512
QKV_WIDTH = 1536
SSM_WIDTH = 512
SSM_GROUPS = 32
SSM_GROUP = 16
SSM_STATE = 64
IN_WIDTH = 7168
D_FF = 2816
FF_SHARD = D_FF // N_DEV
FF_PAD = 384
D_FF_PAD = FF_PAD * N_DEV
DN_ALPHA = 2.0 ** 0.25
LN_EPS = 1e-5
NEG_INF = -1e30
ROPE_THETA = 10000.0
BLOCK = 128
GROUPS = ((1, 16), (4, 4), (16, 1))

ADAM_LR = 0.001
ADAM_B1 = 0.9
ADAM_B2 = 0.999
ADAM_EPS = 1e-08
ADAM_WD = 0.01
ADAM_STEP = 10

VMEM_LIMIT = 56 * 1024 * 1024


_pallas_call = pl.pallas_call


def _cparams(**kw):
    return pltpu.CompilerParams(vmem_limit_bytes=VMEM_LIMIT, **kw)


def _dot(a, b):
    return jnp.dot(a, b, preferred_element_type=F32)


def _dot_nt(a, b):
    return lax.dot_general(a, b, (((1,), (1,)), ((), ())), preferred_element_type=F32)


def _side_by_side(w_ref, row=None):
    rows = slice(None) if row is None else pl.ds(row, 1)
    return jnp.concatenate([w_ref[i, rows, :] for i in range(w_ref.shape[0])], axis=1)


def _dot_tn(a, b):
    return lax.dot_general(a, b, (((0,), (0,)), ((), ())), preferred_element_type=F32)


def _rope_tables():
    half = HEAD_DIM // 2
    inv_freq = np.float32(ROPE_THETA) ** (-np.arange(half, dtype=np.float32) / np.float32(half))
    ang = np.arange(SEQ, dtype=np.float32)[:, None] * inv_freq[None, :]
    cos, sin = np.cos(ang).astype(np.float32), np.sin(ang).astype(np.float32)
    tables = np.tile(cos, (1, 4)), np.tile(np.concatenate([-sin, sin], axis=1), (1, 2))

    def by_phase(t):
        return np.stack([t.reshape(SEQ // d, d, 128).transpose(1, 0, 2).reshape(SEQ, 128) for d, _ in GROUPS])

    return jnp.asarray(by_phase(tables[0])), jnp.asarray(by_phase(tables[1]))


def _swap_halves(x):
    lane = lax.broadcasted_iota(jnp.int32, x.shape, 1)
    return jnp.where((lane & 63) < 32, pltpu.roll(x, 96, axis=1), pltpu.roll(x, 32, axis=1))


def _group_rows(d, nb, r, i):
    src = pl.ds(i * BLOCK, BLOCK) if d == 1 else pl.ds(r + i * BLOCK * d, BLOCK, stride=d)
    return src, pl.ds((r * nb + i) * BLOCK, BLOCK)


def _attn_masks():
    a_idx = lax.broadcasted_iota(jnp.int32, (2 * BLOCK, 2 * BLOCK), 0) & (BLOCK - 1)
    c_idx = lax.broadcasted_iota(jnp.int32, (2 * BLOCK, 2 * BLOCK), 1)
    cur_ok = jnp.logical_and(c_idx >= BLOCK, c_idx - BLOCK <= a_idx)
    prev_ok = jnp.logical_and(c_idx < BLOCK, c_idx >= a_idx)
    lane = lax.broadcasted_iota(jnp.int32, (BLOCK, 128), 1)
    return cur_ok, prev_ok, lane < HEAD_DIM


def _stack_heads(t, head0):
    zero = jnp.zeros_like(t)
    return jnp.concatenate([jnp.where(head0, t, zero), jnp.where(head0, zero, t)], axis=0)


def _unstack_heads(t2, head0):
    return jnp.where(head0, t2[:BLOCK], t2[BLOCK:])


def _attn_fwd(proj, cos_t, sin_t, ride=None):
    def body(q0, q1, q2, k0, k1, k2, v0, v1, v2, cos_ref, sin_ref, attn_ref, lse_ref, qpm_ref, kpm_ref, vpm_ref,
             qs, ks, vs, os_, ms, ls, acc, mnat, lnat):
        cur_ok, prev_ok, head0 = _attn_masks()
        ks[:BLOCK, :] = jnp.zeros((BLOCK, 128), BF16)
        vs[:BLOCK, :] = jnp.zeros((BLOCK, 128), BF16)
        for g, (d, nb) in enumerate(GROUPS):
            q_ref, k_ref, v_ref = (q0, q1, q2)[g], (k0, k1, k2)[g], (v0, v1, v2)[g]
            for r in range(d):
                for i in range(nb):
                    src, dst = _group_rows(d, nb, r, i)
                    below = pl.ds(dst.start + BLOCK, BLOCK)
                    c, s = cos_ref[g, dst, :], sin_ref[g, dst, :]
                    q = q_ref[src, :]
                    k = k_ref[src, :]
                    qs[dst, :] = ((q * c + _swap_halves(q) * s) * 0.125).astype(BF16)
                    ks[below, :] = (k * c + _swap_halves(k) * s).astype(BF16)
                    vs[below, :] = v_ref[src, :].astype(BF16)
                    qpm_ref[g, dst, :], kpm_ref[g, dst, :], vpm_ref[g, dst, :] = qs[dst, :], ks[below, :], vs[below, :]

            def block(b, carry, nb=nb):
                has_prev = (b & (nb - 1)) > 0
                cur = pl.ds(pl.multiple_of(b * BLOCK, BLOCK), BLOCK)
                window = pl.ds(pl.multiple_of(b * BLOCK, BLOCK), 2 * BLOCK)
                valid = jnp.logical_or(cur_ok, jnp.logical_and(prev_ok, has_prev))
                s = jnp.where(valid, _dot_nt(_stack_heads(qs[cur, :], head0), ks[window, :]), NEG_INF)
                m = jnp.max(s, axis=1, keepdims=True)
                p = jnp.exp(s - m)
                os_[cur, :] = _unstack_heads(_dot(p.astype(BF16), vs[window, :]), head0)
                ms[cur, :] = _unstack_heads(m, head0)
                ls[cur, :] = _unstack_heads(jnp.sum(p, axis=1, keepdims=True), head0)
                return carry

            lax.fori_loop(0, SEQ // BLOCK, block, 0, unroll=16)

            for r in range(d):
                for i in range(nb):
                    src, dst = _group_rows(d, nb, r, i)
                    if g == 0:
                        acc[src, :], mnat[src, :], lnat[src, :] = os_[dst, :], ms[dst, :], ls[dst, :]
                    else:
                        m_old, m_g = mnat[src, :], ms[dst, :]
                        m_new = jnp.maximum(m_old, m_g)
                        a_old, a_g = jnp.exp(m_old - m_new), jnp.exp(m_g - m_new)
                        acc[src, :] = a_old * acc[src, :] + a_g * os_[dst, :]
                        lnat[src, :] = a_old * lnat[src, :] + a_g * ls[dst, :]
                        mnat[src, :] = m_new
        for i in range(SEQ // BLOCK):
            rows = pl.ds(i * BLOCK, BLOCK)
            l = lnat[rows, :]
            attn_ref[rows, :] = acc[rows, :] / l
            lse_ref[rows, :] = mnat[rows, :] + jnp.log(l)

    def col(base):
        return pl.BlockSpec((SEQ, 128), lambda hp, base=base: (0, base + hp))

    in_specs = [col(g * 4) for g in range(3)] + [col(12 + g * 4) for g in range(3)] + [col(24 + g * 4) for g in range(3)]
    table = pl.BlockSpec((3, SEQ, 128), lambda hp: (0, 0, 0), pipeline_mode=pl.Buffered(1))
    out = pl.BlockSpec((SEQ, 128), lambda hp: (0, hp))
    by_phase = pl.BlockSpec((3, SEQ, 128), lambda hp: (0, 0, hp))
    return _call(
        body, "attn_fwd", (4,), in_specs + [table, table], [out, out] + [by_phase] * 3,
        [_sds((SEQ, ATTN_WIDTH), F32), _sds((SEQ, ATTN_WIDTH), F32)] + [_sds((3, SEQ, ATTN_WIDTH), BF16)] * 3,
        [pltpu.VMEM((SEQ, 128), BF16)] + [pltpu.VMEM((SEQ + BLOCK, 128), BF16)] * 2 + [pltpu.VMEM((SEQ, 128), F32)] * 6,
        [proj] * 9 + [cos_t, sin_t], ride)


def _attn_bwd_group_body(g):
    d, nb = GROUPS[g]

    def body(qs_ref, ks_ref, vs_ref, cos_ref, sin_ref, lse_ref, dattn_ref, dsum_ref, dproj_ref,
             ks, vs, dos, lss, dss, dqs, dks, dvs, stage, outs, sems):
        cur_ok, prev_ok, head0 = _attn_masks()
        qs = qs_ref.at[g]
        ks[:BLOCK, :] = jnp.zeros((BLOCK, 128), BF16)
        vs[:BLOCK, :] = jnp.zeros((BLOCK, 128), BF16)
        dks[:BLOCK, :] = jnp.zeros((BLOCK, 128), F32)
        dvs[:BLOCK, :] = jnp.zeros((BLOCK, 128), F32)
        for r in range(d):
            for i in range(nb):
                src, dst = _group_rows(d, nb, r, i)
                below = pl.ds(dst.start + BLOCK, BLOCK)
                ks[below, :] = ks_ref[g, dst, :]
                vs[below, :] = vs_ref[g, dst, :]
                dos[dst, :] = dattn_ref[src, :].astype(BF16)
                for per_head, spread in ((dsum_ref[src, :], dss), (lse_ref[src, :], lss)):
                    other = pltpu.roll(per_head, HEAD_DIM, axis=1)
                    spread[0, dst, :] = jnp.where(head0, per_head, other)
                    spread[1, dst, :] = jnp.where(head0, other, per_head)
                dks[below, :] = jnp.zeros((BLOCK, 128), F32)
                dvs[below, :] = jnp.zeros((BLOCK, 128), F32)

        def per_stacked_row(spread, cur):
            h0, h1 = spread[0, cur, :], spread[1, cur, :]
            return jnp.concatenate([jnp.concatenate([h0, h0], axis=1), jnp.concatenate([h1, h1], axis=1)], axis=0)

        def block(b, carry):
            has_prev = (b & (nb - 1)) > 0
            cur = pl.ds(pl.multiple_of(b * BLOCK, BLOCK), BLOCK)
            window = pl.ds(pl.multiple_of(b * BLOCK, BLOCK), 2 * BLOCK)
            valid = jnp.logical_or(cur_ok, jnp.logical_and(prev_ok, has_prev))
            q2, do2 = _stack_heads(qs[cur, :], head0), _stack_heads(dos[cur, :], head0)
            kw, vw = ks[window, :], vs[window, :]
            s = jnp.where(valid, _dot_nt(q2, kw), NEG_INF)
            p = jnp.exp(s - per_stacked_row(lss, cur))
            ds = (p * (_dot_nt(do2, vw) - per_stacked_row(dss, cur))).astype(BF16)
            dvs[window, :] += _dot_tn(p.astype(BF16), do2)
            dks[window, :] += _dot_tn(ds, q2)
            dqs[cur, :] = _unstack_heads(_dot(ds, kw), head0)
            return carry

        lax.fori_loop(0, SEQ // BLOCK, block, 0, unroll=16)

        hp = pl.program_id(0)
        copies = []
        for kind in range(3):
            for r in range(d):
                for i in range(nb):
                    src, dst = _group_rows(d, nb, r, i)
                    below = pl.ds(dst.start + BLOCK, BLOCK)
                    if kind == 2:
                        stage[src, :] = dvs[below, :]
                    else:
                        c, s = cos_ref[g, dst, :], sin_ref[g, dst, :]
                        t = dqs[dst, :] * 0.125 if kind == 0 else dks[below, :]
                        stage[src, :] = t * c - _swap_halves(t) * s
            for i in range(SEQ // MM_ROWS):
                rows = pl.ds(i * MM_ROWS, MM_ROWS)
                outs[kind, rows, :] = stage[rows, :].astype(BF16)
            column = pl.multiple_of((kind * 12 + g * 4 + hp) * 128, 128)
            copies.append(pltpu.make_async_copy(outs.at[kind], dproj_ref.at[:, pl.ds(column, 128)], sems.at[kind]))
            copies[-1].start()
        for cp in copies:
            cp.wait()

    return body


def _attn_bwd(q_pm, k_pm, v_pm, cos_t, sin_t, attn, lse, dattn, dproj, ride=None):
    groups = [_attn_bwd_group_body(g) for g in range(3)]

    def body(qs_ref, ks_ref, vs_ref, cos_ref, sin_ref, attn_ref, lse_ref, dattn_ref, dproj_in, dproj_ref, dsum, *scratch):
        del dproj_in
        head0 = _attn_masks()[2]
        for i in range(SEQ // BLOCK):
            rows = pl.ds(i * BLOCK, BLOCK)
            prod = dattn_ref[rows, :] * attn_ref[rows, :]
            d0 = jnp.sum(jnp.where(head0, prod, 0.0), axis=1, keepdims=True)
            d1 = jnp.sum(jnp.where(head0, 0.0, prod), axis=1, keepdims=True)
            dsum[rows, :] = jnp.where(head0, d0, d1)
        for g in range(3):
            groups[g](qs_ref, ks_ref, vs_ref, cos_ref, sin_ref, lse_ref, dattn_ref, dsum, dproj_ref, *scratch)

    def col(base):
        return pl.BlockSpec((SEQ, 128), lambda hp, base=base: (0, base + hp))

    table = pl.BlockSpec((3, SEQ, 128), lambda hp: (0, 0, 0), pipeline_mode=pl.Buffered(1))
    by_phase = pl.BlockSpec((3, SEQ, 128), lambda hp: (0, 0, hp))
    return _call(
        body, "attn_bwd", (4,), [by_phase] * 3 + [table, table, col(0), col(0), col(0), ANY],
        [ANY], [_sds((SEQ, IN_WIDTH), BF16)],
        [pltpu.VMEM((SEQ, 128), F32)]
        + [pltpu.VMEM((SEQ + BLOCK, 128), BF16)] * 2 + [pltpu.VMEM((SEQ, 128), BF16)]
        + [pltpu.VMEM((2, SEQ, 128), F32)] * 2 + [pltpu.VMEM((SEQ, 128), F32)]
        + [pltpu.VMEM((SEQ + BLOCK, 128), F32)] * 2 + [pltpu.VMEM((SEQ, 128), F32)]
        + [pltpu.VMEM((3, SEQ, 128), BF16), pltpu.SemaphoreType.DMA((3,))],
        [q_pm, k_pm, v_pm, cos_t, sin_t, attn, lse, dattn, dproj], ride, aliases={8: 0})


SSM_CHUNKS = 4
CHUNK_STATES = 512
SCAN_ROWS = 8
U_COL = (3 * QKV_WIDTH) // 128


def _cmul(xr, xi, yr, yi):
    return xr * yr - xi * yi, xr * yi + xi * yr


def _ssm_prep(a_re, a_im, log_dt, b_re_t, b_im_t):
    def body(ar_ref, ai_ref, ldt_ref, br_ref, bi_ref, abr_ref, abi_ref, er_ref, ei_ref, bbr_ref, bbi_ref):
        ar, ai = ar_ref[...], ai_ref[...]
        dt = jnp.exp(ldt_ref[...])
        mag = jnp.exp(ar * dt)
        abr, abi = mag * jnp.cos(ai * dt), mag * jnp.sin(ai * dt)
        den = ar * ar + ai * ai
        nr, ni = abr - 1.0, abi
        er, ei = (nr * ar + ni * ai) / den, (ni * ar - nr * ai) / den
        abr_ref[...], abi_ref[...], er_ref[...], ei_ref[...] = abr, abi, er, ei
        er3, ei3 = er[:, None, :], ei[:, None, :]
        br, bi = br_ref[...], bi_ref[...]
        bbr_ref[...] = er3 * br - ei3 * bi
        bbi_ref[...] = er3 * bi + ei3 * br

    gp = jax.ShapeDtypeStruct(a_re.shape, F32)
    gb = jax.ShapeDtypeStruct(b_re_t.shape, F32)
    return _pallas_call(body, name="ssm_prep", out_shape=(gp, gp, gp, gp, gb, gb))(a_re, a_im, log_dt, b_re_t, b_im_t)


def _ssm_param_bwd(a_re, a_im, log_dt, b_re_t, b_im_t, abar_re, abar_im, e_re, e_im, ga_re, ga_im, gbb_re_t, gbb_im_t):
    def body(ar_ref, ai_ref, ldt_ref, br_ref, bi_ref, abr_ref, abi_ref, er_ref, ei_ref, gar_ref, gai_ref, gbr_ref, gbi_ref,
             o_ar, o_ai, o_ldt, o_br, o_bi):
        ar, ai = ar_ref[...], ai_ref[...]
        dt = jnp.exp(ldt_ref[...])
        er, ei = er_ref[...], ei_ref[...]
        br, bi, gbr, gbi = br_ref[...], bi_ref[...], gbr_ref[...], gbi_ref[...]
        er3, ei3 = er[:, None, :], ei[:, None, :]
        o_br[...] = er3 * gbr + ei3 * gbi
        o_bi[...] = er3 * gbi - ei3 * gbr
        ge_r = jnp.sum(br * gbr + bi * gbi, axis=1)
        ge_i = jnp.sum(br * gbi - bi * gbr, axis=1)
        den = ar * ar + ai * ai
        ilr, ili = ar / den, -ai / den
        t_r, t_i = _cmul(ilr, -ili, ge_r, ge_i)
        gab_r, gab_i = gar_ref[...] + t_r, gai_ref[...] + t_i
        gz_r, gz_i = _cmul(abr_ref[...], -abi_ref[...], gab_r, gab_i)
        el_r, el_i = _cmul(er, ei, ilr, ili)
        u_r, u_i = _cmul(el_r, -el_i, ge_r, ge_i)
        o_ar[...] = dt * gz_r - u_r
        o_ai[...] = dt * gz_i - u_i
        o_ldt[...] = jnp.sum(gz_r * ar + gz_i * ai, axis=1, keepdims=True) * dt

    gp = jax.ShapeDtypeStruct(a_re.shape, F32)
    gb = jax.ShapeDtypeStruct(b_re_t.shape, F32)
    return _pallas_call(body, name="ssm_param_bwd", out_shape=(gp, gp, jax.ShapeDtypeStruct(log_dt.shape, F32), gb, gb))(
        a_re, a_im, log_dt, b_re_t, b_im_t, abar_re, abar_im, e_re, e_im, ga_re, ga_im, gbb_re_t, gbb_im_t)


def _block_diag(blocks_re, blocks_im, sign_im, rows_are_channels):
    both = jnp.stack([blocks_re, sign_im * blocks_im]).reshape(2, SSM_CHUNKS, 8, SSM_GROUP, SSM_STATE)
    eye = jnp.eye(8, dtype=F32)
    if rows_are_channels:
        return jnp.einsum("rcghp,gk->cghrkp", both, eye).reshape(SSM_CHUNKS, 128, 2 * CHUNK_STATES)
    return jnp.einsum("rcghp,gk->crkpgh", both, eye).reshape(SSM_CHUNKS, 2 * CHUNK_STATES, 128)


def _diagonal_blocks(mat, part):
    first = [part * CHUNK_STATES + SSM_STATE * g for g in range(8)]
    return jnp.concatenate([mat[SSM_GROUP * g:SSM_GROUP * (g + 1), first[g]:first[g] + SSM_STATE] for g in range(8)], axis=0)


def _scan_consts(a_ref, conj, reverse):
    ar = jnp.broadcast_to(a_ref[:, :CHUNK_STATES], (SCAN_ROWS, CHUNK_STATES))
    ai = jnp.broadcast_to(a_ref[:, CHUNK_STATES:], (SCAN_ROWS, CHUNK_STATES))
    if conj:
        ai = -ai
    row = lax.broadcasted_iota(jnp.int32, (SCAN_ROWS, CHUNK_STATES), 0)
    if reverse:
        row = SCAN_ROWS - 1 - row
    zero = jnp.zeros_like(ar)
    steps = []
    pr, pi = ar, ai
    for shift in (1, 2, 4):
        keep = row >= shift
        steps.append((SCAN_ROWS - shift if reverse else shift, jnp.where(keep, pr, zero), jnp.where(keep, pi, zero)))
        pr, pi = _cmul(pr, pi, pr, pi)
    first = row == 0
    return steps, (jnp.where(first, ar, zero), jnp.where(first, ai, zero)), first


def _scan_tile(xr, xi, prev_r, prev_i, steps, carry_in, reverse):
    edge = SCAN_ROWS - 1 if reverse else 1
    cr, ci = pltpu.roll(prev_r, edge, axis=0), pltpu.roll(prev_i, edge, axis=0)
    xr, xi = xr + carry_in[0] * cr - carry_in[1] * ci, xi + carry_in[0] * ci + carry_in[1] * cr
    for shift, mr, mi in steps:
        sr, si = pltpu.roll(xr, shift, axis=0), pltpu.roll(xi, shift, axis=0)
        xr, xi = xr + mr * sr - mi * si, xi + mr * si + mi * sr
    return xr, xi


MM_ROWS = 256


def _ssm_fwd(proj, bmat, cmat, a_chunks, d_skip, ride=None):
    def body(u_ref, b_ref, c_ref, a_ref, d_ref, y_ref, states_ref, h_ref):
        for i in range(SEQ // MM_ROWS):
            rows = pl.ds(i * MM_ROWS, MM_ROWS)
            h_ref[rows, :] = _dot(u_ref[rows, :].astype(BF16), b_ref[...])
        steps, carry_in, _ = _scan_consts(a_ref, conj=False, reverse=False)

        def tile(k, carry):
            rows = pl.ds(pl.multiple_of(k * SCAN_ROWS, SCAN_ROWS), SCAN_ROWS)
            xr, xi = _scan_tile(h_ref[rows, :CHUNK_STATES], h_ref[rows, CHUNK_STATES:], carry[0], carry[1], steps, carry_in, False)
            h_ref[rows, :CHUNK_STATES] = xr
            h_ref[rows, CHUNK_STATES:] = xi
            return xr, xi

        zero = jnp.zeros((SCAN_ROWS, CHUNK_STATES), F32)
        lax.fori_loop(0, SEQ // SCAN_ROWS, tile, (zero, zero), unroll=4)
        for i in range(SEQ // MM_ROWS):
            rows = pl.ds(i * MM_ROWS, MM_ROWS)
            states = h_ref[rows, :].astype(BF16)
            states_ref[rows, :] = states
            y_ref[rows, :] = _dot(states, c_ref[...]) + d_ref[...] * u_ref[rows, :]

    return _call(
        body, "ssm_fwd", (SSM_CHUNKS,),
        [pl.BlockSpec((SEQ, 128), lambda c: (0, U_COL + c)),
         pl.BlockSpec((None, 128, 2 * CHUNK_STATES), lambda c: (c, 0, 0)),
         pl.BlockSpec((None, 2 * CHUNK_STATES, 128), lambda c: (c, 0, 0)),
         pl.BlockSpec((None, 1, 2 * CHUNK_STATES), lambda c: (c, 0, 0)),
         pl.BlockSpec((1, 128), lambda c: (0, c))],
        [pl.BlockSpec((SEQ, 128), lambda c: (0, c)), pl.BlockSpec((SEQ, 2 * CHUNK_STATES), lambda c: (0, c))],
        [_sds((SEQ, SSM_WIDTH), F32), _sds((SEQ, SSM_CHUNKS * 2 * CHUNK_STATES), BF16)],
        [pltpu.VMEM((SEQ, 2 * CHUNK_STATES), F32)],
        [proj, bmat, cmat, a_chunks, d_skip], ride)


def _ssm_bwd(dys, proj, h, bmat, cmat, a_chunks, d_skip, dproj, ride=None):
    def body(dy_ref, u_ref, states_ref, b_ref, c_ref, a_ref, d_ref, dproj_in, du_ref, db_re_ref, db_im_ref, dc_re_ref, dc_im_ref,
             da_ref, dd_ref, g_ref, h_ref):
        del dproj_in
        dsum = jnp.zeros((1, 128), F32)
        dcm = jnp.zeros((128, 2 * CHUNK_STATES), F32)
        for i in range(SEQ // MM_ROWS):
            rows = pl.ds(i * MM_ROWS, MM_ROWS)
            h_ref[rows, :] = states_ref[rows, :].astype(F32)
            dy = dy_ref[rows, :]
            g_ref[rows, :] = _dot_nt(dy.astype(BF16), c_ref[...])
            dsum += jnp.sum(dy * u_ref[rows, :], axis=0, keepdims=True)
            dcm += _dot_tn(dy.astype(BF16), states_ref[rows, :])
        dd_ref[...] = dsum
        dc_re_ref[...] = _diagonal_blocks(dcm, 0)
        dc_im_ref[...] = -_diagonal_blocks(dcm, 1)
        steps, carry_in, _ = _scan_consts(a_ref, conj=True, reverse=True)
        first_row = lax.broadcasted_iota(jnp.int32, (SCAN_ROWS, CHUNK_STATES), 0) == 0
        n_tiles = SEQ // SCAN_ROWS

        def tile(j, carry):
            k = n_tiles - 1 - j
            rows = pl.ds(pl.multiple_of(k * SCAN_ROWS, SCAN_ROWS), SCAN_ROWS)
            before = pl.ds(pl.multiple_of(jnp.maximum(k - 1, 0) * SCAN_ROWS, SCAN_ROWS), SCAN_ROWS)
            gr, gi = _scan_tile(g_ref[rows, :CHUNK_STATES], g_ref[rows, CHUNK_STATES:], carry[0], carry[1], steps, carry_in, True)
            g_ref[rows, :CHUNK_STATES] = gr
            g_ref[rows, CHUNK_STATES:] = gi
            has_before = jnp.where(k > 0, 1.0, 0.0)
            hr = jnp.where(first_row, pltpu.roll(h_ref[before, :CHUNK_STATES], 1, axis=0) * has_before,
                           pltpu.roll(h_ref[rows, :CHUNK_STATES], 1, axis=0))
            hi = jnp.where(first_row, pltpu.roll(h_ref[before, CHUNK_STATES:], 1, axis=0) * has_before,
                           pltpu.roll(h_ref[rows, CHUNK_STATES:], 1, axis=0))
            return gr, gi, carry[2] + hr * gr + hi * gi, carry[3] + hr * gi - hi * gr

        zero = jnp.zeros((SCAN_ROWS, CHUNK_STATES), F32)
        _, _, sar, sai = lax.fori_loop(0, n_tiles, tile, (zero, zero, zero, zero), unroll=4)
        da_ref[:, :CHUNK_STATES] = jnp.sum(sar, axis=0, keepdims=True)
        da_ref[:, CHUNK_STATES:] = jnp.sum(sai, axis=0, keepdims=True)
        dbm = jnp.zeros((128, 2 * CHUNK_STATES), F32)
        for i in range(SEQ // MM_ROWS):
            rows = pl.ds(i * MM_ROWS, MM_ROWS)
            g = g_ref[rows, :].astype(BF16)
            du_ref[rows, :] = (_dot_nt(g, b_ref[...]) + d_ref[...] * dy_ref[rows, :]).astype(BF16)
            dbm += _dot_tn(u_ref[rows, :].astype(BF16), g)
        db_re_ref[...] = _diagonal_blocks(dbm, 0)
        db_im_ref[...] = _diagonal_blocks(dbm, 1)

    chunk_col = pl.BlockSpec((SEQ, 128), lambda c: (0, c))
    blocks = pl.BlockSpec((None, 128, SSM_STATE), lambda c: (c, 0, 0))
    return _call(
        body, "ssm_bwd", (SSM_CHUNKS,),
        [chunk_col,
         pl.BlockSpec((SEQ, 128), lambda c: (0, U_COL + c)),
         pl.BlockSpec((SEQ, 2 * CHUNK_STATES), lambda c: (0, c)),
         pl.BlockSpec((None, 128, 2 * CHUNK_STATES), lambda c: (c, 0, 0)),
         pl.BlockSpec((None, 2 * CHUNK_STATES, 128), lambda c: (c, 0, 0)),
         pl.BlockSpec((None, 1, 2 * CHUNK_STATES), lambda c: (c, 0, 0)),
         pl.BlockSpec((1, 128), lambda c: (0, c)), ANY],
        [pl.BlockSpec((SEQ, 128), lambda c: (0, U_COL + c)), blocks, blocks, blocks, blocks,
         pl.BlockSpec((None, 1, 2 * CHUNK_STATES), lambda c: (c, 0, 0)),
         pl.BlockSpec((1, 128), lambda c: (0, c))],
        [_sds((SEQ, IN_WIDTH), BF16)] + [_sds((SSM_CHUNKS, 128, SSM_STATE), F32)] * 4
        + [_sds((SSM_CHUNKS, 1, 2 * CHUNK_STATES), F32), _sds((1, SSM_WIDTH), F32)],
        [pltpu.VMEM((SEQ, 2 * CHUNK_STATES), F32)] * 2, [dys, proj, h, bmat, cmat, a_chunks, d_skip, dproj], ride, aliases={7: 0})


def _ssm_tables(abar_re, abar_im, bbar_re_t, bbar_im_t, c_re, c_im):
    bmat = _block_diag(bbar_re_t, bbar_im_t, 1.0, True).astype(BF16)
    cmat = _block_diag(c_re, c_im, -1.0, False).astype(BF16)
    a_chunks = jnp.concatenate([abar_re.reshape(SSM_CHUNKS, 1, CHUNK_STATES), abar_im.reshape(SSM_CHUNKS, 1, CHUNK_STATES)], axis=2)
    return bmat, cmat, a_chunks


GL_COL = (3 * QKV_WIDTH + SSM_WIDTH) // D_MODEL
GELU_C = math.sqrt(2.0 / math.pi)
GELU_A = 0.044715


def _sds(shape, dtype):
    return jax.ShapeDtypeStruct(shape, dtype)


def _gelu(x):
    t = jnp.tanh(GELU_C * (x + GELU_A * x * x * x))
    return 0.5 * x * (1.0 + t), t


def _gelu_grad(x, t):
    return 0.5 * (1.0 + t) + 0.5 * x * (1.0 - t * t) * GELU_C * (1.0 + 3.0 * GELU_A * x * x)


def _layer_norm(r, g, b):
    mu = jnp.mean(r, axis=-1, keepdims=True)
    xc = r - mu
    rstd = lax.rsqrt(jnp.mean(xc * xc, axis=-1, keepdims=True) + LN_EPS)
    xhat = xc * rstd
    return xhat * g + b, xhat, rstd


def _layer_norm_bwd(dy, xhat, rstd, g):
    dxhat = dy * g
    m1 = jnp.mean(dxhat, axis=-1, keepdims=True)
    m2 = jnp.mean(dxhat * xhat, axis=-1, keepdims=True)
    return rstd * (dxhat - m1 - xhat * m2)


def _proj(x, w_in, ride=None):
    tm, tn = 1024, 1792

    def body(x_ref, w_ref, o_ref):
        o_ref[...] = _dot(x_ref[...].astype(BF16), _side_by_side(w_ref))

    return _call(
        body, "proj", (SEQ // tm, IN_WIDTH // tn),
        [pl.BlockSpec((tm, D_MODEL), lambda i, j: (i, 0)), pl.BlockSpec((2, D_MODEL, tn // 2), lambda i, j: (j, 0, 0))],
        [pl.BlockSpec((tm, tn), lambda i, j: (i, j))], [_sds((SEQ, IN_WIDTH), F32)], [], [x, w_in], ride)


def _row_spec(tm, width, col=0):
    return pl.BlockSpec((tm, width), lambda i, col=col: (i, col))


def _full_spec(shape):
    return pl.BlockSpec(shape, lambda i: (0,) * len(shape))


def _weight_spec(shape):
    return pl.BlockSpec(shape, lambda i: (0,) * len(shape), pipeline_mode=pl.Buffered(1))


def _mixer_out(attn, ys, proj, x, w_ab, w_sb, w_glu, w_out, b_gate, ln_g, ln_b, ride=None):
    tm = 512

    def body(attn_ref, ys_ref, gl0_ref, gl1_ref, x_ref, wab_ref, wsb_ref, wglu_ref, wout_ref, bg_ref, g_ref, b_ref,
             h_ref, xhat_ref, rstd_ref, glu_ref, ya_ref, yssm_ref):
        gy, _ = _gelu(ys_ref[...])
        glu = _dot(gy.astype(BF16), _side_by_side(wglu_ref))
        glu_ref[...] = glu.astype(BF16)
        y_s = glu[:, :SSM_WIDTH] * jax.nn.sigmoid(glu[:, SSM_WIDTH:])
        y_ssm = _dot(y_s.astype(BF16), _side_by_side(wsb_ref))
        y_attn = _dot(attn_ref[...].astype(BF16), _side_by_side(wab_ref))
        ya_ref[...] = y_attn.astype(BF16)
        yssm_ref[...] = y_ssm.astype(BF16)
        g0 = jax.nn.sigmoid(gl0_ref[...] + _side_by_side(bg_ref, 0))
        g1 = jax.nn.sigmoid(gl1_ref[...] + _side_by_side(bg_ref, 1))
        mixed = g0 * y_attn + g1 * y_ssm
        r1 = DN_ALPHA * x_ref[...] + _dot(mixed.astype(BF16), wout_ref[...])
        h, xhat, rstd = _layer_norm(r1, g_ref[...], b_ref[...])
        h_ref[...] = h
        xhat_ref[...] = xhat
        rstd_ref[...] = jnp.broadcast_to(rstd, (tm, 128))

    wide = _sds((SEQ, D_MODEL), F32)
    return _call(
        body, "mixer_out", (SEQ // tm,),
        [_row_spec(tm, ATTN_WIDTH), _row_spec(tm, SSM_WIDTH), _row_spec(tm, D_MODEL, GL_COL), _row_spec(tm, D_MODEL, GL_COL + 1),
         _row_spec(tm, D_MODEL), _weight_spec((N_DEV, ATTN_WIDTH, 128)), _weight_spec((N_DEV, SSM_WIDTH, 128)),
         _weight_spec((N_DEV, SSM_WIDTH, 128)), _weight_spec((D_MODEL, D_MODEL)), _full_spec((N_DEV, 2, 128)),
         _full_spec((1, D_MODEL)), _full_spec((1, D_MODEL))],
        [_row_spec(tm, D_MODEL), _row_spec(tm, D_MODEL), _row_spec(tm, 128), _row_spec(tm, D_MODEL),
         _row_spec(tm, D_MODEL), _row_spec(tm, D_MODEL)],
        [wide, wide, _sds((SEQ, 128), F32)] + [_sds((SEQ, D_MODEL), BF16)] * 3, [],
        [attn, ys, proj, proj, x, w_ab, w_sb, w_glu, w_out, b_gate, ln_g, ln_b], ride)


def _ff_up(h, w_gate, w_up, ride=None):
    tm, tn = 1024, 768

    def body(h_ref, wg_ref, wu_ref, a_ref, b_ref, f_ref):
        hb = h_ref[...].astype(BF16)
        a, b = _dot(hb, _side_by_side(wg_ref)), _dot(hb, _side_by_side(wu_ref))
        a_ref[...] = a.astype(BF16)
        b_ref[...] = b.astype(BF16)
        f_ref[...] = (a * jax.nn.sigmoid(a) * b).astype(BF16)

    tile = pl.BlockSpec((tm, tn), lambda i, j: (i, j))
    wtile = pl.BlockSpec((tn // FF_PAD, D_MODEL, FF_PAD), lambda i, j: (j, 0, 0))
    out = _sds((SEQ, D_FF_PAD), BF16)
    return _call(body, "ff_up", (SEQ // tm, D_FF_PAD // tn), [pl.BlockSpec((tm, D_MODEL), lambda i, j: (i, 0)), wtile, wtile],
                 [tile, tile, tile], [out, out, out], [], [h, w_gate, w_up], ride)


def _ff_down_loss(f, w_down, h, target, ln_g, ln_b):
    tm = 512

    def body(f_ref, w_ref, h_ref, t_ref, g_ref, b_ref, dr_ref, dg_ref, db_ref, loss_ref):
        @pl.when(pl.program_id(0) == 0)
        def _():
            dg_ref[...] = jnp.zeros_like(dg_ref)
            db_ref[...] = jnp.zeros_like(db_ref)
            loss_ref[...] = jnp.zeros_like(loss_ref)

        r2 = DN_ALPHA * h_ref[...] + _dot(f_ref[...], w_ref[...])
        g = g_ref[...]
        out, xhat, rstd = _layer_norm(r2, g, b_ref[...])
        err = out - t_ref[...]
        loss_ref[...] += 0.5 * jnp.sum(jnp.mean(err * err, axis=-1, keepdims=True), axis=0, keepdims=True)
        dout = err * (1.0 / D_MODEL)
        dg_ref[...] += jnp.sum(dout * xhat, axis=0, keepdims=True)
        db_ref[...] += jnp.sum(dout, axis=0, keepdims=True)
        dr_ref[...] = _layer_norm_bwd(dout, xhat, rstd, g)

    vec = _sds((1, D_MODEL), F32)
    return _pallas_call(
        body, name="ff_down_loss", grid=(SEQ // tm,),
        in_specs=[_row_spec(tm, D_FF_PAD), _weight_spec((D_FF_PAD, D_MODEL)), _row_spec(tm, D_MODEL), _row_spec(tm, D_MODEL),
                  _full_spec((1, D_MODEL)), _full_spec((1, D_MODEL))],
        out_specs=(_row_spec(tm, D_MODEL), _full_spec((1, D_MODEL)), _full_spec((1, D_MODEL)), _full_spec((1, 128))),
        out_shape=(_sds((SEQ, D_MODEL), F32), vec, vec, _sds((1, 128), F32)),
        compiler_params=_cparams(dimension_semantics=("arbitrary",)),
    )(f, w_down, h, target, ln_g, ln_b)


def _ff_down_bwd(dr2, w_down, a, b):
    tm, tn = 1024, 768

    def body(dr_ref, w_ref, a_ref, b_ref, da_ref, db_ref):
        df = _dot_nt(dr_ref[...].astype(BF16), w_ref[...])
        av, bv = a_ref[...].astype(F32), b_ref[...].astype(F32)
        sg = jax.nn.sigmoid(av)
        da_ref[...] = (df * bv * sg * (1.0 + av * (1.0 - sg))).astype(BF16)
        db_ref[...] = (df * av * sg).astype(BF16)

    tile = pl.BlockSpec((tm, tn), lambda i, j: (i, j))
    out = _sds((SEQ, D_FF_PAD), BF16)
    return _pallas_call(
        body, name="ff_down_bwd", grid=(SEQ // tm, D_FF_PAD // tn),
        in_specs=[pl.BlockSpec((tm, D_MODEL), lambda i, j: (i, 0)), pl.BlockSpec((tn, D_MODEL), lambda i, j: (j, 0)), tile, tile],
        out_specs=(tile, tile), out_shape=(out, out),
        compiler_params=_cparams(dimension_semantics=("arbitrary", "arbitrary")),
    )(dr2, w_down, a, b)


def _ff_up_bwd(da, db, w_gate, w_up, dr2, xhat1, rstd1, ln_g, ride=None):
    tm, tk = 1024, 768
    nk = D_FF_PAD // tk

    def body(da_ref, db_ref, wg_ref, wu_ref, dr2_ref, xhat_ref, rstd_ref, g_ref, dr1_ref, dg_ref, dbias_ref, acc):
        i, k = pl.program_id(0), pl.program_id(1)

        @pl.when(jnp.logical_and(i == 0, k == 0))
        def _():
            dg_ref[...] = jnp.zeros_like(dg_ref)
            dbias_ref[...] = jnp.zeros_like(dbias_ref)

        part = _dot_nt(da_ref[...], _side_by_side(wg_ref)) + _dot_nt(db_ref[...], _side_by_side(wu_ref))

        @pl.when(k == 0)
        def _():
            acc[...] = part

        @pl.when(k > 0)
        def _():
            acc[...] += part

        @pl.when(k == nk - 1)
        def _():
            dh = DN_ALPHA * dr2_ref[...] + acc[...]
            xhat = xhat_ref[...]
            dg_ref[...] += jnp.sum(dh * xhat, axis=0, keepdims=True)
            dbias_ref[...] += jnp.sum(dh, axis=0, keepdims=True)
            rstd = jnp.max(rstd_ref[...], axis=1, keepdims=True)
            dr1_ref[...] = _layer_norm_bwd(dh, xhat, rstd, g_ref[...])

    hid = pl.BlockSpec((tm, tk), lambda i, k: (i, k))
    wtile = pl.BlockSpec((tk // FF_PAD, D_MODEL, FF_PAD), lambda i, k: (k, 0, 0))
    row = pl.BlockSpec((tm, D_MODEL), lambda i, k: (i, 0))
    vec = pl.BlockSpec((1, D_MODEL), lambda i, k: (0, 0))
    return _call(
        body, "ff_up_bwd", (SEQ // tm, nk),
        [hid, hid, wtile, wtile, row, row, pl.BlockSpec((tm, 128), lambda i, k: (i, 0)), vec],
        [row, vec, vec], [_sds((SEQ, D_MODEL), F32), _sds((1, D_MODEL), F32), _sds((1, D_MODEL), F32)],
        [pltpu.VMEM((tm, D_MODEL), F32)], [da, db, w_gate, w_up, dr2, xhat1, rstd1, ln_g], ride)


def _mixer_bwd(dr1, proj, y_attn, y_ssm, glu, ys, w_ab, w_sb, w_glu, w_out, b_gate):
    tm = 256

    def body(dr1_ref, gl0_ref, gl1_ref, ya_ref, yssm_ref, glu_ref, ys_ref, wab_ref, wsb_ref, wglu_ref, wout_ref, bg_ref,
             dya_ref, dyssm_ref, dgl_ref, dattn_ref, dglu_ref, dys_ref, mixed_ref, ysb_ref, gy_ref, dbg_ref, stage, copied):
        @pl.when(pl.program_id(0) == 0)
        def _():
            dbg_ref[...] = jnp.zeros_like(dbg_ref)

        dmixed = _dot_nt(dr1_ref[...].astype(BF16), wout_ref[...])
        g0 = jax.nn.sigmoid(gl0_ref[...] + _side_by_side(bg_ref, 0))
        g1 = jax.nn.sigmoid(gl1_ref[...] + _side_by_side(bg_ref, 1))
        y_attn, y_ssm = ya_ref[...].astype(F32), yssm_ref[...].astype(F32)
        mixed_ref[...] = (g0 * y_attn + g1 * y_ssm).astype(BF16)
        dya = (dmixed * g0).astype(BF16)
        dyssm = (dmixed * g1).astype(BF16)
        dya_ref[...] = dya
        dyssm_ref[...] = dyssm
        dgl0 = dmixed * y_attn * g0 * (1.0 - g0)
        dgl1 = dmixed * y_ssm * g1 * (1.0 - g1)
        i, last = pl.program_id(0), SEQ // tm - 1
        slot = i & 1

        def copy_out(buffer, tile):
            window = dgl_ref.at[pl.ds(pl.multiple_of(tile * tm, tm), tm), pl.ds(GL_COL * D_MODEL, 2 * D_MODEL)]
            return pltpu.make_async_copy(stage.at[buffer], window, copied.at[buffer])

        @pl.when(i >= 2)
        def _():
            copy_out(slot, i - 2).wait()

        stage[slot, :, :D_MODEL] = dgl0.astype(BF16)
        stage[slot, :, D_MODEL:] = dgl1.astype(BF16)
        copy_out(slot, i).start()

        @pl.when(i == last)
        def _():
            copy_out(1 - slot, i - 1).wait()
            copy_out(slot, i).wait()
        dbg_ref[:, :D_MODEL] += jnp.sum(dgl0, axis=0, keepdims=True)
        dbg_ref[:, D_MODEL:] += jnp.sum(dgl1, axis=0, keepdims=True)
        dattn_ref[...] = _dot_nt(dya, _side_by_side(wab_ref))
        dy_s = _dot_nt(dyssm, _side_by_side(wsb_ref))
        glu = glu_ref[...].astype(F32)
        glu1, sg = glu[:, :SSM_WIDTH], jax.nn.sigmoid(glu[:, SSM_WIDTH:])
        ysb_ref[...] = (glu1 * sg).astype(BF16)
        dglu1 = (dy_s * sg).astype(BF16)
        dglu2 = (dy_s * glu1 * sg * (1.0 - sg)).astype(BF16)
        dglu_ref[:, :SSM_WIDTH] = dglu1
        dglu_ref[:, SSM_WIDTH:] = dglu2
        dgy = _dot_nt(jnp.concatenate([dglu1, dglu2], axis=1), _side_by_side(wglu_ref))
        ys = ys_ref[...]
        gy, t = _gelu(ys)
        gy_ref[...] = gy.astype(BF16)
        dys_ref[...] = dgy * _gelu_grad(ys, t)

    wide_b, half_b = _sds((SEQ, D_MODEL), BF16), _sds((SEQ, SSM_WIDTH), BF16)
    half_f = _sds((SEQ, SSM_WIDTH), F32)
    return _pallas_call(
        body, name="mixer_bwd", grid=(SEQ // tm,),
        in_specs=[_row_spec(tm, D_MODEL), _row_spec(tm, D_MODEL, GL_COL), _row_spec(tm, D_MODEL, GL_COL + 1), _row_spec(tm, D_MODEL),
                  _row_spec(tm, D_MODEL), _row_spec(tm, D_MODEL), _row_spec(tm, SSM_WIDTH), _full_spec((N_DEV, ATTN_WIDTH, 128)),
                  _full_spec((N_DEV, SSM_WIDTH, 128)), _full_spec((N_DEV, SSM_WIDTH, 128)), _full_spec((D_MODEL, D_MODEL)),
                  _full_spec((N_DEV, 2, 128))],
        out_specs=(_row_spec(tm, D_MODEL), _row_spec(tm, D_MODEL), ANY, _row_spec(tm, ATTN_WIDTH),
                   _row_spec(tm, D_MODEL), _row_spec(tm, SSM_WIDTH), _row_spec(tm, D_MODEL), _row_spec(tm, SSM_WIDTH),
                   _row_spec(tm, SSM_WIDTH), _full_spec((1, 2 * D_MODEL))),
        out_shape=(wide_b, wide_b, _sds((SEQ, IN_WIDTH), BF16), half_f, wide_b, half_f, wide_b, half_b, half_b,
                   _sds((1, 2 * D_MODEL), F32)),
        scratch_shapes=[pltpu.VMEM((2, tm, 2 * D_MODEL), BF16), pltpu.SemaphoreType.DMA((2,))],
        compiler_params=_cparams(dimension_semantics=("arbitrary",)),
    )(dr1, proj, proj, y_attn, y_ssm, glu, ys, w_ab, w_sb, w_glu, w_out, b_gate)


def _grad_x(dproj, w_in, dr1, ride=None):
    tm, tk = 1024, 1792
    nk = IN_WIDTH // tk

    def body(dp_ref, w_ref, dr1_ref, o_ref, acc):
        k = pl.program_id(1)
        part = _dot_nt(dp_ref[...], _side_by_side(w_ref))

        @pl.when(k == 0)
        def _():
            acc[...] = part

        @pl.when(k > 0)
        def _():
            acc[...] += part

        @pl.when(k == nk - 1)
        def _():
            o_ref[...] = DN_ALPHA * dr1_ref[...] + acc[...]

    row = pl.BlockSpec((tm, D_MODEL), lambda i, k: (i, 0))
    return _call(
        body, "grad_x", (SEQ // tm, nk),
        [pl.BlockSpec((tm, tk), lambda i, k: (i, k)), pl.BlockSpec((2, D_MODEL, tk // 2), lambda i, k: (k, 0, 0)), row],
        [row], [_sds((SEQ, D_MODEL), F32)], [pltpu.VMEM((tm, D_MODEL), F32)], [dproj, w_in, dr1], ride)


def _weight_grad(a, b, name, shard_cols=None):
    k, n = a.shape[1], b.shape[1]
    tk = k if shard_cols else k // N_DEV
    tn = n // 4 if shard_cols else min(n, 1024)

    def body(a_ref, b_ref, o_ref):
        grad = _dot_tn(a_ref[...].astype(BF16), b_ref[...].astype(BF16))
        if shard_cols:
            o_ref[0] = grad[:, :shard_cols].astype(BF16)
            o_ref[1] = grad[:, shard_cols:].astype(BF16)
        else:
            o_ref[...] = grad.astype(BF16)

    if shard_cols:
        out_spec = pl.BlockSpec((2, None, tk, shard_cols), lambda kk, j: (0, j, kk, 0))
        out_shape = _sds((2, 4, k, shard_cols), BF16)
    else:
        out_spec = pl.BlockSpec((None, None, tk, tn), lambda kk, j: (kk % 2, kk // 2, 0, j))
        out_shape = _sds((2, 4, tk, n), BF16)
    return _call(body, name, (k // tk, n // tn),
                 [pl.BlockSpec((SEQ, tk), lambda kk, j: (0, kk)), pl.BlockSpec((SEQ, tn), lambda kk, j: (0, j))],
                 [out_spec], [out_shape], [], [a, b])[0]


def _weight_grad_rows(a, b, core, name, shard_cols, first, count, ride, in_place=False):
    k = a.shape[1]

    def body(a_ref, b_ref, o_ref):
        o_ref[...] = _dot_tn(a_ref[...].astype(BF16), b_ref[...].astype(BF16)).astype(BF16)

    def shard(j, core_ref):
        row = j + first
        return 0, jnp.where(row < 4, 2 * row + 1 - core_ref[0], 2 * (row - 4) + core_ref[0])

    return _call(body, name, (count,),
                 [pl.BlockSpec((SEQ, k), lambda j, core_ref: (0, 0), pipeline_mode=pl.Buffered(1)),
                  pl.BlockSpec((SEQ, shard_cols), shard)],
                 [pl.BlockSpec((None, k, shard_cols), lambda j, core_ref: (j + first, 0, 0))],
                 [_sds((N_DEV, k, shard_cols), BF16)], [], [a, b], ride, aliases={2: 0} if in_place else None, prefetch=core)


MESH = pl.DeviceIdType.MESH
ANY = pl.BlockSpec(memory_space=pl.ANY)


def _place():
    return lax.axis_index("x"), lax.axis_index("y"), lax.axis_index("c")


def _other_chips(x, y):
    return [(1 - x, y), (x, 1 - y), (1 - x, 1 - y)]


class _Ride:
    def __init__(self, operands, results, aliases, sems, start, wait):
        self.operands, self.results, self.aliases, self.sems = list(operands), list(results), dict(aliases), list(sems)
        self.start, self.wait = start, wait

    def __add__(self, other):
        n_in, n_out, n_sem = len(self.operands), len(self.results), len(self.sems)

        def both(which):
            def run(ins, outs, sems):
                getattr(self, which)(ins[:n_in], outs[:n_out], sems[:n_sem])
                getattr(other, which)(ins[n_in:], outs[n_out:], sems[n_sem:])
            return run

        aliases = {**self.aliases, **{n_in + i: n_out + j for i, j in other.aliases.items()}}
        return _Ride(self.operands + other.operands, self.results + other.results, aliases, self.sems + other.sems,
                     both("start"), both("wait"))


def _call(body, name, grid, in_specs, out_specs, out_shape, scratch_shapes, operands, ride=None, aliases=None, prefetch=None):
    in_specs, out_specs, out_shape = list(in_specs), list(out_specs), list(out_shape)
    scratch_shapes, operands, aliases = list(scratch_shapes), list(operands), dict(aliases or {})
    kernel_body = body
    if ride is not None:
        n_in, n_out, n_scr, r_in, r_out = len(in_specs), len(out_specs), len(scratch_shapes), len(ride.operands), len(ride.results)

        def kernel_body(*refs):
            out0, scr0 = n_in + r_in, n_in + r_in + n_out + r_out
            ride_refs = (refs[n_in:out0], refs[out0 + n_out:scr0], refs[scr0 + n_scr:])
            ids = [pl.program_id(i) for i in range(len(grid))]
            first = functools.reduce(jnp.logical_and, [i == 0 for i in ids])
            last = functools.reduce(jnp.logical_and, [i == g - 1 for i, g in zip(ids, grid)])

            @pl.when(first)
            def _():
                ride.start(*ride_refs)

            body(*refs[:n_in], *refs[out0:out0 + n_out], *refs[scr0:scr0 + n_scr])

            @pl.when(last)
            def _():
                ride.wait(*ride_refs)

        aliases.update({n_in + i: n_out + j for i, j in ride.aliases.items()})
        in_specs += [ANY] * r_in
        out_specs += [ANY] * r_out
        out_shape += ride.results
        scratch_shapes += ride.sems
        operands += ride.operands
    params = _cparams(dimension_semantics=("arbitrary",) * len(grid))
    if prefetch is None:
        return _pallas_call(
            kernel_body, name=name, grid=grid, in_specs=in_specs, out_specs=out_specs, out_shape=out_shape,
            scratch_shapes=scratch_shapes, input_output_aliases=aliases, compiler_params=params,
        )(*operands)

    def with_prefetch(prefetch_ref, *refs):
        kernel_body(*refs)

    return _pallas_call(
        with_prefetch, name=name,
        grid_spec=pltpu.PrefetchScalarGridSpec(num_scalar_prefetch=1, grid=grid, in_specs=in_specs, out_specs=out_specs,
                                               scratch_shapes=scratch_shapes),
        out_shape=out_shape, input_output_aliases={i + 1: j for i, j in aliases.items()}, compiler_params=params,
    )(prefetch, *operands)


def _after(*arrays):
    return _Ride(arrays, [], {}, [], lambda *refs: None, lambda *refs: None)


def _gather_first_level(shards):
    n = len(shards)

    def copies(ins, outs, sems, landed):
        send_sems, recv_sems, local_sems = sems
        x, y, c = _place()
        peers = [(x, y, 1 - c)] + [(px, py, c) for px, py in _other_chips(x, y)]

        def row(peer):
            return 4 * x + 2 * y + c if not landed else 4 * peer[0] + 2 * peer[1] + peer[2]

        local = [pltpu.make_async_copy(ins[a], outs[a].at[4 * x + 2 * y + c], local_sems.at[a]) for a in range(n)]
        remote = [pltpu.make_async_remote_copy(
            src_ref=ins[a], dst_ref=outs[a].at[row(peer)], send_sem=send_sems.at[a, k], recv_sem=recv_sems.at[a, k],
            device_id=peer, device_id_type=MESH) for a in range(n) for k, peer in enumerate(peers)]
        return local, remote

    def start(ins, outs, sems):
        local, remote = copies(ins, outs, sems, False)
        for cp in local + remote:
            cp.start()

    def wait(ins, outs, sems):
        local, sent = copies(ins, outs, sems, False)
        for cp in copies(ins, outs, sems, True)[1]:
            cp.wait_recv()
        for cp in sent:
            cp.wait_send()
        for cp in local:
            cp.wait()

    return _Ride(shards, [_sds((N_DEV,) + s.shape, s.dtype) for s in shards], {},
                 [pltpu.SemaphoreType.DMA((n, 4)), pltpu.SemaphoreType.DMA((n, 4)), pltpu.SemaphoreType.DMA((n,))], start, wait)


def _gather_second_level(buffers):
    n = len(buffers)

    def copies(outs, sems, core):
        send_sems, recv_sems = sems
        x, y, c = _place()
        return [pltpu.make_async_remote_copy(
            src_ref=outs[a].at[4 * px + 2 * py + core], dst_ref=outs[a].at[4 * px + 2 * py + core], send_sem=send_sems.at[a, j],
            recv_sem=recv_sems.at[a, j], device_id=(x, y, 1 - c), device_id_type=MESH)
            for a in range(n) for j, (px, py) in enumerate(_other_chips(x, y))]

    def start(ins, outs, sems):
        for cp in copies(outs, sems, lax.axis_index("c")):
            cp.start()

    def wait(ins, outs, sems):
        for cp in copies(outs, sems, 1 - lax.axis_index("c")):
            cp.wait_recv()
        for cp in copies(outs, sems, lax.axis_index("c")):
            cp.wait_send()

    return _Ride(buffers, [_sds(b.shape, b.dtype) for b in buffers], {i: i for i in range(n)},
                 [pltpu.SemaphoreType.DMA((n, 3)), pltpu.SemaphoreType.DMA((n, 3))], start, wait)


def _relayed_gather(shards):
    n = len(shards)
    buffers = [_sds((N_DEV,) + s.shape, s.dtype) for s in shards]
    dma = pltpu.SemaphoreType.DMA

    def remote(src, dst, send_sem, recv_sem, to):
        return pltpu.make_async_remote_copy(src_ref=src, dst_ref=dst, send_sem=send_sem, recv_sem=recv_sem,
                                            device_id=to, device_id_type=MESH)

    def row(px, py, pc):
        return 4 * px + 2 * py + pc

    def ride(operands, aliases, sems, copies):
        def start(ins, outs, sem_refs):
            local, sent = copies(ins, outs, sem_refs, False)
            for cp in local + sent:
                cp.start()

        def wait(ins, outs, sem_refs):
            local, sent = copies(ins, outs, sem_refs, False)
            for cp in copies(ins, outs, sem_refs, True)[1]:
                cp.wait_recv()
            for cp in sent:
                cp.wait_send()
            for cp in local:
                cp.wait()

        return _Ride(operands, buffers, aliases, sems, start, wait)

    def first(ins, outs, sems, landed):
        x, y, c = _place()
        peers = [(x, y, 1 - c), (1 - x, y, c), (x, 1 - y, c)]
        local = [pltpu.make_async_copy(ins[a], outs[a].at[row(x, y, c)], sems[2].at[a]) for a in range(n)]
        return local, [remote(ins[a], outs[a].at[row(*peer) if landed else row(x, y, c)], sems[0].at[a, k], sems[1].at[a, k], peer)
                       for a in range(n) for k, peer in enumerate(peers)]

    def second(ins, outs, sems, landed):
        x, y, c = _place()
        mine = 1 - c if landed else c
        copies = []
        for a in range(n):
            half = shards[a].shape[0] // 2
            over_x, over_y, diagonal = outs[a].at[row(1 - x, y, mine)], outs[a].at[row(x, 1 - y, mine)], outs[a].at[row(1 - x, 1 - y, c)]
            lower, upper = pl.ds(0, half), pl.ds(half, half)
            copies += [remote(over_x, over_x, sems[0].at[a, 0], sems[1].at[a, 0], (x, y, 1 - c)),
                       remote(over_y, over_y, sems[0].at[a, 1], sems[1].at[a, 1], (x, y, 1 - c))]
            if landed:
                copies += [remote(diagonal.at[lower], diagonal.at[lower], sems[0].at[a, 2], sems[1].at[a, 2], (1 - x, y, c)),
                           remote(diagonal.at[upper], diagonal.at[upper], sems[0].at[a, 3], sems[1].at[a, 3], (x, 1 - y, c))]
            else:
                copies += [remote(over_y.at[lower], over_y.at[lower], sems[0].at[a, 2], sems[1].at[a, 2], (1 - x, y, c)),
                           remote(over_x.at[upper], over_x.at[upper], sems[0].at[a, 3], sems[1].at[a, 3], (x, 1 - y, c))]
        return [], copies

    def third(ins, outs, sems, landed):
        x, y, c = _place()
        return [], [remote(outs[a].at[row(1 - x, 1 - y, 1 - c if landed else c)], outs[a].at[row(1 - x, 1 - y, 1 - c if landed else c)],
                           sems[0].at[a], sems[1].at[a], (x, y, 1 - c)) for a in range(n)]

    def later(copies, n_sems):
        return lambda partly: ride(partly, {i: i for i in range(n)}, [dma((n,) + n_sems), dma((n,) + n_sems)], copies)

    return ride(shards, {}, [dma((n, 3)), dma((n, 3)), dma((n,))], first), later(second, (4,)), later(third, ())


def _sibling_swap_ride(grads, halves=True):
    n = len(grads)

    def copies(ins, outs, sems):
        x, y, c = _place()
        return [pltpu.make_async_remote_copy(
            src_ref=(ins[a].at[1 - c] if halves else ins[a]).at[p], dst_ref=outs[a].at[p], send_sem=sems[0].at[4 * a + p],
            recv_sem=sems[1].at[4 * a + p], device_id=(x, y, 1 - c), device_id_type=MESH) for a in range(n) for p in range(4)]

    def start(ins, outs, sems):
        for cp in copies(ins, outs, sems):
            cp.start()

    def wait(ins, outs, sems):
        for cp in copies(ins, outs, sems):
            cp.wait()

    return _Ride(grads, [_sds((4,) + g.shape[-2:], g.dtype) for g in grads], {},
                 [pltpu.SemaphoreType.DMA((4 * n,)), pltpu.SemaphoreType.DMA((4 * n,))], start, wait)


def _chip_swap_ride(sums):
    n = len(sums)

    def copies(ins, outs, sems, landed):
        send_sems, recv_sems, local_sems = sems
        x, y, c = _place()
        mine = 2 * x + y
        local = [pltpu.make_async_copy(ins[a].at[mine], outs[a].at[mine], local_sems.at[a]) for a in range(n)]
        remote = [pltpu.make_async_remote_copy(
            src_ref=ins[a].at[2 * px + py], dst_ref=outs[a].at[2 * px + py if landed else mine], send_sem=send_sems.at[a, j],
            recv_sem=recv_sems.at[a, j], device_id=(px, py, c), device_id_type=MESH)
            for a in range(n) for j, (px, py) in enumerate(_other_chips(x, y))]
        return local, remote

    def start(ins, outs, sems):
        local, remote = copies(ins, outs, sems, False)
        for cp in local + remote:
            cp.start()

    def wait(ins, outs, sems):
        local, sent = copies(ins, outs, sems, False)
        for cp in copies(ins, outs, sems, True)[1]:
            cp.wait_recv()
        for cp in sent:
            cp.wait_send()
        for cp in local:
            cp.wait()

    return _Ride(sums, [_sds(s.shape, s.dtype) for s in sums], {},
                 [pltpu.SemaphoreType.DMA((n, 3)), pltpu.SemaphoreType.DMA((n, 3)), pltpu.SemaphoreType.DMA((n,))], start, wait)


def _send_buffers(shards, name):
    n = len(shards)

    def body(*refs):
        for (w, transposed, rows, cols), w_ref, o_ref in zip(shards, refs[:n], refs[n:]):
            if transposed:
                c, r = w.shape
                padded = jnp.concatenate([w_ref[...], jnp.zeros((cols - c, r), F32)], axis=0) if cols > c else w_ref[...]
                o_ref[...] = padded.T.astype(BF16)
            else:
                r, c = w.shape
                if (r, c) != (rows, cols):
                    o_ref[...] = jnp.zeros((rows, cols), BF16)
                o_ref[:r, :c] = w_ref[...].astype(BF16)

    return _pallas_call(body, name=name, out_shape=[_sds((rows, cols), BF16) for _, _, rows, cols in shards])(
        *[w for w, _, _, _ in shards])


def _all_gather(shards, name):
    n = len(shards)
    first, second, third = _relayed_gather(shards)
    levels = [first, second(shards), third(shards)]
    counts = [len(level.sems) for level in levels]

    def body(*refs):
        ins, outs, sems = refs[:n], refs[n:2 * n], refs[2 * n:]
        for i, level in enumerate(levels):
            mine = sems[sum(counts[:i]):sum(counts[:i + 1])]
            level.start(ins, outs, mine)
            level.wait(ins, outs, mine)

    return _pallas_call(
        body, name=name, in_specs=[ANY] * n, out_specs=[ANY] * n, out_shape=first.results,
        scratch_shapes=[s for level in levels for s in level.sems],
    )(*shards)


HBM = pl.BlockSpec(memory_space=pltpu.HBM)
SEMAPHORES = pl.BlockSpec(memory_space=pltpu.SEMAPHORE)
IN_FLIGHT = pltpu.CompilerParams(has_side_effects=pltpu.SideEffectType.DATAFLOW_SIDE_EFFECTING)


def _chip_swap_copies(src_refs, land_refs, send_sems, recv_sems, landed):
    x, y, c = _place()
    return [pltpu.make_async_remote_copy(
        src_ref=src.at[2 * px + py], dst_ref=land.at[2 * px + py if landed else 2 * x + y], send_sem=send_sems.at[3 * a + j],
        recv_sem=recv_sems.at[3 * a + j], device_id=(px, py, c), device_id_type=MESH)
        for a, (src, land) in enumerate(zip(src_refs, land_refs)) for j, (px, py) in enumerate(_other_chips(x, y))]


def _chip_swap_start(sums, name):
    n = len(sums)

    def body(*refs):
        src_refs, land_refs, (send_sems, recv_sems), token = refs[:n], refs[n:2 * n], refs[2 * n:2 * n + 2], refs[-1]
        for cp in _chip_swap_copies(src_refs, land_refs, send_sems, recv_sems, False):
            cp.start()
        token[...] = jnp.zeros_like(token)

    kept = [pltpu.HBM(s.shape, s.dtype) for s in sums]
    out = _pallas_call(
        body, name=name,
        out_shape=[pltpu.SemaphoreType.DMA((3 * n,)), pltpu.SemaphoreType.DMA((3 * n,))] + kept + kept + [_sds((8, 128), F32)],
        in_specs=[HBM] * (2 * n), out_specs=[SEMAPHORES, SEMAPHORES] + [HBM] * (2 * n) + [pl.BlockSpec(memory_space=pltpu.VMEM)],
        input_output_aliases={i: 2 + i for i in range(2 * n)}, compiler_params=IN_FLIGHT,
    )(*[pltpu.with_memory_space_constraint(s, pltpu.HBM) for s in sums],
      *[pltpu.with_memory_space_constraint(lax.empty(s.shape, s.dtype), pltpu.HBM) for s in sums])
    return out[0], out[1], out[2:2 + n], out[2 + n:2 + 2 * n], out[-1]


def _chip_swap_wait(send_sems, recv_sems, sums, landings, after, name):
    n = len(sums)

    def body(*refs):
        src_refs, land_refs, (send_sems, recv_sems) = refs[:n], refs[n:2 * n], refs[2 * n:2 * n + 2]
        for cp in _chip_swap_copies(src_refs, land_refs, send_sems, recv_sems, False):
            cp.wait_send()
        for cp in _chip_swap_copies(src_refs, land_refs, send_sems, recv_sems, True):
            cp.wait_recv()

    out = _pallas_call(
        body, name=name, out_shape=[pltpu.HBM(s.shape, s.dtype) for s in list(sums) + list(landings)],
        in_specs=[HBM] * (2 * n) + [SEMAPHORES, SEMAPHORES] + [ANY] * len(after), out_specs=[HBM] * (2 * n),
        input_output_aliases={i: i for i in range(2 * n)}, compiler_params=IN_FLIGHT,
    )(*sums, *landings, send_sems, recv_sems, *after)
    return out[:n], out[n:]


def _pair_sums(gs, rs, core, name):
    n_arrays = len(gs)

    def body(core_ref, *refs):
        for g_ref, r_ref, o_ref in zip(refs[:n_arrays], refs[n_arrays:2 * n_arrays], refs[2 * n_arrays:]):
            o_ref[...] = (g_ref[...].astype(F32) + r_ref[...].astype(F32)).astype(o_ref.dtype)

    def chip(g):
        return pl.BlockSpec((None,) + g.shape[-2:], lambda p, core_ref: (p, 0, 0))

    def own(g):
        if g.ndim == 3:
            return pl.BlockSpec((None,) + g.shape[-2:], lambda p, core_ref: (p + 4, 0, 0))
        return pl.BlockSpec((None, None) + g.shape[2:], lambda p, core_ref: (core_ref[0], p, 0, 0))

    return _pallas_call(
        body, name=name,
        grid_spec=pltpu.PrefetchScalarGridSpec(
            num_scalar_prefetch=1, grid=(4,), in_specs=[own(g) for g in gs] + [chip(g) for g in gs],
            out_specs=[chip(g) for g in gs]),
        out_shape=[_sds((4,) + g.shape[-2:], g.dtype) for g in gs], compiler_params=_cparams(dimension_semantics=("arbitrary",)),
    )(core, *gs, *rs)


def _adamw_math(w, g, m, v):
    m = ADAM_B1 * m + (1.0 - ADAM_B1) * g
    v = ADAM_B2 * v + (1.0 - ADAM_B2) * (g * g)
    m_hat = m / (1.0 - ADAM_B1 ** ADAM_STEP)
    v_hat = v / (1.0 - ADAM_B2 ** ADAM_STEP)
    return -ADAM_LR * (m_hat / (jnp.sqrt(v_hat) + ADAM_EPS) + ADAM_WD * w), m, v


def _adamw_many(weights, name, ride=None):
    steps = 4
    in_specs, out_specs, out_shape, operands, tiles = [], [], [], [], []
    for w, m, v, parts, own, transposed in weights:
        _, pr, pc = parts.shape
        if transposed:
            c, r = w.shape
            tile = pl.BlockSpec((c, r // steps), lambda i: (0, i))
            part_tile = pl.BlockSpec((4, r // steps, pc), lambda i: (0, i, 0))
            tiles.append((c, r // steps))
        elif w.shape[0] % (8 * steps) == 0:
            r, c = w.shape
            tile = pl.BlockSpec((r // steps, c), lambda i: (i, 0))
            part_tile = pl.BlockSpec((4, r // steps, pc), lambda i: (0, i, 0))
            tiles.append((r // steps, c))
        else:
            tile = pl.BlockSpec(w.shape, lambda i: (0, 0))
            part_tile = pl.BlockSpec(parts.shape, lambda i: (0, 0, 0))
            tiles.append(w.shape)
        in_specs += [tile, tile, tile] + [part_tile] * (1 if own is None else 2)
        out_specs += [tile] * 4
        out_shape += [_sds(w.shape, F32)] * 4
        operands += [w, m, v, parts] + ([] if own is None else [own])
    n_in = len(operands)

    def body(*refs):
        ins, outs = list(refs[:n_in]), refs[n_in:]
        this_chip = 2 * lax.axis_index("x") + lax.axis_index("y")
        for k, (_, _, _, _, own, transposed) in enumerate(weights):
            w_ref, m_ref, v_ref, p_ref = ins[:4]
            own_ref = None if own is None else ins[4]
            del ins[:4 if own is None else 5]
            rows, cols = tiles[k]
            g = None
            for q in range(4):
                index = (q,) if transposed else (q, slice(0, rows), slice(0, cols))
                part = p_ref[index] if own is None else jnp.where(this_chip == q, own_ref[index], p_ref[index])
                g = part.astype(F32) if g is None else g + part.astype(F32)
            if transposed:
                g = g.T[:rows]
            g_out, d_out, m_out, v_out = outs[4 * k:4 * k + 4]
            g_out[...] = g
            d_out[...], m_out[...], v_out[...] = _adamw_math(w_ref[...], g, m_ref[...], v_ref[...])

    return _call(body, name, (steps,), in_specs, out_specs, out_shape, [], operands, ride)


SMALL = ("ssm_a_re", "ssm_a_im", "ssm_log_dt", "ssm_b_re", "ssm_b_im", "ssm_c_re", "ssm_c_im", "ssm_d",
         "ln1_g", "ln1_b", "ln2_g", "ln2_b")


def _pack_rows(arrays):
    rows = []
    for a in arrays:
        flat = a.reshape(-1)
        rows.append(jnp.pad(flat, (0, -flat.shape[0] % 128)).reshape(-1, 128))
    packed = jnp.concatenate(rows, axis=0)
    return jnp.pad(packed, ((0, -packed.shape[0] % 8), (0, 0)))


def _unpack_rows(packed, shapes):
    out, row = [], 0
    for shape in shapes:
        size = math.prod(shape)
        n_rows = -(-size // 128)
        out.append(packed[row:row + n_rows].reshape(-1)[:size].reshape(shape))
        row += n_rows
    return out


def _sum_devices(parts):
    def body(p_ref, o_ref):
        total = p_ref[0]
        for dev in range(1, N_DEV):
            total = total + p_ref[dev]
        o_ref[...] = total

    return _pallas_call(body, name="sum_devices", out_shape=_sds(parts.shape[1:], F32))(parts)


def _adamw_replicated(ws, ms, vs, gs):
    n = len(ws)

    def body(*refs):
        w_refs, m_refs, v_refs, g_refs, d_out, m_out, v_out = (refs[i * n:(i + 1) * n] for i in range(7))
        for i in range(n):
            d_out[i][...], m_out[i][...], v_out[i][...] = _adamw_math(w_refs[i][...], g_refs[i][...], m_refs[i][...], v_refs[i][...])

    out = _pallas_call(body, name="adamw_replicated", out_shape=[_sds(w.shape, F32) for w in ws] * 3,
                       compiler_params=_cparams())(*ws, *ms, *vs, *gs)
    return out[:n], out[n:2 * n], out[2 * n:]


def kernel(x, w_in, b_gate, w_attn_br, w_ssm_br, w_out, ssm_a_re, ssm_a_im, ssm_log_dt, ssm_b_re, ssm_b_im, ssm_c_re, ssm_c_im, ssm_d, w_glu, ln1_g, ln1_b, w_ff_gate, w_ff_up, w_ff_down, ln2_g, ln2_b, loss_target, m_w_in, m_b_gate, m_w_attn_br, m_w_ssm_br, m_w_out, m_ssm_a_re, m_ssm_a_im, m_ssm_log_dt, m_ssm_b_re, m_ssm_b_im, m_ssm_c_re, m_ssm_c_im, m_ssm_d, m_w_glu, m_ln1_g, m_ln1_b, m_w_ff_gate, m_w_ff_up, m_w_ff_down, m_ln2_g, m_ln2_b, v_w_in, v_b_gate, v_w_attn_br, v_w_ssm_br, v_w_out, v_ssm_a_re, v_ssm_a_im, v_ssm_log_dt, v_ssm_b_re, v_ssm_b_im, v_ssm_c_re, v_ssm_c_im, v_ssm_d, v_w_glu, v_ln1_g, v_ln1_b, v_w_ff_gate, v_w_ff_up, v_w_ff_down, v_ln2_g, v_ln2_b):
    given = dict(locals())
    x2, target = x[0], loss_target[0]
    core = lax.axis_index("c").astype(jnp.int32).reshape(1)

    sharded = ("w_in", "w_attn_br", "w_ssm_br", "w_glu", "w_ff_gate", "w_ff_up", "b_gate", "w_out", "w_ff_down")
    send_shape = dict(w_in=(D_MODEL, 896), w_attn_br=(ATTN_WIDTH, 128), w_ssm_br=(SSM_WIDTH, 128), w_glu=(SSM_WIDTH, 128),
                      w_out=(128, D_MODEL), w_ff_gate=(D_MODEL, FF_PAD), w_ff_up=(D_MODEL, FF_PAD), w_ff_down=(FF_PAD, D_MODEL))
    local = {k: given[k][0] for k in sharded}
    narrow = ("w_ff_gate", "w_ff_up")
    def to_send(k):
        return (local[k].T, True, *send_shape[k]) if k in narrow else (local[k], False, *send_shape[k])

    later = [k for k in sharded if k not in ("w_in", "b_gate")]
    sends = dict(zip(["w_in"] + later, _send_buffers([to_send("w_in")], "send_w_in")
                     + _send_buffers([to_send(k) for k in later], "send_weights")))
    sends["b_gate"] = local["b_gate"]
    mixer_weights = ("w_attn_br", "w_ssm_br", "w_glu", "b_gate", "w_out")
    ff_weights = ("w_ff_gate", "w_ff_up", "w_ff_down")
    wt = {}
    wt["w_in"], = _all_gather([sends["w_in"]], "gather_w_in")

    a_re, a_im, log_dt = ssm_a_re[0], ssm_a_im[0], ssm_log_dt[0].reshape(SSM_GROUPS, 1)
    b_re_t, b_im_t = ssm_b_re[0].transpose(0, 2, 1), ssm_b_im[0].transpose(0, 2, 1)
    abar_re, abar_im, e_re, e_im, bbar_re_t, bbar_im_t = _ssm_prep(a_re, a_im, log_dt, b_re_t, b_im_t)
    bmat, cmat, a_chunks = _ssm_tables(abar_re, abar_im, bbar_re_t, bbar_im_t, ssm_c_re[0], ssm_c_im[0])
    cos_t, sin_t = _rope_tables()

    big_mixer, ff_in = [k for k in mixer_weights if k != "b_gate"], ("w_ff_gate", "w_ff_up")
    n_mixer = len(big_mixer)
    mixer_1, mixer_2, mixer_3 = _relayed_gather([sends[k] for k in big_mixer])
    ff_in_1, ff_in_2, ff_in_3 = _relayed_gather([sends[k] for k in ff_in])
    ff_down_1, ff_down_2, ff_down_3 = _relayed_gather([sends["w_ff_down"]])
    proj, *landed = _proj(x2, wt["w_in"], mixer_1 + _gather_first_level([sends["b_gate"]]))
    mixer, bias = landed[:n_mixer], landed[n_mixer:]
    attn, lse, q_pm, k_pm, v_pm, *landed = _attn_fwd(proj, cos_t, sin_t,
                                                     mixer_2(mixer) + _gather_second_level(bias) + ff_in_1)
    mixer, b_gate_full, ff = landed[:n_mixer], landed[n_mixer], landed[n_mixer + 1:]
    ys, states, *landed = _ssm_fwd(proj, bmat, cmat, a_chunks, ssm_d, mixer_3(mixer) + ff_in_2(ff) + ff_down_1)
    wt.update(zip(big_mixer, landed[:n_mixer]))
    ff, ff_down = landed[n_mixer:n_mixer + 2], landed[n_mixer + 2:]
    wt["w_out"] = wt["w_out"].reshape(D_MODEL, D_MODEL)
    h, xhat1, rstd1, glu, y_attn, y_ssm, *landed = _mixer_out(
        attn, ys, proj, x2, wt["w_attn_br"], wt["w_ssm_br"], wt["w_glu"], wt["w_out"], b_gate_full, ln1_g, ln1_b,
        ff_in_3(ff) + ff_down_2(ff_down))
    wt.update(zip(ff_in, landed[:2]))
    ff_a, ff_b, ff_f, w_ff_down = _ff_up(h, wt["w_ff_gate"], wt["w_ff_up"], ff_down_3(landed[2:]))
    wt["w_ff_down"] = w_ff_down.reshape(D_FF_PAD, D_MODEL)
    dr2, d_ln2_g, d_ln2_b, loss_lanes = _ff_down_loss(ff_f, wt["w_ff_down"], h, target, ln2_g, ln2_b)

    def pair_sums(names, contrib, from_sibling):
        return _pair_sums([contrib[k] for k in names], from_sibling, core, "pair_sums_" + names[0])

    d_a, d_b = _ff_down_bwd(dr2, wt["w_ff_down"], ff_a, ff_b)
    contrib = dict(w_ff_gate=_weight_grad(h, d_a, "wgrad_w_ff_gate", FF_PAD),
                   w_ff_up=_weight_grad(h, d_b, "wgrad_w_ff_up", FF_PAD),
                   w_ff_down=_weight_grad(ff_f, dr2, "wgrad_w_ff_down"))
    dr1, d_ln1_g, d_ln1_b, *from_sibling = _ff_up_bwd(
        d_a, d_b, wt["w_ff_gate"], wt["w_ff_up"], dr2, xhat1, rstd1, ln1_g, _sibling_swap_ride([contrib[k] for k in ff_weights]))
    ff_sums = pair_sums(ff_weights, contrib, from_sibling)

    d_ya, d_yssm, d_proj, d_attn, d_glu, d_ys, mixed, y_s, gy, d_bg = _mixer_bwd(
        dr1, proj, y_attn, y_ssm, glu, ys, wt["w_attn_br"], wt["w_ssm_br"], wt["w_glu"], wt["w_out"], b_gate_full)
    contrib.update(w_attn_br=_weight_grad(attn, d_ya, "wgrad_w_attn_br", 128),
                   w_ssm_br=_weight_grad(y_s, d_yssm, "wgrad_w_ssm_br", 128),
                   w_glu=_weight_grad(gy, d_glu, "wgrad_w_glu", 128),
                   w_out=_weight_grad(mixed, dr1, "wgrad_w_out"),
                   b_gate=d_bg.reshape(2, 4, 2, 128).transpose(2, 1, 0, 3))
    d_proj, *landed = _attn_bwd(q_pm, k_pm, v_pm, cos_t, sin_t, attn, lse, d_attn, d_proj,
                                _chip_swap_ride(ff_sums) + _sibling_swap_ride([contrib[k] for k in mixer_weights]))
    parts, own_sums = dict(zip(ff_weights, landed[:len(ff_weights)])), {}
    mixer_sums = pair_sums(mixer_weights, contrib, landed[len(ff_weights):])
    landed = _ssm_bwd(d_ys, proj, states, bmat, cmat, a_chunks, ssm_d, d_proj, _chip_swap_ride(mixer_sums))
    d_proj, *ssm_blocks, d_abar, d_skip = landed[:7]
    parts.update(zip(mixer_weights, landed[7:]))

    gbb_re_t, gbb_im_t, gc_re, gc_im = (blocks.reshape(SSM_GROUPS, SSM_GROUP, SSM_STATE) for blocks in ssm_blocks)
    ga_re = d_abar[:, 0, :CHUNK_STATES].reshape(SSM_GROUPS, SSM_STATE)
    ga_im = d_abar[:, 0, CHUNK_STATES:].reshape(SSM_GROUPS, SSM_STATE)
    g_a_re, g_a_im, g_log_dt, g_b_re_t, g_b_im_t = _ssm_param_bwd(
        a_re, a_im, log_dt, b_re_t, b_im_t, abar_re, abar_im, e_re, e_im, ga_re, ga_im, gbb_re_t, gbb_im_t)
    mine = [g_a_re, g_a_im, g_log_dt, g_b_re_t, g_b_im_t, gc_re, gc_im,
            d_skip, d_ln1_g, d_ln1_b, d_ln2_g, d_ln2_b]
    small_packed = _pack_rows(mine + [loss_lanes])

    w_in_contrib, small_partly = _weight_grad_rows(x2, d_proj, core, "wgrad_w_in_first", 896, 0, 6,
                                                   _gather_first_level([small_packed]))
    w_in_contrib, from_sibling, every = _weight_grad_rows(
        x2, d_proj, core, "wgrad_w_in_rest", 896, 6, 2,
        _sibling_swap_ride([w_in_contrib], halves=False) + _gather_second_level([small_partly]), in_place=True)
    w_in_sum, = _pair_sums([w_in_contrib], [from_sibling], core, "pair_sums_w_in")
    send_sems, recv_sems, w_in_sum, landing, token = _chip_swap_start([w_in_sum], "w_in_chip_swap_start")

    def adamw_of(k):
        taken = (lambda a: a.T) if k in narrow else (lambda a: a)
        return taken(local[k]), taken(given["m_" + k][0]), taken(given["v_" + k][0]), parts[k], own_sums.get(k), k in narrow

    others = [k for k in sharded if k != "w_in"]
    updated = _adamw_many([adamw_of(k) for k in others], "adamw_others", _after(token))
    grad_x, = _grad_x(d_proj, wt["w_in"], dr1, _after(token))

    def held(k, a):
        return a.transpose(0, 1, 3, 2) if k in ("ssm_b_re", "ssm_b_im") else a

    *small_grads, loss_sum = _unpack_rows(_sum_devices(every), [held(k, given[k]).shape for k in SMALL] + [(1, 128)])
    small = _adamw_replicated([held(k, given[k]) for k in SMALL], [held(k, given["m_" + k]) for k in SMALL],
                              [held(k, given["v_" + k]) for k in SMALL], small_grads)
    loss = loss_sum[0, 0]

    (own_sums["w_in"],), (parts["w_in"],) = _chip_swap_wait(
        send_sems, recv_sems, w_in_sum, landing, [grad_x, updated[0], small[0][0]], "w_in_chip_swap_wait")
    updated += _adamw_many([adamw_of("w_in")], "adamw_w_in")

    grads, deltas, new_m, new_v = {}, {}, {}, {}
    for i, k in enumerate(others + ["w_in"]):
        out = [o.T if k in narrow else o for o in updated[4 * i:4 * i + 4]]
        grads[k], deltas[k], new_m[k], new_v[k] = (o.reshape((1,) + local[k].shape) for o in out)
    for res, values in zip((grads, deltas, new_m, new_v), (small_grads,) + small):
        res.update((k, held(k, a)) for k, a in zip(SMALL, values))

    order = ("w_in", "b_gate", "w_attn_br", "w_ssm_br", "w_out", "ssm_a_re", "ssm_a_im", "ssm_log_dt", "ssm_b_re", "ssm_b_im",
             "ssm_c_re", "ssm_c_im", "ssm_d", "w_glu", "ln1_g", "ln1_b", "w_ff_gate", "w_ff_up", "w_ff_down", "ln2_g", "ln2_b")
    return (loss, grad_x[None], *[grads[k] for k in order], *[deltas[k] for k in order], *[new_m[k] for k in order],
            *[new_v[k] for k in order])
```

```python
import functools
import math

import jax
import jax.numpy as jnp
import numpy as np
from jax import lax
from jax.experimental import pallas as pl
from jax.experimental.pallas import tpu as pltpu

F32 = jnp.float32
BF16 = jnp.bfloat16

N_DEV = 8
SEQ = 2048
D_MODEL = 1024
HEAD_DIM = 64
ATTN_WIDTH = 512
QKV_WIDTH = 1536
SSM_WIDTH = 512
SSM_GROUPS = 32
SSM_GROUP = 16
SSM_STATE = 64
IN_WIDTH = 7168
D_FF = 2816
FF_SHARD = D_FF // N_DEV
FF_PAD = 384
D_FF_PAD = FF_PAD * N_DEV
DN_ALPHA = 2.0 ** 0.25
LN_EPS = 1e-5
NEG_INF = -1e30
ROPE_THETA = 10000.0
BLOCK = 128
GROUPS = ((1, 16), (4, 4), (16, 1))

ADAM_LR = 0.001
ADAM_B1 = 0.9
ADAM_B2 = 0.999
ADAM_EPS = 1e-08
ADAM_WD = 0.01
ADAM_STEP = 10

VMEM_LIMIT = 56 * 1024 * 1024


_pallas_call = pl.pallas_call


def _cparams(**kw):
    return pltpu.CompilerParams(vmem_limit_bytes=VMEM_LIMIT, **kw)


def _dot(a, b):
    return jnp.dot(a, b, preferred_element_type=F32)


def _dot_nt(a, b):
    return lax.dot_general(a, b, (((1,), (1,)), ((), ())), preferred_element_type=F32)


def _side_by_side(w_ref, row=None):
    rows = slice(None) if row is None else pl.ds(row, 1)
    return jnp.concatenate([w_ref[i, rows, :] for i in range(w_ref.shape[0])], axis=1)


def _dot_tn(a, b):
    return lax.dot_general(a, b, (((0,), (0,)), ((), ())), preferred_element_type=F32)


def _rope_tables():
    half = HEAD_DIM // 2
    inv_freq = np.float32(ROPE_THETA) ** (-np.arange(half, dtype=np.float32) / np.float32(half))
    ang = np.arange(SEQ, dtype=np.float32)[:, None] * inv_freq[None, :]
    cos, sin = np.cos(ang).astype(np.float32), np.sin(ang).astype(np.float32)
    tables = np.tile(cos, (1, 4)), np.tile(np.concatenate([-sin, sin], axis=1), (1, 2))

    def by_phase(t):
        return np.stack([t.reshape(SEQ // d, d, 128).transpose(1, 0, 2).reshape(SEQ, 128) for d, _ in GROUPS])

    return jnp.asarray(by_phase(tables[0])), jnp.asarray(by_phase(tables[1]))


def _swap_halves(x):
    lane = lax.broadcasted_iota(jnp.int32, x.shape, 1)
    return jnp.where((lane & 63) < 32, pltpu.roll(x, 96, axis=1), pltpu.roll(x, 32, axis=1))


def _group_rows(d, nb, r, i):
    src = pl.ds(i * BLOCK, BLOCK) if d == 1 else pl.ds(r + i * BLOCK * d, BLOCK, stride=d)
    return src, pl.ds((r * nb + i) * BLOCK, BLOCK)


def _attn_masks():
    a_idx = lax.broadcasted_iota(jnp.int32, (2 * BLOCK, 2 * BLOCK), 0) & (BLOCK - 1)
    c_idx = lax.broadcasted_iota(jnp.int32, (2 * BLOCK, 2 * BLOCK), 1)
    cur_ok = jnp.logical_and(c_idx >= BLOCK, c_idx - BLOCK <= a_idx)
    prev_ok = jnp.logical_and(c_idx < BLOCK, c_idx >= a_idx)
    lane = lax.broadcasted_iota(jnp.int32, (BLOCK, 128), 1)
    return cur_ok, prev_ok, lane < HEAD_DIM


def _stack_heads(t, head0):
    zero = jnp.zeros_like(t)
    return jnp.concatenate([jnp.where(head0, t, zero), jnp.where(head0, zero, t)], axis=0)


def _unstack_heads(t2, head0):
    return jnp.where(head0, t2[:BLOCK], t2[BLOCK:])


def _attn_fwd(proj, cos_t, sin_t, ride=None):
    def body(q0, q1, q2, k0, k1, k2, v0, v1, v2, cos_ref, sin_ref, attn_ref, lse_ref, qpm_ref, kpm_ref, vpm_ref,
             qs, ks, vs, os_, ms, ls, acc, mnat, lnat):
        cur_ok, prev_ok, head0 = _attn_masks()
        ks[:BLOCK, :] = jnp.zeros((BLOCK, 128), BF16)
        vs[:BLOCK, :] = jnp.zeros((BLOCK, 128), BF16)
        for g, (d, nb) in enumerate(GROUPS):
            q_ref, k_ref, v_ref = (q0, q1, q2)[g], (k0, k1, k2)[g], (v0, v1, v2)[g]
            for r in range(d):
                for i in range(nb):
                    src, dst = _group_rows(d, nb, r, i)
                    below = pl.ds(dst.start + BLOCK, BLOCK)
                    c, s = cos_ref[g, dst, :], sin_ref[g, dst, :]
                    q = q_ref[src, :]
                    k = k_ref[src, :]
                    qs[dst, :] = ((q * c + _swap_halves(q) * s) * 0.125).astype(BF16)
                    ks[below, :] = (k * c + _swap_halves(k) * s).astype(BF16)
                    vs[below, :] = v_ref[src, :].astype(BF16)
                    qpm_ref[g, dst, :], kpm_ref[g, dst, :], vpm_ref[g, dst, :] = qs[dst, :], ks[below, :], vs[below, :]

            def block(b, carry, nb=nb):
                has_prev = (b & (nb - 1)) > 0
                cur = pl.ds(pl.multiple_of(b * BLOCK, BLOCK), BLOCK)
                window = pl.ds(pl.multiple_of(b * BLOCK, BLOCK), 2 * BLOCK)
                valid = jnp.logical_or(cur_ok, jnp.logical_and(prev_ok, has_prev))
                s = jnp.where(valid, _dot_nt(_stack_heads(qs[cur, :], head0), ks[window, :]), NEG_INF)
                m = jnp.max(s, axis=1, keepdims=True)
                p = jnp.exp(s - m)
                os_[cur, :] = _unstack_heads(_dot(p.astype(BF16), vs[window, :]), head0)
                ms[cur, :] = _unstack_heads(m, head0)
                ls[cur, :] = _unstack_heads(jnp.sum(p, axis=1, keepdims=True), head0)
                return carry

            lax.fori_loop(0, SEQ // BLOCK, block, 0, unroll=16)

            for r in range(d):
                for i in range(nb):
                    src, dst = _group_rows(d, nb, r, i)
                    if g == 0:
                        acc[src, :], mnat[src, :], lnat[src, :] = os_[dst, :], ms[dst, :], ls[dst, :]
                    else:
                        m_old, m_g = mnat[src, :], ms[dst, :]
                        m_new = jnp.maximum(m_old, m_g)
                        a_old, a_g = jnp.exp(m_old - m_new), jnp.exp(m_g - m_new)
                        acc[src, :] = a_old * acc[src, :] + a_g * os_[dst, :]
                        lnat[src, :] = a_old * lnat[src, :] + a_g * ls[dst, :]
                        mnat[src, :] = m_new
        for i in range(SEQ // BLOCK):
            rows = pl.ds(i * BLOCK, BLOCK)
            l = lnat[rows, :]
            attn_ref[rows, :] = acc[rows, :] / l
            lse_ref[rows, :] = mnat[rows, :] + jnp.log(l)

    def col(base):
        return pl.BlockSpec((SEQ, 128), lambda hp, base=base: (0, base + hp))

    in_specs = [col(g * 4) for g in range(3)] + [col(12 + g * 4) for g in range(3)] + [col(24 + g * 4) for g in range(3)]
    table = pl.BlockSpec((3, SEQ, 128), lambda hp: (0, 0, 0), pipeline_mode=pl.Buffered(1))
    out = pl.BlockSpec((SEQ, 128), lambda hp: (0, hp))
    by_phase = pl.BlockSpec((3, SEQ, 128), lambda hp: (0, 0, hp))
    return _call(
        body, "attn_fwd", (4,), in_specs + [table, table], [out, out] + [by_phase] * 3,
        [_sds((SEQ, ATTN_WIDTH), F32), _sds((SEQ, ATTN_WIDTH), F32)] + [_sds((3, SEQ, ATTN_WIDTH), BF16)] * 3,
        [pltpu.VMEM((SEQ, 128), BF16)] + [pltpu.VMEM((SEQ + BLOCK, 128), BF16)] * 2 + [pltpu.VMEM((SEQ, 128), F32)] * 6,
        [proj] * 9 + [cos_t, sin_t], ride)


def _attn_bwd_group_body(g):
    d, nb = GROUPS[g]

    def body(qs_ref, ks_ref, vs_ref, cos_ref, sin_ref, lse_ref, dattn_ref, dsum_ref, dproj_ref,
             ks, vs, dos, lss, dss, dqs, dks, dvs, stage, outs, sems):
        cur_ok, prev_ok, head0 = _attn_masks()
        qs = qs_ref.at[g]
        ks[:BLOCK, :] = jnp.zeros((BLOCK, 128), BF16)
        vs[:BLOCK, :] = jnp.zeros((BLOCK, 128), BF16)
        dks[:BLOCK, :] = jnp.zeros((BLOCK, 128), F32)
        dvs[:BLOCK, :] = jnp.zeros((BLOCK, 128), F32)
        for r in range(d):
            for i in range(nb):
                src, dst = _group_rows(d, nb, r, i)
                below = pl.ds(dst.start + BLOCK, BLOCK)
                ks[below, :] = ks_ref[g, dst, :]
                vs[below, :] = vs_ref[g, dst, :]
                dos[dst, :] = dattn_ref[src, :].astype(BF16)
                for per_head, spread in ((dsum_ref[src, :], dss), (lse_ref[src, :], lss)):
                    other = pltpu.roll(per_head, HEAD_DIM, axis=1)
                    spread[0, dst, :] = jnp.where(head0, per_head, other)
                    spread[1, dst, :] = jnp.where(head0, other, per_head)
                dks[below, :] = jnp.zeros((BLOCK, 128), F32)
                dvs[below, :] = jnp.zeros((BLOCK, 128), F32)

        def per_stacked_row(spread, cur):
            h0, h1 = spread[0, cur, :], spread[1, cur, :]
            return jnp.concatenate([jnp.concatenate([h0, h0], axis=1), jnp.concatenate([h1, h1], axis=1)], axis=0)

        def block(b, carry):
            has_prev = (b & (nb - 1)) > 0
            cur = pl.ds(pl.multiple_of(b * BLOCK, BLOCK), BLOCK)
            window = pl.ds(pl.multiple_of(b * BLOCK, BLOCK), 2 * BLOCK)
            valid = jnp.logical_or(cur_ok, jnp.logical_and(prev_ok, has_prev))
            q2, do2 = _stack_heads(qs[cur, :], head0), _stack_heads(dos[cur, :], head0)
            kw, vw = ks[window, :], vs[window, :]
            s = jnp.where(valid, _dot_nt(q2, kw), NEG_INF)
            p = jnp.exp(s - per_stacked_row(lss, cur))
            ds = (p * (_dot_nt(do2, vw) - per_stacked_row(dss, cur))).astype(BF16)
            dvs[window, :] += _dot_tn(p.astype(BF16), do2)
            dks[window, :] += _dot_tn(ds, q2)
            dqs[cur, :] = _unstack_heads(_dot(ds, kw), head0)
            return carry

        lax.fori_loop(0, SEQ // BLOCK, block, 0, unroll=16)

        hp = pl.program_id(0)
        copies = []
        for kind in range(3):
            for r in range(d):
                for i in range(nb):
                    src, dst = _group_rows(d, nb, r, i)
                    below = pl.ds(dst.start + BLOCK, BLOCK)
                    if kind == 2:
                        stage[src, :] = dvs[below, :]
                    else:
                        c, s = cos_ref[g, dst, :], sin_ref[g, dst, :]
                        t = dqs[dst, :] * 0.125 if kind == 0 else dks[below, :]
                        stage[src, :] = t * c - _swap_halves(t) * s
            for i in range(SEQ // MM_ROWS):
                rows = pl.ds(i * MM_ROWS, MM_ROWS)
                outs[kind, rows, :] = stage[rows, :].astype(BF16)
            column = pl.multiple_of((kind * 12 + g * 4 + hp) * 128, 128)
            copies.append(pltpu.make_async_copy(outs.at[kind], dproj_ref.at[:, pl.ds(column, 128)], sems.at[kind]))
            copies[-1].start()
        for cp in copies:
            cp.wait()

    return body


def _attn_bwd(q_pm, k_pm, v_pm, cos_t, sin_t, attn, lse, dattn, dproj, ride=None):
    groups = [_attn_bwd_group_body(g) for g in range(3)]

    def body(qs_ref, ks_ref, vs_ref, cos_ref, sin_ref, attn_ref, lse_ref, dattn_ref, dproj_in, dproj_ref, dsum, *scratch):
        del dproj_in
        head0 = _attn_masks()[2]
        for i in range(SEQ // BLOCK):
            rows = pl.ds(i * BLOCK, BLOCK)
            prod = dattn_ref[rows, :] * attn_ref[rows, :]
            d0 = jnp.sum(jnp.where(head0, prod, 0.0), axis=1, keepdims=True)
            d1 = jnp.sum(jnp.where(head0, 0.0, prod), axis=1, keepdims=True)
            dsum[rows, :] = jnp.where(head0, d0, d1)
        for g in range(3):
            groups[g](qs_ref, ks_ref, vs_ref, cos_ref, sin_ref, lse_ref, dattn_ref, dsum, dproj_ref, *scratch)

    def col(base):
        return pl.BlockSpec((SEQ, 128), lambda hp, base=base: (0, base + hp))

    table = pl.BlockSpec((3, SEQ, 128), lambda hp: (0, 0, 0), pipeline_mode=pl.Buffered(1))
    by_phase = pl.BlockSpec((3, SEQ, 128), lambda hp: (0, 0, hp))
    return _call(
        body, "attn_bwd", (4,), [by_phase] * 3 + [table, table, col(0), col(0), col(0), ANY],
        [ANY], [_sds((SEQ, IN_WIDTH), BF16)],
        [pltpu.VMEM((SEQ, 128), F32)]
        + [pltpu.VMEM((SEQ + BLOCK, 128), BF16)] * 2 + [pltpu.VMEM((SEQ, 128), BF16)]
        + [pltpu.VMEM((2, SEQ, 128), F32)] * 2 + [pltpu.VMEM((SEQ, 128), F32)]
        + [pltpu.VMEM((SEQ + BLOCK, 128), F32)] * 2 + [pltpu.VMEM((SEQ, 128), F32)]
        + [pltpu.VMEM((3, SEQ, 128), BF16), pltpu.SemaphoreType.DMA((3,))],
        [q_pm, k_pm, v_pm, cos_t, sin_t, attn, lse, dattn, dproj], ride, aliases={8: 0})


SSM_CHUNKS = 4
CHUNK_STATES = 512
SCAN_ROWS = 8
U_COL = (3 * QKV_WIDTH) // 128


def _cmul(xr, xi, yr, yi):
    return xr * yr - xi * yi, xr * yi + xi * yr


def _ssm_prep(a_re, a_im, log_dt, b_re_t, b_im_t):
    def body(ar_ref, ai_ref, ldt_ref, br_ref, bi_ref, abr_ref, abi_ref, er_ref, ei_ref, bbr_ref, bbi_ref):
        ar, ai = ar_ref[...], ai_ref[...]
        dt = jnp.exp(ldt_ref[...])
        mag = jnp.exp(ar * dt)
        abr, abi = mag * jnp.cos(ai * dt), mag * jnp.sin(ai * dt)
        den = ar * ar + ai * ai
        nr, ni = abr - 1.0, abi
        er, ei = (nr * ar + ni * ai) / den, (ni * ar - nr * ai) / den
        abr_ref[...], abi_ref[...], er_ref[...], ei_ref[...] = abr, abi, er, ei
        er3, ei3 = er[:, None, :], ei[:, None, :]
        br, bi = br_ref[...], bi_ref[...]
        bbr_ref[...] = er3 * br - ei3 * bi
        bbi_ref[...] = er3 * bi + ei3 * br

    gp = jax.ShapeDtypeStruct(a_re.shape, F32)
    gb = jax.ShapeDtypeStruct(b_re_t.shape, F32)
    return _pallas_call(body, name="ssm_prep", out_shape=(gp, gp, gp, gp, gb, gb))(a_re, a_im, log_dt, b_re_t, b_im_t)


def _ssm_param_bwd(a_re, a_im, log_dt, b_re_t, b_im_t, abar_re, abar_im, e_re, e_im, ga_re, ga_im, gbb_re_t, gbb_im_t):
    def body(ar_ref, ai_ref, ldt_ref, br_ref, bi_ref, abr_ref, abi_ref, er_ref, ei_ref, gar_ref, gai_ref, gbr_ref, gbi_ref,
             o_ar, o_ai, o_ldt, o_br, o_bi):
        ar, ai = ar_ref[...], ai_ref[...]
        dt = jnp.exp(ldt_ref[...])
        er, ei = er_ref[...], ei_ref[...]
        br, bi, gbr, gbi = br_ref[...], bi_ref[...], gbr_ref[...], gbi_ref[...]
        er3, ei3 = er[:, None, :], ei[:, None, :]
        o_br[...] = er3 * gbr + ei3 * gbi
        o_bi[...] = er3 * gbi - ei3 * gbr
        ge_r = jnp.sum(br * gbr + bi * gbi, axis=1)
        ge_i = jnp.sum(br * gbi - bi * gbr, axis=1)
        den = ar * ar + ai * ai
        ilr, ili = ar / den, -ai / den
        t_r, t_i = _cmul(ilr, -ili, ge_r, ge_i)
        gab_r, gab_i = gar_ref[...] + t_r, gai_ref[...] + t_i
        gz_r, gz_i = _cmul(abr_ref[...], -abi_ref[...], gab_r, gab_i)
        el_r, el_i = _cmul(er, ei, ilr, ili)
        u_r, u_i = _cmul(el_r, -el_i, ge_r, ge_i)
        o_ar[...] = dt * gz_r - u_r
        o_ai[...] = dt * gz_i - u_i
        o_ldt[...] = jnp.sum(gz_r * ar + gz_i * ai, axis=1, keepdims=True) * dt

    gp = jax.ShapeDtypeStruct(a_re.shape, F32)
    gb = jax.ShapeDtypeStruct(b_re_t.shape, F32)
    return _pallas_call(body, name="ssm_param_bwd", out_shape=(gp, gp, jax.ShapeDtypeStruct(log_dt.shape, F32), gb, gb))(
        a_re, a_im, log_dt, b_re_t, b_im_t, abar_re, abar_im, e_re, e_im, ga_re, ga_im, gbb_re_t, gbb_im_t)


def _block_diag(blocks_re, blocks_im, sign_im, rows_are_channels):
    both = jnp.stack([blocks_re, sign_im * blocks_im]).reshape(2, SSM_CHUNKS, 8, SSM_GROUP, SSM_STATE)
    eye = jnp.eye(8, dtype=F32)
    if rows_are_channels:
        return jnp.einsum("rcghp,gk->cghrkp", both, eye).reshape(SSM_CHUNKS, 128, 2 * CHUNK_STATES)
    return jnp.einsum("rcghp,gk->crkpgh", both, eye).reshape(SSM_CHUNKS, 2 * CHUNK_STATES, 128)


def _diagonal_blocks(mat, part):
    first = [part * CHUNK_STATES + SSM_STATE * g for g in range(8)]
    return jnp.concatenate([mat[SSM_GROUP * g:SSM_GROUP * (g + 1), first[g]:first[g] + SSM_STATE] for g in range(8)], axis=0)


def _scan_consts(a_ref, conj, reverse):
    ar = jnp.broadcast_to(a_ref[:, :CHUNK_STATES], (SCAN_ROWS, CHUNK_STATES))
    ai = jnp.broadcast_to(a_ref[:, CHUNK_STATES:], (SCAN_ROWS, CHUNK_STATES))
    if conj:
        ai = -ai
    row = lax.broadcasted_iota(jnp.int32, (SCAN_ROWS, CHUNK_STATES), 0)
    if reverse:
        row = SCAN_ROWS - 1 - row
    zero = jnp.zeros_like(ar)
    steps = []
    pr, pi = ar, ai
    for shift in (1, 2, 4):
        keep = row >= shift
        steps.append((SCAN_ROWS - shift if reverse else shift, jnp.where(keep, pr, zero), jnp.where(keep, pi, zero)))
        pr, pi = _cmul(pr, pi, pr, pi)
    first = row == 0
    return steps, (jnp.where(first, ar, zero), jnp.where(first, ai, zero)), first


def _scan_tile(xr, xi, prev_r, prev_i, steps, carry_in, reverse):
    edge = SCAN_ROWS - 1 if reverse else 1
    cr, ci = pltpu.roll(prev_r, edge, axis=0), pltpu.roll(prev_i, edge, axis=0)
    xr, xi = xr + carry_in[0] * cr - carry_in[1] * ci, xi + carry_in[0] * ci + carry_in[1] * cr
    for shift, mr, mi in steps:
        sr, si = pltpu.roll(xr, shift, axis=0), pltpu.roll(xi, shift, axis=0)
        xr, xi = xr + mr * sr - mi * si, xi + mr * si + mi * sr
    return xr, xi


MM_ROWS = 256


def _ssm_fwd(proj, bmat, cmat, a_chunks, d_skip, ride=None):
    def body(u_ref, b_ref, c_ref, a_ref, d_ref, y_ref, states_ref, h_ref):
        for i in range(SEQ // MM_ROWS):
            rows = pl.ds(i * MM_ROWS, MM_ROWS)
            h_ref[rows, :] = _dot(u_ref[rows, :].astype(BF16), b_ref[...])
        steps, carry_in, _ = _scan_consts(a_ref, conj=False, reverse=False)

        def tile(k, carry):
            rows = pl.ds(pl.multiple_of(k * SCAN_ROWS, SCAN_ROWS), SCAN_ROWS)
            xr, xi = _scan_tile(h_ref[rows, :CHUNK_STATES], h_ref[rows, CHUNK_STATES:], carry[0], carry[1], steps, carry_in, False)
            h_ref[rows, :CHUNK_STATES] = xr
            h_ref[rows, CHUNK_STATES:] = xi
            return xr, xi

        zero = jnp.zeros((SCAN_ROWS, CHUNK_STATES), F32)
        lax.fori_loop(0, SEQ // SCAN_ROWS, tile, (zero, zero), unroll=4)
        for i in range(SEQ // MM_ROWS):
            rows = pl.ds(i * MM_ROWS, MM_ROWS)
            states = h_ref[rows, :].astype(BF16)
            states_ref[rows, :] = states
            y_ref[rows, :] = _dot(states, c_ref[...]) + d_ref[...] * u_ref[rows, :]

    return _call(
        body, "ssm_fwd", (SSM_CHUNKS,),
        [pl.BlockSpec((SEQ, 128), lambda c: (0, U_COL + c)),
         pl.BlockSpec((None, 128, 2 * CHUNK_STATES), lambda c: (c, 0, 0)),
         pl.BlockSpec((None, 2 * CHUNK_STATES, 128), lambda c: (c, 0, 0)),
         pl.BlockSpec((None, 1, 2 * CHUNK_STATES), lambda c: (c, 0, 0)),
         pl.BlockSpec((1, 128), lambda c: (0, c))],
        [pl.BlockSpec((SEQ, 128), lambda c: (0, c)), pl.BlockSpec((SEQ, 2 * CHUNK_STATES), lambda c: (0, c))],
        [_sds((SEQ, SSM_WIDTH), F32), _sds((SEQ, SSM_CHUNKS * 2 * CHUNK_STATES), BF16)],
        [pltpu.VMEM((SEQ, 2 * CHUNK_STATES), F32)],
        [proj, bmat, cmat, a_chunks, d_skip], ride)


def _ssm_bwd(dys, proj, h, bmat, cmat, a_chunks, d_skip, dproj, ride=None):
    def body(dy_ref, u_ref, states_ref, b_ref, c_ref, a_ref, d_ref, dproj_in, du_ref, db_re_ref, db_im_ref, dc_re_ref, dc_im_ref,
             da_ref, dd_ref, g_ref, h_ref):
        del dproj_in
        dsum = jnp.zeros((1, 128), F32)
        dcm = jnp.zeros((128, 2 * CHUNK_STATES), F32)
        for i in range(SEQ // MM_ROWS):
            rows = pl.ds(i * MM_ROWS, MM_ROWS)
            h_ref[rows, :] = states_ref[rows, :].astype(F32)
            dy = dy_ref[rows, :]
            g_ref[rows, :] = _dot_nt(dy.astype(BF16), c_ref[...])
            dsum += jnp.sum(dy * u_ref[rows, :], axis=0, keepdims=True)
            dcm += _dot_tn(dy.astype(BF16), states_ref[rows, :])
        dd_ref[...] = dsum
        dc_re_ref[...] = _diagonal_blocks(dcm, 0)
        dc_im_ref[...] = _diagonal_blocks(dcm, 1)
        steps, carry_in, _ = _scan_consts(a_ref, conj=True, reverse=True)
        first_row = lax.broadcasted_iota(jnp.int32, (SCAN_ROWS, CHUNK_STATES), 0) == 0
        n_tiles = SEQ // SCAN_ROWS

        def tile(j, carry):
            k = n_tiles - 1 - j
            rows = pl.ds(pl.multiple_of(k * SCAN_ROWS, SCAN_ROWS), SCAN_ROWS)
            before = pl.ds(pl.multiple_of(jnp.maximum(k - 1, 0) * SCAN_ROWS, SCAN_ROWS), SCAN_ROWS)
            gr, gi = _scan_tile(g_ref[rows, :CHUNK_STATES], g_ref[rows, CHUNK_STATES:], carry[0], carry[1], steps, carry_in, True)
            g_ref[rows, :CHUNK_STATES] = gr
            g_ref[rows, CHUNK_STATES:] = gi
            has_before = jnp.where(k > 0, 1.0, 0.0)
            hr = jnp.where(first_row, pltpu.roll(h_ref[before, :CHUNK_STATES], 1, axis=0) * has_before,
                           pltpu.roll(h_ref[rows, :CHUNK_STATES], 1, axis=0))
            hi = jnp.where(first_row, pltpu.roll(h_ref[before, CHUNK_STATES:], 1, axis=0) * has_before,
                           pltpu.roll(h_ref[rows, CHUNK_STATES:], 1, axis=0))
            return gr, gi, carry[2] + hr * gr + hi * gi, carry[3] + hr * gi - hi * gr

        zero = jnp.zeros((SCAN_ROWS, CHUNK_STATES), F32)
        _, _, sar, sai = lax.fori_loop(0, n_tiles, tile, (zero, zero, zero, zero), unroll=4)
        da_ref[:, :CHUNK_STATES] = jnp.sum(sar, axis=0, keepdims=True)
        da_ref[:, CHUNK_STATES:] = jnp.sum(sai, axis=0, keepdims=True)
        dbm = jnp.zeros((128, 2 * CHUNK_STATES), F32)
        for i in range(SEQ // MM_ROWS):
            rows = pl.ds(i * MM_ROWS, MM_ROWS)
            g = g_ref[rows, :].astype(BF16)
            du_ref[rows, :] = (_dot_nt(g, b_ref[...]) + d_ref[...] * dy_ref[rows, :]).astype(BF16)
            dbm += _dot_tn(u_ref[rows, :].astype(BF16), g)
        db_re_ref[...] = _diagonal_blocks(dbm, 0)
        db_im_ref[...] = _diagonal_blocks(dbm, 1)

    chunk_col = pl.BlockSpec((SEQ, 128), lambda c: (0, c))
    blocks = pl.BlockSpec((None, 128, SSM_STATE), lambda c: (c, 0, 0))
    return _call(
        body, "ssm_bwd", (SSM_CHUNKS,),
        [chunk_col,
         pl.BlockSpec((SEQ, 128), lambda c: (0, U_COL + c)),
         pl.BlockSpec((SEQ, 2 * CHUNK_STATES), lambda c: (0, c)),
         pl.BlockSpec((None, 128, 2 * CHUNK_STATES), lambda c: (c, 0, 0)),
         pl.BlockSpec((None, 2 * CHUNK_STATES, 128), lambda c: (c, 0, 0)),
         pl.BlockSpec((None, 1, 2 * CHUNK_STATES), lambda c: (c, 0, 0)),
         pl.BlockSpec((1, 128), lambda c: (0, c)), ANY],
        [pl.BlockSpec((SEQ, 128), lambda c: (0, U_COL + c)), blocks, blocks, blocks, blocks,
         pl.BlockSpec((None, 1, 2 * CHUNK_STATES), lambda c: (c, 0, 0)),
         pl.BlockSpec((1, 128), lambda c: (0, c))],
        [_sds((SEQ, IN_WIDTH), BF16)] + [_sds((SSM_CHUNKS, 128, SSM_STATE), F32)] * 4
        + [_sds((SSM_CHUNKS, 1, 2 * CHUNK_STATES), F32), _sds((1, SSM_WIDTH), F32)],
        [pltpu.VMEM((SEQ, 2 * CHUNK_STATES), F32)] * 2, [dys, proj, h, bmat, cmat, a_chunks, d_skip, dproj], ride, aliases={7: 0})


def _ssm_tables(abar_re, abar_im, bbar_re_t, bbar_im_t, c_re, c_im):
    bmat = _block_diag(bbar_re_t, bbar_im_t, 1.0, True).astype(BF16)
    cmat = _block_diag(c_re, c_im, -1.0, False).astype(BF16)
    a_chunks = jnp.concatenate([abar_re.reshape(SSM_CHUNKS, 1, CHUNK_STATES), abar_im.reshape(SSM_CHUNKS, 1, CHUNK_STATES)], axis=2)
    return bmat, cmat, a_chunks


GL_COL = (3 * QKV_WIDTH + SSM_WIDTH) // D_MODEL
GELU_C = math.sqrt(2.0 / math.pi)
GELU_A = 0.044715


def _sds(shape, dtype):
    return jax.ShapeDtypeStruct(shape, dtype)


def _gelu(x):
    t = jnp.tanh(GELU_C * (x + GELU_A * x * x * x))
    return 0.5 * x * (1.0 + t), t


def _gelu_grad(x, t):
    return 0.5 * (1.0 + t) + 0.5 * x * (1.0 - t * t) * GELU_C * (1.0 + 3.0 * GELU_A * x * x)


def _layer_norm(r, g, b):
    mu = jnp.mean(r, axis=-1, keepdims=True)
    xc = r - mu
    rstd = lax.rsqrt(jnp.mean(xc * xc, axis=-1, keepdims=True) + LN_EPS)
    xhat = xc * rstd
    return xhat * g + b, xhat, rstd


def _layer_norm_bwd(dy, xhat, rstd, g):
    dxhat = dy * g
    m1 = jnp.mean(dxhat, axis=-1, keepdims=True)
    m2 = jnp.mean(dxhat * xhat, axis=-1, keepdims=True)
    return rstd * (dxhat - m1 - xhat * m2)


def _proj(x, w_in, ride=None):
    tm, tn = 1024, 1792

    def body(x_ref, w_ref, o_ref):
        o_ref[...] = _dot(x_ref[...].astype(BF16), _side_by_side(w_ref))

    return _call(
        body, "proj", (SEQ // tm, IN_WIDTH // tn),
        [pl.BlockSpec((tm, D_MODEL), lambda i, j: (i, 0)), pl.BlockSpec((2, D_MODEL, tn // 2), lambda i, j: (j, 0, 0))],
        [pl.BlockSpec((tm, tn), lambda i, j: (i, j))], [_sds((SEQ, IN_WIDTH), F32)], [], [x, w_in], ride)


def _row_spec(tm, width, col=0):
    return pl.BlockSpec((tm, width), lambda i, col=col: (i, col))


def _full_spec(shape):
    return pl.BlockSpec(shape, lambda i: (0,) * len(shape))


def _weight_spec(shape):
    return pl.BlockSpec(shape, lambda i: (0,) * len(shape), pipeline_mode=pl.Buffered(1))


def _mixer_out(attn, ys, proj, x, w_ab, w_sb, w_glu, w_out, b_gate, ln_g, ln_b, ride=None):
    tm = 512

    def body(attn_ref, ys_ref, gl0_ref, gl1_ref, x_ref, wab_ref, wsb_ref, wglu_ref, wout_ref, bg_ref, g_ref, b_ref,
             h_ref, xhat_ref, rstd_ref, glu_ref, ya_ref, yssm_ref):
        gy, _ = _gelu(ys_ref[...])
        glu = _dot(gy.astype(BF16), _side_by_side(wglu_ref))
        glu_ref[...] = glu.astype(BF16)
        y_s = glu[:, :SSM_WIDTH] * jax.nn.sigmoid(glu[:, SSM_WIDTH:])
        y_ssm = _dot(y_s.astype(BF16), _side_by_side(wsb_ref))
        y_attn = _dot(attn_ref[...].astype(BF16), _side_by_side(wab_ref))
        ya_ref[...] = y_attn.astype(BF16)
        yssm_ref[...] = y_ssm.astype(BF16)
        g0 = jax.nn.sigmoid(gl0_ref[...] + _side_by_side(bg_ref, 0))
        g1 = jax.nn.sigmoid(gl1_ref[...] + _side_by_side(bg_ref, 1))
        mixed = g0 * y_attn + g1 * y_ssm
        r1 = DN_ALPHA * x_ref[...] + _dot(mixed.astype(BF16), wout_ref[...])
        h, xhat, rstd = _layer_norm(r1, g_ref[...], b_ref[...])
        h_ref[...] = h
        xhat_ref[...] = xhat
        rstd_ref[...] = jnp.broadcast_to(rstd, (tm, 128))

    wide = _sds((SEQ, D_MODEL), F32)
    return _call(
        body, "mixer_out", (SEQ // tm,),
        [_row_spec(tm, ATTN_WIDTH), _row_spec(tm, SSM_WIDTH), _row_spec(tm, D_MODEL, GL_COL), _row_spec(tm, D_MODEL, GL_COL + 1),
         _row_spec(tm, D_MODEL), _weight_spec((N_DEV, ATTN_WIDTH, 128)), _weight_spec((N_DEV, SSM_WIDTH, 128)),
         _weight_spec((N_DEV, SSM_WIDTH, 128)), _weight_spec((D_MODEL, D_MODEL)), _full_spec((N_DEV, 2, 128)),
         _full_spec((1, D_MODEL)), _full_spec((1, D_MODEL))],
        [_row_spec(tm, D_MODEL), _row_spec(tm, D_MODEL), _row_spec(tm, 128), _row_spec(tm, D_MODEL),
         _row_spec(tm, D_MODEL), _row_spec(tm, D_MODEL)],
        [wide, wide, _sds((SEQ, 128), F32)] + [_sds((SEQ, D_MODEL), BF16)] * 3, [],
        [attn, ys, proj, proj, x, w_ab, w_sb, w_glu, w_out, b_gate, ln_g, ln_b], ride)


def _ff_up(h, w_gate, w_up, ride=None):
    tm, tn = 1024, 768

    def body(h_ref, wg_ref, wu_ref, a_ref, b_ref, f_ref):
        hb = h_ref[...].astype(BF16)
        a, b = _dot(hb, _side_by_side(wg_ref)), _dot(hb, _side_by_side(wu_ref))
        a_ref[...] = a.astype(BF16)
        b_ref[...] = b.astype(BF16)
        f_ref[...] = (a * jax.nn.sigmoid(a) * b).astype(BF16)

    tile = pl.BlockSpec((tm, tn), lambda i, j: (i, j))
    wtile = pl.BlockSpec((tn // FF_PAD, D_MODEL, FF_PAD), lambda i, j: (j, 0, 0))
    out = _sds((SEQ, D_FF_PAD), BF16)
    return _call(body, "ff_up", (SEQ // tm, D_FF_PAD // tn), [pl.BlockSpec((tm, D_MODEL), lambda i, j: (i, 0)), wtile, wtile],
                 [tile, tile, tile], [out, out, out], [], [h, w_gate, w_up], ride)


def _ff_down_loss(f, w_down, h, target, ln_g, ln_b):
    tm = 512

    def body(f_ref, w_ref, h_ref, t_ref, g_ref, b_ref, dr_ref, dg_ref, db_ref, loss_ref):
        @pl.when(pl.program_id(0) == 0)
        def _():
            dg_ref[...] = jnp.zeros_like(dg_ref)
            db_ref[...] = jnp.zeros_like(db_ref)
            loss_ref[...] = jnp.zeros_like(loss_ref)

        r2 = DN_ALPHA * h_ref[...] + _dot(f_ref[...], w_ref[...])
        g = g_ref[...]
        out, xhat, rstd = _layer_norm(r2, g, b_ref[...])
        err = out - t_ref[...]
        loss_ref[...] += 0.5 * jnp.sum(jnp.mean(err * err, axis=-1, keepdims=True), axis=0, keepdims=True)
        dout = err * (1.0 / D_MODEL)
        dg_ref[...] += jnp.sum(dout * xhat, axis=0, keepdims=True)
        db_ref[...] += jnp.sum(dout, axis=0, keepdims=True)
        dr_ref[...] = _layer_norm_bwd(dout, xhat, rstd, g)

    vec = _sds((1, D_MODEL), F32)
    return _pallas_call(
        body, name="ff_down_loss", grid=(SEQ // tm,),
        in_specs=[_row_spec(tm, D_FF_PAD), _weight_spec((D_FF_PAD, D_MODEL)), _row_spec(tm, D_MODEL), _row_spec(tm, D_MODEL),
                  _full_spec((1, D_MODEL)), _full_spec((1, D_MODEL))],
        out_specs=(_row_spec(tm, D_MODEL), _full_spec((1, D_MODEL)), _full_spec((1, D_MODEL)), _full_spec((1, 128))),
        out_shape=(_sds((SEQ, D_MODEL), F32), vec, vec, _sds((1, 128), F32)),
        compiler_params=_cparams(dimension_semantics=("arbitrary",)),
    )(f, w_down, h, target, ln_g, ln_b)


def _ff_down_bwd(dr2, w_down, a, b):
    tm, tn = 1024, 768

    def body(dr_ref, w_ref, a_ref, b_ref, da_ref, db_ref):
        df = _dot_nt(dr_ref[...].astype(BF16), w_ref[...])
        av, bv = a_ref[...].astype(F32), b_ref[...].astype(F32)
        sg = jax.nn.sigmoid(av)
        da_ref[...] = (df * bv * sg * (1.0 + av * (1.0 - sg))).astype(BF16)
        db_ref[...] = (df * av * sg).astype(BF16)

    tile = pl.BlockSpec((tm, tn), lambda i, j: (i, j))
    out = _sds((SEQ, D_FF_PAD), BF16)
    return _pallas_call(
        body, name="ff_down_bwd", grid=(SEQ // tm, D_FF_PAD // tn),
        in_specs=[pl.BlockSpec((tm, D_MODEL), lambda i, j: (i, 0)), pl.BlockSpec((tn, D_MODEL), lambda i, j: (j, 0)), tile, tile],
        out_specs=(tile, tile), out_shape=(out, out),
        compiler_params=_cparams(dimension_semantics=("arbitrary", "arbitrary")),
    )(dr2, w_down, a, b)


def _ff_up_bwd(da, db, w_gate, w_up, dr2, xhat1, rstd1, ln_g, ride=None):
    tm, tk = 1024, 768
    nk = D_FF_PAD // tk

    def body(da_ref, db_ref, wg_ref, wu_ref, dr2_ref, xhat_ref, rstd_ref, g_ref, dr1_ref, dg_ref, dbias_ref, acc):
        i, k = pl.program_id(0), pl.program_id(1)

        @pl.when(jnp.logical_and(i == 0, k == 0))
        def _():
            dg_ref[...] = jnp.zeros_like(dg_ref)
            dbias_ref[...] = jnp.zeros_like(dbias_ref)

        part = _dot_nt(da_ref[...], _side_by_side(wg_ref)) + _dot_nt(db_ref[...], _side_by_side(wu_ref))

        @pl.when(k == 0)
        def _():
            acc[...] = part

        @pl.when(k > 0)
        def _():
            acc[...] += part

        @pl.when(k == nk - 1)
        def _():
            dh = DN_ALPHA * dr2_ref[...] + acc[...]
            xhat = xhat_ref[...]
            dg_ref[...] += jnp.sum(dh * xhat, axis=0, keepdims=True)
            dbias_ref[...] += jnp.sum(dh, axis=0, keepdims=True)
            rstd = jnp.max(rstd_ref[...], axis=1, keepdims=True)
            dr1_ref[...] = _layer_norm_bwd(dh, xhat, rstd, g_ref[...])

    hid = pl.BlockSpec((tm, tk), lambda i, k: (i, k))
    wtile = pl.BlockSpec((tk // FF_PAD, D_MODEL, FF_PAD), lambda i, k: (k, 0, 0))
    row = pl.BlockSpec((tm, D_MODEL), lambda i, k: (i, 0))
    vec = pl.BlockSpec((1, D_MODEL), lambda i, k: (0, 0))
    return _call(
        body, "ff_up_bwd", (SEQ // tm, nk),
        [hid, hid, wtile, wtile, row, row, pl.BlockSpec((tm, 128), lambda i, k: (i, 0)), vec],
        [row, vec, vec], [_sds((SEQ, D_MODEL), F32), _sds((1, D_MODEL), F32), _sds((1, D_MODEL), F32)],
        [pltpu.VMEM((tm, D_MODEL), F32)], [da, db, w_gate, w_up, dr2, xhat1, rstd1, ln_g], ride)


def _mixer_bwd(dr1, proj, y_attn, y_ssm, glu, ys, w_ab, w_sb, w_glu, w_out, b_gate):
    tm = 256

    def body(dr1_ref, gl0_ref, gl1_ref, ya_ref, yssm_ref, glu_ref, ys_ref, wab_ref, wsb_ref, wglu_ref, wout_ref, bg_ref,
             dya_ref, dyssm_ref, dgl_ref, dattn_ref, dglu_ref, dys_ref, mixed_ref, ysb_ref, gy_ref, dbg_ref, stage, copied):
        @pl.when(pl.program_id(0) == 0)
        def _():
            dbg_ref[...] = jnp.zeros_like(dbg_ref)

        dmixed = _dot_nt(dr1_ref[...].astype(BF16), wout_ref[...])
        g0 = jax.nn.sigmoid(gl0_ref[...] + _side_by_side(bg_ref, 0))
        g1 = jax.nn.sigmoid(gl1_ref[...] + _side_by_side(bg_ref, 1))
        y_attn, y_ssm = ya_ref[...].astype(F32), yssm_ref[...].astype(F32)
        mixed_ref[...] = (g0 * y_attn + g1 * y_ssm).astype(BF16)
        dya = (dmixed * g0).astype(BF16)
        dyssm = (dmixed * g1).astype(BF16)
        dya_ref[...] = dya
        dyssm_ref[...] = dyssm
        dgl0 = dmixed * y_attn * g0 * (1.0 - g0)
        dgl1 = dmixed * y_ssm * g1 * (1.0 - g1)
        i, last = pl.program_id(0), SEQ // tm - 1
        slot = i & 1

        def copy_out(buffer, tile):
            window = dgl_ref.at[pl.ds(pl.multiple_of(tile * tm, tm), tm), pl.ds(GL_COL * D_MODEL, 2 * D_MODEL)]
            return pltpu.make_async_copy(stage.at[buffer], window, copied.at[buffer])

        @pl.when(i >= 2)
        def _():
            copy_out(slot, i - 2).wait()

        stage[slot, :, :D_MODEL] = dgl0.astype(BF16)
        stage[slot, :, D_MODEL:] = dgl1.astype(BF16)
        copy_out(slot, i).start()

        @pl.when(i == last)
        def _():
            copy_out(1 - slot, i - 1).wait()
            copy_out(slot, i).wait()
        dbg_ref[:, :D_MODEL] += jnp.sum(dgl0, axis=0, keepdims=True)
        dbg_ref[:, D_MODEL:] += jnp.sum(dgl1, axis=0, keepdims=True)
        dattn_ref[...] = _dot_nt(dya, _side_by_side(wab_ref))
        dy_s = _dot_nt(dyssm, _side_by_side(wsb_ref))
        glu = glu_ref[...].astype(F32)
        glu1, sg = glu[:, :SSM_WIDTH], jax.nn.sigmoid(glu[:, SSM_WIDTH:])
        ysb_ref[...] = (glu1 * sg).astype(BF16)
        dglu1 = (dy_s * sg).astype(BF16)
        dglu2 = (dy_s * glu1 * sg * (1.0 - sg)).astype(BF16)
        dglu_ref[:, :SSM_WIDTH] = dglu1
        dglu_ref[:, SSM_WIDTH:] = dglu2
        dgy = _dot_nt(jnp.concatenate([dglu1, dglu2], axis=1), _side_by_side(wglu_ref))
        ys = ys_ref[...]
        gy, t = _gelu(ys)
        gy_ref[...] = gy.astype(BF16)
        dys_ref[...] = dgy * _gelu_grad(ys, t)

    wide_b, half_b = _sds((SEQ, D_MODEL), BF16), _sds((SEQ, SSM_WIDTH), BF16)
    half_f = _sds((SEQ, SSM_WIDTH), F32)
    return _pallas_call(
        body, name="mixer_bwd", grid=(SEQ // tm,),
        in_specs=[_row_spec(tm, D_MODEL), _row_spec(tm, D_MODEL, GL_COL), _row_spec(tm, D_MODEL, GL_COL + 1), _row_spec(tm, D_MODEL),
                  _row_spec(tm, D_MODEL), _row_spec(tm, D_MODEL), _row_spec(tm, SSM_WIDTH), _full_spec((N_DEV, ATTN_WIDTH, 128)),
                  _full_spec((N_DEV, SSM_WIDTH, 128)), _full_spec((N_DEV, SSM_WIDTH, 128)), _full_spec((D_MODEL, D_MODEL)),
                  _full_spec((N_DEV, 2, 128))],
        out_specs=(_row_spec(tm, D_MODEL), _row_spec(tm, D_MODEL), ANY, _row_spec(tm, ATTN_WIDTH),
                   _row_spec(tm, D_MODEL), _row_spec(tm, SSM_WIDTH), _row_spec(tm, D_MODEL), _row_spec(tm, SSM_WIDTH),
                   _row_spec(tm, SSM_WIDTH), _full_spec((1, 2 * D_MODEL))),
        out_shape=(wide_b, wide_b, _sds((SEQ, IN_WIDTH), BF16), half_f, wide_b, half_f, wide_b, half_b, half_b,
                   _sds((1, 2 * D_MODEL), F32)),
        scratch_shapes=[pltpu.VMEM((2, tm, 2 * D_MODEL), BF16), pltpu.SemaphoreType.DMA((2,))],
        compiler_params=_cparams(dimension_semantics=("arbitrary",)),
    )(dr1, proj, proj, y_attn, y_ssm, glu, ys, w_ab, w_sb, w_glu, w_out, b_gate)


def _grad_x(dproj, w_in, dr1, ride=None):
    tm, tk = 1024, 1792
    nk = IN_WIDTH // tk

    def body(dp_ref, w_ref, dr1_ref, o_ref, acc):
        k = pl.program_id(1)
        part = _dot_nt(dp_ref[...], _side_by_side(w_ref))

        @pl.when(k == 0)
        def _():
            acc[...] = part

        @pl.when(k > 0)
        def _():
            acc[...] += part

        @pl.when(k == nk - 1)
        def _():
            o_ref[...] = DN_ALPHA * dr1_ref[...] + acc[...]

    row = pl.BlockSpec((tm, D_MODEL), lambda i, k: (i, 0))
    return _call(
        body, "grad_x", (SEQ // tm, nk),
        [pl.BlockSpec((tm, tk), lambda i, k: (i, k)), pl.BlockSpec((2, D_MODEL, tk // 2), lambda i, k: (k, 0, 0)), row],
        [row], [_sds((SEQ, D_MODEL), F32)], [pltpu.VMEM((tm, D_MODEL), F32)], [dproj, w_in, dr1], ride)


def _weight_grad(a, b, name, shard_cols=None):
    k, n = a.shape[1], b.shape[1]
    tk = k if shard_cols else k // N_DEV
    tn = n // 4 if shard_cols else min(n, 1024)

    def body(a_ref, b_ref, o_ref):
        grad = _dot_tn(a_ref[...].astype(BF16), b_ref[...].astype(BF16))
        if shard_cols:
            o_ref[0] = grad[:, :shard_cols].astype(BF16)
            o_ref[1] = grad[:, shard_cols:].astype(BF16)
        else:
            o_ref[...] = grad.astype(BF16)

    if shard_cols:
        out_spec = pl.BlockSpec((2, None, tk, shard_cols), lambda kk, j: (0, j, kk, 0))
        out_shape = _sds((2, 4, k, shard_cols), BF16)
    else:
        out_spec = pl.BlockSpec((None, None, tk, tn), lambda kk, j: (kk % 2, kk // 2, 0, j))
        out_shape = _sds((2, 4, tk, n), BF16)
    return _call(body, name, (k // tk, n // tn),
                 [pl.BlockSpec((SEQ, tk), lambda kk, j: (0, kk)), pl.BlockSpec((SEQ, tn), lambda kk, j: (0, j))],
                 [out_spec], [out_shape], [], [a, b])[0]


def _weight_grad_rows(a, b, core, name, shard_cols, first, count, ride, in_place=False):
    k = a.shape[1]

    def body(a_ref, b_ref, o_ref):
        o_ref[...] = _dot_tn(a_ref[...].astype(BF16), b_ref[...].astype(BF16)).astype(BF16)

    def shard(j, core_ref):
        row = j + first
        return 0, jnp.where(row < 4, 2 * row + 1 - core_ref[0], 2 * (row - 4) + core_ref[0])

    return _call(body, name, (count,),
                 [pl.BlockSpec((SEQ, k), lambda j, core_ref: (0, 0), pipeline_mode=pl.Buffered(1)),
                  pl.BlockSpec((SEQ, shard_cols), shard)],
                 [pl.BlockSpec((None, k, shard_cols), lambda j, core_ref: (j + first, 0, 0))],
                 [_sds((N_DEV, k, shard_cols), BF16)], [], [a, b], ride, aliases={2: 0} if in_place else None, prefetch=core)


MESH = pl.DeviceIdType.MESH
ANY = pl.BlockSpec(memory_space=pl.ANY)


def _place():
    return lax.axis_index("x"), lax.axis_index("y"), lax.axis_index("c")


def _other_chips(x, y):
    return [(1 - x, y), (x, 1 - y), (1 - x, 1 - y)]


class _Ride:
    def __init__(self, operands, results, aliases, sems, start, wait):
        self.operands, self.results, self.aliases, self.sems = list(operands), list(results), dict(aliases), list(sems)
        self.start, self.wait = start, wait

    def __add__(self, other):
        n_in, n_out, n_sem = len(self.operands), len(self.results), len(self.sems)

        def both(which):
            def run(ins, outs, sems):
                getattr(self, which)(ins[:n_in], outs[:n_out], sems[:n_sem])
                getattr(other, which)(ins[n_in:], outs[n_out:], sems[n_sem:])
            return run

        aliases = {**self.aliases, **{n_in + i: n_out + j for i, j in other.aliases.items()}}
        return _Ride(self.operands + other.operands, self.results + other.results, aliases, self.sems + other.sems,
                     both("start"), both("wait"))


def _call(body, name, grid, in_specs, out_specs, out_shape, scratch_shapes, operands, ride=None, aliases=None, prefetch=None):
    in_specs, out_specs, out_shape = list(in_specs), list(out_specs), list(out_shape)
    scratch_shapes, operands, aliases = list(scratch_shapes), list(operands), dict(aliases or {})
    kernel_body = body
    if ride is not None:
        n_in, n_out, n_scr, r_in, r_out = len(in_specs), len(out_specs), len(scratch_shapes), len(ride.operands), len(ride.results)

        def kernel_body(*refs):
            out0, scr0 = n_in + r_in, n_in + r_in + n_out + r_out
            ride_refs = (refs[n_in:out0], refs[out0 + n_out:scr0], refs[scr0 + n_scr:])
            ids = [pl.program_id(i) for i in range(len(grid))]
            first = functools.reduce(jnp.logical_and, [i == 0 for i in ids])
            last = functools.reduce(jnp.logical_and, [i == g - 1 for i, g in zip(ids, grid)])

            @pl.when(first)
            def _():
                ride.start(*ride_refs)

            body(*refs[:n_in], *refs[out0:out0 + n_out], *refs[scr0:scr0 + n_scr])

            @pl.when(last)
            def _():
                ride.wait(*ride_refs)

        aliases.update({n_in + i: n_out + j for i, j in ride.aliases.items()})
        in_specs += [ANY] * r_in
        out_specs += [ANY] * r_out
        out_shape += ride.results
        scratch_shapes += ride.sems
        operands += ride.operands
    params = _cparams(dimension_semantics=("arbitrary",) * len(grid))
    if prefetch is None:
        return _pallas_call(
            kernel_body, name=name, grid=grid, in_specs=in_specs, out_specs=out_specs, out_shape=out_shape,
            scratch_shapes=scratch_shapes, input_output_aliases=aliases, compiler_params=params,
        )(*operands)

    def with_prefetch(prefetch_ref, *refs):
        kernel_body(*refs)

    return _pallas_call(
        with_prefetch, name=name,
        grid_spec=pltpu.PrefetchScalarGridSpec(num_scalar_prefetch=1, grid=grid, in_specs=in_specs, out_specs=out_specs,
                                               scratch_shapes=scratch_shapes),
        out_shape=out_shape, input_output_aliases={i + 1: j for i, j in aliases.items()}, compiler_params=params,
    )(prefetch, *operands)


def _after(*arrays):
    return _Ride(arrays, [], {}, [], lambda *refs: None, lambda *refs: None)


def _gather_first_level(shards):
    n = len(shards)

    def copies(ins, outs, sems, landed):
        send_sems, recv_sems, local_sems = sems
        x, y, c = _place()
        peers = [(x, y, 1 - c)] + [(px, py, c) for px, py in _other_chips(x, y)]

        def row(peer):
            return 4 * x + 2 * y + c if not landed else 4 * peer[0] + 2 * peer[1] + peer[2]

        local = [pltpu.make_async_copy(ins[a], outs[a].at[4 * x + 2 * y + c], local_sems.at[a]) for a in range(n)]
        remote = [pltpu.make_async_remote_copy(
            src_ref=ins[a], dst_ref=outs[a].at[row(peer)], send_sem=send_sems.at[a, k], recv_sem=recv_sems.at[a, k],
            device_id=peer, device_id_type=MESH) for a in range(n) for k, peer in enumerate(peers)]
        return local, remote

    def start(ins, outs, sems):
        local, remote = copies(ins, outs, sems, False)
        for cp in local + remote:
            cp.start()

    def wait(ins, outs, sems):
        local, sent = copies(ins, outs, sems, False)
        for cp in copies(ins, outs, sems, True)[1]:
            cp.wait_recv()
        for cp in sent:
            cp.wait_send()
        for cp in local:
            cp.wait()

    return _Ride(shards, [_sds((N_DEV,) + s.shape, s.dtype) for s in shards], {},
                 [pltpu.SemaphoreType.DMA((n, 4)), pltpu.SemaphoreType.DMA((n, 4)), pltpu.SemaphoreType.DMA((n,))], start, wait)


def _gather_second_level(buffers):
    n = len(buffers)

    def copies(outs, sems, core):
        send_sems, recv_sems = sems
        x, y, c = _place()
        return [pltpu.make_async_remote_copy(
            src_ref=outs[a].at[4 * px + 2 * py + core], dst_ref=outs[a].at[4 * px + 2 * py + core], send_sem=send_sems.at[a, j],
            recv_sem=recv_sems.at[a, j], device_id=(x, y, 1 - c), device_id_type=MESH)
            for a in range(n) for j, (px, py) in enumerate(_other_chips(x, y))]

    def start(ins, outs, sems):
        for cp in copies(outs, sems, lax.axis_index("c")):
            cp.start()

    def wait(ins, outs, sems):
        for cp in copies(outs, sems, 1 - lax.axis_index("c")):
            cp.wait_recv()
        for cp in copies(outs, sems, lax.axis_index("c")):
            cp.wait_send()

    return _Ride(buffers, [_sds(b.shape, b.dtype) for b in buffers], {i: i for i in range(n)},
                 [pltpu.SemaphoreType.DMA((n, 3)), pltpu.SemaphoreType.DMA((n, 3))], start, wait)


def _relayed_gather(shards):
    n = len(shards)
    buffers = [_sds((N_DEV,) + s.shape, s.dtype) for s in shards]
    dma = pltpu.SemaphoreType.DMA

    def remote(src, dst, send_sem, recv_sem, to):
        return pltpu.make_async_remote_copy(src_ref=src, dst_ref=dst, send_sem=send_sem, recv_sem=recv_sem,
                                            device_id=to, device_id_type=MESH)

    def row(px, py, pc):
        return 4 * px + 2 * py + pc

    def ride(operands, aliases, sems, copies):
        def start(ins, outs, sem_refs):
            local, sent = copies(ins, outs, sem_refs, False)
            for cp in local + sent:
                cp.start()

        def wait(ins, outs, sem_refs):
            local, sent = copies(ins, outs, sem_refs, False)
            for cp in copies(ins, outs, sem_refs, True)[1]:
                cp.wait_recv()
            for cp in sent:
                cp.wait_send()
            for cp in local:
                cp.wait()

        return _Ride(operands, buffers, aliases, sems, start, wait)

    def first(ins, outs, sems, landed):
        x, y, c = _place()
        peers = [(x, y, 1 - c), (1 - x, y, c), (x, 1 - y, c)]
        local = [pltpu.make_async_copy(ins[a], outs[a].at[row(x, y, c)], sems[2].at[a]) for a in range(n)]
        return local, [remote(ins[a], outs[a].at[row(*peer) if landed else row(x, y, c)], sems[0].at[a, k], sems[1].at[a, k], peer)
                       for a in range(n) for k, peer in enumerate(peers)]

    def second(ins, outs, sems, landed):
        x, y, c = _place()
        mine = 1 - c if landed else c
        copies = []
        for a in range(n):
            half = shards[a].shape[0] // 2
            over_x, over_y, diagonal = outs[a].at[row(1 - x, y, mine)], outs[a].at[row(x, 1 - y, mine)], outs[a].at[row(1 - x, 1 - y, c)]
            lower, upper = pl.ds(0, half), pl.ds(half, half)
            copies += [remote(over_x, over_x, sems[0].at[a, 0], sems[1].at[a, 0], (x, y, 1 - c)),
                       remote(over_y, over_y, sems[0].at[a, 1], sems[1].at[a, 1], (x, y, 1 - c))]
            if landed:
                copies += [remote(diagonal.at[lower], diagonal.at[lower], sems[0].at[a, 2], sems[1].at[a, 2], (1 - x, y, c)),
                           remote(diagonal.at[upper], diagonal.at[upper], sems[0].at[a, 3], sems[1].at[a, 3], (x, 1 - y, c))]
            else:
                copies += [remote(over_y.at[lower], over_y.at[lower], sems[0].at[a, 2], sems[1].at[a, 2], (1 - x, y, c)),
                           remote(over_x.at[upper], over_x.at[upper], sems[0].at[a, 3], sems[1].at[a, 3], (x, 1 - y, c))]
        return [], copies

    def third(ins, outs, sems, landed):
        x, y, c = _place()
        return [], [remote(outs[a].at[row(1 - x, 1 - y, 1 - c if landed else c)], outs[a].at[row(1 - x, 1 - y, 1 - c if landed else c)],
                           sems[0].at[a], sems[1].at[a], (x, y, 1 - c)) for a in range(n)]

    def later(copies, n_sems):
        return lambda partly: ride(partly, {i: i for i in range(n)}, [dma((n,) + n_sems), dma((n,) + n_sems)], copies)

    return ride(shards, {}, [dma((n, 3)), dma((n, 3)), dma((n,))], first), later(second, (4,)), later(third, ())


def _sibling_swap_ride(grads, halves=True):
    n = len(grads)

    def copies(ins, outs, sems):
        x, y, c = _place()
        return [pltpu.make_async_remote_copy(
            src_ref=ins[a].at[1 - c] if halves else ins[a].at[pl.ds(0, 4)], dst_ref=outs[a], send_sem=sems[0].at[a],
            recv_sem=sems[1].at[a], device_id=(x, y, 1 - c), device_id_type=MESH) for a in range(n)]

    def start(ins, outs, sems):
        for cp in copies(ins, outs, sems):
            cp.start()

    def wait(ins, outs, sems):
        for cp in copies(ins, outs, sems):
            cp.wait()

    return _Ride(grads, [_sds((4,) + g.shape[-2:], g.dtype) for g in grads], {},
                 [pltpu.SemaphoreType.DMA((n,)), pltpu.SemaphoreType.DMA((n,))], start, wait)


def _chip_swap_ride(sums):
    n = len(sums)

    def copies(ins, outs, sems, landed):
        send_sems, recv_sems, local_sems = sems
        x, y, c = _place()
        mine = 2 * x + y
        local = [pltpu.make_async_copy(ins[a].at[mine], outs[a].at[mine], local_sems.at[a]) for a in range(n)]
        remote = [pltpu.make_async_remote_copy(
            src_ref=ins[a].at[2 * px + py], dst_ref=outs[a].at[2 * px + py if landed else mine], send_sem=send_sems.at[a, j],
            recv_sem=recv_sems.at[a, j], device_id=(px, py, c), device_id_type=MESH)
            for a in range(n) for j, (px, py) in enumerate(_other_chips(x, y))]
        return local, remote

    def start(ins, outs, sems):
        local, remote = copies(ins, outs, sems, False)
        for cp in local + remote:
            cp.start()

    def wait(ins, outs, sems):
        local, sent = copies(ins, outs, sems, False)
        for cp in copies(ins, outs, sems, True)[1]:
            cp.wait_recv()
        for cp in sent:
            cp.wait_send()
        for cp in local:
            cp.wait()

    return _Ride(sums, [_sds(s.shape, s.dtype) for s in sums], {},
                 [pltpu.SemaphoreType.DMA((n, 3)), pltpu.SemaphoreType.DMA((n, 3)), pltpu.SemaphoreType.DMA((n,))], start, wait)


def _send_buffers(shards, name):
    n = len(shards)

    def body(*refs):
        for (w, transposed, rows, cols), w_ref, o_ref in zip(shards, refs[:n], refs[n:]):
            if transposed:
                c, r = w.shape
                padded = jnp.concatenate([w_ref[...], jnp.zeros((cols - c, r), F32)], axis=0) if cols > c else w_ref[...]
                o_ref[...] = padded.T.astype(BF16)
            else:
                r, c = w.shape
                if (r, c) != (rows, cols):
                    o_ref[...] = jnp.zeros((rows, cols), BF16)
                o_ref[:r, :c] = w_ref[...].astype(BF16)

    return _pallas_call(body, name=name, out_shape=[_sds((rows, cols), BF16) for _, _, rows, cols in shards])(
        *[w for w, _, _, _ in shards])


def _all_gather(shards, name):
    n = len(shards)
    first, second, third = _relayed_gather(shards)
    levels = [first, second(shards), third(shards)]
    counts = [len(level.sems) for level in levels]

    def body(*refs):
        ins, outs, sems, staged = refs[:n], refs[n:2 * n], refs[2 * n:-n], refs[-n:]
        for a in range(n):
            pltpu.sync_copy(ins[a], staged[a])
        for i, level in enumerate(levels):
            mine = sems[sum(counts[:i]):sum(counts[:i + 1])]
            level.start(staged, outs, mine)
            level.wait(staged, outs, mine)

    return _pallas_call(
        body, name=name, in_specs=[ANY] * n, out_specs=[ANY] * n, out_shape=first.results,
        scratch_shapes=[s for level in levels for s in level.sems] + [pltpu.VMEM(s.shape, s.dtype) for s in shards],
    )(*shards)


HBM = pl.BlockSpec(memory_space=pltpu.HBM)
SEMAPHORES = pl.BlockSpec(memory_space=pltpu.SEMAPHORE)
IN_FLIGHT = pltpu.CompilerParams(has_side_effects=pltpu.SideEffectType.DATAFLOW_SIDE_EFFECTING)


def _chip_swap_copies(src_refs, land_refs, send_sems, recv_sems, landed):
    x, y, c = _place()
    return [pltpu.make_async_remote_copy(
        src_ref=src.at[2 * px + py], dst_ref=land.at[2 * px + py if landed else 2 * x + y], send_sem=send_sems.at[3 * a + j],
        recv_sem=recv_sems.at[3 * a + j], device_id=(px, py, c), device_id_type=MESH)
        for a, (src, land) in enumerate(zip(src_refs, land_refs)) for j, (px, py) in enumerate(_other_chips(x, y))]


def _chip_swap_start(sums, name):
    n = len(sums)

    def body(*refs):
        src_refs, land_refs, (send_sems, recv_sems), token = refs[:n], refs[n:2 * n], refs[2 * n:2 * n + 2], refs[-1]
        for cp in _chip_swap_copies(src_refs, land_refs, send_sems, recv_sems, False):
            cp.start()
        token[...] = jnp.zeros_like(token)

    kept = [pltpu.HBM(s.shape, s.dtype) for s in sums]
    out = _pallas_call(
        body, name=name,
        out_shape=[pltpu.SemaphoreType.DMA((3 * n,)), pltpu.SemaphoreType.DMA((3 * n,))] + kept + kept + [_sds((8, 128), F32)],
        in_specs=[HBM] * (2 * n), out_specs=[SEMAPHORES, SEMAPHORES] + [HBM] * (2 * n) + [pl.BlockSpec(memory_space=pltpu.VMEM)],
        input_output_aliases={i: 2 + i for i in range(2 * n)}, compiler_params=IN_FLIGHT,
    )(*[pltpu.with_memory_space_constraint(s, pltpu.HBM) for s in sums],
      *[pltpu.with_memory_space_constraint(lax.empty(s.shape, s.dtype), pltpu.HBM) for s in sums])
    return out[0], out[1], out[2:2 + n], out[2 + n:2 + 2 * n], out[-1]


def _chip_swap_wait(send_sems, recv_sems, sums, landings, after, name):
    n = len(sums)

    def body(*refs):
        src_refs, land_refs, (send_sems, recv_sems) = refs[:n], refs[n:2 * n], refs[2 * n:2 * n + 2]
        for cp in _chip_swap_copies(src_refs, land_refs, send_sems, recv_sems, False):
            cp.wait_send()
        for cp in _chip_swap_copies(src_refs, land_refs, send_sems, recv_sems, True):
            cp.wait_recv()

    out = _pallas_call(
        body, name=name, out_shape=[pltpu.HBM(s.shape, s.dtype) for s in list(sums) + list(landings)],
        in_specs=[HBM] * (2 * n) + [SEMAPHORES, SEMAPHORES] + [ANY] * len(after), out_specs=[HBM] * (2 * n),
        input_output_aliases={i: i for i in range(2 * n)}, compiler_params=IN_FLIGHT,
    )(*sums, *landings, send_sems, recv_sems, *after)
    return out[:n], out[n:]


def _pair_sums(gs, rs, core, name):
    n_arrays = len(gs)

    def body(core_ref, *refs):
        for g_ref, r_ref, o_ref in zip(refs[:n_arrays], refs[n_arrays:2 * n_arrays], refs[2 * n_arrays:]):
            o_ref[...] = (g_ref[...].astype(F32) + r_ref[...].astype(F32)).astype(o_ref.dtype)

    def chip(g):
        return pl.BlockSpec((None,) + g.shape[-2:], lambda p, core_ref: (p, 0, 0))

    def own(g):
        if g.ndim == 3:
            return pl.BlockSpec((None,) + g.shape[-2:], lambda p, core_ref: (p + 4, 0, 0))
        return pl.BlockSpec((None, None) + g.shape[2:], lambda p, core_ref: (core_ref[0], p, 0, 0))

    return _pallas_call(
        body, name=name,
        grid_spec=pltpu.PrefetchScalarGridSpec(
            num_scalar_prefetch=1, grid=(4,), in_specs=[own(g) for g in gs] + [chip(g) for g in gs],
            out_specs=[chip(g) for g in gs]),
        out_shape=[_sds((4,) + g.shape[-2:], g.dtype) for g in gs], compiler_params=_cparams(dimension_semantics=("arbitrary",)),
    )(core, *gs, *rs)


def _adamw_math(w, g, m, v):
    m = ADAM_B1 * m + (1.0 - ADAM_B1) * g
    v = ADAM_B2 * v + (1.0 - ADAM_B2) * (g * g)
    m_hat = m / (1.0 - ADAM_B1 ** ADAM_STEP)
    v_hat = v / (1.0 - ADAM_B2 ** ADAM_STEP)
    return -ADAM_LR * (m_hat / (jnp.sqrt(v_hat) + ADAM_EPS) + ADAM_WD * w), m, v


def _adamw_many(weights, name, ride=None):
    steps = 4
    in_specs, out_specs, out_shape, operands, tiles = [], [], [], [], []
    for w, m, v, parts, own, transposed in weights:
        _, pr, pc = parts.shape
        if transposed:
            c, r = w.shape
            tile = pl.BlockSpec((c, r // steps), lambda i: (0, i))
            part_tile = pl.BlockSpec((4, r // steps, pc), lambda i: (0, i, 0))
            tiles.append((c, r // steps))
        elif w.shape[0] % (8 * steps) == 0:
            r, c = w.shape
            tile = pl.BlockSpec((r // steps, c), lambda i: (i, 0))
            part_tile = pl.BlockSpec((4, r // steps, pc), lambda i: (0, i, 0))
            tiles.append((r // steps, c))
        else:
            tile = pl.BlockSpec(w.shape, lambda i: (0, 0))
            part_tile = pl.BlockSpec(parts.shape, lambda i: (0, 0, 0))
            tiles.append(w.shape)
        in_specs += [tile, tile, tile] + [part_tile] * (1 if own is None else 2)
        out_specs += [tile] * 4
        out_shape += [_sds(w.shape, F32)] * 4
        operands += [w, m, v, parts] + ([] if own is None else [own])
    n_in = len(operands)

    def body(*refs):
        ins, outs = list(refs[:n_in]), refs[n_in:]
        this_chip = 2 * lax.axis_index("x") + lax.axis_index("y")
        for k, (_, _, _, _, own, transposed) in enumerate(weights):
            w_ref, m_ref, v_ref, p_ref = ins[:4]
            own_ref = None if own is None else ins[4]
            del ins[:4 if own is None else 5]
            rows, cols = tiles[k]
            g = None
            for q in range(4):
                index = (q,) if transposed else (q, slice(0, rows), slice(0, cols))
                part = p_ref[index] if own is None else jnp.where(this_chip == q, own_ref[index], p_ref[index])
                g = part.astype(F32) if g is None else g + part.astype(F32)
            if transposed:
                g = g.T[:rows]
            g_out, d_out, m_out, v_out = outs[4 * k:4 * k + 4]
            g_out[...] = g
            d_out[...], m_out[...], v_out[...] = _adamw_math(w_ref[...], g, m_ref[...], v_ref[...])

    return _call(body, name, (steps,), in_specs, out_specs, out_shape, [], operands, ride)


SMALL = ("ssm_a_re", "ssm_a_im", "ssm_log_dt", "ssm_b_re", "ssm_b_im", "ssm_c_re", "ssm_c_im", "ssm_d",
         "ln1_g", "ln1_b", "ln2_g", "ln2_b")


def _pack_rows(arrays):
    rows = []
    for a in arrays:
        flat = a.reshape(-1)
        rows.append(jnp.pad(flat, (0, -flat.shape[0] % 128)).reshape(-1, 128))
    packed = jnp.concatenate(rows, axis=0)
    return jnp.pad(packed, ((0, -packed.shape[0] % 8), (0, 0)))


def _unpack_rows(packed, shapes):
    out, row = [], 0
    for shape in shapes:
        size = math.prod(shape)
        n_rows = -(-size // 128)
        out.append(packed[row:row + n_rows].reshape(-1)[:size].reshape(shape))
        row += n_rows
    return out


def _sum_devices(parts):
    def body(p_ref, o_ref):
        total = p_ref[0]
        for dev in range(1, N_DEV):
            total = total + p_ref[dev]
        o_ref[...] = total

    return _pallas_call(body, name="sum_devices", out_shape=_sds(parts.shape[1:], F32))(parts)


def _adamw_replicated(ws, ms, vs, gs):
    n = len(ws)

    def body(*refs):
        w_refs, m_refs, v_refs, g_refs, d_out, m_out, v_out = (refs[i * n:(i + 1) * n] for i in range(7))
        for i in range(n):
            d_out[i][...], m_out[i][...], v_out[i][...] = _adamw_math(w_refs[i][...], g_refs[i][...], m_refs[i][...], v_refs[i][...])

    out = _pallas_call(body, name="adamw_replicated", out_shape=[_sds(w.shape, F32) for w in ws] * 3,
                       compiler_params=_cparams())(*ws, *ms, *vs, *gs)
    return out[:n], out[n:2 * n], out[2 * n:]


def kernel(x, w_in, b_gate, w_attn_br, w_ssm_br, w_out, ssm_a_re, ssm_a_im, ssm_log_dt, ssm_b_re, ssm_b_im, ssm_c_re, ssm_c_im, ssm_d, w_glu, ln1_g, ln1_b, w_ff_gate, w_ff_up, w_ff_down, ln2_g, ln2_b, loss_target, m_w_in, m_b_gate, m_w_attn_br, m_w_ssm_br, m_w_out, m_ssm_a_re, m_ssm_a_im, m_ssm_log_dt, m_ssm_b_re, m_ssm_b_im, m_ssm_c_re, m_ssm_c_im, m_ssm_d, m_w_glu, m_ln1_g, m_ln1_b, m_w_ff_gate, m_w_ff_up, m_w_ff_down, m_ln2_g, m_ln2_b, v_w_in, v_b_gate, v_w_attn_br, v_w_ssm_br, v_w_out, v_ssm_a_re, v_ssm_a_im, v_ssm_log_dt, v_ssm_b_re, v_ssm_b_im, v_ssm_c_re, v_ssm_c_im, v_ssm_d, v_w_glu, v_ln1_g, v_ln1_b, v_w_ff_gate, v_w_ff_up, v_w_ff_down, v_ln2_g, v_ln2_b):
    given = dict(locals())
    x2, target = x[0], loss_target[0]
    core = lax.axis_index("c").astype(jnp.int32).reshape(1)

    sharded = ("w_in", "w_attn_br", "w_ssm_br", "w_glu", "w_ff_gate", "w_ff_up", "b_gate", "w_out", "w_ff_down")
    send_shape = dict(w_in=(D_MODEL, 896), w_attn_br=(ATTN_WIDTH, 128), w_ssm_br=(SSM_WIDTH, 128), w_glu=(SSM_WIDTH, 128),
                      w_out=(128, D_MODEL), w_ff_gate=(D_MODEL, FF_PAD), w_ff_up=(D_MODEL, FF_PAD), w_ff_down=(FF_PAD, D_MODEL))
    local = {k: given[k][0] for k in sharded}
    narrow = ("w_ff_gate", "w_ff_up")
    def to_send(k):
        return (local[k].T, True, *send_shape[k]) if k in narrow else (local[k], False, *send_shape[k])

    later = [k for k in sharded if k not in ("w_in", "b_gate")]
    sends = dict(zip(["w_in"] + later, _send_buffers([to_send("w_in")], "send_w_in")
                     + _send_buffers([to_send(k) for k in later], "send_weights")))
    sends["b_gate"] = local["b_gate"]
    mixer_weights = ("w_attn_br", "w_ssm_br", "w_glu", "b_gate", "w_out")
    ff_weights = ("w_ff_gate", "w_ff_up", "w_ff_down")
    wt = {}
    wt["w_in"], = _all_gather([sends["w_in"]], "gather_w_in")

    a_re, a_im, log_dt = ssm_a_re[0], ssm_a_im[0], ssm_log_dt[0].reshape(SSM_GROUPS, 1)
    b_re_t, b_im_t = ssm_b_re[0].transpose(0, 2, 1), ssm_b_im[0].transpose(0, 2, 1)
    abar_re, abar_im, e_re, e_im, bbar_re_t, bbar_im_t = _ssm_prep(a_re, a_im, log_dt, b_re_t, b_im_t)
    bmat, cmat, a_chunks = _ssm_tables(abar_re, abar_im, bbar_re_t, bbar_im_t, ssm_c_re[0], ssm_c_im[0])
    cos_t, sin_t = _rope_tables()

    big_mixer, ff_in = [k for k in mixer_weights if k != "b_gate"], ("w_ff_gate", "w_ff_up")
    n_mixer = len(big_mixer)
    mixer_1, mixer_2, mixer_3 = _relayed_gather([sends[k] for k in big_mixer])
    ff_in_1, ff_in_2, ff_in_3 = _relayed_gather([sends[k] for k in ff_in])
    ff_down_1, ff_down_2, ff_down_3 = _relayed_gather([sends["w_ff_down"]])
    proj, *landed = _proj(x2, wt["w_in"], mixer_1 + _gather_first_level([sends["b_gate"]]))
    mixer, bias = landed[:n_mixer], landed[n_mixer:]
    attn, lse, q_pm, k_pm, v_pm, *landed = _attn_fwd(proj, cos_t, sin_t,
                                                     mixer_2(mixer) + _gather_second_level(bias) + ff_in_1)
    mixer, b_gate_full, ff = landed[:n_mixer], landed[n_mixer], landed[n_mixer + 1:]
    ys, states, *landed = _ssm_fwd(proj, bmat, cmat, a_chunks, ssm_d, mixer_3(mixer) + ff_in_2(ff) + ff_down_1)
    wt.update(zip(big_mixer, landed[:n_mixer]))
    ff, ff_down = landed[n_mixer:n_mixer + 2], landed[n_mixer + 2:]
    wt["w_out"] = wt["w_out"].reshape(D_MODEL, D_MODEL)
    h, xhat1, rstd1, glu, y_attn, y_ssm, *landed = _mixer_out(
        attn, ys, proj, x2, wt["w_attn_br"], wt["w_ssm_br"], wt["w_glu"], wt["w_out"], b_gate_full, ln1_g, ln1_b,
        ff_in_3(ff) + ff_down_2(ff_down))
    wt.update(zip(ff_in, landed[:2]))
    ff_a, ff_b, ff_f, w_ff_down = _ff_up(h, wt["w_ff_gate"], wt["w_ff_up"], ff_down_3(landed[2:]))
    wt["w_ff_down"] = w_ff_down.reshape(D_FF_PAD, D_MODEL)
    dr2, d_ln2_g, d_ln2_b, loss_lanes = _ff_down_loss(ff_f, wt["w_ff_down"], h, target, ln2_g, ln2_b)

    def pair_sums(names, contrib, from_sibling):
        return _pair_sums([contrib[k] for k in names], from_sibling, core, "pair_sums_" + names[0])

    d_a, d_b = _ff_down_bwd(dr2, wt["w_ff_down"], ff_a, ff_b)
    contrib = dict(w_ff_gate=_weight_grad(h, d_a, "wgrad_w_ff_gate", FF_PAD),
                   w_ff_up=_weight_grad(h, d_b, "wgrad_w_ff_up", FF_PAD),
                   w_ff_down=_weight_grad(ff_f, dr2, "wgrad_w_ff_down"))
    dr1, d_ln1_g, d_ln1_b, *from_sibling = _ff_up_bwd(
        d_a, d_b, wt["w_ff_gate"], wt["w_ff_up"], dr2, xhat1, rstd1, ln1_g, _sibling_swap_ride([contrib[k] for k in ff_weights]))
    ff_sums = pair_sums(ff_weights, contrib, from_sibling)

    d_ya, d_yssm, d_proj, d_attn, d_glu, d_ys, mixed, y_s, gy, d_bg = _mixer_bwd(
        dr1, proj, y_attn, y_ssm, glu, ys, wt["w_attn_br"], wt["w_ssm_br"], wt["w_glu"], wt["w_out"], b_gate_full)
    contrib.update(w_attn_br=_weight_grad(attn, d_ya, "wgrad_w_attn_br", 128),
                   w_ssm_br=_weight_grad(y_s, d_yssm, "wgrad_w_ssm_br", 128),
                   w_glu=_weight_grad(gy, d_glu, "wgrad_w_glu", 128),
                   w_out=_weight_grad(mixed, dr1, "wgrad_w_out"),
                   b_gate=d_bg.reshape(2, 4, 2, 128).transpose(2, 1, 0, 3))
    d_proj, *landed = _attn_bwd(q_pm, k_pm, v_pm, cos_t, sin_t, attn, lse, d_attn, d_proj,
                                _chip_swap_ride(ff_sums) + _sibling_swap_ride([contrib[k] for k in mixer_weights]))
    parts, own_sums = dict(zip(ff_weights, landed[:len(ff_weights)])), {}
    mixer_sums = pair_sums(mixer_weights, contrib, landed[len(ff_weights):])
    landed = _ssm_bwd(d_ys, proj, states, bmat, cmat, a_chunks, ssm_d, d_proj, _chip_swap_ride(mixer_sums))
    d_proj, *ssm_blocks, d_abar, d_skip = landed[:7]
    parts.update(zip(mixer_weights, landed[7:]))

    gbb_re_t, gbb_im_t, gc_re, gc_im = (blocks.reshape(SSM_GROUPS, SSM_GROUP, SSM_STATE) for blocks in ssm_blocks)
    ga_re = d_abar[:, 0, :CHUNK_STATES].reshape(SSM_GROUPS, SSM_STATE)
    ga_im = d_abar[:, 0, CHUNK_STATES:].reshape(SSM_GROUPS, SSM_STATE)
    g_a_re, g_a_im, g_log_dt, g_b_re_t, g_b_im_t = _ssm_param_bwd(
        a_re, a_im, log_dt, b_re_t, b_im_t, abar_re, abar_im, e_re, e_im, ga_re, ga_im, gbb_re_t, gbb_im_t)
    mine = [g_a_re, g_a_im, g_log_dt, g_b_re_t, g_b_im_t, gc_re, -gc_im,
            d_skip, d_ln1_g, d_ln1_b, d_ln2_g, d_ln2_b]
    small_packed = _pack_rows(mine + [loss_lanes])

    w_in_contrib, small_partly = _weight_grad_rows(x2, d_proj, core, "wgrad_w_in_first", 896, 0, 6,
                                                   _gather_first_level([small_packed]))
    w_in_contrib, from_sibling, every = _weight_grad_rows(
        x2, d_proj, core, "wgrad_w_in_rest", 896, 6, 2,
        _sibling_swap_ride([w_in_contrib], halves=False) + _gather_second_level([small_partly]), in_place=True)
    w_in_sum, = _pair_sums([w_in_contrib], [from_sibling], core, "pair_sums_w_in")
    send_sems, recv_sems, w_in_sum, landing, token = _chip_swap_start([w_in_sum], "w_in_chip_swap_start")

    def adamw_of(k):
        taken = (lambda a: a.T) if k in narrow else (lambda a: a)
        return taken(local[k]), taken(given["m_" + k][0]), taken(given["v_" + k][0]), parts[k], own_sums.get(k), k in narrow

    others = [k for k in sharded if k != "w_in"]
    updated = _adamw_many([adamw_of(k) for k in others], "adamw_others", _after(token))
    grad_x, = _grad_x(d_proj, wt["w_in"], dr1, _after(token))

    def held(k, a):
        return a.transpose(0, 1, 3, 2) if k in ("ssm_b_re", "ssm_b_im") else a

    *small_grads, loss_sum = _unpack_rows(_sum_devices(every), [held(k, given[k]).shape for k in SMALL] + [(1, 128)])
    small = _adamw_replicated([held(k, given[k]) for k in SMALL], [held(k, given["m_" + k]) for k in SMALL],
                              [held(k, given["v_" + k]) for k in SMALL], small_grads)
    loss = loss_sum[0, 0]

    (own_sums["w_in"],), (parts["w_in"],) = _chip_swap_wait(
        send_sems, recv_sems, w_in_sum, landing, [grad_x, updated[0], small[0][0]], "w_in_chip_swap_wait")
    updated += _adamw_many([adamw_of("w_in")], "adamw_w_in")

    grads, deltas, new_m, new_v = {}, {}, {}, {}
    for i, k in enumerate(others + ["w_in"]):
        out = [o.T if k in narrow else o for o in updated[4 * i:4 * i + 4]]
        grads[k], deltas[k], new_m[k], new_v[k] = (o.reshape((1,) + local[k].shape) for o in out)
    for res, values in zip((grads, deltas, new_m, new_v), (small_grads,) + small):
        res.update((k, held(k, a)) for k, a in zip(SMALL, values))

    order = ("w_in", "b_gate", "w_attn_br", "w_ssm_br", "w_out", "ssm_a_re", "ssm_a_im", "ssm_log_dt", "ssm_b_re", "ssm_b_im",
             "ssm_c_re", "ssm_c_im", "ssm_d", "w_glu", "ln1_g", "ln1_b", "w_ff_gate", "w_ff_up", "w_ff_down", "ln2_g", "ln2_b")
    return (loss, grad_x[None], *[grads[k] for k in order], *[deltas[k] for k in order], *[new_m[k] for k in order],
            *[new_v[k] for k in order])
```

```python
import functools
import math

import jax
import jax.numpy as jnp
import numpy as np
from jax import lax
from jax.experimental import pallas as pl
from jax.experimental.pallas import tpu as pltpu

F32 = jnp.float32
BF16 = jnp.bfloat16

N_DEV = 8
SEQ = 2048
D_MODEL = 1024
HEAD_DIM = 64
ATTN_WIDTH = 512
QKV_WIDTH = 1536
SSM_WIDTH = 512
SSM_GROUPS = 32
SSM_GROUP = 16
SSM_STATE = 64
IN_WIDTH = 7168
D_FF = 2816
FF_SHARD = D_FF // N_DEV
FF_PAD = 384
D_FF_PAD = FF_PAD * N_DEV
DN_ALPHA = 2.0 ** 0.25
LN_EPS = 1e-5
NEG_INF = -1e30
ROPE_THETA = 10000.0
BLOCK = 128
GROUPS = ((1, 16), (4, 4), (16, 1))

ADAM_LR = 0.001
ADAM_B1 = 0.9
ADAM_B2 = 0.999
ADAM_EPS = 1e-08
ADAM_WD = 0.01
ADAM_STEP = 10

VMEM_LIMIT = 56 * 1024 * 1024


_pallas_call = pl.pallas_call


def _cparams(**kw):
    return pltpu.CompilerParams(vmem_limit_bytes=VMEM_LIMIT, **kw)


def _dot(a, b):
    return jnp.dot(a, b, preferred_element_type=F32)


def _dot_nt(a, b):
    return lax.dot_general(a, b, (((1,), (1,)), ((), ())), preferred_element_type=F32)


def _side_by_side(w_ref, row=None):
    rows = slice(None) if row is None else pl.ds(row, 1)
    return jnp.concatenate([w_ref[i, rows, :] for i in range(w_ref.shape[0])], axis=1)


def _dot_tn(a, b):
    return lax.dot_general(a, b, (((0,), (0,)), ((), ())), preferred_element_type=F32)


def _rope_tables():
    half = HEAD_DIM // 2
    inv_freq = np.float32(ROPE_THETA) ** (-np.arange(half, dtype=np.float32) / np.float32(half))
    ang = np.arange(SEQ, dtype=np.float32)[:, None] * inv_freq[None, :]
    cos, sin = np.cos(ang).astype(np.float32), np.sin(ang).astype(np.float32)
    tables = np.tile(cos, (1, 4)), np.tile(np.concatenate([-sin, sin], axis=1), (1, 2))

    def by_phase(t):
        return np.stack([t.reshape(SEQ // d, d, 128).transpose(1, 0, 2).reshape(SEQ, 128) for d, _ in GROUPS])

    return jnp.asarray(by_phase(tables[0])), jnp.asarray(by_phase(tables[1]))


def _swap_halves(x):
    lane = lax.broadcasted_iota(jnp.int32, x.shape, 1)
    return jnp.where((lane & 63) < 32, pltpu.roll(x, 96, axis=1), pltpu.roll(x, 32, axis=1))


def _group_rows(d, nb, r, i):
    src = pl.ds(i * BLOCK, BLOCK) if d == 1 else pl.ds(r + i * BLOCK * d, BLOCK, stride=d)
    return src, pl.ds((r * nb + i) * BLOCK, BLOCK)


def _attn_masks():
    a_idx = lax.broadcasted_iota(jnp.int32, (2 * BLOCK, 2 * BLOCK), 0) & (BLOCK - 1)
    c_idx = lax.broadcasted_iota(jnp.int32, (2 * BLOCK, 2 * BLOCK), 1)
    cur_ok = jnp.logical_and(c_idx >= BLOCK, c_idx - BLOCK <= a_idx)
    prev_ok = jnp.logical_and(c_idx < BLOCK, c_idx >= a_idx)
    lane = lax.broadcasted_iota(jnp.int32, (BLOCK, 128), 1)
    return cur_ok, prev_ok, lane < HEAD_DIM


def _stack_heads(t, head0):
    zero = jnp.zeros_like(t)
    return jnp.concatenate([jnp.where(head0, t, zero), jnp.where(head0, zero, t)], axis=0)


def _unstack_heads(t2, head0):
    return jnp.where(head0, t2[:BLOCK], t2[BLOCK:])


def _attn_fwd(proj, cos_t, sin_t, ride=None):
    def body(q0, q1, q2, k0, k1, k2, v0, v1, v2, cos_ref, sin_ref, attn_ref, lse_ref, qpm_ref, kpm_ref, vpm_ref,
             qs, ks, vs, os_, ms, ls, acc, mnat, lnat):
        cur_ok, prev_ok, head0 = _attn_masks()
        ks[:BLOCK, :] = jnp.zeros((BLOCK, 128), BF16)
        vs[:BLOCK, :] = jnp.zeros((BLOCK, 128), BF16)
        for g, (d, nb) in enumerate(GROUPS):
            q_ref, k_ref, v_ref = (q0, q1, q2)[g], (k0, k1, k2)[g], (v0, v1, v2)[g]
            for r in range(d):
                for i in range(nb):
                    src, dst = _group_rows(d, nb, r, i)
                    below = pl.ds(dst.start + BLOCK, BLOCK)
                    c, s = cos_ref[g, dst, :], sin_ref[g, dst, :]
                    q = q_ref[src, :]
                    k = k_ref[src, :]
                    qs[dst, :] = ((q * c + _swap_halves(q) * s) * 0.125).astype(BF16)
                    ks[below, :] = (k * c + _swap_halves(k) * s).astype(BF16)
                    vs[below, :] = v_ref[src, :].astype(BF16)
                    qpm_ref[g, dst, :], kpm_ref[g, dst, :], vpm_ref[g, dst, :] = qs[dst, :], ks[below, :], vs[below, :]

            def block(b, carry, nb=nb):
                has_prev = (b & (nb - 1)) > 0
                cur = pl.ds(pl.multiple_of(b * BLOCK, BLOCK), BLOCK)
                window = pl.ds(pl.multiple_of(b * BLOCK, BLOCK), 2 * BLOCK)
                valid = jnp.logical_or(cur_ok, jnp.logical_and(prev_ok, has_prev))
                s = jnp.where(valid, _dot_nt(_stack_heads(qs[cur, :], head0), ks[window, :]), NEG_INF)
                m = jnp.max(s, axis=1, keepdims=True)
                p = jnp.exp(s - m)
                os_[cur, :] = _unstack_heads(_dot(p.astype(BF16), vs[window, :]), head0)
                ms[cur, :] = _unstack_heads(m, head0)
                ls[cur, :] = _unstack_heads(jnp.sum(p, axis=1, keepdims=True), head0)
                return carry

            lax.fori_loop(0, SEQ // BLOCK, block, 0, unroll=16)

            for r in range(d):
                for i in range(nb):
                    src, dst = _group_rows(d, nb, r, i)
                    if g == 0:
                        acc[src, :], mnat[src, :], lnat[src, :] = os_[dst, :], ms[dst, :], ls[dst, :]
                    else:
                        m_old, m_g = mnat[src, :], ms[dst, :]
                        m_new = jnp.maximum(m_old, m_g)
                        a_old, a_g = jnp.exp(m_old - m_new), jnp.exp(m_g - m_new)
                        acc[src, :] = a_old * acc[src, :] + a_g * os_[dst, :]
                        lnat[src, :] = a_old * lnat[src, :] + a_g * ls[dst, :]
                        mnat[src, :] = m_new
        for i in range(SEQ // BLOCK):
            rows = pl.ds(i * BLOCK, BLOCK)
            l = lnat[rows, :]
            attn_ref[rows, :] = acc[rows, :] / l
            lse_ref[rows, :] = mnat[rows, :] + jnp.log(l)

    def col(base):
        return pl.BlockSpec((SEQ, 128), lambda hp, base=base: (0, base + hp))

    in_specs = [col(g * 4) for g in range(3)] + [col(12 + g * 4) for g in range(3)] + [col(24 + g * 4) for g in range(3)]
    table = pl.BlockSpec((3, SEQ, 128), lambda hp: (0, 0, 0), pipeline_mode=pl.Buffered(1))
    out = pl.BlockSpec((SEQ, 128), lambda hp: (0, hp))
    by_phase = pl.BlockSpec((3, SEQ, 128), lambda hp: (0, 0, hp))
    return _call(
        body, "attn_fwd", (4,), in_specs + [table, table], [out, out] + [by_phase] * 3,
        [_sds((SEQ, ATTN_WIDTH), F32), _sds((SEQ, ATTN_WIDTH), F32)] + [_sds((3, SEQ, ATTN_WIDTH), BF16)] * 3,
        [pltpu.VMEM((SEQ, 128), BF16)] + [pltpu.VMEM((SEQ + BLOCK, 128), BF16)] * 2 + [pltpu.VMEM((SEQ, 128), F32)] * 6,
        [proj] * 9 + [cos_t, sin_t], ride)


def _attn_bwd_group_body(g):
    d, nb = GROUPS[g]

    def body(qs_ref, ks_ref, vs_ref, cos_ref, sin_ref, lse_ref, dattn_ref, dsum_ref, dproj_ref,
             ks, vs, dos, lss, dss, dqs, dks, dvs, stage, outs, sems):
        cur_ok, prev_ok, head0 = _attn_masks()
        qs = qs_ref.at[g]
        ks[:BLOCK, :] = jnp.zeros((BLOCK, 128), BF16)
        vs[:BLOCK, :] = jnp.zeros((BLOCK, 128), BF16)
        dks[:BLOCK, :] = jnp.zeros((BLOCK, 128), F32)
        dvs[:BLOCK, :] = jnp.zeros((BLOCK, 128), F32)
        for r in range(d):
            for i in range(nb):
                src, dst = _group_rows(d, nb, r, i)
                below = pl.ds(dst.start + BLOCK, BLOCK)
                ks[below, :] = ks_ref[g, dst, :]
                vs[below, :] = vs_ref[g, dst, :]
                dos[dst, :] = dattn_ref[src, :].astype(BF16)
                for per_head, spread in ((dsum_ref[src, :], dss), (lse_ref[src, :], lss)):
                    other = pltpu.roll(per_head, HEAD_DIM, axis=1)
                    spread[0, dst, :] = jnp.where(head0, per_head, other)
                    spread[1, dst, :] = jnp.where(head0, other, per_head)
                dks[below, :] = jnp.zeros((BLOCK, 128), F32)
                dvs[below, :] = jnp.zeros((BLOCK, 128), F32)

        def per_stacked_row(spread, cur):
            h0, h1 = spread[0, cur, :], spread[1, cur, :]
            return jnp.concatenate([jnp.concatenate([h0, h0], axis=1), jnp.concatenate([h1, h1], axis=1)], axis=0)

        def block(b, carry):
            has_prev = (b & (nb - 1)) > 0
            cur = pl.ds(pl.multiple_of(b * BLOCK, BLOCK), BLOCK)
            window = pl.ds(pl.multiple_of(b * BLOCK, BLOCK), 2 * BLOCK)
            valid = jnp.logical_or(cur_ok, jnp.logical_and(prev_ok, has_prev))
            q2, do2 = _stack_heads(qs[cur, :], head0), _stack_heads(dos[cur, :], head0)
            kw, vw = ks[window, :], vs[window, :]
            s = jnp.where(valid, _dot_nt(q2, kw), NEG_INF)
            p = jnp.exp(s - per_stacked_row(lss, cur))
            ds = (p * (_dot_nt(do2, vw) - per_stacked_row(dss, cur))).astype(BF16)
            dvs[window, :] += _dot_tn(p.astype(BF16), do2)
            dks[window, :] += _dot_tn(ds, q2)
            dqs[cur, :] = _unstack_heads(_dot(ds, kw), head0)
            return carry

        lax.fori_loop(0, SEQ // BLOCK, block, 0, unroll=16)

        hp = pl.program_id(0)
        copies = []
        for kind in range(3):
            for r in range(d):
                for i in range(nb):
                    src, dst = _group_rows(d, nb, r, i)
                    below = pl.ds(dst.start + BLOCK, BLOCK)
                    if kind == 2:
                        stage[src, :] = dvs[below, :]
                    else:
                        c, s = cos_ref[g, dst, :], sin_ref[g, dst, :]
                        t = dqs[dst, :] * 0.125 if kind == 0 else dks[below, :]
                        stage[src, :] = t * c - _swap_halves(t) * s
            for i in range(SEQ // MM_ROWS):
                rows = pl.ds(i * MM_ROWS, MM_ROWS)
                outs[kind, rows, :] = stage[rows, :].astype(BF16)
            column = pl.multiple_of((kind * 12 + g * 4 + hp) * 128, 128)
            copies.append(pltpu.make_async_copy(outs.at[kind], dproj_ref.at[:, pl.ds(column, 128)], sems.at[kind]))
            copies[-1].start()
        for cp in copies:
            cp.wait()

    return body


def _attn_bwd(q_pm, k_pm, v_pm, cos_t, sin_t, attn, lse, dattn, dproj, ride=None):
    groups = [_attn_bwd_group_body(g) for g in range(3)]

    def body(qs_ref, ks_ref, vs_ref, cos_ref, sin_ref, attn_ref, lse_ref, dattn_ref, dproj_in, dproj_ref, dsum, *scratch):
        del dproj_in
        head0 = _attn_masks()[2]
        for i in range(SEQ // BLOCK):
            rows = pl.ds(i * BLOCK, BLOCK)
            prod = dattn_ref[rows, :] * attn_ref[rows, :]
            d0 = jnp.sum(jnp.where(head0, prod, 0.0), axis=1, keepdims=True)
            d1 = jnp.sum(jnp.where(head0, 0.0, prod), axis=1, keepdims=True)
            dsum[rows, :] = jnp.where(head0, d0, d1)
        for g in range(3):
            groups[g](qs_ref, ks_ref, vs_ref, cos_ref, sin_ref, lse_ref, dattn_ref, dsum, dproj_ref, *scratch)

    def col(base):
        return pl.BlockSpec((SEQ, 128), lambda hp, base=base: (0, base + hp))

    table = pl.BlockSpec((3, SEQ, 128), lambda hp: (0, 0, 0), pipeline_mode=pl.Buffered(1))
    by_phase = pl.BlockSpec((3, SEQ, 128), lambda hp: (0, 0, hp))
    return _call(
        body, "attn_bwd", (4,), [by_phase] * 3 + [table, table, col(0), col(0), col(0), ANY],
        [ANY], [_sds((SEQ, IN_WIDTH), BF16)],
        [pltpu.VMEM((SEQ, 128), F32)]
        + [pltpu.VMEM((SEQ + BLOCK, 128), BF16)] * 2 + [pltpu.VMEM((SEQ, 128), BF16)]
        + [pltpu.VMEM((2, SEQ, 128), F32)] * 2 + [pltpu.VMEM((SEQ, 128), F32)]
        + [pltpu.VMEM((SEQ + BLOCK, 128), F32)] * 2 + [pltpu.VMEM((SEQ, 128), F32)]
        + [pltpu.VMEM((3, SEQ, 128), BF16), pltpu.SemaphoreType.DMA((3,))],
        [q_pm, k_pm, v_pm, cos_t, sin_t, attn, lse, dattn, dproj], ride, aliases={8: 0})


SSM_CHUNKS = 4
CHUNK_STATES = 512
SCAN_ROWS = 8
U_COL = (3 * QKV_WIDTH) // 128


def _cmul(xr, xi, yr, yi):
    return xr * yr - xi * yi, xr * yi + xi * yr


def _ssm_prep(a_re, a_im, log_dt, b_re_t, b_im_t):
    def body(ar_ref, ai_ref, ldt_ref, br_ref, bi_ref, abr_ref, abi_ref, er_ref, ei_ref, bbr_ref, bbi_ref):
        ar, ai = ar_ref[...], ai_ref[...]
        dt = jnp.exp(ldt_ref[...])
        mag = jnp.exp(ar * dt)
        abr, abi = mag * jnp.cos(ai * dt), mag * jnp.sin(ai * dt)
        den = ar * ar + ai * ai
        nr, ni = abr - 1.0, abi
        er, ei = (nr * ar + ni * ai) / den, (ni * ar - nr * ai) / den
        abr_ref[...], abi_ref[...], er_ref[...], ei_ref[...] = abr, abi, er, ei
        er3, ei3 = er[:, None, :], ei[:, None, :]
        br, bi = br_ref[...], bi_ref[...]
        bbr_ref[...] = er3 * br - ei3 * bi
        bbi_ref[...] = er3 * bi + ei3 * br

    gp = jax.ShapeDtypeStruct(a_re.shape, F32)
    gb = jax.ShapeDtypeStruct(b_re_t.shape, F32)
    return _pallas_call(body, name="ssm_prep", out_shape=(gp, gp, gp, gp, gb, gb))(a_re, a_im, log_dt, b_re_t, b_im_t)


def _ssm_param_bwd(a_re, a_im, log_dt, b_re_t, b_im_t, abar_re, abar_im, e_re, e_im, ga_re, ga_im, gbb_re_t, gbb_im_t):
    def body(ar_ref, ai_ref, ldt_ref, br_ref, bi_ref, abr_ref, abi_ref, er_ref, ei_ref, gar_ref, gai_ref, gbr_ref, gbi_ref,
             o_ar, o_ai, o_ldt, o_br, o_bi):
        ar, ai = ar_ref[...], ai_ref[...]
        dt = jnp.exp(ldt_ref[...])
        er, ei = er_ref[...], ei_ref[...]
        br, bi, gbr, gbi = br_ref[...], bi_ref[...], gbr_ref[...], gbi_ref[...]
        er3, ei3 = er[:, None, :], ei[:, None, :]
        o_br[...] = er3 * gbr + ei3 * gbi
        o_bi[...] = er3 * gbi - ei3 * gbr
        ge_r = jnp.sum(br * gbr + bi * gbi, axis=1)
        ge_i = jnp.sum(br * gbi - bi * gbr, axis=1)
        den = ar * ar + ai * ai
        ilr, ili = ar / den, -ai / den
        t_r, t_i = _cmul(ilr, -ili, ge_r, ge_i)
        gab_r, gab_i = gar_ref[...] + t_r, gai_ref[...] + t_i
        gz_r, gz_i = _cmul(abr_ref[...], -abi_ref[...], gab_r, gab_i)
        el_r, el_i = _cmul(er, ei, ilr, ili)
        u_r, u_i = _cmul(el_r, -el_i, ge_r, ge_i)
        o_ar[...] = dt * gz_r - u_r
        o_ai[...] = dt * gz_i - u_i
        o_ldt[...] = jnp.sum(gz_r * ar + gz_i * ai, axis=1, keepdims=True) * dt

    gp = jax.ShapeDtypeStruct(a_re.shape, F32)
    gb = jax.ShapeDtypeStruct(b_re_t.shape, F32)
    return _pallas_call(body, name="ssm_param_bwd", out_shape=(gp, gp, jax.ShapeDtypeStruct(log_dt.shape, F32), gb, gb))(
        a_re, a_im, log_dt, b_re_t, b_im_t, abar_re, abar_im, e_re, e_im, ga_re, ga_im, gbb_re_t, gbb_im_t)


def _block_diag(blocks_re, blocks_im, sign_im, rows_are_channels):
    both = jnp.stack([blocks_re, sign_im * blocks_im]).reshape(2, SSM_CHUNKS, 8, SSM_GROUP, SSM_STATE)
    eye = jnp.eye(8, dtype=F32)
    if rows_are_channels:
        return jnp.einsum("rcghp,gk->cghrkp", both, eye).reshape(SSM_CHUNKS, 128, 2 * CHUNK_STATES)
    return jnp.einsum("rcghp,gk->crkpgh", both, eye).reshape(SSM_CHUNKS, 2 * CHUNK_STATES, 128)


def _diagonal_blocks(mat, part):
    first = [part * CHUNK_STATES + SSM_STATE * g for g in range(8)]
    return jnp.concatenate([mat[SSM_GROUP * g:SSM_GROUP * (g + 1), first[g]:first[g] + SSM_STATE] for g in range(8)], axis=0)


def _scan_consts(a_ref, conj, reverse):
    ar = jnp.broadcast_to(a_ref[:, :CHUNK_STATES], (SCAN_ROWS, CHUNK_STATES))
    ai = jnp.broadcast_to(a_ref[:, CHUNK_STATES:], (SCAN_ROWS, CHUNK_STATES))
    if conj:
        ai = -ai
    row = lax.broadcasted_iota(jnp.int32, (SCAN_ROWS, CHUNK_STATES), 0)
    if reverse:
        row = SCAN_ROWS - 1 - row
    zero = jnp.zeros_like(ar)
    steps = []
    pr, pi = ar, ai
    for shift in (1, 2, 4):
        keep = row >= shift
        steps.append((SCAN_ROWS - shift if reverse else shift, jnp.where(keep, pr, zero), jnp.where(keep, pi, zero)))
        pr, pi = _cmul(pr, pi, pr, pi)
    first = row == 0
    return steps, (jnp.where(first, ar, zero), jnp.where(first, ai, zero)), first


def _scan_tile(xr, xi, prev_r, prev_i, steps, carry_in, reverse):
    edge = SCAN_ROWS - 1 if reverse else 1
    cr, ci = pltpu.roll(prev_r, edge, axis=0), pltpu.roll(prev_i, edge, axis=0)
    xr, xi = xr + carry_in[0] * cr - carry_in[1] * ci, xi + carry_in[0] * ci + carry_in[1] * cr
    for shift, mr, mi in steps:
        sr, si = pltpu.roll(xr, shift, axis=0), pltpu.roll(xi, shift, axis=0)
        xr, xi = xr + mr * sr - mi * si, xi + mr * si + mi * sr
    return xr, xi


MM_ROWS = 256


def _ssm_fwd(proj, bmat, cmat, a_chunks, d_skip, ride=None):
    def body(u_ref, b_ref, c_ref, a_ref, d_ref, y_ref, states_ref, h_ref):
        for i in range(SEQ // MM_ROWS):
            rows = pl.ds(i * MM_ROWS, MM_ROWS)
            h_ref[rows, :] = _dot(u_ref[rows, :].astype(BF16), b_ref[...])
        steps, carry_in, _ = _scan_consts(a_ref, conj=False, reverse=False)

        def tile(k, carry):
            rows = pl.ds(pl.multiple_of(k * SCAN_ROWS, SCAN_ROWS), SCAN_ROWS)
            xr, xi = _scan_tile(h_ref[rows, :CHUNK_STATES], h_ref[rows, CHUNK_STATES:], carry[0], carry[1], steps, carry_in, False)
            h_ref[rows, :CHUNK_STATES] = xr
            h_ref[rows, CHUNK_STATES:] = xi
            return xr, xi

        zero = jnp.zeros((SCAN_ROWS, CHUNK_STATES), F32)
        lax.fori_loop(0, SEQ // SCAN_ROWS, tile, (zero, zero), unroll=4)
        for i in range(SEQ // MM_ROWS):
            rows = pl.ds(i * MM_ROWS, MM_ROWS)
            states = h_ref[rows, :].astype(BF16)
            states_ref[rows, :] = states
            y_ref[rows, :] = _dot(states, c_ref[...]) + d_ref[...] * u_ref[rows, :]

    return _call(
        body, "ssm_fwd", (SSM_CHUNKS,),
        [pl.BlockSpec((SEQ, 128), lambda c: (0, U_COL + c)),
         pl.BlockSpec((None, 128, 2 * CHUNK_STATES), lambda c: (c, 0, 0)),
         pl.BlockSpec((None, 2 * CHUNK_STATES, 128), lambda c: (c, 0, 0)),
         pl.BlockSpec((None, 1, 2 * CHUNK_STATES), lambda c: (c, 0, 0)),
         pl.BlockSpec((1, 128), lambda c: (0, c))],
        [pl.BlockSpec((SEQ, 128), lambda c: (0, c)), pl.BlockSpec((SEQ, 2 * CHUNK_STATES), lambda c: (0, c))],
        [_sds((SEQ, SSM_WIDTH), F32), _sds((SEQ, SSM_CHUNKS * 2 * CHUNK_STATES), BF16)],
        [pltpu.VMEM((SEQ, 2 * CHUNK_STATES), F32)],
        [proj, bmat, cmat, a_chunks, d_skip], ride)


def _ssm_bwd(dys, proj, h, bmat, cmat, a_chunks, d_skip, dproj, ride=None):
    def body(dy_ref, u_ref, states_ref, b_ref, c_ref, a_ref, d_ref, dproj_in, du_ref, db_re_ref, db_im_ref, dc_re_ref, dc_im_ref,
             da_ref, dd_ref, g_ref, h_ref):
        del dproj_in
        dsum = jnp.zeros((1, 128), F32)
        dcm = jnp.zeros((128, 2 * CHUNK_STATES), F32)
        for i in range(SEQ // MM_ROWS):
            rows = pl.ds(i * MM_ROWS, MM_ROWS)
            h_ref[rows, :] = states_ref[rows, :].astype(F32)
            dy = dy_ref[rows, :]
            g_ref[rows, :] = _dot_nt(dy.astype(BF16), c_ref[...])
            dsum += jnp.sum(dy * u_ref[rows, :], axis=0, keepdims=True)
            dcm += _dot_tn(dy.astype(BF16), states_ref[rows, :])
        dd_ref[...] = dsum
        dc_re_ref[...] = _diagonal_blocks(dcm, 0)
        dc_im_ref[...] = _diagonal_blocks(dcm, 1)
        steps, carry_in, _ = _scan_consts(a_ref, conj=True, reverse=True)
        first_row = lax.broadcasted_iota(jnp.int32, (SCAN_ROWS, CHUNK_STATES), 0) == 0
        n_tiles = SEQ // SCAN_ROWS

        def tile(j, carry):
            k = n_tiles - 1 - j
            rows = pl.ds(pl.multiple_of(k * SCAN_ROWS, SCAN_ROWS), SCAN_ROWS)
            before = pl.ds(pl.multiple_of(jnp.maximum(k - 1, 0) * SCAN_ROWS, SCAN_ROWS), SCAN_ROWS)
            gr, gi = _scan_tile(g_ref[rows, :CHUNK_STATES], g_ref[rows, CHUNK_STATES:], carry[0], carry[1], steps, carry_in, True)
            g_ref[rows, :CHUNK_STATES] = gr
            g_ref[rows, CHUNK_STATES:] = gi
            has_before = jnp.where(k > 0, 1.0, 0.0)
            hr = jnp.where(first_row, pltpu.roll(h_ref[before, :CHUNK_STATES], 1, axis=0) * has_before,
                           pltpu.roll(h_ref[rows, :CHUNK_STATES], 1, axis=0))
            hi = jnp.where(first_row, pltpu.roll(h_ref[before, CHUNK_STATES:], 1, axis=0) * has_before,
                           pltpu.roll(h_ref[rows, CHUNK_STATES:], 1, axis=0))
            return gr, gi, carry[2] + hr * gr + hi * gi, carry[3] + hr * gi - hi * gr

        zero = jnp.zeros((SCAN_ROWS, CHUNK_STATES), F32)
        _, _, sar, sai = lax.fori_loop(0, n_tiles, tile, (zero, zero, zero, zero), unroll=4)
        da_ref[:, :CHUNK_STATES] = jnp.sum(sar, axis=0, keepdims=True)
        da_ref[:, CHUNK_STATES:] = jnp.sum(sai, axis=0, keepdims=True)
        dbm = jnp.zeros((128, 2 * CHUNK_STATES), F32)
        for i in range(SEQ // MM_ROWS):
            rows = pl.ds(i * MM_ROWS, MM_ROWS)
            g = g_ref[rows, :].astype(BF16)
            du_ref[rows, :] = (_dot_nt(g, b_ref[...]) + d_ref[...] * dy_ref[rows, :]).astype(BF16)
            dbm += _dot_tn(u_ref[rows, :].astype(BF16), g)
        db_re_ref[...] = _diagonal_blocks(dbm, 0)
        db_im_ref[...] = _diagonal_blocks(dbm, 1)

    chunk_col = pl.BlockSpec((SEQ, 128), lambda c: (0, c))
    blocks = pl.BlockSpec((None, 128, SSM_STATE), lambda c: (c, 0, 0))
    return _call(
        body, "ssm_bwd", (SSM_CHUNKS,),
        [chunk_col,
         pl.BlockSpec((SEQ, 128), lambda c: (0, U_COL + c)),
         pl.BlockSpec((SEQ, 2 * CHUNK_STATES), lambda c: (0, c)),
         pl.BlockSpec((None, 128, 2 * CHUNK_STATES), lambda c: (c, 0, 0)),
         pl.BlockSpec((None, 2 * CHUNK_STATES, 128), lambda c: (c, 0, 0)),
         pl.BlockSpec((None, 1, 2 * CHUNK_STATES), lambda c: (c, 0, 0)),
         pl.BlockSpec((1, 128), lambda c: (0, c)), ANY],
        [pl.BlockSpec((SEQ, 128), lambda c: (0, U_COL + c)), blocks, blocks, blocks, blocks,
         pl.BlockSpec((None, 1, 2 * CHUNK_STATES), lambda c: (c, 0, 0)),
         pl.BlockSpec((1, 128), lambda c: (0, c))],
        [_sds((SEQ, IN_WIDTH), BF16)] + [_sds((SSM_CHUNKS, 128, SSM_STATE), F32)] * 4
        + [_sds((SSM_CHUNKS, 1, 2 * CHUNK_STATES), F32), _sds((1, SSM_WIDTH), F32)],
        [pltpu.VMEM((SEQ, 2 * CHUNK_STATES), F32)] * 2, [dys, proj, h, bmat, cmat, a_chunks, d_skip, dproj], ride, aliases={7: 0})


def _ssm_tables(abar_re, abar_im, bbar_re_t, bbar_im_t, c_re, c_im):
    bmat = _block_diag(bbar_re_t, bbar_im_t, 1.0, True).astype(BF16)
    cmat = _block_diag(c_re, c_im, -1.0, False).astype(BF16)
    a_chunks = jnp.concatenate([abar_re.reshape(SSM_CHUNKS, 1, CHUNK_STATES), abar_im.reshape(SSM_CHUNKS, 1, CHUNK_STATES)], axis=2)
    return bmat, cmat, a_chunks


GL_COL = (3 * QKV_WIDTH + SSM_WIDTH) // D_MODEL
GELU_C = math.sqrt(2.0 / math.pi)
GELU_A = 0.044715


def _sds(shape, dtype):
    return jax.ShapeDtypeStruct(shape, dtype)


def _gelu(x):
    t = jnp.tanh(GELU_C * (x + GELU_A * x * x * x))
    return 0.5 * x * (1.0 + t), t


def _gelu_grad(x, t):
    return 0.5 * (1.0 + t) + 0.5 * x * (1.0 - t * t) * GELU_C * (1.0 + 3.0 * GELU_A * x * x)


def _layer_norm(r, g, b):
    mu = jnp.mean(r, axis=-1, keepdims=True)
    xc = r - mu
    rstd = lax.rsqrt(jnp.mean(xc * xc, axis=-1, keepdims=True) + LN_EPS)
    xhat = xc * rstd
    return xhat * g + b, xhat, rstd


def _layer_norm_bwd(dy, xhat, rstd, g):
    dxhat = dy * g
    m1 = jnp.mean(dxhat, axis=-1, keepdims=True)
    m2 = jnp.mean(dxhat * xhat, axis=-1, keepdims=True)
    return rstd * (dxhat - m1 - xhat * m2)


def _proj(x, w_in, ride=None):
    tm, tn = 1024, 1792

    def body(x_ref, w_ref, o_ref):
        o_ref[...] = _dot(x_ref[...].astype(BF16), _side_by_side(w_ref))

    return _call(
        body, "proj", (SEQ // tm, IN_WIDTH // tn),
        [pl.BlockSpec((tm, D_MODEL), lambda i, j: (i, 0)), pl.BlockSpec((2, D_MODEL, tn // 2), lambda i, j: (j, 0, 0))],
        [pl.BlockSpec((tm, tn), lambda i, j: (i, j))], [_sds((SEQ, IN_WIDTH), F32)], [], [x, w_in], ride)


def _row_spec(tm, width, col=0):
    return pl.BlockSpec((tm, width), lambda i, col=col: (i, col))


def _full_spec(shape):
    return pl.BlockSpec(shape, lambda i: (0,) * len(shape))


def _weight_spec(shape):
    return pl.BlockSpec(shape, lambda i: (0,) * len(shape), pipeline_mode=pl.Buffered(1))


def _mixer_out(attn, ys, proj, x, w_ab, w_sb, w_glu, w_out, b_gate, ln_g, ln_b, ride=None):
    tm = 512

    def body(attn_ref, ys_ref, gl0_ref, gl1_ref, x_ref, wab_ref, wsb_ref, wglu_ref, wout_ref, bg_ref, g_ref, b_ref,
             h_ref, xhat_ref, rstd_ref, glu_ref, ya_ref, yssm_ref):
        gy, _ = _gelu(ys_ref[...])
        glu = _dot(gy.astype(BF16), _side_by_side(wglu_ref))
        glu_ref[...] = glu.astype(BF16)
        y_s = glu[:, :SSM_WIDTH] * jax.nn.sigmoid(glu[:, SSM_WIDTH:])
        y_ssm = _dot(y_s.astype(BF16), _side_by_side(wsb_ref))
        y_attn = _dot(attn_ref[...].astype(BF16), _side_by_side(wab_ref))
        ya_ref[...] = y_attn.astype(BF16)
        yssm_ref[...] = y_ssm.astype(BF16)
        g0 = jax.nn.sigmoid(gl0_ref[...] + _side_by_side(bg_ref, 0))
        g1 = jax.nn.sigmoid(gl1_ref[...] + _side_by_side(bg_ref, 1))
        mixed = g0 * y_attn + g1 * y_ssm
        r1 = DN_ALPHA * x_ref[...] + _dot(mixed.astype(BF16), wout_ref[...])
        h, xhat, rstd = _layer_norm(r1, g_ref[...], b_ref[...])
        h_ref[...] = h
        xhat_ref[...] = xhat
        rstd_ref[...] = jnp.broadcast_to(rstd, (tm, 128))

    wide = _sds((SEQ, D_MODEL), F32)
    return _call(
        body, "mixer_out", (SEQ // tm,),
        [_row_spec(tm, ATTN_WIDTH), _row_spec(tm, SSM_WIDTH), _row_spec(tm, D_MODEL, GL_COL), _row_spec(tm, D_MODEL, GL_COL + 1),
         _row_spec(tm, D_MODEL), _weight_spec((N_DEV, ATTN_WIDTH, 128)), _weight_spec((N_DEV, SSM_WIDTH, 128)),
         _weight_spec((N_DEV, SSM_WIDTH, 128)), _weight_spec((D_MODEL, D_MODEL)), _full_spec((N_DEV, 2, 128)),
         _full_spec((1, D_MODEL)), _full_spec((1, D_MODEL))],
        [_row_spec(tm, D_MODEL), _row_spec(tm, D_MODEL), _row_spec(tm, 128), _row_spec(tm, D_MODEL),
         _row_spec(tm, D_MODEL), _row_spec(tm, D_MODEL)],
        [wide, wide, _sds((SEQ, 128), F32)] + [_sds((SEQ, D_MODEL), BF16)] * 3, [],
        [attn, ys, proj, proj, x, w_ab, w_sb, w_glu, w_out, b_gate, ln_g, ln_b], ride)


def _ff_up(h, w_gate, w_up, ride=None):
    tm, tn = 1024, 768

    def body(h_ref, wg_ref, wu_ref, a_ref, b_ref, f_ref):
        hb = h_ref[...].astype(BF16)
        a, b = _dot(hb, _side_by_side(wg_ref)), _dot(hb, _side_by_side(wu_ref))
        a_ref[...] = a.astype(BF16)
        b_ref[...] = b.astype(BF16)
        f_ref[...] = (a * jax.nn.sigmoid(a) * b).astype(BF16)

    tile = pl.BlockSpec((tm, tn), lambda i, j: (i, j))
    wtile = pl.BlockSpec((tn // FF_PAD, D_MODEL, FF_PAD), lambda i, j: (j, 0, 0))
    out = _sds((SEQ, D_FF_PAD), BF16)
    return _call(body, "ff_up", (SEQ // tm, D_FF_PAD // tn), [pl.BlockSpec((tm, D_MODEL), lambda i, j: (i, 0)), wtile, wtile],
                 [tile, tile, tile], [out, out, out], [], [h, w_gate, w_up], ride)


def _ff_down_loss(f, w_down, h, target, ln_g, ln_b):
    tm = 512

    def body(f_ref, w_ref, h_ref, t_ref, g_ref, b_ref, dr_ref, dg_ref, db_ref, loss_ref):
        @pl.when(pl.program_id(0) == 0)
        def _():
            dg_ref[...] = jnp.zeros_like(dg_ref)
            db_ref[...] = jnp.zeros_like(db_ref)
            loss_ref[...] = jnp.zeros_like(loss_ref)

        r2 = DN_ALPHA * h_ref[...] + _dot(f_ref[...], w_ref[...])
        g = g_ref[...]
        out, xhat, rstd = _layer_norm(r2, g, b_ref[...])
        err = out - t_ref[...]
        loss_ref[...] += 0.5 * jnp.sum(jnp.mean(err * err, axis=-1, keepdims=True), axis=0, keepdims=True)
        dout = err * (1.0 / D_MODEL)
        dg_ref[...] += jnp.sum(dout * xhat, axis=0, keepdims=True)
        db_ref[...] += jnp.sum(dout, axis=0, keepdims=True)
        dr_ref[...] = _layer_norm_bwd(dout, xhat, rstd, g)

    vec = _sds((1, D_MODEL), F32)
    return _pallas_call(
        body, name="ff_down_loss", grid=(SEQ // tm,),
        in_specs=[_row_spec(tm, D_FF_PAD), _weight_spec((D_FF_PAD, D_MODEL)), _row_spec(tm, D_MODEL), _row_spec(tm, D_MODEL),
                  _full_spec((1, D_MODEL)), _full_spec((1, D_MODEL))],
        out_specs=(_row_spec(tm, D_MODEL), _full_spec((1, D_MODEL)), _full_spec((1, D_MODEL)), _full_spec((1, 128))),
        out_shape=(_sds((SEQ, D_MODEL), F32), vec, vec, _sds((1, 128), F32)),
        compiler_params=_cparams(dimension_semantics=("arbitrary",)),
    )(f, w_down, h, target, ln_g, ln_b)


def _ff_down_bwd(dr2, w_down, a, b):
    tm, tn = 1024, 768

    def body(dr_ref, w_ref, a_ref, b_ref, da_ref, db_ref):
        df = _dot_nt(dr_ref[...].astype(BF16), w_ref[...])
        av, bv = a_ref[...].astype(F32), b_ref[...].astype(F32)
        sg = jax.nn.sigmoid(av)
        da_ref[...] = (df * bv * sg * (1.0 + av * (1.0 - sg))).astype(BF16)
        db_ref[...] = (df * av * sg).astype(BF16)

    tile = pl.BlockSpec((tm, tn), lambda i, j: (i, j))
    out = _sds((SEQ, D_FF_PAD), BF16)
    return _pallas_call(
        body, name="ff_down_bwd", grid=(SEQ // tm, D_FF_PAD // tn),
        in_specs=[pl.BlockSpec((tm, D_MODEL), lambda i, j: (i, 0)), pl.BlockSpec((tn, D_MODEL), lambda i, j: (j, 0)), tile, tile],
        out_specs=(tile, tile), out_shape=(out, out),
        compiler_params=_cparams(dimension_semantics=("arbitrary", "arbitrary")),
    )(dr2, w_down, a, b)


def _ff_up_bwd(da, db, w_gate, w_up, dr2, xhat1, rstd1, ln_g, ride=None):
    tm, tk = 1024, 768
    nk = D_FF_PAD // tk

    def body(da_ref, db_ref, wg_ref, wu_ref, dr2_ref, xhat_ref, rstd_ref, g_ref, dr1_ref, dg_ref, dbias_ref, acc):
        i, k = pl.program_id(0), pl.program_id(1)

        @pl.when(jnp.logical_and(i == 0, k == 0))
        def _():
            dg_ref[...] = jnp.zeros_like(dg_ref)
            dbias_ref[...] = jnp.zeros_like(dbias_ref)

        part = _dot_nt(da_ref[...], _side_by_side(wg_ref)) + _dot_nt(db_ref[...], _side_by_side(wu_ref))

        @pl.when(k == 0)
        def _():
            acc[...] = part

        @pl.when(k > 0)
        def _():
            acc[...] += part

        @pl.when(k == nk - 1)
        def _():
            dh = DN_ALPHA * dr2_ref[...] + acc[...]
            xhat = xhat_ref[...]
            dg_ref[...] += jnp.sum(dh * xhat, axis=0, keepdims=True)
            dbias_ref[...] += jnp.sum(dh, axis=0, keepdims=True)
            rstd = jnp.max(rstd_ref[...], axis=1, keepdims=True)
            dr1_ref[...] = _layer_norm_bwd(dh, xhat, rstd, g_ref[...])

    hid = pl.BlockSpec((tm, tk), lambda i, k: (i, k))
    wtile = pl.BlockSpec((tk // FF_PAD, D_MODEL, FF_PAD), lambda i, k: (k, 0, 0))
    row = pl.BlockSpec((tm, D_MODEL), lambda i, k: (i, 0))
    vec = pl.BlockSpec((1, D_MODEL), lambda i, k: (0, 0))
    return _call(
        body, "ff_up_bwd", (SEQ // tm, nk),
        [hid, hid, wtile, wtile, row, row, pl.BlockSpec((tm, 128), lambda i, k: (i, 0)), vec],
        [row, vec, vec], [_sds((SEQ, D_MODEL), F32), _sds((1, D_MODEL), F32), _sds((1, D_MODEL), F32)],
        [pltpu.VMEM((tm, D_MODEL), F32)], [da, db, w_gate, w_up, dr2, xhat1, rstd1, ln_g], ride)


def _mixer_bwd(dr1, proj, y_attn, y_ssm, glu, ys, w_ab, w_sb, w_glu, w_out, b_gate):
    tm = 256

    def body(dr1_ref, gl0_ref, gl1_ref, ya_ref, yssm_ref, glu_ref, ys_ref, wab_ref, wsb_ref, wglu_ref, wout_ref, bg_ref,
             dya_ref, dyssm_ref, dgl_ref, dattn_ref, dglu_ref, dys_ref, mixed_ref, ysb_ref, gy_ref, dbg_ref, stage, copied):
        @pl.when(pl.program_id(0) == 0)
        def _():
            dbg_ref[...] = jnp.zeros_like(dbg_ref)

        dmixed = _dot_nt(dr1_ref[...].astype(BF16), wout_ref[...])
        g0 = jax.nn.sigmoid(gl0_ref[...] + _side_by_side(bg_ref, 0))
        g1 = jax.nn.sigmoid(gl1_ref[...] + _side_by_side(bg_ref, 1))
        y_attn, y_ssm = ya_ref[...].astype(F32), yssm_ref[...].astype(F32)
        mixed_ref[...] = (g0 * y_attn + g1 * y_ssm).astype(BF16)
        dya = (dmixed * g0).astype(BF16)
        dyssm = (dmixed * g1).astype(BF16)
        dya_ref[...] = dya
        dyssm_ref[...] = dyssm
        dgl0 = dmixed * y_attn * g0 * (1.0 - g0)
        dgl1 = dmixed * y_ssm * g1 * (1.0 - g1)
        i, last = pl.program_id(0), SEQ // tm - 1
        slot = i & 1

        def copy_out(buffer, tile):
            window = dgl_ref.at[pl.ds(pl.multiple_of(tile * tm, tm), tm), pl.ds(GL_COL * D_MODEL, 2 * D_MODEL)]
            return pltpu.make_async_copy(stage.at[buffer], window, copied.at[buffer])

        @pl.when(i >= 2)
        def _():
            copy_out(slot, i - 2).wait()

        stage[slot, :, :D_MODEL] = dgl0.astype(BF16)
        stage[slot, :, D_MODEL:] = dgl1.astype(BF16)
        copy_out(slot, i).start()

        @pl.when(i == last)
        def _():
            copy_out(1 - slot, i - 1).wait()
            copy_out(slot, i).wait()
        dbg_ref[:, :D_MODEL] += jnp.sum(dgl0, axis=0, keepdims=True)
        dbg_ref[:, D_MODEL:] += jnp.sum(dgl1, axis=0, keepdims=True)
        dattn_ref[...] = _dot_nt(dya, _side_by_side(wab_ref))
        dy_s = _dot_nt(dyssm, _side_by_side(wsb_ref))
        glu = glu_ref[...].astype(F32)
        glu1, sg = glu[:, :SSM_WIDTH], jax.nn.sigmoid(glu[:, SSM_WIDTH:])
        ysb_ref[...] = (glu1 * sg).astype(BF16)
        dglu1 = (dy_s * sg).astype(BF16)
        dglu2 = (dy_s * glu1 * sg * (1.0 - sg)).astype(BF16)
        dglu_ref[:, :SSM_WIDTH] = dglu1
        dglu_ref[:, SSM_WIDTH:] = dglu2
        dgy = _dot_nt(jnp.concatenate([dglu1, dglu2], axis=1), _side_by_side(wglu_ref))
        ys = ys_ref[...]
        gy, t = _gelu(ys)
        gy_ref[...] = gy.astype(BF16)
        dys_ref[...] = dgy * _gelu_grad(ys, t)

    wide_b, half_b = _sds((SEQ, D_MODEL), BF16), _sds((SEQ, SSM_WIDTH), BF16)
    half_f = _sds((SEQ, SSM_WIDTH), F32)
    return _pallas_call(
        body, name="mixer_bwd", grid=(SEQ // tm,),
        in_specs=[_row_spec(tm, D_MODEL), _row_spec(tm, D_MODEL, GL_COL), _row_spec(tm, D_MODEL, GL_COL + 1), _row_spec(tm, D_MODEL),
                  _row_spec(tm, D_MODEL), _row_spec(tm, D_MODEL), _row_spec(tm, SSM_WIDTH), _full_spec((N_DEV, ATTN_WIDTH, 128)),
                  _full_spec((N_DEV, SSM_WIDTH, 128)), _full_spec((N_DEV, SSM_WIDTH, 128)), _full_spec((D_MODEL, D_MODEL)),
                  _full_spec((N_DEV, 2, 128))],
        out_specs=(_row_spec(tm, D_MODEL), _row_spec(tm, D_MODEL), ANY, _row_spec(tm, ATTN_WIDTH),
                   _row_spec(tm, D_MODEL), _row_spec(tm, SSM_WIDTH), _row_spec(tm, D_MODEL), _row_spec(tm, SSM_WIDTH),
                   _row_spec(tm, SSM_WIDTH), _full_spec((1, 2 * D_MODEL))),
        out_shape=(wide_b, wide_b, _sds((SEQ, IN_WIDTH), BF16), half_f, wide_b, half_f, wide_b, half_b, half_b,
                   _sds((1, 2 * D_MODEL), F32)),
        scratch_shapes=[pltpu.VMEM((2, tm, 2 * D_MODEL), BF16), pltpu.SemaphoreType.DMA((2,))],
        compiler_params=_cparams(dimension_semantics=("arbitrary",)),
    )(dr1, proj, proj, y_attn, y_ssm, glu, ys, w_ab, w_sb, w_glu, w_out, b_gate)


def _grad_x(dproj, w_in, dr1, ride=None):
    tm, tk = 1024, 1792
    nk = IN_WIDTH // tk

    def body(dp_ref, w_ref, dr1_ref, o_ref, acc):
        k = pl.program_id(1)
        part = _dot_nt(dp_ref[...], _side_by_side(w_ref))

        @pl.when(k == 0)
        def _():
            acc[...] = part

        @pl.when(k > 0)
        def _():
            acc[...] += part

        @pl.when(k == nk - 1)
        def _():
            o_ref[...] = DN_ALPHA * dr1_ref[...] + acc[...]

    row = pl.BlockSpec((tm, D_MODEL), lambda i, k: (i, 0))
    return _call(
        body, "grad_x", (SEQ // tm, nk),
        [pl.BlockSpec((tm, tk), lambda i, k: (i, k)), pl.BlockSpec((2, D_MODEL, tk // 2), lambda i, k: (k, 0, 0)), row],
        [row], [_sds((SEQ, D_MODEL), F32)], [pltpu.VMEM((tm, D_MODEL), F32)], [dproj, w_in, dr1], ride)


def _weight_grad(a, b, name, shard_cols=None):
    k, n = a.shape[1], b.shape[1]
    tk = k if shard_cols else k // N_DEV
    tn = n // 4 if shard_cols else min(n, 1024)

    def body(a_ref, b_ref, o_ref):
        grad = _dot_tn(a_ref[...].astype(BF16), b_ref[...].astype(BF16))
        if shard_cols:
            o_ref[0] = grad[:, :shard_cols].astype(BF16)
            o_ref[1] = grad[:, shard_cols:].astype(BF16)
        else:
            o_ref[...] = grad.astype(BF16)

    if shard_cols:
        out_spec = pl.BlockSpec((2, None, tk, shard_cols), lambda kk, j: (0, j, kk, 0))
        out_shape = _sds((2, 4, k, shard_cols), BF16)
    else:
        out_spec = pl.BlockSpec((None, None, tk, tn), lambda kk, j: (kk % 2, kk // 2, 0, j))
        out_shape = _sds((2, 4, tk, n), BF16)
    return _call(body, name, (k // tk, n // tn),
                 [pl.BlockSpec((SEQ, tk), lambda kk, j: (0, kk)), pl.BlockSpec((SEQ, tn), lambda kk, j: (0, j))],
                 [out_spec], [out_shape], [], [a, b])[0]


def _weight_grad_rows(a, b, core, name, shard_cols, first, count, ride, in_place=False):
    k = a.shape[1]

    def body(a_ref, b_ref, o_ref):
        o_ref[...] = _dot_tn(a_ref[...].astype(BF16), b_ref[...].astype(BF16)).astype(BF16)

    def shard(j, core_ref):
        row = j + first
        return 0, jnp.where(row < 4, 2 * row + 1 - core_ref[0], 2 * (row - 4) + core_ref[0])

    return _call(body, name, (count,),
                 [pl.BlockSpec((SEQ, k), lambda j, core_ref: (0, 0), pipeline_mode=pl.Buffered(1)),
                  pl.BlockSpec((SEQ, shard_cols), shard)],
                 [pl.BlockSpec((None, k, shard_cols), lambda j, core_ref: (j + first, 0, 0))],
                 [_sds((N_DEV, k, shard_cols), BF16)], [], [a, b], ride, aliases={2: 0} if in_place else None, prefetch=core)


MESH = pl.DeviceIdType.MESH
ANY = pl.BlockSpec(memory_space=pl.ANY)


def _place():
    return lax.axis_index("x"), lax.axis_index("y"), lax.axis_index("c")


def _other_chips(x, y):
    return [(1 - x, y), (x, 1 - y), (1 - x, 1 - y)]


class _Ride:
    def __init__(self, operands, results, aliases, sems, start, wait):
        self.operands, self.results, self.aliases, self.sems = list(operands), list(results), dict(aliases), list(sems)
        self.start, self.wait = start, wait

    def __add__(self, other):
        n_in, n_out, n_sem = len(self.operands), len(self.results), len(self.sems)

        def both(which):
            def run(ins, outs, sems):
                getattr(self, which)(ins[:n_in], outs[:n_out], sems[:n_sem])
                getattr(other, which)(ins[n_in:], outs[n_out:], sems[n_sem:])
            return run

        aliases = {**self.aliases, **{n_in + i: n_out + j for i, j in other.aliases.items()}}
        return _Ride(self.operands + other.operands, self.results + other.results, aliases, self.sems + other.sems,
                     both("start"), both("wait"))


def _call(body, name, grid, in_specs, out_specs, out_shape, scratch_shapes, operands, ride=None, aliases=None, prefetch=None):
    in_specs, out_specs, out_shape = list(in_specs), list(out_specs), list(out_shape)
    scratch_shapes, operands, aliases = list(scratch_shapes), list(operands), dict(aliases or {})
    kernel_body = body
    if ride is not None:
        n_in, n_out, n_scr, r_in, r_out = len(in_specs), len(out_specs), len(scratch_shapes), len(ride.operands), len(ride.results)

        def kernel_body(*refs):
            out0, scr0 = n_in + r_in, n_in + r_in + n_out + r_out
            ride_refs = (refs[n_in:out0], refs[out0 + n_out:scr0], refs[scr0 + n_scr:])
            ids = [pl.program_id(i) for i in range(len(grid))]
            first = functools.reduce(jnp.logical_and, [i == 0 for i in ids])
            last = functools.reduce(jnp.logical_and, [i == g - 1 for i, g in zip(ids, grid)])

            @pl.when(first)
            def _():
                ride.start(*ride_refs)

            body(*refs[:n_in], *refs[out0:out0 + n_out], *refs[scr0:scr0 + n_scr])

            @pl.when(last)
            def _():
                ride.wait(*ride_refs)

        aliases.update({n_in + i: n_out + j for i, j in ride.aliases.items()})
        in_specs += [ANY] * r_in
        out_specs += [ANY] * r_out
        out_shape += ride.results
        scratch_shapes += ride.sems
        operands += ride.operands
    params = _cparams(dimension_semantics=("arbitrary",) * len(grid))
    if prefetch is None:
        return _pallas_call(
            kernel_body, name=name, grid=grid, in_specs=in_specs, out_specs=out_specs, out_shape=out_shape,
            scratch_shapes=scratch_shapes, input_output_aliases=aliases, compiler_params=params,
        )(*operands)

    def with_prefetch(prefetch_ref, *refs):
        kernel_body(*refs)

    return _pallas_call(
        with_prefetch, name=name,
        grid_spec=pltpu.PrefetchScalarGridSpec(num_scalar_prefetch=1, grid=grid, in_specs=in_specs, out_specs=out_specs,
                                               scratch_shapes=scratch_shapes),
        out_shape=out_shape, input_output_aliases={i + 1: j for i, j in aliases.items()}, compiler_params=params,
    )(prefetch, *operands)


def _after(*arrays):
    return _Ride(arrays, [], {}, [], lambda *refs: None, lambda *refs: None)


def _gather_first_level(shards):
    n = len(shards)

    def copies(ins, outs, sems, landed):
        send_sems, recv_sems, local_sems, *staged = sems
        x, y, c = _place()
        peers = [(x, y, 1 - c)] + [(px, py, c) for px, py in _other_chips(x, y)]

        def row(peer):
            return 4 * x + 2 * y + c if not landed else 4 * peer[0] + 2 * peer[1] + peer[2]

        local = [pltpu.make_async_copy(staged[a], outs[a].at[4 * x + 2 * y + c], local_sems.at[a]) for a in range(n)]
        remote = [pltpu.make_async_remote_copy(
            src_ref=staged[a], dst_ref=outs[a].at[row(peer)], send_sem=send_sems.at[a, k], recv_sem=recv_sems.at[a, k],
            device_id=peer, device_id_type=MESH) for a in range(n) for k, peer in enumerate(peers)]
        return local, remote

    def start(ins, outs, sems):
        for a in range(n):
            pltpu.sync_copy(ins[a], sems[3 + a])
        local, remote = copies(ins, outs, sems, False)
        for cp in local + remote:
            cp.start()

    def wait(ins, outs, sems):
        local, sent = copies(ins, outs, sems, False)
        for cp in copies(ins, outs, sems, True)[1]:
            cp.wait_recv()
        for cp in sent:
            cp.wait_send()
        for cp in local:
            cp.wait()

    return _Ride(shards, [_sds((N_DEV,) + s.shape, s.dtype) for s in shards], {},
                 [pltpu.SemaphoreType.DMA((n, 4)), pltpu.SemaphoreType.DMA((n, 4)), pltpu.SemaphoreType.DMA((n,))]
                 + [pltpu.VMEM(s.shape, s.dtype) for s in shards], start, wait)


def _gather_second_level(buffers):
    n = len(buffers)

    def copies(outs, sems, core):
        send_sems, recv_sems = sems
        x, y, c = _place()
        return [pltpu.make_async_remote_copy(
            src_ref=outs[a].at[4 * px + 2 * py + core], dst_ref=outs[a].at[4 * px + 2 * py + core], send_sem=send_sems.at[a, j],
            recv_sem=recv_sems.at[a, j], device_id=(x, y, 1 - c), device_id_type=MESH)
            for a in range(n) for j, (px, py) in enumerate(_other_chips(x, y))]

    def start(ins, outs, sems):
        for cp in copies(outs, sems, lax.axis_index("c")):
            cp.start()

    def wait(ins, outs, sems):
        for cp in copies(outs, sems, 1 - lax.axis_index("c")):
            cp.wait_recv()
        for cp in copies(outs, sems, lax.axis_index("c")):
            cp.wait_send()

    return _Ride(buffers, [_sds(b.shape, b.dtype) for b in buffers], {i: i for i in range(n)},
                 [pltpu.SemaphoreType.DMA((n, 3)), pltpu.SemaphoreType.DMA((n, 3))], start, wait)


def _relayed_gather(shards):
    n = len(shards)
    buffers = [_sds((N_DEV,) + s.shape, s.dtype) for s in shards]
    dma = pltpu.SemaphoreType.DMA

    def remote(src, dst, send_sem, recv_sem, to):
        return pltpu.make_async_remote_copy(src_ref=src, dst_ref=dst, send_sem=send_sem, recv_sem=recv_sem,
                                            device_id=to, device_id_type=MESH)

    def row(px, py, pc):
        return 4 * px + 2 * py + pc

    def ride(operands, aliases, sems, copies):
        def start(ins, outs, sem_refs):
            local, sent = copies(ins, outs, sem_refs, False)
            for cp in local + sent:
                cp.start()

        def wait(ins, outs, sem_refs):
            local, sent = copies(ins, outs, sem_refs, False)
            for cp in copies(ins, outs, sem_refs, True)[1]:
                cp.wait_recv()
            for cp in sent:
                cp.wait_send()
            for cp in local:
                cp.wait()

        return _Ride(operands, buffers, aliases, sems, start, wait)

    def first(ins, outs, sems, landed):
        x, y, c = _place()
        peers = [(x, y, 1 - c), (1 - x, y, c), (x, 1 - y, c)]
        local = [pltpu.make_async_copy(ins[a], outs[a].at[row(x, y, c)], sems[2].at[a]) for a in range(n)]
        return local, [remote(ins[a], outs[a].at[row(*peer) if landed else row(x, y, c)], sems[0].at[a, k], sems[1].at[a, k], peer)
                       for a in range(n) for k, peer in enumerate(peers)]

    def second(ins, outs, sems, landed):
        x, y, c = _place()
        mine = 1 - c if landed else c
        copies = []
        for a in range(n):
            half = shards[a].shape[0] // 2
            over_x, over_y, diagonal = outs[a].at[row(1 - x, y, mine)], outs[a].at[row(x, 1 - y, mine)], outs[a].at[row(1 - x, 1 - y, c)]
            lower, upper = pl.ds(0, half), pl.ds(half, half)
            copies += [remote(over_x, over_x, sems[0].at[a, 0], sems[1].at[a, 0], (x, y, 1 - c)),
                       remote(over_y, over_y, sems[0].at[a, 1], sems[1].at[a, 1], (x, y, 1 - c))]
            if landed:
                copies += [remote(diagonal.at[lower], diagonal.at[lower], sems[0].at[a, 2], sems[1].at[a, 2], (1 - x, y, c)),
                           remote(diagonal.at[upper], diagonal.at[upper], sems[0].at[a, 3], sems[1].at[a, 3], (x, 1 - y, c))]
            else:
                copies += [remote(over_y.at[lower], over_y.at[lower], sems[0].at[a, 2], sems[1].at[a, 2], (1 - x, y, c)),
                           remote(over_x.at[upper], over_x.at[upper], sems[0].at[a, 3], sems[1].at[a, 3], (x, 1 - y, c))]
        return [], copies

    def third(ins, outs, sems, landed):
        x, y, c = _place()
        return [], [remote(outs[a].at[row(1 - x, 1 - y, 1 - c if landed else c)], outs[a].at[row(1 - x, 1 - y, 1 - c if landed else c)],
                           sems[0].at[a], sems[1].at[a], (x, y, 1 - c)) for a in range(n)]

    def later(copies, n_sems):
        return lambda partly: ride(partly, {i: i for i in range(n)}, [dma((n,) + n_sems), dma((n,) + n_sems)], copies)

    return ride(shards, {}, [dma((n, 3)), dma((n, 3)), dma((n,))], first), later(second, (4,)), later(third, ())


def _sibling_swap_ride(grads, halves=True):
    n = len(grads)

    def copies(ins, outs, sems):
        x, y, c = _place()
        return [pltpu.make_async_remote_copy(
            src_ref=ins[a].at[1 - c] if halves else sems[2 + a], dst_ref=outs[a], send_sem=sems[0].at[a],
            recv_sem=sems[1].at[a], device_id=(x, y, 1 - c), device_id_type=MESH) for a in range(n)]

    def start(ins, outs, sems):
        if not halves:
            for a in range(n):
                pltpu.sync_copy(ins[a].at[pl.ds(0, 4)], sems[2 + a])
        for cp in copies(ins, outs, sems):
            cp.start()

    def wait(ins, outs, sems):
        for cp in copies(ins, outs, sems):
            cp.wait()

    return _Ride(grads, [_sds((4,) + g.shape[-2:], g.dtype) for g in grads], {},
                 [pltpu.SemaphoreType.DMA((n,)), pltpu.SemaphoreType.DMA((n,))]
                 + ([] if halves else [pltpu.VMEM((4,) + g.shape[-2:], g.dtype) for g in grads]), start, wait)


def _chip_swap_ride(sums):
    n = len(sums)

    def copies(ins, outs, sems, landed):
        send_sems, recv_sems, local_sems = sems
        x, y, c = _place()
        mine = 2 * x + y
        local = [pltpu.make_async_copy(ins[a].at[mine], outs[a].at[mine], local_sems.at[a]) for a in range(n)]
        remote = [pltpu.make_async_remote_copy(
            src_ref=ins[a].at[2 * px + py], dst_ref=outs[a].at[2 * px + py if landed else mine], send_sem=send_sems.at[a, j],
            recv_sem=recv_sems.at[a, j], device_id=(px, py, c), device_id_type=MESH)
            for a in range(n) for j, (px, py) in enumerate(_other_chips(x, y))]
        return local, remote

    def start(ins, outs, sems):
        local, remote = copies(ins, outs, sems, False)
        for cp in local + remote:
            cp.start()

    def wait(ins, outs, sems):
        local, sent = copies(ins, outs, sems, False)
        for cp in copies(ins, outs, sems, True)[1]:
            cp.wait_recv()
        for cp in sent:
            cp.wait_send()
        for cp in local:
            cp.wait()

    return _Ride(sums, [_sds(s.shape, s.dtype) for s in sums], {},
                 [pltpu.SemaphoreType.DMA((n, 3)), pltpu.SemaphoreType.DMA((n, 3)), pltpu.SemaphoreType.DMA((n,))], start, wait)


def _send_buffers(shards, name):
    n = len(shards)

    def body(*refs):
        for (w, transposed, rows, cols), w_ref, o_ref in zip(shards, refs[:n], refs[n:]):
            if transposed:
                c, r = w.shape
                padded = jnp.concatenate([w_ref[...], jnp.zeros((cols - c, r), F32)], axis=0) if cols > c else w_ref[...]
                o_ref[...] = padded.T.astype(BF16)
            else:
                r, c = w.shape
                if (r, c) != (rows, cols):
                    o_ref[...] = jnp.zeros((rows, cols), BF16)
                o_ref[:r, :c] = w_ref[...].astype(BF16)

    return _pallas_call(body, name=name, out_shape=[_sds((rows, cols), BF16) for _, _, rows, cols in shards])(
        *[w for w, _, _, _ in shards])


def _all_gather(shards, name):
    n = len(shards)
    first, second, third = _relayed_gather(shards)
    levels = [first, second(shards), third(shards)]
    counts = [len(level.sems) for level in levels]

    def body(*refs):
        ins, outs, sems, staged = refs[:n], refs[n:2 * n], refs[2 * n:-n], refs[-n:]
        for a in range(n):
            pltpu.sync_copy(ins[a], staged[a])
        for i, level in enumerate(levels):
            mine = sems[sum(counts[:i]):sum(counts[:i + 1])]
            level.start(staged, outs, mine)
            level.wait(staged, outs, mine)

    return _pallas_call(
        body, name=name, in_specs=[ANY] * n, out_specs=[ANY] * n, out_shape=first.results,
        scratch_shapes=[s for level in levels for s in level.sems] + [pltpu.VMEM(s.shape, s.dtype) for s in shards],
    )(*shards)


HBM = pl.BlockSpec(memory_space=pltpu.HBM)
SEMAPHORES = pl.BlockSpec(memory_space=pltpu.SEMAPHORE)
IN_FLIGHT = pltpu.CompilerParams(has_side_effects=pltpu.SideEffectType.DATAFLOW_SIDE_EFFECTING)


def _chip_swap_copies(src_refs, land_refs, send_sems, recv_sems, landed):
    x, y, c = _place()
    return [pltpu.make_async_remote_copy(
        src_ref=src.at[2 * px + py], dst_ref=land.at[2 * px + py if landed else 2 * x + y], send_sem=send_sems.at[3 * a + j],
        recv_sem=recv_sems.at[3 * a + j], device_id=(px, py, c), device_id_type=MESH)
        for a, (src, land) in enumerate(zip(src_refs, land_refs)) for j, (px, py) in enumerate(_other_chips(x, y))]


def _chip_swap_start(sums, name):
    n = len(sums)

    def body(*refs):
        src_refs, land_refs, (send_sems, recv_sems), token = refs[:n], refs[n:2 * n], refs[2 * n:2 * n + 2], refs[-1]
        for cp in _chip_swap_copies(src_refs, land_refs, send_sems, recv_sems, False):
            cp.start()
        token[...] = jnp.zeros_like(token)

    kept = [pltpu.HBM(s.shape, s.dtype) for s in sums]
    out = _pallas_call(
        body, name=name,
        out_shape=[pltpu.SemaphoreType.DMA((3 * n,)), pltpu.SemaphoreType.DMA((3 * n,))] + kept + kept + [_sds((8, 128), F32)],
        in_specs=[HBM] * (2 * n), out_specs=[SEMAPHORES, SEMAPHORES] + [HBM] * (2 * n) + [pl.BlockSpec(memory_space=pltpu.VMEM)],
        input_output_aliases={i: 2 + i for i in range(2 * n)}, compiler_params=IN_FLIGHT,
    )(*[pltpu.with_memory_space_constraint(s, pltpu.HBM) for s in sums],
      *[pltpu.with_memory_space_constraint(lax.empty(s.shape, s.dtype), pltpu.HBM) for s in sums])
    return out[0], out[1], out[2:2 + n], out[2 + n:2 + 2 * n], out[-1]


def _chip_swap_wait(send_sems, recv_sems, sums, landings, after, name):
    n = len(sums)

    def body(*refs):
        src_refs, land_refs, (send_sems, recv_sems) = refs[:n], refs[n:2 * n], refs[2 * n:2 * n + 2]
        for cp in _chip_swap_copies(src_refs, land_refs, send_sems, recv_sems, False):
            cp.wait_send()
        for cp in _chip_swap_copies(src_refs, land_refs, send_sems, recv_sems, True):
            cp.wait_recv()

    out = _pallas_call(
        body, name=name, out_shape=[pltpu.HBM(s.shape, s.dtype) for s in list(sums) + list(landings)],
        in_specs=[HBM] * (2 * n) + [SEMAPHORES, SEMAPHORES] + [ANY] * len(after), out_specs=[HBM] * (2 * n),
        input_output_aliases={i: i for i in range(2 * n)}, compiler_params=IN_FLIGHT,
    )(*sums, *landings, send_sems, recv_sems, *after)
    return out[:n], out[n:]


def _pair_sums(gs, rs, core, name):
    n_arrays = len(gs)

    def body(core_ref, *refs):
        for g_ref, r_ref, o_ref in zip(refs[:n_arrays], refs[n_arrays:2 * n_arrays], refs[2 * n_arrays:]):
            o_ref[...] = (g_ref[...].astype(F32) + r_ref[...].astype(F32)).astype(o_ref.dtype)

    def chip(g):
        return pl.BlockSpec((None,) + g.shape[-2:], lambda p, core_ref: (p, 0, 0))

    def own(g):
        if g.ndim == 3:
            return pl.BlockSpec((None,) + g.shape[-2:], lambda p, core_ref: (p + 4, 0, 0))
        return pl.BlockSpec((None, None) + g.shape[2:], lambda p, core_ref: (core_ref[0], p, 0, 0))

    return _pallas_call(
        body, name=name,
        grid_spec=pltpu.PrefetchScalarGridSpec(
            num_scalar_prefetch=1, grid=(4,), in_specs=[own(g) for g in gs] + [chip(g) for g in gs],
            out_specs=[chip(g) for g in gs]),
        out_shape=[_sds((4,) + g.shape[-2:], g.dtype) for g in gs], compiler_params=_cparams(dimension_semantics=("arbitrary",)),
    )(core, *gs, *rs)


def _adamw_math(w, g, m, v):
    m = ADAM_B1 * m + (1.0 - ADAM_B1) * g
    v = ADAM_B2 * v + (1.0 - ADAM_B2) * (g * g)
    m_hat = m / (1.0 - ADAM_B1 ** ADAM_STEP)
    v_hat = v / (1.0 - ADAM_B2 ** ADAM_STEP)
    return -ADAM_LR * (m_hat / (jnp.sqrt(v_hat) + ADAM_EPS) + ADAM_WD * w), m, v


def _adamw_many(weights, name, ride=None):
    steps = 4
    in_specs, out_specs, out_shape, operands, tiles = [], [], [], [], []
    for w, m, v, parts, own, transposed in weights:
        _, pr, pc = parts.shape
        if transposed:
            c, r = w.shape
            tile = pl.BlockSpec((c, r // steps), lambda i: (0, i))
            part_tile = pl.BlockSpec((4, r // steps, pc), lambda i: (0, i, 0))
            tiles.append((c, r // steps))
        elif w.shape[0] % (8 * steps) == 0:
            r, c = w.shape
            tile = pl.BlockSpec((r // steps, c), lambda i: (i, 0))
            part_tile = pl.BlockSpec((4, r // steps, pc), lambda i: (0, i, 0))
            tiles.append((r // steps, c))
        else:
            tile = pl.BlockSpec(w.shape, lambda i: (0, 0))
            part_tile = pl.BlockSpec(parts.shape, lambda i: (0, 0, 0))
            tiles.append(w.shape)
        in_specs += [tile, tile, tile] + [part_tile] * (1 if own is None else 2)
        out_specs += [tile] * 4
        out_shape += [_sds(w.shape, F32)] * 4
        operands += [w, m, v, parts] + ([] if own is None else [own])
    n_in = len(operands)

    def body(*refs):
        ins, outs = list(refs[:n_in]), refs[n_in:]
        this_chip = 2 * lax.axis_index("x") + lax.axis_index("y")
        for k, (_, _, _, _, own, transposed) in enumerate(weights):
            w_ref, m_ref, v_ref, p_ref = ins[:4]
            own_ref = None if own is None else ins[4]
            del ins[:4 if own is None else 5]
            rows, cols = tiles[k]
            g = None
            for q in range(4):
                index = (q,) if transposed else (q, slice(0, rows), slice(0, cols))
                part = p_ref[index] if own is None else jnp.where(this_chip == q, own_ref[index], p_ref[index])
                g = part.astype(F32) if g is None else g + part.astype(F32)
            if transposed:
                g = g.T[:rows]
            g_out, d_out, m_out, v_out = outs[4 * k:4 * k + 4]
            g_out[...] = g
            d_out[...], m_out[...], v_out[...] = _adamw_math(w_ref[...], g, m_ref[...], v_ref[...])

    return _call(body, name, (steps,), in_specs, out_specs, out_shape, [], operands, ride)


SMALL = ("ssm_a_re", "ssm_a_im", "ssm_log_dt", "ssm_b_re", "ssm_b_im", "ssm_c_re", "ssm_c_im", "ssm_d",
         "ln1_g", "ln1_b", "ln2_g", "ln2_b")


def _pack_rows(arrays):
    rows = []
    for a in arrays:
        flat = a.reshape(-1)
        rows.append(jnp.pad(flat, (0, -flat.shape[0] % 128)).reshape(-1, 128))
    packed = jnp.concatenate(rows, axis=0)
    return jnp.pad(packed, ((0, -packed.shape[0] % 8), (0, 0)))


def _unpack_rows(packed, shapes):
    out, row = [], 0
    for shape in shapes:
        size = math.prod(shape)
        n_rows = -(-size // 128)
        out.append(packed[row:row + n_rows].reshape(-1)[:size].reshape(shape))
        row += n_rows
    return out


def _sum_devices(parts):
    def body(p_ref, o_ref):
        total = p_ref[0]
        for dev in range(1, N_DEV):
            total = total + p_ref[dev]
        o_ref[...] = total

    return _pallas_call(body, name="sum_devices", out_shape=_sds(parts.shape[1:], F32))(parts)


def _adamw_replicated(ws, ms, vs, gs):
    n = len(ws)

    def body(*refs):
        w_refs, m_refs, v_refs, g_refs, d_out, m_out, v_out = (refs[i * n:(i + 1) * n] for i in range(7))
        for i in range(n):
            d_out[i][...], m_out[i][...], v_out[i][...] = _adamw_math(w_refs[i][...], g_refs[i][...], m_refs[i][...], v_refs[i][...])

    out = _pallas_call(body, name="adamw_replicated", out_shape=[_sds(w.shape, F32) for w in ws] * 3,
                       compiler_params=_cparams())(*ws, *ms, *vs, *gs)
    return out[:n], out[n:2 * n], out[2 * n:]


def kernel(x, w_in, b_gate, w_attn_br, w_ssm_br, w_out, ssm_a_re, ssm_a_im, ssm_log_dt, ssm_b_re, ssm_b_im, ssm_c_re, ssm_c_im, ssm_d, w_glu, ln1_g, ln1_b, w_ff_gate, w_ff_up, w_ff_down, ln2_g, ln2_b, loss_target, m_w_in, m_b_gate, m_w_attn_br, m_w_ssm_br, m_w_out, m_ssm_a_re, m_ssm_a_im, m_ssm_log_dt, m_ssm_b_re, m_ssm_b_im, m_ssm_c_re, m_ssm_c_im, m_ssm_d, m_w_glu, m_ln1_g, m_ln1_b, m_w_ff_gate, m_w_ff_up, m_w_ff_down, m_ln2_g, m_ln2_b, v_w_in, v_b_gate, v_w_attn_br, v_w_ssm_br, v_w_out, v_ssm_a_re, v_ssm_a_im, v_ssm_log_dt, v_ssm_b_re, v_ssm_b_im, v_ssm_c_re, v_ssm_c_im, v_ssm_d, v_w_glu, v_ln1_g, v_ln1_b, v_w_ff_gate, v_w_ff_up, v_w_ff_down, v_ln2_g, v_ln2_b):
    given = dict(locals())
    x2, target = x[0], loss_target[0]
    core = lax.axis_index("c").astype(jnp.int32).reshape(1)

    sharded = ("w_in", "w_attn_br", "w_ssm_br", "w_glu", "w_ff_gate", "w_ff_up", "b_gate", "w_out", "w_ff_down")
    send_shape = dict(w_in=(D_MODEL, 896), w_attn_br=(ATTN_WIDTH, 128), w_ssm_br=(SSM_WIDTH, 128), w_glu=(SSM_WIDTH, 128),
                      w_out=(128, D_MODEL), w_ff_gate=(D_MODEL, FF_PAD), w_ff_up=(D_MODEL, FF_PAD), w_ff_down=(FF_PAD, D_MODEL))
    local = {k: given[k][0] for k in sharded}
    narrow = ("w_ff_gate", "w_ff_up")
    def to_send(k):
        return (local[k].T, True, *send_shape[k]) if k in narrow else (local[k], False, *send_shape[k])

    later = [k for k in sharded if k not in ("w_in", "b_gate")]
    sends = dict(zip(["w_in"] + later, _send_buffers([to_send("w_in")], "send_w_in")
                     + _send_buffers([to_send(k) for k in later], "send_weights")))
    sends["b_gate"] = local["b_gate"]
    mixer_weights = ("w_attn_br", "w_ssm_br", "w_glu", "b_gate", "w_out")
    ff_weights = ("w_ff_gate", "w_ff_up", "w_ff_down")
    wt = {}
    wt["w_in"], = _all_gather([sends["w_in"]], "gather_w_in")

    a_re, a_im, log_dt = ssm_a_re[0], ssm_a_im[0], ssm_log_dt[0].reshape(SSM_GROUPS, 1)
    b_re_t, b_im_t = ssm_b_re[0].transpose(0, 2, 1), ssm_b_im[0].transpose(0, 2, 1)
    abar_re, abar_im, e_re, e_im, bbar_re_t, bbar_im_t = _ssm_prep(a_re, a_im, log_dt, b_re_t, b_im_t)
    bmat, cmat, a_chunks = _ssm_tables(abar_re, abar_im, bbar_re_t, bbar_im_t, ssm_c_re[0], ssm_c_im[0])
    cos_t, sin_t = _rope_tables()

    big_mixer, ff_in = [k for k in mixer_weights if k != "b_gate"], ("w_ff_gate", "w_ff_up")
    n_mixer = len(big_mixer)
    mixer_1, mixer_2, mixer_3 = _relayed_gather([sends[k] for k in big_mixer])
    ff_in_1, ff_in_2, ff_in_3 = _relayed_gather([sends[k] for k in ff_in])
    ff_down_1, ff_down_2, ff_down_3 = _relayed_gather([sends["w_ff_down"]])
    proj, *landed = _proj(x2, wt["w_in"], mixer_1 + _gather_first_level([sends["b_gate"]]))
    mixer, bias = landed[:n_mixer], landed[n_mixer:]
    attn, lse, q_pm, k_pm, v_pm, *landed = _attn_fwd(proj, cos_t, sin_t,
                                                     mixer_2(mixer) + _gather_second_level(bias) + ff_in_1)
    mixer, b_gate_full, ff = landed[:n_mixer], landed[n_mixer], landed[n_mixer + 1:]
    ys, states, *landed = _ssm_fwd(proj, bmat, cmat, a_chunks, ssm_d, mixer_3(mixer) + ff_in_2(ff) + ff_down_1)
    wt.update(zip(big_mixer, landed[:n_mixer]))
    ff, ff_down = landed[n_mixer:n_mixer + 2], landed[n_mixer + 2:]
    wt["w_out"] = wt["w_out"].reshape(D_MODEL, D_MODEL)
    h, xhat1, rstd1, glu, y_attn, y_ssm, *landed = _mixer_out(
        attn, ys, proj, x2, wt["w_attn_br"], wt["w_ssm_br"], wt["w_glu"], wt["w_out"], b_gate_full, ln1_g, ln1_b,
        ff_in_3(ff) + ff_down_2(ff_down))
    wt.update(zip(ff_in, landed[:2]))
    ff_a, ff_b, ff_f, w_ff_down = _ff_up(h, wt["w_ff_gate"], wt["w_ff_up"], ff_down_3(landed[2:]))
    wt["w_ff_down"] = w_ff_down.reshape(D_FF_PAD, D_MODEL)
    dr2, d_ln2_g, d_ln2_b, loss_lanes = _ff_down_loss(ff_f, wt["w_ff_down"], h, target, ln2_g, ln2_b)

    def pair_sums(names, contrib, from_sibling):
        return _pair_sums([contrib[k] for k in names], from_sibling, core, "pair_sums_" + names[0])

    d_a, d_b = _ff_down_bwd(dr2, wt["w_ff_down"], ff_a, ff_b)
    contrib = dict(w_ff_gate=_weight_grad(h, d_a, "wgrad_w_ff_gate", FF_PAD),
                   w_ff_up=_weight_grad(h, d_b, "wgrad_w_ff_up", FF_PAD),
                   w_ff_down=_weight_grad(ff_f, dr2, "wgrad_w_ff_down"))
    dr1, d_ln1_g, d_ln1_b, *from_sibling = _ff_up_bwd(
        d_a, d_b, wt["w_ff_gate"], wt["w_ff_up"], dr2, xhat1, rstd1, ln1_g, _sibling_swap_ride([contrib[k] for k in ff_weights]))
    ff_sums = pair_sums(ff_weights, contrib, from_sibling)

    d_ya, d_yssm, d_proj, d_attn, d_glu, d_ys, mixed, y_s, gy, d_bg = _mixer_bwd(
        dr1, proj, y_attn, y_ssm, glu, ys, wt["w_attn_br"], wt["w_ssm_br"], wt["w_glu"], wt["w_out"], b_gate_full)
    contrib.update(w_attn_br=_weight_grad(attn, d_ya, "wgrad_w_attn_br", 128),
                   w_ssm_br=_weight_grad(y_s, d_yssm, "wgrad_w_ssm_br", 128),
                   w_glu=_weight_grad(gy, d_glu, "wgrad_w_glu", 128),
                   w_out=_weight_grad(mixed, dr1, "wgrad_w_out"),
                   b_gate=d_bg.reshape(2, 4, 2, 128).transpose(2, 1, 0, 3))
    d_proj, *landed = _attn_bwd(q_pm, k_pm, v_pm, cos_t, sin_t, attn, lse, d_attn, d_proj,
                                _chip_swap_ride(ff_sums) + _sibling_swap_ride([contrib[k] for k in mixer_weights]))
    parts, own_sums = dict(zip(ff_weights, landed[:len(ff_weights)])), {}
    mixer_sums = pair_sums(mixer_weights, contrib, landed[len(ff_weights):])
    landed = _ssm_bwd(d_ys, proj, states, bmat, cmat, a_chunks, ssm_d, d_proj, _chip_swap_ride(mixer_sums))
    d_proj, *ssm_blocks, d_abar, d_skip = landed[:7]
    parts.update(zip(mixer_weights, landed[7:]))

    gbb_re_t, gbb_im_t, gc_re, gc_im = (blocks.reshape(SSM_GROUPS, SSM_GROUP, SSM_STATE) for blocks in ssm_blocks)
    ga_re = d_abar[:, 0, :CHUNK_STATES].reshape(SSM_GROUPS, SSM_STATE)
    ga_im = d_abar[:, 0, CHUNK_STATES:].reshape(SSM_GROUPS, SSM_STATE)
    g_a_re, g_a_im, g_log_dt, g_b_re_t, g_b_im_t = _ssm_param_bwd(
        a_re, a_im, log_dt, b_re_t, b_im_t, abar_re, abar_im, e_re, e_im, ga_re, ga_im, gbb_re_t, gbb_im_t)
    mine = [g_a_re, g_a_im, g_log_dt, g_b_re_t, g_b_im_t, gc_re, -gc_im,
            d_skip, d_ln1_g, d_ln1_b, d_ln2_g, d_ln2_b]
    small_packed = _pack_rows(mine + [loss_lanes])

    w_in_contrib, small_partly = _weight_grad_rows(x2, d_proj, core, "wgrad_w_in_first", 896, 0, 6,
                                                   _gather_first_level([small_packed]))
    w_in_contrib, from_sibling, every = _weight_grad_rows(
        x2, d_proj, core, "wgrad_w_in_rest", 896, 6, 2,
        _sibling_swap_ride([w_in_contrib], halves=False) + _gather_second_level([small_partly]), in_place=True)
    w_in_sum, = _pair_sums([w_in_contrib], [from_sibling], core, "pair_sums_w_in")
    send_sems, recv_sems, w_in_sum, landing, token = _chip_swap_start([w_in_sum], "w_in_chip_swap_start")

    def adamw_of(k):
        taken = (lambda a: a.T) if k in narrow else (lambda a: a)
        return taken(local[k]), taken(given["m_" + k][0]), taken(given["v_" + k][0]), parts[k], own_sums.get(k), k in narrow

    others = [k for k in sharded if k != "w_in"]
    updated = _adamw_many([adamw_of(k) for k in others], "adamw_others", _after(token))
    grad_x, = _grad_x(d_proj, wt["w_in"], dr1, _after(token))

    def held(k, a):
        return a.transpose(0, 1, 3, 2) if k in ("ssm_b_re", "ssm_b_im") else a

    *small_grads, loss_sum = _unpack_rows(_sum_devices(every), [held(k, given[k]).shape for k in SMALL] + [(1, 128)])
    small = _adamw_replicated([held(k, given[k]) for k in SMALL], [held(k, given["m_" + k]) for k in SMALL],
                              [held(k, given["v_" + k]) for k in SMALL], small_grads)
    loss = loss_sum[0, 0]

    (own_sums["w_in"],), (parts["w_in"],) = _chip_swap_wait(
        send_sems, recv_sems, w_in_sum, landing, [grad_x, updated[0], small[0][0]], "w_in_chip_swap_wait")
    updated += _adamw_many([adamw_of("w_in")], "adamw_w_in")

    grads, deltas, new_m, new_v = {}, {}, {}, {}
    for i, k in enumerate(others + ["w_in"]):
        out = [o.T if k in narrow else o for o in updated[4 * i:4 * i + 4]]
        grads[k], deltas[k], new_m[k], new_v[k] = (o.reshape((1,) + local[k].shape) for o in out)
    for res, values in zip((grads, deltas, new_m, new_v), (small_grads,) + small):
        res.update((k, held(k, a)) for k, a in zip(SMALL, values))

    order = ("w_in", "b_gate", "w_attn_br", "w_ssm_br", "w_out", "ssm_a_re", "ssm_a_im", "ssm_log_dt", "ssm_b_re", "ssm_b_im",
             "ssm_c_re", "ssm_c_im", "ssm_d", "w_glu", "ln1_g", "ln1_b", "w_ff_gate", "w_ff_up", "w_ff_down", "ln2_g", "ln2_b")
    return (loss, grad_x[None], *[grads[k] for k in order], *[deltas[k] for k in order], *[new_m[k] for k in order],
            *[new_v[k] for k in order])
```

```python
import functools
import math

import jax
import jax.numpy as jnp
import numpy as np
from jax import lax
from jax.experimental import pallas as pl
from jax.experimental.pallas import tpu as pltpu

F32 = jnp.float32
BF16 = jnp.bfloat16

N_DEV = 8
SEQ = 2048
D_MODEL = 1024
HEAD_DIM = 64
ATTN_WIDTH = 512
QKV_WIDTH = 1536
SSM_WIDTH = 512
SSM_GROUPS = 32
SSM_GROUP = 16
SSM_STATE = 64
IN_WIDTH = 7168
D_FF = 2816
FF_SHARD = D_FF // N_DEV
FF_PAD = 384
D_FF_PAD = FF_PAD * N_DEV
DN_ALPHA = 2.0 ** 0.25
LN_EPS = 1e-5
NEG_INF = -1e30
ROPE_THETA = 10000.0
BLOCK = 128
GROUPS = ((1, 16), (4, 4), (16, 1))

ADAM_LR = 0.001
ADAM_B1 = 0.9
ADAM_B2 = 0.999
ADAM_EPS = 1e-08
ADAM_WD = 0.01
ADAM_STEP = 10

VMEM_LIMIT = 56 * 1024 * 1024


_pallas_call = pl.pallas_call


def _cparams(**kw):
    return pltpu.CompilerParams(vmem_limit_bytes=VMEM_LIMIT, **kw)


def _dot(a, b):
    return jnp.dot(a, b, preferred_element_type=F32)


def _dot_nt(a, b):
    return lax.dot_general(a, b, (((1,), (1,)), ((), ())), preferred_element_type=F32)


def _side_by_side(w_ref, row=None):
    rows = slice(None) if row is None else pl.ds(row, 1)
    return jnp.concatenate([w_ref[i, rows, :] for i in range(w_ref.shape[0])], axis=1)


def _dot_tn(a, b):
    return lax.dot_general(a, b, (((0,), (0,)), ((), ())), preferred_element_type=F32)


def _rope_tables():
    half = HEAD_DIM // 2
    inv_freq = np.float32(ROPE_THETA) ** (-np.arange(half, dtype=np.float32) / np.float32(half))
    ang = np.arange(SEQ, dtype=np.float32)[:, None] * inv_freq[None, :]
    cos, sin = np.cos(ang).astype(np.float32), np.sin(ang).astype(np.float32)
    tables = np.tile(cos, (1, 4)), np.tile(np.concatenate([-sin, sin], axis=1), (1, 2))

    def by_phase(t):
        return np.stack([t.reshape(SEQ // d, d, 128).transpose(1, 0, 2).reshape(SEQ, 128) for d, _ in GROUPS])

    return jnp.asarray(by_phase(tables[0])), jnp.asarray(by_phase(tables[1]))


def _swap_halves(x):
    lane = lax.broadcasted_iota(jnp.int32, x.shape, 1)
    return jnp.where((lane & 63) < 32, pltpu.roll(x, 96, axis=1), pltpu.roll(x, 32, axis=1))


def _group_rows(d, nb, r, i):
    src = pl.ds(i * BLOCK, BLOCK) if d == 1 else pl.ds(r + i * BLOCK * d, BLOCK, stride=d)
    return src, pl.ds((r * nb + i) * BLOCK, BLOCK)


def _attn_masks():
    a_idx = lax.broadcasted_iota(jnp.int32, (2 * BLOCK, 2 * BLOCK), 0) & (BLOCK - 1)
    c_idx = lax.broadcasted_iota(jnp.int32, (2 * BLOCK, 2 * BLOCK), 1)
    cur_ok = jnp.logical_and(c_idx >= BLOCK, c_idx - BLOCK <= a_idx)
    prev_ok = jnp.logical_and(c_idx < BLOCK, c_idx >= a_idx)
    lane = lax.broadcasted_iota(jnp.int32, (BLOCK, 128), 1)
    return cur_ok, prev_ok, lane < HEAD_DIM


def _stack_heads(t, head0):
    zero = jnp.zeros_like(t)
    return jnp.concatenate([jnp.where(head0, t, zero), jnp.where(head0, zero, t)], axis=0)


def _unstack_heads(t2, head0):
    return jnp.where(head0, t2[:BLOCK], t2[BLOCK:])


def _attn_fwd(proj, cos_t, sin_t, ride=None):
    def body(q0, q1, q2, k0, k1, k2, v0, v1, v2, cos_ref, sin_ref, attn_ref, lse_ref, qpm_ref, kpm_ref, vpm_ref,
             qs, ks, vs, os_, ms, ls, acc, mnat, lnat):
        cur_ok, prev_ok, head0 = _attn_masks()
        ks[:BLOCK, :] = jnp.zeros((BLOCK, 128), BF16)
        vs[:BLOCK, :] = jnp.zeros((BLOCK, 128), BF16)
        for g, (d, nb) in enumerate(GROUPS):
            q_ref, k_ref, v_ref = (q0, q1, q2)[g], (k0, k1, k2)[g], (v0, v1, v2)[g]
            for r in range(d):
                for i in range(nb):
                    src, dst = _group_rows(d, nb, r, i)
                    below = pl.ds(dst.start + BLOCK, BLOCK)
                    c, s = cos_ref[g, dst, :], sin_ref[g, dst, :]
                    q = q_ref[src, :]
                    k = k_ref[src, :]
                    qs[dst, :] = ((q * c + _swap_halves(q) * s) * 0.125).astype(BF16)
                    ks[below, :] = (k * c + _swap_halves(k) * s).astype(BF16)
                    vs[below, :] = v_ref[src, :].astype(BF16)
                    qpm_ref[g, dst, :], kpm_ref[g, dst, :], vpm_ref[g, dst, :] = qs[dst, :], ks[below, :], vs[below, :]

            def block(b, carry, nb=nb):
                has_prev = (b & (nb - 1)) > 0
                cur = pl.ds(pl.multiple_of(b * BLOCK, BLOCK), BLOCK)
                window = pl.ds(pl.multiple_of(b * BLOCK, BLOCK), 2 * BLOCK)
                valid = jnp.logical_or(cur_ok, jnp.logical_and(prev_ok, has_prev))
                s = jnp.where(valid, _dot_nt(_stack_heads(qs[cur, :], head0), ks[window, :]), NEG_INF)
                m = jnp.max(s, axis=1, keepdims=True)
                p = jnp.exp(s - m)
                os_[cur, :] = _unstack_heads(_dot(p.astype(BF16), vs[window, :]), head0)
                ms[cur, :] = _unstack_heads(m, head0)
                ls[cur, :] = _unstack_heads(jnp.sum(p, axis=1, keepdims=True), head0)
                return carry

            lax.fori_loop(0, SEQ // BLOCK, block, 0, unroll=16)

            for r in range(d):
                for i in range(nb):
                    src, dst = _group_rows(d, nb, r, i)
                    if g == 0:
                        acc[src, :], mnat[src, :], lnat[src, :] = os_[dst, :], ms[dst, :], ls[dst, :]
                    else:
                        m_old, m_g = mnat[src, :], ms[dst, :]
                        m_new = jnp.maximum(m_old, m_g)
                        a_old, a_g = jnp.exp(m_old - m_new), jnp.exp(m_g - m_new)
                        acc[src, :] = a_old * acc[src, :] + a_g * os_[dst, :]
                        lnat[src, :] = a_old * lnat[src, :] + a_g * ls[dst, :]
                        mnat[src, :] = m_new
        for i in range(SEQ // BLOCK):
            rows = pl.ds(i * BLOCK, BLOCK)
            l = lnat[rows, :]
            attn_ref[rows, :] = acc[rows, :] / l
            lse_ref[rows, :] = mnat[rows, :] + jnp.log(l)

    def col(base):
        return pl.BlockSpec((SEQ, 128), lambda hp, base=base: (0, base + hp))

    in_specs = [col(g * 4) for g in range(3)] + [col(12 + g * 4) for g in range(3)] + [col(24 + g * 4) for g in range(3)]
    table = pl.BlockSpec((3, SEQ, 128), lambda hp: (0, 0, 0), pipeline_mode=pl.Buffered(1))
    out = pl.BlockSpec((SEQ, 128), lambda hp: (0, hp))
    by_phase = pl.BlockSpec((3, SEQ, 128), lambda hp: (0, 0, hp))
    return _call(
        body, "attn_fwd", (4,), in_specs + [table, table], [out, out] + [by_phase] * 3,
        [_sds((SEQ, ATTN_WIDTH), F32), _sds((SEQ, ATTN_WIDTH), F32)] + [_sds((3, SEQ, ATTN_WIDTH), BF16)] * 3,
        [pltpu.VMEM((SEQ, 128), BF16)] + [pltpu.VMEM((SEQ + BLOCK, 128), BF16)] * 2 + [pltpu.VMEM((SEQ, 128), F32)] * 6,
        [proj] * 9 + [cos_t, sin_t], ride)


def _attn_bwd_group_body(g):
    d, nb = GROUPS[g]

    def body(qs_ref, ks_ref, vs_ref, cos_ref, sin_ref, lse_ref, dattn_ref, dsum_ref, dproj_ref,
             ks, vs, dos, lss, dss, dqs, dks, dvs, stage, outs, sems):
        cur_ok, prev_ok, head0 = _attn_masks()
        qs = qs_ref.at[g]
        ks[:BLOCK, :] = jnp.zeros((BLOCK, 128), BF16)
        vs[:BLOCK, :] = jnp.zeros((BLOCK, 128), BF16)
        dks[:BLOCK, :] = jnp.zeros((BLOCK, 128), F32)
        dvs[:BLOCK, :] = jnp.zeros((BLOCK, 128), F32)
        for r in range(d):
            for i in range(nb):
                src, dst = _group_rows(d, nb, r, i)
                below = pl.ds(dst.start + BLOCK, BLOCK)
                ks[below, :] = ks_ref[g, dst, :]
                vs[below, :] = vs_ref[g, dst, :]
                dos[dst, :] = dattn_ref[src, :].astype(BF16)
                for per_head, spread in ((dsum_ref[src, :], dss), (lse_ref[src, :], lss)):
                    other = pltpu.roll(per_head, HEAD_DIM, axis=1)
                    spread[0, dst, :] = jnp.where(head0, per_head, other)
                    spread[1, dst, :] = jnp.where(head0, other, per_head)
                dks[below, :] = jnp.zeros((BLOCK, 128), F32)
                dvs[below, :] = jnp.zeros((BLOCK, 128), F32)

        def per_stacked_row(spread, cur):
            h0, h1 = spread[0, cur, :], spread[1, cur, :]
            return jnp.concatenate([jnp.concatenate([h0, h0], axis=1), jnp.concatenate([h1, h1], axis=1)], axis=0)

        def block(b, carry):
            has_prev = (b & (nb - 1)) > 0
            cur = pl.ds(pl.multiple_of(b * BLOCK, BLOCK), BLOCK)
            window = pl.ds(pl.multiple_of(b * BLOCK, BLOCK), 2 * BLOCK)
            valid = jnp.logical_or(cur_ok, jnp.logical_and(prev_ok, has_prev))
            q2, do2 = _stack_heads(qs[cur, :], head0), _stack_heads(dos[cur, :], head0)
            kw, vw = ks[window, :], vs[window, :]
            s = jnp.where(valid, _dot_nt(q2, kw), NEG_INF)
            p = jnp.exp(s - per_stacked_row(lss, cur))
            ds = (p * (_dot_nt(do2, vw) - per_stacked_row(dss, cur))).astype(BF16)
            dvs[window, :] += _dot_tn(p.astype(BF16), do2)
            dks[window, :] += _dot_tn(ds, q2)
            dqs[cur, :] = _unstack_heads(_dot(ds, kw), head0)
            return carry

        lax.fori_loop(0, SEQ // BLOCK, block, 0, unroll=16)

        hp = pl.program_id(0)
        copies = []
        for kind in range(3):
            for r in range(d):
                for i in range(nb):
                    src, dst = _group_rows(d, nb, r, i)
                    below = pl.ds(dst.start + BLOCK, BLOCK)
                    if kind == 2:
                        stage[src, :] = dvs[below, :]
                    else:
                        c, s = cos_ref[g, dst, :], sin_ref[g, dst, :]
                        t = dqs[dst, :] * 0.125 if kind == 0 else dks[below, :]
                        stage[src, :] = t * c - _swap_halves(t) * s
            for i in range(SEQ // MM_ROWS):
                rows = pl.ds(i * MM_ROWS, MM_ROWS)
                outs[kind, rows, :] = stage[rows, :].astype(BF16)
            column = pl.multiple_of((kind * 12 + g * 4 + hp) * 128, 128)
            copies.append(pltpu.make_async_copy(outs.at[kind], dproj_ref.at[:, pl.ds(column, 128)], sems.at[kind]))
            copies[-1].start()
        for cp in copies:
            cp.wait()

    return body


def _attn_bwd(q_pm, k_pm, v_pm, cos_t, sin_t, attn, lse, dattn, dproj, ride=None):
    groups = [_attn_bwd_group_body(g) for g in range(3)]

    def body(qs_ref, ks_ref, vs_ref, cos_ref, sin_ref, attn_ref, lse_ref, dattn_ref, dproj_in, dproj_ref, dsum, *scratch):
        del dproj_in
        head0 = _attn_masks()[2]
        for i in range(SEQ // BLOCK):
            rows = pl.ds(i * BLOCK, BLOCK)
            prod = dattn_ref[rows, :] * attn_ref[rows, :]
            d0 = jnp.sum(jnp.where(head0, prod, 0.0), axis=1, keepdims=True)
            d1 = jnp.sum(jnp.where(head0, 0.0, prod), axis=1, keepdims=True)
            dsum[rows, :] = jnp.where(head0, d0, d1)
        for g in range(3):
            groups[g](qs_ref, ks_ref, vs_ref, cos_ref, sin_ref, lse_ref, dattn_ref, dsum, dproj_ref, *scratch)

    def col(base):
        return pl.BlockSpec((SEQ, 128), lambda hp, base=base: (0, base + hp))

    table = pl.BlockSpec((3, SEQ, 128), lambda hp: (0, 0, 0), pipeline_mode=pl.Buffered(1))
    by_phase = pl.BlockSpec((3, SEQ, 128), lambda hp: (0, 0, hp))
    return _call(
        body, "attn_bwd", (4,), [by_phase] * 3 + [table, table, col(0), col(0), col(0), ANY],
        [ANY], [_sds((SEQ, IN_WIDTH), BF16)],
        [pltpu.VMEM((SEQ, 128), F32)]
        + [pltpu.VMEM((SEQ + BLOCK, 128), BF16)] * 2 + [pltpu.VMEM((SEQ, 128), BF16)]
        + [pltpu.VMEM((2, SEQ, 128), F32)] * 2 + [pltpu.VMEM((SEQ, 128), F32)]
        + [pltpu.VMEM((SEQ + BLOCK, 128), F32)] * 2 + [pltpu.VMEM((SEQ, 128), F32)]
        + [pltpu.VMEM((3, SEQ, 128), BF16), pltpu.SemaphoreType.DMA((3,))],
        [q_pm, k_pm, v_pm, cos_t, sin_t, attn, lse, dattn, dproj], ride, aliases={8: 0})


SSM_CHUNKS = 4
CHUNK_STATES = 512
SCAN_ROWS = 8
U_COL = (3 * QKV_WIDTH) // 128


def _cmul(xr, xi, yr, yi):
    return xr * yr - xi * yi, xr * yi + xi * yr


def _ssm_prep(a_re, a_im, log_dt, b_re_t, b_im_t):
    def body(ar_ref, ai_ref, ldt_ref, br_ref, bi_ref, abr_ref, abi_ref, er_ref, ei_ref, bbr_ref, bbi_ref):
        ar, ai = ar_ref[...], ai_ref[...]
        dt = jnp.exp(ldt_ref[...])
        mag = jnp.exp(ar * dt)
        abr, abi = mag * jnp.cos(ai * dt), mag * jnp.sin(ai * dt)
        den = ar * ar + ai * ai
        nr, ni = abr - 1.0, abi
        er, ei = (nr * ar + ni * ai) / den, (ni * ar - nr * ai) / den
        abr_ref[...], abi_ref[...], er_ref[...], ei_ref[...] = abr, abi, er, ei
        er3, ei3 = er[:, None, :], ei[:, None, :]
        br, bi = br_ref[...], bi_ref[...]
        bbr_ref[...] = er3 * br - ei3 * bi
        bbi_ref[...] = er3 * bi + ei3 * br

    gp = jax.ShapeDtypeStruct(a_re.shape, F32)
    gb = jax.ShapeDtypeStruct(b_re_t.shape, F32)
    return _pallas_call(body, name="ssm_prep", out_shape=(gp, gp, gp, gp, gb, gb))(a_re, a_im, log_dt, b_re_t, b_im_t)


def _ssm_param_bwd(a_re, a_im, log_dt, b_re_t, b_im_t, abar_re, abar_im, e_re, e_im, ga_re, ga_im, gbb_re_t, gbb_im_t):
    def body(ar_ref, ai_ref, ldt_ref, br_ref, bi_ref, abr_ref, abi_ref, er_ref, ei_ref, gar_ref, gai_ref, gbr_ref, gbi_ref,
             o_ar, o_ai, o_ldt, o_br, o_bi):
        ar, ai = ar_ref[...], ai_ref[...]
        dt = jnp.exp(ldt_ref[...])
        er, ei = er_ref[...], ei_ref[...]
        br, bi, gbr, gbi = br_ref[...], bi_ref[...], gbr_ref[...], gbi_ref[...]
        er3, ei3 = er[:, None, :], ei[:, None, :]
        o_br[...] = er3 * gbr + ei3 * gbi
        o_bi[...] = er3 * gbi - ei3 * gbr
        ge_r = jnp.sum(br * gbr + bi * gbi, axis=1)
        ge_i = jnp.sum(br * gbi - bi * gbr, axis=1)
        den = ar * ar + ai * ai
        ilr, ili = ar / den, -ai / den
        t_r, t_i = _cmul(ilr, -ili, ge_r, ge_i)
        gab_r, gab_i = gar_ref[...] + t_r, gai_ref[...] + t_i
        gz_r, gz_i = _cmul(abr_ref[...], -abi_ref[...], gab_r, gab_i)
        el_r, el_i = _cmul(er, ei, ilr, ili)
        u_r, u_i = _cmul(el_r, -el_i, ge_r, ge_i)
        o_ar[...] = dt * gz_r - u_r
        o_ai[...] = dt * gz_i - u_i
        o_ldt[...] = jnp.sum(gz_r * ar + gz_i * ai, axis=1, keepdims=True) * dt

    gp = jax.ShapeDtypeStruct(a_re.shape, F32)
    gb = jax.ShapeDtypeStruct(b_re_t.shape, F32)
    return _pallas_call(body, name="ssm_param_bwd", out_shape=(gp, gp, jax.ShapeDtypeStruct(log_dt.shape, F32), gb, gb))(
        a_re, a_im, log_dt, b_re_t, b_im_t, abar_re, abar_im, e_re, e_im, ga_re, ga_im, gbb_re_t, gbb_im_t)


def _block_diag(blocks_re, blocks_im, sign_im, rows_are_channels):
    both = jnp.stack([blocks_re, sign_im * blocks_im]).reshape(2, SSM_CHUNKS, 8, SSM_GROUP, SSM_STATE)
    eye = jnp.eye(8, dtype=F32)
    if rows_are_channels:
        return jnp.einsum("rcghp,gk->cghrkp", both, eye).reshape(SSM_CHUNKS, 128, 2 * CHUNK_STATES)
    return jnp.einsum("rcghp,gk->crkpgh", both, eye).reshape(SSM_CHUNKS, 2 * CHUNK_STATES, 128)


def _diagonal_blocks(mat, part):
    first = [part * CHUNK_STATES + SSM_STATE * g for g in range(8)]
    return jnp.concatenate([mat[SSM_GROUP * g:SSM_GROUP * (g + 1), first[g]:first[g] + SSM_STATE] for g in range(8)], axis=0)


def _scan_consts(a_ref, conj, reverse):
    ar = jnp.broadcast_to(a_ref[:, :CHUNK_STATES], (SCAN_ROWS, CHUNK_STATES))
    ai = jnp.broadcast_to(a_ref[:, CHUNK_STATES:], (SCAN_ROWS, CHUNK_STATES))
    if conj:
        ai = -ai
    row = lax.broadcasted_iota(jnp.int32, (SCAN_ROWS, CHUNK_STATES), 0)
    if reverse:
        row = SCAN_ROWS - 1 - row
    zero = jnp.zeros_like(ar)
    steps = []
    pr, pi = ar, ai
    for shift in (1, 2, 4):
        keep = row >= shift
        steps.append((SCAN_ROWS - shift if reverse else shift, jnp.where(keep, pr, zero), jnp.where(keep, pi, zero)))
        pr, pi = _cmul(pr, pi, pr, pi)
    first = row == 0
    return steps, (jnp.where(first, ar, zero), jnp.where(first, ai, zero)), first


def _scan_tile(xr, xi, prev_r, prev_i, steps, carry_in, reverse):
    edge = SCAN_ROWS - 1 if reverse else 1
    cr, ci = pltpu.roll(prev_r, edge, axis=0), pltpu.roll(prev_i, edge, axis=0)
    xr, xi = xr + carry_in[0] * cr - carry_in[1] * ci, xi + carry_in[0] * ci + carry_in[1] * cr
    for shift, mr, mi in steps:
        sr, si = pltpu.roll(xr, shift, axis=0), pltpu.roll(xi, shift, axis=0)
        xr, xi = xr + mr * sr - mi * si, xi + mr * si + mi * sr
    return xr, xi


MM_ROWS = 256


def _ssm_fwd(proj, bmat, cmat, a_chunks, d_skip, ride=None):
    def body(u_ref, b_ref, c_ref, a_ref, d_ref, y_ref, states_ref, h_ref):
        for i in range(SEQ // MM_ROWS):
            rows = pl.ds(i * MM_ROWS, MM_ROWS)
            h_ref[rows, :] = _dot(u_ref[rows, :].astype(BF16), b_ref[...])
        steps, carry_in, _ = _scan_consts(a_ref, conj=False, reverse=False)

        def tile(k, carry):
            rows = pl.ds(pl.multiple_of(k * SCAN_ROWS, SCAN_ROWS), SCAN_ROWS)
            xr, xi = _scan_tile(h_ref[rows, :CHUNK_STATES], h_ref[rows, CHUNK_STATES:], carry[0], carry[1], steps, carry_in, False)
            h_ref[rows, :CHUNK_STATES] = xr
            h_ref[rows, CHUNK_STATES:] = xi
            return xr, xi

        zero = jnp.zeros((SCAN_ROWS, CHUNK_STATES), F32)
        lax.fori_loop(0, SEQ // SCAN_ROWS, tile, (zero, zero), unroll=4)
        for i in range(SEQ // MM_ROWS):
            rows = pl.ds(i * MM_ROWS, MM_ROWS)
            states = h_ref[rows, :].astype(BF16)
            states_ref[rows, :] = states
            y_ref[rows, :] = _dot(states, c_ref[...]) + d_ref[...] * u_ref[rows, :]

    return _call(
        body, "ssm_fwd", (SSM_CHUNKS,),
        [pl.BlockSpec((SEQ, 128), lambda c: (0, U_COL + c)),
         pl.BlockSpec((None, 128, 2 * CHUNK_STATES), lambda c: (c, 0, 0)),
         pl.BlockSpec((None, 2 * CHUNK_STATES, 128), lambda c: (c, 0, 0)),
         pl.BlockSpec((None, 1, 2 * CHUNK_STATES), lambda c: (c, 0, 0)),
         pl.BlockSpec((1, 128), lambda c: (0, c))],
        [pl.BlockSpec((SEQ, 128), lambda c: (0, c)), pl.BlockSpec((SEQ, 2 * CHUNK_STATES), lambda c: (0, c))],
        [_sds((SEQ, SSM_WIDTH), F32), _sds((SEQ, SSM_CHUNKS * 2 * CHUNK_STATES), BF16)],
        [pltpu.VMEM((SEQ, 2 * CHUNK_STATES), F32)],
        [proj, bmat, cmat, a_chunks, d_skip], ride)


def _ssm_bwd(dys, proj, h, bmat, cmat, a_chunks, d_skip, dproj, ride=None):
    def body(dy_ref, u_ref, states_ref, b_ref, c_ref, a_ref, d_ref, dproj_in, du_ref, db_re_ref, db_im_ref, dc_re_ref, dc_im_ref,
             da_ref, dd_ref, g_ref, h_ref):
        del dproj_in
        dsum = jnp.zeros((1, 128), F32)
        dcm = jnp.zeros((128, 2 * CHUNK_STATES), F32)
        for i in range(SEQ // MM_ROWS):
            rows = pl.ds(i * MM_ROWS, MM_ROWS)
            h_ref[rows, :] = states_ref[rows, :].astype(F32)
            dy = dy_ref[rows, :]
            g_ref[rows, :] = _dot_nt(dy.astype(BF16), c_ref[...])
            dsum += jnp.sum(dy * u_ref[rows, :], axis=0, keepdims=True)
            dcm += _dot_tn(dy.astype(BF16), states_ref[rows, :])
        dd_ref[...] = dsum
        dc_re_ref[...] = _diagonal_blocks(dcm, 0)
        dc_im_ref[...] = _diagonal_blocks(dcm, 1)
        steps, carry_in, _ = _scan_consts(a_ref, conj=True, reverse=True)
        first_row = lax.broadcasted_iota(jnp.int32, (SCAN_ROWS, CHUNK_STATES), 0) == 0
        n_tiles = SEQ // SCAN_ROWS

        def tile(j, carry):
            k = n_tiles - 1 - j
            rows = pl.ds(pl.multiple_of(k * SCAN_ROWS, SCAN_ROWS), SCAN_ROWS)
            before = pl.ds(pl.multiple_of(jnp.maximum(k - 1, 0) * SCAN_ROWS, SCAN_ROWS), SCAN_ROWS)
            gr, gi = _scan_tile(g_ref[rows, :CHUNK_STATES], g_ref[rows, CHUNK_STATES:], carry[0], carry[1], steps, carry_in, True)
            g_ref[rows, :CHUNK_STATES] = gr
            g_ref[rows, CHUNK_STATES:] = gi
            has_before = jnp.where(k > 0, 1.0, 0.0)
            hr = jnp.where(first_row, pltpu.roll(h_ref[before, :CHUNK_STATES], 1, axis=0) * has_before,
                           pltpu.roll(h_ref[rows, :CHUNK_STATES], 1, axis=0))
            hi = jnp.where(first_row, pltpu.roll(h_ref[before, CHUNK_STATES:], 1, axis=0) * has_before,
                           pltpu.roll(h_ref[rows, CHUNK_STATES:], 1, axis=0))
            return gr, gi, carry[2] + hr * gr + hi * gi, carry[3] + hr * gi - hi * gr

        zero = jnp.zeros((SCAN_ROWS, CHUNK_STATES), F32)
        _, _, sar, sai = lax.fori_loop(0, n_tiles, tile, (zero, zero, zero, zero), unroll=4)
        da_ref[:, :CHUNK_STATES] = jnp.sum(sar, axis=0, keepdims=True)
        da_ref[:, CHUNK_STATES:] = jnp.sum(sai, axis=0, keepdims=True)
        dbm = jnp.zeros((128, 2 * CHUNK_STATES), F32)
        for i in range(SEQ // MM_ROWS):
            rows = pl.ds(i * MM_ROWS, MM_ROWS)
            g = g_ref[rows, :].astype(BF16)
            du_ref[rows, :] = (_dot_nt(g, b_ref[...]) + d_ref[...] * dy_ref[rows, :]).astype(BF16)
            dbm += _dot_tn(u_ref[rows, :].astype(BF16), g)
        db_re_ref[...] = _diagonal_blocks(dbm, 0)
        db_im_ref[...] = _diagonal_blocks(dbm, 1)

    chunk_col = pl.BlockSpec((SEQ, 128), lambda c: (0, c))
    blocks = pl.BlockSpec((None, 128, SSM_STATE), lambda c: (c, 0, 0))
    return _call(
        body, "ssm_bwd", (SSM_CHUNKS,),
        [chunk_col,
         pl.BlockSpec((SEQ, 128), lambda c: (0, U_COL + c)),
         pl.BlockSpec((SEQ, 2 * CHUNK_STATES), lambda c: (0, c)),
         pl.BlockSpec((None, 128, 2 * CHUNK_STATES), lambda c: (c, 0, 0)),
         pl.BlockSpec((None, 2 * CHUNK_STATES, 128), lambda c: (c, 0, 0)),
         pl.BlockSpec((None, 1, 2 * CHUNK_STATES), lambda c: (c, 0, 0)),
         pl.BlockSpec((1, 128), lambda c: (0, c)), ANY],
        [pl.BlockSpec((SEQ, 128), lambda c: (0, U_COL + c)), blocks, blocks, blocks, blocks,
         pl.BlockSpec((None, 1, 2 * CHUNK_STATES), lambda c: (c, 0, 0)),
         pl.BlockSpec((1, 128), lambda c: (0, c))],
        [_sds((SEQ, IN_WIDTH), BF16)] + [_sds((SSM_CHUNKS, 128, SSM_STATE), F32)] * 4
        + [_sds((SSM_CHUNKS, 1, 2 * CHUNK_STATES), F32), _sds((1, SSM_WIDTH), F32)],
        [pltpu.VMEM((SEQ, 2 * CHUNK_STATES), F32)] * 2, [dys, proj, h, bmat, cmat, a_chunks, d_skip, dproj], ride, aliases={7: 0})


def _ssm_tables(abar_re, abar_im, bbar_re_t, bbar_im_t, c_re, c_im):
    bmat = _block_diag(bbar_re_t, bbar_im_t, 1.0, True).astype(BF16)
    cmat = _block_diag(c_re, c_im, -1.0, False).astype(BF16)
    a_chunks = jnp.concatenate([abar_re.reshape(SSM_CHUNKS, 1, CHUNK_STATES), abar_im.reshape(SSM_CHUNKS, 1, CHUNK_STATES)], axis=2)
    return bmat, cmat, a_chunks


GL_COL = (3 * QKV_WIDTH + SSM_WIDTH) // D_MODEL
GELU_C = math.sqrt(2.0 / math.pi)
GELU_A = 0.044715


def _sds(shape, dtype):
    return jax.ShapeDtypeStruct(shape, dtype)


def _gelu(x):
    t = jnp.tanh(GELU_C * (x + GELU_A * x * x * x))
    return 0.5 * x * (1.0 + t), t


def _gelu_grad(x, t):
    return 0.5 * (1.0 + t) + 0.5 * x * (1.0 - t * t) * GELU_C * (1.0 + 3.0 * GELU_A * x * x)


def _layer_norm(r, g, b):
    mu = jnp.mean(r, axis=-1, keepdims=True)
    xc = r - mu
    rstd = lax.rsqrt(jnp.mean(xc * xc, axis=-1, keepdims=True) + LN_EPS)
    xhat = xc * rstd
    return xhat * g + b, xhat, rstd


def _layer_norm_bwd(dy, xhat, rstd, g):
    dxhat = dy * g
    m1 = jnp.mean(dxhat, axis=-1, keepdims=True)
    m2 = jnp.mean(dxhat * xhat, axis=-1, keepdims=True)
    return rstd * (dxhat - m1 - xhat * m2)


def _proj(x, w_in, ride=None):
    tm, tn = 1024, 1792

    def body(x_ref, w_ref, o_ref):
        o_ref[...] = _dot(x_ref[...].astype(BF16), _side_by_side(w_ref))

    return _call(
        body, "proj", (SEQ // tm, IN_WIDTH // tn),
        [pl.BlockSpec((tm, D_MODEL), lambda i, j: (i, 0)), pl.BlockSpec((2, D_MODEL, tn // 2), lambda i, j: (j, 0, 0))],
        [pl.BlockSpec((tm, tn), lambda i, j: (i, j))], [_sds((SEQ, IN_WIDTH), F32)], [], [x, w_in], ride)


def _row_spec(tm, width, col=0):
    return pl.BlockSpec((tm, width), lambda i, col=col: (i, col))


def _full_spec(shape):
    return pl.BlockSpec(shape, lambda i: (0,) * len(shape))


def _weight_spec(shape):
    return pl.BlockSpec(shape, lambda i: (0,) * len(shape), pipeline_mode=pl.Buffered(1))


def _mixer_out(attn, ys, proj, x, w_ab, w_sb, w_glu, w_out, b_gate, ln_g, ln_b, ride=None):
    tm = 512

    def body(attn_ref, ys_ref, gl0_ref, gl1_ref, x_ref, wab_ref, wsb_ref, wglu_ref, wout_ref, bg_ref, g_ref, b_ref,
             h_ref, xhat_ref, rstd_ref, glu_ref, ya_ref, yssm_ref):
        gy, _ = _gelu(ys_ref[...])
        glu = _dot(gy.astype(BF16), _side_by_side(wglu_ref))
        glu_ref[...] = glu.astype(BF16)
        y_s = glu[:, :SSM_WIDTH] * jax.nn.sigmoid(glu[:, SSM_WIDTH:])
        y_ssm = _dot(y_s.astype(BF16), _side_by_side(wsb_ref))
        y_attn = _dot(attn_ref[...].astype(BF16), _side_by_side(wab_ref))
        ya_ref[...] = y_attn.astype(BF16)
        yssm_ref[...] = y_ssm.astype(BF16)
        g0 = jax.nn.sigmoid(gl0_ref[...] + _side_by_side(bg_ref, 0))
        g1 = jax.nn.sigmoid(gl1_ref[...] + _side_by_side(bg_ref, 1))
        mixed = g0 * y_attn + g1 * y_ssm
        r1 = DN_ALPHA * x_ref[...] + _dot(mixed.astype(BF16), wout_ref[...])
        h, xhat, rstd = _layer_norm(r1, g_ref[...], b_ref[...])
        h_ref[...] = h
        xhat_ref[...] = xhat
        rstd_ref[...] = jnp.broadcast_to(rstd, (tm, 128))

    wide = _sds((SEQ, D_MODEL), F32)
    return _call(
        body, "mixer_out", (SEQ // tm,),
        [_row_spec(tm, ATTN_WIDTH), _row_spec(tm, SSM_WIDTH), _row_spec(tm, D_MODEL, GL_COL), _row_spec(tm, D_MODEL, GL_COL + 1),
         _row_spec(tm, D_MODEL), _weight_spec((N_DEV, ATTN_WIDTH, 128)), _weight_spec((N_DEV, SSM_WIDTH, 128)),
         _weight_spec((N_DEV, SSM_WIDTH, 128)), _weight_spec((D_MODEL, D_MODEL)), _full_spec((N_DEV, 2, 128)),
         _full_spec((1, D_MODEL)), _full_spec((1, D_MODEL))],
        [_row_spec(tm, D_MODEL), _row_spec(tm, D_MODEL), _row_spec(tm, 128), _row_spec(tm, D_MODEL),
         _row_spec(tm, D_MODEL), _row_spec(tm, D_MODEL)],
        [wide, wide, _sds((SEQ, 128), F32)] + [_sds((SEQ, D_MODEL), BF16)] * 3, [],
        [attn, ys, proj, proj, x, w_ab, w_sb, w_glu, w_out, b_gate, ln_g, ln_b], ride)


def _ff_up(h, w_gate, w_up, ride=None):
    tm, tn = 1024, 768

    def body(h_ref, wg_ref, wu_ref, a_ref, b_ref, f_ref):
        hb = h_ref[...].astype(BF16)
        a, b = _dot(hb, _side_by_side(wg_ref)), _dot(hb, _side_by_side(wu_ref))
        a_ref[...] = a.astype(BF16)
        b_ref[...] = b.astype(BF16)
        f_ref[...] = (a * jax.nn.sigmoid(a) * b).astype(BF16)

    tile = pl.BlockSpec((tm, tn), lambda i, j: (i, j))
    wtile = pl.BlockSpec((tn // FF_PAD, D_MODEL, FF_PAD), lambda i, j: (j, 0, 0))
    out = _sds((SEQ, D_FF_PAD), BF16)
    return _call(body, "ff_up", (SEQ // tm, D_FF_PAD // tn), [pl.BlockSpec((tm, D_MODEL), lambda i, j: (i, 0)), wtile, wtile],
                 [tile, tile, tile], [out, out, out], [], [h, w_gate, w_up], ride)


def _ff_down_loss(f, w_down, h, target, ln_g, ln_b):
    tm = 512

    def body(f_ref, w_ref, h_ref, t_ref, g_ref, b_ref, dr_ref, dg_ref, db_ref, loss_ref):
        @pl.when(pl.program_id(0) == 0)
        def _():
            dg_ref[...] = jnp.zeros_like(dg_ref)
            db_ref[...] = jnp.zeros_like(db_ref)
            loss_ref[...] = jnp.zeros_like(loss_ref)

        r2 = DN_ALPHA * h_ref[...] + _dot(f_ref[...], w_ref[...])
        g = g_ref[...]
        out, xhat, rstd = _layer_norm(r2, g, b_ref[...])
        err = out - t_ref[...]
        loss_ref[...] += 0.5 * jnp.sum(jnp.mean(err * err, axis=-1, keepdims=True), axis=0, keepdims=True)
        dout = err * (1.0 / D_MODEL)
        dg_ref[...] += jnp.sum(dout * xhat, axis=0, keepdims=True)
        db_ref[...] += jnp.sum(dout, axis=0, keepdims=True)
        dr_ref[...] = _layer_norm_bwd(dout, xhat, rstd, g)

    vec = _sds((1, D_MODEL), F32)
    return _pallas_call(
        body, name="ff_down_loss", grid=(SEQ // tm,),
        in_specs=[_row_spec(tm, D_FF_PAD), _weight_spec((D_FF_PAD, D_MODEL)), _row_spec(tm, D_MODEL), _row_spec(tm, D_MODEL),
                  _full_spec((1, D_MODEL)), _full_spec((1, D_MODEL))],
        out_specs=(_row_spec(tm, D_MODEL), _full_spec((1, D_MODEL)), _full_spec((1, D_MODEL)), _full_spec((1, 128))),
        out_shape=(_sds((SEQ, D_MODEL), F32), vec, vec, _sds((1, 128), F32)),
        compiler_params=_cparams(dimension_semantics=("arbitrary",)),
    )(f, w_down, h, target, ln_g, ln_b)


def _ff_down_bwd(dr2, w_down, a, b):
    tm, tn = 1024, 768

    def body(dr_ref, w_ref, a_ref, b_ref, da_ref, db_ref):
        df = _dot_nt(dr_ref[...].astype(BF16), w_ref[...])
        av, bv = a_ref[...].astype(F32), b_ref[...].astype(F32)
        sg = jax.nn.sigmoid(av)
        da_ref[...] = (df * bv * sg * (1.0 + av * (1.0 - sg))).astype(BF16)
        db_ref[...] = (df * av * sg).astype(BF16)

    tile = pl.BlockSpec((tm, tn), lambda i, j: (i, j))
    out = _sds((SEQ, D_FF_PAD), BF16)
    return _pallas_call(
        body, name="ff_down_bwd", grid=(SEQ // tm, D_FF_PAD // tn),
        in_specs=[pl.BlockSpec((tm, D_MODEL), lambda i, j: (i, 0)), pl.BlockSpec((tn, D_MODEL), lambda i, j: (j, 0)), tile, tile],
        out_specs=(tile, tile), out_shape=(out, out),
        compiler_params=_cparams(dimension_semantics=("arbitrary", "arbitrary")),
    )(dr2, w_down, a, b)


def _ff_up_bwd(da, db, w_gate, w_up, dr2, xhat1, rstd1, ln_g, ride=None):
    tm, tk = 1024, 768
    nk = D_FF_PAD // tk

    def body(da_ref, db_ref, wg_ref, wu_ref, dr2_ref, xhat_ref, rstd_ref, g_ref, dr1_ref, dg_ref, dbias_ref, acc):
        i, k = pl.program_id(0), pl.program_id(1)

        @pl.when(jnp.logical_and(i == 0, k == 0))
        def _():
            dg_ref[...] = jnp.zeros_like(dg_ref)
            dbias_ref[...] = jnp.zeros_like(dbias_ref)

        part = _dot_nt(da_ref[...], _side_by_side(wg_ref)) + _dot_nt(db_ref[...], _side_by_side(wu_ref))

        @pl.when(k == 0)
        def _():
            acc[...] = part

        @pl.when(k > 0)
        def _():
            acc[...] += part

        @pl.when(k == nk - 1)
        def _():
            dh = DN_ALPHA * dr2_ref[...] + acc[...]
            xhat = xhat_ref[...]
            dg_ref[...] += jnp.sum(dh * xhat, axis=0, keepdims=True)
            dbias_ref[...] += jnp.sum(dh, axis=0, keepdims=True)
            rstd = jnp.max(rstd_ref[...], axis=1, keepdims=True)
            dr1_ref[...] = _layer_norm_bwd(dh, xhat, rstd, g_ref[...])

    hid = pl.BlockSpec((tm, tk), lambda i, k: (i, k))
    wtile = pl.BlockSpec((tk // FF_PAD, D_MODEL, FF_PAD), lambda i, k: (k, 0, 0))
    row = pl.BlockSpec((tm, D_MODEL), lambda i, k: (i, 0))
    vec = pl.BlockSpec((1, D_MODEL), lambda i, k: (0, 0))
    return _call(
        body, "ff_up_bwd", (SEQ // tm, nk),
        [hid, hid, wtile, wtile, row, row, pl.BlockSpec((tm, 128), lambda i, k: (i, 0)), vec],
        [row, vec, vec], [_sds((SEQ, D_MODEL), F32), _sds((1, D_MODEL), F32), _sds((1, D_MODEL), F32)],
        [pltpu.VMEM((tm, D_MODEL), F32)], [da, db, w_gate, w_up, dr2, xhat1, rstd1, ln_g], ride)


def _mixer_bwd(dr1, proj, y_attn, y_ssm, glu, ys, w_ab, w_sb, w_glu, w_out, b_gate):
    tm = 256

    def body(dr1_ref, gl0_ref, gl1_ref, ya_ref, yssm_ref, glu_ref, ys_ref, wab_ref, wsb_ref, wglu_ref, wout_ref, bg_ref,
             dya_ref, dyssm_ref, dgl_ref, dattn_ref, dglu_ref, dys_ref, mixed_ref, ysb_ref, gy_ref, dbg_ref, stage, copied):
        @pl.when(pl.program_id(0) == 0)
        def _():
            dbg_ref[...] = jnp.zeros_like(dbg_ref)

        dmixed = _dot_nt(dr1_ref[...].astype(BF16), wout_ref[...])
        g0 = jax.nn.sigmoid(gl0_ref[...] + _side_by_side(bg_ref, 0))
        g1 = jax.nn.sigmoid(gl1_ref[...] + _side_by_side(bg_ref, 1))
        y_attn, y_ssm = ya_ref[...].astype(F32), yssm_ref[...].astype(F32)
        mixed_ref[...] = (g0 * y_attn + g1 * y_ssm).astype(BF16)
        dya = (dmixed * g0).astype(BF16)
        dyssm = (dmixed * g1).astype(BF16)
        dya_ref[...] = dya
        dyssm_ref[...] = dyssm
        dgl0 = dmixed * y_attn * g0 * (1.0 - g0)
        dgl1 = dmixed * y_ssm * g1 * (1.0 - g1)
        i, last = pl.program_id(0), SEQ // tm - 1
        slot = i & 1

        def copy_out(buffer, tile):
            window = dgl_ref.at[pl.ds(pl.multiple_of(tile * tm, tm), tm), pl.ds(GL_COL * D_MODEL, 2 * D_MODEL)]
            return pltpu.make_async_copy(stage.at[buffer], window, copied.at[buffer])

        @pl.when(i >= 2)
        def _():
            copy_out(slot, i - 2).wait()

        stage[slot, :, :D_MODEL] = dgl0.astype(BF16)
        stage[slot, :, D_MODEL:] = dgl1.astype(BF16)
        copy_out(slot, i).start()

        @pl.when(i == last)
        def _():
            copy_out(1 - slot, i - 1).wait()
            copy_out(slot, i).wait()
        dbg_ref[:, :D_MODEL] += jnp.sum(dgl0, axis=0, keepdims=True)
        dbg_ref[:, D_MODEL:] += jnp.sum(dgl1, axis=0, keepdims=True)
        dattn_ref[...] = _dot_nt(dya, _side_by_side(wab_ref))
        dy_s = _dot_nt(dyssm, _side_by_side(wsb_ref))
        glu = glu_ref[...].astype(F32)
        glu1, sg = glu[:, :SSM_WIDTH], jax.nn.sigmoid(glu[:, SSM_WIDTH:])
        ysb_ref[...] = (glu1 * sg).astype(BF16)
        dglu1 = (dy_s * sg).astype(BF16)
        dglu2 = (dy_s * glu1 * sg * (1.0 - sg)).astype(BF16)
        dglu_ref[:, :SSM_WIDTH] = dglu1
        dglu_ref[:, SSM_WIDTH:] = dglu2
        dgy = _dot_nt(jnp.concatenate([dglu1, dglu2], axis=1), _side_by_side(wglu_ref))
        ys = ys_ref[...]
        gy, t = _gelu(ys)
        gy_ref[...] = gy.astype(BF16)
        dys_ref[...] = dgy * _gelu_grad(ys, t)

    wide_b, half_b = _sds((SEQ, D_MODEL), BF16), _sds((SEQ, SSM_WIDTH), BF16)
    half_f = _sds((SEQ, SSM_WIDTH), F32)
    return _pallas_call(
        body, name="mixer_bwd", grid=(SEQ // tm,),
        in_specs=[_row_spec(tm, D_MODEL), _row_spec(tm, D_MODEL, GL_COL), _row_spec(tm, D_MODEL, GL_COL + 1), _row_spec(tm, D_MODEL),
                  _row_spec(tm, D_MODEL), _row_spec(tm, D_MODEL), _row_spec(tm, SSM_WIDTH), _full_spec((N_DEV, ATTN_WIDTH, 128)),
                  _full_spec((N_DEV, SSM_WIDTH, 128)), _full_spec((N_DEV, SSM_WIDTH, 128)), _full_spec((D_MODEL, D_MODEL)),
                  _full_spec((N_DEV, 2, 128))],
        out_specs=(_row_spec(tm, D_MODEL), _row_spec(tm, D_MODEL), ANY, _row_spec(tm, ATTN_WIDTH),
                   _row_spec(tm, D_MODEL), _row_spec(tm, SSM_WIDTH), _row_spec(tm, D_MODEL), _row_spec(tm, SSM_WIDTH),
                   _row_spec(tm, SSM_WIDTH), _full_spec((1, 2 * D_MODEL))),
        out_shape=(wide_b, wide_b, _sds((SEQ, IN_WIDTH), BF16), half_f, wide_b, half_f, wide_b, half_b, half_b,
                   _sds((1, 2 * D_MODEL), F32)),
        scratch_shapes=[pltpu.VMEM((2, tm, 2 * D_MODEL), BF16), pltpu.SemaphoreType.DMA((2,))],
        compiler_params=_cparams(dimension_semantics=("arbitrary",)),
    )(dr1, proj, proj, y_attn, y_ssm, glu, ys, w_ab, w_sb, w_glu, w_out, b_gate)


def _grad_x(dproj, w_in, dr1, ride=None):
    tm, tk = 1024, 1792
    nk = IN_WIDTH // tk

    def body(dp_ref, w_ref, dr1_ref, o_ref, acc):
        k = pl.program_id(1)
        part = _dot_nt(dp_ref[...], _side_by_side(w_ref))

        @pl.when(k == 0)
        def _():
            acc[...] = part

        @pl.when(k > 0)
        def _():
            acc[...] += part

        @pl.when(k == nk - 1)
        def _():
            o_ref[...] = DN_ALPHA * dr1_ref[...] + acc[...]

    row = pl.BlockSpec((tm, D_MODEL), lambda i, k: (i, 0))
    return _call(
        body, "grad_x", (SEQ // tm, nk),
        [pl.BlockSpec((tm, tk), lambda i, k: (i, k)), pl.BlockSpec((2, D_MODEL, tk // 2), lambda i, k: (k, 0, 0)), row],
        [row], [_sds((SEQ, D_MODEL), F32)], [pltpu.VMEM((tm, D_MODEL), F32)], [dproj, w_in, dr1], ride)


def _weight_grad(a, b, name, shard_cols=None):
    k, n = a.shape[1], b.shape[1]
    tk = k if shard_cols else k // N_DEV
    tn = n // 4 if shard_cols else min(n, 1024)

    def body(a_ref, b_ref, o_ref):
        grad = _dot_tn(a_ref[...].astype(BF16), b_ref[...].astype(BF16))
        if shard_cols:
            o_ref[0] = grad[:, :shard_cols].astype(BF16)
            o_ref[1] = grad[:, shard_cols:].astype(BF16)
        else:
            o_ref[...] = grad.astype(BF16)

    if shard_cols:
        out_spec = pl.BlockSpec((2, None, tk, shard_cols), lambda kk, j: (0, j, kk, 0))
        out_shape = _sds((2, 4, k, shard_cols), BF16)
    else:
        out_spec = pl.BlockSpec((None, None, tk, tn), lambda kk, j: (kk % 2, kk // 2, 0, j))
        out_shape = _sds((2, 4, tk, n), BF16)
    return _call(body, name, (k // tk, n // tn),
                 [pl.BlockSpec((SEQ, tk), lambda kk, j: (0, kk)), pl.BlockSpec((SEQ, tn), lambda kk, j: (0, j))],
                 [out_spec], [out_shape], [], [a, b])[0]


def _weight_grad_rows(a, b, core, name, shard_cols, first, count, ride, in_place=False):
    k = a.shape[1]

    def body(a_ref, b_ref, o_ref):
        o_ref[...] = _dot_tn(a_ref[...].astype(BF16), b_ref[...].astype(BF16)).astype(BF16)

    def shard(j, core_ref):
        row = j + first
        return 0, jnp.where(row < 4, 2 * row + 1 - core_ref[0], 2 * (row - 4) + core_ref[0])

    return _call(body, name, (count,),
                 [pl.BlockSpec((SEQ, k), lambda j, core_ref: (0, 0), pipeline_mode=pl.Buffered(1)),
                  pl.BlockSpec((SEQ, shard_cols), shard)],
                 [pl.BlockSpec((None, k, shard_cols), lambda j, core_ref: (j + first, 0, 0))],
                 [_sds((N_DEV, k, shard_cols), BF16)], [], [a, b], ride, aliases={2: 0} if in_place else None, prefetch=core)


MESH = pl.DeviceIdType.MESH
ANY = pl.BlockSpec(memory_space=pl.ANY)


def _place():
    return lax.axis_index("x"), lax.axis_index("y"), lax.axis_index("c")


def _other_chips(x, y):
    return [(1 - x, y), (x, 1 - y), (1 - x, 1 - y)]


class _Ride:
    def __init__(self, operands, results, aliases, sems, start, wait):
        self.operands, self.results, self.aliases, self.sems = list(operands), list(results), dict(aliases), list(sems)
        self.start, self.wait = start, wait

    def __add__(self, other):
        n_in, n_out, n_sem = len(self.operands), len(self.results), len(self.sems)

        def both(which):
            def run(ins, outs, sems):
                getattr(self, which)(ins[:n_in], outs[:n_out], sems[:n_sem])
                getattr(other, which)(ins[n_in:], outs[n_out:], sems[n_sem:])
            return run

        aliases = {**self.aliases, **{n_in + i: n_out + j for i, j in other.aliases.items()}}
        return _Ride(self.operands + other.operands, self.results + other.results, aliases, self.sems + other.sems,
                     both("start"), both("wait"))


def _call(body, name, grid, in_specs, out_specs, out_shape, scratch_shapes, operands, ride=None, aliases=None, prefetch=None):
    in_specs, out_specs, out_shape = list(in_specs), list(out_specs), list(out_shape)
    scratch_shapes, operands, aliases = list(scratch_shapes), list(operands), dict(aliases or {})
    kernel_body = body
    if ride is not None:
        n_in, n_out, n_scr, r_in, r_out = len(in_specs), len(out_specs), len(scratch_shapes), len(ride.operands), len(ride.results)

        def kernel_body(*refs):
            out0, scr0 = n_in + r_in, n_in + r_in + n_out + r_out
            ride_refs = (refs[n_in:out0], refs[out0 + n_out:scr0], refs[scr0 + n_scr:])
            ids = [pl.program_id(i) for i in range(len(grid))]
            first = functools.reduce(jnp.logical_and, [i == 0 for i in ids])
            last = functools.reduce(jnp.logical_and, [i == g - 1 for i, g in zip(ids, grid)])

            @pl.when(first)
            def _():
                ride.start(*ride_refs)

            body(*refs[:n_in], *refs[out0:out0 + n_out], *refs[scr0:scr0 + n_scr])

            @pl.when(last)
            def _():
                ride.wait(*ride_refs)

        aliases.update({n_in + i: n_out + j for i, j in ride.aliases.items()})
        in_specs += [ANY] * r_in
        out_specs += [ANY] * r_out
        out_shape += ride.results
        scratch_shapes += ride.sems
        operands += ride.operands
    params = _cparams(dimension_semantics=("arbitrary",) * len(grid))
    if prefetch is None:
        return _pallas_call(
            kernel_body, name=name, grid=grid, in_specs=in_specs, out_specs=out_specs, out_shape=out_shape,
            scratch_shapes=scratch_shapes, input_output_aliases=aliases, compiler_params=params,
        )(*operands)

    def with_prefetch(prefetch_ref, *refs):
        kernel_body(*refs)

    return _pallas_call(
        with_prefetch, name=name,
        grid_spec=pltpu.PrefetchScalarGridSpec(num_scalar_prefetch=1, grid=grid, in_specs=in_specs, out_specs=out_specs,
                                               scratch_shapes=scratch_shapes),
        out_shape=out_shape, input_output_aliases={i + 1: j for i, j in aliases.items()}, compiler_params=params,
    )(prefetch, *operands)


def _after(*arrays):
    return _Ride(arrays, [], {}, [], lambda *refs: None, lambda *refs: None)


def _gather_first_level(shards):
    n = len(shards)

    def copies(ins, outs, sems, landed):
        send_sems, recv_sems, local_sems = sems
        x, y, c = _place()
        peers = [(x, y, 1 - c)] + [(px, py, c) for px, py in _other_chips(x, y)]

        def row(peer):
            return 4 * x + 2 * y + c if not landed else 4 * peer[0] + 2 * peer[1] + peer[2]

        local = [pltpu.make_async_copy(ins[a], outs[a].at[4 * x + 2 * y + c], local_sems.at[a]) for a in range(n)]
        remote = [pltpu.make_async_remote_copy(
            src_ref=ins[a], dst_ref=outs[a].at[row(peer)], send_sem=send_sems.at[a, k], recv_sem=recv_sems.at[a, k],
            device_id=peer, device_id_type=MESH) for a in range(n) for k, peer in enumerate(peers)]
        return local, remote

    def start(ins, outs, sems):
        local, remote = copies(ins, outs, sems, False)
        for cp in local + remote:
            cp.start()

    def wait(ins, outs, sems):
        local, sent = copies(ins, outs, sems, False)
        for cp in copies(ins, outs, sems, True)[1]:
            cp.wait_recv()
        for cp in sent:
            cp.wait_send()
        for cp in local:
            cp.wait()

    return _Ride(shards, [_sds((N_DEV,) + s.shape, s.dtype) for s in shards], {},
                 [pltpu.SemaphoreType.DMA((n, 4)), pltpu.SemaphoreType.DMA((n, 4)), pltpu.SemaphoreType.DMA((n,))], start, wait)


def _gather_second_level(buffers):
    n = len(buffers)

    def copies(outs, sems, core):
        send_sems, recv_sems = sems
        x, y, c = _place()
        return [pltpu.make_async_remote_copy(
            src_ref=outs[a].at[4 * px + 2 * py + core], dst_ref=outs[a].at[4 * px + 2 * py + core], send_sem=send_sems.at[a, j],
            recv_sem=recv_sems.at[a, j], device_id=(x, y, 1 - c), device_id_type=MESH)
            for a in range(n) for j, (px, py) in enumerate(_other_chips(x, y))]

    def start(ins, outs, sems):
        for cp in copies(outs, sems, lax.axis_index("c")):
            cp.start()

    def wait(ins, outs, sems):
        for cp in copies(outs, sems, 1 - lax.axis_index("c")):
            cp.wait_recv()
        for cp in copies(outs, sems, lax.axis_index("c")):
            cp.wait_send()

    return _Ride(buffers, [_sds(b.shape, b.dtype) for b in buffers], {i: i for i in range(n)},
                 [pltpu.SemaphoreType.DMA((n, 3)), pltpu.SemaphoreType.DMA((n, 3))], start, wait)


def _relayed_gather(shards):
    n = len(shards)
    buffers = [_sds((N_DEV,) + s.shape, s.dtype) for s in shards]
    dma = pltpu.SemaphoreType.DMA

    def remote(src, dst, send_sem, recv_sem, to):
        return pltpu.make_async_remote_copy(src_ref=src, dst_ref=dst, send_sem=send_sem, recv_sem=recv_sem,
                                            device_id=to, device_id_type=MESH)

    def row(px, py, pc):
        return 4 * px + 2 * py + pc

    def ride(operands, aliases, sems, copies):
        def start(ins, outs, sem_refs):
            local, sent = copies(ins, outs, sem_refs, False)
            for cp in local + sent:
                cp.start()

        def wait(ins, outs, sem_refs):
            local, sent = copies(ins, outs, sem_refs, False)
            for cp in copies(ins, outs, sem_refs, True)[1]:
                cp.wait_recv()
            for cp in sent:
                cp.wait_send()
            for cp in local:
                cp.wait()

        return _Ride(operands, buffers, aliases, sems, start, wait)

    def first(ins, outs, sems, landed):
        x, y, c = _place()
        peers = [(x, y, 1 - c), (1 - x, y, c), (x, 1 - y, c)]
        local = [pltpu.make_async_copy(ins[a], outs[a].at[row(x, y, c)], sems[2].at[a]) for a in range(n)]
        return local, [remote(ins[a], outs[a].at[row(*peer) if landed else row(x, y, c)], sems[0].at[a, k], sems[1].at[a, k], peer)
                       for a in range(n) for k, peer in enumerate(peers)]

    def second(ins, outs, sems, landed):
        x, y, c = _place()
        mine = 1 - c if landed else c
        copies = []
        for a in range(n):
            half = shards[a].shape[0] // 2
            over_x, over_y, diagonal = outs[a].at[row(1 - x, y, mine)], outs[a].at[row(x, 1 - y, mine)], outs[a].at[row(1 - x, 1 - y, c)]
            from_x, from_y = (ins[n + 2 * a], ins[n + 2 * a + 1]) if len(ins) > n and not landed else (over_x, over_y)
            lower, upper = pl.ds(0, half), pl.ds(half, half)
            copies += [remote(from_x, over_x, sems[0].at[a, 0], sems[1].at[a, 0], (x, y, 1 - c)),
                       remote(from_y, over_y, sems[0].at[a, 1], sems[1].at[a, 1], (x, y, 1 - c))]
            if landed:
                copies += [remote(diagonal.at[lower], diagonal.at[lower], sems[0].at[a, 2], sems[1].at[a, 2], (1 - x, y, c)),
                           remote(diagonal.at[upper], diagonal.at[upper], sems[0].at[a, 3], sems[1].at[a, 3], (x, 1 - y, c))]
            else:
                copies += [remote(from_y.at[lower], over_y.at[lower], sems[0].at[a, 2], sems[1].at[a, 2], (1 - x, y, c)),
                           remote(from_x.at[upper], over_x.at[upper], sems[0].at[a, 3], sems[1].at[a, 3], (x, 1 - y, c))]
        return [], copies

    def third(ins, outs, sems, landed):
        x, y, c = _place()
        return [], [remote(outs[a].at[row(1 - x, 1 - y, 1 - c if landed else c)], outs[a].at[row(1 - x, 1 - y, 1 - c if landed else c)],
                           sems[0].at[a], sems[1].at[a], (x, y, 1 - c)) for a in range(n)]

    def later(copies, n_sems):
        return lambda partly: ride(partly, {i: i for i in range(n)}, [dma((n,) + n_sems), dma((n,) + n_sems)], copies)

    return ride(shards, {}, [dma((n, 3)), dma((n, 3)), dma((n,))], first), later(second, (4,)), later(third, ())


def _sibling_swap_ride(grads, halves=True):
    n = len(grads)

    def copies(ins, outs, sems):
        x, y, c = _place()
        return [pltpu.make_async_remote_copy(
            src_ref=ins[a].at[1 - c] if halves else ins[a].at[pl.ds(0, 4)], dst_ref=outs[a], send_sem=sems[0].at[a],
            recv_sem=sems[1].at[a], device_id=(x, y, 1 - c), device_id_type=MESH) for a in range(n)]

    def start(ins, outs, sems):
        for cp in copies(ins, outs, sems):
            cp.start()

    def wait(ins, outs, sems):
        for cp in copies(ins, outs, sems):
            cp.wait()

    return _Ride(grads, [_sds((4,) + g.shape[-2:], g.dtype) for g in grads], {},
                 [pltpu.SemaphoreType.DMA((n,)), pltpu.SemaphoreType.DMA((n,))], start, wait)


def _chip_swap_ride(sums):
    n = len(sums)

    def copies(ins, outs, sems, landed):
        send_sems, recv_sems, local_sems = sems
        x, y, c = _place()
        mine = 2 * x + y
        local = [pltpu.make_async_copy(ins[a].at[mine], outs[a].at[mine], local_sems.at[a]) for a in range(n)]
        remote = [pltpu.make_async_remote_copy(
            src_ref=ins[a].at[2 * px + py], dst_ref=outs[a].at[2 * px + py if landed else mine], send_sem=send_sems.at[a, j],
            recv_sem=recv_sems.at[a, j], device_id=(px, py, c), device_id_type=MESH)
            for a in range(n) for j, (px, py) in enumerate(_other_chips(x, y))]
        return local, remote

    def start(ins, outs, sems):
        local, remote = copies(ins, outs, sems, False)
        for cp in local + remote:
            cp.start()

    def wait(ins, outs, sems):
        local, sent = copies(ins, outs, sems, False)
        for cp in copies(ins, outs, sems, True)[1]:
            cp.wait_recv()
        for cp in sent:
            cp.wait_send()
        for cp in local:
            cp.wait()

    return _Ride(sums, [_sds(s.shape, s.dtype) for s in sums], {},
                 [pltpu.SemaphoreType.DMA((n, 3)), pltpu.SemaphoreType.DMA((n, 3)), pltpu.SemaphoreType.DMA((n,))], start, wait)


def _send_buffers(shards, name):
    n = len(shards)

    def body(*refs):
        for (w, transposed, rows, cols), w_ref, o_ref in zip(shards, refs[:n], refs[n:]):
            if transposed:
                c, r = w.shape
                padded = jnp.concatenate([w_ref[...], jnp.zeros((cols - c, r), F32)], axis=0) if cols > c else w_ref[...]
                o_ref[...] = padded.T.astype(BF16)
            else:
                r, c = w.shape
                if (r, c) != (rows, cols):
                    o_ref[...] = jnp.zeros((rows, cols), BF16)
                o_ref[:r, :c] = w_ref[...].astype(BF16)

    return _pallas_call(body, name=name, out_shape=[_sds((rows, cols), BF16) for _, _, rows, cols in shards])(
        *[w for w, _, _, _ in shards])


def _all_gather(shards, name):
    n = len(shards)
    first, second, third = _relayed_gather(shards)
    levels = [first, second(shards), third(shards)]
    counts = [len(level.sems) for level in levels]

    def body(*refs):
        ins, outs, sems, staged = refs[:n], refs[n:2 * n], refs[2 * n:-3 * n], refs[-3 * n:]
        x, y, c = _place()
        for a in range(n):
            pltpu.sync_copy(ins[a], staged[a])
        for i, level in enumerate(levels):
            mine = sems[sum(counts[:i]):sum(counts[:i + 1])]
            if i == 1:
                for a in range(n):
                    pltpu.sync_copy(outs[a].at[4 * (1 - x) + 2 * y + c], staged[n + 2 * a])
                    pltpu.sync_copy(outs[a].at[4 * x + 2 * (1 - y) + c], staged[n + 2 * a + 1])
            level.start(staged, outs, mine)
            level.wait(staged, outs, mine)

    return _pallas_call(
        body, name=name, in_specs=[ANY] * n, out_specs=[ANY] * n, out_shape=first.results,
        scratch_shapes=[s for level in levels for s in level.sems] + [pltpu.VMEM(s.shape, s.dtype) for s in shards]
        + [pltpu.VMEM(s.shape, s.dtype) for s in shards for _ in range(2)],
    )(*shards)


HBM = pl.BlockSpec(memory_space=pltpu.HBM)
SEMAPHORES = pl.BlockSpec(memory_space=pltpu.SEMAPHORE)
IN_FLIGHT = pltpu.CompilerParams(has_side_effects=pltpu.SideEffectType.DATAFLOW_SIDE_EFFECTING)


def _chip_swap_copies(src_refs, land_refs, send_sems, recv_sems, landed):
    x, y, c = _place()
    return [pltpu.make_async_remote_copy(
        src_ref=src.at[2 * px + py], dst_ref=land.at[2 * px + py if landed else 2 * x + y], send_sem=send_sems.at[3 * a + j],
        recv_sem=recv_sems.at[3 * a + j], device_id=(px, py, c), device_id_type=MESH)
        for a, (src, land) in enumerate(zip(src_refs, land_refs)) for j, (px, py) in enumerate(_other_chips(x, y))]


def _chip_swap_start(sums, name):
    n = len(sums)

    def body(*refs):
        src_refs, land_refs, (send_sems, recv_sems), token = refs[:n], refs[n:2 * n], refs[2 * n:2 * n + 2], refs[-1]
        for cp in _chip_swap_copies(src_refs, land_refs, send_sems, recv_sems, False):
            cp.start()
        token[...] = jnp.zeros_like(token)

    kept = [pltpu.HBM(s.shape, s.dtype) for s in sums]
    out = _pallas_call(
        body, name=name,
        out_shape=[pltpu.SemaphoreType.DMA((3 * n,)), pltpu.SemaphoreType.DMA((3 * n,))] + kept + kept + [_sds((8, 128), F32)],
        in_specs=[HBM] * (2 * n), out_specs=[SEMAPHORES, SEMAPHORES] + [HBM] * (2 * n) + [pl.BlockSpec(memory_space=pltpu.VMEM)],
        input_output_aliases={i: 2 + i for i in range(2 * n)}, compiler_params=IN_FLIGHT,
    )(*[pltpu.with_memory_space_constraint(s, pltpu.HBM) for s in sums],
      *[pltpu.with_memory_space_constraint(lax.empty(s.shape, s.dtype), pltpu.HBM) for s in sums])
    return out[0], out[1], out[2:2 + n], out[2 + n:2 + 2 * n], out[-1]


def _chip_swap_wait(send_sems, recv_sems, sums, landings, after, name):
    n = len(sums)

    def body(*refs):
        src_refs, land_refs, (send_sems, recv_sems) = refs[:n], refs[n:2 * n], refs[2 * n:2 * n + 2]
        for cp in _chip_swap_copies(src_refs, land_refs, send_sems, recv_sems, False):
            cp.wait_send()
        for cp in _chip_swap_copies(src_refs, land_refs, send_sems, recv_sems, True):
            cp.wait_recv()

    out = _pallas_call(
        body, name=name, out_shape=[pltpu.HBM(s.shape, s.dtype) for s in list(sums) + list(landings)],
        in_specs=[HBM] * (2 * n) + [SEMAPHORES, SEMAPHORES] + [ANY] * len(after), out_specs=[HBM] * (2 * n),
        input_output_aliases={i: i for i in range(2 * n)}, compiler_params=IN_FLIGHT,
    )(*sums, *landings, send_sems, recv_sems, *after)
    return out[:n], out[n:]


def _pair_sums(gs, rs, core, name):
    n_arrays = len(gs)

    def body(core_ref, *refs):
        for g_ref, r_ref, o_ref in zip(refs[:n_arrays], refs[n_arrays:2 * n_arrays], refs[2 * n_arrays:]):
            o_ref[...] = (g_ref[...].astype(F32) + r_ref[...].astype(F32)).astype(o_ref.dtype)

    def chip(g):
        return pl.BlockSpec((None,) + g.shape[-2:], lambda p, core_ref: (p, 0, 0))

    def own(g):
        if g.ndim == 3:
            return pl.BlockSpec((None,) + g.shape[-2:], lambda p, core_ref: (p + 4, 0, 0))
        return pl.BlockSpec((None, None) + g.shape[2:], lambda p, core_ref: (core_ref[0], p, 0, 0))

    return _pallas_call(
        body, name=name,
        grid_spec=pltpu.PrefetchScalarGridSpec(
            num_scalar_prefetch=1, grid=(4,), in_specs=[own(g) for g in gs] + [chip(g) for g in gs],
            out_specs=[chip(g) for g in gs]),
        out_shape=[_sds((4,) + g.shape[-2:], g.dtype) for g in gs], compiler_params=_cparams(dimension_semantics=("arbitrary",)),
    )(core, *gs, *rs)


def _adamw_math(w, g, m, v):
    m = ADAM_B1 * m + (1.0 - ADAM_B1) * g
    v = ADAM_B2 * v + (1.0 - ADAM_B2) * (g * g)
    m_hat = m / (1.0 - ADAM_B1 ** ADAM_STEP)
    v_hat = v / (1.0 - ADAM_B2 ** ADAM_STEP)
    return -ADAM_LR * (m_hat / (jnp.sqrt(v_hat) + ADAM_EPS) + ADAM_WD * w), m, v


def _adamw_many(weights, name, ride=None):
    steps = 4
    in_specs, out_specs, out_shape, operands, tiles = [], [], [], [], []
    for w, m, v, parts, own, transposed in weights:
        _, pr, pc = parts.shape
        if transposed:
            c, r = w.shape
            tile = pl.BlockSpec((c, r // steps), lambda i: (0, i))
            part_tile = pl.BlockSpec((4, r // steps, pc), lambda i: (0, i, 0))
            tiles.append((c, r // steps))
        elif w.shape[0] % (8 * steps) == 0:
            r, c = w.shape
            tile = pl.BlockSpec((r // steps, c), lambda i: (i, 0))
            part_tile = pl.BlockSpec((4, r // steps, pc), lambda i: (0, i, 0))
            tiles.append((r // steps, c))
        else:
            tile = pl.BlockSpec(w.shape, lambda i: (0, 0))
            part_tile = pl.BlockSpec(parts.shape, lambda i: (0, 0, 0))
            tiles.append(w.shape)
        in_specs += [tile, tile, tile] + [part_tile] * (1 if own is None else 2)
        out_specs += [tile] * 4
        out_shape += [_sds(w.shape, F32)] * 4
        operands += [w, m, v, parts] + ([] if own is None else [own])
    n_in = len(operands)

    def body(*refs):
        ins, outs = list(refs[:n_in]), refs[n_in:]
        this_chip = 2 * lax.axis_index("x") + lax.axis_index("y")
        for k, (_, _, _, _, own, transposed) in enumerate(weights):
            w_ref, m_ref, v_ref, p_ref = ins[:4]
            own_ref = None if own is None else ins[4]
            del ins[:4 if own is None else 5]
            rows, cols = tiles[k]
            g = None
            for q in range(4):
                index = (q,) if transposed else (q, slice(0, rows), slice(0, cols))
                part = p_ref[index] if own is None else jnp.where(this_chip == q, own_ref[index], p_ref[index])
                g = part.astype(F32) if g is None else g + part.astype(F32)
            if transposed:
                g = g.T[:rows]
            g_out, d_out, m_out, v_out = outs[4 * k:4 * k + 4]
            g_out[...] = g
            d_out[...], m_out[...], v_out[...] = _adamw_math(w_ref[...], g, m_ref[...], v_ref[...])

    return _call(body, name, (steps,), in_specs, out_specs, out_shape, [], operands, ride)


SMALL = ("ssm_a_re", "ssm_a_im", "ssm_log_dt", "ssm_b_re", "ssm_b_im", "ssm_c_re", "ssm_c_im", "ssm_d",
         "ln1_g", "ln1_b", "ln2_g", "ln2_b")


def _pack_rows(arrays):
    rows = []
    for a in arrays:
        flat = a.reshape(-1)
        rows.append(jnp.pad(flat, (0, -flat.shape[0] % 128)).reshape(-1, 128))
    packed = jnp.concatenate(rows, axis=0)
    return jnp.pad(packed, ((0, -packed.shape[0] % 8), (0, 0)))


def _unpack_rows(packed, shapes):
    out, row = [], 0
    for shape in shapes:
        size = math.prod(shape)
        n_rows = -(-size // 128)
        out.append(packed[row:row + n_rows].reshape(-1)[:size].reshape(shape))
        row += n_rows
    return out


def _sum_devices(parts):
    def body(p_ref, o_ref):
        total = p_ref[0]
        for dev in range(1, N_DEV):
            total = total + p_ref[dev]
        o_ref[...] = total

    return _pallas_call(body, name="sum_devices", out_shape=_sds(parts.shape[1:], F32))(parts)


def _adamw_replicated(ws, ms, vs, gs):
    n = len(ws)

    def body(*refs):
        w_refs, m_refs, v_refs, g_refs, d_out, m_out, v_out = (refs[i * n:(i + 1) * n] for i in range(7))
        for i in range(n):
            d_out[i][...], m_out[i][...], v_out[i][...] = _adamw_math(w_refs[i][...], g_refs[i][...], m_refs[i][...], v_refs[i][...])

    out = _pallas_call(body, name="adamw_replicated", out_shape=[_sds(w.shape, F32) for w in ws] * 3,
                       compiler_params=_cparams())(*ws, *ms, *vs, *gs)
    return out[:n], out[n:2 * n], out[2 * n:]


def kernel(x, w_in, b_gate, w_attn_br, w_ssm_br, w_out, ssm_a_re, ssm_a_im, ssm_log_dt, ssm_b_re, ssm_b_im, ssm_c_re, ssm_c_im, ssm_d, w_glu, ln1_g, ln1_b, w_ff_gate, w_ff_up, w_ff_down, ln2_g, ln2_b, loss_target, m_w_in, m_b_gate, m_w_attn_br, m_w_ssm_br, m_w_out, m_ssm_a_re, m_ssm_a_im, m_ssm_log_dt, m_ssm_b_re, m_ssm_b_im, m_ssm_c_re, m_ssm_c_im, m_ssm_d, m_w_glu, m_ln1_g, m_ln1_b, m_w_ff_gate, m_w_ff_up, m_w_ff_down, m_ln2_g, m_ln2_b, v_w_in, v_b_gate, v_w_attn_br, v_w_ssm_br, v_w_out, v_ssm_a_re, v_ssm_a_im, v_ssm_log_dt, v_ssm_b_re, v_ssm_b_im, v_ssm_c_re, v_ssm_c_im, v_ssm_d, v_w_glu, v_ln1_g, v_ln1_b, v_w_ff_gate, v_w_ff_up, v_w_ff_down, v_ln2_g, v_ln2_b):
    given = dict(locals())
    x2, target = x[0], loss_target[0]
    core = lax.axis_index("c").astype(jnp.int32).reshape(1)

    sharded = ("w_in", "w_attn_br", "w_ssm_br", "w_glu", "w_ff_gate", "w_ff_up", "b_gate", "w_out", "w_ff_down")
    send_shape = dict(w_in=(D_MODEL, 896), w_attn_br=(ATTN_WIDTH, 128), w_ssm_br=(SSM_WIDTH, 128), w_glu=(SSM_WIDTH, 128),
                      w_out=(128, D_MODEL), w_ff_gate=(D_MODEL, FF_PAD), w_ff_up=(D_MODEL, FF_PAD), w_ff_down=(FF_PAD, D_MODEL))
    local = {k: given[k][0] for k in sharded}
    narrow = ("w_ff_gate", "w_ff_up")
    def to_send(k):
        return (local[k].T, True, *send_shape[k]) if k in narrow else (local[k], False, *send_shape[k])

    later = [k for k in sharded if k not in ("w_in", "b_gate")]
    sends = dict(zip(["w_in"] + later, _send_buffers([to_send("w_in")], "send_w_in")
                     + _send_buffers([to_send(k) for k in later], "send_weights")))
    sends["b_gate"] = local["b_gate"]
    mixer_weights = ("w_attn_br", "w_ssm_br", "w_glu", "b_gate", "w_out")
    ff_weights = ("w_ff_gate", "w_ff_up", "w_ff_down")
    wt = {}
    wt["w_in"], = _all_gather([sends["w_in"]], "gather_w_in")

    a_re, a_im, log_dt = ssm_a_re[0], ssm_a_im[0], ssm_log_dt[0].reshape(SSM_GROUPS, 1)
    b_re_t, b_im_t = ssm_b_re[0].transpose(0, 2, 1), ssm_b_im[0].transpose(0, 2, 1)
    abar_re, abar_im, e_re, e_im, bbar_re_t, bbar_im_t = _ssm_prep(a_re, a_im, log_dt, b_re_t, b_im_t)
    bmat, cmat, a_chunks = _ssm_tables(abar_re, abar_im, bbar_re_t, bbar_im_t, ssm_c_re[0], ssm_c_im[0])
    cos_t, sin_t = _rope_tables()

    big_mixer, ff_in = [k for k in mixer_weights if k != "b_gate"], ("w_ff_gate", "w_ff_up")
    n_mixer = len(big_mixer)
    mixer_1, mixer_2, mixer_3 = _relayed_gather([sends[k] for k in big_mixer])
    ff_in_1, ff_in_2, ff_in_3 = _relayed_gather([sends[k] for k in ff_in])
    ff_down_1, ff_down_2, ff_down_3 = _relayed_gather([sends["w_ff_down"]])
    proj, *landed = _proj(x2, wt["w_in"], mixer_1 + _gather_first_level([sends["b_gate"]]))
    mixer, bias = landed[:n_mixer], landed[n_mixer:]
    attn, lse, q_pm, k_pm, v_pm, *landed = _attn_fwd(proj, cos_t, sin_t,
                                                     mixer_2(mixer) + _gather_second_level(bias) + ff_in_1)
    mixer, b_gate_full, ff = landed[:n_mixer], landed[n_mixer], landed[n_mixer + 1:]
    ys, states, *landed = _ssm_fwd(proj, bmat, cmat, a_chunks, ssm_d, mixer_3(mixer) + ff_in_2(ff) + ff_down_1)
    wt.update(zip(big_mixer, landed[:n_mixer]))
    ff, ff_down = landed[n_mixer:n_mixer + 2], landed[n_mixer + 2:]
    wt["w_out"] = wt["w_out"].reshape(D_MODEL, D_MODEL)
    h, xhat1, rstd1, glu, y_attn, y_ssm, *landed = _mixer_out(
        attn, ys, proj, x2, wt["w_attn_br"], wt["w_ssm_br"], wt["w_glu"], wt["w_out"], b_gate_full, ln1_g, ln1_b,
        ff_in_3(ff) + ff_down_2(ff_down))
    wt.update(zip(ff_in, landed[:2]))
    ff_a, ff_b, ff_f, w_ff_down = _ff_up(h, wt["w_ff_gate"], wt["w_ff_up"], ff_down_3(landed[2:]))
    wt["w_ff_down"] = w_ff_down.reshape(D_FF_PAD, D_MODEL)
    dr2, d_ln2_g, d_ln2_b, loss_lanes = _ff_down_loss(ff_f, wt["w_ff_down"], h, target, ln2_g, ln2_b)

    def pair_sums(names, contrib, from_sibling):
        return _pair_sums([contrib[k] for k in names], from_sibling, core, "pair_sums_" + names[0])

    d_a, d_b = _ff_down_bwd(dr2, wt["w_ff_down"], ff_a, ff_b)
    contrib = dict(w_ff_gate=_weight_grad(h, d_a, "wgrad_w_ff_gate", FF_PAD),
                   w_ff_up=_weight_grad(h, d_b, "wgrad_w_ff_up", FF_PAD),
                   w_ff_down=_weight_grad(ff_f, dr2, "wgrad_w_ff_down"))
    dr1, d_ln1_g, d_ln1_b, *from_sibling = _ff_up_bwd(
        d_a, d_b, wt["w_ff_gate"], wt["w_ff_up"], dr2, xhat1, rstd1, ln1_g, _sibling_swap_ride([contrib[k] for k in ff_weights]))
    ff_sums = pair_sums(ff_weights, contrib, from_sibling)

    d_ya, d_yssm, d_proj, d_attn, d_glu, d_ys, mixed, y_s, gy, d_bg = _mixer_bwd(
        dr1, proj, y_attn, y_ssm, glu, ys, wt["w_attn_br"], wt["w_ssm_br"], wt["w_glu"], wt["w_out"], b_gate_full)
    contrib.update(w_attn_br=_weight_grad(attn, d_ya, "wgrad_w_attn_br", 128),
                   w_ssm_br=_weight_grad(y_s, d_yssm, "wgrad_w_ssm_br", 128),
                   w_glu=_weight_grad(gy, d_glu, "wgrad_w_glu", 128),
                   w_out=_weight_grad(mixed, dr1, "wgrad_w_out"),
                   b_gate=d_bg.reshape(2, 4, 2, 128).transpose(2, 1, 0, 3))
    d_proj, *landed = _attn_bwd(q_pm, k_pm, v_pm, cos_t, sin_t, attn, lse, d_attn, d_proj,
                                _chip_swap_ride(ff_sums) + _sibling_swap_ride([contrib[k] for k in mixer_weights]))
    parts, own_sums = dict(zip(ff_weights, landed[:len(ff_weights)])), {}
    mixer_sums = pair_sums(mixer_weights, contrib, landed[len(ff_weights):])
    landed = _ssm_bwd(d_ys, proj, states, bmat, cmat, a_chunks, ssm_d, d_proj, _chip_swap_ride(mixer_sums))
    d_proj, *ssm_blocks, d_abar, d_skip = landed[:7]
    parts.update(zip(mixer_weights, landed[7:]))

    gbb_re_t, gbb_im_t, gc_re, gc_im = (blocks.reshape(SSM_GROUPS, SSM_GROUP, SSM_STATE) for blocks in ssm_blocks)
    ga_re = d_abar[:, 0, :CHUNK_STATES].reshape(SSM_GROUPS, SSM_STATE)
    ga_im = d_abar[:, 0, CHUNK_STATES:].reshape(SSM_GROUPS, SSM_STATE)
    g_a_re, g_a_im, g_log_dt, g_b_re_t, g_b_im_t = _ssm_param_bwd(
        a_re, a_im, log_dt, b_re_t, b_im_t, abar_re, abar_im, e_re, e_im, ga_re, ga_im, gbb_re_t, gbb_im_t)
    mine = [g_a_re, g_a_im, g_log_dt, g_b_re_t, g_b_im_t, gc_re, -gc_im,
            d_skip, d_ln1_g, d_ln1_b, d_ln2_g, d_ln2_b]
    small_packed = _pack_rows(mine + [loss_lanes])

    w_in_contrib, small_partly = _weight_grad_rows(x2, d_proj, core, "wgrad_w_in_first", 896, 0, 6,
                                                   _gather_first_level([small_packed]))
    w_in_contrib, from_sibling, every = _weight_grad_rows(
        x2, d_proj, core, "wgrad_w_in_rest", 896, 6, 2,
        _sibling_swap_ride([w_in_contrib], halves=False) + _gather_second_level([small_partly]), in_place=True)
    w_in_sum, = _pair_sums([w_in_contrib], [from_sibling], core, "pair_sums_w_in")
    send_sems, recv_sems, w_in_sum, landing, token = _chip_swap_start([w_in_sum], "w_in_chip_swap_start")

    def adamw_of(k):
        taken = (lambda a: a.T) if k in narrow else (lambda a: a)
        return taken(local[k]), taken(given["m_" + k][0]), taken(given["v_" + k][0]), parts[k], own_sums.get(k), k in narrow

    others = [k for k in sharded if k != "w_in"]
    updated = _adamw_many([adamw_of(k) for k in others], "adamw_others", _after(token))
    grad_x, = _grad_x(d_proj, wt["w_in"], dr1, _after(token))

    def held(k, a):
        return a.transpose(0, 1, 3, 2) if k in ("ssm_b_re", "ssm_b_im") else a

    *small_grads, loss_sum = _unpack_rows(_sum_devices(every), [held(k, given[k]).shape for k in SMALL] + [(1, 128)])
    small = _adamw_replicated([held(k, given[k]) for k in SMALL], [held(k, given["m_" + k]) for k in SMALL],
                              [held(k, given["v_" + k]) for k in SMALL], small_grads)
    loss = loss_sum[0, 0]

    (own_sums["w_in"],), (parts["w_in"],) = _chip_swap_wait(
        send_sems, recv_sems, w_in_sum, landing, [grad_x, updated[0], small[0][0]], "w_in_chip_swap_wait")
    updated += _adamw_many([adamw_of("w_in")], "adamw_w_in")

    grads, deltas, new_m, new_v = {}, {}, {}, {}
    for i, k in enumerate(others + ["w_in"]):
        out = [o.T if k in narrow else o for o in updated[4 * i:4 * i + 4]]
        grads[k], deltas[k], new_m[k], new_v[k] = (o.reshape((1,) + local[k].shape) for o in out)
    for res, values in zip((grads, deltas, new_m, new_v), (small_grads,) + small):
        res.update((k, held(k, a)) for k, a in zip(SMALL, values))

    order = ("w_in", "b_gate", "w_attn_br", "w_ssm_br", "w_out", "ssm_a_re", "ssm_a_im", "ssm_log_dt", "ssm_b_re", "ssm_b_im",
             "ssm_c_re", "ssm_c_im", "ssm_d", "w_glu", "ln1_g", "ln1_b", "w_ff_gate", "w_ff_up", "w_ff_down", "ln2_g", "ln2_b")
    return (loss, grad_x[None], *[grads[k] for k in order], *[deltas[k] for k in order], *[new_m[k] for k in order],
            *[new_v[k] for k in order])
```

```python
import functools
import math

import jax
import jax.numpy as jnp
import numpy as np
from jax import lax
from jax.experimental import pallas as pl
from jax.experimental.pallas import tpu as pltpu

F32 = jnp.float32
BF16 = jnp.bfloat16

N_DEV = 8
SEQ = 2048
D_MODEL = 1024
HEAD_DIM = 64
ATTN_WIDTH = 512
QKV_WIDTH = 1536
SSM_WIDTH = 512
SSM_GROUPS = 32
SSM_GROUP = 16
SSM_STATE = 64
IN_WIDTH = 7168
D_FF = 2816
FF_SHARD = D_FF // N_DEV
FF_PAD = 384
D_FF_PAD = FF_PAD * N_DEV
DN_ALPHA = 2.0 ** 0.25
LN_EPS = 1e-5
NEG_INF = -1e30
ROPE_THETA = 10000.0
BLOCK = 128
GROUPS = ((1, 16), (4, 4), (16, 1))

ADAM_LR = 0.001
ADAM_B1 = 0.9
ADAM_B2 = 0.999
ADAM_EPS = 1e-08
ADAM_WD = 0.01
ADAM_STEP = 10

VMEM_LIMIT = 56 * 1024 * 1024


_pallas_call = pl.pallas_call


def _cparams(**kw):
    return pltpu.CompilerParams(vmem_limit_bytes=VMEM_LIMIT, **kw)


def _dot(a, b):
    return jnp.dot(a, b, preferred_element_type=F32)


def _dot_nt(a, b):
    return lax.dot_general(a, b, (((1,), (1,)), ((), ())), preferred_element_type=F32)


def _side_by_side(w_ref, row=None):
    rows = slice(None) if row is None else pl.ds(row, 1)
    return jnp.concatenate([w_ref[i, rows, :] for i in range(w_ref.shape[0])], axis=1)


def _dot_tn(a, b):
    return lax.dot_general(a, b, (((0,), (0,)), ((), ())), preferred_element_type=F32)


def _rope_tables():
    half = HEAD_DIM // 2
    inv_freq = np.float32(ROPE_THETA) ** (-np.arange(half, dtype=np.float32) / np.float32(half))
    ang = np.arange(SEQ, dtype=np.float32)[:, None] * inv_freq[None, :]
    cos, sin = np.cos(ang).astype(np.float32), np.sin(ang).astype(np.float32)
    tables = np.tile(cos, (1, 4)), np.tile(np.concatenate([-sin, sin], axis=1), (1, 2))

    def by_phase(t):
        return np.stack([t.reshape(SEQ // d, d, 128).transpose(1, 0, 2).reshape(SEQ, 128) for d, _ in GROUPS])

    return jnp.asarray(by_phase(tables[0])), jnp.asarray(by_phase(tables[1]))


def _swap_halves(x):
    lane = lax.broadcasted_iota(jnp.int32, x.shape, 1)
    return jnp.where((lane & 63) < 32, pltpu.roll(x, 96, axis=1), pltpu.roll(x, 32, axis=1))


def _group_rows(d, nb, r, i):
    src = pl.ds(i * BLOCK, BLOCK) if d == 1 else pl.ds(r + i * BLOCK * d, BLOCK, stride=d)
    return src, pl.ds((r * nb + i) * BLOCK, BLOCK)


def _attn_masks():
    a_idx = lax.broadcasted_iota(jnp.int32, (2 * BLOCK, 2 * BLOCK), 0) & (BLOCK - 1)
    c_idx = lax.broadcasted_iota(jnp.int32, (2 * BLOCK, 2 * BLOCK), 1)
    cur_ok = jnp.logical_and(c_idx >= BLOCK, c_idx - BLOCK <= a_idx)
    prev_ok = jnp.logical_and(c_idx < BLOCK, c_idx >= a_idx)
    lane = lax.broadcasted_iota(jnp.int32, (BLOCK, 128), 1)
    return cur_ok, prev_ok, lane < HEAD_DIM


def _stack_heads(t, head0):
    zero = jnp.zeros_like(t)
    return jnp.concatenate([jnp.where(head0, t, zero), jnp.where(head0, zero, t)], axis=0)


def _unstack_heads(t2, head0):
    return jnp.where(head0, t2[:BLOCK], t2[BLOCK:])


def _attn_fwd(proj, cos_t, sin_t, ride=None):
    def body(q0, q1, q2, k0, k1, k2, v0, v1, v2, cos_ref, sin_ref, attn_ref, lse_ref, qpm_ref, kpm_ref, vpm_ref,
             qs, ks, vs, os_, ms, ls, acc, mnat, lnat):
        cur_ok, prev_ok, head0 = _attn_masks()
        ks[:BLOCK, :] = jnp.zeros((BLOCK, 128), BF16)
        vs[:BLOCK, :] = jnp.zeros((BLOCK, 128), BF16)
        for g, (d, nb) in enumerate(GROUPS):
            q_ref, k_ref, v_ref = (q0, q1, q2)[g], (k0, k1, k2)[g], (v0, v1, v2)[g]
            for r in range(d):
                for i in range(nb):
                    src, dst = _group_rows(d, nb, r, i)
                    below = pl.ds(dst.start + BLOCK, BLOCK)
                    c, s = cos_ref[g, dst, :], sin_ref[g, dst, :]
                    q = q_ref[src, :]
                    k = k_ref[src, :]
                    qs[dst, :] = ((q * c + _swap_halves(q) * s) * 0.125).astype(BF16)
                    ks[below, :] = (k * c + _swap_halves(k) * s).astype(BF16)
                    vs[below, :] = v_ref[src, :].astype(BF16)
                    qpm_ref[g, dst, :], kpm_ref[g, dst, :], vpm_ref[g, dst, :] = qs[dst, :], ks[below, :], vs[below, :]

            def block(b, carry, nb=nb):
                has_prev = (b & (nb - 1)) > 0
                cur = pl.ds(pl.multiple_of(b * BLOCK, BLOCK), BLOCK)
                window = pl.ds(pl.multiple_of(b * BLOCK, BLOCK), 2 * BLOCK)
                valid = jnp.logical_or(cur_ok, jnp.logical_and(prev_ok, has_prev))
                s = jnp.where(valid, _dot_nt(_stack_heads(qs[cur, :], head0), ks[window, :]), NEG_INF)
                m = jnp.max(s, axis=1, keepdims=True)
                p = jnp.exp(s - m)
                os_[cur, :] = _unstack_heads(_dot(p.astype(BF16), vs[window, :]), head0)
                ms[cur, :] = _unstack_heads(m, head0)
                ls[cur, :] = _unstack_heads(jnp.sum(p, axis=1, keepdims=True), head0)
                return carry

            lax.fori_loop(0, SEQ // BLOCK, block, 0, unroll=16)

            for r in range(d):
                for i in range(nb):
                    src, dst = _group_rows(d, nb, r, i)
                    if g == 0:
                        acc[src, :], mnat[src, :], lnat[src, :] = os_[dst, :], ms[dst, :], ls[dst, :]
                    else:
                        m_old, m_g = mnat[src, :], ms[dst, :]
                        m_new = jnp.maximum(m_old, m_g)
                        a_old, a_g = jnp.exp(m_old - m_new), jnp.exp(m_g - m_new)
                        acc[src, :] = a_old * acc[src, :] + a_g * os_[dst, :]
                        lnat[src, :] = a_old * lnat[src, :] + a_g * ls[dst, :]
                        mnat[src, :] = m_new
        for i in range(SEQ // BLOCK):
            rows = pl.ds(i * BLOCK, BLOCK)
            l = lnat[rows, :]
            attn_ref[rows, :] = acc[rows, :] / l
            lse_ref[rows, :] = mnat[rows, :] + jnp.log(l)

    def col(base):
        return pl.BlockSpec((SEQ, 128), lambda hp, base=base: (0, base + hp))

    in_specs = [col(g * 4) for g in range(3)] + [col(12 + g * 4) for g in range(3)] + [col(24 + g * 4) for g in range(3)]
    table = pl.BlockSpec((3, SEQ, 128), lambda hp: (0, 0, 0), pipeline_mode=pl.Buffered(1))
    out = pl.BlockSpec((SEQ, 128), lambda hp: (0, hp))
    by_phase = pl.BlockSpec((3, SEQ, 128), lambda hp: (0, 0, hp))
    return _call(
        body, "attn_fwd", (4,), in_specs + [table, table], [out, out] + [by_phase] * 3,
        [_sds((SEQ, ATTN_WIDTH), F32), _sds((SEQ, ATTN_WIDTH), F32)] + [_sds((3, SEQ, ATTN_WIDTH), BF16)] * 3,
        [pltpu.VMEM((SEQ, 128), BF16)] + [pltpu.VMEM((SEQ + BLOCK, 128), BF16)] * 2 + [pltpu.VMEM((SEQ, 128), F32)] * 6,
        [proj] * 9 + [cos_t, sin_t], ride)


def _attn_bwd_group_body(g):
    d, nb = GROUPS[g]

    def body(qs_ref, ks_ref, vs_ref, cos_ref, sin_ref, lse_ref, dattn_ref, dsum_ref, dproj_ref,
             ks, vs, dos, lss, dss, dqs, dks, dvs, stage, outs, sems):
        cur_ok, prev_ok, head0 = _attn_masks()
        qs = qs_ref.at[g]
        ks[:BLOCK, :] = jnp.zeros((BLOCK, 128), BF16)
        vs[:BLOCK, :] = jnp.zeros((BLOCK, 128), BF16)
        dks[:BLOCK, :] = jnp.zeros((BLOCK, 128), F32)
        dvs[:BLOCK, :] = jnp.zeros((BLOCK, 128), F32)
        for r in range(d):
            for i in range(nb):
                src, dst = _group_rows(d, nb, r, i)
                below = pl.ds(dst.start + BLOCK, BLOCK)
                ks[below, :] = ks_ref[g, dst, :]
                vs[below, :] = vs_ref[g, dst, :]
                dos[dst, :] = dattn_ref[src, :].astype(BF16)
                for per_head, spread in ((dsum_ref[src, :], dss), (lse_ref[src, :], lss)):
                    other = pltpu.roll(per_head, HEAD_DIM, axis=1)
                    spread[0, dst, :] = jnp.where(head0, per_head, other)
                    spread[1, dst, :] = jnp.where(head0, other, per_head)
                dks[below, :] = jnp.zeros((BLOCK, 128), F32)
                dvs[below, :] = jnp.zeros((BLOCK, 128), F32)

        def per_stacked_row(spread, cur):
            h0, h1 = spread[0, cur, :], spread[1, cur, :]
            return jnp.concatenate([jnp.concatenate([h0, h0], axis=1), jnp.concatenate([h1, h1], axis=1)], axis=0)

        def block(b, carry):
            has_prev = (b & (nb - 1)) > 0
            cur = pl.ds(pl.multiple_of(b * BLOCK, BLOCK), BLOCK)
            window = pl.ds(pl.multiple_of(b * BLOCK, BLOCK), 2 * BLOCK)
            valid = jnp.logical_or(cur_ok, jnp.logical_and(prev_ok, has_prev))
            q2, do2 = _stack_heads(qs[cur, :], head0), _stack_heads(dos[cur, :], head0)
            kw, vw = ks[window, :], vs[window, :]
            s = jnp.where(valid, _dot_nt(q2, kw), NEG_INF)
            p = jnp.exp(s - per_stacked_row(lss, cur))
            ds = (p * (_dot_nt(do2, vw) - per_stacked_row(dss, cur))).astype(BF16)
            dvs[window, :] += _dot_tn(p.astype(BF16), do2)
            dks[window, :] += _dot_tn(ds, q2)
            dqs[cur, :] = _unstack_heads(_dot(ds, kw), head0)
            return carry

        lax.fori_loop(0, SEQ // BLOCK, block, 0, unroll=16)

        hp = pl.program_id(0)
        copies = []
        for kind in range(3):
            for r in range(d):
                for i in range(nb):
                    src, dst = _group_rows(d, nb, r, i)
                    below = pl.ds(dst.start + BLOCK, BLOCK)
                    if kind == 2:
                        stage[src, :] = dvs[below, :]
                    else:
                        c, s = cos_ref[g, dst, :], sin_ref[g, dst, :]
                        t = dqs[dst, :] * 0.125 if kind == 0 else dks[below, :]
                        stage[src, :] = t * c - _swap_halves(t) * s
            for i in range(SEQ // MM_ROWS):
                rows = pl.ds(i * MM_ROWS, MM_ROWS)
                outs[kind, rows, :] = stage[rows, :].astype(BF16)
            column = pl.multiple_of((kind * 12 + g * 4 + hp) * 128, 128)
            copies.append(pltpu.make_async_copy(outs.at[kind], dproj_ref.at[:, pl.ds(column, 128)], sems.at[kind]))
            copies[-1].start()
        for cp in copies:
            cp.wait()

    return body


def _attn_bwd(q_pm, k_pm, v_pm, cos_t, sin_t, attn, lse, dattn, dproj, ride=None):
    groups = [_attn_bwd_group_body(g) for g in range(3)]

    def body(qs_ref, ks_ref, vs_ref, cos_ref, sin_ref, attn_ref, lse_ref, dattn_ref, dproj_in, dproj_ref, dsum, *scratch):
        del dproj_in
        head0 = _attn_masks()[2]
        for i in range(SEQ // BLOCK):
            rows = pl.ds(i * BLOCK, BLOCK)
            prod = dattn_ref[rows, :] * attn_ref[rows, :]
            d0 = jnp.sum(jnp.where(head0, prod, 0.0), axis=1, keepdims=True)
            d1 = jnp.sum(jnp.where(head0, 0.0, prod), axis=1, keepdims=True)
            dsum[rows, :] = jnp.where(head0, d0, d1)
        for g in range(3):
            groups[g](qs_ref, ks_ref, vs_ref, cos_ref, sin_ref, lse_ref, dattn_ref, dsum, dproj_ref, *scratch)

    def col(base):
        return pl.BlockSpec((SEQ, 128), lambda hp, base=base: (0, base + hp))

    table = pl.BlockSpec((3, SEQ, 128), lambda hp: (0, 0, 0), pipeline_mode=pl.Buffered(1))
    by_phase = pl.BlockSpec((3, SEQ, 128), lambda hp: (0, 0, hp))
    return _call(
        body, "attn_bwd", (4,), [by_phase] * 3 + [table, table, col(0), col(0), col(0), ANY],
        [ANY], [_sds((SEQ, IN_WIDTH), BF16)],
        [pltpu.VMEM((SEQ, 128), F32)]
        + [pltpu.VMEM((SEQ + BLOCK, 128), BF16)] * 2 + [pltpu.VMEM((SEQ, 128), BF16)]
        + [pltpu.VMEM((2, SEQ, 128), F32)] * 2 + [pltpu.VMEM((SEQ, 128), F32)]
        + [pltpu.VMEM((SEQ + BLOCK, 128), F32)] * 2 + [pltpu.VMEM((SEQ, 128), F32)]
        + [pltpu.VMEM((3, SEQ, 128), BF16), pltpu.SemaphoreType.DMA((3,))],
        [q_pm, k_pm, v_pm, cos_t, sin_t, attn, lse, dattn, dproj], ride, aliases={8: 0})


SSM_CHUNKS = 4
CHUNK_STATES = 512
SCAN_ROWS = 8
U_COL = (3 * QKV_WIDTH) // 128


def _cmul(xr, xi, yr, yi):
    return xr * yr - xi * yi, xr * yi + xi * yr


def _ssm_prep(a_re, a_im, log_dt, b_re_t, b_im_t):
    def body(ar_ref, ai_ref, ldt_ref, br_ref, bi_ref, abr_ref, abi_ref, er_ref, ei_ref, bbr_ref, bbi_ref):
        ar, ai = ar_ref[...], ai_ref[...]
        dt = jnp.exp(ldt_ref[...])
        mag = jnp.exp(ar * dt)
        abr, abi = mag * jnp.cos(ai * dt), mag * jnp.sin(ai * dt)
        den = ar * ar + ai * ai
        nr, ni = abr - 1.0, abi
        er, ei = (nr * ar + ni * ai) / den, (ni * ar - nr * ai) / den
        abr_ref[...], abi_ref[...], er_ref[...], ei_ref[...] = abr, abi, er, ei
        er3, ei3 = er[:, None, :], ei[:, None, :]
        br, bi = br_ref[...], bi_ref[...]
        bbr_ref[...] = er3 * br - ei3 * bi
        bbi_ref[...] = er3 * bi + ei3 * br

    gp = jax.ShapeDtypeStruct(a_re.shape, F32)
    gb = jax.ShapeDtypeStruct(b_re_t.shape, F32)
    return _pallas_call(body, name="ssm_prep", out_shape=(gp, gp, gp, gp, gb, gb))(a_re, a_im, log_dt, b_re_t, b_im_t)


def _ssm_param_bwd(a_re, a_im, log_dt, b_re_t, b_im_t, abar_re, abar_im, e_re, e_im, ga_re, ga_im, gbb_re_t, gbb_im_t):
    def body(ar_ref, ai_ref, ldt_ref, br_ref, bi_ref, abr_ref, abi_ref, er_ref, ei_ref, gar_ref, gai_ref, gbr_ref, gbi_ref,
             o_ar, o_ai, o_ldt, o_br, o_bi):
        ar, ai = ar_ref[...], ai_ref[...]
        dt = jnp.exp(ldt_ref[...])
        er, ei = er_ref[...], ei_ref[...]
        br, bi, gbr, gbi = br_ref[...], bi_ref[...], gbr_ref[...], gbi_ref[...]
        er3, ei3 = er[:, None, :], ei[:, None, :]
        o_br[...] = er3 * gbr + ei3 * gbi
        o_bi[...] = er3 * gbi - ei3 * gbr
        ge_r = jnp.sum(br * gbr + bi * gbi, axis=1)
        ge_i = jnp.sum(br * gbi - bi * gbr, axis=1)
        den = ar * ar + ai * ai
        ilr, ili = ar / den, -ai / den
        t_r, t_i = _cmul(ilr, -ili, ge_r, ge_i)
        gab_r, gab_i = gar_ref[...] + t_r, gai_ref[...] + t_i
        gz_r, gz_i = _cmul(abr_ref[...], -abi_ref[...], gab_r, gab_i)
        el_r, el_i = _cmul(er, ei, ilr, ili)
        u_r, u_i = _cmul(el_r, -el_i, ge_r, ge_i)
        o_ar[...] = dt * gz_r - u_r
        o_ai[...] = dt * gz_i - u_i
        o_ldt[...] = jnp.sum(gz_r * ar + gz_i * ai, axis=1, keepdims=True) * dt

    gp = jax.ShapeDtypeStruct(a_re.shape, F32)
    gb = jax.ShapeDtypeStruct(b_re_t.shape, F32)
    return _pallas_call(body, name="ssm_param_bwd", out_shape=(gp, gp, jax.ShapeDtypeStruct(log_dt.shape, F32), gb, gb))(
        a_re, a_im, log_dt, b_re_t, b_im_t, abar_re, abar_im, e_re, e_im, ga_re, ga_im, gbb_re_t, gbb_im_t)


def _block_diag(blocks_re, blocks_im, sign_im, rows_are_channels):
    both = jnp.stack([blocks_re, sign_im * blocks_im]).reshape(2, SSM_CHUNKS, 8, SSM_GROUP, SSM_STATE)
    eye = jnp.eye(8, dtype=F32)
    if rows_are_channels:
        return jnp.einsum("rcghp,gk->cghrkp", both, eye).reshape(SSM_CHUNKS, 128, 2 * CHUNK_STATES)
    return jnp.einsum("rcghp,gk->crkpgh", both, eye).reshape(SSM_CHUNKS, 2 * CHUNK_STATES, 128)


def _diagonal_blocks(mat, part):
    first = [part * CHUNK_STATES + SSM_STATE * g for g in range(8)]
    return jnp.concatenate([mat[SSM_GROUP * g:SSM_GROUP * (g + 1), first[g]:first[g] + SSM_STATE] for g in range(8)], axis=0)


def _scan_consts(a_ref, conj, reverse):
    ar = jnp.broadcast_to(a_ref[:, :CHUNK_STATES], (SCAN_ROWS, CHUNK_STATES))
    ai = jnp.broadcast_to(a_ref[:, CHUNK_STATES:], (SCAN_ROWS, CHUNK_STATES))
    if conj:
        ai = -ai
    row = lax.broadcasted_iota(jnp.int32, (SCAN_ROWS, CHUNK_STATES), 0)
    if reverse:
        row = SCAN_ROWS - 1 - row
    zero = jnp.zeros_like(ar)
    steps = []
    pr, pi = ar, ai
    for shift in (1, 2, 4):
        keep = row >= shift
        steps.append((SCAN_ROWS - shift if reverse else shift, jnp.where(keep, pr, zero), jnp.where(keep, pi, zero)))
        pr, pi = _cmul(pr, pi, pr, pi)
    first = row == 0
    return steps, (jnp.where(first, ar, zero), jnp.where(first, ai, zero)), first


def _scan_tile(xr, xi, prev_r, prev_i, steps, carry_in, reverse):
    edge = SCAN_ROWS - 1 if reverse else 1
    cr, ci = pltpu.roll(prev_r, edge, axis=0), pltpu.roll(prev_i, edge, axis=0)
    xr, xi = xr + carry_in[0] * cr - carry_in[1] * ci, xi + carry_in[0] * ci + carry_in[1] * cr
    for shift, mr, mi in steps:
        sr, si = pltpu.roll(xr, shift, axis=0), pltpu.roll(xi, shift, axis=0)
        xr, xi = xr + mr * sr - mi * si, xi + mr * si + mi * sr
    return xr, xi


MM_ROWS = 256


def _ssm_fwd(proj, bmat, cmat, a_chunks, d_skip, ride=None):
    def body(u_ref, b_ref, c_ref, a_ref, d_ref, y_ref, states_ref, h_ref):
        for i in range(SEQ // MM_ROWS):
            rows = pl.ds(i * MM_ROWS, MM_ROWS)
            h_ref[rows, :] = _dot(u_ref[rows, :].astype(BF16), b_ref[...])
        steps, carry_in, _ = _scan_consts(a_ref, conj=False, reverse=False)

        def tile(k, carry):
            rows = pl.ds(pl.multiple_of(k * SCAN_ROWS, SCAN_ROWS), SCAN_ROWS)
            xr, xi = _scan_tile(h_ref[rows, :CHUNK_STATES], h_ref[rows, CHUNK_STATES:], carry[0], carry[1], steps, carry_in, False)
            h_ref[rows, :CHUNK_STATES] = xr
            h_ref[rows, CHUNK_STATES:] = xi
            return xr, xi

        zero = jnp.zeros((SCAN_ROWS, CHUNK_STATES), F32)
        lax.fori_loop(0, SEQ // SCAN_ROWS, tile, (zero, zero), unroll=4)
        for i in range(SEQ // MM_ROWS):
            rows = pl.ds(i * MM_ROWS, MM_ROWS)
            states = h_ref[rows, :].astype(BF16)
            states_ref[rows, :] = states
            y_ref[rows, :] = _dot(states, c_ref[...]) + d_ref[...] * u_ref[rows, :]

    return _call(
        body, "ssm_fwd", (SSM_CHUNKS,),
        [pl.BlockSpec((SEQ, 128), lambda c: (0, U_COL + c)),
         pl.BlockSpec((None, 128, 2 * CHUNK_STATES), lambda c: (c, 0, 0)),
         pl.BlockSpec((None, 2 * CHUNK_STATES, 128), lambda c: (c, 0, 0)),
         pl.BlockSpec((None, 1, 2 * CHUNK_STATES), lambda c: (c, 0, 0)),
         pl.BlockSpec((1, 128), lambda c: (0, c))],
        [pl.BlockSpec((SEQ, 128), lambda c: (0, c)), pl.BlockSpec((SEQ, 2 * CHUNK_STATES), lambda c: (0, c))],
        [_sds((SEQ, SSM_WIDTH), F32), _sds((SEQ, SSM_CHUNKS * 2 * CHUNK_STATES), BF16)],
        [pltpu.VMEM((SEQ, 2 * CHUNK_STATES), F32)],
        [proj, bmat, cmat, a_chunks, d_skip], ride)


def _ssm_bwd(dys, proj, h, bmat, cmat, a_chunks, d_skip, dproj, ride=None):
    def body(dy_ref, u_ref, states_ref, b_ref, c_ref, a_ref, d_ref, dproj_in, du_ref, db_re_ref, db_im_ref, dc_re_ref, dc_im_ref,
             da_ref, dd_ref, g_ref, h_ref):
        del dproj_in
        dsum = jnp.zeros((1, 128), F32)
        dcm = jnp.zeros((128, 2 * CHUNK_STATES), F32)
        for i in range(SEQ // MM_ROWS):
            rows = pl.ds(i * MM_ROWS, MM_ROWS)
            h_ref[rows, :] = states_ref[rows, :].astype(F32)
            dy = dy_ref[rows, :]
            g_ref[rows, :] = _dot_nt(dy.astype(BF16), c_ref[...])
            dsum += jnp.sum(dy * u_ref[rows, :], axis=0, keepdims=True)
            dcm += _dot_tn(dy.astype(BF16), states_ref[rows, :])
        dd_ref[...] = dsum
        dc_re_ref[...] = _diagonal_blocks(dcm, 0)
        dc_im_ref[...] = _diagonal_blocks(dcm, 1)
        steps, carry_in, _ = _scan_consts(a_ref, conj=True, reverse=True)
        first_row = lax.broadcasted_iota(jnp.int32, (SCAN_ROWS, CHUNK_STATES), 0) == 0
        n_tiles = SEQ // SCAN_ROWS

        def tile(j, carry):
            k = n_tiles - 1 - j
            rows = pl.ds(pl.multiple_of(k * SCAN_ROWS, SCAN_ROWS), SCAN_ROWS)
            before = pl.ds(pl.multiple_of(jnp.maximum(k - 1, 0) * SCAN_ROWS, SCAN_ROWS), SCAN_ROWS)
            gr, gi = _scan_tile(g_ref[rows, :CHUNK_STATES], g_ref[rows, CHUNK_STATES:], carry[0], carry[1], steps, carry_in, True)
            g_ref[rows, :CHUNK_STATES] = gr
            g_ref[rows, CHUNK_STATES:] = gi
            has_before = jnp.where(k > 0, 1.0, 0.0)
            hr = jnp.where(first_row, pltpu.roll(h_ref[before, :CHUNK_STATES], 1, axis=0) * has_before,
                           pltpu.roll(h_ref[rows, :CHUNK_STATES], 1, axis=0))
            hi = jnp.where(first_row, pltpu.roll(h_ref[before, CHUNK_STATES:], 1, axis=0) * has_before,
                           pltpu.roll(h_ref[rows, CHUNK_STATES:], 1, axis=0))
            return gr, gi, carry[2] + hr * gr + hi * gi, carry[3] + hr * gi - hi * gr

        zero = jnp.zeros((SCAN_ROWS, CHUNK_STATES), F32)
        _, _, sar, sai = lax.fori_loop(0, n_tiles, tile, (zero, zero, zero, zero), unroll=4)
        da_ref[:, :CHUNK_STATES] = jnp.sum(sar, axis=0, keepdims=True)
        da_ref[:, CHUNK_STATES:] = jnp.sum(sai, axis=0, keepdims=True)
        dbm = jnp.zeros((128, 2 * CHUNK_STATES), F32)
        for i in range(SEQ // MM_ROWS):
            rows = pl.ds(i * MM_ROWS, MM_ROWS)
            g = g_ref[rows, :].astype(BF16)
            du_ref[rows, :] = (_dot_nt(g, b_ref[...]) + d_ref[...] * dy_ref[rows, :]).astype(BF16)
            dbm += _dot_tn(u_ref[rows, :].astype(BF16), g)
        db_re_ref[...] = _diagonal_blocks(dbm, 0)
        db_im_ref[...] = _diagonal_blocks(dbm, 1)

    chunk_col = pl.BlockSpec((SEQ, 128), lambda c: (0, c))
    blocks = pl.BlockSpec((None, 128, SSM_STATE), lambda c: (c, 0, 0))
    return _call(
        body, "ssm_bwd", (SSM_CHUNKS,),
        [chunk_col,
         pl.BlockSpec((SEQ, 128), lambda c: (0, U_COL + c)),
         pl.BlockSpec((SEQ, 2 * CHUNK_STATES), lambda c: (0, c)),
         pl.BlockSpec((None, 128, 2 * CHUNK_STATES), lambda c: (c, 0, 0)),
         pl.BlockSpec((None, 2 * CHUNK_STATES, 128), lambda c: (c, 0, 0)),
         pl.BlockSpec((None, 1, 2 * CHUNK_STATES), lambda c: (c, 0, 0)),
         pl.BlockSpec((1, 128), lambda c: (0, c)), ANY],
        [pl.BlockSpec((SEQ, 128), lambda c: (0, U_COL + c)), blocks, blocks, blocks, blocks,
         pl.BlockSpec((None, 1, 2 * CHUNK_STATES), lambda c: (c, 0, 0)),
         pl.BlockSpec((1, 128), lambda c: (0, c))],
        [_sds((SEQ, IN_WIDTH), BF16)] + [_sds((SSM_CHUNKS, 128, SSM_STATE), F32)] * 4
        + [_sds((SSM_CHUNKS, 1, 2 * CHUNK_STATES), F32), _sds((1, SSM_WIDTH), F32)],
        [pltpu.VMEM((SEQ, 2 * CHUNK_STATES), F32)] * 2, [dys, proj, h, bmat, cmat, a_chunks, d_skip, dproj], ride, aliases={7: 0})


def _ssm_tables(abar_re, abar_im, bbar_re_t, bbar_im_t, c_re, c_im):
    bmat = _block_diag(bbar_re_t, bbar_im_t, 1.0, True).astype(BF16)
    cmat = _block_diag(c_re, c_im, -1.0, False).astype(BF16)
    a_chunks = jnp.concatenate([abar_re.reshape(SSM_CHUNKS, 1, CHUNK_STATES), abar_im.reshape(SSM_CHUNKS, 1, CHUNK_STATES)], axis=2)
    return bmat, cmat, a_chunks


GL_COL = (3 * QKV_WIDTH + SSM_WIDTH) // D_MODEL
GELU_C = math.sqrt(2.0 / math.pi)
GELU_A = 0.044715


def _sds(shape, dtype):
    return jax.ShapeDtypeStruct(shape, dtype)


def _gelu(x):
    t = jnp.tanh(GELU_C * (x + GELU_A * x * x * x))
    return 0.5 * x * (1.0 + t), t


def _gelu_grad(x, t):
    return 0.5 * (1.0 + t) + 0.5 * x * (1.0 - t * t) * GELU_C * (1.0 + 3.0 * GELU_A * x * x)


def _layer_norm(r, g, b):
    mu = jnp.mean(r, axis=-1, keepdims=True)
    xc = r - mu
    rstd = lax.rsqrt(jnp.mean(xc * xc, axis=-1, keepdims=True) + LN_EPS)
    xhat = xc * rstd
    return xhat * g + b, xhat, rstd


def _layer_norm_bwd(dy, xhat, rstd, g):
    dxhat = dy * g
    m1 = jnp.mean(dxhat, axis=-1, keepdims=True)
    m2 = jnp.mean(dxhat * xhat, axis=-1, keepdims=True)
    return rstd * (dxhat - m1 - xhat * m2)


def _proj(x, w_in, ride=None):
    tm, tn = 1024, 1792

    def body(x_ref, w_ref, o_ref):
        o_ref[...] = _dot(x_ref[...].astype(BF16), _side_by_side(w_ref))

    return _call(
        body, "proj", (SEQ // tm, IN_WIDTH // tn),
        [pl.BlockSpec((tm, D_MODEL), lambda i, j: (i, 0)), pl.BlockSpec((2, D_MODEL, tn // 2), lambda i, j: (j, 0, 0))],
        [pl.BlockSpec((tm, tn), lambda i, j: (i, j))], [_sds((SEQ, IN_WIDTH), F32)], [], [x, w_in], ride)


def _row_spec(tm, width, col=0):
    return pl.BlockSpec((tm, width), lambda i, col=col: (i, col))


def _full_spec(shape):
    return pl.BlockSpec(shape, lambda i: (0,) * len(shape))


def _weight_spec(shape):
    return pl.BlockSpec(shape, lambda i: (0,) * len(shape), pipeline_mode=pl.Buffered(1))


def _mixer_out(attn, ys, proj, x, w_ab, w_sb, w_glu, w_out, b_gate, ln_g, ln_b, ride=None):
    tm = 512

    def body(attn_ref, ys_ref, gl0_ref, gl1_ref, x_ref, wab_ref, wsb_ref, wglu_ref, wout_ref, bg_ref, g_ref, b_ref,
             h_ref, xhat_ref, rstd_ref, glu_ref, ya_ref, yssm_ref):
        gy, _ = _gelu(ys_ref[...])
        glu = _dot(gy.astype(BF16), _side_by_side(wglu_ref))
        glu_ref[...] = glu.astype(BF16)
        y_s = glu[:, :SSM_WIDTH] * jax.nn.sigmoid(glu[:, SSM_WIDTH:])
        y_ssm = _dot(y_s.astype(BF16), _side_by_side(wsb_ref))
        y_attn = _dot(attn_ref[...].astype(BF16), _side_by_side(wab_ref))
        ya_ref[...] = y_attn.astype(BF16)
        yssm_ref[...] = y_ssm.astype(BF16)
        g0 = jax.nn.sigmoid(gl0_ref[...] + _side_by_side(bg_ref, 0))
        g1 = jax.nn.sigmoid(gl1_ref[...] + _side_by_side(bg_ref, 1))
        mixed = g0 * y_attn + g1 * y_ssm
        r1 = DN_ALPHA * x_ref[...] + _dot(mixed.astype(BF16), wout_ref[...])
        h, xhat, rstd = _layer_norm(r1, g_ref[...], b_ref[...])
        h_ref[...] = h
        xhat_ref[...] = xhat
        rstd_ref[...] = jnp.broadcast_to(rstd, (tm, 128))

    wide = _sds((SEQ, D_MODEL), F32)
    return _call(
        body, "mixer_out", (SEQ // tm,),
        [_row_spec(tm, ATTN_WIDTH), _row_spec(tm, SSM_WIDTH), _row_spec(tm, D_MODEL, GL_COL), _row_spec(tm, D_MODEL, GL_COL + 1),
         _row_spec(tm, D_MODEL), _weight_spec((N_DEV, ATTN_WIDTH, 128)), _weight_spec((N_DEV, SSM_WIDTH, 128)),
         _weight_spec((N_DEV, SSM_WIDTH, 128)), _weight_spec((D_MODEL, D_MODEL)), _full_spec((N_DEV, 2, 128)),
         _full_spec((1, D_MODEL)), _full_spec((1, D_MODEL))],
        [_row_spec(tm, D_MODEL), _row_spec(tm, D_MODEL), _row_spec(tm, 128), _row_spec(tm, D_MODEL),
         _row_spec(tm, D_MODEL), _row_spec(tm, D_MODEL)],
        [wide, wide, _sds((SEQ, 128), F32)] + [_sds((SEQ, D_MODEL), BF16)] * 3, [],
        [attn, ys, proj, proj, x, w_ab, w_sb, w_glu, w_out, b_gate, ln_g, ln_b], ride)


def _ff_up(h, w_gate, w_up, ride=None):
    tm, tn = 1024, 768

    def body(h_ref, wg_ref, wu_ref, a_ref, b_ref, f_ref):
        hb = h_ref[...].astype(BF16)
        a, b = _dot(hb, _side_by_side(wg_ref)), _dot(hb, _side_by_side(wu_ref))
        a_ref[...] = a.astype(BF16)
        b_ref[...] = b.astype(BF16)
        f_ref[...] = (a * jax.nn.sigmoid(a) * b).astype(BF16)

    tile = pl.BlockSpec((tm, tn), lambda i, j: (i, j))
    wtile = pl.BlockSpec((tn // FF_PAD, D_MODEL, FF_PAD), lambda i, j: (j, 0, 0))
    out = _sds((SEQ, D_FF_PAD), BF16)
    return _call(body, "ff_up", (SEQ // tm, D_FF_PAD // tn), [pl.BlockSpec((tm, D_MODEL), lambda i, j: (i, 0)), wtile, wtile],
                 [tile, tile, tile], [out, out, out], [], [h, w_gate, w_up], ride)


def _ff_down_loss(f, w_down, h, target, ln_g, ln_b):
    tm = 512

    def body(f_ref, w_ref, h_ref, t_ref, g_ref, b_ref, dr_ref, dg_ref, db_ref, loss_ref):
        @pl.when(pl.program_id(0) == 0)
        def _():
            dg_ref[...] = jnp.zeros_like(dg_ref)
            db_ref[...] = jnp.zeros_like(db_ref)
            loss_ref[...] = jnp.zeros_like(loss_ref)

        r2 = DN_ALPHA * h_ref[...] + _dot(f_ref[...], w_ref[...])
        g = g_ref[...]
        out, xhat, rstd = _layer_norm(r2, g, b_ref[...])
        err = out - t_ref[...]
        loss_ref[...] += 0.5 * jnp.sum(jnp.mean(err * err, axis=-1, keepdims=True), axis=0, keepdims=True)
        dout = err * (1.0 / D_MODEL)
        dg_ref[...] += jnp.sum(dout * xhat, axis=0, keepdims=True)
        db_ref[...] += jnp.sum(dout, axis=0, keepdims=True)
        dr_ref[...] = _layer_norm_bwd(dout, xhat, rstd, g)

    vec = _sds((1, D_MODEL), F32)
    return _pallas_call(
        body, name="ff_down_loss", grid=(SEQ // tm,),
        in_specs=[_row_spec(tm, D_FF_PAD), _weight_spec((D_FF_PAD, D_MODEL)), _row_spec(tm, D_MODEL), _row_spec(tm, D_MODEL),
                  _full_spec((1, D_MODEL)), _full_spec((1, D_MODEL))],
        out_specs=(_row_spec(tm, D_MODEL), _full_spec((1, D_MODEL)), _full_spec((1, D_MODEL)), _full_spec((1, 128))),
        out_shape=(_sds((SEQ, D_MODEL), F32), vec, vec, _sds((1, 128), F32)),
        compiler_params=_cparams(dimension_semantics=("arbitrary",)),
    )(f, w_down, h, target, ln_g, ln_b)


def _ff_down_bwd(dr2, w_down, a, b):
    tm, tn = 1024, 768

    def body(dr_ref, w_ref, a_ref, b_ref, da_ref, db_ref):
        df = _dot_nt(dr_ref[...].astype(BF16), w_ref[...])
        av, bv = a_ref[...].astype(F32), b_ref[...].astype(F32)
        sg = jax.nn.sigmoid(av)
        da_ref[...] = (df * bv * sg * (1.0 + av * (1.0 - sg))).astype(BF16)
        db_ref[...] = (df * av * sg).astype(BF16)

    tile = pl.BlockSpec((tm, tn), lambda i, j: (i, j))
    out = _sds((SEQ, D_FF_PAD), BF16)
    return _pallas_call(
        body, name="ff_down_bwd", grid=(SEQ // tm, D_FF_PAD // tn),
        in_specs=[pl.BlockSpec((tm, D_MODEL), lambda i, j: (i, 0)), pl.BlockSpec((tn, D_MODEL), lambda i, j: (j, 0)), tile, tile],
        out_specs=(tile, tile), out_shape=(out, out),
        compiler_params=_cparams(dimension_semantics=("arbitrary", "arbitrary")),
    )(dr2, w_down, a, b)


def _ff_up_bwd(da, db, w_gate, w_up, dr2, xhat1, rstd1, ln_g, ride=None):
    tm, tk = 1024, 768
    nk = D_FF_PAD // tk

    def body(da_ref, db_ref, wg_ref, wu_ref, dr2_ref, xhat_ref, rstd_ref, g_ref, dr1_ref, dg_ref, dbias_ref, acc):
        i, k = pl.program_id(0), pl.program_id(1)

        @pl.when(jnp.logical_and(i == 0, k == 0))
        def _():
            dg_ref[...] = jnp.zeros_like(dg_ref)
            dbias_ref[...] = jnp.zeros_like(dbias_ref)

        part = _dot_nt(da_ref[...], _side_by_side(wg_ref)) + _dot_nt(db_ref[...], _side_by_side(wu_ref))

        @pl.when(k == 0)
        def _():
            acc[...] = part

        @pl.when(k > 0)
        def _():
            acc[...] += part

        @pl.when(k == nk - 1)
        def _():
            dh = DN_ALPHA * dr2_ref[...] + acc[...]
            xhat = xhat_ref[...]
            dg_ref[...] += jnp.sum(dh * xhat, axis=0, keepdims=True)
            dbias_ref[...] += jnp.sum(dh, axis=0, keepdims=True)
            rstd = jnp.max(rstd_ref[...], axis=1, keepdims=True)
            dr1_ref[...] = _layer_norm_bwd(dh, xhat, rstd, g_ref[...])

    hid = pl.BlockSpec((tm, tk), lambda i, k: (i, k))
    wtile = pl.BlockSpec((tk // FF_PAD, D_MODEL, FF_PAD), lambda i, k: (k, 0, 0))
    row = pl.BlockSpec((tm, D_MODEL), lambda i, k: (i, 0))
    vec = pl.BlockSpec((1, D_MODEL), lambda i, k: (0, 0))
    return _call(
        body, "ff_up_bwd", (SEQ // tm, nk),
        [hid, hid, wtile, wtile, row, row, pl.BlockSpec((tm, 128), lambda i, k: (i, 0)), vec],
        [row, vec, vec], [_sds((SEQ, D_MODEL), F32), _sds((1, D_MODEL), F32), _sds((1, D_MODEL), F32)],
        [pltpu.VMEM((tm, D_MODEL), F32)], [da, db, w_gate, w_up, dr2, xhat1, rstd1, ln_g], ride)


def _mixer_bwd(dr1, proj, y_attn, y_ssm, glu, ys, w_ab, w_sb, w_glu, w_out, b_gate):
    tm = 256

    def body(dr1_ref, gl0_ref, gl1_ref, ya_ref, yssm_ref, glu_ref, ys_ref, wab_ref, wsb_ref, wglu_ref, wout_ref, bg_ref,
             dya_ref, dyssm_ref, dgl_ref, dattn_ref, dglu_ref, dys_ref, mixed_ref, ysb_ref, gy_ref, dbg_ref, stage, copied):
        @pl.when(pl.program_id(0) == 0)
        def _():
            dbg_ref[...] = jnp.zeros_like(dbg_ref)

        dmixed = _dot_nt(dr1_ref[...].astype(BF16), wout_ref[...])
        g0 = jax.nn.sigmoid(gl0_ref[...] + _side_by_side(bg_ref, 0))
        g1 = jax.nn.sigmoid(gl1_ref[...] + _side_by_side(bg_ref, 1))
        y_attn, y_ssm = ya_ref[...].astype(F32), yssm_ref[...].astype(F32)
        mixed_ref[...] = (g0 * y_attn + g1 * y_ssm).astype(BF16)
        dya = (dmixed * g0).astype(BF16)
        dyssm = (dmixed * g1).astype(BF16)
        dya_ref[...] = dya
        dyssm_ref[...] = dyssm
        dgl0 = dmixed * y_attn * g0 * (1.0 - g0)
        dgl1 = dmixed * y_ssm * g1 * (1.0 - g1)
        i, last = pl.program_id(0), SEQ // tm - 1
        slot = i & 1

        def copy_out(buffer, tile):
            window = dgl_ref.at[pl.ds(pl.multiple_of(tile * tm, tm), tm), pl.ds(GL_COL * D_MODEL, 2 * D_MODEL)]
            return pltpu.make_async_copy(stage.at[buffer], window, copied.at[buffer])

        @pl.when(i >= 2)
        def _():
            copy_out(slot, i - 2).wait()

        stage[slot, :, :D_MODEL] = dgl0.astype(BF16)
        stage[slot, :, D_MODEL:] = dgl1.astype(BF16)
        copy_out(slot, i).start()

        @pl.when(i == last)
        def _():
            copy_out(1 - slot, i - 1).wait()
            copy_out(slot, i).wait()
        dbg_ref[:, :D_MODEL] += jnp.sum(dgl0, axis=0, keepdims=True)
        dbg_ref[:, D_MODEL:] += jnp.sum(dgl1, axis=0, keepdims=True)
        dattn_ref[...] = _dot_nt(dya, _side_by_side(wab_ref))
        dy_s = _dot_nt(dyssm, _side_by_side(wsb_ref))
        glu = glu_ref[...].astype(F32)
        glu1, sg = glu[:, :SSM_WIDTH], jax.nn.sigmoid(glu[:, SSM_WIDTH:])
        ysb_ref[...] = (glu1 * sg).astype(BF16)
        dglu1 = (dy_s * sg).astype(BF16)
        dglu2 = (dy_s * glu1 * sg * (1.0 - sg)).astype(BF16)
        dglu_ref[:, :SSM_WIDTH] = dglu1
        dglu_ref[:, SSM_WIDTH:] = dglu2
        dgy = _dot_nt(jnp.concatenate([dglu1, dglu2], axis=1), _side_by_side(wglu_ref))
        ys = ys_ref[...]
        gy, t = _gelu(ys)
        gy_ref[...] = gy.astype(BF16)
        dys_ref[...] = dgy * _gelu_grad(ys, t)

    wide_b, half_b = _sds((SEQ, D_MODEL), BF16), _sds((SEQ, SSM_WIDTH), BF16)
    half_f = _sds((SEQ, SSM_WIDTH), F32)
    return _pallas_call(
        body, name="mixer_bwd", grid=(SEQ // tm,),
        in_specs=[_row_spec(tm, D_MODEL), _row_spec(tm, D_MODEL, GL_COL), _row_spec(tm, D_MODEL, GL_COL + 1), _row_spec(tm, D_MODEL),
                  _row_spec(tm, D_MODEL), _row_spec(tm, D_MODEL), _row_spec(tm, SSM_WIDTH), _full_spec((N_DEV, ATTN_WIDTH, 128)),
                  _full_spec((N_DEV, SSM_WIDTH, 128)), _full_spec((N_DEV, SSM_WIDTH, 128)), _full_spec((D_MODEL, D_MODEL)),
                  _full_spec((N_DEV, 2, 128))],
        out_specs=(_row_spec(tm, D_MODEL), _row_spec(tm, D_MODEL), ANY, _row_spec(tm, ATTN_WIDTH),
                   _row_spec(tm, D_MODEL), _row_spec(tm, SSM_WIDTH), _row_spec(tm, D_MODEL), _row_spec(tm, SSM_WIDTH),
                   _row_spec(tm, SSM_WIDTH), _full_spec((1, 2 * D_MODEL))),
        out_shape=(wide_b, wide_b, _sds((SEQ, IN_WIDTH), BF16), half_f, wide_b, half_f, wide_b, half_b, half_b,
                   _sds((1, 2 * D_MODEL), F32)),
        scratch_shapes=[pltpu.VMEM((2, tm, 2 * D_MODEL), BF16), pltpu.SemaphoreType.DMA((2,))],
        compiler_params=_cparams(dimension_semantics=("arbitrary",)),
    )(dr1, proj, proj, y_attn, y_ssm, glu, ys, w_ab, w_sb, w_glu, w_out, b_gate)


def _grad_x(dproj, w_in, dr1, ride=None):
    tm, tk = 1024, 1792
    nk = IN_WIDTH // tk

    def body(dp_ref, w_ref, dr1_ref, o_ref, acc):
        k = pl.program_id(1)
        part = _dot_nt(dp_ref[...], _side_by_side(w_ref))

        @pl.when(k == 0)
        def _():
            acc[...] = part

        @pl.when(k > 0)
        def _():
            acc[...] += part

        @pl.when(k == nk - 1)
        def _():
            o_ref[...] = DN_ALPHA * dr1_ref[...] + acc[...]

    row = pl.BlockSpec((tm, D_MODEL), lambda i, k: (i, 0))
    return _call(
        body, "grad_x", (SEQ // tm, nk),
        [pl.BlockSpec((tm, tk), lambda i, k: (i, k)), pl.BlockSpec((2, D_MODEL, tk // 2), lambda i, k: (k, 0, 0)), row],
        [row], [_sds((SEQ, D_MODEL), F32)], [pltpu.VMEM((tm, D_MODEL), F32)], [dproj, w_in, dr1], ride)


def _weight_grad(a, b, name, shard_cols=None):
    k, n = a.shape[1], b.shape[1]
    tk = k if shard_cols else k // N_DEV
    tn = n // 4 if shard_cols else min(n, 1024)

    def body(a_ref, b_ref, o_ref):
        grad = _dot_tn(a_ref[...].astype(BF16), b_ref[...].astype(BF16))
        if shard_cols:
            o_ref[0] = grad[:, :shard_cols].astype(BF16)
            o_ref[1] = grad[:, shard_cols:].astype(BF16)
        else:
            o_ref[...] = grad.astype(BF16)

    if shard_cols:
        out_spec = pl.BlockSpec((2, None, tk, shard_cols), lambda kk, j: (0, j, kk, 0))
        out_shape = _sds((2, 4, k, shard_cols), BF16)
    else:
        out_spec = pl.BlockSpec((None, None, tk, tn), lambda kk, j: (kk % 2, kk // 2, 0, j))
        out_shape = _sds((2, 4, tk, n), BF16)
    return _call(body, name, (k // tk, n // tn),
                 [pl.BlockSpec((SEQ, tk), lambda kk, j: (0, kk)), pl.BlockSpec((SEQ, tn), lambda kk, j: (0, j))],
                 [out_spec], [out_shape], [], [a, b])[0]


def _weight_grad_rows(a, b, core, name, shard_cols, first, count, ride, in_place=False):
    k = a.shape[1]

    def body(a_ref, b_ref, o_ref):
        o_ref[...] = _dot_tn(a_ref[...].astype(BF16), b_ref[...].astype(BF16)).astype(BF16)

    def shard(j, core_ref):
        row = j + first
        return 0, jnp.where(row < 4, 2 * row + 1 - core_ref[0], 2 * (row - 4) + core_ref[0])

    return _call(body, name, (count,),
                 [pl.BlockSpec((SEQ, k), lambda j, core_ref: (0, 0), pipeline_mode=pl.Buffered(1)),
                  pl.BlockSpec((SEQ, shard_cols), shard)],
                 [pl.BlockSpec((None, k, shard_cols), lambda j, core_ref: (j + first, 0, 0))],
                 [_sds((N_DEV, k, shard_cols), BF16)], [], [a, b], ride, aliases={2: 0} if in_place else None, prefetch=core)


MESH = pl.DeviceIdType.MESH
ANY = pl.BlockSpec(memory_space=pl.ANY)


def _place():
    return lax.axis_index("x"), lax.axis_index("y"), lax.axis_index("c")


def _other_chips(x, y):
    return [(1 - x, y), (x, 1 - y), (1 - x, 1 - y)]


class _Ride:
    def __init__(self, operands, results, aliases, sems, start, wait):
        self.operands, self.results, self.aliases, self.sems = list(operands), list(results), dict(aliases), list(sems)
        self.start, self.wait = start, wait

    def __add__(self, other):
        n_in, n_out, n_sem = len(self.operands), len(self.results), len(self.sems)

        def both(which):
            def run(ins, outs, sems):
                getattr(self, which)(ins[:n_in], outs[:n_out], sems[:n_sem])
                getattr(other, which)(ins[n_in:], outs[n_out:], sems[n_sem:])
            return run

        aliases = {**self.aliases, **{n_in + i: n_out + j for i, j in other.aliases.items()}}
        return _Ride(self.operands + other.operands, self.results + other.results, aliases, self.sems + other.sems,
                     both("start"), both("wait"))


def _call(body, name, grid, in_specs, out_specs, out_shape, scratch_shapes, operands, ride=None, aliases=None, prefetch=None):
    in_specs, out_specs, out_shape = list(in_specs), list(out_specs), list(out_shape)
    scratch_shapes, operands, aliases = list(scratch_shapes), list(operands), dict(aliases or {})
    kernel_body = body
    if ride is not None:
        n_in, n_out, n_scr, r_in, r_out = len(in_specs), len(out_specs), len(scratch_shapes), len(ride.operands), len(ride.results)

        def kernel_body(*refs):
            out0, scr0 = n_in + r_in, n_in + r_in + n_out + r_out
            ride_refs = (refs[n_in:out0], refs[out0 + n_out:scr0], refs[scr0 + n_scr:])
            ids = [pl.program_id(i) for i in range(len(grid))]
            first = functools.reduce(jnp.logical_and, [i == 0 for i in ids])
            last = functools.reduce(jnp.logical_and, [i == g - 1 for i, g in zip(ids, grid)])

            @pl.when(first)
            def _():
                ride.start(*ride_refs)

            body(*refs[:n_in], *refs[out0:out0 + n_out], *refs[scr0:scr0 + n_scr])

            @pl.when(last)
            def _():
                ride.wait(*ride_refs)

        aliases.update({n_in + i: n_out + j for i, j in ride.aliases.items()})
        in_specs += [ANY] * r_in
        out_specs += [ANY] * r_out
        out_shape += ride.results
        scratch_shapes += ride.sems
        operands += ride.operands
    params = _cparams(dimension_semantics=("arbitrary",) * len(grid))
    if prefetch is None:
        return _pallas_call(
            kernel_body, name=name, grid=grid, in_specs=in_specs, out_specs=out_specs, out_shape=out_shape,
            scratch_shapes=scratch_shapes, input_output_aliases=aliases, compiler_params=params,
        )(*operands)

    def with_prefetch(prefetch_ref, *refs):
        kernel_body(*refs)

    return _pallas_call(
        with_prefetch, name=name,
        grid_spec=pltpu.PrefetchScalarGridSpec(num_scalar_prefetch=1, grid=grid, in_specs=in_specs, out_specs=out_specs,
                                               scratch_shapes=scratch_shapes),
        out_shape=out_shape, input_output_aliases={i + 1: j for i, j in aliases.items()}, compiler_params=params,
    )(prefetch, *operands)


def _after(*arrays):
    return _Ride(arrays, [], {}, [], lambda *refs: None, lambda *refs: None)


def _gather_first_level(shards):
    n = len(shards)

    def copies(ins, outs, sems, landed):
        send_sems, recv_sems, local_sems = sems
        x, y, c = _place()
        peers = [(x, y, 1 - c)] + [(px, py, c) for px, py in _other_chips(x, y)]

        def row(peer):
            return 4 * x + 2 * y + c if not landed else 4 * peer[0] + 2 * peer[1] + peer[2]

        local = [pltpu.make_async_copy(ins[a], outs[a].at[4 * x + 2 * y + c], local_sems.at[a]) for a in range(n)]
        remote = [pltpu.make_async_remote_copy(
            src_ref=ins[a], dst_ref=outs[a].at[row(peer)], send_sem=send_sems.at[a, k], recv_sem=recv_sems.at[a, k],
            device_id=peer, device_id_type=MESH) for a in range(n) for k, peer in enumerate(peers)]
        return local, remote

    def start(ins, outs, sems):
        local, remote = copies(ins, outs, sems, False)
        for cp in local + remote:
            cp.start()

    def wait(ins, outs, sems):
        local, sent = copies(ins, outs, sems, False)
        for cp in copies(ins, outs, sems, True)[1]:
            cp.wait_recv()
        for cp in sent:
            cp.wait_send()
        for cp in local:
            cp.wait()

    return _Ride(shards, [_sds((N_DEV,) + s.shape, s.dtype) for s in shards], {},
                 [pltpu.SemaphoreType.DMA((n, 4)), pltpu.SemaphoreType.DMA((n, 4)), pltpu.SemaphoreType.DMA((n,))], start, wait)


def _gather_second_level(buffers):
    n = len(buffers)

    def copies(outs, sems, core):
        send_sems, recv_sems = sems
        x, y, c = _place()
        return [pltpu.make_async_remote_copy(
            src_ref=outs[a].at[4 * px + 2 * py + core], dst_ref=outs[a].at[4 * px + 2 * py + core], send_sem=send_sems.at[a, j],
            recv_sem=recv_sems.at[a, j], device_id=(x, y, 1 - c), device_id_type=MESH)
            for a in range(n) for j, (px, py) in enumerate(_other_chips(x, y))]

    def start(ins, outs, sems):
        for cp in copies(outs, sems, lax.axis_index("c")):
            cp.start()

    def wait(ins, outs, sems):
        for cp in copies(outs, sems, 1 - lax.axis_index("c")):
            cp.wait_recv()
        for cp in copies(outs, sems, lax.axis_index("c")):
            cp.wait_send()

    return _Ride(buffers, [_sds(b.shape, b.dtype) for b in buffers], {i: i for i in range(n)},
                 [pltpu.SemaphoreType.DMA((n, 3)), pltpu.SemaphoreType.DMA((n, 3))], start, wait)


def _relayed_gather(shards):
    n = len(shards)
    buffers = [_sds((N_DEV,) + s.shape, s.dtype) for s in shards]
    dma = pltpu.SemaphoreType.DMA

    def remote(src, dst, send_sem, recv_sem, to):
        return pltpu.make_async_remote_copy(src_ref=src, dst_ref=dst, send_sem=send_sem, recv_sem=recv_sem,
                                            device_id=to, device_id_type=MESH)

    def row(px, py, pc):
        return 4 * px + 2 * py + pc

    def ride(operands, aliases, sems, copies):
        def start(ins, outs, sem_refs):
            local, sent = copies(ins, outs, sem_refs, False)
            for cp in local + sent:
                cp.start()

        def wait(ins, outs, sem_refs):
            local, sent = copies(ins, outs, sem_refs, False)
            for cp in copies(ins, outs, sem_refs, True)[1]:
                cp.wait_recv()
            for cp in sent:
                cp.wait_send()
            for cp in local:
                cp.wait()

        return _Ride(operands, buffers, aliases, sems, start, wait)

    def first(ins, outs, sems, landed):
        x, y, c = _place()
        peers = [(x, y, 1 - c), (1 - x, y, c), (x, 1 - y, c)]
        local = [pltpu.make_async_copy(ins[a], outs[a].at[row(x, y, c)], sems[2].at[a]) for a in range(n)]
        return local, [remote(ins[a], outs[a].at[row(*peer) if landed else row(x, y, c)], sems[0].at[a, k], sems[1].at[a, k], peer)
                       for a in range(n) for k, peer in enumerate(peers)]

    def second(ins, outs, sems, landed):
        x, y, c = _place()
        mine = 1 - c if landed else c
        copies = []
        for a in range(n):
            half = shards[a].shape[0] // 2
            over_x, over_y, diagonal = outs[a].at[row(1 - x, y, mine)], outs[a].at[row(x, 1 - y, mine)], outs[a].at[row(1 - x, 1 - y, c)]
            lower, upper = pl.ds(0, half), pl.ds(half, half)
            copies += [remote(over_x, over_x, sems[0].at[a, 0], sems[1].at[a, 0], (x, y, 1 - c)),
                       remote(over_y, over_y, sems[0].at[a, 1], sems[1].at[a, 1], (x, y, 1 - c))]
            if landed:
                copies += [remote(diagonal.at[lower], diagonal.at[lower], sems[0].at[a, 2], sems[1].at[a, 2], (1 - x, y, c)),
                           remote(diagonal.at[upper], diagonal.at[upper], sems[0].at[a, 3], sems[1].at[a, 3], (x, 1 - y, c))]
            else:
                copies += [remote(over_y.at[lower], over_y.at[lower], sems[0].at[a, 2], sems[1].at[a, 2], (1 - x, y, c)),
                           remote(over_x.at[upper], over_x.at[upper], sems[0].at[a, 3], sems[1].at[a, 3], (x, 1 - y, c))]
        return [], copies

    def third(ins, outs, sems, landed):
        x, y, c = _place()
        return [], [remote(outs[a].at[row(1 - x, 1 - y, 1 - c if landed else c)], outs[a].at[row(1 - x, 1 - y, 1 - c if landed else c)],
                           sems[0].at[a], sems[1].at[a], (x, y, 1 - c)) for a in range(n)]

    def later(copies, n_sems):
        return lambda partly: ride(partly, {i: i for i in range(n)}, [dma((n,) + n_sems), dma((n,) + n_sems)], copies)

    return ride(shards, {}, [dma((n, 3)), dma((n, 3)), dma((n,))], first), later(second, (4,)), later(third, ())


def _sibling_swap_ride(grads, halves=True):
    n = len(grads)

    def copies(ins, outs, sems):
        x, y, c = _place()
        return [pltpu.make_async_remote_copy(
            src_ref=ins[a].at[1 - c] if halves else ins[a].at[pl.ds(0, 4)], dst_ref=outs[a], send_sem=sems[0].at[a],
            recv_sem=sems[1].at[a], device_id=(x, y, 1 - c), device_id_type=MESH) for a in range(n)]

    def start(ins, outs, sems):
        for cp in copies(ins, outs, sems):
            cp.start()

    def wait(ins, outs, sems):
        for cp in copies(ins, outs, sems):
            cp.wait()

    return _Ride(grads, [_sds((4,) + g.shape[-2:], g.dtype) for g in grads], {},
                 [pltpu.SemaphoreType.DMA((n,)), pltpu.SemaphoreType.DMA((n,))], start, wait)


def _chip_swap_ride(sums):
    n = len(sums)

    def copies(ins, outs, sems, landed):
        send_sems, recv_sems, local_sems = sems
        x, y, c = _place()
        mine = 2 * x + y
        local = [pltpu.make_async_copy(ins[a].at[mine], outs[a].at[mine], local_sems.at[a]) for a in range(n)]
        remote = [pltpu.make_async_remote_copy(
            src_ref=ins[a].at[2 * px + py], dst_ref=outs[a].at[2 * px + py if landed else mine], send_sem=send_sems.at[a, j],
            recv_sem=recv_sems.at[a, j], device_id=(px, py, c), device_id_type=MESH)
            for a in range(n) for j, (px, py) in enumerate(_other_chips(x, y))]
        return local, remote

    def start(ins, outs, sems):
        local, remote = copies(ins, outs, sems, False)
        for cp in local + remote:
            cp.start()

    def wait(ins, outs, sems):
        local, sent = copies(ins, outs, sems, False)
        for cp in copies(ins, outs, sems, True)[1]:
            cp.wait_recv()
        for cp in sent:
            cp.wait_send()
        for cp in local:
            cp.wait()

    return _Ride(sums, [_sds(s.shape, s.dtype) for s in sums], {},
                 [pltpu.SemaphoreType.DMA((n, 3)), pltpu.SemaphoreType.DMA((n, 3)), pltpu.SemaphoreType.DMA((n,))], start, wait)


def _send_buffers(shards, name):
    n = len(shards)

    def body(*refs):
        for (w, transposed, rows, cols), w_ref, o_ref in zip(shards, refs[:n], refs[n:]):
            if transposed:
                c, r = w.shape
                padded = jnp.concatenate([w_ref[...], jnp.zeros((cols - c, r), F32)], axis=0) if cols > c else w_ref[...]
                o_ref[...] = padded.T.astype(BF16)
            else:
                r, c = w.shape
                if (r, c) != (rows, cols):
                    o_ref[...] = jnp.zeros((rows, cols), BF16)
                o_ref[:r, :c] = w_ref[...].astype(BF16)

    return _pallas_call(body, name=name, out_shape=[_sds((rows, cols), BF16) for _, _, rows, cols in shards])(
        *[w for w, _, _, _ in shards])


def _all_gather(shards, name):
    n = len(shards)
    sent = [_sds(s.shape, BF16) for s in shards]
    first, second, third = _relayed_gather(sent)
    levels = [first, second(sent), third(sent)]
    counts = [len(level.sems) for level in levels]

    def body(*refs):
        ins, outs, sems, wide, staged = refs[:n], refs[n:2 * n], refs[2 * n:-2 * n], refs[-2 * n:-n], refs[-n:]
        for a in range(n):
            pltpu.sync_copy(ins[a], wide[a])
            staged[a][...] = wide[a][...].astype(BF16)
        for i, level in enumerate(levels):
            mine = sems[sum(counts[:i]):sum(counts[:i + 1])]
            level.start(staged, outs, mine)
            level.wait(staged, outs, mine)

    return _pallas_call(
        body, name=name, in_specs=[ANY] * n, out_specs=[ANY] * n, out_shape=first.results,
        scratch_shapes=[s for level in levels for s in level.sems] + [pltpu.VMEM(s.shape, F32) for s in shards]
        + [pltpu.VMEM(s.shape, BF16) for s in shards],
    )(*shards)


HBM = pl.BlockSpec(memory_space=pltpu.HBM)
SEMAPHORES = pl.BlockSpec(memory_space=pltpu.SEMAPHORE)
IN_FLIGHT = pltpu.CompilerParams(has_side_effects=pltpu.SideEffectType.DATAFLOW_SIDE_EFFECTING)


def _chip_swap_copies(src_refs, land_refs, send_sems, recv_sems, landed):
    x, y, c = _place()
    return [pltpu.make_async_remote_copy(
        src_ref=src.at[2 * px + py], dst_ref=land.at[2 * px + py if landed else 2 * x + y], send_sem=send_sems.at[3 * a + j],
        recv_sem=recv_sems.at[3 * a + j], device_id=(px, py, c), device_id_type=MESH)
        for a, (src, land) in enumerate(zip(src_refs, land_refs)) for j, (px, py) in enumerate(_other_chips(x, y))]


def _chip_swap_start(sums, name):
    n = len(sums)

    def body(*refs):
        src_refs, land_refs, (send_sems, recv_sems), token = refs[:n], refs[n:2 * n], refs[2 * n:2 * n + 2], refs[-1]
        for cp in _chip_swap_copies(src_refs, land_refs, send_sems, recv_sems, False):
            cp.start()
        token[...] = jnp.zeros_like(token)

    kept = [pltpu.HBM(s.shape, s.dtype) for s in sums]
    out = _pallas_call(
        body, name=name,
        out_shape=[pltpu.SemaphoreType.DMA((3 * n,)), pltpu.SemaphoreType.DMA((3 * n,))] + kept + kept + [_sds((8, 128), F32)],
        in_specs=[HBM] * (2 * n), out_specs=[SEMAPHORES, SEMAPHORES] + [HBM] * (2 * n) + [pl.BlockSpec(memory_space=pltpu.VMEM)],
        input_output_aliases={i: 2 + i for i in range(2 * n)}, compiler_params=IN_FLIGHT,
    )(*[pltpu.with_memory_space_constraint(s, pltpu.HBM) for s in sums],
      *[pltpu.with_memory_space_constraint(lax.empty(s.shape, s.dtype), pltpu.HBM) for s in sums])
    return out[0], out[1], out[2:2 + n], out[2 + n:2 + 2 * n], out[-1]


def _chip_swap_wait(send_sems, recv_sems, sums, landings, after, name):
    n = len(sums)

    def body(*refs):
        src_refs, land_refs, (send_sems, recv_sems) = refs[:n], refs[n:2 * n], refs[2 * n:2 * n + 2]
        for cp in _chip_swap_copies(src_refs, land_refs, send_sems, recv_sems, False):
            cp.wait_send()
        for cp in _chip_swap_copies(src_refs, land_refs, send_sems, recv_sems, True):
            cp.wait_recv()

    out = _pallas_call(
        body, name=name, out_shape=[pltpu.HBM(s.shape, s.dtype) for s in list(sums) + list(landings)],
        in_specs=[HBM] * (2 * n) + [SEMAPHORES, SEMAPHORES] + [ANY] * len(after), out_specs=[HBM] * (2 * n),
        input_output_aliases={i: i for i in range(2 * n)}, compiler_params=IN_FLIGHT,
    )(*sums, *landings, send_sems, recv_sems, *after)
    return out[:n], out[n:]


def _pair_sums(gs, rs, core, name):
    n_arrays = len(gs)

    def body(core_ref, *refs):
        for g_ref, r_ref, o_ref in zip(refs[:n_arrays], refs[n_arrays:2 * n_arrays], refs[2 * n_arrays:]):
            o_ref[...] = (g_ref[...].astype(F32) + r_ref[...].astype(F32)).astype(o_ref.dtype)

    def chip(g):
        return pl.BlockSpec((None,) + g.shape[-2:], lambda p, core_ref: (p, 0, 0))

    def own(g):
        if g.ndim == 3:
            return pl.BlockSpec((None,) + g.shape[-2:], lambda p, core_ref: (p + 4, 0, 0))
        return pl.BlockSpec((None, None) + g.shape[2:], lambda p, core_ref: (core_ref[0], p, 0, 0))

    return _pallas_call(
        body, name=name,
        grid_spec=pltpu.PrefetchScalarGridSpec(
            num_scalar_prefetch=1, grid=(4,), in_specs=[own(g) for g in gs] + [chip(g) for g in gs],
            out_specs=[chip(g) for g in gs]),
        out_shape=[_sds((4,) + g.shape[-2:], g.dtype) for g in gs], compiler_params=_cparams(dimension_semantics=("arbitrary",)),
    )(core, *gs, *rs)


def _adamw_math(w, g, m, v):
    m = ADAM_B1 * m + (1.0 - ADAM_B1) * g
    v = ADAM_B2 * v + (1.0 - ADAM_B2) * (g * g)
    m_hat = m / (1.0 - ADAM_B1 ** ADAM_STEP)
    v_hat = v / (1.0 - ADAM_B2 ** ADAM_STEP)
    return -ADAM_LR * (m_hat / (jnp.sqrt(v_hat) + ADAM_EPS) + ADAM_WD * w), m, v


def _adamw_many(weights, name, ride=None):
    steps = 4
    in_specs, out_specs, out_shape, operands, tiles = [], [], [], [], []
    for w, m, v, parts, own, transposed in weights:
        _, pr, pc = parts.shape
        if transposed:
            c, r = w.shape
            tile = pl.BlockSpec((c, r // steps), lambda i: (0, i))
            part_tile = pl.BlockSpec((4, r // steps, pc), lambda i: (0, i, 0))
            tiles.append((c, r // steps))
        elif w.shape[0] % (8 * steps) == 0:
            r, c = w.shape
            tile = pl.BlockSpec((r // steps, c), lambda i: (i, 0))
            part_tile = pl.BlockSpec((4, r // steps, pc), lambda i: (0, i, 0))
            tiles.append((r // steps, c))
        else:
            tile = pl.BlockSpec(w.shape, lambda i: (0, 0))
            part_tile = pl.BlockSpec(parts.shape, lambda i: (0, 0, 0))
            tiles.append(w.shape)
        in_specs += [tile, tile, tile] + [part_tile] * (1 if own is None else 2)
        out_specs += [tile] * 4
        out_shape += [_sds(w.shape, F32)] * 4
        operands += [w, m, v, parts] + ([] if own is None else [own])
    n_in = len(operands)

    def body(*refs):
        ins, outs = list(refs[:n_in]), refs[n_in:]
        this_chip = 2 * lax.axis_index("x") + lax.axis_index("y")
        for k, (_, _, _, _, own, transposed) in enumerate(weights):
            w_ref, m_ref, v_ref, p_ref = ins[:4]
            own_ref = None if own is None else ins[4]
            del ins[:4 if own is None else 5]
            rows, cols = tiles[k]
            g = None
            for q in range(4):
                index = (q,) if transposed else (q, slice(0, rows), slice(0, cols))
                part = p_ref[index] if own is None else jnp.where(this_chip == q, own_ref[index], p_ref[index])
                g = part.astype(F32) if g is None else g + part.astype(F32)
            if transposed:
                g = g.T[:rows]
            g_out, d_out, m_out, v_out = outs[4 * k:4 * k + 4]
            g_out[...] = g
            d_out[...], m_out[...], v_out[...] = _adamw_math(w_ref[...], g, m_ref[...], v_ref[...])

    return _call(body, name, (steps,), in_specs, out_specs, out_shape, [], operands, ride)


SMALL = ("ssm_a_re", "ssm_a_im", "ssm_log_dt", "ssm_b_re", "ssm_b_im", "ssm_c_re", "ssm_c_im", "ssm_d",
         "ln1_g", "ln1_b", "ln2_g", "ln2_b")


def _pack_rows(arrays):
    rows = []
    for a in arrays:
        flat = a.reshape(-1)
        rows.append(jnp.pad(flat, (0, -flat.shape[0] % 128)).reshape(-1, 128))
    packed = jnp.concatenate(rows, axis=0)
    return jnp.pad(packed, ((0, -packed.shape[0] % 8), (0, 0)))


def _unpack_rows(packed, shapes):
    out, row = [], 0
    for shape in shapes:
        size = math.prod(shape)
        n_rows = -(-size // 128)
        out.append(packed[row:row + n_rows].reshape(-1)[:size].reshape(shape))
        row += n_rows
    return out


def _sum_devices(parts):
    def body(p_ref, o_ref):
        total = p_ref[0]
        for dev in range(1, N_DEV):
            total = total + p_ref[dev]
        o_ref[...] = total

    return _pallas_call(body, name="sum_devices", out_shape=_sds(parts.shape[1:], F32))(parts)


def _adamw_replicated(ws, ms, vs, gs):
    n = len(ws)

    def body(*refs):
        w_refs, m_refs, v_refs, g_refs, d_out, m_out, v_out = (refs[i * n:(i + 1) * n] for i in range(7))
        for i in range(n):
            d_out[i][...], m_out[i][...], v_out[i][...] = _adamw_math(w_refs[i][...], g_refs[i][...], m_refs[i][...], v_refs[i][...])

    out = _pallas_call(body, name="adamw_replicated", out_shape=[_sds(w.shape, F32) for w in ws] * 3,
                       compiler_params=_cparams())(*ws, *ms, *vs, *gs)
    return out[:n], out[n:2 * n], out[2 * n:]


def kernel(x, w_in, b_gate, w_attn_br, w_ssm_br, w_out, ssm_a_re, ssm_a_im, ssm_log_dt, ssm_b_re, ssm_b_im, ssm_c_re, ssm_c_im, ssm_d, w_glu, ln1_g, ln1_b, w_ff_gate, w_ff_up, w_ff_down, ln2_g, ln2_b, loss_target, m_w_in, m_b_gate, m_w_attn_br, m_w_ssm_br, m_w_out, m_ssm_a_re, m_ssm_a_im, m_ssm_log_dt, m_ssm_b_re, m_ssm_b_im, m_ssm_c_re, m_ssm_c_im, m_ssm_d, m_w_glu, m_ln1_g, m_ln1_b, m_w_ff_gate, m_w_ff_up, m_w_ff_down, m_ln2_g, m_ln2_b, v_w_in, v_b_gate, v_w_attn_br, v_w_ssm_br, v_w_out, v_ssm_a_re, v_ssm_a_im, v_ssm_log_dt, v_ssm_b_re, v_ssm_b_im, v_ssm_c_re, v_ssm_c_im, v_ssm_d, v_w_glu, v_ln1_g, v_ln1_b, v_w_ff_gate, v_w_ff_up, v_w_ff_down, v_ln2_g, v_ln2_b):
    given = dict(locals())
    x2, target = x[0], loss_target[0]
    core = lax.axis_index("c").astype(jnp.int32).reshape(1)

    sharded = ("w_in", "w_attn_br", "w_ssm_br", "w_glu", "w_ff_gate", "w_ff_up", "b_gate", "w_out", "w_ff_down")
    send_shape = dict(w_in=(D_MODEL, 896), w_attn_br=(ATTN_WIDTH, 128), w_ssm_br=(SSM_WIDTH, 128), w_glu=(SSM_WIDTH, 128),
                      w_out=(128, D_MODEL), w_ff_gate=(D_MODEL, FF_PAD), w_ff_up=(D_MODEL, FF_PAD), w_ff_down=(FF_PAD, D_MODEL))
    local = {k: given[k][0] for k in sharded}
    narrow = ("w_ff_gate", "w_ff_up")
    def to_send(k):
        return (local[k].T, True, *send_shape[k]) if k in narrow else (local[k], False, *send_shape[k])

    later = [k for k in sharded if k not in ("w_in", "b_gate")]
    sends = dict(zip(later, _send_buffers([to_send(k) for k in later], "send_weights")))
    sends["b_gate"] = local["b_gate"]
    mixer_weights = ("w_attn_br", "w_ssm_br", "w_glu", "b_gate", "w_out")
    ff_weights = ("w_ff_gate", "w_ff_up", "w_ff_down")
    wt = {}
    wt["w_in"], = _all_gather([local["w_in"]], "gather_w_in")

    a_re, a_im, log_dt = ssm_a_re[0], ssm_a_im[0], ssm_log_dt[0].reshape(SSM_GROUPS, 1)
    b_re_t, b_im_t = ssm_b_re[0].transpose(0, 2, 1), ssm_b_im[0].transpose(0, 2, 1)
    abar_re, abar_im, e_re, e_im, bbar_re_t, bbar_im_t = _ssm_prep(a_re, a_im, log_dt, b_re_t, b_im_t)
    bmat, cmat, a_chunks = _ssm_tables(abar_re, abar_im, bbar_re_t, bbar_im_t, ssm_c_re[0], ssm_c_im[0])
    cos_t, sin_t = _rope_tables()

    big_mixer, ff_in = [k for k in mixer_weights if k != "b_gate"], ("w_ff_gate", "w_ff_up")
    n_mixer = len(big_mixer)
    mixer_1, mixer_2, mixer_3 = _relayed_gather([sends[k] for k in big_mixer])
    ff_in_1, ff_in_2, ff_in_3 = _relayed_gather([sends[k] for k in ff_in])
    ff_down_1, ff_down_2, ff_down_3 = _relayed_gather([sends["w_ff_down"]])
    proj, *landed = _proj(x2, wt["w_in"], mixer_1 + _gather_first_level([sends["b_gate"]]))
    mixer, bias = landed[:n_mixer], landed[n_mixer:]
    attn, lse, q_pm, k_pm, v_pm, *landed = _attn_fwd(proj, cos_t, sin_t,
                                                     mixer_2(mixer) + _gather_second_level(bias) + ff_in_1)
    mixer, b_gate_full, ff = landed[:n_mixer], landed[n_mixer], landed[n_mixer + 1:]
    ys, states, *landed = _ssm_fwd(proj, bmat, cmat, a_chunks, ssm_d, mixer_3(mixer) + ff_in_2(ff) + ff_down_1)
    wt.update(zip(big_mixer, landed[:n_mixer]))
    ff, ff_down = landed[n_mixer:n_mixer + 2], landed[n_mixer + 2:]
    wt["w_out"] = wt["w_out"].reshape(D_MODEL, D_MODEL)
    h, xhat1, rstd1, glu, y_attn, y_ssm, *landed = _mixer_out(
        attn, ys, proj, x2, wt["w_attn_br"], wt["w_ssm_br"], wt["w_glu"], wt["w_out"], b_gate_full, ln1_g, ln1_b,
        ff_in_3(ff) + ff_down_2(ff_down))
    wt.update(zip(ff_in, landed[:2]))
    ff_a, ff_b, ff_f, w_ff_down = _ff_up(h, wt["w_ff_gate"], wt["w_ff_up"], ff_down_3(landed[2:]))
    wt["w_ff_down"] = w_ff_down.reshape(D_FF_PAD, D_MODEL)
    dr2, d_ln2_g, d_ln2_b, loss_lanes = _ff_down_loss(ff_f, wt["w_ff_down"], h, target, ln2_g, ln2_b)

    def pair_sums(names, contrib, from_sibling):
        return _pair_sums([contrib[k] for k in names], from_sibling, core, "pair_sums_" + names[0])

    d_a, d_b = _ff_down_bwd(dr2, wt["w_ff_down"], ff_a, ff_b)
    contrib = dict(w_ff_gate=_weight_grad(h, d_a, "wgrad_w_ff_gate", FF_PAD),
                   w_ff_up=_weight_grad(h, d_b, "wgrad_w_ff_up", FF_PAD),
                   w_ff_down=_weight_grad(ff_f, dr2, "wgrad_w_ff_down"))
    dr1, d_ln1_g, d_ln1_b, *from_sibling = _ff_up_bwd(
        d_a, d_b, wt["w_ff_gate"], wt["w_ff_up"], dr2, xhat1, rstd1, ln1_g, _sibling_swap_ride([contrib[k] for k in ff_weights]))
    ff_sums = pair_sums(ff_weights, contrib, from_sibling)

    d_ya, d_yssm, d_proj, d_attn, d_glu, d_ys, mixed, y_s, gy, d_bg = _mixer_bwd(
        dr1, proj, y_attn, y_ssm, glu, ys, wt["w_attn_br"], wt["w_ssm_br"], wt["w_glu"], wt["w_out"], b_gate_full)
    contrib.update(w_attn_br=_weight_grad(attn, d_ya, "wgrad_w_attn_br", 128),
                   w_ssm_br=_weight_grad(y_s, d_yssm, "wgrad_w_ssm_br", 128),
                   w_glu=_weight_grad(gy, d_glu, "wgrad_w_glu", 128),
                   w_out=_weight_grad(mixed, dr1, "wgrad_w_out"),
                   b_gate=d_bg.reshape(2, 4, 2, 128).transpose(2, 1, 0, 3))
    d_proj, *landed = _attn_bwd(q_pm, k_pm, v_pm, cos_t, sin_t, attn, lse, d_attn, d_proj,
                                _chip_swap_ride(ff_sums) + _sibling_swap_ride([contrib[k] for k in mixer_weights]))
    parts, own_sums = dict(zip(ff_weights, landed[:len(ff_weights)])), {}
    mixer_sums = pair_sums(mixer_weights, contrib, landed[len(ff_weights):])
    landed = _ssm_bwd(d_ys, proj, states, bmat, cmat, a_chunks, ssm_d, d_proj, _chip_swap_ride(mixer_sums))
    d_proj, *ssm_blocks, d_abar, d_skip = landed[:7]
    parts.update(zip(mixer_weights, landed[7:]))

    gbb_re_t, gbb_im_t, gc_re, gc_im = (blocks.reshape(SSM_GROUPS, SSM_GROUP, SSM_STATE) for blocks in ssm_blocks)
    ga_re = d_abar[:, 0, :CHUNK_STATES].reshape(SSM_GROUPS, SSM_STATE)
    ga_im = d_abar[:, 0, CHUNK_STATES:].reshape(SSM_GROUPS, SSM_STATE)
    g_a_re, g_a_im, g_log_dt, g_b_re_t, g_b_im_t = _ssm_param_bwd(
        a_re, a_im, log_dt, b_re_t, b_im_t, abar_re, abar_im, e_re, e_im, ga_re, ga_im, gbb_re_t, gbb_im_t)
    mine = [g_a_re, g_a_im, g_log_dt, g_b_re_t, g_b_im_t, gc_re, -gc_im,
            d_skip, d_ln1_g, d_ln1_b, d_ln2_g, d_ln2_b]
    small_packed = _pack_rows(mine + [loss_lanes])

    w_in_contrib, small_partly = _weight_grad_rows(x2, d_proj, core, "wgrad_w_in_first", 896, 0, 6,
                                                   _gather_first_level([small_packed]))
    w_in_contrib, from_sibling, every = _weight_grad_rows(
        x2, d_proj, core, "wgrad_w_in_rest", 896, 6, 2,
        _sibling_swap_ride([w_in_contrib], halves=False) + _gather_second_level([small_partly]), in_place=True)
    w_in_sum, = _pair_sums([w_in_contrib], [from_sibling], core, "pair_sums_w_in")
    send_sems, recv_sems, w_in_sum, landing, token = _chip_swap_start([w_in_sum], "w_in_chip_swap_start")

    def adamw_of(k):
        taken = (lambda a: a.T) if k in narrow else (lambda a: a)
        return taken(local[k]), taken(given["m_" + k][0]), taken(given["v_" + k][0]), parts[k], own_sums.get(k), k in narrow

    others = [k for k in sharded if k != "w_in"]
    updated = _adamw_many([adamw_of(k) for k in others], "adamw_others", _after(token))
    grad_x, = _grad_x(d_proj, wt["w_in"], dr1, _after(token))

    def held(k, a):
        return a.transpose(0, 1, 3, 2) if k in ("ssm_b_re", "ssm_b_im") else a

    *small_grads, loss_sum = _unpack_rows(_sum_devices(every), [held(k, given[k]).shape for k in SMALL] + [(1, 128)])
    small = _adamw_replicated([held(k, given[k]) for k in SMALL], [held(k, given["m_" + k]) for k in SMALL],
                              [held(k, given["v_" + k]) for k in SMALL], small_grads)
    loss = loss_sum[0, 0]

    (own_sums["w_in"],), (parts["w_in"],) = _chip_swap_wait(
        send_sems, recv_sems, w_in_sum, landing, [grad_x, updated[0], small[0][0]], "w_in_chip_swap_wait")
    updated += _adamw_many([adamw_of("w_in")], "adamw_w_in")

    grads, deltas, new_m, new_v = {}, {}, {}, {}
    for i, k in enumerate(others + ["w_in"]):
        out = [o.T if k in narrow else o for o in updated[4 * i:4 * i + 4]]
        grads[k], deltas[k], new_m[k], new_v[k] = (o.reshape((1,) + local[k].shape) for o in out)
    for res, values in zip((grads, deltas, new_m, new_v), (small_grads,) + small):
        res.update((k, held(k, a)) for k, a in zip(SMALL, values))

    order = ("w_in", "b_gate", "w_attn_br", "w_ssm_br", "w_out", "ssm_a_re", "ssm_a_im", "ssm_log_dt", "ssm_b_re", "ssm_b_im",
             "ssm_c_re", "ssm_c_im", "ssm_d", "w_glu", "ln1_g", "ln1_b", "w_ff_gate", "w_ff_up", "w_ff_down", "ln2_g", "ln2_b")
    return (loss, grad_x[None], *[grads[k] for k in order], *[deltas[k] for k in order], *[new_m[k] for k in order],
            *[new_v[k] for k in order])
```

```python
import functools
import math

import jax
import jax.numpy as jnp
import numpy as np
from jax import lax
from jax.experimental import pallas as pl
from jax.experimental.pallas import tpu as pltpu

F32 = jnp.float32
BF16 = jnp.bfloat16

N_DEV = 8
SEQ = 2048
D_MODEL = 1024
HEAD_DIM = 64
ATTN_WIDTH = 512
QKV_WIDTH = 1536
SSM_WIDTH = 512
SSM_GROUPS = 32
SSM_GROUP = 16
SSM_STATE = 64
IN_WIDTH = 7168
D_FF = 2816
FF_SHARD = D_FF // N_DEV
FF_PAD = 384
D_FF_PAD = FF_PAD * N_DEV
DN_ALPHA = 2.0 ** 0.25
LN_EPS = 1e-5
NEG_INF = -1e30
ROPE_THETA = 10000.0
BLOCK = 128
GROUPS = ((1, 16), (4, 4), (16, 1))

ADAM_LR = 0.001
ADAM_B1 = 0.9
ADAM_B2 = 0.999
ADAM_EPS = 1e-08
ADAM_WD = 0.01
ADAM_STEP = 10

VMEM_LIMIT = 56 * 1024 * 1024


_pallas_call = pl.pallas_call


def _cparams(**kw):
    return pltpu.CompilerParams(vmem_limit_bytes=VMEM_LIMIT, **kw)


def _dot(a, b):
    return jnp.dot(a, b, preferred_element_type=F32)


def _dot_nt(a, b):
    return lax.dot_general(a, b, (((1,), (1,)), ((), ())), preferred_element_type=F32)


def _side_by_side(w_ref, row=None):
    rows = slice(None) if row is None else pl.ds(row, 1)
    return jnp.concatenate([w_ref[i, rows, :] for i in range(w_ref.shape[0])], axis=1)


def _dot_tn(a, b):
    return lax.dot_general(a, b, (((0,), (0,)), ((), ())), preferred_element_type=F32)


def _rope_tables():
    half = HEAD_DIM // 2
    inv_freq = np.float32(ROPE_THETA) ** (-np.arange(half, dtype=np.float32) / np.float32(half))
    ang = np.arange(SEQ, dtype=np.float32)[:, None] * inv_freq[None, :]
    cos, sin = np.cos(ang).astype(np.float32), np.sin(ang).astype(np.float32)
    tables = np.tile(cos, (1, 4)), np.tile(np.concatenate([-sin, sin], axis=1), (1, 2))

    def by_phase(t):
        return np.stack([t.reshape(SEQ // d, d, 128).transpose(1, 0, 2).reshape(SEQ, 128) for d, _ in GROUPS])

    return jnp.asarray(by_phase(tables[0])), jnp.asarray(by_phase(tables[1]))


def _swap_halves(x):
    lane = lax.broadcasted_iota(jnp.int32, x.shape, 1)
    return jnp.where((lane & 63) < 32, pltpu.roll(x, 96, axis=1), pltpu.roll(x, 32, axis=1))


def _group_rows(d, nb, r, i):
    src = pl.ds(i * BLOCK, BLOCK) if d == 1 else pl.ds(r + i * BLOCK * d, BLOCK, stride=d)
    return src, pl.ds((r * nb + i) * BLOCK, BLOCK)


def _attn_masks():
    a_idx = lax.broadcasted_iota(jnp.int32, (2 * BLOCK, 2 * BLOCK), 0) & (BLOCK - 1)
    c_idx = lax.broadcasted_iota(jnp.int32, (2 * BLOCK, 2 * BLOCK), 1)
    cur_ok = jnp.logical_and(c_idx >= BLOCK, c_idx - BLOCK <= a_idx)
    prev_ok = jnp.logical_and(c_idx < BLOCK, c_idx >= a_idx)
    lane = lax.broadcasted_iota(jnp.int32, (BLOCK, 128), 1)
    return cur_ok, prev_ok, lane < HEAD_DIM


def _stack_heads(t, head0):
    zero = jnp.zeros_like(t)
    return jnp.concatenate([jnp.where(head0, t, zero), jnp.where(head0, zero, t)], axis=0)


def _unstack_heads(t2, head0):
    return jnp.where(head0, t2[:BLOCK], t2[BLOCK:])


def _attn_fwd(proj, cos_t, sin_t, ride=None):
    def body(q0, q1, q2, k0, k1, k2, v0, v1, v2, cos_ref, sin_ref, attn_ref, lse_ref, qpm_ref, kpm_ref, vpm_ref,
             qs, ks, vs, os_, ms, ls, acc, mnat, lnat):
        cur_ok, prev_ok, head0 = _attn_masks()
        ks[:BLOCK, :] = jnp.zeros((BLOCK, 128), BF16)
        vs[:BLOCK, :] = jnp.zeros((BLOCK, 128), BF16)
        for g, (d, nb) in enumerate(GROUPS):
            q_ref, k_ref, v_ref = (q0, q1, q2)[g], (k0, k1, k2)[g], (v0, v1, v2)[g]
            for r in range(d):
                for i in range(nb):
                    src, dst = _group_rows(d, nb, r, i)
                    below = pl.ds(dst.start + BLOCK, BLOCK)
                    c, s = cos_ref[g, dst, :], sin_ref[g, dst, :]
                    q = q_ref[src, :]
                    k = k_ref[src, :]
                    qs[dst, :] = ((q * c + _swap_halves(q) * s) * 0.125).astype(BF16)
                    ks[below, :] = (k * c + _swap_halves(k) * s).astype(BF16)
                    vs[below, :] = v_ref[src, :].astype(BF16)
                    qpm_ref[g, dst, :], kpm_ref[g, dst, :], vpm_ref[g, dst, :] = qs[dst, :], ks[below, :], vs[below, :]

            def block(b, carry, nb=nb):
                has_prev = (b & (nb - 1)) > 0
                cur = pl.ds(pl.multiple_of(b * BLOCK, BLOCK), BLOCK)
                window = pl.ds(pl.multiple_of(b * BLOCK, BLOCK), 2 * BLOCK)
                valid = jnp.logical_or(cur_ok, jnp.logical_and(prev_ok, has_prev))
                s = jnp.where(valid, _dot_nt(_stack_heads(qs[cur, :], head0), ks[window, :]), NEG_INF)
                m = jnp.max(s, axis=1, keepdims=True)
                p = jnp.exp(s - m)
                os_[cur, :] = _unstack_heads(_dot(p.astype(BF16), vs[window, :]), head0)
                ms[cur, :] = _unstack_heads(m, head0)
                ls[cur, :] = _unstack_heads(jnp.sum(p, axis=1, keepdims=True), head0)
                return carry

            lax.fori_loop(0, SEQ // BLOCK, block, 0, unroll=16)

            for r in range(d):
                for i in range(nb):
                    src, dst = _group_rows(d, nb, r, i)
                    if g == 0:
                        acc[src, :], mnat[src, :], lnat[src, :] = os_[dst, :], ms[dst, :], ls[dst, :]
                    else:
                        m_old, m_g = mnat[src, :], ms[dst, :]
                        m_new = jnp.maximum(m_old, m_g)
                        a_old, a_g = jnp.exp(m_old - m_new), jnp.exp(m_g - m_new)
                        acc[src, :] = a_old * acc[src, :] + a_g * os_[dst, :]
                        lnat[src, :] = a_old * lnat[src, :] + a_g * ls[dst, :]
                        mnat[src, :] = m_new
        for i in range(SEQ // BLOCK):
            rows = pl.ds(i * BLOCK, BLOCK)
            l = lnat[rows, :]
            attn_ref[rows, :] = acc[rows, :] / l
            lse_ref[rows, :] = mnat[rows, :] + jnp.log(l)

    def col(base):
        return pl.BlockSpec((SEQ, 128), lambda hp, base=base: (0, base + hp))

    in_specs = [col(g * 4) for g in range(3)] + [col(12 + g * 4) for g in range(3)] + [col(24 + g * 4) for g in range(3)]
    table = pl.BlockSpec((3, SEQ, 128), lambda hp: (0, 0, 0), pipeline_mode=pl.Buffered(1))
    out = pl.BlockSpec((SEQ, 128), lambda hp: (0, hp))
    by_phase = pl.BlockSpec((3, SEQ, 128), lambda hp: (0, 0, hp))
    return _call(
        body, "attn_fwd", (4,), in_specs + [table, table], [out, out] + [by_phase] * 3,
        [_sds((SEQ, ATTN_WIDTH), F32), _sds((SEQ, ATTN_WIDTH), F32)] + [_sds((3, SEQ, ATTN_WIDTH), BF16)] * 3,
        [pltpu.VMEM((SEQ, 128), BF16)] + [pltpu.VMEM((SEQ + BLOCK, 128), BF16)] * 2 + [pltpu.VMEM((SEQ, 128), F32)] * 6,
        [proj] * 9 + [cos_t, sin_t], ride)


def _attn_bwd_group_body(g):
    d, nb = GROUPS[g]

    def body(qs_ref, ks_ref, vs_ref, cos_ref, sin_ref, lse_ref, dattn_ref, dsum_ref, dproj_ref,
             ks, vs, dos, lss, dss, dqs, dks, dvs, stage, outs, sems):
        cur_ok, prev_ok, head0 = _attn_masks()
        qs = qs_ref.at[g]
        ks[:BLOCK, :] = jnp.zeros((BLOCK, 128), BF16)
        vs[:BLOCK, :] = jnp.zeros((BLOCK, 128), BF16)
        dks[:BLOCK, :] = jnp.zeros((BLOCK, 128), F32)
        dvs[:BLOCK, :] = jnp.zeros((BLOCK, 128), F32)
        for r in range(d):
            for i in range(nb):
                src, dst = _group_rows(d, nb, r, i)
                below = pl.ds(dst.start + BLOCK, BLOCK)
                ks[below, :] = ks_ref[g, dst, :]
                vs[below, :] = vs_ref[g, dst, :]
                dos[dst, :] = dattn_ref[src, :].astype(BF16)
                for per_head, spread in ((dsum_ref[src, :], dss), (lse_ref[src, :], lss)):
                    other = pltpu.roll(per_head, HEAD_DIM, axis=1)
                    spread[0, dst, :] = jnp.where(head0, per_head, other)
                    spread[1, dst, :] = jnp.where(head0, other, per_head)
                dks[below, :] = jnp.zeros((BLOCK, 128), F32)
                dvs[below, :] = jnp.zeros((BLOCK, 128), F32)

        def per_stacked_row(spread, cur):
            h0, h1 = spread[0, cur, :], spread[1, cur, :]
            return jnp.concatenate([jnp.concatenate([h0, h0], axis=1), jnp.concatenate([h1, h1], axis=1)], axis=0)

        def block(b, carry):
            has_prev = (b & (nb - 1)) > 0
            cur = pl.ds(pl.multiple_of(b * BLOCK, BLOCK), BLOCK)
            window = pl.ds(pl.multiple_of(b * BLOCK, BLOCK), 2 * BLOCK)
            valid = jnp.logical_or(cur_ok, jnp.logical_and(prev_ok, has_prev))
            q2, do2 = _stack_heads(qs[cur, :], head0), _stack_heads(dos[cur, :], head0)
            kw, vw = ks[window, :], vs[window, :]
            s = jnp.where(valid, _dot_nt(q2, kw), NEG_INF)
            p = jnp.exp(s - per_stacked_row(lss, cur))
            ds = (p * (_dot_nt(do2, vw) - per_stacked_row(dss, cur))).astype(BF16)
            dvs[window, :] += _dot_tn(p.astype(BF16), do2)
            dks[window, :] += _dot_tn(ds, q2)
            dqs[cur, :] = _unstack_heads(_dot(ds, kw), head0)
            return carry

        lax.fori_loop(0, SEQ // BLOCK, block, 0, unroll=16)

        hp = pl.program_id(0)
        copies = []
        for kind in range(3):
            for r in range(d):
                for i in range(nb):
                    src, dst = _group_rows(d, nb, r, i)
                    below = pl.ds(dst.start + BLOCK, BLOCK)
                    if kind == 2:
                        stage[src, :] = dvs[below, :]
                    else:
                        c, s = cos_ref[g, dst, :], sin_ref[g, dst, :]
                        t = dqs[dst, :] * 0.125 if kind == 0 else dks[below, :]
                        stage[src, :] = t * c - _swap_halves(t) * s
            for i in range(SEQ // MM_ROWS):
                rows = pl.ds(i * MM_ROWS, MM_ROWS)
                outs[kind, rows, :] = stage[rows, :].astype(BF16)
            column = pl.multiple_of((kind * 12 + g * 4 + hp) * 128, 128)
            copies.append(pltpu.make_async_copy(outs.at[kind], dproj_ref.at[:, pl.ds(column, 128)], sems.at[kind]))
            copies[-1].start()
        for cp in copies:
            cp.wait()

    return body


def _attn_bwd(q_pm, k_pm, v_pm, cos_t, sin_t, attn, lse, dattn, dproj, ride=None):
    groups = [_attn_bwd_group_body(g) for g in range(3)]

    def body(qs_ref, ks_ref, vs_ref, cos_ref, sin_ref, attn_ref, lse_ref, dattn_ref, dproj_in, dproj_ref, dsum, *scratch):
        del dproj_in
        head0 = _attn_masks()[2]
        for i in range(SEQ // BLOCK):
            rows = pl.ds(i * BLOCK, BLOCK)
            prod = dattn_ref[rows, :] * attn_ref[rows, :]
            d0 = jnp.sum(jnp.where(head0, prod, 0.0), axis=1, keepdims=True)
            d1 = jnp.sum(jnp.where(head0, 0.0, prod), axis=1, keepdims=True)
            dsum[rows, :] = jnp.where(head0, d0, d1)
        for g in range(3):
            groups[g](qs_ref, ks_ref, vs_ref, cos_ref, sin_ref, lse_ref, dattn_ref, dsum, dproj_ref, *scratch)

    def col(base):
        return pl.BlockSpec((SEQ, 128), lambda hp, base=base: (0, base + hp))

    table = pl.BlockSpec((3, SEQ, 128), lambda hp: (0, 0, 0), pipeline_mode=pl.Buffered(1))
    by_phase = pl.BlockSpec((3, SEQ, 128), lambda hp: (0, 0, hp))
    return _call(
        body, "attn_bwd", (4,), [by_phase] * 3 + [table, table, col(0), col(0), col(0), ANY],
        [ANY], [_sds((SEQ, IN_WIDTH), BF16)],
        [pltpu.VMEM((SEQ, 128), F32)]
        + [pltpu.VMEM((SEQ + BLOCK, 128), BF16)] * 2 + [pltpu.VMEM((SEQ, 128), BF16)]
        + [pltpu.VMEM((2, SEQ, 128), F32)] * 2 + [pltpu.VMEM((SEQ, 128), F32)]
        + [pltpu.VMEM((SEQ + BLOCK, 128), F32)] * 2 + [pltpu.VMEM((SEQ, 128), F32)]
        + [pltpu.VMEM((3, SEQ, 128), BF16), pltpu.SemaphoreType.DMA((3,))],
        [q_pm, k_pm, v_pm, cos_t, sin_t, attn, lse, dattn, dproj], ride, aliases={8: 0})


SSM_CHUNKS = 4
CHUNK_STATES = 512
SCAN_ROWS = 8
U_COL = (3 * QKV_WIDTH) // 128


def _cmul(xr, xi, yr, yi):
    return xr * yr - xi * yi, xr * yi + xi * yr


def _ssm_prep(a_re, a_im, log_dt, b_re_t, b_im_t):
    def body(ar_ref, ai_ref, ldt_ref, br_ref, bi_ref, abr_ref, abi_ref, er_ref, ei_ref, bbr_ref, bbi_ref):
        ar, ai = ar_ref[...], ai_ref[...]
        dt = jnp.exp(ldt_ref[...])
        mag = jnp.exp(ar * dt)
        abr, abi = mag * jnp.cos(ai * dt), mag * jnp.sin(ai * dt)
        den = ar * ar + ai * ai
        nr, ni = abr - 1.0, abi
        er, ei = (nr * ar + ni * ai) / den, (ni * ar - nr * ai) / den
        abr_ref[...], abi_ref[...], er_ref[...], ei_ref[...] = abr, abi, er, ei
        er3, ei3 = er[:, None, :], ei[:, None, :]
        br, bi = br_ref[...], bi_ref[...]
        bbr_ref[...] = er3 * br - ei3 * bi
        bbi_ref[...] = er3 * bi + ei3 * br

    gp = jax.ShapeDtypeStruct(a_re.shape, F32)
    gb = jax.ShapeDtypeStruct(b_re_t.shape, F32)
    return _pallas_call(body, name="ssm_prep", out_shape=(gp, gp, gp, gp, gb, gb))(a_re, a_im, log_dt, b_re_t, b_im_t)


def _ssm_param_bwd(a_re, a_im, log_dt, b_re_t, b_im_t, abar_re, abar_im, e_re, e_im, ga_re, ga_im, gbb_re_t, gbb_im_t):
    def body(ar_ref, ai_ref, ldt_ref, br_ref, bi_ref, abr_ref, abi_ref, er_ref, ei_ref, gar_ref, gai_ref, gbr_ref, gbi_ref,
             o_ar, o_ai, o_ldt, o_br, o_bi):
        ar, ai = ar_ref[...], ai_ref[...]
        dt = jnp.exp(ldt_ref[...])
        er, ei = er_ref[...], ei_ref[...]
        br, bi, gbr, gbi = br_ref[...], bi_ref[...], gbr_ref[...], gbi_ref[...]
        er3, ei3 = er[:, None, :], ei[:, None, :]
        o_br[...] = er3 * gbr + ei3 * gbi
        o_bi[...] = er3 * gbi - ei3 * gbr
        ge_r = jnp.sum(br * gbr + bi * gbi, axis=1)
        ge_i = jnp.sum(br * gbi - bi * gbr, axis=1)
        den = ar * ar + ai * ai
        ilr, ili = ar / den, -ai / den
        t_r, t_i = _cmul(ilr, -ili, ge_r, ge_i)
        gab_r, gab_i = gar_ref[...] + t_r, gai_ref[...] + t_i
        gz_r, gz_i = _cmul(abr_ref[...], -abi_ref[...], gab_r, gab_i)
        el_r, el_i = _cmul(er, ei, ilr, ili)
        u_r, u_i = _cmul(el_r, -el_i, ge_r, ge_i)
        o_ar[...] = dt * gz_r - u_r
        o_ai[...] = dt * gz_i - u_i
        o_ldt[...] = jnp.sum(gz_r * ar + gz_i * ai, axis=1, keepdims=True) * dt

    gp = jax.ShapeDtypeStruct(a_re.shape, F32)
    gb = jax.ShapeDtypeStruct(b_re_t.shape, F32)
    return _pallas_call(body, name="ssm_param_bwd", out_shape=(gp, gp, jax.ShapeDtypeStruct(log_dt.shape, F32), gb, gb))(
        a_re, a_im, log_dt, b_re_t, b_im_t, abar_re, abar_im, e_re, e_im, ga_re, ga_im, gbb_re_t, gbb_im_t)


def _block_diag(blocks_re, blocks_im, sign_im, rows_are_channels):
    both = jnp.stack([blocks_re, sign_im * blocks_im]).reshape(2, SSM_CHUNKS, 8, SSM_GROUP, SSM_STATE)
    eye = jnp.eye(8, dtype=F32)
    if rows_are_channels:
        return jnp.einsum("rcghp,gk->cghrkp", both, eye).reshape(SSM_CHUNKS, 128, 2 * CHUNK_STATES)
    return jnp.einsum("rcghp,gk->crkpgh", both, eye).reshape(SSM_CHUNKS, 2 * CHUNK_STATES, 128)


def _diagonal_blocks(mat, part):
    first = [part * CHUNK_STATES + SSM_STATE * g for g in range(8)]
    return jnp.concatenate([mat[SSM_GROUP * g:SSM_GROUP * (g + 1), first[g]:first[g] + SSM_STATE] for g in range(8)], axis=0)


def _scan_consts(a_ref, conj, reverse):
    ar = jnp.broadcast_to(a_ref[:, :CHUNK_STATES], (SCAN_ROWS, CHUNK_STATES))
    ai = jnp.broadcast_to(a_ref[:, CHUNK_STATES:], (SCAN_ROWS, CHUNK_STATES))
    if conj:
        ai = -ai
    row = lax.broadcasted_iota(jnp.int32, (SCAN_ROWS, CHUNK_STATES), 0)
    if reverse:
        row = SCAN_ROWS - 1 - row
    zero = jnp.zeros_like(ar)
    steps = []
    pr, pi = ar, ai
    for shift in (1, 2, 4):
        keep = row >= shift
        steps.append((SCAN_ROWS - shift if reverse else shift, jnp.where(keep, pr, zero), jnp.where(keep, pi, zero)))
        pr, pi = _cmul(pr, pi, pr, pi)
    first = row == 0
    return steps, (jnp.where(first, ar, zero), jnp.where(first, ai, zero)), first


def _scan_tile(xr, xi, prev_r, prev_i, steps, carry_in, reverse):
    edge = SCAN_ROWS - 1 if reverse else 1
    cr, ci = pltpu.roll(prev_r, edge, axis=0), pltpu.roll(prev_i, edge, axis=0)
    xr, xi = xr + carry_in[0] * cr - carry_in[1] * ci, xi + carry_in[0] * ci + carry_in[1] * cr
    for shift, mr, mi in steps:
        sr, si = pltpu.roll(xr, shift, axis=0), pltpu.roll(xi, shift, axis=0)
        xr, xi = xr + mr * sr - mi * si, xi + mr * si + mi * sr
    return xr, xi


MM_ROWS = 256


def _ssm_fwd(proj, bmat, cmat, a_chunks, d_skip, ride=None):
    def body(u_ref, b_ref, c_ref, a_ref, d_ref, y_ref, states_ref, h_ref):
        for i in range(SEQ // MM_ROWS):
            rows = pl.ds(i * MM_ROWS, MM_ROWS)
            h_ref[rows, :] = _dot(u_ref[rows, :].astype(BF16), b_ref[...])
        steps, carry_in, _ = _scan_consts(a_ref, conj=False, reverse=False)

        def tile(k, carry):
            rows = pl.ds(pl.multiple_of(k * SCAN_ROWS, SCAN_ROWS), SCAN_ROWS)
            xr, xi = _scan_tile(h_ref[rows, :CHUNK_STATES], h_ref[rows, CHUNK_STATES:], carry[0], carry[1], steps, carry_in, False)
            h_ref[rows, :CHUNK_STATES] = xr
            h_ref[rows, CHUNK_STATES:] = xi
            return xr, xi

        zero = jnp.zeros((SCAN_ROWS, CHUNK_STATES), F32)
        lax.fori_loop(0, SEQ // SCAN_ROWS, tile, (zero, zero), unroll=4)
        for i in range(SEQ // MM_ROWS):
            rows = pl.ds(i * MM_ROWS, MM_ROWS)
            states = h_ref[rows, :].astype(BF16)
            states_ref[rows, :] = states
            y_ref[rows, :] = _dot(states, c_ref[...]) + d_ref[...] * u_ref[rows, :]

    return _call(
        body, "ssm_fwd", (SSM_CHUNKS,),
        [pl.BlockSpec((SEQ, 128), lambda c: (0, U_COL + c)),
         pl.BlockSpec((None, 128, 2 * CHUNK_STATES), lambda c: (c, 0, 0)),
         pl.BlockSpec((None, 2 * CHUNK_STATES, 128), lambda c: (c, 0, 0)),
         pl.BlockSpec((None, 1, 2 * CHUNK_STATES), lambda c: (c, 0, 0)),
         pl.BlockSpec((1, 128), lambda c: (0, c))],
        [pl.BlockSpec((SEQ, 128), lambda c: (0, c)), pl.BlockSpec((SEQ, 2 * CHUNK_STATES), lambda c: (0, c))],
        [_sds((SEQ, SSM_WIDTH), F32), _sds((SEQ, SSM_CHUNKS * 2 * CHUNK_STATES), BF16)],
        [pltpu.VMEM((SEQ, 2 * CHUNK_STATES), F32)],
        [proj, bmat, cmat, a_chunks, d_skip], ride)


def _ssm_bwd(dys, proj, h, bmat, cmat, a_chunks, d_skip, dproj, ride=None):
    def body(dy_ref, u_ref, states_ref, b_ref, c_ref, a_ref, d_ref, dproj_in, du_ref, db_re_ref, db_im_ref, dc_re_ref, dc_im_ref,
             da_ref, dd_ref, g_ref, h_ref):
        del dproj_in
        dsum = jnp.zeros((1, 128), F32)
        dcm = jnp.zeros((128, 2 * CHUNK_STATES), F32)
        for i in range(SEQ // MM_ROWS):
            rows = pl.ds(i * MM_ROWS, MM_ROWS)
            h_ref[rows, :] = states_ref[rows, :].astype(F32)
            dy = dy_ref[rows, :]
            g_ref[rows, :] = _dot_nt(dy.astype(BF16), c_ref[...])
            dsum += jnp.sum(dy * u_ref[rows, :], axis=0, keepdims=True)
            dcm += _dot_tn(dy.astype(BF16), states_ref[rows, :])
        dd_ref[...] = dsum
        dc_re_ref[...] = _diagonal_blocks(dcm, 0)
        dc_im_ref[...] = _diagonal_blocks(dcm, 1)
        steps, carry_in, _ = _scan_consts(a_ref, conj=True, reverse=True)
        first_row = lax.broadcasted_iota(jnp.int32, (SCAN_ROWS, CHUNK_STATES), 0) == 0
        n_tiles = SEQ // SCAN_ROWS

        def tile(j, carry):
            k = n_tiles - 1 - j
            rows = pl.ds(pl.multiple_of(k * SCAN_ROWS, SCAN_ROWS), SCAN_ROWS)
            before = pl.ds(pl.multiple_of(jnp.maximum(k - 1, 0) * SCAN_ROWS, SCAN_ROWS), SCAN_ROWS)
            gr, gi = _scan_tile(g_ref[rows, :CHUNK_STATES], g_ref[rows, CHUNK_STATES:], carry[0], carry[1], steps, carry_in, True)
            g_ref[rows, :CHUNK_STATES] = gr
            g_ref[rows, CHUNK_STATES:] = gi
            has_before = jnp.where(k > 0, 1.0, 0.0)
            hr = jnp.where(first_row, pltpu.roll(h_ref[before, :CHUNK_STATES], 1, axis=0) * has_before,
                           pltpu.roll(h_ref[rows, :CHUNK_STATES], 1, axis=0))
            hi = jnp.where(first_row, pltpu.roll(h_ref[before, CHUNK_STATES:], 1, axis=0) * has_before,
                           pltpu.roll(h_ref[rows, CHUNK_STATES:], 1, axis=0))
            return gr, gi, carry[2] + hr * gr + hi * gi, carry[3] + hr * gi - hi * gr

        zero = jnp.zeros((SCAN_ROWS, CHUNK_STATES), F32)
        _, _, sar, sai = lax.fori_loop(0, n_tiles, tile, (zero, zero, zero, zero), unroll=4)
        da_ref[:, :CHUNK_STATES] = jnp.sum(sar, axis=0, keepdims=True)
        da_ref[:, CHUNK_STATES:] = jnp.sum(sai, axis=0, keepdims=True)
        dbm = jnp.zeros((128, 2 * CHUNK_STATES), F32)
        for i in range(SEQ // MM_ROWS):
            rows = pl.ds(i * MM_ROWS, MM_ROWS)
            g = g_ref[rows, :].astype(BF16)
            du_ref[rows, :] = (_dot_nt(g, b_ref[...]) + d_ref[...] * dy_ref[rows, :]).astype(BF16)
            dbm += _dot_tn(u_ref[rows, :].astype(BF16), g)
        db_re_ref[...] = _diagonal_blocks(dbm, 0)
        db_im_ref[...] = _diagonal_blocks(dbm, 1)

    chunk_col = pl.BlockSpec((SEQ, 128), lambda c: (0, c))
    blocks = pl.BlockSpec((None, 128, SSM_STATE), lambda c: (c, 0, 0))
    return _call(
        body, "ssm_bwd", (SSM_CHUNKS,),
        [chunk_col,
         pl.BlockSpec((SEQ, 128), lambda c: (0, U_COL + c)),
         pl.BlockSpec((SEQ, 2 * CHUNK_STATES), lambda c: (0, c)),
         pl.BlockSpec((None, 128, 2 * CHUNK_STATES), lambda c: (c, 0, 0)),
         pl.BlockSpec((None, 2 * CHUNK_STATES, 128), lambda c: (c, 0, 0)),
         pl.BlockSpec((None, 1, 2 * CHUNK_STATES), lambda c: (c, 0, 0)),
         pl.BlockSpec((1, 128), lambda c: (0, c)), ANY],
        [pl.BlockSpec((SEQ, 128), lambda c: (0, U_COL + c)), blocks, blocks, blocks, blocks,
         pl.BlockSpec((None, 1, 2 * CHUNK_STATES), lambda c: (c, 0, 0)),
         pl.BlockSpec((1, 128), lambda c: (0, c))],
        [_sds((SEQ, IN_WIDTH), BF16)] + [_sds((SSM_CHUNKS, 128, SSM_STATE), F32)] * 4
        + [_sds((SSM_CHUNKS, 1, 2 * CHUNK_STATES), F32), _sds((1, SSM_WIDTH), F32)],
        [pltpu.VMEM((SEQ, 2 * CHUNK_STATES), F32)] * 2, [dys, proj, h, bmat, cmat, a_chunks, d_skip, dproj], ride, aliases={7: 0})


def _ssm_tables(abar_re, abar_im, bbar_re_t, bbar_im_t, c_re, c_im):
    bmat = _block_diag(bbar_re_t, bbar_im_t, 1.0, True).astype(BF16)
    cmat = _block_diag(c_re, c_im, -1.0, False).astype(BF16)
    a_chunks = jnp.concatenate([abar_re.reshape(SSM_CHUNKS, 1, CHUNK_STATES), abar_im.reshape(SSM_CHUNKS, 1, CHUNK_STATES)], axis=2)
    return bmat, cmat, a_chunks


GL_COL = (3 * QKV_WIDTH + SSM_WIDTH) // D_MODEL
GELU_C = math.sqrt(2.0 / math.pi)
GELU_A = 0.044715


def _sds(shape, dtype):
    return jax.ShapeDtypeStruct(shape, dtype)


def _gelu(x):
    t = jnp.tanh(GELU_C * (x + GELU_A * x * x * x))
    return 0.5 * x * (1.0 + t), t


def _gelu_grad(x, t):
    return 0.5 * (1.0 + t) + 0.5 * x * (1.0 - t * t) * GELU_C * (1.0 + 3.0 * GELU_A * x * x)


def _layer_norm(r, g, b):
    mu = jnp.mean(r, axis=-1, keepdims=True)
    xc = r - mu
    rstd = lax.rsqrt(jnp.mean(xc * xc, axis=-1, keepdims=True) + LN_EPS)
    xhat = xc * rstd
    return xhat * g + b, xhat, rstd


def _layer_norm_bwd(dy, xhat, rstd, g):
    dxhat = dy * g
    m1 = jnp.mean(dxhat, axis=-1, keepdims=True)
    m2 = jnp.mean(dxhat * xhat, axis=-1, keepdims=True)
    return rstd * (dxhat - m1 - xhat * m2)


def _proj(x, w_in, ride=None):
    tm, tn = 1024, 1792

    def body(x_ref, w_ref, o_ref):
        o_ref[...] = _dot(x_ref[...].astype(BF16), _side_by_side(w_ref))

    return _call(
        body, "proj", (SEQ // tm, IN_WIDTH // tn),
        [pl.BlockSpec((tm, D_MODEL), lambda i, j: (i, 0)), pl.BlockSpec((2, D_MODEL, tn // 2), lambda i, j: (j, 0, 0))],
        [pl.BlockSpec((tm, tn), lambda i, j: (i, j))], [_sds((SEQ, IN_WIDTH), F32)], [], [x, w_in], ride)


def _row_spec(tm, width, col=0):
    return pl.BlockSpec((tm, width), lambda i, col=col: (i, col))


def _full_spec(shape):
    return pl.BlockSpec(shape, lambda i: (0,) * len(shape))


def _weight_spec(shape):
    return pl.BlockSpec(shape, lambda i: (0,) * len(shape), pipeline_mode=pl.Buffered(1))


def _mixer_out(attn, ys, proj, x, w_ab, w_sb, w_glu, w_out, b_gate, ln_g, ln_b, ride=None):
    tm = 512

    def body(attn_ref, ys_ref, gl0_ref, gl1_ref, x_ref, wab_ref, wsb_ref, wglu_ref, wout_ref, bg_ref, g_ref, b_ref,
             h_ref, xhat_ref, rstd_ref, glu_ref, ya_ref, yssm_ref):
        gy, _ = _gelu(ys_ref[...])
        glu = _dot(gy.astype(BF16), _side_by_side(wglu_ref))
        glu_ref[...] = glu.astype(BF16)
        y_s = glu[:, :SSM_WIDTH] * jax.nn.sigmoid(glu[:, SSM_WIDTH:])
        y_ssm = _dot(y_s.astype(BF16), _side_by_side(wsb_ref))
        y_attn = _dot(attn_ref[...].astype(BF16), _side_by_side(wab_ref))
        ya_ref[...] = y_attn.astype(BF16)
        yssm_ref[...] = y_ssm.astype(BF16)
        g0 = jax.nn.sigmoid(gl0_ref[...] + _side_by_side(bg_ref, 0))
        g1 = jax.nn.sigmoid(gl1_ref[...] + _side_by_side(bg_ref, 1))
        mixed = g0 * y_attn + g1 * y_ssm
        r1 = DN_ALPHA * x_ref[...] + _dot(mixed.astype(BF16), wout_ref[...])
        h, xhat, rstd = _layer_norm(r1, g_ref[...], b_ref[...])
        h_ref[...] = h
        xhat_ref[...] = xhat
        rstd_ref[...] = jnp.broadcast_to(rstd, (tm, 128))

    wide = _sds((SEQ, D_MODEL), F32)
    return _call(
        body, "mixer_out", (SEQ // tm,),
        [_row_spec(tm, ATTN_WIDTH), _row_spec(tm, SSM_WIDTH), _row_spec(tm, D_MODEL, GL_COL), _row_spec(tm, D_MODEL, GL_COL + 1),
         _row_spec(tm, D_MODEL), _weight_spec((N_DEV, ATTN_WIDTH, 128)), _weight_spec((N_DEV, SSM_WIDTH, 128)),
         _weight_spec((N_DEV, SSM_WIDTH, 128)), _weight_spec((D_MODEL, D_MODEL)), _full_spec((N_DEV, 2, 128)),
         _full_spec((1, D_MODEL)), _full_spec((1, D_MODEL))],
        [_row_spec(tm, D_MODEL), _row_spec(tm, D_MODEL), _row_spec(tm, 128), _row_spec(tm, D_MODEL),
         _row_spec(tm, D_MODEL), _row_spec(tm, D_MODEL)],
        [wide, wide, _sds((SEQ, 128), F32)] + [_sds((SEQ, D_MODEL), BF16)] * 3, [],
        [attn, ys, proj, proj, x, w_ab, w_sb, w_glu, w_out, b_gate, ln_g, ln_b], ride)


def _ff_up(h, w_gate, w_up, ride=None):
    tm, tn = 1024, 768

    def body(h_ref, wg_ref, wu_ref, a_ref, b_ref, f_ref):
        hb = h_ref[...].astype(BF16)
        a, b = _dot(hb, _side_by_side(wg_ref)), _dot(hb, _side_by_side(wu_ref))
        a_ref[...] = a.astype(BF16)
        b_ref[...] = b.astype(BF16)
        f_ref[...] = (a * jax.nn.sigmoid(a) * b).astype(BF16)

    tile = pl.BlockSpec((tm, tn), lambda i, j: (i, j))
    wtile = pl.BlockSpec((tn // FF_PAD, D_MODEL, FF_PAD), lambda i, j: (j, 0, 0))
    out = _sds((SEQ, D_FF_PAD), BF16)
    return _call(body, "ff_up", (SEQ // tm, D_FF_PAD // tn), [pl.BlockSpec((tm, D_MODEL), lambda i, j: (i, 0)), wtile, wtile],
                 [tile, tile, tile], [out, out, out], [], [h, w_gate, w_up], ride)


def _ff_down_loss(f, w_down, h, target, ln_g, ln_b):
    tm = 512

    def body(f_ref, w_ref, h_ref, t_ref, g_ref, b_ref, dr_ref, dg_ref, db_ref, loss_ref):
        @pl.when(pl.program_id(0) == 0)
        def _():
            dg_ref[...] = jnp.zeros_like(dg_ref)
            db_ref[...] = jnp.zeros_like(db_ref)
            loss_ref[...] = jnp.zeros_like(loss_ref)

        r2 = DN_ALPHA * h_ref[...] + _dot(f_ref[...], w_ref[...])
        g = g_ref[...]
        out, xhat, rstd = _layer_norm(r2, g, b_ref[...])
        err = out - t_ref[...]
        loss_ref[...] += 0.5 * jnp.sum(jnp.mean(err * err, axis=-1, keepdims=True), axis=0, keepdims=True)
        dout = err * (1.0 / D_MODEL)
        dg_ref[...] += jnp.sum(dout * xhat, axis=0, keepdims=True)
        db_ref[...] += jnp.sum(dout, axis=0, keepdims=True)
        dr_ref[...] = _layer_norm_bwd(dout, xhat, rstd, g)

    vec = _sds((1, D_MODEL), F32)
    return _pallas_call(
        body, name="ff_down_loss", grid=(SEQ // tm,),
        in_specs=[_row_spec(tm, D_FF_PAD), _weight_spec((D_FF_PAD, D_MODEL)), _row_spec(tm, D_MODEL), _row_spec(tm, D_MODEL),
                  _full_spec((1, D_MODEL)), _full_spec((1, D_MODEL))],
        out_specs=(_row_spec(tm, D_MODEL), _full_spec((1, D_MODEL)), _full_spec((1, D_MODEL)), _full_spec((1, 128))),
        out_shape=(_sds((SEQ, D_MODEL), F32), vec, vec, _sds((1, 128), F32)),
        compiler_params=_cparams(dimension_semantics=("arbitrary",)),
    )(f, w_down, h, target, ln_g, ln_b)


def _ff_down_bwd(dr2, w_down, a, b):
    tm, tn = 1024, 768

    def body(dr_ref, w_ref, a_ref, b_ref, da_ref, db_ref):
        df = _dot_nt(dr_ref[...].astype(BF16), w_ref[...])
        av, bv = a_ref[...].astype(F32), b_ref[...].astype(F32)
        sg = jax.nn.sigmoid(av)
        da_ref[...] = (df * bv * sg * (1.0 + av * (1.0 - sg))).astype(BF16)
        db_ref[...] = (df * av * sg).astype(BF16)

    tile = pl.BlockSpec((tm, tn), lambda i, j: (i, j))
    out = _sds((SEQ, D_FF_PAD), BF16)
    return _pallas_call(
        body, name="ff_down_bwd", grid=(SEQ // tm, D_FF_PAD // tn),
        in_specs=[pl.BlockSpec((tm, D_MODEL), lambda i, j: (i, 0)), pl.BlockSpec((tn, D_MODEL), lambda i, j: (j, 0)), tile, tile],
        out_specs=(tile, tile), out_shape=(out, out),
        compiler_params=_cparams(dimension_semantics=("arbitrary", "arbitrary")),
    )(dr2, w_down, a, b)


def _ff_up_bwd(da, db, w_gate, w_up, dr2, xhat1, rstd1, ln_g, ride=None):
    tm, tk = 1024, 768
    nk = D_FF_PAD // tk

    def body(da_ref, db_ref, wg_ref, wu_ref, dr2_ref, xhat_ref, rstd_ref, g_ref, dr1_ref, dg_ref, dbias_ref, acc):
        i, k = pl.program_id(0), pl.program_id(1)

        @pl.when(jnp.logical_and(i == 0, k == 0))
        def _():
            dg_ref[...] = jnp.zeros_like(dg_ref)
            dbias_ref[...] = jnp.zeros_like(dbias_ref)

        part = _dot_nt(da_ref[...], _side_by_side(wg_ref)) + _dot_nt(db_ref[...], _side_by_side(wu_ref))

        @pl.when(k == 0)
        def _():
            acc[...] = part

        @pl.when(k > 0)
        def _():
            acc[...] += part

        @pl.when(k == nk - 1)
        def _():
            dh = DN_ALPHA * dr2_ref[...] + acc[...]
            xhat = xhat_ref[...]
            dg_ref[...] += jnp.sum(dh * xhat, axis=0, keepdims=True)
            dbias_ref[...] += jnp.sum(dh, axis=0, keepdims=True)
            rstd = jnp.max(rstd_ref[...], axis=1, keepdims=True)
            dr1_ref[...] = _layer_norm_bwd(dh, xhat, rstd, g_ref[...])

    hid = pl.BlockSpec((tm, tk), lambda i, k: (i, k))
    wtile = pl.BlockSpec((tk // FF_PAD, D_MODEL, FF_PAD), lambda i, k: (k, 0, 0))
    row = pl.BlockSpec((tm, D_MODEL), lambda i, k: (i, 0))
    vec = pl.BlockSpec((1, D_MODEL), lambda i, k: (0, 0))
    return _call(
        body, "ff_up_bwd", (SEQ // tm, nk),
        [hid, hid, wtile, wtile, row, row, pl.BlockSpec((tm, 128), lambda i, k: (i, 0)), vec],
        [row, vec, vec], [_sds((SEQ, D_MODEL), F32), _sds((1, D_MODEL), F32), _sds((1, D_MODEL), F32)],
        [pltpu.VMEM((tm, D_MODEL), F32)], [da, db, w_gate, w_up, dr2, xhat1, rstd1, ln_g], ride)


def _mixer_bwd(dr1, proj, y_attn, y_ssm, glu, ys, w_ab, w_sb, w_glu, w_out, b_gate):
    tm = 256

    def body(dr1_ref, gl0_ref, gl1_ref, ya_ref, yssm_ref, glu_ref, ys_ref, wab_ref, wsb_ref, wglu_ref, wout_ref, bg_ref,
             dya_ref, dyssm_ref, dgl_ref, dattn_ref, dglu_ref, dys_ref, mixed_ref, ysb_ref, gy_ref, dbg_ref, stage, copied):
        @pl.when(pl.program_id(0) == 0)
        def _():
            dbg_ref[...] = jnp.zeros_like(dbg_ref)

        dmixed = _dot_nt(dr1_ref[...].astype(BF16), wout_ref[...])
        g0 = jax.nn.sigmoid(gl0_ref[...] + _side_by_side(bg_ref, 0))
        g1 = jax.nn.sigmoid(gl1_ref[...] + _side_by_side(bg_ref, 1))
        y_attn, y_ssm = ya_ref[...].astype(F32), yssm_ref[...].astype(F32)
        mixed_ref[...] = (g0 * y_attn + g1 * y_ssm).astype(BF16)
        dya = (dmixed * g0).astype(BF16)
        dyssm = (dmixed * g1).astype(BF16)
        dya_ref[...] = dya
        dyssm_ref[...] = dyssm
        dgl0 = dmixed * y_attn * g0 * (1.0 - g0)
        dgl1 = dmixed * y_ssm * g1 * (1.0 - g1)
        i, last = pl.program_id(0), SEQ // tm - 1
        slot = i & 1

        def copy_out(buffer, tile):
            window = dgl_ref.at[pl.ds(pl.multiple_of(tile * tm, tm), tm), pl.ds(GL_COL * D_MODEL, 2 * D_MODEL)]
            return pltpu.make_async_copy(stage.at[buffer], window, copied.at[buffer])

        @pl.when(i >= 2)
        def _():
            copy_out(slot, i - 2).wait()

        stage[slot, :, :D_MODEL] = dgl0.astype(BF16)
        stage[slot, :, D_MODEL:] = dgl1.astype(BF16)
        copy_out(slot, i).start()

        @pl.when(i == last)
        def _():
            copy_out(1 - slot, i - 1).wait()
            copy_out(slot, i).wait()
        dbg_ref[:, :D_MODEL] += jnp.sum(dgl0, axis=0, keepdims=True)
        dbg_ref[:, D_MODEL:] += jnp.sum(dgl1, axis=0, keepdims=True)
        dattn_ref[...] = _dot_nt(dya, _side_by_side(wab_ref))
        dy_s = _dot_nt(dyssm, _side_by_side(wsb_ref))
        glu = glu_ref[...].astype(F32)
        glu1, sg = glu[:, :SSM_WIDTH], jax.nn.sigmoid(glu[:, SSM_WIDTH:])
        ysb_ref[...] = (glu1 * sg).astype(BF16)
        dglu1 = (dy_s * sg).astype(BF16)
        dglu2 = (dy_s * glu1 * sg * (1.0 - sg)).astype(BF16)
        dglu_ref[:, :SSM_WIDTH] = dglu1
        dglu_ref[:, SSM_WIDTH:] = dglu2
        dgy = _dot_nt(jnp.concatenate([dglu1, dglu2], axis=1), _side_by_side(wglu_ref))
        ys = ys_ref[...]
        gy, t = _gelu(ys)
        gy_ref[...] = gy.astype(BF16)
        dys_ref[...] = dgy * _gelu_grad(ys, t)

    wide_b, half_b = _sds((SEQ, D_MODEL), BF16), _sds((SEQ, SSM_WIDTH), BF16)
    half_f = _sds((SEQ, SSM_WIDTH), F32)
    return _pallas_call(
        body, name="mixer_bwd", grid=(SEQ // tm,),
        in_specs=[_row_spec(tm, D_MODEL), _row_spec(tm, D_MODEL, GL_COL), _row_spec(tm, D_MODEL, GL_COL + 1), _row_spec(tm, D_MODEL),
                  _row_spec(tm, D_MODEL), _row_spec(tm, D_MODEL), _row_spec(tm, SSM_WIDTH), _full_spec((N_DEV, ATTN_WIDTH, 128)),
                  _full_spec((N_DEV, SSM_WIDTH, 128)), _full_spec((N_DEV, SSM_WIDTH, 128)), _full_spec((D_MODEL, D_MODEL)),
                  _full_spec((N_DEV, 2, 128))],
        out_specs=(_row_spec(tm, D_MODEL), _row_spec(tm, D_MODEL), ANY, _row_spec(tm, ATTN_WIDTH),
                   _row_spec(tm, D_MODEL), _row_spec(tm, SSM_WIDTH), _row_spec(tm, D_MODEL), _row_spec(tm, SSM_WIDTH),
                   _row_spec(tm, SSM_WIDTH), _full_spec((1, 2 * D_MODEL))),
        out_shape=(wide_b, wide_b, _sds((SEQ, IN_WIDTH), BF16), half_f, wide_b, half_f, wide_b, half_b, half_b,
                   _sds((1, 2 * D_MODEL), F32)),
        scratch_shapes=[pltpu.VMEM((2, tm, 2 * D_MODEL), BF16), pltpu.SemaphoreType.DMA((2,))],
        compiler_params=_cparams(dimension_semantics=("arbitrary",)),
    )(dr1, proj, proj, y_attn, y_ssm, glu, ys, w_ab, w_sb, w_glu, w_out, b_gate)


def _grad_x(dproj, w_in, dr1, ride=None):
    tm, tk = 1024, 1792
    nk = IN_WIDTH // tk

    def body(dp_ref, w_ref, dr1_ref, o_ref, acc):
        k = pl.program_id(1)
        part = _dot_nt(dp_ref[...], _side_by_side(w_ref))

        @pl.when(k == 0)
        def _():
            acc[...] = part

        @pl.when(k > 0)
        def _():
            acc[...] += part

        @pl.when(k == nk - 1)
        def _():
            o_ref[...] = DN_ALPHA * dr1_ref[...] + acc[...]

    row = pl.BlockSpec((tm, D_MODEL), lambda i, k: (i, 0))
    return _call(
        body, "grad_x", (SEQ // tm, nk),
        [pl.BlockSpec((tm, tk), lambda i, k: (i, k)), pl.BlockSpec((2, D_MODEL, tk // 2), lambda i, k: (k, 0, 0)), row],
        [row], [_sds((SEQ, D_MODEL), F32)], [pltpu.VMEM((tm, D_MODEL), F32)], [dproj, w_in, dr1], ride)


def _weight_grad(a, b, name, shard_cols=None):
    k, n = a.shape[1], b.shape[1]
    tk = k if shard_cols else k // N_DEV
    tn = n // 4 if shard_cols else min(n, 1024)

    def body(a_ref, b_ref, o_ref):
        grad = _dot_tn(a_ref[...].astype(BF16), b_ref[...].astype(BF16))
        if shard_cols:
            o_ref[0] = grad[:, :shard_cols].astype(BF16)
            o_ref[1] = grad[:, shard_cols:].astype(BF16)
        else:
            o_ref[...] = grad.astype(BF16)

    if shard_cols:
        out_spec = pl.BlockSpec((2, None, tk, shard_cols), lambda kk, j: (0, j, kk, 0))
        out_shape = _sds((2, 4, k, shard_cols), BF16)
    else:
        out_spec = pl.BlockSpec((None, None, tk, tn), lambda kk, j: (kk % 2, kk // 2, 0, j))
        out_shape = _sds((2, 4, tk, n), BF16)
    return _call(body, name, (k // tk, n // tn),
                 [pl.BlockSpec((SEQ, tk), lambda kk, j: (0, kk)), pl.BlockSpec((SEQ, tn), lambda kk, j: (0, j))],
                 [out_spec], [out_shape], [], [a, b])[0]


def _weight_grad_rows(a, b, core, name, shard_cols, first, count, ride, in_place=False):
    k = a.shape[1]

    def body(a_ref, b_ref, o_ref):
        o_ref[...] = _dot_tn(a_ref[...].astype(BF16), b_ref[...].astype(BF16)).astype(BF16)

    def shard(j, core_ref):
        row = j + first
        return 0, jnp.where(row < 4, 2 * row + 1 - core_ref[0], 2 * (row - 4) + core_ref[0])

    return _call(body, name, (count,),
                 [pl.BlockSpec((SEQ, k), lambda j, core_ref: (0, 0), pipeline_mode=pl.Buffered(1)),
                  pl.BlockSpec((SEQ, shard_cols), shard)],
                 [pl.BlockSpec((None, k, shard_cols), lambda j, core_ref: (j + first, 0, 0))],
                 [_sds((N_DEV, k, shard_cols), BF16)], [], [a, b], ride, aliases={2: 0} if in_place else None, prefetch=core)


MESH = pl.DeviceIdType.MESH
ANY = pl.BlockSpec(memory_space=pl.ANY)


def _place():
    return lax.axis_index("x"), lax.axis_index("y"), lax.axis_index("c")


def _other_chips(x, y):
    return [(1 - x, y), (x, 1 - y), (1 - x, 1 - y)]


class _Ride:
    def __init__(self, operands, results, aliases, sems, start, wait):
        self.operands, self.results, self.aliases, self.sems = list(operands), list(results), dict(aliases), list(sems)
        self.start, self.wait = start, wait

    def __add__(self, other):
        n_in, n_out, n_sem = len(self.operands), len(self.results), len(self.sems)

        def both(which):
            def run(ins, outs, sems):
                getattr(self, which)(ins[:n_in], outs[:n_out], sems[:n_sem])
                getattr(other, which)(ins[n_in:], outs[n_out:], sems[n_sem:])
            return run

        aliases = {**self.aliases, **{n_in + i: n_out + j for i, j in other.aliases.items()}}
        return _Ride(self.operands + other.operands, self.results + other.results, aliases, self.sems + other.sems,
                     both("start"), both("wait"))


def _call(body, name, grid, in_specs, out_specs, out_shape, scratch_shapes, operands, ride=None, aliases=None, prefetch=None):
    in_specs, out_specs, out_shape = list(in_specs), list(out_specs), list(out_shape)
    scratch_shapes, operands, aliases = list(scratch_shapes), list(operands), dict(aliases or {})
    kernel_body = body
    if ride is not None:
        n_in, n_out, n_scr, r_in, r_out = len(in_specs), len(out_specs), len(scratch_shapes), len(ride.operands), len(ride.results)

        def kernel_body(*refs):
            out0, scr0 = n_in + r_in, n_in + r_in + n_out + r_out
            ride_refs = (refs[n_in:out0], refs[out0 + n_out:scr0], refs[scr0 + n_scr:])
            ids = [pl.program_id(i) for i in range(len(grid))]
            first = functools.reduce(jnp.logical_and, [i == 0 for i in ids])
            last = functools.reduce(jnp.logical_and, [i == g - 1 for i, g in zip(ids, grid)])

            @pl.when(first)
            def _():
                ride.start(*ride_refs)

            body(*refs[:n_in], *refs[out0:out0 + n_out], *refs[scr0:scr0 + n_scr])

            @pl.when(last)
            def _():
                ride.wait(*ride_refs)

        aliases.update({n_in + i: n_out + j for i, j in ride.aliases.items()})
        in_specs += [ANY] * r_in
        out_specs += [ANY] * r_out
        out_shape += ride.results
        scratch_shapes += ride.sems
        operands += ride.operands
    params = _cparams(dimension_semantics=("arbitrary",) * len(grid))
    if prefetch is None:
        return _pallas_call(
            kernel_body, name=name, grid=grid, in_specs=in_specs, out_specs=out_specs, out_shape=out_shape,
            scratch_shapes=scratch_shapes, input_output_aliases=aliases, compiler_params=params,
        )(*operands)

    def with_prefetch(prefetch_ref, *refs):
        kernel_body(*refs)

    return _pallas_call(
        with_prefetch, name=name,
        grid_spec=pltpu.PrefetchScalarGridSpec(num_scalar_prefetch=1, grid=grid, in_specs=in_specs, out_specs=out_specs,
                                               scratch_shapes=scratch_shapes),
        out_shape=out_shape, input_output_aliases={i + 1: j for i, j in aliases.items()}, compiler_params=params,
    )(prefetch, *operands)


def _after(*arrays):
    return _Ride(arrays, [], {}, [], lambda *refs: None, lambda *refs: None)


def _gather_first_level(shards):
    n = len(shards)

    def copies(ins, outs, sems, landed):
        send_sems, recv_sems, local_sems = sems
        x, y, c = _place()
        peers = [(x, y, 1 - c)] + [(px, py, c) for px, py in _other_chips(x, y)]

        def row(peer):
            return 4 * x + 2 * y + c if not landed else 4 * peer[0] + 2 * peer[1] + peer[2]

        local = [pltpu.make_async_copy(ins[a], outs[a].at[4 * x + 2 * y + c], local_sems.at[a]) for a in range(n)]
        remote = [pltpu.make_async_remote_copy(
            src_ref=ins[a], dst_ref=outs[a].at[row(peer)], send_sem=send_sems.at[a, k], recv_sem=recv_sems.at[a, k],
            device_id=peer, device_id_type=MESH) for a in range(n) for k, peer in enumerate(peers)]
        return local, remote

    def start(ins, outs, sems):
        local, remote = copies(ins, outs, sems, False)
        for cp in local + remote:
            cp.start()

    def wait(ins, outs, sems):
        local, sent = copies(ins, outs, sems, False)
        for cp in copies(ins, outs, sems, True)[1]:
            cp.wait_recv()
        for cp in sent:
            cp.wait_send()
        for cp in local:
            cp.wait()

    return _Ride(shards, [_sds((N_DEV,) + s.shape, s.dtype) for s in shards], {},
                 [pltpu.SemaphoreType.DMA((n, 4)), pltpu.SemaphoreType.DMA((n, 4)), pltpu.SemaphoreType.DMA((n,))], start, wait)


def _gather_second_level(buffers):
    n = len(buffers)

    def copies(outs, sems, core):
        send_sems, recv_sems = sems
        x, y, c = _place()
        return [pltpu.make_async_remote_copy(
            src_ref=outs[a].at[4 * px + 2 * py + core], dst_ref=outs[a].at[4 * px + 2 * py + core], send_sem=send_sems.at[a, j],
            recv_sem=recv_sems.at[a, j], device_id=(x, y, 1 - c), device_id_type=MESH)
            for a in range(n) for j, (px, py) in enumerate(_other_chips(x, y))]

    def start(ins, outs, sems):
        for cp in copies(outs, sems, lax.axis_index("c")):
            cp.start()

    def wait(ins, outs, sems):
        for cp in copies(outs, sems, 1 - lax.axis_index("c")):
            cp.wait_recv()
        for cp in copies(outs, sems, lax.axis_index("c")):
            cp.wait_send()

    return _Ride(buffers, [_sds(b.shape, b.dtype) for b in buffers], {i: i for i in range(n)},
                 [pltpu.SemaphoreType.DMA((n, 3)), pltpu.SemaphoreType.DMA((n, 3))], start, wait)


def _relayed_gather(shards):
    n = len(shards)
    buffers = [_sds((N_DEV,) + s.shape, s.dtype) for s in shards]
    dma = pltpu.SemaphoreType.DMA

    def remote(src, dst, send_sem, recv_sem, to):
        return pltpu.make_async_remote_copy(src_ref=src, dst_ref=dst, send_sem=send_sem, recv_sem=recv_sem,
                                            device_id=to, device_id_type=MESH)

    def row(px, py, pc):
        return 4 * px + 2 * py + pc

    def ride(operands, aliases, sems, copies):
        def start(ins, outs, sem_refs):
            local, sent = copies(ins, outs, sem_refs, False)
            for cp in local + sent:
                cp.start()

        def wait(ins, outs, sem_refs):
            local, sent = copies(ins, outs, sem_refs, False)
            for cp in copies(ins, outs, sem_refs, True)[1]:
                cp.wait_recv()
            for cp in sent:
                cp.wait_send()
            for cp in local:
                cp.wait()

        return _Ride(operands, buffers, aliases, sems, start, wait)

    def first(ins, outs, sems, landed):
        x, y, c = _place()
        peers = [(x, y, 1 - c), (1 - x, y, c), (x, 1 - y, c)]
        local = [pltpu.make_async_copy(ins[a], outs[a].at[row(x, y, c)], sems[2].at[a]) for a in range(n)]
        return local, [remote(ins[a], outs[a].at[row(*peer) if landed else row(x, y, c)], sems[0].at[a, k], sems[1].at[a, k], peer)
                       for a in range(n) for k, peer in enumerate(peers)]

    def second(ins, outs, sems, landed):
        x, y, c = _place()
        mine = 1 - c if landed else c
        copies = []
        for a in range(n):
            half = shards[a].shape[0] // 2
            over_x, over_y, diagonal = outs[a].at[row(1 - x, y, mine)], outs[a].at[row(x, 1 - y, mine)], outs[a].at[row(1 - x, 1 - y, c)]
            lower, upper = pl.ds(0, half), pl.ds(half, half)
            copies += [remote(over_x, over_x, sems[0].at[a, 0], sems[1].at[a, 0], (x, y, 1 - c)),
                       remote(over_y, over_y, sems[0].at[a, 1], sems[1].at[a, 1], (x, y, 1 - c))]
            if landed:
                copies += [remote(diagonal.at[lower], diagonal.at[lower], sems[0].at[a, 2], sems[1].at[a, 2], (1 - x, y, c)),
                           remote(diagonal.at[upper], diagonal.at[upper], sems[0].at[a, 3], sems[1].at[a, 3], (x, 1 - y, c))]
            else:
                copies += [remote(over_y.at[lower], over_y.at[lower], sems[0].at[a, 2], sems[1].at[a, 2], (1 - x, y, c)),
                           remote(over_x.at[upper], over_x.at[upper], sems[0].at[a, 3], sems[1].at[a, 3], (x, 1 - y, c))]
        return [], copies

    def third(ins, outs, sems, landed):
        x, y, c = _place()
        return [], [remote(outs[a].at[row(1 - x, 1 - y, 1 - c if landed else c)], outs[a].at[row(1 - x, 1 - y, 1 - c if landed else c)],
                           sems[0].at[a], sems[1].at[a], (x, y, 1 - c)) for a in range(n)]

    def later(copies, n_sems):
        return lambda partly: ride(partly, {i: i for i in range(n)}, [dma((n,) + n_sems), dma((n,) + n_sems)], copies)

    return ride(shards, {}, [dma((n, 3)), dma((n, 3)), dma((n,))], first), later(second, (4,)), later(third, ())


def _sibling_swap_ride(grads, halves=True):
    n = len(grads)

    def copies(ins, outs, sems):
        x, y, c = _place()
        return [pltpu.make_async_remote_copy(
            src_ref=ins[a].at[1 - c] if halves else ins[a].at[pl.ds(0, 4)], dst_ref=outs[a], send_sem=sems[0].at[a],
            recv_sem=sems[1].at[a], device_id=(x, y, 1 - c), device_id_type=MESH) for a in range(n)]

    def start(ins, outs, sems):
        for cp in copies(ins, outs, sems):
            cp.start()

    def wait(ins, outs, sems):
        for cp in copies(ins, outs, sems):
            cp.wait()

    return _Ride(grads, [_sds((4,) + g.shape[-2:], g.dtype) for g in grads], {},
                 [pltpu.SemaphoreType.DMA((n,)), pltpu.SemaphoreType.DMA((n,))], start, wait)


def _chip_swap_ride(sums):
    n = len(sums)

    def copies(ins, outs, sems, landed):
        send_sems, recv_sems, local_sems = sems
        x, y, c = _place()
        mine = 2 * x + y
        local = [pltpu.make_async_copy(ins[a].at[mine], outs[a].at[mine], local_sems.at[a]) for a in range(n)]
        remote = [pltpu.make_async_remote_copy(
            src_ref=ins[a].at[2 * px + py], dst_ref=outs[a].at[2 * px + py if landed else mine], send_sem=send_sems.at[a, j],
            recv_sem=recv_sems.at[a, j], device_id=(px, py, c), device_id_type=MESH)
            for a in range(n) for j, (px, py) in enumerate(_other_chips(x, y))]
        return local, remote

    def start(ins, outs, sems):
        local, remote = copies(ins, outs, sems, False)
        for cp in local + remote:
            cp.start()

    def wait(ins, outs, sems):
        local, sent = copies(ins, outs, sems, False)
        for cp in copies(ins, outs, sems, True)[1]:
            cp.wait_recv()
        for cp in sent:
            cp.wait_send()
        for cp in local:
            cp.wait()

    return _Ride(sums, [_sds(s.shape, s.dtype) for s in sums], {},
                 [pltpu.SemaphoreType.DMA((n, 3)), pltpu.SemaphoreType.DMA((n, 3)), pltpu.SemaphoreType.DMA((n,))], start, wait)


def _round_for_sending(w_ref, o_ref, transposed):
    rows, cols = o_ref.shape
    if transposed:
        c, r = w_ref.shape
        padded = jnp.concatenate([w_ref[...], jnp.zeros((cols - c, r), F32)], axis=0) if cols > c else w_ref[...]
        o_ref[...] = padded.T.astype(BF16)
    else:
        r, c = w_ref.shape
        if (r, c) != (rows, cols):
            o_ref[...] = jnp.zeros((rows, cols), BF16)
        o_ref[:r, :c] = w_ref[...].astype(BF16)


def _all_gather(shards, others, name):
    n, m = len(shards), len(others)
    sent = [_sds(s.shape, BF16) for s in shards]
    first, second, third = _relayed_gather(sent)
    levels = [first, second(sent), third(sent)]
    counts = [len(level.sems) for level in levels]
    n_sems = sum(counts)

    def body(*refs):
        ins, other_ins, outs, other_outs = refs[:n], refs[n:n + m], refs[n + m:2 * n + m], refs[2 * n + m:2 * (n + m)]
        scratch = refs[2 * (n + m):]
        sems, wide, staged = scratch[:n_sems], scratch[n_sems:n_sems + n], scratch[n_sems + n:n_sems + 2 * n]
        other_wide, other_rounded = scratch[n_sems + 2 * n:n_sems + 2 * n + m], scratch[n_sems + 2 * n + m:]
        for a in range(n):
            pltpu.sync_copy(ins[a], wide[a])
            staged[a][...] = wide[a][...].astype(BF16)
        for i, level in enumerate(levels):
            mine = sems[sum(counts[:i]):sum(counts[:i + 1])]
            level.start(staged, outs, mine)
            if i == 0:
                for (_, transposed, _, _), w_ref, o_ref, w_vmem, o_vmem in zip(others, other_ins, other_outs, other_wide,
                                                                               other_rounded):
                    pltpu.sync_copy(w_ref, w_vmem)
                    _round_for_sending(w_vmem, o_vmem, transposed)
                    pltpu.sync_copy(o_vmem, o_ref)
            level.wait(staged, outs, mine)

    return _pallas_call(
        body, name=name, in_specs=[ANY] * (n + m), out_specs=[ANY] * (n + m),
        out_shape=first.results + [_sds((rows, cols), BF16) for _, _, rows, cols in others],
        scratch_shapes=[s for level in levels for s in level.sems] + [pltpu.VMEM(s.shape, F32) for s in shards]
        + [pltpu.VMEM(s.shape, BF16) for s in shards] + [pltpu.VMEM(w.shape, F32) for w, _, _, _ in others]
        + [pltpu.VMEM((rows, cols), BF16) for _, _, rows, cols in others],
        compiler_params=_cparams(),
    )(*shards, *[w for w, _, _, _ in others])


HBM = pl.BlockSpec(memory_space=pltpu.HBM)
SEMAPHORES = pl.BlockSpec(memory_space=pltpu.SEMAPHORE)
IN_FLIGHT = pltpu.CompilerParams(has_side_effects=pltpu.SideEffectType.DATAFLOW_SIDE_EFFECTING)


def _chip_swap_copies(src_refs, land_refs, send_sems, recv_sems, landed):
    x, y, c = _place()
    return [pltpu.make_async_remote_copy(
        src_ref=src.at[2 * px + py], dst_ref=land.at[2 * px + py if landed else 2 * x + y], send_sem=send_sems.at[3 * a + j],
        recv_sem=recv_sems.at[3 * a + j], device_id=(px, py, c), device_id_type=MESH)
        for a, (src, land) in enumerate(zip(src_refs, land_refs)) for j, (px, py) in enumerate(_other_chips(x, y))]


def _chip_swap_start(sums, name):
    n = len(sums)

    def body(*refs):
        src_refs, land_refs, (send_sems, recv_sems), token = refs[:n], refs[n:2 * n], refs[2 * n:2 * n + 2], refs[-1]
        for cp in _chip_swap_copies(src_refs, land_refs, send_sems, recv_sems, False):
            cp.start()
        token[...] = jnp.zeros_like(token)

    kept = [pltpu.HBM(s.shape, s.dtype) for s in sums]
    out = _pallas_call(
        body, name=name,
        out_shape=[pltpu.SemaphoreType.DMA((3 * n,)), pltpu.SemaphoreType.DMA((3 * n,))] + kept + kept + [_sds((8, 128), F32)],
        in_specs=[HBM] * (2 * n), out_specs=[SEMAPHORES, SEMAPHORES] + [HBM] * (2 * n) + [pl.BlockSpec(memory_space=pltpu.VMEM)],
        input_output_aliases={i: 2 + i for i in range(2 * n)}, compiler_params=IN_FLIGHT,
    )(*[pltpu.with_memory_space_constraint(s, pltpu.HBM) for s in sums],
      *[pltpu.with_memory_space_constraint(lax.empty(s.shape, s.dtype), pltpu.HBM) for s in sums])
    return out[0], out[1], out[2:2 + n], out[2 + n:2 + 2 * n], out[-1]


def _chip_swap_wait(send_sems, recv_sems, sums, landings, after, name):
    n = len(sums)

    def body(*refs):
        src_refs, land_refs, (send_sems, recv_sems) = refs[:n], refs[n:2 * n], refs[2 * n:2 * n + 2]
        for cp in _chip_swap_copies(src_refs, land_refs, send_sems, recv_sems, False):
            cp.wait_send()
        for cp in _chip_swap_copies(src_refs, land_refs, send_sems, recv_sems, True):
            cp.wait_recv()

    out = _pallas_call(
        body, name=name, out_shape=[pltpu.HBM(s.shape, s.dtype) for s in list(sums) + list(landings)],
        in_specs=[HBM] * (2 * n) + [SEMAPHORES, SEMAPHORES] + [ANY] * len(after), out_specs=[HBM] * (2 * n),
        input_output_aliases={i: i for i in range(2 * n)}, compiler_params=IN_FLIGHT,
    )(*sums, *landings, send_sems, recv_sems, *after)
    return out[:n], out[n:]


def _pair_sums(gs, rs, core, name):
    n_arrays = len(gs)

    def body(core_ref, *refs):
        for g_ref, r_ref, o_ref in zip(refs[:n_arrays], refs[n_arrays:2 * n_arrays], refs[2 * n_arrays:]):
            o_ref[...] = (g_ref[...].astype(F32) + r_ref[...].astype(F32)).astype(o_ref.dtype)

    def chip(g):
        return pl.BlockSpec((None,) + g.shape[-2:], lambda p, core_ref: (p, 0, 0))

    def own(g):
        if g.ndim == 3:
            return pl.BlockSpec((None,) + g.shape[-2:], lambda p, core_ref: (p + 4, 0, 0))
        return pl.BlockSpec((None, None) + g.shape[2:], lambda p, core_ref: (core_ref[0], p, 0, 0))

    return _pallas_call(
        body, name=name,
        grid_spec=pltpu.PrefetchScalarGridSpec(
            num_scalar_prefetch=1, grid=(4,), in_specs=[own(g) for g in gs] + [chip(g) for g in gs],
            out_specs=[chip(g) for g in gs]),
        out_shape=[_sds((4,) + g.shape[-2:], g.dtype) for g in gs], compiler_params=_cparams(dimension_semantics=("arbitrary",)),
    )(core, *gs, *rs)


def _adamw_math(w, g, m, v):
    m = ADAM_B1 * m + (1.0 - ADAM_B1) * g
    v = ADAM_B2 * v + (1.0 - ADAM_B2) * (g * g)
    m_hat = m / (1.0 - ADAM_B1 ** ADAM_STEP)
    v_hat = v / (1.0 - ADAM_B2 ** ADAM_STEP)
    return -ADAM_LR * (m_hat / (jnp.sqrt(v_hat) + ADAM_EPS) + ADAM_WD * w), m, v


def _adamw_many(weights, name, ride=None):
    steps = 4
    in_specs, out_specs, out_shape, operands, tiles = [], [], [], [], []
    for w, m, v, parts, own, transposed in weights:
        _, pr, pc = parts.shape
        if transposed:
            c, r = w.shape
            tile = pl.BlockSpec((c, r // steps), lambda i: (0, i))
            part_tile = pl.BlockSpec((4, r // steps, pc), lambda i: (0, i, 0))
            tiles.append((c, r // steps))
        elif w.shape[0] % (8 * steps) == 0:
            r, c = w.shape
            tile = pl.BlockSpec((r // steps, c), lambda i: (i, 0))
            part_tile = pl.BlockSpec((4, r // steps, pc), lambda i: (0, i, 0))
            tiles.append((r // steps, c))
        else:
            tile = pl.BlockSpec(w.shape, lambda i: (0, 0))
            part_tile = pl.BlockSpec(parts.shape, lambda i: (0, 0, 0))
            tiles.append(w.shape)
        in_specs += [tile, tile, tile] + [part_tile] * (1 if own is None else 2)
        out_specs += [tile] * 4
        out_shape += [_sds(w.shape, F32)] * 4
        operands += [w, m, v, parts] + ([] if own is None else [own])
    n_in = len(operands)

    def body(*refs):
        ins, outs = list(refs[:n_in]), refs[n_in:]
        this_chip = 2 * lax.axis_index("x") + lax.axis_index("y")
        for k, (_, _, _, _, own, transposed) in enumerate(weights):
            w_ref, m_ref, v_ref, p_ref = ins[:4]
            own_ref = None if own is None else ins[4]
            del ins[:4 if own is None else 5]
            rows, cols = tiles[k]
            g = None
            for q in range(4):
                index = (q,) if transposed else (q, slice(0, rows), slice(0, cols))
                part = p_ref[index] if own is None else jnp.where(this_chip == q, own_ref[index], p_ref[index])
                g = part.astype(F32) if g is None else g + part.astype(F32)
            if transposed:
                g = g.T[:rows]
            g_out, d_out, m_out, v_out = outs[4 * k:4 * k + 4]
            g_out[...] = g
            d_out[...], m_out[...], v_out[...] = _adamw_math(w_ref[...], g, m_ref[...], v_ref[...])

    return _call(body, name, (steps,), in_specs, out_specs, out_shape, [], operands, ride)


SMALL = ("ssm_a_re", "ssm_a_im", "ssm_log_dt", "ssm_b_re", "ssm_b_im", "ssm_c_re", "ssm_c_im", "ssm_d",
         "ln1_g", "ln1_b", "ln2_g", "ln2_b")


def _pack_rows(arrays):
    rows = []
    for a in arrays:
        flat = a.reshape(-1)
        rows.append(jnp.pad(flat, (0, -flat.shape[0] % 128)).reshape(-1, 128))
    packed = jnp.concatenate(rows, axis=0)
    return jnp.pad(packed, ((0, -packed.shape[0] % 8), (0, 0)))


def _unpack_rows(packed, shapes):
    out, row = [], 0
    for shape in shapes:
        size = math.prod(shape)
        n_rows = -(-size // 128)
        out.append(packed[row:row + n_rows].reshape(-1)[:size].reshape(shape))
        row += n_rows
    return out


def _sum_devices(parts):
    def body(p_ref, o_ref):
        total = p_ref[0]
        for dev in range(1, N_DEV):
            total = total + p_ref[dev]
        o_ref[...] = total

    return _pallas_call(body, name="sum_devices", out_shape=_sds(parts.shape[1:], F32))(parts)


def _adamw_replicated(ws, ms, vs, gs):
    n = len(ws)

    def body(*refs):
        w_refs, m_refs, v_refs, g_refs, d_out, m_out, v_out = (refs[i * n:(i + 1) * n] for i in range(7))
        for i in range(n):
            d_out[i][...], m_out[i][...], v_out[i][...] = _adamw_math(w_refs[i][...], g_refs[i][...], m_refs[i][...], v_refs[i][...])

    out = _pallas_call(body, name="adamw_replicated", out_shape=[_sds(w.shape, F32) for w in ws] * 3,
                       compiler_params=_cparams())(*ws, *ms, *vs, *gs)
    return out[:n], out[n:2 * n], out[2 * n:]


def kernel(x, w_in, b_gate, w_attn_br, w_ssm_br, w_out, ssm_a_re, ssm_a_im, ssm_log_dt, ssm_b_re, ssm_b_im, ssm_c_re, ssm_c_im, ssm_d, w_glu, ln1_g, ln1_b, w_ff_gate, w_ff_up, w_ff_down, ln2_g, ln2_b, loss_target, m_w_in, m_b_gate, m_w_attn_br, m_w_ssm_br, m_w_out, m_ssm_a_re, m_ssm_a_im, m_ssm_log_dt, m_ssm_b_re, m_ssm_b_im, m_ssm_c_re, m_ssm_c_im, m_ssm_d, m_w_glu, m_ln1_g, m_ln1_b, m_w_ff_gate, m_w_ff_up, m_w_ff_down, m_ln2_g, m_ln2_b, v_w_in, v_b_gate, v_w_attn_br, v_w_ssm_br, v_w_out, v_ssm_a_re, v_ssm_a_im, v_ssm_log_dt, v_ssm_b_re, v_ssm_b_im, v_ssm_c_re, v_ssm_c_im, v_ssm_d, v_w_glu, v_ln1_g, v_ln1_b, v_w_ff_gate, v_w_ff_up, v_w_ff_down, v_ln2_g, v_ln2_b):
    given = dict(locals())
    x2, target = x[0], loss_target[0]
    core = lax.axis_index("c").astype(jnp.int32).reshape(1)

    sharded = ("w_in", "w_attn_br", "w_ssm_br", "w_glu", "w_ff_gate", "w_ff_up", "b_gate", "w_out", "w_ff_down")
    send_shape = dict(w_in=(D_MODEL, 896), w_attn_br=(ATTN_WIDTH, 128), w_ssm_br=(SSM_WIDTH, 128), w_glu=(SSM_WIDTH, 128),
                      w_out=(128, D_MODEL), w_ff_gate=(D_MODEL, FF_PAD), w_ff_up=(D_MODEL, FF_PAD), w_ff_down=(FF_PAD, D_MODEL))
    local = {k: given[k][0] for k in sharded}
    narrow = ("w_ff_gate", "w_ff_up")
    def to_send(k):
        return (local[k].T, True, *send_shape[k]) if k in narrow else (local[k], False, *send_shape[k])

    later = [k for k in sharded if k not in ("w_in", "b_gate")]
    gathered, *rounded = _all_gather([local["w_in"]], [to_send(k) for k in later], "gather_w_in")
    sends = dict(zip(later, rounded))
    sends["b_gate"] = local["b_gate"]
    mixer_weights = ("w_attn_br", "w_ssm_br", "w_glu", "b_gate", "w_out")
    ff_weights = ("w_ff_gate", "w_ff_up", "w_ff_down")
    wt = {"w_in": gathered}

    a_re, a_im, log_dt = ssm_a_re[0], ssm_a_im[0], ssm_log_dt[0].reshape(SSM_GROUPS, 1)
    b_re_t, b_im_t = ssm_b_re[0].transpose(0, 2, 1), ssm_b_im[0].transpose(0, 2, 1)
    abar_re, abar_im, e_re, e_im, bbar_re_t, bbar_im_t = _ssm_prep(a_re, a_im, log_dt, b_re_t, b_im_t)
    bmat, cmat, a_chunks = _ssm_tables(abar_re, abar_im, bbar_re_t, bbar_im_t, ssm_c_re[0], ssm_c_im[0])
    cos_t, sin_t = _rope_tables()

    big_mixer, ff_in = [k for k in mixer_weights if k != "b_gate"], ("w_ff_gate", "w_ff_up")
    n_mixer = len(big_mixer)
    mixer_1, mixer_2, mixer_3 = _relayed_gather([sends[k] for k in big_mixer])
    ff_in_1, ff_in_2, ff_in_3 = _relayed_gather([sends[k] for k in ff_in])
    ff_down_1, ff_down_2, ff_down_3 = _relayed_gather([sends["w_ff_down"]])
    proj, *landed = _proj(x2, wt["w_in"], mixer_1 + _gather_first_level([sends["b_gate"]]))
    mixer, bias = landed[:n_mixer], landed[n_mixer:]
    attn, lse, q_pm, k_pm, v_pm, *landed = _attn_fwd(proj, cos_t, sin_t,
                                                     mixer_2(mixer) + _gather_second_level(bias) + ff_in_1)
    mixer, b_gate_full, ff = landed[:n_mixer], landed[n_mixer], landed[n_mixer + 1:]
    ys, states, *landed = _ssm_fwd(proj, bmat, cmat, a_chunks, ssm_d, mixer_3(mixer) + ff_in_2(ff) + ff_down_1)
    wt.update(zip(big_mixer, landed[:n_mixer]))
    ff, ff_down = landed[n_mixer:n_mixer + 2], landed[n_mixer + 2:]
    wt["w_out"] = wt["w_out"].reshape(D_MODEL, D_MODEL)
    h, xhat1, rstd1, glu, y_attn, y_ssm, *landed = _mixer_out(
        attn, ys, proj, x2, wt["w_attn_br"], wt["w_ssm_br"], wt["w_glu"], wt["w_out"], b_gate_full, ln1_g, ln1_b,
        ff_in_3(ff) + ff_down_2(ff_down))
    wt.update(zip(ff_in, landed[:2]))
    ff_a, ff_b, ff_f, w_ff_down = _ff_up(h, wt["w_ff_gate"], wt["w_ff_up"], ff_down_3(landed[2:]))
    wt["w_ff_down"] = w_ff_down.reshape(D_FF_PAD, D_MODEL)
    dr2, d_ln2_g, d_ln2_b, loss_lanes = _ff_down_loss(ff_f, wt["w_ff_down"], h, target, ln2_g, ln2_b)

    def pair_sums(names, contrib, from_sibling):
        return _pair_sums([contrib[k] for k in names], from_sibling, core, "pair_sums_" + names[0])

    d_a, d_b = _ff_down_bwd(dr2, wt["w_ff_down"], ff_a, ff_b)
    contrib = dict(w_ff_gate=_weight_grad(h, d_a, "wgrad_w_ff_gate", FF_PAD),
                   w_ff_up=_weight_grad(h, d_b, "wgrad_w_ff_up", FF_PAD),
                   w_ff_down=_weight_grad(ff_f, dr2, "wgrad_w_ff_down"))
    dr1, d_ln1_g, d_ln1_b, *from_sibling = _ff_up_bwd(
        d_a, d_b, wt["w_ff_gate"], wt["w_ff_up"], dr2, xhat1, rstd1, ln1_g, _sibling_swap_ride([contrib[k] for k in ff_weights]))
    ff_sums = pair_sums(ff_weights, contrib, from_sibling)

    d_ya, d_yssm, d_proj, d_attn, d_glu, d_ys, mixed, y_s, gy, d_bg = _mixer_bwd(
        dr1, proj, y_attn, y_ssm, glu, ys, wt["w_attn_br"], wt["w_ssm_br"], wt["w_glu"], wt["w_out"], b_gate_full)
    contrib.update(w_attn_br=_weight_grad(attn, d_ya, "wgrad_w_attn_br", 128),
                   w_ssm_br=_weight_grad(y_s, d_yssm, "wgrad_w_ssm_br", 128),
                   w_glu=_weight_grad(gy, d_glu, "wgrad_w_glu", 128),
                   w_out=_weight_grad(mixed, dr1, "wgrad_w_out"),
                   b_gate=d_bg.reshape(2, 4, 2, 128).transpose(2, 1, 0, 3))
    d_proj, *landed = _attn_bwd(q_pm, k_pm, v_pm, cos_t, sin_t, attn, lse, d_attn, d_proj,
                                _chip_swap_ride(ff_sums) + _sibling_swap_ride([contrib[k] for k in mixer_weights]))
    parts, own_sums = dict(zip(ff_weights, landed[:len(ff_weights)])), {}
    mixer_sums = pair_sums(mixer_weights, contrib, landed[len(ff_weights):])
    landed = _ssm_bwd(d_ys, proj, states, bmat, cmat, a_chunks, ssm_d, d_proj, _chip_swap_ride(mixer_sums))
    d_proj, *ssm_blocks, d_abar, d_skip = landed[:7]
    parts.update(zip(mixer_weights, landed[7:]))

    gbb_re_t, gbb_im_t, gc_re, gc_im = (blocks.reshape(SSM_GROUPS, SSM_GROUP, SSM_STATE) for blocks in ssm_blocks)
    ga_re = d_abar[:, 0, :CHUNK_STATES].reshape(SSM_GROUPS, SSM_STATE)
    ga_im = d_abar[:, 0, CHUNK_STATES:].reshape(SSM_GROUPS, SSM_STATE)
    g_a_re, g_a_im, g_log_dt, g_b_re_t, g_b_im_t = _ssm_param_bwd(
        a_re, a_im, log_dt, b_re_t, b_im_t, abar_re, abar_im, e_re, e_im, ga_re, ga_im, gbb_re_t, gbb_im_t)
    mine = [g_a_re, g_a_im, g_log_dt, g_b_re_t, g_b_im_t, gc_re, -gc_im,
            d_skip, d_ln1_g, d_ln1_b, d_ln2_g, d_ln2_b]
    small_packed = _pack_rows(mine + [loss_lanes])

    w_in_contrib, small_partly = _weight_grad_rows(x2, d_proj, core, "wgrad_w_in_first", 896, 0, 6,
                                                   _gather_first_level([small_packed]))
    w_in_contrib, from_sibling, every = _weight_grad_rows(
        x2, d_proj, core, "wgrad_w_in_rest", 896, 6, 2,
        _sibling_swap_ride([w_in_contrib], halves=False) + _gather_second_level([small_partly]), in_place=True)
    w_in_sum, = _pair_sums([w_in_contrib], [from_sibling], core, "pair_sums_w_in")
    send_sems, recv_sems, w_in_sum, landing, token = _chip_swap_start([w_in_sum], "w_in_chip_swap_start")

    def adamw_of(k):
        taken = (lambda a: a.T) if k in narrow else (lambda a: a)
        return taken(local[k]), taken(given["m_" + k][0]), taken(given["v_" + k][0]), parts[k], own_sums.get(k), k in narrow

    others = [k for k in sharded if k != "w_in"]
    updated = _adamw_many([adamw_of(k) for k in others], "adamw_others", _after(token))
    grad_x, = _grad_x(d_proj, wt["w_in"], dr1, _after(token))

    def held(k, a):
        return a.transpose(0, 1, 3, 2) if k in ("ssm_b_re", "ssm_b_im") else a

    *small_grads, loss_sum = _unpack_rows(_sum_devices(every), [held(k, given[k]).shape for k in SMALL] + [(1, 128)])
    small = _adamw_replicated([held(k, given[k]) for k in SMALL], [held(k, given["m_" + k]) for k in SMALL],
                              [held(k, given["v_" + k]) for k in SMALL], small_grads)
    loss = loss_sum[0, 0]

    (own_sums["w_in"],), (parts["w_in"],) = _chip_swap_wait(
        send_sems, recv_sems, w_in_sum, landing, [grad_x, updated[0], small[0][0]], "w_in_chip_swap_wait")
    updated += _adamw_many([adamw_of("w_in")], "adamw_w_in")

    grads, deltas, new_m, new_v = {}, {}, {}, {}
    for i, k in enumerate(others + ["w_in"]):
        out = [o.T if k in narrow else o for o in updated[4 * i:4 * i + 4]]
        grads[k], deltas[k], new_m[k], new_v[k] = (o.reshape((1,) + local[k].shape) for o in out)
    for res, values in zip((grads, deltas, new_m, new_v), (small_grads,) + small):
        res.update((k, held(k, a)) for k, a in zip(SMALL, values))

    order = ("w_in", "b_gate", "w_attn_br", "w_ssm_br", "w_out", "ssm_a_re", "ssm_a_im", "ssm_log_dt", "ssm_b_re", "ssm_b_im",
             "ssm_c_re", "ssm_c_im", "ssm_d", "w_glu", "ln1_g", "ln1_b", "w_ff_gate", "w_ff_up", "w_ff_down", "ln2_g", "ln2_b")
    return (loss, grad_x[None], *[grads[k] for k in order], *[deltas[k] for k in order], *[new_m[k] for k in order],
            *[new_v[k] for k in order])
```
